```python
import math
import jax, jax.numpy as jnp
from jax import lax
import numpy as np

D_MODEL = 2048
BATCH = 16
SEQ = 2048
DEPTH = 1

CHUNK = 64
Q_BLOCK = 128
MAX_OFFSET = 65536

MLA_HEADS = 8
Q_LORA = 512
KV_LORA = 256
NOPE_DIM = 128
ROPE_DIM = 64
V_DIM = 128
ROPE_THETA = 10000.0

FOX_HEADS = 8
FOX_HEAD_DIM = 128

D_FF = 5632
CONV_WIDTH = 3

N_BRANCHES = 2
EPS = 1e-6
NEG_INF = -1e30

SPLITS = (
    Q_LORA,
    KV_LORA,
    ROPE_DIM,
    FOX_HEADS * FOX_HEAD_DIM,
    FOX_HEADS * FOX_HEAD_DIM,
    FOX_HEADS * FOX_HEAD_DIM,
    FOX_HEADS,
    N_BRANCHES * D_MODEL,
)
D_IN = sum(SPLITS)

kernel_name = "hybrid_mla_fox_convffn_block"


def rmsnorm(x, g):
    xf = x.astype(jnp.float32)
    y = xf * lax.rsqrt(jnp.mean(xf * xf, axis=-1, keepdims=True) + EPS)
    return (y * g.astype(jnp.float32)).astype(x.dtype)


def rope(x, cos, sin):
    half = x.shape[-1] // 2
    x1, x2 = x[..., :half], x[..., half:]
    return jnp.concatenate([x1 * cos - x2 * sin, x2 * cos + x1 * sin], axis=-1)


def rope_tables(positions, dtype):
    inv_freq = 1.0 / (ROPE_THETA ** (jnp.arange(0, ROPE_DIM, 2, dtype=jnp.float32) / ROPE_DIM))
    ang = positions.astype(jnp.float32)[..., None] * inv_freq
    return jnp.cos(ang).astype(dtype), jnp.sin(ang).astype(dtype)


def block_attention(q, k, v, causal_unit, log_decay=None):
    S = q.shape[1]
    scale = q.shape[-1] ** -0.5
    pos = jnp.arange(S)
    outs = []
    for i in range(S // Q_BLOCK):
        start, end = i * Q_BLOCK, (i + 1) * Q_BLOCK
        s = jnp.einsum('bqhd,bkhd->bhqk', q[:, start:end], k[:, :end]).astype(jnp.float32) * scale
        if log_decay is not None:
            s = s + log_decay[:, :, start:end, None] - log_decay[:, :, None, :end]
        visible = (pos[None, :end] // causal_unit) <= (pos[start:end, None] // causal_unit)
        s = jnp.where(visible, s, NEG_INF)
        p = jax.nn.softmax(s, axis=-1)
        outs.append(jnp.einsum('bhqk,bkhd->bqhd', p.astype(v.dtype), v[:, :end]))
    return jnp.concatenate(outs, axis=1)


def _fwd_setup_inputs(seed: int = 0) -> dict:
    key = jax.random.key(seed)
    ks = jax.random.split(key, 24)
    f32 = jnp.float32
    nrm = lambda k, shape, fan_in: jax.random.normal(k, shape, f32) * (fan_in ** -0.5)
    gain = lambda k, n: 1.0 + 0.05 * jax.random.normal(k, (n,), f32)
    offset = jax.random.randint(ks[1], (BATCH, 1), 0, MAX_OFFSET, dtype=jnp.int32)
    positions = offset + jnp.arange(SEQ, dtype=jnp.int32)[None, :]
    return {
        "x": jax.random.normal(ks[0], (BATCH, SEQ, D_MODEL), f32),
        "positions": positions,
        "pre_mix_norm": gain(ks[2], D_MODEL),
        "w_in": nrm(ks[3], (D_MODEL, D_IN), D_MODEL),
        "q_a_norm": gain(ks[4], Q_LORA),
        "w_uq": nrm(ks[5], (Q_LORA, MLA_HEADS * (NOPE_DIM + ROPE_DIM)), Q_LORA),
        "kv_a_norm": gain(ks[6], KV_LORA),
        "w_ukv": nrm(ks[7], (KV_LORA, MLA_HEADS * (NOPE_DIM + V_DIM)), KV_LORA),
        "b_forget": jax.random.uniform(ks[8], (FOX_HEADS,), f32, 1.0, 4.0),
        "b_gate": 0.02 * jax.random.normal(ks[9], (N_BRANCHES * D_MODEL,), f32),
        "w_branch_mla": nrm(ks[10], (MLA_HEADS * V_DIM, D_MODEL), MLA_HEADS * V_DIM),
        "w_branch_fox": nrm(ks[11], (FOX_HEADS * FOX_HEAD_DIM, D_MODEL), FOX_HEADS * FOX_HEAD_DIM),
        "w_out": nrm(ks[12], (D_MODEL, D_MODEL), D_MODEL),
        "post_mix_norm": gain(ks[13], D_MODEL),
        "pre_ffn_norm": gain(ks[14], D_MODEL),
        "w_up": nrm(ks[15], (D_MODEL, 2 * D_FF), D_MODEL),
        "conv_w": nrm(ks[16], (CONV_WIDTH, 2 * D_FF), CONV_WIDTH),
        "conv_b": 0.02 * jax.random.normal(ks[17], (2 * D_FF,), f32),
        "w_down": nrm(ks[18], (D_FF, D_MODEL), D_FF),
        "post_ffn_norm": gain(ks[19], D_MODEL),
    }


def _fwd_reference(x, positions, pre_mix_norm, w_in, q_a_norm, w_uq, kv_a_norm, w_ukv,
              b_forget, b_gate, w_branch_mla, w_branch_fox, w_out, post_mix_norm,
              pre_ffn_norm, w_up, conv_w, conv_b, w_down, post_ffn_norm):
    B, S, _ = x.shape
    cos, sin = rope_tables(positions, x.dtype)
    for _layer in range(DEPTH):
        h = rmsnorm(x, pre_mix_norm)
        proj = h @ w_in
        cuts = np.cumsum(SPLITS)[:-1].tolist()
        q_lat, kv_lat, k_pe, fq, fk, fv, f_logit, g_logit = jnp.split(proj, cuts, axis=-1)

        q = (rmsnorm(q_lat, q_a_norm) @ w_uq).reshape(B, S, MLA_HEADS, NOPE_DIM + ROPE_DIM)
        q_nope, q_pe = q[..., :NOPE_DIM], q[..., NOPE_DIM:]
        q_pe = rope(q_pe, cos[:, :, None, :], sin[:, :, None, :])
        kv = (rmsnorm(kv_lat, kv_a_norm) @ w_ukv).reshape(B, S, MLA_HEADS, NOPE_DIM + V_DIM)
        k_nope, v_mla = kv[..., :NOPE_DIM], kv[..., NOPE_DIM:]
        k_pe = rope(k_pe, cos, sin)[:, :, None, :]
        q_mla = jnp.concatenate([q_nope, q_pe], axis=-1)
        k_mla = jnp.concatenate([k_nope, jnp.broadcast_to(k_pe, (B, S, MLA_HEADS, ROPE_DIM))], axis=-1)
        o_mla = block_attention(q_mla, k_mla, v_mla, CHUNK).reshape(B, S, MLA_HEADS * V_DIM)

        log_f = jax.nn.log_sigmoid(f_logit.astype(jnp.float32) + b_forget.astype(jnp.float32))
        c = jnp.transpose(jnp.cumsum(log_f, axis=1), (0, 2, 1))
        shp = (B, S, FOX_HEADS, FOX_HEAD_DIM)
        o_fox = block_attention(fq.reshape(shp), fk.reshape(shp), fv.reshape(shp), 1, c)
        o_fox = o_fox.reshape(B, S, FOX_HEADS * FOX_HEAD_DIM)

        gates = jax.nn.sigmoid((g_logit + b_gate).astype(jnp.float32)).astype(x.dtype)
        g_mla, g_fox = gates[..., :D_MODEL], gates[..., D_MODEL:]
        merged = g_mla * (o_mla @ w_branch_mla) + g_fox * (o_fox @ w_branch_fox)
        x = x + rmsnorm(merged @ w_out, post_mix_norm)

        h2 = rmsnorm(x, pre_ffn_norm)
        u = h2 @ w_up
        u_pad = jnp.pad(u, ((0, 0), (CONV_WIDTH - 1, 0), (0, 0)))
        u = sum(conv_w[j] * u_pad[:, j:j + S] for j in range(CONV_WIDTH)) + conv_b
        gate, val = u[..., :D_FF], u[..., D_FF:]
        ff = (jax.nn.gelu(gate, approximate=True) * val) @ w_down
        x = x + rmsnorm(ff, post_ffn_norm)
    return x


import jax as _jax
import jax.numpy as _jnp

TWIN_FORMAT = 'train_step'
FWD_PARAMS = ['x', 'positions', 'pre_mix_norm', 'w_in', 'q_a_norm', 'w_uq', 'kv_a_norm', 'w_ukv', 'b_forget', 'b_gate', 'w_branch_mla', 'w_branch_fox', 'w_out', 'post_mix_norm', 'pre_ffn_norm', 'w_up', 'conv_w', 'conv_b', 'w_down', 'post_ffn_norm']
TWIN_WEIGHTS = ['pre_mix_norm', 'w_in', 'q_a_norm', 'w_uq', 'kv_a_norm', 'w_ukv', 'b_forget', 'b_gate', 'w_branch_mla', 'w_branch_fox', 'w_out', 'post_mix_norm', 'pre_ffn_norm', 'w_up', 'conv_w', 'conv_b', 'w_down', 'post_ffn_norm']
TWIN_DIFF_INPUT = 'x'
TWIN_INPUTS = ['x', 'positions', 'pre_mix_norm', 'w_in', 'q_a_norm', 'w_uq', 'kv_a_norm', 'w_ukv', 'b_forget', 'b_gate', 'w_branch_mla', 'w_branch_fox', 'w_out', 'post_mix_norm', 'pre_ffn_norm', 'w_up', 'conv_w', 'conv_b', 'w_down', 'post_ffn_norm', 'loss_target', 'm_pre_mix_norm', 'm_w_in', 'm_q_a_norm', 'm_w_uq', 'm_kv_a_norm', 'm_w_ukv', 'm_b_forget', 'm_b_gate', 'm_w_branch_mla', 'm_w_branch_fox', 'm_w_out', 'm_post_mix_norm', 'm_pre_ffn_norm', 'm_w_up', 'm_conv_w', 'm_conv_b', 'm_w_down', 'm_post_ffn_norm', 'v_pre_mix_norm', 'v_w_in', 'v_q_a_norm', 'v_w_uq', 'v_kv_a_norm', 'v_w_ukv', 'v_b_forget', 'v_b_gate', 'v_w_branch_mla', 'v_w_branch_fox', 'v_w_out', 'v_post_mix_norm', 'v_pre_ffn_norm', 'v_w_up', 'v_conv_w', 'v_conv_b', 'v_w_down', 'v_post_ffn_norm']
TWIN_OUTPUTS = ['loss', 'grad_x', 'grad_pre_mix_norm', 'grad_w_in', 'grad_q_a_norm', 'grad_w_uq', 'grad_kv_a_norm', 'grad_w_ukv', 'grad_b_forget', 'grad_b_gate', 'grad_w_branch_mla', 'grad_w_branch_fox', 'grad_w_out', 'grad_post_mix_norm', 'grad_pre_ffn_norm', 'grad_w_up', 'grad_conv_w', 'grad_conv_b', 'grad_w_down', 'grad_post_ffn_norm', 'delta_pre_mix_norm', 'delta_w_in', 'delta_q_a_norm', 'delta_w_uq', 'delta_kv_a_norm', 'delta_w_ukv', 'delta_b_forget', 'delta_b_gate', 'delta_w_branch_mla', 'delta_w_branch_fox', 'delta_w_out', 'delta_post_mix_norm', 'delta_pre_ffn_norm', 'delta_w_up', 'delta_conv_w', 'delta_conv_b', 'delta_w_down', 'delta_post_ffn_norm', 'new_m_pre_mix_norm', 'new_m_w_in', 'new_m_q_a_norm', 'new_m_w_uq', 'new_m_kv_a_norm', 'new_m_w_ukv', 'new_m_b_forget', 'new_m_b_gate', 'new_m_w_branch_mla', 'new_m_w_branch_fox', 'new_m_w_out', 'new_m_post_mix_norm', 'new_m_pre_ffn_norm', 'new_m_w_up', 'new_m_conv_w', 'new_m_conv_b', 'new_m_w_down', 'new_m_post_ffn_norm', 'new_v_pre_mix_norm', 'new_v_w_in', 'new_v_q_a_norm', 'new_v_w_uq', 'new_v_kv_a_norm', 'new_v_w_ukv', 'new_v_b_forget', 'new_v_b_gate', 'new_v_w_branch_mla', 'new_v_w_branch_fox', 'new_v_w_out', 'new_v_post_mix_norm', 'new_v_pre_ffn_norm', 'new_v_w_up', 'new_v_conv_w', 'new_v_conv_b', 'new_v_w_down', 'new_v_post_ffn_norm']
TWIN_LEAF_KINDS = {'loss': 'loss', 'grad_x': 'grad_x', 'grad_pre_mix_norm': 'grad_w', 'grad_w_in': 'grad_w', 'grad_q_a_norm': 'grad_w', 'grad_w_uq': 'grad_w', 'grad_kv_a_norm': 'grad_w', 'grad_w_ukv': 'grad_w', 'grad_b_forget': 'grad_w', 'grad_b_gate': 'grad_w', 'grad_w_branch_mla': 'grad_w', 'grad_w_branch_fox': 'grad_w', 'grad_w_out': 'grad_w', 'grad_post_mix_norm': 'grad_w', 'grad_pre_ffn_norm': 'grad_w', 'grad_w_up': 'grad_w', 'grad_conv_w': 'grad_w', 'grad_conv_b': 'grad_w', 'grad_w_down': 'grad_w', 'grad_post_ffn_norm': 'grad_w', 'delta_pre_mix_norm': 'delta_w', 'delta_w_in': 'delta_w', 'delta_q_a_norm': 'delta_w', 'delta_w_uq': 'delta_w', 'delta_kv_a_norm': 'delta_w', 'delta_w_ukv': 'delta_w', 'delta_b_forget': 'delta_w', 'delta_b_gate': 'delta_w', 'delta_w_branch_mla': 'delta_w', 'delta_w_branch_fox': 'delta_w', 'delta_w_out': 'delta_w', 'delta_post_mix_norm': 'delta_w', 'delta_pre_ffn_norm': 'delta_w', 'delta_w_up': 'delta_w', 'delta_conv_w': 'delta_w', 'delta_conv_b': 'delta_w', 'delta_w_down': 'delta_w', 'delta_post_ffn_norm': 'delta_w', 'new_m_pre_mix_norm': 'new_m', 'new_m_w_in': 'new_m', 'new_m_q_a_norm': 'new_m', 'new_m_w_uq': 'new_m', 'new_m_kv_a_norm': 'new_m', 'new_m_w_ukv': 'new_m', 'new_m_b_forget': 'new_m', 'new_m_b_gate': 'new_m', 'new_m_w_branch_mla': 'new_m', 'new_m_w_branch_fox': 'new_m', 'new_m_w_out': 'new_m', 'new_m_post_mix_norm': 'new_m', 'new_m_pre_ffn_norm': 'new_m', 'new_m_w_up': 'new_m', 'new_m_conv_w': 'new_m', 'new_m_conv_b': 'new_m', 'new_m_w_down': 'new_m', 'new_m_post_ffn_norm': 'new_m', 'new_v_pre_mix_norm': 'new_v', 'new_v_w_in': 'new_v', 'new_v_q_a_norm': 'new_v', 'new_v_w_uq': 'new_v', 'new_v_kv_a_norm': 'new_v', 'new_v_w_ukv': 'new_v', 'new_v_b_forget': 'new_v', 'new_v_b_gate': 'new_v', 'new_v_w_branch_mla': 'new_v', 'new_v_w_branch_fox': 'new_v', 'new_v_w_out': 'new_v', 'new_v_post_mix_norm': 'new_v', 'new_v_pre_ffn_norm': 'new_v', 'new_v_w_up': 'new_v', 'new_v_conv_w': 'new_v', 'new_v_conv_b': 'new_v', 'new_v_w_down': 'new_v', 'new_v_post_ffn_norm': 'new_v'}


def _forward(args):
    return _fwd_reference(*[args[k] for k in FWD_PARAMS])


def _output_shape():
    out = _jax.eval_shape(lambda: _forward(_fwd_setup_inputs(0)))
    return out.shape, out.dtype

N_MICROBATCH = 1
ADAM_LR = 0.001
ADAM_B1 = 0.9
ADAM_B2 = 0.999
ADAM_EPS = 1e-08
ADAM_WD = 0.01
ADAM_STEP = 10
PER_EXAMPLE_BATCH_AXIS = {'x': 0, 'positions': 0, 'loss_target': 0}
SHARED_INPUTS = []
_WEIGHT_DTYPES = {'pre_mix_norm': _jnp.float32, 'w_in': _jnp.float32, 'q_a_norm': _jnp.float32, 'w_uq': _jnp.float32, 'kv_a_norm': _jnp.float32, 'w_ukv': _jnp.float32, 'b_forget': _jnp.float32, 'b_gate': _jnp.float32, 'w_branch_mla': _jnp.float32, 'w_branch_fox': _jnp.float32, 'w_out': _jnp.float32, 'post_mix_norm': _jnp.float32, 'pre_ffn_norm': _jnp.float32, 'w_up': _jnp.float32, 'conv_w': _jnp.float32, 'conv_b': _jnp.float32, 'w_down': _jnp.float32, 'post_ffn_norm': _jnp.float32}
MOMENT_SCALE = {'pre_mix_norm': 3.472186e-01, 'w_in': 1.768401e-01, 'q_a_norm': 1.476963e-01, 'w_uq': 8.456292e-02, 'kv_a_norm': 3.018970e-01, 'w_ukv': 1.016801e-01, 'b_forget': 2.496556e+00, 'b_gate': 7.504119e-02, 'w_branch_mla': 8.103084e-02, 'w_branch_fox': 2.591809e-01, 'w_out': 2.624253e-01, 'post_mix_norm': 1.601289e+01, 'pre_ffn_norm': 2.558628e-01, 'w_up': 1.051901e-01, 'conv_w': 1.101183e-01, 'conv_b': 1.537615e-01, 'w_down': 1.900735e-01, 'post_ffn_norm': 1.599544e+01}


def _to_microbatches(a, axis):
    t = _jnp.moveaxis(a, axis, 0)
    t = t.reshape((N_MICROBATCH, t.shape[0] // N_MICROBATCH) + t.shape[1:])
    return _jnp.moveaxis(t, 1, axis + 1)


def setup_inputs(seed: int = 0) -> dict:
    inp = _fwd_setup_inputs(seed)
    key = _jax.random.fold_in(_jax.random.key(seed), 7919)
    shape, _ = _output_shape()
    out = dict(inp)
    out["loss_target"] = _jax.random.normal(_jax.random.fold_in(key, 0), shape, _jnp.float32)
    for i, name in enumerate(TWIN_WEIGHTS):
        w = inp[name].astype(_jnp.float32)
        if MOMENT_SCALE is None:
            s = _jnp.sqrt(_jnp.mean(_jnp.square(w)) + 1e-30)
        else:
            s = MOMENT_SCALE[name]
        km, kv = _jax.random.split(_jax.random.fold_in(key, i + 1))
        out[name] = w
        out["m_" + name] = s * _jax.random.normal(km, w.shape, _jnp.float32)
        out["v_" + name] = (s * s) * _jax.random.uniform(kv, w.shape, _jnp.float32, 0.5, 1.5)
    if N_MICROBATCH > 1:
        for name, axis in PER_EXAMPLE_BATCH_AXIS.items():
            out[name] = _to_microbatches(out[name], axis)
    return {'x': out['x'], 'positions': out['positions'], 'pre_mix_norm': out['pre_mix_norm'], 'w_in': out['w_in'], 'q_a_norm': out['q_a_norm'], 'w_uq': out['w_uq'], 'kv_a_norm': out['kv_a_norm'], 'w_ukv': out['w_ukv'], 'b_forget': out['b_forget'], 'b_gate': out['b_gate'], 'w_branch_mla': out['w_branch_mla'], 'w_branch_fox': out['w_branch_fox'], 'w_out': out['w_out'], 'post_mix_norm': out['post_mix_norm'], 'pre_ffn_norm': out['pre_ffn_norm'], 'w_up': out['w_up'], 'conv_w': out['conv_w'], 'conv_b': out['conv_b'], 'w_down': out['w_down'], 'post_ffn_norm': out['post_ffn_norm'], 'loss_target': out['loss_target'], 'm_pre_mix_norm': out['m_pre_mix_norm'], 'm_w_in': out['m_w_in'], 'm_q_a_norm': out['m_q_a_norm'], 'm_w_uq': out['m_w_uq'], 'm_kv_a_norm': out['m_kv_a_norm'], 'm_w_ukv': out['m_w_ukv'], 'm_b_forget': out['m_b_forget'], 'm_b_gate': out['m_b_gate'], 'm_w_branch_mla': out['m_w_branch_mla'], 'm_w_branch_fox': out['m_w_branch_fox'], 'm_w_out': out['m_w_out'], 'm_post_mix_norm': out['m_post_mix_norm'], 'm_pre_ffn_norm': out['m_pre_ffn_norm'], 'm_w_up': out['m_w_up'], 'm_conv_w': out['m_conv_w'], 'm_conv_b': out['m_conv_b'], 'm_w_down': out['m_w_down'], 'm_post_ffn_norm': out['m_post_ffn_norm'], 'v_pre_mix_norm': out['v_pre_mix_norm'], 'v_w_in': out['v_w_in'], 'v_q_a_norm': out['v_q_a_norm'], 'v_w_uq': out['v_w_uq'], 'v_kv_a_norm': out['v_kv_a_norm'], 'v_w_ukv': out['v_w_ukv'], 'v_b_forget': out['v_b_forget'], 'v_b_gate': out['v_b_gate'], 'v_w_branch_mla': out['v_w_branch_mla'], 'v_w_branch_fox': out['v_w_branch_fox'], 'v_w_out': out['v_w_out'], 'v_post_mix_norm': out['v_post_mix_norm'], 'v_pre_ffn_norm': out['v_pre_ffn_norm'], 'v_w_up': out['v_w_up'], 'v_conv_w': out['v_conv_w'], 'v_conv_b': out['v_conv_b'], 'v_w_down': out['v_w_down'], 'v_post_ffn_norm': out['v_post_ffn_norm']}


def _loss(weights, diff, rest, loss_target):
    with _jax.named_scope("forward"):
        args = {**rest, TWIN_DIFF_INPUT: diff, **{k: w.astype(_WEIGHT_DTYPES[k]) for k, w in weights.items()}}
        y = _forward(args)
    with _jax.named_scope("loss_head"):
        err = _jnp.square(y.astype(_jnp.float32) - loss_target)
        return 0.5 * _jnp.sum(_jnp.mean(err, axis=-1)) if err.ndim else 0.5 * err


def _adamw(w, g, m, v):
    m = ADAM_B1 * m + (1.0 - ADAM_B1) * g
    v = ADAM_B2 * v + (1.0 - ADAM_B2) * _jnp.square(g)
    m_hat = m / (1.0 - ADAM_B1 ** ADAM_STEP)
    v_hat = v / (1.0 - ADAM_B2 ** ADAM_STEP)
    delta = -ADAM_LR * (m_hat / (_jnp.sqrt(v_hat) + ADAM_EPS) + ADAM_WD * w)
    return delta, m, v


def reference(x, positions, pre_mix_norm, w_in, q_a_norm, w_uq, kv_a_norm, w_ukv, b_forget, b_gate, w_branch_mla, w_branch_fox, w_out, post_mix_norm, pre_ffn_norm, w_up, conv_w, conv_b, w_down, post_ffn_norm, loss_target, m_pre_mix_norm, m_w_in, m_q_a_norm, m_w_uq, m_kv_a_norm, m_w_ukv, m_b_forget, m_b_gate, m_w_branch_mla, m_w_branch_fox, m_w_out, m_post_mix_norm, m_pre_ffn_norm, m_w_up, m_conv_w, m_conv_b, m_w_down, m_post_ffn_norm, v_pre_mix_norm, v_w_in, v_q_a_norm, v_w_uq, v_kv_a_norm, v_w_ukv, v_b_forget, v_b_gate, v_w_branch_mla, v_w_branch_fox, v_w_out, v_post_mix_norm, v_pre_ffn_norm, v_w_up, v_conv_w, v_conv_b, v_w_down, v_post_ffn_norm):
    given = dict(x=x, positions=positions, pre_mix_norm=pre_mix_norm, w_in=w_in, q_a_norm=q_a_norm, w_uq=w_uq, kv_a_norm=kv_a_norm, w_ukv=w_ukv, b_forget=b_forget, b_gate=b_gate, w_branch_mla=w_branch_mla, w_branch_fox=w_branch_fox, w_out=w_out, post_mix_norm=post_mix_norm, pre_ffn_norm=pre_ffn_norm, w_up=w_up, conv_w=conv_w, conv_b=conv_b, w_down=w_down, post_ffn_norm=post_ffn_norm, loss_target=loss_target, m_pre_mix_norm=m_pre_mix_norm, m_w_in=m_w_in, m_q_a_norm=m_q_a_norm, m_w_uq=m_w_uq, m_kv_a_norm=m_kv_a_norm, m_w_ukv=m_w_ukv, m_b_forget=m_b_forget, m_b_gate=m_b_gate, m_w_branch_mla=m_w_branch_mla, m_w_branch_fox=m_w_branch_fox, m_w_out=m_w_out, m_post_mix_norm=m_post_mix_norm, m_pre_ffn_norm=m_pre_ffn_norm, m_w_up=m_w_up, m_conv_w=m_conv_w, m_conv_b=m_conv_b, m_w_down=m_w_down, m_post_ffn_norm=m_post_ffn_norm, v_pre_mix_norm=v_pre_mix_norm, v_w_in=v_w_in, v_q_a_norm=v_q_a_norm, v_w_uq=v_w_uq, v_kv_a_norm=v_kv_a_norm, v_w_ukv=v_w_ukv, v_b_forget=v_b_forget, v_b_gate=v_b_gate, v_w_branch_mla=v_w_branch_mla, v_w_branch_fox=v_w_branch_fox, v_w_out=v_w_out, v_post_mix_norm=v_post_mix_norm, v_pre_ffn_norm=v_pre_ffn_norm, v_w_up=v_w_up, v_conv_w=v_conv_w, v_conv_b=v_conv_b, v_w_down=v_w_down, v_post_ffn_norm=v_post_ffn_norm)
    weights = {n: given[n] for n in TWIN_WEIGHTS}
    shared = {n: given[n] for n in SHARED_INPUTS}
    per_example = {n: given[n] for n in ['x', 'positions']}
    grad_fn = _jax.value_and_grad(_loss, argnums=(0, 1))

    def one_microbatch(ex, loss_target):
        ex = dict(ex)
        diff = ex.pop(TWIN_DIFF_INPUT)
        return grad_fn(weights, diff, {**shared, **ex}, loss_target)

    if N_MICROBATCH == 1:
        loss, (grad_w, grad_x) = one_microbatch(per_example, given["loss_target"])
    else:
        def body(carry, xs):
            loss_sum, grad_sum = carry
            l_k, (gw_k, gx_k) = one_microbatch(xs[0], xs[1])
            with _jax.named_scope("update"):
                return (loss_sum + l_k, _jax.tree.map(_jnp.add, grad_sum, gw_k)), gx_k

        init = (_jnp.zeros((), _jnp.float32), _jax.tree.map(_jnp.zeros_like, weights))
        (loss, grad_w), grad_x = _jax.lax.scan(body, init, (per_example, given["loss_target"]))
    with _jax.named_scope("update"):
        delta_w, new_m, new_v = {}, {}, {}
        for n in TWIN_WEIGHTS:
            delta_w[n], new_m[n], new_v[n] = _adamw(weights[n], grad_w[n], given["m_" + n], given["v_" + n])
    return (loss, grad_x, *[grad_w[n] for n in TWIN_WEIGHTS], *[delta_w[n] for n in TWIN_WEIGHTS],
            *[new_m[n] for n in TWIN_WEIGHTS], *[new_v[n] for n in TWIN_WEIGHTS])
```

```python
import functools
import math

import jax
import jax.numpy as jnp
from jax import lax
from jax.experimental import pallas as pl
from jax.experimental.pallas import tpu as pltpu

F32, BF16 = jnp.float32, jnp.bfloat16
MESH = pl.DeviceIdType.MESH

HEADS = 8
NOPE, ROPE, VDIM = 128, 64, 128
QL, KVL = 512, 256
FDIM = 128
CHUNK = 64
ROPE_THETA = 10000.0
EPS = 1e-6
NEG_INF = -1e30
ADAM_LR, ADAM_B1, ADAM_B2, ADAM_EPS, ADAM_WD, ADAM_STEP = 0.001, 0.9, 0.999, 1e-08, 0.01, 10

VMEM_LIMIT_BYTES = 52 * 1024 * 1024
LANES = 128
N_CHIPS = 4


def _params(sem):
    return pltpu.CompilerParams(dimension_semantics=sem, vmem_limit_bytes=VMEM_LIMIT_BYTES)


def _tile(n, target, mult):
    if n <= target:
        return n
    t = (target // mult) * mult
    while t >= mult:
        if n % t == 0:
            return t
        t -= mult
    raise ValueError(f"no tile for {n} (target {target}, multiple of {mult})")


class _Plain:
    def __init__(self, perm=None):
        self.perm = perm

    def spec(self, tr, tc, rc):
        perm = self.perm

        def imap(i, j, k):
            r, c = rc(i, j, k)
            return (r, perm(c) if perm is not None else c)

        return pl.BlockSpec((tr, tc), imap)

    def shape(self, rows, cols):
        return (rows, cols)


class _Chunked:
    def __init__(self, n):
        self.n = n

    def spec(self, tr, tc, rc):
        assert self.n % tc == 0, (self.n, tc)
        per = self.n // tc

        def imap(i, j, k):
            r, c = rc(i, j, k)
            return (c // per, r, c % per)

        return pl.BlockSpec((None, tr, tc), imap)

    def shape(self, rows, cols):
        assert cols == N_CHIPS * self.n
        return (N_CHIPS, rows, self.n)


_DIMS = {"nn": (((1,), (0,)), ((), ())), "nt": (((1,), (1,)), ((), ())), "tn": (((0,), (0,)), ((), ()))}


def _mm(name, mode, a, b, m, n, k, *, tm=1024, tn=1024, tk=512, la=None, lb=None, lo=None, out_dtype=F32):
    la, lb, lo = la or _Plain(), lb or _Plain(), lo or _Plain()
    tm, tn, tk = _tile(m, tm, 128), _tile(n, tn, 128), _tile(k, tk, 128)
    nk = k // tk
    if mode == "nn":
        a_spec = la.spec(tm, tk, lambda i, j, kk: (i, kk))
        b_spec = lb.spec(tk, tn, lambda i, j, kk: (kk, j))
    elif mode == "nt":
        a_spec = la.spec(tm, tk, lambda i, j, kk: (i, kk))
        b_spec = lb.spec(tn, tk, lambda i, j, kk: (j, kk))
    else:
        a_spec = la.spec(tk, tm, lambda i, j, kk: (kk, i))
        b_spec = lb.spec(tk, tn, lambda i, j, kk: (kk, j))
    o_spec = lo.spec(tm, tn, lambda i, j, kk: (i, j))
    dims = _DIMS[mode]

    def body(a_ref, b_ref, o_ref, acc_ref):
        kk = pl.program_id(2)

        @pl.when(kk == 0)
        def _():
            acc_ref[...] = jnp.zeros_like(acc_ref)

        acc_ref[...] += lax.dot_general(a_ref[...].astype(BF16), b_ref[...].astype(BF16), dims,
                                        preferred_element_type=F32)

        @pl.when(kk == nk - 1)
        def _():
            o_ref[...] = acc_ref[...].astype(o_ref.dtype)

    return pl.pallas_call(
        body, name=name, grid=(m // tm, n // tn, nk),
        in_specs=[a_spec, b_spec], out_specs=o_spec,
        out_shape=jax.ShapeDtypeStruct(lo.shape(m, n), out_dtype),
        scratch_shapes=[pltpu.VMEM((tm, tn), F32)],
        compiler_params=_params(("parallel", "parallel", "arbitrary")),
    )(a, b)


def _rows(name, fn, rows_in, vecs_in, rows_out, accs_out, n_rows, tr=256):
    tr = _tile(n_rows, tr, 16)
    nr, nv, no = len(rows_in), len(vecs_in), len(rows_out)

    def body(*refs):
        ins, vecs = refs[:nr], refs[nr:nr + nv]
        outs, accs = refs[nr + nv:nr + nv + no], refs[nr + nv + no:]
        ro, ac = fn([r[...] for r in ins], [v[...] for v in vecs])
        for o_ref, val in zip(outs, ro):
            o_ref[...] = val.astype(o_ref.dtype)
        if accs:
            @pl.when(pl.program_id(0) == 0)
            def _():
                for a_ref in accs:
                    a_ref[...] = jnp.zeros_like(a_ref)

            for a_ref, val in zip(accs, ac):
                a_ref[...] += val

    in_specs = [pl.BlockSpec((tr, cols), functools.partial(lambda i, cb: (i, cb), cb=cb)) for _, cols, cb in rows_in]
    in_specs += [pl.BlockSpec(v.shape, lambda i: (0, 0)) for v in vecs_in]
    out_specs = [pl.BlockSpec((tr, cols), lambda i: (i, 0)) for cols, _ in rows_out]
    out_specs += [pl.BlockSpec((r, cols), lambda i: (0, 0)) for r, cols in accs_out]
    out_shape = [jax.ShapeDtypeStruct((n_rows, cols), dt) for cols, dt in rows_out]
    out_shape += [jax.ShapeDtypeStruct((r, cols), F32) for r, cols in accs_out]
    res = pl.pallas_call(
        body, name=name, grid=(n_rows // tr,), in_specs=in_specs, out_specs=out_specs, out_shape=out_shape,
        compiler_params=_params(("arbitrary",)),
    )(*[a for a, _, _ in rows_in], *vecs_in)
    return res


def _colsum(v):
    return jnp.sum(v, axis=0, keepdims=True)


def _rstd(x):
    return lax.rsqrt(jnp.mean(x * x, axis=-1, keepdims=True) + EPS)


def _rms_bwd(x, g, dy):
    r = _rstd(x)
    xh = x * r
    dxh = dy * g
    dx = r * (dxh - xh * jnp.mean(dxh * xh, axis=-1, keepdims=True))
    return dx, _colsum(dy * xh)


def _sigmoid(z):
    return 1.0 / (1.0 + jnp.exp(-z))


_GELU_K = math.sqrt(2.0 / math.pi)


def _gelu_parts(g):
    t = jnp.tanh(_GELU_K * (g + 0.044715 * g * g * g))
    gel = 0.5 * g * (1.0 + t)
    dgel = 0.5 * (1.0 + t) + 0.5 * g * (1.0 - t * t) * (_GELU_K * (1.0 + 3.0 * 0.044715 * g * g))
    return gel, dgel


def _visible(qi, ki, t, unit):
    rows = qi * t + lax.broadcasted_iota(jnp.int32, (t, t), 0)
    cols = ki * t + lax.broadcasted_iota(jnp.int32, (t, t), 1)
    if unit > 1:
        sh = int(math.log2(unit))
        assert 1 << sh == unit
        rows, cols = jnp.right_shift(rows, sh), jnp.right_shift(cols, sh)
    return cols <= rows


def _cat(refs):
    vals = [r[...].astype(BF16) for r in refs]
    return vals[0] if len(vals) == 1 else jnp.concatenate(vals, axis=1)


def _piece_spec(t, piece, row_of):
    _, base, stride = piece
    return pl.BlockSpec((t, LANES), lambda b, h, i, j: (row_of(b, i, j), base + stride * h))


def _scores(q_refs, k_refs, bias_refs, qi, ki, t, unit, scale):
    s = lax.dot_general(_cat(q_refs), _cat(k_refs), _DIMS["nt"], preferred_element_type=F32) * scale
    if bias_refs:
        s = s + bias_refs[0][...] - bias_refs[1][...]
    return jnp.where(_visible(qi, ki, t, unit), s, NEG_INF)


def _attn_fwd(name, qp, kp, vp, bias, unit, scale, n_seq, seq, t):
    nb = seq // t
    n_tok = n_seq * seq
    nq, nk_p = len(qp), len(kp)
    nbias = 2 if bias is not None else 0

    def body(*refs):
        q_refs, k_refs = refs[:nq], refs[nq:nq + nk_p]
        v_ref = refs[nq + nk_p]
        bias_refs = refs[nq + nk_p + 1:nq + nk_p + 1 + nbias]
        o_ref, lse_ref, m_s, l_s, acc_s = refs[nq + nk_p + 1 + nbias:]
        qi, ki = pl.program_id(2), pl.program_id(3)

        @pl.when(ki == 0)
        def _():
            m_s[...] = jnp.full_like(m_s, NEG_INF)
            l_s[...] = jnp.zeros_like(l_s)
            acc_s[...] = jnp.zeros_like(acc_s)

        @pl.when(ki <= qi)
        def _():
            s = _scores(q_refs, k_refs, bias_refs, qi, ki, t, unit, scale)
            m_new = jnp.maximum(m_s[...], jnp.max(s, axis=1, keepdims=True))
            alpha = jnp.exp(m_s[...] - m_new)
            p = jnp.exp(s - m_new)
            l_s[...] = alpha * l_s[...] + jnp.sum(p, axis=1, keepdims=True)
            acc_s[...] = alpha * acc_s[...] + jnp.dot(p.astype(BF16), v_ref[...].astype(BF16),
                                                      preferred_element_type=F32)
            m_s[...] = m_new

        @pl.when(ki == qi)
        def _():
            o_ref[...] = acc_s[...] / l_s[...]
            lse_ref[...] = m_s[...] + jnp.log(l_s[...])

    q_row = lambda b, i, j: b * nb + i
    k_row = lambda b, i, j: b * nb + jnp.minimum(j, i)
    in_specs = [_piece_spec(t, p, q_row) for p in qp] + [_piece_spec(t, p, k_row) for p in kp]
    in_specs.append(_piece_spec(t, vp, k_row))
    args = [p[0] for p in qp] + [p[0] for p in kp] + [vp[0]]
    if bias is not None:
        in_specs.append(pl.BlockSpec((None, t, 1), lambda b, h, i, j: (h, b * nb + i, 0)))
        in_specs.append(pl.BlockSpec((None, 1, t), lambda b, h, i, j: (b * HEADS + h, 0, jnp.minimum(j, i))))
        args += list(bias)
    return pl.pallas_call(
        body, name=name, grid=(n_seq, HEADS, nb, nb), in_specs=in_specs,
        out_specs=[pl.BlockSpec((t, LANES), lambda b, h, i, j: (b * nb + i, h)),
                   pl.BlockSpec((None, t, 1), lambda b, h, i, j: (h, b * nb + i, 0))],
        out_shape=[jax.ShapeDtypeStruct((n_tok, HEADS * LANES), F32),
                   jax.ShapeDtypeStruct((HEADS, n_tok, 1), F32)],
        scratch_shapes=[pltpu.VMEM((t, 1), F32), pltpu.VMEM((t, 1), F32), pltpu.VMEM((t, LANES), F32)],
        compiler_params=_params(("parallel", "parallel", "arbitrary", "arbitrary")),
    )(*args)


def _attn_bwd_dq(name, qp, kp, vp, o, do, lse, bias, unit, scale, n_seq, seq, t):
    nb = seq // t
    n_tok = n_seq * seq
    nq, nk_p = len(qp), len(kp)
    nbias = 2 if bias is not None else 0
    n_in = nq + nk_p + 4 + nbias
    n_out = nq + (1 if bias is not None else 0)

    def body(*refs):
        q_refs, k_refs = refs[:nq], refs[nq:nq + nk_p]
        v_ref, o_ref, do_ref, lse_ref = refs[nq + nk_p:nq + nk_p + 4]
        bias_refs = refs[nq + nk_p + 4:n_in]
        outs = refs[n_in:n_in + n_out]
        dq_s, delta_s, dc_s = refs[n_in + n_out:]
        qi, ki = pl.program_id(2), pl.program_id(3)

        @pl.when(ki == 0)
        def _():
            dq_s[...] = jnp.zeros_like(dq_s)
            dc_s[...] = jnp.zeros_like(dc_s)
            delta_s[...] = jnp.sum(do_ref[...] * o_ref[...], axis=1, keepdims=True)

        @pl.when(ki <= qi)
        def _():
            s = _scores(q_refs, k_refs, bias_refs, qi, ki, t, unit, scale)
            p = jnp.exp(s - lse_ref[...])
            dp = lax.dot_general(do_ref[...].astype(BF16), v_ref[...].astype(BF16), _DIMS["nt"],
                                 preferred_element_type=F32)
            ds = p * (dp - delta_s[...])
            dq_s[...] += jnp.dot(ds.astype(BF16), _cat(k_refs), preferred_element_type=F32)
            dc_s[...] += jnp.sum(ds, axis=1, keepdims=True)

        @pl.when(ki == qi)
        def _():
            for n_p in range(nq):
                outs[n_p][...] = dq_s[:, n_p * LANES:(n_p + 1) * LANES] * scale
            if bias is not None:
                outs[nq][...] = dc_s[...]

    q_row = lambda b, i, j: b * nb + i
    k_row = lambda b, i, j: b * nb + jnp.minimum(j, i)
    head_q = pl.BlockSpec((t, LANES), lambda b, h, i, j: (b * nb + i, h))
    col_q = pl.BlockSpec((None, t, 1), lambda b, h, i, j: (h, b * nb + i, 0))
    in_specs = [_piece_spec(t, p, q_row) for p in qp] + [_piece_spec(t, p, k_row) for p in kp]
    in_specs += [_piece_spec(t, vp, k_row), head_q, head_q, col_q]
    args = [p[0] for p in qp] + [p[0] for p in kp] + [vp[0], o, do, lse]
    if bias is not None:
        in_specs += [col_q, pl.BlockSpec((None, 1, t), lambda b, h, i, j: (b * HEADS + h, 0, jnp.minimum(j, i)))]
        args += list(bias)
    out_specs = [head_q] * nq + ([col_q] if bias is not None else [])
    out_shape = [jax.ShapeDtypeStruct((n_tok, HEADS * LANES), F32)] * nq
    if bias is not None:
        out_shape.append(jax.ShapeDtypeStruct((HEADS, n_tok, 1), F32))
    return pl.pallas_call(
        body, name=name, grid=(n_seq, HEADS, nb, nb), in_specs=in_specs, out_specs=out_specs, out_shape=out_shape,
        scratch_shapes=[pltpu.VMEM((t, nq * LANES), F32), pltpu.VMEM((t, 1), F32), pltpu.VMEM((t, 1), F32)],
        compiler_params=_params(("parallel", "parallel", "arbitrary", "arbitrary")),
    )(*args)


def _attn_bwd_dkv(name, qp, kp, vp, o, do, lse, bias, unit, scale, n_seq, seq, t):
    nb = seq // t
    n_tok = n_seq * seq
    nq, nk_p = len(qp), len(kp)
    nbias = 2 if bias is not None else 0
    n_in = nq + nk_p + 4 + nbias
    n_out = nk_p + 1 + (1 if bias is not None else 0)

    def body(*refs):
        q_refs, k_refs = refs[:nq], refs[nq:nq + nk_p]
        v_ref, o_ref, do_ref, lse_ref = refs[nq + nk_p:nq + nk_p + 4]
        bias_refs = refs[nq + nk_p + 4:n_in]
        outs = refs[n_in:n_in + n_out]
        dk_s, dv_s, dc_s = refs[n_in + n_out:]
        ki, qi = pl.program_id(2), pl.program_id(3)

        @pl.when(qi == 0)
        def _():
            dk_s[...] = jnp.zeros_like(dk_s)
            dv_s[...] = jnp.zeros_like(dv_s)
            dc_s[...] = jnp.zeros_like(dc_s)

        @pl.when(qi >= ki)
        def _():
            s = _scores(q_refs, k_refs, bias_refs, qi, ki, t, unit, scale)
            p = jnp.exp(s - lse_ref[...])
            do_b = do_ref[...].astype(BF16)
            delta = jnp.sum(do_ref[...] * o_ref[...], axis=1, keepdims=True)
            dp = lax.dot_general(do_b, v_ref[...].astype(BF16), _DIMS["nt"], preferred_element_type=F32)
            ds = p * (dp - delta)
            dv_s[...] += lax.dot_general(p.astype(BF16), do_b, _DIMS["tn"], preferred_element_type=F32)
            dk_s[...] += lax.dot_general(ds.astype(BF16), _cat(q_refs), _DIMS["tn"], preferred_element_type=F32)
            dc_s[...] -= jnp.sum(ds, axis=0, keepdims=True)

        @pl.when(qi == nb - 1)
        def _():
            for n_p in range(nk_p):
                outs[n_p][...] = dk_s[:, n_p * LANES:(n_p + 1) * LANES] * scale
            outs[nk_p][...] = dv_s[...]
            if bias is not None:
                outs[nk_p + 1][...] = dc_s[...]

    q_row = lambda b, i, j: b * nb + jnp.maximum(j, i)
    k_row = lambda b, i, j: b * nb + i
    head_q = pl.BlockSpec((t, LANES), lambda b, h, i, j: (b * nb + jnp.maximum(j, i), h))
    col_q = pl.BlockSpec((None, t, 1), lambda b, h, i, j: (h, b * nb + jnp.maximum(j, i), 0))
    head_k = pl.BlockSpec((t, LANES), lambda b, h, i, j: (b * nb + i, h))
    row_k = pl.BlockSpec((None, 1, t), lambda b, h, i, j: (b * HEADS + h, 0, i))
    in_specs = [_piece_spec(t, p, q_row) for p in qp] + [_piece_spec(t, p, k_row) for p in kp]
    in_specs += [_piece_spec(t, vp, k_row), head_q, head_q, col_q]
    args = [p[0] for p in qp] + [p[0] for p in kp] + [vp[0], o, do, lse]
    if bias is not None:
        in_specs += [col_q, row_k]
        args += list(bias)
    out_specs = [head_k] * (nk_p + 1) + ([row_k] if bias is not None else [])
    out_shape = [jax.ShapeDtypeStruct((n_tok, HEADS * LANES), F32)] * (nk_p + 1)
    if bias is not None:
        out_shape.append(jax.ShapeDtypeStruct((n_seq * HEADS, 1, seq), F32))
    return pl.pallas_call(
        body, name=name, grid=(n_seq, HEADS, nb, nb), in_specs=in_specs, out_specs=out_specs, out_shape=out_shape,
        scratch_shapes=[pltpu.VMEM((t, nk_p * LANES), F32), pltpu.VMEM((t, LANES), F32), pltpu.VMEM((1, t), F32)],
        compiler_params=_params(("parallel", "parallel", "arbitrary", "arbitrary")),
    )(*args)


def _seq_cumsum(name, x, col_block, n_seq, seq, reverse, pre=None, vec=None):
    t = _tile(seq, 256, 128)
    nb = seq // t

    def body(*refs):
        x_ref = refs[0]
        vec_ref = refs[1] if vec is not None else None
        o_ref, carry = refs[-2], refs[-1]

        @pl.when(pl.program_id(1) == 0)
        def _():
            carry[...] = jnp.zeros_like(carry)

        v = x_ref[...]
        if pre is not None:
            v = pre(v, vec_ref[...])
        r = lax.broadcasted_iota(jnp.int32, (t, t), 0)
        c = lax.broadcasted_iota(jnp.int32, (t, t), 1)
        tri = jnp.where((c >= r) if reverse else (c <= r), 1.0, 0.0).astype(BF16)
        hi = v.astype(BF16)
        mid = (v - hi.astype(F32)).astype(BF16)
        lo = (v - hi.astype(F32) - mid.astype(F32)).astype(BF16)
        acc = jnp.dot(tri, hi, preferred_element_type=F32)
        acc += jnp.dot(tri, mid, preferred_element_type=F32)
        acc += jnp.dot(tri, lo, preferred_element_type=F32)
        o_ref[...] = acc + carry[...]
        carry[...] += _colsum(v)

    blk = (lambda b, i: (b * nb + nb - 1 - i)) if reverse else (lambda b, i: (b * nb + i))
    in_specs = [pl.BlockSpec((t, LANES), lambda b, i: (blk(b, i), col_block))]
    args = [x]
    if vec is not None:
        in_specs.append(pl.BlockSpec(vec.shape, lambda b, i: (0, 0)))
        args.append(vec)
    return pl.pallas_call(
        body, name=name, grid=(n_seq, nb), in_specs=in_specs,
        out_specs=pl.BlockSpec((t, LANES), lambda b, i: (blk(b, i), 0)),
        out_shape=jax.ShapeDtypeStruct((n_seq * seq, LANES), F32),
        scratch_shapes=[pltpu.VMEM((1, LANES), F32)],
        compiler_params=_params(("arbitrary", "arbitrary")),
    )(*args)


def _log_sigmoid(z):
    return -(jnp.maximum(-z, 0.0) + jnp.log(1.0 + jnp.exp(-jnp.abs(z))))


def _shift_down(u, prev_ref, n):
    out = pltpu.roll(u, n, 0)
    row = lax.broadcasted_iota(jnp.int32, u.shape, 0)
    for r in range(n):
        out = jnp.where(row == r, prev_ref[8 - n + r:8 - n + r + 1, :], out)
    return out


def _shift_up(u, next_ref, n):
    ts = u.shape[0]
    out = pltpu.roll(u, ts - n, 0)
    row = lax.broadcasted_iota(jnp.int32, u.shape, 0)
    for r in range(n):
        out = jnp.where(row == ts - n + r, next_ref[r:r + 1, :], out)
    return out


def _conv_taps(u, prev_ref, w_ref, b_ref):
    s1, s2 = _shift_down(u, prev_ref, 1), _shift_down(u, prev_ref, 2)
    return (w_ref[0:1, :] * s2 + w_ref[1:2, :] * s1 + w_ref[2:3, :] * u) + b_ref[...], s1, s2


def _conv_glu_fwd(u_il, cw_il, cb_il, n_seq, seq, wt):
    n_tok, two_f = u_il.shape
    nct = two_f // (2 * wt)
    ts = _tile(seq, 256, 8)
    ns = seq // ts

    def body(u_ref, w_ref, b_ref, a_ref, carry):
        @pl.when(pl.program_id(2) == 0)
        def _():
            carry[...] = jnp.zeros_like(carry)

        u = u_ref[...]
        uc, _, _ = _conv_taps(u, carry, w_ref, b_ref)
        gel, _ = _gelu_parts(uc[:, :wt])
        a_ref[...] = (gel * uc[:, wt:]).astype(a_ref.dtype)
        carry[...] = u[ts - 8:, :]

    return pl.pallas_call(
        body, name="conv_glu_fwd", grid=(nct, n_seq, ns),
        in_specs=[pl.BlockSpec((ts, 2 * wt), lambda j, b, s: (b * ns + s, j)),
                  pl.BlockSpec((3, 2 * wt), lambda j, b, s: (0, j)),
                  pl.BlockSpec((1, 2 * wt), lambda j, b, s: (0, j))],
        out_specs=pl.BlockSpec((ts, wt), lambda j, b, s: (b * ns + s, j)),
        out_shape=jax.ShapeDtypeStruct((n_tok, two_f // 2), BF16),
        scratch_shapes=[pltpu.VMEM((8, 2 * wt), F32)],
        compiler_params=_params(("parallel", "arbitrary", "arbitrary")),
    )(u_il, cw_il, cb_il)


def _conv_glu_bwd_pre(u_il, da, cw_il, cb_il, n_seq, seq, wt):
    n_tok, two_f = u_il.shape
    nct = two_f // (2 * wt)
    ts = _tile(seq, 256, 8)
    ns = seq // ts

    def body(u_ref, da_ref, w_ref, b_ref, d_ref, acc_ref, carry):
        first = jnp.logical_and(pl.program_id(1) == 0, pl.program_id(2) == 0)

        @pl.when(first)
        def _():
            acc_ref[...] = jnp.zeros_like(acc_ref)

        @pl.when(pl.program_id(2) == 0)
        def _():
            carry[...] = jnp.zeros_like(carry)

        u = u_ref[...]
        uc, s1, s2 = _conv_taps(u, carry, w_ref, b_ref)
        gel, dgel = _gelu_parts(uc[:, :wt])
        da_v = da_ref[...]
        d = jnp.concatenate([da_v * uc[:, wt:] * dgel, da_v * gel], axis=1)
        d_ref[...] = d
        acc_ref[0:1, :] += _colsum(d * s2)
        acc_ref[1:2, :] += _colsum(d * s1)
        acc_ref[2:3, :] += _colsum(d * u)
        acc_ref[3:4, :] += _colsum(d)
        carry[...] = u[ts - 8:, :]

    return pl.pallas_call(
        body, name="conv_glu_bwd_pre", grid=(nct, n_seq, ns),
        in_specs=[pl.BlockSpec((ts, 2 * wt), lambda j, b, s: (b * ns + s, j)),
                  pl.BlockSpec((ts, wt), lambda j, b, s: (b * ns + s, j)),
                  pl.BlockSpec((3, 2 * wt), lambda j, b, s: (0, j)),
                  pl.BlockSpec((1, 2 * wt), lambda j, b, s: (0, j))],
        out_specs=[pl.BlockSpec((ts, 2 * wt), lambda j, b, s: (b * ns + s, j)),
                   pl.BlockSpec((8, 2 * wt), lambda j, b, s: (0, j))],
        out_shape=[jax.ShapeDtypeStruct((n_tok, two_f), F32), jax.ShapeDtypeStruct((8, two_f), F32)],
        scratch_shapes=[pltpu.VMEM((8, 2 * wt), F32)],
        compiler_params=_params(("parallel", "arbitrary", "arbitrary")),
    )(u_il, da, cw_il, cb_il)


def _conv_bwd_input(d_il, cw_il, n_seq, seq, wt):
    n_tok, two_f = d_il.shape
    nct = two_f // (2 * wt)
    ts = _tile(seq, 256, 8)
    ns = seq // ts

    def body(d_ref, w_ref, o_ref, carry):
        @pl.when(pl.program_id(2) == 0)
        def _():
            carry[...] = jnp.zeros_like(carry)

        d = d_ref[...]
        o_ref[...] = (w_ref[2:3, :] * d + w_ref[1:2, :] * _shift_up(d, carry, 1)
                      + w_ref[0:1, :] * _shift_up(d, carry, 2)).astype(o_ref.dtype)
        carry[...] = d[:8, :]

    rev = lambda j, b, s: (b * ns + ns - 1 - s, j)
    return pl.pallas_call(
        body, name="conv_bwd_input", grid=(nct, n_seq, ns),
        in_specs=[pl.BlockSpec((ts, 2 * wt), rev), pl.BlockSpec((3, 2 * wt), lambda j, b, s: (0, j))],
        out_specs=pl.BlockSpec((ts, 2 * wt), rev),
        out_shape=jax.ShapeDtypeStruct((n_tok, two_f), BF16),
        scratch_shapes=[pltpu.VMEM((8, 2 * wt), F32)],
        compiler_params=_params(("parallel", "arbitrary", "arbitrary")),
    )(d_il, cw_il)


HBM = pl.BlockSpec(memory_space=pltpu.HBM)
_CHIP_FLIPS = ((1, 0), (0, 1), (1, 1))


def _place():
    x, y, c = lax.axis_index("x"), lax.axis_index("y"), lax.axis_index("c")
    return x, y, c, 2 * x + y


def _flip(v, f):
    return 1 - v if f else v


def _half_rows(c, half):
    return pl.ds(pl.multiple_of(c * half, 16), half)


def _remote(src, dst, ssem, rsem, dev):
    return pltpu.make_async_remote_copy(src_ref=src, dst_ref=dst, send_sem=ssem, recv_sem=rsem,
                                        device_id=dev, device_id_type=MESH)


def _comm_call(name, body, ins, out_shapes, n_remote, n_local):
    return pl.pallas_call(
        body, name=name, in_specs=[HBM] * len(ins), out_specs=[HBM] * len(out_shapes),
        out_shape=[pltpu.HBM(s.shape, s.dtype) for s in out_shapes],
        scratch_shapes=[pltpu.SemaphoreType.DMA((n_remote,)), pltpu.SemaphoreType.DMA((n_remote,)),
                        pltpu.SemaphoreType.DMA((n_local,))],
    )(*ins)


def _all_gather_weights(shards, smalls):
    n, ns = len(shards), len(smalls)
    nt = n + ns

    def body(*refs):
        src, dst = refs[:nt], refs[nt:2 * nt]
        ssem, rsem, lsem = refs[2 * nt:]
        x, y, c, me = _place()
        sib = (x, y, 1 - c)
        waits = []
        for w in range(nt):
            loc = pltpu.make_async_copy(src[w], dst[w].at[me], lsem.at[w])
            loc.start()
            waits.append(loc)
        small_cps = []
        for s in range(ns):
            for k, (fx, fy) in enumerate(_CHIP_FLIPS):
                sem = 6 * n + 3 * s + k
                cp = _remote(src[n + s], dst[n + s].at[me], ssem.at[sem], rsem.at[sem], (_flip(x, fx), _flip(y, fy), c))
                cp.start()
                small_cps.append(cp)
        for w in range(n):
            rows = _half_rows(c, shards[w].shape[0] // 2)
            for k, (fx, fy) in enumerate(_CHIP_FLIPS):
                cp = _remote(src[w].at[rows], dst[w].at[me, rows], ssem.at[w * 6 + k], rsem.at[w * 6 + k],
                             (_flip(x, fx), _flip(y, fy), c))
                cp.start()
                waits.append(cp)
        fwd = []
        for w in range(n):
            rows = _half_rows(c, shards[w].shape[0] // 2)
            for k, (fx, fy) in enumerate(_CHIP_FLIPS):
                peer = 2 * _flip(x, fx) + _flip(y, fy)
                got = dst[w].at[peer, rows]
                _remote(got, got, ssem.at[w * 6 + k], rsem.at[w * 6 + k], sib).wait_recv()
                cp = _remote(got, got, ssem.at[w * 6 + 3 + k], rsem.at[w * 6 + 3 + k], sib)
                cp.start()
                fwd.append(cp)
        for w in range(n):
            rows = _half_rows(1 - c, shards[w].shape[0] // 2)
            for k, (fx, fy) in enumerate(_CHIP_FLIPS):
                peer = 2 * _flip(x, fx) + _flip(y, fy)
                got = dst[w].at[peer, rows]
                _remote(got, got, ssem.at[w * 6 + 3 + k], rsem.at[w * 6 + 3 + k], sib).wait_recv()
        for cp in small_cps:
            cp.wait()
        for cp in waits[nt:] + fwd:
            cp.wait_send()
        for loc in waits[:nt]:
            loc.wait()

    outs = [jax.ShapeDtypeStruct((N_CHIPS,) + s.shape, s.dtype) for s in list(shards) + list(smalls)]
    res = _comm_call("all_gather_weights", body, list(shards) + list(smalls), outs, 6 * n + 3 * ns, nt)
    return res[:n], res[n:]


def _pair_split(grads):
    n = len(grads)

    def body(*refs):
        src, mine, got = refs[:n], refs[n:2 * n], refs[2 * n:3 * n]
        ssem, rsem, lsem = refs[3 * n:]
        x, y, c, _ = _place()
        sib = (x, y, 1 - c)
        cps, locs = [], []
        for w in range(n):
            half = grads[w].shape[1] // 2
            loc = pltpu.make_async_copy(src[w].at[:, _half_rows(c, half)], mine[w], lsem.at[w])
            loc.start()
            locs.append(loc)
            cp = _remote(src[w].at[:, _half_rows(1 - c, half)], got[w], ssem.at[w], rsem.at[w], sib)
            cp.start()
            cps.append(cp)
        for cp in cps:
            cp.wait()
        for loc in locs:
            loc.wait()

    outs = [jax.ShapeDtypeStruct((g.shape[0], g.shape[1] // 2, g.shape[2]), g.dtype) for g in grads] * 2
    res = _comm_call("rs_pair_split", body, grads, outs, n, n)
    return res[:n], res[n:]


def _chip_scatter(parts):
    n = len(parts)

    def body(*refs):
        src, dst = refs[:n], refs[n:2 * n]
        ssem, rsem, lsem = refs[2 * n:]
        x, y, c, me = _place()
        cps, locs = [], []
        for w in range(n):
            loc = pltpu.make_async_copy(src[w].at[me], dst[w].at[3], lsem.at[w])
            loc.start()
            locs.append(loc)
            for k, (fx, fy) in enumerate(_CHIP_FLIPS):
                px, py = _flip(x, fx), _flip(y, fy)
                cp = _remote(src[w].at[2 * px + py], dst[w].at[k], ssem.at[w * 3 + k], rsem.at[w * 3 + k], (px, py, c))
                cp.start()
                cps.append(cp)
        for cp in cps:
            cp.wait()
        for loc in locs:
            loc.wait()

    outs = [jax.ShapeDtypeStruct(p.shape, p.dtype) for p in parts]
    return _comm_call("rs_chip_scatter", body, parts, outs, 3 * n, n)


def _pair_join(halves):
    n = len(halves)

    def body(*refs):
        src, dst = refs[:n], refs[n:2 * n]
        ssem, rsem, lsem = refs[2 * n:]
        x, y, c, _ = _place()
        sib = (x, y, 1 - c)
        cps, locs = [], []
        for w in range(n):
            half = halves[w].shape[0]
            loc = pltpu.make_async_copy(src[w], dst[w].at[_half_rows(c, half)], lsem.at[w])
            loc.start()
            locs.append(loc)
            cp = _remote(src[w], dst[w].at[_half_rows(c, half)], ssem.at[w], rsem.at[w], sib)
            cp.start()
            cps.append(cp)
        for w, cp in enumerate(cps):
            cp.wait_send()
            theirs = dst[w].at[_half_rows(1 - c, halves[w].shape[0])]
            _remote(theirs, theirs, ssem.at[w], rsem.at[w], sib).wait_recv()
        for loc in locs:
            loc.wait()

    outs = [jax.ShapeDtypeStruct((2 * h.shape[0], h.shape[1]), h.dtype) for h in halves]
    return _comm_call("rs_pair_join", body, halves, outs, n, n)


def _gather_small(vec):
    def body(src, dst, ssem, rsem, lsem):
        x, y, c, _ = _place()
        me = 4 * x + 2 * y + c
        loc = pltpu.make_async_copy(src, dst.at[me], lsem.at[0])
        loc.start()
        cps = []
        for r in range(1, 8):
            dev = (_flip(x, r & 4), _flip(y, r & 2), _flip(c, r & 1))
            cp = _remote(src, dst.at[me], ssem.at[r - 1], rsem.at[r - 1], dev)
            cp.start()
            cps.append(cp)
        for cp in cps:
            cp.wait()
        loc.wait()

    out = jax.ShapeDtypeStruct((8,) + vec.shape, vec.dtype)
    return _comm_call("gather_small", body, [vec], [out], 7, 1)[0]


def _sum_slots(name, stacked, order, out_dtype=F32):
    n, r, c = stacked.shape
    tr = _tile(r, 256, 8)

    def body(s_ref, o_ref):
        acc = s_ref[order[0]].astype(F32)
        for s in order[1:]:
            acc = acc + s_ref[s].astype(F32)
        o_ref[...] = acc.astype(o_ref.dtype)

    return pl.pallas_call(
        body, name=name, grid=(r // tr,),
        in_specs=[pl.BlockSpec((n, tr, c), lambda i: (0, i, 0))], out_specs=pl.BlockSpec((tr, c), lambda i: (i, 0)),
        out_shape=jax.ShapeDtypeStruct((r, c), out_dtype), compiler_params=_params(("parallel",)),
    )(stacked)


def _adamw(name, w, g, m, v):
    r, c = w.shape
    tr = _tile(r, 256, 8)
    bc1, bc2 = 1.0 - ADAM_B1 ** ADAM_STEP, 1.0 - ADAM_B2 ** ADAM_STEP

    def body(w_ref, g_ref, m_ref, v_ref, d_ref, nm_ref, nv_ref):
        g_v = g_ref[...]
        nm = ADAM_B1 * m_ref[...] + (1.0 - ADAM_B1) * g_v
        nv = ADAM_B2 * v_ref[...] + (1.0 - ADAM_B2) * (g_v * g_v)
        d_ref[...] = -ADAM_LR * ((nm / bc1) / (jnp.sqrt(nv / bc2) + ADAM_EPS) + ADAM_WD * w_ref[...])
        nm_ref[...] = nm
        nv_ref[...] = nv

    spec = pl.BlockSpec((tr, c), lambda i: (i, 0))
    return pl.pallas_call(
        body, name=name, grid=(r // tr,), in_specs=[spec] * 4, out_specs=[spec] * 3,
        out_shape=[jax.ShapeDtypeStruct((r, c), F32)] * 3, compiler_params=_params(("parallel",)),
    )(w, g, m, v)


def _pad_cols(a, cols):
    return jnp.pad(a, ((0, 0), (0, cols - a.shape[1])))


def _rot_cols(w):
    h = w.shape[-1] // 2
    return jnp.concatenate([-w[..., h:], w[..., :h]], axis=-1)


def _unrot_cols(d):
    h = d.shape[-1] // 2
    return jnp.concatenate([d[..., h:], -d[..., :h]], axis=-1)


def _logical(g):
    return jnp.transpose(g, (1, 0, 2)).reshape(g.shape[1], N_CHIPS * g.shape[2])


def _chunks(a, n):
    return jnp.transpose(a.reshape(a.shape[0], N_CHIPS, n), (1, 0, 2))


def kernel(x, positions, pre_mix_norm, w_in, q_a_norm, w_uq, kv_a_norm, w_ukv, b_forget, b_gate, w_branch_mla, w_branch_fox, w_out, post_mix_norm, pre_ffn_norm, w_up, conv_w, conv_b, w_down, post_ffn_norm, loss_target, m_pre_mix_norm, m_w_in, m_q_a_norm, m_w_uq, m_kv_a_norm, m_w_ukv, m_b_forget, m_b_gate, m_w_branch_mla, m_w_branch_fox, m_w_out, m_post_mix_norm, m_pre_ffn_norm, m_w_up, m_conv_w, m_conv_b, m_w_down, m_post_ffn_norm, v_pre_mix_norm, v_w_in, v_q_a_norm, v_w_uq, v_kv_a_norm, v_w_ukv, v_b_forget, v_b_gate, v_w_branch_mla, v_w_branch_fox, v_w_out, v_post_mix_norm, v_pre_ffn_norm, v_w_up, v_conv_w, v_conv_b, v_w_down, v_post_ffn_norm):
    n_seq, seq, d = x.shape
    n_tok = n_seq * seq
    d_in = N_CHIPS * w_in.shape[1]
    two_f = N_CHIPS * w_up.shape[1]
    ff_dim = two_f // 2
    assert d_in == QL + KVL + ROPE + 3 * HEADS * FDIM + HEADS + 2 * d
    n_in_shard = w_in.shape[1]
    in_pad = -(-n_in_shard // LANES) * LANES
    hd = HEADS * LANES
    xc, yc, cc = lax.axis_index("x"), lax.axis_index("y"), lax.axis_index("c")
    chip = 2 * xc + yc
    t_attn = _tile(seq, 256, 128)

    shards = [_pad_cols(w_in, in_pad).astype(BF16), w_uq.astype(BF16), w_ukv.astype(BF16), w_branch_mla.astype(BF16),
              w_branch_fox.astype(BF16), w_out.astype(BF16), w_up.astype(BF16), w_down.astype(BF16)]
    (g_in, g_uq, g_ukv, g_bm, g_bf, g_out, g_up, g_down), (g_cw,) = _all_gather_weights(
        shards, [jnp.pad(conv_w, ((0, 5), (0, 0)))])

    win_l = jnp.transpose(g_in[:, :, :n_in_shard], (1, 0, 2)).reshape(d, d_in)
    o_q, o_kv, o_kpe = 0, QL, QL + KVL
    o_f = o_kpe + ROPE
    o_fl = o_f + 3 * hd
    o_g = o_fl + HEADS
    w_kpe = win_l[:, o_kpe:o_f]
    zeros = lambda n: jnp.zeros((d, n), BF16)
    win_p = jnp.concatenate([
        win_l[:, o_g:], win_l[:, o_f:o_fl], win_l[:, o_q:o_kpe], w_kpe, zeros(LANES - ROPE),
        _rot_cols(w_kpe), win_l[:, o_fl:o_g], zeros(LANES - ROPE - HEADS)], axis=1)
    n_p = win_p.shape[1]
    cb_gm, cb_gf = 0, 1
    cb_fq, cb_fk, cb_fv = [(2 * d + i * hd) // LANES for i in range(3)]
    c_lat = 2 * d + 3 * hd
    c_kx, c_kr = c_lat + QL + KVL, c_lat + QL + KVL + LANES
    assert n_p == c_kr + LANES

    uq3 = _logical(g_uq).reshape(QL, HEADS, NOPE + ROPE)
    pe = uq3[:, :, NOPE:]
    pad_pe = lambda a: jnp.pad(a, ((0, 0), (0, 0), (0, LANES - ROPE))).reshape(QL, hd)
    wuq_p = jnp.concatenate([uq3[:, :, :NOPE].reshape(QL, hd), pad_pe(pe), pad_pe(_rot_cols(pe))], axis=1)
    ukv3 = _logical(g_ukv).reshape(KVL, HEADS, NOPE + VDIM)
    wukv_p = jnp.concatenate([ukv3[:, :, :NOPE].reshape(KVL, hd), ukv3[:, :, NOPE:].reshape(KVL, hd)], axis=1)

    n_bm, n_up = w_branch_mla.shape[1], w_up.shape[1]
    l_bm, l_up = _Chunked(n_bm), _Chunked(n_up)
    wt = n_up // 2
    n_ut = two_f // wt
    il = lambda cblk: jnp.where(cblk < n_ut // 2, 2 * cblk, 2 * (cblk - n_ut // 2) + 1)
    l_il = _Plain(il)
    to_il = lambda a: a.reshape(a.shape[0], 2, n_ut // 2, wt).transpose(0, 2, 1, 3).reshape(a.shape[0], two_f)
    from_il = lambda a: a.reshape(a.shape[0], n_ut // 2, 2, wt).transpose(0, 2, 1, 3).reshape(a.shape[0], two_f)
    w_down_full = g_down.reshape(ff_dim, d)
    w_out_full = g_out.reshape(d, d)

    inv_freq = 1.0 / (ROPE_THETA ** (jnp.arange(0, ROPE, 2, dtype=F32) / ROPE))
    ang = positions.astype(F32).reshape(n_tok, 1) * inv_freq
    cos, sin = jnp.cos(ang), jnp.sin(ang)
    cs = _pad_cols(jnp.concatenate([cos, cos], axis=1), LANES)
    sn = _pad_cols(jnp.concatenate([sin, sin], axis=1), LANES)

    row = lambda v: v.reshape(1, -1)
    x2 = x.reshape(n_tok, d)
    tgt = loss_target.reshape(n_tok, d)

    (h,) = _rows("rms_pre_mix", lambda r, v: ([r[0] * _rstd(r[0]) * v[0]], []),
                 [(x2, d, 0)], [row(pre_mix_norm)], [(d, BF16)], [], n_tok)
    proj = _mm("proj_in", "nn", h, win_p, n_tok, n_p, d)

    bf_vec = jnp.pad(row(b_forget), ((0, 0), (ROPE, LANES - ROPE - HEADS)))

    def lat_fwd(r, v):
        ql, kvl = r[0], r[1]
        return [ql * _rstd(ql) * v[0], kvl * _rstd(kvl) * v[1], r[2] * r[4] + r[3] * r[5]], []

    qn, kvn, rk = _rows("latent_norms", lat_fwd,
                        [(proj, QL, c_lat // QL), (proj, KVL, (c_lat + QL) // KVL), (proj, LANES, c_kx // LANES),
                         (proj, LANES, c_kr // LANES), (cs, LANES, 0), (sn, LANES, 0)],
                        [row(q_a_norm), row(kv_a_norm)], [(QL, BF16), (KVL, BF16), (LANES, BF16)], [], n_tok)
    q_p = _mm("q_up", "nn", qn, wuq_p, n_tok, 3 * hd, QL)
    kv_p = _mm("kv_up", "nn", kvn, wukv_p, n_tok, 2 * hd, KVL)

    def rope_q(r, v):
        c8, s8 = jnp.tile(r[2], (1, HEADS)), jnp.tile(r[3], (1, HEADS))
        return [r[0] * c8 + r[1] * s8], []

    (rq,) = _rows("rope_q", rope_q, [(q_p, hd, 1), (q_p, hd, 2), (cs, LANES, 0), (sn, LANES, 0)], [],
                  [(hd, BF16)], [], n_tok)

    mla_q = [(q_p, 0, 1), (rq, 0, 1)]
    mla_k = [(kv_p, 0, 1), (rk, 0, 0)]
    mla_v = (kv_p, HEADS, 1)
    mla_scale = (NOPE + ROPE) ** -0.5
    o_mla, lse_mla = _attn_fwd("mla_fwd", mla_q, mla_k, mla_v, None, CHUNK, mla_scale, n_seq, seq, t_attn)

    c_run = _seq_cumsum("forget_cumsum", proj, c_kr // LANES, n_seq, seq, False,
                        pre=lambda z, b: _log_sigmoid(z + b), vec=bf_vec)
    c8 = c_run[:, ROPE:ROPE + HEADS]
    c_col = jnp.transpose(c8).reshape(HEADS, n_tok, 1)
    c_rowf = jnp.transpose(c8.reshape(n_seq, seq, HEADS), (0, 2, 1)).reshape(n_seq * HEADS, 1, seq)
    fox_q, fox_k, fox_v = [(proj, cb_fq, 1)], [(proj, cb_fk, 1)], (proj, cb_fv, 1)
    fox_scale = FDIM ** -0.5
    o_fox, lse_fox = _attn_fwd("fox_fwd", fox_q, fox_k, fox_v, (c_col, c_rowf), 1, fox_scale, n_seq, seq, t_attn)

    pm = _mm("branch_mla", "nn", o_mla, g_bm, n_tok, d, hd, lb=l_bm, tn=n_bm)
    pf = _mm("branch_fox", "nn", o_fox, g_bf, n_tok, d, hd, lb=l_bm, tn=n_bm)
    bg = row(b_gate)

    def merge(r, v):
        return [_sigmoid(r[0] + v[0]) * r[2] + _sigmoid(r[1] + v[1]) * r[3]], []

    (merged,) = _rows("gate_merge", merge, [(proj, d, cb_gm), (proj, d, cb_gf), (pm, d, 0), (pf, d, 0)],
                      [bg[:, :d], bg[:, d:]], [(d, BF16)], [], n_tok)
    y1 = _mm("mix_out", "nn", merged, w_out_full, n_tok, d, d)

    def resid_norm(r, v):
        x1v = r[0] + r[1] * _rstd(r[1]) * v[0]
        return [x1v, x1v * _rstd(x1v) * v[1]], []

    x1, h2 = _rows("post_mix_pre_ffn", resid_norm, [(x2, d, 0), (y1, d, 0)], [row(post_mix_norm), row(pre_ffn_norm)],
                   [(d, F32), (d, BF16)], [], n_tok)

    u_il = _mm("ffn_up", "nn", h2, g_up, n_tok, two_f, d, lb=l_up, lo=l_il, tn=wt)
    cw_il = to_il(_logical(g_cw)[:3])
    cb_il = to_il(row(conv_b))
    act = _conv_glu_fwd(u_il, cw_il, cb_il, n_seq, seq, wt)
    ff = _mm("ffn_down", "nn", act, w_down_full, n_tok, d, ff_dim)

    def final(r, v):
        x1v, ffv, tg = r
        diff = x1v + ffv * _rstd(ffv) * v[0] - tg
        dx2v = diff / d
        dffv, dg4 = _rms_bwd(ffv, v[0], dx2v)
        sq = jnp.sum(jnp.sum(diff * diff, axis=1, keepdims=True), axis=0, keepdims=True)
        return [dx2v, dffv], [dg4, jnp.broadcast_to(sq, (1, LANES))]

    dx2, dff, dg_post_ffn, sq_sum = _rows("loss_post_ffn_bwd", final, [(x1, d, 0), (ff, d, 0), (tgt, d, 0)],
                                          [row(post_ffn_norm)], [(d, F32), (d, BF16)], [(1, d), (1, LANES)], n_tok)
    dact = _mm("ffn_down_dx", "nt", dff, w_down_full, n_tok, ff_dim, d)
    gw_down = _mm("ffn_down_dw", "tn", act, dff, ff_dim, d, n_tok, out_dtype=BF16)
    d_il, conv_acc = _conv_glu_bwd_pre(u_il, dact, cw_il, cb_il, n_seq, seq, wt)
    du_il = _conv_bwd_input(d_il, cw_il, n_seq, seq, wt)
    dh2 = _mm("ffn_up_dx", "nt", du_il, g_up, n_tok, d, two_f, la=l_il, lb=l_up, tk=wt)
    gw_up = _mm("ffn_up_dw", "tn", h2, du_il, d, two_f, n_tok, lb=l_il, lo=l_up, tn=wt, out_dtype=BF16)

    def mid_bwd(r, v):
        x1v, y1v, dx2v, dh2v = r
        d3, dg3 = _rms_bwd(x1v, v[1], dh2v)
        dx1v = dx2v + d3
        dy1v, dg2 = _rms_bwd(y1v, v[0], dx1v)
        return [dx1v, dy1v], [dg3, dg2]

    dx1, dy1, dg_pre_ffn, dg_post_mix = _rows(
        "pre_ffn_post_mix_bwd", mid_bwd, [(x1, d, 0), (y1, d, 0), (dx2, d, 0), (dh2, d, 0)],
        [row(post_mix_norm), row(pre_ffn_norm)], [(d, F32), (d, BF16)], [(1, d), (1, d)], n_tok)
    dmerged = _mm("mix_out_dx", "nt", dy1, w_out_full, n_tok, d, d)
    gw_out = _mm("mix_out_dw", "tn", merged, dy1, d, d, n_tok, out_dtype=BF16)

    def gate_bwd(r, v):
        zm, zf, pmv, pfv, dm = r
        gm, gf = _sigmoid(zm + v[0]), _sigmoid(zf + v[1])
        dzm, dzf = dm * pmv * gm * (1.0 - gm), dm * pfv * gf * (1.0 - gf)
        return [dm * gm, dm * gf, dzm, dzf], [_colsum(dzm), _colsum(dzf)]

    dpm, dpf, dzm, dzf, dbg_m, dbg_f = _rows(
        "gate_merge_bwd", gate_bwd, [(proj, d, cb_gm), (proj, d, cb_gf), (pm, d, 0), (pf, d, 0), (dmerged, d, 0)],
        [bg[:, :d], bg[:, d:]], [(d, BF16)] * 4, [(1, d), (1, d)], n_tok)
    tk_b = min(n_bm, 512)
    do_mla = _mm("branch_mla_dx", "nt", dpm, g_bm, n_tok, hd, d, lb=l_bm, tk=tk_b)
    do_fox = _mm("branch_fox_dx", "nt", dpf, g_bf, n_tok, hd, d, lb=l_bm, tk=tk_b)
    gw_bm = _mm("branch_mla_dw", "tn", o_mla, dpm, hd, d, n_tok, lo=l_bm, tn=n_bm, out_dtype=BF16)
    gw_bf = _mm("branch_fox_dw", "tn", o_fox, dpf, hd, d, n_tok, lo=l_bm, tn=n_bm, out_dtype=BF16)

    dq_nope, drq = _attn_bwd_dq("mla_bwd_dq", mla_q, mla_k, mla_v, o_mla, do_mla, lse_mla, None, CHUNK, mla_scale,
                                n_seq, seq, t_attn)
    dk_nope, drk_h, dv_mla = _attn_bwd_dkv("mla_bwd_dkv", mla_q, mla_k, mla_v, o_mla, do_mla, lse_mla, None, CHUNK,
                                           mla_scale, n_seq, seq, t_attn)
    fox_bias = (c_col, c_rowf)
    dfq, dc_q = _attn_bwd_dq("fox_bwd_dq", fox_q, fox_k, fox_v, o_fox, do_fox, lse_fox, fox_bias, 1, fox_scale,
                             n_seq, seq, t_attn)
    dfk, dfv, dc_k = _attn_bwd_dkv("fox_bwd_dkv", fox_q, fox_k, fox_v, o_fox, do_fox, lse_fox, fox_bias, 1, fox_scale,
                                   n_seq, seq, t_attn)
    dc8 = jnp.transpose(dc_q.reshape(HEADS, n_tok)) + jnp.transpose(dc_k.reshape(n_seq, HEADS, seq), (0, 2, 1)).reshape(n_tok, HEADS)
    dc128 = jnp.pad(dc8, ((0, 0), (ROPE, LANES - ROPE - HEADS)))
    dlogf = _seq_cumsum("forget_cumsum_bwd", dc128, 0, n_seq, seq, True)

    def mla_pack(r, v):
        dqn_v, drq_v, dkn_v, dv_v, drk_v, c1, s1 = r
        c8, s8 = jnp.tile(c1, (1, HEADS)), jnp.tile(s1, (1, HEADS))
        drk = drk_v[:, :LANES]
        for hh in range(1, HEADS):
            drk = drk + drk_v[:, hh * LANES:(hh + 1) * LANES]
        return [jnp.concatenate([dqn_v, drq_v * c8, drq_v * s8], axis=1), jnp.concatenate([dkn_v, dv_v], axis=1),
                drk * c1, drk * s1], []

    dq_p, dkv_p, dkx, dkr = _rows(
        "mla_rope_bwd", mla_pack,
        [(dq_nope, hd, 0), (drq, hd, 0), (dk_nope, hd, 0), (dv_mla, hd, 0), (drk_h, hd, 0), (cs, LANES, 0), (sn, LANES, 0)],
        [], [(3 * hd, BF16), (2 * hd, BF16), (LANES, F32), (LANES, F32)], [], n_tok)
    dqn = _mm("q_up_dx", "nt", dq_p, wuq_p, n_tok, QL, 3 * hd)
    gw_uq_p = _mm("q_up_dw", "tn", qn, dq_p, QL, 3 * hd, n_tok, out_dtype=BF16)
    dkvn = _mm("kv_up_dx", "nt", dkv_p, wukv_p, n_tok, KVL, 2 * hd)
    gw_ukv_p = _mm("kv_up_dw", "tn", kvn, dkv_p, KVL, 2 * hd, n_tok, out_dtype=BF16)

    def lat_bwd(r, v):
        ql, kvl, dqn_v, dkvn_v, dkx_v, dkr_v, zblk, dlf = r
        dql, dgq = _rms_bwd(ql, v[0], dqn_v)
        dkvl, dgkv = _rms_bwd(kvl, v[1], dkvn_v)
        dfl = dlf * _sigmoid(-(zblk + v[2]))
        return [jnp.concatenate([dql, dkvl, dkx_v, dkr_v + dfl], axis=1)], [dgq, dgkv, _colsum(dfl)]

    dlat, dg_q, dg_kv, dbf = _rows(
        "latent_bwd", lat_bwd,
        [(proj, QL, c_lat // QL), (proj, KVL, (c_lat + QL) // KVL), (dqn, QL, 0), (dkvn, KVL, 0), (dkx, LANES, 0),
         (dkr, LANES, 0), (proj, LANES, c_kr // LANES), (dlogf, LANES, 0)],
        [row(q_a_norm), row(kv_a_norm), bf_vec], [(QL + KVL + 2 * LANES, BF16)], [(1, QL), (1, KVL), (1, LANES)], n_tok)
    dproj = jnp.concatenate([dzm, dzf, dfq.astype(BF16), dfk.astype(BF16), dfv.astype(BF16), dlat], axis=1)
    dh = _mm("proj_in_dx", "nt", dproj, win_p, n_tok, d, n_p)
    gw_in_p = _mm("proj_in_dw", "tn", h, dproj, d, n_p, n_tok, out_dtype=BF16)

    def first_bwd(r, v):
        dxa, dg1 = _rms_bwd(r[0], v[0], r[1])
        return [r[2] + dxa], [dg1]

    grad_x, dg_pre_mix = _rows("pre_mix_bwd", first_bwd, [(x2, d, 0), (dh, d, 0), (dx1, d, 0)], [row(pre_mix_norm)],
                               [(d, F32)], [(1, d)], n_tok)

    f32 = lambda a: a.astype(F32)
    kr_blk = gw_in_p[:, c_kr:c_kr + LANES]
    d_kpe = (f32(gw_in_p[:, c_kx:c_kx + ROPE]) + _unrot_cols(f32(kr_blk[:, :ROPE]))).astype(BF16)
    gw_in_l = jnp.concatenate([gw_in_p[:, c_lat:c_lat + QL + KVL], d_kpe, gw_in_p[:, 2 * d:c_lat],
                               kr_blk[:, ROPE:ROPE + HEADS], gw_in_p[:, :2 * d]], axis=1)
    gc_in = jnp.pad(_chunks(gw_in_l, n_in_shard), ((0, 0), (0, 0), (0, in_pad - n_in_shard)))
    uq_parts = [gw_uq_p[:, i * hd:(i + 1) * hd].reshape(QL, HEADS, LANES) for i in range(3)]
    d_pe = (f32(uq_parts[1][:, :, :ROPE]) + _unrot_cols(f32(uq_parts[2][:, :, :ROPE]))).astype(BF16)
    gc_uq = _chunks(jnp.concatenate([uq_parts[0], d_pe], axis=2).reshape(QL, HEADS * (NOPE + ROPE)), w_uq.shape[1])
    gc_ukv = _chunks(jnp.concatenate([gw_ukv_p[:, :hd].reshape(KVL, HEADS, NOPE), gw_ukv_p[:, hd:].reshape(KVL, HEADS, VDIM)],
                                     axis=2).reshape(KVL, HEADS * (NOPE + VDIM)), w_ukv.shape[1])
    grads = [gc_in, gc_uq, gc_ukv, gw_bm, gw_bf, gw_out.reshape(N_CHIPS, d // N_CHIPS, d), gw_up,
             gw_down.reshape(N_CHIPS, ff_dim // N_CHIPS, d)]

    big = ["w_in", "w_uq", "w_ukv", "w_branch_mla", "w_branch_fox", "w_out", "w_up", "w_down"]
    mine, theirs = _pair_split(grads)
    parts = []
    for nm, a, b in zip(big, mine, theirs):
        r2, c2 = a.shape[0] * a.shape[1], a.shape[2]
        (p,) = _rows("rs_pair_add_" + nm, lambda r, v: ([r[0].astype(F32) + r[1].astype(F32)], []),
                     [(a.reshape(r2, c2), c2, 0), (b.reshape(r2, c2), c2, 0)], [], [(c2, BF16)], [], r2)
        parts.append(p.reshape(a.shape))
    landed = _chip_scatter(parts)
    halves = [_sum_slots("rs_chip_sum_" + nm, s, (3, 0, 1, 2)) for nm, s in zip(big, landed)]
    full = _pair_join(halves)
    g_big = dict(zip(big, full))
    g_big["w_in"] = g_big["w_in"][:, :n_in_shard]

    conv_acc_l = from_il(conv_acc)
    pieces = [dg_pre_mix, dg_q, dg_kv, dbf, dbg_m, dbg_f, dg_post_mix, dg_pre_ffn, conv_acc_l[3:4], dg_post_ffn,
              conv_acc_l[0:1], conv_acc_l[1:2], conv_acc_l[2:3], sq_sum]
    sizes = [p.shape[1] for p in pieces]
    flat = jnp.concatenate(pieces, axis=1)
    n_rows = -(-flat.shape[1] // (8 * LANES)) * 8
    flat = _pad_cols(flat, n_rows * LANES).reshape(n_rows, LANES)
    total = _sum_slots("small_sum", _gather_small(flat), tuple(range(8))).reshape(1, n_rows * LANES)
    offs = [sum(sizes[:i]) for i in range(len(sizes))]
    tot = [total[0, o:o + s] for o, s in zip(offs, sizes)]
    loss = 0.5 * tot[13][0] / d
    g_small = {"pre_mix_norm": tot[0], "q_a_norm": tot[1], "kv_a_norm": tot[2], "b_forget": tot[3][ROPE:ROPE + HEADS],
               "b_gate": jnp.concatenate([tot[4], tot[5]]), "post_mix_norm": tot[6], "pre_ffn_norm": tot[7],
               "conv_b": tot[8], "post_ffn_norm": tot[9]}
    gcw_full = jnp.stack([tot[10], tot[11], tot[12]])
    g_conv_w = lax.dynamic_slice(gcw_full, (0, chip * n_up), (3, n_up))

    given = dict(pre_mix_norm=(pre_mix_norm, m_pre_mix_norm, v_pre_mix_norm), w_in=(w_in, m_w_in, v_w_in),
                 q_a_norm=(q_a_norm, m_q_a_norm, v_q_a_norm), w_uq=(w_uq, m_w_uq, v_w_uq),
                 kv_a_norm=(kv_a_norm, m_kv_a_norm, v_kv_a_norm), w_ukv=(w_ukv, m_w_ukv, v_w_ukv),
                 b_forget=(b_forget, m_b_forget, v_b_forget), b_gate=(b_gate, m_b_gate, v_b_gate),
                 w_branch_mla=(w_branch_mla, m_w_branch_mla, v_w_branch_mla),
                 w_branch_fox=(w_branch_fox, m_w_branch_fox, v_w_branch_fox), w_out=(w_out, m_w_out, v_w_out),
                 post_mix_norm=(post_mix_norm, m_post_mix_norm, v_post_mix_norm),
                 pre_ffn_norm=(pre_ffn_norm, m_pre_ffn_norm, v_pre_ffn_norm), w_up=(w_up, m_w_up, v_w_up),
                 conv_w=(conv_w, m_conv_w, v_conv_w), conv_b=(conv_b, m_conv_b, v_conv_b),
                 w_down=(w_down, m_w_down, v_w_down), post_ffn_norm=(post_ffn_norm, m_post_ffn_norm, v_post_ffn_norm))
    order = list(given)
    grad, delta, new_m, new_v = {}, {}, {}, {}
    for nm in big + ["conv_w"]:
        grad[nm] = g_big[nm] if nm in g_big else g_conv_w
        delta[nm], new_m[nm], new_v[nm] = _adamw("adamw_" + nm, given[nm][0], grad[nm], given[nm][1], given[nm][2])
    small = list(g_small)
    padded = [-(-g_small[nm].shape[0] // LANES) * LANES for nm in small]
    s_rows = -(-sum(padded) // (8 * LANES)) * 8

    def pack(vals):
        cat = jnp.concatenate([jnp.pad(a, (0, p - a.shape[0])) for a, p in zip(vals, padded)])
        return jnp.pad(cat, (0, s_rows * LANES - cat.shape[0])).reshape(s_rows, LANES)

    packed = _adamw("adamw_small", pack([given[nm][0] for nm in small]), pack([g_small[nm] for nm in small]),
                    pack([given[nm][1] for nm in small]), pack([given[nm][2] for nm in small]))
    s_offs = [sum(padded[:i]) for i in range(len(small))]
    for nm, o in zip(small, s_offs):
        n_el = g_small[nm].shape[0]
        grad[nm] = g_small[nm]
        delta[nm], new_m[nm], new_v[nm] = [p.reshape(-1)[o:o + n_el] for p in packed]
    return (loss, grad_x.reshape(n_seq, seq, d), *[grad[nm] for nm in order], *[delta[nm] for nm in order],
            *[new_m[nm] for nm in order], *[new_v[nm] for nm in order])
```

```python
import functools
import math

import jax
import jax.numpy as jnp
from jax import lax
from jax.experimental import pallas as pl
from jax.experimental.pallas import tpu as pltpu

F32, BF16 = jnp.float32, jnp.bfloat16
MESH = pl.DeviceIdType.MESH

HEADS = 8
NOPE, ROPE, VDIM = 128, 64, 128
QL, KVL = 512, 256
FDIM = 128
CHUNK = 64
ROPE_THETA = 10000.0
EPS = 1e-6
NEG_INF = -1e30
ADAM_LR, ADAM_B1, ADAM_B2, ADAM_EPS, ADAM_WD, ADAM_STEP = 0.001, 0.9, 0.999, 1e-08, 0.01, 10

VMEM_LIMIT_BYTES = 52 * 1024 * 1024
LANES = 128
N_CHIPS = 4


def _params(sem):
    return pltpu.CompilerParams(dimension_semantics=sem, vmem_limit_bytes=VMEM_LIMIT_BYTES)


def _tile(n, target, mult):
    if n <= target:
        return n
    t = (target // mult) * mult
    while t >= mult:
        if n % t == 0:
            return t
        t -= mult
    raise ValueError(f"no tile for {n} (target {target}, multiple of {mult})")


class _Plain:
    def __init__(self, perm=None):
        self.perm = perm

    def spec(self, tr, tc, rc):
        perm = self.perm

        def imap(i, j, k):
            r, c = rc(i, j, k)
            return (r, perm(c) if perm is not None else c)

        return pl.BlockSpec((tr, tc), imap)

    def shape(self, rows, cols):
        return (rows, cols)


class _Chunked:
    def __init__(self, n):
        self.n = n

    def spec(self, tr, tc, rc):
        assert self.n % tc == 0, (self.n, tc)
        per = self.n // tc

        def imap(i, j, k):
            r, c = rc(i, j, k)
            return (c // per, r, c % per)

        return pl.BlockSpec((None, tr, tc), imap)

    def shape(self, rows, cols):
        assert cols == N_CHIPS * self.n
        return (N_CHIPS, rows, self.n)


_DIMS = {"nn": (((1,), (0,)), ((), ())), "nt": (((1,), (1,)), ((), ())), "tn": (((0,), (0,)), ((), ()))}


def _mm(name, mode, a, b, m, n, k, *, tm=1024, tn=1024, tk=512, la=None, lb=None, lo=None, out_dtype=F32):
    la, lb, lo = la or _Plain(), lb or _Plain(), lo or _Plain()
    tm, tn, tk = _tile(m, tm, 128), _tile(n, tn, 128), _tile(k, tk, 128)
    nk = k // tk
    if mode == "nn":
        a_spec = la.spec(tm, tk, lambda i, j, kk: (i, kk))
        b_spec = lb.spec(tk, tn, lambda i, j, kk: (kk, j))
    elif mode == "nt":
        a_spec = la.spec(tm, tk, lambda i, j, kk: (i, kk))
        b_spec = lb.spec(tn, tk, lambda i, j, kk: (j, kk))
    else:
        a_spec = la.spec(tk, tm, lambda i, j, kk: (kk, i))
        b_spec = lb.spec(tk, tn, lambda i, j, kk: (kk, j))
    o_spec = lo.spec(tm, tn, lambda i, j, kk: (i, j))
    dims = _DIMS[mode]

    def body(a_ref, b_ref, o_ref, acc_ref):
        kk = pl.program_id(2)

        @pl.when(kk == 0)
        def _():
            acc_ref[...] = jnp.zeros_like(acc_ref)

        acc_ref[...] += lax.dot_general(a_ref[...].astype(BF16), b_ref[...].astype(BF16), dims,
                                        preferred_element_type=F32)

        @pl.when(kk == nk - 1)
        def _():
            o_ref[...] = acc_ref[...].astype(o_ref.dtype)

    return pl.pallas_call(
        body, name=name, grid=(m // tm, n // tn, nk),
        in_specs=[a_spec, b_spec], out_specs=o_spec,
        out_shape=jax.ShapeDtypeStruct(lo.shape(m, n), out_dtype),
        scratch_shapes=[pltpu.VMEM((tm, tn), F32)],
        compiler_params=_params(("parallel", "parallel", "arbitrary")),
    )(a, b)


def _rows(name, fn, rows_in, vecs_in, rows_out, accs_out, n_rows, tr=256):
    tr = _tile(n_rows, tr, 16)
    nr, nv, no = len(rows_in), len(vecs_in), len(rows_out)

    def body(*refs):
        ins, vecs = refs[:nr], refs[nr:nr + nv]
        outs, accs = refs[nr + nv:nr + nv + no], refs[nr + nv + no:]
        ro, ac = fn([r[...] for r in ins], [v[...] for v in vecs])
        for o_ref, val in zip(outs, ro):
            o_ref[...] = val.astype(o_ref.dtype)
        if accs:
            @pl.when(pl.program_id(0) == 0)
            def _():
                for a_ref in accs:
                    a_ref[...] = jnp.zeros_like(a_ref)

            for a_ref, val in zip(accs, ac):
                a_ref[...] += val

    in_specs = [pl.BlockSpec((tr, cols), functools.partial(lambda i, cb: (i, cb), cb=cb)) for _, cols, cb in rows_in]
    in_specs += [pl.BlockSpec(v.shape, lambda i: (0, 0)) for v in vecs_in]
    out_specs = [pl.BlockSpec((tr, cols), lambda i: (i, 0)) for cols, _ in rows_out]
    out_specs += [pl.BlockSpec((r, cols), lambda i: (0, 0)) for r, cols in accs_out]
    out_shape = [jax.ShapeDtypeStruct((n_rows, cols), dt) for cols, dt in rows_out]
    out_shape += [jax.ShapeDtypeStruct((r, cols), F32) for r, cols in accs_out]
    res = pl.pallas_call(
        body, name=name, grid=(n_rows // tr,), in_specs=in_specs, out_specs=out_specs, out_shape=out_shape,
        compiler_params=_params(("arbitrary",)),
    )(*[a for a, _, _ in rows_in], *vecs_in)
    return res


def _colsum(v):
    return jnp.sum(v, axis=0, keepdims=True)


def _rstd(x):
    return lax.rsqrt(jnp.mean(x * x, axis=-1, keepdims=True) + EPS)


def _rms_bwd(x, g, dy):
    r = _rstd(x)
    xh = x * r
    dxh = dy * g
    dx = r * (dxh - xh * jnp.mean(dxh * xh, axis=-1, keepdims=True))
    return dx, _colsum(dy * xh)


def _sigmoid(z):
    return 1.0 / (1.0 + jnp.exp(-z))


_GELU_K = math.sqrt(2.0 / math.pi)


def _gelu_parts(g):
    t = jnp.tanh(_GELU_K * (g + 0.044715 * g * g * g))
    gel = 0.5 * g * (1.0 + t)
    dgel = 0.5 * (1.0 + t) + 0.5 * g * (1.0 - t * t) * (_GELU_K * (1.0 + 3.0 * 0.044715 * g * g))
    return gel, dgel


def _visible(qi, ki, t, unit):
    rows = qi * t + lax.broadcasted_iota(jnp.int32, (t, t), 0)
    cols = ki * t + lax.broadcasted_iota(jnp.int32, (t, t), 1)
    if unit > 1:
        sh = int(math.log2(unit))
        assert 1 << sh == unit
        rows, cols = jnp.right_shift(rows, sh), jnp.right_shift(cols, sh)
    return cols <= rows


def _cat(refs):
    vals = [r[...].astype(BF16) for r in refs]
    return vals[0] if len(vals) == 1 else jnp.concatenate(vals, axis=1)


def _piece_spec(t, piece, row_of):
    _, base, stride = piece
    return pl.BlockSpec((t, LANES), lambda b, h, i, j: (row_of(b, i, j), base + stride * h))


def _scores(q_refs, k_refs, bias_refs, qi, ki, t, unit, scale):
    s = lax.dot_general(_cat(q_refs), _cat(k_refs), _DIMS["nt"], preferred_element_type=F32) * scale
    if bias_refs:
        s = s + bias_refs[0][...] - bias_refs[1][...]
    return jnp.where(_visible(qi, ki, t, unit), s, NEG_INF)


def _attn_fwd(name, qp, kp, vp, bias, unit, scale, n_seq, seq, t):
    nb = seq // t
    n_tok = n_seq * seq
    nq, nk_p = len(qp), len(kp)
    nbias = 2 if bias is not None else 0

    def body(*refs):
        q_refs, k_refs = refs[:nq], refs[nq:nq + nk_p]
        v_ref = refs[nq + nk_p]
        bias_refs = refs[nq + nk_p + 1:nq + nk_p + 1 + nbias]
        o_ref, lse_ref, m_s, l_s, acc_s = refs[nq + nk_p + 1 + nbias:]
        qi, ki = pl.program_id(2), pl.program_id(3)

        @pl.when(ki == 0)
        def _():
            m_s[...] = jnp.full_like(m_s, NEG_INF)
            l_s[...] = jnp.zeros_like(l_s)
            acc_s[...] = jnp.zeros_like(acc_s)

        @pl.when(ki <= qi)
        def _():
            s = _scores(q_refs, k_refs, bias_refs, qi, ki, t, unit, scale)
            m_new = jnp.maximum(m_s[...], jnp.max(s, axis=1, keepdims=True))
            alpha = jnp.exp(m_s[...] - m_new)
            p = jnp.exp(s - m_new)
            l_s[...] = alpha * l_s[...] + jnp.sum(p, axis=1, keepdims=True)
            acc_s[...] = alpha * acc_s[...] + jnp.dot(p.astype(BF16), v_ref[...].astype(BF16),
                                                      preferred_element_type=F32)
            m_s[...] = m_new

        @pl.when(ki == qi)
        def _():
            o_ref[...] = acc_s[...] / l_s[...]
            lse_ref[...] = m_s[...] + jnp.log(l_s[...])

    q_row = lambda b, i, j: b * nb + i
    k_row = lambda b, i, j: b * nb + jnp.minimum(j, i)
    in_specs = [_piece_spec(t, p, q_row) for p in qp] + [_piece_spec(t, p, k_row) for p in kp]
    in_specs.append(_piece_spec(t, vp, k_row))
    args = [p[0] for p in qp] + [p[0] for p in kp] + [vp[0]]
    if bias is not None:
        in_specs.append(pl.BlockSpec((None, t, 1), lambda b, h, i, j: (h, b * nb + i, 0)))
        in_specs.append(pl.BlockSpec((None, 1, t), lambda b, h, i, j: (b * HEADS + h, 0, jnp.minimum(j, i))))
        args += list(bias)
    return pl.pallas_call(
        body, name=name, grid=(n_seq, HEADS, nb, nb), in_specs=in_specs,
        out_specs=[pl.BlockSpec((t, LANES), lambda b, h, i, j: (b * nb + i, h)),
                   pl.BlockSpec((None, t, 1), lambda b, h, i, j: (h, b * nb + i, 0))],
        out_shape=[jax.ShapeDtypeStruct((n_tok, HEADS * LANES), F32),
                   jax.ShapeDtypeStruct((HEADS, n_tok, 1), F32)],
        scratch_shapes=[pltpu.VMEM((t, 1), F32), pltpu.VMEM((t, 1), F32), pltpu.VMEM((t, LANES), F32)],
        compiler_params=_params(("parallel", "parallel", "arbitrary", "arbitrary")),
    )(*args)


def _attn_bwd_dq(name, qp, kp, vp, o, do, lse, bias, unit, scale, n_seq, seq, t):
    nb = seq // t
    n_tok = n_seq * seq
    nq, nk_p = len(qp), len(kp)
    nbias = 2 if bias is not None else 0
    n_in = nq + nk_p + 4 + nbias
    n_out = nq + (1 if bias is not None else 0)

    def body(*refs):
        q_refs, k_refs = refs[:nq], refs[nq:nq + nk_p]
        v_ref, o_ref, do_ref, lse_ref = refs[nq + nk_p:nq + nk_p + 4]
        bias_refs = refs[nq + nk_p + 4:n_in]
        outs = refs[n_in:n_in + n_out]
        dq_s, delta_s, dc_s = refs[n_in + n_out:]
        qi, ki = pl.program_id(2), pl.program_id(3)

        @pl.when(ki == 0)
        def _():
            dq_s[...] = jnp.zeros_like(dq_s)
            dc_s[...] = jnp.zeros_like(dc_s)
            delta_s[...] = jnp.sum(do_ref[...] * o_ref[...], axis=1, keepdims=True)

        @pl.when(ki <= qi)
        def _():
            s = _scores(q_refs, k_refs, bias_refs, qi, ki, t, unit, scale)
            p = jnp.exp(s - lse_ref[...])
            dp = lax.dot_general(do_ref[...].astype(BF16), v_ref[...].astype(BF16), _DIMS["nt"],
                                 preferred_element_type=F32)
            ds = p * (dp - delta_s[...])
            dq_s[...] += jnp.dot(ds.astype(BF16), _cat(k_refs), preferred_element_type=F32)
            dc_s[...] += jnp.sum(ds, axis=1, keepdims=True)

        @pl.when(ki == qi)
        def _():
            for n_p in range(nq):
                outs[n_p][...] = dq_s[:, n_p * LANES:(n_p + 1) * LANES] * scale
            if bias is not None:
                outs[nq][...] = dc_s[...]

    q_row = lambda b, i, j: b * nb + i
    k_row = lambda b, i, j: b * nb + jnp.minimum(j, i)
    head_q = pl.BlockSpec((t, LANES), lambda b, h, i, j: (b * nb + i, h))
    col_q = pl.BlockSpec((None, t, 1), lambda b, h, i, j: (h, b * nb + i, 0))
    in_specs = [_piece_spec(t, p, q_row) for p in qp] + [_piece_spec(t, p, k_row) for p in kp]
    in_specs += [_piece_spec(t, vp, k_row), head_q, head_q, col_q]
    args = [p[0] for p in qp] + [p[0] for p in kp] + [vp[0], o, do, lse]
    if bias is not None:
        in_specs += [col_q, pl.BlockSpec((None, 1, t), lambda b, h, i, j: (b * HEADS + h, 0, jnp.minimum(j, i)))]
        args += list(bias)
    out_specs = [head_q] * nq + ([col_q] if bias is not None else [])
    out_shape = [jax.ShapeDtypeStruct((n_tok, HEADS * LANES), F32)] * nq
    if bias is not None:
        out_shape.append(jax.ShapeDtypeStruct((HEADS, n_tok, 1), F32))
    return pl.pallas_call(
        body, name=name, grid=(n_seq, HEADS, nb, nb), in_specs=in_specs, out_specs=out_specs, out_shape=out_shape,
        scratch_shapes=[pltpu.VMEM((t, nq * LANES), F32), pltpu.VMEM((t, 1), F32), pltpu.VMEM((t, 1), F32)],
        compiler_params=_params(("parallel", "parallel", "arbitrary", "arbitrary")),
    )(*args)


def _attn_bwd_dkv(name, qp, kp, vp, o, do, lse, bias, unit, scale, n_seq, seq, t):
    nb = seq // t
    n_tok = n_seq * seq
    nq, nk_p = len(qp), len(kp)
    nbias = 2 if bias is not None else 0
    n_in = nq + nk_p + 4 + nbias
    n_out = nk_p + 1 + (1 if bias is not None else 0)

    def body(*refs):
        q_refs, k_refs = refs[:nq], refs[nq:nq + nk_p]
        v_ref, o_ref, do_ref, lse_ref = refs[nq + nk_p:nq + nk_p + 4]
        bias_refs = refs[nq + nk_p + 4:n_in]
        outs = refs[n_in:n_in + n_out]
        dk_s, dv_s, dc_s = refs[n_in + n_out:]
        ki, qi = pl.program_id(2), pl.program_id(3)

        @pl.when(qi == 0)
        def _():
            dk_s[...] = jnp.zeros_like(dk_s)
            dv_s[...] = jnp.zeros_like(dv_s)
            dc_s[...] = jnp.zeros_like(dc_s)

        @pl.when(qi >= ki)
        def _():
            s = _scores(q_refs, k_refs, bias_refs, qi, ki, t, unit, scale)
            p = jnp.exp(s - lse_ref[...])
            do_b = do_ref[...].astype(BF16)
            delta = jnp.sum(do_ref[...] * o_ref[...], axis=1, keepdims=True)
            dp = lax.dot_general(do_b, v_ref[...].astype(BF16), _DIMS["nt"], preferred_element_type=F32)
            ds = p * (dp - delta)
            dv_s[...] += lax.dot_general(p.astype(BF16), do_b, _DIMS["tn"], preferred_element_type=F32)
            dk_s[...] += lax.dot_general(ds.astype(BF16), _cat(q_refs), _DIMS["tn"], preferred_element_type=F32)
            dc_s[...] -= jnp.sum(ds, axis=0, keepdims=True)

        @pl.when(qi == nb - 1)
        def _():
            for n_p in range(nk_p):
                outs[n_p][...] = dk_s[:, n_p * LANES:(n_p + 1) * LANES] * scale
            outs[nk_p][...] = dv_s[...]
            if bias is not None:
                outs[nk_p + 1][...] = dc_s[...]

    q_row = lambda b, i, j: b * nb + jnp.maximum(j, i)
    k_row = lambda b, i, j: b * nb + i
    head_q = pl.BlockSpec((t, LANES), lambda b, h, i, j: (b * nb + jnp.maximum(j, i), h))
    col_q = pl.BlockSpec((None, t, 1), lambda b, h, i, j: (h, b * nb + jnp.maximum(j, i), 0))
    head_k = pl.BlockSpec((t, LANES), lambda b, h, i, j: (b * nb + i, h))
    row_k = pl.BlockSpec((None, 1, t), lambda b, h, i, j: (b * HEADS + h, 0, i))
    in_specs = [_piece_spec(t, p, q_row) for p in qp] + [_piece_spec(t, p, k_row) for p in kp]
    in_specs += [_piece_spec(t, vp, k_row), head_q, head_q, col_q]
    args = [p[0] for p in qp] + [p[0] for p in kp] + [vp[0], o, do, lse]
    if bias is not None:
        in_specs += [col_q, row_k]
        args += list(bias)
    out_specs = [head_k] * (nk_p + 1) + ([row_k] if bias is not None else [])
    out_shape = [jax.ShapeDtypeStruct((n_tok, HEADS * LANES), F32)] * (nk_p + 1)
    if bias is not None:
        out_shape.append(jax.ShapeDtypeStruct((n_seq * HEADS, 1, seq), F32))
    return pl.pallas_call(
        body, name=name, grid=(n_seq, HEADS, nb, nb), in_specs=in_specs, out_specs=out_specs, out_shape=out_shape,
        scratch_shapes=[pltpu.VMEM((t, nk_p * LANES), F32), pltpu.VMEM((t, LANES), F32), pltpu.VMEM((1, t), F32)],
        compiler_params=_params(("parallel", "parallel", "arbitrary", "arbitrary")),
    )(*args)


def _seq_cumsum(name, x, col_block, n_seq, seq, reverse, pre=None, vec=None):
    t = _tile(seq, 256, 128)
    nb = seq // t

    def body(*refs):
        x_ref = refs[0]
        vec_ref = refs[1] if vec is not None else None
        o_ref, carry = refs[-2], refs[-1]

        @pl.when(pl.program_id(1) == 0)
        def _():
            carry[...] = jnp.zeros_like(carry)

        v = x_ref[...]
        if pre is not None:
            v = pre(v, vec_ref[...])
        r = lax.broadcasted_iota(jnp.int32, (t, t), 0)
        c = lax.broadcasted_iota(jnp.int32, (t, t), 1)
        tri = jnp.where((c >= r) if reverse else (c <= r), 1.0, 0.0).astype(BF16)
        hi = v.astype(BF16)
        mid = (v - hi.astype(F32)).astype(BF16)
        lo = (v - hi.astype(F32) - mid.astype(F32)).astype(BF16)
        acc = jnp.dot(tri, hi, preferred_element_type=F32)
        acc += jnp.dot(tri, mid, preferred_element_type=F32)
        acc += jnp.dot(tri, lo, preferred_element_type=F32)
        o_ref[...] = acc + carry[...]
        carry[...] += _colsum(v)

    blk = (lambda b, i: (b * nb + nb - 1 - i)) if reverse else (lambda b, i: (b * nb + i))
    in_specs = [pl.BlockSpec((t, LANES), lambda b, i: (blk(b, i), col_block))]
    args = [x]
    if vec is not None:
        in_specs.append(pl.BlockSpec(vec.shape, lambda b, i: (0, 0)))
        args.append(vec)
    return pl.pallas_call(
        body, name=name, grid=(n_seq, nb), in_specs=in_specs,
        out_specs=pl.BlockSpec((t, LANES), lambda b, i: (blk(b, i), 0)),
        out_shape=jax.ShapeDtypeStruct((n_seq * seq, LANES), F32),
        scratch_shapes=[pltpu.VMEM((1, LANES), F32)],
        compiler_params=_params(("arbitrary", "arbitrary")),
    )(*args)


def _log_sigmoid(z):
    return -(jnp.maximum(-z, 0.0) + jnp.log(1.0 + jnp.exp(-jnp.abs(z))))


def _shift_down(u, prev_ref, n):
    out = pltpu.roll(u, n, 0)
    row = lax.broadcasted_iota(jnp.int32, u.shape, 0)
    for r in range(n):
        out = jnp.where(row == r, prev_ref[8 - n + r:8 - n + r + 1, :], out)
    return out


def _shift_up(u, next_ref, n):
    ts = u.shape[0]
    out = pltpu.roll(u, ts - n, 0)
    row = lax.broadcasted_iota(jnp.int32, u.shape, 0)
    for r in range(n):
        out = jnp.where(row == ts - n + r, next_ref[r:r + 1, :], out)
    return out


def _conv_taps(u, prev_ref, w_ref, b_ref):
    s1, s2 = _shift_down(u, prev_ref, 1), _shift_down(u, prev_ref, 2)
    return (w_ref[0:1, :] * s2 + w_ref[1:2, :] * s1 + w_ref[2:3, :] * u) + b_ref[...], s1, s2


def _conv_glu_fwd(u_il, cw_il, cb_il, n_seq, seq, wt):
    n_tok, two_f = u_il.shape
    nct = two_f // (2 * wt)
    ts = _tile(seq, 256, 8)
    ns = seq // ts

    def body(u_ref, w_ref, b_ref, a_ref, carry):
        @pl.when(pl.program_id(2) == 0)
        def _():
            carry[...] = jnp.zeros_like(carry)

        u = u_ref[...]
        uc, _, _ = _conv_taps(u, carry, w_ref, b_ref)
        gel, _ = _gelu_parts(uc[:, :wt])
        a_ref[...] = (gel * uc[:, wt:]).astype(a_ref.dtype)
        carry[...] = u[ts - 8:, :]

    return pl.pallas_call(
        body, name="conv_glu_fwd", grid=(nct, n_seq, ns),
        in_specs=[pl.BlockSpec((ts, 2 * wt), lambda j, b, s: (b * ns + s, j)),
                  pl.BlockSpec((3, 2 * wt), lambda j, b, s: (0, j)),
                  pl.BlockSpec((1, 2 * wt), lambda j, b, s: (0, j))],
        out_specs=pl.BlockSpec((ts, wt), lambda j, b, s: (b * ns + s, j)),
        out_shape=jax.ShapeDtypeStruct((n_tok, two_f // 2), BF16),
        scratch_shapes=[pltpu.VMEM((8, 2 * wt), F32)],
        compiler_params=_params(("parallel", "arbitrary", "arbitrary")),
    )(u_il, cw_il, cb_il)


def _conv_glu_bwd_pre(u_il, da, cw_il, cb_il, n_seq, seq, wt):
    n_tok, two_f = u_il.shape
    nct = two_f // (2 * wt)
    ts = _tile(seq, 256, 8)
    ns = seq // ts

    def body(u_ref, da_ref, w_ref, b_ref, d_ref, acc_ref, carry):
        first = jnp.logical_and(pl.program_id(1) == 0, pl.program_id(2) == 0)

        @pl.when(first)
        def _():
            acc_ref[...] = jnp.zeros_like(acc_ref)

        @pl.when(pl.program_id(2) == 0)
        def _():
            carry[...] = jnp.zeros_like(carry)

        u = u_ref[...]
        uc, s1, s2 = _conv_taps(u, carry, w_ref, b_ref)
        gel, dgel = _gelu_parts(uc[:, :wt])
        da_v = da_ref[...]
        d = jnp.concatenate([da_v * uc[:, wt:] * dgel, da_v * gel], axis=1)
        d_ref[...] = d
        acc_ref[0:1, :] += _colsum(d * s2)
        acc_ref[1:2, :] += _colsum(d * s1)
        acc_ref[2:3, :] += _colsum(d * u)
        acc_ref[3:4, :] += _colsum(d)
        carry[...] = u[ts - 8:, :]

    return pl.pallas_call(
        body, name="conv_glu_bwd_pre", grid=(nct, n_seq, ns),
        in_specs=[pl.BlockSpec((ts, 2 * wt), lambda j, b, s: (b * ns + s, j)),
                  pl.BlockSpec((ts, wt), lambda j, b, s: (b * ns + s, j)),
                  pl.BlockSpec((3, 2 * wt), lambda j, b, s: (0, j)),
                  pl.BlockSpec((1, 2 * wt), lambda j, b, s: (0, j))],
        out_specs=[pl.BlockSpec((ts, 2 * wt), lambda j, b, s: (b * ns + s, j)),
                   pl.BlockSpec((8, 2 * wt), lambda j, b, s: (0, j))],
        out_shape=[jax.ShapeDtypeStruct((n_tok, two_f), F32), jax.ShapeDtypeStruct((8, two_f), F32)],
        scratch_shapes=[pltpu.VMEM((8, 2 * wt), F32)],
        compiler_params=_params(("parallel", "arbitrary", "arbitrary")),
    )(u_il, da, cw_il, cb_il)


def _conv_bwd_input(d_il, cw_il, n_seq, seq, wt):
    n_tok, two_f = d_il.shape
    nct = two_f // (2 * wt)
    ts = _tile(seq, 256, 8)
    ns = seq // ts

    def body(d_ref, w_ref, o_ref, carry):
        @pl.when(pl.program_id(2) == 0)
        def _():
            carry[...] = jnp.zeros_like(carry)

        d = d_ref[...]
        o_ref[...] = (w_ref[2:3, :] * d + w_ref[1:2, :] * _shift_up(d, carry, 1)
                      + w_ref[0:1, :] * _shift_up(d, carry, 2)).astype(o_ref.dtype)
        carry[...] = d[:8, :]

    rev = lambda j, b, s: (b * ns + ns - 1 - s, j)
    return pl.pallas_call(
        body, name="conv_bwd_input", grid=(nct, n_seq, ns),
        in_specs=[pl.BlockSpec((ts, 2 * wt), rev), pl.BlockSpec((3, 2 * wt), lambda j, b, s: (0, j))],
        out_specs=pl.BlockSpec((ts, 2 * wt), rev),
        out_shape=jax.ShapeDtypeStruct((n_tok, two_f), BF16),
        scratch_shapes=[pltpu.VMEM((8, 2 * wt), F32)],
        compiler_params=_params(("parallel", "arbitrary", "arbitrary")),
    )(d_il, cw_il)


HBM = pl.BlockSpec(memory_space=pltpu.HBM)
_CHIP_FLIPS = ((1, 0), (0, 1), (1, 1))


def _place():
    x, y, c = lax.axis_index("x"), lax.axis_index("y"), lax.axis_index("c")
    return x, y, c, 2 * x + y


def _flip(v, f):
    return 1 - v if f else v


def _half_rows(c, half):
    return pl.ds(pl.multiple_of(c * half, 16), half)


def _remote(src, dst, ssem, rsem, dev):
    return pltpu.make_async_remote_copy(src_ref=src, dst_ref=dst, send_sem=ssem, recv_sem=rsem,
                                        device_id=dev, device_id_type=MESH)


def _comm_call(name, body, ins, out_shapes, n_sems):
    return pl.pallas_call(
        body, name=name, in_specs=[HBM] * len(ins), out_specs=[HBM] * len(out_shapes),
        out_shape=[pltpu.HBM(s.shape, s.dtype) for s in out_shapes],
        scratch_shapes=[pltpu.SemaphoreType.DMA((n_sems,)), pltpu.SemaphoreType.DMA((n_sems,))],
    )(*ins)


def _all_gather_weights(shards, smalls):
    n, ns = len(shards), len(smalls)
    nt = n + ns

    def body(*refs):
        src, dst = refs[:nt], refs[nt:2 * nt]
        ssem, rsem = refs[2 * nt:]
        x, y, c, me = _place()
        sib = (x, y, 1 - c)
        waits = []
        small_cps = []
        for s in range(ns):
            for k, (fx, fy) in enumerate(_CHIP_FLIPS):
                sem = 6 * n + 3 * s + k
                cp = _remote(src[n + s], dst[n + s].at[me], ssem.at[sem], rsem.at[sem], (_flip(x, fx), _flip(y, fy), c))
                cp.start()
                small_cps.append(cp)
        for w in range(n):
            rows = _half_rows(c, shards[w].shape[0] // 2)
            for k, (fx, fy) in enumerate(_CHIP_FLIPS):
                cp = _remote(src[w].at[rows], dst[w].at[me, rows], ssem.at[w * 6 + k], rsem.at[w * 6 + k],
                             (_flip(x, fx), _flip(y, fy), c))
                cp.start()
                waits.append(cp)
        fwd = []
        for w in range(n):
            rows = _half_rows(c, shards[w].shape[0] // 2)
            for k, (fx, fy) in enumerate(_CHIP_FLIPS):
                peer = 2 * _flip(x, fx) + _flip(y, fy)
                got = dst[w].at[peer, rows]
                _remote(got, got, ssem.at[w * 6 + k], rsem.at[w * 6 + k], sib).wait_recv()
                cp = _remote(got, got, ssem.at[w * 6 + 3 + k], rsem.at[w * 6 + 3 + k], sib)
                cp.start()
                fwd.append(cp)
        for w in range(n):
            rows = _half_rows(1 - c, shards[w].shape[0] // 2)
            for k, (fx, fy) in enumerate(_CHIP_FLIPS):
                peer = 2 * _flip(x, fx) + _flip(y, fy)
                got = dst[w].at[peer, rows]
                _remote(got, got, ssem.at[w * 6 + 3 + k], rsem.at[w * 6 + 3 + k], sib).wait_recv()
        for cp in small_cps:
            cp.wait()
        for cp in waits + fwd:
            cp.wait_send()

    outs = [jax.ShapeDtypeStruct((N_CHIPS,) + s.shape, s.dtype) for s in list(shards) + list(smalls)]
    res = _comm_call("all_gather_weights", body, list(shards) + list(smalls), outs, 6 * n + 3 * ns)
    return res[:n], res[n:]


def _pair_split(grads):
    n = len(grads)

    def body(*refs):
        src, got = refs[:n], refs[n:2 * n]
        ssem, rsem = refs[2 * n:]
        x, y, c, _ = _place()
        cps = []
        for w in range(n):
            half = grads[w].shape[1] // 2
            cp = _remote(src[w].at[:, _half_rows(1 - c, half)], got[w], ssem.at[w], rsem.at[w], (x, y, 1 - c))
            cp.start()
            cps.append(cp)
        for cp in cps:
            cp.wait()

    outs = [jax.ShapeDtypeStruct((g.shape[0], g.shape[1] // 2, g.shape[2]), g.dtype) for g in grads]
    return _comm_call("rs_pair_split", body, grads, outs, n)


def _chip_scatter(parts):
    n = len(parts)

    def body(*refs):
        src, dst = refs[:n], refs[n:2 * n]
        ssem, rsem = refs[2 * n:]
        x, y, c, _ = _place()
        cps = []
        for w in range(n):
            for k, (fx, fy) in enumerate(_CHIP_FLIPS):
                px, py = _flip(x, fx), _flip(y, fy)
                cp = _remote(src[w].at[2 * px + py], dst[w].at[k], ssem.at[w * 3 + k], rsem.at[w * 3 + k], (px, py, c))
                cp.start()
                cps.append(cp)
        for cp in cps:
            cp.wait()

    outs = [jax.ShapeDtypeStruct((3,) + p.shape[1:], p.dtype) for p in parts]
    return _comm_call("rs_chip_scatter", body, parts, outs, 3 * n)


def _pair_swap(halves):
    n = len(halves)

    def body(*refs):
        src, dst = refs[:n], refs[n:2 * n]
        ssem, rsem = refs[2 * n:]
        x, y, c, _ = _place()
        cps = []
        for w in range(n):
            cp = _remote(src[w], dst[w], ssem.at[w], rsem.at[w], (x, y, 1 - c))
            cp.start()
            cps.append(cp)
        for cp in cps:
            cp.wait()

    outs = [jax.ShapeDtypeStruct(h.shape, h.dtype) for h in halves]
    return _comm_call("rs_pair_swap", body, halves, outs, n)


def _gather_small(vec):
    def body(src, dst, ssem, rsem):
        x, y, c, _ = _place()
        me = 4 * x + 2 * y + c
        cps = []
        for r in range(1, 8):
            dev = (_flip(x, r & 4), _flip(y, r & 2), _flip(c, r & 1))
            cp = _remote(src, dst.at[me], ssem.at[r - 1], rsem.at[r - 1], dev)
            cp.start()
            cps.append(cp)
        for cp in cps:
            cp.wait()

    out = jax.ShapeDtypeStruct((8,) + vec.shape, vec.dtype)
    return _comm_call("gather_small", body, [vec], [out], 7)[0]


def _sum_slots(name, stacked, first=None):
    n, r, c = stacked.shape
    tr = _tile(r, 256, 8)

    def body(*refs):
        s_ref, o_ref = refs[-2], refs[-1]
        acc = refs[0][...].astype(F32) if first is not None else s_ref[0].astype(F32)
        for s in range(0 if first is not None else 1, n):
            acc = acc + s_ref[s].astype(F32)
        o_ref[...] = acc

    row_spec = pl.BlockSpec((tr, c), lambda i: (i, 0))
    in_specs = ([row_spec] if first is not None else []) + [pl.BlockSpec((n, tr, c), lambda i: (0, i, 0))]
    args = ([first] if first is not None else []) + [stacked]
    return pl.pallas_call(
        body, name=name, grid=(r // tr,), in_specs=in_specs, out_specs=row_spec,
        out_shape=jax.ShapeDtypeStruct((r, c), F32), compiler_params=_params(("parallel",)),
    )(*args)


def _adamw(name, w, g, m, v):
    r, c = w.shape
    tr = _tile(r, 256, 8)
    bc1, bc2 = 1.0 - ADAM_B1 ** ADAM_STEP, 1.0 - ADAM_B2 ** ADAM_STEP

    def body(w_ref, g_ref, m_ref, v_ref, d_ref, nm_ref, nv_ref):
        g_v = g_ref[...]
        nm = ADAM_B1 * m_ref[...] + (1.0 - ADAM_B1) * g_v
        nv = ADAM_B2 * v_ref[...] + (1.0 - ADAM_B2) * (g_v * g_v)
        d_ref[...] = -ADAM_LR * ((nm / bc1) / (jnp.sqrt(nv / bc2) + ADAM_EPS) + ADAM_WD * w_ref[...])
        nm_ref[...] = nm
        nv_ref[...] = nv

    spec = pl.BlockSpec((tr, c), lambda i: (i, 0))
    return pl.pallas_call(
        body, name=name, grid=(r // tr,), in_specs=[spec] * 4, out_specs=[spec] * 3,
        out_shape=[jax.ShapeDtypeStruct((r, c), F32)] * 3, compiler_params=_params(("parallel",)),
    )(w, g, m, v)


def _pad_cols(a, cols):
    return jnp.pad(a, ((0, 0), (0, cols - a.shape[1])))


def _rot_cols(w):
    h = w.shape[-1] // 2
    return jnp.concatenate([-w[..., h:], w[..., :h]], axis=-1)


def _unrot_cols(d):
    h = d.shape[-1] // 2
    return jnp.concatenate([d[..., h:], -d[..., :h]], axis=-1)


def _logical(g):
    return jnp.transpose(g, (1, 0, 2)).reshape(g.shape[1], N_CHIPS * g.shape[2])


def _chunks(a, n):
    return jnp.transpose(a.reshape(a.shape[0], N_CHIPS, n), (1, 0, 2))


def kernel(x, positions, pre_mix_norm, w_in, q_a_norm, w_uq, kv_a_norm, w_ukv, b_forget, b_gate, w_branch_mla, w_branch_fox, w_out, post_mix_norm, pre_ffn_norm, w_up, conv_w, conv_b, w_down, post_ffn_norm, loss_target, m_pre_mix_norm, m_w_in, m_q_a_norm, m_w_uq, m_kv_a_norm, m_w_ukv, m_b_forget, m_b_gate, m_w_branch_mla, m_w_branch_fox, m_w_out, m_post_mix_norm, m_pre_ffn_norm, m_w_up, m_conv_w, m_conv_b, m_w_down, m_post_ffn_norm, v_pre_mix_norm, v_w_in, v_q_a_norm, v_w_uq, v_kv_a_norm, v_w_ukv, v_b_forget, v_b_gate, v_w_branch_mla, v_w_branch_fox, v_w_out, v_post_mix_norm, v_pre_ffn_norm, v_w_up, v_conv_w, v_conv_b, v_w_down, v_post_ffn_norm):
    n_seq, seq, d = x.shape
    n_tok = n_seq * seq
    d_in = N_CHIPS * w_in.shape[1]
    two_f = N_CHIPS * w_up.shape[1]
    ff_dim = two_f // 2
    assert d_in == QL + KVL + ROPE + 3 * HEADS * FDIM + HEADS + 2 * d
    n_in_shard = w_in.shape[1]
    in_pad = -(-n_in_shard // LANES) * LANES
    hd = HEADS * LANES
    xc, yc, cc = lax.axis_index("x"), lax.axis_index("y"), lax.axis_index("c")
    chip = 2 * xc + yc
    t_attn = _tile(seq, 256, 128)

    shards = [_pad_cols(w_in, in_pad).astype(BF16), w_uq.astype(BF16), w_ukv.astype(BF16), w_branch_mla.astype(BF16),
              w_branch_fox.astype(BF16), w_out.astype(BF16), w_up.astype(BF16), w_down.astype(BF16)]
    cw8 = jnp.pad(conv_w, ((0, 5), (0, 0)))
    gathered, (g_cw,) = _all_gather_weights(shards, [cw8])
    put_own = lambda g, s: lax.dynamic_update_slice(g, s[None], (chip, 0, 0))
    g_in, g_uq, g_ukv, g_bm, g_bf, g_out, g_up, g_down = [put_own(g, s) for g, s in zip(gathered, shards)]
    g_cw = put_own(g_cw, cw8)

    win_l = jnp.transpose(g_in[:, :, :n_in_shard], (1, 0, 2)).reshape(d, d_in)
    o_q, o_kv, o_kpe = 0, QL, QL + KVL
    o_f = o_kpe + ROPE
    o_fl = o_f + 3 * hd
    o_g = o_fl + HEADS
    w_kpe = win_l[:, o_kpe:o_f]
    zeros = lambda n: jnp.zeros((d, n), BF16)
    win_p = jnp.concatenate([
        win_l[:, o_g:], win_l[:, o_f:o_fl], win_l[:, o_q:o_kpe], w_kpe, zeros(LANES - ROPE),
        _rot_cols(w_kpe), win_l[:, o_fl:o_g], zeros(LANES - ROPE - HEADS)], axis=1)
    n_p = win_p.shape[1]
    cb_gm, cb_gf = 0, 1
    cb_fq, cb_fk, cb_fv = [(2 * d + i * hd) // LANES for i in range(3)]
    c_lat = 2 * d + 3 * hd
    c_kx, c_kr = c_lat + QL + KVL, c_lat + QL + KVL + LANES
    assert n_p == c_kr + LANES

    uq3 = _logical(g_uq).reshape(QL, HEADS, NOPE + ROPE)
    pe = uq3[:, :, NOPE:]
    pad_pe = lambda a: jnp.pad(a, ((0, 0), (0, 0), (0, LANES - ROPE))).reshape(QL, hd)
    wuq_p = jnp.concatenate([uq3[:, :, :NOPE].reshape(QL, hd), pad_pe(pe), pad_pe(_rot_cols(pe))], axis=1)
    ukv3 = _logical(g_ukv).reshape(KVL, HEADS, NOPE + VDIM)
    wukv_p = jnp.concatenate([ukv3[:, :, :NOPE].reshape(KVL, hd), ukv3[:, :, NOPE:].reshape(KVL, hd)], axis=1)

    n_bm, n_up = w_branch_mla.shape[1], w_up.shape[1]
    l_bm, l_up = _Chunked(n_bm), _Chunked(n_up)
    wt = n_up // 2
    n_ut = two_f // wt
    il = lambda cblk: jnp.where(cblk < n_ut // 2, 2 * cblk, 2 * (cblk - n_ut // 2) + 1)
    l_il = _Plain(il)
    to_il = lambda a: a.reshape(a.shape[0], 2, n_ut // 2, wt).transpose(0, 2, 1, 3).reshape(a.shape[0], two_f)
    from_il = lambda a: a.reshape(a.shape[0], n_ut // 2, 2, wt).transpose(0, 2, 1, 3).reshape(a.shape[0], two_f)
    w_down_full = g_down.reshape(ff_dim, d)
    w_out_full = g_out.reshape(d, d)

    inv_freq = 1.0 / (ROPE_THETA ** (jnp.arange(0, ROPE, 2, dtype=F32) / ROPE))
    ang = positions.astype(F32).reshape(n_tok, 1) * inv_freq
    cos, sin = jnp.cos(ang), jnp.sin(ang)
    cs = _pad_cols(jnp.concatenate([cos, cos], axis=1), LANES)
    sn = _pad_cols(jnp.concatenate([sin, sin], axis=1), LANES)

    row = lambda v: v.reshape(1, -1)
    x2 = x.reshape(n_tok, d)
    tgt = loss_target.reshape(n_tok, d)

    (h,) = _rows("rms_pre_mix", lambda r, v: ([r[0] * _rstd(r[0]) * v[0]], []),
                 [(x2, d, 0)], [row(pre_mix_norm)], [(d, BF16)], [], n_tok)
    proj = _mm("proj_in", "nn", h, win_p, n_tok, n_p, d)

    bf_vec = jnp.pad(row(b_forget), ((0, 0), (ROPE, LANES - ROPE - HEADS)))

    def lat_fwd(r, v):
        ql, kvl = r[0], r[1]
        return [ql * _rstd(ql) * v[0], kvl * _rstd(kvl) * v[1], r[2] * r[4] + r[3] * r[5]], []

    qn, kvn, rk = _rows("latent_norms", lat_fwd,
                        [(proj, QL, c_lat // QL), (proj, KVL, (c_lat + QL) // KVL), (proj, LANES, c_kx // LANES),
                         (proj, LANES, c_kr // LANES), (cs, LANES, 0), (sn, LANES, 0)],
                        [row(q_a_norm), row(kv_a_norm)], [(QL, BF16), (KVL, BF16), (LANES, BF16)], [], n_tok)
    q_p = _mm("q_up", "nn", qn, wuq_p, n_tok, 3 * hd, QL)
    kv_p = _mm("kv_up", "nn", kvn, wukv_p, n_tok, 2 * hd, KVL)

    def rope_q(r, v):
        c8, s8 = jnp.tile(r[2], (1, HEADS)), jnp.tile(r[3], (1, HEADS))
        return [r[0] * c8 + r[1] * s8], []

    (rq,) = _rows("rope_q", rope_q, [(q_p, hd, 1), (q_p, hd, 2), (cs, LANES, 0), (sn, LANES, 0)], [],
                  [(hd, BF16)], [], n_tok)

    mla_q = [(q_p, 0, 1), (rq, 0, 1)]
    mla_k = [(kv_p, 0, 1), (rk, 0, 0)]
    mla_v = (kv_p, HEADS, 1)
    mla_scale = (NOPE + ROPE) ** -0.5
    o_mla, lse_mla = _attn_fwd("mla_fwd", mla_q, mla_k, mla_v, None, CHUNK, mla_scale, n_seq, seq, t_attn)

    c_run = _seq_cumsum("forget_cumsum", proj, c_kr // LANES, n_seq, seq, False,
                        pre=lambda z, b: _log_sigmoid(z + b), vec=bf_vec)
    c8 = c_run[:, ROPE:ROPE + HEADS]
    c_col = jnp.transpose(c8).reshape(HEADS, n_tok, 1)
    c_rowf = jnp.transpose(c8.reshape(n_seq, seq, HEADS), (0, 2, 1)).reshape(n_seq * HEADS, 1, seq)
    fox_q, fox_k, fox_v = [(proj, cb_fq, 1)], [(proj, cb_fk, 1)], (proj, cb_fv, 1)
    fox_scale = FDIM ** -0.5
    o_fox, lse_fox = _attn_fwd("fox_fwd", fox_q, fox_k, fox_v, (c_col, c_rowf), 1, fox_scale, n_seq, seq, t_attn)

    pm = _mm("branch_mla", "nn", o_mla, g_bm, n_tok, d, hd, lb=l_bm, tn=n_bm)
    pf = _mm("branch_fox", "nn", o_fox, g_bf, n_tok, d, hd, lb=l_bm, tn=n_bm)
    bg = row(b_gate)

    def merge(r, v):
        return [_sigmoid(r[0] + v[0]) * r[2] + _sigmoid(r[1] + v[1]) * r[3]], []

    (merged,) = _rows("gate_merge", merge, [(proj, d, cb_gm), (proj, d, cb_gf), (pm, d, 0), (pf, d, 0)],
                      [bg[:, :d], bg[:, d:]], [(d, BF16)], [], n_tok)
    y1 = _mm("mix_out", "nn", merged, w_out_full, n_tok, d, d)

    def resid_norm(r, v):
        x1v = r[0] + r[1] * _rstd(r[1]) * v[0]
        return [x1v, x1v * _rstd(x1v) * v[1]], []

    x1, h2 = _rows("post_mix_pre_ffn", resid_norm, [(x2, d, 0), (y1, d, 0)], [row(post_mix_norm), row(pre_ffn_norm)],
                   [(d, F32), (d, BF16)], [], n_tok)

    u_il = _mm("ffn_up", "nn", h2, g_up, n_tok, two_f, d, lb=l_up, lo=l_il, tn=wt)
    cw_il = to_il(_logical(g_cw)[:3])
    cb_il = to_il(row(conv_b))
    act = _conv_glu_fwd(u_il, cw_il, cb_il, n_seq, seq, wt)
    ff = _mm("ffn_down", "nn", act, w_down_full, n_tok, d, ff_dim)

    def final(r, v):
        x1v, ffv, tg = r
        diff = x1v + ffv * _rstd(ffv) * v[0] - tg
        dx2v = diff / d
        dffv, dg4 = _rms_bwd(ffv, v[0], dx2v)
        sq = jnp.sum(jnp.sum(diff * diff, axis=1, keepdims=True), axis=0, keepdims=True)
        return [dx2v, dffv], [dg4, jnp.broadcast_to(sq, (1, LANES))]

    dx2, dff, dg_post_ffn, sq_sum = _rows("loss_post_ffn_bwd", final, [(x1, d, 0), (ff, d, 0), (tgt, d, 0)],
                                          [row(post_ffn_norm)], [(d, F32), (d, BF16)], [(1, d), (1, LANES)], n_tok)
    dact = _mm("ffn_down_dx", "nt", dff, w_down_full, n_tok, ff_dim, d)
    gw_down = _mm("ffn_down_dw", "tn", act, dff, ff_dim, d, n_tok, out_dtype=BF16)
    d_il, conv_acc = _conv_glu_bwd_pre(u_il, dact, cw_il, cb_il, n_seq, seq, wt)
    du_il = _conv_bwd_input(d_il, cw_il, n_seq, seq, wt)
    dh2 = _mm("ffn_up_dx", "nt", du_il, g_up, n_tok, d, two_f, la=l_il, lb=l_up, tk=wt)
    gw_up = _mm("ffn_up_dw", "tn", h2, du_il, d, two_f, n_tok, lb=l_il, lo=l_up, tn=wt, out_dtype=BF16)

    def mid_bwd(r, v):
        x1v, y1v, dx2v, dh2v = r
        d3, dg3 = _rms_bwd(x1v, v[1], dh2v)
        dx1v = dx2v + d3
        dy1v, dg2 = _rms_bwd(y1v, v[0], dx1v)
        return [dx1v, dy1v], [dg3, dg2]

    dx1, dy1, dg_pre_ffn, dg_post_mix = _rows(
        "pre_ffn_post_mix_bwd", mid_bwd, [(x1, d, 0), (y1, d, 0), (dx2, d, 0), (dh2, d, 0)],
        [row(post_mix_norm), row(pre_ffn_norm)], [(d, F32), (d, BF16)], [(1, d), (1, d)], n_tok)
    dmerged = _mm("mix_out_dx", "nt", dy1, w_out_full, n_tok, d, d)
    gw_out = _mm("mix_out_dw", "tn", merged, dy1, d, d, n_tok, out_dtype=BF16)

    def gate_bwd(r, v):
        zm, zf, pmv, pfv, dm = r
        gm, gf = _sigmoid(zm + v[0]), _sigmoid(zf + v[1])
        dzm, dzf = dm * pmv * gm * (1.0 - gm), dm * pfv * gf * (1.0 - gf)
        return [dm * gm, dm * gf, dzm, dzf], [_colsum(dzm), _colsum(dzf)]

    dpm, dpf, dzm, dzf, dbg_m, dbg_f = _rows(
        "gate_merge_bwd", gate_bwd, [(proj, d, cb_gm), (proj, d, cb_gf), (pm, d, 0), (pf, d, 0), (dmerged, d, 0)],
        [bg[:, :d], bg[:, d:]], [(d, BF16)] * 4, [(1, d), (1, d)], n_tok)
    tk_b = min(n_bm, 512)
    do_mla = _mm("branch_mla_dx", "nt", dpm, g_bm, n_tok, hd, d, lb=l_bm, tk=tk_b)
    do_fox = _mm("branch_fox_dx", "nt", dpf, g_bf, n_tok, hd, d, lb=l_bm, tk=tk_b)
    gw_bm = _mm("branch_mla_dw", "tn", o_mla, dpm, hd, d, n_tok, lo=l_bm, tn=n_bm, out_dtype=BF16)
    gw_bf = _mm("branch_fox_dw", "tn", o_fox, dpf, hd, d, n_tok, lo=l_bm, tn=n_bm, out_dtype=BF16)

    dq_nope, drq = _attn_bwd_dq("mla_bwd_dq", mla_q, mla_k, mla_v, o_mla, do_mla, lse_mla, None, CHUNK, mla_scale,
                                n_seq, seq, t_attn)
    dk_nope, drk_h, dv_mla = _attn_bwd_dkv("mla_bwd_dkv", mla_q, mla_k, mla_v, o_mla, do_mla, lse_mla, None, CHUNK,
                                           mla_scale, n_seq, seq, t_attn)
    fox_bias = (c_col, c_rowf)
    dfq, dc_q = _attn_bwd_dq("fox_bwd_dq", fox_q, fox_k, fox_v, o_fox, do_fox, lse_fox, fox_bias, 1, fox_scale,
                             n_seq, seq, t_attn)
    dfk, dfv, dc_k = _attn_bwd_dkv("fox_bwd_dkv", fox_q, fox_k, fox_v, o_fox, do_fox, lse_fox, fox_bias, 1, fox_scale,
                                   n_seq, seq, t_attn)
    dc8 = jnp.transpose(dc_q.reshape(HEADS, n_tok)) + jnp.transpose(dc_k.reshape(n_seq, HEADS, seq), (0, 2, 1)).reshape(n_tok, HEADS)
    dc128 = jnp.pad(dc8, ((0, 0), (ROPE, LANES - ROPE - HEADS)))
    dlogf = _seq_cumsum("forget_cumsum_bwd", dc128, 0, n_seq, seq, True)

    def mla_pack(r, v):
        dqn_v, drq_v, dkn_v, dv_v, drk_v, c1, s1 = r
        c8, s8 = jnp.tile(c1, (1, HEADS)), jnp.tile(s1, (1, HEADS))
        drk = drk_v[:, :LANES]
        for hh in range(1, HEADS):
            drk = drk + drk_v[:, hh * LANES:(hh + 1) * LANES]
        return [jnp.concatenate([dqn_v, drq_v * c8, drq_v * s8], axis=1), jnp.concatenate([dkn_v, dv_v], axis=1),
                drk * c1, drk * s1], []

    dq_p, dkv_p, dkx, dkr = _rows(
        "mla_rope_bwd", mla_pack,
        [(dq_nope, hd, 0), (drq, hd, 0), (dk_nope, hd, 0), (dv_mla, hd, 0), (drk_h, hd, 0), (cs, LANES, 0), (sn, LANES, 0)],
        [], [(3 * hd, BF16), (2 * hd, BF16), (LANES, F32), (LANES, F32)], [], n_tok)
    dqn = _mm("q_up_dx", "nt", dq_p, wuq_p, n_tok, QL, 3 * hd)
    gw_uq_p = _mm("q_up_dw", "tn", qn, dq_p, QL, 3 * hd, n_tok, out_dtype=BF16)
    dkvn = _mm("kv_up_dx", "nt", dkv_p, wukv_p, n_tok, KVL, 2 * hd)
    gw_ukv_p = _mm("kv_up_dw", "tn", kvn, dkv_p, KVL, 2 * hd, n_tok, out_dtype=BF16)

    def lat_bwd(r, v):
        ql, kvl, dqn_v, dkvn_v, dkx_v, dkr_v, zblk, dlf = r
        dql, dgq = _rms_bwd(ql, v[0], dqn_v)
        dkvl, dgkv = _rms_bwd(kvl, v[1], dkvn_v)
        dfl = dlf * _sigmoid(-(zblk + v[2]))
        return [jnp.concatenate([dql, dkvl, dkx_v, dkr_v + dfl], axis=1)], [dgq, dgkv, _colsum(dfl)]

    dlat, dg_q, dg_kv, dbf = _rows(
        "latent_bwd", lat_bwd,
        [(proj, QL, c_lat // QL), (proj, KVL, (c_lat + QL) // KVL), (dqn, QL, 0), (dkvn, KVL, 0), (dkx, LANES, 0),
         (dkr, LANES, 0), (proj, LANES, c_kr // LANES), (dlogf, LANES, 0)],
        [row(q_a_norm), row(kv_a_norm), bf_vec], [(QL + KVL + 2 * LANES, BF16)], [(1, QL), (1, KVL), (1, LANES)], n_tok)
    dproj = jnp.concatenate([dzm, dzf, dfq.astype(BF16), dfk.astype(BF16), dfv.astype(BF16), dlat], axis=1)
    dh = _mm("proj_in_dx", "nt", dproj, win_p, n_tok, d, n_p)
    gw_in_p = _mm("proj_in_dw", "tn", h, dproj, d, n_p, n_tok, out_dtype=BF16)

    def first_bwd(r, v):
        dxa, dg1 = _rms_bwd(r[0], v[0], r[1])
        return [r[2] + dxa], [dg1]

    grad_x, dg_pre_mix = _rows("pre_mix_bwd", first_bwd, [(x2, d, 0), (dh, d, 0), (dx1, d, 0)], [row(pre_mix_norm)],
                               [(d, F32)], [(1, d)], n_tok)

    f32 = lambda a: a.astype(F32)
    kr_blk = gw_in_p[:, c_kr:c_kr + LANES]
    d_kpe = (f32(gw_in_p[:, c_kx:c_kx + ROPE]) + _unrot_cols(f32(kr_blk[:, :ROPE]))).astype(BF16)
    gw_in_l = jnp.concatenate([gw_in_p[:, c_lat:c_lat + QL + KVL], d_kpe, gw_in_p[:, 2 * d:c_lat],
                               kr_blk[:, ROPE:ROPE + HEADS], gw_in_p[:, :2 * d]], axis=1)
    gc_in = jnp.pad(_chunks(gw_in_l, n_in_shard), ((0, 0), (0, 0), (0, in_pad - n_in_shard)))
    uq_parts = [gw_uq_p[:, i * hd:(i + 1) * hd].reshape(QL, HEADS, LANES) for i in range(3)]
    d_pe = (f32(uq_parts[1][:, :, :ROPE]) + _unrot_cols(f32(uq_parts[2][:, :, :ROPE]))).astype(BF16)
    gc_uq = _chunks(jnp.concatenate([uq_parts[0], d_pe], axis=2).reshape(QL, HEADS * (NOPE + ROPE)), w_uq.shape[1])
    gc_ukv = _chunks(jnp.concatenate([gw_ukv_p[:, :hd].reshape(KVL, HEADS, NOPE), gw_ukv_p[:, hd:].reshape(KVL, HEADS, VDIM)],
                                     axis=2).reshape(KVL, HEADS * (NOPE + VDIM)), w_ukv.shape[1])
    grads = [gc_in, gc_uq, gc_ukv, gw_bm, gw_bf, gw_out.reshape(N_CHIPS, d // N_CHIPS, d), gw_up,
             gw_down.reshape(N_CHIPS, ff_dim // N_CHIPS, d)]

    big = ["w_in", "w_uq", "w_ukv", "w_branch_mla", "w_branch_fox", "w_out", "w_up", "w_down"]
    theirs = _pair_split(grads)
    parts = []
    for nm, g, b in zip(big, grads, theirs):
        a = lax.dynamic_slice_in_dim(g, cc * b.shape[1], b.shape[1], axis=1)
        r2, c2 = a.shape[0] * a.shape[1], a.shape[2]
        (p,) = _rows("rs_pair_add_" + nm, lambda r, v: ([r[0].astype(F32) + r[1].astype(F32)], []),
                     [(a.reshape(r2, c2), c2, 0), (b.reshape(r2, c2), c2, 0)], [], [(c2, BF16)], [], r2)
        parts.append(p.reshape(a.shape))
    landed = _chip_scatter(parts)
    halves = [_sum_slots("rs_chip_sum_" + nm, s, first=lax.dynamic_index_in_dim(p, chip, 0, keepdims=False))
              for nm, s, p in zip(big, landed, parts)]
    other = _pair_swap(halves)
    full = [jnp.concatenate([jnp.where(cc == 0, a, b), jnp.where(cc == 0, b, a)], axis=0) for a, b in zip(halves, other)]
    g_big = dict(zip(big, full))
    g_big["w_in"] = g_big["w_in"][:, :n_in_shard]

    conv_acc_l = from_il(conv_acc)
    pieces = [dg_pre_mix, dg_q, dg_kv, dbf, dbg_m, dbg_f, dg_post_mix, dg_pre_ffn, conv_acc_l[3:4], dg_post_ffn,
              conv_acc_l[0:1], conv_acc_l[1:2], conv_acc_l[2:3], sq_sum]
    sizes = [p.shape[1] for p in pieces]
    flat = jnp.concatenate(pieces, axis=1)
    n_rows = -(-flat.shape[1] // (8 * LANES)) * 8
    flat = _pad_cols(flat, n_rows * LANES).reshape(n_rows, LANES)
    slots = lax.dynamic_update_slice(_gather_small(flat), flat[None], (2 * chip + cc, 0, 0))
    total = _sum_slots("small_sum", slots).reshape(1, n_rows * LANES)
    offs = [sum(sizes[:i]) for i in range(len(sizes))]
    tot = [total[0, o:o + s] for o, s in zip(offs, sizes)]
    loss = 0.5 * tot[13][0] / d
    g_small = {"pre_mix_norm": tot[0], "q_a_norm": tot[1], "kv_a_norm": tot[2], "b_forget": tot[3][ROPE:ROPE + HEADS],
               "b_gate": jnp.concatenate([tot[4], tot[5]]), "post_mix_norm": tot[6], "pre_ffn_norm": tot[7],
               "conv_b": tot[8], "post_ffn_norm": tot[9]}
    gcw_full = jnp.stack([tot[10], tot[11], tot[12]])
    g_conv_w = lax.dynamic_slice(gcw_full, (0, chip * n_up), (3, n_up))

    given = dict(pre_mix_norm=(pre_mix_norm, m_pre_mix_norm, v_pre_mix_norm), w_in=(w_in, m_w_in, v_w_in),
                 q_a_norm=(q_a_norm, m_q_a_norm, v_q_a_norm), w_uq=(w_uq, m_w_uq, v_w_uq),
                 kv_a_norm=(kv_a_norm, m_kv_a_norm, v_kv_a_norm), w_ukv=(w_ukv, m_w_ukv, v_w_ukv),
                 b_forget=(b_forget, m_b_forget, v_b_forget), b_gate=(b_gate, m_b_gate, v_b_gate),
                 w_branch_mla=(w_branch_mla, m_w_branch_mla, v_w_branch_mla),
                 w_branch_fox=(w_branch_fox, m_w_branch_fox, v_w_branch_fox), w_out=(w_out, m_w_out, v_w_out),
                 post_mix_norm=(post_mix_norm, m_post_mix_norm, v_post_mix_norm),
                 pre_ffn_norm=(pre_ffn_norm, m_pre_ffn_norm, v_pre_ffn_norm), w_up=(w_up, m_w_up, v_w_up),
                 conv_w=(conv_w, m_conv_w, v_conv_w), conv_b=(conv_b, m_conv_b, v_conv_b),
                 w_down=(w_down, m_w_down, v_w_down), post_ffn_norm=(post_ffn_norm, m_post_ffn_norm, v_post_ffn_norm))
    order = list(given)
    grad, delta, new_m, new_v = {}, {}, {}, {}
    for nm in big + ["conv_w"]:
        grad[nm] = g_big[nm] if nm in g_big else g_conv_w
        delta[nm], new_m[nm], new_v[nm] = _adamw("adamw_" + nm, given[nm][0], grad[nm], given[nm][1], given[nm][2])
    small = list(g_small)
    padded = [-(-g_small[nm].shape[0] // LANES) * LANES for nm in small]
    s_rows = -(-sum(padded) // (8 * LANES)) * 8

    def pack(vals):
        cat = jnp.concatenate([jnp.pad(a, (0, p - a.shape[0])) for a, p in zip(vals, padded)])
        return jnp.pad(cat, (0, s_rows * LANES - cat.shape[0])).reshape(s_rows, LANES)

    packed = _adamw("adamw_small", pack([given[nm][0] for nm in small]), pack([g_small[nm] for nm in small]),
                    pack([given[nm][1] for nm in small]), pack([given[nm][2] for nm in small]))
    s_offs = [sum(padded[:i]) for i in range(len(small))]
    for nm, o in zip(small, s_offs):
        n_el = g_small[nm].shape[0]
        grad[nm] = g_small[nm]
        delta[nm], new_m[nm], new_v[nm] = [p.reshape(-1)[o:o + n_el] for p in packed]
    return (loss, grad_x.reshape(n_seq, seq, d), *[grad[nm] for nm in order], *[delta[nm] for nm in order],
            *[new_m[nm] for nm in order], *[new_v[nm] for nm in order])
```

```python
import functools
import math

import jax
import jax.numpy as jnp
from jax import lax
from jax.experimental import pallas as pl
from jax.experimental.pallas import tpu as pltpu

F32, BF16 = jnp.float32, jnp.bfloat16
MESH = pl.DeviceIdType.MESH

HEADS = 8
NOPE, ROPE, VDIM = 128, 64, 128
QL, KVL = 512, 256
FDIM = 128
CHUNK = 64
ROPE_THETA = 10000.0
EPS = 1e-6
NEG_INF = -1e30
ADAM_LR, ADAM_B1, ADAM_B2, ADAM_EPS, ADAM_WD, ADAM_STEP = 0.001, 0.9, 0.999, 1e-08, 0.01, 10

VMEM_LIMIT_BYTES = 52 * 1024 * 1024
LANES = 128
N_CHIPS = 4


def _params(sem):
    return pltpu.CompilerParams(dimension_semantics=sem, vmem_limit_bytes=VMEM_LIMIT_BYTES)


def _tile(n, target, mult):
    if n <= target:
        return n
    t = (target // mult) * mult
    while t >= mult:
        if n % t == 0:
            return t
        t -= mult
    raise ValueError(f"no tile for {n} (target {target}, multiple of {mult})")


class _Plain:
    def __init__(self, perm=None):
        self.perm = perm

    def spec(self, tr, tc, rc):
        perm = self.perm

        def imap(i, j, k):
            r, c = rc(i, j, k)
            return (r, perm(c) if perm is not None else c)

        return pl.BlockSpec((tr, tc), imap)

    def shape(self, rows, cols):
        return (rows, cols)


class _Chunked:
    def __init__(self, n):
        self.n = n

    def spec(self, tr, tc, rc):
        assert self.n % tc == 0, (self.n, tc)
        per = self.n // tc

        def imap(i, j, k):
            r, c = rc(i, j, k)
            return (c // per, r, c % per)

        return pl.BlockSpec((None, tr, tc), imap)

    def shape(self, rows, cols):
        assert cols == N_CHIPS * self.n
        return (N_CHIPS, rows, self.n)


_DIMS = {"nn": (((1,), (0,)), ((), ())), "nt": (((1,), (1,)), ((), ())), "tn": (((0,), (0,)), ((), ()))}


def _mm(name, mode, a, b, m, n, k, *, tm=1024, tn=1024, tk=512, la=None, lb=None, lo=None, out_dtype=F32):
    la, lb, lo = la or _Plain(), lb or _Plain(), lo or _Plain()
    tm, tn, tk = _tile(m, tm, 128), _tile(n, tn, 128), _tile(k, tk, 128)
    nk = k // tk
    if mode == "nn":
        a_spec = la.spec(tm, tk, lambda i, j, kk: (i, kk))
        b_spec = lb.spec(tk, tn, lambda i, j, kk: (kk, j))
    elif mode == "nt":
        a_spec = la.spec(tm, tk, lambda i, j, kk: (i, kk))
        b_spec = lb.spec(tn, tk, lambda i, j, kk: (j, kk))
    else:
        a_spec = la.spec(tk, tm, lambda i, j, kk: (kk, i))
        b_spec = lb.spec(tk, tn, lambda i, j, kk: (kk, j))
    o_spec = lo.spec(tm, tn, lambda i, j, kk: (i, j))
    dims = _DIMS[mode]

    def body(a_ref, b_ref, o_ref, acc_ref):
        kk = pl.program_id(2)

        @pl.when(kk == 0)
        def _():
            acc_ref[...] = jnp.zeros_like(acc_ref)

        acc_ref[...] += lax.dot_general(a_ref[...].astype(BF16), b_ref[...].astype(BF16), dims,
                                        preferred_element_type=F32)

        @pl.when(kk == nk - 1)
        def _():
            o_ref[...] = acc_ref[...].astype(o_ref.dtype)

    return pl.pallas_call(
        body, name=name, grid=(m // tm, n // tn, nk),
        in_specs=[a_spec, b_spec], out_specs=o_spec,
        out_shape=jax.ShapeDtypeStruct(lo.shape(m, n), out_dtype),
        scratch_shapes=[pltpu.VMEM((tm, tn), F32)],
        compiler_params=_params(("parallel", "parallel", "arbitrary")),
    )(a, b)


def _rows(name, fn, rows_in, vecs_in, rows_out, accs_out, n_rows, tr=256):
    tr = _tile(n_rows, tr, 16)
    nr, nv, no = len(rows_in), len(vecs_in), len(rows_out)

    def body(*refs):
        ins, vecs = refs[:nr], refs[nr:nr + nv]
        outs, accs = refs[nr + nv:nr + nv + no], refs[nr + nv + no:]
        ro, ac = fn([r[...] for r in ins], [v[...] for v in vecs])
        for o_ref, val in zip(outs, ro):
            o_ref[...] = val.astype(o_ref.dtype)
        if accs:
            @pl.when(pl.program_id(0) == 0)
            def _():
                for a_ref in accs:
                    a_ref[...] = jnp.zeros_like(a_ref)

            for a_ref, val in zip(accs, ac):
                a_ref[...] += val

    in_specs = [pl.BlockSpec((tr, cols), functools.partial(lambda i, cb: (i, cb), cb=cb)) for _, cols, cb in rows_in]
    in_specs += [pl.BlockSpec(v.shape, lambda i: (0, 0)) for v in vecs_in]
    out_specs = [pl.BlockSpec((tr, cols), lambda i: (i, 0)) for cols, _ in rows_out]
    out_specs += [pl.BlockSpec((r, cols), lambda i: (0, 0)) for r, cols in accs_out]
    out_shape = [jax.ShapeDtypeStruct((n_rows, cols), dt) for cols, dt in rows_out]
    out_shape += [jax.ShapeDtypeStruct((r, cols), F32) for r, cols in accs_out]
    res = pl.pallas_call(
        body, name=name, grid=(n_rows // tr,), in_specs=in_specs, out_specs=out_specs, out_shape=out_shape,
        compiler_params=_params(("arbitrary",)),
    )(*[a for a, _, _ in rows_in], *vecs_in)
    return res


def _colsum(v):
    return jnp.sum(v, axis=0, keepdims=True)


def _rstd(x):
    return lax.rsqrt(jnp.mean(x * x, axis=-1, keepdims=True) + EPS)


def _rms_bwd(x, g, dy):
    r = _rstd(x)
    xh = x * r
    dxh = dy * g
    dx = r * (dxh - xh * jnp.mean(dxh * xh, axis=-1, keepdims=True))
    return dx, _colsum(dy * xh)


def _sigmoid(z):
    return 1.0 / (1.0 + jnp.exp(-z))


_GELU_K = math.sqrt(2.0 / math.pi)


def _gelu_parts(g):
    t = jnp.tanh(_GELU_K * (g + 0.044715 * g * g * g))
    gel = 0.5 * g * (1.0 + t)
    dgel = 0.5 * (1.0 + t) + 0.5 * g * (1.0 - t * t) * (_GELU_K * (1.0 + 3.0 * 0.044715 * g * g))
    return gel, dgel


def _diag_visible(t, unit):
    rows = lax.broadcasted_iota(jnp.int32, (t, t), 0)
    cols = lax.broadcasted_iota(jnp.int32, (t, t), 1)
    if unit > 1:
        sh = int(math.log2(unit))
        assert 1 << sh == unit and t % unit == 0
        rows, cols = jnp.right_shift(rows, sh), jnp.right_shift(cols, sh)
    return cols <= rows


def _lane_pick(tile, lane):
    idx = lax.broadcasted_iota(jnp.int32, tile.shape, 1)
    return jnp.sum(jnp.where(idx == lane, tile, 0.0), axis=1, keepdims=True)


def _lane_put(tile, lane, col):
    idx = lax.broadcasted_iota(jnp.int32, tile.shape, 1)
    return jnp.where(idx == lane, col, tile)


def _head_cat(refs, shared, rows, h):
    hs = slice(h * LANES, (h + 1) * LANES)
    vals = [(r[rows, :] if sh else r[rows, hs]).astype(BF16) for r, sh in zip(refs, shared)]
    return vals[0] if len(vals) == 1 else jnp.concatenate(vals, axis=1)


def _blk_rows(i, t):
    return pl.ds(pl.multiple_of(i * t, t), t)


def _piece_specs(pieces, rows, row_idx):
    return [pl.BlockSpec((rows, LANES if sh else HEADS * LANES), functools.partial(lambda b, i, cb: (row_idx(b, i), cb), cb=cb))
            for _, cb, sh in pieces]


def _attn_fwd(name, qp, kp, vp, bias, unit, scale, n_seq, seq, t):
    nb = seq // t
    n_tok = n_seq * seq
    nq, nk_p = len(qp), len(kp)
    q_sh, k_sh = [p[2] for p in qp], [p[2] for p in kp]
    nbias = 2 if bias is not None else 0

    def body(*refs):
        q_refs, k_refs = refs[:nq], refs[nq:nq + nk_p]
        v_ref = refs[nq + nk_p]
        bias_refs = refs[nq + nk_p + 1:nq + nk_p + 1 + nbias]
        o_ref, lse_ref = refs[nq + nk_p + 1 + nbias:]
        qi = pl.program_id(1)
        lse_tile = jnp.zeros((t, LANES), F32)
        for h in range(HEADS):
            hs = slice(h * LANES, (h + 1) * LANES)
            q = _head_cat(q_refs, q_sh, slice(None), h)
            cq = _lane_pick(bias_refs[0][...], ROPE + h) if bias is not None else None

            def block(kb, carry, diag, h=h, hs=hs, q=q, cq=cq):
                m, l, acc = carry
                rows = _blk_rows(kb, t)
                s = lax.dot_general(q, _head_cat(k_refs, k_sh, rows, h), _DIMS["nt"], preferred_element_type=F32) * scale
                if bias is not None:
                    s = s + cq - bias_refs[1][kb, h:h + 1, :]
                if diag:
                    s = jnp.where(_diag_visible(t, unit), s, NEG_INF)
                m_new = jnp.maximum(m, jnp.max(s, axis=1, keepdims=True))
                alpha = jnp.exp(m - m_new)
                p = jnp.exp(s - m_new)
                l = alpha * l + jnp.sum(p, axis=1, keepdims=True)
                acc = alpha * acc + jnp.dot(p.astype(BF16), v_ref[rows, hs].astype(BF16), preferred_element_type=F32)
                return m_new, l, acc

            init = (jnp.full((t, 1), NEG_INF, F32), jnp.zeros((t, 1), F32), jnp.zeros((t, LANES), F32))
            carry = lax.fori_loop(0, qi, lambda kb, c: block(kb, c, False), init)
            m, l, acc = block(qi, carry, True)
            o_ref[:, hs] = acc / l
            lse_tile = _lane_put(lse_tile, h, m + jnp.log(l))
        lse_ref[...] = lse_tile

    tile_row = lambda b, i: b * nb + i
    seq_row = lambda b, i: b
    lane_tile = pl.BlockSpec((t, LANES), lambda b, i: (b * nb + i, 0))
    in_specs = _piece_specs(qp, t, tile_row) + _piece_specs(kp, seq, seq_row) + _piece_specs([vp + (False,)], seq, seq_row)
    args = [p[0] for p in qp] + [p[0] for p in kp] + [vp[0]]
    if bias is not None:
        in_specs += [lane_tile, pl.BlockSpec((None, nb, HEADS, t), lambda b, i: (b, 0, 0, 0))]
        args += list(bias)
    return pl.pallas_call(
        body, name=name, grid=(n_seq, nb), in_specs=in_specs,
        out_specs=[pl.BlockSpec((t, HEADS * LANES), lambda b, i: (b * nb + i, 0)), lane_tile],
        out_shape=[jax.ShapeDtypeStruct((n_tok, HEADS * LANES), F32), jax.ShapeDtypeStruct((n_tok, LANES), F32)],
        compiler_params=_params(("parallel", "arbitrary")),
    )(*args)


def _attn_bwd_dq(name, qp, kp, vp, o, do, lse, bias, unit, scale, n_seq, seq, t):
    nb = seq // t
    n_tok = n_seq * seq
    nq, nk_p = len(qp), len(kp)
    q_sh, k_sh = [p[2] for p in qp], [p[2] for p in kp]
    nbias = 2 if bias is not None else 0
    n_in = nq + nk_p + 4 + nbias

    def body(*refs):
        q_refs, k_refs = refs[:nq], refs[nq:nq + nk_p]
        v_ref, o_ref, do_ref, lse_ref = refs[nq + nk_p:nq + nk_p + 4]
        bias_refs = refs[nq + nk_p + 4:n_in]
        dq_refs = refs[n_in:n_in + nq]
        delta_ref, dob_ref = refs[n_in + nq:n_in + nq + 2]
        qi = pl.program_id(1)
        delta_tile = jnp.zeros((t, LANES), F32)
        dc_tile = jnp.zeros((t, LANES), F32)
        lse_all = lse_ref[...]
        for h in range(HEADS):
            hs = slice(h * LANES, (h + 1) * LANES)
            q = _head_cat(q_refs, q_sh, slice(None), h)
            do_f = do_ref[:, hs]
            do_b = do_f.astype(BF16)
            dob_ref[:, hs] = do_b
            delta = jnp.sum(do_f * o_ref[:, hs], axis=1, keepdims=True)
            lse = _lane_pick(lse_all, h)
            cq = _lane_pick(bias_refs[0][...], ROPE + h) if bias is not None else None

            def block(kb, carry, diag, h=h, hs=hs, q=q, cq=cq, do_b=do_b, delta=delta, lse=lse):
                dq_acc, dc_acc = carry
                rows = _blk_rows(kb, t)
                k = _head_cat(k_refs, k_sh, rows, h)
                s = lax.dot_general(q, k, _DIMS["nt"], preferred_element_type=F32) * scale
                if bias is not None:
                    s = s + cq - bias_refs[1][kb, h:h + 1, :]
                if diag:
                    s = jnp.where(_diag_visible(t, unit), s, NEG_INF)
                p = jnp.exp(s - lse)
                dp = lax.dot_general(do_b, v_ref[rows, hs].astype(BF16), _DIMS["nt"], preferred_element_type=F32)
                ds = p * (dp - delta)
                return (dq_acc + jnp.dot(ds.astype(BF16), k, preferred_element_type=F32),
                        dc_acc + jnp.sum(ds, axis=1, keepdims=True))

            init = (jnp.zeros((t, nq * LANES), F32), jnp.zeros((t, 1), F32))
            carry = lax.fori_loop(0, qi, lambda kb, c: block(kb, c, False), init)
            dq_acc, dc_acc = block(qi, carry, True)
            for n_p in range(nq):
                dq_refs[n_p][:, hs] = dq_acc[:, n_p * LANES:(n_p + 1) * LANES] * scale
            delta_tile = _lane_put(delta_tile, h, delta)
            dc_tile = _lane_put(dc_tile, ROPE + h, dc_acc)
        delta_ref[...] = delta_tile
        if bias is not None:
            refs[n_in + nq + 2][...] = dc_tile

    tile_row = lambda b, i: b * nb + i
    seq_row = lambda b, i: b
    lane_tile = pl.BlockSpec((t, LANES), lambda b, i: (b * nb + i, 0))
    head_tile = pl.BlockSpec((t, HEADS * LANES), lambda b, i: (b * nb + i, 0))
    in_specs = _piece_specs(qp, t, tile_row) + _piece_specs(kp, seq, seq_row) + _piece_specs([vp + (False,)], seq, seq_row)
    in_specs += [head_tile, head_tile, lane_tile]
    args = [p[0] for p in qp] + [p[0] for p in kp] + [vp[0], o, do, lse]
    if bias is not None:
        in_specs += [lane_tile, pl.BlockSpec((None, nb, HEADS, t), lambda b, i: (b, 0, 0, 0))]
        args += list(bias)
    out_specs = [head_tile] * nq + [lane_tile, head_tile] + ([lane_tile] if bias is not None else [])
    out_shape = [jax.ShapeDtypeStruct((n_tok, HEADS * LANES), F32)] * nq
    out_shape += [jax.ShapeDtypeStruct((n_tok, LANES), F32), jax.ShapeDtypeStruct((n_tok, HEADS * LANES), BF16)]
    if bias is not None:
        out_shape.append(jax.ShapeDtypeStruct((n_tok, LANES), F32))
    return pl.pallas_call(
        body, name=name, grid=(n_seq, nb), in_specs=in_specs, out_specs=out_specs, out_shape=out_shape,
        compiler_params=_params(("parallel", "arbitrary")),
    )(*args)


def _attn_bwd_dkv(name, qp, kp, vp, dob, lse, delta, bias, unit, scale, n_seq, seq, t):
    nb = seq // t
    n_tok = n_seq * seq
    nq, nk_p = len(qp), len(kp)
    q_sh, k_sh = [p[2] for p in qp], [p[2] for p in kp]
    nbias = 2 if bias is not None else 0
    n_in = nq + nk_p + 4 + nbias

    def body(*refs):
        q_refs, k_refs = refs[:nq], refs[nq:nq + nk_p]
        v_ref, dob_ref, lse_ref, delta_ref = refs[nq + nk_p:nq + nk_p + 4]
        bias_refs = refs[nq + nk_p + 4:n_in]
        dk_refs = refs[n_in:n_in + nk_p]
        dv_ref = refs[n_in + nk_p]
        ki = pl.program_id(1)
        shared_acc = [jnp.zeros((t, LANES), F32) for _ in range(nk_p)]
        for h in range(HEADS):
            hs = slice(h * LANES, (h + 1) * LANES)
            k = _head_cat(k_refs, k_sh, slice(None), h)
            v = v_ref[:, hs].astype(BF16)
            ck = bias_refs[1][h:h + 1, :] if bias is not None else None

            def block(qb, carry, diag, h=h, hs=hs, k=k, v=v, ck=ck):
                dk_acc, dv_acc, dc_acc = carry
                rows = _blk_rows(qb, t)
                q = _head_cat(q_refs, q_sh, rows, h)
                s = lax.dot_general(q, k, _DIMS["nt"], preferred_element_type=F32) * scale
                if bias is not None:
                    s = s + _lane_pick(bias_refs[0][rows, :], ROPE + h) - ck
                if diag:
                    s = jnp.where(_diag_visible(t, unit), s, NEG_INF)
                p = jnp.exp(s - _lane_pick(lse_ref[rows, :], h))
                do_b = dob_ref[rows, hs]
                dp = lax.dot_general(do_b, v, _DIMS["nt"], preferred_element_type=F32)
                ds = p * (dp - _lane_pick(delta_ref[rows, :], h))
                return (dk_acc + lax.dot_general(ds.astype(BF16), q, _DIMS["tn"], preferred_element_type=F32),
                        dv_acc + lax.dot_general(p.astype(BF16), do_b, _DIMS["tn"], preferred_element_type=F32),
                        dc_acc - jnp.sum(ds, axis=0, keepdims=True))

            init = (jnp.zeros((t, nk_p * LANES), F32), jnp.zeros((t, LANES), F32), jnp.zeros((1, t), F32))
            carry = block(ki, init, True)
            dk_acc, dv_acc, dc_acc = lax.fori_loop(ki + 1, nb, lambda qb, c: block(qb, c, False), carry)
            for n_p in range(nk_p):
                part = dk_acc[:, n_p * LANES:(n_p + 1) * LANES] * scale
                if k_sh[n_p]:
                    shared_acc[n_p] = shared_acc[n_p] + part
                else:
                    dk_refs[n_p][:, hs] = part
            dv_ref[:, hs] = dv_acc
            if bias is not None:
                refs[n_in + nk_p + 1][h:h + 1, :] = dc_acc
        for n_p in range(nk_p):
            if k_sh[n_p]:
                dk_refs[n_p][...] = shared_acc[n_p]

    tile_row = lambda b, i: b * nb + i
    seq_row = lambda b, i: b
    lane_seq = pl.BlockSpec((seq, LANES), lambda b, i: (b, 0))
    head_tile = pl.BlockSpec((t, HEADS * LANES), lambda b, i: (b * nb + i, 0))
    row_tile = pl.BlockSpec((None, None, HEADS, t), lambda b, i: (b, i, 0, 0))
    in_specs = _piece_specs(qp, seq, seq_row) + _piece_specs(kp, t, tile_row) + _piece_specs([vp + (False,)], t, tile_row)
    in_specs += [pl.BlockSpec((seq, HEADS * LANES), lambda b, i: (b, 0)), lane_seq, lane_seq]
    args = [p[0] for p in qp] + [p[0] for p in kp] + [vp[0], dob, lse, delta]
    if bias is not None:
        in_specs += [lane_seq, row_tile]
        args += list(bias)
    out_specs = [pl.BlockSpec((t, LANES if sh else HEADS * LANES), lambda b, i: (b * nb + i, 0)) for sh in k_sh] + [head_tile]
    out_shape = [jax.ShapeDtypeStruct((n_tok, LANES if sh else HEADS * LANES), F32) for sh in k_sh]
    out_shape.append(jax.ShapeDtypeStruct((n_tok, HEADS * LANES), F32))
    if bias is not None:
        out_specs.append(row_tile)
        out_shape.append(jax.ShapeDtypeStruct((n_seq, nb, HEADS, t), F32))
    return pl.pallas_call(
        body, name=name, grid=(n_seq, nb), in_specs=in_specs, out_specs=out_specs, out_shape=out_shape,
        compiler_params=_params(("parallel", "arbitrary")),
    )(*args)


def _old_attn_bwd_dq(name, qp, kp, vp, o, do, lse, bias, unit, scale, n_seq, seq, t):
    nb = seq // t
    n_tok = n_seq * seq
    nq, nk_p = len(qp), len(kp)
    nbias = 2 if bias is not None else 0
    n_in = nq + nk_p + 4 + nbias
    n_out = nq + (1 if bias is not None else 0)

    def body(*refs):
        q_refs, k_refs = refs[:nq], refs[nq:nq + nk_p]
        v_ref, o_ref, do_ref, lse_ref = refs[nq + nk_p:nq + nk_p + 4]
        bias_refs = refs[nq + nk_p + 4:n_in]
        outs = refs[n_in:n_in + n_out]
        dq_s, delta_s, dc_s = refs[n_in + n_out:]
        qi, ki = pl.program_id(2), pl.program_id(3)

        @pl.when(ki == 0)
        def _():
            dq_s[...] = jnp.zeros_like(dq_s)
            dc_s[...] = jnp.zeros_like(dc_s)
            delta_s[...] = jnp.sum(do_ref[...] * o_ref[...], axis=1, keepdims=True)

        @pl.when(ki <= qi)
        def _():
            s = _scores(q_refs, k_refs, bias_refs, qi, ki, t, unit, scale)
            p = jnp.exp(s - lse_ref[...])
            dp = lax.dot_general(do_ref[...].astype(BF16), v_ref[...].astype(BF16), _DIMS["nt"],
                                 preferred_element_type=F32)
            ds = p * (dp - delta_s[...])
            dq_s[...] += jnp.dot(ds.astype(BF16), _cat(k_refs), preferred_element_type=F32)
            dc_s[...] += jnp.sum(ds, axis=1, keepdims=True)

        @pl.when(ki == qi)
        def _():
            for n_p in range(nq):
                outs[n_p][...] = dq_s[:, n_p * LANES:(n_p + 1) * LANES] * scale
            if bias is not None:
                outs[nq][...] = dc_s[...]

    q_row = lambda b, i, j: b * nb + i
    k_row = lambda b, i, j: b * nb + jnp.minimum(j, i)
    head_q = pl.BlockSpec((t, LANES), lambda b, h, i, j: (b * nb + i, h))
    col_q = pl.BlockSpec((None, t, 1), lambda b, h, i, j: (h, b * nb + i, 0))
    in_specs = [_piece_spec(t, p, q_row) for p in qp] + [_piece_spec(t, p, k_row) for p in kp]
    in_specs += [_piece_spec(t, vp, k_row), head_q, head_q, col_q]
    args = [p[0] for p in qp] + [p[0] for p in kp] + [vp[0], o, do, lse]
    if bias is not None:
        in_specs += [col_q, pl.BlockSpec((None, 1, t), lambda b, h, i, j: (b * HEADS + h, 0, jnp.minimum(j, i)))]
        args += list(bias)
    out_specs = [head_q] * nq + ([col_q] if bias is not None else [])
    out_shape = [jax.ShapeDtypeStruct((n_tok, HEADS * LANES), F32)] * nq
    if bias is not None:
        out_shape.append(jax.ShapeDtypeStruct((HEADS, n_tok, 1), F32))
    return pl.pallas_call(
        body, name=name, grid=(n_seq, HEADS, nb, nb), in_specs=in_specs, out_specs=out_specs, out_shape=out_shape,
        scratch_shapes=[pltpu.VMEM((t, nq * LANES), F32), pltpu.VMEM((t, 1), F32), pltpu.VMEM((t, 1), F32)],
        compiler_params=_params(("parallel", "parallel", "arbitrary", "arbitrary")),
    )(*args)


def _old_attn_bwd_dkv(name, qp, kp, vp, o, do, lse, bias, unit, scale, n_seq, seq, t):
    nb = seq // t
    n_tok = n_seq * seq
    nq, nk_p = len(qp), len(kp)
    nbias = 2 if bias is not None else 0
    n_in = nq + nk_p + 4 + nbias
    n_out = nk_p + 1 + (1 if bias is not None else 0)

    def body(*refs):
        q_refs, k_refs = refs[:nq], refs[nq:nq + nk_p]
        v_ref, o_ref, do_ref, lse_ref = refs[nq + nk_p:nq + nk_p + 4]
        bias_refs = refs[nq + nk_p + 4:n_in]
        outs = refs[n_in:n_in + n_out]
        dk_s, dv_s, dc_s = refs[n_in + n_out:]
        ki, qi = pl.program_id(2), pl.program_id(3)

        @pl.when(qi == 0)
        def _():
            dk_s[...] = jnp.zeros_like(dk_s)
            dv_s[...] = jnp.zeros_like(dv_s)
            dc_s[...] = jnp.zeros_like(dc_s)

        @pl.when(qi >= ki)
        def _():
            s = _scores(q_refs, k_refs, bias_refs, qi, ki, t, unit, scale)
            p = jnp.exp(s - lse_ref[...])
            do_b = do_ref[...].astype(BF16)
            delta = jnp.sum(do_ref[...] * o_ref[...], axis=1, keepdims=True)
            dp = lax.dot_general(do_b, v_ref[...].astype(BF16), _DIMS["nt"], preferred_element_type=F32)
            ds = p * (dp - delta)
            dv_s[...] += lax.dot_general(p.astype(BF16), do_b, _DIMS["tn"], preferred_element_type=F32)
            dk_s[...] += lax.dot_general(ds.astype(BF16), _cat(q_refs), _DIMS["tn"], preferred_element_type=F32)
            dc_s[...] -= jnp.sum(ds, axis=0, keepdims=True)

        @pl.when(qi == nb - 1)
        def _():
            for n_p in range(nk_p):
                outs[n_p][...] = dk_s[:, n_p * LANES:(n_p + 1) * LANES] * scale
            outs[nk_p][...] = dv_s[...]
            if bias is not None:
                outs[nk_p + 1][...] = dc_s[...]

    q_row = lambda b, i, j: b * nb + jnp.maximum(j, i)
    k_row = lambda b, i, j: b * nb + i
    head_q = pl.BlockSpec((t, LANES), lambda b, h, i, j: (b * nb + jnp.maximum(j, i), h))
    col_q = pl.BlockSpec((None, t, 1), lambda b, h, i, j: (h, b * nb + jnp.maximum(j, i), 0))
    head_k = pl.BlockSpec((t, LANES), lambda b, h, i, j: (b * nb + i, h))
    row_k = pl.BlockSpec((None, 1, t), lambda b, h, i, j: (b * HEADS + h, 0, i))
    in_specs = [_piece_spec(t, p, q_row) for p in qp] + [_piece_spec(t, p, k_row) for p in kp]
    in_specs += [_piece_spec(t, vp, k_row), head_q, head_q, col_q]
    args = [p[0] for p in qp] + [p[0] for p in kp] + [vp[0], o, do, lse]
    if bias is not None:
        in_specs += [col_q, row_k]
        args += list(bias)
    out_specs = [head_k] * (nk_p + 1) + ([row_k] if bias is not None else [])
    out_shape = [jax.ShapeDtypeStruct((n_tok, HEADS * LANES), F32)] * (nk_p + 1)
    if bias is not None:
        out_shape.append(jax.ShapeDtypeStruct((n_seq * HEADS, 1, seq), F32))
    return pl.pallas_call(
        body, name=name, grid=(n_seq, HEADS, nb, nb), in_specs=in_specs, out_specs=out_specs, out_shape=out_shape,
        scratch_shapes=[pltpu.VMEM((t, nk_p * LANES), F32), pltpu.VMEM((t, LANES), F32), pltpu.VMEM((1, t), F32)],
        compiler_params=_params(("parallel", "parallel", "arbitrary", "arbitrary")),
    )(*args)


def _seq_cumsum(name, x, col_block, n_seq, seq, reverse, pre=None, vec=None):
    t = _tile(seq, 256, 128)
    nb = seq // t

    def body(*refs):
        x_ref = refs[0]
        vec_ref = refs[1] if vec is not None else None
        o_ref, carry = refs[-2], refs[-1]

        @pl.when(pl.program_id(1) == 0)
        def _():
            carry[...] = jnp.zeros_like(carry)

        v = x_ref[...]
        if pre is not None:
            v = pre(v, vec_ref[...])
        r = lax.broadcasted_iota(jnp.int32, (t, t), 0)
        c = lax.broadcasted_iota(jnp.int32, (t, t), 1)
        tri = jnp.where((c >= r) if reverse else (c <= r), 1.0, 0.0).astype(BF16)
        hi = v.astype(BF16)
        mid = (v - hi.astype(F32)).astype(BF16)
        lo = (v - hi.astype(F32) - mid.astype(F32)).astype(BF16)
        acc = jnp.dot(tri, hi, preferred_element_type=F32)
        acc += jnp.dot(tri, mid, preferred_element_type=F32)
        acc += jnp.dot(tri, lo, preferred_element_type=F32)
        o_ref[...] = acc + carry[...]
        carry[...] += _colsum(v)

    blk = (lambda b, i: (b * nb + nb - 1 - i)) if reverse else (lambda b, i: (b * nb + i))
    in_specs = [pl.BlockSpec((t, LANES), lambda b, i: (blk(b, i), col_block))]
    args = [x]
    if vec is not None:
        in_specs.append(pl.BlockSpec(vec.shape, lambda b, i: (0, 0)))
        args.append(vec)
    return pl.pallas_call(
        body, name=name, grid=(n_seq, nb), in_specs=in_specs,
        out_specs=pl.BlockSpec((t, LANES), lambda b, i: (blk(b, i), 0)),
        out_shape=jax.ShapeDtypeStruct((n_seq * seq, LANES), F32),
        scratch_shapes=[pltpu.VMEM((1, LANES), F32)],
        compiler_params=_params(("arbitrary", "arbitrary")),
    )(*args)


def _log_sigmoid(z):
    return -(jnp.maximum(-z, 0.0) + jnp.log(1.0 + jnp.exp(-jnp.abs(z))))


def _shift_down(u, prev_ref, n):
    out = pltpu.roll(u, n, 0)
    row = lax.broadcasted_iota(jnp.int32, u.shape, 0)
    for r in range(n):
        out = jnp.where(row == r, prev_ref[8 - n + r:8 - n + r + 1, :], out)
    return out


def _shift_up(u, next_ref, n):
    ts = u.shape[0]
    out = pltpu.roll(u, ts - n, 0)
    row = lax.broadcasted_iota(jnp.int32, u.shape, 0)
    for r in range(n):
        out = jnp.where(row == ts - n + r, next_ref[r:r + 1, :], out)
    return out


def _conv_taps(u, prev_ref, w_ref, b_ref):
    s1, s2 = _shift_down(u, prev_ref, 1), _shift_down(u, prev_ref, 2)
    return (w_ref[0:1, :] * s2 + w_ref[1:2, :] * s1 + w_ref[2:3, :] * u) + b_ref[...], s1, s2


def _conv_glu_fwd(u_il, cw_il, cb_il, n_seq, seq, wt):
    n_tok, two_f = u_il.shape
    nct = two_f // (2 * wt)
    ts = _tile(seq, 256, 8)
    ns = seq // ts

    def body(u_ref, w_ref, b_ref, a_ref, carry):
        @pl.when(pl.program_id(2) == 0)
        def _():
            carry[...] = jnp.zeros_like(carry)

        u = u_ref[...]
        uc, _, _ = _conv_taps(u, carry, w_ref, b_ref)
        gel, _ = _gelu_parts(uc[:, :wt])
        a_ref[...] = (gel * uc[:, wt:]).astype(a_ref.dtype)
        carry[...] = u[ts - 8:, :]

    return pl.pallas_call(
        body, name="conv_glu_fwd", grid=(nct, n_seq, ns),
        in_specs=[pl.BlockSpec((ts, 2 * wt), lambda j, b, s: (b * ns + s, j)),
                  pl.BlockSpec((3, 2 * wt), lambda j, b, s: (0, j)),
                  pl.BlockSpec((1, 2 * wt), lambda j, b, s: (0, j))],
        out_specs=pl.BlockSpec((ts, wt), lambda j, b, s: (b * ns + s, j)),
        out_shape=jax.ShapeDtypeStruct((n_tok, two_f // 2), BF16),
        scratch_shapes=[pltpu.VMEM((8, 2 * wt), F32)],
        compiler_params=_params(("parallel", "arbitrary", "arbitrary")),
    )(u_il, cw_il, cb_il)


def _conv_glu_bwd_pre(u_il, da, cw_il, cb_il, n_seq, seq, wt):
    n_tok, two_f = u_il.shape
    nct = two_f // (2 * wt)
    ts = _tile(seq, 256, 8)
    ns = seq // ts

    def body(u_ref, da_ref, w_ref, b_ref, d_ref, acc_ref, carry):
        first = jnp.logical_and(pl.program_id(1) == 0, pl.program_id(2) == 0)

        @pl.when(first)
        def _():
            acc_ref[...] = jnp.zeros_like(acc_ref)

        @pl.when(pl.program_id(2) == 0)
        def _():
            carry[...] = jnp.zeros_like(carry)

        u = u_ref[...]
        uc, s1, s2 = _conv_taps(u, carry, w_ref, b_ref)
        gel, dgel = _gelu_parts(uc[:, :wt])
        da_v = da_ref[...]
        d = jnp.concatenate([da_v * uc[:, wt:] * dgel, da_v * gel], axis=1)
        d_ref[...] = d
        acc_ref[0:1, :] += _colsum(d * s2)
        acc_ref[1:2, :] += _colsum(d * s1)
        acc_ref[2:3, :] += _colsum(d * u)
        acc_ref[3:4, :] += _colsum(d)
        carry[...] = u[ts - 8:, :]

    return pl.pallas_call(
        body, name="conv_glu_bwd_pre", grid=(nct, n_seq, ns),
        in_specs=[pl.BlockSpec((ts, 2 * wt), lambda j, b, s: (b * ns + s, j)),
                  pl.BlockSpec((ts, wt), lambda j, b, s: (b * ns + s, j)),
                  pl.BlockSpec((3, 2 * wt), lambda j, b, s: (0, j)),
                  pl.BlockSpec((1, 2 * wt), lambda j, b, s: (0, j))],
        out_specs=[pl.BlockSpec((ts, 2 * wt), lambda j, b, s: (b * ns + s, j)),
                   pl.BlockSpec((8, 2 * wt), lambda j, b, s: (0, j))],
        out_shape=[jax.ShapeDtypeStruct((n_tok, two_f), F32), jax.ShapeDtypeStruct((8, two_f), F32)],
        scratch_shapes=[pltpu.VMEM((8, 2 * wt), F32)],
        compiler_params=_params(("parallel", "arbitrary", "arbitrary")),
    )(u_il, da, cw_il, cb_il)


def _conv_bwd_input(d_il, cw_il, n_seq, seq, wt):
    n_tok, two_f = d_il.shape
    nct = two_f // (2 * wt)
    ts = _tile(seq, 256, 8)
    ns = seq // ts

    def body(d_ref, w_ref, o_ref, carry):
        @pl.when(pl.program_id(2) == 0)
        def _():
            carry[...] = jnp.zeros_like(carry)

        d = d_ref[...]
        o_ref[...] = (w_ref[2:3, :] * d + w_ref[1:2, :] * _shift_up(d, carry, 1)
                      + w_ref[0:1, :] * _shift_up(d, carry, 2)).astype(o_ref.dtype)
        carry[...] = d[:8, :]

    rev = lambda j, b, s: (b * ns + ns - 1 - s, j)
    return pl.pallas_call(
        body, name="conv_bwd_input", grid=(nct, n_seq, ns),
        in_specs=[pl.BlockSpec((ts, 2 * wt), rev), pl.BlockSpec((3, 2 * wt), lambda j, b, s: (0, j))],
        out_specs=pl.BlockSpec((ts, 2 * wt), rev),
        out_shape=jax.ShapeDtypeStruct((n_tok, two_f), BF16),
        scratch_shapes=[pltpu.VMEM((8, 2 * wt), F32)],
        compiler_params=_params(("parallel", "arbitrary", "arbitrary")),
    )(d_il, cw_il)


HBM = pl.BlockSpec(memory_space=pltpu.HBM)
_CHIP_FLIPS = ((1, 0), (0, 1), (1, 1))


def _place():
    x, y, c = lax.axis_index("x"), lax.axis_index("y"), lax.axis_index("c")
    return x, y, c, 2 * x + y


def _flip(v, f):
    return 1 - v if f else v


def _half_rows(c, half):
    return pl.ds(pl.multiple_of(c * half, 16), half)


def _remote(src, dst, ssem, rsem, dev):
    return pltpu.make_async_remote_copy(src_ref=src, dst_ref=dst, send_sem=ssem, recv_sem=rsem,
                                        device_id=dev, device_id_type=MESH)


def _comm_call(name, body, ins, out_shapes, n_sems):
    return pl.pallas_call(
        body, name=name, in_specs=[HBM] * len(ins), out_specs=[HBM] * len(out_shapes),
        out_shape=[pltpu.HBM(s.shape, s.dtype) for s in out_shapes],
        scratch_shapes=[pltpu.SemaphoreType.DMA((n_sems,)), pltpu.SemaphoreType.DMA((n_sems,))],
    )(*ins)


def _all_gather_weights(shards, smalls):
    n, ns = len(shards), len(smalls)
    nt = n + ns

    def body(*refs):
        src, dst = refs[:nt], refs[nt:2 * nt]
        ssem, rsem = refs[2 * nt:]
        x, y, c, me = _place()
        sib = (x, y, 1 - c)
        waits = []
        small_cps = []
        for s in range(ns):
            for k, (fx, fy) in enumerate(_CHIP_FLIPS):
                sem = 6 * n + 3 * s + k
                cp = _remote(src[n + s], dst[n + s].at[me], ssem.at[sem], rsem.at[sem], (_flip(x, fx), _flip(y, fy), c))
                cp.start()
                small_cps.append(cp)
        for w in range(n):
            rows = _half_rows(c, shards[w].shape[0] // 2)
            for k, (fx, fy) in enumerate(_CHIP_FLIPS):
                cp = _remote(src[w].at[rows], dst[w].at[me, rows], ssem.at[w * 6 + k], rsem.at[w * 6 + k],
                             (_flip(x, fx), _flip(y, fy), c))
                cp.start()
                waits.append(cp)
        fwd = []
        for w in range(n):
            rows = _half_rows(c, shards[w].shape[0] // 2)
            for k, (fx, fy) in enumerate(_CHIP_FLIPS):
                peer = 2 * _flip(x, fx) + _flip(y, fy)
                got = dst[w].at[peer, rows]
                _remote(got, got, ssem.at[w * 6 + k], rsem.at[w * 6 + k], sib).wait_recv()
                cp = _remote(got, got, ssem.at[w * 6 + 3 + k], rsem.at[w * 6 + 3 + k], sib)
                cp.start()
                fwd.append(cp)
        for w in range(n):
            rows = _half_rows(1 - c, shards[w].shape[0] // 2)
            for k, (fx, fy) in enumerate(_CHIP_FLIPS):
                peer = 2 * _flip(x, fx) + _flip(y, fy)
                got = dst[w].at[peer, rows]
                _remote(got, got, ssem.at[w * 6 + 3 + k], rsem.at[w * 6 + 3 + k], sib).wait_recv()
        for cp in small_cps:
            cp.wait()
        for cp in waits + fwd:
            cp.wait_send()

    outs = [jax.ShapeDtypeStruct((N_CHIPS,) + s.shape, s.dtype) for s in list(shards) + list(smalls)]
    res = _comm_call("all_gather_weights", body, list(shards) + list(smalls), outs, 6 * n + 3 * ns)
    return res[:n], res[n:]


def _pair_split(grads):
    n = len(grads)

    def body(*refs):
        src, got = refs[:n], refs[n:2 * n]
        ssem, rsem = refs[2 * n:]
        x, y, c, _ = _place()
        cps = []
        for w in range(n):
            half = grads[w].shape[1] // 2
            cp = _remote(src[w].at[:, _half_rows(1 - c, half)], got[w], ssem.at[w], rsem.at[w], (x, y, 1 - c))
            cp.start()
            cps.append(cp)
        for cp in cps:
            cp.wait()

    outs = [jax.ShapeDtypeStruct((g.shape[0], g.shape[1] // 2, g.shape[2]), g.dtype) for g in grads]
    return _comm_call("rs_pair_split", body, grads, outs, n)


def _chip_scatter(parts):
    n = len(parts)

    def body(*refs):
        src, dst = refs[:n], refs[n:2 * n]
        ssem, rsem = refs[2 * n:]
        x, y, c, _ = _place()
        cps = []
        for w in range(n):
            for k, (fx, fy) in enumerate(_CHIP_FLIPS):
                px, py = _flip(x, fx), _flip(y, fy)
                cp = _remote(src[w].at[2 * px + py], dst[w].at[k], ssem.at[w * 3 + k], rsem.at[w * 3 + k], (px, py, c))
                cp.start()
                cps.append(cp)
        for cp in cps:
            cp.wait()

    outs = [jax.ShapeDtypeStruct((3,) + p.shape[1:], p.dtype) for p in parts]
    return _comm_call("rs_chip_scatter", body, parts, outs, 3 * n)


def _pair_swap(halves):
    n = len(halves)

    def body(*refs):
        src, dst = refs[:n], refs[n:2 * n]
        ssem, rsem = refs[2 * n:]
        x, y, c, _ = _place()
        cps = []
        for w in range(n):
            cp = _remote(src[w], dst[w], ssem.at[w], rsem.at[w], (x, y, 1 - c))
            cp.start()
            cps.append(cp)
        for cp in cps:
            cp.wait()

    outs = [jax.ShapeDtypeStruct(h.shape, h.dtype) for h in halves]
    return _comm_call("rs_pair_swap", body, halves, outs, n)


def _gather_small(vec):
    def body(src, dst, ssem, rsem):
        x, y, c, _ = _place()
        me = 4 * x + 2 * y + c
        cps = []
        for r in range(1, 8):
            dev = (_flip(x, r & 4), _flip(y, r & 2), _flip(c, r & 1))
            cp = _remote(src, dst.at[me], ssem.at[r - 1], rsem.at[r - 1], dev)
            cp.start()
            cps.append(cp)
        for cp in cps:
            cp.wait()

    out = jax.ShapeDtypeStruct((8,) + vec.shape, vec.dtype)
    return _comm_call("gather_small", body, [vec], [out], 7)[0]


def _sum_slots(name, stacked, first=None):
    n, r, c = stacked.shape
    tr = _tile(r, 256, 8)

    def body(*refs):
        s_ref, o_ref = refs[-2], refs[-1]
        acc = refs[0][...].astype(F32) if first is not None else s_ref[0].astype(F32)
        for s in range(0 if first is not None else 1, n):
            acc = acc + s_ref[s].astype(F32)
        o_ref[...] = acc

    row_spec = pl.BlockSpec((tr, c), lambda i: (i, 0))
    in_specs = ([row_spec] if first is not None else []) + [pl.BlockSpec((n, tr, c), lambda i: (0, i, 0))]
    args = ([first] if first is not None else []) + [stacked]
    return pl.pallas_call(
        body, name=name, grid=(r // tr,), in_specs=in_specs, out_specs=row_spec,
        out_shape=jax.ShapeDtypeStruct((r, c), F32), compiler_params=_params(("parallel",)),
    )(*args)


def _adamw(name, w, g, m, v):
    r, c = w.shape
    tr = _tile(r, 256, 8)
    bc1, bc2 = 1.0 - ADAM_B1 ** ADAM_STEP, 1.0 - ADAM_B2 ** ADAM_STEP

    def body(w_ref, g_ref, m_ref, v_ref, d_ref, nm_ref, nv_ref):
        g_v = g_ref[...]
        nm = ADAM_B1 * m_ref[...] + (1.0 - ADAM_B1) * g_v
        nv = ADAM_B2 * v_ref[...] + (1.0 - ADAM_B2) * (g_v * g_v)
        d_ref[...] = -ADAM_LR * ((nm / bc1) / (jnp.sqrt(nv / bc2) + ADAM_EPS) + ADAM_WD * w_ref[...])
        nm_ref[...] = nm
        nv_ref[...] = nv

    spec = pl.BlockSpec((tr, c), lambda i: (i, 0))
    return pl.pallas_call(
        body, name=name, grid=(r // tr,), in_specs=[spec] * 4, out_specs=[spec] * 3,
        out_shape=[jax.ShapeDtypeStruct((r, c), F32)] * 3, compiler_params=_params(("parallel",)),
    )(w, g, m, v)


def _pad_cols(a, cols):
    return jnp.pad(a, ((0, 0), (0, cols - a.shape[1])))


def _rot_cols(w):
    h = w.shape[-1] // 2
    return jnp.concatenate([-w[..., h:], w[..., :h]], axis=-1)


def _unrot_cols(d):
    h = d.shape[-1] // 2
    return jnp.concatenate([d[..., h:], -d[..., :h]], axis=-1)


def _logical(g):
    return jnp.transpose(g, (1, 0, 2)).reshape(g.shape[1], N_CHIPS * g.shape[2])


def _chunks(a, n):
    return jnp.transpose(a.reshape(a.shape[0], N_CHIPS, n), (1, 0, 2))


def kernel(x, positions, pre_mix_norm, w_in, q_a_norm, w_uq, kv_a_norm, w_ukv, b_forget, b_gate, w_branch_mla, w_branch_fox, w_out, post_mix_norm, pre_ffn_norm, w_up, conv_w, conv_b, w_down, post_ffn_norm, loss_target, m_pre_mix_norm, m_w_in, m_q_a_norm, m_w_uq, m_kv_a_norm, m_w_ukv, m_b_forget, m_b_gate, m_w_branch_mla, m_w_branch_fox, m_w_out, m_post_mix_norm, m_pre_ffn_norm, m_w_up, m_conv_w, m_conv_b, m_w_down, m_post_ffn_norm, v_pre_mix_norm, v_w_in, v_q_a_norm, v_w_uq, v_kv_a_norm, v_w_ukv, v_b_forget, v_b_gate, v_w_branch_mla, v_w_branch_fox, v_w_out, v_post_mix_norm, v_pre_ffn_norm, v_w_up, v_conv_w, v_conv_b, v_w_down, v_post_ffn_norm):
    n_seq, seq, d = x.shape
    n_tok = n_seq * seq
    d_in = N_CHIPS * w_in.shape[1]
    two_f = N_CHIPS * w_up.shape[1]
    ff_dim = two_f // 2
    assert d_in == QL + KVL + ROPE + 3 * HEADS * FDIM + HEADS + 2 * d
    n_in_shard = w_in.shape[1]
    in_pad = -(-n_in_shard // LANES) * LANES
    hd = HEADS * LANES
    xc, yc, cc = lax.axis_index("x"), lax.axis_index("y"), lax.axis_index("c")
    chip = 2 * xc + yc
    t_attn = _tile(seq, 256, 128)

    shards = [_pad_cols(w_in, in_pad).astype(BF16), w_uq.astype(BF16), w_ukv.astype(BF16), w_branch_mla.astype(BF16),
              w_branch_fox.astype(BF16), w_out.astype(BF16), w_up.astype(BF16), w_down.astype(BF16)]
    cw8 = jnp.pad(conv_w, ((0, 5), (0, 0)))
    gathered, (g_cw,) = _all_gather_weights(shards, [cw8])
    put_own = lambda g, s: lax.dynamic_update_slice(g, s[None], (chip, 0, 0))
    g_in, g_uq, g_ukv, g_bm, g_bf, g_out, g_up, g_down = [put_own(g, s) for g, s in zip(gathered, shards)]
    g_cw = put_own(g_cw, cw8)

    win_l = jnp.transpose(g_in[:, :, :n_in_shard], (1, 0, 2)).reshape(d, d_in)
    o_q, o_kv, o_kpe = 0, QL, QL + KVL
    o_f = o_kpe + ROPE
    o_fl = o_f + 3 * hd
    o_g = o_fl + HEADS
    w_kpe = win_l[:, o_kpe:o_f]
    zeros = lambda n: jnp.zeros((d, n), BF16)
    win_p = jnp.concatenate([
        win_l[:, o_g:], win_l[:, o_q:o_kpe], w_kpe, zeros(LANES - ROPE),
        _rot_cols(w_kpe), win_l[:, o_fl:o_g], zeros(LANES - ROPE - HEADS), win_l[:, o_f:o_fl]], axis=1)
    n_p = win_p.shape[1]
    cb_gm, cb_gf = 0, 1
    c_lat = 2 * d
    c_kx, c_kr = c_lat + QL + KVL, c_lat + QL + KVL + LANES
    n_pa = c_kr + LANES
    assert n_p == n_pa + 3 * hd

    uq3 = _logical(g_uq).reshape(QL, HEADS, NOPE + ROPE)
    pe = uq3[:, :, NOPE:]
    pad_pe = lambda a: jnp.pad(a, ((0, 0), (0, 0), (0, LANES - ROPE))).reshape(QL, hd)
    wuq_p = jnp.concatenate([uq3[:, :, :NOPE].reshape(QL, hd), pad_pe(pe), pad_pe(_rot_cols(pe))], axis=1)
    ukv3 = _logical(g_ukv).reshape(KVL, HEADS, NOPE + VDIM)
    wukv_p = jnp.concatenate([ukv3[:, :, :NOPE].reshape(KVL, hd), ukv3[:, :, NOPE:].reshape(KVL, hd)], axis=1)

    n_bm, n_up = w_branch_mla.shape[1], w_up.shape[1]
    l_bm, l_up = _Chunked(n_bm), _Chunked(n_up)
    wt = n_up // 2
    n_ut = two_f // wt
    il = lambda cblk: jnp.where(cblk < n_ut // 2, 2 * cblk, 2 * (cblk - n_ut // 2) + 1)
    l_il = _Plain(il)
    to_il = lambda a: a.reshape(a.shape[0], 2, n_ut // 2, wt).transpose(0, 2, 1, 3).reshape(a.shape[0], two_f)
    from_il = lambda a: a.reshape(a.shape[0], n_ut // 2, 2, wt).transpose(0, 2, 1, 3).reshape(a.shape[0], two_f)
    w_down_full = g_down.reshape(ff_dim, d)
    w_out_full = g_out.reshape(d, d)

    inv_freq = 1.0 / (ROPE_THETA ** (jnp.arange(0, ROPE, 2, dtype=F32) / ROPE))
    ang = positions.astype(F32).reshape(n_tok, 1) * inv_freq
    cos, sin = jnp.cos(ang), jnp.sin(ang)
    cs = _pad_cols(jnp.concatenate([cos, cos], axis=1), LANES)
    sn = _pad_cols(jnp.concatenate([sin, sin], axis=1), LANES)

    row = lambda v: v.reshape(1, -1)
    x2 = x.reshape(n_tok, d)
    tgt = loss_target.reshape(n_tok, d)

    (h,) = _rows("rms_pre_mix", lambda r, v: ([r[0] * _rstd(r[0]) * v[0]], []),
                 [(x2, d, 0)], [row(pre_mix_norm)], [(d, BF16)], [], n_tok)
    proj = _mm("proj_in", "nn", h, win_p, n_tok, n_pa, d)
    tn_f = _tile(3 * hd, 1024, 128)
    assert n_pa % tn_f == 0
    proj_f = _mm("proj_in_fox", "nn", h, win_p, n_tok, 3 * hd, d, tn=tn_f, lb=_Plain(lambda cblk: cblk + n_pa // tn_f),
                 out_dtype=BF16)

    bf_vec = jnp.pad(row(b_forget), ((0, 0), (ROPE, LANES - ROPE - HEADS)))

    def lat_fwd(r, v):
        ql, kvl = r[0], r[1]
        return [ql * _rstd(ql) * v[0], kvl * _rstd(kvl) * v[1], r[2] * r[4] + r[3] * r[5]], []

    qn, kvn, rk = _rows("latent_norms", lat_fwd,
                        [(proj, QL, c_lat // QL), (proj, KVL, (c_lat + QL) // KVL), (proj, LANES, c_kx // LANES),
                         (proj, LANES, c_kr // LANES), (cs, LANES, 0), (sn, LANES, 0)],
                        [row(q_a_norm), row(kv_a_norm)], [(QL, BF16), (KVL, BF16), (LANES, BF16)], [], n_tok)
    q_p = _mm("q_up", "nn", qn, wuq_p, n_tok, 3 * hd, QL)
    kv_p = _mm("kv_up", "nn", kvn, wukv_p, n_tok, 2 * hd, KVL, out_dtype=BF16)

    def rope_q(r, v):
        c8, s8 = jnp.tile(r[3], (1, HEADS)), jnp.tile(r[4], (1, HEADS))
        return [r[0], r[1] * c8 + r[2] * s8], []

    q_nope, rq = _rows("rope_q", rope_q, [(q_p, hd, 0), (q_p, hd, 1), (q_p, hd, 2), (cs, LANES, 0), (sn, LANES, 0)], [],
                       [(hd, BF16), (hd, BF16)], [], n_tok)

    mla_q = [(q_nope, 0, False), (rq, 0, False)]
    mla_k = [(kv_p, 0, False), (rk, 0, True)]
    mla_v = (kv_p, 1)
    mla_scale = (NOPE + ROPE) ** -0.5
    o_mla, lse_mla = _attn_fwd("mla_fwd", mla_q, mla_k, mla_v, None, CHUNK, mla_scale, n_seq, seq, t_attn)

    c_run = _seq_cumsum("forget_cumsum", proj, c_kr // LANES, n_seq, seq, False,
                        pre=lambda z, b: _log_sigmoid(z + b), vec=bf_vec)
    nb_attn = seq // t_attn
    c_rowf = jnp.transpose(c_run[:, ROPE:ROPE + HEADS].reshape(n_seq, nb_attn, t_attn, HEADS), (0, 1, 3, 2))
    fox_q, fox_k, fox_v = [(proj_f, 0, False)], [(proj_f, 1, False)], (proj_f, 2)
    fox_scale = FDIM ** -0.5
    fox_bias = (c_run, c_rowf)
    o_fox, lse_fox = _attn_fwd("fox_fwd", fox_q, fox_k, fox_v, fox_bias, 1, fox_scale, n_seq, seq, t_attn)

    pm = _mm("branch_mla", "nn", o_mla, g_bm, n_tok, d, hd, lb=l_bm, tn=n_bm)
    pf = _mm("branch_fox", "nn", o_fox, g_bf, n_tok, d, hd, lb=l_bm, tn=n_bm)
    bg = row(b_gate)

    def merge(r, v):
        return [_sigmoid(r[0] + v[0]) * r[2] + _sigmoid(r[1] + v[1]) * r[3]], []

    (merged,) = _rows("gate_merge", merge, [(proj, d, cb_gm), (proj, d, cb_gf), (pm, d, 0), (pf, d, 0)],
                      [bg[:, :d], bg[:, d:]], [(d, BF16)], [], n_tok)
    y1 = _mm("mix_out", "nn", merged, w_out_full, n_tok, d, d)

    def resid_norm(r, v):
        x1v = r[0] + r[1] * _rstd(r[1]) * v[0]
        return [x1v, x1v * _rstd(x1v) * v[1]], []

    x1, h2 = _rows("post_mix_pre_ffn", resid_norm, [(x2, d, 0), (y1, d, 0)], [row(post_mix_norm), row(pre_ffn_norm)],
                   [(d, F32), (d, BF16)], [], n_tok)

    u_il = _mm("ffn_up", "nn", h2, g_up, n_tok, two_f, d, lb=l_up, lo=l_il, tn=wt)
    cw_il = to_il(_logical(g_cw)[:3])
    cb_il = to_il(row(conv_b))
    act = _conv_glu_fwd(u_il, cw_il, cb_il, n_seq, seq, wt)
    ff = _mm("ffn_down", "nn", act, w_down_full, n_tok, d, ff_dim)

    def final(r, v):
        x1v, ffv, tg = r
        diff = x1v + ffv * _rstd(ffv) * v[0] - tg
        dx2v = diff / d
        dffv, dg4 = _rms_bwd(ffv, v[0], dx2v)
        sq = jnp.sum(jnp.sum(diff * diff, axis=1, keepdims=True), axis=0, keepdims=True)
        return [dx2v, dffv], [dg4, jnp.broadcast_to(sq, (1, LANES))]

    dx2, dff, dg_post_ffn, sq_sum = _rows("loss_post_ffn_bwd", final, [(x1, d, 0), (ff, d, 0), (tgt, d, 0)],
                                          [row(post_ffn_norm)], [(d, F32), (d, BF16)], [(1, d), (1, LANES)], n_tok)
    dact = _mm("ffn_down_dx", "nt", dff, w_down_full, n_tok, ff_dim, d)
    gw_down = _mm("ffn_down_dw", "tn", act, dff, ff_dim, d, n_tok, out_dtype=BF16)
    d_il, conv_acc = _conv_glu_bwd_pre(u_il, dact, cw_il, cb_il, n_seq, seq, wt)
    du_il = _conv_bwd_input(d_il, cw_il, n_seq, seq, wt)
    dh2 = _mm("ffn_up_dx", "nt", du_il, g_up, n_tok, d, two_f, la=l_il, lb=l_up, tk=wt)
    gw_up = _mm("ffn_up_dw", "tn", h2, du_il, d, two_f, n_tok, lb=l_il, lo=l_up, tn=wt, out_dtype=BF16)

    def mid_bwd(r, v):
        x1v, y1v, dx2v, dh2v = r
        d3, dg3 = _rms_bwd(x1v, v[1], dh2v)
        dx1v = dx2v + d3
        dy1v, dg2 = _rms_bwd(y1v, v[0], dx1v)
        return [dx1v, dy1v], [dg3, dg2]

    dx1, dy1, dg_pre_ffn, dg_post_mix = _rows(
        "pre_ffn_post_mix_bwd", mid_bwd, [(x1, d, 0), (y1, d, 0), (dx2, d, 0), (dh2, d, 0)],
        [row(post_mix_norm), row(pre_ffn_norm)], [(d, F32), (d, BF16)], [(1, d), (1, d)], n_tok)
    dmerged = _mm("mix_out_dx", "nt", dy1, w_out_full, n_tok, d, d)
    gw_out = _mm("mix_out_dw", "tn", merged, dy1, d, d, n_tok, out_dtype=BF16)

    def gate_bwd(r, v):
        zm, zf, pmv, pfv, dm = r
        gm, gf = _sigmoid(zm + v[0]), _sigmoid(zf + v[1])
        dzm, dzf = dm * pmv * gm * (1.0 - gm), dm * pfv * gf * (1.0 - gf)
        return [dm * gm, dm * gf, dzm, dzf], [_colsum(dzm), _colsum(dzf)]

    dpm, dpf, dzm, dzf, dbg_m, dbg_f = _rows(
        "gate_merge_bwd", gate_bwd, [(proj, d, cb_gm), (proj, d, cb_gf), (pm, d, 0), (pf, d, 0), (dmerged, d, 0)],
        [bg[:, :d], bg[:, d:]], [(d, BF16)] * 4, [(1, d), (1, d)], n_tok)
    tk_b = min(n_bm, 512)
    do_mla = _mm("branch_mla_dx", "nt", dpm, g_bm, n_tok, hd, d, lb=l_bm, tk=tk_b)
    do_fox = _mm("branch_fox_dx", "nt", dpf, g_bf, n_tok, hd, d, lb=l_bm, tk=tk_b)
    gw_bm = _mm("branch_mla_dw", "tn", o_mla, dpm, hd, d, n_tok, lo=l_bm, tn=n_bm, out_dtype=BF16)
    gw_bf = _mm("branch_fox_dw", "tn", o_fox, dpf, hd, d, n_tok, lo=l_bm, tn=n_bm, out_dtype=BF16)

    dq_nope, drq, delta_mla, dob_mla = _attn_bwd_dq("mla_bwd_dq", mla_q, mla_k, mla_v, o_mla, do_mla, lse_mla, None, CHUNK,
                                                    mla_scale, n_seq, seq, t_attn)
    dk_nope, drk, dv_mla = _attn_bwd_dkv("mla_bwd_dkv", mla_q, mla_k, mla_v, dob_mla, lse_mla, delta_mla, None, CHUNK,
                                         mla_scale, n_seq, seq, t_attn)
    dfq, delta_fox, dob_fox, dc_q = _attn_bwd_dq("fox_bwd_dq", fox_q, fox_k, fox_v, o_fox, do_fox, lse_fox, fox_bias, 1,
                                                 fox_scale, n_seq, seq, t_attn)
    dfk, dfv, dc_k = _attn_bwd_dkv("fox_bwd_dkv", fox_q, fox_k, fox_v, dob_fox, lse_fox, delta_fox, fox_bias, 1, fox_scale,
                                   n_seq, seq, t_attn)
    dc_k8 = jnp.transpose(dc_k, (0, 1, 3, 2)).reshape(n_tok, HEADS)
    dc128 = dc_q + jnp.pad(dc_k8, ((0, 0), (ROPE, LANES - ROPE - HEADS)))
    dlogf = _seq_cumsum("forget_cumsum_bwd", dc128, 0, n_seq, seq, True)

    def mla_pack(r, v):
        dqn_v, drq_v, dkn_v, dv_v, drk_v, c1, s1 = r
        c8, s8 = jnp.tile(c1, (1, HEADS)), jnp.tile(s1, (1, HEADS))
        return [jnp.concatenate([dqn_v, drq_v * c8, drq_v * s8], axis=1), jnp.concatenate([dkn_v, dv_v], axis=1),
                drk_v * c1, drk_v * s1], []

    dq_p, dkv_p, dkx, dkr = _rows(
        "mla_rope_bwd", mla_pack,
        [(dq_nope, hd, 0), (drq, hd, 0), (dk_nope, hd, 0), (dv_mla, hd, 0), (drk, LANES, 0), (cs, LANES, 0), (sn, LANES, 0)],
        [], [(3 * hd, BF16), (2 * hd, BF16), (LANES, F32), (LANES, F32)], [], n_tok)
    dqn = _mm("q_up_dx", "nt", dq_p, wuq_p, n_tok, QL, 3 * hd)
    gw_uq_p = _mm("q_up_dw", "tn", qn, dq_p, QL, 3 * hd, n_tok, out_dtype=BF16)
    dkvn = _mm("kv_up_dx", "nt", dkv_p, wukv_p, n_tok, KVL, 2 * hd)
    gw_ukv_p = _mm("kv_up_dw", "tn", kvn, dkv_p, KVL, 2 * hd, n_tok, out_dtype=BF16)

    def lat_bwd(r, v):
        ql, kvl, dqn_v, dkvn_v, dkx_v, dkr_v, zblk, dlf = r
        dql, dgq = _rms_bwd(ql, v[0], dqn_v)
        dkvl, dgkv = _rms_bwd(kvl, v[1], dkvn_v)
        dfl = dlf * _sigmoid(-(zblk + v[2]))
        return [jnp.concatenate([dql, dkvl, dkx_v, dkr_v + dfl], axis=1)], [dgq, dgkv, _colsum(dfl)]

    dlat, dg_q, dg_kv, dbf = _rows(
        "latent_bwd", lat_bwd,
        [(proj, QL, c_lat // QL), (proj, KVL, (c_lat + QL) // KVL), (dqn, QL, 0), (dkvn, KVL, 0), (dkx, LANES, 0),
         (dkr, LANES, 0), (proj, LANES, c_kr // LANES), (dlogf, LANES, 0)],
        [row(q_a_norm), row(kv_a_norm), bf_vec], [(QL + KVL + 2 * LANES, BF16)], [(1, QL), (1, KVL), (1, LANES)], n_tok)
    dproj = jnp.concatenate([dzm, dzf, dlat, dfq.astype(BF16), dfk.astype(BF16), dfv.astype(BF16)], axis=1)
    dh = _mm("proj_in_dx", "nt", dproj, win_p, n_tok, d, n_p)
    gw_in_p = _mm("proj_in_dw", "tn", h, dproj, d, n_p, n_tok, out_dtype=BF16)

    def first_bwd(r, v):
        dxa, dg1 = _rms_bwd(r[0], v[0], r[1])
        return [r[2] + dxa], [dg1]

    grad_x, dg_pre_mix = _rows("pre_mix_bwd", first_bwd, [(x2, d, 0), (dh, d, 0), (dx1, d, 0)], [row(pre_mix_norm)],
                               [(d, F32)], [(1, d)], n_tok)

    f32 = lambda a: a.astype(F32)
    kr_blk = gw_in_p[:, c_kr:c_kr + LANES]
    d_kpe = (f32(gw_in_p[:, c_kx:c_kx + ROPE]) + _unrot_cols(f32(kr_blk[:, :ROPE]))).astype(BF16)
    gw_in_l = jnp.concatenate([gw_in_p[:, c_lat:c_lat + QL + KVL], d_kpe, gw_in_p[:, n_pa:],
                               kr_blk[:, ROPE:ROPE + HEADS], gw_in_p[:, :2 * d]], axis=1)
    gc_in = jnp.pad(_chunks(gw_in_l, n_in_shard), ((0, 0), (0, 0), (0, in_pad - n_in_shard)))
    uq_parts = [gw_uq_p[:, i * hd:(i + 1) * hd].reshape(QL, HEADS, LANES) for i in range(3)]
    d_pe = (f32(uq_parts[1][:, :, :ROPE]) + _unrot_cols(f32(uq_parts[2][:, :, :ROPE]))).astype(BF16)
    gc_uq = _chunks(jnp.concatenate([uq_parts[0], d_pe], axis=2).reshape(QL, HEADS * (NOPE + ROPE)), w_uq.shape[1])
    gc_ukv = _chunks(jnp.concatenate([gw_ukv_p[:, :hd].reshape(KVL, HEADS, NOPE), gw_ukv_p[:, hd:].reshape(KVL, HEADS, VDIM)],
                                     axis=2).reshape(KVL, HEADS * (NOPE + VDIM)), w_ukv.shape[1])
    grads = [gc_in, gc_uq, gc_ukv, gw_bm, gw_bf, gw_out.reshape(N_CHIPS, d // N_CHIPS, d), gw_up,
             gw_down.reshape(N_CHIPS, ff_dim // N_CHIPS, d)]

    big = ["w_in", "w_uq", "w_ukv", "w_branch_mla", "w_branch_fox", "w_out", "w_up", "w_down"]
    theirs = _pair_split(grads)
    parts = []
    for nm, g, b in zip(big, grads, theirs):
        a = lax.dynamic_slice_in_dim(g, cc * b.shape[1], b.shape[1], axis=1)
        r2, c2 = a.shape[0] * a.shape[1], a.shape[2]
        (p,) = _rows("rs_pair_add_" + nm, lambda r, v: ([r[0].astype(F32) + r[1].astype(F32)], []),
                     [(a.reshape(r2, c2), c2, 0), (b.reshape(r2, c2), c2, 0)], [], [(c2, BF16)], [], r2)
        parts.append(p.reshape(a.shape))
    landed = _chip_scatter(parts)
    halves = [_sum_slots("rs_chip_sum_" + nm, s, first=lax.dynamic_index_in_dim(p, chip, 0, keepdims=False))
              for nm, s, p in zip(big, landed, parts)]
    other = _pair_swap(halves)
    full = [jnp.concatenate([jnp.where(cc == 0, a, b), jnp.where(cc == 0, b, a)], axis=0) for a, b in zip(halves, other)]
    g_big = dict(zip(big, full))
    g_big["w_in"] = g_big["w_in"][:, :n_in_shard]

    conv_acc_l = from_il(conv_acc)
    pieces = [dg_pre_mix, dg_q, dg_kv, dbf, dbg_m, dbg_f, dg_post_mix, dg_pre_ffn, conv_acc_l[3:4], dg_post_ffn,
              conv_acc_l[0:1], conv_acc_l[1:2], conv_acc_l[2:3], sq_sum]
    sizes = [p.shape[1] for p in pieces]
    flat = jnp.concatenate(pieces, axis=1)
    n_rows = -(-flat.shape[1] // (8 * LANES)) * 8
    flat = _pad_cols(flat, n_rows * LANES).reshape(n_rows, LANES)
    slots = lax.dynamic_update_slice(_gather_small(flat), flat[None], (2 * chip + cc, 0, 0))
    total = _sum_slots("small_sum", slots).reshape(1, n_rows * LANES)
    offs = [sum(sizes[:i]) for i in range(len(sizes))]
    tot = [total[0, o:o + s] for o, s in zip(offs, sizes)]
    loss = 0.5 * tot[13][0] / d
    g_small = {"pre_mix_norm": tot[0], "q_a_norm": tot[1], "kv_a_norm": tot[2], "b_forget": tot[3][ROPE:ROPE + HEADS],
               "b_gate": jnp.concatenate([tot[4], tot[5]]), "post_mix_norm": tot[6], "pre_ffn_norm": tot[7],
               "conv_b": tot[8], "post_ffn_norm": tot[9]}
    gcw_full = jnp.stack([tot[10], tot[11], tot[12]])
    g_conv_w = lax.dynamic_slice(gcw_full, (0, chip * n_up), (3, n_up))

    given = dict(pre_mix_norm=(pre_mix_norm, m_pre_mix_norm, v_pre_mix_norm), w_in=(w_in, m_w_in, v_w_in),
                 q_a_norm=(q_a_norm, m_q_a_norm, v_q_a_norm), w_uq=(w_uq, m_w_uq, v_w_uq),
                 kv_a_norm=(kv_a_norm, m_kv_a_norm, v_kv_a_norm), w_ukv=(w_ukv, m_w_ukv, v_w_ukv),
                 b_forget=(b_forget, m_b_forget, v_b_forget), b_gate=(b_gate, m_b_gate, v_b_gate),
                 w_branch_mla=(w_branch_mla, m_w_branch_mla, v_w_branch_mla),
                 w_branch_fox=(w_branch_fox, m_w_branch_fox, v_w_branch_fox), w_out=(w_out, m_w_out, v_w_out),
                 post_mix_norm=(post_mix_norm, m_post_mix_norm, v_post_mix_norm),
                 pre_ffn_norm=(pre_ffn_norm, m_pre_ffn_norm, v_pre_ffn_norm), w_up=(w_up, m_w_up, v_w_up),
                 conv_w=(conv_w, m_conv_w, v_conv_w), conv_b=(conv_b, m_conv_b, v_conv_b),
                 w_down=(w_down, m_w_down, v_w_down), post_ffn_norm=(post_ffn_norm, m_post_ffn_norm, v_post_ffn_norm))
    order = list(given)
    grad, delta, new_m, new_v = {}, {}, {}, {}
    for nm in big + ["conv_w"]:
        grad[nm] = g_big[nm] if nm in g_big else g_conv_w
        delta[nm], new_m[nm], new_v[nm] = _adamw("adamw_" + nm, given[nm][0], grad[nm], given[nm][1], given[nm][2])
    small = list(g_small)
    padded = [-(-g_small[nm].shape[0] // LANES) * LANES for nm in small]
    s_rows = -(-sum(padded) // (8 * LANES)) * 8

    def pack(vals):
        cat = jnp.concatenate([jnp.pad(a, (0, p - a.shape[0])) for a, p in zip(vals, padded)])
        return jnp.pad(cat, (0, s_rows * LANES - cat.shape[0])).reshape(s_rows, LANES)

    packed = _adamw("adamw_small", pack([given[nm][0] for nm in small]), pack([g_small[nm] for nm in small]),
                    pack([given[nm][1] for nm in small]), pack([given[nm][2] for nm in small]))
    s_offs = [sum(padded[:i]) for i in range(len(small))]
    for nm, o in zip(small, s_offs):
        n_el = g_small[nm].shape[0]
        grad[nm] = g_small[nm]
        delta[nm], new_m[nm], new_v[nm] = [p.reshape(-1)[o:o + n_el] for p in packed]
    return (loss, grad_x.reshape(n_seq, seq, d), *[grad[nm] for nm in order], *[delta[nm] for nm in order],
            *[new_m[nm] for nm in order], *[new_v[nm] for nm in order])
```

```python
import functools
import math

import jax
import jax.numpy as jnp
from jax import lax
from jax.experimental import pallas as pl
from jax.experimental.pallas import tpu as pltpu

F32, BF16 = jnp.float32, jnp.bfloat16
MESH = pl.DeviceIdType.MESH

HEADS = 8
NOPE, ROPE, VDIM = 128, 64, 128
QL, KVL = 512, 256
FDIM = 128
CHUNK = 64
ROPE_THETA = 10000.0
EPS = 1e-6
NEG_INF = -1e30
ADAM_LR, ADAM_B1, ADAM_B2, ADAM_EPS, ADAM_WD, ADAM_STEP = 0.001, 0.9, 0.999, 1e-08, 0.01, 10

VMEM_LIMIT_BYTES = 52 * 1024 * 1024
LANES = 128
N_CHIPS = 4


def _params(sem):
    return pltpu.CompilerParams(dimension_semantics=sem, vmem_limit_bytes=VMEM_LIMIT_BYTES)


def _tile(n, target, mult):
    if n <= target:
        return n
    t = (target // mult) * mult
    while t >= mult:
        if n % t == 0:
            return t
        t -= mult
    raise ValueError(f"no tile for {n} (target {target}, multiple of {mult})")


class _Plain:
    def __init__(self, perm=None):
        self.perm = perm

    def spec(self, tr, tc, rc):
        perm = self.perm

        def imap(i, j, k):
            r, c = rc(i, j, k)
            return (r, perm(c) if perm is not None else c)

        return pl.BlockSpec((tr, tc), imap)

    def shape(self, rows, cols):
        return (rows, cols)


class _Chunked:
    def __init__(self, n):
        self.n = n

    def spec(self, tr, tc, rc):
        assert self.n % tc == 0, (self.n, tc)
        per = self.n // tc

        def imap(i, j, k):
            r, c = rc(i, j, k)
            return (c // per, r, c % per)

        return pl.BlockSpec((None, tr, tc), imap)

    def shape(self, rows, cols):
        assert cols == N_CHIPS * self.n
        return (N_CHIPS, rows, self.n)


_DIMS = {"nn": (((1,), (0,)), ((), ())), "nt": (((1,), (1,)), ((), ())), "tn": (((0,), (0,)), ((), ()))}


def _mm_single(name, mode, a, b, m, n, k, tm, tn, la, lb, lo, out_dtype):
    if mode == "nn":
        a_spec = la.spec(tm, k, lambda i, j, kk: (i, 0))
        b_spec = lb.spec(k, tn, lambda i, j, kk: (0, j))
    elif mode == "nt":
        a_spec = la.spec(tm, k, lambda i, j, kk: (i, 0))
        b_spec = lb.spec(tn, k, lambda i, j, kk: (j, 0))
    else:
        a_spec = la.spec(k, tm, lambda i, j, kk: (0, i))
        b_spec = lb.spec(k, tn, lambda i, j, kk: (0, j))
    o_spec = lo.spec(tm, tn, lambda i, j, kk: (i, j))
    dims = _DIMS[mode]

    def body(a_ref, b_ref, o_ref):
        o_ref[...] = lax.dot_general(a_ref[...].astype(BF16), b_ref[...].astype(BF16), dims,
                                     preferred_element_type=F32).astype(o_ref.dtype)

    return pl.pallas_call(
        body, name=name, grid=(m // tm, n // tn, 1),
        in_specs=[a_spec, b_spec], out_specs=o_spec,
        out_shape=jax.ShapeDtypeStruct(lo.shape(m, n), out_dtype),
        compiler_params=_params(("parallel", "parallel", "arbitrary")),
    )(a, b)


def _mm(name, mode, a, b, m, n, k, *, tm=1024, tn=1024, tk=2048, la=None, lb=None, lo=None, out_dtype=F32):
    la, lb, lo = la or _Plain(), lb or _Plain(), lo or _Plain()
    tm, tn, tk = _tile(m, tm, 128), _tile(n, tn, 128), _tile(k, tk, 128)
    nk = k // tk
    if nk == 1:
        return _mm_single(name, mode, a, b, m, n, k, tm, tn, la, lb, lo, out_dtype)
    if mode == "nn":
        a_spec = la.spec(tm, tk, lambda i, j, kk: (i, kk))
        b_spec = lb.spec(tk, tn, lambda i, j, kk: (kk, j))
    elif mode == "nt":
        a_spec = la.spec(tm, tk, lambda i, j, kk: (i, kk))
        b_spec = lb.spec(tn, tk, lambda i, j, kk: (j, kk))
    else:
        a_spec = la.spec(tk, tm, lambda i, j, kk: (kk, i))
        b_spec = lb.spec(tk, tn, lambda i, j, kk: (kk, j))
    o_spec = lo.spec(tm, tn, lambda i, j, kk: (i, j))
    dims = _DIMS[mode]

    def body(a_ref, b_ref, o_ref, acc_ref):
        kk = pl.program_id(2)

        @pl.when(kk == 0)
        def _():
            acc_ref[...] = jnp.zeros_like(acc_ref)

        acc_ref[...] += lax.dot_general(a_ref[...].astype(BF16), b_ref[...].astype(BF16), dims,
                                        preferred_element_type=F32)

        @pl.when(kk == nk - 1)
        def _():
            o_ref[...] = acc_ref[...].astype(o_ref.dtype)

    return pl.pallas_call(
        body, name=name, grid=(m // tm, n // tn, nk),
        in_specs=[a_spec, b_spec], out_specs=o_spec,
        out_shape=jax.ShapeDtypeStruct(lo.shape(m, n), out_dtype),
        scratch_shapes=[pltpu.VMEM((tm, tn), F32)],
        compiler_params=_params(("parallel", "parallel", "arbitrary")),
    )(a, b)


def _rows(name, fn, rows_in, vecs_in, rows_out, accs_out, n_rows, tr=256):
    tr = _tile(n_rows, tr, 16)
    nr, nv, no = len(rows_in), len(vecs_in), len(rows_out)

    def body(*refs):
        ins, vecs = refs[:nr], refs[nr:nr + nv]
        outs, accs = refs[nr + nv:nr + nv + no], refs[nr + nv + no:]
        ro, ac = fn([r[...] for r in ins], [v[...] for v in vecs])
        for o_ref, val in zip(outs, ro):
            o_ref[...] = val.astype(o_ref.dtype)
        if accs:
            @pl.when(pl.program_id(0) == 0)
            def _():
                for a_ref in accs:
                    a_ref[...] = jnp.zeros_like(a_ref)

            for a_ref, val in zip(accs, ac):
                a_ref[...] += val

    in_specs = [pl.BlockSpec((tr, cols), functools.partial(lambda i, cb: (i, cb), cb=cb)) for _, cols, cb in rows_in]
    in_specs += [pl.BlockSpec(v.shape, lambda i: (0, 0)) for v in vecs_in]
    out_specs = [pl.BlockSpec((tr, cols), lambda i: (i, 0)) for cols, _ in rows_out]
    out_specs += [pl.BlockSpec((r, cols), lambda i: (0, 0)) for r, cols in accs_out]
    out_shape = [jax.ShapeDtypeStruct((n_rows, cols), dt) for cols, dt in rows_out]
    out_shape += [jax.ShapeDtypeStruct((r, cols), F32) for r, cols in accs_out]
    res = pl.pallas_call(
        body, name=name, grid=(n_rows // tr,), in_specs=in_specs, out_specs=out_specs, out_shape=out_shape,
        compiler_params=_params(("arbitrary",)),
    )(*[a for a, _, _ in rows_in], *vecs_in)
    return res


def _colsum(v):
    return jnp.sum(v, axis=0, keepdims=True)


def _rstd(x):
    return lax.rsqrt(jnp.mean(x * x, axis=-1, keepdims=True) + EPS)


def _rms_bwd(x, g, dy):
    r = _rstd(x)
    xh = x * r
    dxh = dy * g
    dx = r * (dxh - xh * jnp.mean(dxh * xh, axis=-1, keepdims=True))
    return dx, _colsum(dy * xh)


def _sigmoid(z):
    return 1.0 / (1.0 + jnp.exp(-z))


_GELU_K = math.sqrt(2.0 / math.pi)


def _gelu_parts(g):
    t = jnp.tanh(_GELU_K * (g + 0.044715 * g * g * g))
    gel = 0.5 * g * (1.0 + t)
    dgel = 0.5 * (1.0 + t) + 0.5 * g * (1.0 - t * t) * (_GELU_K * (1.0 + 3.0 * 0.044715 * g * g))
    return gel, dgel


def _diag_visible(t, unit):
    rows = lax.broadcasted_iota(jnp.int32, (t, t), 0)
    cols = lax.broadcasted_iota(jnp.int32, (t, t), 1)
    if unit > 1:
        sh = int(math.log2(unit))
        assert 1 << sh == unit and t % unit == 0
        rows, cols = jnp.right_shift(rows, sh), jnp.right_shift(cols, sh)
    return cols <= rows


def _lane_pick(tile, lane):
    idx = lax.broadcasted_iota(jnp.int32, tile.shape, 1)
    return jnp.sum(jnp.where(idx == lane, tile, 0.0), axis=1, keepdims=True)


def _lane_put(tile, lane, col):
    idx = lax.broadcasted_iota(jnp.int32, tile.shape, 1)
    return jnp.where(idx == lane, col, tile)


def _head_cat(refs, shared, rows, h):
    hs = slice(h * LANES, (h + 1) * LANES)
    vals = [(r[rows, :] if sh else r[rows, hs]).astype(BF16) for r, sh in zip(refs, shared)]
    return vals[0] if len(vals) == 1 else jnp.concatenate(vals, axis=1)


def _blk_rows(i, t):
    return pl.ds(pl.multiple_of(i * t, t), t)


def _piece_specs(pieces, rows, row_idx):
    return [pl.BlockSpec((rows, LANES if sh else HEADS * LANES), functools.partial(lambda b, i, cb: (row_idx(b, i), cb), cb=cb))
            for _, cb, sh in pieces]


def _attn_fwd(name, qp, kp, vp, bias, unit, scale, n_seq, seq, t):
    nb = seq // t
    n_tok = n_seq * seq
    nq, nk_p = len(qp), len(kp)
    q_sh, k_sh = [p[2] for p in qp], [p[2] for p in kp]
    nbias = 2 if bias is not None else 0

    def body(*refs):
        q_refs, k_refs = refs[:nq], refs[nq:nq + nk_p]
        v_ref = refs[nq + nk_p]
        bias_refs = refs[nq + nk_p + 1:nq + nk_p + 1 + nbias]
        o_ref, lse_ref = refs[nq + nk_p + 1 + nbias:]
        qi = pl.program_id(1)
        lse_tile = jnp.zeros((t, LANES), F32)
        for h in range(HEADS):
            hs = slice(h * LANES, (h + 1) * LANES)
            q = _head_cat(q_refs, q_sh, slice(None), h)
            cq = _lane_pick(bias_refs[0][...], ROPE + h) if bias is not None else None

            def block(kb, carry, diag, h=h, hs=hs, q=q, cq=cq):
                m, l, acc = carry
                rows = _blk_rows(kb, t)
                s = lax.dot_general(q, _head_cat(k_refs, k_sh, rows, h), _DIMS["nt"], preferred_element_type=F32) * scale
                if bias is not None:
                    s = s + cq - bias_refs[1][kb, h:h + 1, :]
                if diag:
                    s = jnp.where(_diag_visible(t, unit), s, NEG_INF)
                m_new = jnp.maximum(m, jnp.max(s, axis=1, keepdims=True))
                alpha = jnp.exp(m - m_new)
                p = jnp.exp(s - m_new)
                l = alpha * l + jnp.sum(p, axis=1, keepdims=True)
                acc = alpha * acc + jnp.dot(p.astype(BF16), v_ref[rows, hs].astype(BF16), preferred_element_type=F32)
                return m_new, l, acc

            init = (jnp.full((t, 1), NEG_INF, F32), jnp.zeros((t, 1), F32), jnp.zeros((t, LANES), F32))
            carry = lax.fori_loop(0, qi, lambda kb, c: block(kb, c, False), init)
            m, l, acc = block(qi, carry, True)
            o_ref[:, hs] = acc / l
            lse_tile = _lane_put(lse_tile, h, m + jnp.log(l))
        lse_ref[...] = lse_tile

    tile_row = lambda b, i: b * nb + i
    seq_row = lambda b, i: b
    lane_tile = pl.BlockSpec((t, LANES), lambda b, i: (b * nb + i, 0))
    in_specs = _piece_specs(qp, t, tile_row) + _piece_specs(kp, seq, seq_row) + _piece_specs([vp + (False,)], seq, seq_row)
    args = [p[0] for p in qp] + [p[0] for p in kp] + [vp[0]]
    if bias is not None:
        in_specs += [lane_tile, pl.BlockSpec((None, nb, HEADS, t), lambda b, i: (b, 0, 0, 0))]
        args += list(bias)
    return pl.pallas_call(
        body, name=name, grid=(n_seq, nb), in_specs=in_specs,
        out_specs=[pl.BlockSpec((t, HEADS * LANES), lambda b, i: (b * nb + i, 0)), lane_tile],
        out_shape=[jax.ShapeDtypeStruct((n_tok, HEADS * LANES), F32), jax.ShapeDtypeStruct((n_tok, LANES), F32)],
        compiler_params=_params(("parallel", "arbitrary")),
    )(*args)


def _attn_bwd_dq(name, qp, kp, vp, o, do, lse, bias, unit, scale, n_seq, seq, t):
    nb = seq // t
    n_tok = n_seq * seq
    nq, nk_p = len(qp), len(kp)
    q_sh, k_sh = [p[2] for p in qp], [p[2] for p in kp]
    nbias = 2 if bias is not None else 0
    n_in = nq + nk_p + 4 + nbias

    def body(*refs):
        q_refs, k_refs = refs[:nq], refs[nq:nq + nk_p]
        v_ref, o_ref, do_ref, lse_ref = refs[nq + nk_p:nq + nk_p + 4]
        bias_refs = refs[nq + nk_p + 4:n_in]
        dq_refs = refs[n_in:n_in + nq]
        delta_ref, dob_ref = refs[n_in + nq:n_in + nq + 2]
        qi = pl.program_id(1)
        delta_tile = jnp.zeros((t, LANES), F32)
        dc_tile = jnp.zeros((t, LANES), F32)
        lse_all = lse_ref[...]
        for h in range(HEADS):
            hs = slice(h * LANES, (h + 1) * LANES)
            q = _head_cat(q_refs, q_sh, slice(None), h)
            do_f = do_ref[:, hs]
            do_b = do_f.astype(BF16)
            dob_ref[:, hs] = do_b
            delta = jnp.sum(do_f * o_ref[:, hs], axis=1, keepdims=True)
            lse = _lane_pick(lse_all, h)
            cq = _lane_pick(bias_refs[0][...], ROPE + h) if bias is not None else None

            def block(kb, carry, diag, h=h, hs=hs, q=q, cq=cq, do_b=do_b, delta=delta, lse=lse):
                dq_acc, dc_acc = carry
                rows = _blk_rows(kb, t)
                k = _head_cat(k_refs, k_sh, rows, h)
                s = lax.dot_general(q, k, _DIMS["nt"], preferred_element_type=F32) * scale
                if bias is not None:
                    s = s + cq - bias_refs[1][kb, h:h + 1, :]
                if diag:
                    s = jnp.where(_diag_visible(t, unit), s, NEG_INF)
                p = jnp.exp(s - lse)
                dp = lax.dot_general(do_b, v_ref[rows, hs].astype(BF16), _DIMS["nt"], preferred_element_type=F32)
                ds = p * (dp - delta)
                return (dq_acc + jnp.dot(ds.astype(BF16), k, preferred_element_type=F32),
                        dc_acc + jnp.sum(ds, axis=1, keepdims=True))

            init = (jnp.zeros((t, nq * LANES), F32), jnp.zeros((t, 1), F32))
            carry = lax.fori_loop(0, qi, lambda kb, c: block(kb, c, False), init)
            dq_acc, dc_acc = block(qi, carry, True)
            for n_p in range(nq):
                dq_refs[n_p][:, hs] = dq_acc[:, n_p * LANES:(n_p + 1) * LANES] * scale
            delta_tile = _lane_put(delta_tile, h, delta)
            dc_tile = _lane_put(dc_tile, ROPE + h, dc_acc)
        delta_ref[...] = delta_tile
        if bias is not None:
            refs[n_in + nq + 2][...] = dc_tile

    tile_row = lambda b, i: b * nb + i
    seq_row = lambda b, i: b
    lane_tile = pl.BlockSpec((t, LANES), lambda b, i: (b * nb + i, 0))
    head_tile = pl.BlockSpec((t, HEADS * LANES), lambda b, i: (b * nb + i, 0))
    in_specs = _piece_specs(qp, t, tile_row) + _piece_specs(kp, seq, seq_row) + _piece_specs([vp + (False,)], seq, seq_row)
    in_specs += [head_tile, head_tile, lane_tile]
    args = [p[0] for p in qp] + [p[0] for p in kp] + [vp[0], o, do, lse]
    if bias is not None:
        in_specs += [lane_tile, pl.BlockSpec((None, nb, HEADS, t), lambda b, i: (b, 0, 0, 0))]
        args += list(bias)
    out_specs = [head_tile] * nq + [lane_tile, head_tile] + ([lane_tile] if bias is not None else [])
    out_shape = [jax.ShapeDtypeStruct((n_tok, HEADS * LANES), F32)] * nq
    out_shape += [jax.ShapeDtypeStruct((n_tok, LANES), F32), jax.ShapeDtypeStruct((n_tok, HEADS * LANES), BF16)]
    if bias is not None:
        out_shape.append(jax.ShapeDtypeStruct((n_tok, LANES), F32))
    return pl.pallas_call(
        body, name=name, grid=(n_seq, nb), in_specs=in_specs, out_specs=out_specs, out_shape=out_shape,
        compiler_params=_params(("parallel", "arbitrary")),
    )(*args)


def _attn_bwd_dkv(name, qp, kp, vp, dob, lse, delta, bias, unit, scale, n_seq, seq, t):
    nb = seq // t
    n_tok = n_seq * seq
    nq, nk_p = len(qp), len(kp)
    q_sh, k_sh = [p[2] for p in qp], [p[2] for p in kp]
    nbias = 2 if bias is not None else 0
    n_in = nq + nk_p + 4 + nbias

    def body(*refs):
        q_refs, k_refs = refs[:nq], refs[nq:nq + nk_p]
        v_ref, dob_ref, lse_ref, delta_ref = refs[nq + nk_p:nq + nk_p + 4]
        bias_refs = refs[nq + nk_p + 4:n_in]
        dk_refs = refs[n_in:n_in + nk_p]
        dv_ref = refs[n_in + nk_p]
        ki = pl.program_id(1)
        shared_acc = [jnp.zeros((t, LANES), F32) for _ in range(nk_p)]
        for h in range(HEADS):
            hs = slice(h * LANES, (h + 1) * LANES)
            k = _head_cat(k_refs, k_sh, slice(None), h)
            v = v_ref[:, hs].astype(BF16)
            ck = bias_refs[1][h:h + 1, :] if bias is not None else None

            def block(qb, carry, diag, h=h, hs=hs, k=k, v=v, ck=ck):
                dk_acc, dv_acc, dc_acc = carry
                rows = _blk_rows(qb, t)
                q = _head_cat(q_refs, q_sh, rows, h)
                s = lax.dot_general(q, k, _DIMS["nt"], preferred_element_type=F32) * scale
                if bias is not None:
                    s = s + _lane_pick(bias_refs[0][rows, :], ROPE + h) - ck
                if diag:
                    s = jnp.where(_diag_visible(t, unit), s, NEG_INF)
                p = jnp.exp(s - _lane_pick(lse_ref[rows, :], h))
                do_b = dob_ref[rows, hs]
                dp = lax.dot_general(do_b, v, _DIMS["nt"], preferred_element_type=F32)
                ds = p * (dp - _lane_pick(delta_ref[rows, :], h))
                return (dk_acc + lax.dot_general(ds.astype(BF16), q, _DIMS["tn"], preferred_element_type=F32),
                        dv_acc + lax.dot_general(p.astype(BF16), do_b, _DIMS["tn"], preferred_element_type=F32),
                        dc_acc - jnp.sum(ds, axis=0, keepdims=True))

            init = (jnp.zeros((t, nk_p * LANES), F32), jnp.zeros((t, LANES), F32), jnp.zeros((1, t), F32))
            carry = block(ki, init, True)
            dk_acc, dv_acc, dc_acc = lax.fori_loop(ki + 1, nb, lambda qb, c: block(qb, c, False), carry)
            for n_p in range(nk_p):
                part = dk_acc[:, n_p * LANES:(n_p + 1) * LANES] * scale
                if k_sh[n_p]:
                    shared_acc[n_p] = shared_acc[n_p] + part
                else:
                    dk_refs[n_p][:, hs] = part
            dv_ref[:, hs] = dv_acc
            if bias is not None:
                refs[n_in + nk_p + 1][h:h + 1, :] = dc_acc
        for n_p in range(nk_p):
            if k_sh[n_p]:
                dk_refs[n_p][...] = shared_acc[n_p]

    tile_row = lambda b, i: b * nb + i
    seq_row = lambda b, i: b
    lane_seq = pl.BlockSpec((seq, LANES), lambda b, i: (b, 0))
    head_tile = pl.BlockSpec((t, HEADS * LANES), lambda b, i: (b * nb + i, 0))
    row_tile = pl.BlockSpec((None, None, HEADS, t), lambda b, i: (b, i, 0, 0))
    in_specs = _piece_specs(qp, seq, seq_row) + _piece_specs(kp, t, tile_row) + _piece_specs([vp + (False,)], t, tile_row)
    in_specs += [pl.BlockSpec((seq, HEADS * LANES), lambda b, i: (b, 0)), lane_seq, lane_seq]
    args = [p[0] for p in qp] + [p[0] for p in kp] + [vp[0], dob, lse, delta]
    if bias is not None:
        in_specs += [lane_seq, row_tile]
        args += list(bias)
    out_specs = [pl.BlockSpec((t, LANES if sh else HEADS * LANES), lambda b, i: (b * nb + i, 0)) for sh in k_sh] + [head_tile]
    out_shape = [jax.ShapeDtypeStruct((n_tok, LANES if sh else HEADS * LANES), F32) for sh in k_sh]
    out_shape.append(jax.ShapeDtypeStruct((n_tok, HEADS * LANES), F32))
    if bias is not None:
        out_specs.append(row_tile)
        out_shape.append(jax.ShapeDtypeStruct((n_seq, nb, HEADS, t), F32))
    return pl.pallas_call(
        body, name=name, grid=(n_seq, nb), in_specs=in_specs, out_specs=out_specs, out_shape=out_shape,
        compiler_params=_params(("parallel", "arbitrary")),
    )(*args)


def _old_attn_bwd_dq(name, qp, kp, vp, o, do, lse, bias, unit, scale, n_seq, seq, t):
    nb = seq // t
    n_tok = n_seq * seq
    nq, nk_p = len(qp), len(kp)
    nbias = 2 if bias is not None else 0
    n_in = nq + nk_p + 4 + nbias
    n_out = nq + (1 if bias is not None else 0)

    def body(*refs):
        q_refs, k_refs = refs[:nq], refs[nq:nq + nk_p]
        v_ref, o_ref, do_ref, lse_ref = refs[nq + nk_p:nq + nk_p + 4]
        bias_refs = refs[nq + nk_p + 4:n_in]
        outs = refs[n_in:n_in + n_out]
        dq_s, delta_s, dc_s = refs[n_in + n_out:]
        qi, ki = pl.program_id(2), pl.program_id(3)

        @pl.when(ki == 0)
        def _():
            dq_s[...] = jnp.zeros_like(dq_s)
            dc_s[...] = jnp.zeros_like(dc_s)
            delta_s[...] = jnp.sum(do_ref[...] * o_ref[...], axis=1, keepdims=True)

        @pl.when(ki <= qi)
        def _():
            s = _scores(q_refs, k_refs, bias_refs, qi, ki, t, unit, scale)
            p = jnp.exp(s - lse_ref[...])
            dp = lax.dot_general(do_ref[...].astype(BF16), v_ref[...].astype(BF16), _DIMS["nt"],
                                 preferred_element_type=F32)
            ds = p * (dp - delta_s[...])
            dq_s[...] += jnp.dot(ds.astype(BF16), _cat(k_refs), preferred_element_type=F32)
            dc_s[...] += jnp.sum(ds, axis=1, keepdims=True)

        @pl.when(ki == qi)
        def _():
            for n_p in range(nq):
                outs[n_p][...] = dq_s[:, n_p * LANES:(n_p + 1) * LANES] * scale
            if bias is not None:
                outs[nq][...] = dc_s[...]

    q_row = lambda b, i, j: b * nb + i
    k_row = lambda b, i, j: b * nb + jnp.minimum(j, i)
    head_q = pl.BlockSpec((t, LANES), lambda b, h, i, j: (b * nb + i, h))
    col_q = pl.BlockSpec((None, t, 1), lambda b, h, i, j: (h, b * nb + i, 0))
    in_specs = [_piece_spec(t, p, q_row) for p in qp] + [_piece_spec(t, p, k_row) for p in kp]
    in_specs += [_piece_spec(t, vp, k_row), head_q, head_q, col_q]
    args = [p[0] for p in qp] + [p[0] for p in kp] + [vp[0], o, do, lse]
    if bias is not None:
        in_specs += [col_q, pl.BlockSpec((None, 1, t), lambda b, h, i, j: (b * HEADS + h, 0, jnp.minimum(j, i)))]
        args += list(bias)
    out_specs = [head_q] * nq + ([col_q] if bias is not None else [])
    out_shape = [jax.ShapeDtypeStruct((n_tok, HEADS * LANES), F32)] * nq
    if bias is not None:
        out_shape.append(jax.ShapeDtypeStruct((HEADS, n_tok, 1), F32))
    return pl.pallas_call(
        body, name=name, grid=(n_seq, HEADS, nb, nb), in_specs=in_specs, out_specs=out_specs, out_shape=out_shape,
        scratch_shapes=[pltpu.VMEM((t, nq * LANES), F32), pltpu.VMEM((t, 1), F32), pltpu.VMEM((t, 1), F32)],
        compiler_params=_params(("parallel", "parallel", "arbitrary", "arbitrary")),
    )(*args)


def _old_attn_bwd_dkv(name, qp, kp, vp, o, do, lse, bias, unit, scale, n_seq, seq, t):
    nb = seq // t
    n_tok = n_seq * seq
    nq, nk_p = len(qp), len(kp)
    nbias = 2 if bias is not None else 0
    n_in = nq + nk_p + 4 + nbias
    n_out = nk_p + 1 + (1 if bias is not None else 0)

    def body(*refs):
        q_refs, k_refs = refs[:nq], refs[nq:nq + nk_p]
        v_ref, o_ref, do_ref, lse_ref = refs[nq + nk_p:nq + nk_p + 4]
        bias_refs = refs[nq + nk_p + 4:n_in]
        outs = refs[n_in:n_in + n_out]
        dk_s, dv_s, dc_s = refs[n_in + n_out:]
        ki, qi = pl.program_id(2), pl.program_id(3)

        @pl.when(qi == 0)
        def _():
            dk_s[...] = jnp.zeros_like(dk_s)
            dv_s[...] = jnp.zeros_like(dv_s)
            dc_s[...] = jnp.zeros_like(dc_s)

        @pl.when(qi >= ki)
        def _():
            s = _scores(q_refs, k_refs, bias_refs, qi, ki, t, unit, scale)
            p = jnp.exp(s - lse_ref[...])
            do_b = do_ref[...].astype(BF16)
            delta = jnp.sum(do_ref[...] * o_ref[...], axis=1, keepdims=True)
            dp = lax.dot_general(do_b, v_ref[...].astype(BF16), _DIMS["nt"], preferred_element_type=F32)
            ds = p * (dp - delta)
            dv_s[...] += lax.dot_general(p.astype(BF16), do_b, _DIMS["tn"], preferred_element_type=F32)
            dk_s[...] += lax.dot_general(ds.astype(BF16), _cat(q_refs), _DIMS["tn"], preferred_element_type=F32)
            dc_s[...] -= jnp.sum(ds, axis=0, keepdims=True)

        @pl.when(qi == nb - 1)
        def _():
            for n_p in range(nk_p):
                outs[n_p][...] = dk_s[:, n_p * LANES:(n_p + 1) * LANES] * scale
            outs[nk_p][...] = dv_s[...]
            if bias is not None:
                outs[nk_p + 1][...] = dc_s[...]

    q_row = lambda b, i, j: b * nb + jnp.maximum(j, i)
    k_row = lambda b, i, j: b * nb + i
    head_q = pl.BlockSpec((t, LANES), lambda b, h, i, j: (b * nb + jnp.maximum(j, i), h))
    col_q = pl.BlockSpec((None, t, 1), lambda b, h, i, j: (h, b * nb + jnp.maximum(j, i), 0))
    head_k = pl.BlockSpec((t, LANES), lambda b, h, i, j: (b * nb + i, h))
    row_k = pl.BlockSpec((None, 1, t), lambda b, h, i, j: (b * HEADS + h, 0, i))
    in_specs = [_piece_spec(t, p, q_row) for p in qp] + [_piece_spec(t, p, k_row) for p in kp]
    in_specs += [_piece_spec(t, vp, k_row), head_q, head_q, col_q]
    args = [p[0] for p in qp] + [p[0] for p in kp] + [vp[0], o, do, lse]
    if bias is not None:
        in_specs += [col_q, row_k]
        args += list(bias)
    out_specs = [head_k] * (nk_p + 1) + ([row_k] if bias is not None else [])
    out_shape = [jax.ShapeDtypeStruct((n_tok, HEADS * LANES), F32)] * (nk_p + 1)
    if bias is not None:
        out_shape.append(jax.ShapeDtypeStruct((n_seq * HEADS, 1, seq), F32))
    return pl.pallas_call(
        body, name=name, grid=(n_seq, HEADS, nb, nb), in_specs=in_specs, out_specs=out_specs, out_shape=out_shape,
        scratch_shapes=[pltpu.VMEM((t, nk_p * LANES), F32), pltpu.VMEM((t, LANES), F32), pltpu.VMEM((1, t), F32)],
        compiler_params=_params(("parallel", "parallel", "arbitrary", "arbitrary")),
    )(*args)


def _seq_cumsum(name, x, col_block, n_seq, seq, reverse, pre=None, vec=None):
    t = _tile(seq, 256, 128)
    nb = seq // t

    def body(*refs):
        x_ref = refs[0]
        vec_ref = refs[1] if vec is not None else None
        o_ref, carry = refs[-2], refs[-1]

        @pl.when(pl.program_id(1) == 0)
        def _():
            carry[...] = jnp.zeros_like(carry)

        v = x_ref[...]
        if pre is not None:
            v = pre(v, vec_ref[...])
        r = lax.broadcasted_iota(jnp.int32, (t, t), 0)
        c = lax.broadcasted_iota(jnp.int32, (t, t), 1)
        tri = jnp.where((c >= r) if reverse else (c <= r), 1.0, 0.0).astype(BF16)
        hi = v.astype(BF16)
        mid = (v - hi.astype(F32)).astype(BF16)
        lo = (v - hi.astype(F32) - mid.astype(F32)).astype(BF16)
        acc = jnp.dot(tri, hi, preferred_element_type=F32)
        acc += jnp.dot(tri, mid, preferred_element_type=F32)
        acc += jnp.dot(tri, lo, preferred_element_type=F32)
        o_ref[...] = acc + carry[...]
        carry[...] += _colsum(v)

    blk = (lambda b, i: (b * nb + nb - 1 - i)) if reverse else (lambda b, i: (b * nb + i))
    in_specs = [pl.BlockSpec((t, LANES), lambda b, i: (blk(b, i), col_block))]
    args = [x]
    if vec is not None:
        in_specs.append(pl.BlockSpec(vec.shape, lambda b, i: (0, 0)))
        args.append(vec)
    return pl.pallas_call(
        body, name=name, grid=(n_seq, nb), in_specs=in_specs,
        out_specs=pl.BlockSpec((t, LANES), lambda b, i: (blk(b, i), 0)),
        out_shape=jax.ShapeDtypeStruct((n_seq * seq, LANES), F32),
        scratch_shapes=[pltpu.VMEM((1, LANES), F32)],
        compiler_params=_params(("arbitrary", "arbitrary")),
    )(*args)


def _log_sigmoid(z):
    return -(jnp.maximum(-z, 0.0) + jnp.log(1.0 + jnp.exp(-jnp.abs(z))))


def _shift_down(u, prev_ref, n):
    out = pltpu.roll(u, n, 0)
    row = lax.broadcasted_iota(jnp.int32, u.shape, 0)
    for r in range(n):
        out = jnp.where(row == r, prev_ref[8 - n + r:8 - n + r + 1, :], out)
    return out


def _shift_up(u, next_ref, n):
    ts = u.shape[0]
    out = pltpu.roll(u, ts - n, 0)
    row = lax.broadcasted_iota(jnp.int32, u.shape, 0)
    for r in range(n):
        out = jnp.where(row == ts - n + r, next_ref[r:r + 1, :], out)
    return out


def _conv_taps(u, prev_ref, w_ref, b_ref):
    s1, s2 = _shift_down(u, prev_ref, 1), _shift_down(u, prev_ref, 2)
    return (w_ref[0:1, :] * s2 + w_ref[1:2, :] * s1 + w_ref[2:3, :] * u) + b_ref[...], s1, s2


def _conv_glu_fwd(u_il, cw_il, cb_il, n_seq, seq, wt):
    n_tok, two_f = u_il.shape
    nct = two_f // (2 * wt)
    ts = _tile(seq, 256, 8)
    ns = seq // ts

    def body(u_ref, w_ref, b_ref, a_ref, carry):
        @pl.when(pl.program_id(2) == 0)
        def _():
            carry[...] = jnp.zeros_like(carry)

        u = u_ref[...]
        uc, _, _ = _conv_taps(u, carry, w_ref, b_ref)
        gel, _ = _gelu_parts(uc[:, :wt])
        a_ref[...] = (gel * uc[:, wt:]).astype(a_ref.dtype)
        carry[...] = u[ts - 8:, :]

    return pl.pallas_call(
        body, name="conv_glu_fwd", grid=(nct, n_seq, ns),
        in_specs=[pl.BlockSpec((ts, 2 * wt), lambda j, b, s: (b * ns + s, j)),
                  pl.BlockSpec((3, 2 * wt), lambda j, b, s: (0, j)),
                  pl.BlockSpec((1, 2 * wt), lambda j, b, s: (0, j))],
        out_specs=pl.BlockSpec((ts, wt), lambda j, b, s: (b * ns + s, j)),
        out_shape=jax.ShapeDtypeStruct((n_tok, two_f // 2), BF16),
        scratch_shapes=[pltpu.VMEM((8, 2 * wt), F32)],
        compiler_params=_params(("parallel", "arbitrary", "arbitrary")),
    )(u_il, cw_il, cb_il)


def _conv_glu_bwd_pre(u_il, da, cw_il, cb_il, n_seq, seq, wt):
    n_tok, two_f = u_il.shape
    nct = two_f // (2 * wt)
    ts = _tile(seq, 256, 8)
    ns = seq // ts

    def body(u_ref, da_ref, w_ref, b_ref, d_ref, acc_ref, carry):
        first = jnp.logical_and(pl.program_id(1) == 0, pl.program_id(2) == 0)

        @pl.when(first)
        def _():
            acc_ref[...] = jnp.zeros_like(acc_ref)

        @pl.when(pl.program_id(2) == 0)
        def _():
            carry[...] = jnp.zeros_like(carry)

        u = u_ref[...]
        uc, s1, s2 = _conv_taps(u, carry, w_ref, b_ref)
        gel, dgel = _gelu_parts(uc[:, :wt])
        da_v = da_ref[...]
        d = jnp.concatenate([da_v * uc[:, wt:] * dgel, da_v * gel], axis=1)
        d_ref[...] = d
        acc_ref[0:1, :] += _colsum(d * s2)
        acc_ref[1:2, :] += _colsum(d * s1)
        acc_ref[2:3, :] += _colsum(d * u)
        acc_ref[3:4, :] += _colsum(d)
        carry[...] = u[ts - 8:, :]

    return pl.pallas_call(
        body, name="conv_glu_bwd_pre", grid=(nct, n_seq, ns),
        in_specs=[pl.BlockSpec((ts, 2 * wt), lambda j, b, s: (b * ns + s, j)),
                  pl.BlockSpec((ts, wt), lambda j, b, s: (b * ns + s, j)),
                  pl.BlockSpec((3, 2 * wt), lambda j, b, s: (0, j)),
                  pl.BlockSpec((1, 2 * wt), lambda j, b, s: (0, j))],
        out_specs=[pl.BlockSpec((ts, 2 * wt), lambda j, b, s: (b * ns + s, j)),
                   pl.BlockSpec((8, 2 * wt), lambda j, b, s: (0, j))],
        out_shape=[jax.ShapeDtypeStruct((n_tok, two_f), F32), jax.ShapeDtypeStruct((8, two_f), F32)],
        scratch_shapes=[pltpu.VMEM((8, 2 * wt), F32)],
        compiler_params=_params(("parallel", "arbitrary", "arbitrary")),
    )(u_il, da, cw_il, cb_il)


def _conv_bwd_input(d_il, cw_il, n_seq, seq, wt):
    n_tok, two_f = d_il.shape
    nct = two_f // (2 * wt)
    ts = _tile(seq, 256, 8)
    ns = seq // ts

    def body(d_ref, w_ref, o_ref, carry):
        @pl.when(pl.program_id(2) == 0)
        def _():
            carry[...] = jnp.zeros_like(carry)

        d = d_ref[...]
        o_ref[...] = (w_ref[2:3, :] * d + w_ref[1:2, :] * _shift_up(d, carry, 1)
                      + w_ref[0:1, :] * _shift_up(d, carry, 2)).astype(o_ref.dtype)
        carry[...] = d[:8, :]

    rev = lambda j, b, s: (b * ns + ns - 1 - s, j)
    return pl.pallas_call(
        body, name="conv_bwd_input", grid=(nct, n_seq, ns),
        in_specs=[pl.BlockSpec((ts, 2 * wt), rev), pl.BlockSpec((3, 2 * wt), lambda j, b, s: (0, j))],
        out_specs=pl.BlockSpec((ts, 2 * wt), rev),
        out_shape=jax.ShapeDtypeStruct((n_tok, two_f), BF16),
        scratch_shapes=[pltpu.VMEM((8, 2 * wt), F32)],
        compiler_params=_params(("parallel", "arbitrary", "arbitrary")),
    )(d_il, cw_il)


HBM = pl.BlockSpec(memory_space=pltpu.HBM)
_CHIP_FLIPS = ((1, 0), (0, 1), (1, 1))


def _place():
    x, y, c = lax.axis_index("x"), lax.axis_index("y"), lax.axis_index("c")
    return x, y, c, 2 * x + y


def _flip(v, f):
    return 1 - v if f else v


def _half_rows(c, half):
    return pl.ds(pl.multiple_of(c * half, 16), half)


def _remote(src, dst, ssem, rsem, dev):
    return pltpu.make_async_remote_copy(src_ref=src, dst_ref=dst, send_sem=ssem, recv_sem=rsem,
                                        device_id=dev, device_id_type=MESH)


def _comm_call(name, body, ins, out_shapes, n_sems):
    return pl.pallas_call(
        body, name=name, in_specs=[HBM] * len(ins), out_specs=[HBM] * len(out_shapes),
        out_shape=[pltpu.HBM(s.shape, s.dtype) for s in out_shapes],
        scratch_shapes=[pltpu.SemaphoreType.DMA((n_sems,)), pltpu.SemaphoreType.DMA((n_sems,))],
    )(*ins)


def _all_gather_weights(shards, smalls):
    n, ns = len(shards), len(smalls)
    nt = n + ns

    def body(*refs):
        src, dst = refs[:nt], refs[nt:2 * nt]
        ssem, rsem = refs[2 * nt:]
        x, y, c, me = _place()
        sib = (x, y, 1 - c)
        waits = []
        small_cps = []
        for s in range(ns):
            for k, (fx, fy) in enumerate(_CHIP_FLIPS):
                sem = 6 * n + 3 * s + k
                cp = _remote(src[n + s], dst[n + s].at[me], ssem.at[sem], rsem.at[sem], (_flip(x, fx), _flip(y, fy), c))
                cp.start()
                small_cps.append(cp)
        for w in range(n):
            rows = _half_rows(c, shards[w].shape[0] // 2)
            for k, (fx, fy) in enumerate(_CHIP_FLIPS):
                cp = _remote(src[w].at[rows], dst[w].at[me, rows], ssem.at[w * 6 + k], rsem.at[w * 6 + k],
                             (_flip(x, fx), _flip(y, fy), c))
                cp.start()
                waits.append(cp)
        fwd = []
        for w in range(n):
            rows = _half_rows(c, shards[w].shape[0] // 2)
            for k, (fx, fy) in enumerate(_CHIP_FLIPS):
                peer = 2 * _flip(x, fx) + _flip(y, fy)
                got = dst[w].at[peer, rows]
                _remote(got, got, ssem.at[w * 6 + k], rsem.at[w * 6 + k], sib).wait_recv()
                cp = _remote(got, got, ssem.at[w * 6 + 3 + k], rsem.at[w * 6 + 3 + k], sib)
                cp.start()
                fwd.append(cp)
        for w in range(n):
            rows = _half_rows(1 - c, shards[w].shape[0] // 2)
            for k, (fx, fy) in enumerate(_CHIP_FLIPS):
                peer = 2 * _flip(x, fx) + _flip(y, fy)
                got = dst[w].at[peer, rows]
                _remote(got, got, ssem.at[w * 6 + 3 + k], rsem.at[w * 6 + 3 + k], sib).wait_recv()
        for cp in small_cps:
            cp.wait()
        for cp in waits + fwd:
            cp.wait_send()

    outs = [jax.ShapeDtypeStruct((N_CHIPS,) + s.shape, s.dtype) for s in list(shards) + list(smalls)]
    res = _comm_call("all_gather_weights", body, list(shards) + list(smalls), outs, 6 * n + 3 * ns)
    return res[:n], res[n:]


def _pair_split(grads):
    n = len(grads)

    def body(*refs):
        src, got = refs[:n], refs[n:2 * n]
        ssem, rsem = refs[2 * n:]
        x, y, c, _ = _place()
        cps = []
        for w in range(n):
            half = grads[w].shape[1] // 2
            cp = _remote(src[w].at[:, _half_rows(1 - c, half)], got[w], ssem.at[w], rsem.at[w], (x, y, 1 - c))
            cp.start()
            cps.append(cp)
        for cp in cps:
            cp.wait()

    outs = [jax.ShapeDtypeStruct((g.shape[0], g.shape[1] // 2, g.shape[2]), g.dtype) for g in grads]
    return _comm_call("rs_pair_split", body, grads, outs, n)


def _chip_scatter(parts):
    n = len(parts)

    def body(*refs):
        src, dst = refs[:n], refs[n:2 * n]
        ssem, rsem = refs[2 * n:]
        x, y, c, _ = _place()
        cps = []
        for w in range(n):
            for k, (fx, fy) in enumerate(_CHIP_FLIPS):
                px, py = _flip(x, fx), _flip(y, fy)
                cp = _remote(src[w].at[2 * px + py], dst[w].at[k], ssem.at[w * 3 + k], rsem.at[w * 3 + k], (px, py, c))
                cp.start()
                cps.append(cp)
        for cp in cps:
            cp.wait()

    outs = [jax.ShapeDtypeStruct((3,) + p.shape[1:], p.dtype) for p in parts]
    return _comm_call("rs_chip_scatter", body, parts, outs, 3 * n)


def _pair_swap(halves):
    n = len(halves)

    def body(*refs):
        src, dst = refs[:n], refs[n:2 * n]
        ssem, rsem = refs[2 * n:]
        x, y, c, _ = _place()
        cps = []
        for w in range(n):
            cp = _remote(src[w], dst[w], ssem.at[w], rsem.at[w], (x, y, 1 - c))
            cp.start()
            cps.append(cp)
        for cp in cps:
            cp.wait()

    outs = [jax.ShapeDtypeStruct(h.shape, h.dtype) for h in halves]
    return _comm_call("rs_pair_swap", body, halves, outs, n)


def _gather_small(vec):
    def body(src, dst, ssem, rsem):
        x, y, c, _ = _place()
        me = 4 * x + 2 * y + c
        cps = []
        for r in range(1, 8):
            dev = (_flip(x, r & 4), _flip(y, r & 2), _flip(c, r & 1))
            cp = _remote(src, dst.at[me], ssem.at[r - 1], rsem.at[r - 1], dev)
            cp.start()
            cps.append(cp)
        for cp in cps:
            cp.wait()

    out = jax.ShapeDtypeStruct((8,) + vec.shape, vec.dtype)
    return _comm_call("gather_small", body, [vec], [out], 7)[0]


def _sum_slots(name, stacked, first=None):
    n, r, c = stacked.shape
    tr = _tile(r, 256, 8)

    def body(*refs):
        s_ref, o_ref = refs[-2], refs[-1]
        acc = refs[0][...].astype(F32) if first is not None else s_ref[0].astype(F32)
        for s in range(0 if first is not None else 1, n):
            acc = acc + s_ref[s].astype(F32)
        o_ref[...] = acc

    row_spec = pl.BlockSpec((tr, c), lambda i: (i, 0))
    in_specs = ([row_spec] if first is not None else []) + [pl.BlockSpec((n, tr, c), lambda i: (0, i, 0))]
    args = ([first] if first is not None else []) + [stacked]
    return pl.pallas_call(
        body, name=name, grid=(r // tr,), in_specs=in_specs, out_specs=row_spec,
        out_shape=jax.ShapeDtypeStruct((r, c), F32), compiler_params=_params(("parallel",)),
    )(*args)


def _adamw(name, w, g, m, v):
    r, c = w.shape
    tr = _tile(r, 256, 8)
    bc1, bc2 = 1.0 - ADAM_B1 ** ADAM_STEP, 1.0 - ADAM_B2 ** ADAM_STEP

    def body(w_ref, g_ref, m_ref, v_ref, d_ref, nm_ref, nv_ref):
        g_v = g_ref[...]
        nm = ADAM_B1 * m_ref[...] + (1.0 - ADAM_B1) * g_v
        nv = ADAM_B2 * v_ref[...] + (1.0 - ADAM_B2) * (g_v * g_v)
        d_ref[...] = -ADAM_LR * ((nm / bc1) / (jnp.sqrt(nv / bc2) + ADAM_EPS) + ADAM_WD * w_ref[...])
        nm_ref[...] = nm
        nv_ref[...] = nv

    spec = pl.BlockSpec((tr, c), lambda i: (i, 0))
    return pl.pallas_call(
        body, name=name, grid=(r // tr,), in_specs=[spec] * 4, out_specs=[spec] * 3,
        out_shape=[jax.ShapeDtypeStruct((r, c), F32)] * 3, compiler_params=_params(("parallel",)),
    )(w, g, m, v)


def _pad_cols(a, cols):
    return jnp.pad(a, ((0, 0), (0, cols - a.shape[1])))


def _rot_cols(w):
    h = w.shape[-1] // 2
    return jnp.concatenate([-w[..., h:], w[..., :h]], axis=-1)


def _unrot_cols(d):
    h = d.shape[-1] // 2
    return jnp.concatenate([d[..., h:], -d[..., :h]], axis=-1)


def _logical(g):
    return jnp.transpose(g, (1, 0, 2)).reshape(g.shape[1], N_CHIPS * g.shape[2])


def _chunks(a, n):
    return jnp.transpose(a.reshape(a.shape[0], N_CHIPS, n), (1, 0, 2))


def kernel(x, positions, pre_mix_norm, w_in, q_a_norm, w_uq, kv_a_norm, w_ukv, b_forget, b_gate, w_branch_mla, w_branch_fox, w_out, post_mix_norm, pre_ffn_norm, w_up, conv_w, conv_b, w_down, post_ffn_norm, loss_target, m_pre_mix_norm, m_w_in, m_q_a_norm, m_w_uq, m_kv_a_norm, m_w_ukv, m_b_forget, m_b_gate, m_w_branch_mla, m_w_branch_fox, m_w_out, m_post_mix_norm, m_pre_ffn_norm, m_w_up, m_conv_w, m_conv_b, m_w_down, m_post_ffn_norm, v_pre_mix_norm, v_w_in, v_q_a_norm, v_w_uq, v_kv_a_norm, v_w_ukv, v_b_forget, v_b_gate, v_w_branch_mla, v_w_branch_fox, v_w_out, v_post_mix_norm, v_pre_ffn_norm, v_w_up, v_conv_w, v_conv_b, v_w_down, v_post_ffn_norm):
    n_seq, seq, d = x.shape
    n_tok = n_seq * seq
    d_in = N_CHIPS * w_in.shape[1]
    two_f = N_CHIPS * w_up.shape[1]
    ff_dim = two_f // 2
    assert d_in == QL + KVL + ROPE + 3 * HEADS * FDIM + HEADS + 2 * d
    n_in_shard = w_in.shape[1]
    in_pad = -(-n_in_shard // LANES) * LANES
    hd = HEADS * LANES
    xc, yc, cc = lax.axis_index("x"), lax.axis_index("y"), lax.axis_index("c")
    chip = 2 * xc + yc
    t_attn = _tile(seq, 256, 128)

    shards = [_pad_cols(w_in, in_pad).astype(BF16), w_uq.astype(BF16), w_ukv.astype(BF16), w_branch_mla.astype(BF16),
              w_branch_fox.astype(BF16), w_out.astype(BF16), w_up.astype(BF16), w_down.astype(BF16)]
    cw8 = jnp.pad(conv_w, ((0, 5), (0, 0)))
    gathered, (g_cw,) = _all_gather_weights(shards, [cw8])
    put_own = lambda g, s: lax.dynamic_update_slice(g, s[None], (chip, 0, 0))
    g_in, g_uq, g_ukv, g_bm, g_bf, g_out, g_up, g_down = [put_own(g, s) for g, s in zip(gathered, shards)]
    g_cw = put_own(g_cw, cw8)

    o_q, o_kv, o_kpe = 0, QL, QL + KVL
    o_f = o_kpe + ROPE
    o_fl = o_f + 3 * hd
    o_g = o_fl + HEADS

    def chip_cols(lo, hi):
        out = []
        while lo < hi:
            j = lo // n_in_shard
            end = min(hi, (j + 1) * n_in_shard)
            out.append((j, lo - j * n_in_shard, end - j * n_in_shard))
            lo = end
        return out

    take = lambda lo, hi: [g_in[j, :, a:b] for j, a, b in chip_cols(lo, hi)]
    w_kpe = jnp.concatenate(take(o_kpe, o_f), axis=1)
    zeros = lambda n: jnp.zeros((d, n), BF16)
    win_p = jnp.concatenate(
        take(o_g, d_in) + take(o_q, o_kpe) + [w_kpe, zeros(LANES - ROPE), _rot_cols(w_kpe)] + take(o_fl, o_g)
        + [zeros(LANES - ROPE - HEADS)] + take(o_f, o_fl), axis=1)
    n_p = win_p.shape[1]
    cb_gm, cb_gf = 0, 1
    c_lat = 2 * d
    c_kx, c_kr = c_lat + QL + KVL, c_lat + QL + KVL + LANES
    n_pa = c_kr + LANES
    assert n_p == n_pa + 3 * hd

    uq3 = _logical(g_uq).reshape(QL, HEADS, NOPE + ROPE)
    pe = uq3[:, :, NOPE:]
    pad_pe = lambda a: jnp.pad(a, ((0, 0), (0, 0), (0, LANES - ROPE))).reshape(QL, hd)
    wuq_p = jnp.concatenate([uq3[:, :, :NOPE].reshape(QL, hd), pad_pe(pe), pad_pe(_rot_cols(pe))], axis=1)
    ukv3 = _logical(g_ukv).reshape(KVL, HEADS, NOPE + VDIM)
    wukv_p = jnp.concatenate([ukv3[:, :, :NOPE].reshape(KVL, hd), ukv3[:, :, NOPE:].reshape(KVL, hd)], axis=1)

    n_bm, n_up = w_branch_mla.shape[1], w_up.shape[1]
    l_bm, l_up = _Chunked(n_bm), _Chunked(n_up)
    wt = n_up // 2
    n_ut = two_f // wt
    il = lambda cblk: jnp.where(cblk < n_ut // 2, 2 * cblk, 2 * (cblk - n_ut // 2) + 1)
    l_il = _Plain(il)
    to_il = lambda a: a.reshape(a.shape[0], 2, n_ut // 2, wt).transpose(0, 2, 1, 3).reshape(a.shape[0], two_f)
    from_il = lambda a: a.reshape(a.shape[0], n_ut // 2, 2, wt).transpose(0, 2, 1, 3).reshape(a.shape[0], two_f)
    w_down_full = g_down.reshape(ff_dim, d)
    w_out_full = g_out.reshape(d, d)

    inv_freq = 1.0 / (ROPE_THETA ** (jnp.arange(0, ROPE, 2, dtype=F32) / ROPE))
    ang = positions.astype(F32).reshape(n_tok, 1) * inv_freq
    cos, sin = jnp.cos(ang), jnp.sin(ang)
    cs = _pad_cols(jnp.concatenate([cos, cos], axis=1), LANES)
    sn = _pad_cols(jnp.concatenate([sin, sin], axis=1), LANES)

    row = lambda v: v.reshape(1, -1)
    x2 = x.reshape(n_tok, d)
    tgt = loss_target.reshape(n_tok, d)

    (h,) = _rows("rms_pre_mix", lambda r, v: ([r[0] * _rstd(r[0]) * v[0]], []),
                 [(x2, d, 0)], [row(pre_mix_norm)], [(d, BF16)], [], n_tok)
    proj = _mm("proj_in", "nn", h, win_p, n_tok, n_pa, d)
    tn_f = _tile(3 * hd, 1024, 128)
    assert n_pa % tn_f == 0
    proj_f = _mm("proj_in_fox", "nn", h, win_p, n_tok, 3 * hd, d, tn=tn_f, lb=_Plain(lambda cblk: cblk + n_pa // tn_f),
                 out_dtype=BF16)

    bf_vec = jnp.pad(row(b_forget), ((0, 0), (ROPE, LANES - ROPE - HEADS)))

    def lat_fwd(r, v):
        ql, kvl = r[0], r[1]
        return [ql * _rstd(ql) * v[0], kvl * _rstd(kvl) * v[1], r[2] * r[4] + r[3] * r[5]], []

    qn, kvn, rk = _rows("latent_norms", lat_fwd,
                        [(proj, QL, c_lat // QL), (proj, KVL, (c_lat + QL) // KVL), (proj, LANES, c_kx // LANES),
                         (proj, LANES, c_kr // LANES), (cs, LANES, 0), (sn, LANES, 0)],
                        [row(q_a_norm), row(kv_a_norm)], [(QL, BF16), (KVL, BF16), (LANES, BF16)], [], n_tok)
    q_p = _mm("q_up", "nn", qn, wuq_p, n_tok, 3 * hd, QL)
    kv_p = _mm("kv_up", "nn", kvn, wukv_p, n_tok, 2 * hd, KVL, out_dtype=BF16)

    def rope_q(r, v):
        c8, s8 = jnp.tile(r[3], (1, HEADS)), jnp.tile(r[4], (1, HEADS))
        return [r[0], r[1] * c8 + r[2] * s8], []

    q_nope, rq = _rows("rope_q", rope_q, [(q_p, hd, 0), (q_p, hd, 1), (q_p, hd, 2), (cs, LANES, 0), (sn, LANES, 0)], [],
                       [(hd, BF16), (hd, BF16)], [], n_tok)

    mla_q = [(q_nope, 0, False), (rq, 0, False)]
    mla_k = [(kv_p, 0, False), (rk, 0, True)]
    mla_v = (kv_p, 1)
    mla_scale = (NOPE + ROPE) ** -0.5
    o_mla, lse_mla = _attn_fwd("mla_fwd", mla_q, mla_k, mla_v, None, CHUNK, mla_scale, n_seq, seq, t_attn)

    c_run = _seq_cumsum("forget_cumsum", proj, c_kr // LANES, n_seq, seq, False,
                        pre=lambda z, b: _log_sigmoid(z + b), vec=bf_vec)
    nb_attn = seq // t_attn
    c_rowf = jnp.transpose(c_run[:, ROPE:ROPE + HEADS].reshape(n_seq, nb_attn, t_attn, HEADS), (0, 1, 3, 2))
    fox_q, fox_k, fox_v = [(proj_f, 0, False)], [(proj_f, 1, False)], (proj_f, 2)
    fox_scale = FDIM ** -0.5
    fox_bias = (c_run, c_rowf)
    o_fox, lse_fox = _attn_fwd("fox_fwd", fox_q, fox_k, fox_v, fox_bias, 1, fox_scale, n_seq, seq, t_attn)

    pm = _mm("branch_mla", "nn", o_mla, g_bm, n_tok, d, hd, lb=l_bm, tn=n_bm)
    pf = _mm("branch_fox", "nn", o_fox, g_bf, n_tok, d, hd, lb=l_bm, tn=n_bm)
    bg = row(b_gate)

    def merge(r, v):
        return [_sigmoid(r[0] + v[0]) * r[2] + _sigmoid(r[1] + v[1]) * r[3]], []

    (merged,) = _rows("gate_merge", merge, [(proj, d, cb_gm), (proj, d, cb_gf), (pm, d, 0), (pf, d, 0)],
                      [bg[:, :d], bg[:, d:]], [(d, BF16)], [], n_tok)
    y1 = _mm("mix_out", "nn", merged, w_out_full, n_tok, d, d)

    def resid_norm(r, v):
        x1v = r[0] + r[1] * _rstd(r[1]) * v[0]
        return [x1v, x1v * _rstd(x1v) * v[1]], []

    x1, h2 = _rows("post_mix_pre_ffn", resid_norm, [(x2, d, 0), (y1, d, 0)], [row(post_mix_norm), row(pre_ffn_norm)],
                   [(d, F32), (d, BF16)], [], n_tok)

    u_il = _mm("ffn_up", "nn", h2, g_up, n_tok, two_f, d, lb=l_up, lo=l_il, tn=wt)
    cw_il = to_il(_logical(g_cw)[:3])
    cb_il = to_il(row(conv_b))
    act = _conv_glu_fwd(u_il, cw_il, cb_il, n_seq, seq, wt)
    ff = _mm("ffn_down", "nn", act, w_down_full, n_tok, d, ff_dim)

    def final(r, v):
        x1v, ffv, tg = r
        diff = x1v + ffv * _rstd(ffv) * v[0] - tg
        dx2v = diff / d
        dffv, dg4 = _rms_bwd(ffv, v[0], dx2v)
        sq = jnp.sum(jnp.sum(diff * diff, axis=1, keepdims=True), axis=0, keepdims=True)
        return [dx2v, dffv], [dg4, jnp.broadcast_to(sq, (1, LANES))]

    dx2, dff, dg_post_ffn, sq_sum = _rows("loss_post_ffn_bwd", final, [(x1, d, 0), (ff, d, 0), (tgt, d, 0)],
                                          [row(post_ffn_norm)], [(d, F32), (d, BF16)], [(1, d), (1, LANES)], n_tok)
    dact = _mm("ffn_down_dx", "nt", dff, w_down_full, n_tok, ff_dim, d, tn=wt)
    gw_down = _mm("ffn_down_dw", "tn", act, dff, ff_dim, d, n_tok, tm=wt, out_dtype=BF16)
    d_il, conv_acc = _conv_glu_bwd_pre(u_il, dact, cw_il, cb_il, n_seq, seq, wt)
    du_il = _conv_bwd_input(d_il, cw_il, n_seq, seq, wt)
    dh2 = _mm("ffn_up_dx", "nt", du_il, g_up, n_tok, d, two_f, la=l_il, lb=l_up, tk=wt)
    gw_up = _mm("ffn_up_dw", "tn", h2, du_il, d, two_f, n_tok, lb=l_il, lo=l_up, tn=wt, out_dtype=BF16)

    def mid_bwd(r, v):
        x1v, y1v, dx2v, dh2v = r
        d3, dg3 = _rms_bwd(x1v, v[1], dh2v)
        dx1v = dx2v + d3
        dy1v, dg2 = _rms_bwd(y1v, v[0], dx1v)
        return [dx1v, dy1v], [dg3, dg2]

    dx1, dy1, dg_pre_ffn, dg_post_mix = _rows(
        "pre_ffn_post_mix_bwd", mid_bwd, [(x1, d, 0), (y1, d, 0), (dx2, d, 0), (dh2, d, 0)],
        [row(post_mix_norm), row(pre_ffn_norm)], [(d, F32), (d, BF16)], [(1, d), (1, d)], n_tok)
    dmerged = _mm("mix_out_dx", "nt", dy1, w_out_full, n_tok, d, d)
    gw_out = _mm("mix_out_dw", "tn", merged, dy1, d, d, n_tok, out_dtype=BF16)

    def gate_bwd(r, v):
        zm, zf, pmv, pfv, dm = r
        gm, gf = _sigmoid(zm + v[0]), _sigmoid(zf + v[1])
        dzm, dzf = dm * pmv * gm * (1.0 - gm), dm * pfv * gf * (1.0 - gf)
        return [dm * gm, dm * gf, dzm, dzf], [_colsum(dzm), _colsum(dzf)]

    dpm, dpf, dzm, dzf, dbg_m, dbg_f = _rows(
        "gate_merge_bwd", gate_bwd, [(proj, d, cb_gm), (proj, d, cb_gf), (pm, d, 0), (pf, d, 0), (dmerged, d, 0)],
        [bg[:, :d], bg[:, d:]], [(d, BF16)] * 4, [(1, d), (1, d)], n_tok)
    tk_b = min(n_bm, 512)
    do_mla = _mm("branch_mla_dx", "nt", dpm, g_bm, n_tok, hd, d, lb=l_bm, tk=tk_b)
    do_fox = _mm("branch_fox_dx", "nt", dpf, g_bf, n_tok, hd, d, lb=l_bm, tk=tk_b)
    gw_bm = _mm("branch_mla_dw", "tn", o_mla, dpm, hd, d, n_tok, lo=l_bm, tn=n_bm, out_dtype=BF16)
    gw_bf = _mm("branch_fox_dw", "tn", o_fox, dpf, hd, d, n_tok, lo=l_bm, tn=n_bm, out_dtype=BF16)

    dq_nope, drq, delta_mla, dob_mla = _attn_bwd_dq("mla_bwd_dq", mla_q, mla_k, mla_v, o_mla, do_mla, lse_mla, None, CHUNK,
                                                    mla_scale, n_seq, seq, t_attn)
    dk_nope, drk, dv_mla = _attn_bwd_dkv("mla_bwd_dkv", mla_q, mla_k, mla_v, dob_mla, lse_mla, delta_mla, None, CHUNK,
                                         mla_scale, n_seq, seq, t_attn)
    dfq, delta_fox, dob_fox, dc_q = _attn_bwd_dq("fox_bwd_dq", fox_q, fox_k, fox_v, o_fox, do_fox, lse_fox, fox_bias, 1,
                                                 fox_scale, n_seq, seq, t_attn)
    dfk, dfv, dc_k = _attn_bwd_dkv("fox_bwd_dkv", fox_q, fox_k, fox_v, dob_fox, lse_fox, delta_fox, fox_bias, 1, fox_scale,
                                   n_seq, seq, t_attn)
    dc_k8 = jnp.transpose(dc_k, (0, 1, 3, 2)).reshape(n_tok, HEADS)
    dc128 = dc_q + jnp.pad(dc_k8, ((0, 0), (ROPE, LANES - ROPE - HEADS)))
    dlogf = _seq_cumsum("forget_cumsum_bwd", dc128, 0, n_seq, seq, True)

    def mla_pack(r, v):
        dqn_v, drq_v, dkn_v, dv_v, drk_v, c1, s1 = r
        c8, s8 = jnp.tile(c1, (1, HEADS)), jnp.tile(s1, (1, HEADS))
        return [jnp.concatenate([dqn_v, drq_v * c8, drq_v * s8], axis=1), jnp.concatenate([dkn_v, dv_v], axis=1),
                drk_v * c1, drk_v * s1], []

    dq_p, dkv_p, dkx, dkr = _rows(
        "mla_rope_bwd", mla_pack,
        [(dq_nope, hd, 0), (drq, hd, 0), (dk_nope, hd, 0), (dv_mla, hd, 0), (drk, LANES, 0), (cs, LANES, 0), (sn, LANES, 0)],
        [], [(3 * hd, BF16), (2 * hd, BF16), (LANES, F32), (LANES, F32)], [], n_tok)
    dqn = _mm("q_up_dx", "nt", dq_p, wuq_p, n_tok, QL, 3 * hd)
    gw_uq_p = _mm("q_up_dw", "tn", qn, dq_p, QL, 3 * hd, n_tok, out_dtype=BF16)
    dkvn = _mm("kv_up_dx", "nt", dkv_p, wukv_p, n_tok, KVL, 2 * hd)
    gw_ukv_p = _mm("kv_up_dw", "tn", kvn, dkv_p, KVL, 2 * hd, n_tok, out_dtype=BF16)

    def lat_bwd(r, v):
        ql, kvl, dqn_v, dkvn_v, dkx_v, dkr_v, zblk, dlf = r
        dql, dgq = _rms_bwd(ql, v[0], dqn_v)
        dkvl, dgkv = _rms_bwd(kvl, v[1], dkvn_v)
        dfl = dlf * _sigmoid(-(zblk + v[2]))
        return [jnp.concatenate([dql, dkvl, dkx_v, dkr_v + dfl], axis=1)], [dgq, dgkv, _colsum(dfl)]

    dlat, dg_q, dg_kv, dbf = _rows(
        "latent_bwd", lat_bwd,
        [(proj, QL, c_lat // QL), (proj, KVL, (c_lat + QL) // KVL), (dqn, QL, 0), (dkvn, KVL, 0), (dkx, LANES, 0),
         (dkr, LANES, 0), (proj, LANES, c_kr // LANES), (dlogf, LANES, 0)],
        [row(q_a_norm), row(kv_a_norm), bf_vec], [(QL + KVL + 2 * LANES, BF16)], [(1, QL), (1, KVL), (1, LANES)], n_tok)
    dproj = jnp.concatenate([dzm, dzf, dlat, dfq.astype(BF16), dfk.astype(BF16), dfv.astype(BF16)], axis=1)
    dh = _mm("proj_in_dx", "nt", dproj, win_p, n_tok, d, n_p)
    gw_in_p = _mm("proj_in_dw", "tn", h, dproj, d, n_p, n_tok, out_dtype=BF16)

    def first_bwd(r, v):
        dxa, dg1 = _rms_bwd(r[0], v[0], r[1])
        return [r[2] + dxa], [dg1]

    grad_x, dg_pre_mix = _rows("pre_mix_bwd", first_bwd, [(x2, d, 0), (dh, d, 0), (dx1, d, 0)], [row(pre_mix_norm)],
                               [(d, F32)], [(1, d)], n_tok)

    f32 = lambda a: a.astype(F32)
    kr_blk = gw_in_p[:, c_kr:c_kr + LANES]
    d_kpe = (f32(gw_in_p[:, c_kx:c_kx + ROPE]) + _unrot_cols(f32(kr_blk[:, :ROPE]))).astype(BF16)
    in_pieces = [(o_q, gw_in_p, c_lat, QL + KVL), (o_kpe, d_kpe, 0, ROPE), (o_f, gw_in_p, n_pa, 3 * hd),
                 (o_fl, kr_blk, ROPE, HEADS), (o_g, gw_in_p, 0, 2 * d)]
    gc_in = []
    for j in range(N_CHIPS):
        lo, hi, cols = j * n_in_shard, (j + 1) * n_in_shard, []
        for first, arr, at, width in in_pieces:
            a, b = max(lo, first), min(hi, first + width)
            if a < b:
                cols.append(arr[:, at + a - first:at + b - first])
        cols.append(jnp.zeros((d, in_pad - n_in_shard), BF16))
        gc_in.append(jnp.concatenate(cols, axis=1))
    gc_in = jnp.stack(gc_in)
    uq_parts = [gw_uq_p[:, i * hd:(i + 1) * hd].reshape(QL, HEADS, LANES) for i in range(3)]
    d_pe = (f32(uq_parts[1][:, :, :ROPE]) + _unrot_cols(f32(uq_parts[2][:, :, :ROPE]))).astype(BF16)
    gc_uq = _chunks(jnp.concatenate([uq_parts[0], d_pe], axis=2).reshape(QL, HEADS * (NOPE + ROPE)), w_uq.shape[1])
    gc_ukv = _chunks(jnp.concatenate([gw_ukv_p[:, :hd].reshape(KVL, HEADS, NOPE), gw_ukv_p[:, hd:].reshape(KVL, HEADS, VDIM)],
                                     axis=2).reshape(KVL, HEADS * (NOPE + VDIM)), w_ukv.shape[1])
    grads = [gc_in, gc_uq, gc_ukv, gw_bm, gw_bf, gw_out.reshape(N_CHIPS, d // N_CHIPS, d), gw_up,
             gw_down.reshape(N_CHIPS, ff_dim // N_CHIPS, d)]

    big = ["w_in", "w_uq", "w_ukv", "w_branch_mla", "w_branch_fox", "w_out", "w_up", "w_down"]
    theirs = _pair_split(grads)
    parts = []
    for nm, g, b in zip(big, grads, theirs):
        a = lax.dynamic_slice_in_dim(g, cc * b.shape[1], b.shape[1], axis=1)
        r2, c2 = a.shape[0] * a.shape[1], a.shape[2]
        (p,) = _rows("rs_pair_add_" + nm, lambda r, v: ([r[0].astype(F32) + r[1].astype(F32)], []),
                     [(a.reshape(r2, c2), c2, 0), (b.reshape(r2, c2), c2, 0)], [], [(c2, BF16)], [], r2)
        parts.append(p.reshape(a.shape))
    landed = _chip_scatter(parts)
    halves = [_sum_slots("rs_chip_sum_" + nm, s, first=lax.dynamic_index_in_dim(p, chip, 0, keepdims=False))
              for nm, s, p in zip(big, landed, parts)]
    other = _pair_swap(halves)
    full = [jnp.concatenate([jnp.where(cc == 0, a, b), jnp.where(cc == 0, b, a)], axis=0) for a, b in zip(halves, other)]
    g_big = dict(zip(big, full))
    g_big["w_in"] = g_big["w_in"][:, :n_in_shard]

    conv_acc_l = from_il(conv_acc)
    pieces = [dg_pre_mix, dg_q, dg_kv, dbf, dbg_m, dbg_f, dg_post_mix, dg_pre_ffn, conv_acc_l[3:4], dg_post_ffn,
              conv_acc_l[0:1], conv_acc_l[1:2], conv_acc_l[2:3], sq_sum]
    sizes = [p.shape[1] for p in pieces]
    flat = jnp.concatenate(pieces, axis=1)
    n_rows = -(-flat.shape[1] // (8 * LANES)) * 8
    flat = _pad_cols(flat, n_rows * LANES).reshape(n_rows, LANES)
    slots = lax.dynamic_update_slice(_gather_small(flat), flat[None], (2 * chip + cc, 0, 0))
    total = _sum_slots("small_sum", slots).reshape(1, n_rows * LANES)
    offs = [sum(sizes[:i]) for i in range(len(sizes))]
    tot = [total[0, o:o + s] for o, s in zip(offs, sizes)]
    loss = 0.5 * tot[13][0] / d
    g_small = {"pre_mix_norm": tot[0], "q_a_norm": tot[1], "kv_a_norm": tot[2], "b_forget": tot[3][ROPE:ROPE + HEADS],
               "b_gate": jnp.concatenate([tot[4], tot[5]]), "post_mix_norm": tot[6], "pre_ffn_norm": tot[7],
               "conv_b": tot[8], "post_ffn_norm": tot[9]}
    gcw_full = jnp.stack([tot[10], tot[11], tot[12]])
    g_conv_w = lax.dynamic_slice(gcw_full, (0, chip * n_up), (3, n_up))

    given = dict(pre_mix_norm=(pre_mix_norm, m_pre_mix_norm, v_pre_mix_norm), w_in=(w_in, m_w_in, v_w_in),
                 q_a_norm=(q_a_norm, m_q_a_norm, v_q_a_norm), w_uq=(w_uq, m_w_uq, v_w_uq),
                 kv_a_norm=(kv_a_norm, m_kv_a_norm, v_kv_a_norm), w_ukv=(w_ukv, m_w_ukv, v_w_ukv),
                 b_forget=(b_forget, m_b_forget, v_b_forget), b_gate=(b_gate, m_b_gate, v_b_gate),
                 w_branch_mla=(w_branch_mla, m_w_branch_mla, v_w_branch_mla),
                 w_branch_fox=(w_branch_fox, m_w_branch_fox, v_w_branch_fox), w_out=(w_out, m_w_out, v_w_out),
                 post_mix_norm=(post_mix_norm, m_post_mix_norm, v_post_mix_norm),
                 pre_ffn_norm=(pre_ffn_norm, m_pre_ffn_norm, v_pre_ffn_norm), w_up=(w_up, m_w_up, v_w_up),
                 conv_w=(conv_w, m_conv_w, v_conv_w), conv_b=(conv_b, m_conv_b, v_conv_b),
                 w_down=(w_down, m_w_down, v_w_down), post_ffn_norm=(post_ffn_norm, m_post_ffn_norm, v_post_ffn_norm))
    order = list(given)
    grad, delta, new_m, new_v = {}, {}, {}, {}
    for nm in big + ["conv_w"]:
        grad[nm] = g_big[nm] if nm in g_big else g_conv_w
        delta[nm], new_m[nm], new_v[nm] = _adamw("adamw_" + nm, given[nm][0], grad[nm], given[nm][1], given[nm][2])
    small = list(g_small)
    padded = [-(-g_small[nm].shape[0] // LANES) * LANES for nm in small]
    s_rows = -(-sum(padded) // (8 * LANES)) * 8

    def pack(vals):
        cat = jnp.concatenate([jnp.pad(a, (0, p - a.shape[0])) for a, p in zip(vals, padded)])
        return jnp.pad(cat, (0, s_rows * LANES - cat.shape[0])).reshape(s_rows, LANES)

    packed = _adamw("adamw_small", pack([given[nm][0] for nm in small]), pack([g_small[nm] for nm in small]),
                    pack([given[nm][1] for nm in small]), pack([given[nm][2] for nm in small]))
    s_offs = [sum(padded[:i]) for i in range(len(small))]
    for nm, o in zip(small, s_offs):
        n_el = g_small[nm].shape[0]
        grad[nm] = g_small[nm]
        delta[nm], new_m[nm], new_v[nm] = [p.reshape(-1)[o:o + n_el] for p in packed]
    return (loss, grad_x.reshape(n_seq, seq, d), *[grad[nm] for nm in order], *[delta[nm] for nm in order],
            *[new_m[nm] for nm in order], *[new_v[nm] for nm in order])
```

```python
import functools
import math

import jax
import jax.numpy as jnp
from jax import lax
from jax.experimental import pallas as pl
from jax.experimental.pallas import tpu as pltpu

F32, BF16 = jnp.float32, jnp.bfloat16
MESH = pl.DeviceIdType.MESH

HEADS = 8
NOPE, ROPE, VDIM = 128, 64, 128
QL, KVL = 512, 256
FDIM = 128
CHUNK = 64
ROPE_THETA = 10000.0
EPS = 1e-6
NEG_INF = -1e30
ADAM_LR, ADAM_B1, ADAM_B2, ADAM_EPS, ADAM_WD, ADAM_STEP = 0.001, 0.9, 0.999, 1e-08, 0.01, 10

VMEM_LIMIT_BYTES = 52 * 1024 * 1024
LANES = 128
N_CHIPS = 4


def _params(sem):
    return pltpu.CompilerParams(dimension_semantics=sem, vmem_limit_bytes=VMEM_LIMIT_BYTES)


def _tile(n, target, mult):
    if n <= target:
        return n
    t = (target // mult) * mult
    while t >= mult:
        if n % t == 0:
            return t
        t -= mult
    raise ValueError(f"no tile for {n} (target {target}, multiple of {mult})")


class _Plain:
    def __init__(self, perm=None):
        self.perm = perm

    def spec(self, tr, tc, rc):
        perm = self.perm

        def imap(i, j, k):
            r, c = rc(i, j, k)
            return (r, perm(c) if perm is not None else c)

        return pl.BlockSpec((tr, tc), imap)

    def shape(self, rows, cols):
        return (rows, cols)


class _Chunked:
    def __init__(self, n):
        self.n = n

    def spec(self, tr, tc, rc):
        assert self.n % tc == 0, (self.n, tc)
        per = self.n // tc

        def imap(i, j, k):
            r, c = rc(i, j, k)
            return (c // per, r, c % per)

        return pl.BlockSpec((None, tr, tc), imap)

    def shape(self, rows, cols):
        assert cols == N_CHIPS * self.n
        return (N_CHIPS, rows, self.n)


_DIMS = {"nn": (((1,), (0,)), ((), ())), "nt": (((1,), (1,)), ((), ())), "tn": (((0,), (0,)), ((), ()))}


def _mm_single(name, mode, a, b, m, n, k, tm, tn, la, lb, lo, out_dtype):
    if mode == "nn":
        a_spec = la.spec(tm, k, lambda i, j, kk: (i, 0))
        b_spec = lb.spec(k, tn, lambda i, j, kk: (0, j))
    elif mode == "nt":
        a_spec = la.spec(tm, k, lambda i, j, kk: (i, 0))
        b_spec = lb.spec(tn, k, lambda i, j, kk: (j, 0))
    else:
        a_spec = la.spec(k, tm, lambda i, j, kk: (0, i))
        b_spec = lb.spec(k, tn, lambda i, j, kk: (0, j))
    o_spec = lo.spec(tm, tn, lambda i, j, kk: (i, j))
    dims = _DIMS[mode]

    def body(a_ref, b_ref, o_ref):
        o_ref[...] = lax.dot_general(a_ref[...].astype(BF16), b_ref[...].astype(BF16), dims,
                                     preferred_element_type=F32).astype(o_ref.dtype)

    return pl.pallas_call(
        body, name=name, grid=(m // tm, n // tn, 1),
        in_specs=[a_spec, b_spec], out_specs=o_spec,
        out_shape=jax.ShapeDtypeStruct(lo.shape(m, n), out_dtype),
        compiler_params=_params(("parallel", "parallel", "arbitrary")),
    )(a, b)


def _mm(name, mode, a, b, m, n, k, *, tm=1024, tn=1024, tk=2048, la=None, lb=None, lo=None, out_dtype=F32):
    la, lb, lo = la or _Plain(), lb or _Plain(), lo or _Plain()
    tm, tn, tk = _tile(m, tm, 128), _tile(n, tn, 128), _tile(k, tk, 128)
    nk = k // tk
    if nk == 1:
        return _mm_single(name, mode, a, b, m, n, k, tm, tn, la, lb, lo, out_dtype)
    if mode == "nn":
        a_spec = la.spec(tm, tk, lambda i, j, kk: (i, kk))
        b_spec = lb.spec(tk, tn, lambda i, j, kk: (kk, j))
    elif mode == "nt":
        a_spec = la.spec(tm, tk, lambda i, j, kk: (i, kk))
        b_spec = lb.spec(tn, tk, lambda i, j, kk: (j, kk))
    else:
        a_spec = la.spec(tk, tm, lambda i, j, kk: (kk, i))
        b_spec = lb.spec(tk, tn, lambda i, j, kk: (kk, j))
    o_spec = lo.spec(tm, tn, lambda i, j, kk: (i, j))
    dims = _DIMS[mode]

    def body(a_ref, b_ref, o_ref, acc_ref):
        kk = pl.program_id(2)

        @pl.when(kk == 0)
        def _():
            acc_ref[...] = jnp.zeros_like(acc_ref)

        acc_ref[...] += lax.dot_general(a_ref[...].astype(BF16), b_ref[...].astype(BF16), dims,
                                        preferred_element_type=F32)

        @pl.when(kk == nk - 1)
        def _():
            o_ref[...] = acc_ref[...].astype(o_ref.dtype)

    return pl.pallas_call(
        body, name=name, grid=(m // tm, n // tn, nk),
        in_specs=[a_spec, b_spec], out_specs=o_spec,
        out_shape=jax.ShapeDtypeStruct(lo.shape(m, n), out_dtype),
        scratch_shapes=[pltpu.VMEM((tm, tn), F32)],
        compiler_params=_params(("parallel", "parallel", "arbitrary")),
    )(a, b)


def _rows(name, fn, rows_in, vecs_in, rows_out, accs_out, n_rows, tr=256):
    tr = _tile(n_rows, tr, 16)
    nr, nv, no = len(rows_in), len(vecs_in), len(rows_out)

    def body(*refs):
        ins, vecs = refs[:nr], refs[nr:nr + nv]
        outs, accs = refs[nr + nv:nr + nv + no], refs[nr + nv + no:]
        ro, ac = fn([r[...] for r in ins], [v[...] for v in vecs])
        for o_ref, val in zip(outs, ro):
            o_ref[...] = val.astype(o_ref.dtype)
        if accs:
            @pl.when(pl.program_id(0) == 0)
            def _():
                for a_ref in accs:
                    a_ref[...] = jnp.zeros_like(a_ref)

            for a_ref, val in zip(accs, ac):
                a_ref[...] += val

    in_specs = [pl.BlockSpec((tr, cols), functools.partial(lambda i, cb: (i, cb), cb=cb)) for _, cols, cb in rows_in]
    in_specs += [pl.BlockSpec(v.shape, lambda i: (0, 0)) for v in vecs_in]
    out_specs = [pl.BlockSpec((tr, cols), lambda i: (i, 0)) for cols, _ in rows_out]
    out_specs += [pl.BlockSpec((r, cols), lambda i: (0, 0)) for r, cols in accs_out]
    out_shape = [jax.ShapeDtypeStruct((n_rows, cols), dt) for cols, dt in rows_out]
    out_shape += [jax.ShapeDtypeStruct((r, cols), F32) for r, cols in accs_out]
    res = pl.pallas_call(
        body, name=name, grid=(n_rows // tr,), in_specs=in_specs, out_specs=out_specs, out_shape=out_shape,
        compiler_params=_params(("arbitrary",)),
    )(*[a for a, _, _ in rows_in], *vecs_in)
    return res


def _colsum(v):
    return jnp.sum(v, axis=0, keepdims=True)


def _rstd(x):
    return lax.rsqrt(jnp.mean(x * x, axis=-1, keepdims=True) + EPS)


def _rms_bwd(x, g, dy):
    r = _rstd(x)
    xh = x * r
    dxh = dy * g
    dx = r * (dxh - xh * jnp.mean(dxh * xh, axis=-1, keepdims=True))
    return dx, _colsum(dy * xh)


def _sigmoid(z):
    return 1.0 / (1.0 + jnp.exp(-z))


_GELU_K = math.sqrt(2.0 / math.pi)


def _gelu_parts(g):
    t = jnp.tanh(_GELU_K * (g + 0.044715 * g * g * g))
    gel = 0.5 * g * (1.0 + t)
    dgel = 0.5 * (1.0 + t) + 0.5 * g * (1.0 - t * t) * (_GELU_K * (1.0 + 3.0 * 0.044715 * g * g))
    return gel, dgel


def _diag_visible(t, unit):
    rows = lax.broadcasted_iota(jnp.int32, (t, t), 0)
    cols = lax.broadcasted_iota(jnp.int32, (t, t), 1)
    if unit > 1:
        sh = int(math.log2(unit))
        assert 1 << sh == unit and t % unit == 0
        rows, cols = jnp.right_shift(rows, sh), jnp.right_shift(cols, sh)
    return cols <= rows


def _lane_pick(tile, lane):
    idx = lax.broadcasted_iota(jnp.int32, tile.shape, 1)
    return jnp.sum(jnp.where(idx == lane, tile, 0.0), axis=1, keepdims=True)


def _lane_put(tile, lane, col):
    idx = lax.broadcasted_iota(jnp.int32, tile.shape, 1)
    return jnp.where(idx == lane, col, tile)


def _head_cat(refs, shared, rows, h):
    hs = slice(h * LANES, (h + 1) * LANES)
    vals = [(r[rows, :] if sh else r[rows, hs]).astype(BF16) for r, sh in zip(refs, shared)]
    return vals[0] if len(vals) == 1 else jnp.concatenate(vals, axis=1)


def _blk_rows(i, t):
    return pl.ds(pl.multiple_of(i * t, t), t)


def _piece_specs(pieces, rows, row_idx):
    return [pl.BlockSpec((rows, LANES if sh else HEADS * LANES), functools.partial(lambda b, i, cb: (row_idx(b, i), cb), cb=cb))
            for _, cb, sh in pieces]


def _attn_fwd(name, qp, kp, vp, bias, unit, scale, n_seq, seq, t, side=None):
    nb = seq // t
    n_tok = n_seq * seq
    nq, nk_p = len(qp), len(kp)
    q_sh, k_sh = [p[2] for p in qp], [p[2] for p in kp]
    nbias = 2 if bias is not None else 0

    def body(*refs):
        q_refs, k_refs = refs[:nq], refs[nq:nq + nk_p]
        v_ref = refs[nq + nk_p]
        bias_refs = refs[nq + nk_p + 1:nq + nk_p + 1 + nbias]
        o_ref, lse_ref = refs[nq + nk_p + 1 + nbias:]
        qi = pl.program_id(1)
        lse_tile = jnp.zeros((t, LANES), F32)
        for h in range(HEADS):
            hs = slice(h * LANES, (h + 1) * LANES)
            q = _head_cat(q_refs, q_sh, slice(None), h)
            cq = _lane_pick(bias_refs[0][...], ROPE + h) if bias is not None else None

            def block(kb, carry, diag, h=h, hs=hs, q=q, cq=cq):
                m, l, acc = carry
                rows = _blk_rows(kb, t)
                s = lax.dot_general(q, _head_cat(k_refs, k_sh, rows, h), _DIMS["nt"], preferred_element_type=F32) * scale
                if bias is not None:
                    s = s + cq - bias_refs[1][kb, h:h + 1, :]
                if diag:
                    s = jnp.where(_diag_visible(t, unit), s, NEG_INF)
                m_new = jnp.maximum(m, jnp.max(s, axis=1, keepdims=True))
                alpha = jnp.exp(m - m_new)
                p = jnp.exp(s - m_new)
                l = alpha * l + jnp.sum(p, axis=1, keepdims=True)
                acc = alpha * acc + jnp.dot(p.astype(BF16), v_ref[rows, hs].astype(BF16), preferred_element_type=F32)
                return m_new, l, acc

            init = (jnp.full((t, 1), NEG_INF, F32), jnp.zeros((t, 1), F32), jnp.zeros((t, LANES), F32))
            carry = lax.fori_loop(0, qi, lambda kb, c: block(kb, c, False), init)
            m, l, acc = block(qi, carry, True)
            o_ref[:, hs] = acc / l
            lse_tile = _lane_put(lse_tile, h, m + jnp.log(l))
        lse_ref[...] = lse_tile

    tile_row = lambda b, i: b * nb + i
    seq_row = lambda b, i: b
    lane_tile = pl.BlockSpec((t, LANES), lambda b, i: (b * nb + i, 0))
    in_specs = _piece_specs(qp, t, tile_row) + _piece_specs(kp, seq, seq_row) + _piece_specs([vp + (False,)], seq, seq_row)
    args = [p[0] for p in qp] + [p[0] for p in kp] + [vp[0]]
    if bias is not None:
        in_specs += [lane_tile, pl.BlockSpec((None, nb, HEADS, t), lambda b, i: (b, 0, 0, 0))]
        args += list(bias)
    return _hosted_call(
        body, name, (n_seq, nb), in_specs,
        [pl.BlockSpec((t, HEADS * LANES), lambda b, i: (b * nb + i, 0)), lane_tile],
        [jax.ShapeDtypeStruct((n_tok, HEADS * LANES), F32), jax.ShapeDtypeStruct((n_tok, LANES), F32)], args, side)


def _attn_bwd_dq(name, qp, kp, vp, o, do, lse, bias, unit, scale, n_seq, seq, t, side=None):
    nb = seq // t
    n_tok = n_seq * seq
    nq, nk_p = len(qp), len(kp)
    q_sh, k_sh = [p[2] for p in qp], [p[2] for p in kp]
    nbias = 2 if bias is not None else 0
    n_in = nq + nk_p + 4 + nbias

    def body(*refs):
        q_refs, k_refs = refs[:nq], refs[nq:nq + nk_p]
        v_ref, o_ref, do_ref, lse_ref = refs[nq + nk_p:nq + nk_p + 4]
        bias_refs = refs[nq + nk_p + 4:n_in]
        dq_refs = refs[n_in:n_in + nq]
        delta_ref, dob_ref = refs[n_in + nq:n_in + nq + 2]
        qi = pl.program_id(1)
        delta_tile = jnp.zeros((t, LANES), F32)
        dc_tile = jnp.zeros((t, LANES), F32)
        lse_all = lse_ref[...]
        for h in range(HEADS):
            hs = slice(h * LANES, (h + 1) * LANES)
            q = _head_cat(q_refs, q_sh, slice(None), h)
            do_f = do_ref[:, hs]
            do_b = do_f.astype(BF16)
            dob_ref[:, hs] = do_b
            delta = jnp.sum(do_f * o_ref[:, hs], axis=1, keepdims=True)
            lse = _lane_pick(lse_all, h)
            cq = _lane_pick(bias_refs[0][...], ROPE + h) if bias is not None else None

            def block(kb, carry, diag, h=h, hs=hs, q=q, cq=cq, do_b=do_b, delta=delta, lse=lse):
                dq_acc, dc_acc = carry
                rows = _blk_rows(kb, t)
                k = _head_cat(k_refs, k_sh, rows, h)
                s = lax.dot_general(q, k, _DIMS["nt"], preferred_element_type=F32) * scale
                if bias is not None:
                    s = s + cq - bias_refs[1][kb, h:h + 1, :]
                if diag:
                    s = jnp.where(_diag_visible(t, unit), s, NEG_INF)
                p = jnp.exp(s - lse)
                dp = lax.dot_general(do_b, v_ref[rows, hs].astype(BF16), _DIMS["nt"], preferred_element_type=F32)
                ds = p * (dp - delta)
                return (dq_acc + jnp.dot(ds.astype(BF16), k, preferred_element_type=F32),
                        dc_acc + jnp.sum(ds, axis=1, keepdims=True))

            init = (jnp.zeros((t, nq * LANES), F32), jnp.zeros((t, 1), F32))
            carry = lax.fori_loop(0, qi, lambda kb, c: block(kb, c, False), init)
            dq_acc, dc_acc = block(qi, carry, True)
            for n_p in range(nq):
                dq_refs[n_p][:, hs] = dq_acc[:, n_p * LANES:(n_p + 1) * LANES] * scale
            delta_tile = _lane_put(delta_tile, h, delta)
            dc_tile = _lane_put(dc_tile, ROPE + h, dc_acc)
        delta_ref[...] = delta_tile
        if bias is not None:
            refs[n_in + nq + 2][...] = dc_tile

    tile_row = lambda b, i: b * nb + i
    seq_row = lambda b, i: b
    lane_tile = pl.BlockSpec((t, LANES), lambda b, i: (b * nb + i, 0))
    head_tile = pl.BlockSpec((t, HEADS * LANES), lambda b, i: (b * nb + i, 0))
    in_specs = _piece_specs(qp, t, tile_row) + _piece_specs(kp, seq, seq_row) + _piece_specs([vp + (False,)], seq, seq_row)
    in_specs += [head_tile, head_tile, lane_tile]
    args = [p[0] for p in qp] + [p[0] for p in kp] + [vp[0], o, do, lse]
    if bias is not None:
        in_specs += [lane_tile, pl.BlockSpec((None, nb, HEADS, t), lambda b, i: (b, 0, 0, 0))]
        args += list(bias)
    out_specs = [head_tile] * nq + [lane_tile, head_tile] + ([lane_tile] if bias is not None else [])
    out_shape = [jax.ShapeDtypeStruct((n_tok, HEADS * LANES), F32)] * nq
    out_shape += [jax.ShapeDtypeStruct((n_tok, LANES), F32), jax.ShapeDtypeStruct((n_tok, HEADS * LANES), BF16)]
    if bias is not None:
        out_shape.append(jax.ShapeDtypeStruct((n_tok, LANES), F32))
    return _hosted_call(body, name, (n_seq, nb), in_specs, out_specs, out_shape, args, side)


def _attn_bwd_dkv(name, qp, kp, vp, dob, lse, delta, bias, unit, scale, n_seq, seq, t, side=None):
    nb = seq // t
    n_tok = n_seq * seq
    nq, nk_p = len(qp), len(kp)
    q_sh, k_sh = [p[2] for p in qp], [p[2] for p in kp]
    nbias = 2 if bias is not None else 0
    n_in = nq + nk_p + 4 + nbias

    def body(*refs):
        q_refs, k_refs = refs[:nq], refs[nq:nq + nk_p]
        v_ref, dob_ref, lse_ref, delta_ref = refs[nq + nk_p:nq + nk_p + 4]
        bias_refs = refs[nq + nk_p + 4:n_in]
        dk_refs = refs[n_in:n_in + nk_p]
        dv_ref = refs[n_in + nk_p]
        ki = pl.program_id(1)
        shared_acc = [jnp.zeros((t, LANES), F32) for _ in range(nk_p)]
        for h in range(HEADS):
            hs = slice(h * LANES, (h + 1) * LANES)
            k = _head_cat(k_refs, k_sh, slice(None), h)
            v = v_ref[:, hs].astype(BF16)
            ck = bias_refs[1][h:h + 1, :] if bias is not None else None

            def block(qb, carry, diag, h=h, hs=hs, k=k, v=v, ck=ck):
                dk_acc, dv_acc, dc_acc = carry
                rows = _blk_rows(qb, t)
                q = _head_cat(q_refs, q_sh, rows, h)
                s = lax.dot_general(q, k, _DIMS["nt"], preferred_element_type=F32) * scale
                if bias is not None:
                    s = s + _lane_pick(bias_refs[0][rows, :], ROPE + h) - ck
                if diag:
                    s = jnp.where(_diag_visible(t, unit), s, NEG_INF)
                p = jnp.exp(s - _lane_pick(lse_ref[rows, :], h))
                do_b = dob_ref[rows, hs]
                dp = lax.dot_general(do_b, v, _DIMS["nt"], preferred_element_type=F32)
                ds = p * (dp - _lane_pick(delta_ref[rows, :], h))
                return (dk_acc + lax.dot_general(ds.astype(BF16), q, _DIMS["tn"], preferred_element_type=F32),
                        dv_acc + lax.dot_general(p.astype(BF16), do_b, _DIMS["tn"], preferred_element_type=F32),
                        dc_acc - jnp.sum(ds, axis=0, keepdims=True))

            init = (jnp.zeros((t, nk_p * LANES), F32), jnp.zeros((t, LANES), F32), jnp.zeros((1, t), F32))
            carry = block(ki, init, True)
            dk_acc, dv_acc, dc_acc = lax.fori_loop(ki + 1, nb, lambda qb, c: block(qb, c, False), carry)
            for n_p in range(nk_p):
                part = dk_acc[:, n_p * LANES:(n_p + 1) * LANES] * scale
                if k_sh[n_p]:
                    shared_acc[n_p] = shared_acc[n_p] + part
                else:
                    dk_refs[n_p][:, hs] = part
            dv_ref[:, hs] = dv_acc
            if bias is not None:
                refs[n_in + nk_p + 1][h:h + 1, :] = dc_acc
        for n_p in range(nk_p):
            if k_sh[n_p]:
                dk_refs[n_p][...] = shared_acc[n_p]

    tile_row = lambda b, i: b * nb + i
    seq_row = lambda b, i: b
    lane_seq = pl.BlockSpec((seq, LANES), lambda b, i: (b, 0))
    head_tile = pl.BlockSpec((t, HEADS * LANES), lambda b, i: (b * nb + i, 0))
    row_tile = pl.BlockSpec((None, None, HEADS, t), lambda b, i: (b, i, 0, 0))
    in_specs = _piece_specs(qp, seq, seq_row) + _piece_specs(kp, t, tile_row) + _piece_specs([vp + (False,)], t, tile_row)
    in_specs += [pl.BlockSpec((seq, HEADS * LANES), lambda b, i: (b, 0)), lane_seq, lane_seq]
    args = [p[0] for p in qp] + [p[0] for p in kp] + [vp[0], dob, lse, delta]
    if bias is not None:
        in_specs += [lane_seq, row_tile]
        args += list(bias)
    out_specs = [pl.BlockSpec((t, LANES if sh else HEADS * LANES), lambda b, i: (b * nb + i, 0)) for sh in k_sh] + [head_tile]
    out_shape = [jax.ShapeDtypeStruct((n_tok, LANES if sh else HEADS * LANES), F32) for sh in k_sh]
    out_shape.append(jax.ShapeDtypeStruct((n_tok, HEADS * LANES), F32))
    if bias is not None:
        out_specs.append(row_tile)
        out_shape.append(jax.ShapeDtypeStruct((n_seq, nb, HEADS, t), F32))
    return _hosted_call(body, name, (n_seq, nb), in_specs, out_specs, out_shape, args, side)


def _old_attn_bwd_dq(name, qp, kp, vp, o, do, lse, bias, unit, scale, n_seq, seq, t):
    nb = seq // t
    n_tok = n_seq * seq
    nq, nk_p = len(qp), len(kp)
    nbias = 2 if bias is not None else 0
    n_in = nq + nk_p + 4 + nbias
    n_out = nq + (1 if bias is not None else 0)

    def body(*refs):
        q_refs, k_refs = refs[:nq], refs[nq:nq + nk_p]
        v_ref, o_ref, do_ref, lse_ref = refs[nq + nk_p:nq + nk_p + 4]
        bias_refs = refs[nq + nk_p + 4:n_in]
        outs = refs[n_in:n_in + n_out]
        dq_s, delta_s, dc_s = refs[n_in + n_out:]
        qi, ki = pl.program_id(2), pl.program_id(3)

        @pl.when(ki == 0)
        def _():
            dq_s[...] = jnp.zeros_like(dq_s)
            dc_s[...] = jnp.zeros_like(dc_s)
            delta_s[...] = jnp.sum(do_ref[...] * o_ref[...], axis=1, keepdims=True)

        @pl.when(ki <= qi)
        def _():
            s = _scores(q_refs, k_refs, bias_refs, qi, ki, t, unit, scale)
            p = jnp.exp(s - lse_ref[...])
            dp = lax.dot_general(do_ref[...].astype(BF16), v_ref[...].astype(BF16), _DIMS["nt"],
                                 preferred_element_type=F32)
            ds = p * (dp - delta_s[...])
            dq_s[...] += jnp.dot(ds.astype(BF16), _cat(k_refs), preferred_element_type=F32)
            dc_s[...] += jnp.sum(ds, axis=1, keepdims=True)

        @pl.when(ki == qi)
        def _():
            for n_p in range(nq):
                outs[n_p][...] = dq_s[:, n_p * LANES:(n_p + 1) * LANES] * scale
            if bias is not None:
                outs[nq][...] = dc_s[...]

    q_row = lambda b, i, j: b * nb + i
    k_row = lambda b, i, j: b * nb + jnp.minimum(j, i)
    head_q = pl.BlockSpec((t, LANES), lambda b, h, i, j: (b * nb + i, h))
    col_q = pl.BlockSpec((None, t, 1), lambda b, h, i, j: (h, b * nb + i, 0))
    in_specs = [_piece_spec(t, p, q_row) for p in qp] + [_piece_spec(t, p, k_row) for p in kp]
    in_specs += [_piece_spec(t, vp, k_row), head_q, head_q, col_q]
    args = [p[0] for p in qp] + [p[0] for p in kp] + [vp[0], o, do, lse]
    if bias is not None:
        in_specs += [col_q, pl.BlockSpec((None, 1, t), lambda b, h, i, j: (b * HEADS + h, 0, jnp.minimum(j, i)))]
        args += list(bias)
    out_specs = [head_q] * nq + ([col_q] if bias is not None else [])
    out_shape = [jax.ShapeDtypeStruct((n_tok, HEADS * LANES), F32)] * nq
    if bias is not None:
        out_shape.append(jax.ShapeDtypeStruct((HEADS, n_tok, 1), F32))
    return pl.pallas_call(
        body, name=name, grid=(n_seq, HEADS, nb, nb), in_specs=in_specs, out_specs=out_specs, out_shape=out_shape,
        scratch_shapes=[pltpu.VMEM((t, nq * LANES), F32), pltpu.VMEM((t, 1), F32), pltpu.VMEM((t, 1), F32)],
        compiler_params=_params(("parallel", "parallel", "arbitrary", "arbitrary")),
    )(*args)


def _old_attn_bwd_dkv(name, qp, kp, vp, o, do, lse, bias, unit, scale, n_seq, seq, t):
    nb = seq // t
    n_tok = n_seq * seq
    nq, nk_p = len(qp), len(kp)
    nbias = 2 if bias is not None else 0
    n_in = nq + nk_p + 4 + nbias
    n_out = nk_p + 1 + (1 if bias is not None else 0)

    def body(*refs):
        q_refs, k_refs = refs[:nq], refs[nq:nq + nk_p]
        v_ref, o_ref, do_ref, lse_ref = refs[nq + nk_p:nq + nk_p + 4]
        bias_refs = refs[nq + nk_p + 4:n_in]
        outs = refs[n_in:n_in + n_out]
        dk_s, dv_s, dc_s = refs[n_in + n_out:]
        ki, qi = pl.program_id(2), pl.program_id(3)

        @pl.when(qi == 0)
        def _():
            dk_s[...] = jnp.zeros_like(dk_s)
            dv_s[...] = jnp.zeros_like(dv_s)
            dc_s[...] = jnp.zeros_like(dc_s)

        @pl.when(qi >= ki)
        def _():
            s = _scores(q_refs, k_refs, bias_refs, qi, ki, t, unit, scale)
            p = jnp.exp(s - lse_ref[...])
            do_b = do_ref[...].astype(BF16)
            delta = jnp.sum(do_ref[...] * o_ref[...], axis=1, keepdims=True)
            dp = lax.dot_general(do_b, v_ref[...].astype(BF16), _DIMS["nt"], preferred_element_type=F32)
            ds = p * (dp - delta)
            dv_s[...] += lax.dot_general(p.astype(BF16), do_b, _DIMS["tn"], preferred_element_type=F32)
            dk_s[...] += lax.dot_general(ds.astype(BF16), _cat(q_refs), _DIMS["tn"], preferred_element_type=F32)
            dc_s[...] -= jnp.sum(ds, axis=0, keepdims=True)

        @pl.when(qi == nb - 1)
        def _():
            for n_p in range(nk_p):
                outs[n_p][...] = dk_s[:, n_p * LANES:(n_p + 1) * LANES] * scale
            outs[nk_p][...] = dv_s[...]
            if bias is not None:
                outs[nk_p + 1][...] = dc_s[...]

    q_row = lambda b, i, j: b * nb + jnp.maximum(j, i)
    k_row = lambda b, i, j: b * nb + i
    head_q = pl.BlockSpec((t, LANES), lambda b, h, i, j: (b * nb + jnp.maximum(j, i), h))
    col_q = pl.BlockSpec((None, t, 1), lambda b, h, i, j: (h, b * nb + jnp.maximum(j, i), 0))
    head_k = pl.BlockSpec((t, LANES), lambda b, h, i, j: (b * nb + i, h))
    row_k = pl.BlockSpec((None, 1, t), lambda b, h, i, j: (b * HEADS + h, 0, i))
    in_specs = [_piece_spec(t, p, q_row) for p in qp] + [_piece_spec(t, p, k_row) for p in kp]
    in_specs += [_piece_spec(t, vp, k_row), head_q, head_q, col_q]
    args = [p[0] for p in qp] + [p[0] for p in kp] + [vp[0], o, do, lse]
    if bias is not None:
        in_specs += [col_q, row_k]
        args += list(bias)
    out_specs = [head_k] * (nk_p + 1) + ([row_k] if bias is not None else [])
    out_shape = [jax.ShapeDtypeStruct((n_tok, HEADS * LANES), F32)] * (nk_p + 1)
    if bias is not None:
        out_shape.append(jax.ShapeDtypeStruct((n_seq * HEADS, 1, seq), F32))
    return pl.pallas_call(
        body, name=name, grid=(n_seq, HEADS, nb, nb), in_specs=in_specs, out_specs=out_specs, out_shape=out_shape,
        scratch_shapes=[pltpu.VMEM((t, nk_p * LANES), F32), pltpu.VMEM((t, LANES), F32), pltpu.VMEM((1, t), F32)],
        compiler_params=_params(("parallel", "parallel", "arbitrary", "arbitrary")),
    )(*args)


def _seq_cumsum(name, x, col_block, n_seq, seq, reverse, pre=None, vec=None):
    t = _tile(seq, 256, 128)
    nb = seq // t

    def body(*refs):
        x_ref = refs[0]
        vec_ref = refs[1] if vec is not None else None
        o_ref, carry = refs[-2], refs[-1]

        @pl.when(pl.program_id(1) == 0)
        def _():
            carry[...] = jnp.zeros_like(carry)

        v = x_ref[...]
        if pre is not None:
            v = pre(v, vec_ref[...])
        r = lax.broadcasted_iota(jnp.int32, (t, t), 0)
        c = lax.broadcasted_iota(jnp.int32, (t, t), 1)
        tri = jnp.where((c >= r) if reverse else (c <= r), 1.0, 0.0).astype(BF16)
        hi = v.astype(BF16)
        mid = (v - hi.astype(F32)).astype(BF16)
        lo = (v - hi.astype(F32) - mid.astype(F32)).astype(BF16)
        acc = jnp.dot(tri, hi, preferred_element_type=F32)
        acc += jnp.dot(tri, mid, preferred_element_type=F32)
        acc += jnp.dot(tri, lo, preferred_element_type=F32)
        o_ref[...] = acc + carry[...]
        carry[...] += _colsum(v)

    blk = (lambda b, i: (b * nb + nb - 1 - i)) if reverse else (lambda b, i: (b * nb + i))
    in_specs = [pl.BlockSpec((t, LANES), lambda b, i: (blk(b, i), col_block))]
    args = [x]
    if vec is not None:
        in_specs.append(pl.BlockSpec(vec.shape, lambda b, i: (0, 0)))
        args.append(vec)
    return pl.pallas_call(
        body, name=name, grid=(n_seq, nb), in_specs=in_specs,
        out_specs=pl.BlockSpec((t, LANES), lambda b, i: (blk(b, i), 0)),
        out_shape=jax.ShapeDtypeStruct((n_seq * seq, LANES), F32),
        scratch_shapes=[pltpu.VMEM((1, LANES), F32)],
        compiler_params=_params(("arbitrary", "arbitrary")),
    )(*args)


def _log_sigmoid(z):
    return -(jnp.maximum(-z, 0.0) + jnp.log(1.0 + jnp.exp(-jnp.abs(z))))


def _shift_down(u, prev_ref, n):
    out = pltpu.roll(u, n, 0)
    row = lax.broadcasted_iota(jnp.int32, u.shape, 0)
    for r in range(n):
        out = jnp.where(row == r, prev_ref[8 - n + r:8 - n + r + 1, :], out)
    return out


def _shift_up(u, next_ref, n):
    ts = u.shape[0]
    out = pltpu.roll(u, ts - n, 0)
    row = lax.broadcasted_iota(jnp.int32, u.shape, 0)
    for r in range(n):
        out = jnp.where(row == ts - n + r, next_ref[r:r + 1, :], out)
    return out


def _conv_taps(u, prev_ref, w_ref, b_ref):
    s1, s2 = _shift_down(u, prev_ref, 1), _shift_down(u, prev_ref, 2)
    return (w_ref[0:1, :] * s2 + w_ref[1:2, :] * s1 + w_ref[2:3, :] * u) + b_ref[...], s1, s2


def _conv_glu_fwd(u_il, cw_il, cb_il, n_seq, seq, wt):
    n_tok, two_f = u_il.shape
    nct = two_f // (2 * wt)
    ts = _tile(seq, 256, 8)
    ns = seq // ts

    def body(u_ref, w_ref, b_ref, a_ref, carry):
        @pl.when(pl.program_id(2) == 0)
        def _():
            carry[...] = jnp.zeros_like(carry)

        u = u_ref[...]
        uc, _, _ = _conv_taps(u, carry, w_ref, b_ref)
        gel, _ = _gelu_parts(uc[:, :wt])
        a_ref[...] = (gel * uc[:, wt:]).astype(a_ref.dtype)
        carry[...] = u[ts - 8:, :]

    return pl.pallas_call(
        body, name="conv_glu_fwd", grid=(nct, n_seq, ns),
        in_specs=[pl.BlockSpec((ts, 2 * wt), lambda j, b, s: (b * ns + s, j)),
                  pl.BlockSpec((3, 2 * wt), lambda j, b, s: (0, j)),
                  pl.BlockSpec((1, 2 * wt), lambda j, b, s: (0, j))],
        out_specs=pl.BlockSpec((ts, wt), lambda j, b, s: (b * ns + s, j)),
        out_shape=jax.ShapeDtypeStruct((n_tok, two_f // 2), BF16),
        scratch_shapes=[pltpu.VMEM((8, 2 * wt), F32)],
        compiler_params=_params(("parallel", "arbitrary", "arbitrary")),
    )(u_il, cw_il, cb_il)


def _conv_glu_bwd_pre(u_il, da, cw_il, cb_il, n_seq, seq, wt):
    n_tok, two_f = u_il.shape
    nct = two_f // (2 * wt)
    ts = _tile(seq, 256, 8)
    ns = seq // ts

    def body(u_ref, da_ref, w_ref, b_ref, d_ref, acc_ref, carry):
        first = jnp.logical_and(pl.program_id(1) == 0, pl.program_id(2) == 0)

        @pl.when(first)
        def _():
            acc_ref[...] = jnp.zeros_like(acc_ref)

        @pl.when(pl.program_id(2) == 0)
        def _():
            carry[...] = jnp.zeros_like(carry)

        u = u_ref[...]
        uc, s1, s2 = _conv_taps(u, carry, w_ref, b_ref)
        gel, dgel = _gelu_parts(uc[:, :wt])
        da_v = da_ref[...]
        d = jnp.concatenate([da_v * uc[:, wt:] * dgel, da_v * gel], axis=1)
        d_ref[...] = d
        acc_ref[0:1, :] += _colsum(d * s2)
        acc_ref[1:2, :] += _colsum(d * s1)
        acc_ref[2:3, :] += _colsum(d * u)
        acc_ref[3:4, :] += _colsum(d)
        carry[...] = u[ts - 8:, :]

    return pl.pallas_call(
        body, name="conv_glu_bwd_pre", grid=(nct, n_seq, ns),
        in_specs=[pl.BlockSpec((ts, 2 * wt), lambda j, b, s: (b * ns + s, j)),
                  pl.BlockSpec((ts, wt), lambda j, b, s: (b * ns + s, j)),
                  pl.BlockSpec((3, 2 * wt), lambda j, b, s: (0, j)),
                  pl.BlockSpec((1, 2 * wt), lambda j, b, s: (0, j))],
        out_specs=[pl.BlockSpec((ts, 2 * wt), lambda j, b, s: (b * ns + s, j)),
                   pl.BlockSpec((8, 2 * wt), lambda j, b, s: (0, j))],
        out_shape=[jax.ShapeDtypeStruct((n_tok, two_f), F32), jax.ShapeDtypeStruct((8, two_f), F32)],
        scratch_shapes=[pltpu.VMEM((8, 2 * wt), F32)],
        compiler_params=_params(("parallel", "arbitrary", "arbitrary")),
    )(u_il, da, cw_il, cb_il)


def _conv_bwd_input(d_il, cw_il, n_seq, seq, wt):
    n_tok, two_f = d_il.shape
    nct = two_f // (2 * wt)
    ts = _tile(seq, 256, 8)
    ns = seq // ts

    def body(d_ref, w_ref, o_ref, carry):
        @pl.when(pl.program_id(2) == 0)
        def _():
            carry[...] = jnp.zeros_like(carry)

        d = d_ref[...]
        o_ref[...] = (w_ref[2:3, :] * d + w_ref[1:2, :] * _shift_up(d, carry, 1)
                      + w_ref[0:1, :] * _shift_up(d, carry, 2)).astype(o_ref.dtype)
        carry[...] = d[:8, :]

    rev = lambda j, b, s: (b * ns + ns - 1 - s, j)
    return pl.pallas_call(
        body, name="conv_bwd_input", grid=(nct, n_seq, ns),
        in_specs=[pl.BlockSpec((ts, 2 * wt), rev), pl.BlockSpec((3, 2 * wt), lambda j, b, s: (0, j))],
        out_specs=pl.BlockSpec((ts, 2 * wt), rev),
        out_shape=jax.ShapeDtypeStruct((n_tok, two_f), BF16),
        scratch_shapes=[pltpu.VMEM((8, 2 * wt), F32)],
        compiler_params=_params(("parallel", "arbitrary", "arbitrary")),
    )(d_il, cw_il)


HBM = pl.BlockSpec(memory_space=pltpu.HBM)
_CHIP_FLIPS = ((1, 0), (0, 1), (1, 1))


def _place():
    x, y, c = lax.axis_index("x"), lax.axis_index("y"), lax.axis_index("c")
    return x, y, c, 2 * x + y


def _flip(v, f):
    return 1 - v if f else v


def _half_rows(c, half):
    return pl.ds(pl.multiple_of(c * half, 16), half)


def _remote(src, dst, ssem, rsem, dev):
    return pltpu.make_async_remote_copy(src_ref=src, dst_ref=dst, send_sem=ssem, recv_sem=rsem,
                                        device_id=dev, device_id_type=MESH)


def _comm_call(name, body, ins, out_shapes, n_sems):
    return pl.pallas_call(
        body, name=name, in_specs=[HBM] * len(ins), out_specs=[HBM] * len(out_shapes),
        out_shape=[pltpu.HBM(s.shape, s.dtype) for s in out_shapes],
        scratch_shapes=[pltpu.SemaphoreType.DMA((n_sems,)), pltpu.SemaphoreType.DMA((n_sems,))],
    )(*ins)


def _all_gather_weights(shards, smalls):
    side = _gather_side(shards, smalls)
    nt = len(shards) + len(smalls)

    def body(*refs):
        for part in (side.start, side.mid, side.end):
            part(refs[:nt], refs[nt:2 * nt], *refs[2 * nt:])

    res = _comm_call("all_gather_weights", body, side.ins, side.outs, side.n_sems)
    return res[:len(shards)], res[len(shards):]


def _pair_split(name, grads):
    n = len(grads)

    def body(*refs):
        src, got = refs[:n], refs[n:2 * n]
        ssem, rsem = refs[2 * n:]
        x, y, c, _ = _place()
        cps = []
        for w in range(n):
            half = grads[w].shape[1] // 2
            cp = _remote(src[w].at[:, _half_rows(1 - c, half)], got[w], ssem.at[w], rsem.at[w], (x, y, 1 - c))
            cp.start()
            cps.append(cp)
        for cp in cps:
            cp.wait()

    outs = [jax.ShapeDtypeStruct((g.shape[0], g.shape[1] // 2, g.shape[2]), g.dtype) for g in grads]
    return _comm_call(name, body, grads, outs, n)


def _chip_scatter(parts):
    side = _scatter_side(parts)

    def body(*refs):
        n = len(parts)
        side.start(refs[:n], refs[n:2 * n], *refs[2 * n:])
        side.end(refs[:n], refs[n:2 * n], *refs[2 * n:])

    return _comm_call("rs_chip_scatter", body, parts, side.outs, side.n_sems)


class _Side:
    def __init__(self, ins, outs, n_sems, start, mid, end, mid_step=None):
        self.ins, self.outs, self.n_sems = list(ins), list(outs), n_sems
        self.start, self.mid, self.end, self.mid_step = start, mid, end, mid_step


def _scatter_side(parts):
    n = len(parts)

    def copies(src, dst, ssem, rsem):
        x, y, c, _ = _place()
        out = []
        for w in range(n):
            for k, (fx, fy) in enumerate(_CHIP_FLIPS):
                px, py = _flip(x, fx), _flip(y, fy)
                out.append(_remote(src[w].at[2 * px + py], dst[w].at[k], ssem.at[w * 3 + k], rsem.at[w * 3 + k], (px, py, c)))
        return out

    def start(src, dst, ssem, rsem):
        for cp in copies(src, dst, ssem, rsem):
            cp.start()

    def end(src, dst, ssem, rsem):
        for cp in copies(src, dst, ssem, rsem):
            cp.wait()

    outs = [jax.ShapeDtypeStruct((3,) + p.shape[1:], p.dtype) for p in parts]
    return _Side(parts, outs, 3 * n, start, None, end)


def _gather_side(shards, smalls, mid_step=None):
    n, ns = len(shards), len(smalls)

    def ici(src, dst, ssem, rsem, w, k):
        x, y, c, me = _place()
        fx, fy = _CHIP_FLIPS[k]
        rows = _half_rows(c, shards[w].shape[0] // 2)
        return _remote(src[w].at[rows], dst[w].at[me, rows], ssem.at[w * 6 + k], rsem.at[w * 6 + k],
                       (_flip(x, fx), _flip(y, fy), c))

    def small(src, dst, ssem, rsem, s, k):
        x, y, c, me = _place()
        fx, fy = _CHIP_FLIPS[k]
        sem = 6 * n + 3 * s + k
        return _remote(src[n + s], dst[n + s].at[me], ssem.at[sem], rsem.at[sem], (_flip(x, fx), _flip(y, fy), c))

    def landed(dst, ssem, rsem, w, k, sender_c, sem_off):
        x, y, c, _ = _place()
        fx, fy = _CHIP_FLIPS[k]
        got = dst[w].at[2 * _flip(x, fx) + _flip(y, fy), _half_rows(sender_c, shards[w].shape[0] // 2)]
        return _remote(got, got, ssem.at[w * 6 + sem_off + k], rsem.at[w * 6 + sem_off + k], (x, y, 1 - c))

    def start(src, dst, ssem, rsem):
        for s in range(ns):
            for k in range(3):
                small(src, dst, ssem, rsem, s, k).start()
        for w in range(n):
            for k in range(3):
                ici(src, dst, ssem, rsem, w, k).start()

    def mid(src, dst, ssem, rsem):
        c = lax.axis_index("c")
        for w in range(n):
            for k in range(3):
                landed(dst, ssem, rsem, w, k, c, 0).wait_recv()
                landed(dst, ssem, rsem, w, k, c, 3).start()

    def end(src, dst, ssem, rsem):
        c = lax.axis_index("c")
        for w in range(n):
            for k in range(3):
                landed(dst, ssem, rsem, w, k, 1 - c, 3).wait_recv()
        for s in range(ns):
            for k in range(3):
                small(src, dst, ssem, rsem, s, k).wait()
        for w in range(n):
            for k in range(3):
                ici(src, dst, ssem, rsem, w, k).wait_send()
                landed(dst, ssem, rsem, w, k, c, 3).wait_send()

    outs = [jax.ShapeDtypeStruct((N_CHIPS,) + a.shape, a.dtype) for a in list(shards) + list(smalls)]
    return _Side(list(shards) + list(smalls), outs, 6 * n + 3 * ns, start, mid, end, mid_step)


def _host(body, n_in, n_out, side, n_steps, inner):
    if side is None:
        return body
    ns_in, ns_out = len(side.ins), len(side.outs)

    def wrapped(*refs):
        ins, s_ins = refs[:n_in], refs[n_in:n_in + ns_in]
        outs = refs[n_in + ns_in:n_in + ns_in + n_out]
        s_outs = refs[n_in + ns_in + n_out:n_in + ns_in + n_out + ns_out]
        sems = refs[n_in + ns_in + n_out + ns_out:]
        step = pl.program_id(0) * inner + pl.program_id(1)

        @pl.when(step == 0)
        def _():
            side.start(s_ins, s_outs, *sems)

        if side.mid is not None:
            @pl.when(step == side.mid_step)
            def _():
                side.mid(s_ins, s_outs, *sems)

        body(*ins, *outs)

        @pl.when(step == n_steps - 1)
        def _():
            side.end(s_ins, s_outs, *sems)

    return wrapped


def _hosted_call(body, name, grid, in_specs, out_specs, out_shape, args, side):
    n_in, n_out = len(in_specs), len(out_specs)
    kern = _host(body, n_in, n_out, side, grid[0] * grid[1], grid[1])
    if side is None:
        return pl.pallas_call(kern, name=name, grid=grid, in_specs=in_specs, out_specs=out_specs, out_shape=out_shape,
                              compiler_params=_params(("parallel", "arbitrary")))(*args), []
    res = pl.pallas_call(
        kern, name=name, grid=grid, in_specs=in_specs + [HBM] * len(side.ins), out_specs=out_specs + [HBM] * len(side.outs),
        out_shape=list(out_shape) + [pltpu.HBM(s.shape, s.dtype) for s in side.outs],
        scratch_shapes=[pltpu.SemaphoreType.DMA((side.n_sems,)), pltpu.SemaphoreType.DMA((side.n_sems,))],
        compiler_params=_params(("arbitrary", "arbitrary")),
    )(*args, *side.ins)
    return res[:n_out], res[n_out:]


def _pair_swap(halves):
    n = len(halves)

    def body(*refs):
        src, dst = refs[:n], refs[n:2 * n]
        ssem, rsem = refs[2 * n:]
        x, y, c, _ = _place()
        cps = []
        for w in range(n):
            cp = _remote(src[w], dst[w], ssem.at[w], rsem.at[w], (x, y, 1 - c))
            cp.start()
            cps.append(cp)
        for cp in cps:
            cp.wait()

    outs = [jax.ShapeDtypeStruct(h.shape, h.dtype) for h in halves]
    return _comm_call("rs_pair_swap", body, halves, outs, n)


def _gather_small(vec):
    def body(src, dst, ssem, rsem):
        x, y, c, _ = _place()
        me = 4 * x + 2 * y + c
        cps = []
        for r in range(1, 8):
            dev = (_flip(x, r & 4), _flip(y, r & 2), _flip(c, r & 1))
            cp = _remote(src, dst.at[me], ssem.at[r - 1], rsem.at[r - 1], dev)
            cp.start()
            cps.append(cp)
        for cp in cps:
            cp.wait()

    out = jax.ShapeDtypeStruct((8,) + vec.shape, vec.dtype)
    return _comm_call("gather_small", body, [vec], [out], 7)[0]


def _sum_slots(name, stacked, first=None):
    n, r, c = stacked.shape
    tr = _tile(r, 256, 8)

    def body(*refs):
        s_ref, o_ref = refs[-2], refs[-1]
        acc = refs[0][...].astype(F32) if first is not None else s_ref[0].astype(F32)
        for s in range(0 if first is not None else 1, n):
            acc = acc + s_ref[s].astype(F32)
        o_ref[...] = acc

    row_spec = pl.BlockSpec((tr, c), lambda i: (i, 0))
    in_specs = ([row_spec] if first is not None else []) + [pl.BlockSpec((n, tr, c), lambda i: (0, i, 0))]
    args = ([first] if first is not None else []) + [stacked]
    return pl.pallas_call(
        body, name=name, grid=(r // tr,), in_specs=in_specs, out_specs=row_spec,
        out_shape=jax.ShapeDtypeStruct((r, c), F32), compiler_params=_params(("parallel",)),
    )(*args)


def _adamw(name, w, g, m, v):
    r, c = w.shape
    tr = _tile(r, 256, 8)
    bc1, bc2 = 1.0 - ADAM_B1 ** ADAM_STEP, 1.0 - ADAM_B2 ** ADAM_STEP

    def body(w_ref, g_ref, m_ref, v_ref, d_ref, nm_ref, nv_ref):
        g_v = g_ref[...]
        nm = ADAM_B1 * m_ref[...] + (1.0 - ADAM_B1) * g_v
        nv = ADAM_B2 * v_ref[...] + (1.0 - ADAM_B2) * (g_v * g_v)
        d_ref[...] = -ADAM_LR * ((nm / bc1) / (jnp.sqrt(nv / bc2) + ADAM_EPS) + ADAM_WD * w_ref[...])
        nm_ref[...] = nm
        nv_ref[...] = nv

    spec = pl.BlockSpec((tr, c), lambda i: (i, 0))
    return pl.pallas_call(
        body, name=name, grid=(r // tr,), in_specs=[spec] * 4, out_specs=[spec] * 3,
        out_shape=[jax.ShapeDtypeStruct((r, c), F32)] * 3, compiler_params=_params(("parallel",)),
    )(w, g, m, v)


def _pad_cols(a, cols):
    return jnp.pad(a, ((0, 0), (0, cols - a.shape[1])))


def _rot_cols(w):
    h = w.shape[-1] // 2
    return jnp.concatenate([-w[..., h:], w[..., :h]], axis=-1)


def _unrot_cols(d):
    h = d.shape[-1] // 2
    return jnp.concatenate([d[..., h:], -d[..., :h]], axis=-1)


def _logical(g):
    return jnp.transpose(g, (1, 0, 2)).reshape(g.shape[1], N_CHIPS * g.shape[2])


def _chunks(a, n):
    return jnp.transpose(a.reshape(a.shape[0], N_CHIPS, n), (1, 0, 2))


def kernel(x, positions, pre_mix_norm, w_in, q_a_norm, w_uq, kv_a_norm, w_ukv, b_forget, b_gate, w_branch_mla, w_branch_fox, w_out, post_mix_norm, pre_ffn_norm, w_up, conv_w, conv_b, w_down, post_ffn_norm, loss_target, m_pre_mix_norm, m_w_in, m_q_a_norm, m_w_uq, m_kv_a_norm, m_w_ukv, m_b_forget, m_b_gate, m_w_branch_mla, m_w_branch_fox, m_w_out, m_post_mix_norm, m_pre_ffn_norm, m_w_up, m_conv_w, m_conv_b, m_w_down, m_post_ffn_norm, v_pre_mix_norm, v_w_in, v_q_a_norm, v_w_uq, v_kv_a_norm, v_w_ukv, v_b_forget, v_b_gate, v_w_branch_mla, v_w_branch_fox, v_w_out, v_post_mix_norm, v_pre_ffn_norm, v_w_up, v_conv_w, v_conv_b, v_w_down, v_post_ffn_norm):
    n_seq, seq, d = x.shape
    n_tok = n_seq * seq
    d_in = N_CHIPS * w_in.shape[1]
    two_f = N_CHIPS * w_up.shape[1]
    ff_dim = two_f // 2
    assert d_in == QL + KVL + ROPE + 3 * HEADS * FDIM + HEADS + 2 * d
    n_in_shard = w_in.shape[1]
    in_pad = -(-n_in_shard // LANES) * LANES
    hd = HEADS * LANES
    xc, yc, cc = lax.axis_index("x"), lax.axis_index("y"), lax.axis_index("c")
    chip = 2 * xc + yc
    t_attn = _tile(seq, 256, 128)

    shards = [_pad_cols(w_in, in_pad).astype(BF16), w_uq.astype(BF16), w_ukv.astype(BF16), w_branch_mla.astype(BF16),
              w_branch_fox.astype(BF16), w_out.astype(BF16), w_up.astype(BF16), w_down.astype(BF16)]
    cw8 = jnp.pad(conv_w, ((0, 5), (0, 0)))
    put_own = lambda g, s: lax.dynamic_update_slice(g, s[None], (chip, 0, 0))
    gathered, (g_cw,) = _all_gather_weights(shards[:3], [cw8])
    g_in, g_uq, g_ukv = [put_own(g, s) for g, s in zip(gathered, shards[:3])]
    g_cw = put_own(g_cw, cw8)
    n_attn_steps = n_seq * (seq // t_attn)
    side_mla = _gather_side([shards[3], shards[4], shards[5], shards[7]], [], mid_step=max(n_attn_steps - 3, 0))
    side_fox = _gather_side([shards[6]], [], mid_step=max(n_attn_steps - 2, 0))

    o_q, o_kv, o_kpe = 0, QL, QL + KVL
    o_f = o_kpe + ROPE
    o_fl = o_f + 3 * hd
    o_g = o_fl + HEADS

    def chip_cols(lo, hi):
        out = []
        while lo < hi:
            j = lo // n_in_shard
            end = min(hi, (j + 1) * n_in_shard)
            out.append((j, lo - j * n_in_shard, end - j * n_in_shard))
            lo = end
        return out

    take = lambda lo, hi: [g_in[j, :, a:b] for j, a, b in chip_cols(lo, hi)]
    w_kpe = jnp.concatenate(take(o_kpe, o_f), axis=1)
    zeros = lambda n: jnp.zeros((d, n), BF16)
    win_p = jnp.concatenate(
        take(o_g, d_in) + take(o_q, o_kpe) + [w_kpe, zeros(LANES - ROPE), _rot_cols(w_kpe)] + take(o_fl, o_g)
        + [zeros(LANES - ROPE - HEADS)] + take(o_f, o_fl), axis=1)
    n_p = win_p.shape[1]
    cb_gm, cb_gf = 0, 1
    c_lat = 2 * d
    c_kx, c_kr = c_lat + QL + KVL, c_lat + QL + KVL + LANES
    n_pa = c_kr + LANES
    assert n_p == n_pa + 3 * hd

    uq3 = _logical(g_uq).reshape(QL, HEADS, NOPE + ROPE)
    pe = uq3[:, :, NOPE:]
    pad_pe = lambda a: jnp.pad(a, ((0, 0), (0, 0), (0, LANES - ROPE))).reshape(QL, hd)
    wuq_p = jnp.concatenate([uq3[:, :, :NOPE].reshape(QL, hd), pad_pe(pe), pad_pe(_rot_cols(pe))], axis=1)
    ukv3 = _logical(g_ukv).reshape(KVL, HEADS, NOPE + VDIM)
    wukv_p = jnp.concatenate([ukv3[:, :, :NOPE].reshape(KVL, hd), ukv3[:, :, NOPE:].reshape(KVL, hd)], axis=1)

    n_bm, n_up = w_branch_mla.shape[1], w_up.shape[1]
    l_bm, l_up = _Chunked(n_bm), _Chunked(n_up)
    wt = n_up // 2
    n_ut = two_f // wt
    il = lambda cblk: jnp.where(cblk < n_ut // 2, 2 * cblk, 2 * (cblk - n_ut // 2) + 1)
    l_il = _Plain(il)
    to_il = lambda a: a.reshape(a.shape[0], 2, n_ut // 2, wt).transpose(0, 2, 1, 3).reshape(a.shape[0], two_f)
    from_il = lambda a: a.reshape(a.shape[0], n_ut // 2, 2, wt).transpose(0, 2, 1, 3).reshape(a.shape[0], two_f)

    inv_freq = 1.0 / (ROPE_THETA ** (jnp.arange(0, ROPE, 2, dtype=F32) / ROPE))
    ang = positions.astype(F32).reshape(n_tok, 1) * inv_freq
    cos, sin = jnp.cos(ang), jnp.sin(ang)
    cs = _pad_cols(jnp.concatenate([cos, cos], axis=1), LANES)
    sn = _pad_cols(jnp.concatenate([sin, sin], axis=1), LANES)

    row = lambda v: v.reshape(1, -1)
    x2 = x.reshape(n_tok, d)
    tgt = loss_target.reshape(n_tok, d)

    (h,) = _rows("rms_pre_mix", lambda r, v: ([r[0] * _rstd(r[0]) * v[0]], []),
                 [(x2, d, 0)], [row(pre_mix_norm)], [(d, BF16)], [], n_tok)
    proj = _mm("proj_in", "nn", h, win_p, n_tok, n_pa, d)
    tn_f = _tile(3 * hd, 1024, 128)
    assert n_pa % tn_f == 0
    proj_f = _mm("proj_in_fox", "nn", h, win_p, n_tok, 3 * hd, d, tn=tn_f, lb=_Plain(lambda cblk: cblk + n_pa // tn_f),
                 out_dtype=BF16)

    bf_vec = jnp.pad(row(b_forget), ((0, 0), (ROPE, LANES - ROPE - HEADS)))

    def lat_fwd(r, v):
        ql, kvl = r[0], r[1]
        return [ql * _rstd(ql) * v[0], kvl * _rstd(kvl) * v[1], r[2] * r[4] + r[3] * r[5]], []

    qn, kvn, rk = _rows("latent_norms", lat_fwd,
                        [(proj, QL, c_lat // QL), (proj, KVL, (c_lat + QL) // KVL), (proj, LANES, c_kx // LANES),
                         (proj, LANES, c_kr // LANES), (cs, LANES, 0), (sn, LANES, 0)],
                        [row(q_a_norm), row(kv_a_norm)], [(QL, BF16), (KVL, BF16), (LANES, BF16)], [], n_tok)
    q_p = _mm("q_up", "nn", qn, wuq_p, n_tok, 3 * hd, QL)
    kv_p = _mm("kv_up", "nn", kvn, wukv_p, n_tok, 2 * hd, KVL, out_dtype=BF16)

    def rope_q(r, v):
        c8, s8 = jnp.tile(r[3], (1, HEADS)), jnp.tile(r[4], (1, HEADS))
        return [r[0], r[1] * c8 + r[2] * s8], []

    q_nope, rq = _rows("rope_q", rope_q, [(q_p, hd, 0), (q_p, hd, 1), (q_p, hd, 2), (cs, LANES, 0), (sn, LANES, 0)], [],
                       [(hd, BF16), (hd, BF16)], [], n_tok)

    mla_q = [(q_nope, 0, False), (rq, 0, False)]
    mla_k = [(kv_p, 0, False), (rk, 0, True)]
    mla_v = (kv_p, 1)
    mla_scale = (NOPE + ROPE) ** -0.5
    (o_mla, lse_mla), got = _attn_fwd("mla_fwd", mla_q, mla_k, mla_v, None, CHUNK, mla_scale, n_seq, seq, t_attn, side=side_mla)
    g_bm, g_bf, g_out, g_down = [put_own(g, s) for g, s in zip(got, side_mla.ins)]
    w_down_full = g_down.reshape(ff_dim, d)
    w_out_full = g_out.reshape(d, d)

    c_run = _seq_cumsum("forget_cumsum", proj, c_kr // LANES, n_seq, seq, False,
                        pre=lambda z, b: _log_sigmoid(z + b), vec=bf_vec)
    nb_attn = seq // t_attn
    c_rowf = jnp.transpose(c_run[:, ROPE:ROPE + HEADS].reshape(n_seq, nb_attn, t_attn, HEADS), (0, 1, 3, 2))
    fox_q, fox_k, fox_v = [(proj_f, 0, False)], [(proj_f, 1, False)], (proj_f, 2)
    fox_scale = FDIM ** -0.5
    fox_bias = (c_run, c_rowf)
    (o_fox, lse_fox), got = _attn_fwd("fox_fwd", fox_q, fox_k, fox_v, fox_bias, 1, fox_scale, n_seq, seq, t_attn, side=side_fox)
    g_up = put_own(got[0], side_fox.ins[0])

    pm = _mm("branch_mla", "nn", o_mla, g_bm, n_tok, d, hd, lb=l_bm, tn=n_bm)
    pf = _mm("branch_fox", "nn", o_fox, g_bf, n_tok, d, hd, lb=l_bm, tn=n_bm)
    bg = row(b_gate)

    def merge(r, v):
        return [_sigmoid(r[0] + v[0]) * r[2] + _sigmoid(r[1] + v[1]) * r[3]], []

    (merged,) = _rows("gate_merge", merge, [(proj, d, cb_gm), (proj, d, cb_gf), (pm, d, 0), (pf, d, 0)],
                      [bg[:, :d], bg[:, d:]], [(d, BF16)], [], n_tok)
    y1 = _mm("mix_out", "nn", merged, w_out_full, n_tok, d, d)

    def resid_norm(r, v):
        x1v = r[0] + r[1] * _rstd(r[1]) * v[0]
        return [x1v, x1v * _rstd(x1v) * v[1]], []

    x1, h2 = _rows("post_mix_pre_ffn", resid_norm, [(x2, d, 0), (y1, d, 0)], [row(post_mix_norm), row(pre_ffn_norm)],
                   [(d, F32), (d, BF16)], [], n_tok)

    u_il = _mm("ffn_up", "nn", h2, g_up, n_tok, two_f, d, lb=l_up, lo=l_il, tn=wt)
    cw_il = to_il(_logical(g_cw)[:3])
    cb_il = to_il(row(conv_b))
    act = _conv_glu_fwd(u_il, cw_il, cb_il, n_seq, seq, wt)
    ff = _mm("ffn_down", "nn", act, w_down_full, n_tok, d, ff_dim)

    def final(r, v):
        x1v, ffv, tg = r
        diff = x1v + ffv * _rstd(ffv) * v[0] - tg
        dx2v = diff / d
        dffv, dg4 = _rms_bwd(ffv, v[0], dx2v)
        sq = jnp.sum(jnp.sum(diff * diff, axis=1, keepdims=True), axis=0, keepdims=True)
        return [dx2v, dffv], [dg4, jnp.broadcast_to(sq, (1, LANES))]

    dx2, dff, dg_post_ffn, sq_sum = _rows("loss_post_ffn_bwd", final, [(x1, d, 0), (ff, d, 0), (tgt, d, 0)],
                                          [row(post_ffn_norm)], [(d, F32), (d, BF16)], [(1, d), (1, LANES)], n_tok)
    dact = _mm("ffn_down_dx", "nt", dff, w_down_full, n_tok, ff_dim, d, tn=wt)
    gw_down = _mm("ffn_down_dw", "tn", act, dff, ff_dim, d, n_tok, tm=wt, out_dtype=BF16)
    d_il, conv_acc = _conv_glu_bwd_pre(u_il, dact, cw_il, cb_il, n_seq, seq, wt)
    du_il = _conv_bwd_input(d_il, cw_il, n_seq, seq, wt)
    dh2 = _mm("ffn_up_dx", "nt", du_il, g_up, n_tok, d, two_f, la=l_il, lb=l_up, tk=wt)
    gw_up = _mm("ffn_up_dw", "tn", h2, du_il, d, two_f, n_tok, lb=l_il, lo=l_up, tn=wt, out_dtype=BF16)

    def mid_bwd(r, v):
        x1v, y1v, dx2v, dh2v = r
        d3, dg3 = _rms_bwd(x1v, v[1], dh2v)
        dx1v = dx2v + d3
        dy1v, dg2 = _rms_bwd(y1v, v[0], dx1v)
        return [dx1v, dy1v], [dg3, dg2]

    dx1, dy1, dg_pre_ffn, dg_post_mix = _rows(
        "pre_ffn_post_mix_bwd", mid_bwd, [(x1, d, 0), (y1, d, 0), (dx2, d, 0), (dh2, d, 0)],
        [row(post_mix_norm), row(pre_ffn_norm)], [(d, F32), (d, BF16)], [(1, d), (1, d)], n_tok)
    dmerged = _mm("mix_out_dx", "nt", dy1, w_out_full, n_tok, d, d)
    gw_out = _mm("mix_out_dw", "tn", merged, dy1, d, d, n_tok, out_dtype=BF16)

    def gate_bwd(r, v):
        zm, zf, pmv, pfv, dm = r
        gm, gf = _sigmoid(zm + v[0]), _sigmoid(zf + v[1])
        dzm, dzf = dm * pmv * gm * (1.0 - gm), dm * pfv * gf * (1.0 - gf)
        return [dm * gm, dm * gf, dzm, dzf], [_colsum(dzm), _colsum(dzf)]

    dpm, dpf, dzm, dzf, dbg_m, dbg_f = _rows(
        "gate_merge_bwd", gate_bwd, [(proj, d, cb_gm), (proj, d, cb_gf), (pm, d, 0), (pf, d, 0), (dmerged, d, 0)],
        [bg[:, :d], bg[:, d:]], [(d, BF16)] * 4, [(1, d), (1, d)], n_tok)
    tk_b = min(n_bm, 512)
    do_mla = _mm("branch_mla_dx", "nt", dpm, g_bm, n_tok, hd, d, lb=l_bm, tk=tk_b)
    do_fox = _mm("branch_fox_dx", "nt", dpf, g_bf, n_tok, hd, d, lb=l_bm, tk=tk_b)
    gw_bm = _mm("branch_mla_dw", "tn", o_mla, dpm, hd, d, n_tok, lo=l_bm, tn=n_bm, out_dtype=BF16)
    gw_bf = _mm("branch_fox_dw", "tn", o_fox, dpf, hd, d, n_tok, lo=l_bm, tn=n_bm, out_dtype=BF16)

    def pair_reduce(tag, names, grads):
        theirs = _pair_split("rs_pair_split_" + tag, grads)
        parts = []
        for nm, g, b in zip(names, grads, theirs):
            a = lax.dynamic_slice_in_dim(g, cc * b.shape[1], b.shape[1], axis=1)
            r2, c2 = a.shape[0] * a.shape[1], a.shape[2]
            (p,) = _rows("rs_pair_add_" + nm, lambda r, v: ([r[0].astype(F32) + r[1].astype(F32)], []),
                         [(a.reshape(r2, c2), c2, 0), (b.reshape(r2, c2), c2, 0)], [], [(c2, BF16)], [], r2)
            parts.append(p.reshape(a.shape))
        return parts

    early = ["w_down", "w_out", "w_branch_mla", "w_branch_fox", "w_up"]
    parts_early = pair_reduce("early", early, [gw_down.reshape(N_CHIPS, ff_dim // N_CHIPS, d),
                                               gw_out.reshape(N_CHIPS, d // N_CHIPS, d), gw_bm, gw_bf, gw_up])
    side_dq, side_dkv = _scatter_side(parts_early[:4]), _scatter_side(parts_early[4:])
    (dq_nope, drq, delta_mla, dob_mla), landed_a = _attn_bwd_dq(
        "mla_bwd_dq", mla_q, mla_k, mla_v, o_mla, do_mla, lse_mla, None, CHUNK, mla_scale, n_seq, seq, t_attn, side=side_dq)
    (dk_nope, drk, dv_mla), landed_b = _attn_bwd_dkv(
        "mla_bwd_dkv", mla_q, mla_k, mla_v, dob_mla, lse_mla, delta_mla, None, CHUNK, mla_scale, n_seq, seq, t_attn,
        side=side_dkv)
    landed_early = list(landed_a) + list(landed_b)
    (dfq, delta_fox, dob_fox, dc_q), _ = _attn_bwd_dq("fox_bwd_dq", fox_q, fox_k, fox_v, o_fox, do_fox, lse_fox, fox_bias, 1,
                                                      fox_scale, n_seq, seq, t_attn)
    (dfk, dfv, dc_k), _ = _attn_bwd_dkv("fox_bwd_dkv", fox_q, fox_k, fox_v, dob_fox, lse_fox, delta_fox, fox_bias, 1, fox_scale,
                                        n_seq, seq, t_attn)
    dc_k8 = jnp.transpose(dc_k, (0, 1, 3, 2)).reshape(n_tok, HEADS)
    dc128 = dc_q + jnp.pad(dc_k8, ((0, 0), (ROPE, LANES - ROPE - HEADS)))
    dlogf = _seq_cumsum("forget_cumsum_bwd", dc128, 0, n_seq, seq, True)

    def mla_pack(r, v):
        dqn_v, drq_v, dkn_v, dv_v, drk_v, c1, s1 = r
        c8, s8 = jnp.tile(c1, (1, HEADS)), jnp.tile(s1, (1, HEADS))
        return [jnp.concatenate([dqn_v, drq_v * c8, drq_v * s8], axis=1), jnp.concatenate([dkn_v, dv_v], axis=1),
                drk_v * c1, drk_v * s1], []

    dq_p, dkv_p, dkx, dkr = _rows(
        "mla_rope_bwd", mla_pack,
        [(dq_nope, hd, 0), (drq, hd, 0), (dk_nope, hd, 0), (dv_mla, hd, 0), (drk, LANES, 0), (cs, LANES, 0), (sn, LANES, 0)],
        [], [(3 * hd, BF16), (2 * hd, BF16), (LANES, F32), (LANES, F32)], [], n_tok)
    dqn = _mm("q_up_dx", "nt", dq_p, wuq_p, n_tok, QL, 3 * hd)
    gw_uq_p = _mm("q_up_dw", "tn", qn, dq_p, QL, 3 * hd, n_tok, out_dtype=BF16)
    dkvn = _mm("kv_up_dx", "nt", dkv_p, wukv_p, n_tok, KVL, 2 * hd)
    gw_ukv_p = _mm("kv_up_dw", "tn", kvn, dkv_p, KVL, 2 * hd, n_tok, out_dtype=BF16)

    def lat_bwd(r, v):
        ql, kvl, dqn_v, dkvn_v, dkx_v, dkr_v, zblk, dlf = r
        dql, dgq = _rms_bwd(ql, v[0], dqn_v)
        dkvl, dgkv = _rms_bwd(kvl, v[1], dkvn_v)
        dfl = dlf * _sigmoid(-(zblk + v[2]))
        return [jnp.concatenate([dql, dkvl, dkx_v, dkr_v + dfl], axis=1)], [dgq, dgkv, _colsum(dfl)]

    dlat, dg_q, dg_kv, dbf = _rows(
        "latent_bwd", lat_bwd,
        [(proj, QL, c_lat // QL), (proj, KVL, (c_lat + QL) // KVL), (dqn, QL, 0), (dkvn, KVL, 0), (dkx, LANES, 0),
         (dkr, LANES, 0), (proj, LANES, c_kr // LANES), (dlogf, LANES, 0)],
        [row(q_a_norm), row(kv_a_norm), bf_vec], [(QL + KVL + 2 * LANES, BF16)], [(1, QL), (1, KVL), (1, LANES)], n_tok)
    dproj = jnp.concatenate([dzm, dzf, dlat, dfq.astype(BF16), dfk.astype(BF16), dfv.astype(BF16)], axis=1)
    dh = _mm("proj_in_dx", "nt", dproj, win_p, n_tok, d, n_p)
    gw_in_p = _mm("proj_in_dw", "tn", h, dproj, d, n_p, n_tok, out_dtype=BF16)

    def first_bwd(r, v):
        dxa, dg1 = _rms_bwd(r[0], v[0], r[1])
        return [r[2] + dxa], [dg1]

    grad_x, dg_pre_mix = _rows("pre_mix_bwd", first_bwd, [(x2, d, 0), (dh, d, 0), (dx1, d, 0)], [row(pre_mix_norm)],
                               [(d, F32)], [(1, d)], n_tok)

    f32 = lambda a: a.astype(F32)
    kr_blk = gw_in_p[:, c_kr:c_kr + LANES]
    d_kpe = (f32(gw_in_p[:, c_kx:c_kx + ROPE]) + _unrot_cols(f32(kr_blk[:, :ROPE]))).astype(BF16)
    in_pieces = [(o_q, gw_in_p, c_lat, QL + KVL), (o_kpe, d_kpe, 0, ROPE), (o_f, gw_in_p, n_pa, 3 * hd),
                 (o_fl, kr_blk, ROPE, HEADS), (o_g, gw_in_p, 0, 2 * d)]
    gc_in = []
    for j in range(N_CHIPS):
        lo, hi, cols = j * n_in_shard, (j + 1) * n_in_shard, []
        for first, arr, at, width in in_pieces:
            a, b = max(lo, first), min(hi, first + width)
            if a < b:
                cols.append(arr[:, at + a - first:at + b - first])
        cols.append(jnp.zeros((d, in_pad - n_in_shard), BF16))
        gc_in.append(jnp.concatenate(cols, axis=1))
    gc_in = jnp.stack(gc_in)
    uq_parts = [gw_uq_p[:, i * hd:(i + 1) * hd].reshape(QL, HEADS, LANES) for i in range(3)]
    d_pe = (f32(uq_parts[1][:, :, :ROPE]) + _unrot_cols(f32(uq_parts[2][:, :, :ROPE]))).astype(BF16)
    gc_uq = _chunks(jnp.concatenate([uq_parts[0], d_pe], axis=2).reshape(QL, HEADS * (NOPE + ROPE)), w_uq.shape[1])
    gc_ukv = _chunks(jnp.concatenate([gw_ukv_p[:, :hd].reshape(KVL, HEADS, NOPE), gw_ukv_p[:, hd:].reshape(KVL, HEADS, VDIM)],
                                     axis=2).reshape(KVL, HEADS * (NOPE + VDIM)), w_ukv.shape[1])
    grads = [gc_in, gc_uq, gc_ukv]

    late = ["w_in", "w_uq", "w_ukv"]
    parts_late = pair_reduce("late", late, grads)
    big = late + early
    landed = list(_chip_scatter(parts_late)) + landed_early
    halves = [_sum_slots("rs_chip_sum_" + nm, s, first=lax.dynamic_index_in_dim(p, chip, 0, keepdims=False))
              for nm, s, p in zip(big, landed, parts_late + parts_early)]
    other = _pair_swap(halves)
    full = [jnp.concatenate([jnp.where(cc == 0, a, b), jnp.where(cc == 0, b, a)], axis=0) for a, b in zip(halves, other)]
    g_big = dict(zip(big, full))
    g_big["w_in"] = g_big["w_in"][:, :n_in_shard]

    conv_acc_l = from_il(conv_acc)
    pieces = [dg_pre_mix, dg_q, dg_kv, dbf, dbg_m, dbg_f, dg_post_mix, dg_pre_ffn, conv_acc_l[3:4], dg_post_ffn,
              conv_acc_l[0:1], conv_acc_l[1:2], conv_acc_l[2:3], sq_sum]
    sizes = [p.shape[1] for p in pieces]
    flat = jnp.concatenate(pieces, axis=1)
    n_rows = -(-flat.shape[1] // (8 * LANES)) * 8
    flat = _pad_cols(flat, n_rows * LANES).reshape(n_rows, LANES)
    slots = lax.dynamic_update_slice(_gather_small(flat), flat[None], (2 * chip + cc, 0, 0))
    total = _sum_slots("small_sum", slots).reshape(1, n_rows * LANES)
    offs = [sum(sizes[:i]) for i in range(len(sizes))]
    tot = [total[0, o:o + s] for o, s in zip(offs, sizes)]
    loss = 0.5 * tot[13][0] / d
    g_small = {"pre_mix_norm": tot[0], "q_a_norm": tot[1], "kv_a_norm": tot[2], "b_forget": tot[3][ROPE:ROPE + HEADS],
               "b_gate": jnp.concatenate([tot[4], tot[5]]), "post_mix_norm": tot[6], "pre_ffn_norm": tot[7],
               "conv_b": tot[8], "post_ffn_norm": tot[9]}
    gcw_full = jnp.stack([tot[10], tot[11], tot[12]])
    g_conv_w = lax.dynamic_slice(gcw_full, (0, chip * n_up), (3, n_up))

    given = dict(pre_mix_norm=(pre_mix_norm, m_pre_mix_norm, v_pre_mix_norm), w_in=(w_in, m_w_in, v_w_in),
                 q_a_norm=(q_a_norm, m_q_a_norm, v_q_a_norm), w_uq=(w_uq, m_w_uq, v_w_uq),
                 kv_a_norm=(kv_a_norm, m_kv_a_norm, v_kv_a_norm), w_ukv=(w_ukv, m_w_ukv, v_w_ukv),
                 b_forget=(b_forget, m_b_forget, v_b_forget), b_gate=(b_gate, m_b_gate, v_b_gate),
                 w_branch_mla=(w_branch_mla, m_w_branch_mla, v_w_branch_mla),
                 w_branch_fox=(w_branch_fox, m_w_branch_fox, v_w_branch_fox), w_out=(w_out, m_w_out, v_w_out),
                 post_mix_norm=(post_mix_norm, m_post_mix_norm, v_post_mix_norm),
                 pre_ffn_norm=(pre_ffn_norm, m_pre_ffn_norm, v_pre_ffn_norm), w_up=(w_up, m_w_up, v_w_up),
                 conv_w=(conv_w, m_conv_w, v_conv_w), conv_b=(conv_b, m_conv_b, v_conv_b),
                 w_down=(w_down, m_w_down, v_w_down), post_ffn_norm=(post_ffn_norm, m_post_ffn_norm, v_post_ffn_norm))
    order = list(given)
    grad, delta, new_m, new_v = {}, {}, {}, {}
    for nm in big + ["conv_w"]:
        grad[nm] = g_big[nm] if nm in g_big else g_conv_w
        delta[nm], new_m[nm], new_v[nm] = _adamw("adamw_" + nm, given[nm][0], grad[nm], given[nm][1], given[nm][2])
    small = list(g_small)
    padded = [-(-g_small[nm].shape[0] // LANES) * LANES for nm in small]
    s_rows = -(-sum(padded) // (8 * LANES)) * 8

    def pack(vals):
        cat = jnp.concatenate([jnp.pad(a, (0, p - a.shape[0])) for a, p in zip(vals, padded)])
        return jnp.pad(cat, (0, s_rows * LANES - cat.shape[0])).reshape(s_rows, LANES)

    packed = _adamw("adamw_small", pack([given[nm][0] for nm in small]), pack([g_small[nm] for nm in small]),
                    pack([given[nm][1] for nm in small]), pack([given[nm][2] for nm in small]))
    s_offs = [sum(padded[:i]) for i in range(len(small))]
    for nm, o in zip(small, s_offs):
        n_el = g_small[nm].shape[0]
        grad[nm] = g_small[nm]
        delta[nm], new_m[nm], new_v[nm] = [p.reshape(-1)[o:o + n_el] for p in packed]
    return (loss, grad_x.reshape(n_seq, seq, d), *[grad[nm] for nm in order], *[delta[nm] for nm in order],
            *[new_m[nm] for nm in order], *[new_v[nm] for nm in order])
```

```python
import functools
import math

import jax
import jax.numpy as jnp
from jax import lax
from jax.experimental import pallas as pl
from jax.experimental.pallas import tpu as pltpu

F32, BF16 = jnp.float32, jnp.bfloat16
MESH = pl.DeviceIdType.MESH

HEADS = 8
NOPE, ROPE, VDIM = 128, 64, 128
QL, KVL = 512, 256
FDIM = 128
CHUNK = 64
ROPE_THETA = 10000.0
EPS = 1e-6
NEG_INF = -1e30
ADAM_LR, ADAM_B1, ADAM_B2, ADAM_EPS, ADAM_WD, ADAM_STEP = 0.001, 0.9, 0.999, 1e-08, 0.01, 10

VMEM_LIMIT_BYTES = 52 * 1024 * 1024
LANES = 128
N_CHIPS = 4


def _params(sem):
    return pltpu.CompilerParams(dimension_semantics=sem, vmem_limit_bytes=VMEM_LIMIT_BYTES)


def _tile(n, target, mult):
    if n <= target:
        return n
    t = (target // mult) * mult
    while t >= mult:
        if n % t == 0:
            return t
        t -= mult
    raise ValueError(f"no tile for {n} (target {target}, multiple of {mult})")


class _Plain:
    def __init__(self, perm=None):
        self.perm = perm

    def spec(self, tr, tc, rc):
        perm = self.perm

        def imap(i, j, k):
            r, c = rc(i, j, k)
            return (r, perm(c) if perm is not None else c)

        return pl.BlockSpec((tr, tc), imap)

    def shape(self, rows, cols):
        return (rows, cols)


class _Chunked:
    def __init__(self, n):
        self.n = n

    def spec(self, tr, tc, rc):
        assert self.n % tc == 0, (self.n, tc)
        per = self.n // tc

        def imap(i, j, k):
            r, c = rc(i, j, k)
            return (c // per, r, c % per)

        return pl.BlockSpec((None, tr, tc), imap)

    def shape(self, rows, cols):
        assert cols == N_CHIPS * self.n
        return (N_CHIPS, rows, self.n)


_DIMS = {"nn": (((1,), (0,)), ((), ())), "nt": (((1,), (1,)), ((), ())), "tn": (((0,), (0,)), ((), ()))}


def _mm_single(name, mode, a, b, m, n, k, tm, tn, la, lb, lo, out_dtype):
    if mode == "nn":
        a_spec = la.spec(tm, k, lambda i, j, kk: (i, 0))
        b_spec = lb.spec(k, tn, lambda i, j, kk: (0, j))
    elif mode == "nt":
        a_spec = la.spec(tm, k, lambda i, j, kk: (i, 0))
        b_spec = lb.spec(tn, k, lambda i, j, kk: (j, 0))
    else:
        a_spec = la.spec(k, tm, lambda i, j, kk: (0, i))
        b_spec = lb.spec(k, tn, lambda i, j, kk: (0, j))
    o_spec = lo.spec(tm, tn, lambda i, j, kk: (i, j))
    dims = _DIMS[mode]

    def body(a_ref, b_ref, o_ref):
        o_ref[...] = lax.dot_general(a_ref[...].astype(BF16), b_ref[...].astype(BF16), dims,
                                     preferred_element_type=F32).astype(o_ref.dtype)

    return pl.pallas_call(
        body, name=name, grid=(m // tm, n // tn, 1),
        in_specs=[a_spec, b_spec], out_specs=o_spec,
        out_shape=jax.ShapeDtypeStruct(lo.shape(m, n), out_dtype),
        compiler_params=_params(("parallel", "parallel", "arbitrary")),
    )(a, b)


def _mm(name, mode, a, b, m, n, k, *, tm=1024, tn=1024, tk=2048, la=None, lb=None, lo=None, out_dtype=F32):
    la, lb, lo = la or _Plain(), lb or _Plain(), lo or _Plain()
    tm, tn, tk = _tile(m, tm, 128), _tile(n, tn, 128), _tile(k, tk, 128)
    nk = k // tk
    if nk == 1:
        return _mm_single(name, mode, a, b, m, n, k, tm, tn, la, lb, lo, out_dtype)
    if mode == "nn":
        a_spec = la.spec(tm, tk, lambda i, j, kk: (i, kk))
        b_spec = lb.spec(tk, tn, lambda i, j, kk: (kk, j))
    elif mode == "nt":
        a_spec = la.spec(tm, tk, lambda i, j, kk: (i, kk))
        b_spec = lb.spec(tn, tk, lambda i, j, kk: (j, kk))
    else:
        a_spec = la.spec(tk, tm, lambda i, j, kk: (kk, i))
        b_spec = lb.spec(tk, tn, lambda i, j, kk: (kk, j))
    o_spec = lo.spec(tm, tn, lambda i, j, kk: (i, j))
    dims = _DIMS[mode]

    def body(a_ref, b_ref, o_ref, acc_ref):
        kk = pl.program_id(2)

        @pl.when(kk == 0)
        def _():
            acc_ref[...] = jnp.zeros_like(acc_ref)

        acc_ref[...] += lax.dot_general(a_ref[...].astype(BF16), b_ref[...].astype(BF16), dims,
                                        preferred_element_type=F32)

        @pl.when(kk == nk - 1)
        def _():
            o_ref[...] = acc_ref[...].astype(o_ref.dtype)

    return pl.pallas_call(
        body, name=name, grid=(m // tm, n // tn, nk),
        in_specs=[a_spec, b_spec], out_specs=o_spec,
        out_shape=jax.ShapeDtypeStruct(lo.shape(m, n), out_dtype),
        scratch_shapes=[pltpu.VMEM((tm, tn), F32)],
        compiler_params=_params(("parallel", "parallel", "arbitrary")),
    )(a, b)


def _rows(name, fn, rows_in, vecs_in, rows_out, accs_out, n_rows, tr=256):
    tr = _tile(n_rows, tr, 16)
    nr, nv, no = len(rows_in), len(vecs_in), len(rows_out)

    def body(*refs):
        ins, vecs = refs[:nr], refs[nr:nr + nv]
        outs, accs = refs[nr + nv:nr + nv + no], refs[nr + nv + no:]
        ro, ac = fn([r[...] for r in ins], [v[...] for v in vecs])
        for o_ref, val in zip(outs, ro):
            o_ref[...] = val.astype(o_ref.dtype)
        if accs:
            @pl.when(pl.program_id(0) == 0)
            def _():
                for a_ref in accs:
                    a_ref[...] = jnp.zeros_like(a_ref)

            for a_ref, val in zip(accs, ac):
                a_ref[...] += val

    in_specs = [pl.BlockSpec((tr, cols), functools.partial(lambda i, cb: (i, cb), cb=cb)) for _, cols, cb in rows_in]
    in_specs += [pl.BlockSpec(v.shape, lambda i: (0, 0)) for v in vecs_in]
    out_specs = [pl.BlockSpec((tr, cols), lambda i: (i, 0)) for cols, _ in rows_out]
    out_specs += [pl.BlockSpec((r, cols), lambda i: (0, 0)) for r, cols in accs_out]
    out_shape = [jax.ShapeDtypeStruct((n_rows, cols), dt) for cols, dt in rows_out]
    out_shape += [jax.ShapeDtypeStruct((r, cols), F32) for r, cols in accs_out]
    res = pl.pallas_call(
        body, name=name, grid=(n_rows // tr,), in_specs=in_specs, out_specs=out_specs, out_shape=out_shape,
        compiler_params=_params(("arbitrary",)),
    )(*[a for a, _, _ in rows_in], *vecs_in)
    return res


def _colsum(v):
    return jnp.sum(v, axis=0, keepdims=True)


def _rstd(x):
    return lax.rsqrt(jnp.mean(x * x, axis=-1, keepdims=True) + EPS)


def _rms_bwd(x, g, dy):
    r = _rstd(x)
    xh = x * r
    dxh = dy * g
    dx = r * (dxh - xh * jnp.mean(dxh * xh, axis=-1, keepdims=True))
    return dx, _colsum(dy * xh)


def _sigmoid(z):
    return 1.0 / (1.0 + jnp.exp(-z))


_GELU_K = math.sqrt(2.0 / math.pi)


def _gelu_parts(g):
    t = jnp.tanh(_GELU_K * (g + 0.044715 * g * g * g))
    gel = 0.5 * g * (1.0 + t)
    dgel = 0.5 * (1.0 + t) + 0.5 * g * (1.0 - t * t) * (_GELU_K * (1.0 + 3.0 * 0.044715 * g * g))
    return gel, dgel


def _diag_visible(t, unit):
    rows = lax.broadcasted_iota(jnp.int32, (t, t), 0)
    cols = lax.broadcasted_iota(jnp.int32, (t, t), 1)
    if unit > 1:
        sh = int(math.log2(unit))
        assert 1 << sh == unit and t % unit == 0
        rows, cols = jnp.right_shift(rows, sh), jnp.right_shift(cols, sh)
    return cols <= rows


def _lane_pick(tile, lane):
    idx = lax.broadcasted_iota(jnp.int32, tile.shape, 1)
    return jnp.sum(jnp.where(idx == lane, tile, 0.0), axis=1, keepdims=True)


def _lane_put(tile, lane, col):
    idx = lax.broadcasted_iota(jnp.int32, tile.shape, 1)
    return jnp.where(idx == lane, col, tile)


def _head_cat(refs, shared, rows, h):
    hs = slice(h * LANES, (h + 1) * LANES)
    vals = [(r[rows, :] if sh else r[rows, hs]).astype(BF16) for r, sh in zip(refs, shared)]
    return vals[0] if len(vals) == 1 else jnp.concatenate(vals, axis=1)


def _blk_rows(i, t):
    return pl.ds(pl.multiple_of(i * t, t), t)


def _piece_specs(pieces, rows, row_idx):
    return [pl.BlockSpec((rows, LANES if sh else HEADS * LANES), functools.partial(lambda b, i, cb: (row_idx(b, i), cb), cb=cb))
            for _, cb, sh in pieces]


def _attn_fwd(name, qp, kp, vp, bias, unit, scale, n_seq, seq, t, side=None):
    nb = seq // t
    n_tok = n_seq * seq
    nq, nk_p = len(qp), len(kp)
    q_sh, k_sh = [p[2] for p in qp], [p[2] for p in kp]
    nbias = 2 if bias is not None else 0

    def body(*refs):
        q_refs, k_refs = refs[:nq], refs[nq:nq + nk_p]
        v_ref = refs[nq + nk_p]
        bias_refs = refs[nq + nk_p + 1:nq + nk_p + 1 + nbias]
        o_ref, lse_ref = refs[nq + nk_p + 1 + nbias:]
        qi = pl.program_id(1)
        lse_tile = jnp.zeros((t, LANES), F32)
        for h in range(HEADS):
            hs = slice(h * LANES, (h + 1) * LANES)
            q = _head_cat(q_refs, q_sh, slice(None), h)
            cq = _lane_pick(bias_refs[0][...], ROPE + h) if bias is not None else None

            def block(kb, carry, diag, h=h, hs=hs, q=q, cq=cq):
                m, l, acc = carry
                rows = _blk_rows(kb, t)
                s = lax.dot_general(q, _head_cat(k_refs, k_sh, rows, h), _DIMS["nt"], preferred_element_type=F32) * scale
                if bias is not None:
                    s = s + cq - bias_refs[1][kb, h:h + 1, :]
                if diag:
                    s = jnp.where(_diag_visible(t, unit), s, NEG_INF)
                m_new = jnp.maximum(m, jnp.max(s, axis=1, keepdims=True))
                alpha = jnp.exp(m - m_new)
                p = jnp.exp(s - m_new)
                l = alpha * l + jnp.sum(p, axis=1, keepdims=True)
                acc = alpha * acc + jnp.dot(p.astype(BF16), v_ref[rows, hs].astype(BF16), preferred_element_type=F32)
                return m_new, l, acc

            init = (jnp.full((t, 1), NEG_INF, F32), jnp.zeros((t, 1), F32), jnp.zeros((t, LANES), F32))
            carry = lax.fori_loop(0, qi, lambda kb, c: block(kb, c, False), init)
            m, l, acc = block(qi, carry, True)
            o_ref[:, hs] = acc / l
            lse_tile = _lane_put(lse_tile, h, m + jnp.log(l))
        lse_ref[...] = lse_tile

    tile_row = lambda b, i: b * nb + i
    seq_row = lambda b, i: b
    lane_tile = pl.BlockSpec((t, LANES), lambda b, i: (b * nb + i, 0))
    in_specs = _piece_specs(qp, t, tile_row) + _piece_specs(kp, seq, seq_row) + _piece_specs([vp + (False,)], seq, seq_row)
    args = [p[0] for p in qp] + [p[0] for p in kp] + [vp[0]]
    if bias is not None:
        in_specs += [lane_tile, pl.BlockSpec((None, nb, HEADS, t), lambda b, i: (b, 0, 0, 0))]
        args += list(bias)
    return _hosted_call(
        body, name, (n_seq, nb), in_specs,
        [pl.BlockSpec((t, HEADS * LANES), lambda b, i: (b * nb + i, 0)), lane_tile],
        [jax.ShapeDtypeStruct((n_tok, HEADS * LANES), F32), jax.ShapeDtypeStruct((n_tok, LANES), F32)], args, side)


def _attn_bwd_dq(name, qp, kp, vp, o, do, lse, bias, unit, scale, n_seq, seq, t, side=None):
    nb = seq // t
    n_tok = n_seq * seq
    nq, nk_p = len(qp), len(kp)
    q_sh, k_sh = [p[2] for p in qp], [p[2] for p in kp]
    nbias = 2 if bias is not None else 0
    n_in = nq + nk_p + 4 + nbias

    def body(*refs):
        q_refs, k_refs = refs[:nq], refs[nq:nq + nk_p]
        v_ref, o_ref, do_ref, lse_ref = refs[nq + nk_p:nq + nk_p + 4]
        bias_refs = refs[nq + nk_p + 4:n_in]
        dq_refs = refs[n_in:n_in + nq]
        delta_ref, dob_ref = refs[n_in + nq:n_in + nq + 2]
        qi = pl.program_id(1)
        delta_tile = jnp.zeros((t, LANES), F32)
        dc_tile = jnp.zeros((t, LANES), F32)
        lse_all = lse_ref[...]
        for h in range(HEADS):
            hs = slice(h * LANES, (h + 1) * LANES)
            q = _head_cat(q_refs, q_sh, slice(None), h)
            do_f = do_ref[:, hs]
            do_b = do_f.astype(BF16)
            dob_ref[:, hs] = do_b
            delta = jnp.sum(do_f * o_ref[:, hs], axis=1, keepdims=True)
            lse = _lane_pick(lse_all, h)
            cq = _lane_pick(bias_refs[0][...], ROPE + h) if bias is not None else None

            def block(kb, carry, diag, h=h, hs=hs, q=q, cq=cq, do_b=do_b, delta=delta, lse=lse):
                dq_acc, dc_acc = carry
                rows = _blk_rows(kb, t)
                k = _head_cat(k_refs, k_sh, rows, h)
                s = lax.dot_general(q, k, _DIMS["nt"], preferred_element_type=F32) * scale
                if bias is not None:
                    s = s + cq - bias_refs[1][kb, h:h + 1, :]
                if diag:
                    s = jnp.where(_diag_visible(t, unit), s, NEG_INF)
                p = jnp.exp(s - lse)
                dp = lax.dot_general(do_b, v_ref[rows, hs].astype(BF16), _DIMS["nt"], preferred_element_type=F32)
                ds = p * (dp - delta)
                return (dq_acc + jnp.dot(ds.astype(BF16), k, preferred_element_type=F32),
                        dc_acc + jnp.sum(ds, axis=1, keepdims=True))

            init = (jnp.zeros((t, nq * LANES), F32), jnp.zeros((t, 1), F32))
            carry = lax.fori_loop(0, qi, lambda kb, c: block(kb, c, False), init)
            dq_acc, dc_acc = block(qi, carry, True)
            for n_p in range(nq):
                dq_refs[n_p][:, hs] = dq_acc[:, n_p * LANES:(n_p + 1) * LANES] * scale
            delta_tile = _lane_put(delta_tile, h, delta)
            dc_tile = _lane_put(dc_tile, ROPE + h, dc_acc)
        delta_ref[...] = delta_tile
        if bias is not None:
            refs[n_in + nq + 2][...] = dc_tile

    tile_row = lambda b, i: b * nb + i
    seq_row = lambda b, i: b
    lane_tile = pl.BlockSpec((t, LANES), lambda b, i: (b * nb + i, 0))
    head_tile = pl.BlockSpec((t, HEADS * LANES), lambda b, i: (b * nb + i, 0))
    in_specs = _piece_specs(qp, t, tile_row) + _piece_specs(kp, seq, seq_row) + _piece_specs([vp + (False,)], seq, seq_row)
    in_specs += [head_tile, head_tile, lane_tile]
    args = [p[0] for p in qp] + [p[0] for p in kp] + [vp[0], o, do, lse]
    if bias is not None:
        in_specs += [lane_tile, pl.BlockSpec((None, nb, HEADS, t), lambda b, i: (b, 0, 0, 0))]
        args += list(bias)
    out_specs = [head_tile] * nq + [lane_tile, head_tile] + ([lane_tile] if bias is not None else [])
    out_shape = [jax.ShapeDtypeStruct((n_tok, HEADS * LANES), F32)] * nq
    out_shape += [jax.ShapeDtypeStruct((n_tok, LANES), F32), jax.ShapeDtypeStruct((n_tok, HEADS * LANES), BF16)]
    if bias is not None:
        out_shape.append(jax.ShapeDtypeStruct((n_tok, LANES), F32))
    return _hosted_call(body, name, (n_seq, nb), in_specs, out_specs, out_shape, args, side)


def _attn_bwd_dkv(name, qp, kp, vp, dob, lse, delta, bias, unit, scale, n_seq, seq, t, side=None):
    nb = seq // t
    n_tok = n_seq * seq
    nq, nk_p = len(qp), len(kp)
    q_sh, k_sh = [p[2] for p in qp], [p[2] for p in kp]
    nbias = 2 if bias is not None else 0
    n_in = nq + nk_p + 4 + nbias

    def body(*refs):
        q_refs, k_refs = refs[:nq], refs[nq:nq + nk_p]
        v_ref, dob_ref, lse_ref, delta_ref = refs[nq + nk_p:nq + nk_p + 4]
        bias_refs = refs[nq + nk_p + 4:n_in]
        dk_refs = refs[n_in:n_in + nk_p]
        dv_ref = refs[n_in + nk_p]
        ki = pl.program_id(1)
        shared_acc = [jnp.zeros((t, LANES), F32) for _ in range(nk_p)]
        for h in range(HEADS):
            hs = slice(h * LANES, (h + 1) * LANES)
            k = _head_cat(k_refs, k_sh, slice(None), h)
            v = v_ref[:, hs].astype(BF16)
            ck = bias_refs[1][h:h + 1, :] if bias is not None else None

            def block(qb, carry, diag, h=h, hs=hs, k=k, v=v, ck=ck):
                dk_acc, dv_acc, dc_acc = carry
                rows = _blk_rows(qb, t)
                q = _head_cat(q_refs, q_sh, rows, h)
                s = lax.dot_general(q, k, _DIMS["nt"], preferred_element_type=F32) * scale
                if bias is not None:
                    s = s + _lane_pick(bias_refs[0][rows, :], ROPE + h) - ck
                if diag:
                    s = jnp.where(_diag_visible(t, unit), s, NEG_INF)
                p = jnp.exp(s - _lane_pick(lse_ref[rows, :], h))
                do_b = dob_ref[rows, hs]
                dp = lax.dot_general(do_b, v, _DIMS["nt"], preferred_element_type=F32)
                ds = p * (dp - _lane_pick(delta_ref[rows, :], h))
                return (dk_acc + lax.dot_general(ds.astype(BF16), q, _DIMS["tn"], preferred_element_type=F32),
                        dv_acc + lax.dot_general(p.astype(BF16), do_b, _DIMS["tn"], preferred_element_type=F32),
                        dc_acc - jnp.sum(ds, axis=0, keepdims=True))

            init = (jnp.zeros((t, nk_p * LANES), F32), jnp.zeros((t, LANES), F32), jnp.zeros((1, t), F32))
            carry = block(ki, init, True)
            dk_acc, dv_acc, dc_acc = lax.fori_loop(ki + 1, nb, lambda qb, c: block(qb, c, False), carry)
            for n_p in range(nk_p):
                part = dk_acc[:, n_p * LANES:(n_p + 1) * LANES] * scale
                if k_sh[n_p]:
                    shared_acc[n_p] = shared_acc[n_p] + part
                else:
                    dk_refs[n_p][:, hs] = part
            dv_ref[:, hs] = dv_acc
            if bias is not None:
                refs[n_in + nk_p + 1][h:h + 1, :] = dc_acc
        for n_p in range(nk_p):
            if k_sh[n_p]:
                dk_refs[n_p][...] = shared_acc[n_p]

    tile_row = lambda b, i: b * nb + i
    seq_row = lambda b, i: b
    lane_seq = pl.BlockSpec((seq, LANES), lambda b, i: (b, 0))
    head_tile = pl.BlockSpec((t, HEADS * LANES), lambda b, i: (b * nb + i, 0))
    row_tile = pl.BlockSpec((None, None, HEADS, t), lambda b, i: (b, i, 0, 0))
    in_specs = _piece_specs(qp, seq, seq_row) + _piece_specs(kp, t, tile_row) + _piece_specs([vp + (False,)], t, tile_row)
    in_specs += [pl.BlockSpec((seq, HEADS * LANES), lambda b, i: (b, 0)), lane_seq, lane_seq]
    args = [p[0] for p in qp] + [p[0] for p in kp] + [vp[0], dob, lse, delta]
    if bias is not None:
        in_specs += [lane_seq, row_tile]
        args += list(bias)
    out_specs = [pl.BlockSpec((t, LANES if sh else HEADS * LANES), lambda b, i: (b * nb + i, 0)) for sh in k_sh] + [head_tile]
    out_shape = [jax.ShapeDtypeStruct((n_tok, LANES if sh else HEADS * LANES), F32) for sh in k_sh]
    out_shape.append(jax.ShapeDtypeStruct((n_tok, HEADS * LANES), F32))
    if bias is not None:
        out_specs.append(row_tile)
        out_shape.append(jax.ShapeDtypeStruct((n_seq, nb, HEADS, t), F32))
    return _hosted_call(body, name, (n_seq, nb), in_specs, out_specs, out_shape, args, side)


def _old_attn_bwd_dq(name, qp, kp, vp, o, do, lse, bias, unit, scale, n_seq, seq, t):
    nb = seq // t
    n_tok = n_seq * seq
    nq, nk_p = len(qp), len(kp)
    nbias = 2 if bias is not None else 0
    n_in = nq + nk_p + 4 + nbias
    n_out = nq + (1 if bias is not None else 0)

    def body(*refs):
        q_refs, k_refs = refs[:nq], refs[nq:nq + nk_p]
        v_ref, o_ref, do_ref, lse_ref = refs[nq + nk_p:nq + nk_p + 4]
        bias_refs = refs[nq + nk_p + 4:n_in]
        outs = refs[n_in:n_in + n_out]
        dq_s, delta_s, dc_s = refs[n_in + n_out:]
        qi, ki = pl.program_id(2), pl.program_id(3)

        @pl.when(ki == 0)
        def _():
            dq_s[...] = jnp.zeros_like(dq_s)
            dc_s[...] = jnp.zeros_like(dc_s)
            delta_s[...] = jnp.sum(do_ref[...] * o_ref[...], axis=1, keepdims=True)

        @pl.when(ki <= qi)
        def _():
            s = _scores(q_refs, k_refs, bias_refs, qi, ki, t, unit, scale)
            p = jnp.exp(s - lse_ref[...])
            dp = lax.dot_general(do_ref[...].astype(BF16), v_ref[...].astype(BF16), _DIMS["nt"],
                                 preferred_element_type=F32)
            ds = p * (dp - delta_s[...])
            dq_s[...] += jnp.dot(ds.astype(BF16), _cat(k_refs), preferred_element_type=F32)
            dc_s[...] += jnp.sum(ds, axis=1, keepdims=True)

        @pl.when(ki == qi)
        def _():
            for n_p in range(nq):
                outs[n_p][...] = dq_s[:, n_p * LANES:(n_p + 1) * LANES] * scale
            if bias is not None:
                outs[nq][...] = dc_s[...]

    q_row = lambda b, i, j: b * nb + i
    k_row = lambda b, i, j: b * nb + jnp.minimum(j, i)
    head_q = pl.BlockSpec((t, LANES), lambda b, h, i, j: (b * nb + i, h))
    col_q = pl.BlockSpec((None, t, 1), lambda b, h, i, j: (h, b * nb + i, 0))
    in_specs = [_piece_spec(t, p, q_row) for p in qp] + [_piece_spec(t, p, k_row) for p in kp]
    in_specs += [_piece_spec(t, vp, k_row), head_q, head_q, col_q]
    args = [p[0] for p in qp] + [p[0] for p in kp] + [vp[0], o, do, lse]
    if bias is not None:
        in_specs += [col_q, pl.BlockSpec((None, 1, t), lambda b, h, i, j: (b * HEADS + h, 0, jnp.minimum(j, i)))]
        args += list(bias)
    out_specs = [head_q] * nq + ([col_q] if bias is not None else [])
    out_shape = [jax.ShapeDtypeStruct((n_tok, HEADS * LANES), F32)] * nq
    if bias is not None:
        out_shape.append(jax.ShapeDtypeStruct((HEADS, n_tok, 1), F32))
    return pl.pallas_call(
        body, name=name, grid=(n_seq, HEADS, nb, nb), in_specs=in_specs, out_specs=out_specs, out_shape=out_shape,
        scratch_shapes=[pltpu.VMEM((t, nq * LANES), F32), pltpu.VMEM((t, 1), F32), pltpu.VMEM((t, 1), F32)],
        compiler_params=_params(("parallel", "parallel", "arbitrary", "arbitrary")),
    )(*args)


def _old_attn_bwd_dkv(name, qp, kp, vp, o, do, lse, bias, unit, scale, n_seq, seq, t):
    nb = seq // t
    n_tok = n_seq * seq
    nq, nk_p = len(qp), len(kp)
    nbias = 2 if bias is not None else 0
    n_in = nq + nk_p + 4 + nbias
    n_out = nk_p + 1 + (1 if bias is not None else 0)

    def body(*refs):
        q_refs, k_refs = refs[:nq], refs[nq:nq + nk_p]
        v_ref, o_ref, do_ref, lse_ref = refs[nq + nk_p:nq + nk_p + 4]
        bias_refs = refs[nq + nk_p + 4:n_in]
        outs = refs[n_in:n_in + n_out]
        dk_s, dv_s, dc_s = refs[n_in + n_out:]
        ki, qi = pl.program_id(2), pl.program_id(3)

        @pl.when(qi == 0)
        def _():
            dk_s[...] = jnp.zeros_like(dk_s)
            dv_s[...] = jnp.zeros_like(dv_s)
            dc_s[...] = jnp.zeros_like(dc_s)

        @pl.when(qi >= ki)
        def _():
            s = _scores(q_refs, k_refs, bias_refs, qi, ki, t, unit, scale)
            p = jnp.exp(s - lse_ref[...])
            do_b = do_ref[...].astype(BF16)
            delta = jnp.sum(do_ref[...] * o_ref[...], axis=1, keepdims=True)
            dp = lax.dot_general(do_b, v_ref[...].astype(BF16), _DIMS["nt"], preferred_element_type=F32)
            ds = p * (dp - delta)
            dv_s[...] += lax.dot_general(p.astype(BF16), do_b, _DIMS["tn"], preferred_element_type=F32)
            dk_s[...] += lax.dot_general(ds.astype(BF16), _cat(q_refs), _DIMS["tn"], preferred_element_type=F32)
            dc_s[...] -= jnp.sum(ds, axis=0, keepdims=True)

        @pl.when(qi == nb - 1)
        def _():
            for n_p in range(nk_p):
                outs[n_p][...] = dk_s[:, n_p * LANES:(n_p + 1) * LANES] * scale
            outs[nk_p][...] = dv_s[...]
            if bias is not None:
                outs[nk_p + 1][...] = dc_s[...]

    q_row = lambda b, i, j: b * nb + jnp.maximum(j, i)
    k_row = lambda b, i, j: b * nb + i
    head_q = pl.BlockSpec((t, LANES), lambda b, h, i, j: (b * nb + jnp.maximum(j, i), h))
    col_q = pl.BlockSpec((None, t, 1), lambda b, h, i, j: (h, b * nb + jnp.maximum(j, i), 0))
    head_k = pl.BlockSpec((t, LANES), lambda b, h, i, j: (b * nb + i, h))
    row_k = pl.BlockSpec((None, 1, t), lambda b, h, i, j: (b * HEADS + h, 0, i))
    in_specs = [_piece_spec(t, p, q_row) for p in qp] + [_piece_spec(t, p, k_row) for p in kp]
    in_specs += [_piece_spec(t, vp, k_row), head_q, head_q, col_q]
    args = [p[0] for p in qp] + [p[0] for p in kp] + [vp[0], o, do, lse]
    if bias is not None:
        in_specs += [col_q, row_k]
        args += list(bias)
    out_specs = [head_k] * (nk_p + 1) + ([row_k] if bias is not None else [])
    out_shape = [jax.ShapeDtypeStruct((n_tok, HEADS * LANES), F32)] * (nk_p + 1)
    if bias is not None:
        out_shape.append(jax.ShapeDtypeStruct((n_seq * HEADS, 1, seq), F32))
    return pl.pallas_call(
        body, name=name, grid=(n_seq, HEADS, nb, nb), in_specs=in_specs, out_specs=out_specs, out_shape=out_shape,
        scratch_shapes=[pltpu.VMEM((t, nk_p * LANES), F32), pltpu.VMEM((t, LANES), F32), pltpu.VMEM((1, t), F32)],
        compiler_params=_params(("parallel", "parallel", "arbitrary", "arbitrary")),
    )(*args)


def _seq_cumsum(name, x, col_block, n_seq, seq, reverse, pre=None, vec=None):
    t = _tile(seq, 256, 128)
    nb = seq // t

    def body(*refs):
        x_ref = refs[0]
        vec_ref = refs[1] if vec is not None else None
        o_ref, carry = refs[-2], refs[-1]

        @pl.when(pl.program_id(1) == 0)
        def _():
            carry[...] = jnp.zeros_like(carry)

        v = x_ref[...]
        if pre is not None:
            v = pre(v, vec_ref[...])
        r = lax.broadcasted_iota(jnp.int32, (t, t), 0)
        c = lax.broadcasted_iota(jnp.int32, (t, t), 1)
        tri = jnp.where((c >= r) if reverse else (c <= r), 1.0, 0.0).astype(BF16)
        hi = v.astype(BF16)
        mid = (v - hi.astype(F32)).astype(BF16)
        lo = (v - hi.astype(F32) - mid.astype(F32)).astype(BF16)
        acc = jnp.dot(tri, hi, preferred_element_type=F32)
        acc += jnp.dot(tri, mid, preferred_element_type=F32)
        acc += jnp.dot(tri, lo, preferred_element_type=F32)
        o_ref[...] = acc + carry[...]
        carry[...] += _colsum(v)

    blk = (lambda b, i: (b * nb + nb - 1 - i)) if reverse else (lambda b, i: (b * nb + i))
    in_specs = [pl.BlockSpec((t, LANES), lambda b, i: (blk(b, i), col_block))]
    args = [x]
    if vec is not None:
        in_specs.append(pl.BlockSpec(vec.shape, lambda b, i: (0, 0)))
        args.append(vec)
    return pl.pallas_call(
        body, name=name, grid=(n_seq, nb), in_specs=in_specs,
        out_specs=pl.BlockSpec((t, LANES), lambda b, i: (blk(b, i), 0)),
        out_shape=jax.ShapeDtypeStruct((n_seq * seq, LANES), F32),
        scratch_shapes=[pltpu.VMEM((1, LANES), F32)],
        compiler_params=_params(("arbitrary", "arbitrary")),
    )(*args)


def _log_sigmoid(z):
    return -(jnp.maximum(-z, 0.0) + jnp.log(1.0 + jnp.exp(-jnp.abs(z))))


def _shift_down(u, prev_ref, n):
    out = pltpu.roll(u, n, 0)
    row = lax.broadcasted_iota(jnp.int32, u.shape, 0)
    for r in range(n):
        out = jnp.where(row == r, prev_ref[8 - n + r:8 - n + r + 1, :], out)
    return out


def _shift_up(u, next_ref, n):
    ts = u.shape[0]
    out = pltpu.roll(u, ts - n, 0)
    row = lax.broadcasted_iota(jnp.int32, u.shape, 0)
    for r in range(n):
        out = jnp.where(row == ts - n + r, next_ref[r:r + 1, :], out)
    return out


def _conv_taps(u, prev_ref, w_ref, b_ref):
    s1, s2 = _shift_down(u, prev_ref, 1), _shift_down(u, prev_ref, 2)
    return (w_ref[0:1, :] * s2 + w_ref[1:2, :] * s1 + w_ref[2:3, :] * u) + b_ref[...], s1, s2


def _conv_glu_fwd(u_il, cw_il, cb_il, n_seq, seq, wt):
    n_tok, two_f = u_il.shape
    nct = two_f // (2 * wt)
    ts = _tile(seq, 256, 8)
    ns = seq // ts

    def body(u_ref, w_ref, b_ref, a_ref, carry):
        @pl.when(pl.program_id(2) == 0)
        def _():
            carry[...] = jnp.zeros_like(carry)

        u = u_ref[...]
        uc, _, _ = _conv_taps(u, carry, w_ref, b_ref)
        gel, _ = _gelu_parts(uc[:, :wt])
        a_ref[...] = (gel * uc[:, wt:]).astype(a_ref.dtype)
        carry[...] = u[ts - 8:, :]

    return pl.pallas_call(
        body, name="conv_glu_fwd", grid=(nct, n_seq, ns),
        in_specs=[pl.BlockSpec((ts, 2 * wt), lambda j, b, s: (b * ns + s, j)),
                  pl.BlockSpec((3, 2 * wt), lambda j, b, s: (0, j)),
                  pl.BlockSpec((1, 2 * wt), lambda j, b, s: (0, j))],
        out_specs=pl.BlockSpec((ts, wt), lambda j, b, s: (b * ns + s, j)),
        out_shape=jax.ShapeDtypeStruct((n_tok, two_f // 2), BF16),
        scratch_shapes=[pltpu.VMEM((8, 2 * wt), F32)],
        compiler_params=_params(("parallel", "arbitrary", "arbitrary")),
    )(u_il, cw_il, cb_il)


def _conv_glu_bwd_pre(u_il, da, cw_il, cb_il, n_seq, seq, wt):
    n_tok, two_f = u_il.shape
    nct = two_f // (2 * wt)
    ts = _tile(seq, 256, 8)
    ns = seq // ts

    def body(u_ref, da_ref, w_ref, b_ref, d_ref, acc_ref, carry):
        first = jnp.logical_and(pl.program_id(1) == 0, pl.program_id(2) == 0)

        @pl.when(first)
        def _():
            acc_ref[...] = jnp.zeros_like(acc_ref)

        @pl.when(pl.program_id(2) == 0)
        def _():
            carry[...] = jnp.zeros_like(carry)

        u = u_ref[...]
        uc, s1, s2 = _conv_taps(u, carry, w_ref, b_ref)
        gel, dgel = _gelu_parts(uc[:, :wt])
        da_v = da_ref[...]
        d = jnp.concatenate([da_v * uc[:, wt:] * dgel, da_v * gel], axis=1)
        d_ref[...] = d
        acc_ref[0:1, :] += _colsum(d * s2)
        acc_ref[1:2, :] += _colsum(d * s1)
        acc_ref[2:3, :] += _colsum(d * u)
        acc_ref[3:4, :] += _colsum(d)
        carry[...] = u[ts - 8:, :]

    return pl.pallas_call(
        body, name="conv_glu_bwd_pre", grid=(nct, n_seq, ns),
        in_specs=[pl.BlockSpec((ts, 2 * wt), lambda j, b, s: (b * ns + s, j)),
                  pl.BlockSpec((ts, wt), lambda j, b, s: (b * ns + s, j)),
                  pl.BlockSpec((3, 2 * wt), lambda j, b, s: (0, j)),
                  pl.BlockSpec((1, 2 * wt), lambda j, b, s: (0, j))],
        out_specs=[pl.BlockSpec((ts, 2 * wt), lambda j, b, s: (b * ns + s, j)),
                   pl.BlockSpec((8, 2 * wt), lambda j, b, s: (0, j))],
        out_shape=[jax.ShapeDtypeStruct((n_tok, two_f), F32), jax.ShapeDtypeStruct((8, two_f), F32)],
        scratch_shapes=[pltpu.VMEM((8, 2 * wt), F32)],
        compiler_params=_params(("parallel", "arbitrary", "arbitrary")),
    )(u_il, da, cw_il, cb_il)


def _conv_bwd_input(d_il, cw_il, n_seq, seq, wt):
    n_tok, two_f = d_il.shape
    nct = two_f // (2 * wt)
    ts = _tile(seq, 256, 8)
    ns = seq // ts

    def body(d_ref, w_ref, o_ref, carry):
        @pl.when(pl.program_id(2) == 0)
        def _():
            carry[...] = jnp.zeros_like(carry)

        d = d_ref[...]
        o_ref[...] = (w_ref[2:3, :] * d + w_ref[1:2, :] * _shift_up(d, carry, 1)
                      + w_ref[0:1, :] * _shift_up(d, carry, 2)).astype(o_ref.dtype)
        carry[...] = d[:8, :]

    rev = lambda j, b, s: (b * ns + ns - 1 - s, j)
    return pl.pallas_call(
        body, name="conv_bwd_input", grid=(nct, n_seq, ns),
        in_specs=[pl.BlockSpec((ts, 2 * wt), rev), pl.BlockSpec((3, 2 * wt), lambda j, b, s: (0, j))],
        out_specs=pl.BlockSpec((ts, 2 * wt), rev),
        out_shape=jax.ShapeDtypeStruct((n_tok, two_f), BF16),
        scratch_shapes=[pltpu.VMEM((8, 2 * wt), F32)],
        compiler_params=_params(("parallel", "arbitrary", "arbitrary")),
    )(d_il, cw_il)


HBM = pl.BlockSpec(memory_space=pltpu.HBM)
_CHIP_FLIPS = ((1, 0), (0, 1), (1, 1))


def _place():
    x, y, c = lax.axis_index("x"), lax.axis_index("y"), lax.axis_index("c")
    return x, y, c, 2 * x + y


def _flip(v, f):
    return 1 - v if f else v


def _half_rows(c, half):
    return pl.ds(pl.multiple_of(c * half, 16), half)


def _remote(src, dst, ssem, rsem, dev):
    return pltpu.make_async_remote_copy(src_ref=src, dst_ref=dst, send_sem=ssem, recv_sem=rsem,
                                        device_id=dev, device_id_type=MESH)


def _comm_call(name, body, ins, out_shapes, n_sems):
    return pl.pallas_call(
        body, name=name, in_specs=[HBM] * len(ins), out_specs=[HBM] * len(out_shapes),
        out_shape=[pltpu.HBM(s.shape, s.dtype) for s in out_shapes],
        scratch_shapes=[pltpu.SemaphoreType.DMA((n_sems,)), pltpu.SemaphoreType.DMA((n_sems,))],
    )(*ins)


def _all_gather_weights(shards, smalls):
    side = _gather_side(shards, smalls)
    nt = len(shards) + len(smalls)

    def body(*refs):
        for part in (side.start, side.mid, side.end):
            part(refs[:nt], refs[nt:2 * nt], *refs[2 * nt:])

    res = _comm_call("all_gather_weights", body, side.ins, side.outs, side.n_sems)
    return res[:len(shards)], res[len(shards):]


def _pair_split(name, grads):
    n = len(grads)

    def body(*refs):
        src, got = refs[:n], refs[n:2 * n]
        ssem, rsem = refs[2 * n:]
        x, y, c, _ = _place()
        cps = []
        for w in range(n):
            half = grads[w].shape[1] // 2
            cp = _remote(src[w].at[:, _half_rows(1 - c, half)], got[w], ssem.at[w], rsem.at[w], (x, y, 1 - c))
            cp.start()
            cps.append(cp)
        for cp in cps:
            cp.wait()

    outs = [jax.ShapeDtypeStruct((g.shape[0], g.shape[1] // 2, g.shape[2]), g.dtype) for g in grads]
    return _comm_call(name, body, grads, outs, n)


def _chip_scatter(parts):
    side = _scatter_side(parts)

    def body(*refs):
        n = len(parts)
        side.start(refs[:n], refs[n:2 * n], *refs[2 * n:])
        side.end(refs[:n], refs[n:2 * n], *refs[2 * n:])

    return _comm_call("rs_chip_scatter", body, parts, side.outs, side.n_sems)


class _Side:
    def __init__(self, ins, outs, n_sems, start, mid, end, mid_step=None):
        self.ins, self.outs, self.n_sems = list(ins), list(outs), n_sems
        self.start, self.mid, self.end, self.mid_step = start, mid, end, mid_step


def _scatter_side(parts):
    n = len(parts)

    def copies(src, dst, ssem, rsem):
        x, y, c, _ = _place()
        out = []
        for w in range(n):
            for k, (fx, fy) in enumerate(_CHIP_FLIPS):
                px, py = _flip(x, fx), _flip(y, fy)
                out.append(_remote(src[w].at[2 * px + py], dst[w].at[k], ssem.at[w * 3 + k], rsem.at[w * 3 + k], (px, py, c)))
        return out

    def start(src, dst, ssem, rsem):
        for cp in copies(src, dst, ssem, rsem):
            cp.start()

    def end(src, dst, ssem, rsem):
        for cp in copies(src, dst, ssem, rsem):
            cp.wait()

    outs = [jax.ShapeDtypeStruct((3,) + p.shape[1:], p.dtype) for p in parts]
    return _Side(parts, outs, 3 * n, start, None, end)


def _gather_side(shards, smalls, mid_step=None):
    n, ns = len(shards), len(smalls)

    def ici(src, dst, ssem, rsem, w, k):
        x, y, c, me = _place()
        fx, fy = _CHIP_FLIPS[k]
        rows = _half_rows(c, shards[w].shape[0] // 2)
        return _remote(src[w].at[rows], dst[w].at[me, rows], ssem.at[w * 6 + k], rsem.at[w * 6 + k],
                       (_flip(x, fx), _flip(y, fy), c))

    def small(src, dst, ssem, rsem, s, k):
        x, y, c, me = _place()
        fx, fy = _CHIP_FLIPS[k]
        sem = 6 * n + 3 * s + k
        return _remote(src[n + s], dst[n + s].at[me], ssem.at[sem], rsem.at[sem], (_flip(x, fx), _flip(y, fy), c))

    def landed(dst, ssem, rsem, w, k, sender_c, sem_off):
        x, y, c, _ = _place()
        fx, fy = _CHIP_FLIPS[k]
        got = dst[w].at[2 * _flip(x, fx) + _flip(y, fy), _half_rows(sender_c, shards[w].shape[0] // 2)]
        return _remote(got, got, ssem.at[w * 6 + sem_off + k], rsem.at[w * 6 + sem_off + k], (x, y, 1 - c))

    def start(src, dst, ssem, rsem):
        for s in range(ns):
            for k in range(3):
                small(src, dst, ssem, rsem, s, k).start()
        for w in range(n):
            for k in range(3):
                ici(src, dst, ssem, rsem, w, k).start()

    def mid(src, dst, ssem, rsem):
        c = lax.axis_index("c")
        for w in range(n):
            for k in range(3):
                landed(dst, ssem, rsem, w, k, c, 0).wait_recv()
                landed(dst, ssem, rsem, w, k, c, 3).start()

    def end(src, dst, ssem, rsem):
        c = lax.axis_index("c")
        for w in range(n):
            for k in range(3):
                landed(dst, ssem, rsem, w, k, 1 - c, 3).wait_recv()
        for s in range(ns):
            for k in range(3):
                small(src, dst, ssem, rsem, s, k).wait()
        for w in range(n):
            for k in range(3):
                ici(src, dst, ssem, rsem, w, k).wait_send()
                landed(dst, ssem, rsem, w, k, c, 3).wait_send()

    outs = [jax.ShapeDtypeStruct((N_CHIPS,) + a.shape, a.dtype) for a in list(shards) + list(smalls)]
    return _Side(list(shards) + list(smalls), outs, 6 * n + 3 * ns, start, mid, end, mid_step)


def _host(body, n_in, n_out, side, n_steps, inner):
    if side is None:
        return body
    ns_in, ns_out = len(side.ins), len(side.outs)

    def wrapped(*refs):
        ins, s_ins = refs[:n_in], refs[n_in:n_in + ns_in]
        outs = refs[n_in + ns_in:n_in + ns_in + n_out]
        s_outs = refs[n_in + ns_in + n_out:n_in + ns_in + n_out + ns_out]
        sems = refs[n_in + ns_in + n_out + ns_out:]
        step = pl.program_id(0) * inner + pl.program_id(1)

        @pl.when(step == 0)
        def _():
            side.start(s_ins, s_outs, *sems)

        if side.mid is not None:
            @pl.when(step == side.mid_step)
            def _():
                side.mid(s_ins, s_outs, *sems)

        body(*ins, *outs)

        @pl.when(step == n_steps - 1)
        def _():
            side.end(s_ins, s_outs, *sems)

    return wrapped


def _hosted_call(body, name, grid, in_specs, out_specs, out_shape, args, side):
    n_in, n_out = len(in_specs), len(out_specs)
    kern = _host(body, n_in, n_out, side, grid[0] * grid[1], grid[1])
    if side is None:
        return pl.pallas_call(kern, name=name, grid=grid, in_specs=in_specs, out_specs=out_specs, out_shape=out_shape,
                              compiler_params=_params(("parallel", "arbitrary")))(*args), []
    res = pl.pallas_call(
        kern, name=name, grid=grid, in_specs=in_specs + [HBM] * len(side.ins), out_specs=out_specs + [HBM] * len(side.outs),
        out_shape=list(out_shape) + [pltpu.HBM(s.shape, s.dtype) for s in side.outs],
        scratch_shapes=[pltpu.SemaphoreType.DMA((side.n_sems,)), pltpu.SemaphoreType.DMA((side.n_sems,))],
        compiler_params=_params(("arbitrary", "arbitrary")),
    )(*args, *side.ins)
    return res[:n_out], res[n_out:]


def _pair_swap(halves):
    n = len(halves)

    def body(*refs):
        src, dst = refs[:n], refs[n:2 * n]
        ssem, rsem = refs[2 * n:]
        x, y, c, _ = _place()
        cps = []
        for w in range(n):
            cp = _remote(src[w], dst[w], ssem.at[w], rsem.at[w], (x, y, 1 - c))
            cp.start()
            cps.append(cp)
        for cp in cps:
            cp.wait()

    outs = [jax.ShapeDtypeStruct(h.shape, h.dtype) for h in halves]
    return _comm_call("rs_pair_swap", body, halves, outs, n)


def _gather_small(vec):
    def body(src, dst, ssem, rsem):
        x, y, c, _ = _place()
        me = 4 * x + 2 * y + c
        cps = []
        for r in range(1, 8):
            dev = (_flip(x, r & 4), _flip(y, r & 2), _flip(c, r & 1))
            cp = _remote(src, dst.at[me], ssem.at[r - 1], rsem.at[r - 1], dev)
            cp.start()
            cps.append(cp)
        for cp in cps:
            cp.wait()

    out = jax.ShapeDtypeStruct((8,) + vec.shape, vec.dtype)
    return _comm_call("gather_small", body, [vec], [out], 7)[0]


def _sum_slots(name, stacked, first=None):
    n, r, c = stacked.shape
    tr = _tile(r, 256, 8)

    def body(*refs):
        s_ref, o_ref = refs[-2], refs[-1]
        acc = refs[0][...].astype(F32) if first is not None else s_ref[0].astype(F32)
        for s in range(0 if first is not None else 1, n):
            acc = acc + s_ref[s].astype(F32)
        o_ref[...] = acc

    row_spec = pl.BlockSpec((tr, c), lambda i: (i, 0))
    in_specs = ([row_spec] if first is not None else []) + [pl.BlockSpec((n, tr, c), lambda i: (0, i, 0))]
    args = ([first] if first is not None else []) + [stacked]
    return pl.pallas_call(
        body, name=name, grid=(r // tr,), in_specs=in_specs, out_specs=row_spec,
        out_shape=jax.ShapeDtypeStruct((r, c), F32), compiler_params=_params(("parallel",)),
    )(*args)


def _adamw(name, w, g, m, v):
    r, c = w.shape
    by_cols = r % 8 != 0 and c % LANES == 0
    tr, tc = (r, _tile(c, 256, LANES)) if by_cols else (_tile(r, 256, 8), c)
    bc1, bc2 = 1.0 - ADAM_B1 ** ADAM_STEP, 1.0 - ADAM_B2 ** ADAM_STEP

    def body(w_ref, g_ref, m_ref, v_ref, d_ref, nm_ref, nv_ref):
        g_v = g_ref[...]
        nm = ADAM_B1 * m_ref[...] + (1.0 - ADAM_B1) * g_v
        nv = ADAM_B2 * v_ref[...] + (1.0 - ADAM_B2) * (g_v * g_v)
        d_ref[...] = -ADAM_LR * ((nm / bc1) / (jnp.sqrt(nv / bc2) + ADAM_EPS) + ADAM_WD * w_ref[...])
        nm_ref[...] = nm
        nv_ref[...] = nv

    spec = pl.BlockSpec((tr, tc), (lambda i: (0, i)) if by_cols else (lambda i: (i, 0)))
    return pl.pallas_call(
        body, name=name, grid=(c // tc if by_cols else r // tr,), in_specs=[spec] * 4, out_specs=[spec] * 3,
        out_shape=[jax.ShapeDtypeStruct((r, c), F32)] * 3, compiler_params=_params(("parallel",)),
    )(w, g, m, v)


def _pad_cols(a, cols):
    return jnp.pad(a, ((0, 0), (0, cols - a.shape[1])))


def _rot_cols(w):
    h = w.shape[-1] // 2
    return jnp.concatenate([-w[..., h:], w[..., :h]], axis=-1)


def _unrot_cols(d):
    h = d.shape[-1] // 2
    return jnp.concatenate([d[..., h:], -d[..., :h]], axis=-1)


def _logical(g):
    return jnp.transpose(g, (1, 0, 2)).reshape(g.shape[1], N_CHIPS * g.shape[2])


def _chunks(a, n):
    return jnp.transpose(a.reshape(a.shape[0], N_CHIPS, n), (1, 0, 2))


def kernel(x, positions, pre_mix_norm, w_in, q_a_norm, w_uq, kv_a_norm, w_ukv, b_forget, b_gate, w_branch_mla, w_branch_fox, w_out, post_mix_norm, pre_ffn_norm, w_up, conv_w, conv_b, w_down, post_ffn_norm, loss_target, m_pre_mix_norm, m_w_in, m_q_a_norm, m_w_uq, m_kv_a_norm, m_w_ukv, m_b_forget, m_b_gate, m_w_branch_mla, m_w_branch_fox, m_w_out, m_post_mix_norm, m_pre_ffn_norm, m_w_up, m_conv_w, m_conv_b, m_w_down, m_post_ffn_norm, v_pre_mix_norm, v_w_in, v_q_a_norm, v_w_uq, v_kv_a_norm, v_w_ukv, v_b_forget, v_b_gate, v_w_branch_mla, v_w_branch_fox, v_w_out, v_post_mix_norm, v_pre_ffn_norm, v_w_up, v_conv_w, v_conv_b, v_w_down, v_post_ffn_norm):
    n_seq, seq, d = x.shape
    n_tok = n_seq * seq
    d_in = N_CHIPS * w_in.shape[1]
    two_f = N_CHIPS * w_up.shape[1]
    ff_dim = two_f // 2
    assert d_in == QL + KVL + ROPE + 3 * HEADS * FDIM + HEADS + 2 * d
    n_in_shard = w_in.shape[1]
    in_pad = -(-n_in_shard // LANES) * LANES
    hd = HEADS * LANES
    xc, yc, cc = lax.axis_index("x"), lax.axis_index("y"), lax.axis_index("c")
    chip = 2 * xc + yc
    t_attn = _tile(seq, 512, 128)

    shards = [_pad_cols(w_in, in_pad).astype(BF16), w_uq.astype(BF16), w_ukv.astype(BF16), w_branch_mla.astype(BF16),
              w_branch_fox.astype(BF16), w_out.astype(BF16), w_up.astype(BF16), w_down.astype(BF16)]
    cw8 = jnp.pad(conv_w, ((0, 5), (0, 0)))
    put_own = lambda g, s: lax.dynamic_update_slice(g, s[None], (chip, 0, 0))
    gathered, (g_cw,) = _all_gather_weights(shards[:3], [cw8])
    g_in, g_uq, g_ukv = [put_own(g, s) for g, s in zip(gathered, shards[:3])]
    g_cw = put_own(g_cw, cw8)
    n_attn_steps = n_seq * (seq // t_attn)
    side_mla = _gather_side([shards[3], shards[4], shards[5], shards[7]], [], mid_step=max(n_attn_steps - 2, 0))
    side_fox = _gather_side([shards[6]], [], mid_step=n_attn_steps - 1)

    o_q, o_kv, o_kpe = 0, QL, QL + KVL
    o_f = o_kpe + ROPE
    o_fl = o_f + 3 * hd
    o_g = o_fl + HEADS

    def chip_cols(lo, hi):
        out = []
        while lo < hi:
            j = lo // n_in_shard
            end = min(hi, (j + 1) * n_in_shard)
            out.append((j, lo - j * n_in_shard, end - j * n_in_shard))
            lo = end
        return out

    take = lambda lo, hi: [g_in[j, :, a:b] for j, a, b in chip_cols(lo, hi)]
    w_kpe = jnp.concatenate(take(o_kpe, o_f), axis=1)
    zeros = lambda n: jnp.zeros((d, n), BF16)
    win_p = jnp.concatenate(
        take(o_g, d_in) + take(o_q, o_kpe) + [w_kpe, zeros(LANES - ROPE), _rot_cols(w_kpe)] + take(o_fl, o_g)
        + [zeros(LANES - ROPE - HEADS)] + take(o_f, o_fl), axis=1)
    n_p = win_p.shape[1]
    cb_gm, cb_gf = 0, 1
    c_lat = 2 * d
    c_kx, c_kr = c_lat + QL + KVL, c_lat + QL + KVL + LANES
    n_pa = c_kr + LANES
    assert n_p == n_pa + 3 * hd

    uq3 = _logical(g_uq).reshape(QL, HEADS, NOPE + ROPE)
    pe = uq3[:, :, NOPE:]
    pad_pe = lambda a: jnp.pad(a, ((0, 0), (0, 0), (0, LANES - ROPE))).reshape(QL, hd)
    wuq_p = jnp.concatenate([uq3[:, :, :NOPE].reshape(QL, hd), pad_pe(pe), pad_pe(_rot_cols(pe))], axis=1)
    ukv3 = _logical(g_ukv).reshape(KVL, HEADS, NOPE + VDIM)
    wukv_p = jnp.concatenate([ukv3[:, :, :NOPE].reshape(KVL, hd), ukv3[:, :, NOPE:].reshape(KVL, hd)], axis=1)

    n_bm, n_up = w_branch_mla.shape[1], w_up.shape[1]
    l_bm, l_up = _Chunked(n_bm), _Chunked(n_up)
    wt = n_up // 2
    n_ut = two_f // wt
    il = lambda cblk: jnp.where(cblk < n_ut // 2, 2 * cblk, 2 * (cblk - n_ut // 2) + 1)
    l_il = _Plain(il)
    to_il = lambda a: a.reshape(a.shape[0], 2, n_ut // 2, wt).transpose(0, 2, 1, 3).reshape(a.shape[0], two_f)
    from_il = lambda a: a.reshape(a.shape[0], n_ut // 2, 2, wt).transpose(0, 2, 1, 3).reshape(a.shape[0], two_f)

    inv_freq = 1.0 / (ROPE_THETA ** (jnp.arange(0, ROPE, 2, dtype=F32) / ROPE))
    ang = positions.astype(F32).reshape(n_tok, 1) * inv_freq
    cos, sin = jnp.cos(ang), jnp.sin(ang)
    cs = _pad_cols(jnp.concatenate([cos, cos], axis=1), LANES)
    sn = _pad_cols(jnp.concatenate([sin, sin], axis=1), LANES)

    row = lambda v: v.reshape(1, -1)
    x2 = x.reshape(n_tok, d)
    tgt = loss_target.reshape(n_tok, d)

    (h,) = _rows("rms_pre_mix", lambda r, v: ([r[0] * _rstd(r[0]) * v[0]], []),
                 [(x2, d, 0)], [row(pre_mix_norm)], [(d, BF16)], [], n_tok)
    proj = _mm("proj_in", "nn", h, win_p, n_tok, n_pa, d)
    tn_f = _tile(3 * hd, 1024, 128)
    assert n_pa % tn_f == 0
    proj_f = _mm("proj_in_fox", "nn", h, win_p, n_tok, 3 * hd, d, tn=tn_f, lb=_Plain(lambda cblk: cblk + n_pa // tn_f),
                 out_dtype=BF16)

    bf_vec = jnp.pad(row(b_forget), ((0, 0), (ROPE, LANES - ROPE - HEADS)))

    def lat_fwd(r, v):
        ql, kvl = r[0], r[1]
        return [ql * _rstd(ql) * v[0], kvl * _rstd(kvl) * v[1], r[2] * r[4] + r[3] * r[5]], []

    qn, kvn, rk = _rows("latent_norms", lat_fwd,
                        [(proj, QL, c_lat // QL), (proj, KVL, (c_lat + QL) // KVL), (proj, LANES, c_kx // LANES),
                         (proj, LANES, c_kr // LANES), (cs, LANES, 0), (sn, LANES, 0)],
                        [row(q_a_norm), row(kv_a_norm)], [(QL, BF16), (KVL, BF16), (LANES, BF16)], [], n_tok)
    q_p = _mm("q_up", "nn", qn, wuq_p, n_tok, 3 * hd, QL)
    kv_p = _mm("kv_up", "nn", kvn, wukv_p, n_tok, 2 * hd, KVL, out_dtype=BF16)

    def rope_q(r, v):
        c8, s8 = jnp.tile(r[3], (1, HEADS)), jnp.tile(r[4], (1, HEADS))
        return [r[0], r[1] * c8 + r[2] * s8], []

    q_nope, rq = _rows("rope_q", rope_q, [(q_p, hd, 0), (q_p, hd, 1), (q_p, hd, 2), (cs, LANES, 0), (sn, LANES, 0)], [],
                       [(hd, BF16), (hd, BF16)], [], n_tok)

    mla_q = [(q_nope, 0, False), (rq, 0, False)]
    mla_k = [(kv_p, 0, False), (rk, 0, True)]
    mla_v = (kv_p, 1)
    mla_scale = (NOPE + ROPE) ** -0.5
    (o_mla, lse_mla), got = _attn_fwd("mla_fwd", mla_q, mla_k, mla_v, None, CHUNK, mla_scale, n_seq, seq, t_attn, side=side_mla)
    g_bm, g_bf, g_out, g_down = [put_own(g, s) for g, s in zip(got, side_mla.ins)]
    w_down_full = g_down.reshape(ff_dim, d)
    w_out_full = g_out.reshape(d, d)

    c_run = _seq_cumsum("forget_cumsum", proj, c_kr // LANES, n_seq, seq, False,
                        pre=lambda z, b: _log_sigmoid(z + b), vec=bf_vec)
    nb_attn = seq // t_attn
    c_rowf = jnp.transpose(c_run[:, ROPE:ROPE + HEADS].reshape(n_seq, nb_attn, t_attn, HEADS), (0, 1, 3, 2))
    fox_q, fox_k, fox_v = [(proj_f, 0, False)], [(proj_f, 1, False)], (proj_f, 2)
    fox_scale = FDIM ** -0.5
    fox_bias = (c_run, c_rowf)
    (o_fox, lse_fox), got = _attn_fwd("fox_fwd", fox_q, fox_k, fox_v, fox_bias, 1, fox_scale, n_seq, seq, t_attn, side=side_fox)
    g_up = put_own(got[0], side_fox.ins[0])

    pm = _mm("branch_mla", "nn", o_mla, g_bm, n_tok, d, hd, lb=l_bm, tn=n_bm)
    pf = _mm("branch_fox", "nn", o_fox, g_bf, n_tok, d, hd, lb=l_bm, tn=n_bm)
    bg = row(b_gate)

    def merge(r, v):
        return [_sigmoid(r[0] + v[0]) * r[2] + _sigmoid(r[1] + v[1]) * r[3]], []

    (merged,) = _rows("gate_merge", merge, [(proj, d, cb_gm), (proj, d, cb_gf), (pm, d, 0), (pf, d, 0)],
                      [bg[:, :d], bg[:, d:]], [(d, BF16)], [], n_tok)
    y1 = _mm("mix_out", "nn", merged, w_out_full, n_tok, d, d)

    def resid_norm(r, v):
        x1v = r[0] + r[1] * _rstd(r[1]) * v[0]
        return [x1v, x1v * _rstd(x1v) * v[1]], []

    x1, h2 = _rows("post_mix_pre_ffn", resid_norm, [(x2, d, 0), (y1, d, 0)], [row(post_mix_norm), row(pre_ffn_norm)],
                   [(d, F32), (d, BF16)], [], n_tok)

    u_il = _mm("ffn_up", "nn", h2, g_up, n_tok, two_f, d, lb=l_up, lo=l_il, tn=wt)
    cw_il = to_il(_logical(g_cw)[:3])
    cb_il = to_il(row(conv_b))
    act = _conv_glu_fwd(u_il, cw_il, cb_il, n_seq, seq, wt)
    ff = _mm("ffn_down", "nn", act, w_down_full, n_tok, d, ff_dim)

    def final(r, v):
        x1v, ffv, tg = r
        diff = x1v + ffv * _rstd(ffv) * v[0] - tg
        dx2v = diff / d
        dffv, dg4 = _rms_bwd(ffv, v[0], dx2v)
        sq = jnp.sum(jnp.sum(diff * diff, axis=1, keepdims=True), axis=0, keepdims=True)
        return [dx2v, dffv], [dg4, jnp.broadcast_to(sq, (1, LANES))]

    dx2, dff, dg_post_ffn, sq_sum = _rows("loss_post_ffn_bwd", final, [(x1, d, 0), (ff, d, 0), (tgt, d, 0)],
                                          [row(post_ffn_norm)], [(d, F32), (d, BF16)], [(1, d), (1, LANES)], n_tok)
    dact = _mm("ffn_down_dx", "nt", dff, w_down_full, n_tok, ff_dim, d, tn=wt)
    gw_down = _mm("ffn_down_dw", "tn", act, dff, ff_dim, d, n_tok, tm=wt, out_dtype=BF16)
    d_il, conv_acc = _conv_glu_bwd_pre(u_il, dact, cw_il, cb_il, n_seq, seq, wt)
    du_il = _conv_bwd_input(d_il, cw_il, n_seq, seq, wt)
    dh2 = _mm("ffn_up_dx", "nt", du_il, g_up, n_tok, d, two_f, la=l_il, lb=l_up, tk=wt)
    gw_up = _mm("ffn_up_dw", "tn", h2, du_il, d, two_f, n_tok, lb=l_il, lo=l_up, tn=wt, out_dtype=BF16)

    def mid_bwd(r, v):
        x1v, y1v, dx2v, dh2v = r
        d3, dg3 = _rms_bwd(x1v, v[1], dh2v)
        dx1v = dx2v + d3
        dy1v, dg2 = _rms_bwd(y1v, v[0], dx1v)
        return [dx1v, dy1v], [dg3, dg2]

    dx1, dy1, dg_pre_ffn, dg_post_mix = _rows(
        "pre_ffn_post_mix_bwd", mid_bwd, [(x1, d, 0), (y1, d, 0), (dx2, d, 0), (dh2, d, 0)],
        [row(post_mix_norm), row(pre_ffn_norm)], [(d, F32), (d, BF16)], [(1, d), (1, d)], n_tok)
    dmerged = _mm("mix_out_dx", "nt", dy1, w_out_full, n_tok, d, d)
    gw_out = _mm("mix_out_dw", "tn", merged, dy1, d, d, n_tok, out_dtype=BF16)

    def gate_bwd(r, v):
        zm, zf, pmv, pfv, dm = r
        gm, gf = _sigmoid(zm + v[0]), _sigmoid(zf + v[1])
        dzm, dzf = dm * pmv * gm * (1.0 - gm), dm * pfv * gf * (1.0 - gf)
        return [dm * gm, dm * gf, dzm, dzf], [_colsum(dzm), _colsum(dzf)]

    dpm, dpf, dzm, dzf, dbg_m, dbg_f = _rows(
        "gate_merge_bwd", gate_bwd, [(proj, d, cb_gm), (proj, d, cb_gf), (pm, d, 0), (pf, d, 0), (dmerged, d, 0)],
        [bg[:, :d], bg[:, d:]], [(d, BF16)] * 4, [(1, d), (1, d)], n_tok)
    tk_b = min(n_bm, 512)
    do_mla = _mm("branch_mla_dx", "nt", dpm, g_bm, n_tok, hd, d, lb=l_bm, tk=tk_b)
    do_fox = _mm("branch_fox_dx", "nt", dpf, g_bf, n_tok, hd, d, lb=l_bm, tk=tk_b)
    gw_bm = _mm("branch_mla_dw", "tn", o_mla, dpm, hd, d, n_tok, lo=l_bm, tn=n_bm, out_dtype=BF16)
    gw_bf = _mm("branch_fox_dw", "tn", o_fox, dpf, hd, d, n_tok, lo=l_bm, tn=n_bm, out_dtype=BF16)

    def pair_reduce(tag, names, grads):
        theirs = _pair_split("rs_pair_split_" + tag, grads)
        parts = []
        for nm, g, b in zip(names, grads, theirs):
            a = lax.dynamic_slice_in_dim(g, cc * b.shape[1], b.shape[1], axis=1)
            r2, c2 = a.shape[0] * a.shape[1], a.shape[2]
            (p,) = _rows("rs_pair_add_" + nm, lambda r, v: ([r[0].astype(F32) + r[1].astype(F32)], []),
                         [(a.reshape(r2, c2), c2, 0), (b.reshape(r2, c2), c2, 0)], [], [(c2, BF16)], [], r2)
            parts.append(p.reshape(a.shape))
        return parts

    early = ["w_down", "w_out", "w_branch_mla", "w_branch_fox", "w_up"]
    parts_early = pair_reduce("early", early, [gw_down.reshape(N_CHIPS, ff_dim // N_CHIPS, d),
                                               gw_out.reshape(N_CHIPS, d // N_CHIPS, d), gw_bm, gw_bf, gw_up])
    side_dq, side_dkv = _scatter_side(parts_early[:4]), _scatter_side(parts_early[4:])
    (dq_nope, drq, delta_mla, dob_mla), landed_a = _attn_bwd_dq(
        "mla_bwd_dq", mla_q, mla_k, mla_v, o_mla, do_mla, lse_mla, None, CHUNK, mla_scale, n_seq, seq, t_attn, side=side_dq)
    (dk_nope, drk, dv_mla), landed_b = _attn_bwd_dkv(
        "mla_bwd_dkv", mla_q, mla_k, mla_v, dob_mla, lse_mla, delta_mla, None, CHUNK, mla_scale, n_seq, seq, t_attn,
        side=side_dkv)
    landed_early = list(landed_a) + list(landed_b)
    (dfq, delta_fox, dob_fox, dc_q), _ = _attn_bwd_dq("fox_bwd_dq", fox_q, fox_k, fox_v, o_fox, do_fox, lse_fox, fox_bias, 1,
                                                      fox_scale, n_seq, seq, t_attn)
    (dfk, dfv, dc_k), _ = _attn_bwd_dkv("fox_bwd_dkv", fox_q, fox_k, fox_v, dob_fox, lse_fox, delta_fox, fox_bias, 1, fox_scale,
                                        n_seq, seq, t_attn)
    dc_k8 = jnp.transpose(dc_k, (0, 1, 3, 2)).reshape(n_tok, HEADS)
    dc128 = dc_q + jnp.pad(dc_k8, ((0, 0), (ROPE, LANES - ROPE - HEADS)))
    dlogf = _seq_cumsum("forget_cumsum_bwd", dc128, 0, n_seq, seq, True)

    def mla_pack(r, v):
        dqn_v, drq_v, dkn_v, dv_v, drk_v, c1, s1 = r
        c8, s8 = jnp.tile(c1, (1, HEADS)), jnp.tile(s1, (1, HEADS))
        return [jnp.concatenate([dqn_v, drq_v * c8, drq_v * s8], axis=1), jnp.concatenate([dkn_v, dv_v], axis=1),
                drk_v * c1, drk_v * s1], []

    dq_p, dkv_p, dkx, dkr = _rows(
        "mla_rope_bwd", mla_pack,
        [(dq_nope, hd, 0), (drq, hd, 0), (dk_nope, hd, 0), (dv_mla, hd, 0), (drk, LANES, 0), (cs, LANES, 0), (sn, LANES, 0)],
        [], [(3 * hd, BF16), (2 * hd, BF16), (LANES, F32), (LANES, F32)], [], n_tok)
    dqn = _mm("q_up_dx", "nt", dq_p, wuq_p, n_tok, QL, 3 * hd)
    gw_uq_p = _mm("q_up_dw", "tn", qn, dq_p, QL, 3 * hd, n_tok, out_dtype=BF16)
    dkvn = _mm("kv_up_dx", "nt", dkv_p, wukv_p, n_tok, KVL, 2 * hd)
    gw_ukv_p = _mm("kv_up_dw", "tn", kvn, dkv_p, KVL, 2 * hd, n_tok, out_dtype=BF16)

    def lat_bwd(r, v):
        ql, kvl, dqn_v, dkvn_v, dkx_v, dkr_v, zblk, dlf = r
        dql, dgq = _rms_bwd(ql, v[0], dqn_v)
        dkvl, dgkv = _rms_bwd(kvl, v[1], dkvn_v)
        dfl = dlf * _sigmoid(-(zblk + v[2]))
        return [jnp.concatenate([dql, dkvl, dkx_v, dkr_v + dfl], axis=1)], [dgq, dgkv, _colsum(dfl)]

    dlat, dg_q, dg_kv, dbf = _rows(
        "latent_bwd", lat_bwd,
        [(proj, QL, c_lat // QL), (proj, KVL, (c_lat + QL) // KVL), (dqn, QL, 0), (dkvn, KVL, 0), (dkx, LANES, 0),
         (dkr, LANES, 0), (proj, LANES, c_kr // LANES), (dlogf, LANES, 0)],
        [row(q_a_norm), row(kv_a_norm), bf_vec], [(QL + KVL + 2 * LANES, BF16)], [(1, QL), (1, KVL), (1, LANES)], n_tok)
    dproj = jnp.concatenate([dzm, dzf, dlat, dfq.astype(BF16), dfk.astype(BF16), dfv.astype(BF16)], axis=1)
    dh = _mm("proj_in_dx", "nt", dproj, win_p, n_tok, d, n_p)
    gw_in_p = _mm("proj_in_dw", "tn", h, dproj, d, n_p, n_tok, out_dtype=BF16)

    def first_bwd(r, v):
        dxa, dg1 = _rms_bwd(r[0], v[0], r[1])
        return [r[2] + dxa], [dg1]

    grad_x, dg_pre_mix = _rows("pre_mix_bwd", first_bwd, [(x2, d, 0), (dh, d, 0), (dx1, d, 0)], [row(pre_mix_norm)],
                               [(d, F32)], [(1, d)], n_tok)

    f32 = lambda a: a.astype(F32)
    kr_blk = gw_in_p[:, c_kr:c_kr + LANES]
    d_kpe = (f32(gw_in_p[:, c_kx:c_kx + ROPE]) + _unrot_cols(f32(kr_blk[:, :ROPE]))).astype(BF16)
    in_pieces = [(o_q, gw_in_p, c_lat, QL + KVL), (o_kpe, d_kpe, 0, ROPE), (o_f, gw_in_p, n_pa, 3 * hd),
                 (o_fl, kr_blk, ROPE, HEADS), (o_g, gw_in_p, 0, 2 * d)]
    gc_in = []
    for j in range(N_CHIPS):
        lo, hi, cols = j * n_in_shard, (j + 1) * n_in_shard, []
        for first, arr, at, width in in_pieces:
            a, b = max(lo, first), min(hi, first + width)
            if a < b:
                cols.append(arr[:, at + a - first:at + b - first])
        cols.append(jnp.zeros((d, in_pad - n_in_shard), BF16))
        gc_in.append(jnp.concatenate(cols, axis=1))
    gc_in = jnp.stack(gc_in)
    uq_parts = [gw_uq_p[:, i * hd:(i + 1) * hd].reshape(QL, HEADS, LANES) for i in range(3)]
    d_pe = (f32(uq_parts[1][:, :, :ROPE]) + _unrot_cols(f32(uq_parts[2][:, :, :ROPE]))).astype(BF16)
    gc_uq = _chunks(jnp.concatenate([uq_parts[0], d_pe], axis=2).reshape(QL, HEADS * (NOPE + ROPE)), w_uq.shape[1])
    gc_ukv = _chunks(jnp.concatenate([gw_ukv_p[:, :hd].reshape(KVL, HEADS, NOPE), gw_ukv_p[:, hd:].reshape(KVL, HEADS, VDIM)],
                                     axis=2).reshape(KVL, HEADS * (NOPE + VDIM)), w_ukv.shape[1])
    grads = [gc_in, gc_uq, gc_ukv]

    late = ["w_in", "w_uq", "w_ukv"]
    parts_late = pair_reduce("late", late, grads)
    big = late + early
    landed = list(_chip_scatter(parts_late)) + landed_early
    halves = [_sum_slots("rs_chip_sum_" + nm, s, first=lax.dynamic_index_in_dim(p, chip, 0, keepdims=False))
              for nm, s, p in zip(big, landed, parts_late + parts_early)]
    other = _pair_swap(halves)
    full = [jnp.concatenate([jnp.where(cc == 0, a, b), jnp.where(cc == 0, b, a)], axis=0) for a, b in zip(halves, other)]
    g_big = dict(zip(big, full))
    g_big["w_in"] = g_big["w_in"][:, :n_in_shard]

    conv_acc_l = from_il(conv_acc)
    pieces = [dg_pre_mix, dg_q, dg_kv, dbf, dbg_m, dbg_f, dg_post_mix, dg_pre_ffn, conv_acc_l[3:4], dg_post_ffn,
              conv_acc_l[0:1], conv_acc_l[1:2], conv_acc_l[2:3], sq_sum]
    sizes = [p.shape[1] for p in pieces]
    flat = jnp.concatenate(pieces, axis=1)
    n_rows = -(-flat.shape[1] // (8 * LANES)) * 8
    flat = _pad_cols(flat, n_rows * LANES).reshape(n_rows, LANES)
    slots = lax.dynamic_update_slice(_gather_small(flat), flat[None], (2 * chip + cc, 0, 0))
    total = _sum_slots("small_sum", slots).reshape(1, n_rows * LANES)
    offs = [sum(sizes[:i]) for i in range(len(sizes))]
    tot = [total[0, o:o + s] for o, s in zip(offs, sizes)]
    loss = 0.5 * tot[13][0] / d
    g_small = {"pre_mix_norm": tot[0], "q_a_norm": tot[1], "kv_a_norm": tot[2], "b_forget": tot[3][ROPE:ROPE + HEADS],
               "b_gate": jnp.concatenate([tot[4], tot[5]]), "post_mix_norm": tot[6], "pre_ffn_norm": tot[7],
               "conv_b": tot[8], "post_ffn_norm": tot[9]}
    gcw_full = jnp.stack([tot[10], tot[11], tot[12]])
    g_conv_w = lax.dynamic_slice(gcw_full, (0, chip * n_up), (3, n_up))

    given = dict(pre_mix_norm=(pre_mix_norm, m_pre_mix_norm, v_pre_mix_norm), w_in=(w_in, m_w_in, v_w_in),
                 q_a_norm=(q_a_norm, m_q_a_norm, v_q_a_norm), w_uq=(w_uq, m_w_uq, v_w_uq),
                 kv_a_norm=(kv_a_norm, m_kv_a_norm, v_kv_a_norm), w_ukv=(w_ukv, m_w_ukv, v_w_ukv),
                 b_forget=(b_forget, m_b_forget, v_b_forget), b_gate=(b_gate, m_b_gate, v_b_gate),
                 w_branch_mla=(w_branch_mla, m_w_branch_mla, v_w_branch_mla),
                 w_branch_fox=(w_branch_fox, m_w_branch_fox, v_w_branch_fox), w_out=(w_out, m_w_out, v_w_out),
                 post_mix_norm=(post_mix_norm, m_post_mix_norm, v_post_mix_norm),
                 pre_ffn_norm=(pre_ffn_norm, m_pre_ffn_norm, v_pre_ffn_norm), w_up=(w_up, m_w_up, v_w_up),
                 conv_w=(conv_w, m_conv_w, v_conv_w), conv_b=(conv_b, m_conv_b, v_conv_b),
                 w_down=(w_down, m_w_down, v_w_down), post_ffn_norm=(post_ffn_norm, m_post_ffn_norm, v_post_ffn_norm))
    order = list(given)
    grad, delta, new_m, new_v = {}, {}, {}, {}
    for nm in big + ["conv_w"]:
        grad[nm] = g_big[nm] if nm in g_big else g_conv_w
        if nm == "w_in":
            tr_out = _adamw("adamw_" + nm, *[jnp.transpose(a) for a in (given[nm][0], grad[nm], given[nm][1], given[nm][2])])
            delta[nm], new_m[nm], new_v[nm] = [jnp.transpose(a) for a in tr_out]
            continue
        delta[nm], new_m[nm], new_v[nm] = _adamw("adamw_" + nm, given[nm][0], grad[nm], given[nm][1], given[nm][2])
    small = list(g_small)
    padded = [-(-g_small[nm].shape[0] // LANES) * LANES for nm in small]
    s_rows = -(-sum(padded) // (8 * LANES)) * 8

    def pack(vals):
        cat = jnp.concatenate([jnp.pad(a, (0, p - a.shape[0])) for a, p in zip(vals, padded)])
        return jnp.pad(cat, (0, s_rows * LANES - cat.shape[0])).reshape(s_rows, LANES)

    packed = _adamw("adamw_small", pack([given[nm][0] for nm in small]), pack([g_small[nm] for nm in small]),
                    pack([given[nm][1] for nm in small]), pack([given[nm][2] for nm in small]))
    s_offs = [sum(padded[:i]) for i in range(len(small))]
    for nm, o in zip(small, s_offs):
        n_el = g_small[nm].shape[0]
        grad[nm] = g_small[nm]
        delta[nm], new_m[nm], new_v[nm] = [p.reshape(-1)[o:o + n_el] for p in packed]
    return (loss, grad_x.reshape(n_seq, seq, d), *[grad[nm] for nm in order], *[delta[nm] for nm in order],
            *[new_m[nm] for nm in order], *[new_v[nm] for nm in order])
```

```python
import functools
import math

import jax
import jax.numpy as jnp
from jax import lax
from jax.experimental import pallas as pl
from jax.experimental.pallas import tpu as pltpu

F32, BF16 = jnp.float32, jnp.bfloat16
MESH = pl.DeviceIdType.MESH

HEADS = 8
NOPE, ROPE, VDIM = 128, 64, 128
QL, KVL = 512, 256
FDIM = 128
CHUNK = 64
ROPE_THETA = 10000.0
EPS = 1e-6
NEG_INF = -1e30
ADAM_LR, ADAM_B1, ADAM_B2, ADAM_EPS, ADAM_WD, ADAM_STEP = 0.001, 0.9, 0.999, 1e-08, 0.01, 10

VMEM_LIMIT_BYTES = 52 * 1024 * 1024
LANES = 128
N_CHIPS = 4


def _params(sem):
    return pltpu.CompilerParams(dimension_semantics=sem, vmem_limit_bytes=VMEM_LIMIT_BYTES)


def _tile(n, target, mult):
    if n <= target:
        return n
    t = (target // mult) * mult
    while t >= mult:
        if n % t == 0:
            return t
        t -= mult
    raise ValueError(f"no tile for {n} (target {target}, multiple of {mult})")


class _Plain:
    def __init__(self, perm=None):
        self.perm = perm

    def spec(self, tr, tc, rc):
        perm = self.perm

        def imap(i, j, k):
            r, c = rc(i, j, k)
            return (r, perm(c) if perm is not None else c)

        return pl.BlockSpec((tr, tc), imap)

    def shape(self, rows, cols):
        return (rows, cols)


class _Chunked:
    def __init__(self, n):
        self.n = n

    def spec(self, tr, tc, rc):
        assert self.n % tc == 0, (self.n, tc)
        per = self.n // tc

        def imap(i, j, k):
            r, c = rc(i, j, k)
            return (c // per, r, c % per)

        return pl.BlockSpec((None, tr, tc), imap)

    def shape(self, rows, cols):
        assert cols == N_CHIPS * self.n
        return (N_CHIPS, rows, self.n)


_DIMS = {"nn": (((1,), (0,)), ((), ())), "nt": (((1,), (1,)), ((), ())), "tn": (((0,), (0,)), ((), ()))}


def _mm_single(name, mode, a, b, m, n, k, tm, tn, la, lb, lo, out_dtype, side):
    if mode == "nn":
        a_spec = la.spec(tm, k, lambda i, j, kk: (i, 0))
        b_spec = lb.spec(k, tn, lambda i, j, kk: (0, j))
    elif mode == "nt":
        a_spec = la.spec(tm, k, lambda i, j, kk: (i, 0))
        b_spec = lb.spec(tn, k, lambda i, j, kk: (j, 0))
    else:
        a_spec = la.spec(k, tm, lambda i, j, kk: (0, i))
        b_spec = lb.spec(k, tn, lambda i, j, kk: (0, j))
    o_spec = lo.spec(tm, tn, lambda i, j, kk: (i, j))
    dims = _DIMS[mode]

    def body(a_ref, b_ref, o_ref):
        o_ref[...] = lax.dot_general(a_ref[...].astype(BF16), b_ref[...].astype(BF16), dims,
                                     preferred_element_type=F32).astype(o_ref.dtype)

    (out,), got = _hosted_call(body, name, (m // tm, n // tn, 1), [a_spec, b_spec], [o_spec],
                               [jax.ShapeDtypeStruct(lo.shape(m, n), out_dtype)], (a, b), side,
                               semantics=("parallel", "parallel", "arbitrary"))
    return out if side is None else (out, got)


def _mm(name, mode, a, b, m, n, k, *, tm=1024, tn=1024, tk=2048, la=None, lb=None, lo=None, out_dtype=F32, side=None):
    la, lb, lo = la or _Plain(), lb or _Plain(), lo or _Plain()
    tm, tn, tk = _tile(m, tm, 128), _tile(n, tn, 128), _tile(k, tk, 128)
    nk = k // tk
    if nk == 1:
        return _mm_single(name, mode, a, b, m, n, k, tm, tn, la, lb, lo, out_dtype, side)
    if mode == "nn":
        a_spec = la.spec(tm, tk, lambda i, j, kk: (i, kk))
        b_spec = lb.spec(tk, tn, lambda i, j, kk: (kk, j))
    elif mode == "nt":
        a_spec = la.spec(tm, tk, lambda i, j, kk: (i, kk))
        b_spec = lb.spec(tn, tk, lambda i, j, kk: (j, kk))
    else:
        a_spec = la.spec(tk, tm, lambda i, j, kk: (kk, i))
        b_spec = lb.spec(tk, tn, lambda i, j, kk: (kk, j))
    o_spec = lo.spec(tm, tn, lambda i, j, kk: (i, j))
    dims = _DIMS[mode]

    def body(a_ref, b_ref, o_ref, acc_ref):
        kk = pl.program_id(2)

        @pl.when(kk == 0)
        def _():
            acc_ref[...] = jnp.zeros_like(acc_ref)

        acc_ref[...] += lax.dot_general(a_ref[...].astype(BF16), b_ref[...].astype(BF16), dims,
                                        preferred_element_type=F32)

        @pl.when(kk == nk - 1)
        def _():
            o_ref[...] = acc_ref[...].astype(o_ref.dtype)

    (out,), got = _hosted_call(body, name, (m // tm, n // tn, nk), [a_spec, b_spec], [o_spec],
                               [jax.ShapeDtypeStruct(lo.shape(m, n), out_dtype)], (a, b), side,
                               semantics=("parallel", "parallel", "arbitrary"), scratch=[pltpu.VMEM((tm, tn), F32)])
    return out if side is None else (out, got)


def _rows(name, fn, rows_in, vecs_in, rows_out, accs_out, n_rows, tr=256):
    tr = _tile(n_rows, tr, 16)
    nr, nv, no = len(rows_in), len(vecs_in), len(rows_out)

    def body(*refs):
        ins, vecs = refs[:nr], refs[nr:nr + nv]
        outs, accs = refs[nr + nv:nr + nv + no], refs[nr + nv + no:]
        ro, ac = fn([r[...] for r in ins], [v[...] for v in vecs])
        for o_ref, val in zip(outs, ro):
            o_ref[...] = val.astype(o_ref.dtype)
        if accs:
            @pl.when(pl.program_id(0) == 0)
            def _():
                for a_ref in accs:
                    a_ref[...] = jnp.zeros_like(a_ref)

            for a_ref, val in zip(accs, ac):
                a_ref[...] += val

    in_specs = [pl.BlockSpec((tr, cols), functools.partial(lambda i, cb: (i, cb), cb=cb)) for _, cols, cb in rows_in]
    in_specs += [pl.BlockSpec(v.shape, lambda i: (0, 0)) for v in vecs_in]
    out_specs = [pl.BlockSpec((tr, cols), lambda i: (i, 0)) for cols, _ in rows_out]
    out_specs += [pl.BlockSpec((r, cols), lambda i: (0, 0)) for r, cols in accs_out]
    out_shape = [jax.ShapeDtypeStruct((n_rows, cols), dt) for cols, dt in rows_out]
    out_shape += [jax.ShapeDtypeStruct((r, cols), F32) for r, cols in accs_out]
    res = pl.pallas_call(
        body, name=name, grid=(n_rows // tr,), in_specs=in_specs, out_specs=out_specs, out_shape=out_shape,
        compiler_params=_params(("arbitrary",)),
    )(*[a for a, _, _ in rows_in], *vecs_in)
    return res


def _colsum(v):
    return jnp.sum(v, axis=0, keepdims=True)


def _rstd(x):
    return lax.rsqrt(jnp.mean(x * x, axis=-1, keepdims=True) + EPS)


def _rms_bwd(x, g, dy):
    r = _rstd(x)
    xh = x * r
    dxh = dy * g
    dx = r * (dxh - xh * jnp.mean(dxh * xh, axis=-1, keepdims=True))
    return dx, _colsum(dy * xh)


def _sigmoid(z):
    return 1.0 / (1.0 + jnp.exp(-z))


_GELU_K = math.sqrt(2.0 / math.pi)


def _gelu_parts(g):
    t = jnp.tanh(_GELU_K * (g + 0.044715 * g * g * g))
    gel = 0.5 * g * (1.0 + t)
    dgel = 0.5 * (1.0 + t) + 0.5 * g * (1.0 - t * t) * (_GELU_K * (1.0 + 3.0 * 0.044715 * g * g))
    return gel, dgel


def _diag_visible(t, unit):
    rows = lax.broadcasted_iota(jnp.int32, (t, t), 0)
    cols = lax.broadcasted_iota(jnp.int32, (t, t), 1)
    if unit > 1:
        sh = int(math.log2(unit))
        assert 1 << sh == unit and t % unit == 0
        rows, cols = jnp.right_shift(rows, sh), jnp.right_shift(cols, sh)
    return cols <= rows


def _lane_pick(tile, lane):
    idx = lax.broadcasted_iota(jnp.int32, tile.shape, 1)
    return jnp.sum(jnp.where(idx == lane, tile, 0.0), axis=1, keepdims=True)


def _lane_put(tile, lane, col):
    idx = lax.broadcasted_iota(jnp.int32, tile.shape, 1)
    return jnp.where(idx == lane, col, tile)


def _head_cat(refs, shared, rows, h):
    hs = slice(h * LANES, (h + 1) * LANES)
    vals = [(r[rows, :] if sh else r[rows, hs]).astype(BF16) for r, sh in zip(refs, shared)]
    return vals[0] if len(vals) == 1 else jnp.concatenate(vals, axis=1)


def _blk_rows(i, t):
    return pl.ds(pl.multiple_of(i * t, t), t)


def _piece_specs(pieces, rows, row_idx):
    return [pl.BlockSpec((rows, LANES if sh else HEADS * LANES), functools.partial(lambda b, i, cb: (row_idx(b, i), cb), cb=cb))
            for _, cb, sh in pieces]


def _attn_fwd(name, qp, kp, vp, bias, unit, scale, n_seq, seq, t, side=None):
    nb = seq // t
    n_tok = n_seq * seq
    nq, nk_p = len(qp), len(kp)
    q_sh, k_sh = [p[2] for p in qp], [p[2] for p in kp]
    nbias = 2 if bias is not None else 0

    def body(*refs):
        q_refs, k_refs = refs[:nq], refs[nq:nq + nk_p]
        v_ref = refs[nq + nk_p]
        bias_refs = refs[nq + nk_p + 1:nq + nk_p + 1 + nbias]
        o_ref, lse_ref = refs[nq + nk_p + 1 + nbias:]
        qi = pl.program_id(1)
        lse_tile = jnp.zeros((t, LANES), F32)
        for h in range(HEADS):
            hs = slice(h * LANES, (h + 1) * LANES)
            q = _head_cat(q_refs, q_sh, slice(None), h)
            cq = _lane_pick(bias_refs[0][...], ROPE + h) if bias is not None else None

            def block(kb, carry, diag, h=h, hs=hs, q=q, cq=cq):
                m, l, acc = carry
                rows = _blk_rows(kb, t)
                s = lax.dot_general(q, _head_cat(k_refs, k_sh, rows, h), _DIMS["nt"], preferred_element_type=F32) * scale
                if bias is not None:
                    s = s + cq - bias_refs[1][kb, h:h + 1, :]
                if diag:
                    s = jnp.where(_diag_visible(t, unit), s, NEG_INF)
                m_new = jnp.maximum(m, jnp.max(s, axis=1, keepdims=True))
                alpha = jnp.exp(m - m_new)
                p = jnp.exp(s - m_new)
                l = alpha * l + jnp.sum(p, axis=1, keepdims=True)
                acc = alpha * acc + jnp.dot(p.astype(BF16), v_ref[rows, hs].astype(BF16), preferred_element_type=F32)
                return m_new, l, acc

            init = (jnp.full((t, 1), NEG_INF, F32), jnp.zeros((t, 1), F32), jnp.zeros((t, LANES), F32))
            carry = lax.fori_loop(0, qi, lambda kb, c: block(kb, c, False), init)
            m, l, acc = block(qi, carry, True)
            o_ref[:, hs] = acc / l
            lse_tile = _lane_put(lse_tile, h, m + jnp.log(l))
        lse_ref[...] = lse_tile

    tile_row = lambda b, i: b * nb + i
    seq_row = lambda b, i: b
    lane_tile = pl.BlockSpec((t, LANES), lambda b, i: (b * nb + i, 0))
    in_specs = _piece_specs(qp, t, tile_row) + _piece_specs(kp, seq, seq_row) + _piece_specs([vp + (False,)], seq, seq_row)
    args = [p[0] for p in qp] + [p[0] for p in kp] + [vp[0]]
    if bias is not None:
        in_specs += [lane_tile, pl.BlockSpec((None, nb, HEADS, t), lambda b, i: (b, 0, 0, 0))]
        args += list(bias)
    return _hosted_call(
        body, name, (n_seq, nb), in_specs,
        [pl.BlockSpec((t, HEADS * LANES), lambda b, i: (b * nb + i, 0)), lane_tile],
        [jax.ShapeDtypeStruct((n_tok, HEADS * LANES), F32), jax.ShapeDtypeStruct((n_tok, LANES), F32)], args, side)


def _attn_bwd_dq(name, qp, kp, vp, o, do, lse, bias, unit, scale, n_seq, seq, t, side=None):
    nb = seq // t
    n_tok = n_seq * seq
    nq, nk_p = len(qp), len(kp)
    q_sh, k_sh = [p[2] for p in qp], [p[2] for p in kp]
    nbias = 2 if bias is not None else 0
    n_in = nq + nk_p + 4 + nbias

    def body(*refs):
        q_refs, k_refs = refs[:nq], refs[nq:nq + nk_p]
        v_ref, o_ref, do_ref, lse_ref = refs[nq + nk_p:nq + nk_p + 4]
        bias_refs = refs[nq + nk_p + 4:n_in]
        dq_refs = refs[n_in:n_in + nq]
        delta_ref, dob_ref = refs[n_in + nq:n_in + nq + 2]
        qi = pl.program_id(1)
        delta_tile = jnp.zeros((t, LANES), F32)
        dc_tile = jnp.zeros((t, LANES), F32)
        lse_all = lse_ref[...]
        for h in range(HEADS):
            hs = slice(h * LANES, (h + 1) * LANES)
            q = _head_cat(q_refs, q_sh, slice(None), h)
            do_f = do_ref[:, hs]
            do_b = do_f.astype(BF16)
            dob_ref[:, hs] = do_b
            delta = jnp.sum(do_f * o_ref[:, hs], axis=1, keepdims=True)
            lse = _lane_pick(lse_all, h)
            cq = _lane_pick(bias_refs[0][...], ROPE + h) if bias is not None else None

            def block(kb, carry, diag, h=h, hs=hs, q=q, cq=cq, do_b=do_b, delta=delta, lse=lse):
                dq_acc, dc_acc = carry
                rows = _blk_rows(kb, t)
                k = _head_cat(k_refs, k_sh, rows, h)
                s = lax.dot_general(q, k, _DIMS["nt"], preferred_element_type=F32) * scale
                if bias is not None:
                    s = s + cq - bias_refs[1][kb, h:h + 1, :]
                if diag:
                    s = jnp.where(_diag_visible(t, unit), s, NEG_INF)
                p = jnp.exp(s - lse)
                dp = lax.dot_general(do_b, v_ref[rows, hs].astype(BF16), _DIMS["nt"], preferred_element_type=F32)
                ds = p * (dp - delta)
                return (dq_acc + jnp.dot(ds.astype(BF16), k, preferred_element_type=F32),
                        dc_acc + jnp.sum(ds, axis=1, keepdims=True))

            init = (jnp.zeros((t, nq * LANES), F32), jnp.zeros((t, 1), F32))
            carry = lax.fori_loop(0, qi, lambda kb, c: block(kb, c, False), init)
            dq_acc, dc_acc = block(qi, carry, True)
            for n_p in range(nq):
                dq_refs[n_p][:, hs] = dq_acc[:, n_p * LANES:(n_p + 1) * LANES] * scale
            delta_tile = _lane_put(delta_tile, h, delta)
            dc_tile = _lane_put(dc_tile, ROPE + h, dc_acc)
        delta_ref[...] = delta_tile
        if bias is not None:
            refs[n_in + nq + 2][...] = dc_tile

    tile_row = lambda b, i: b * nb + i
    seq_row = lambda b, i: b
    lane_tile = pl.BlockSpec((t, LANES), lambda b, i: (b * nb + i, 0))
    head_tile = pl.BlockSpec((t, HEADS * LANES), lambda b, i: (b * nb + i, 0))
    in_specs = _piece_specs(qp, t, tile_row) + _piece_specs(kp, seq, seq_row) + _piece_specs([vp + (False,)], seq, seq_row)
    in_specs += [head_tile, head_tile, lane_tile]
    args = [p[0] for p in qp] + [p[0] for p in kp] + [vp[0], o, do, lse]
    if bias is not None:
        in_specs += [lane_tile, pl.BlockSpec((None, nb, HEADS, t), lambda b, i: (b, 0, 0, 0))]
        args += list(bias)
    out_specs = [head_tile] * nq + [lane_tile, head_tile] + ([lane_tile] if bias is not None else [])
    out_shape = [jax.ShapeDtypeStruct((n_tok, HEADS * LANES), F32)] * nq
    out_shape += [jax.ShapeDtypeStruct((n_tok, LANES), F32), jax.ShapeDtypeStruct((n_tok, HEADS * LANES), BF16)]
    if bias is not None:
        out_shape.append(jax.ShapeDtypeStruct((n_tok, LANES), F32))
    return _hosted_call(body, name, (n_seq, nb), in_specs, out_specs, out_shape, args, side)


def _attn_bwd_dkv(name, qp, kp, vp, dob, lse, delta, bias, unit, scale, n_seq, seq, t, side=None):
    nb = seq // t
    n_tok = n_seq * seq
    nq, nk_p = len(qp), len(kp)
    q_sh, k_sh = [p[2] for p in qp], [p[2] for p in kp]
    nbias = 2 if bias is not None else 0
    n_in = nq + nk_p + 4 + nbias

    def body(*refs):
        q_refs, k_refs = refs[:nq], refs[nq:nq + nk_p]
        v_ref, dob_ref, lse_ref, delta_ref = refs[nq + nk_p:nq + nk_p + 4]
        bias_refs = refs[nq + nk_p + 4:n_in]
        dk_refs = refs[n_in:n_in + nk_p]
        dv_ref = refs[n_in + nk_p]
        ki = pl.program_id(1)
        shared_acc = [jnp.zeros((t, LANES), F32) for _ in range(nk_p)]
        for h in range(HEADS):
            hs = slice(h * LANES, (h + 1) * LANES)
            k = _head_cat(k_refs, k_sh, slice(None), h)
            v = v_ref[:, hs].astype(BF16)
            ck = bias_refs[1][h:h + 1, :] if bias is not None else None

            def block(qb, carry, diag, h=h, hs=hs, k=k, v=v, ck=ck):
                dk_acc, dv_acc, dc_acc = carry
                rows = _blk_rows(qb, t)
                q = _head_cat(q_refs, q_sh, rows, h)
                s = lax.dot_general(q, k, _DIMS["nt"], preferred_element_type=F32) * scale
                if bias is not None:
                    s = s + _lane_pick(bias_refs[0][rows, :], ROPE + h) - ck
                if diag:
                    s = jnp.where(_diag_visible(t, unit), s, NEG_INF)
                p = jnp.exp(s - _lane_pick(lse_ref[rows, :], h))
                do_b = dob_ref[rows, hs]
                dp = lax.dot_general(do_b, v, _DIMS["nt"], preferred_element_type=F32)
                ds = p * (dp - _lane_pick(delta_ref[rows, :], h))
                return (dk_acc + lax.dot_general(ds.astype(BF16), q, _DIMS["tn"], preferred_element_type=F32),
                        dv_acc + lax.dot_general(p.astype(BF16), do_b, _DIMS["tn"], preferred_element_type=F32),
                        dc_acc - jnp.sum(ds, axis=0, keepdims=True))

            init = (jnp.zeros((t, nk_p * LANES), F32), jnp.zeros((t, LANES), F32), jnp.zeros((1, t), F32))
            carry = block(ki, init, True)
            dk_acc, dv_acc, dc_acc = lax.fori_loop(ki + 1, nb, lambda qb, c: block(qb, c, False), carry)
            for n_p in range(nk_p):
                part = dk_acc[:, n_p * LANES:(n_p + 1) * LANES] * scale
                if k_sh[n_p]:
                    shared_acc[n_p] = shared_acc[n_p] + part
                else:
                    dk_refs[n_p][:, hs] = part
            dv_ref[:, hs] = dv_acc
            if bias is not None:
                refs[n_in + nk_p + 1][h:h + 1, :] = dc_acc
        for n_p in range(nk_p):
            if k_sh[n_p]:
                dk_refs[n_p][...] = shared_acc[n_p]

    tile_row = lambda b, i: b * nb + i
    seq_row = lambda b, i: b
    lane_seq = pl.BlockSpec((seq, LANES), lambda b, i: (b, 0))
    head_tile = pl.BlockSpec((t, HEADS * LANES), lambda b, i: (b * nb + i, 0))
    row_tile = pl.BlockSpec((None, None, HEADS, t), lambda b, i: (b, i, 0, 0))
    in_specs = _piece_specs(qp, seq, seq_row) + _piece_specs(kp, t, tile_row) + _piece_specs([vp + (False,)], t, tile_row)
    in_specs += [pl.BlockSpec((seq, HEADS * LANES), lambda b, i: (b, 0)), lane_seq, lane_seq]
    args = [p[0] for p in qp] + [p[0] for p in kp] + [vp[0], dob, lse, delta]
    if bias is not None:
        in_specs += [lane_seq, row_tile]
        args += list(bias)
    out_specs = [pl.BlockSpec((t, LANES if sh else HEADS * LANES), lambda b, i: (b * nb + i, 0)) for sh in k_sh] + [head_tile]
    out_shape = [jax.ShapeDtypeStruct((n_tok, LANES if sh else HEADS * LANES), F32) for sh in k_sh]
    out_shape.append(jax.ShapeDtypeStruct((n_tok, HEADS * LANES), F32))
    if bias is not None:
        out_specs.append(row_tile)
        out_shape.append(jax.ShapeDtypeStruct((n_seq, nb, HEADS, t), F32))
    return _hosted_call(body, name, (n_seq, nb), in_specs, out_specs, out_shape, args, side)


def _old_attn_bwd_dq(name, qp, kp, vp, o, do, lse, bias, unit, scale, n_seq, seq, t):
    nb = seq // t
    n_tok = n_seq * seq
    nq, nk_p = len(qp), len(kp)
    nbias = 2 if bias is not None else 0
    n_in = nq + nk_p + 4 + nbias
    n_out = nq + (1 if bias is not None else 0)

    def body(*refs):
        q_refs, k_refs = refs[:nq], refs[nq:nq + nk_p]
        v_ref, o_ref, do_ref, lse_ref = refs[nq + nk_p:nq + nk_p + 4]
        bias_refs = refs[nq + nk_p + 4:n_in]
        outs = refs[n_in:n_in + n_out]
        dq_s, delta_s, dc_s = refs[n_in + n_out:]
        qi, ki = pl.program_id(2), pl.program_id(3)

        @pl.when(ki == 0)
        def _():
            dq_s[...] = jnp.zeros_like(dq_s)
            dc_s[...] = jnp.zeros_like(dc_s)
            delta_s[...] = jnp.sum(do_ref[...] * o_ref[...], axis=1, keepdims=True)

        @pl.when(ki <= qi)
        def _():
            s = _scores(q_refs, k_refs, bias_refs, qi, ki, t, unit, scale)
            p = jnp.exp(s - lse_ref[...])
            dp = lax.dot_general(do_ref[...].astype(BF16), v_ref[...].astype(BF16), _DIMS["nt"],
                                 preferred_element_type=F32)
            ds = p * (dp - delta_s[...])
            dq_s[...] += jnp.dot(ds.astype(BF16), _cat(k_refs), preferred_element_type=F32)
            dc_s[...] += jnp.sum(ds, axis=1, keepdims=True)

        @pl.when(ki == qi)
        def _():
            for n_p in range(nq):
                outs[n_p][...] = dq_s[:, n_p * LANES:(n_p + 1) * LANES] * scale
            if bias is not None:
                outs[nq][...] = dc_s[...]

    q_row = lambda b, i, j: b * nb + i
    k_row = lambda b, i, j: b * nb + jnp.minimum(j, i)
    head_q = pl.BlockSpec((t, LANES), lambda b, h, i, j: (b * nb + i, h))
    col_q = pl.BlockSpec((None, t, 1), lambda b, h, i, j: (h, b * nb + i, 0))
    in_specs = [_piece_spec(t, p, q_row) for p in qp] + [_piece_spec(t, p, k_row) for p in kp]
    in_specs += [_piece_spec(t, vp, k_row), head_q, head_q, col_q]
    args = [p[0] for p in qp] + [p[0] for p in kp] + [vp[0], o, do, lse]
    if bias is not None:
        in_specs += [col_q, pl.BlockSpec((None, 1, t), lambda b, h, i, j: (b * HEADS + h, 0, jnp.minimum(j, i)))]
        args += list(bias)
    out_specs = [head_q] * nq + ([col_q] if bias is not None else [])
    out_shape = [jax.ShapeDtypeStruct((n_tok, HEADS * LANES), F32)] * nq
    if bias is not None:
        out_shape.append(jax.ShapeDtypeStruct((HEADS, n_tok, 1), F32))
    return pl.pallas_call(
        body, name=name, grid=(n_seq, HEADS, nb, nb), in_specs=in_specs, out_specs=out_specs, out_shape=out_shape,
        scratch_shapes=[pltpu.VMEM((t, nq * LANES), F32), pltpu.VMEM((t, 1), F32), pltpu.VMEM((t, 1), F32)],
        compiler_params=_params(("parallel", "parallel", "arbitrary", "arbitrary")),
    )(*args)


def _old_attn_bwd_dkv(name, qp, kp, vp, o, do, lse, bias, unit, scale, n_seq, seq, t):
    nb = seq // t
    n_tok = n_seq * seq
    nq, nk_p = len(qp), len(kp)
    nbias = 2 if bias is not None else 0
    n_in = nq + nk_p + 4 + nbias
    n_out = nk_p + 1 + (1 if bias is not None else 0)

    def body(*refs):
        q_refs, k_refs = refs[:nq], refs[nq:nq + nk_p]
        v_ref, o_ref, do_ref, lse_ref = refs[nq + nk_p:nq + nk_p + 4]
        bias_refs = refs[nq + nk_p + 4:n_in]
        outs = refs[n_in:n_in + n_out]
        dk_s, dv_s, dc_s = refs[n_in + n_out:]
        ki, qi = pl.program_id(2), pl.program_id(3)

        @pl.when(qi == 0)
        def _():
            dk_s[...] = jnp.zeros_like(dk_s)
            dv_s[...] = jnp.zeros_like(dv_s)
            dc_s[...] = jnp.zeros_like(dc_s)

        @pl.when(qi >= ki)
        def _():
            s = _scores(q_refs, k_refs, bias_refs, qi, ki, t, unit, scale)
            p = jnp.exp(s - lse_ref[...])
            do_b = do_ref[...].astype(BF16)
            delta = jnp.sum(do_ref[...] * o_ref[...], axis=1, keepdims=True)
            dp = lax.dot_general(do_b, v_ref[...].astype(BF16), _DIMS["nt"], preferred_element_type=F32)
            ds = p * (dp - delta)
            dv_s[...] += lax.dot_general(p.astype(BF16), do_b, _DIMS["tn"], preferred_element_type=F32)
            dk_s[...] += lax.dot_general(ds.astype(BF16), _cat(q_refs), _DIMS["tn"], preferred_element_type=F32)
            dc_s[...] -= jnp.sum(ds, axis=0, keepdims=True)

        @pl.when(qi == nb - 1)
        def _():
            for n_p in range(nk_p):
                outs[n_p][...] = dk_s[:, n_p * LANES:(n_p + 1) * LANES] * scale
            outs[nk_p][...] = dv_s[...]
            if bias is not None:
                outs[nk_p + 1][...] = dc_s[...]

    q_row = lambda b, i, j: b * nb + jnp.maximum(j, i)
    k_row = lambda b, i, j: b * nb + i
    head_q = pl.BlockSpec((t, LANES), lambda b, h, i, j: (b * nb + jnp.maximum(j, i), h))
    col_q = pl.BlockSpec((None, t, 1), lambda b, h, i, j: (h, b * nb + jnp.maximum(j, i), 0))
    head_k = pl.BlockSpec((t, LANES), lambda b, h, i, j: (b * nb + i, h))
    row_k = pl.BlockSpec((None, 1, t), lambda b, h, i, j: (b * HEADS + h, 0, i))
    in_specs = [_piece_spec(t, p, q_row) for p in qp] + [_piece_spec(t, p, k_row) for p in kp]
    in_specs += [_piece_spec(t, vp, k_row), head_q, head_q, col_q]
    args = [p[0] for p in qp] + [p[0] for p in kp] + [vp[0], o, do, lse]
    if bias is not None:
        in_specs += [col_q, row_k]
        args += list(bias)
    out_specs = [head_k] * (nk_p + 1) + ([row_k] if bias is not None else [])
    out_shape = [jax.ShapeDtypeStruct((n_tok, HEADS * LANES), F32)] * (nk_p + 1)
    if bias is not None:
        out_shape.append(jax.ShapeDtypeStruct((n_seq * HEADS, 1, seq), F32))
    return pl.pallas_call(
        body, name=name, grid=(n_seq, HEADS, nb, nb), in_specs=in_specs, out_specs=out_specs, out_shape=out_shape,
        scratch_shapes=[pltpu.VMEM((t, nk_p * LANES), F32), pltpu.VMEM((t, LANES), F32), pltpu.VMEM((1, t), F32)],
        compiler_params=_params(("parallel", "parallel", "arbitrary", "arbitrary")),
    )(*args)


def _seq_cumsum(name, x, col_block, n_seq, seq, reverse, pre=None, vec=None):
    t = _tile(seq, 256, 128)
    nb = seq // t

    def body(*refs):
        x_ref = refs[0]
        vec_ref = refs[1] if vec is not None else None
        o_ref, carry = refs[-2], refs[-1]

        @pl.when(pl.program_id(1) == 0)
        def _():
            carry[...] = jnp.zeros_like(carry)

        v = x_ref[...]
        if pre is not None:
            v = pre(v, vec_ref[...])
        r = lax.broadcasted_iota(jnp.int32, (t, t), 0)
        c = lax.broadcasted_iota(jnp.int32, (t, t), 1)
        tri = jnp.where((c >= r) if reverse else (c <= r), 1.0, 0.0).astype(BF16)
        hi = v.astype(BF16)
        mid = (v - hi.astype(F32)).astype(BF16)
        lo = (v - hi.astype(F32) - mid.astype(F32)).astype(BF16)
        acc = jnp.dot(tri, hi, preferred_element_type=F32)
        acc += jnp.dot(tri, mid, preferred_element_type=F32)
        acc += jnp.dot(tri, lo, preferred_element_type=F32)
        o_ref[...] = acc + carry[...]
        carry[...] += _colsum(v)

    blk = (lambda b, i: (b * nb + nb - 1 - i)) if reverse else (lambda b, i: (b * nb + i))
    in_specs = [pl.BlockSpec((t, LANES), lambda b, i: (blk(b, i), col_block))]
    args = [x]
    if vec is not None:
        in_specs.append(pl.BlockSpec(vec.shape, lambda b, i: (0, 0)))
        args.append(vec)
    return pl.pallas_call(
        body, name=name, grid=(n_seq, nb), in_specs=in_specs,
        out_specs=pl.BlockSpec((t, LANES), lambda b, i: (blk(b, i), 0)),
        out_shape=jax.ShapeDtypeStruct((n_seq * seq, LANES), F32),
        scratch_shapes=[pltpu.VMEM((1, LANES), F32)],
        compiler_params=_params(("arbitrary", "arbitrary")),
    )(*args)


def _log_sigmoid(z):
    return -(jnp.maximum(-z, 0.0) + jnp.log(1.0 + jnp.exp(-jnp.abs(z))))


def _shift_down(u, prev_ref, n):
    out = pltpu.roll(u, n, 0)
    row = lax.broadcasted_iota(jnp.int32, u.shape, 0)
    for r in range(n):
        out = jnp.where(row == r, prev_ref[8 - n + r:8 - n + r + 1, :], out)
    return out


def _shift_up(u, next_ref, n):
    ts = u.shape[0]
    out = pltpu.roll(u, ts - n, 0)
    row = lax.broadcasted_iota(jnp.int32, u.shape, 0)
    for r in range(n):
        out = jnp.where(row == ts - n + r, next_ref[r:r + 1, :], out)
    return out


def _conv_taps(u, prev_ref, w_ref, b_ref):
    s1, s2 = _shift_down(u, prev_ref, 1), _shift_down(u, prev_ref, 2)
    return (w_ref[0:1, :] * s2 + w_ref[1:2, :] * s1 + w_ref[2:3, :] * u) + b_ref[...], s1, s2


def _conv_glu_fwd(u_il, cw_il, cb_il, n_seq, seq, wt):
    n_tok, two_f = u_il.shape
    nct = two_f // (2 * wt)
    ts = _tile(seq, 256, 8)
    ns = seq // ts

    def body(u_ref, w_ref, b_ref, a_ref, carry):
        @pl.when(pl.program_id(2) == 0)
        def _():
            carry[...] = jnp.zeros_like(carry)

        u = u_ref[...]
        uc, _, _ = _conv_taps(u, carry, w_ref, b_ref)
        gel, _ = _gelu_parts(uc[:, :wt])
        a_ref[...] = (gel * uc[:, wt:]).astype(a_ref.dtype)
        carry[...] = u[ts - 8:, :]

    return pl.pallas_call(
        body, name="conv_glu_fwd", grid=(nct, n_seq, ns),
        in_specs=[pl.BlockSpec((ts, 2 * wt), lambda j, b, s: (b * ns + s, j)),
                  pl.BlockSpec((3, 2 * wt), lambda j, b, s: (0, j)),
                  pl.BlockSpec((1, 2 * wt), lambda j, b, s: (0, j))],
        out_specs=pl.BlockSpec((ts, wt), lambda j, b, s: (b * ns + s, j)),
        out_shape=jax.ShapeDtypeStruct((n_tok, two_f // 2), BF16),
        scratch_shapes=[pltpu.VMEM((8, 2 * wt), F32)],
        compiler_params=_params(("parallel", "arbitrary", "arbitrary")),
    )(u_il, cw_il, cb_il)


def _conv_glu_bwd_pre(u_il, da, cw_il, cb_il, n_seq, seq, wt):
    n_tok, two_f = u_il.shape
    nct = two_f // (2 * wt)
    ts = _tile(seq, 256, 8)
    ns = seq // ts

    def body(u_ref, da_ref, w_ref, b_ref, d_ref, acc_ref, carry):
        first = jnp.logical_and(pl.program_id(1) == 0, pl.program_id(2) == 0)

        @pl.when(first)
        def _():
            acc_ref[...] = jnp.zeros_like(acc_ref)

        @pl.when(pl.program_id(2) == 0)
        def _():
            carry[...] = jnp.zeros_like(carry)

        u = u_ref[...]
        uc, s1, s2 = _conv_taps(u, carry, w_ref, b_ref)
        gel, dgel = _gelu_parts(uc[:, :wt])
        da_v = da_ref[...]
        d = jnp.concatenate([da_v * uc[:, wt:] * dgel, da_v * gel], axis=1)
        d_ref[...] = d
        acc_ref[0:1, :] += _colsum(d * s2)
        acc_ref[1:2, :] += _colsum(d * s1)
        acc_ref[2:3, :] += _colsum(d * u)
        acc_ref[3:4, :] += _colsum(d)
        carry[...] = u[ts - 8:, :]

    return pl.pallas_call(
        body, name="conv_glu_bwd_pre", grid=(nct, n_seq, ns),
        in_specs=[pl.BlockSpec((ts, 2 * wt), lambda j, b, s: (b * ns + s, j)),
                  pl.BlockSpec((ts, wt), lambda j, b, s: (b * ns + s, j)),
                  pl.BlockSpec((3, 2 * wt), lambda j, b, s: (0, j)),
                  pl.BlockSpec((1, 2 * wt), lambda j, b, s: (0, j))],
        out_specs=[pl.BlockSpec((ts, 2 * wt), lambda j, b, s: (b * ns + s, j)),
                   pl.BlockSpec((8, 2 * wt), lambda j, b, s: (0, j))],
        out_shape=[jax.ShapeDtypeStruct((n_tok, two_f), F32), jax.ShapeDtypeStruct((8, two_f), F32)],
        scratch_shapes=[pltpu.VMEM((8, 2 * wt), F32)],
        compiler_params=_params(("parallel", "arbitrary", "arbitrary")),
    )(u_il, da, cw_il, cb_il)


def _conv_bwd_input(d_il, cw_il, n_seq, seq, wt):
    n_tok, two_f = d_il.shape
    nct = two_f // (2 * wt)
    ts = _tile(seq, 256, 8)
    ns = seq // ts

    def body(d_ref, w_ref, o_ref, carry):
        @pl.when(pl.program_id(2) == 0)
        def _():
            carry[...] = jnp.zeros_like(carry)

        d = d_ref[...]
        o_ref[...] = (w_ref[2:3, :] * d + w_ref[1:2, :] * _shift_up(d, carry, 1)
                      + w_ref[0:1, :] * _shift_up(d, carry, 2)).astype(o_ref.dtype)
        carry[...] = d[:8, :]

    rev = lambda j, b, s: (b * ns + ns - 1 - s, j)
    return pl.pallas_call(
        body, name="conv_bwd_input", grid=(nct, n_seq, ns),
        in_specs=[pl.BlockSpec((ts, 2 * wt), rev), pl.BlockSpec((3, 2 * wt), lambda j, b, s: (0, j))],
        out_specs=pl.BlockSpec((ts, 2 * wt), rev),
        out_shape=jax.ShapeDtypeStruct((n_tok, two_f), BF16),
        scratch_shapes=[pltpu.VMEM((8, 2 * wt), F32)],
        compiler_params=_params(("parallel", "arbitrary", "arbitrary")),
    )(d_il, cw_il)


HBM = pl.BlockSpec(memory_space=pltpu.HBM)
_CHIP_FLIPS = ((1, 0), (0, 1), (1, 1))


def _place():
    x, y, c = lax.axis_index("x"), lax.axis_index("y"), lax.axis_index("c")
    return x, y, c, 2 * x + y


def _flip(v, f):
    return 1 - v if f else v


def _half_rows(c, half):
    return pl.ds(pl.multiple_of(c * half, 16), half)


def _remote(src, dst, ssem, rsem, dev):
    return pltpu.make_async_remote_copy(src_ref=src, dst_ref=dst, send_sem=ssem, recv_sem=rsem,
                                        device_id=dev, device_id_type=MESH)


def _comm_call(name, body, ins, out_shapes, n_sems):
    return pl.pallas_call(
        body, name=name, in_specs=[HBM] * len(ins), out_specs=[HBM] * len(out_shapes),
        out_shape=[pltpu.HBM(s.shape, s.dtype) for s in out_shapes],
        scratch_shapes=[pltpu.SemaphoreType.DMA((n_sems,)), pltpu.SemaphoreType.DMA((n_sems,))],
    )(*ins)


def _all_gather_weights(shards, smalls):
    side = _gather_side(shards, smalls)
    nt = len(shards) + len(smalls)

    def body(*refs):
        for part in (side.start, side.mid, side.end):
            part(refs[:nt], refs[nt:2 * nt], *refs[2 * nt:])

    res = _comm_call("all_gather_weights", body, side.ins, side.outs, side.n_sems)
    return res[:len(shards)], res[len(shards):]


def _pair_split(name, grads):
    n = len(grads)

    def body(*refs):
        src, got = refs[:n], refs[n:2 * n]
        ssem, rsem = refs[2 * n:]
        x, y, c, _ = _place()
        cps = []
        for w in range(n):
            half = grads[w].shape[1] // 2
            cp = _remote(src[w].at[:, _half_rows(1 - c, half)], got[w], ssem.at[w], rsem.at[w], (x, y, 1 - c))
            cp.start()
            cps.append(cp)
        for cp in cps:
            cp.wait()

    outs = [jax.ShapeDtypeStruct((g.shape[0], g.shape[1] // 2, g.shape[2]), g.dtype) for g in grads]
    return _comm_call(name, body, grads, outs, n)


def _chip_scatter(parts):
    side = _scatter_side(parts)

    def body(*refs):
        n = len(parts)
        side.start(refs[:n], refs[n:2 * n], *refs[2 * n:])
        side.end(refs[:n], refs[n:2 * n], *refs[2 * n:])

    return _comm_call("rs_chip_scatter", body, parts, side.outs, side.n_sems)


class _Side:
    def __init__(self, ins, outs, n_sems, start, mid, end, mid_step=None):
        self.ins, self.outs, self.n_sems = list(ins), list(outs), n_sems
        self.start, self.mid, self.end, self.mid_step = start, mid, end, mid_step


def _scatter_side(parts):
    n = len(parts)

    def copies(src, dst, ssem, rsem):
        x, y, c, _ = _place()
        out = []
        for w in range(n):
            for k, (fx, fy) in enumerate(_CHIP_FLIPS):
                px, py = _flip(x, fx), _flip(y, fy)
                out.append(_remote(src[w].at[2 * px + py], dst[w].at[k], ssem.at[w * 3 + k], rsem.at[w * 3 + k], (px, py, c)))
        return out

    def start(src, dst, ssem, rsem):
        for cp in copies(src, dst, ssem, rsem):
            cp.start()

    def end(src, dst, ssem, rsem):
        for cp in copies(src, dst, ssem, rsem):
            cp.wait()

    outs = [jax.ShapeDtypeStruct((3,) + p.shape[1:], p.dtype) for p in parts]
    return _Side(parts, outs, 3 * n, start, None, end)


def _gather_side(shards, smalls, mid_step=None):
    n, ns = len(shards), len(smalls)

    def ici(src, dst, ssem, rsem, w, k):
        x, y, c, me = _place()
        fx, fy = _CHIP_FLIPS[k]
        rows = _half_rows(c, shards[w].shape[0] // 2)
        return _remote(src[w].at[rows], dst[w].at[me, rows], ssem.at[w * 6 + k], rsem.at[w * 6 + k],
                       (_flip(x, fx), _flip(y, fy), c))

    def small(src, dst, ssem, rsem, s, k):
        x, y, c, me = _place()
        fx, fy = _CHIP_FLIPS[k]
        sem = 6 * n + 3 * s + k
        return _remote(src[n + s], dst[n + s].at[me], ssem.at[sem], rsem.at[sem], (_flip(x, fx), _flip(y, fy), c))

    def landed(dst, ssem, rsem, w, k, sender_c, sem_off):
        x, y, c, _ = _place()
        fx, fy = _CHIP_FLIPS[k]
        got = dst[w].at[2 * _flip(x, fx) + _flip(y, fy), _half_rows(sender_c, shards[w].shape[0] // 2)]
        return _remote(got, got, ssem.at[w * 6 + sem_off + k], rsem.at[w * 6 + sem_off + k], (x, y, 1 - c))

    def start(src, dst, ssem, rsem):
        for s in range(ns):
            for k in range(3):
                small(src, dst, ssem, rsem, s, k).start()
        for w in range(n):
            for k in range(3):
                ici(src, dst, ssem, rsem, w, k).start()

    def mid(src, dst, ssem, rsem):
        c = lax.axis_index("c")
        for w in range(n):
            for k in range(3):
                landed(dst, ssem, rsem, w, k, c, 0).wait_recv()
                landed(dst, ssem, rsem, w, k, c, 3).start()

    def end(src, dst, ssem, rsem):
        c = lax.axis_index("c")
        for w in range(n):
            for k in range(3):
                landed(dst, ssem, rsem, w, k, 1 - c, 3).wait_recv()
        for s in range(ns):
            for k in range(3):
                small(src, dst, ssem, rsem, s, k).wait()
        for w in range(n):
            for k in range(3):
                ici(src, dst, ssem, rsem, w, k).wait_send()
                landed(dst, ssem, rsem, w, k, c, 3).wait_send()

    outs = [jax.ShapeDtypeStruct((N_CHIPS,) + a.shape, a.dtype) for a in list(shards) + list(smalls)]
    return _Side(list(shards) + list(smalls), outs, 6 * n + 3 * ns, start, mid, end, mid_step)


def _host(body, n_in, n_out, side, grid):
    if side is None:
        return body
    ns_in, ns_out = len(side.ins), len(side.outs)
    n_steps = math.prod(grid)
    mid_step = side.mid_step
    if side.mid is not None and not isinstance(mid_step, int):
        mid_step = min(n_steps - 1, int(mid_step * n_steps))

    def wrapped(*refs):
        ins, s_ins = refs[:n_in], refs[n_in:n_in + ns_in]
        outs = refs[n_in + ns_in:n_in + ns_in + n_out]
        s_outs = refs[n_in + ns_in + n_out:n_in + ns_in + n_out + ns_out]
        rest = refs[n_in + ns_in + n_out + ns_out:]
        sems = rest[-2:]
        step = 0
        for axis, extent in enumerate(grid):
            step = step * extent + pl.program_id(axis)

        @pl.when(step == 0)
        def _():
            side.start(s_ins, s_outs, *sems)

        if side.mid is not None:
            @pl.when(step == mid_step)
            def _():
                side.mid(s_ins, s_outs, *sems)

        body(*ins, *outs, *rest[:-2])

        @pl.when(step == n_steps - 1)
        def _():
            side.end(s_ins, s_outs, *sems)

    return wrapped


def _hosted_call(body, name, grid, in_specs, out_specs, out_shape, args, side, semantics=("parallel", "arbitrary"),
                 scratch=()):
    n_in, n_out = len(in_specs), len(out_specs)
    kern = _host(body, n_in, n_out, side, grid)
    if side is None:
        return pl.pallas_call(kern, name=name, grid=grid, in_specs=in_specs, out_specs=out_specs, out_shape=out_shape,
                              scratch_shapes=list(scratch), compiler_params=_params(semantics))(*args), []
    res = pl.pallas_call(
        kern, name=name, grid=grid, in_specs=in_specs + [HBM] * len(side.ins), out_specs=out_specs + [HBM] * len(side.outs),
        out_shape=list(out_shape) + [pltpu.HBM(s.shape, s.dtype) for s in side.outs],
        scratch_shapes=list(scratch) + [pltpu.SemaphoreType.DMA((side.n_sems,)), pltpu.SemaphoreType.DMA((side.n_sems,))],
        compiler_params=_params(("arbitrary",) * len(grid)),
    )(*args, *side.ins)
    return res[:n_out], res[n_out:]


def _pair_swap(halves):
    n = len(halves)

    def body(*refs):
        src, dst = refs[:n], refs[n:2 * n]
        ssem, rsem = refs[2 * n:]
        x, y, c, _ = _place()
        cps = []
        for w in range(n):
            cp = _remote(src[w], dst[w], ssem.at[w], rsem.at[w], (x, y, 1 - c))
            cp.start()
            cps.append(cp)
        for cp in cps:
            cp.wait()

    outs = [jax.ShapeDtypeStruct(h.shape, h.dtype) for h in halves]
    return _comm_call("rs_pair_swap", body, halves, outs, n)


def _gather_small(vec):
    def body(src, dst, ssem, rsem):
        x, y, c, _ = _place()
        me = 4 * x + 2 * y + c
        cps = []
        for r in range(1, 8):
            dev = (_flip(x, r & 4), _flip(y, r & 2), _flip(c, r & 1))
            cp = _remote(src, dst.at[me], ssem.at[r - 1], rsem.at[r - 1], dev)
            cp.start()
            cps.append(cp)
        for cp in cps:
            cp.wait()

    out = jax.ShapeDtypeStruct((8,) + vec.shape, vec.dtype)
    return _comm_call("gather_small", body, [vec], [out], 7)[0]


def _sum_slots(name, stacked, first=None):
    n, r, c = stacked.shape
    tr = _tile(r, 256, 8)

    def body(*refs):
        s_ref, o_ref = refs[-2], refs[-1]
        acc = refs[0][...].astype(F32) if first is not None else s_ref[0].astype(F32)
        for s in range(0 if first is not None else 1, n):
            acc = acc + s_ref[s].astype(F32)
        o_ref[...] = acc

    row_spec = pl.BlockSpec((tr, c), lambda i: (i, 0))
    in_specs = ([row_spec] if first is not None else []) + [pl.BlockSpec((n, tr, c), lambda i: (0, i, 0))]
    args = ([first] if first is not None else []) + [stacked]
    return pl.pallas_call(
        body, name=name, grid=(r // tr,), in_specs=in_specs, out_specs=row_spec,
        out_shape=jax.ShapeDtypeStruct((r, c), F32), compiler_params=_params(("parallel",)),
    )(*args)


def _adamw(name, w, g, m, v):
    r, c = w.shape
    by_cols = r % 8 != 0 and c % LANES == 0
    tr, tc = (r, _tile(c, 256, LANES)) if by_cols else (_tile(r, 256, 8), c)
    bc1, bc2 = 1.0 - ADAM_B1 ** ADAM_STEP, 1.0 - ADAM_B2 ** ADAM_STEP

    def body(w_ref, g_ref, m_ref, v_ref, d_ref, nm_ref, nv_ref):
        g_v = g_ref[...]
        nm = ADAM_B1 * m_ref[...] + (1.0 - ADAM_B1) * g_v
        nv = ADAM_B2 * v_ref[...] + (1.0 - ADAM_B2) * (g_v * g_v)
        d_ref[...] = -ADAM_LR * ((nm / bc1) / (jnp.sqrt(nv / bc2) + ADAM_EPS) + ADAM_WD * w_ref[...])
        nm_ref[...] = nm
        nv_ref[...] = nv

    spec = pl.BlockSpec((tr, tc), (lambda i: (0, i)) if by_cols else (lambda i: (i, 0)))
    return pl.pallas_call(
        body, name=name, grid=(c // tc if by_cols else r // tr,), in_specs=[spec] * 4, out_specs=[spec] * 3,
        out_shape=[jax.ShapeDtypeStruct((r, c), F32)] * 3, compiler_params=_params(("parallel",)),
    )(w, g, m, v)


def _pad_cols(a, cols):
    return jnp.pad(a, ((0, 0), (0, cols - a.shape[1])))


def _rot_cols(w):
    h = w.shape[-1] // 2
    return jnp.concatenate([-w[..., h:], w[..., :h]], axis=-1)


def _unrot_cols(d):
    h = d.shape[-1] // 2
    return jnp.concatenate([d[..., h:], -d[..., :h]], axis=-1)


def _logical(g):
    return jnp.transpose(g, (1, 0, 2)).reshape(g.shape[1], N_CHIPS * g.shape[2])


def _chunks(a, n):
    return jnp.transpose(a.reshape(a.shape[0], N_CHIPS, n), (1, 0, 2))


def kernel(x, positions, pre_mix_norm, w_in, q_a_norm, w_uq, kv_a_norm, w_ukv, b_forget, b_gate, w_branch_mla, w_branch_fox, w_out, post_mix_norm, pre_ffn_norm, w_up, conv_w, conv_b, w_down, post_ffn_norm, loss_target, m_pre_mix_norm, m_w_in, m_q_a_norm, m_w_uq, m_kv_a_norm, m_w_ukv, m_b_forget, m_b_gate, m_w_branch_mla, m_w_branch_fox, m_w_out, m_post_mix_norm, m_pre_ffn_norm, m_w_up, m_conv_w, m_conv_b, m_w_down, m_post_ffn_norm, v_pre_mix_norm, v_w_in, v_q_a_norm, v_w_uq, v_kv_a_norm, v_w_ukv, v_b_forget, v_b_gate, v_w_branch_mla, v_w_branch_fox, v_w_out, v_post_mix_norm, v_pre_ffn_norm, v_w_up, v_conv_w, v_conv_b, v_w_down, v_post_ffn_norm):
    n_seq, seq, d = x.shape
    n_tok = n_seq * seq
    d_in = N_CHIPS * w_in.shape[1]
    two_f = N_CHIPS * w_up.shape[1]
    ff_dim = two_f // 2
    assert d_in == QL + KVL + ROPE + 3 * HEADS * FDIM + HEADS + 2 * d
    n_in_shard = w_in.shape[1]
    in_pad = -(-n_in_shard // LANES) * LANES
    hd = HEADS * LANES
    xc, yc, cc = lax.axis_index("x"), lax.axis_index("y"), lax.axis_index("c")
    chip = 2 * xc + yc
    t_attn = _tile(seq, 512, 128)

    shards = [_pad_cols(w_in, in_pad).astype(BF16), w_uq.astype(BF16), w_ukv.astype(BF16), w_branch_mla.astype(BF16),
              w_branch_fox.astype(BF16), w_out.astype(BF16), w_up.astype(BF16), w_down.astype(BF16)]
    cw8 = jnp.pad(conv_w, ((0, 5), (0, 0)))
    put_own = lambda g, s: lax.dynamic_update_slice(g, s[None], (chip, 0, 0))
    gathered, (g_cw,) = _all_gather_weights(shards[:3], [cw8])
    g_in, g_uq, g_ukv = [put_own(g, s) for g, s in zip(gathered, shards[:3])]
    g_cw = put_own(g_cw, cw8)
    n_attn_steps = n_seq * (seq // t_attn)
    up_rows = shards[6].shape[0] // 2
    side_proj = _gather_side([shards[3], shards[4], shards[5]], [], mid_step=0.9)
    side_mla = _gather_side([shards[6][:up_rows]], [], mid_step=max(n_attn_steps - 2, 0))
    side_fox = _gather_side([shards[6][up_rows:]], [], mid_step=max(n_attn_steps - 2, 0))
    side_ffn = _gather_side([shards[7]], [], mid_step=0.7)

    o_q, o_kv, o_kpe = 0, QL, QL + KVL
    o_f = o_kpe + ROPE
    o_fl = o_f + 3 * hd
    o_g = o_fl + HEADS

    def chip_cols(lo, hi):
        out = []
        while lo < hi:
            j = lo // n_in_shard
            end = min(hi, (j + 1) * n_in_shard)
            out.append((j, lo - j * n_in_shard, end - j * n_in_shard))
            lo = end
        return out

    take = lambda lo, hi: [g_in[j, :, a:b] for j, a, b in chip_cols(lo, hi)]
    w_kpe = jnp.concatenate(take(o_kpe, o_f), axis=1)
    zeros = lambda n: jnp.zeros((d, n), BF16)
    win_p = jnp.concatenate(
        take(o_g, d_in) + take(o_q, o_kpe) + [w_kpe, zeros(LANES - ROPE), _rot_cols(w_kpe)] + take(o_fl, o_g)
        + [zeros(LANES - ROPE - HEADS)] + take(o_f, o_fl), axis=1)
    n_p = win_p.shape[1]
    cb_gm, cb_gf = 0, 1
    c_lat = 2 * d
    c_kx, c_kr = c_lat + QL + KVL, c_lat + QL + KVL + LANES
    n_pa = c_kr + LANES
    assert n_p == n_pa + 3 * hd

    uq3 = _logical(g_uq).reshape(QL, HEADS, NOPE + ROPE)
    pe = uq3[:, :, NOPE:]
    pad_pe = lambda a: jnp.pad(a, ((0, 0), (0, 0), (0, LANES - ROPE))).reshape(QL, hd)
    wuq_p = jnp.concatenate([uq3[:, :, :NOPE].reshape(QL, hd), pad_pe(pe), pad_pe(_rot_cols(pe))], axis=1)
    ukv3 = _logical(g_ukv).reshape(KVL, HEADS, NOPE + VDIM)
    wukv_p = jnp.concatenate([ukv3[:, :, :NOPE].reshape(KVL, hd), ukv3[:, :, NOPE:].reshape(KVL, hd)], axis=1)

    n_bm, n_up = w_branch_mla.shape[1], w_up.shape[1]
    l_bm, l_up = _Chunked(n_bm), _Chunked(n_up)
    wt = n_up // 2
    n_ut = two_f // wt
    il = lambda cblk: jnp.where(cblk < n_ut // 2, 2 * cblk, 2 * (cblk - n_ut // 2) + 1)
    l_il = _Plain(il)
    to_il = lambda a: a.reshape(a.shape[0], 2, n_ut // 2, wt).transpose(0, 2, 1, 3).reshape(a.shape[0], two_f)
    from_il = lambda a: a.reshape(a.shape[0], n_ut // 2, 2, wt).transpose(0, 2, 1, 3).reshape(a.shape[0], two_f)

    inv_freq = 1.0 / (ROPE_THETA ** (jnp.arange(0, ROPE, 2, dtype=F32) / ROPE))
    ang = positions.astype(F32).reshape(n_tok, 1) * inv_freq
    cos, sin = jnp.cos(ang), jnp.sin(ang)
    cs = _pad_cols(jnp.concatenate([cos, cos], axis=1), LANES)
    sn = _pad_cols(jnp.concatenate([sin, sin], axis=1), LANES)

    row = lambda v: v.reshape(1, -1)
    x2 = x.reshape(n_tok, d)
    tgt = loss_target.reshape(n_tok, d)

    (h,) = _rows("rms_pre_mix", lambda r, v: ([r[0] * _rstd(r[0]) * v[0]], []),
                 [(x2, d, 0)], [row(pre_mix_norm)], [(d, BF16)], [], n_tok)
    proj, got = _mm("proj_in", "nn", h, win_p, n_tok, n_pa, d, side=side_proj)
    g_bm, g_bf, g_out = [put_own(g, s) for g, s in zip(got, side_proj.ins)]
    w_out_full = g_out.reshape(d, d)
    tn_f = _tile(3 * hd, 1024, 128)
    assert n_pa % tn_f == 0
    proj_f = _mm("proj_in_fox", "nn", h, win_p, n_tok, 3 * hd, d, tn=tn_f, lb=_Plain(lambda cblk: cblk + n_pa // tn_f),
                 out_dtype=BF16)

    bf_vec = jnp.pad(row(b_forget), ((0, 0), (ROPE, LANES - ROPE - HEADS)))

    def lat_fwd(r, v):
        ql, kvl = r[0], r[1]
        return [ql * _rstd(ql) * v[0], kvl * _rstd(kvl) * v[1], r[2] * r[4] + r[3] * r[5]], []

    qn, kvn, rk = _rows("latent_norms", lat_fwd,
                        [(proj, QL, c_lat // QL), (proj, KVL, (c_lat + QL) // KVL), (proj, LANES, c_kx // LANES),
                         (proj, LANES, c_kr // LANES), (cs, LANES, 0), (sn, LANES, 0)],
                        [row(q_a_norm), row(kv_a_norm)], [(QL, BF16), (KVL, BF16), (LANES, BF16)], [], n_tok)
    q_p = _mm("q_up", "nn", qn, wuq_p, n_tok, 3 * hd, QL)
    kv_p = _mm("kv_up", "nn", kvn, wukv_p, n_tok, 2 * hd, KVL, out_dtype=BF16)

    def rope_q(r, v):
        c8, s8 = jnp.tile(r[3], (1, HEADS)), jnp.tile(r[4], (1, HEADS))
        return [r[0], r[1] * c8 + r[2] * s8], []

    q_nope, rq = _rows("rope_q", rope_q, [(q_p, hd, 0), (q_p, hd, 1), (q_p, hd, 2), (cs, LANES, 0), (sn, LANES, 0)], [],
                       [(hd, BF16), (hd, BF16)], [], n_tok)

    mla_q = [(q_nope, 0, False), (rq, 0, False)]
    mla_k = [(kv_p, 0, False), (rk, 0, True)]
    mla_v = (kv_p, 1)
    mla_scale = (NOPE + ROPE) ** -0.5
    (o_mla, lse_mla), got = _attn_fwd("mla_fwd", mla_q, mla_k, mla_v, None, CHUNK, mla_scale, n_seq, seq, t_attn, side=side_mla)
    g_up_top = put_own(got[0], side_mla.ins[0])

    c_run = _seq_cumsum("forget_cumsum", proj, c_kr // LANES, n_seq, seq, False,
                        pre=lambda z, b: _log_sigmoid(z + b), vec=bf_vec)
    nb_attn = seq // t_attn
    c_rowf = jnp.transpose(c_run[:, ROPE:ROPE + HEADS].reshape(n_seq, nb_attn, t_attn, HEADS), (0, 1, 3, 2))
    fox_q, fox_k, fox_v = [(proj_f, 0, False)], [(proj_f, 1, False)], (proj_f, 2)
    fox_scale = FDIM ** -0.5
    fox_bias = (c_run, c_rowf)
    (o_fox, lse_fox), got = _attn_fwd("fox_fwd", fox_q, fox_k, fox_v, fox_bias, 1, fox_scale, n_seq, seq, t_attn, side=side_fox)
    g_up = jnp.concatenate([g_up_top, put_own(got[0], side_fox.ins[0])], axis=1)

    pm = _mm("branch_mla", "nn", o_mla, g_bm, n_tok, d, hd, lb=l_bm, tn=n_bm)
    pf = _mm("branch_fox", "nn", o_fox, g_bf, n_tok, d, hd, lb=l_bm, tn=n_bm)
    bg = row(b_gate)

    def merge(r, v):
        return [_sigmoid(r[0] + v[0]) * r[2] + _sigmoid(r[1] + v[1]) * r[3]], []

    (merged,) = _rows("gate_merge", merge, [(proj, d, cb_gm), (proj, d, cb_gf), (pm, d, 0), (pf, d, 0)],
                      [bg[:, :d], bg[:, d:]], [(d, BF16)], [], n_tok)
    y1 = _mm("mix_out", "nn", merged, w_out_full, n_tok, d, d)

    def resid_norm(r, v):
        x1v = r[0] + r[1] * _rstd(r[1]) * v[0]
        return [x1v, x1v * _rstd(x1v) * v[1]], []

    x1, h2 = _rows("post_mix_pre_ffn", resid_norm, [(x2, d, 0), (y1, d, 0)], [row(post_mix_norm), row(pre_ffn_norm)],
                   [(d, F32), (d, BF16)], [], n_tok)

    u_il, got = _mm("ffn_up", "nn", h2, g_up, n_tok, two_f, d, lb=l_up, lo=l_il, tn=wt, side=side_ffn)
    w_down_full = put_own(got[0], side_ffn.ins[0]).reshape(ff_dim, d)
    cw_il = to_il(_logical(g_cw)[:3])
    cb_il = to_il(row(conv_b))
    act = _conv_glu_fwd(u_il, cw_il, cb_il, n_seq, seq, wt)
    ff = _mm("ffn_down", "nn", act, w_down_full, n_tok, d, ff_dim)

    def final(r, v):
        x1v, ffv, tg = r
        diff = x1v + ffv * _rstd(ffv) * v[0] - tg
        dx2v = diff / d
        dffv, dg4 = _rms_bwd(ffv, v[0], dx2v)
        sq = jnp.sum(jnp.sum(diff * diff, axis=1, keepdims=True), axis=0, keepdims=True)
        return [dx2v, dffv], [dg4, jnp.broadcast_to(sq, (1, LANES))]

    dx2, dff, dg_post_ffn, sq_sum = _rows("loss_post_ffn_bwd", final, [(x1, d, 0), (ff, d, 0), (tgt, d, 0)],
                                          [row(post_ffn_norm)], [(d, F32), (d, BF16)], [(1, d), (1, LANES)], n_tok)
    rs_parts, rs_landed = {}, {}

    def pair_reduce(tag, names, grads):
        theirs = _pair_split("rs_pair_split_" + tag, grads)
        for nm, g, b in zip(names, grads, theirs):
            a = lax.dynamic_slice_in_dim(g, cc * b.shape[1], b.shape[1], axis=1)
            r2, c2 = a.shape[0] * a.shape[1], a.shape[2]
            (p,) = _rows("rs_pair_add_" + nm, lambda r, v: ([r[0].astype(F32) + r[1].astype(F32)], []),
                         [(a.reshape(r2, c2), c2, 0), (b.reshape(r2, c2), c2, 0)], [], [(c2, BF16)], [], r2)
            rs_parts[nm] = p.reshape(a.shape)

    dact = _mm("ffn_down_dx", "nt", dff, w_down_full, n_tok, ff_dim, d, tn=wt)
    gw_down = _mm("ffn_down_dw", "tn", act, dff, ff_dim, d, n_tok, tm=wt, out_dtype=BF16)
    pair_reduce("down", ["w_down"], [gw_down.reshape(N_CHIPS, ff_dim // N_CHIPS, d)])
    d_il, conv_acc = _conv_glu_bwd_pre(u_il, dact, cw_il, cb_il, n_seq, seq, wt)
    du_il = _conv_bwd_input(d_il, cw_il, n_seq, seq, wt)
    gw_up, got = _mm("ffn_up_dw", "tn", h2, du_il, d, two_f, n_tok, lb=l_il, lo=l_up, tn=wt, out_dtype=BF16,
                     side=_scatter_side([rs_parts["w_down"]]))
    rs_landed["w_down"] = got[0]
    pair_reduce("up", ["w_up"], [gw_up])
    dh2, got = _mm("ffn_up_dx", "nt", du_il, g_up, n_tok, d, two_f, la=l_il, lb=l_up, tk=wt,
                   side=_scatter_side([rs_parts["w_up"]]))
    rs_landed["w_up"] = got[0]

    def mid_bwd(r, v):
        x1v, y1v, dx2v, dh2v = r
        d3, dg3 = _rms_bwd(x1v, v[1], dh2v)
        dx1v = dx2v + d3
        dy1v, dg2 = _rms_bwd(y1v, v[0], dx1v)
        return [dx1v, dy1v], [dg3, dg2]

    dx1, dy1, dg_pre_ffn, dg_post_mix = _rows(
        "pre_ffn_post_mix_bwd", mid_bwd, [(x1, d, 0), (y1, d, 0), (dx2, d, 0), (dh2, d, 0)],
        [row(post_mix_norm), row(pre_ffn_norm)], [(d, F32), (d, BF16)], [(1, d), (1, d)], n_tok)
    dmerged = _mm("mix_out_dx", "nt", dy1, w_out_full, n_tok, d, d)
    gw_out = _mm("mix_out_dw", "tn", merged, dy1, d, d, n_tok, out_dtype=BF16)

    def gate_bwd(r, v):
        zm, zf, pmv, pfv, dm = r
        gm, gf = _sigmoid(zm + v[0]), _sigmoid(zf + v[1])
        dzm, dzf = dm * pmv * gm * (1.0 - gm), dm * pfv * gf * (1.0 - gf)
        return [dm * gm, dm * gf, dzm, dzf], [_colsum(dzm), _colsum(dzf)]

    dpm, dpf, dzm, dzf, dbg_m, dbg_f = _rows(
        "gate_merge_bwd", gate_bwd, [(proj, d, cb_gm), (proj, d, cb_gf), (pm, d, 0), (pf, d, 0), (dmerged, d, 0)],
        [bg[:, :d], bg[:, d:]], [(d, BF16)] * 4, [(1, d), (1, d)], n_tok)
    tk_b = min(n_bm, 512)
    do_mla = _mm("branch_mla_dx", "nt", dpm, g_bm, n_tok, hd, d, lb=l_bm, tk=tk_b)
    do_fox = _mm("branch_fox_dx", "nt", dpf, g_bf, n_tok, hd, d, lb=l_bm, tk=tk_b)
    gw_bm = _mm("branch_mla_dw", "tn", o_mla, dpm, hd, d, n_tok, lo=l_bm, tn=n_bm, out_dtype=BF16)
    gw_bf = _mm("branch_fox_dw", "tn", o_fox, dpf, hd, d, n_tok, lo=l_bm, tn=n_bm, out_dtype=BF16)

    pair_reduce("mix", ["w_out", "w_branch_mla", "w_branch_fox"], [gw_out.reshape(N_CHIPS, d // N_CHIPS, d), gw_bm, gw_bf])
    (dq_nope, drq, delta_mla, dob_mla), got = _attn_bwd_dq(
        "mla_bwd_dq", mla_q, mla_k, mla_v, o_mla, do_mla, lse_mla, None, CHUNK, mla_scale, n_seq, seq, t_attn,
        side=_scatter_side([rs_parts[nm] for nm in ("w_out", "w_branch_mla", "w_branch_fox")]))
    rs_landed.update(zip(("w_out", "w_branch_mla", "w_branch_fox"), got))
    (dk_nope, drk, dv_mla), _ = _attn_bwd_dkv(
        "mla_bwd_dkv", mla_q, mla_k, mla_v, dob_mla, lse_mla, delta_mla, None, CHUNK, mla_scale, n_seq, seq, t_attn)
    (dfq, delta_fox, dob_fox, dc_q), _ = _attn_bwd_dq("fox_bwd_dq", fox_q, fox_k, fox_v, o_fox, do_fox, lse_fox, fox_bias, 1,
                                                      fox_scale, n_seq, seq, t_attn)
    (dfk, dfv, dc_k), _ = _attn_bwd_dkv("fox_bwd_dkv", fox_q, fox_k, fox_v, dob_fox, lse_fox, delta_fox, fox_bias, 1, fox_scale,
                                        n_seq, seq, t_attn)
    dc_k8 = jnp.transpose(dc_k, (0, 1, 3, 2)).reshape(n_tok, HEADS)
    dc128 = dc_q + jnp.pad(dc_k8, ((0, 0), (ROPE, LANES - ROPE - HEADS)))
    dlogf = _seq_cumsum("forget_cumsum_bwd", dc128, 0, n_seq, seq, True)

    def mla_pack(r, v):
        dqn_v, drq_v, dkn_v, dv_v, drk_v, c1, s1 = r
        c8, s8 = jnp.tile(c1, (1, HEADS)), jnp.tile(s1, (1, HEADS))
        return [jnp.concatenate([dqn_v, drq_v * c8, drq_v * s8], axis=1), jnp.concatenate([dkn_v, dv_v], axis=1),
                drk_v * c1, drk_v * s1], []

    dq_p, dkv_p, dkx, dkr = _rows(
        "mla_rope_bwd", mla_pack,
        [(dq_nope, hd, 0), (drq, hd, 0), (dk_nope, hd, 0), (dv_mla, hd, 0), (drk, LANES, 0), (cs, LANES, 0), (sn, LANES, 0)],
        [], [(3 * hd, BF16), (2 * hd, BF16), (LANES, F32), (LANES, F32)], [], n_tok)
    dqn = _mm("q_up_dx", "nt", dq_p, wuq_p, n_tok, QL, 3 * hd)
    gw_uq_p = _mm("q_up_dw", "tn", qn, dq_p, QL, 3 * hd, n_tok, out_dtype=BF16)
    dkvn = _mm("kv_up_dx", "nt", dkv_p, wukv_p, n_tok, KVL, 2 * hd)
    gw_ukv_p = _mm("kv_up_dw", "tn", kvn, dkv_p, KVL, 2 * hd, n_tok, out_dtype=BF16)

    def lat_bwd(r, v):
        ql, kvl, dqn_v, dkvn_v, dkx_v, dkr_v, zblk, dlf = r
        dql, dgq = _rms_bwd(ql, v[0], dqn_v)
        dkvl, dgkv = _rms_bwd(kvl, v[1], dkvn_v)
        dfl = dlf * _sigmoid(-(zblk + v[2]))
        return [jnp.concatenate([dql, dkvl, dkx_v, dkr_v + dfl], axis=1)], [dgq, dgkv, _colsum(dfl)]

    dlat, dg_q, dg_kv, dbf = _rows(
        "latent_bwd", lat_bwd,
        [(proj, QL, c_lat // QL), (proj, KVL, (c_lat + QL) // KVL), (dqn, QL, 0), (dkvn, KVL, 0), (dkx, LANES, 0),
         (dkr, LANES, 0), (proj, LANES, c_kr // LANES), (dlogf, LANES, 0)],
        [row(q_a_norm), row(kv_a_norm), bf_vec], [(QL + KVL + 2 * LANES, BF16)], [(1, QL), (1, KVL), (1, LANES)], n_tok)
    dproj = jnp.concatenate([dzm, dzf, dlat, dfq.astype(BF16), dfk.astype(BF16), dfv.astype(BF16)], axis=1)
    gw_in_p = _mm("proj_in_dw", "tn", h, dproj, d, n_p, n_tok, out_dtype=BF16)

    f32 = lambda a: a.astype(F32)
    kr_blk = gw_in_p[:, c_kr:c_kr + LANES]
    d_kpe = (f32(gw_in_p[:, c_kx:c_kx + ROPE]) + _unrot_cols(f32(kr_blk[:, :ROPE]))).astype(BF16)
    in_pieces = [(o_q, gw_in_p, c_lat, QL + KVL), (o_kpe, d_kpe, 0, ROPE), (o_f, gw_in_p, n_pa, 3 * hd),
                 (o_fl, kr_blk, ROPE, HEADS), (o_g, gw_in_p, 0, 2 * d)]
    gc_in = []
    for j in range(N_CHIPS):
        lo, hi, cols = j * n_in_shard, (j + 1) * n_in_shard, []
        for first, arr, at, width in in_pieces:
            a, b = max(lo, first), min(hi, first + width)
            if a < b:
                cols.append(arr[:, at + a - first:at + b - first])
        cols.append(jnp.zeros((d, in_pad - n_in_shard), BF16))
        gc_in.append(jnp.concatenate(cols, axis=1))
    gc_in = jnp.stack(gc_in)
    uq_parts = [gw_uq_p[:, i * hd:(i + 1) * hd].reshape(QL, HEADS, LANES) for i in range(3)]
    d_pe = (f32(uq_parts[1][:, :, :ROPE]) + _unrot_cols(f32(uq_parts[2][:, :, :ROPE]))).astype(BF16)
    gc_uq = _chunks(jnp.concatenate([uq_parts[0], d_pe], axis=2).reshape(QL, HEADS * (NOPE + ROPE)), w_uq.shape[1])
    gc_ukv = _chunks(jnp.concatenate([gw_ukv_p[:, :hd].reshape(KVL, HEADS, NOPE), gw_ukv_p[:, hd:].reshape(KVL, HEADS, VDIM)],
                                     axis=2).reshape(KVL, HEADS * (NOPE + VDIM)), w_ukv.shape[1])
    grads = [gc_in, gc_uq, gc_ukv]

    late = ["w_in", "w_uq", "w_ukv"]
    pair_reduce("late", late, grads)
    dh, got = _mm("proj_in_dx", "nt", dproj, win_p, n_tok, d, n_p, side=_scatter_side([rs_parts[nm] for nm in late]))
    rs_landed.update(zip(late, got))

    def first_bwd(r, v):
        dxa, dg1 = _rms_bwd(r[0], v[0], r[1])
        return [r[2] + dxa], [dg1]

    grad_x, dg_pre_mix = _rows("pre_mix_bwd", first_bwd, [(x2, d, 0), (dh, d, 0), (dx1, d, 0)], [row(pre_mix_norm)],
                               [(d, F32)], [(1, d)], n_tok)
    big = list(rs_parts)
    halves = [_sum_slots("rs_chip_sum_" + nm, rs_landed[nm],
                         first=lax.dynamic_index_in_dim(rs_parts[nm], chip, 0, keepdims=False)) for nm in big]
    other = _pair_swap(halves)
    full = [jnp.concatenate([jnp.where(cc == 0, a, b), jnp.where(cc == 0, b, a)], axis=0) for a, b in zip(halves, other)]
    g_big = dict(zip(big, full))
    g_big["w_in"] = g_big["w_in"][:, :n_in_shard]

    conv_acc_l = from_il(conv_acc)
    pieces = [dg_pre_mix, dg_q, dg_kv, dbf, dbg_m, dbg_f, dg_post_mix, dg_pre_ffn, conv_acc_l[3:4], dg_post_ffn,
              conv_acc_l[0:1], conv_acc_l[1:2], conv_acc_l[2:3], sq_sum]
    sizes = [p.shape[1] for p in pieces]
    flat = jnp.concatenate(pieces, axis=1)
    n_rows = -(-flat.shape[1] // (8 * LANES)) * 8
    flat = _pad_cols(flat, n_rows * LANES).reshape(n_rows, LANES)
    slots = lax.dynamic_update_slice(_gather_small(flat), flat[None], (2 * chip + cc, 0, 0))
    total = _sum_slots("small_sum", slots).reshape(1, n_rows * LANES)
    offs = [sum(sizes[:i]) for i in range(len(sizes))]
    tot = [total[0, o:o + s] for o, s in zip(offs, sizes)]
    loss = 0.5 * tot[13][0] / d
    g_small = {"pre_mix_norm": tot[0], "q_a_norm": tot[1], "kv_a_norm": tot[2], "b_forget": tot[3][ROPE:ROPE + HEADS],
               "b_gate": jnp.concatenate([tot[4], tot[5]]), "post_mix_norm": tot[6], "pre_ffn_norm": tot[7],
               "conv_b": tot[8], "post_ffn_norm": tot[9]}
    gcw_full = jnp.stack([tot[10], tot[11], tot[12]])
    g_conv_w = lax.dynamic_slice(gcw_full, (0, chip * n_up), (3, n_up))

    given = dict(pre_mix_norm=(pre_mix_norm, m_pre_mix_norm, v_pre_mix_norm), w_in=(w_in, m_w_in, v_w_in),
                 q_a_norm=(q_a_norm, m_q_a_norm, v_q_a_norm), w_uq=(w_uq, m_w_uq, v_w_uq),
                 kv_a_norm=(kv_a_norm, m_kv_a_norm, v_kv_a_norm), w_ukv=(w_ukv, m_w_ukv, v_w_ukv),
                 b_forget=(b_forget, m_b_forget, v_b_forget), b_gate=(b_gate, m_b_gate, v_b_gate),
                 w_branch_mla=(w_branch_mla, m_w_branch_mla, v_w_branch_mla),
                 w_branch_fox=(w_branch_fox, m_w_branch_fox, v_w_branch_fox), w_out=(w_out, m_w_out, v_w_out),
                 post_mix_norm=(post_mix_norm, m_post_mix_norm, v_post_mix_norm),
                 pre_ffn_norm=(pre_ffn_norm, m_pre_ffn_norm, v_pre_ffn_norm), w_up=(w_up, m_w_up, v_w_up),
                 conv_w=(conv_w, m_conv_w, v_conv_w), conv_b=(conv_b, m_conv_b, v_conv_b),
                 w_down=(w_down, m_w_down, v_w_down), post_ffn_norm=(post_ffn_norm, m_post_ffn_norm, v_post_ffn_norm))
    order = list(given)
    grad, delta, new_m, new_v = {}, {}, {}, {}
    for nm in big + ["conv_w"]:
        grad[nm] = g_big[nm] if nm in g_big else g_conv_w
        if nm == "w_in":
            tr_out = _adamw("adamw_" + nm, *[jnp.transpose(a) for a in (given[nm][0], grad[nm], given[nm][1], given[nm][2])])
            delta[nm], new_m[nm], new_v[nm] = [jnp.transpose(a) for a in tr_out]
            continue
        delta[nm], new_m[nm], new_v[nm] = _adamw("adamw_" + nm, given[nm][0], grad[nm], given[nm][1], given[nm][2])
    small = list(g_small)
    padded = [-(-g_small[nm].shape[0] // LANES) * LANES for nm in small]
    s_rows = -(-sum(padded) // (8 * LANES)) * 8

    def pack(vals):
        cat = jnp.concatenate([jnp.pad(a, (0, p - a.shape[0])) for a, p in zip(vals, padded)])
        return jnp.pad(cat, (0, s_rows * LANES - cat.shape[0])).reshape(s_rows, LANES)

    packed = _adamw("adamw_small", pack([given[nm][0] for nm in small]), pack([g_small[nm] for nm in small]),
                    pack([given[nm][1] for nm in small]), pack([given[nm][2] for nm in small]))
    s_offs = [sum(padded[:i]) for i in range(len(small))]
    for nm, o in zip(small, s_offs):
        n_el = g_small[nm].shape[0]
        grad[nm] = g_small[nm]
        delta[nm], new_m[nm], new_v[nm] = [p.reshape(-1)[o:o + n_el] for p in packed]
    return (loss, grad_x.reshape(n_seq, seq, d), *[grad[nm] for nm in order], *[delta[nm] for nm in order],
            *[new_m[nm] for nm in order], *[new_v[nm] for nm in order])
```

```python
import functools
import math

import jax
import jax.numpy as jnp
from jax import lax
from jax.experimental import pallas as pl
from jax.experimental.pallas import tpu as pltpu

F32, BF16 = jnp.float32, jnp.bfloat16
MESH = pl.DeviceIdType.MESH

HEADS = 8
NOPE, ROPE, VDIM = 128, 64, 128
QL, KVL = 512, 256
FDIM = 128
CHUNK = 64
ROPE_THETA = 10000.0
EPS = 1e-6
NEG_INF = -1e30
ADAM_LR, ADAM_B1, ADAM_B2, ADAM_EPS, ADAM_WD, ADAM_STEP = 0.001, 0.9, 0.999, 1e-08, 0.01, 10

VMEM_LIMIT_BYTES = 52 * 1024 * 1024
LANES = 128
N_CHIPS = 4


def _params(sem):
    return pltpu.CompilerParams(dimension_semantics=sem, vmem_limit_bytes=VMEM_LIMIT_BYTES)


def _tile(n, target, mult):
    if n <= target:
        return n
    t = (target // mult) * mult
    while t >= mult:
        if n % t == 0:
            return t
        t -= mult
    raise ValueError(f"no tile for {n} (target {target}, multiple of {mult})")


class _Plain:
    def __init__(self, perm=None):
        self.perm = perm

    def spec(self, tr, tc, rc):
        perm = self.perm

        def imap(i, j, k):
            r, c = rc(i, j, k)
            return (r, perm(c) if perm is not None else c)

        return pl.BlockSpec((tr, tc), imap)

    def shape(self, rows, cols):
        return (rows, cols)


class _Chunked:
    def __init__(self, n):
        self.n = n

    def spec(self, tr, tc, rc):
        assert self.n % tc == 0, (self.n, tc)
        per = self.n // tc

        def imap(i, j, k):
            r, c = rc(i, j, k)
            return (c // per, r, c % per)

        return pl.BlockSpec((None, tr, tc), imap)

    def shape(self, rows, cols):
        assert cols == N_CHIPS * self.n
        return (N_CHIPS, rows, self.n)


_DIMS = {"nn": (((1,), (0,)), ((), ())), "nt": (((1,), (1,)), ((), ())), "tn": (((0,), (0,)), ((), ()))}


def _mm_single(name, mode, a, b, m, n, k, tm, tn, la, lb, lo, out_dtype, side):
    if mode == "nn":
        a_spec = la.spec(tm, k, lambda i, j, kk: (i, 0))
        b_spec = lb.spec(k, tn, lambda i, j, kk: (0, j))
    elif mode == "nt":
        a_spec = la.spec(tm, k, lambda i, j, kk: (i, 0))
        b_spec = lb.spec(tn, k, lambda i, j, kk: (j, 0))
    else:
        a_spec = la.spec(k, tm, lambda i, j, kk: (0, i))
        b_spec = lb.spec(k, tn, lambda i, j, kk: (0, j))
    o_spec = lo.spec(tm, tn, lambda i, j, kk: (i, j))
    dims = _DIMS[mode]

    def body(a_ref, b_ref, o_ref):
        o_ref[...] = lax.dot_general(a_ref[...].astype(BF16), b_ref[...].astype(BF16), dims,
                                     preferred_element_type=F32).astype(o_ref.dtype)

    (out,), got = _hosted_call(body, name, (m // tm, n // tn, 1), [a_spec, b_spec], [o_spec],
                               [jax.ShapeDtypeStruct(lo.shape(m, n), out_dtype)], (a, b), side,
                               semantics=("parallel", "parallel", "arbitrary"))
    return out if side is None else (out, got)


def _mm(name, mode, a, b, m, n, k, *, tm=1024, tn=1024, tk=2048, la=None, lb=None, lo=None, out_dtype=F32, side=None):
    la, lb, lo = la or _Plain(), lb or _Plain(), lo or _Plain()
    tm, tn, tk = _tile(m, tm, 128), _tile(n, tn, 128), _tile(k, tk, 128)
    nk = k // tk
    if nk == 1:
        return _mm_single(name, mode, a, b, m, n, k, tm, tn, la, lb, lo, out_dtype, side)
    if mode == "nn":
        a_spec = la.spec(tm, tk, lambda i, j, kk: (i, kk))
        b_spec = lb.spec(tk, tn, lambda i, j, kk: (kk, j))
    elif mode == "nt":
        a_spec = la.spec(tm, tk, lambda i, j, kk: (i, kk))
        b_spec = lb.spec(tn, tk, lambda i, j, kk: (j, kk))
    else:
        a_spec = la.spec(tk, tm, lambda i, j, kk: (kk, i))
        b_spec = lb.spec(tk, tn, lambda i, j, kk: (kk, j))
    o_spec = lo.spec(tm, tn, lambda i, j, kk: (i, j))
    dims = _DIMS[mode]

    def body(a_ref, b_ref, o_ref, acc_ref):
        kk = pl.program_id(2)

        @pl.when(kk == 0)
        def _():
            acc_ref[...] = jnp.zeros_like(acc_ref)

        acc_ref[...] += lax.dot_general(a_ref[...].astype(BF16), b_ref[...].astype(BF16), dims,
                                        preferred_element_type=F32)

        @pl.when(kk == nk - 1)
        def _():
            o_ref[...] = acc_ref[...].astype(o_ref.dtype)

    (out,), got = _hosted_call(body, name, (m // tm, n // tn, nk), [a_spec, b_spec], [o_spec],
                               [jax.ShapeDtypeStruct(lo.shape(m, n), out_dtype)], (a, b), side,
                               semantics=("parallel", "parallel", "arbitrary"), scratch=[pltpu.VMEM((tm, tn), F32)])
    return out if side is None else (out, got)


def _rows(name, fn, rows_in, vecs_in, rows_out, accs_out, n_rows, tr=256):
    tr = _tile(n_rows, tr, 16)
    nr, nv, no = len(rows_in), len(vecs_in), len(rows_out)

    def body(*refs):
        ins, vecs = refs[:nr], refs[nr:nr + nv]
        outs, accs = refs[nr + nv:nr + nv + no], refs[nr + nv + no:]
        ro, ac = fn([r[...] for r in ins], [v[...] for v in vecs])
        for o_ref, val in zip(outs, ro):
            o_ref[...] = val.astype(o_ref.dtype)
        if accs:
            @pl.when(pl.program_id(0) == 0)
            def _():
                for a_ref in accs:
                    a_ref[...] = jnp.zeros_like(a_ref)

            for a_ref, val in zip(accs, ac):
                a_ref[...] += val

    in_specs = [pl.BlockSpec((tr, cols), functools.partial(lambda i, cb: (i, cb), cb=cb)) for _, cols, cb in rows_in]
    in_specs += [pl.BlockSpec(v.shape, lambda i: (0, 0)) for v in vecs_in]
    out_specs = [pl.BlockSpec((tr, cols), lambda i: (i, 0)) for cols, _ in rows_out]
    out_specs += [pl.BlockSpec((r, cols), lambda i: (0, 0)) for r, cols in accs_out]
    out_shape = [jax.ShapeDtypeStruct((n_rows, cols), dt) for cols, dt in rows_out]
    out_shape += [jax.ShapeDtypeStruct((r, cols), F32) for r, cols in accs_out]
    res = pl.pallas_call(
        body, name=name, grid=(n_rows // tr,), in_specs=in_specs, out_specs=out_specs, out_shape=out_shape,
        compiler_params=_params(("arbitrary",)),
    )(*[a for a, _, _ in rows_in], *vecs_in)
    return res


def _colsum(v):
    return jnp.sum(v, axis=0, keepdims=True)


def _rstd(x):
    return lax.rsqrt(jnp.mean(x * x, axis=-1, keepdims=True) + EPS)


def _rms_bwd(x, g, dy):
    r = _rstd(x)
    xh = x * r
    dxh = dy * g
    dx = r * (dxh - xh * jnp.mean(dxh * xh, axis=-1, keepdims=True))
    return dx, _colsum(dy * xh)


def _sigmoid(z):
    return 1.0 / (1.0 + jnp.exp(-z))


_GELU_K = math.sqrt(2.0 / math.pi)


def _gelu_parts(g):
    t = jnp.tanh(_GELU_K * (g + 0.044715 * g * g * g))
    gel = 0.5 * g * (1.0 + t)
    dgel = 0.5 * (1.0 + t) + 0.5 * g * (1.0 - t * t) * (_GELU_K * (1.0 + 3.0 * 0.044715 * g * g))
    return gel, dgel


def _diag_visible(t, unit):
    rows = lax.broadcasted_iota(jnp.int32, (t, t), 0)
    cols = lax.broadcasted_iota(jnp.int32, (t, t), 1)
    if unit > 1:
        sh = int(math.log2(unit))
        assert 1 << sh == unit and t % unit == 0
        rows, cols = jnp.right_shift(rows, sh), jnp.right_shift(cols, sh)
    return cols <= rows


def _lane_pick(tile, lane):
    idx = lax.broadcasted_iota(jnp.int32, tile.shape, 1)
    return jnp.sum(jnp.where(idx == lane, tile, 0.0), axis=1, keepdims=True)


def _lane_put(tile, lane, col):
    idx = lax.broadcasted_iota(jnp.int32, tile.shape, 1)
    return jnp.where(idx == lane, col, tile)


def _head_cat(refs, shared, rows, h):
    hs = slice(h * LANES, (h + 1) * LANES)
    vals = [(r[rows, :] if sh else r[rows, hs]).astype(BF16) for r, sh in zip(refs, shared)]
    return vals[0] if len(vals) == 1 else jnp.concatenate(vals, axis=1)


def _blk_rows(i, t):
    return pl.ds(pl.multiple_of(i * t, t), t)


def _piece_specs(pieces, rows, row_idx):
    return [pl.BlockSpec((rows, LANES if sh else HEADS * LANES), functools.partial(lambda b, i, cb: (row_idx(b, i), cb), cb=cb))
            for _, cb, sh in pieces]


def _attn_fwd(name, qp, kp, vp, bias, unit, scale, n_seq, seq, t, side=None):
    nb = seq // t
    n_tok = n_seq * seq
    nq, nk_p = len(qp), len(kp)
    q_sh, k_sh = [p[2] for p in qp], [p[2] for p in kp]
    nbias = 2 if bias is not None else 0

    def body(*refs):
        q_refs, k_refs = refs[:nq], refs[nq:nq + nk_p]
        v_ref = refs[nq + nk_p]
        bias_refs = refs[nq + nk_p + 1:nq + nk_p + 1 + nbias]
        o_ref, lse_ref = refs[nq + nk_p + 1 + nbias:]
        qi = pl.program_id(1)
        lse_tile = jnp.zeros((t, LANES), F32)
        for h in range(HEADS):
            hs = slice(h * LANES, (h + 1) * LANES)
            q = _head_cat(q_refs, q_sh, slice(None), h)
            cq = _lane_pick(bias_refs[0][...], ROPE + h) if bias is not None else None

            def block(kb, carry, diag, h=h, hs=hs, q=q, cq=cq):
                m, l, acc = carry
                rows = _blk_rows(kb, t)
                s = lax.dot_general(q, _head_cat(k_refs, k_sh, rows, h), _DIMS["nt"], preferred_element_type=F32) * scale
                if bias is not None:
                    s = s + cq - bias_refs[1][kb, h:h + 1, :]
                if diag:
                    s = jnp.where(_diag_visible(t, unit), s, NEG_INF)
                m_new = jnp.maximum(m, jnp.max(s, axis=1, keepdims=True))
                alpha = jnp.exp(m - m_new)
                p = jnp.exp(s - m_new)
                l = alpha * l + jnp.sum(p, axis=1, keepdims=True)
                acc = alpha * acc + jnp.dot(p.astype(BF16), v_ref[rows, hs].astype(BF16), preferred_element_type=F32)
                return m_new, l, acc

            init = (jnp.full((t, 1), NEG_INF, F32), jnp.zeros((t, 1), F32), jnp.zeros((t, LANES), F32))
            carry = lax.fori_loop(0, qi, lambda kb, c: block(kb, c, False), init)
            m, l, acc = block(qi, carry, True)
            o_ref[:, hs] = acc / l
            lse_tile = _lane_put(lse_tile, h, m + jnp.log(l))
        lse_ref[...] = lse_tile

    tile_row = lambda b, i: b * nb + i
    seq_row = lambda b, i: b
    lane_tile = pl.BlockSpec((t, LANES), lambda b, i: (b * nb + i, 0))
    in_specs = _piece_specs(qp, t, tile_row) + _piece_specs(kp, seq, seq_row) + _piece_specs([vp + (False,)], seq, seq_row)
    args = [p[0] for p in qp] + [p[0] for p in kp] + [vp[0]]
    if bias is not None:
        in_specs += [lane_tile, pl.BlockSpec((None, nb, HEADS, t), lambda b, i: (b, 0, 0, 0))]
        args += list(bias)
    return _hosted_call(
        body, name, (n_seq, nb), in_specs,
        [pl.BlockSpec((t, HEADS * LANES), lambda b, i: (b * nb + i, 0)), lane_tile],
        [jax.ShapeDtypeStruct((n_tok, HEADS * LANES), F32), jax.ShapeDtypeStruct((n_tok, LANES), F32)], args, side)


def _attn_bwd_dq(name, qp, kp, vp, o, do, lse, bias, unit, scale, n_seq, seq, t, side=None):
    nb = seq // t
    n_tok = n_seq * seq
    nq, nk_p = len(qp), len(kp)
    q_sh, k_sh = [p[2] for p in qp], [p[2] for p in kp]
    nbias = 2 if bias is not None else 0
    n_in = nq + nk_p + 4 + nbias

    def body(*refs):
        q_refs, k_refs = refs[:nq], refs[nq:nq + nk_p]
        v_ref, o_ref, do_ref, lse_ref = refs[nq + nk_p:nq + nk_p + 4]
        bias_refs = refs[nq + nk_p + 4:n_in]
        dq_refs = refs[n_in:n_in + nq]
        delta_ref, dob_ref = refs[n_in + nq:n_in + nq + 2]
        qi = pl.program_id(1)
        delta_tile = jnp.zeros((t, LANES), F32)
        dc_tile = jnp.zeros((t, LANES), F32)
        lse_all = lse_ref[...]
        for h in range(HEADS):
            hs = slice(h * LANES, (h + 1) * LANES)
            q = _head_cat(q_refs, q_sh, slice(None), h)
            do_f = do_ref[:, hs]
            do_b = do_f.astype(BF16)
            dob_ref[:, hs] = do_b
            delta = jnp.sum(do_f * o_ref[:, hs], axis=1, keepdims=True)
            lse = _lane_pick(lse_all, h)
            cq = _lane_pick(bias_refs[0][...], ROPE + h) if bias is not None else None

            def block(kb, carry, diag, h=h, hs=hs, q=q, cq=cq, do_b=do_b, delta=delta, lse=lse):
                dq_acc, dc_acc = carry
                rows = _blk_rows(kb, t)
                k = _head_cat(k_refs, k_sh, rows, h)
                s = lax.dot_general(q, k, _DIMS["nt"], preferred_element_type=F32) * scale
                if bias is not None:
                    s = s + cq - bias_refs[1][kb, h:h + 1, :]
                if diag:
                    s = jnp.where(_diag_visible(t, unit), s, NEG_INF)
                p = jnp.exp(s - lse)
                dp = lax.dot_general(do_b, v_ref[rows, hs].astype(BF16), _DIMS["nt"], preferred_element_type=F32)
                ds = p * (dp - delta)
                return (dq_acc + jnp.dot(ds.astype(BF16), k, preferred_element_type=F32),
                        dc_acc + jnp.sum(ds, axis=1, keepdims=True))

            init = (jnp.zeros((t, nq * LANES), F32), jnp.zeros((t, 1), F32))
            carry = lax.fori_loop(0, qi, lambda kb, c: block(kb, c, False), init)
            dq_acc, dc_acc = block(qi, carry, True)
            for n_p in range(nq):
                dq_refs[n_p][:, hs] = dq_acc[:, n_p * LANES:(n_p + 1) * LANES] * scale
            delta_tile = _lane_put(delta_tile, h, delta)
            dc_tile = _lane_put(dc_tile, ROPE + h, dc_acc)
        delta_ref[...] = delta_tile
        if bias is not None:
            refs[n_in + nq + 2][...] = dc_tile

    tile_row = lambda b, i: b * nb + i
    seq_row = lambda b, i: b
    lane_tile = pl.BlockSpec((t, LANES), lambda b, i: (b * nb + i, 0))
    head_tile = pl.BlockSpec((t, HEADS * LANES), lambda b, i: (b * nb + i, 0))
    in_specs = _piece_specs(qp, t, tile_row) + _piece_specs(kp, seq, seq_row) + _piece_specs([vp + (False,)], seq, seq_row)
    in_specs += [head_tile, head_tile, lane_tile]
    args = [p[0] for p in qp] + [p[0] for p in kp] + [vp[0], o, do, lse]
    if bias is not None:
        in_specs += [lane_tile, pl.BlockSpec((None, nb, HEADS, t), lambda b, i: (b, 0, 0, 0))]
        args += list(bias)
    out_specs = [head_tile] * nq + [lane_tile, head_tile] + ([lane_tile] if bias is not None else [])
    out_shape = [jax.ShapeDtypeStruct((n_tok, HEADS * LANES), F32)] * nq
    out_shape += [jax.ShapeDtypeStruct((n_tok, LANES), F32), jax.ShapeDtypeStruct((n_tok, HEADS * LANES), BF16)]
    if bias is not None:
        out_shape.append(jax.ShapeDtypeStruct((n_tok, LANES), F32))
    return _hosted_call(body, name, (n_seq, nb), in_specs, out_specs, out_shape, args, side)


def _attn_bwd_dkv(name, qp, kp, vp, dob, lse, delta, bias, unit, scale, n_seq, seq, t, side=None):
    nb = seq // t
    n_tok = n_seq * seq
    nq, nk_p = len(qp), len(kp)
    q_sh, k_sh = [p[2] for p in qp], [p[2] for p in kp]
    nbias = 2 if bias is not None else 0
    n_in = nq + nk_p + 4 + nbias

    def body(*refs):
        q_refs, k_refs = refs[:nq], refs[nq:nq + nk_p]
        v_ref, dob_ref, lse_ref, delta_ref = refs[nq + nk_p:nq + nk_p + 4]
        bias_refs = refs[nq + nk_p + 4:n_in]
        dk_refs = refs[n_in:n_in + nk_p]
        dv_ref = refs[n_in + nk_p]
        ki = pl.program_id(1)
        shared_acc = [jnp.zeros((t, LANES), F32) for _ in range(nk_p)]
        for h in range(HEADS):
            hs = slice(h * LANES, (h + 1) * LANES)
            k = _head_cat(k_refs, k_sh, slice(None), h)
            v = v_ref[:, hs].astype(BF16)
            ck = bias_refs[1][h:h + 1, :] if bias is not None else None

            def block(qb, carry, diag, h=h, hs=hs, k=k, v=v, ck=ck):
                dk_acc, dv_acc, dc_acc = carry
                rows = _blk_rows(qb, t)
                q = _head_cat(q_refs, q_sh, rows, h)
                s = lax.dot_general(q, k, _DIMS["nt"], preferred_element_type=F32) * scale
                if bias is not None:
                    s = s + _lane_pick(bias_refs[0][rows, :], ROPE + h) - ck
                if diag:
                    s = jnp.where(_diag_visible(t, unit), s, NEG_INF)
                p = jnp.exp(s - _lane_pick(lse_ref[rows, :], h))
                do_b = dob_ref[rows, hs]
                dp = lax.dot_general(do_b, v, _DIMS["nt"], preferred_element_type=F32)
                ds = p * (dp - _lane_pick(delta_ref[rows, :], h))
                return (dk_acc + lax.dot_general(ds.astype(BF16), q, _DIMS["tn"], preferred_element_type=F32),
                        dv_acc + lax.dot_general(p.astype(BF16), do_b, _DIMS["tn"], preferred_element_type=F32),
                        dc_acc - jnp.sum(ds, axis=0, keepdims=True))

            init = (jnp.zeros((t, nk_p * LANES), F32), jnp.zeros((t, LANES), F32), jnp.zeros((1, t), F32))
            carry = block(ki, init, True)
            dk_acc, dv_acc, dc_acc = lax.fori_loop(ki + 1, nb, lambda qb, c: block(qb, c, False), carry)
            for n_p in range(nk_p):
                part = dk_acc[:, n_p * LANES:(n_p + 1) * LANES] * scale
                if k_sh[n_p]:
                    shared_acc[n_p] = shared_acc[n_p] + part
                else:
                    dk_refs[n_p][:, hs] = part
            dv_ref[:, hs] = dv_acc
            if bias is not None:
                refs[n_in + nk_p + 1][h:h + 1, :] = dc_acc
        for n_p in range(nk_p):
            if k_sh[n_p]:
                dk_refs[n_p][...] = shared_acc[n_p]

    tile_row = lambda b, i: b * nb + i
    seq_row = lambda b, i: b
    lane_seq = pl.BlockSpec((seq, LANES), lambda b, i: (b, 0))
    head_tile = pl.BlockSpec((t, HEADS * LANES), lambda b, i: (b * nb + i, 0))
    row_tile = pl.BlockSpec((None, None, HEADS, t), lambda b, i: (b, i, 0, 0))
    in_specs = _piece_specs(qp, seq, seq_row) + _piece_specs(kp, t, tile_row) + _piece_specs([vp + (False,)], t, tile_row)
    in_specs += [pl.BlockSpec((seq, HEADS * LANES), lambda b, i: (b, 0)), lane_seq, lane_seq]
    args = [p[0] for p in qp] + [p[0] for p in kp] + [vp[0], dob, lse, delta]
    if bias is not None:
        in_specs += [lane_seq, row_tile]
        args += list(bias)
    out_specs = [pl.BlockSpec((t, LANES if sh else HEADS * LANES), lambda b, i: (b * nb + i, 0)) for sh in k_sh] + [head_tile]
    out_shape = [jax.ShapeDtypeStruct((n_tok, LANES if sh else HEADS * LANES), F32) for sh in k_sh]
    out_shape.append(jax.ShapeDtypeStruct((n_tok, HEADS * LANES), F32))
    if bias is not None:
        out_specs.append(row_tile)
        out_shape.append(jax.ShapeDtypeStruct((n_seq, nb, HEADS, t), F32))
    return _hosted_call(body, name, (n_seq, nb), in_specs, out_specs, out_shape, args, side)


def _old_attn_bwd_dq(name, qp, kp, vp, o, do, lse, bias, unit, scale, n_seq, seq, t):
    nb = seq // t
    n_tok = n_seq * seq
    nq, nk_p = len(qp), len(kp)
    nbias = 2 if bias is not None else 0
    n_in = nq + nk_p + 4 + nbias
    n_out = nq + (1 if bias is not None else 0)

    def body(*refs):
        q_refs, k_refs = refs[:nq], refs[nq:nq + nk_p]
        v_ref, o_ref, do_ref, lse_ref = refs[nq + nk_p:nq + nk_p + 4]
        bias_refs = refs[nq + nk_p + 4:n_in]
        outs = refs[n_in:n_in + n_out]
        dq_s, delta_s, dc_s = refs[n_in + n_out:]
        qi, ki = pl.program_id(2), pl.program_id(3)

        @pl.when(ki == 0)
        def _():
            dq_s[...] = jnp.zeros_like(dq_s)
            dc_s[...] = jnp.zeros_like(dc_s)
            delta_s[...] = jnp.sum(do_ref[...] * o_ref[...], axis=1, keepdims=True)

        @pl.when(ki <= qi)
        def _():
            s = _scores(q_refs, k_refs, bias_refs, qi, ki, t, unit, scale)
            p = jnp.exp(s - lse_ref[...])
            dp = lax.dot_general(do_ref[...].astype(BF16), v_ref[...].astype(BF16), _DIMS["nt"],
                                 preferred_element_type=F32)
            ds = p * (dp - delta_s[...])
            dq_s[...] += jnp.dot(ds.astype(BF16), _cat(k_refs), preferred_element_type=F32)
            dc_s[...] += jnp.sum(ds, axis=1, keepdims=True)

        @pl.when(ki == qi)
        def _():
            for n_p in range(nq):
                outs[n_p][...] = dq_s[:, n_p * LANES:(n_p + 1) * LANES] * scale
            if bias is not None:
                outs[nq][...] = dc_s[...]

    q_row = lambda b, i, j: b * nb + i
    k_row = lambda b, i, j: b * nb + jnp.minimum(j, i)
    head_q = pl.BlockSpec((t, LANES), lambda b, h, i, j: (b * nb + i, h))
    col_q = pl.BlockSpec((None, t, 1), lambda b, h, i, j: (h, b * nb + i, 0))
    in_specs = [_piece_spec(t, p, q_row) for p in qp] + [_piece_spec(t, p, k_row) for p in kp]
    in_specs += [_piece_spec(t, vp, k_row), head_q, head_q, col_q]
    args = [p[0] for p in qp] + [p[0] for p in kp] + [vp[0], o, do, lse]
    if bias is not None:
        in_specs += [col_q, pl.BlockSpec((None, 1, t), lambda b, h, i, j: (b * HEADS + h, 0, jnp.minimum(j, i)))]
        args += list(bias)
    out_specs = [head_q] * nq + ([col_q] if bias is not None else [])
    out_shape = [jax.ShapeDtypeStruct((n_tok, HEADS * LANES), F32)] * nq
    if bias is not None:
        out_shape.append(jax.ShapeDtypeStruct((HEADS, n_tok, 1), F32))
    return pl.pallas_call(
        body, name=name, grid=(n_seq, HEADS, nb, nb), in_specs=in_specs, out_specs=out_specs, out_shape=out_shape,
        scratch_shapes=[pltpu.VMEM((t, nq * LANES), F32), pltpu.VMEM((t, 1), F32), pltpu.VMEM((t, 1), F32)],
        compiler_params=_params(("parallel", "parallel", "arbitrary", "arbitrary")),
    )(*args)


def _old_attn_bwd_dkv(name, qp, kp, vp, o, do, lse, bias, unit, scale, n_seq, seq, t):
    nb = seq // t
    n_tok = n_seq * seq
    nq, nk_p = len(qp), len(kp)
    nbias = 2 if bias is not None else 0
    n_in = nq + nk_p + 4 + nbias
    n_out = nk_p + 1 + (1 if bias is not None else 0)

    def body(*refs):
        q_refs, k_refs = refs[:nq], refs[nq:nq + nk_p]
        v_ref, o_ref, do_ref, lse_ref = refs[nq + nk_p:nq + nk_p + 4]
        bias_refs = refs[nq + nk_p + 4:n_in]
        outs = refs[n_in:n_in + n_out]
        dk_s, dv_s, dc_s = refs[n_in + n_out:]
        ki, qi = pl.program_id(2), pl.program_id(3)

        @pl.when(qi == 0)
        def _():
            dk_s[...] = jnp.zeros_like(dk_s)
            dv_s[...] = jnp.zeros_like(dv_s)
            dc_s[...] = jnp.zeros_like(dc_s)

        @pl.when(qi >= ki)
        def _():
            s = _scores(q_refs, k_refs, bias_refs, qi, ki, t, unit, scale)
            p = jnp.exp(s - lse_ref[...])
            do_b = do_ref[...].astype(BF16)
            delta = jnp.sum(do_ref[...] * o_ref[...], axis=1, keepdims=True)
            dp = lax.dot_general(do_b, v_ref[...].astype(BF16), _DIMS["nt"], preferred_element_type=F32)
            ds = p * (dp - delta)
            dv_s[...] += lax.dot_general(p.astype(BF16), do_b, _DIMS["tn"], preferred_element_type=F32)
            dk_s[...] += lax.dot_general(ds.astype(BF16), _cat(q_refs), _DIMS["tn"], preferred_element_type=F32)
            dc_s[...] -= jnp.sum(ds, axis=0, keepdims=True)

        @pl.when(qi == nb - 1)
        def _():
            for n_p in range(nk_p):
                outs[n_p][...] = dk_s[:, n_p * LANES:(n_p + 1) * LANES] * scale
            outs[nk_p][...] = dv_s[...]
            if bias is not None:
                outs[nk_p + 1][...] = dc_s[...]

    q_row = lambda b, i, j: b * nb + jnp.maximum(j, i)
    k_row = lambda b, i, j: b * nb + i
    head_q = pl.BlockSpec((t, LANES), lambda b, h, i, j: (b * nb + jnp.maximum(j, i), h))
    col_q = pl.BlockSpec((None, t, 1), lambda b, h, i, j: (h, b * nb + jnp.maximum(j, i), 0))
    head_k = pl.BlockSpec((t, LANES), lambda b, h, i, j: (b * nb + i, h))
    row_k = pl.BlockSpec((None, 1, t), lambda b, h, i, j: (b * HEADS + h, 0, i))
    in_specs = [_piece_spec(t, p, q_row) for p in qp] + [_piece_spec(t, p, k_row) for p in kp]
    in_specs += [_piece_spec(t, vp, k_row), head_q, head_q, col_q]
    args = [p[0] for p in qp] + [p[0] for p in kp] + [vp[0], o, do, lse]
    if bias is not None:
        in_specs += [col_q, row_k]
        args += list(bias)
    out_specs = [head_k] * (nk_p + 1) + ([row_k] if bias is not None else [])
    out_shape = [jax.ShapeDtypeStruct((n_tok, HEADS * LANES), F32)] * (nk_p + 1)
    if bias is not None:
        out_shape.append(jax.ShapeDtypeStruct((n_seq * HEADS, 1, seq), F32))
    return pl.pallas_call(
        body, name=name, grid=(n_seq, HEADS, nb, nb), in_specs=in_specs, out_specs=out_specs, out_shape=out_shape,
        scratch_shapes=[pltpu.VMEM((t, nk_p * LANES), F32), pltpu.VMEM((t, LANES), F32), pltpu.VMEM((1, t), F32)],
        compiler_params=_params(("parallel", "parallel", "arbitrary", "arbitrary")),
    )(*args)


def _seq_cumsum(name, x, col_block, n_seq, seq, reverse, pre=None, vec=None):
    t = _tile(seq, 256, 128)
    nb = seq // t

    def body(*refs):
        x_ref = refs[0]
        vec_ref = refs[1] if vec is not None else None
        o_ref, carry = refs[-2], refs[-1]

        @pl.when(pl.program_id(1) == 0)
        def _():
            carry[...] = jnp.zeros_like(carry)

        v = x_ref[...]
        if pre is not None:
            v = pre(v, vec_ref[...])
        r = lax.broadcasted_iota(jnp.int32, (t, t), 0)
        c = lax.broadcasted_iota(jnp.int32, (t, t), 1)
        tri = jnp.where((c >= r) if reverse else (c <= r), 1.0, 0.0).astype(BF16)
        hi = v.astype(BF16)
        mid = (v - hi.astype(F32)).astype(BF16)
        lo = (v - hi.astype(F32) - mid.astype(F32)).astype(BF16)
        acc = jnp.dot(tri, hi, preferred_element_type=F32)
        acc += jnp.dot(tri, mid, preferred_element_type=F32)
        acc += jnp.dot(tri, lo, preferred_element_type=F32)
        o_ref[...] = acc + carry[...]
        carry[...] += _colsum(v)

    blk = (lambda b, i: (b * nb + nb - 1 - i)) if reverse else (lambda b, i: (b * nb + i))
    in_specs = [pl.BlockSpec((t, LANES), lambda b, i: (blk(b, i), col_block))]
    args = [x]
    if vec is not None:
        in_specs.append(pl.BlockSpec(vec.shape, lambda b, i: (0, 0)))
        args.append(vec)
    return pl.pallas_call(
        body, name=name, grid=(n_seq, nb), in_specs=in_specs,
        out_specs=pl.BlockSpec((t, LANES), lambda b, i: (blk(b, i), 0)),
        out_shape=jax.ShapeDtypeStruct((n_seq * seq, LANES), F32),
        scratch_shapes=[pltpu.VMEM((1, LANES), F32)],
        compiler_params=_params(("arbitrary", "arbitrary")),
    )(*args)


def _log_sigmoid(z):
    return -(jnp.maximum(-z, 0.0) + jnp.log(1.0 + jnp.exp(-jnp.abs(z))))


def _shift_down(u, prev_ref, n):
    out = pltpu.roll(u, n, 0)
    row = lax.broadcasted_iota(jnp.int32, u.shape, 0)
    for r in range(n):
        out = jnp.where(row == r, prev_ref[8 - n + r:8 - n + r + 1, :], out)
    return out


def _shift_up(u, next_ref, n):
    ts = u.shape[0]
    out = pltpu.roll(u, ts - n, 0)
    row = lax.broadcasted_iota(jnp.int32, u.shape, 0)
    for r in range(n):
        out = jnp.where(row == ts - n + r, next_ref[r:r + 1, :], out)
    return out


def _conv_taps(u, prev_ref, w_ref, b_ref):
    s1, s2 = _shift_down(u, prev_ref, 1), _shift_down(u, prev_ref, 2)
    return (w_ref[0:1, :] * s2 + w_ref[1:2, :] * s1 + w_ref[2:3, :] * u) + b_ref[...], s1, s2


def _conv_glu_fwd(u_il, cw_il, cb_il, n_seq, seq, wt):
    n_tok, two_f = u_il.shape
    nct = two_f // (2 * wt)
    ts = _tile(seq, 256, 8)
    ns = seq // ts

    def body(u_ref, w_ref, b_ref, a_ref, carry):
        @pl.when(pl.program_id(2) == 0)
        def _():
            carry[...] = jnp.zeros_like(carry)

        u = u_ref[...].astype(F32)
        uc, _, _ = _conv_taps(u, carry, w_ref, b_ref)
        gel, _ = _gelu_parts(uc[:, :wt])
        a_ref[...] = (gel * uc[:, wt:]).astype(a_ref.dtype)
        carry[...] = u[ts - 8:, :]

    return pl.pallas_call(
        body, name="conv_glu_fwd", grid=(nct, n_seq, ns),
        in_specs=[pl.BlockSpec((ts, 2 * wt), lambda j, b, s: (b * ns + s, j)),
                  pl.BlockSpec((3, 2 * wt), lambda j, b, s: (0, j)),
                  pl.BlockSpec((1, 2 * wt), lambda j, b, s: (0, j))],
        out_specs=pl.BlockSpec((ts, wt), lambda j, b, s: (b * ns + s, j)),
        out_shape=jax.ShapeDtypeStruct((n_tok, two_f // 2), BF16),
        scratch_shapes=[pltpu.VMEM((8, 2 * wt), F32)],
        compiler_params=_params(("parallel", "arbitrary", "arbitrary")),
    )(u_il, cw_il, cb_il)


def _conv_glu_bwd_pre(u_il, da, cw_il, cb_il, n_seq, seq, wt):
    n_tok, two_f = u_il.shape
    nct = two_f // (2 * wt)
    ts = _tile(seq, 256, 8)
    ns = seq // ts

    def body(u_ref, da_ref, w_ref, b_ref, d_ref, acc_ref, carry):
        first = jnp.logical_and(pl.program_id(1) == 0, pl.program_id(2) == 0)

        @pl.when(first)
        def _():
            acc_ref[...] = jnp.zeros_like(acc_ref)

        @pl.when(pl.program_id(2) == 0)
        def _():
            carry[...] = jnp.zeros_like(carry)

        u = u_ref[...].astype(F32)
        uc, s1, s2 = _conv_taps(u, carry, w_ref, b_ref)
        gel, dgel = _gelu_parts(uc[:, :wt])
        da_v = da_ref[...].astype(F32)
        d = jnp.concatenate([da_v * uc[:, wt:] * dgel, da_v * gel], axis=1)
        d_ref[...] = d.astype(d_ref.dtype)
        acc_ref[0:1, :] += _colsum(d * s2)
        acc_ref[1:2, :] += _colsum(d * s1)
        acc_ref[2:3, :] += _colsum(d * u)
        acc_ref[3:4, :] += _colsum(d)
        carry[...] = u[ts - 8:, :]

    return pl.pallas_call(
        body, name="conv_glu_bwd_pre", grid=(nct, n_seq, ns),
        in_specs=[pl.BlockSpec((ts, 2 * wt), lambda j, b, s: (b * ns + s, j)),
                  pl.BlockSpec((ts, wt), lambda j, b, s: (b * ns + s, j)),
                  pl.BlockSpec((3, 2 * wt), lambda j, b, s: (0, j)),
                  pl.BlockSpec((1, 2 * wt), lambda j, b, s: (0, j))],
        out_specs=[pl.BlockSpec((ts, 2 * wt), lambda j, b, s: (b * ns + s, j)),
                   pl.BlockSpec((8, 2 * wt), lambda j, b, s: (0, j))],
        out_shape=[jax.ShapeDtypeStruct((n_tok, two_f), BF16), jax.ShapeDtypeStruct((8, two_f), F32)],
        scratch_shapes=[pltpu.VMEM((8, 2 * wt), F32)],
        compiler_params=_params(("parallel", "arbitrary", "arbitrary")),
    )(u_il, da, cw_il, cb_il)


def _conv_bwd_input(d_il, cw_il, n_seq, seq, wt):
    n_tok, two_f = d_il.shape
    nct = two_f // (2 * wt)
    ts = _tile(seq, 256, 8)
    ns = seq // ts

    def body(d_ref, w_ref, o_ref, carry):
        @pl.when(pl.program_id(2) == 0)
        def _():
            carry[...] = jnp.zeros_like(carry)

        d = d_ref[...].astype(F32)
        o_ref[...] = (w_ref[2:3, :] * d + w_ref[1:2, :] * _shift_up(d, carry, 1)
                      + w_ref[0:1, :] * _shift_up(d, carry, 2)).astype(o_ref.dtype)
        carry[...] = d[:8, :]

    rev = lambda j, b, s: (b * ns + ns - 1 - s, j)
    return pl.pallas_call(
        body, name="conv_bwd_input", grid=(nct, n_seq, ns),
        in_specs=[pl.BlockSpec((ts, 2 * wt), rev), pl.BlockSpec((3, 2 * wt), lambda j, b, s: (0, j))],
        out_specs=pl.BlockSpec((ts, 2 * wt), rev),
        out_shape=jax.ShapeDtypeStruct((n_tok, two_f), BF16),
        scratch_shapes=[pltpu.VMEM((8, 2 * wt), F32)],
        compiler_params=_params(("parallel", "arbitrary", "arbitrary")),
    )(d_il, cw_il)


HBM = pl.BlockSpec(memory_space=pltpu.HBM)
_CHIP_FLIPS = ((1, 0), (0, 1), (1, 1))


def _place():
    x, y, c = lax.axis_index("x"), lax.axis_index("y"), lax.axis_index("c")
    return x, y, c, 2 * x + y


def _flip(v, f):
    return 1 - v if f else v


def _half_rows(c, half):
    return pl.ds(pl.multiple_of(c * half, 16), half)


def _remote(src, dst, ssem, rsem, dev):
    return pltpu.make_async_remote_copy(src_ref=src, dst_ref=dst, send_sem=ssem, recv_sem=rsem,
                                        device_id=dev, device_id_type=MESH)


def _comm_call(name, body, ins, out_shapes, n_sems):
    return pl.pallas_call(
        body, name=name, in_specs=[HBM] * len(ins), out_specs=[HBM] * len(out_shapes),
        out_shape=[pltpu.HBM(s.shape, s.dtype) for s in out_shapes],
        scratch_shapes=[pltpu.SemaphoreType.DMA((n_sems,)), pltpu.SemaphoreType.DMA((n_sems,))],
    )(*ins)


def _all_gather_weights(shards, smalls):
    side = _gather_side(shards, smalls)
    nt = len(shards) + len(smalls)

    def body(*refs):
        for part in (side.start, side.mid, side.end):
            part(refs[:nt], refs[nt:2 * nt], *refs[2 * nt:])

    res = _comm_call("all_gather_weights", body, side.ins, side.outs, side.n_sems)
    return res[:len(shards)], res[len(shards):]


def _pair_split(name, grads):
    n = len(grads)

    def body(*refs):
        src, got = refs[:n], refs[n:2 * n]
        ssem, rsem = refs[2 * n:]
        x, y, c, _ = _place()
        cps = []
        for w in range(n):
            half = grads[w].shape[1] // 2
            cp = _remote(src[w].at[:, _half_rows(1 - c, half)], got[w], ssem.at[w], rsem.at[w], (x, y, 1 - c))
            cp.start()
            cps.append(cp)
        for cp in cps:
            cp.wait()

    outs = [jax.ShapeDtypeStruct((g.shape[0], g.shape[1] // 2, g.shape[2]), g.dtype) for g in grads]
    return _comm_call(name, body, grads, outs, n)


def _chip_scatter(parts):
    side = _scatter_side(parts)

    def body(*refs):
        n = len(parts)
        side.start(refs[:n], refs[n:2 * n], *refs[2 * n:])
        side.end(refs[:n], refs[n:2 * n], *refs[2 * n:])

    return _comm_call("rs_chip_scatter", body, parts, side.outs, side.n_sems)


class _Side:
    def __init__(self, ins, outs, n_sems, start, mid, end, mid_step=None):
        self.ins, self.outs, self.n_sems = list(ins), list(outs), n_sems
        self.start, self.mid, self.end, self.mid_step = start, mid, end, mid_step


def _scatter_side(parts):
    n = len(parts)

    def copies(src, dst, ssem, rsem):
        x, y, c, _ = _place()
        out = []
        for w in range(n):
            for k, (fx, fy) in enumerate(_CHIP_FLIPS):
                px, py = _flip(x, fx), _flip(y, fy)
                out.append(_remote(src[w].at[2 * px + py], dst[w].at[k], ssem.at[w * 3 + k], rsem.at[w * 3 + k], (px, py, c)))
        return out

    def start(src, dst, ssem, rsem):
        for cp in copies(src, dst, ssem, rsem):
            cp.start()

    def end(src, dst, ssem, rsem):
        for cp in copies(src, dst, ssem, rsem):
            cp.wait()

    outs = [jax.ShapeDtypeStruct((3,) + p.shape[1:], p.dtype) for p in parts]
    return _Side(parts, outs, 3 * n, start, None, end)


def _gather_side(shards, smalls, mid_step=None):
    n, ns = len(shards), len(smalls)

    def ici(src, dst, ssem, rsem, w, k):
        x, y, c, me = _place()
        fx, fy = _CHIP_FLIPS[k]
        rows = _half_rows(c, shards[w].shape[0] // 2)
        return _remote(src[w].at[rows], dst[w].at[me, rows], ssem.at[w * 6 + k], rsem.at[w * 6 + k],
                       (_flip(x, fx), _flip(y, fy), c))

    def small(src, dst, ssem, rsem, s, k):
        x, y, c, me = _place()
        fx, fy = _CHIP_FLIPS[k]
        sem = 6 * n + 3 * s + k
        return _remote(src[n + s], dst[n + s].at[me], ssem.at[sem], rsem.at[sem], (_flip(x, fx), _flip(y, fy), c))

    def landed(dst, ssem, rsem, w, k, sender_c, sem_off):
        x, y, c, _ = _place()
        fx, fy = _CHIP_FLIPS[k]
        got = dst[w].at[2 * _flip(x, fx) + _flip(y, fy), _half_rows(sender_c, shards[w].shape[0] // 2)]
        return _remote(got, got, ssem.at[w * 6 + sem_off + k], rsem.at[w * 6 + sem_off + k], (x, y, 1 - c))

    def start(src, dst, ssem, rsem):
        for s in range(ns):
            for k in range(3):
                small(src, dst, ssem, rsem, s, k).start()
        for w in range(n):
            for k in range(3):
                ici(src, dst, ssem, rsem, w, k).start()

    def mid(src, dst, ssem, rsem):
        c = lax.axis_index("c")
        for w in range(n):
            for k in range(3):
                landed(dst, ssem, rsem, w, k, c, 0).wait_recv()
                landed(dst, ssem, rsem, w, k, c, 3).start()

    def end(src, dst, ssem, rsem):
        c = lax.axis_index("c")
        for w in range(n):
            for k in range(3):
                landed(dst, ssem, rsem, w, k, 1 - c, 3).wait_recv()
        for s in range(ns):
            for k in range(3):
                small(src, dst, ssem, rsem, s, k).wait()
        for w in range(n):
            for k in range(3):
                ici(src, dst, ssem, rsem, w, k).wait_send()
                landed(dst, ssem, rsem, w, k, c, 3).wait_send()

    outs = [jax.ShapeDtypeStruct((N_CHIPS,) + a.shape, a.dtype) for a in list(shards) + list(smalls)]
    return _Side(list(shards) + list(smalls), outs, 6 * n + 3 * ns, start, mid, end, mid_step)


def _host(body, n_in, n_out, side, grid):
    if side is None:
        return body
    ns_in, ns_out = len(side.ins), len(side.outs)
    n_steps = math.prod(grid)
    mid_step = side.mid_step
    if side.mid is not None and not isinstance(mid_step, int):
        mid_step = min(n_steps - 1, int(mid_step * n_steps))

    def wrapped(*refs):
        ins, s_ins = refs[:n_in], refs[n_in:n_in + ns_in]
        outs = refs[n_in + ns_in:n_in + ns_in + n_out]
        s_outs = refs[n_in + ns_in + n_out:n_in + ns_in + n_out + ns_out]
        rest = refs[n_in + ns_in + n_out + ns_out:]
        sems = rest[-2:]
        step = 0
        for axis, extent in enumerate(grid):
            step = step * extent + pl.program_id(axis)

        @pl.when(step == 0)
        def _():
            side.start(s_ins, s_outs, *sems)

        if side.mid is not None:
            @pl.when(step == mid_step)
            def _():
                side.mid(s_ins, s_outs, *sems)

        body(*ins, *outs, *rest[:-2])

        @pl.when(step == n_steps - 1)
        def _():
            side.end(s_ins, s_outs, *sems)

    return wrapped


def _hosted_call(body, name, grid, in_specs, out_specs, out_shape, args, side, semantics=("parallel", "arbitrary"),
                 scratch=()):
    n_in, n_out = len(in_specs), len(out_specs)
    kern = _host(body, n_in, n_out, side, grid)
    if side is None:
        return pl.pallas_call(kern, name=name, grid=grid, in_specs=in_specs, out_specs=out_specs, out_shape=out_shape,
                              scratch_shapes=list(scratch), compiler_params=_params(semantics))(*args), []
    res = pl.pallas_call(
        kern, name=name, grid=grid, in_specs=in_specs + [HBM] * len(side.ins), out_specs=out_specs + [HBM] * len(side.outs),
        out_shape=list(out_shape) + [pltpu.HBM(s.shape, s.dtype) for s in side.outs],
        scratch_shapes=list(scratch) + [pltpu.SemaphoreType.DMA((side.n_sems,)), pltpu.SemaphoreType.DMA((side.n_sems,))],
        compiler_params=_params(("arbitrary",) * len(grid)),
    )(*args, *side.ins)
    return res[:n_out], res[n_out:]


def _pair_swap(halves):
    n = len(halves)

    def body(*refs):
        src, dst = refs[:n], refs[n:2 * n]
        ssem, rsem = refs[2 * n:]
        x, y, c, _ = _place()
        cps = []
        for w in range(n):
            cp = _remote(src[w], dst[w], ssem.at[w], rsem.at[w], (x, y, 1 - c))
            cp.start()
            cps.append(cp)
        for cp in cps:
            cp.wait()

    outs = [jax.ShapeDtypeStruct(h.shape, h.dtype) for h in halves]
    return _comm_call("rs_pair_swap", body, halves, outs, n)


def _gather_small(vec):
    def body(src, dst, ssem, rsem):
        x, y, c, _ = _place()
        me = 4 * x + 2 * y + c
        cps = []
        for r in range(1, 8):
            dev = (_flip(x, r & 4), _flip(y, r & 2), _flip(c, r & 1))
            cp = _remote(src, dst.at[me], ssem.at[r - 1], rsem.at[r - 1], dev)
            cp.start()
            cps.append(cp)
        for cp in cps:
            cp.wait()

    out = jax.ShapeDtypeStruct((8,) + vec.shape, vec.dtype)
    return _comm_call("gather_small", body, [vec], [out], 7)[0]


def _pair_add(name, g, theirs, core):
    n, half, b = theirs.shape
    tr = _tile(half, 256, 16)
    nt = half // tr

    def body(c_ref, g_ref, t_ref, o_ref):
        o_ref[...] = (g_ref[...].astype(F32) + t_ref[...].astype(F32)).astype(o_ref.dtype)

    same = pl.BlockSpec((None, tr, b), lambda j, i, c: (j, i, 0))
    grid_spec = pltpu.PrefetchScalarGridSpec(
        num_scalar_prefetch=1, grid=(n, nt),
        in_specs=[pl.BlockSpec((None, tr, b), lambda j, i, c: (j, c[0] * nt + i, 0)), same], out_specs=same)
    return pl.pallas_call(body, name=name, grid_spec=grid_spec, out_shape=jax.ShapeDtypeStruct(theirs.shape, BF16),
                          compiler_params=_params(("parallel", "parallel")))(core, g, theirs)


def _sum_slots(name, stacked, first=None):
    n, r, c = stacked.shape
    tr = _tile(r, 256, 8)

    def body(*refs):
        s_ref, o_ref = refs[-2], refs[-1]
        acc = refs[0][...].astype(F32) if first is not None else s_ref[0].astype(F32)
        for s in range(0 if first is not None else 1, n):
            acc = acc + s_ref[s].astype(F32)
        o_ref[...] = acc

    row_spec = pl.BlockSpec((tr, c), lambda i: (i, 0))
    in_specs = ([row_spec] if first is not None else []) + [pl.BlockSpec((n, tr, c), lambda i: (0, i, 0))]
    args = ([first] if first is not None else []) + [stacked]
    return pl.pallas_call(
        body, name=name, grid=(r // tr,), in_specs=in_specs, out_specs=row_spec,
        out_shape=jax.ShapeDtypeStruct((r, c), F32), compiler_params=_params(("parallel",)),
    )(*args)


def _adamw(name, w, g, m, v):
    r, c = w.shape
    by_cols = r % 8 != 0 and c % LANES == 0
    tr, tc = (r, _tile(c, 256, LANES)) if by_cols else (_tile(r, 256, 8), c)
    bc1, bc2 = 1.0 - ADAM_B1 ** ADAM_STEP, 1.0 - ADAM_B2 ** ADAM_STEP

    def body(w_ref, g_ref, m_ref, v_ref, d_ref, nm_ref, nv_ref):
        g_v = g_ref[...]
        nm = ADAM_B1 * m_ref[...] + (1.0 - ADAM_B1) * g_v
        nv = ADAM_B2 * v_ref[...] + (1.0 - ADAM_B2) * (g_v * g_v)
        d_ref[...] = -ADAM_LR * ((nm / bc1) / (jnp.sqrt(nv / bc2) + ADAM_EPS) + ADAM_WD * w_ref[...])
        nm_ref[...] = nm
        nv_ref[...] = nv

    spec = pl.BlockSpec((tr, tc), (lambda i: (0, i)) if by_cols else (lambda i: (i, 0)))
    return pl.pallas_call(
        body, name=name, grid=(c // tc if by_cols else r // tr,), in_specs=[spec] * 4, out_specs=[spec] * 3,
        out_shape=[jax.ShapeDtypeStruct((r, c), F32)] * 3, compiler_params=_params(("parallel",)),
    )(w, g, m, v)


def _pad_cols(a, cols):
    return jnp.pad(a, ((0, 0), (0, cols - a.shape[1])))


def _rot_cols(w):
    h = w.shape[-1] // 2
    return jnp.concatenate([-w[..., h:], w[..., :h]], axis=-1)


def _unrot_cols(d):
    h = d.shape[-1] // 2
    return jnp.concatenate([d[..., h:], -d[..., :h]], axis=-1)


def _logical(g):
    return jnp.transpose(g, (1, 0, 2)).reshape(g.shape[1], N_CHIPS * g.shape[2])


def _chunks(a, n):
    return jnp.transpose(a.reshape(a.shape[0], N_CHIPS, n), (1, 0, 2))


def kernel(x, positions, pre_mix_norm, w_in, q_a_norm, w_uq, kv_a_norm, w_ukv, b_forget, b_gate, w_branch_mla, w_branch_fox, w_out, post_mix_norm, pre_ffn_norm, w_up, conv_w, conv_b, w_down, post_ffn_norm, loss_target, m_pre_mix_norm, m_w_in, m_q_a_norm, m_w_uq, m_kv_a_norm, m_w_ukv, m_b_forget, m_b_gate, m_w_branch_mla, m_w_branch_fox, m_w_out, m_post_mix_norm, m_pre_ffn_norm, m_w_up, m_conv_w, m_conv_b, m_w_down, m_post_ffn_norm, v_pre_mix_norm, v_w_in, v_q_a_norm, v_w_uq, v_kv_a_norm, v_w_ukv, v_b_forget, v_b_gate, v_w_branch_mla, v_w_branch_fox, v_w_out, v_post_mix_norm, v_pre_ffn_norm, v_w_up, v_conv_w, v_conv_b, v_w_down, v_post_ffn_norm):
    n_seq, seq, d = x.shape
    n_tok = n_seq * seq
    d_in = N_CHIPS * w_in.shape[1]
    two_f = N_CHIPS * w_up.shape[1]
    ff_dim = two_f // 2
    assert d_in == QL + KVL + ROPE + 3 * HEADS * FDIM + HEADS + 2 * d
    n_in_shard = w_in.shape[1]
    in_pad = -(-n_in_shard // LANES) * LANES
    hd = HEADS * LANES
    xc, yc, cc = lax.axis_index("x"), lax.axis_index("y"), lax.axis_index("c")
    chip = 2 * xc + yc
    t_attn = _tile(seq, 512, 128)

    shards = [_pad_cols(w_in, in_pad).astype(BF16), w_uq.astype(BF16), w_ukv.astype(BF16), w_branch_mla.astype(BF16),
              w_branch_fox.astype(BF16), w_out.astype(BF16), w_up.astype(BF16), w_down.astype(BF16)]
    cw8 = jnp.pad(conv_w, ((0, 5), (0, 0)))
    put_own = lambda g, s: lax.dynamic_update_slice(g, s[None], (chip, 0, 0))
    gathered, (g_cw,) = _all_gather_weights(shards[:3], [cw8])
    g_in, g_uq, g_ukv = [put_own(g, s) for g, s in zip(gathered, shards[:3])]
    g_cw = put_own(g_cw, cw8)
    n_attn_steps = n_seq * (seq // t_attn)
    up_rows = shards[6].shape[0] // 2
    side_proj = _gather_side([shards[3], shards[4], shards[5]], [], mid_step=0.9)
    side_mla = _gather_side([shards[6][:up_rows]], [], mid_step=max(n_attn_steps - 2, 0))
    side_fox = _gather_side([shards[6][up_rows:]], [], mid_step=max(n_attn_steps - 2, 0))
    side_ffn = _gather_side([shards[7]], [], mid_step=0.7)

    o_q, o_kv, o_kpe = 0, QL, QL + KVL
    o_f = o_kpe + ROPE
    o_fl = o_f + 3 * hd
    o_g = o_fl + HEADS

    def chip_cols(lo, hi):
        out = []
        while lo < hi:
            j = lo // n_in_shard
            end = min(hi, (j + 1) * n_in_shard)
            out.append((j, lo - j * n_in_shard, end - j * n_in_shard))
            lo = end
        return out

    take = lambda lo, hi: [g_in[j, :, a:b] for j, a, b in chip_cols(lo, hi)]
    w_kpe = jnp.concatenate(take(o_kpe, o_f), axis=1)
    zeros = lambda n: jnp.zeros((d, n), BF16)
    win_p = jnp.concatenate(
        take(o_g, d_in) + take(o_q, o_kpe) + [w_kpe, zeros(LANES - ROPE), _rot_cols(w_kpe)] + take(o_fl, o_g)
        + [zeros(LANES - ROPE - HEADS)] + take(o_f, o_fl), axis=1)
    n_p = win_p.shape[1]
    cb_gm, cb_gf = 0, 1
    c_lat = 2 * d
    c_kx, c_kr = c_lat + QL + KVL, c_lat + QL + KVL + LANES
    n_pa = c_kr + LANES
    assert n_p == n_pa + 3 * hd

    uq3 = _logical(g_uq).reshape(QL, HEADS, NOPE + ROPE)
    pe = uq3[:, :, NOPE:]
    pad_pe = lambda a: jnp.pad(a, ((0, 0), (0, 0), (0, LANES - ROPE))).reshape(QL, hd)
    wuq_p = jnp.concatenate([uq3[:, :, :NOPE].reshape(QL, hd), pad_pe(pe), pad_pe(_rot_cols(pe))], axis=1)
    ukv3 = _logical(g_ukv).reshape(KVL, HEADS, NOPE + VDIM)
    wukv_p = jnp.concatenate([ukv3[:, :, :NOPE].reshape(KVL, hd), ukv3[:, :, NOPE:].reshape(KVL, hd)], axis=1)

    n_bm, n_up = w_branch_mla.shape[1], w_up.shape[1]
    l_bm, l_up = _Chunked(n_bm), _Chunked(n_up)
    wt = n_up // 2
    n_ut = two_f // wt
    il = lambda cblk: jnp.where(cblk < n_ut // 2, 2 * cblk, 2 * (cblk - n_ut // 2) + 1)
    l_il = _Plain(il)
    to_il = lambda a: a.reshape(a.shape[0], 2, n_ut // 2, wt).transpose(0, 2, 1, 3).reshape(a.shape[0], two_f)
    from_il = lambda a: a.reshape(a.shape[0], n_ut // 2, 2, wt).transpose(0, 2, 1, 3).reshape(a.shape[0], two_f)

    inv_freq = 1.0 / (ROPE_THETA ** (jnp.arange(0, ROPE, 2, dtype=F32) / ROPE))
    ang = positions.astype(F32).reshape(n_tok, 1) * inv_freq
    cos, sin = jnp.cos(ang), jnp.sin(ang)
    cs = _pad_cols(jnp.concatenate([cos, cos], axis=1), LANES)
    sn = _pad_cols(jnp.concatenate([sin, sin], axis=1), LANES)

    row = lambda v: v.reshape(1, -1)
    x2 = x.reshape(n_tok, d)
    tgt = loss_target.reshape(n_tok, d)

    (h,) = _rows("rms_pre_mix", lambda r, v: ([r[0] * _rstd(r[0]) * v[0]], []),
                 [(x2, d, 0)], [row(pre_mix_norm)], [(d, BF16)], [], n_tok)
    proj, got = _mm("proj_in", "nn", h, win_p, n_tok, n_pa, d, side=side_proj)
    g_bm, g_bf, g_out = [put_own(g, s) for g, s in zip(got, side_proj.ins)]
    w_out_full = g_out.reshape(d, d)
    tn_f = _tile(3 * hd, 1024, 128)
    assert n_pa % tn_f == 0
    proj_f = _mm("proj_in_fox", "nn", h, win_p, n_tok, 3 * hd, d, tn=tn_f, lb=_Plain(lambda cblk: cblk + n_pa // tn_f),
                 out_dtype=BF16)

    bf_vec = jnp.pad(row(b_forget), ((0, 0), (ROPE, LANES - ROPE - HEADS)))

    def lat_fwd(r, v):
        ql, kvl = r[0], r[1]
        return [ql * _rstd(ql) * v[0], kvl * _rstd(kvl) * v[1], r[2] * r[4] + r[3] * r[5]], []

    qn, kvn, rk = _rows("latent_norms", lat_fwd,
                        [(proj, QL, c_lat // QL), (proj, KVL, (c_lat + QL) // KVL), (proj, LANES, c_kx // LANES),
                         (proj, LANES, c_kr // LANES), (cs, LANES, 0), (sn, LANES, 0)],
                        [row(q_a_norm), row(kv_a_norm)], [(QL, BF16), (KVL, BF16), (LANES, BF16)], [], n_tok)
    q_p = _mm("q_up", "nn", qn, wuq_p, n_tok, 3 * hd, QL)
    kv_p = _mm("kv_up", "nn", kvn, wukv_p, n_tok, 2 * hd, KVL, out_dtype=BF16)

    def rope_q(r, v):
        c8, s8 = jnp.tile(r[3], (1, HEADS)), jnp.tile(r[4], (1, HEADS))
        return [r[0], r[1] * c8 + r[2] * s8], []

    q_nope, rq = _rows("rope_q", rope_q, [(q_p, hd, 0), (q_p, hd, 1), (q_p, hd, 2), (cs, LANES, 0), (sn, LANES, 0)], [],
                       [(hd, BF16), (hd, BF16)], [], n_tok)

    mla_q = [(q_nope, 0, False), (rq, 0, False)]
    mla_k = [(kv_p, 0, False), (rk, 0, True)]
    mla_v = (kv_p, 1)
    mla_scale = (NOPE + ROPE) ** -0.5
    (o_mla, lse_mla), got = _attn_fwd("mla_fwd", mla_q, mla_k, mla_v, None, CHUNK, mla_scale, n_seq, seq, t_attn, side=side_mla)
    g_up_top = put_own(got[0], side_mla.ins[0])

    c_run = _seq_cumsum("forget_cumsum", proj, c_kr // LANES, n_seq, seq, False,
                        pre=lambda z, b: _log_sigmoid(z + b), vec=bf_vec)
    nb_attn = seq // t_attn
    c_rowf = jnp.transpose(c_run[:, ROPE:ROPE + HEADS].reshape(n_seq, nb_attn, t_attn, HEADS), (0, 1, 3, 2))
    fox_q, fox_k, fox_v = [(proj_f, 0, False)], [(proj_f, 1, False)], (proj_f, 2)
    fox_scale = FDIM ** -0.5
    fox_bias = (c_run, c_rowf)
    (o_fox, lse_fox), got = _attn_fwd("fox_fwd", fox_q, fox_k, fox_v, fox_bias, 1, fox_scale, n_seq, seq, t_attn, side=side_fox)
    g_up = jnp.concatenate([g_up_top, put_own(got[0], side_fox.ins[0])], axis=1)

    pm = _mm("branch_mla", "nn", o_mla, g_bm, n_tok, d, hd, lb=l_bm, tn=n_bm)
    pf = _mm("branch_fox", "nn", o_fox, g_bf, n_tok, d, hd, lb=l_bm, tn=n_bm)
    bg = row(b_gate)

    def merge(r, v):
        return [_sigmoid(r[0] + v[0]) * r[2] + _sigmoid(r[1] + v[1]) * r[3]], []

    (merged,) = _rows("gate_merge", merge, [(proj, d, cb_gm), (proj, d, cb_gf), (pm, d, 0), (pf, d, 0)],
                      [bg[:, :d], bg[:, d:]], [(d, BF16)], [], n_tok)
    y1 = _mm("mix_out", "nn", merged, w_out_full, n_tok, d, d)

    def resid_norm(r, v):
        x1v = r[0] + r[1] * _rstd(r[1]) * v[0]
        return [x1v, x1v * _rstd(x1v) * v[1]], []

    x1, h2 = _rows("post_mix_pre_ffn", resid_norm, [(x2, d, 0), (y1, d, 0)], [row(post_mix_norm), row(pre_ffn_norm)],
                   [(d, F32), (d, BF16)], [], n_tok)

    u_il, got = _mm("ffn_up", "nn", h2, g_up, n_tok, two_f, d, lb=l_up, lo=l_il, tn=wt, out_dtype=BF16, side=side_ffn)
    w_down_full = put_own(got[0], side_ffn.ins[0]).reshape(ff_dim, d)
    cw_il = to_il(_logical(g_cw)[:3])
    cb_il = to_il(row(conv_b))
    act = _conv_glu_fwd(u_il, cw_il, cb_il, n_seq, seq, wt)
    ff = _mm("ffn_down", "nn", act, w_down_full, n_tok, d, ff_dim)

    def final(r, v):
        x1v, ffv, tg = r
        diff = x1v + ffv * _rstd(ffv) * v[0] - tg
        dx2v = diff / d
        dffv, dg4 = _rms_bwd(ffv, v[0], dx2v)
        sq = jnp.sum(jnp.sum(diff * diff, axis=1, keepdims=True), axis=0, keepdims=True)
        return [dx2v, dffv], [dg4, jnp.broadcast_to(sq, (1, LANES))]

    dx2, dff, dg_post_ffn, sq_sum = _rows("loss_post_ffn_bwd", final, [(x1, d, 0), (ff, d, 0), (tgt, d, 0)],
                                          [row(post_ffn_norm)], [(d, F32), (d, BF16)], [(1, d), (1, LANES)], n_tok)
    rs_parts, rs_landed = {}, {}
    core = jnp.reshape(cc, (1,)).astype(jnp.int32)

    def pair_reduce(tag, names, grads):
        theirs = _pair_split("rs_pair_split_" + tag, grads)
        for nm, g, b in zip(names, grads, theirs):
            rs_parts[nm] = _pair_add("rs_pair_add_" + nm, g, b, core)

    dact = _mm("ffn_down_dx", "nt", dff, w_down_full, n_tok, ff_dim, d, tn=wt, out_dtype=BF16)
    gw_down = _mm("ffn_down_dw", "tn", act, dff, ff_dim, d, n_tok, tm=wt, out_dtype=BF16)
    pair_reduce("down", ["w_down"], [gw_down.reshape(N_CHIPS, ff_dim // N_CHIPS, d)])
    d_il, conv_acc = _conv_glu_bwd_pre(u_il, dact, cw_il, cb_il, n_seq, seq, wt)
    du_il = _conv_bwd_input(d_il, cw_il, n_seq, seq, wt)
    gw_up, got = _mm("ffn_up_dw", "tn", h2, du_il, d, two_f, n_tok, lb=l_il, lo=l_up, tn=wt, out_dtype=BF16,
                     side=_scatter_side([rs_parts["w_down"]]))
    rs_landed["w_down"] = got[0]
    pair_reduce("up", ["w_up"], [gw_up])
    dh2, got = _mm("ffn_up_dx", "nt", du_il, g_up, n_tok, d, two_f, la=l_il, lb=l_up, tk=wt,
                   side=_scatter_side([rs_parts["w_up"]]))
    rs_landed["w_up"] = got[0]

    def mid_bwd(r, v):
        x1v, y1v, dx2v, dh2v = r
        d3, dg3 = _rms_bwd(x1v, v[1], dh2v)
        dx1v = dx2v + d3
        dy1v, dg2 = _rms_bwd(y1v, v[0], dx1v)
        return [dx1v, dy1v], [dg3, dg2]

    dx1, dy1, dg_pre_ffn, dg_post_mix = _rows(
        "pre_ffn_post_mix_bwd", mid_bwd, [(x1, d, 0), (y1, d, 0), (dx2, d, 0), (dh2, d, 0)],
        [row(post_mix_norm), row(pre_ffn_norm)], [(d, F32), (d, BF16)], [(1, d), (1, d)], n_tok)
    dmerged = _mm("mix_out_dx", "nt", dy1, w_out_full, n_tok, d, d)
    gw_out = _mm("mix_out_dw", "tn", merged, dy1, d, d, n_tok, out_dtype=BF16)

    def gate_bwd(r, v):
        zm, zf, pmv, pfv, dm = r
        gm, gf = _sigmoid(zm + v[0]), _sigmoid(zf + v[1])
        dzm, dzf = dm * pmv * gm * (1.0 - gm), dm * pfv * gf * (1.0 - gf)
        return [dm * gm, dm * gf, dzm, dzf], [_colsum(dzm), _colsum(dzf)]

    dpm, dpf, dzm, dzf, dbg_m, dbg_f = _rows(
        "gate_merge_bwd", gate_bwd, [(proj, d, cb_gm), (proj, d, cb_gf), (pm, d, 0), (pf, d, 0), (dmerged, d, 0)],
        [bg[:, :d], bg[:, d:]], [(d, BF16)] * 4, [(1, d), (1, d)], n_tok)
    tk_b = min(n_bm, 512)
    do_mla = _mm("branch_mla_dx", "nt", dpm, g_bm, n_tok, hd, d, lb=l_bm, tk=tk_b)
    do_fox = _mm("branch_fox_dx", "nt", dpf, g_bf, n_tok, hd, d, lb=l_bm, tk=tk_b)
    gw_bm = _mm("branch_mla_dw", "tn", o_mla, dpm, hd, d, n_tok, lo=l_bm, tn=n_bm, out_dtype=BF16)
    gw_bf = _mm("branch_fox_dw", "tn", o_fox, dpf, hd, d, n_tok, lo=l_bm, tn=n_bm, out_dtype=BF16)

    pair_reduce("mix", ["w_out", "w_branch_mla", "w_branch_fox"], [gw_out.reshape(N_CHIPS, d // N_CHIPS, d), gw_bm, gw_bf])
    (dq_nope, drq, delta_mla, dob_mla), got = _attn_bwd_dq(
        "mla_bwd_dq", mla_q, mla_k, mla_v, o_mla, do_mla, lse_mla, None, CHUNK, mla_scale, n_seq, seq, t_attn,
        side=_scatter_side([rs_parts[nm] for nm in ("w_out", "w_branch_mla", "w_branch_fox")]))
    rs_landed.update(zip(("w_out", "w_branch_mla", "w_branch_fox"), got))
    (dk_nope, drk, dv_mla), _ = _attn_bwd_dkv(
        "mla_bwd_dkv", mla_q, mla_k, mla_v, dob_mla, lse_mla, delta_mla, None, CHUNK, mla_scale, n_seq, seq, t_attn)
    (dfq, delta_fox, dob_fox, dc_q), _ = _attn_bwd_dq("fox_bwd_dq", fox_q, fox_k, fox_v, o_fox, do_fox, lse_fox, fox_bias, 1,
                                                      fox_scale, n_seq, seq, t_attn)
    (dfk, dfv, dc_k), _ = _attn_bwd_dkv("fox_bwd_dkv", fox_q, fox_k, fox_v, dob_fox, lse_fox, delta_fox, fox_bias, 1, fox_scale,
                                        n_seq, seq, t_attn)
    dc_k8 = jnp.transpose(dc_k, (0, 1, 3, 2)).reshape(n_tok, HEADS)
    dc128 = dc_q + jnp.pad(dc_k8, ((0, 0), (ROPE, LANES - ROPE - HEADS)))
    dlogf = _seq_cumsum("forget_cumsum_bwd", dc128, 0, n_seq, seq, True)

    def mla_pack(r, v):
        dqn_v, drq_v, dkn_v, dv_v, drk_v, c1, s1 = r
        c8, s8 = jnp.tile(c1, (1, HEADS)), jnp.tile(s1, (1, HEADS))
        return [jnp.concatenate([dqn_v, drq_v * c8, drq_v * s8], axis=1), jnp.concatenate([dkn_v, dv_v], axis=1),
                drk_v * c1, drk_v * s1], []

    dq_p, dkv_p, dkx, dkr = _rows(
        "mla_rope_bwd", mla_pack,
        [(dq_nope, hd, 0), (drq, hd, 0), (dk_nope, hd, 0), (dv_mla, hd, 0), (drk, LANES, 0), (cs, LANES, 0), (sn, LANES, 0)],
        [], [(3 * hd, BF16), (2 * hd, BF16), (LANES, F32), (LANES, F32)], [], n_tok)
    dqn = _mm("q_up_dx", "nt", dq_p, wuq_p, n_tok, QL, 3 * hd)
    gw_uq_p = _mm("q_up_dw", "tn", qn, dq_p, QL, 3 * hd, n_tok, out_dtype=BF16)
    dkvn = _mm("kv_up_dx", "nt", dkv_p, wukv_p, n_tok, KVL, 2 * hd)
    gw_ukv_p = _mm("kv_up_dw", "tn", kvn, dkv_p, KVL, 2 * hd, n_tok, out_dtype=BF16)

    def lat_bwd(r, v):
        ql, kvl, dqn_v, dkvn_v, dkx_v, dkr_v, zblk, dlf = r
        dql, dgq = _rms_bwd(ql, v[0], dqn_v)
        dkvl, dgkv = _rms_bwd(kvl, v[1], dkvn_v)
        dfl = dlf * _sigmoid(-(zblk + v[2]))
        return [jnp.concatenate([dql, dkvl, dkx_v, dkr_v + dfl], axis=1)], [dgq, dgkv, _colsum(dfl)]

    dlat, dg_q, dg_kv, dbf = _rows(
        "latent_bwd", lat_bwd,
        [(proj, QL, c_lat // QL), (proj, KVL, (c_lat + QL) // KVL), (dqn, QL, 0), (dkvn, KVL, 0), (dkx, LANES, 0),
         (dkr, LANES, 0), (proj, LANES, c_kr // LANES), (dlogf, LANES, 0)],
        [row(q_a_norm), row(kv_a_norm), bf_vec], [(QL + KVL + 2 * LANES, BF16)], [(1, QL), (1, KVL), (1, LANES)], n_tok)
    dproj = jnp.concatenate([dzm, dzf, dlat, dfq.astype(BF16), dfk.astype(BF16), dfv.astype(BF16)], axis=1)
    gw_in_p = _mm("proj_in_dw", "tn", h, dproj, d, n_p, n_tok, out_dtype=BF16)

    f32 = lambda a: a.astype(F32)
    kr_blk = gw_in_p[:, c_kr:c_kr + LANES]
    d_kpe = (f32(gw_in_p[:, c_kx:c_kx + ROPE]) + _unrot_cols(f32(kr_blk[:, :ROPE]))).astype(BF16)
    in_pieces = [(o_q, gw_in_p, c_lat, QL + KVL), (o_kpe, d_kpe, 0, ROPE), (o_f, gw_in_p, n_pa, 3 * hd),
                 (o_fl, kr_blk, ROPE, HEADS), (o_g, gw_in_p, 0, 2 * d)]
    gc_in = []
    for j in range(N_CHIPS):
        lo, hi, cols = j * n_in_shard, (j + 1) * n_in_shard, []
        for first, arr, at, width in in_pieces:
            a, b = max(lo, first), min(hi, first + width)
            if a < b:
                cols.append(arr[:, at + a - first:at + b - first])
        cols.append(jnp.zeros((d, in_pad - n_in_shard), BF16))
        gc_in.append(jnp.concatenate(cols, axis=1))
    gc_in = jnp.stack(gc_in)
    uq_parts = [gw_uq_p[:, i * hd:(i + 1) * hd].reshape(QL, HEADS, LANES) for i in range(3)]
    d_pe = (f32(uq_parts[1][:, :, :ROPE]) + _unrot_cols(f32(uq_parts[2][:, :, :ROPE]))).astype(BF16)
    gc_uq = _chunks(jnp.concatenate([uq_parts[0], d_pe], axis=2).reshape(QL, HEADS * (NOPE + ROPE)), w_uq.shape[1])
    gc_ukv = _chunks(jnp.concatenate([gw_ukv_p[:, :hd].reshape(KVL, HEADS, NOPE), gw_ukv_p[:, hd:].reshape(KVL, HEADS, VDIM)],
                                     axis=2).reshape(KVL, HEADS * (NOPE + VDIM)), w_ukv.shape[1])
    grads = [gc_in, gc_uq, gc_ukv]

    late = ["w_in", "w_uq", "w_ukv"]
    pair_reduce("late", late, grads)
    dh, got = _mm("proj_in_dx", "nt", dproj, win_p, n_tok, d, n_p, side=_scatter_side([rs_parts[nm] for nm in late]))
    rs_landed.update(zip(late, got))

    def first_bwd(r, v):
        dxa, dg1 = _rms_bwd(r[0], v[0], r[1])
        return [r[2] + dxa], [dg1]

    grad_x, dg_pre_mix = _rows("pre_mix_bwd", first_bwd, [(x2, d, 0), (dh, d, 0), (dx1, d, 0)], [row(pre_mix_norm)],
                               [(d, F32)], [(1, d)], n_tok)
    big = list(rs_parts)
    halves = [_sum_slots("rs_chip_sum_" + nm, rs_landed[nm],
                         first=lax.dynamic_index_in_dim(rs_parts[nm], chip, 0, keepdims=False)) for nm in big]
    other = _pair_swap(halves)
    full = [jnp.concatenate([jnp.where(cc == 0, a, b), jnp.where(cc == 0, b, a)], axis=0) for a, b in zip(halves, other)]
    g_big = dict(zip(big, full))
    g_big["w_in"] = g_big["w_in"][:, :n_in_shard]

    conv_acc_l = from_il(conv_acc)
    pieces = [dg_pre_mix, dg_q, dg_kv, dbf, dbg_m, dbg_f, dg_post_mix, dg_pre_ffn, conv_acc_l[3:4], dg_post_ffn,
              conv_acc_l[0:1], conv_acc_l[1:2], conv_acc_l[2:3], sq_sum]
    sizes = [p.shape[1] for p in pieces]
    flat = jnp.concatenate(pieces, axis=1)
    n_rows = -(-flat.shape[1] // (8 * LANES)) * 8
    flat = _pad_cols(flat, n_rows * LANES).reshape(n_rows, LANES)
    slots = lax.dynamic_update_slice(_gather_small(flat), flat[None], (2 * chip + cc, 0, 0))
    total = _sum_slots("small_sum", slots).reshape(1, n_rows * LANES)
    offs = [sum(sizes[:i]) for i in range(len(sizes))]
    tot = [total[0, o:o + s] for o, s in zip(offs, sizes)]
    loss = 0.5 * tot[13][0] / d
    g_small = {"pre_mix_norm": tot[0], "q_a_norm": tot[1], "kv_a_norm": tot[2], "b_forget": tot[3][ROPE:ROPE + HEADS],
               "b_gate": jnp.concatenate([tot[4], tot[5]]), "post_mix_norm": tot[6], "pre_ffn_norm": tot[7],
               "conv_b": tot[8], "post_ffn_norm": tot[9]}
    gcw_full = jnp.stack([tot[10], tot[11], tot[12]])
    g_conv_w = lax.dynamic_slice(gcw_full, (0, chip * n_up), (3, n_up))

    given = dict(pre_mix_norm=(pre_mix_norm, m_pre_mix_norm, v_pre_mix_norm), w_in=(w_in, m_w_in, v_w_in),
                 q_a_norm=(q_a_norm, m_q_a_norm, v_q_a_norm), w_uq=(w_uq, m_w_uq, v_w_uq),
                 kv_a_norm=(kv_a_norm, m_kv_a_norm, v_kv_a_norm), w_ukv=(w_ukv, m_w_ukv, v_w_ukv),
                 b_forget=(b_forget, m_b_forget, v_b_forget), b_gate=(b_gate, m_b_gate, v_b_gate),
                 w_branch_mla=(w_branch_mla, m_w_branch_mla, v_w_branch_mla),
                 w_branch_fox=(w_branch_fox, m_w_branch_fox, v_w_branch_fox), w_out=(w_out, m_w_out, v_w_out),
                 post_mix_norm=(post_mix_norm, m_post_mix_norm, v_post_mix_norm),
                 pre_ffn_norm=(pre_ffn_norm, m_pre_ffn_norm, v_pre_ffn_norm), w_up=(w_up, m_w_up, v_w_up),
                 conv_w=(conv_w, m_conv_w, v_conv_w), conv_b=(conv_b, m_conv_b, v_conv_b),
                 w_down=(w_down, m_w_down, v_w_down), post_ffn_norm=(post_ffn_norm, m_post_ffn_norm, v_post_ffn_norm))
    order = list(given)
    grad, delta, new_m, new_v = {}, {}, {}, {}
    for nm in big + ["conv_w"]:
        grad[nm] = g_big[nm] if nm in g_big else g_conv_w
        if nm == "w_in":
            tr_out = _adamw("adamw_" + nm, *[jnp.transpose(a) for a in (given[nm][0], grad[nm], given[nm][1], given[nm][2])])
            delta[nm], new_m[nm], new_v[nm] = [jnp.transpose(a) for a in tr_out]
            continue
        delta[nm], new_m[nm], new_v[nm] = _adamw("adamw_" + nm, given[nm][0], grad[nm], given[nm][1], given[nm][2])
    small = list(g_small)
    padded = [-(-g_small[nm].shape[0] // LANES) * LANES for nm in small]
    s_rows = -(-sum(padded) // (8 * LANES)) * 8

    def pack(vals):
        cat = jnp.concatenate([jnp.pad(a, (0, p - a.shape[0])) for a, p in zip(vals, padded)])
        return jnp.pad(cat, (0, s_rows * LANES - cat.shape[0])).reshape(s_rows, LANES)

    packed = _adamw("adamw_small", pack([given[nm][0] for nm in small]), pack([g_small[nm] for nm in small]),
                    pack([given[nm][1] for nm in small]), pack([given[nm][2] for nm in small]))
    s_offs = [sum(padded[:i]) for i in range(len(small))]
    for nm, o in zip(small, s_offs):
        n_el = g_small[nm].shape[0]
        grad[nm] = g_small[nm]
        delta[nm], new_m[nm], new_v[nm] = [p.reshape(-1)[o:o + n_el] for p in packed]
    return (loss, grad_x.reshape(n_seq, seq, d), *[grad[nm] for nm in order], *[delta[nm] for nm in order],
            *[new_m[nm] for nm in order], *[new_v[nm] for nm in order])
```

```python
import functools
import math

import jax
import jax.numpy as jnp
from jax import lax
from jax.experimental import pallas as pl
from jax.experimental.pallas import tpu as pltpu

F32, BF16 = jnp.float32, jnp.bfloat16
MESH = pl.DeviceIdType.MESH

HEADS = 8
NOPE, ROPE, VDIM = 128, 64, 128
QL, KVL = 512, 256
FDIM = 128
CHUNK = 64
ROPE_THETA = 10000.0
EPS = 1e-6
NEG_INF = -1e30
ADAM_LR, ADAM_B1, ADAM_B2, ADAM_EPS, ADAM_WD, ADAM_STEP = 0.001, 0.9, 0.999, 1e-08, 0.01, 10

VMEM_LIMIT_BYTES = 52 * 1024 * 1024
LANES = 128
N_CHIPS = 4


def _params(sem):
    return pltpu.CompilerParams(dimension_semantics=sem, vmem_limit_bytes=VMEM_LIMIT_BYTES)


def _tile(n, target, mult):
    if n <= target:
        return n
    t = (target // mult) * mult
    while t >= mult:
        if n % t == 0:
            return t
        t -= mult
    raise ValueError(f"no tile for {n} (target {target}, multiple of {mult})")


class _Plain:
    def __init__(self, perm=None):
        self.perm = perm

    def spec(self, tr, tc, rc):
        perm = self.perm

        def imap(i, j, k):
            r, c = rc(i, j, k)
            return (r, perm(c) if perm is not None else c)

        return pl.BlockSpec((tr, tc), imap)

    def shape(self, rows, cols):
        return (rows, cols)


class _Chunked:
    def __init__(self, n):
        self.n = n

    def spec(self, tr, tc, rc):
        assert self.n % tc == 0, (self.n, tc)
        per = self.n // tc

        def imap(i, j, k):
            r, c = rc(i, j, k)
            return (c // per, r, c % per)

        return pl.BlockSpec((None, tr, tc), imap)

    def shape(self, rows, cols):
        assert cols == N_CHIPS * self.n
        return (N_CHIPS, rows, self.n)


_DIMS = {"nn": (((1,), (0,)), ((), ())), "nt": (((1,), (1,)), ((), ())), "tn": (((0,), (0,)), ((), ()))}


def _mm_single(name, mode, a, b, m, n, k, tm, tn, la, lb, lo, out_dtype, side):
    if mode == "nn":
        a_spec = la.spec(tm, k, lambda i, j, kk: (i, 0))
        b_spec = lb.spec(k, tn, lambda i, j, kk: (0, j))
    elif mode == "nt":
        a_spec = la.spec(tm, k, lambda i, j, kk: (i, 0))
        b_spec = lb.spec(tn, k, lambda i, j, kk: (j, 0))
    else:
        a_spec = la.spec(k, tm, lambda i, j, kk: (0, i))
        b_spec = lb.spec(k, tn, lambda i, j, kk: (0, j))
    o_spec = lo.spec(tm, tn, lambda i, j, kk: (i, j))
    dims = _DIMS[mode]

    def body(a_ref, b_ref, o_ref):
        o_ref[...] = lax.dot_general(a_ref[...].astype(BF16), b_ref[...].astype(BF16), dims,
                                     preferred_element_type=F32).astype(o_ref.dtype)

    (out,), got = _hosted_call(body, name, (m // tm, n // tn, 1), [a_spec, b_spec], [o_spec],
                               [jax.ShapeDtypeStruct(lo.shape(m, n), out_dtype)], (a, b), side,
                               semantics=("parallel", "parallel", "arbitrary"))
    return out if side is None else (out, got)


def _mm(name, mode, a, b, m, n, k, *, tm=1024, tn=1024, tk=2048, la=None, lb=None, lo=None, out_dtype=F32, side=None):
    la, lb, lo = la or _Plain(), lb or _Plain(), lo or _Plain()
    tm, tn, tk = _tile(m, tm, 128), _tile(n, tn, 128), _tile(k, tk, 128)
    nk = k // tk
    if nk == 1:
        return _mm_single(name, mode, a, b, m, n, k, tm, tn, la, lb, lo, out_dtype, side)
    if mode == "nn":
        a_spec = la.spec(tm, tk, lambda i, j, kk: (i, kk))
        b_spec = lb.spec(tk, tn, lambda i, j, kk: (kk, j))
    elif mode == "nt":
        a_spec = la.spec(tm, tk, lambda i, j, kk: (i, kk))
        b_spec = lb.spec(tn, tk, lambda i, j, kk: (j, kk))
    else:
        a_spec = la.spec(tk, tm, lambda i, j, kk: (kk, i))
        b_spec = lb.spec(tk, tn, lambda i, j, kk: (kk, j))
    o_spec = lo.spec(tm, tn, lambda i, j, kk: (i, j))
    dims = _DIMS[mode]

    def body(a_ref, b_ref, o_ref, acc_ref):
        kk = pl.program_id(2)

        @pl.when(kk == 0)
        def _():
            acc_ref[...] = jnp.zeros_like(acc_ref)

        acc_ref[...] += lax.dot_general(a_ref[...].astype(BF16), b_ref[...].astype(BF16), dims,
                                        preferred_element_type=F32)

        @pl.when(kk == nk - 1)
        def _():
            o_ref[...] = acc_ref[...].astype(o_ref.dtype)

    (out,), got = _hosted_call(body, name, (m // tm, n // tn, nk), [a_spec, b_spec], [o_spec],
                               [jax.ShapeDtypeStruct(lo.shape(m, n), out_dtype)], (a, b), side,
                               semantics=("parallel", "parallel", "arbitrary"), scratch=[pltpu.VMEM((tm, tn), F32)])
    return out if side is None else (out, got)


def _mm_parts(name, mode, a, b, m, n, k, *, part=1024, tm=1024, tn=1024, tk=2048, out_dtype=F32, side=None):
    parts = b if mode == "tn" else a
    widths = [p.shape[1] for p in parts]
    assert all(w % part == 0 for w in widths) and sum(widths) == (n if mode == "tn" else k)
    offs = [sum(widths[:i]) // part for i in range(len(widths))]
    nblk = [w // part for w in widths]
    if mode == "tn":
        tn, tk = part, _tile(k, tk, 128)
    else:
        tk, tn = part, _tile(n, tn, 128)
    tm = _tile(m, tm, 128)
    nk = k // tk
    grid = (m // tm, n // tn, nk)
    np_ = len(parts)

    def inside(idx, p):
        return jnp.logical_and(idx >= offs[p], idx < offs[p] + nblk[p])

    def part_spec(p):
        if mode == "tn":
            def imap(i, j, kk):
                on = inside(j, p)
                return (jnp.where(on, kk, 0), jnp.clip(j - offs[p], 0, nblk[p] - 1))
            return pl.BlockSpec((tk, tn), imap)

        def imap(i, j, kk):
            return (i, jnp.clip(kk - offs[p], 0, nblk[p] - 1))
        return pl.BlockSpec((tm, tk), imap)

    if mode == "tn":
        in_specs = [pl.BlockSpec((tk, tm), lambda i, j, kk: (kk, i))] + [part_spec(p) for p in range(np_)]
        args = [a] + list(parts)
    else:
        in_specs = [part_spec(p) for p in range(np_)] + [pl.BlockSpec((tn, tk), lambda i, j, kk: (j, kk))]
        args = list(parts) + [b]
    dims = _DIMS[mode]

    def body(*refs):
        o_ref, acc_ref = refs[-2], refs[-1]
        j, kk = pl.program_id(1), pl.program_id(2)

        @pl.when(kk == 0)
        def _():
            acc_ref[...] = jnp.zeros_like(acc_ref)

        for p in range(np_):
            @pl.when(inside(j if mode == "tn" else kk, p))
            def _(p=p):
                lhs, rhs = (refs[0], refs[1 + p]) if mode == "tn" else (refs[p], refs[np_])
                acc_ref[...] += lax.dot_general(lhs[...].astype(BF16), rhs[...].astype(BF16), dims, preferred_element_type=F32)

        @pl.when(kk == nk - 1)
        def _():
            o_ref[...] = acc_ref[...].astype(o_ref.dtype)

    (out,), got = _hosted_call(body, name, grid, in_specs, [pl.BlockSpec((tm, tn), lambda i, j, kk: (i, j))],
                               [jax.ShapeDtypeStruct((m, n), out_dtype)], args, side,
                               semantics=("parallel", "parallel", "arbitrary"), scratch=[pltpu.VMEM((tm, tn), F32)])
    return out if side is None else (out, got)


def _rows(name, fn, rows_in, vecs_in, rows_out, accs_out, n_rows, tr=256):
    tr = _tile(n_rows, tr, 16)
    nr, nv, no = len(rows_in), len(vecs_in), len(rows_out)

    def body(*refs):
        ins, vecs = refs[:nr], refs[nr:nr + nv]
        outs, accs = refs[nr + nv:nr + nv + no], refs[nr + nv + no:]
        ro, ac = fn([r[...] for r in ins], [v[...] for v in vecs])
        for o_ref, val in zip(outs, ro):
            o_ref[...] = val.astype(o_ref.dtype)
        if accs:
            @pl.when(pl.program_id(0) == 0)
            def _():
                for a_ref in accs:
                    a_ref[...] = jnp.zeros_like(a_ref)

            for a_ref, val in zip(accs, ac):
                a_ref[...] += val

    in_specs = [pl.BlockSpec((tr, cols), functools.partial(lambda i, cb: (i, cb), cb=cb)) for _, cols, cb in rows_in]
    in_specs += [pl.BlockSpec(v.shape, lambda i: (0, 0)) for v in vecs_in]
    out_specs = [pl.BlockSpec((tr, cols), lambda i: (i, 0)) for cols, _ in rows_out]
    out_specs += [pl.BlockSpec((r, cols), lambda i: (0, 0)) for r, cols in accs_out]
    out_shape = [jax.ShapeDtypeStruct((n_rows, cols), dt) for cols, dt in rows_out]
    out_shape += [jax.ShapeDtypeStruct((r, cols), F32) for r, cols in accs_out]
    res = pl.pallas_call(
        body, name=name, grid=(n_rows // tr,), in_specs=in_specs, out_specs=out_specs, out_shape=out_shape,
        compiler_params=_params(("arbitrary",)),
    )(*[a for a, _, _ in rows_in], *vecs_in)
    return res


def _colsum(v):
    return jnp.sum(v, axis=0, keepdims=True)


def _rstd(x):
    return lax.rsqrt(jnp.mean(x * x, axis=-1, keepdims=True) + EPS)


def _rms_bwd(x, g, dy):
    r = _rstd(x)
    xh = x * r
    dxh = dy * g
    dx = r * (dxh - xh * jnp.mean(dxh * xh, axis=-1, keepdims=True))
    return dx, _colsum(dy * xh)


def _sigmoid(z):
    return 1.0 / (1.0 + jnp.exp(-z))


_GELU_K = math.sqrt(2.0 / math.pi)


def _gelu_parts(g):
    t = jnp.tanh(_GELU_K * (g + 0.044715 * g * g * g))
    gel = 0.5 * g * (1.0 + t)
    dgel = 0.5 * (1.0 + t) + 0.5 * g * (1.0 - t * t) * (_GELU_K * (1.0 + 3.0 * 0.044715 * g * g))
    return gel, dgel


def _diag_visible(t, unit):
    rows = lax.broadcasted_iota(jnp.int32, (t, t), 0)
    cols = lax.broadcasted_iota(jnp.int32, (t, t), 1)
    if unit > 1:
        sh = int(math.log2(unit))
        assert 1 << sh == unit and t % unit == 0
        rows, cols = jnp.right_shift(rows, sh), jnp.right_shift(cols, sh)
    return cols <= rows


def _lane_pick(tile, lane):
    idx = lax.broadcasted_iota(jnp.int32, tile.shape, 1)
    return jnp.sum(jnp.where(idx == lane, tile, 0.0), axis=1, keepdims=True)


def _lane_put(tile, lane, col):
    idx = lax.broadcasted_iota(jnp.int32, tile.shape, 1)
    return jnp.where(idx == lane, col, tile)


def _head_cat(refs, shared, rows, h):
    hs = slice(h * LANES, (h + 1) * LANES)
    vals = [(r[rows, :] if sh else r[rows, hs]).astype(BF16) for r, sh in zip(refs, shared)]
    return vals[0] if len(vals) == 1 else jnp.concatenate(vals, axis=1)


def _blk_rows(i, t):
    return pl.ds(pl.multiple_of(i * t, t), t)


def _piece_specs(pieces, rows, row_idx):
    return [pl.BlockSpec((rows, LANES if sh else HEADS * LANES), functools.partial(lambda b, i, cb: (row_idx(b, i), cb), cb=cb))
            for _, cb, sh in pieces]


def _attn_fwd(name, qp, kp, vp, bias, unit, scale, n_seq, seq, t, side=None):
    nb = seq // t
    n_tok = n_seq * seq
    nq, nk_p = len(qp), len(kp)
    q_sh, k_sh = [p[2] for p in qp], [p[2] for p in kp]
    nbias = 2 if bias is not None else 0

    def body(*refs):
        q_refs, k_refs = refs[:nq], refs[nq:nq + nk_p]
        v_ref = refs[nq + nk_p]
        bias_refs = refs[nq + nk_p + 1:nq + nk_p + 1 + nbias]
        o_ref, lse_ref = refs[nq + nk_p + 1 + nbias:]
        qi = pl.program_id(1)
        lse_tile = jnp.zeros((t, LANES), F32)
        for h in range(HEADS):
            hs = slice(h * LANES, (h + 1) * LANES)
            q = _head_cat(q_refs, q_sh, slice(None), h)
            cq = _lane_pick(bias_refs[0][...], ROPE + h) if bias is not None else None

            def block(kb, carry, diag, h=h, hs=hs, q=q, cq=cq):
                m, l, acc = carry
                rows = _blk_rows(kb, t)
                s = lax.dot_general(q, _head_cat(k_refs, k_sh, rows, h), _DIMS["nt"], preferred_element_type=F32) * scale
                if bias is not None:
                    s = s + cq - bias_refs[1][kb, h:h + 1, :]
                if diag:
                    s = jnp.where(_diag_visible(t, unit), s, NEG_INF)
                m_new = jnp.maximum(m, jnp.max(s, axis=1, keepdims=True))
                alpha = jnp.exp(m - m_new)
                p = jnp.exp(s - m_new)
                l = alpha * l + jnp.sum(p, axis=1, keepdims=True)
                acc = alpha * acc + jnp.dot(p.astype(BF16), v_ref[rows, hs].astype(BF16), preferred_element_type=F32)
                return m_new, l, acc

            init = (jnp.full((t, 1), NEG_INF, F32), jnp.zeros((t, 1), F32), jnp.zeros((t, LANES), F32))
            carry = lax.fori_loop(0, qi, lambda kb, c: block(kb, c, False), init)
            m, l, acc = block(qi, carry, True)
            o_ref[:, hs] = acc / l
            lse_tile = _lane_put(lse_tile, h, m + jnp.log(l))
        lse_ref[...] = lse_tile

    tile_row = lambda b, i: b * nb + i
    seq_row = lambda b, i: b
    lane_tile = pl.BlockSpec((t, LANES), lambda b, i: (b * nb + i, 0))
    in_specs = _piece_specs(qp, t, tile_row) + _piece_specs(kp, seq, seq_row) + _piece_specs([vp + (False,)], seq, seq_row)
    args = [p[0] for p in qp] + [p[0] for p in kp] + [vp[0]]
    if bias is not None:
        in_specs += [lane_tile, pl.BlockSpec((None, nb, HEADS, t), lambda b, i: (b, 0, 0, 0))]
        args += list(bias)
    return _hosted_call(
        body, name, (n_seq, nb), in_specs,
        [pl.BlockSpec((t, HEADS * LANES), lambda b, i: (b * nb + i, 0)), lane_tile],
        [jax.ShapeDtypeStruct((n_tok, HEADS * LANES), F32), jax.ShapeDtypeStruct((n_tok, LANES), F32)], args, side)


def _attn_bwd_dq(name, qp, kp, vp, o, do, lse, bias, unit, scale, n_seq, seq, t, side=None, grad_dtype=F32):
    nb = seq // t
    n_tok = n_seq * seq
    nq, nk_p = len(qp), len(kp)
    q_sh, k_sh = [p[2] for p in qp], [p[2] for p in kp]
    nbias = 2 if bias is not None else 0
    n_in = nq + nk_p + 4 + nbias

    def body(*refs):
        q_refs, k_refs = refs[:nq], refs[nq:nq + nk_p]
        v_ref, o_ref, do_ref, lse_ref = refs[nq + nk_p:nq + nk_p + 4]
        bias_refs = refs[nq + nk_p + 4:n_in]
        dq_refs = refs[n_in:n_in + nq]
        delta_ref, dob_ref = refs[n_in + nq:n_in + nq + 2]
        qi = pl.program_id(1)
        delta_tile = jnp.zeros((t, LANES), F32)
        dc_tile = jnp.zeros((t, LANES), F32)
        lse_all = lse_ref[...]
        for h in range(HEADS):
            hs = slice(h * LANES, (h + 1) * LANES)
            q = _head_cat(q_refs, q_sh, slice(None), h)
            do_f = do_ref[:, hs]
            do_b = do_f.astype(BF16)
            dob_ref[:, hs] = do_b
            delta = jnp.sum(do_f * o_ref[:, hs], axis=1, keepdims=True)
            lse = _lane_pick(lse_all, h)
            cq = _lane_pick(bias_refs[0][...], ROPE + h) if bias is not None else None

            def block(kb, carry, diag, h=h, hs=hs, q=q, cq=cq, do_b=do_b, delta=delta, lse=lse):
                dq_acc, dc_acc = carry
                rows = _blk_rows(kb, t)
                k = _head_cat(k_refs, k_sh, rows, h)
                s = lax.dot_general(q, k, _DIMS["nt"], preferred_element_type=F32) * scale
                if bias is not None:
                    s = s + cq - bias_refs[1][kb, h:h + 1, :]
                if diag:
                    s = jnp.where(_diag_visible(t, unit), s, NEG_INF)
                p = jnp.exp(s - lse)
                dp = lax.dot_general(do_b, v_ref[rows, hs].astype(BF16), _DIMS["nt"], preferred_element_type=F32)
                ds = p * (dp - delta)
                return (dq_acc + jnp.dot(ds.astype(BF16), k, preferred_element_type=F32),
                        dc_acc + jnp.sum(ds, axis=1, keepdims=True))

            init = (jnp.zeros((t, nq * LANES), F32), jnp.zeros((t, 1), F32))
            carry = lax.fori_loop(0, qi, lambda kb, c: block(kb, c, False), init)
            dq_acc, dc_acc = block(qi, carry, True)
            for n_p in range(nq):
                dq_refs[n_p][:, hs] = (dq_acc[:, n_p * LANES:(n_p + 1) * LANES] * scale).astype(grad_dtype)
            delta_tile = _lane_put(delta_tile, h, delta)
            dc_tile = _lane_put(dc_tile, ROPE + h, dc_acc)
        delta_ref[...] = delta_tile
        if bias is not None:
            refs[n_in + nq + 2][...] = dc_tile

    tile_row = lambda b, i: b * nb + i
    seq_row = lambda b, i: b
    lane_tile = pl.BlockSpec((t, LANES), lambda b, i: (b * nb + i, 0))
    head_tile = pl.BlockSpec((t, HEADS * LANES), lambda b, i: (b * nb + i, 0))
    in_specs = _piece_specs(qp, t, tile_row) + _piece_specs(kp, seq, seq_row) + _piece_specs([vp + (False,)], seq, seq_row)
    in_specs += [head_tile, head_tile, lane_tile]
    args = [p[0] for p in qp] + [p[0] for p in kp] + [vp[0], o, do, lse]
    if bias is not None:
        in_specs += [lane_tile, pl.BlockSpec((None, nb, HEADS, t), lambda b, i: (b, 0, 0, 0))]
        args += list(bias)
    out_specs = [head_tile] * nq + [lane_tile, head_tile] + ([lane_tile] if bias is not None else [])
    out_shape = [jax.ShapeDtypeStruct((n_tok, HEADS * LANES), grad_dtype)] * nq
    out_shape += [jax.ShapeDtypeStruct((n_tok, LANES), F32), jax.ShapeDtypeStruct((n_tok, HEADS * LANES), BF16)]
    if bias is not None:
        out_shape.append(jax.ShapeDtypeStruct((n_tok, LANES), F32))
    return _hosted_call(body, name, (n_seq, nb), in_specs, out_specs, out_shape, args, side)


def _attn_bwd_dkv(name, qp, kp, vp, dob, lse, delta, bias, unit, scale, n_seq, seq, t, side=None, grad_dtype=F32):
    nb = seq // t
    n_tok = n_seq * seq
    nq, nk_p = len(qp), len(kp)
    q_sh, k_sh = [p[2] for p in qp], [p[2] for p in kp]
    nbias = 2 if bias is not None else 0
    n_in = nq + nk_p + 4 + nbias

    def body(*refs):
        q_refs, k_refs = refs[:nq], refs[nq:nq + nk_p]
        v_ref, dob_ref, lse_ref, delta_ref = refs[nq + nk_p:nq + nk_p + 4]
        bias_refs = refs[nq + nk_p + 4:n_in]
        dk_refs = refs[n_in:n_in + nk_p]
        dv_ref = refs[n_in + nk_p]
        ki = pl.program_id(1)
        shared_acc = [jnp.zeros((t, LANES), F32) for _ in range(nk_p)]
        for h in range(HEADS):
            hs = slice(h * LANES, (h + 1) * LANES)
            k = _head_cat(k_refs, k_sh, slice(None), h)
            v = v_ref[:, hs].astype(BF16)
            ck = bias_refs[1][h:h + 1, :] if bias is not None else None

            def block(qb, carry, diag, h=h, hs=hs, k=k, v=v, ck=ck):
                dk_acc, dv_acc, dc_acc = carry
                rows = _blk_rows(qb, t)
                q = _head_cat(q_refs, q_sh, rows, h)
                s = lax.dot_general(q, k, _DIMS["nt"], preferred_element_type=F32) * scale
                if bias is not None:
                    s = s + _lane_pick(bias_refs[0][rows, :], ROPE + h) - ck
                if diag:
                    s = jnp.where(_diag_visible(t, unit), s, NEG_INF)
                p = jnp.exp(s - _lane_pick(lse_ref[rows, :], h))
                do_b = dob_ref[rows, hs]
                dp = lax.dot_general(do_b, v, _DIMS["nt"], preferred_element_type=F32)
                ds = p * (dp - _lane_pick(delta_ref[rows, :], h))
                return (dk_acc + lax.dot_general(ds.astype(BF16), q, _DIMS["tn"], preferred_element_type=F32),
                        dv_acc + lax.dot_general(p.astype(BF16), do_b, _DIMS["tn"], preferred_element_type=F32),
                        dc_acc - jnp.sum(ds, axis=0, keepdims=True))

            init = (jnp.zeros((t, nk_p * LANES), F32), jnp.zeros((t, LANES), F32), jnp.zeros((1, t), F32))
            carry = block(ki, init, True)
            dk_acc, dv_acc, dc_acc = lax.fori_loop(ki + 1, nb, lambda qb, c: block(qb, c, False), carry)
            for n_p in range(nk_p):
                part = dk_acc[:, n_p * LANES:(n_p + 1) * LANES] * scale
                if k_sh[n_p]:
                    shared_acc[n_p] = shared_acc[n_p] + part
                else:
                    dk_refs[n_p][:, hs] = part.astype(grad_dtype)
            dv_ref[:, hs] = dv_acc.astype(grad_dtype)
            if bias is not None:
                refs[n_in + nk_p + 1][h:h + 1, :] = dc_acc
        for n_p in range(nk_p):
            if k_sh[n_p]:
                dk_refs[n_p][...] = shared_acc[n_p]

    tile_row = lambda b, i: b * nb + i
    seq_row = lambda b, i: b
    lane_seq = pl.BlockSpec((seq, LANES), lambda b, i: (b, 0))
    head_tile = pl.BlockSpec((t, HEADS * LANES), lambda b, i: (b * nb + i, 0))
    row_tile = pl.BlockSpec((None, None, HEADS, t), lambda b, i: (b, i, 0, 0))
    in_specs = _piece_specs(qp, seq, seq_row) + _piece_specs(kp, t, tile_row) + _piece_specs([vp + (False,)], t, tile_row)
    in_specs += [pl.BlockSpec((seq, HEADS * LANES), lambda b, i: (b, 0)), lane_seq, lane_seq]
    args = [p[0] for p in qp] + [p[0] for p in kp] + [vp[0], dob, lse, delta]
    if bias is not None:
        in_specs += [lane_seq, row_tile]
        args += list(bias)
    out_specs = [pl.BlockSpec((t, LANES if sh else HEADS * LANES), lambda b, i: (b * nb + i, 0)) for sh in k_sh] + [head_tile]
    out_shape = [jax.ShapeDtypeStruct((n_tok, LANES), F32) if sh else jax.ShapeDtypeStruct((n_tok, HEADS * LANES), grad_dtype)
                 for sh in k_sh]
    out_shape.append(jax.ShapeDtypeStruct((n_tok, HEADS * LANES), grad_dtype))
    if bias is not None:
        out_specs.append(row_tile)
        out_shape.append(jax.ShapeDtypeStruct((n_seq, nb, HEADS, t), F32))
    return _hosted_call(body, name, (n_seq, nb), in_specs, out_specs, out_shape, args, side)


def _old_attn_bwd_dq(name, qp, kp, vp, o, do, lse, bias, unit, scale, n_seq, seq, t):
    nb = seq // t
    n_tok = n_seq * seq
    nq, nk_p = len(qp), len(kp)
    nbias = 2 if bias is not None else 0
    n_in = nq + nk_p + 4 + nbias
    n_out = nq + (1 if bias is not None else 0)

    def body(*refs):
        q_refs, k_refs = refs[:nq], refs[nq:nq + nk_p]
        v_ref, o_ref, do_ref, lse_ref = refs[nq + nk_p:nq + nk_p + 4]
        bias_refs = refs[nq + nk_p + 4:n_in]
        outs = refs[n_in:n_in + n_out]
        dq_s, delta_s, dc_s = refs[n_in + n_out:]
        qi, ki = pl.program_id(2), pl.program_id(3)

        @pl.when(ki == 0)
        def _():
            dq_s[...] = jnp.zeros_like(dq_s)
            dc_s[...] = jnp.zeros_like(dc_s)
            delta_s[...] = jnp.sum(do_ref[...] * o_ref[...], axis=1, keepdims=True)

        @pl.when(ki <= qi)
        def _():
            s = _scores(q_refs, k_refs, bias_refs, qi, ki, t, unit, scale)
            p = jnp.exp(s - lse_ref[...])
            dp = lax.dot_general(do_ref[...].astype(BF16), v_ref[...].astype(BF16), _DIMS["nt"],
                                 preferred_element_type=F32)
            ds = p * (dp - delta_s[...])
            dq_s[...] += jnp.dot(ds.astype(BF16), _cat(k_refs), preferred_element_type=F32)
            dc_s[...] += jnp.sum(ds, axis=1, keepdims=True)

        @pl.when(ki == qi)
        def _():
            for n_p in range(nq):
                outs[n_p][...] = dq_s[:, n_p * LANES:(n_p + 1) * LANES] * scale
            if bias is not None:
                outs[nq][...] = dc_s[...]

    q_row = lambda b, i, j: b * nb + i
    k_row = lambda b, i, j: b * nb + jnp.minimum(j, i)
    head_q = pl.BlockSpec((t, LANES), lambda b, h, i, j: (b * nb + i, h))
    col_q = pl.BlockSpec((None, t, 1), lambda b, h, i, j: (h, b * nb + i, 0))
    in_specs = [_piece_spec(t, p, q_row) for p in qp] + [_piece_spec(t, p, k_row) for p in kp]
    in_specs += [_piece_spec(t, vp, k_row), head_q, head_q, col_q]
    args = [p[0] for p in qp] + [p[0] for p in kp] + [vp[0], o, do, lse]
    if bias is not None:
        in_specs += [col_q, pl.BlockSpec((None, 1, t), lambda b, h, i, j: (b * HEADS + h, 0, jnp.minimum(j, i)))]
        args += list(bias)
    out_specs = [head_q] * nq + ([col_q] if bias is not None else [])
    out_shape = [jax.ShapeDtypeStruct((n_tok, HEADS * LANES), F32)] * nq
    if bias is not None:
        out_shape.append(jax.ShapeDtypeStruct((HEADS, n_tok, 1), F32))
    return pl.pallas_call(
        body, name=name, grid=(n_seq, HEADS, nb, nb), in_specs=in_specs, out_specs=out_specs, out_shape=out_shape,
        scratch_shapes=[pltpu.VMEM((t, nq * LANES), F32), pltpu.VMEM((t, 1), F32), pltpu.VMEM((t, 1), F32)],
        compiler_params=_params(("parallel", "parallel", "arbitrary", "arbitrary")),
    )(*args)


def _old_attn_bwd_dkv(name, qp, kp, vp, o, do, lse, bias, unit, scale, n_seq, seq, t):
    nb = seq // t
    n_tok = n_seq * seq
    nq, nk_p = len(qp), len(kp)
    nbias = 2 if bias is not None else 0
    n_in = nq + nk_p + 4 + nbias
    n_out = nk_p + 1 + (1 if bias is not None else 0)

    def body(*refs):
        q_refs, k_refs = refs[:nq], refs[nq:nq + nk_p]
        v_ref, o_ref, do_ref, lse_ref = refs[nq + nk_p:nq + nk_p + 4]
        bias_refs = refs[nq + nk_p + 4:n_in]
        outs = refs[n_in:n_in + n_out]
        dk_s, dv_s, dc_s = refs[n_in + n_out:]
        ki, qi = pl.program_id(2), pl.program_id(3)

        @pl.when(qi == 0)
        def _():
            dk_s[...] = jnp.zeros_like(dk_s)
            dv_s[...] = jnp.zeros_like(dv_s)
            dc_s[...] = jnp.zeros_like(dc_s)

        @pl.when(qi >= ki)
        def _():
            s = _scores(q_refs, k_refs, bias_refs, qi, ki, t, unit, scale)
            p = jnp.exp(s - lse_ref[...])
            do_b = do_ref[...].astype(BF16)
            delta = jnp.sum(do_ref[...] * o_ref[...], axis=1, keepdims=True)
            dp = lax.dot_general(do_b, v_ref[...].astype(BF16), _DIMS["nt"], preferred_element_type=F32)
            ds = p * (dp - delta)
            dv_s[...] += lax.dot_general(p.astype(BF16), do_b, _DIMS["tn"], preferred_element_type=F32)
            dk_s[...] += lax.dot_general(ds.astype(BF16), _cat(q_refs), _DIMS["tn"], preferred_element_type=F32)
            dc_s[...] -= jnp.sum(ds, axis=0, keepdims=True)

        @pl.when(qi == nb - 1)
        def _():
            for n_p in range(nk_p):
                outs[n_p][...] = dk_s[:, n_p * LANES:(n_p + 1) * LANES] * scale
            outs[nk_p][...] = dv_s[...]
            if bias is not None:
                outs[nk_p + 1][...] = dc_s[...]

    q_row = lambda b, i, j: b * nb + jnp.maximum(j, i)
    k_row = lambda b, i, j: b * nb + i
    head_q = pl.BlockSpec((t, LANES), lambda b, h, i, j: (b * nb + jnp.maximum(j, i), h))
    col_q = pl.BlockSpec((None, t, 1), lambda b, h, i, j: (h, b * nb + jnp.maximum(j, i), 0))
    head_k = pl.BlockSpec((t, LANES), lambda b, h, i, j: (b * nb + i, h))
    row_k = pl.BlockSpec((None, 1, t), lambda b, h, i, j: (b * HEADS + h, 0, i))
    in_specs = [_piece_spec(t, p, q_row) for p in qp] + [_piece_spec(t, p, k_row) for p in kp]
    in_specs += [_piece_spec(t, vp, k_row), head_q, head_q, col_q]
    args = [p[0] for p in qp] + [p[0] for p in kp] + [vp[0], o, do, lse]
    if bias is not None:
        in_specs += [col_q, row_k]
        args += list(bias)
    out_specs = [head_k] * (nk_p + 1) + ([row_k] if bias is not None else [])
    out_shape = [jax.ShapeDtypeStruct((n_tok, HEADS * LANES), F32)] * (nk_p + 1)
    if bias is not None:
        out_shape.append(jax.ShapeDtypeStruct((n_seq * HEADS, 1, seq), F32))
    return pl.pallas_call(
        body, name=name, grid=(n_seq, HEADS, nb, nb), in_specs=in_specs, out_specs=out_specs, out_shape=out_shape,
        scratch_shapes=[pltpu.VMEM((t, nk_p * LANES), F32), pltpu.VMEM((t, LANES), F32), pltpu.VMEM((1, t), F32)],
        compiler_params=_params(("parallel", "parallel", "arbitrary", "arbitrary")),
    )(*args)


def _seq_cumsum(name, x, col_block, n_seq, seq, reverse, pre=None, vec=None):
    t = _tile(seq, 256, 128)
    nb = seq // t

    def body(*refs):
        x_ref = refs[0]
        vec_ref = refs[1] if vec is not None else None
        o_ref, carry = refs[-2], refs[-1]

        @pl.when(pl.program_id(1) == 0)
        def _():
            carry[...] = jnp.zeros_like(carry)

        v = x_ref[...]
        if pre is not None:
            v = pre(v, vec_ref[...])
        r = lax.broadcasted_iota(jnp.int32, (t, t), 0)
        c = lax.broadcasted_iota(jnp.int32, (t, t), 1)
        tri = jnp.where((c >= r) if reverse else (c <= r), 1.0, 0.0).astype(BF16)
        hi = v.astype(BF16)
        mid = (v - hi.astype(F32)).astype(BF16)
        lo = (v - hi.astype(F32) - mid.astype(F32)).astype(BF16)
        acc = jnp.dot(tri, hi, preferred_element_type=F32)
        acc += jnp.dot(tri, mid, preferred_element_type=F32)
        acc += jnp.dot(tri, lo, preferred_element_type=F32)
        o_ref[...] = acc + carry[...]
        carry[...] += _colsum(v)

    blk = (lambda b, i: (b * nb + nb - 1 - i)) if reverse else (lambda b, i: (b * nb + i))
    in_specs = [pl.BlockSpec((t, LANES), lambda b, i: (blk(b, i), col_block))]
    args = [x]
    if vec is not None:
        in_specs.append(pl.BlockSpec(vec.shape, lambda b, i: (0, 0)))
        args.append(vec)
    return pl.pallas_call(
        body, name=name, grid=(n_seq, nb), in_specs=in_specs,
        out_specs=pl.BlockSpec((t, LANES), lambda b, i: (blk(b, i), 0)),
        out_shape=jax.ShapeDtypeStruct((n_seq * seq, LANES), F32),
        scratch_shapes=[pltpu.VMEM((1, LANES), F32)],
        compiler_params=_params(("arbitrary", "arbitrary")),
    )(*args)


def _log_sigmoid(z):
    return -(jnp.maximum(-z, 0.0) + jnp.log(1.0 + jnp.exp(-jnp.abs(z))))


def _shift_down(u, prev_ref, n):
    out = pltpu.roll(u, n, 0)
    row = lax.broadcasted_iota(jnp.int32, u.shape, 0)
    for r in range(n):
        out = jnp.where(row == r, prev_ref[8 - n + r:8 - n + r + 1, :], out)
    return out


def _shift_up(u, next_ref, n):
    ts = u.shape[0]
    out = pltpu.roll(u, ts - n, 0)
    row = lax.broadcasted_iota(jnp.int32, u.shape, 0)
    for r in range(n):
        out = jnp.where(row == ts - n + r, next_ref[r:r + 1, :], out)
    return out


def _conv_taps(u, prev_ref, w_ref, b_ref):
    s1, s2 = _shift_down(u, prev_ref, 1), _shift_down(u, prev_ref, 2)
    return (w_ref[0:1, :] * s2 + w_ref[1:2, :] * s1 + w_ref[2:3, :] * u) + b_ref[...], s1, s2


def _conv_glu_fwd(u_il, cw_il, cb_il, n_seq, seq, wt):
    n_tok, two_f = u_il.shape
    nct = two_f // (2 * wt)
    ts = _tile(seq, 256, 8)
    ns = seq // ts

    def body(u_ref, w_ref, b_ref, a_ref, carry):
        @pl.when(pl.program_id(2) == 0)
        def _():
            carry[...] = jnp.zeros_like(carry)

        u = u_ref[...].astype(F32)
        uc, _, _ = _conv_taps(u, carry, w_ref, b_ref)
        gel, _ = _gelu_parts(uc[:, :wt])
        a_ref[...] = (gel * uc[:, wt:]).astype(a_ref.dtype)
        carry[...] = u[ts - 8:, :]

    return pl.pallas_call(
        body, name="conv_glu_fwd", grid=(nct, n_seq, ns),
        in_specs=[pl.BlockSpec((ts, 2 * wt), lambda j, b, s: (b * ns + s, j)),
                  pl.BlockSpec((3, 2 * wt), lambda j, b, s: (0, j)),
                  pl.BlockSpec((1, 2 * wt), lambda j, b, s: (0, j))],
        out_specs=pl.BlockSpec((ts, wt), lambda j, b, s: (b * ns + s, j)),
        out_shape=jax.ShapeDtypeStruct((n_tok, two_f // 2), BF16),
        scratch_shapes=[pltpu.VMEM((8, 2 * wt), F32)],
        compiler_params=_params(("parallel", "arbitrary", "arbitrary")),
    )(u_il, cw_il, cb_il)


def _conv_glu_bwd_pre(u_il, da, cw_il, cb_il, n_seq, seq, wt):
    n_tok, two_f = u_il.shape
    nct = two_f // (2 * wt)
    ts = _tile(seq, 256, 8)
    ns = seq // ts

    def body(u_ref, da_ref, w_ref, b_ref, d_ref, acc_ref, carry):
        first = jnp.logical_and(pl.program_id(1) == 0, pl.program_id(2) == 0)

        @pl.when(first)
        def _():
            acc_ref[...] = jnp.zeros_like(acc_ref)

        @pl.when(pl.program_id(2) == 0)
        def _():
            carry[...] = jnp.zeros_like(carry)

        u = u_ref[...].astype(F32)
        uc, s1, s2 = _conv_taps(u, carry, w_ref, b_ref)
        gel, dgel = _gelu_parts(uc[:, :wt])
        da_v = da_ref[...].astype(F32)
        d = jnp.concatenate([da_v * uc[:, wt:] * dgel, da_v * gel], axis=1)
        d_ref[...] = d.astype(d_ref.dtype)
        acc_ref[0:1, :] += _colsum(d * s2)
        acc_ref[1:2, :] += _colsum(d * s1)
        acc_ref[2:3, :] += _colsum(d * u)
        acc_ref[3:4, :] += _colsum(d)
        carry[...] = u[ts - 8:, :]

    return pl.pallas_call(
        body, name="conv_glu_bwd_pre", grid=(nct, n_seq, ns),
        in_specs=[pl.BlockSpec((ts, 2 * wt), lambda j, b, s: (b * ns + s, j)),
                  pl.BlockSpec((ts, wt), lambda j, b, s: (b * ns + s, j)),
                  pl.BlockSpec((3, 2 * wt), lambda j, b, s: (0, j)),
                  pl.BlockSpec((1, 2 * wt), lambda j, b, s: (0, j))],
        out_specs=[pl.BlockSpec((ts, 2 * wt), lambda j, b, s: (b * ns + s, j)),
                   pl.BlockSpec((8, 2 * wt), lambda j, b, s: (0, j))],
        out_shape=[jax.ShapeDtypeStruct((n_tok, two_f), BF16), jax.ShapeDtypeStruct((8, two_f), F32)],
        scratch_shapes=[pltpu.VMEM((8, 2 * wt), F32)],
        compiler_params=_params(("parallel", "arbitrary", "arbitrary")),
    )(u_il, da, cw_il, cb_il)


def _conv_bwd_input(d_il, cw_il, n_seq, seq, wt):
    n_tok, two_f = d_il.shape
    nct = two_f // (2 * wt)
    ts = _tile(seq, 256, 8)
    ns = seq // ts

    def body(d_ref, w_ref, o_ref, carry):
        @pl.when(pl.program_id(2) == 0)
        def _():
            carry[...] = jnp.zeros_like(carry)

        d = d_ref[...].astype(F32)
        o_ref[...] = (w_ref[2:3, :] * d + w_ref[1:2, :] * _shift_up(d, carry, 1)
                      + w_ref[0:1, :] * _shift_up(d, carry, 2)).astype(o_ref.dtype)
        carry[...] = d[:8, :]

    rev = lambda j, b, s: (b * ns + ns - 1 - s, j)
    return pl.pallas_call(
        body, name="conv_bwd_input", grid=(nct, n_seq, ns),
        in_specs=[pl.BlockSpec((ts, 2 * wt), rev), pl.BlockSpec((3, 2 * wt), lambda j, b, s: (0, j))],
        out_specs=pl.BlockSpec((ts, 2 * wt), rev),
        out_shape=jax.ShapeDtypeStruct((n_tok, two_f), BF16),
        scratch_shapes=[pltpu.VMEM((8, 2 * wt), F32)],
        compiler_params=_params(("parallel", "arbitrary", "arbitrary")),
    )(d_il, cw_il)


HBM = pl.BlockSpec(memory_space=pltpu.HBM)
_CHIP_FLIPS = ((1, 0), (0, 1), (1, 1))


def _place():
    x, y, c = lax.axis_index("x"), lax.axis_index("y"), lax.axis_index("c")
    return x, y, c, 2 * x + y


def _flip(v, f):
    return 1 - v if f else v


def _half_rows(c, half):
    return pl.ds(pl.multiple_of(c * half, 16), half)


def _remote(src, dst, ssem, rsem, dev):
    return pltpu.make_async_remote_copy(src_ref=src, dst_ref=dst, send_sem=ssem, recv_sem=rsem,
                                        device_id=dev, device_id_type=MESH)


def _comm_call(name, body, ins, out_shapes, n_sems):
    return pl.pallas_call(
        body, name=name, in_specs=[HBM] * len(ins), out_specs=[HBM] * len(out_shapes),
        out_shape=[pltpu.HBM(s.shape, s.dtype) for s in out_shapes],
        scratch_shapes=[pltpu.SemaphoreType.DMA((n_sems,)), pltpu.SemaphoreType.DMA((n_sems,))],
    )(*ins)


def _all_gather_weights(shards, smalls):
    side = _gather_side(shards, smalls)
    nt = len(shards) + len(smalls)

    def body(*refs):
        for part in (side.start, side.mid, side.end):
            part(refs[:nt], refs[nt:2 * nt], *refs[2 * nt:])

    res = _comm_call("all_gather_weights", body, side.ins, side.outs, side.n_sems)
    return res[:len(shards)], res[len(shards):]


def _pair_split(name, grads):
    n = len(grads)

    def body(*refs):
        src, got = refs[:n], refs[n:2 * n]
        ssem, rsem = refs[2 * n:]
        x, y, c, _ = _place()
        cps = []
        for w in range(n):
            half = grads[w].shape[1] // 2
            cp = _remote(src[w].at[:, _half_rows(1 - c, half)], got[w], ssem.at[w], rsem.at[w], (x, y, 1 - c))
            cp.start()
            cps.append(cp)
        for cp in cps:
            cp.wait()

    outs = [jax.ShapeDtypeStruct((g.shape[0], g.shape[1] // 2, g.shape[2]), g.dtype) for g in grads]
    return _comm_call(name, body, grads, outs, n)


def _chip_scatter(parts):
    side = _scatter_side(parts)

    def body(*refs):
        n = len(parts)
        side.start(refs[:n], refs[n:2 * n], *refs[2 * n:])
        side.end(refs[:n], refs[n:2 * n], *refs[2 * n:])

    return _comm_call("rs_chip_scatter", body, parts, side.outs, side.n_sems)


class _Side:
    def __init__(self, ins, outs, n_sems, start, mid, end, mid_step=None):
        self.ins, self.outs, self.n_sems = list(ins), list(outs), n_sems
        self.start, self.mid, self.end, self.mid_step = start, mid, end, mid_step


def _scatter_side(parts):
    n = len(parts)

    def copies(src, dst, ssem, rsem):
        x, y, c, _ = _place()
        out = []
        for w in range(n):
            for k, (fx, fy) in enumerate(_CHIP_FLIPS):
                px, py = _flip(x, fx), _flip(y, fy)
                out.append(_remote(src[w].at[2 * px + py], dst[w].at[k], ssem.at[w * 3 + k], rsem.at[w * 3 + k], (px, py, c)))
        return out

    def start(src, dst, ssem, rsem):
        for cp in copies(src, dst, ssem, rsem):
            cp.start()

    def end(src, dst, ssem, rsem):
        for cp in copies(src, dst, ssem, rsem):
            cp.wait()

    outs = [jax.ShapeDtypeStruct((3,) + p.shape[1:], p.dtype) for p in parts]
    return _Side(parts, outs, 3 * n, start, None, end)


def _gather_side(shards, smalls, mid_step=None):
    n, ns = len(shards), len(smalls)

    def ici(src, dst, ssem, rsem, w, k):
        x, y, c, me = _place()
        fx, fy = _CHIP_FLIPS[k]
        rows = _half_rows(c, shards[w].shape[0] // 2)
        return _remote(src[w].at[rows], dst[w].at[me, rows], ssem.at[w * 6 + k], rsem.at[w * 6 + k],
                       (_flip(x, fx), _flip(y, fy), c))

    def small(src, dst, ssem, rsem, s, k):
        x, y, c, me = _place()
        fx, fy = _CHIP_FLIPS[k]
        sem = 6 * n + 3 * s + k
        return _remote(src[n + s], dst[n + s].at[me], ssem.at[sem], rsem.at[sem], (_flip(x, fx), _flip(y, fy), c))

    def landed(dst, ssem, rsem, w, k, sender_c, sem_off):
        x, y, c, _ = _place()
        fx, fy = _CHIP_FLIPS[k]
        got = dst[w].at[2 * _flip(x, fx) + _flip(y, fy), _half_rows(sender_c, shards[w].shape[0] // 2)]
        return _remote(got, got, ssem.at[w * 6 + sem_off + k], rsem.at[w * 6 + sem_off + k], (x, y, 1 - c))

    def start(src, dst, ssem, rsem):
        for s in range(ns):
            for k in range(3):
                small(src, dst, ssem, rsem, s, k).start()
        for w in range(n):
            for k in range(3):
                ici(src, dst, ssem, rsem, w, k).start()

    def mid(src, dst, ssem, rsem):
        c = lax.axis_index("c")
        for w in range(n):
            for k in range(3):
                landed(dst, ssem, rsem, w, k, c, 0).wait_recv()
                landed(dst, ssem, rsem, w, k, c, 3).start()

    def end(src, dst, ssem, rsem):
        c = lax.axis_index("c")
        for w in range(n):
            for k in range(3):
                landed(dst, ssem, rsem, w, k, 1 - c, 3).wait_recv()
        for s in range(ns):
            for k in range(3):
                small(src, dst, ssem, rsem, s, k).wait()
        for w in range(n):
            for k in range(3):
                ici(src, dst, ssem, rsem, w, k).wait_send()
                landed(dst, ssem, rsem, w, k, c, 3).wait_send()

    outs = [jax.ShapeDtypeStruct((N_CHIPS,) + a.shape, a.dtype) for a in list(shards) + list(smalls)]
    return _Side(list(shards) + list(smalls), outs, 6 * n + 3 * ns, start, mid, end, mid_step)


def _host(body, n_in, n_out, side, grid):
    if side is None:
        return body
    ns_in, ns_out = len(side.ins), len(side.outs)
    n_steps = math.prod(grid)
    mid_step = side.mid_step
    if side.mid is not None and not isinstance(mid_step, int):
        mid_step = min(n_steps - 1, int(mid_step * n_steps))

    def wrapped(*refs):
        ins, s_ins = refs[:n_in], refs[n_in:n_in + ns_in]
        outs = refs[n_in + ns_in:n_in + ns_in + n_out]
        s_outs = refs[n_in + ns_in + n_out:n_in + ns_in + n_out + ns_out]
        rest = refs[n_in + ns_in + n_out + ns_out:]
        sems = rest[-2:]
        step = 0
        for axis, extent in enumerate(grid):
            step = step * extent + pl.program_id(axis)

        @pl.when(step == 0)
        def _():
            side.start(s_ins, s_outs, *sems)

        if side.mid is not None:
            @pl.when(step == mid_step)
            def _():
                side.mid(s_ins, s_outs, *sems)

        body(*ins, *outs, *rest[:-2])

        @pl.when(step == n_steps - 1)
        def _():
            side.end(s_ins, s_outs, *sems)

    return wrapped


def _hosted_call(body, name, grid, in_specs, out_specs, out_shape, args, side, semantics=("parallel", "arbitrary"),
                 scratch=()):
    n_in, n_out = len(in_specs), len(out_specs)
    kern = _host(body, n_in, n_out, side, grid)
    if side is None:
        return pl.pallas_call(kern, name=name, grid=grid, in_specs=in_specs, out_specs=out_specs, out_shape=out_shape,
                              scratch_shapes=list(scratch), compiler_params=_params(semantics))(*args), []
    res = pl.pallas_call(
        kern, name=name, grid=grid, in_specs=in_specs + [HBM] * len(side.ins), out_specs=out_specs + [HBM] * len(side.outs),
        out_shape=list(out_shape) + [pltpu.HBM(s.shape, s.dtype) for s in side.outs],
        scratch_shapes=list(scratch) + [pltpu.SemaphoreType.DMA((side.n_sems,)), pltpu.SemaphoreType.DMA((side.n_sems,))],
        compiler_params=_params(("arbitrary",) * len(grid)),
    )(*args, *side.ins)
    return res[:n_out], res[n_out:]


def _pair_swap(halves):
    n = len(halves)

    def body(*refs):
        src, dst = refs[:n], refs[n:2 * n]
        ssem, rsem = refs[2 * n:]
        x, y, c, _ = _place()
        cps = []
        for w in range(n):
            cp = _remote(src[w], dst[w], ssem.at[w], rsem.at[w], (x, y, 1 - c))
            cp.start()
            cps.append(cp)
        for cp in cps:
            cp.wait()

    outs = [jax.ShapeDtypeStruct(h.shape, h.dtype) for h in halves]
    return _comm_call("rs_pair_swap", body, halves, outs, n)


def _gather_small(vec):
    def body(src, dst, ssem, rsem):
        x, y, c, _ = _place()
        me = 4 * x + 2 * y + c
        cps = []
        for r in range(1, 8):
            dev = (_flip(x, r & 4), _flip(y, r & 2), _flip(c, r & 1))
            cp = _remote(src, dst.at[me], ssem.at[r - 1], rsem.at[r - 1], dev)
            cp.start()
            cps.append(cp)
        for cp in cps:
            cp.wait()

    out = jax.ShapeDtypeStruct((8,) + vec.shape, vec.dtype)
    return _comm_call("gather_small", body, [vec], [out], 7)[0]


def _pair_add(name, g, theirs, core):
    n, half, b = theirs.shape
    tr = _tile(half, 256, 16)
    nt = half // tr

    def body(c_ref, g_ref, t_ref, o_ref):
        o_ref[...] = (g_ref[...].astype(F32) + t_ref[...].astype(F32)).astype(o_ref.dtype)

    same = pl.BlockSpec((None, tr, b), lambda j, i, c: (j, i, 0))
    grid_spec = pltpu.PrefetchScalarGridSpec(
        num_scalar_prefetch=1, grid=(n, nt),
        in_specs=[pl.BlockSpec((None, tr, b), lambda j, i, c: (j, c[0] * nt + i, 0)), same], out_specs=same)
    return pl.pallas_call(body, name=name, grid_spec=grid_spec, out_shape=jax.ShapeDtypeStruct(theirs.shape, BF16),
                          compiler_params=_params(("parallel", "parallel")))(core, g, theirs)


def _sum_slots(name, stacked, first=None):
    n, r, c = stacked.shape
    tr = _tile(r, 256, 8)

    def body(*refs):
        s_ref, o_ref = refs[-2], refs[-1]
        acc = refs[0][...].astype(F32) if first is not None else s_ref[0].astype(F32)
        for s in range(0 if first is not None else 1, n):
            acc = acc + s_ref[s].astype(F32)
        o_ref[...] = acc

    row_spec = pl.BlockSpec((tr, c), lambda i: (i, 0))
    in_specs = ([row_spec] if first is not None else []) + [pl.BlockSpec((n, tr, c), lambda i: (0, i, 0))]
    args = ([first] if first is not None else []) + [stacked]
    return pl.pallas_call(
        body, name=name, grid=(r // tr,), in_specs=in_specs, out_specs=row_spec,
        out_shape=jax.ShapeDtypeStruct((r, c), F32), compiler_params=_params(("parallel",)),
    )(*args)


def _adam_math(w, g, m, v):
    bc1, bc2 = 1.0 - ADAM_B1 ** ADAM_STEP, 1.0 - ADAM_B2 ** ADAM_STEP
    nm = ADAM_B1 * m + (1.0 - ADAM_B1) * g
    nv = ADAM_B2 * v + (1.0 - ADAM_B2) * (g * g)
    return -ADAM_LR * ((nm / bc1) / (jnp.sqrt(nv / bc2) + ADAM_EPS) + ADAM_WD * w), nm, nv


def _adamw_halves(name, w, g_mine, g_theirs, m, v, core):
    r, c = w.shape
    h = r // 2
    tr = _tile(h, 128, 8)
    nth = h // tr

    def body(c_ref, w_ref, gm_ref, gt_ref, m_ref, v_ref, g_ref, d_ref, nm_ref, nv_ref):
        g = jnp.where(pl.program_id(0) // nth == c_ref[0], gm_ref[...], gt_ref[...])
        g_ref[...] = g
        d_ref[...], nm_ref[...], nv_ref[...] = _adam_math(w_ref[...], g, m_ref[...], v_ref[...])

    full = pl.BlockSpec((tr, c), lambda i, cr: (i, 0))
    half = pl.BlockSpec((tr, c), lambda i, cr: (i % nth, 0))
    grid_spec = pltpu.PrefetchScalarGridSpec(num_scalar_prefetch=1, grid=(r // tr,),
                                             in_specs=[full, half, half, full, full], out_specs=[full] * 4)
    return pl.pallas_call(body, name=name, grid_spec=grid_spec, out_shape=[jax.ShapeDtypeStruct((r, c), F32)] * 4,
                          compiler_params=_params(("parallel",)))(core, w, g_mine, g_theirs, m, v)


def _adamw(name, w, g, m, v):
    r, c = w.shape
    by_cols = r % 8 != 0 and c % LANES == 0
    tr, tc = (r, _tile(c, 256, LANES)) if by_cols else (_tile(r, 256, 8), c)

    def body(w_ref, g_ref, m_ref, v_ref, d_ref, nm_ref, nv_ref):
        d_ref[...], nm_ref[...], nv_ref[...] = _adam_math(w_ref[...], g_ref[...], m_ref[...], v_ref[...])

    spec = pl.BlockSpec((tr, tc), (lambda i: (0, i)) if by_cols else (lambda i: (i, 0)))
    return pl.pallas_call(
        body, name=name, grid=(c // tc if by_cols else r // tr,), in_specs=[spec] * 4, out_specs=[spec] * 3,
        out_shape=[jax.ShapeDtypeStruct((r, c), F32)] * 3, compiler_params=_params(("parallel",)),
    )(w, g, m, v)


def _pad_cols(a, cols):
    return jnp.pad(a, ((0, 0), (0, cols - a.shape[1])))


def _rot_cols(w):
    h = w.shape[-1] // 2
    return jnp.concatenate([-w[..., h:], w[..., :h]], axis=-1)


def _unrot_cols(d):
    h = d.shape[-1] // 2
    return jnp.concatenate([d[..., h:], -d[..., :h]], axis=-1)


def _logical(g):
    return jnp.transpose(g, (1, 0, 2)).reshape(g.shape[1], N_CHIPS * g.shape[2])


def _chunks(a, n):
    return jnp.transpose(a.reshape(a.shape[0], N_CHIPS, n), (1, 0, 2))


def kernel(x, positions, pre_mix_norm, w_in, q_a_norm, w_uq, kv_a_norm, w_ukv, b_forget, b_gate, w_branch_mla, w_branch_fox, w_out, post_mix_norm, pre_ffn_norm, w_up, conv_w, conv_b, w_down, post_ffn_norm, loss_target, m_pre_mix_norm, m_w_in, m_q_a_norm, m_w_uq, m_kv_a_norm, m_w_ukv, m_b_forget, m_b_gate, m_w_branch_mla, m_w_branch_fox, m_w_out, m_post_mix_norm, m_pre_ffn_norm, m_w_up, m_conv_w, m_conv_b, m_w_down, m_post_ffn_norm, v_pre_mix_norm, v_w_in, v_q_a_norm, v_w_uq, v_kv_a_norm, v_w_ukv, v_b_forget, v_b_gate, v_w_branch_mla, v_w_branch_fox, v_w_out, v_post_mix_norm, v_pre_ffn_norm, v_w_up, v_conv_w, v_conv_b, v_w_down, v_post_ffn_norm):
    n_seq, seq, d = x.shape
    n_tok = n_seq * seq
    d_in = N_CHIPS * w_in.shape[1]
    two_f = N_CHIPS * w_up.shape[1]
    ff_dim = two_f // 2
    assert d_in == QL + KVL + ROPE + 3 * HEADS * FDIM + HEADS + 2 * d
    n_in_shard = w_in.shape[1]
    in_pad = -(-n_in_shard // LANES) * LANES
    hd = HEADS * LANES
    xc, yc, cc = lax.axis_index("x"), lax.axis_index("y"), lax.axis_index("c")
    chip = 2 * xc + yc
    t_attn = _tile(seq, 512, 128)

    shards = [_pad_cols(w_in, in_pad).astype(BF16), w_uq.astype(BF16), w_ukv.astype(BF16), w_branch_mla.astype(BF16),
              w_branch_fox.astype(BF16), w_out.astype(BF16), w_up.astype(BF16), w_down.astype(BF16)]
    cw8 = jnp.pad(conv_w, ((0, 5), (0, 0)))
    put_own = lambda g, s: lax.dynamic_update_slice(g, s[None], (chip, 0, 0))
    gathered, (g_cw,) = _all_gather_weights(shards[:3], [cw8])
    g_in, g_uq, g_ukv = [put_own(g, s) for g, s in zip(gathered, shards[:3])]
    g_cw = put_own(g_cw, cw8)
    n_attn_steps = n_seq * (seq // t_attn)
    up_rows = shards[6].shape[0] // 2
    side_proj = _gather_side([shards[3], shards[4], shards[5]], [], mid_step=0.9)
    side_mla = _gather_side([shards[6][:up_rows]], [], mid_step=max(n_attn_steps - 2, 0))
    side_fox = _gather_side([shards[6][up_rows:]], [], mid_step=max(n_attn_steps - 2, 0))
    side_ffn = _gather_side([shards[7]], [], mid_step=0.7)

    o_q, o_kv, o_kpe = 0, QL, QL + KVL
    o_f = o_kpe + ROPE
    o_fl = o_f + 3 * hd
    o_g = o_fl + HEADS

    def chip_cols(lo, hi):
        out = []
        while lo < hi:
            j = lo // n_in_shard
            end = min(hi, (j + 1) * n_in_shard)
            out.append((j, lo - j * n_in_shard, end - j * n_in_shard))
            lo = end
        return out

    take = lambda lo, hi: [g_in[j, :, a:b] for j, a, b in chip_cols(lo, hi)]
    w_kpe = jnp.concatenate(take(o_kpe, o_f), axis=1)
    zeros = lambda n: jnp.zeros((d, n), BF16)
    win_p = jnp.concatenate(
        take(o_g, d_in) + take(o_q, o_kpe) + [w_kpe, zeros(LANES - ROPE), _rot_cols(w_kpe)] + take(o_fl, o_g)
        + [zeros(LANES - ROPE - HEADS)] + take(o_f, o_fl), axis=1)
    n_p = win_p.shape[1]
    cb_gm, cb_gf = 0, 1
    c_lat = 2 * d
    c_kx, c_kr = c_lat + QL + KVL, c_lat + QL + KVL + LANES
    n_pa = c_kr + LANES
    assert n_p == n_pa + 3 * hd

    uq3 = _logical(g_uq).reshape(QL, HEADS, NOPE + ROPE)
    pe = uq3[:, :, NOPE:]
    pad_pe = lambda a: jnp.pad(a, ((0, 0), (0, 0), (0, LANES - ROPE))).reshape(QL, hd)
    wuq_p = jnp.concatenate([uq3[:, :, :NOPE].reshape(QL, hd), pad_pe(pe), pad_pe(_rot_cols(pe))], axis=1)
    ukv3 = _logical(g_ukv).reshape(KVL, HEADS, NOPE + VDIM)
    wukv_p = jnp.concatenate([ukv3[:, :, :NOPE].reshape(KVL, hd), ukv3[:, :, NOPE:].reshape(KVL, hd)], axis=1)

    n_bm, n_up = w_branch_mla.shape[1], w_up.shape[1]
    l_bm, l_up = _Chunked(n_bm), _Chunked(n_up)
    wt = n_up // 2
    n_ut = two_f // wt
    il = lambda cblk: jnp.where(cblk < n_ut // 2, 2 * cblk, 2 * (cblk - n_ut // 2) + 1)
    l_il = _Plain(il)
    to_il = lambda a: a.reshape(a.shape[0], 2, n_ut // 2, wt).transpose(0, 2, 1, 3).reshape(a.shape[0], two_f)
    from_il = lambda a: a.reshape(a.shape[0], n_ut // 2, 2, wt).transpose(0, 2, 1, 3).reshape(a.shape[0], two_f)

    inv_freq = 1.0 / (ROPE_THETA ** (jnp.arange(0, ROPE, 2, dtype=F32) / ROPE))
    ang = positions.astype(F32).reshape(n_tok, 1) * inv_freq
    cos, sin = jnp.cos(ang), jnp.sin(ang)
    cs = _pad_cols(jnp.concatenate([cos, cos], axis=1), LANES)
    sn = _pad_cols(jnp.concatenate([sin, sin], axis=1), LANES)

    row = lambda v: v.reshape(1, -1)
    x2 = x.reshape(n_tok, d)
    tgt = loss_target.reshape(n_tok, d)

    (h,) = _rows("rms_pre_mix", lambda r, v: ([r[0] * _rstd(r[0]) * v[0]], []),
                 [(x2, d, 0)], [row(pre_mix_norm)], [(d, BF16)], [], n_tok)
    proj, got = _mm("proj_in", "nn", h, win_p, n_tok, n_pa, d, side=side_proj)
    g_bm, g_bf, g_out = [put_own(g, s) for g, s in zip(got, side_proj.ins)]
    w_out_full = g_out.reshape(d, d)
    tn_f = _tile(3 * hd, 1024, 128)
    assert n_pa % tn_f == 0
    proj_f = _mm("proj_in_fox", "nn", h, win_p, n_tok, 3 * hd, d, tn=tn_f, lb=_Plain(lambda cblk: cblk + n_pa // tn_f),
                 out_dtype=BF16)

    bf_vec = jnp.pad(row(b_forget), ((0, 0), (ROPE, LANES - ROPE - HEADS)))

    def lat_fwd(r, v):
        ql, kvl = r[0], r[1]
        return [ql * _rstd(ql) * v[0], kvl * _rstd(kvl) * v[1], r[2] * r[4] + r[3] * r[5]], []

    qn, kvn, rk = _rows("latent_norms", lat_fwd,
                        [(proj, QL, c_lat // QL), (proj, KVL, (c_lat + QL) // KVL), (proj, LANES, c_kx // LANES),
                         (proj, LANES, c_kr // LANES), (cs, LANES, 0), (sn, LANES, 0)],
                        [row(q_a_norm), row(kv_a_norm)], [(QL, BF16), (KVL, BF16), (LANES, BF16)], [], n_tok)
    q_p = _mm("q_up", "nn", qn, wuq_p, n_tok, 3 * hd, QL)
    kv_p = _mm("kv_up", "nn", kvn, wukv_p, n_tok, 2 * hd, KVL, out_dtype=BF16)

    def rope_q(r, v):
        c8, s8 = jnp.tile(r[3], (1, HEADS)), jnp.tile(r[4], (1, HEADS))
        return [r[0], r[1] * c8 + r[2] * s8], []

    q_nope, rq = _rows("rope_q", rope_q, [(q_p, hd, 0), (q_p, hd, 1), (q_p, hd, 2), (cs, LANES, 0), (sn, LANES, 0)], [],
                       [(hd, BF16), (hd, BF16)], [], n_tok)

    mla_q = [(q_nope, 0, False), (rq, 0, False)]
    mla_k = [(kv_p, 0, False), (rk, 0, True)]
    mla_v = (kv_p, 1)
    mla_scale = (NOPE + ROPE) ** -0.5
    (o_mla, lse_mla), got = _attn_fwd("mla_fwd", mla_q, mla_k, mla_v, None, CHUNK, mla_scale, n_seq, seq, t_attn, side=side_mla)
    g_up_top = put_own(got[0], side_mla.ins[0])

    c_run = _seq_cumsum("forget_cumsum", proj, c_kr // LANES, n_seq, seq, False,
                        pre=lambda z, b: _log_sigmoid(z + b), vec=bf_vec)
    nb_attn = seq // t_attn
    c_rowf = jnp.transpose(c_run[:, ROPE:ROPE + HEADS].reshape(n_seq, nb_attn, t_attn, HEADS), (0, 1, 3, 2))
    fox_q, fox_k, fox_v = [(proj_f, 0, False)], [(proj_f, 1, False)], (proj_f, 2)
    fox_scale = FDIM ** -0.5
    fox_bias = (c_run, c_rowf)
    (o_fox, lse_fox), got = _attn_fwd("fox_fwd", fox_q, fox_k, fox_v, fox_bias, 1, fox_scale, n_seq, seq, t_attn, side=side_fox)
    g_up = jnp.concatenate([g_up_top, put_own(got[0], side_fox.ins[0])], axis=1)

    pm = _mm("branch_mla", "nn", o_mla, g_bm, n_tok, d, hd, lb=l_bm, tn=n_bm)
    pf = _mm("branch_fox", "nn", o_fox, g_bf, n_tok, d, hd, lb=l_bm, tn=n_bm)
    bg = row(b_gate)

    def merge(r, v):
        return [_sigmoid(r[0] + v[0]) * r[2] + _sigmoid(r[1] + v[1]) * r[3]], []

    (merged,) = _rows("gate_merge", merge, [(proj, d, cb_gm), (proj, d, cb_gf), (pm, d, 0), (pf, d, 0)],
                      [bg[:, :d], bg[:, d:]], [(d, BF16)], [], n_tok)
    y1 = _mm("mix_out", "nn", merged, w_out_full, n_tok, d, d)

    def resid_norm(r, v):
        x1v = r[0] + r[1] * _rstd(r[1]) * v[0]
        return [x1v, x1v * _rstd(x1v) * v[1]], []

    x1, h2 = _rows("post_mix_pre_ffn", resid_norm, [(x2, d, 0), (y1, d, 0)], [row(post_mix_norm), row(pre_ffn_norm)],
                   [(d, F32), (d, BF16)], [], n_tok)

    u_il, got = _mm("ffn_up", "nn", h2, g_up, n_tok, two_f, d, lb=l_up, lo=l_il, tn=wt, out_dtype=BF16, side=side_ffn)
    w_down_full = put_own(got[0], side_ffn.ins[0]).reshape(ff_dim, d)
    cw_il = to_il(_logical(g_cw)[:3])
    cb_il = to_il(row(conv_b))
    act = _conv_glu_fwd(u_il, cw_il, cb_il, n_seq, seq, wt)
    ff = _mm("ffn_down", "nn", act, w_down_full, n_tok, d, ff_dim)

    def final(r, v):
        x1v, ffv, tg = r
        diff = x1v + ffv * _rstd(ffv) * v[0] - tg
        dx2v = diff / d
        dffv, dg4 = _rms_bwd(ffv, v[0], dx2v)
        sq = jnp.sum(jnp.sum(diff * diff, axis=1, keepdims=True), axis=0, keepdims=True)
        return [dx2v, dffv], [dg4, jnp.broadcast_to(sq, (1, LANES))]

    dx2, dff, dg_post_ffn, sq_sum = _rows("loss_post_ffn_bwd", final, [(x1, d, 0), (ff, d, 0), (tgt, d, 0)],
                                          [row(post_ffn_norm)], [(d, F32), (d, BF16)], [(1, d), (1, LANES)], n_tok)
    rs_parts, rs_landed = {}, {}
    core = jnp.reshape(cc, (1,)).astype(jnp.int32)

    def pair_reduce(tag, names, grads):
        theirs = _pair_split("rs_pair_split_" + tag, grads)
        for nm, g, b in zip(names, grads, theirs):
            rs_parts[nm] = _pair_add("rs_pair_add_" + nm, g, b, core)

    dact = _mm("ffn_down_dx", "nt", dff, w_down_full, n_tok, ff_dim, d, tn=wt, out_dtype=BF16)
    gw_down = _mm("ffn_down_dw", "tn", act, dff, ff_dim, d, n_tok, tm=wt, out_dtype=BF16)
    pair_reduce("down", ["w_down"], [gw_down.reshape(N_CHIPS, ff_dim // N_CHIPS, d)])
    d_il, conv_acc = _conv_glu_bwd_pre(u_il, dact, cw_il, cb_il, n_seq, seq, wt)
    du_il = _conv_bwd_input(d_il, cw_il, n_seq, seq, wt)
    gw_up, got = _mm("ffn_up_dw", "tn", h2, du_il, d, two_f, n_tok, lb=l_il, lo=l_up, tn=wt, out_dtype=BF16,
                     side=_scatter_side([rs_parts["w_down"]]))
    rs_landed["w_down"] = got[0]
    pair_reduce("up", ["w_up"], [gw_up])
    dh2, got = _mm("ffn_up_dx", "nt", du_il, g_up, n_tok, d, two_f, la=l_il, lb=l_up, tk=wt,
                   side=_scatter_side([rs_parts["w_up"]]))
    rs_landed["w_up"] = got[0]

    def mid_bwd(r, v):
        x1v, y1v, dx2v, dh2v = r
        d3, dg3 = _rms_bwd(x1v, v[1], dh2v)
        dx1v = dx2v + d3
        dy1v, dg2 = _rms_bwd(y1v, v[0], dx1v)
        return [dx1v, dy1v], [dg3, dg2]

    dx1, dy1, dg_pre_ffn, dg_post_mix = _rows(
        "pre_ffn_post_mix_bwd", mid_bwd, [(x1, d, 0), (y1, d, 0), (dx2, d, 0), (dh2, d, 0)],
        [row(post_mix_norm), row(pre_ffn_norm)], [(d, F32), (d, BF16)], [(1, d), (1, d)], n_tok)
    dmerged = _mm("mix_out_dx", "nt", dy1, w_out_full, n_tok, d, d)
    gw_out = _mm("mix_out_dw", "tn", merged, dy1, d, d, n_tok, out_dtype=BF16)

    def gate_bwd(r, v):
        zm, zf, pmv, pfv, dm = r
        gm, gf = _sigmoid(zm + v[0]), _sigmoid(zf + v[1])
        dzm, dzf = dm * pmv * gm * (1.0 - gm), dm * pfv * gf * (1.0 - gf)
        return [dm * gm, dm * gf, jnp.concatenate([dzm, dzf], axis=1)], [_colsum(dzm), _colsum(dzf)]

    dpm, dpf, dz, dbg_m, dbg_f = _rows(
        "gate_merge_bwd", gate_bwd, [(proj, d, cb_gm), (proj, d, cb_gf), (pm, d, 0), (pf, d, 0), (dmerged, d, 0)],
        [bg[:, :d], bg[:, d:]], [(d, BF16), (d, BF16), (2 * d, BF16)], [(1, d), (1, d)], n_tok)
    tk_b = min(n_bm, 512)
    do_mla = _mm("branch_mla_dx", "nt", dpm, g_bm, n_tok, hd, d, lb=l_bm, tk=tk_b)
    do_fox = _mm("branch_fox_dx", "nt", dpf, g_bf, n_tok, hd, d, lb=l_bm, tk=tk_b)
    gw_bm = _mm("branch_mla_dw", "tn", o_mla, dpm, hd, d, n_tok, lo=l_bm, tn=n_bm, out_dtype=BF16)
    gw_bf = _mm("branch_fox_dw", "tn", o_fox, dpf, hd, d, n_tok, lo=l_bm, tn=n_bm, out_dtype=BF16)

    pair_reduce("mix", ["w_out", "w_branch_mla", "w_branch_fox"], [gw_out.reshape(N_CHIPS, d // N_CHIPS, d), gw_bm, gw_bf])
    (dq_nope, drq, delta_mla, dob_mla), got = _attn_bwd_dq(
        "mla_bwd_dq", mla_q, mla_k, mla_v, o_mla, do_mla, lse_mla, None, CHUNK, mla_scale, n_seq, seq, t_attn,
        side=_scatter_side([rs_parts[nm] for nm in ("w_out", "w_branch_mla", "w_branch_fox")]))
    rs_landed.update(zip(("w_out", "w_branch_mla", "w_branch_fox"), got))
    (dk_nope, drk, dv_mla), _ = _attn_bwd_dkv(
        "mla_bwd_dkv", mla_q, mla_k, mla_v, dob_mla, lse_mla, delta_mla, None, CHUNK, mla_scale, n_seq, seq, t_attn)
    (dfq, delta_fox, dob_fox, dc_q), _ = _attn_bwd_dq("fox_bwd_dq", fox_q, fox_k, fox_v, o_fox, do_fox, lse_fox, fox_bias, 1,
                                                      fox_scale, n_seq, seq, t_attn, grad_dtype=BF16)
    (dfk, dfv, dc_k), _ = _attn_bwd_dkv("fox_bwd_dkv", fox_q, fox_k, fox_v, dob_fox, lse_fox, delta_fox, fox_bias, 1, fox_scale,
                                        n_seq, seq, t_attn, grad_dtype=BF16)
    dc_k8 = jnp.transpose(dc_k, (0, 1, 3, 2)).reshape(n_tok, HEADS)
    dc128 = dc_q + jnp.pad(dc_k8, ((0, 0), (ROPE, LANES - ROPE - HEADS)))
    dlogf = _seq_cumsum("forget_cumsum_bwd", dc128, 0, n_seq, seq, True)

    def mla_pack(r, v):
        dqn_v, drq_v, dkn_v, dv_v, drk_v, c1, s1 = r
        c8, s8 = jnp.tile(c1, (1, HEADS)), jnp.tile(s1, (1, HEADS))
        return [jnp.concatenate([dqn_v, drq_v * c8, drq_v * s8], axis=1), jnp.concatenate([dkn_v, dv_v], axis=1),
                drk_v * c1, drk_v * s1], []

    dq_p, dkv_p, dkx, dkr = _rows(
        "mla_rope_bwd", mla_pack,
        [(dq_nope, hd, 0), (drq, hd, 0), (dk_nope, hd, 0), (dv_mla, hd, 0), (drk, LANES, 0), (cs, LANES, 0), (sn, LANES, 0)],
        [], [(3 * hd, BF16), (2 * hd, BF16), (LANES, F32), (LANES, F32)], [], n_tok)
    dqn = _mm("q_up_dx", "nt", dq_p, wuq_p, n_tok, QL, 3 * hd)
    gw_uq_p = _mm("q_up_dw", "tn", qn, dq_p, QL, 3 * hd, n_tok, out_dtype=BF16)
    dkvn = _mm("kv_up_dx", "nt", dkv_p, wukv_p, n_tok, KVL, 2 * hd)
    gw_ukv_p = _mm("kv_up_dw", "tn", kvn, dkv_p, KVL, 2 * hd, n_tok, out_dtype=BF16)

    def lat_bwd(r, v):
        ql, kvl, dqn_v, dkvn_v, dkx_v, dkr_v, zblk, dlf = r
        dql, dgq = _rms_bwd(ql, v[0], dqn_v)
        dkvl, dgkv = _rms_bwd(kvl, v[1], dkvn_v)
        dfl = dlf * _sigmoid(-(zblk + v[2]))
        return [jnp.concatenate([dql, dkvl, dkx_v, dkr_v + dfl], axis=1)], [dgq, dgkv, _colsum(dfl)]

    dlat, dg_q, dg_kv, dbf = _rows(
        "latent_bwd", lat_bwd,
        [(proj, QL, c_lat // QL), (proj, KVL, (c_lat + QL) // KVL), (dqn, QL, 0), (dkvn, KVL, 0), (dkx, LANES, 0),
         (dkr, LANES, 0), (proj, LANES, c_kr // LANES), (dlogf, LANES, 0)],
        [row(q_a_norm), row(kv_a_norm), bf_vec], [(QL + KVL + 2 * LANES, BF16)], [(1, QL), (1, KVL), (1, LANES)], n_tok)
    dproj = [dz, dlat, dfq, dfk, dfv]
    gw_in_p = _mm_parts("proj_in_dw", "tn", h, dproj, d, n_p, n_tok, tk=1024, out_dtype=BF16)

    f32 = lambda a: a.astype(F32)
    kr_blk = gw_in_p[:, c_kr:c_kr + LANES]
    d_kpe = (f32(gw_in_p[:, c_kx:c_kx + ROPE]) + _unrot_cols(f32(kr_blk[:, :ROPE]))).astype(BF16)
    in_pieces = [(o_q, gw_in_p, c_lat, QL + KVL), (o_kpe, d_kpe, 0, ROPE), (o_f, gw_in_p, n_pa, 3 * hd),
                 (o_fl, kr_blk, ROPE, HEADS), (o_g, gw_in_p, 0, 2 * d)]
    gc_in = []
    for j in range(N_CHIPS):
        lo, hi, cols = j * n_in_shard, (j + 1) * n_in_shard, []
        for first, arr, at, width in in_pieces:
            a, b = max(lo, first), min(hi, first + width)
            if a < b:
                cols.append(arr[:, at + a - first:at + b - first])
        cols.append(jnp.zeros((d, in_pad - n_in_shard), BF16))
        gc_in.append(jnp.concatenate(cols, axis=1))
    gc_in = jnp.stack(gc_in)
    uq_parts = [gw_uq_p[:, i * hd:(i + 1) * hd].reshape(QL, HEADS, LANES) for i in range(3)]
    d_pe = (f32(uq_parts[1][:, :, :ROPE]) + _unrot_cols(f32(uq_parts[2][:, :, :ROPE]))).astype(BF16)
    gc_uq = _chunks(jnp.concatenate([uq_parts[0], d_pe], axis=2).reshape(QL, HEADS * (NOPE + ROPE)), w_uq.shape[1])
    gc_ukv = _chunks(jnp.concatenate([gw_ukv_p[:, :hd].reshape(KVL, HEADS, NOPE), gw_ukv_p[:, hd:].reshape(KVL, HEADS, VDIM)],
                                     axis=2).reshape(KVL, HEADS * (NOPE + VDIM)), w_ukv.shape[1])
    grads = [gc_in, gc_uq, gc_ukv]

    late = ["w_in", "w_uq", "w_ukv"]
    pair_reduce("late", late, grads)
    dh, got = _mm_parts("proj_in_dx", "nt", dproj, win_p, n_tok, d, n_p, side=_scatter_side([rs_parts[nm] for nm in late]))
    rs_landed.update(zip(late, got))

    def first_bwd(r, v):
        dxa, dg1 = _rms_bwd(r[0], v[0], r[1])
        return [r[2] + dxa], [dg1]

    grad_x, dg_pre_mix = _rows("pre_mix_bwd", first_bwd, [(x2, d, 0), (dh, d, 0), (dx1, d, 0)], [row(pre_mix_norm)],
                               [(d, F32)], [(1, d)], n_tok)
    big = list(rs_parts)
    halves = [_sum_slots("rs_chip_sum_" + nm, rs_landed[nm],
                         first=lax.dynamic_index_in_dim(rs_parts[nm], chip, 0, keepdims=False)) for nm in big]
    other = _pair_swap(halves)
    g_halves = dict(zip(big, zip(halves, other)))

    conv_acc_l = from_il(conv_acc)
    pieces = [dg_pre_mix, dg_q, dg_kv, dbf, dbg_m, dbg_f, dg_post_mix, dg_pre_ffn, conv_acc_l[3:4], dg_post_ffn,
              conv_acc_l[0:1], conv_acc_l[1:2], conv_acc_l[2:3], sq_sum]
    sizes = [p.shape[1] for p in pieces]
    flat = jnp.concatenate(pieces, axis=1)
    n_rows = -(-flat.shape[1] // (8 * LANES)) * 8
    flat = _pad_cols(flat, n_rows * LANES).reshape(n_rows, LANES)
    slots = lax.dynamic_update_slice(_gather_small(flat), flat[None], (2 * chip + cc, 0, 0))
    total = _sum_slots("small_sum", slots).reshape(1, n_rows * LANES)
    offs = [sum(sizes[:i]) for i in range(len(sizes))]
    tot = [total[0, o:o + s] for o, s in zip(offs, sizes)]
    loss = 0.5 * tot[13][0] / d
    g_small = {"pre_mix_norm": tot[0], "q_a_norm": tot[1], "kv_a_norm": tot[2], "b_forget": tot[3][ROPE:ROPE + HEADS],
               "b_gate": jnp.concatenate([tot[4], tot[5]]), "post_mix_norm": tot[6], "pre_ffn_norm": tot[7],
               "conv_b": tot[8], "post_ffn_norm": tot[9]}
    gcw_full = jnp.stack([tot[10], tot[11], tot[12]])
    g_conv_w = lax.dynamic_slice(gcw_full, (0, chip * n_up), (3, n_up))

    given = dict(pre_mix_norm=(pre_mix_norm, m_pre_mix_norm, v_pre_mix_norm), w_in=(w_in, m_w_in, v_w_in),
                 q_a_norm=(q_a_norm, m_q_a_norm, v_q_a_norm), w_uq=(w_uq, m_w_uq, v_w_uq),
                 kv_a_norm=(kv_a_norm, m_kv_a_norm, v_kv_a_norm), w_ukv=(w_ukv, m_w_ukv, v_w_ukv),
                 b_forget=(b_forget, m_b_forget, v_b_forget), b_gate=(b_gate, m_b_gate, v_b_gate),
                 w_branch_mla=(w_branch_mla, m_w_branch_mla, v_w_branch_mla),
                 w_branch_fox=(w_branch_fox, m_w_branch_fox, v_w_branch_fox), w_out=(w_out, m_w_out, v_w_out),
                 post_mix_norm=(post_mix_norm, m_post_mix_norm, v_post_mix_norm),
                 pre_ffn_norm=(pre_ffn_norm, m_pre_ffn_norm, v_pre_ffn_norm), w_up=(w_up, m_w_up, v_w_up),
                 conv_w=(conv_w, m_conv_w, v_conv_w), conv_b=(conv_b, m_conv_b, v_conv_b),
                 w_down=(w_down, m_w_down, v_w_down), post_ffn_norm=(post_ffn_norm, m_post_ffn_norm, v_post_ffn_norm))
    order = list(given)
    grad, delta, new_m, new_v = {}, {}, {}, {}
    for nm in big:
        mine, theirs = g_halves[nm]
        if nm == "w_in":
            full = jnp.concatenate([jnp.where(cc == 0, mine, theirs), jnp.where(cc == 0, theirs, mine)], axis=0)
            grad[nm] = full[:, :n_in_shard]
            tr_out = _adamw("adamw_" + nm, *[jnp.transpose(a) for a in (given[nm][0], grad[nm], given[nm][1], given[nm][2])])
            delta[nm], new_m[nm], new_v[nm] = [jnp.transpose(a) for a in tr_out]
            continue
        grad[nm], delta[nm], new_m[nm], new_v[nm] = _adamw_halves("adamw_" + nm, given[nm][0], mine, theirs, given[nm][1],
                                                                  given[nm][2], core)
    grad["conv_w"] = g_conv_w
    delta["conv_w"], new_m["conv_w"], new_v["conv_w"] = _adamw("adamw_conv_w", conv_w, g_conv_w, m_conv_w, v_conv_w)
    small = list(g_small)
    padded = [-(-g_small[nm].shape[0] // LANES) * LANES for nm in small]
    s_rows = -(-sum(padded) // (8 * LANES)) * 8

    def pack(vals):
        cat = jnp.concatenate([jnp.pad(a, (0, p - a.shape[0])) for a, p in zip(vals, padded)])
        return jnp.pad(cat, (0, s_rows * LANES - cat.shape[0])).reshape(s_rows, LANES)

    packed = _adamw("adamw_small", pack([given[nm][0] for nm in small]), pack([g_small[nm] for nm in small]),
                    pack([given[nm][1] for nm in small]), pack([given[nm][2] for nm in small]))
    s_offs = [sum(padded[:i]) for i in range(len(small))]
    for nm, o in zip(small, s_offs):
        n_el = g_small[nm].shape[0]
        grad[nm] = g_small[nm]
        delta[nm], new_m[nm], new_v[nm] = [p.reshape(-1)[o:o + n_el] for p in packed]
    return (loss, grad_x.reshape(n_seq, seq, d), *[grad[nm] for nm in order], *[delta[nm] for nm in order],
            *[new_m[nm] for nm in order], *[new_v[nm] for nm in order])
```

```python
import functools
import math

import jax
import jax.numpy as jnp
from jax import lax
from jax.experimental import pallas as pl
from jax.experimental.pallas import tpu as pltpu

F32, BF16 = jnp.float32, jnp.bfloat16
MESH = pl.DeviceIdType.MESH

HEADS = 8
NOPE, ROPE, VDIM = 128, 64, 128
QL, KVL = 512, 256
FDIM = 128
CHUNK = 64
ROPE_THETA = 10000.0
EPS = 1e-6
NEG_INF = -1e30
ADAM_LR, ADAM_B1, ADAM_B2, ADAM_EPS, ADAM_WD, ADAM_STEP = 0.001, 0.9, 0.999, 1e-08, 0.01, 10

VMEM_LIMIT_BYTES = 52 * 1024 * 1024
LANES = 128
N_CHIPS = 4


def _params(sem):
    return pltpu.CompilerParams(dimension_semantics=sem, vmem_limit_bytes=VMEM_LIMIT_BYTES)


def _tile(n, target, mult):
    if n <= target:
        return n
    t = (target // mult) * mult
    while t >= mult:
        if n % t == 0:
            return t
        t -= mult
    raise ValueError(f"no tile for {n} (target {target}, multiple of {mult})")


class _Plain:
    def __init__(self, perm=None):
        self.perm = perm

    def spec(self, tr, tc, rc):
        perm = self.perm

        def imap(i, j, k):
            r, c = rc(i, j, k)
            return (r, perm(c) if perm is not None else c)

        return pl.BlockSpec((tr, tc), imap)

    def shape(self, rows, cols):
        return (rows, cols)


class _Chunked:
    def __init__(self, n):
        self.n = n

    def spec(self, tr, tc, rc):
        assert self.n % tc == 0, (self.n, tc)
        per = self.n // tc

        def imap(i, j, k):
            r, c = rc(i, j, k)
            return (c // per, r, c % per)

        return pl.BlockSpec((None, tr, tc), imap)

    def shape(self, rows, cols):
        assert cols == N_CHIPS * self.n
        return (N_CHIPS, rows, self.n)


_DIMS = {"nn": (((1,), (0,)), ((), ())), "nt": (((1,), (1,)), ((), ())), "tn": (((0,), (0,)), ((), ()))}


def _mm_single(name, mode, a, b, m, n, k, tm, tn, la, lb, lo, out_dtype, side):
    if mode == "nn":
        a_spec = la.spec(tm, k, lambda i, j, kk: (i, 0))
        b_spec = lb.spec(k, tn, lambda i, j, kk: (0, j))
    elif mode == "nt":
        a_spec = la.spec(tm, k, lambda i, j, kk: (i, 0))
        b_spec = lb.spec(tn, k, lambda i, j, kk: (j, 0))
    else:
        a_spec = la.spec(k, tm, lambda i, j, kk: (0, i))
        b_spec = lb.spec(k, tn, lambda i, j, kk: (0, j))
    o_spec = lo.spec(tm, tn, lambda i, j, kk: (i, j))
    dims = _DIMS[mode]

    def body(a_ref, b_ref, o_ref):
        o_ref[...] = lax.dot_general(a_ref[...].astype(BF16), b_ref[...].astype(BF16), dims,
                                     preferred_element_type=F32).astype(o_ref.dtype)

    (out,), got = _hosted_call(body, name, (m // tm, n // tn, 1), [a_spec, b_spec], [o_spec],
                               [jax.ShapeDtypeStruct(lo.shape(m, n), out_dtype)], (a, b), side,
                               semantics=("parallel", "parallel", "arbitrary"))
    return out if side is None else (out, got)


def _mm(name, mode, a, b, m, n, k, *, tm=1024, tn=1024, tk=2048, la=None, lb=None, lo=None, out_dtype=F32, side=None):
    la, lb, lo = la or _Plain(), lb or _Plain(), lo or _Plain()
    tm, tn, tk = _tile(m, tm, 128), _tile(n, tn, 128), _tile(k, tk, 128)
    nk = k // tk
    if nk == 1:
        return _mm_single(name, mode, a, b, m, n, k, tm, tn, la, lb, lo, out_dtype, side)
    if mode == "nn":
        a_spec = la.spec(tm, tk, lambda i, j, kk: (i, kk))
        b_spec = lb.spec(tk, tn, lambda i, j, kk: (kk, j))
    elif mode == "nt":
        a_spec = la.spec(tm, tk, lambda i, j, kk: (i, kk))
        b_spec = lb.spec(tn, tk, lambda i, j, kk: (j, kk))
    else:
        a_spec = la.spec(tk, tm, lambda i, j, kk: (kk, i))
        b_spec = lb.spec(tk, tn, lambda i, j, kk: (kk, j))
    o_spec = lo.spec(tm, tn, lambda i, j, kk: (i, j))
    dims = _DIMS[mode]

    def body(a_ref, b_ref, o_ref, acc_ref):
        kk = pl.program_id(2)

        @pl.when(kk == 0)
        def _():
            acc_ref[...] = jnp.zeros_like(acc_ref)

        acc_ref[...] += lax.dot_general(a_ref[...].astype(BF16), b_ref[...].astype(BF16), dims,
                                        preferred_element_type=F32)

        @pl.when(kk == nk - 1)
        def _():
            o_ref[...] = acc_ref[...].astype(o_ref.dtype)

    (out,), got = _hosted_call(body, name, (m // tm, n // tn, nk), [a_spec, b_spec], [o_spec],
                               [jax.ShapeDtypeStruct(lo.shape(m, n), out_dtype)], (a, b), side,
                               semantics=("parallel", "parallel", "arbitrary"), scratch=[pltpu.VMEM((tm, tn), F32)])
    return out if side is None else (out, got)


def _mm_parts(name, mode, a, b, m, n, k, *, part=1024, tm=1024, tn=1024, tk=2048, out_dtype=F32, side=None):
    parts = b if mode == "tn" else a
    widths = [p.shape[1] for p in parts]
    assert all(w % part == 0 for w in widths) and sum(widths) == (n if mode == "tn" else k)
    offs = [sum(widths[:i]) // part for i in range(len(widths))]
    nblk = [w // part for w in widths]
    if mode == "tn":
        tn, tk = part, _tile(k, tk, 128)
    else:
        tk, tn = part, _tile(n, tn, 128)
    tm = _tile(m, tm, 128)
    nk = k // tk
    grid = (m // tm, n // tn, nk)
    np_ = len(parts)

    def inside(idx, p):
        return jnp.logical_and(idx >= offs[p], idx < offs[p] + nblk[p])

    def part_spec(p):
        if mode == "tn":
            def imap(i, j, kk):
                on = inside(j, p)
                return (jnp.where(on, kk, 0), jnp.clip(j - offs[p], 0, nblk[p] - 1))
            return pl.BlockSpec((tk, tn), imap)

        def imap(i, j, kk):
            return (i, jnp.clip(kk - offs[p], 0, nblk[p] - 1))
        return pl.BlockSpec((tm, tk), imap)

    if mode == "tn":
        in_specs = [pl.BlockSpec((tk, tm), lambda i, j, kk: (kk, i))] + [part_spec(p) for p in range(np_)]
        args = [a] + list(parts)
    else:
        in_specs = [part_spec(p) for p in range(np_)] + [pl.BlockSpec((tn, tk), lambda i, j, kk: (j, kk))]
        args = list(parts) + [b]
    dims = _DIMS[mode]

    def body(*refs):
        o_ref, acc_ref = refs[-2], refs[-1]
        j, kk = pl.program_id(1), pl.program_id(2)

        @pl.when(kk == 0)
        def _():
            acc_ref[...] = jnp.zeros_like(acc_ref)

        for p in range(np_):
            @pl.when(inside(j if mode == "tn" else kk, p))
            def _(p=p):
                lhs, rhs = (refs[0], refs[1 + p]) if mode == "tn" else (refs[p], refs[np_])
                acc_ref[...] += lax.dot_general(lhs[...].astype(BF16), rhs[...].astype(BF16), dims, preferred_element_type=F32)

        @pl.when(kk == nk - 1)
        def _():
            o_ref[...] = acc_ref[...].astype(o_ref.dtype)

    (out,), got = _hosted_call(body, name, grid, in_specs, [pl.BlockSpec((tm, tn), lambda i, j, kk: (i, j))],
                               [jax.ShapeDtypeStruct((m, n), out_dtype)], args, side,
                               semantics=("parallel", "parallel", "arbitrary"), scratch=[pltpu.VMEM((tm, tn), F32)])
    return out if side is None else (out, got)


def _rows(name, fn, rows_in, vecs_in, rows_out, accs_out, n_rows, tr=256):
    tr = _tile(n_rows, tr, 16)
    nr, nv, no = len(rows_in), len(vecs_in), len(rows_out)

    def body(*refs):
        ins, vecs = refs[:nr], refs[nr:nr + nv]
        outs, accs = refs[nr + nv:nr + nv + no], refs[nr + nv + no:]
        ro, ac = fn([r[...] for r in ins], [v[...] for v in vecs])
        for o_ref, val in zip(outs, ro):
            o_ref[...] = val.astype(o_ref.dtype)
        if accs:
            @pl.when(pl.program_id(0) == 0)
            def _():
                for a_ref in accs:
                    a_ref[...] = jnp.zeros_like(a_ref)

            for a_ref, val in zip(accs, ac):
                a_ref[...] += val

    in_specs = [pl.BlockSpec((tr, cols), functools.partial(lambda i, cb: (i, cb), cb=cb)) for _, cols, cb in rows_in]
    in_specs += [pl.BlockSpec(v.shape, lambda i: (0, 0)) for v in vecs_in]
    out_specs = [pl.BlockSpec((tr, cols), lambda i: (i, 0)) for cols, _ in rows_out]
    out_specs += [pl.BlockSpec((r, cols), lambda i: (0, 0)) for r, cols in accs_out]
    out_shape = [jax.ShapeDtypeStruct((n_rows, cols), dt) for cols, dt in rows_out]
    out_shape += [jax.ShapeDtypeStruct((r, cols), F32) for r, cols in accs_out]
    res = pl.pallas_call(
        body, name=name, grid=(n_rows // tr,), in_specs=in_specs, out_specs=out_specs, out_shape=out_shape,
        compiler_params=_params(("arbitrary",)),
    )(*[a for a, _, _ in rows_in], *vecs_in)
    return res


def _colsum(v):
    return jnp.sum(v, axis=0, keepdims=True)


def _rstd(x):
    return lax.rsqrt(jnp.mean(x * x, axis=-1, keepdims=True) + EPS)


def _rms_bwd(x, g, dy):
    r = _rstd(x)
    xh = x * r
    dxh = dy * g
    dx = r * (dxh - xh * jnp.mean(dxh * xh, axis=-1, keepdims=True))
    return dx, _colsum(dy * xh)


def _sigmoid(z):
    return 1.0 / (1.0 + jnp.exp(-z))


_GELU_K = math.sqrt(2.0 / math.pi)


def _gelu_parts(g):
    t = jnp.tanh(_GELU_K * (g + 0.044715 * g * g * g))
    gel = 0.5 * g * (1.0 + t)
    dgel = 0.5 * (1.0 + t) + 0.5 * g * (1.0 - t * t) * (_GELU_K * (1.0 + 3.0 * 0.044715 * g * g))
    return gel, dgel


def _diag_visible(t, unit):
    rows = lax.broadcasted_iota(jnp.int32, (t, t), 0)
    cols = lax.broadcasted_iota(jnp.int32, (t, t), 1)
    if unit > 1:
        sh = int(math.log2(unit))
        assert 1 << sh == unit and t % unit == 0
        rows, cols = jnp.right_shift(rows, sh), jnp.right_shift(cols, sh)
    return cols <= rows


def _lane_pick(tile, lane):
    idx = lax.broadcasted_iota(jnp.int32, tile.shape, 1)
    return jnp.sum(jnp.where(idx == lane, tile, 0.0), axis=1, keepdims=True)


def _lane_put(tile, lane, col):
    idx = lax.broadcasted_iota(jnp.int32, tile.shape, 1)
    return jnp.where(idx == lane, col, tile)


def _head_cat(refs, shared, rows, h):
    hs = slice(h * LANES, (h + 1) * LANES)
    vals = [(r[rows, :] if sh else r[rows, hs]).astype(BF16) for r, sh in zip(refs, shared)]
    return vals[0] if len(vals) == 1 else jnp.concatenate(vals, axis=1)


def _blk_rows(i, t):
    return pl.ds(pl.multiple_of(i * t, t), t)


def _piece_specs(pieces, rows, row_idx):
    return [pl.BlockSpec((rows, LANES if sh else HEADS * LANES), functools.partial(lambda b, i, cb: (row_idx(b, i), cb), cb=cb))
            for _, cb, sh in pieces]


def _attn_fwd(name, qp, kp, vp, bias, unit, scale, n_seq, seq, t, side=None):
    nb = seq // t
    n_tok = n_seq * seq
    nq, nk_p = len(qp), len(kp)
    q_sh, k_sh = [p[2] for p in qp], [p[2] for p in kp]
    nbias = 2 if bias is not None else 0

    def body(*refs):
        q_refs, k_refs = refs[:nq], refs[nq:nq + nk_p]
        v_ref = refs[nq + nk_p]
        bias_refs = refs[nq + nk_p + 1:nq + nk_p + 1 + nbias]
        o_ref, lse_ref = refs[nq + nk_p + 1 + nbias:]
        qi = pl.program_id(1)
        lse_tile = jnp.zeros((t, LANES), F32)
        for h in range(HEADS):
            hs = slice(h * LANES, (h + 1) * LANES)
            q = _head_cat(q_refs, q_sh, slice(None), h)
            cq = _lane_pick(bias_refs[0][...], ROPE + h) if bias is not None else None

            def block(kb, carry, diag, h=h, hs=hs, q=q, cq=cq):
                m, l, acc = carry
                rows = _blk_rows(kb, t)
                s = lax.dot_general(q, _head_cat(k_refs, k_sh, rows, h), _DIMS["nt"], preferred_element_type=F32) * scale
                if bias is not None:
                    s = s + cq - bias_refs[1][kb, h:h + 1, :]
                if diag:
                    s = jnp.where(_diag_visible(t, unit), s, NEG_INF)
                m_new = jnp.maximum(m, jnp.max(s, axis=1, keepdims=True))
                alpha = jnp.exp(m - m_new)
                p = jnp.exp(s - m_new)
                l = alpha * l + jnp.sum(p, axis=1, keepdims=True)
                acc = alpha * acc + jnp.dot(p.astype(BF16), v_ref[rows, hs].astype(BF16), preferred_element_type=F32)
                return m_new, l, acc

            init = (jnp.full((t, 1), NEG_INF, F32), jnp.zeros((t, 1), F32), jnp.zeros((t, LANES), F32))
            carry = lax.fori_loop(0, qi, lambda kb, c: block(kb, c, False), init)
            m, l, acc = block(qi, carry, True)
            o_ref[:, hs] = acc / l
            lse_tile = _lane_put(lse_tile, h, m + jnp.log(l))
        lse_ref[...] = lse_tile

    tile_row = lambda b, i: b * nb + i
    seq_row = lambda b, i: b
    lane_tile = pl.BlockSpec((t, LANES), lambda b, i: (b * nb + i, 0))
    in_specs = _piece_specs(qp, t, tile_row) + _piece_specs(kp, seq, seq_row) + _piece_specs([vp + (False,)], seq, seq_row)
    args = [p[0] for p in qp] + [p[0] for p in kp] + [vp[0]]
    if bias is not None:
        in_specs += [lane_tile, pl.BlockSpec((None, nb, HEADS, t), lambda b, i: (b, 0, 0, 0))]
        args += list(bias)
    return _hosted_call(
        body, name, (n_seq, nb), in_specs,
        [pl.BlockSpec((t, HEADS * LANES), lambda b, i: (b * nb + i, 0)), lane_tile],
        [jax.ShapeDtypeStruct((n_tok, HEADS * LANES), F32), jax.ShapeDtypeStruct((n_tok, LANES), F32)], args, side)


def _attn_bwd_dq(name, qp, kp, vp, o, do, lse, bias, unit, scale, n_seq, seq, t, side=None, grad_dtype=F32):
    nb = seq // t
    n_tok = n_seq * seq
    nq, nk_p = len(qp), len(kp)
    q_sh, k_sh = [p[2] for p in qp], [p[2] for p in kp]
    nbias = 2 if bias is not None else 0
    n_in = nq + nk_p + 4 + nbias

    def body(*refs):
        q_refs, k_refs = refs[:nq], refs[nq:nq + nk_p]
        v_ref, o_ref, do_ref, lse_ref = refs[nq + nk_p:nq + nk_p + 4]
        bias_refs = refs[nq + nk_p + 4:n_in]
        dq_refs = refs[n_in:n_in + nq]
        delta_ref, dob_ref = refs[n_in + nq:n_in + nq + 2]
        qi = pl.program_id(1)
        delta_tile = jnp.zeros((t, LANES), F32)
        dc_tile = jnp.zeros((t, LANES), F32)
        lse_all = lse_ref[...]
        for h in range(HEADS):
            hs = slice(h * LANES, (h + 1) * LANES)
            q = _head_cat(q_refs, q_sh, slice(None), h)
            do_f = do_ref[:, hs]
            do_b = do_f.astype(BF16)
            dob_ref[:, hs] = do_b
            delta = jnp.sum(do_f * o_ref[:, hs], axis=1, keepdims=True)
            lse = _lane_pick(lse_all, h)
            cq = _lane_pick(bias_refs[0][...], ROPE + h) if bias is not None else None

            def block(kb, carry, diag, h=h, hs=hs, q=q, cq=cq, do_b=do_b, delta=delta, lse=lse):
                dq_acc, dc_acc = carry
                rows = _blk_rows(kb, t)
                k = _head_cat(k_refs, k_sh, rows, h)
                s = lax.dot_general(q, k, _DIMS["nt"], preferred_element_type=F32) * scale
                if bias is not None:
                    s = s + cq - bias_refs[1][kb, h:h + 1, :]
                if diag:
                    s = jnp.where(_diag_visible(t, unit), s, NEG_INF)
                p = jnp.exp(s - lse)
                dp = lax.dot_general(do_b, v_ref[rows, hs].astype(BF16), _DIMS["nt"], preferred_element_type=F32)
                ds = p * (dp - delta)
                return (dq_acc + jnp.dot(ds.astype(BF16), k, preferred_element_type=F32),
                        dc_acc + jnp.sum(ds, axis=1, keepdims=True))

            init = (jnp.zeros((t, nq * LANES), F32), jnp.zeros((t, 1), F32))
            carry = lax.fori_loop(0, qi, lambda kb, c: block(kb, c, False), init)
            dq_acc, dc_acc = block(qi, carry, True)
            for n_p in range(nq):
                dq_refs[n_p][:, hs] = (dq_acc[:, n_p * LANES:(n_p + 1) * LANES] * scale).astype(grad_dtype)
            delta_tile = _lane_put(delta_tile, h, delta)
            dc_tile = _lane_put(dc_tile, ROPE + h, dc_acc)
        delta_ref[...] = delta_tile
        if bias is not None:
            refs[n_in + nq + 2][...] = dc_tile

    tile_row = lambda b, i: b * nb + i
    seq_row = lambda b, i: b
    lane_tile = pl.BlockSpec((t, LANES), lambda b, i: (b * nb + i, 0))
    head_tile = pl.BlockSpec((t, HEADS * LANES), lambda b, i: (b * nb + i, 0))
    in_specs = _piece_specs(qp, t, tile_row) + _piece_specs(kp, seq, seq_row) + _piece_specs([vp + (False,)], seq, seq_row)
    in_specs += [head_tile, head_tile, lane_tile]
    args = [p[0] for p in qp] + [p[0] for p in kp] + [vp[0], o, do, lse]
    if bias is not None:
        in_specs += [lane_tile, pl.BlockSpec((None, nb, HEADS, t), lambda b, i: (b, 0, 0, 0))]
        args += list(bias)
    out_specs = [head_tile] * nq + [lane_tile, head_tile] + ([lane_tile] if bias is not None else [])
    out_shape = [jax.ShapeDtypeStruct((n_tok, HEADS * LANES), grad_dtype)] * nq
    out_shape += [jax.ShapeDtypeStruct((n_tok, LANES), F32), jax.ShapeDtypeStruct((n_tok, HEADS * LANES), BF16)]
    if bias is not None:
        out_shape.append(jax.ShapeDtypeStruct((n_tok, LANES), F32))
    return _hosted_call(body, name, (n_seq, nb), in_specs, out_specs, out_shape, args, side)


def _attn_bwd_dkv(name, qp, kp, vp, dob, lse, delta, bias, unit, scale, n_seq, seq, t, side=None, grad_dtype=F32):
    nb = seq // t
    n_tok = n_seq * seq
    nq, nk_p = len(qp), len(kp)
    q_sh, k_sh = [p[2] for p in qp], [p[2] for p in kp]
    nbias = 2 if bias is not None else 0
    n_in = nq + nk_p + 4 + nbias

    def body(*refs):
        q_refs, k_refs = refs[:nq], refs[nq:nq + nk_p]
        v_ref, dob_ref, lse_ref, delta_ref = refs[nq + nk_p:nq + nk_p + 4]
        bias_refs = refs[nq + nk_p + 4:n_in]
        dk_refs = refs[n_in:n_in + nk_p]
        dv_ref = refs[n_in + nk_p]
        ki = pl.program_id(1)
        shared_acc = [jnp.zeros((t, LANES), F32) for _ in range(nk_p)]
        for h in range(HEADS):
            hs = slice(h * LANES, (h + 1) * LANES)
            k = _head_cat(k_refs, k_sh, slice(None), h)
            v = v_ref[:, hs].astype(BF16)
            ck = bias_refs[1][h:h + 1, :] if bias is not None else None

            def block(qb, carry, diag, h=h, hs=hs, k=k, v=v, ck=ck):
                dk_acc, dv_acc, dc_acc = carry
                rows = _blk_rows(qb, t)
                q = _head_cat(q_refs, q_sh, rows, h)
                s = lax.dot_general(q, k, _DIMS["nt"], preferred_element_type=F32) * scale
                if bias is not None:
                    s = s + _lane_pick(bias_refs[0][rows, :], ROPE + h) - ck
                if diag:
                    s = jnp.where(_diag_visible(t, unit), s, NEG_INF)
                p = jnp.exp(s - _lane_pick(lse_ref[rows, :], h))
                do_b = dob_ref[rows, hs]
                dp = lax.dot_general(do_b, v, _DIMS["nt"], preferred_element_type=F32)
                ds = p * (dp - _lane_pick(delta_ref[rows, :], h))
                return (dk_acc + lax.dot_general(ds.astype(BF16), q, _DIMS["tn"], preferred_element_type=F32),
                        dv_acc + lax.dot_general(p.astype(BF16), do_b, _DIMS["tn"], preferred_element_type=F32),
                        dc_acc - jnp.sum(ds, axis=0, keepdims=True))

            init = (jnp.zeros((t, nk_p * LANES), F32), jnp.zeros((t, LANES), F32), jnp.zeros((1, t), F32))
            carry = block(ki, init, True)
            dk_acc, dv_acc, dc_acc = lax.fori_loop(ki + 1, nb, lambda qb, c: block(qb, c, False), carry)
            for n_p in range(nk_p):
                part = dk_acc[:, n_p * LANES:(n_p + 1) * LANES] * scale
                if k_sh[n_p]:
                    shared_acc[n_p] = shared_acc[n_p] + part
                else:
                    dk_refs[n_p][:, hs] = part.astype(grad_dtype)
            dv_ref[:, hs] = dv_acc.astype(grad_dtype)
            if bias is not None:
                refs[n_in + nk_p + 1][h:h + 1, :] = dc_acc
        for n_p in range(nk_p):
            if k_sh[n_p]:
                dk_refs[n_p][...] = shared_acc[n_p]

    tile_row = lambda b, i: b * nb + i
    seq_row = lambda b, i: b
    lane_seq = pl.BlockSpec((seq, LANES), lambda b, i: (b, 0))
    head_tile = pl.BlockSpec((t, HEADS * LANES), lambda b, i: (b * nb + i, 0))
    row_tile = pl.BlockSpec((None, None, HEADS, t), lambda b, i: (b, i, 0, 0))
    in_specs = _piece_specs(qp, seq, seq_row) + _piece_specs(kp, t, tile_row) + _piece_specs([vp + (False,)], t, tile_row)
    in_specs += [pl.BlockSpec((seq, HEADS * LANES), lambda b, i: (b, 0)), lane_seq, lane_seq]
    args = [p[0] for p in qp] + [p[0] for p in kp] + [vp[0], dob, lse, delta]
    if bias is not None:
        in_specs += [lane_seq, row_tile]
        args += list(bias)
    out_specs = [pl.BlockSpec((t, LANES if sh else HEADS * LANES), lambda b, i: (b * nb + i, 0)) for sh in k_sh] + [head_tile]
    out_shape = [jax.ShapeDtypeStruct((n_tok, LANES), F32) if sh else jax.ShapeDtypeStruct((n_tok, HEADS * LANES), grad_dtype)
                 for sh in k_sh]
    out_shape.append(jax.ShapeDtypeStruct((n_tok, HEADS * LANES), grad_dtype))
    if bias is not None:
        out_specs.append(row_tile)
        out_shape.append(jax.ShapeDtypeStruct((n_seq, nb, HEADS, t), F32))
    return _hosted_call(body, name, (n_seq, nb), in_specs, out_specs, out_shape, args, side)


def _old_attn_bwd_dq(name, qp, kp, vp, o, do, lse, bias, unit, scale, n_seq, seq, t):
    nb = seq // t
    n_tok = n_seq * seq
    nq, nk_p = len(qp), len(kp)
    nbias = 2 if bias is not None else 0
    n_in = nq + nk_p + 4 + nbias
    n_out = nq + (1 if bias is not None else 0)

    def body(*refs):
        q_refs, k_refs = refs[:nq], refs[nq:nq + nk_p]
        v_ref, o_ref, do_ref, lse_ref = refs[nq + nk_p:nq + nk_p + 4]
        bias_refs = refs[nq + nk_p + 4:n_in]
        outs = refs[n_in:n_in + n_out]
        dq_s, delta_s, dc_s = refs[n_in + n_out:]
        qi, ki = pl.program_id(2), pl.program_id(3)

        @pl.when(ki == 0)
        def _():
            dq_s[...] = jnp.zeros_like(dq_s)
            dc_s[...] = jnp.zeros_like(dc_s)
            delta_s[...] = jnp.sum(do_ref[...] * o_ref[...], axis=1, keepdims=True)

        @pl.when(ki <= qi)
        def _():
            s = _scores(q_refs, k_refs, bias_refs, qi, ki, t, unit, scale)
            p = jnp.exp(s - lse_ref[...])
            dp = lax.dot_general(do_ref[...].astype(BF16), v_ref[...].astype(BF16), _DIMS["nt"],
                                 preferred_element_type=F32)
            ds = p * (dp - delta_s[...])
            dq_s[...] += jnp.dot(ds.astype(BF16), _cat(k_refs), preferred_element_type=F32)
            dc_s[...] += jnp.sum(ds, axis=1, keepdims=True)

        @pl.when(ki == qi)
        def _():
            for n_p in range(nq):
                outs[n_p][...] = dq_s[:, n_p * LANES:(n_p + 1) * LANES] * scale
            if bias is not None:
                outs[nq][...] = dc_s[...]

    q_row = lambda b, i, j: b * nb + i
    k_row = lambda b, i, j: b * nb + jnp.minimum(j, i)
    head_q = pl.BlockSpec((t, LANES), lambda b, h, i, j: (b * nb + i, h))
    col_q = pl.BlockSpec((None, t, 1), lambda b, h, i, j: (h, b * nb + i, 0))
    in_specs = [_piece_spec(t, p, q_row) for p in qp] + [_piece_spec(t, p, k_row) for p in kp]
    in_specs += [_piece_spec(t, vp, k_row), head_q, head_q, col_q]
    args = [p[0] for p in qp] + [p[0] for p in kp] + [vp[0], o, do, lse]
    if bias is not None:
        in_specs += [col_q, pl.BlockSpec((None, 1, t), lambda b, h, i, j: (b * HEADS + h, 0, jnp.minimum(j, i)))]
        args += list(bias)
    out_specs = [head_q] * nq + ([col_q] if bias is not None else [])
    out_shape = [jax.ShapeDtypeStruct((n_tok, HEADS * LANES), F32)] * nq
    if bias is not None:
        out_shape.append(jax.ShapeDtypeStruct((HEADS, n_tok, 1), F32))
    return pl.pallas_call(
        body, name=name, grid=(n_seq, HEADS, nb, nb), in_specs=in_specs, out_specs=out_specs, out_shape=out_shape,
        scratch_shapes=[pltpu.VMEM((t, nq * LANES), F32), pltpu.VMEM((t, 1), F32), pltpu.VMEM((t, 1), F32)],
        compiler_params=_params(("parallel", "parallel", "arbitrary", "arbitrary")),
    )(*args)


def _old_attn_bwd_dkv(name, qp, kp, vp, o, do, lse, bias, unit, scale, n_seq, seq, t):
    nb = seq // t
    n_tok = n_seq * seq
    nq, nk_p = len(qp), len(kp)
    nbias = 2 if bias is not None else 0
    n_in = nq + nk_p + 4 + nbias
    n_out = nk_p + 1 + (1 if bias is not None else 0)

    def body(*refs):
        q_refs, k_refs = refs[:nq], refs[nq:nq + nk_p]
        v_ref, o_ref, do_ref, lse_ref = refs[nq + nk_p:nq + nk_p + 4]
        bias_refs = refs[nq + nk_p + 4:n_in]
        outs = refs[n_in:n_in + n_out]
        dk_s, dv_s, dc_s = refs[n_in + n_out:]
        ki, qi = pl.program_id(2), pl.program_id(3)

        @pl.when(qi == 0)
        def _():
            dk_s[...] = jnp.zeros_like(dk_s)
            dv_s[...] = jnp.zeros_like(dv_s)
            dc_s[...] = jnp.zeros_like(dc_s)

        @pl.when(qi >= ki)
        def _():
            s = _scores(q_refs, k_refs, bias_refs, qi, ki, t, unit, scale)
            p = jnp.exp(s - lse_ref[...])
            do_b = do_ref[...].astype(BF16)
            delta = jnp.sum(do_ref[...] * o_ref[...], axis=1, keepdims=True)
            dp = lax.dot_general(do_b, v_ref[...].astype(BF16), _DIMS["nt"], preferred_element_type=F32)
            ds = p * (dp - delta)
            dv_s[...] += lax.dot_general(p.astype(BF16), do_b, _DIMS["tn"], preferred_element_type=F32)
            dk_s[...] += lax.dot_general(ds.astype(BF16), _cat(q_refs), _DIMS["tn"], preferred_element_type=F32)
            dc_s[...] -= jnp.sum(ds, axis=0, keepdims=True)

        @pl.when(qi == nb - 1)
        def _():
            for n_p in range(nk_p):
                outs[n_p][...] = dk_s[:, n_p * LANES:(n_p + 1) * LANES] * scale
            outs[nk_p][...] = dv_s[...]
            if bias is not None:
                outs[nk_p + 1][...] = dc_s[...]

    q_row = lambda b, i, j: b * nb + jnp.maximum(j, i)
    k_row = lambda b, i, j: b * nb + i
    head_q = pl.BlockSpec((t, LANES), lambda b, h, i, j: (b * nb + jnp.maximum(j, i), h))
    col_q = pl.BlockSpec((None, t, 1), lambda b, h, i, j: (h, b * nb + jnp.maximum(j, i), 0))
    head_k = pl.BlockSpec((t, LANES), lambda b, h, i, j: (b * nb + i, h))
    row_k = pl.BlockSpec((None, 1, t), lambda b, h, i, j: (b * HEADS + h, 0, i))
    in_specs = [_piece_spec(t, p, q_row) for p in qp] + [_piece_spec(t, p, k_row) for p in kp]
    in_specs += [_piece_spec(t, vp, k_row), head_q, head_q, col_q]
    args = [p[0] for p in qp] + [p[0] for p in kp] + [vp[0], o, do, lse]
    if bias is not None:
        in_specs += [col_q, row_k]
        args += list(bias)
    out_specs = [head_k] * (nk_p + 1) + ([row_k] if bias is not None else [])
    out_shape = [jax.ShapeDtypeStruct((n_tok, HEADS * LANES), F32)] * (nk_p + 1)
    if bias is not None:
        out_shape.append(jax.ShapeDtypeStruct((n_seq * HEADS, 1, seq), F32))
    return pl.pallas_call(
        body, name=name, grid=(n_seq, HEADS, nb, nb), in_specs=in_specs, out_specs=out_specs, out_shape=out_shape,
        scratch_shapes=[pltpu.VMEM((t, nk_p * LANES), F32), pltpu.VMEM((t, LANES), F32), pltpu.VMEM((1, t), F32)],
        compiler_params=_params(("parallel", "parallel", "arbitrary", "arbitrary")),
    )(*args)


def _seq_cumsum(name, x, col_block, n_seq, seq, reverse, pre=None, vec=None):
    t = _tile(seq, 256, 128)
    nb = seq // t

    def body(*refs):
        x_ref = refs[0]
        vec_ref = refs[1] if vec is not None else None
        o_ref, carry = refs[-2], refs[-1]

        @pl.when(pl.program_id(1) == 0)
        def _():
            carry[...] = jnp.zeros_like(carry)

        v = x_ref[...]
        if pre is not None:
            v = pre(v, vec_ref[...])
        r = lax.broadcasted_iota(jnp.int32, (t, t), 0)
        c = lax.broadcasted_iota(jnp.int32, (t, t), 1)
        tri = jnp.where((c >= r) if reverse else (c <= r), 1.0, 0.0).astype(BF16)
        hi = v.astype(BF16)
        mid = (v - hi.astype(F32)).astype(BF16)
        lo = (v - hi.astype(F32) - mid.astype(F32)).astype(BF16)
        acc = jnp.dot(tri, hi, preferred_element_type=F32)
        acc += jnp.dot(tri, mid, preferred_element_type=F32)
        acc += jnp.dot(tri, lo, preferred_element_type=F32)
        o_ref[...] = acc + carry[...]
        carry[...] += _colsum(v)

    blk = (lambda b, i: (b * nb + nb - 1 - i)) if reverse else (lambda b, i: (b * nb + i))
    in_specs = [pl.BlockSpec((t, LANES), lambda b, i: (blk(b, i), col_block))]
    args = [x]
    if vec is not None:
        in_specs.append(pl.BlockSpec(vec.shape, lambda b, i: (0, 0)))
        args.append(vec)
    return pl.pallas_call(
        body, name=name, grid=(n_seq, nb), in_specs=in_specs,
        out_specs=pl.BlockSpec((t, LANES), lambda b, i: (blk(b, i), 0)),
        out_shape=jax.ShapeDtypeStruct((n_seq * seq, LANES), F32),
        scratch_shapes=[pltpu.VMEM((1, LANES), F32)],
        compiler_params=_params(("arbitrary", "arbitrary")),
    )(*args)


def _log_sigmoid(z):
    return -(jnp.maximum(-z, 0.0) + jnp.log(1.0 + jnp.exp(-jnp.abs(z))))


def _shift_down(u, prev_ref, n):
    out = pltpu.roll(u, n, 0)
    row = lax.broadcasted_iota(jnp.int32, u.shape, 0)
    for r in range(n):
        out = jnp.where(row == r, prev_ref[8 - n + r:8 - n + r + 1, :], out)
    return out


def _shift_up(u, next_ref, n):
    ts = u.shape[0]
    out = pltpu.roll(u, ts - n, 0)
    row = lax.broadcasted_iota(jnp.int32, u.shape, 0)
    for r in range(n):
        out = jnp.where(row == ts - n + r, next_ref[r:r + 1, :], out)
    return out


def _conv_taps(u, prev_ref, w_ref, b_ref):
    s1, s2 = _shift_down(u, prev_ref, 1), _shift_down(u, prev_ref, 2)
    return (w_ref[0:1, :] * s2 + w_ref[1:2, :] * s1 + w_ref[2:3, :] * u) + b_ref[...], s1, s2


def _conv_glu_fwd(u_il, cw_il, cb_il, n_seq, seq, wt):
    n_tok, two_f = u_il.shape
    nct = two_f // (2 * wt)
    ts = _tile(seq, 256, 8)
    ns = seq // ts

    def body(u_ref, w_ref, b_ref, a_ref, carry):
        @pl.when(pl.program_id(2) == 0)
        def _():
            carry[...] = jnp.zeros_like(carry)

        u = u_ref[...].astype(F32)
        uc, _, _ = _conv_taps(u, carry, w_ref, b_ref)
        gel, _ = _gelu_parts(uc[:, :wt])
        a_ref[...] = (gel * uc[:, wt:]).astype(a_ref.dtype)
        carry[...] = u[ts - 8:, :]

    return pl.pallas_call(
        body, name="conv_glu_fwd", grid=(nct, n_seq, ns),
        in_specs=[pl.BlockSpec((ts, 2 * wt), lambda j, b, s: (b * ns + s, j)),
                  pl.BlockSpec((3, 2 * wt), lambda j, b, s: (0, j)),
                  pl.BlockSpec((1, 2 * wt), lambda j, b, s: (0, j))],
        out_specs=pl.BlockSpec((ts, wt), lambda j, b, s: (b * ns + s, j)),
        out_shape=jax.ShapeDtypeStruct((n_tok, two_f // 2), BF16),
        scratch_shapes=[pltpu.VMEM((8, 2 * wt), F32)],
        compiler_params=_params(("parallel", "arbitrary", "arbitrary")),
    )(u_il, cw_il, cb_il)


def _conv_glu_bwd_pre(u_il, da, cw_il, cb_il, n_seq, seq, wt):
    n_tok, two_f = u_il.shape
    nct = two_f // (2 * wt)
    ts = _tile(seq, 256, 8)
    ns = seq // ts

    def body(u_ref, da_ref, w_ref, b_ref, d_ref, acc_ref, carry):
        first = jnp.logical_and(pl.program_id(1) == 0, pl.program_id(2) == 0)

        @pl.when(first)
        def _():
            acc_ref[...] = jnp.zeros_like(acc_ref)

        @pl.when(pl.program_id(2) == 0)
        def _():
            carry[...] = jnp.zeros_like(carry)

        u = u_ref[...].astype(F32)
        uc, s1, s2 = _conv_taps(u, carry, w_ref, b_ref)
        gel, dgel = _gelu_parts(uc[:, :wt])
        da_v = da_ref[...].astype(F32)
        d = jnp.concatenate([da_v * uc[:, wt:] * dgel, da_v * gel], axis=1)
        d_ref[...] = d.astype(d_ref.dtype)
        acc_ref[0:1, :] += _colsum(d * s2)
        acc_ref[1:2, :] += _colsum(d * s1)
        acc_ref[2:3, :] += _colsum(d * u)
        acc_ref[3:4, :] += _colsum(d)
        carry[...] = u[ts - 8:, :]

    return pl.pallas_call(
        body, name="conv_glu_bwd_pre", grid=(nct, n_seq, ns),
        in_specs=[pl.BlockSpec((ts, 2 * wt), lambda j, b, s: (b * ns + s, j)),
                  pl.BlockSpec((ts, wt), lambda j, b, s: (b * ns + s, j)),
                  pl.BlockSpec((3, 2 * wt), lambda j, b, s: (0, j)),
                  pl.BlockSpec((1, 2 * wt), lambda j, b, s: (0, j))],
        out_specs=[pl.BlockSpec((ts, 2 * wt), lambda j, b, s: (b * ns + s, j)),
                   pl.BlockSpec((8, 2 * wt), lambda j, b, s: (0, j))],
        out_shape=[jax.ShapeDtypeStruct((n_tok, two_f), BF16), jax.ShapeDtypeStruct((8, two_f), F32)],
        scratch_shapes=[pltpu.VMEM((8, 2 * wt), F32)],
        compiler_params=_params(("parallel", "arbitrary", "arbitrary")),
    )(u_il, da, cw_il, cb_il)


def _conv_bwd_input(d_il, cw_il, n_seq, seq, wt):
    n_tok, two_f = d_il.shape
    nct = two_f // (2 * wt)
    ts = _tile(seq, 256, 8)
    ns = seq // ts

    def body(d_ref, w_ref, o_ref, carry):
        @pl.when(pl.program_id(2) == 0)
        def _():
            carry[...] = jnp.zeros_like(carry)

        d = d_ref[...].astype(F32)
        o_ref[...] = (w_ref[2:3, :] * d + w_ref[1:2, :] * _shift_up(d, carry, 1)
                      + w_ref[0:1, :] * _shift_up(d, carry, 2)).astype(o_ref.dtype)
        carry[...] = d[:8, :]

    rev = lambda j, b, s: (b * ns + ns - 1 - s, j)
    return pl.pallas_call(
        body, name="conv_bwd_input", grid=(nct, n_seq, ns),
        in_specs=[pl.BlockSpec((ts, 2 * wt), rev), pl.BlockSpec((3, 2 * wt), lambda j, b, s: (0, j))],
        out_specs=pl.BlockSpec((ts, 2 * wt), rev),
        out_shape=jax.ShapeDtypeStruct((n_tok, two_f), BF16),
        scratch_shapes=[pltpu.VMEM((8, 2 * wt), F32)],
        compiler_params=_params(("parallel", "arbitrary", "arbitrary")),
    )(d_il, cw_il)


HBM = pl.BlockSpec(memory_space=pltpu.HBM)
_CHIP_FLIPS = ((1, 0), (0, 1), (1, 1))


def _place():
    x, y, c = lax.axis_index("x"), lax.axis_index("y"), lax.axis_index("c")
    return x, y, c, 2 * x + y


def _flip(v, f):
    return 1 - v if f else v


def _half_rows(c, half):
    return pl.ds(pl.multiple_of(c * half, 16), half)


def _remote(src, dst, ssem, rsem, dev):
    return pltpu.make_async_remote_copy(src_ref=src, dst_ref=dst, send_sem=ssem, recv_sem=rsem,
                                        device_id=dev, device_id_type=MESH)


def _comm_call(name, body, ins, out_shapes, n_sems):
    return pl.pallas_call(
        body, name=name, in_specs=[HBM] * len(ins), out_specs=[HBM] * len(out_shapes),
        out_shape=[pltpu.HBM(s.shape, s.dtype) for s in out_shapes],
        scratch_shapes=[pltpu.SemaphoreType.DMA((n_sems,)), pltpu.SemaphoreType.DMA((n_sems,))],
    )(*ins)


def _all_gather_weights(shards, smalls):
    side = _gather_side(shards, smalls)
    nt = len(shards) + len(smalls)

    def body(*refs):
        for part in (side.start, side.mid, side.end):
            part(refs[:nt], refs[nt:2 * nt], *refs[2 * nt:])

    res = _comm_call("all_gather_weights", body, side.ins, side.outs, side.n_sems)
    return res[:len(shards)], res[len(shards):]


def _pair_split(name, grads):
    n = len(grads)

    def body(*refs):
        src, got = refs[:n], refs[n:2 * n]
        ssem, rsem = refs[2 * n:]
        x, y, c, _ = _place()
        cps = []
        for w in range(n):
            half = grads[w].shape[1] // 2
            cp = _remote(src[w].at[:, _half_rows(1 - c, half)], got[w], ssem.at[w], rsem.at[w], (x, y, 1 - c))
            cp.start()
            cps.append(cp)
        for cp in cps:
            cp.wait()

    outs = [jax.ShapeDtypeStruct((g.shape[0], g.shape[1] // 2, g.shape[2]), g.dtype) for g in grads]
    return _comm_call(name, body, grads, outs, n)


def _chip_scatter(parts):
    side = _scatter_side(parts)

    def body(*refs):
        n = len(parts)
        side.start(refs[:n], refs[n:2 * n], *refs[2 * n:])
        side.end(refs[:n], refs[n:2 * n], *refs[2 * n:])

    return _comm_call("rs_chip_scatter", body, parts, side.outs, side.n_sems)


class _Side:
    def __init__(self, ins, outs, n_sems, start, mid, end, mid_step=None):
        self.ins, self.outs, self.n_sems = list(ins), list(outs), n_sems
        self.start, self.mid, self.end, self.mid_step = start, mid, end, mid_step
        self.aliases = {}


def _scatter_side(parts):
    n = len(parts)

    def copies(src, dst, ssem, rsem):
        x, y, c, _ = _place()
        out = []
        for w in range(n):
            for k, (fx, fy) in enumerate(_CHIP_FLIPS):
                px, py = _flip(x, fx), _flip(y, fy)
                out.append(_remote(src[w].at[2 * px + py], dst[w].at[k], ssem.at[w * 3 + k], rsem.at[w * 3 + k], (px, py, c)))
        return out

    def start(src, dst, ssem, rsem):
        for cp in copies(src, dst, ssem, rsem):
            cp.start()

    def end(src, dst, ssem, rsem):
        for cp in copies(src, dst, ssem, rsem):
            cp.wait()

    outs = [jax.ShapeDtypeStruct((3,) + p.shape[1:], p.dtype) for p in parts]
    return _Side(parts, outs, 3 * n, start, None, end)


def _gather_side(shards, smalls, mid_step=None, into=None):
    n, ns = len(shards), len(smalls)
    into = into or [(None, a.shape[0], 0) for a in shards]

    def dst_rows(w, c):
        half = shards[w].shape[0] // 2
        return pl.ds(pl.multiple_of(into[w][2] + c * half, 16), half)

    def ici(src, dst, ssem, rsem, w, k):
        x, y, c, me = _place()
        fx, fy = _CHIP_FLIPS[k]
        rows = _half_rows(c, shards[w].shape[0] // 2)
        return _remote(src[w].at[rows], dst[w].at[me, dst_rows(w, c)], ssem.at[w * 6 + k], rsem.at[w * 6 + k],
                       (_flip(x, fx), _flip(y, fy), c))

    def small(src, dst, ssem, rsem, s, k):
        x, y, c, me = _place()
        fx, fy = _CHIP_FLIPS[k]
        sem = 6 * n + 3 * s + k
        return _remote(src[n + s], dst[n + s].at[me], ssem.at[sem], rsem.at[sem], (_flip(x, fx), _flip(y, fy), c))

    def landed(dst, ssem, rsem, w, k, sender_c, sem_off):
        x, y, c, _ = _place()
        fx, fy = _CHIP_FLIPS[k]
        got = dst[w].at[2 * _flip(x, fx) + _flip(y, fy), dst_rows(w, sender_c)]
        return _remote(got, got, ssem.at[w * 6 + sem_off + k], rsem.at[w * 6 + sem_off + k], (x, y, 1 - c))

    def start(src, dst, ssem, rsem):
        for s in range(ns):
            for k in range(3):
                small(src, dst, ssem, rsem, s, k).start()
        for w in range(n):
            for k in range(3):
                ici(src, dst, ssem, rsem, w, k).start()

    def mid(src, dst, ssem, rsem):
        c = lax.axis_index("c")
        for w in range(n):
            for k in range(3):
                landed(dst, ssem, rsem, w, k, c, 0).wait_recv()
                landed(dst, ssem, rsem, w, k, c, 3).start()

    def end(src, dst, ssem, rsem):
        c = lax.axis_index("c")
        for w in range(n):
            for k in range(3):
                landed(dst, ssem, rsem, w, k, 1 - c, 3).wait_recv()
        for s in range(ns):
            for k in range(3):
                small(src, dst, ssem, rsem, s, k).wait()
        for w in range(n):
            for k in range(3):
                ici(src, dst, ssem, rsem, w, k).wait_send()
                landed(dst, ssem, rsem, w, k, c, 3).wait_send()

    outs = [jax.ShapeDtypeStruct((N_CHIPS, rows, a.shape[1]), a.dtype) for a, (_, rows, _) in zip(shards, into)]
    outs += [jax.ShapeDtypeStruct((N_CHIPS,) + a.shape, a.dtype) for a in smalls]
    filled = [(w, arr) for w, (arr, _, _) in enumerate(into) if arr is not None]
    side = _Side(list(shards) + list(smalls) + [arr for _, arr in filled], outs, 6 * n + 3 * ns, start, mid, end, mid_step)
    side.aliases = {n + ns + i: w for i, (w, _) in enumerate(filled)}
    return side


def _host(body, n_in, n_out, side, grid):
    if side is None:
        return body
    ns_in, ns_out = len(side.ins), len(side.outs)
    n_steps = math.prod(grid)
    mid_step = side.mid_step
    if side.mid is not None and not isinstance(mid_step, int):
        mid_step = min(n_steps - 1, int(mid_step * n_steps))

    def wrapped(*refs):
        ins, s_ins = refs[:n_in], refs[n_in:n_in + ns_in]
        outs = refs[n_in + ns_in:n_in + ns_in + n_out]
        s_outs = refs[n_in + ns_in + n_out:n_in + ns_in + n_out + ns_out]
        rest = refs[n_in + ns_in + n_out + ns_out:]
        sems = rest[-2:]
        step = 0
        for axis, extent in enumerate(grid):
            step = step * extent + pl.program_id(axis)

        @pl.when(step == 0)
        def _():
            side.start(s_ins, s_outs, *sems)

        if side.mid is not None:
            @pl.when(step == mid_step)
            def _():
                side.mid(s_ins, s_outs, *sems)

        body(*ins, *outs, *rest[:-2])

        @pl.when(step == n_steps - 1)
        def _():
            side.end(s_ins, s_outs, *sems)

    return wrapped


def _hosted_call(body, name, grid, in_specs, out_specs, out_shape, args, side, semantics=("parallel", "arbitrary"),
                 scratch=()):
    n_in, n_out = len(in_specs), len(out_specs)
    kern = _host(body, n_in, n_out, side, grid)
    if side is None:
        return pl.pallas_call(kern, name=name, grid=grid, in_specs=in_specs, out_specs=out_specs, out_shape=out_shape,
                              scratch_shapes=list(scratch), compiler_params=_params(semantics))(*args), []
    res = pl.pallas_call(
        kern, name=name, grid=grid, in_specs=in_specs + [HBM] * len(side.ins), out_specs=out_specs + [HBM] * len(side.outs),
        out_shape=list(out_shape) + [pltpu.HBM(s.shape, s.dtype) for s in side.outs],
        scratch_shapes=list(scratch) + [pltpu.SemaphoreType.DMA((side.n_sems,)), pltpu.SemaphoreType.DMA((side.n_sems,))],
        input_output_aliases={n_in + i: n_out + o for i, o in side.aliases.items()},
        compiler_params=_params(("arbitrary",) * len(grid)),
    )(*args, *side.ins)
    return res[:n_out], res[n_out:]


def _pair_swap(halves):
    n = len(halves)

    def body(*refs):
        src, dst = refs[:n], refs[n:2 * n]
        ssem, rsem = refs[2 * n:]
        x, y, c, _ = _place()
        cps = []
        for w in range(n):
            cp = _remote(src[w], dst[w], ssem.at[w], rsem.at[w], (x, y, 1 - c))
            cp.start()
            cps.append(cp)
        for cp in cps:
            cp.wait()

    outs = [jax.ShapeDtypeStruct(h.shape, h.dtype) for h in halves]
    return _comm_call("rs_pair_swap", body, halves, outs, n)


def _gather_small(vec):
    def body(src, dst, ssem, rsem):
        x, y, c, _ = _place()
        me = 4 * x + 2 * y + c
        cps = []
        for r in range(1, 8):
            dev = (_flip(x, r & 4), _flip(y, r & 2), _flip(c, r & 1))
            cp = _remote(src, dst.at[me], ssem.at[r - 1], rsem.at[r - 1], dev)
            cp.start()
            cps.append(cp)
        for cp in cps:
            cp.wait()

    out = jax.ShapeDtypeStruct((8,) + vec.shape, vec.dtype)
    return _comm_call("gather_small", body, [vec], [out], 7)[0]


def _pair_add(name, g, theirs, core):
    n, half, b = theirs.shape
    tr = _tile(half, 256, 16)
    nt = half // tr

    def body(c_ref, g_ref, t_ref, o_ref):
        o_ref[...] = (g_ref[...].astype(F32) + t_ref[...].astype(F32)).astype(o_ref.dtype)

    same = pl.BlockSpec((None, tr, b), lambda j, i, c: (j, i, 0))
    grid_spec = pltpu.PrefetchScalarGridSpec(
        num_scalar_prefetch=1, grid=(n, nt),
        in_specs=[pl.BlockSpec((None, tr, b), lambda j, i, c: (j, c[0] * nt + i, 0)), same], out_specs=same)
    return pl.pallas_call(body, name=name, grid_spec=grid_spec, out_shape=jax.ShapeDtypeStruct(theirs.shape, BF16),
                          compiler_params=_params(("parallel", "parallel")))(core, g, theirs)


def _sum_slots(name, stacked, first=None):
    n, r, c = stacked.shape
    tr = _tile(r, 256, 8)

    def body(*refs):
        s_ref, o_ref = refs[-2], refs[-1]
        acc = refs[0][...].astype(F32) if first is not None else s_ref[0].astype(F32)
        for s in range(0 if first is not None else 1, n):
            acc = acc + s_ref[s].astype(F32)
        o_ref[...] = acc

    row_spec = pl.BlockSpec((tr, c), lambda i: (i, 0))
    in_specs = ([row_spec] if first is not None else []) + [pl.BlockSpec((n, tr, c), lambda i: (0, i, 0))]
    args = ([first] if first is not None else []) + [stacked]
    return pl.pallas_call(
        body, name=name, grid=(r // tr,), in_specs=in_specs, out_specs=row_spec,
        out_shape=jax.ShapeDtypeStruct((r, c), F32), compiler_params=_params(("parallel",)),
    )(*args)


def _adam_math(w, g, m, v):
    bc1, bc2 = 1.0 - ADAM_B1 ** ADAM_STEP, 1.0 - ADAM_B2 ** ADAM_STEP
    nm = ADAM_B1 * m + (1.0 - ADAM_B1) * g
    nv = ADAM_B2 * v + (1.0 - ADAM_B2) * (g * g)
    return -ADAM_LR * ((nm / bc1) / (jnp.sqrt(nv / bc2) + ADAM_EPS) + ADAM_WD * w), nm, nv


def _adamw_halves(name, w, g_mine, g_theirs, m, v, core):
    r, c = w.shape
    h = r // 2
    tr = _tile(h, 128, 8)
    nth = h // tr

    def body(c_ref, w_ref, gm_ref, gt_ref, m_ref, v_ref, g_ref, d_ref, nm_ref, nv_ref):
        g = jnp.where(pl.program_id(0) // nth == c_ref[0], gm_ref[...], gt_ref[...])
        g_ref[...] = g
        d_ref[...], nm_ref[...], nv_ref[...] = _adam_math(w_ref[...], g, m_ref[...], v_ref[...])

    full = pl.BlockSpec((tr, c), lambda i, cr: (i, 0))
    half = pl.BlockSpec((tr, c), lambda i, cr: (i % nth, 0))
    grid_spec = pltpu.PrefetchScalarGridSpec(num_scalar_prefetch=1, grid=(r // tr,),
                                             in_specs=[full, half, half, full, full], out_specs=[full] * 4)
    return pl.pallas_call(body, name=name, grid_spec=grid_spec, out_shape=[jax.ShapeDtypeStruct((r, c), F32)] * 4,
                          compiler_params=_params(("parallel",)))(core, w, g_mine, g_theirs, m, v)


def _adamw(name, w, g, m, v):
    r, c = w.shape
    by_cols = r % 8 != 0 and c % LANES == 0
    tr, tc = (r, _tile(c, 256, LANES)) if by_cols else (_tile(r, 256, 8), c)

    def body(w_ref, g_ref, m_ref, v_ref, d_ref, nm_ref, nv_ref):
        d_ref[...], nm_ref[...], nv_ref[...] = _adam_math(w_ref[...], g_ref[...], m_ref[...], v_ref[...])

    spec = pl.BlockSpec((tr, tc), (lambda i: (0, i)) if by_cols else (lambda i: (i, 0)))
    return pl.pallas_call(
        body, name=name, grid=(c // tc if by_cols else r // tr,), in_specs=[spec] * 4, out_specs=[spec] * 3,
        out_shape=[jax.ShapeDtypeStruct((r, c), F32)] * 3, compiler_params=_params(("parallel",)),
    )(w, g, m, v)


def _pad_cols(a, cols):
    return jnp.pad(a, ((0, 0), (0, cols - a.shape[1])))


def _rot_cols(w):
    h = w.shape[-1] // 2
    return jnp.concatenate([-w[..., h:], w[..., :h]], axis=-1)


def _unrot_cols(d):
    h = d.shape[-1] // 2
    return jnp.concatenate([d[..., h:], -d[..., :h]], axis=-1)


def _logical(g):
    return jnp.transpose(g, (1, 0, 2)).reshape(g.shape[1], N_CHIPS * g.shape[2])


def _chunks(a, n):
    return jnp.transpose(a.reshape(a.shape[0], N_CHIPS, n), (1, 0, 2))


def kernel(x, positions, pre_mix_norm, w_in, q_a_norm, w_uq, kv_a_norm, w_ukv, b_forget, b_gate, w_branch_mla, w_branch_fox, w_out, post_mix_norm, pre_ffn_norm, w_up, conv_w, conv_b, w_down, post_ffn_norm, loss_target, m_pre_mix_norm, m_w_in, m_q_a_norm, m_w_uq, m_kv_a_norm, m_w_ukv, m_b_forget, m_b_gate, m_w_branch_mla, m_w_branch_fox, m_w_out, m_post_mix_norm, m_pre_ffn_norm, m_w_up, m_conv_w, m_conv_b, m_w_down, m_post_ffn_norm, v_pre_mix_norm, v_w_in, v_q_a_norm, v_w_uq, v_kv_a_norm, v_w_ukv, v_b_forget, v_b_gate, v_w_branch_mla, v_w_branch_fox, v_w_out, v_post_mix_norm, v_pre_ffn_norm, v_w_up, v_conv_w, v_conv_b, v_w_down, v_post_ffn_norm):
    n_seq, seq, d = x.shape
    n_tok = n_seq * seq
    d_in = N_CHIPS * w_in.shape[1]
    two_f = N_CHIPS * w_up.shape[1]
    ff_dim = two_f // 2
    assert d_in == QL + KVL + ROPE + 3 * HEADS * FDIM + HEADS + 2 * d
    n_in_shard = w_in.shape[1]
    in_pad = -(-n_in_shard // LANES) * LANES
    hd = HEADS * LANES
    xc, yc, cc = lax.axis_index("x"), lax.axis_index("y"), lax.axis_index("c")
    chip = 2 * xc + yc
    t_attn = _tile(seq, 512, 128)

    shards = [_pad_cols(w_in, in_pad).astype(BF16), w_uq.astype(BF16), w_ukv.astype(BF16), w_branch_mla.astype(BF16),
              w_branch_fox.astype(BF16), w_out.astype(BF16), w_up.astype(BF16), w_down.astype(BF16)]
    cw8 = jnp.pad(conv_w, ((0, 5), (0, 0)))
    put_own = lambda g, s: lax.dynamic_update_slice(g, s[None], (chip, 0, 0))
    gathered, (g_cw,) = _all_gather_weights(shards[:3], [cw8])
    g_in, g_uq, g_ukv = [put_own(g, s) for g, s in zip(gathered, shards[:3])]
    g_cw = put_own(g_cw, cw8)
    n_attn_steps = n_seq * (seq // t_attn)
    up_rows = shards[6].shape[0] // 2
    side_proj = _gather_side([shards[3], shards[4], shards[5]], [], mid_step=0.9)
    side_mla = _gather_side([shards[6][:up_rows]], [], mid_step=max(n_attn_steps - 2, 0), into=[(None, 2 * up_rows, 0)])
    side_ffn = _gather_side([shards[7]], [], mid_step=0.7)

    o_q, o_kv, o_kpe = 0, QL, QL + KVL
    o_f = o_kpe + ROPE
    o_fl = o_f + 3 * hd
    o_g = o_fl + HEADS

    def chip_cols(lo, hi):
        out = []
        while lo < hi:
            j = lo // n_in_shard
            end = min(hi, (j + 1) * n_in_shard)
            out.append((j, lo - j * n_in_shard, end - j * n_in_shard))
            lo = end
        return out

    take = lambda lo, hi: [g_in[j, :, a:b] for j, a, b in chip_cols(lo, hi)]
    w_kpe = jnp.concatenate(take(o_kpe, o_f), axis=1)
    zeros = lambda n: jnp.zeros((d, n), BF16)
    win_p = jnp.concatenate(
        take(o_g, d_in) + take(o_q, o_kpe) + [w_kpe, zeros(LANES - ROPE), _rot_cols(w_kpe)] + take(o_fl, o_g)
        + [zeros(LANES - ROPE - HEADS)] + take(o_f, o_fl), axis=1)
    n_p = win_p.shape[1]
    cb_gm, cb_gf = 0, 1
    c_lat = 2 * d
    c_kx, c_kr = c_lat + QL + KVL, c_lat + QL + KVL + LANES
    n_pa = c_kr + LANES
    assert n_p == n_pa + 3 * hd

    uq3 = _logical(g_uq).reshape(QL, HEADS, NOPE + ROPE)
    pe = uq3[:, :, NOPE:]
    pad_pe = lambda a: jnp.pad(a, ((0, 0), (0, 0), (0, LANES - ROPE))).reshape(QL, hd)
    wuq_p = jnp.concatenate([uq3[:, :, :NOPE].reshape(QL, hd), pad_pe(pe), pad_pe(_rot_cols(pe))], axis=1)
    ukv3 = _logical(g_ukv).reshape(KVL, HEADS, NOPE + VDIM)
    wukv_p = jnp.concatenate([ukv3[:, :, :NOPE].reshape(KVL, hd), ukv3[:, :, NOPE:].reshape(KVL, hd)], axis=1)

    n_bm, n_up = w_branch_mla.shape[1], w_up.shape[1]
    l_bm, l_up = _Chunked(n_bm), _Chunked(n_up)
    wt = n_up // 2
    n_ut = two_f // wt
    il = lambda cblk: jnp.where(cblk < n_ut // 2, 2 * cblk, 2 * (cblk - n_ut // 2) + 1)
    l_il = _Plain(il)
    to_il = lambda a: a.reshape(a.shape[0], 2, n_ut // 2, wt).transpose(0, 2, 1, 3).reshape(a.shape[0], two_f)
    from_il = lambda a: a.reshape(a.shape[0], n_ut // 2, 2, wt).transpose(0, 2, 1, 3).reshape(a.shape[0], two_f)

    inv_freq = 1.0 / (ROPE_THETA ** (jnp.arange(0, ROPE, 2, dtype=F32) / ROPE))
    ang = positions.astype(F32).reshape(n_tok, 1) * inv_freq
    cos, sin = jnp.cos(ang), jnp.sin(ang)
    cs = _pad_cols(jnp.concatenate([cos, cos], axis=1), LANES)
    sn = _pad_cols(jnp.concatenate([sin, sin], axis=1), LANES)

    row = lambda v: v.reshape(1, -1)
    x2 = x.reshape(n_tok, d)
    tgt = loss_target.reshape(n_tok, d)

    (h,) = _rows("rms_pre_mix", lambda r, v: ([r[0] * _rstd(r[0]) * v[0]], []),
                 [(x2, d, 0)], [row(pre_mix_norm)], [(d, BF16)], [], n_tok)
    proj, got = _mm("proj_in", "nn", h, win_p, n_tok, n_pa, d, side=side_proj)
    g_bm, g_bf, g_out = [put_own(g, s) for g, s in zip(got, side_proj.ins)]
    w_out_full = g_out.reshape(d, d)
    tn_f = _tile(3 * hd, 1024, 128)
    assert n_pa % tn_f == 0
    proj_f = _mm("proj_in_fox", "nn", h, win_p, n_tok, 3 * hd, d, tn=tn_f, lb=_Plain(lambda cblk: cblk + n_pa // tn_f),
                 out_dtype=BF16)

    bf_vec = jnp.pad(row(b_forget), ((0, 0), (ROPE, LANES - ROPE - HEADS)))

    def lat_fwd(r, v):
        ql, kvl = r[0], r[1]
        return [ql * _rstd(ql) * v[0], kvl * _rstd(kvl) * v[1], r[2] * r[4] + r[3] * r[5]], []

    qn, kvn, rk = _rows("latent_norms", lat_fwd,
                        [(proj, QL, c_lat // QL), (proj, KVL, (c_lat + QL) // KVL), (proj, LANES, c_kx // LANES),
                         (proj, LANES, c_kr // LANES), (cs, LANES, 0), (sn, LANES, 0)],
                        [row(q_a_norm), row(kv_a_norm)], [(QL, BF16), (KVL, BF16), (LANES, BF16)], [], n_tok)
    q_p = _mm("q_up", "nn", qn, wuq_p, n_tok, 3 * hd, QL)
    kv_p = _mm("kv_up", "nn", kvn, wukv_p, n_tok, 2 * hd, KVL, out_dtype=BF16)

    def rope_q(r, v):
        c8, s8 = jnp.tile(r[3], (1, HEADS)), jnp.tile(r[4], (1, HEADS))
        return [r[0], r[1] * c8 + r[2] * s8], []

    q_nope, rq = _rows("rope_q", rope_q, [(q_p, hd, 0), (q_p, hd, 1), (q_p, hd, 2), (cs, LANES, 0), (sn, LANES, 0)], [],
                       [(hd, BF16), (hd, BF16)], [], n_tok)

    mla_q = [(q_nope, 0, False), (rq, 0, False)]
    mla_k = [(kv_p, 0, False), (rk, 0, True)]
    mla_v = (kv_p, 1)
    mla_scale = (NOPE + ROPE) ** -0.5
    (o_mla, lse_mla), got = _attn_fwd("mla_fwd", mla_q, mla_k, mla_v, None, CHUNK, mla_scale, n_seq, seq, t_attn, side=side_mla)
    side_fox = _gather_side([shards[6][up_rows:]], [], mid_step=max(n_attn_steps - 2, 0), into=[(got[0], 2 * up_rows, up_rows)])

    c_run = _seq_cumsum("forget_cumsum", proj, c_kr // LANES, n_seq, seq, False,
                        pre=lambda z, b: _log_sigmoid(z + b), vec=bf_vec)
    nb_attn = seq // t_attn
    c_rowf = jnp.transpose(c_run[:, ROPE:ROPE + HEADS].reshape(n_seq, nb_attn, t_attn, HEADS), (0, 1, 3, 2))
    fox_q, fox_k, fox_v = [(proj_f, 0, False)], [(proj_f, 1, False)], (proj_f, 2)
    fox_scale = FDIM ** -0.5
    fox_bias = (c_run, c_rowf)
    (o_fox, lse_fox), got = _attn_fwd("fox_fwd", fox_q, fox_k, fox_v, fox_bias, 1, fox_scale, n_seq, seq, t_attn, side=side_fox)
    g_up = put_own(got[0], shards[6])

    pm = _mm("branch_mla", "nn", o_mla, g_bm, n_tok, d, hd, lb=l_bm, tn=n_bm)
    pf = _mm("branch_fox", "nn", o_fox, g_bf, n_tok, d, hd, lb=l_bm, tn=n_bm)
    bg = row(b_gate)

    def merge(r, v):
        return [_sigmoid(r[0] + v[0]) * r[2] + _sigmoid(r[1] + v[1]) * r[3]], []

    (merged,) = _rows("gate_merge", merge, [(proj, d, cb_gm), (proj, d, cb_gf), (pm, d, 0), (pf, d, 0)],
                      [bg[:, :d], bg[:, d:]], [(d, BF16)], [], n_tok)
    y1 = _mm("mix_out", "nn", merged, w_out_full, n_tok, d, d)

    def resid_norm(r, v):
        x1v = r[0] + r[1] * _rstd(r[1]) * v[0]
        return [x1v, x1v * _rstd(x1v) * v[1]], []

    x1, h2 = _rows("post_mix_pre_ffn", resid_norm, [(x2, d, 0), (y1, d, 0)], [row(post_mix_norm), row(pre_ffn_norm)],
                   [(d, F32), (d, BF16)], [], n_tok)

    u_il, got = _mm("ffn_up", "nn", h2, g_up, n_tok, two_f, d, lb=l_up, lo=l_il, tn=wt, out_dtype=BF16, side=side_ffn)
    w_down_full = put_own(got[0], side_ffn.ins[0]).reshape(ff_dim, d)
    cw_il = to_il(_logical(g_cw)[:3])
    cb_il = to_il(row(conv_b))
    act = _conv_glu_fwd(u_il, cw_il, cb_il, n_seq, seq, wt)
    ff = _mm("ffn_down", "nn", act, w_down_full, n_tok, d, ff_dim)

    def final(r, v):
        x1v, ffv, tg = r
        diff = x1v + ffv * _rstd(ffv) * v[0] - tg
        dx2v = diff / d
        dffv, dg4 = _rms_bwd(ffv, v[0], dx2v)
        sq = jnp.sum(jnp.sum(diff * diff, axis=1, keepdims=True), axis=0, keepdims=True)
        return [dx2v, dffv], [dg4, jnp.broadcast_to(sq, (1, LANES))]

    dx2, dff, dg_post_ffn, sq_sum = _rows("loss_post_ffn_bwd", final, [(x1, d, 0), (ff, d, 0), (tgt, d, 0)],
                                          [row(post_ffn_norm)], [(d, F32), (d, BF16)], [(1, d), (1, LANES)], n_tok)
    rs_parts, rs_landed = {}, {}
    core = jnp.reshape(cc, (1,)).astype(jnp.int32)

    def pair_reduce(tag, names, grads):
        theirs = _pair_split("rs_pair_split_" + tag, grads)
        for nm, g, b in zip(names, grads, theirs):
            rs_parts[nm] = _pair_add("rs_pair_add_" + nm, g, b, core)

    dact = _mm("ffn_down_dx", "nt", dff, w_down_full, n_tok, ff_dim, d, tn=wt, out_dtype=BF16)
    gw_down = _mm("ffn_down_dw", "tn", act, dff, ff_dim, d, n_tok, tm=wt, out_dtype=BF16)
    pair_reduce("down", ["w_down"], [gw_down.reshape(N_CHIPS, ff_dim // N_CHIPS, d)])
    d_il, conv_acc = _conv_glu_bwd_pre(u_il, dact, cw_il, cb_il, n_seq, seq, wt)
    du_il = _conv_bwd_input(d_il, cw_il, n_seq, seq, wt)
    gw_up, got = _mm("ffn_up_dw", "tn", h2, du_il, d, two_f, n_tok, lb=l_il, lo=l_up, tn=wt, out_dtype=BF16,
                     side=_scatter_side([rs_parts["w_down"]]))
    rs_landed["w_down"] = got[0]
    pair_reduce("up", ["w_up"], [gw_up])
    dh2, got = _mm("ffn_up_dx", "nt", du_il, g_up, n_tok, d, two_f, la=l_il, lb=l_up, tk=wt,
                   side=_scatter_side([rs_parts["w_up"]]))
    rs_landed["w_up"] = got[0]

    def mid_bwd(r, v):
        x1v, y1v, dx2v, dh2v = r
        d3, dg3 = _rms_bwd(x1v, v[1], dh2v)
        dx1v = dx2v + d3
        dy1v, dg2 = _rms_bwd(y1v, v[0], dx1v)
        return [dx1v, dy1v], [dg3, dg2]

    dx1, dy1, dg_pre_ffn, dg_post_mix = _rows(
        "pre_ffn_post_mix_bwd", mid_bwd, [(x1, d, 0), (y1, d, 0), (dx2, d, 0), (dh2, d, 0)],
        [row(post_mix_norm), row(pre_ffn_norm)], [(d, F32), (d, BF16)], [(1, d), (1, d)], n_tok)
    dmerged = _mm("mix_out_dx", "nt", dy1, w_out_full, n_tok, d, d)
    gw_out = _mm("mix_out_dw", "tn", merged, dy1, d, d, n_tok, out_dtype=BF16)

    def gate_bwd(r, v):
        zm, zf, pmv, pfv, dm = r
        gm, gf = _sigmoid(zm + v[0]), _sigmoid(zf + v[1])
        dzm, dzf = dm * pmv * gm * (1.0 - gm), dm * pfv * gf * (1.0 - gf)
        return [dm * gm, dm * gf, jnp.concatenate([dzm, dzf], axis=1)], [_colsum(dzm), _colsum(dzf)]

    dpm, dpf, dz, dbg_m, dbg_f = _rows(
        "gate_merge_bwd", gate_bwd, [(proj, d, cb_gm), (proj, d, cb_gf), (pm, d, 0), (pf, d, 0), (dmerged, d, 0)],
        [bg[:, :d], bg[:, d:]], [(d, BF16), (d, BF16), (2 * d, BF16)], [(1, d), (1, d)], n_tok)
    tk_b = min(n_bm, 512)
    do_mla = _mm("branch_mla_dx", "nt", dpm, g_bm, n_tok, hd, d, lb=l_bm, tk=tk_b)
    do_fox = _mm("branch_fox_dx", "nt", dpf, g_bf, n_tok, hd, d, lb=l_bm, tk=tk_b)
    gw_bm = _mm("branch_mla_dw", "tn", o_mla, dpm, hd, d, n_tok, lo=l_bm, tn=n_bm, out_dtype=BF16)
    gw_bf = _mm("branch_fox_dw", "tn", o_fox, dpf, hd, d, n_tok, lo=l_bm, tn=n_bm, out_dtype=BF16)

    pair_reduce("mix", ["w_out", "w_branch_mla", "w_branch_fox"], [gw_out.reshape(N_CHIPS, d // N_CHIPS, d), gw_bm, gw_bf])
    (dq_nope, drq, delta_mla, dob_mla), got = _attn_bwd_dq(
        "mla_bwd_dq", mla_q, mla_k, mla_v, o_mla, do_mla, lse_mla, None, CHUNK, mla_scale, n_seq, seq, t_attn,
        side=_scatter_side([rs_parts[nm] for nm in ("w_out", "w_branch_mla", "w_branch_fox")]))
    rs_landed.update(zip(("w_out", "w_branch_mla", "w_branch_fox"), got))
    (dk_nope, drk, dv_mla), _ = _attn_bwd_dkv(
        "mla_bwd_dkv", mla_q, mla_k, mla_v, dob_mla, lse_mla, delta_mla, None, CHUNK, mla_scale, n_seq, seq, t_attn)
    (dfq, delta_fox, dob_fox, dc_q), _ = _attn_bwd_dq("fox_bwd_dq", fox_q, fox_k, fox_v, o_fox, do_fox, lse_fox, fox_bias, 1,
                                                      fox_scale, n_seq, seq, t_attn, grad_dtype=BF16)
    (dfk, dfv, dc_k), _ = _attn_bwd_dkv("fox_bwd_dkv", fox_q, fox_k, fox_v, dob_fox, lse_fox, delta_fox, fox_bias, 1, fox_scale,
                                        n_seq, seq, t_attn, grad_dtype=BF16)
    dc_k8 = jnp.transpose(dc_k, (0, 1, 3, 2)).reshape(n_tok, HEADS)
    dc128 = dc_q + jnp.pad(dc_k8, ((0, 0), (ROPE, LANES - ROPE - HEADS)))
    dlogf = _seq_cumsum("forget_cumsum_bwd", dc128, 0, n_seq, seq, True)

    def mla_pack(r, v):
        dqn_v, drq_v, dkn_v, dv_v, drk_v, c1, s1 = r
        c8, s8 = jnp.tile(c1, (1, HEADS)), jnp.tile(s1, (1, HEADS))
        return [jnp.concatenate([dqn_v, drq_v * c8, drq_v * s8], axis=1), jnp.concatenate([dkn_v, dv_v], axis=1),
                drk_v * c1, drk_v * s1], []

    dq_p, dkv_p, dkx, dkr = _rows(
        "mla_rope_bwd", mla_pack,
        [(dq_nope, hd, 0), (drq, hd, 0), (dk_nope, hd, 0), (dv_mla, hd, 0), (drk, LANES, 0), (cs, LANES, 0), (sn, LANES, 0)],
        [], [(3 * hd, BF16), (2 * hd, BF16), (LANES, F32), (LANES, F32)], [], n_tok)
    dqn = _mm("q_up_dx", "nt", dq_p, wuq_p, n_tok, QL, 3 * hd)
    gw_uq_p = _mm("q_up_dw", "tn", qn, dq_p, QL, 3 * hd, n_tok, out_dtype=BF16)
    dkvn = _mm("kv_up_dx", "nt", dkv_p, wukv_p, n_tok, KVL, 2 * hd)
    gw_ukv_p = _mm("kv_up_dw", "tn", kvn, dkv_p, KVL, 2 * hd, n_tok, out_dtype=BF16)

    def lat_bwd(r, v):
        ql, kvl, dqn_v, dkvn_v, dkx_v, dkr_v, zblk, dlf = r
        dql, dgq = _rms_bwd(ql, v[0], dqn_v)
        dkvl, dgkv = _rms_bwd(kvl, v[1], dkvn_v)
        dfl = dlf * _sigmoid(-(zblk + v[2]))
        return [jnp.concatenate([dql, dkvl, dkx_v, dkr_v + dfl], axis=1)], [dgq, dgkv, _colsum(dfl)]

    dlat, dg_q, dg_kv, dbf = _rows(
        "latent_bwd", lat_bwd,
        [(proj, QL, c_lat // QL), (proj, KVL, (c_lat + QL) // KVL), (dqn, QL, 0), (dkvn, KVL, 0), (dkx, LANES, 0),
         (dkr, LANES, 0), (proj, LANES, c_kr // LANES), (dlogf, LANES, 0)],
        [row(q_a_norm), row(kv_a_norm), bf_vec], [(QL + KVL + 2 * LANES, BF16)], [(1, QL), (1, KVL), (1, LANES)], n_tok)
    dproj = [dz, dlat, dfq, dfk, dfv]
    gw_in_p = _mm_parts("proj_in_dw", "tn", h, dproj, d, n_p, n_tok, tk=1024, out_dtype=BF16)

    f32 = lambda a: a.astype(F32)
    kr_blk = gw_in_p[:, c_kr:c_kr + LANES]
    d_kpe = (f32(gw_in_p[:, c_kx:c_kx + ROPE]) + _unrot_cols(f32(kr_blk[:, :ROPE]))).astype(BF16)
    in_pieces = [(o_q, gw_in_p, c_lat, QL + KVL), (o_kpe, d_kpe, 0, ROPE), (o_f, gw_in_p, n_pa, 3 * hd),
                 (o_fl, kr_blk, ROPE, HEADS), (o_g, gw_in_p, 0, 2 * d)]
    gc_in = []
    for j in range(N_CHIPS):
        lo, hi, cols = j * n_in_shard, (j + 1) * n_in_shard, []
        for first, arr, at, width in in_pieces:
            a, b = max(lo, first), min(hi, first + width)
            if a < b:
                cols.append(arr[:, at + a - first:at + b - first])
        cols.append(jnp.zeros((d, in_pad - n_in_shard), BF16))
        gc_in.append(jnp.concatenate(cols, axis=1))
    gc_in = jnp.stack(gc_in)
    uq_parts = [gw_uq_p[:, i * hd:(i + 1) * hd].reshape(QL, HEADS, LANES) for i in range(3)]
    d_pe = (f32(uq_parts[1][:, :, :ROPE]) + _unrot_cols(f32(uq_parts[2][:, :, :ROPE]))).astype(BF16)
    gc_uq = _chunks(jnp.concatenate([uq_parts[0], d_pe], axis=2).reshape(QL, HEADS * (NOPE + ROPE)), w_uq.shape[1])
    gc_ukv = _chunks(jnp.concatenate([gw_ukv_p[:, :hd].reshape(KVL, HEADS, NOPE), gw_ukv_p[:, hd:].reshape(KVL, HEADS, VDIM)],
                                     axis=2).reshape(KVL, HEADS * (NOPE + VDIM)), w_ukv.shape[1])
    grads = [gc_in, gc_uq, gc_ukv]

    late = ["w_in", "w_uq", "w_ukv"]
    pair_reduce("late", late, grads)
    dh, got = _mm_parts("proj_in_dx", "nt", dproj, win_p, n_tok, d, n_p, side=_scatter_side([rs_parts[nm] for nm in late]))
    rs_landed.update(zip(late, got))

    def first_bwd(r, v):
        dxa, dg1 = _rms_bwd(r[0], v[0], r[1])
        return [r[2] + dxa], [dg1]

    grad_x, dg_pre_mix = _rows("pre_mix_bwd", first_bwd, [(x2, d, 0), (dh, d, 0), (dx1, d, 0)], [row(pre_mix_norm)],
                               [(d, F32)], [(1, d)], n_tok)
    big = list(rs_parts)
    halves = [_sum_slots("rs_chip_sum_" + nm, rs_landed[nm],
                         first=lax.dynamic_index_in_dim(rs_parts[nm], chip, 0, keepdims=False)) for nm in big]
    other = _pair_swap(halves)
    g_halves = dict(zip(big, zip(halves, other)))

    conv_acc_l = from_il(conv_acc)
    pieces = [dg_pre_mix, dg_q, dg_kv, dbf, dbg_m, dbg_f, dg_post_mix, dg_pre_ffn, conv_acc_l[3:4], dg_post_ffn,
              conv_acc_l[0:1], conv_acc_l[1:2], conv_acc_l[2:3], sq_sum]
    sizes = [p.shape[1] for p in pieces]
    flat = jnp.concatenate(pieces, axis=1)
    n_rows = -(-flat.shape[1] // (8 * LANES)) * 8
    flat = _pad_cols(flat, n_rows * LANES).reshape(n_rows, LANES)
    slots = lax.dynamic_update_slice(_gather_small(flat), flat[None], (2 * chip + cc, 0, 0))
    total = _sum_slots("small_sum", slots).reshape(1, n_rows * LANES)
    offs = [sum(sizes[:i]) for i in range(len(sizes))]
    tot = [total[0, o:o + s] for o, s in zip(offs, sizes)]
    loss = 0.5 * tot[13][0] / d
    g_small = {"pre_mix_norm": tot[0], "q_a_norm": tot[1], "kv_a_norm": tot[2], "b_forget": tot[3][ROPE:ROPE + HEADS],
               "b_gate": jnp.concatenate([tot[4], tot[5]]), "post_mix_norm": tot[6], "pre_ffn_norm": tot[7],
               "conv_b": tot[8], "post_ffn_norm": tot[9]}
    gcw_full = jnp.stack([tot[10], tot[11], tot[12]])
    g_conv_w = lax.dynamic_slice(gcw_full, (0, chip * n_up), (3, n_up))

    given = dict(pre_mix_norm=(pre_mix_norm, m_pre_mix_norm, v_pre_mix_norm), w_in=(w_in, m_w_in, v_w_in),
                 q_a_norm=(q_a_norm, m_q_a_norm, v_q_a_norm), w_uq=(w_uq, m_w_uq, v_w_uq),
                 kv_a_norm=(kv_a_norm, m_kv_a_norm, v_kv_a_norm), w_ukv=(w_ukv, m_w_ukv, v_w_ukv),
                 b_forget=(b_forget, m_b_forget, v_b_forget), b_gate=(b_gate, m_b_gate, v_b_gate),
                 w_branch_mla=(w_branch_mla, m_w_branch_mla, v_w_branch_mla),
                 w_branch_fox=(w_branch_fox, m_w_branch_fox, v_w_branch_fox), w_out=(w_out, m_w_out, v_w_out),
                 post_mix_norm=(post_mix_norm, m_post_mix_norm, v_post_mix_norm),
                 pre_ffn_norm=(pre_ffn_norm, m_pre_ffn_norm, v_pre_ffn_norm), w_up=(w_up, m_w_up, v_w_up),
                 conv_w=(conv_w, m_conv_w, v_conv_w), conv_b=(conv_b, m_conv_b, v_conv_b),
                 w_down=(w_down, m_w_down, v_w_down), post_ffn_norm=(post_ffn_norm, m_post_ffn_norm, v_post_ffn_norm))
    order = list(given)
    grad, delta, new_m, new_v = {}, {}, {}, {}
    for nm in big:
        mine, theirs = g_halves[nm]
        if nm == "w_in":
            full = jnp.concatenate([jnp.where(cc == 0, mine, theirs), jnp.where(cc == 0, theirs, mine)], axis=0)
            grad[nm] = full[:, :n_in_shard]
            tr_out = _adamw("adamw_" + nm, *[jnp.transpose(a) for a in (given[nm][0], grad[nm], given[nm][1], given[nm][2])])
            delta[nm], new_m[nm], new_v[nm] = [jnp.transpose(a) for a in tr_out]
            continue
        grad[nm], delta[nm], new_m[nm], new_v[nm] = _adamw_halves("adamw_" + nm, given[nm][0], mine, theirs, given[nm][1],
                                                                  given[nm][2], core)
    grad["conv_w"] = g_conv_w
    delta["conv_w"], new_m["conv_w"], new_v["conv_w"] = _adamw("adamw_conv_w", conv_w, g_conv_w, m_conv_w, v_conv_w)
    small = list(g_small)
    padded = [-(-g_small[nm].shape[0] // LANES) * LANES for nm in small]
    s_rows = -(-sum(padded) // (8 * LANES)) * 8

    def pack(vals):
        cat = jnp.concatenate([jnp.pad(a, (0, p - a.shape[0])) for a, p in zip(vals, padded)])
        return jnp.pad(cat, (0, s_rows * LANES - cat.shape[0])).reshape(s_rows, LANES)

    packed = _adamw("adamw_small", pack([given[nm][0] for nm in small]), pack([g_small[nm] for nm in small]),
                    pack([given[nm][1] for nm in small]), pack([given[nm][2] for nm in small]))
    s_offs = [sum(padded[:i]) for i in range(len(small))]
    for nm, o in zip(small, s_offs):
        n_el = g_small[nm].shape[0]
        grad[nm] = g_small[nm]
        delta[nm], new_m[nm], new_v[nm] = [p.reshape(-1)[o:o + n_el] for p in packed]
    return (loss, grad_x.reshape(n_seq, seq, d), *[grad[nm] for nm in order], *[delta[nm] for nm in order],
            *[new_m[nm] for nm in order], *[new_v[nm] for nm in order])
```

```python
import functools
import math

import jax
import jax.numpy as jnp
from jax import lax
from jax.experimental import pallas as pl
from jax.experimental.pallas import tpu as pltpu

F32, BF16 = jnp.float32, jnp.bfloat16
MESH = pl.DeviceIdType.MESH

HEADS = 8
NOPE, ROPE, VDIM = 128, 64, 128
QL, KVL = 512, 256
FDIM = 128
CHUNK = 64
ROPE_THETA = 10000.0
EPS = 1e-6
NEG_INF = -1e30
ADAM_LR, ADAM_B1, ADAM_B2, ADAM_EPS, ADAM_WD, ADAM_STEP = 0.001, 0.9, 0.999, 1e-08, 0.01, 10

VMEM_LIMIT_BYTES = 52 * 1024 * 1024
LANES = 128
N_CHIPS = 4


def _params(sem):
    return pltpu.CompilerParams(dimension_semantics=sem, vmem_limit_bytes=VMEM_LIMIT_BYTES)


def _tile(n, target, mult):
    if n <= target:
        return n
    t = (target // mult) * mult
    while t >= mult:
        if n % t == 0:
            return t
        t -= mult
    raise ValueError(f"no tile for {n} (target {target}, multiple of {mult})")


class _Plain:
    def __init__(self, perm=None):
        self.perm = perm

    def spec(self, tr, tc, rc):
        perm = self.perm

        def imap(i, j, k):
            r, c = rc(i, j, k)
            return (r, perm(c) if perm is not None else c)

        return pl.BlockSpec((tr, tc), imap)

    def shape(self, rows, cols):
        return (rows, cols)


class _Chunked:
    def __init__(self, n):
        self.n = n

    def spec(self, tr, tc, rc):
        assert self.n % tc == 0, (self.n, tc)
        per = self.n // tc

        def imap(i, j, k):
            r, c = rc(i, j, k)
            return (c // per, r, c % per)

        return pl.BlockSpec((None, tr, tc), imap)

    def shape(self, rows, cols):
        assert cols == N_CHIPS * self.n
        return (N_CHIPS, rows, self.n)


_DIMS = {"nn": (((1,), (0,)), ((), ())), "nt": (((1,), (1,)), ((), ())), "tn": (((0,), (0,)), ((), ()))}


def _mm_single(name, mode, a, b, m, n, k, tm, tn, la, lb, lo, out_dtype, side):
    if mode == "nn":
        a_spec = la.spec(tm, k, lambda i, j, kk: (i, 0))
        b_spec = lb.spec(k, tn, lambda i, j, kk: (0, j))
    elif mode == "nt":
        a_spec = la.spec(tm, k, lambda i, j, kk: (i, 0))
        b_spec = lb.spec(tn, k, lambda i, j, kk: (j, 0))
    else:
        a_spec = la.spec(k, tm, lambda i, j, kk: (0, i))
        b_spec = lb.spec(k, tn, lambda i, j, kk: (0, j))
    o_spec = lo.spec(tm, tn, lambda i, j, kk: (i, j))
    dims = _DIMS[mode]

    def body(a_ref, b_ref, o_ref):
        o_ref[...] = lax.dot_general(a_ref[...].astype(BF16), b_ref[...].astype(BF16), dims,
                                     preferred_element_type=F32).astype(o_ref.dtype)

    (out,), got = _hosted_call(body, name, (m // tm, n // tn, 1), [a_spec, b_spec], [o_spec],
                               [jax.ShapeDtypeStruct(lo.shape(m, n), out_dtype)], (a, b), side,
                               semantics=("parallel", "parallel", "arbitrary"))
    return out if side is None else (out, got)


def _mm(name, mode, a, b, m, n, k, *, tm=1024, tn=1024, tk=2048, la=None, lb=None, lo=None, out_dtype=F32, side=None):
    la, lb, lo = la or _Plain(), lb or _Plain(), lo or _Plain()
    tm, tn, tk = _tile(m, tm, 128), _tile(n, tn, 128), _tile(k, tk, 128)
    nk = k // tk
    if nk == 1:
        return _mm_single(name, mode, a, b, m, n, k, tm, tn, la, lb, lo, out_dtype, side)
    if mode == "nn":
        a_spec = la.spec(tm, tk, lambda i, j, kk: (i, kk))
        b_spec = lb.spec(tk, tn, lambda i, j, kk: (kk, j))
    elif mode == "nt":
        a_spec = la.spec(tm, tk, lambda i, j, kk: (i, kk))
        b_spec = lb.spec(tn, tk, lambda i, j, kk: (j, kk))
    else:
        a_spec = la.spec(tk, tm, lambda i, j, kk: (kk, i))
        b_spec = lb.spec(tk, tn, lambda i, j, kk: (kk, j))
    o_spec = lo.spec(tm, tn, lambda i, j, kk: (i, j))
    dims = _DIMS[mode]

    def body(a_ref, b_ref, o_ref, acc_ref):
        kk = pl.program_id(2)

        @pl.when(kk == 0)
        def _():
            acc_ref[...] = jnp.zeros_like(acc_ref)

        acc_ref[...] += lax.dot_general(a_ref[...].astype(BF16), b_ref[...].astype(BF16), dims,
                                        preferred_element_type=F32)

        @pl.when(kk == nk - 1)
        def _():
            o_ref[...] = acc_ref[...].astype(o_ref.dtype)

    (out,), got = _hosted_call(body, name, (m // tm, n // tn, nk), [a_spec, b_spec], [o_spec],
                               [jax.ShapeDtypeStruct(lo.shape(m, n), out_dtype)], (a, b), side,
                               semantics=("parallel", "parallel", "arbitrary"), scratch=[pltpu.VMEM((tm, tn), F32)])
    return out if side is None else (out, got)


def _mm_parts(name, mode, a, b, m, n, k, *, part=1024, tm=1024, tn=1024, tk=2048, out_dtype=F32, side=None):
    parts = b if mode == "tn" else a
    widths = [p.shape[1] for p in parts]
    assert all(w % part == 0 for w in widths) and sum(widths) == (n if mode == "tn" else k)
    offs = [sum(widths[:i]) // part for i in range(len(widths))]
    nblk = [w // part for w in widths]
    if mode == "tn":
        tn, tk = part, _tile(k, tk, 128)
    else:
        tk, tn = part, _tile(n, tn, 128)
    tm = _tile(m, tm, 128)
    nk = k // tk
    grid = (m // tm, n // tn, nk)
    np_ = len(parts)

    def inside(idx, p):
        return jnp.logical_and(idx >= offs[p], idx < offs[p] + nblk[p])

    def part_spec(p):
        if mode == "tn":
            def imap(i, j, kk):
                on = inside(j, p)
                return (jnp.where(on, kk, 0), jnp.clip(j - offs[p], 0, nblk[p] - 1))
            return pl.BlockSpec((tk, tn), imap)

        def imap(i, j, kk):
            return (i, jnp.clip(kk - offs[p], 0, nblk[p] - 1))
        return pl.BlockSpec((tm, tk), imap)

    if mode == "tn":
        in_specs = [pl.BlockSpec((tk, tm), lambda i, j, kk: (kk, i))] + [part_spec(p) for p in range(np_)]
        args = [a] + list(parts)
    else:
        in_specs = [part_spec(p) for p in range(np_)] + [pl.BlockSpec((tn, tk), lambda i, j, kk: (j, kk))]
        args = list(parts) + [b]
    dims = _DIMS[mode]

    def body(*refs):
        o_ref, acc_ref = refs[-2], refs[-1]
        j, kk = pl.program_id(1), pl.program_id(2)

        @pl.when(kk == 0)
        def _():
            acc_ref[...] = jnp.zeros_like(acc_ref)

        for p in range(np_):
            @pl.when(inside(j if mode == "tn" else kk, p))
            def _(p=p):
                lhs, rhs = (refs[0], refs[1 + p]) if mode == "tn" else (refs[p], refs[np_])
                acc_ref[...] += lax.dot_general(lhs[...].astype(BF16), rhs[...].astype(BF16), dims, preferred_element_type=F32)

        @pl.when(kk == nk - 1)
        def _():
            o_ref[...] = acc_ref[...].astype(o_ref.dtype)

    (out,), got = _hosted_call(body, name, grid, in_specs, [pl.BlockSpec((tm, tn), lambda i, j, kk: (i, j))],
                               [jax.ShapeDtypeStruct((m, n), out_dtype)], args, side,
                               semantics=("parallel", "parallel", "arbitrary"), scratch=[pltpu.VMEM((tm, tn), F32)])
    return out if side is None else (out, got)


def _rows(name, fn, rows_in, vecs_in, rows_out, accs_out, n_rows, tr=256):
    tr = _tile(n_rows, tr, 16)
    nr, nv, no = len(rows_in), len(vecs_in), len(rows_out)

    def body(*refs):
        ins, vecs = refs[:nr], refs[nr:nr + nv]
        outs, accs = refs[nr + nv:nr + nv + no], refs[nr + nv + no:]
        ro, ac = fn([r[...] for r in ins], [v[...] for v in vecs])
        for o_ref, val in zip(outs, ro):
            o_ref[...] = val.astype(o_ref.dtype)
        if accs:
            @pl.when(pl.program_id(0) == 0)
            def _():
                for a_ref in accs:
                    a_ref[...] = jnp.zeros_like(a_ref)

            for a_ref, val in zip(accs, ac):
                a_ref[...] += val

    in_specs = [pl.BlockSpec((tr, cols), functools.partial(lambda i, cb: (i, cb), cb=cb)) for _, cols, cb in rows_in]
    in_specs += [pl.BlockSpec(v.shape, lambda i: (0, 0)) for v in vecs_in]
    out_specs = [pl.BlockSpec((tr, cols), lambda i: (i, 0)) for cols, _ in rows_out]
    out_specs += [pl.BlockSpec((r, cols), lambda i: (0, 0)) for r, cols in accs_out]
    out_shape = [jax.ShapeDtypeStruct((n_rows, cols), dt) for cols, dt in rows_out]
    out_shape += [jax.ShapeDtypeStruct((r, cols), F32) for r, cols in accs_out]
    res = pl.pallas_call(
        body, name=name, grid=(n_rows // tr,), in_specs=in_specs, out_specs=out_specs, out_shape=out_shape,
        compiler_params=_params(("arbitrary",)),
    )(*[a for a, _, _ in rows_in], *vecs_in)
    return res


def _colsum(v):
    return jnp.sum(v, axis=0, keepdims=True)


def _rstd(x):
    return lax.rsqrt(jnp.mean(x * x, axis=-1, keepdims=True) + EPS)


def _rms_bwd(x, g, dy):
    r = _rstd(x)
    xh = x * r
    dxh = dy * g
    dx = r * (dxh - xh * jnp.mean(dxh * xh, axis=-1, keepdims=True))
    return dx, _colsum(dy * xh)


def _sigmoid(z):
    return 1.0 / (1.0 + jnp.exp(-z))


_GELU_K = math.sqrt(2.0 / math.pi)


def _gelu_parts(g):
    t = jnp.tanh(_GELU_K * (g + 0.044715 * g * g * g))
    gel = 0.5 * g * (1.0 + t)
    dgel = 0.5 * (1.0 + t) + 0.5 * g * (1.0 - t * t) * (_GELU_K * (1.0 + 3.0 * 0.044715 * g * g))
    return gel, dgel


def _diag_visible(t, unit):
    rows = lax.broadcasted_iota(jnp.int32, (t, t), 0)
    cols = lax.broadcasted_iota(jnp.int32, (t, t), 1)
    if unit > 1:
        sh = int(math.log2(unit))
        assert 1 << sh == unit and t % unit == 0
        rows, cols = jnp.right_shift(rows, sh), jnp.right_shift(cols, sh)
    return cols <= rows


def _lane_pick(tile, lane):
    idx = lax.broadcasted_iota(jnp.int32, tile.shape, 1)
    return jnp.sum(jnp.where(idx == lane, tile, 0.0), axis=1, keepdims=True)


def _lane_put(tile, lane, col):
    idx = lax.broadcasted_iota(jnp.int32, tile.shape, 1)
    return jnp.where(idx == lane, col, tile)


def _head_cat(refs, shared, rows, h):
    hs = slice(h * LANES, (h + 1) * LANES)
    vals = [(r[rows, :] if sh else r[rows, hs]).astype(BF16) for r, sh in zip(refs, shared)]
    return vals[0] if len(vals) == 1 else jnp.concatenate(vals, axis=1)


def _blk_rows(i, t):
    return pl.ds(pl.multiple_of(i * t, t), t)


def _piece_specs(pieces, rows, row_idx):
    return [pl.BlockSpec((rows, LANES if sh else HEADS * LANES), functools.partial(lambda b, i, cb: (row_idx(b, i), cb), cb=cb))
            for _, cb, sh in pieces]


def _attn_fwd(name, qp, kp, vp, bias, unit, scale, n_seq, seq, t, side=None):
    nb = seq // t
    n_tok = n_seq * seq
    nq, nk_p = len(qp), len(kp)
    q_sh, k_sh = [p[2] for p in qp], [p[2] for p in kp]
    nbias = 2 if bias is not None else 0

    def body(*refs):
        q_refs, k_refs = refs[:nq], refs[nq:nq + nk_p]
        v_ref = refs[nq + nk_p]
        bias_refs = refs[nq + nk_p + 1:nq + nk_p + 1 + nbias]
        o_ref, lse_ref = refs[nq + nk_p + 1 + nbias:]
        qi = pl.program_id(1)
        lse_tile = jnp.zeros((t, LANES), F32)
        for h in range(HEADS):
            hs = slice(h * LANES, (h + 1) * LANES)
            q = _head_cat(q_refs, q_sh, slice(None), h)
            cq = _lane_pick(bias_refs[0][...], ROPE + h) if bias is not None else None

            def block(kb, carry, diag, h=h, hs=hs, q=q, cq=cq):
                m, l, acc = carry
                rows = _blk_rows(kb, t)
                s = lax.dot_general(q, _head_cat(k_refs, k_sh, rows, h), _DIMS["nt"], preferred_element_type=F32) * scale
                if bias is not None:
                    s = s + cq - bias_refs[1][kb, h:h + 1, :]
                if diag:
                    s = jnp.where(_diag_visible(t, unit), s, NEG_INF)
                m_new = jnp.maximum(m, jnp.max(s, axis=1, keepdims=True))
                alpha = jnp.exp(m - m_new)
                p = jnp.exp(s - m_new)
                l = alpha * l + jnp.sum(p, axis=1, keepdims=True)
                acc = alpha * acc + jnp.dot(p.astype(BF16), v_ref[rows, hs].astype(BF16), preferred_element_type=F32)
                return m_new, l, acc

            init = (jnp.full((t, 1), NEG_INF, F32), jnp.zeros((t, 1), F32), jnp.zeros((t, LANES), F32))
            carry = lax.fori_loop(0, qi, lambda kb, c: block(kb, c, False), init)
            m, l, acc = block(qi, carry, True)
            o_ref[:, hs] = acc / l
            lse_tile = _lane_put(lse_tile, h, m + jnp.log(l))
        lse_ref[...] = lse_tile

    tile_row = lambda b, i: b * nb + i
    seq_row = lambda b, i: b
    lane_tile = pl.BlockSpec((t, LANES), lambda b, i: (b * nb + i, 0))
    in_specs = _piece_specs(qp, t, tile_row) + _piece_specs(kp, seq, seq_row) + _piece_specs([vp + (False,)], seq, seq_row)
    args = [p[0] for p in qp] + [p[0] for p in kp] + [vp[0]]
    if bias is not None:
        in_specs += [lane_tile, pl.BlockSpec((None, nb, HEADS, t), lambda b, i: (b, 0, 0, 0))]
        args += list(bias)
    return _hosted_call(
        body, name, (n_seq, nb), in_specs,
        [pl.BlockSpec((t, HEADS * LANES), lambda b, i: (b * nb + i, 0)), lane_tile],
        [jax.ShapeDtypeStruct((n_tok, HEADS * LANES), F32), jax.ShapeDtypeStruct((n_tok, LANES), F32)], args, side)


def _attn_bwd_dq(name, qp, kp, vp, o, do, lse, bias, unit, scale, n_seq, seq, t, side=None, grad_dtype=F32):
    nb = seq // t
    n_tok = n_seq * seq
    nq, nk_p = len(qp), len(kp)
    q_sh, k_sh = [p[2] for p in qp], [p[2] for p in kp]
    nbias = 2 if bias is not None else 0
    n_in = nq + nk_p + 4 + nbias

    def body(*refs):
        q_refs, k_refs = refs[:nq], refs[nq:nq + nk_p]
        v_ref, o_ref, do_ref, lse_ref = refs[nq + nk_p:nq + nk_p + 4]
        bias_refs = refs[nq + nk_p + 4:n_in]
        dq_refs = refs[n_in:n_in + nq]
        delta_ref, dob_ref = refs[n_in + nq:n_in + nq + 2]
        qi = pl.program_id(1)
        delta_tile = jnp.zeros((t, LANES), F32)
        dc_tile = jnp.zeros((t, LANES), F32)
        lse_all = lse_ref[...]
        for h in range(HEADS):
            hs = slice(h * LANES, (h + 1) * LANES)
            q = _head_cat(q_refs, q_sh, slice(None), h)
            do_f = do_ref[:, hs]
            do_b = do_f.astype(BF16)
            dob_ref[:, hs] = do_b
            delta = jnp.sum(do_f * o_ref[:, hs], axis=1, keepdims=True)
            lse = _lane_pick(lse_all, h)
            cq = _lane_pick(bias_refs[0][...], ROPE + h) if bias is not None else None

            def block(kb, carry, diag, h=h, hs=hs, q=q, cq=cq, do_b=do_b, delta=delta, lse=lse):
                dq_acc, dc_acc = carry
                rows = _blk_rows(kb, t)
                k = _head_cat(k_refs, k_sh, rows, h)
                s = lax.dot_general(q, k, _DIMS["nt"], preferred_element_type=F32) * scale
                if bias is not None:
                    s = s + cq - bias_refs[1][kb, h:h + 1, :]
                if diag:
                    s = jnp.where(_diag_visible(t, unit), s, NEG_INF)
                p = jnp.exp(s - lse)
                dp = lax.dot_general(do_b, v_ref[rows, hs].astype(BF16), _DIMS["nt"], preferred_element_type=F32)
                ds = p * (dp - delta)
                return (dq_acc + jnp.dot(ds.astype(BF16), k, preferred_element_type=F32),
                        dc_acc + jnp.sum(ds, axis=1, keepdims=True))

            init = (jnp.zeros((t, nq * LANES), F32), jnp.zeros((t, 1), F32))
            carry = lax.fori_loop(0, qi, lambda kb, c: block(kb, c, False), init)
            dq_acc, dc_acc = block(qi, carry, True)
            for n_p in range(nq):
                dq_refs[n_p][:, hs] = (dq_acc[:, n_p * LANES:(n_p + 1) * LANES] * scale).astype(grad_dtype)
            delta_tile = _lane_put(delta_tile, h, delta)
            dc_tile = _lane_put(dc_tile, ROPE + h, dc_acc)
        delta_ref[...] = delta_tile
        if bias is not None:
            refs[n_in + nq + 2][...] = dc_tile

    tile_row = lambda b, i: b * nb + i
    seq_row = lambda b, i: b
    lane_tile = pl.BlockSpec((t, LANES), lambda b, i: (b * nb + i, 0))
    head_tile = pl.BlockSpec((t, HEADS * LANES), lambda b, i: (b * nb + i, 0))
    in_specs = _piece_specs(qp, t, tile_row) + _piece_specs(kp, seq, seq_row) + _piece_specs([vp + (False,)], seq, seq_row)
    in_specs += [head_tile, head_tile, lane_tile]
    args = [p[0] for p in qp] + [p[0] for p in kp] + [vp[0], o, do, lse]
    if bias is not None:
        in_specs += [lane_tile, pl.BlockSpec((None, nb, HEADS, t), lambda b, i: (b, 0, 0, 0))]
        args += list(bias)
    out_specs = [head_tile] * nq + [lane_tile, head_tile] + ([lane_tile] if bias is not None else [])
    out_shape = [jax.ShapeDtypeStruct((n_tok, HEADS * LANES), grad_dtype)] * nq
    out_shape += [jax.ShapeDtypeStruct((n_tok, LANES), F32), jax.ShapeDtypeStruct((n_tok, HEADS * LANES), BF16)]
    if bias is not None:
        out_shape.append(jax.ShapeDtypeStruct((n_tok, LANES), F32))
    return _hosted_call(body, name, (n_seq, nb), in_specs, out_specs, out_shape, args, side)


def _attn_bwd_dkv(name, qp, kp, vp, dob, lse, delta, bias, unit, scale, n_seq, seq, t, side=None, grad_dtype=F32):
    nb = seq // t
    n_tok = n_seq * seq
    nq, nk_p = len(qp), len(kp)
    q_sh, k_sh = [p[2] for p in qp], [p[2] for p in kp]
    nbias = 2 if bias is not None else 0
    n_in = nq + nk_p + 4 + nbias

    def body(*refs):
        q_refs, k_refs = refs[:nq], refs[nq:nq + nk_p]
        v_ref, dob_ref, lse_ref, delta_ref = refs[nq + nk_p:nq + nk_p + 4]
        bias_refs = refs[nq + nk_p + 4:n_in]
        dk_refs = refs[n_in:n_in + nk_p]
        dv_ref = refs[n_in + nk_p]
        ki = pl.program_id(1)
        shared_acc = [jnp.zeros((t, LANES), F32) for _ in range(nk_p)]
        for h in range(HEADS):
            hs = slice(h * LANES, (h + 1) * LANES)
            k = _head_cat(k_refs, k_sh, slice(None), h)
            v = v_ref[:, hs].astype(BF16)
            ck = bias_refs[1][h:h + 1, :] if bias is not None else None

            def block(qb, carry, diag, h=h, hs=hs, k=k, v=v, ck=ck):
                dk_acc, dv_acc, dc_acc = carry
                rows = _blk_rows(qb, t)
                q = _head_cat(q_refs, q_sh, rows, h)
                s = lax.dot_general(q, k, _DIMS["nt"], preferred_element_type=F32) * scale
                if bias is not None:
                    s = s + _lane_pick(bias_refs[0][rows, :], ROPE + h) - ck
                if diag:
                    s = jnp.where(_diag_visible(t, unit), s, NEG_INF)
                p = jnp.exp(s - _lane_pick(lse_ref[rows, :], h))
                do_b = dob_ref[rows, hs]
                dp = lax.dot_general(do_b, v, _DIMS["nt"], preferred_element_type=F32)
                ds = p * (dp - _lane_pick(delta_ref[rows, :], h))
                return (dk_acc + lax.dot_general(ds.astype(BF16), q, _DIMS["tn"], preferred_element_type=F32),
                        dv_acc + lax.dot_general(p.astype(BF16), do_b, _DIMS["tn"], preferred_element_type=F32),
                        dc_acc - jnp.sum(ds, axis=0, keepdims=True))

            init = (jnp.zeros((t, nk_p * LANES), F32), jnp.zeros((t, LANES), F32), jnp.zeros((1, t), F32))
            carry = block(ki, init, True)
            dk_acc, dv_acc, dc_acc = lax.fori_loop(ki + 1, nb, lambda qb, c: block(qb, c, False), carry)
            for n_p in range(nk_p):
                part = dk_acc[:, n_p * LANES:(n_p + 1) * LANES] * scale
                if k_sh[n_p]:
                    shared_acc[n_p] = shared_acc[n_p] + part
                else:
                    dk_refs[n_p][:, hs] = part.astype(grad_dtype)
            dv_ref[:, hs] = dv_acc.astype(grad_dtype)
            if bias is not None:
                refs[n_in + nk_p + 1][h:h + 1, :] = dc_acc
        for n_p in range(nk_p):
            if k_sh[n_p]:
                dk_refs[n_p][...] = shared_acc[n_p]

    tile_row = lambda b, i: b * nb + i
    seq_row = lambda b, i: b
    lane_seq = pl.BlockSpec((seq, LANES), lambda b, i: (b, 0))
    head_tile = pl.BlockSpec((t, HEADS * LANES), lambda b, i: (b * nb + i, 0))
    row_tile = pl.BlockSpec((None, None, HEADS, t), lambda b, i: (b, i, 0, 0))
    in_specs = _piece_specs(qp, seq, seq_row) + _piece_specs(kp, t, tile_row) + _piece_specs([vp + (False,)], t, tile_row)
    in_specs += [pl.BlockSpec((seq, HEADS * LANES), lambda b, i: (b, 0)), lane_seq, lane_seq]
    args = [p[0] for p in qp] + [p[0] for p in kp] + [vp[0], dob, lse, delta]
    if bias is not None:
        in_specs += [lane_seq, row_tile]
        args += list(bias)
    out_specs = [pl.BlockSpec((t, LANES if sh else HEADS * LANES), lambda b, i: (b * nb + i, 0)) for sh in k_sh] + [head_tile]
    out_shape = [jax.ShapeDtypeStruct((n_tok, LANES), F32) if sh else jax.ShapeDtypeStruct((n_tok, HEADS * LANES), grad_dtype)
                 for sh in k_sh]
    out_shape.append(jax.ShapeDtypeStruct((n_tok, HEADS * LANES), grad_dtype))
    if bias is not None:
        out_specs.append(row_tile)
        out_shape.append(jax.ShapeDtypeStruct((n_seq, nb, HEADS, t), F32))
    return _hosted_call(body, name, (n_seq, nb), in_specs, out_specs, out_shape, args, side)


HEAD_GROUPS = 2


def _attn_delta(name, o, do, n_tok):
    def fn(r, v):
        o_v, do_v = r
        tile = jnp.zeros((o_v.shape[0], LANES), F32)
        for h in range(HEADS):
            hs = slice(h * LANES, (h + 1) * LANES)
            tile = _lane_put(tile, h, jnp.sum(do_v[:, hs] * o_v[:, hs], axis=1, keepdims=True))
        return [tile, do_v], []

    return _rows(name, fn, [(o, HEADS * LANES, 0), (do, HEADS * LANES, 0)], [], [(LANES, F32), (HEADS * LANES, BF16)], [], n_tok)


def _attn_bwd(name, qp, kp, vp, dob, lse, delta, bias, unit, scale, n_seq, seq, t, grad_dtype, side=None):
    nb = seq // t
    n_tok = n_seq * seq
    ng = HEAD_GROUPS
    hg = HEADS // ng
    gw = hg * LANES
    nq, nk_p = len(qp), len(kp)
    q_sh, k_sh = [p[2] for p in qp], [p[2] for p in kp]
    assert not any(q_sh) and nq == nk_p
    nbias = 2 if bias is not None else 0
    n_in = nq + nk_p + 4 + nbias
    n_out = nq + nk_p + 1 + nbias

    def body(*refs):
        q_refs, k_refs = refs[:nq], refs[nq:nq + nk_p]
        v_ref, dob_ref, lse_ref, delta_ref = refs[nq + nk_p:nq + nk_p + 4]
        bias_refs = refs[nq + nk_p + 4:n_in]
        dq_refs, dk_refs = refs[n_in:n_in + nq], refs[n_in + nq:n_in + nq + nk_p]
        dv_ref = refs[n_in + nq + nk_p]
        dq_s, dcq_s = refs[n_in + n_out:]
        g, ki = pl.program_id(1), pl.program_id(2)

        @pl.when(ki == 0)
        def _():
            dq_s[...] = jnp.zeros_like(dq_s)
            dcq_s[...] = jnp.zeros_like(dcq_s)

        shared_acc = [jnp.zeros((t, LANES), F32) for _ in range(nk_p)]
        for hl in range(hg):
            h = g * hg + hl
            hs = slice(hl * LANES, (hl + 1) * LANES)
            k = _head_cat(k_refs, k_sh, slice(None), hl)
            v = v_ref[:, hs].astype(BF16)
            ck = bias_refs[1][pl.ds(h, 1), :] if bias is not None else None

            def block(qb, carry, diag, h=h, hl=hl, hs=hs, k=k, v=v, ck=ck):
                dk_acc, dv_acc, dc_acc = carry
                rows = _blk_rows(qb, t)
                q = _head_cat(q_refs, q_sh, rows, hl)
                s = lax.dot_general(q, k, _DIMS["nt"], preferred_element_type=F32) * scale
                if bias is not None:
                    s = s + _lane_pick(bias_refs[0][rows, :], ROPE + h) - ck
                if diag:
                    s = jnp.where(_diag_visible(t, unit), s, NEG_INF)
                p = jnp.exp(s - _lane_pick(lse_ref[rows, :], h))
                do_b = dob_ref[rows, hs]
                dp = lax.dot_general(do_b, v, _DIMS["nt"], preferred_element_type=F32)
                ds = p * (dp - _lane_pick(delta_ref[rows, :], h))
                ds_b = ds.astype(BF16)
                dq_blk = jnp.dot(ds_b, k, preferred_element_type=F32)
                for n_p in range(nq):
                    dq_s[rows, n_p * gw + hl * LANES:n_p * gw + (hl + 1) * LANES] += dq_blk[:, n_p * LANES:(n_p + 1) * LANES]
                if bias is not None:
                    lane = lax.broadcasted_iota(jnp.int32, (t, LANES), 1)
                    dcq_s[rows, :] += jnp.where(lane == ROPE + h, jnp.sum(ds, axis=1, keepdims=True), 0.0)
                return (dk_acc + lax.dot_general(ds_b, q, _DIMS["tn"], preferred_element_type=F32),
                        dv_acc + lax.dot_general(p.astype(BF16), do_b, _DIMS["tn"], preferred_element_type=F32),
                        dc_acc - jnp.sum(ds, axis=0, keepdims=True))

            init = (jnp.zeros((t, nk_p * LANES), F32), jnp.zeros((t, LANES), F32), jnp.zeros((1, t), F32))
            carry = block(ki, init, True)
            dk_acc, dv_acc, dc_acc = lax.fori_loop(ki + 1, nb, lambda qb, c: block(qb, c, False), carry)
            for n_p in range(nk_p):
                part = dk_acc[:, n_p * LANES:(n_p + 1) * LANES] * scale
                if k_sh[n_p]:
                    shared_acc[n_p] = shared_acc[n_p] + part
                else:
                    dk_refs[n_p][:, hs] = part.astype(grad_dtype)
            dv_ref[:, hs] = dv_acc.astype(grad_dtype)
            if bias is not None:
                refs[n_in + n_out - 1][hl:hl + 1, :] = dc_acc
        for n_p in range(nk_p):
            if k_sh[n_p]:
                dk_refs[n_p][...] = shared_acc[n_p]

        @pl.when(ki == nb - 1)
        def _():
            for n_p in range(nq):
                dq_refs[n_p][...] = (dq_s[:, n_p * gw:(n_p + 1) * gw] * scale).astype(grad_dtype)
            if bias is not None:
                refs[n_in + n_out - 2][...] = dcq_s[...]

    def spec(rows, row_idx, cb, shared):
        if shared:
            return pl.BlockSpec((rows, LANES), lambda b, g, i: (row_idx(b, i), cb))
        return pl.BlockSpec((rows, gw), lambda b, g, i: (row_idx(b, i), cb * ng + g))

    tile_row = lambda b, i: b * nb + i
    seq_row = lambda b, i: b
    lane_seq = pl.BlockSpec((seq, LANES), lambda b, g, i: (b, 0))
    in_specs = [spec(seq, seq_row, cb, sh) for _, cb, sh in qp] + [spec(t, tile_row, cb, sh) for _, cb, sh in kp]
    in_specs += [spec(t, tile_row, vp[1], False), spec(seq, seq_row, 0, False), lane_seq, lane_seq]
    args = [p[0] for p in qp] + [p[0] for p in kp] + [vp[0], dob, lse, delta]
    if bias is not None:
        in_specs += [lane_seq, pl.BlockSpec((None, None, HEADS, t), lambda b, g, i: (b, i, 0, 0))]
        args += list(bias)
    group_tile = pl.BlockSpec((None, t, LANES), lambda b, g, i: (g, b * nb + i, 0))
    out_specs = [spec(seq, seq_row, 0, False)] * nq
    out_specs += [group_tile if sh else spec(t, tile_row, 0, False) for sh in k_sh] + [spec(t, tile_row, 0, False)]
    head_shape = jax.ShapeDtypeStruct((n_tok, HEADS * LANES), grad_dtype)
    out_shape = [head_shape] * nq + [jax.ShapeDtypeStruct((ng, n_tok, LANES), F32) if sh else head_shape for sh in k_sh]
    out_shape.append(head_shape)
    if bias is not None:
        out_specs += [pl.BlockSpec((None, seq, LANES), lambda b, g, i: (g, b, 0)),
                      pl.BlockSpec((None, None, None, hg, t), lambda b, g, i: (b, g, i, 0, 0))]
        out_shape += [jax.ShapeDtypeStruct((ng, n_tok, LANES), F32), jax.ShapeDtypeStruct((n_seq, ng, nb, hg, t), F32)]
    return _hosted_call(body, name, (n_seq, ng, nb), in_specs, out_specs, out_shape, args, side,
                        semantics=("parallel", "arbitrary", "arbitrary"),
                        scratch=[pltpu.VMEM((seq, nq * gw), F32), pltpu.VMEM((seq, LANES), F32)])


def _old_attn_bwd_dq(name, qp, kp, vp, o, do, lse, bias, unit, scale, n_seq, seq, t):
    nb = seq // t
    n_tok = n_seq * seq
    nq, nk_p = len(qp), len(kp)
    nbias = 2 if bias is not None else 0
    n_in = nq + nk_p + 4 + nbias
    n_out = nq + (1 if bias is not None else 0)

    def body(*refs):
        q_refs, k_refs = refs[:nq], refs[nq:nq + nk_p]
        v_ref, o_ref, do_ref, lse_ref = refs[nq + nk_p:nq + nk_p + 4]
        bias_refs = refs[nq + nk_p + 4:n_in]
        outs = refs[n_in:n_in + n_out]
        dq_s, delta_s, dc_s = refs[n_in + n_out:]
        qi, ki = pl.program_id(2), pl.program_id(3)

        @pl.when(ki == 0)
        def _():
            dq_s[...] = jnp.zeros_like(dq_s)
            dc_s[...] = jnp.zeros_like(dc_s)
            delta_s[...] = jnp.sum(do_ref[...] * o_ref[...], axis=1, keepdims=True)

        @pl.when(ki <= qi)
        def _():
            s = _scores(q_refs, k_refs, bias_refs, qi, ki, t, unit, scale)
            p = jnp.exp(s - lse_ref[...])
            dp = lax.dot_general(do_ref[...].astype(BF16), v_ref[...].astype(BF16), _DIMS["nt"],
                                 preferred_element_type=F32)
            ds = p * (dp - delta_s[...])
            dq_s[...] += jnp.dot(ds.astype(BF16), _cat(k_refs), preferred_element_type=F32)
            dc_s[...] += jnp.sum(ds, axis=1, keepdims=True)

        @pl.when(ki == qi)
        def _():
            for n_p in range(nq):
                outs[n_p][...] = dq_s[:, n_p * LANES:(n_p + 1) * LANES] * scale
            if bias is not None:
                outs[nq][...] = dc_s[...]

    q_row = lambda b, i, j: b * nb + i
    k_row = lambda b, i, j: b * nb + jnp.minimum(j, i)
    head_q = pl.BlockSpec((t, LANES), lambda b, h, i, j: (b * nb + i, h))
    col_q = pl.BlockSpec((None, t, 1), lambda b, h, i, j: (h, b * nb + i, 0))
    in_specs = [_piece_spec(t, p, q_row) for p in qp] + [_piece_spec(t, p, k_row) for p in kp]
    in_specs += [_piece_spec(t, vp, k_row), head_q, head_q, col_q]
    args = [p[0] for p in qp] + [p[0] for p in kp] + [vp[0], o, do, lse]
    if bias is not None:
        in_specs += [col_q, pl.BlockSpec((None, 1, t), lambda b, h, i, j: (b * HEADS + h, 0, jnp.minimum(j, i)))]
        args += list(bias)
    out_specs = [head_q] * nq + ([col_q] if bias is not None else [])
    out_shape = [jax.ShapeDtypeStruct((n_tok, HEADS * LANES), F32)] * nq
    if bias is not None:
        out_shape.append(jax.ShapeDtypeStruct((HEADS, n_tok, 1), F32))
    return pl.pallas_call(
        body, name=name, grid=(n_seq, HEADS, nb, nb), in_specs=in_specs, out_specs=out_specs, out_shape=out_shape,
        scratch_shapes=[pltpu.VMEM((t, nq * LANES), F32), pltpu.VMEM((t, 1), F32), pltpu.VMEM((t, 1), F32)],
        compiler_params=_params(("parallel", "parallel", "arbitrary", "arbitrary")),
    )(*args)


def _old_attn_bwd_dkv(name, qp, kp, vp, o, do, lse, bias, unit, scale, n_seq, seq, t):
    nb = seq // t
    n_tok = n_seq * seq
    nq, nk_p = len(qp), len(kp)
    nbias = 2 if bias is not None else 0
    n_in = nq + nk_p + 4 + nbias
    n_out = nk_p + 1 + (1 if bias is not None else 0)

    def body(*refs):
        q_refs, k_refs = refs[:nq], refs[nq:nq + nk_p]
        v_ref, o_ref, do_ref, lse_ref = refs[nq + nk_p:nq + nk_p + 4]
        bias_refs = refs[nq + nk_p + 4:n_in]
        outs = refs[n_in:n_in + n_out]
        dk_s, dv_s, dc_s = refs[n_in + n_out:]
        ki, qi = pl.program_id(2), pl.program_id(3)

        @pl.when(qi == 0)
        def _():
            dk_s[...] = jnp.zeros_like(dk_s)
            dv_s[...] = jnp.zeros_like(dv_s)
            dc_s[...] = jnp.zeros_like(dc_s)

        @pl.when(qi >= ki)
        def _():
            s = _scores(q_refs, k_refs, bias_refs, qi, ki, t, unit, scale)
            p = jnp.exp(s - lse_ref[...])
            do_b = do_ref[...].astype(BF16)
            delta = jnp.sum(do_ref[...] * o_ref[...], axis=1, keepdims=True)
            dp = lax.dot_general(do_b, v_ref[...].astype(BF16), _DIMS["nt"], preferred_element_type=F32)
            ds = p * (dp - delta)
            dv_s[...] += lax.dot_general(p.astype(BF16), do_b, _DIMS["tn"], preferred_element_type=F32)
            dk_s[...] += lax.dot_general(ds.astype(BF16), _cat(q_refs), _DIMS["tn"], preferred_element_type=F32)
            dc_s[...] -= jnp.sum(ds, axis=0, keepdims=True)

        @pl.when(qi == nb - 1)
        def _():
            for n_p in range(nk_p):
                outs[n_p][...] = dk_s[:, n_p * LANES:(n_p + 1) * LANES] * scale
            outs[nk_p][...] = dv_s[...]
            if bias is not None:
                outs[nk_p + 1][...] = dc_s[...]

    q_row = lambda b, i, j: b * nb + jnp.maximum(j, i)
    k_row = lambda b, i, j: b * nb + i
    head_q = pl.BlockSpec((t, LANES), lambda b, h, i, j: (b * nb + jnp.maximum(j, i), h))
    col_q = pl.BlockSpec((None, t, 1), lambda b, h, i, j: (h, b * nb + jnp.maximum(j, i), 0))
    head_k = pl.BlockSpec((t, LANES), lambda b, h, i, j: (b * nb + i, h))
    row_k = pl.BlockSpec((None, 1, t), lambda b, h, i, j: (b * HEADS + h, 0, i))
    in_specs = [_piece_spec(t, p, q_row) for p in qp] + [_piece_spec(t, p, k_row) for p in kp]
    in_specs += [_piece_spec(t, vp, k_row), head_q, head_q, col_q]
    args = [p[0] for p in qp] + [p[0] for p in kp] + [vp[0], o, do, lse]
    if bias is not None:
        in_specs += [col_q, row_k]
        args += list(bias)
    out_specs = [head_k] * (nk_p + 1) + ([row_k] if bias is not None else [])
    out_shape = [jax.ShapeDtypeStruct((n_tok, HEADS * LANES), F32)] * (nk_p + 1)
    if bias is not None:
        out_shape.append(jax.ShapeDtypeStruct((n_seq * HEADS, 1, seq), F32))
    return pl.pallas_call(
        body, name=name, grid=(n_seq, HEADS, nb, nb), in_specs=in_specs, out_specs=out_specs, out_shape=out_shape,
        scratch_shapes=[pltpu.VMEM((t, nk_p * LANES), F32), pltpu.VMEM((t, LANES), F32), pltpu.VMEM((1, t), F32)],
        compiler_params=_params(("parallel", "parallel", "arbitrary", "arbitrary")),
    )(*args)


def _seq_cumsum(name, x, col_block, n_seq, seq, reverse, pre=None, vec=None):
    t = _tile(seq, 256, 128)
    nb = seq // t

    def body(*refs):
        x_ref = refs[0]
        vec_ref = refs[1] if vec is not None else None
        o_ref, carry = refs[-2], refs[-1]

        @pl.when(pl.program_id(1) == 0)
        def _():
            carry[...] = jnp.zeros_like(carry)

        v = x_ref[...]
        if pre is not None:
            v = pre(v, vec_ref[...])
        r = lax.broadcasted_iota(jnp.int32, (t, t), 0)
        c = lax.broadcasted_iota(jnp.int32, (t, t), 1)
        tri = jnp.where((c >= r) if reverse else (c <= r), 1.0, 0.0).astype(BF16)
        hi = v.astype(BF16)
        mid = (v - hi.astype(F32)).astype(BF16)
        lo = (v - hi.astype(F32) - mid.astype(F32)).astype(BF16)
        acc = jnp.dot(tri, hi, preferred_element_type=F32)
        acc += jnp.dot(tri, mid, preferred_element_type=F32)
        acc += jnp.dot(tri, lo, preferred_element_type=F32)
        o_ref[...] = acc + carry[...]
        carry[...] += _colsum(v)

    blk = (lambda b, i: (b * nb + nb - 1 - i)) if reverse else (lambda b, i: (b * nb + i))
    in_specs = [pl.BlockSpec((t, LANES), lambda b, i: (blk(b, i), col_block))]
    args = [x]
    if vec is not None:
        in_specs.append(pl.BlockSpec(vec.shape, lambda b, i: (0, 0)))
        args.append(vec)
    return pl.pallas_call(
        body, name=name, grid=(n_seq, nb), in_specs=in_specs,
        out_specs=pl.BlockSpec((t, LANES), lambda b, i: (blk(b, i), 0)),
        out_shape=jax.ShapeDtypeStruct((n_seq * seq, LANES), F32),
        scratch_shapes=[pltpu.VMEM((1, LANES), F32)],
        compiler_params=_params(("arbitrary", "arbitrary")),
    )(*args)


def _log_sigmoid(z):
    return -(jnp.maximum(-z, 0.0) + jnp.log(1.0 + jnp.exp(-jnp.abs(z))))


def _shift_down(u, prev_ref, n):
    out = pltpu.roll(u, n, 0)
    row = lax.broadcasted_iota(jnp.int32, u.shape, 0)
    for r in range(n):
        out = jnp.where(row == r, prev_ref[8 - n + r:8 - n + r + 1, :], out)
    return out


def _shift_up(u, next_ref, n):
    ts = u.shape[0]
    out = pltpu.roll(u, ts - n, 0)
    row = lax.broadcasted_iota(jnp.int32, u.shape, 0)
    for r in range(n):
        out = jnp.where(row == ts - n + r, next_ref[r:r + 1, :], out)
    return out


def _conv_taps(u, prev_ref, w_ref, b_ref):
    s1, s2 = _shift_down(u, prev_ref, 1), _shift_down(u, prev_ref, 2)
    return (w_ref[0:1, :] * s2 + w_ref[1:2, :] * s1 + w_ref[2:3, :] * u) + b_ref[...], s1, s2


def _conv_glu_fwd(u_il, cw_il, cb_il, n_seq, seq, wt):
    n_tok, two_f = u_il.shape
    nct = two_f // (2 * wt)
    ts = _tile(seq, 256, 8)
    ns = seq // ts

    def body(u_ref, w_ref, b_ref, a_ref, carry):
        @pl.when(pl.program_id(2) == 0)
        def _():
            carry[...] = jnp.zeros_like(carry)

        u = u_ref[...].astype(F32)
        uc, _, _ = _conv_taps(u, carry, w_ref, b_ref)
        gel, _ = _gelu_parts(uc[:, :wt])
        a_ref[...] = (gel * uc[:, wt:]).astype(a_ref.dtype)
        carry[...] = u[ts - 8:, :]

    return pl.pallas_call(
        body, name="conv_glu_fwd", grid=(nct, n_seq, ns),
        in_specs=[pl.BlockSpec((ts, 2 * wt), lambda j, b, s: (b * ns + s, j)),
                  pl.BlockSpec((3, 2 * wt), lambda j, b, s: (0, j)),
                  pl.BlockSpec((1, 2 * wt), lambda j, b, s: (0, j))],
        out_specs=pl.BlockSpec((ts, wt), lambda j, b, s: (b * ns + s, j)),
        out_shape=jax.ShapeDtypeStruct((n_tok, two_f // 2), BF16),
        scratch_shapes=[pltpu.VMEM((8, 2 * wt), F32)],
        compiler_params=_params(("parallel", "arbitrary", "arbitrary")),
    )(u_il, cw_il, cb_il)


def _conv_glu_bwd_pre(u_il, da, cw_il, cb_il, n_seq, seq, wt):
    n_tok, two_f = u_il.shape
    nct = two_f // (2 * wt)
    ts = _tile(seq, 256, 8)
    ns = seq // ts

    def body(u_ref, da_ref, w_ref, b_ref, d_ref, acc_ref, carry):
        first = jnp.logical_and(pl.program_id(1) == 0, pl.program_id(2) == 0)

        @pl.when(first)
        def _():
            acc_ref[...] = jnp.zeros_like(acc_ref)

        @pl.when(pl.program_id(2) == 0)
        def _():
            carry[...] = jnp.zeros_like(carry)

        u = u_ref[...].astype(F32)
        uc, s1, s2 = _conv_taps(u, carry, w_ref, b_ref)
        gel, dgel = _gelu_parts(uc[:, :wt])
        da_v = da_ref[...].astype(F32)
        d = jnp.concatenate([da_v * uc[:, wt:] * dgel, da_v * gel], axis=1)
        d_ref[...] = d.astype(d_ref.dtype)
        acc_ref[0:1, :] += _colsum(d * s2)
        acc_ref[1:2, :] += _colsum(d * s1)
        acc_ref[2:3, :] += _colsum(d * u)
        acc_ref[3:4, :] += _colsum(d)
        carry[...] = u[ts - 8:, :]

    return pl.pallas_call(
        body, name="conv_glu_bwd_pre", grid=(nct, n_seq, ns),
        in_specs=[pl.BlockSpec((ts, 2 * wt), lambda j, b, s: (b * ns + s, j)),
                  pl.BlockSpec((ts, wt), lambda j, b, s: (b * ns + s, j)),
                  pl.BlockSpec((3, 2 * wt), lambda j, b, s: (0, j)),
                  pl.BlockSpec((1, 2 * wt), lambda j, b, s: (0, j))],
        out_specs=[pl.BlockSpec((ts, 2 * wt), lambda j, b, s: (b * ns + s, j)),
                   pl.BlockSpec((8, 2 * wt), lambda j, b, s: (0, j))],
        out_shape=[jax.ShapeDtypeStruct((n_tok, two_f), BF16), jax.ShapeDtypeStruct((8, two_f), F32)],
        scratch_shapes=[pltpu.VMEM((8, 2 * wt), F32)],
        compiler_params=_params(("parallel", "arbitrary", "arbitrary")),
    )(u_il, da, cw_il, cb_il)


def _conv_bwd_input(d_il, cw_il, n_seq, seq, wt):
    n_tok, two_f = d_il.shape
    nct = two_f // (2 * wt)
    ts = _tile(seq, 256, 8)
    ns = seq // ts

    def body(d_ref, w_ref, o_ref, carry):
        @pl.when(pl.program_id(2) == 0)
        def _():
            carry[...] = jnp.zeros_like(carry)

        d = d_ref[...].astype(F32)
        o_ref[...] = (w_ref[2:3, :] * d + w_ref[1:2, :] * _shift_up(d, carry, 1)
                      + w_ref[0:1, :] * _shift_up(d, carry, 2)).astype(o_ref.dtype)
        carry[...] = d[:8, :]

    rev = lambda j, b, s: (b * ns + ns - 1 - s, j)
    return pl.pallas_call(
        body, name="conv_bwd_input", grid=(nct, n_seq, ns),
        in_specs=[pl.BlockSpec((ts, 2 * wt), rev), pl.BlockSpec((3, 2 * wt), lambda j, b, s: (0, j))],
        out_specs=pl.BlockSpec((ts, 2 * wt), rev),
        out_shape=jax.ShapeDtypeStruct((n_tok, two_f), BF16),
        scratch_shapes=[pltpu.VMEM((8, 2 * wt), F32)],
        compiler_params=_params(("parallel", "arbitrary", "arbitrary")),
    )(d_il, cw_il)


HBM = pl.BlockSpec(memory_space=pltpu.HBM)
_CHIP_FLIPS = ((1, 0), (0, 1), (1, 1))


def _place():
    x, y, c = lax.axis_index("x"), lax.axis_index("y"), lax.axis_index("c")
    return x, y, c, 2 * x + y


def _flip(v, f):
    return 1 - v if f else v


def _half_rows(c, half):
    return pl.ds(pl.multiple_of(c * half, 16), half)


def _remote(src, dst, ssem, rsem, dev):
    return pltpu.make_async_remote_copy(src_ref=src, dst_ref=dst, send_sem=ssem, recv_sem=rsem,
                                        device_id=dev, device_id_type=MESH)


def _comm_call(name, body, ins, out_shapes, n_sems):
    return pl.pallas_call(
        body, name=name, in_specs=[HBM] * len(ins), out_specs=[HBM] * len(out_shapes),
        out_shape=[pltpu.HBM(s.shape, s.dtype) for s in out_shapes],
        scratch_shapes=[pltpu.SemaphoreType.DMA((n_sems,)), pltpu.SemaphoreType.DMA((n_sems,))],
    )(*ins)


def _all_gather_weights(shards, smalls):
    side = _gather_side(shards, smalls)
    nt = len(shards) + len(smalls)

    def body(*refs):
        for part in (side.start, side.mid, side.end):
            part(refs[:nt], refs[nt:2 * nt], *refs[2 * nt:])

    res = _comm_call("all_gather_weights", body, side.ins, side.outs, side.n_sems)
    return res[:len(shards)], res[len(shards):]


def _pair_split(name, grads):
    n = len(grads)

    def body(*refs):
        src, got = refs[:n], refs[n:2 * n]
        ssem, rsem = refs[2 * n:]
        x, y, c, _ = _place()
        cps = []
        for w in range(n):
            half = grads[w].shape[1] // 2
            cp = _remote(src[w].at[:, _half_rows(1 - c, half)], got[w], ssem.at[w], rsem.at[w], (x, y, 1 - c))
            cp.start()
            cps.append(cp)
        for cp in cps:
            cp.wait()

    outs = [jax.ShapeDtypeStruct((g.shape[0], g.shape[1] // 2, g.shape[2]), g.dtype) for g in grads]
    return _comm_call(name, body, grads, outs, n)


def _chip_scatter(parts):
    side = _scatter_side(parts)

    def body(*refs):
        n = len(parts)
        side.start(refs[:n], refs[n:2 * n], *refs[2 * n:])
        side.end(refs[:n], refs[n:2 * n], *refs[2 * n:])

    return _comm_call("rs_chip_scatter", body, parts, side.outs, side.n_sems)


class _Side:
    def __init__(self, ins, outs, n_sems, start, mid, end, mid_step=None):
        self.ins, self.outs, self.n_sems = list(ins), list(outs), n_sems
        self.start, self.mid, self.end, self.mid_step = start, mid, end, mid_step
        self.aliases = {}


def _scatter_side(parts):
    n = len(parts)

    def copies(src, dst, ssem, rsem):
        x, y, c, _ = _place()
        out = []
        for w in range(n):
            for k, (fx, fy) in enumerate(_CHIP_FLIPS):
                px, py = _flip(x, fx), _flip(y, fy)
                out.append(_remote(src[w].at[2 * px + py], dst[w].at[k], ssem.at[w * 3 + k], rsem.at[w * 3 + k], (px, py, c)))
        return out

    def start(src, dst, ssem, rsem):
        for cp in copies(src, dst, ssem, rsem):
            cp.start()

    def end(src, dst, ssem, rsem):
        for cp in copies(src, dst, ssem, rsem):
            cp.wait()

    outs = [jax.ShapeDtypeStruct((3,) + p.shape[1:], p.dtype) for p in parts]
    return _Side(parts, outs, 3 * n, start, None, end)


def _gather_side(shards, smalls, mid_step=None, into=None):
    n, ns = len(shards), len(smalls)
    into = into or [(None, a.shape[0], 0) for a in shards]

    def dst_rows(w, c):
        half = shards[w].shape[0] // 2
        return pl.ds(pl.multiple_of(into[w][2] + c * half, 16), half)

    def ici(src, dst, ssem, rsem, w, k):
        x, y, c, me = _place()
        fx, fy = _CHIP_FLIPS[k]
        rows = _half_rows(c, shards[w].shape[0] // 2)
        return _remote(src[w].at[rows], dst[w].at[me, dst_rows(w, c)], ssem.at[w * 6 + k], rsem.at[w * 6 + k],
                       (_flip(x, fx), _flip(y, fy), c))

    def small(src, dst, ssem, rsem, s, k):
        x, y, c, me = _place()
        fx, fy = _CHIP_FLIPS[k]
        sem = 6 * n + 3 * s + k
        return _remote(src[n + s], dst[n + s].at[me], ssem.at[sem], rsem.at[sem], (_flip(x, fx), _flip(y, fy), c))

    def landed(dst, ssem, rsem, w, k, sender_c, sem_off):
        x, y, c, _ = _place()
        fx, fy = _CHIP_FLIPS[k]
        got = dst[w].at[2 * _flip(x, fx) + _flip(y, fy), dst_rows(w, sender_c)]
        return _remote(got, got, ssem.at[w * 6 + sem_off + k], rsem.at[w * 6 + sem_off + k], (x, y, 1 - c))

    def start(src, dst, ssem, rsem):
        for s in range(ns):
            for k in range(3):
                small(src, dst, ssem, rsem, s, k).start()
        for w in range(n):
            for k in range(3):
                ici(src, dst, ssem, rsem, w, k).start()

    def mid(src, dst, ssem, rsem):
        c = lax.axis_index("c")
        for w in range(n):
            for k in range(3):
                landed(dst, ssem, rsem, w, k, c, 0).wait_recv()
                landed(dst, ssem, rsem, w, k, c, 3).start()

    def end(src, dst, ssem, rsem):
        c = lax.axis_index("c")
        for w in range(n):
            for k in range(3):
                landed(dst, ssem, rsem, w, k, 1 - c, 3).wait_recv()
        for s in range(ns):
            for k in range(3):
                small(src, dst, ssem, rsem, s, k).wait()
        for w in range(n):
            for k in range(3):
                ici(src, dst, ssem, rsem, w, k).wait_send()
                landed(dst, ssem, rsem, w, k, c, 3).wait_send()

    outs = [jax.ShapeDtypeStruct((N_CHIPS, rows, a.shape[1]), a.dtype) for a, (_, rows, _) in zip(shards, into)]
    outs += [jax.ShapeDtypeStruct((N_CHIPS,) + a.shape, a.dtype) for a in smalls]
    filled = [(w, arr) for w, (arr, _, _) in enumerate(into) if arr is not None]
    side = _Side(list(shards) + list(smalls) + [arr for _, arr in filled], outs, 6 * n + 3 * ns, start, mid, end, mid_step)
    side.aliases = {n + ns + i: w for i, (w, _) in enumerate(filled)}
    return side


def _host(body, n_in, n_out, side, grid):
    if side is None:
        return body
    ns_in, ns_out = len(side.ins), len(side.outs)
    n_steps = math.prod(grid)
    mid_step = side.mid_step
    if side.mid is not None and not isinstance(mid_step, int):
        mid_step = min(n_steps - 1, int(mid_step * n_steps))

    def wrapped(*refs):
        ins, s_ins = refs[:n_in], refs[n_in:n_in + ns_in]
        outs = refs[n_in + ns_in:n_in + ns_in + n_out]
        s_outs = refs[n_in + ns_in + n_out:n_in + ns_in + n_out + ns_out]
        rest = refs[n_in + ns_in + n_out + ns_out:]
        sems = rest[-2:]
        step = 0
        for axis, extent in enumerate(grid):
            step = step * extent + pl.program_id(axis)

        @pl.when(step == 0)
        def _():
            side.start(s_ins, s_outs, *sems)

        if side.mid is not None:
            @pl.when(step == mid_step)
            def _():
                side.mid(s_ins, s_outs, *sems)

        body(*ins, *outs, *rest[:-2])

        @pl.when(step == n_steps - 1)
        def _():
            side.end(s_ins, s_outs, *sems)

    return wrapped


def _hosted_call(body, name, grid, in_specs, out_specs, out_shape, args, side, semantics=("parallel", "arbitrary"),
                 scratch=()):
    n_in, n_out = len(in_specs), len(out_specs)
    kern = _host(body, n_in, n_out, side, grid)
    if side is None:
        return pl.pallas_call(kern, name=name, grid=grid, in_specs=in_specs, out_specs=out_specs, out_shape=out_shape,
                              scratch_shapes=list(scratch), compiler_params=_params(semantics))(*args), []
    res = pl.pallas_call(
        kern, name=name, grid=grid, in_specs=in_specs + [HBM] * len(side.ins), out_specs=out_specs + [HBM] * len(side.outs),
        out_shape=list(out_shape) + [pltpu.HBM(s.shape, s.dtype) for s in side.outs],
        scratch_shapes=list(scratch) + [pltpu.SemaphoreType.DMA((side.n_sems,)), pltpu.SemaphoreType.DMA((side.n_sems,))],
        input_output_aliases={n_in + i: n_out + o for i, o in side.aliases.items()},
        compiler_params=_params(("arbitrary",) * len(grid)),
    )(*args, *side.ins)
    return res[:n_out], res[n_out:]


def _pair_swap(halves):
    n = len(halves)

    def body(*refs):
        src, dst = refs[:n], refs[n:2 * n]
        ssem, rsem = refs[2 * n:]
        x, y, c, _ = _place()
        cps = []
        for w in range(n):
            cp = _remote(src[w], dst[w], ssem.at[w], rsem.at[w], (x, y, 1 - c))
            cp.start()
            cps.append(cp)
        for cp in cps:
            cp.wait()

    outs = [jax.ShapeDtypeStruct(h.shape, h.dtype) for h in halves]
    return _comm_call("rs_pair_swap", body, halves, outs, n)


def _gather_small(vec):
    def body(src, dst, ssem, rsem):
        x, y, c, _ = _place()
        me = 4 * x + 2 * y + c
        cps = []
        for r in range(1, 8):
            dev = (_flip(x, r & 4), _flip(y, r & 2), _flip(c, r & 1))
            cp = _remote(src, dst.at[me], ssem.at[r - 1], rsem.at[r - 1], dev)
            cp.start()
            cps.append(cp)
        for cp in cps:
            cp.wait()

    out = jax.ShapeDtypeStruct((8,) + vec.shape, vec.dtype)
    return _comm_call("gather_small", body, [vec], [out], 7)[0]


def _pair_add(name, g, theirs, core):
    n, half, b = theirs.shape
    tr = _tile(half, 256, 16)
    nt = half // tr

    def body(c_ref, g_ref, t_ref, o_ref):
        o_ref[...] = (g_ref[...].astype(F32) + t_ref[...].astype(F32)).astype(o_ref.dtype)

    same = pl.BlockSpec((None, tr, b), lambda j, i, c: (j, i, 0))
    grid_spec = pltpu.PrefetchScalarGridSpec(
        num_scalar_prefetch=1, grid=(n, nt),
        in_specs=[pl.BlockSpec((None, tr, b), lambda j, i, c: (j, c[0] * nt + i, 0)), same], out_specs=same)
    return pl.pallas_call(body, name=name, grid_spec=grid_spec, out_shape=jax.ShapeDtypeStruct(theirs.shape, BF16),
                          compiler_params=_params(("parallel", "parallel")))(core, g, theirs)


def _sum_slots(name, stacked, first=None):
    n, r, c = stacked.shape
    tr = _tile(r, 256, 8)

    def body(*refs):
        s_ref, o_ref = refs[-2], refs[-1]
        acc = refs[0][...].astype(F32) if first is not None else s_ref[0].astype(F32)
        for s in range(0 if first is not None else 1, n):
            acc = acc + s_ref[s].astype(F32)
        o_ref[...] = acc

    row_spec = pl.BlockSpec((tr, c), lambda i: (i, 0))
    in_specs = ([row_spec] if first is not None else []) + [pl.BlockSpec((n, tr, c), lambda i: (0, i, 0))]
    args = ([first] if first is not None else []) + [stacked]
    return pl.pallas_call(
        body, name=name, grid=(r // tr,), in_specs=in_specs, out_specs=row_spec,
        out_shape=jax.ShapeDtypeStruct((r, c), F32), compiler_params=_params(("parallel",)),
    )(*args)


def _adam_math(w, g, m, v):
    bc1, bc2 = 1.0 - ADAM_B1 ** ADAM_STEP, 1.0 - ADAM_B2 ** ADAM_STEP
    nm = ADAM_B1 * m + (1.0 - ADAM_B1) * g
    nv = ADAM_B2 * v + (1.0 - ADAM_B2) * (g * g)
    return -ADAM_LR * ((nm / bc1) / (jnp.sqrt(nv / bc2) + ADAM_EPS) + ADAM_WD * w), nm, nv


def _adamw_halves(name, w, g_mine, g_theirs, m, v, core):
    r, c = w.shape
    h = r // 2
    tr = _tile(h, 128, 8)
    nth = h // tr

    def body(c_ref, w_ref, gm_ref, gt_ref, m_ref, v_ref, g_ref, d_ref, nm_ref, nv_ref):
        g = jnp.where(pl.program_id(0) // nth == c_ref[0], gm_ref[...], gt_ref[...])
        g_ref[...] = g
        d_ref[...], nm_ref[...], nv_ref[...] = _adam_math(w_ref[...], g, m_ref[...], v_ref[...])

    full = pl.BlockSpec((tr, c), lambda i, cr: (i, 0))
    half = pl.BlockSpec((tr, c), lambda i, cr: (i % nth, 0))
    grid_spec = pltpu.PrefetchScalarGridSpec(num_scalar_prefetch=1, grid=(r // tr,),
                                             in_specs=[full, half, half, full, full], out_specs=[full] * 4)
    return pl.pallas_call(body, name=name, grid_spec=grid_spec, out_shape=[jax.ShapeDtypeStruct((r, c), F32)] * 4,
                          compiler_params=_params(("parallel",)))(core, w, g_mine, g_theirs, m, v)


def _adamw(name, w, g, m, v):
    r, c = w.shape
    by_cols = r % 8 != 0 and c % LANES == 0
    tr, tc = (r, _tile(c, 256, LANES)) if by_cols else (_tile(r, 256, 8), c)

    def body(w_ref, g_ref, m_ref, v_ref, d_ref, nm_ref, nv_ref):
        d_ref[...], nm_ref[...], nv_ref[...] = _adam_math(w_ref[...], g_ref[...], m_ref[...], v_ref[...])

    spec = pl.BlockSpec((tr, tc), (lambda i: (0, i)) if by_cols else (lambda i: (i, 0)))
    return pl.pallas_call(
        body, name=name, grid=(c // tc if by_cols else r // tr,), in_specs=[spec] * 4, out_specs=[spec] * 3,
        out_shape=[jax.ShapeDtypeStruct((r, c), F32)] * 3, compiler_params=_params(("parallel",)),
    )(w, g, m, v)


def _pad_cols(a, cols):
    return jnp.pad(a, ((0, 0), (0, cols - a.shape[1])))


def _rot_cols(w):
    h = w.shape[-1] // 2
    return jnp.concatenate([-w[..., h:], w[..., :h]], axis=-1)


def _unrot_cols(d):
    h = d.shape[-1] // 2
    return jnp.concatenate([d[..., h:], -d[..., :h]], axis=-1)


def _logical(g):
    return jnp.transpose(g, (1, 0, 2)).reshape(g.shape[1], N_CHIPS * g.shape[2])


def _chunks(a, n):
    return jnp.transpose(a.reshape(a.shape[0], N_CHIPS, n), (1, 0, 2))


def kernel(x, positions, pre_mix_norm, w_in, q_a_norm, w_uq, kv_a_norm, w_ukv, b_forget, b_gate, w_branch_mla, w_branch_fox, w_out, post_mix_norm, pre_ffn_norm, w_up, conv_w, conv_b, w_down, post_ffn_norm, loss_target, m_pre_mix_norm, m_w_in, m_q_a_norm, m_w_uq, m_kv_a_norm, m_w_ukv, m_b_forget, m_b_gate, m_w_branch_mla, m_w_branch_fox, m_w_out, m_post_mix_norm, m_pre_ffn_norm, m_w_up, m_conv_w, m_conv_b, m_w_down, m_post_ffn_norm, v_pre_mix_norm, v_w_in, v_q_a_norm, v_w_uq, v_kv_a_norm, v_w_ukv, v_b_forget, v_b_gate, v_w_branch_mla, v_w_branch_fox, v_w_out, v_post_mix_norm, v_pre_ffn_norm, v_w_up, v_conv_w, v_conv_b, v_w_down, v_post_ffn_norm):
    n_seq, seq, d = x.shape
    n_tok = n_seq * seq
    d_in = N_CHIPS * w_in.shape[1]
    two_f = N_CHIPS * w_up.shape[1]
    ff_dim = two_f // 2
    assert d_in == QL + KVL + ROPE + 3 * HEADS * FDIM + HEADS + 2 * d
    n_in_shard = w_in.shape[1]
    in_pad = -(-n_in_shard // LANES) * LANES
    hd = HEADS * LANES
    xc, yc, cc = lax.axis_index("x"), lax.axis_index("y"), lax.axis_index("c")
    chip = 2 * xc + yc
    t_attn = _tile(seq, 512, 128)

    shards = [_pad_cols(w_in, in_pad).astype(BF16), w_uq.astype(BF16), w_ukv.astype(BF16), w_branch_mla.astype(BF16),
              w_branch_fox.astype(BF16), w_out.astype(BF16), w_up.astype(BF16), w_down.astype(BF16)]
    cw8 = jnp.pad(conv_w, ((0, 5), (0, 0)))
    put_own = lambda g, s: lax.dynamic_update_slice(g, s[None], (chip, 0, 0))
    gathered, (g_cw,) = _all_gather_weights(shards[:3], [cw8])
    g_in, g_uq, g_ukv = [put_own(g, s) for g, s in zip(gathered, shards[:3])]
    g_cw = put_own(g_cw, cw8)
    n_attn_steps = n_seq * (seq // t_attn)
    up_rows = shards[6].shape[0] // 2
    side_proj = _gather_side([shards[3], shards[4], shards[5]], [], mid_step=0.9)
    side_mla = _gather_side([shards[6][:up_rows]], [], mid_step=max(n_attn_steps - 2, 0), into=[(None, 2 * up_rows, 0)])
    side_ffn = _gather_side([shards[7]], [], mid_step=0.7)

    o_q, o_kv, o_kpe = 0, QL, QL + KVL
    o_f = o_kpe + ROPE
    o_fl = o_f + 3 * hd
    o_g = o_fl + HEADS

    def chip_cols(lo, hi):
        out = []
        while lo < hi:
            j = lo // n_in_shard
            end = min(hi, (j + 1) * n_in_shard)
            out.append((j, lo - j * n_in_shard, end - j * n_in_shard))
            lo = end
        return out

    take = lambda lo, hi: [g_in[j, :, a:b] for j, a, b in chip_cols(lo, hi)]
    w_kpe = jnp.concatenate(take(o_kpe, o_f), axis=1)
    zeros = lambda n: jnp.zeros((d, n), BF16)
    win_p = jnp.concatenate(
        take(o_g, d_in) + take(o_q, o_kpe) + [w_kpe, zeros(LANES - ROPE), _rot_cols(w_kpe)] + take(o_fl, o_g)
        + [zeros(LANES - ROPE - HEADS)] + take(o_f, o_fl), axis=1)
    n_p = win_p.shape[1]
    cb_gm, cb_gf = 0, 1
    c_lat = 2 * d
    c_kx, c_kr = c_lat + QL + KVL, c_lat + QL + KVL + LANES
    n_pa = c_kr + LANES
    assert n_p == n_pa + 3 * hd

    uq3 = _logical(g_uq).reshape(QL, HEADS, NOPE + ROPE)
    pe = uq3[:, :, NOPE:]
    pad_pe = lambda a: jnp.pad(a, ((0, 0), (0, 0), (0, LANES - ROPE))).reshape(QL, hd)
    wuq_p = jnp.concatenate([uq3[:, :, :NOPE].reshape(QL, hd), pad_pe(pe), pad_pe(_rot_cols(pe))], axis=1)
    ukv3 = _logical(g_ukv).reshape(KVL, HEADS, NOPE + VDIM)
    wukv_p = jnp.concatenate([ukv3[:, :, :NOPE].reshape(KVL, hd), ukv3[:, :, NOPE:].reshape(KVL, hd)], axis=1)

    n_bm, n_up = w_branch_mla.shape[1], w_up.shape[1]
    l_bm, l_up = _Chunked(n_bm), _Chunked(n_up)
    wt = n_up // 2
    n_ut = two_f // wt
    il = lambda cblk: jnp.where(cblk < n_ut // 2, 2 * cblk, 2 * (cblk - n_ut // 2) + 1)
    l_il = _Plain(il)
    to_il = lambda a: a.reshape(a.shape[0], 2, n_ut // 2, wt).transpose(0, 2, 1, 3).reshape(a.shape[0], two_f)
    from_il = lambda a: a.reshape(a.shape[0], n_ut // 2, 2, wt).transpose(0, 2, 1, 3).reshape(a.shape[0], two_f)

    inv_freq = 1.0 / (ROPE_THETA ** (jnp.arange(0, ROPE, 2, dtype=F32) / ROPE))
    ang = positions.astype(F32).reshape(n_tok, 1) * inv_freq
    cos, sin = jnp.cos(ang), jnp.sin(ang)
    cs = _pad_cols(jnp.concatenate([cos, cos], axis=1), LANES)
    sn = _pad_cols(jnp.concatenate([sin, sin], axis=1), LANES)

    row = lambda v: v.reshape(1, -1)
    x2 = x.reshape(n_tok, d)
    tgt = loss_target.reshape(n_tok, d)

    (h,) = _rows("rms_pre_mix", lambda r, v: ([r[0] * _rstd(r[0]) * v[0]], []),
                 [(x2, d, 0)], [row(pre_mix_norm)], [(d, BF16)], [], n_tok)
    proj, got = _mm("proj_in", "nn", h, win_p, n_tok, n_pa, d, side=side_proj)
    g_bm, g_bf, g_out = [put_own(g, s) for g, s in zip(got, side_proj.ins)]
    w_out_full = g_out.reshape(d, d)
    tn_f = _tile(3 * hd, 1024, 128)
    assert n_pa % tn_f == 0
    proj_f = _mm("proj_in_fox", "nn", h, win_p, n_tok, 3 * hd, d, tn=tn_f, lb=_Plain(lambda cblk: cblk + n_pa // tn_f),
                 out_dtype=BF16)

    bf_vec = jnp.pad(row(b_forget), ((0, 0), (ROPE, LANES - ROPE - HEADS)))

    def lat_fwd(r, v):
        ql, kvl = r[0], r[1]
        return [ql * _rstd(ql) * v[0], kvl * _rstd(kvl) * v[1], r[2] * r[4] + r[3] * r[5]], []

    qn, kvn, rk = _rows("latent_norms", lat_fwd,
                        [(proj, QL, c_lat // QL), (proj, KVL, (c_lat + QL) // KVL), (proj, LANES, c_kx // LANES),
                         (proj, LANES, c_kr // LANES), (cs, LANES, 0), (sn, LANES, 0)],
                        [row(q_a_norm), row(kv_a_norm)], [(QL, BF16), (KVL, BF16), (LANES, BF16)], [], n_tok)
    q_p = _mm("q_up", "nn", qn, wuq_p, n_tok, 3 * hd, QL)
    kv_p = _mm("kv_up", "nn", kvn, wukv_p, n_tok, 2 * hd, KVL, out_dtype=BF16)

    def rope_q(r, v):
        c8, s8 = jnp.tile(r[3], (1, HEADS)), jnp.tile(r[4], (1, HEADS))
        return [r[0], r[1] * c8 + r[2] * s8], []

    q_nope, rq = _rows("rope_q", rope_q, [(q_p, hd, 0), (q_p, hd, 1), (q_p, hd, 2), (cs, LANES, 0), (sn, LANES, 0)], [],
                       [(hd, BF16), (hd, BF16)], [], n_tok)

    mla_q = [(q_nope, 0, False), (rq, 0, False)]
    mla_k = [(kv_p, 0, False), (rk, 0, True)]
    mla_v = (kv_p, 1)
    mla_scale = (NOPE + ROPE) ** -0.5
    (o_mla, lse_mla), got = _attn_fwd("mla_fwd", mla_q, mla_k, mla_v, None, CHUNK, mla_scale, n_seq, seq, t_attn, side=side_mla)
    side_fox = _gather_side([shards[6][up_rows:]], [], mid_step=max(n_attn_steps - 2, 0), into=[(got[0], 2 * up_rows, up_rows)])

    c_run = _seq_cumsum("forget_cumsum", proj, c_kr // LANES, n_seq, seq, False,
                        pre=lambda z, b: _log_sigmoid(z + b), vec=bf_vec)
    nb_attn = seq // t_attn
    c_rowf = jnp.transpose(c_run[:, ROPE:ROPE + HEADS].reshape(n_seq, nb_attn, t_attn, HEADS), (0, 1, 3, 2))
    fox_q, fox_k, fox_v = [(proj_f, 0, False)], [(proj_f, 1, False)], (proj_f, 2)
    fox_scale = FDIM ** -0.5
    fox_bias = (c_run, c_rowf)
    (o_fox, lse_fox), got = _attn_fwd("fox_fwd", fox_q, fox_k, fox_v, fox_bias, 1, fox_scale, n_seq, seq, t_attn, side=side_fox)
    g_up = put_own(got[0], shards[6])

    pm = _mm("branch_mla", "nn", o_mla, g_bm, n_tok, d, hd, lb=l_bm, tn=n_bm)
    pf = _mm("branch_fox", "nn", o_fox, g_bf, n_tok, d, hd, lb=l_bm, tn=n_bm)
    bg = row(b_gate)

    def merge(r, v):
        return [_sigmoid(r[0] + v[0]) * r[2] + _sigmoid(r[1] + v[1]) * r[3]], []

    (merged,) = _rows("gate_merge", merge, [(proj, d, cb_gm), (proj, d, cb_gf), (pm, d, 0), (pf, d, 0)],
                      [bg[:, :d], bg[:, d:]], [(d, BF16)], [], n_tok)
    y1 = _mm("mix_out", "nn", merged, w_out_full, n_tok, d, d)

    def resid_norm(r, v):
        x1v = r[0] + r[1] * _rstd(r[1]) * v[0]
        return [x1v, x1v * _rstd(x1v) * v[1]], []

    x1, h2 = _rows("post_mix_pre_ffn", resid_norm, [(x2, d, 0), (y1, d, 0)], [row(post_mix_norm), row(pre_ffn_norm)],
                   [(d, F32), (d, BF16)], [], n_tok)

    u_il, got = _mm("ffn_up", "nn", h2, g_up, n_tok, two_f, d, lb=l_up, lo=l_il, tn=wt, out_dtype=BF16, side=side_ffn)
    w_down_full = put_own(got[0], side_ffn.ins[0]).reshape(ff_dim, d)
    cw_il = to_il(_logical(g_cw)[:3])
    cb_il = to_il(row(conv_b))
    act = _conv_glu_fwd(u_il, cw_il, cb_il, n_seq, seq, wt)
    ff = _mm("ffn_down", "nn", act, w_down_full, n_tok, d, ff_dim)

    def final(r, v):
        x1v, ffv, tg = r
        diff = x1v + ffv * _rstd(ffv) * v[0] - tg
        dx2v = diff / d
        dffv, dg4 = _rms_bwd(ffv, v[0], dx2v)
        sq = jnp.sum(jnp.sum(diff * diff, axis=1, keepdims=True), axis=0, keepdims=True)
        return [dx2v, dffv], [dg4, jnp.broadcast_to(sq, (1, LANES))]

    dx2, dff, dg_post_ffn, sq_sum = _rows("loss_post_ffn_bwd", final, [(x1, d, 0), (ff, d, 0), (tgt, d, 0)],
                                          [row(post_ffn_norm)], [(d, F32), (d, BF16)], [(1, d), (1, LANES)], n_tok)
    rs_parts, rs_landed = {}, {}
    core = jnp.reshape(cc, (1,)).astype(jnp.int32)

    def pair_reduce(tag, names, grads):
        theirs = _pair_split("rs_pair_split_" + tag, grads)
        for nm, g, b in zip(names, grads, theirs):
            rs_parts[nm] = _pair_add("rs_pair_add_" + nm, g, b, core)

    dact = _mm("ffn_down_dx", "nt", dff, w_down_full, n_tok, ff_dim, d, tn=wt, out_dtype=BF16)
    gw_down = _mm("ffn_down_dw", "tn", act, dff, ff_dim, d, n_tok, tm=wt, out_dtype=BF16)
    pair_reduce("down", ["w_down"], [gw_down.reshape(N_CHIPS, ff_dim // N_CHIPS, d)])
    d_il, conv_acc = _conv_glu_bwd_pre(u_il, dact, cw_il, cb_il, n_seq, seq, wt)
    du_il = _conv_bwd_input(d_il, cw_il, n_seq, seq, wt)
    gw_up, got = _mm("ffn_up_dw", "tn", h2, du_il, d, two_f, n_tok, lb=l_il, lo=l_up, tn=wt, out_dtype=BF16,
                     side=_scatter_side([rs_parts["w_down"]]))
    rs_landed["w_down"] = got[0]
    pair_reduce("up", ["w_up"], [gw_up])
    dh2, got = _mm("ffn_up_dx", "nt", du_il, g_up, n_tok, d, two_f, la=l_il, lb=l_up, tk=wt,
                   side=_scatter_side([rs_parts["w_up"]]))
    rs_landed["w_up"] = got[0]

    def mid_bwd(r, v):
        x1v, y1v, dx2v, dh2v = r
        d3, dg3 = _rms_bwd(x1v, v[1], dh2v)
        dx1v = dx2v + d3
        dy1v, dg2 = _rms_bwd(y1v, v[0], dx1v)
        return [dx1v, dy1v], [dg3, dg2]

    dx1, dy1, dg_pre_ffn, dg_post_mix = _rows(
        "pre_ffn_post_mix_bwd", mid_bwd, [(x1, d, 0), (y1, d, 0), (dx2, d, 0), (dh2, d, 0)],
        [row(post_mix_norm), row(pre_ffn_norm)], [(d, F32), (d, BF16)], [(1, d), (1, d)], n_tok)
    dmerged = _mm("mix_out_dx", "nt", dy1, w_out_full, n_tok, d, d)
    gw_out = _mm("mix_out_dw", "tn", merged, dy1, d, d, n_tok, out_dtype=BF16)

    def gate_bwd(r, v):
        zm, zf, pmv, pfv, dm = r
        gm, gf = _sigmoid(zm + v[0]), _sigmoid(zf + v[1])
        dzm, dzf = dm * pmv * gm * (1.0 - gm), dm * pfv * gf * (1.0 - gf)
        return [dm * gm, dm * gf, jnp.concatenate([dzm, dzf], axis=1)], [_colsum(dzm), _colsum(dzf)]

    dpm, dpf, dz, dbg_m, dbg_f = _rows(
        "gate_merge_bwd", gate_bwd, [(proj, d, cb_gm), (proj, d, cb_gf), (pm, d, 0), (pf, d, 0), (dmerged, d, 0)],
        [bg[:, :d], bg[:, d:]], [(d, BF16), (d, BF16), (2 * d, BF16)], [(1, d), (1, d)], n_tok)
    tk_b = min(n_bm, 512)
    do_mla = _mm("branch_mla_dx", "nt", dpm, g_bm, n_tok, hd, d, lb=l_bm, tk=tk_b)
    do_fox = _mm("branch_fox_dx", "nt", dpf, g_bf, n_tok, hd, d, lb=l_bm, tk=tk_b)
    gw_bm = _mm("branch_mla_dw", "tn", o_mla, dpm, hd, d, n_tok, lo=l_bm, tn=n_bm, out_dtype=BF16)
    gw_bf = _mm("branch_fox_dw", "tn", o_fox, dpf, hd, d, n_tok, lo=l_bm, tn=n_bm, out_dtype=BF16)

    pair_reduce("mix", ["w_out", "w_branch_mla", "w_branch_fox"], [gw_out.reshape(N_CHIPS, d // N_CHIPS, d), gw_bm, gw_bf])
    delta_mla, dob_mla = _attn_delta("mla_delta", o_mla, do_mla, n_tok)
    (dq_nope, drq, dk_nope, drk_g, dv_mla), got = _attn_bwd(
        "mla_bwd", mla_q, mla_k, mla_v, dob_mla, lse_mla, delta_mla, None, CHUNK, mla_scale, n_seq, seq, t_attn, BF16,
        side=_scatter_side([rs_parts[nm] for nm in ("w_out", "w_branch_mla", "w_branch_fox")]))
    rs_landed.update(zip(("w_out", "w_branch_mla", "w_branch_fox"), got))
    delta_fox, dob_fox = _attn_delta("fox_delta", o_fox, do_fox, n_tok)
    (dfq, dfk, dfv, dc_q, dc_k), _ = _attn_bwd("fox_bwd", fox_q, fox_k, fox_v, dob_fox, lse_fox, delta_fox, fox_bias, 1,
                                               fox_scale, n_seq, seq, t_attn, BF16)
    dc_k8 = jnp.transpose(dc_k, (0, 2, 4, 1, 3)).reshape(n_tok, HEADS)
    dc128 = dc_q[0] + dc_q[1] + jnp.pad(dc_k8, ((0, 0), (ROPE, LANES - ROPE - HEADS)))
    dlogf = _seq_cumsum("forget_cumsum_bwd", dc128, 0, n_seq, seq, True)

    def mla_pack(r, v):
        dqn_v, drq_v, dkn_v, dv_v, drk_a, drk_b, c1, s1 = r
        c8, s8 = jnp.tile(c1, (1, HEADS)), jnp.tile(s1, (1, HEADS))
        drk_v = drk_a + drk_b
        return [jnp.concatenate([dqn_v, drq_v * c8, drq_v * s8], axis=1), jnp.concatenate([dkn_v, dv_v], axis=1),
                drk_v * c1, drk_v * s1], []

    dq_p, dkv_p, dkx, dkr = _rows(
        "mla_rope_bwd", mla_pack,
        [(dq_nope, hd, 0), (drq, hd, 0), (dk_nope, hd, 0), (dv_mla, hd, 0), (drk_g[0], LANES, 0), (drk_g[1], LANES, 0),
         (cs, LANES, 0), (sn, LANES, 0)],
        [], [(3 * hd, BF16), (2 * hd, BF16), (LANES, F32), (LANES, F32)], [], n_tok)
    dqn = _mm("q_up_dx", "nt", dq_p, wuq_p, n_tok, QL, 3 * hd)
    gw_uq_p = _mm("q_up_dw", "tn", qn, dq_p, QL, 3 * hd, n_tok, out_dtype=BF16)
    dkvn = _mm("kv_up_dx", "nt", dkv_p, wukv_p, n_tok, KVL, 2 * hd)
    gw_ukv_p = _mm("kv_up_dw", "tn", kvn, dkv_p, KVL, 2 * hd, n_tok, out_dtype=BF16)

    def lat_bwd(r, v):
        ql, kvl, dqn_v, dkvn_v, dkx_v, dkr_v, zblk, dlf = r
        dql, dgq = _rms_bwd(ql, v[0], dqn_v)
        dkvl, dgkv = _rms_bwd(kvl, v[1], dkvn_v)
        dfl = dlf * _sigmoid(-(zblk + v[2]))
        return [jnp.concatenate([dql, dkvl, dkx_v, dkr_v + dfl], axis=1)], [dgq, dgkv, _colsum(dfl)]

    dlat, dg_q, dg_kv, dbf = _rows(
        "latent_bwd", lat_bwd,
        [(proj, QL, c_lat // QL), (proj, KVL, (c_lat + QL) // KVL), (dqn, QL, 0), (dkvn, KVL, 0), (dkx, LANES, 0),
         (dkr, LANES, 0), (proj, LANES, c_kr // LANES), (dlogf, LANES, 0)],
        [row(q_a_norm), row(kv_a_norm), bf_vec], [(QL + KVL + 2 * LANES, BF16)], [(1, QL), (1, KVL), (1, LANES)], n_tok)
    dproj = [dz, dlat, dfq, dfk, dfv]
    gw_in_p = _mm_parts("proj_in_dw", "tn", h, dproj, d, n_p, n_tok, tk=1024, out_dtype=BF16)

    f32 = lambda a: a.astype(F32)
    kr_blk = gw_in_p[:, c_kr:c_kr + LANES]
    d_kpe = (f32(gw_in_p[:, c_kx:c_kx + ROPE]) + _unrot_cols(f32(kr_blk[:, :ROPE]))).astype(BF16)
    in_pieces = [(o_q, gw_in_p, c_lat, QL + KVL), (o_kpe, d_kpe, 0, ROPE), (o_f, gw_in_p, n_pa, 3 * hd),
                 (o_fl, kr_blk, ROPE, HEADS), (o_g, gw_in_p, 0, 2 * d)]
    gc_in = []
    for j in range(N_CHIPS):
        lo, hi, cols = j * n_in_shard, (j + 1) * n_in_shard, []
        for first, arr, at, width in in_pieces:
            a, b = max(lo, first), min(hi, first + width)
            if a < b:
                cols.append(arr[:, at + a - first:at + b - first])
        cols.append(jnp.zeros((d, in_pad - n_in_shard), BF16))
        gc_in.append(jnp.concatenate(cols, axis=1))
    gc_in = jnp.stack(gc_in)
    uq_parts = [gw_uq_p[:, i * hd:(i + 1) * hd].reshape(QL, HEADS, LANES) for i in range(3)]
    d_pe = (f32(uq_parts[1][:, :, :ROPE]) + _unrot_cols(f32(uq_parts[2][:, :, :ROPE]))).astype(BF16)
    gc_uq = _chunks(jnp.concatenate([uq_parts[0], d_pe], axis=2).reshape(QL, HEADS * (NOPE + ROPE)), w_uq.shape[1])
    gc_ukv = _chunks(jnp.concatenate([gw_ukv_p[:, :hd].reshape(KVL, HEADS, NOPE), gw_ukv_p[:, hd:].reshape(KVL, HEADS, VDIM)],
                                     axis=2).reshape(KVL, HEADS * (NOPE + VDIM)), w_ukv.shape[1])
    grads = [gc_in, gc_uq, gc_ukv]

    late = ["w_in", "w_uq", "w_ukv"]
    pair_reduce("late", late, grads)
    dh, got = _mm_parts("proj_in_dx", "nt", dproj, win_p, n_tok, d, n_p, side=_scatter_side([rs_parts[nm] for nm in late]))
    rs_landed.update(zip(late, got))

    def first_bwd(r, v):
        dxa, dg1 = _rms_bwd(r[0], v[0], r[1])
        return [r[2] + dxa], [dg1]

    grad_x, dg_pre_mix = _rows("pre_mix_bwd", first_bwd, [(x2, d, 0), (dh, d, 0), (dx1, d, 0)], [row(pre_mix_norm)],
                               [(d, F32)], [(1, d)], n_tok)
    big = list(rs_parts)
    halves = [_sum_slots("rs_chip_sum_" + nm, rs_landed[nm],
                         first=lax.dynamic_index_in_dim(rs_parts[nm], chip, 0, keepdims=False)) for nm in big]
    other = _pair_swap(halves)
    g_halves = dict(zip(big, zip(halves, other)))

    conv_acc_l = from_il(conv_acc)
    pieces = [dg_pre_mix, dg_q, dg_kv, dbf, dbg_m, dbg_f, dg_post_mix, dg_pre_ffn, conv_acc_l[3:4], dg_post_ffn,
              conv_acc_l[0:1], conv_acc_l[1:2], conv_acc_l[2:3], sq_sum]
    sizes = [p.shape[1] for p in pieces]
    flat = jnp.concatenate(pieces, axis=1)
    n_rows = -(-flat.shape[1] // (8 * LANES)) * 8
    flat = _pad_cols(flat, n_rows * LANES).reshape(n_rows, LANES)
    slots = lax.dynamic_update_slice(_gather_small(flat), flat[None], (2 * chip + cc, 0, 0))
    total = _sum_slots("small_sum", slots).reshape(1, n_rows * LANES)
    offs = [sum(sizes[:i]) for i in range(len(sizes))]
    tot = [total[0, o:o + s] for o, s in zip(offs, sizes)]
    loss = 0.5 * tot[13][0] / d
    g_small = {"pre_mix_norm": tot[0], "q_a_norm": tot[1], "kv_a_norm": tot[2], "b_forget": tot[3][ROPE:ROPE + HEADS],
               "b_gate": jnp.concatenate([tot[4], tot[5]]), "post_mix_norm": tot[6], "pre_ffn_norm": tot[7],
               "conv_b": tot[8], "post_ffn_norm": tot[9]}
    gcw_full = jnp.stack([tot[10], tot[11], tot[12]])
    g_conv_w = lax.dynamic_slice(gcw_full, (0, chip * n_up), (3, n_up))

    given = dict(pre_mix_norm=(pre_mix_norm, m_pre_mix_norm, v_pre_mix_norm), w_in=(w_in, m_w_in, v_w_in),
                 q_a_norm=(q_a_norm, m_q_a_norm, v_q_a_norm), w_uq=(w_uq, m_w_uq, v_w_uq),
                 kv_a_norm=(kv_a_norm, m_kv_a_norm, v_kv_a_norm), w_ukv=(w_ukv, m_w_ukv, v_w_ukv),
                 b_forget=(b_forget, m_b_forget, v_b_forget), b_gate=(b_gate, m_b_gate, v_b_gate),
                 w_branch_mla=(w_branch_mla, m_w_branch_mla, v_w_branch_mla),
                 w_branch_fox=(w_branch_fox, m_w_branch_fox, v_w_branch_fox), w_out=(w_out, m_w_out, v_w_out),
                 post_mix_norm=(post_mix_norm, m_post_mix_norm, v_post_mix_norm),
                 pre_ffn_norm=(pre_ffn_norm, m_pre_ffn_norm, v_pre_ffn_norm), w_up=(w_up, m_w_up, v_w_up),
                 conv_w=(conv_w, m_conv_w, v_conv_w), conv_b=(conv_b, m_conv_b, v_conv_b),
                 w_down=(w_down, m_w_down, v_w_down), post_ffn_norm=(post_ffn_norm, m_post_ffn_norm, v_post_ffn_norm))
    order = list(given)
    grad, delta, new_m, new_v = {}, {}, {}, {}
    for nm in big:
        mine, theirs = g_halves[nm]
        if nm == "w_in":
            full = jnp.concatenate([jnp.where(cc == 0, mine, theirs), jnp.where(cc == 0, theirs, mine)], axis=0)
            grad[nm] = full[:, :n_in_shard]
            tr_out = _adamw("adamw_" + nm, *[jnp.transpose(a) for a in (given[nm][0], grad[nm], given[nm][1], given[nm][2])])
            delta[nm], new_m[nm], new_v[nm] = [jnp.transpose(a) for a in tr_out]
            continue
        grad[nm], delta[nm], new_m[nm], new_v[nm] = _adamw_halves("adamw_" + nm, given[nm][0], mine, theirs, given[nm][1],
                                                                  given[nm][2], core)
    grad["conv_w"] = g_conv_w
    delta["conv_w"], new_m["conv_w"], new_v["conv_w"] = _adamw("adamw_conv_w", conv_w, g_conv_w, m_conv_w, v_conv_w)
    small = list(g_small)
    padded = [-(-g_small[nm].shape[0] // LANES) * LANES for nm in small]
    s_rows = -(-sum(padded) // (8 * LANES)) * 8

    def pack(vals):
        cat = jnp.concatenate([jnp.pad(a, (0, p - a.shape[0])) for a, p in zip(vals, padded)])
        return jnp.pad(cat, (0, s_rows * LANES - cat.shape[0])).reshape(s_rows, LANES)

    packed = _adamw("adamw_small", pack([given[nm][0] for nm in small]), pack([g_small[nm] for nm in small]),
                    pack([given[nm][1] for nm in small]), pack([given[nm][2] for nm in small]))
    s_offs = [sum(padded[:i]) for i in range(len(small))]
    for nm, o in zip(small, s_offs):
        n_el = g_small[nm].shape[0]
        grad[nm] = g_small[nm]
        delta[nm], new_m[nm], new_v[nm] = [p.reshape(-1)[o:o + n_el] for p in packed]
    return (loss, grad_x.reshape(n_seq, seq, d), *[grad[nm] for nm in order], *[delta[nm] for nm in order],
            *[new_m[nm] for nm in order], *[new_v[nm] for nm in order])
```

```python
import functools
import math

import jax
import jax.numpy as jnp
from jax import lax
from jax.experimental import pallas as pl
from jax.experimental.pallas import tpu as pltpu

F32, BF16 = jnp.float32, jnp.bfloat16
MESH = pl.DeviceIdType.MESH

HEADS = 8
NOPE, ROPE, VDIM = 128, 64, 128
QL, KVL = 512, 256
FDIM = 128
CHUNK = 64
ROPE_THETA = 10000.0
EPS = 1e-6
NEG_INF = -1e30
ADAM_LR, ADAM_B1, ADAM_B2, ADAM_EPS, ADAM_WD, ADAM_STEP = 0.001, 0.9, 0.999, 1e-08, 0.01, 10

VMEM_LIMIT_BYTES = 52 * 1024 * 1024
LANES = 128
N_CHIPS = 4


def _params(sem):
    return pltpu.CompilerParams(dimension_semantics=sem, vmem_limit_bytes=VMEM_LIMIT_BYTES)


def _tile(n, target, mult):
    if n <= target:
        return n
    t = (target // mult) * mult
    while t >= mult:
        if n % t == 0:
            return t
        t -= mult
    raise ValueError(f"no tile for {n} (target {target}, multiple of {mult})")


class _Plain:
    def __init__(self, perm=None):
        self.perm = perm

    def spec(self, tr, tc, rc):
        perm = self.perm

        def imap(i, j, k):
            r, c = rc(i, j, k)
            return (r, perm(c) if perm is not None else c)

        return pl.BlockSpec((tr, tc), imap)

    def shape(self, rows, cols):
        return (rows, cols)


class _Chunked:
    def __init__(self, n):
        self.n = n

    def spec(self, tr, tc, rc):
        assert self.n % tc == 0, (self.n, tc)
        per = self.n // tc

        def imap(i, j, k):
            r, c = rc(i, j, k)
            return (c // per, r, c % per)

        return pl.BlockSpec((None, tr, tc), imap)

    def shape(self, rows, cols):
        assert cols == N_CHIPS * self.n
        return (N_CHIPS, rows, self.n)


_DIMS = {"nn": (((1,), (0,)), ((), ())), "nt": (((1,), (1,)), ((), ())), "tn": (((0,), (0,)), ((), ()))}


def _mm_single(name, mode, a, b, m, n, k, tm, tn, la, lb, lo, out_dtype, side):
    if mode == "nn":
        a_spec = la.spec(tm, k, lambda i, j, kk: (i, 0))
        b_spec = lb.spec(k, tn, lambda i, j, kk: (0, j))
    elif mode == "nt":
        a_spec = la.spec(tm, k, lambda i, j, kk: (i, 0))
        b_spec = lb.spec(tn, k, lambda i, j, kk: (j, 0))
    else:
        a_spec = la.spec(k, tm, lambda i, j, kk: (0, i))
        b_spec = lb.spec(k, tn, lambda i, j, kk: (0, j))
    o_spec = lo.spec(tm, tn, lambda i, j, kk: (i, j))
    dims = _DIMS[mode]

    def body(a_ref, b_ref, o_ref):
        o_ref[...] = lax.dot_general(a_ref[...].astype(BF16), b_ref[...].astype(BF16), dims,
                                     preferred_element_type=F32).astype(o_ref.dtype)

    (out,), got = _hosted_call(body, name, (m // tm, n // tn, 1), [a_spec, b_spec], [o_spec],
                               [jax.ShapeDtypeStruct(lo.shape(m, n), out_dtype)], (a, b), side,
                               semantics=("parallel", "parallel", "arbitrary"))
    return out if side is None else (out, got)


def _mm(name, mode, a, b, m, n, k, *, tm=1024, tn=1024, tk=2048, la=None, lb=None, lo=None, out_dtype=F32, side=None):
    la, lb, lo = la or _Plain(), lb or _Plain(), lo or _Plain()
    tm, tn, tk = _tile(m, tm, 128), _tile(n, tn, 128), _tile(k, tk, 128)
    nk = k // tk
    if nk == 1:
        return _mm_single(name, mode, a, b, m, n, k, tm, tn, la, lb, lo, out_dtype, side)
    if mode == "nn":
        a_spec = la.spec(tm, tk, lambda i, j, kk: (i, kk))
        b_spec = lb.spec(tk, tn, lambda i, j, kk: (kk, j))
    elif mode == "nt":
        a_spec = la.spec(tm, tk, lambda i, j, kk: (i, kk))
        b_spec = lb.spec(tn, tk, lambda i, j, kk: (j, kk))
    else:
        a_spec = la.spec(tk, tm, lambda i, j, kk: (kk, i))
        b_spec = lb.spec(tk, tn, lambda i, j, kk: (kk, j))
    o_spec = lo.spec(tm, tn, lambda i, j, kk: (i, j))
    dims = _DIMS[mode]

    def body(a_ref, b_ref, o_ref, acc_ref):
        kk = pl.program_id(2)

        @pl.when(kk == 0)
        def _():
            acc_ref[...] = jnp.zeros_like(acc_ref)

        acc_ref[...] += lax.dot_general(a_ref[...].astype(BF16), b_ref[...].astype(BF16), dims,
                                        preferred_element_type=F32)

        @pl.when(kk == nk - 1)
        def _():
            o_ref[...] = acc_ref[...].astype(o_ref.dtype)

    (out,), got = _hosted_call(body, name, (m // tm, n // tn, nk), [a_spec, b_spec], [o_spec],
                               [jax.ShapeDtypeStruct(lo.shape(m, n), out_dtype)], (a, b), side,
                               semantics=("parallel", "parallel", "arbitrary"), scratch=[pltpu.VMEM((tm, tn), F32)])
    return out if side is None else (out, got)


def _mm_parts(name, mode, a, b, m, n, k, *, part=1024, tm=1024, tn=1024, tk=2048, out_dtype=F32, side=None):
    parts = b if mode == "tn" else a
    widths = [p.shape[1] for p in parts]
    assert all(w % part == 0 for w in widths) and sum(widths) == (n if mode == "tn" else k)
    offs = [sum(widths[:i]) // part for i in range(len(widths))]
    nblk = [w // part for w in widths]
    if mode == "tn":
        tn, tk = part, _tile(k, tk, 128)
    else:
        tk, tn = part, _tile(n, tn, 128)
    tm = _tile(m, tm, 128)
    nk = k // tk
    grid = (m // tm, n // tn, nk)
    np_ = len(parts)

    def inside(idx, p):
        return jnp.logical_and(idx >= offs[p], idx < offs[p] + nblk[p])

    def part_spec(p):
        if mode == "tn":
            def imap(i, j, kk):
                on = inside(j, p)
                return (jnp.where(on, kk, 0), jnp.clip(j - offs[p], 0, nblk[p] - 1))
            return pl.BlockSpec((tk, tn), imap)

        def imap(i, j, kk):
            return (i, jnp.clip(kk - offs[p], 0, nblk[p] - 1))
        return pl.BlockSpec((tm, tk), imap)

    if mode == "tn":
        in_specs = [pl.BlockSpec((tk, tm), lambda i, j, kk: (kk, i))] + [part_spec(p) for p in range(np_)]
        args = [a] + list(parts)
    else:
        in_specs = [part_spec(p) for p in range(np_)] + [pl.BlockSpec((tn, tk), lambda i, j, kk: (j, kk))]
        args = list(parts) + [b]
    dims = _DIMS[mode]

    def body(*refs):
        o_ref, acc_ref = refs[-2], refs[-1]
        j, kk = pl.program_id(1), pl.program_id(2)

        @pl.when(kk == 0)
        def _():
            acc_ref[...] = jnp.zeros_like(acc_ref)

        for p in range(np_):
            @pl.when(inside(j if mode == "tn" else kk, p))
            def _(p=p):
                lhs, rhs = (refs[0], refs[1 + p]) if mode == "tn" else (refs[p], refs[np_])
                acc_ref[...] += lax.dot_general(lhs[...].astype(BF16), rhs[...].astype(BF16), dims, preferred_element_type=F32)

        @pl.when(kk == nk - 1)
        def _():
            o_ref[...] = acc_ref[...].astype(o_ref.dtype)

    (out,), got = _hosted_call(body, name, grid, in_specs, [pl.BlockSpec((tm, tn), lambda i, j, kk: (i, j))],
                               [jax.ShapeDtypeStruct((m, n), out_dtype)], args, side,
                               semantics=("parallel", "parallel", "arbitrary"), scratch=[pltpu.VMEM((tm, tn), F32)])
    return out if side is None else (out, got)


def _rows(name, fn, rows_in, vecs_in, rows_out, accs_out, n_rows, tr=256):
    tr = _tile(n_rows, tr, 16)
    nr, nv, no = len(rows_in), len(vecs_in), len(rows_out)

    def body(*refs):
        ins, vecs = refs[:nr], refs[nr:nr + nv]
        outs, accs = refs[nr + nv:nr + nv + no], refs[nr + nv + no:]
        ro, ac = fn([r[...] for r in ins], [v[...] for v in vecs])
        for o_ref, val in zip(outs, ro):
            o_ref[...] = val.astype(o_ref.dtype)
        if accs:
            @pl.when(pl.program_id(0) == 0)
            def _():
                for a_ref in accs:
                    a_ref[...] = jnp.zeros_like(a_ref)

            for a_ref, val in zip(accs, ac):
                a_ref[...] += val

    in_specs = [pl.BlockSpec((tr, cols), functools.partial(lambda i, cb: (i, cb), cb=cb)) for _, cols, cb in rows_in]
    in_specs += [pl.BlockSpec(v.shape, lambda i: (0, 0)) for v in vecs_in]
    out_specs = [pl.BlockSpec((tr, cols), lambda i: (i, 0)) for cols, _ in rows_out]
    out_specs += [pl.BlockSpec((r, cols), lambda i: (0, 0)) for r, cols in accs_out]
    out_shape = [jax.ShapeDtypeStruct((n_rows, cols), dt) for cols, dt in rows_out]
    out_shape += [jax.ShapeDtypeStruct((r, cols), F32) for r, cols in accs_out]
    res = pl.pallas_call(
        body, name=name, grid=(n_rows // tr,), in_specs=in_specs, out_specs=out_specs, out_shape=out_shape,
        compiler_params=_params(("arbitrary",)),
    )(*[a for a, _, _ in rows_in], *vecs_in)
    return res


def _colsum(v):
    return jnp.sum(v, axis=0, keepdims=True)


def _rstd(x):
    return lax.rsqrt(jnp.mean(x * x, axis=-1, keepdims=True) + EPS)


def _rms_bwd(x, g, dy):
    r = _rstd(x)
    xh = x * r
    dxh = dy * g
    dx = r * (dxh - xh * jnp.mean(dxh * xh, axis=-1, keepdims=True))
    return dx, _colsum(dy * xh)


def _sigmoid(z):
    return 1.0 / (1.0 + jnp.exp(-z))


_GELU_K = math.sqrt(2.0 / math.pi)


def _gelu_parts(g):
    t = jnp.tanh(_GELU_K * (g + 0.044715 * g * g * g))
    gel = 0.5 * g * (1.0 + t)
    dgel = 0.5 * (1.0 + t) + 0.5 * g * (1.0 - t * t) * (_GELU_K * (1.0 + 3.0 * 0.044715 * g * g))
    return gel, dgel


def _diag_visible(t, unit):
    rows = lax.broadcasted_iota(jnp.int32, (t, t), 0)
    cols = lax.broadcasted_iota(jnp.int32, (t, t), 1)
    if unit > 1:
        sh = int(math.log2(unit))
        assert 1 << sh == unit and t % unit == 0
        rows, cols = jnp.right_shift(rows, sh), jnp.right_shift(cols, sh)
    return cols <= rows


def _lane_pick(tile, lane):
    idx = lax.broadcasted_iota(jnp.int32, tile.shape, 1)
    return jnp.sum(jnp.where(idx == lane, tile, 0.0), axis=1, keepdims=True)


def _lane_put(tile, lane, col):
    idx = lax.broadcasted_iota(jnp.int32, tile.shape, 1)
    return jnp.where(idx == lane, col, tile)


def _head_cat(refs, shared, rows, h):
    hs = slice(h * LANES, (h + 1) * LANES)
    vals = [(r[rows, :] if sh else r[rows, hs]).astype(BF16) for r, sh in zip(refs, shared)]
    return vals[0] if len(vals) == 1 else jnp.concatenate(vals, axis=1)


def _blk_rows(i, t):
    return pl.ds(pl.multiple_of(i * t, t), t)


def _piece_specs(pieces, rows, row_idx):
    return [pl.BlockSpec((rows, LANES if sh else HEADS * LANES), functools.partial(lambda b, i, cb: (row_idx(b, i), cb), cb=cb))
            for _, cb, sh in pieces]


def _attn_fwd(name, qp, kp, vp, bias, unit, scale, n_seq, seq, t, side=None):
    nb = seq // t
    n_tok = n_seq * seq
    nq, nk_p = len(qp), len(kp)
    q_sh, k_sh = [p[2] for p in qp], [p[2] for p in kp]
    nbias = 2 if bias is not None else 0

    def body(*refs):
        q_refs, k_refs = refs[:nq], refs[nq:nq + nk_p]
        v_ref = refs[nq + nk_p]
        bias_refs = refs[nq + nk_p + 1:nq + nk_p + 1 + nbias]
        o_ref, lse_ref = refs[nq + nk_p + 1 + nbias:]
        qi = pl.program_id(1)
        lse_tile = jnp.zeros((t, LANES), F32)
        for h in range(HEADS):
            hs = slice(h * LANES, (h + 1) * LANES)
            q = _head_cat(q_refs, q_sh, slice(None), h)
            cq = _lane_pick(bias_refs[0][...], ROPE + h) if bias is not None else None

            def block(kb, carry, diag, h=h, hs=hs, q=q, cq=cq):
                m, l, acc = carry
                rows = _blk_rows(kb, t)
                s = lax.dot_general(q, _head_cat(k_refs, k_sh, rows, h), _DIMS["nt"], preferred_element_type=F32) * scale
                if bias is not None:
                    s = s + cq - bias_refs[1][kb, h:h + 1, :]
                if diag:
                    s = jnp.where(_diag_visible(t, unit), s, NEG_INF)
                m_new = jnp.maximum(m, jnp.max(s, axis=1, keepdims=True))
                alpha = jnp.exp(m - m_new)
                p = jnp.exp(s - m_new)
                l = alpha * l + jnp.sum(p, axis=1, keepdims=True)
                acc = alpha * acc + jnp.dot(p.astype(BF16), v_ref[rows, hs].astype(BF16), preferred_element_type=F32)
                return m_new, l, acc

            init = (jnp.full((t, 1), NEG_INF, F32), jnp.zeros((t, 1), F32), jnp.zeros((t, LANES), F32))
            carry = lax.fori_loop(0, qi, lambda kb, c: block(kb, c, False), init)
            m, l, acc = block(qi, carry, True)
            o_ref[:, hs] = acc / l
            lse_tile = _lane_put(lse_tile, h, m + jnp.log(l))
        lse_ref[...] = lse_tile

    tile_row = lambda b, i: b * nb + i
    seq_row = lambda b, i: b
    lane_tile = pl.BlockSpec((t, LANES), lambda b, i: (b * nb + i, 0))
    in_specs = _piece_specs(qp, t, tile_row) + _piece_specs(kp, seq, seq_row) + _piece_specs([vp + (False,)], seq, seq_row)
    args = [p[0] for p in qp] + [p[0] for p in kp] + [vp[0]]
    if bias is not None:
        in_specs += [lane_tile, pl.BlockSpec((None, nb, HEADS, t), lambda b, i: (b, 0, 0, 0))]
        args += list(bias)
    return _hosted_call(
        body, name, (n_seq, nb), in_specs,
        [pl.BlockSpec((t, HEADS * LANES), lambda b, i: (b * nb + i, 0)), lane_tile],
        [jax.ShapeDtypeStruct((n_tok, HEADS * LANES), F32), jax.ShapeDtypeStruct((n_tok, LANES), F32)], args, side)


def _attn_bwd_dq(name, qp, kp, vp, o, do, lse, bias, unit, scale, n_seq, seq, t, side=None, grad_dtype=F32):
    nb = seq // t
    n_tok = n_seq * seq
    nq, nk_p = len(qp), len(kp)
    q_sh, k_sh = [p[2] for p in qp], [p[2] for p in kp]
    nbias = 2 if bias is not None else 0
    n_in = nq + nk_p + 4 + nbias

    def body(*refs):
        q_refs, k_refs = refs[:nq], refs[nq:nq + nk_p]
        v_ref, o_ref, do_ref, lse_ref = refs[nq + nk_p:nq + nk_p + 4]
        bias_refs = refs[nq + nk_p + 4:n_in]
        dq_refs = refs[n_in:n_in + nq]
        delta_ref, dob_ref = refs[n_in + nq:n_in + nq + 2]
        qi = pl.program_id(1)
        delta_tile = jnp.zeros((t, LANES), F32)
        dc_tile = jnp.zeros((t, LANES), F32)
        lse_all = lse_ref[...]
        for h in range(HEADS):
            hs = slice(h * LANES, (h + 1) * LANES)
            q = _head_cat(q_refs, q_sh, slice(None), h)
            do_f = do_ref[:, hs]
            do_b = do_f.astype(BF16)
            dob_ref[:, hs] = do_b
            delta = jnp.sum(do_f * o_ref[:, hs], axis=1, keepdims=True)
            lse = _lane_pick(lse_all, h)
            cq = _lane_pick(bias_refs[0][...], ROPE + h) if bias is not None else None

            def block(kb, carry, diag, h=h, hs=hs, q=q, cq=cq, do_b=do_b, delta=delta, lse=lse):
                dq_acc, dc_acc = carry
                rows = _blk_rows(kb, t)
                k = _head_cat(k_refs, k_sh, rows, h)
                s = lax.dot_general(q, k, _DIMS["nt"], preferred_element_type=F32) * scale
                if bias is not None:
                    s = s + cq - bias_refs[1][kb, h:h + 1, :]
                if diag:
                    s = jnp.where(_diag_visible(t, unit), s, NEG_INF)
                p = jnp.exp(s - lse)
                dp = lax.dot_general(do_b, v_ref[rows, hs].astype(BF16), _DIMS["nt"], preferred_element_type=F32)
                ds = p * (dp - delta)
                return (dq_acc + jnp.dot(ds.astype(BF16), k, preferred_element_type=F32),
                        dc_acc + jnp.sum(ds, axis=1, keepdims=True))

            init = (jnp.zeros((t, nq * LANES), F32), jnp.zeros((t, 1), F32))
            carry = lax.fori_loop(0, qi, lambda kb, c: block(kb, c, False), init)
            dq_acc, dc_acc = block(qi, carry, True)
            for n_p in range(nq):
                dq_refs[n_p][:, hs] = (dq_acc[:, n_p * LANES:(n_p + 1) * LANES] * scale).astype(grad_dtype)
            delta_tile = _lane_put(delta_tile, h, delta)
            dc_tile = _lane_put(dc_tile, ROPE + h, dc_acc)
        delta_ref[...] = delta_tile
        if bias is not None:
            refs[n_in + nq + 2][...] = dc_tile

    tile_row = lambda b, i: b * nb + i
    seq_row = lambda b, i: b
    lane_tile = pl.BlockSpec((t, LANES), lambda b, i: (b * nb + i, 0))
    head_tile = pl.BlockSpec((t, HEADS * LANES), lambda b, i: (b * nb + i, 0))
    in_specs = _piece_specs(qp, t, tile_row) + _piece_specs(kp, seq, seq_row) + _piece_specs([vp + (False,)], seq, seq_row)
    in_specs += [head_tile, head_tile, lane_tile]
    args = [p[0] for p in qp] + [p[0] for p in kp] + [vp[0], o, do, lse]
    if bias is not None:
        in_specs += [lane_tile, pl.BlockSpec((None, nb, HEADS, t), lambda b, i: (b, 0, 0, 0))]
        args += list(bias)
    out_specs = [head_tile] * nq + [lane_tile, head_tile] + ([lane_tile] if bias is not None else [])
    out_shape = [jax.ShapeDtypeStruct((n_tok, HEADS * LANES), grad_dtype)] * nq
    out_shape += [jax.ShapeDtypeStruct((n_tok, LANES), F32), jax.ShapeDtypeStruct((n_tok, HEADS * LANES), BF16)]
    if bias is not None:
        out_shape.append(jax.ShapeDtypeStruct((n_tok, LANES), F32))
    return _hosted_call(body, name, (n_seq, nb), in_specs, out_specs, out_shape, args, side)


def _attn_bwd_dkv(name, qp, kp, vp, dob, lse, delta, bias, unit, scale, n_seq, seq, t, side=None, grad_dtype=F32):
    nb = seq // t
    n_tok = n_seq * seq
    nq, nk_p = len(qp), len(kp)
    q_sh, k_sh = [p[2] for p in qp], [p[2] for p in kp]
    nbias = 2 if bias is not None else 0
    n_in = nq + nk_p + 4 + nbias

    def body(*refs):
        q_refs, k_refs = refs[:nq], refs[nq:nq + nk_p]
        v_ref, dob_ref, lse_ref, delta_ref = refs[nq + nk_p:nq + nk_p + 4]
        bias_refs = refs[nq + nk_p + 4:n_in]
        dk_refs = refs[n_in:n_in + nk_p]
        dv_ref = refs[n_in + nk_p]
        ki = pl.program_id(1)
        shared_acc = [jnp.zeros((t, LANES), F32) for _ in range(nk_p)]
        for h in range(HEADS):
            hs = slice(h * LANES, (h + 1) * LANES)
            k = _head_cat(k_refs, k_sh, slice(None), h)
            v = v_ref[:, hs].astype(BF16)
            ck = bias_refs[1][h:h + 1, :] if bias is not None else None

            def block(qb, carry, diag, h=h, hs=hs, k=k, v=v, ck=ck):
                dk_acc, dv_acc, dc_acc = carry
                rows = _blk_rows(qb, t)
                q = _head_cat(q_refs, q_sh, rows, h)
                s = lax.dot_general(q, k, _DIMS["nt"], preferred_element_type=F32) * scale
                if bias is not None:
                    s = s + _lane_pick(bias_refs[0][rows, :], ROPE + h) - ck
                if diag:
                    s = jnp.where(_diag_visible(t, unit), s, NEG_INF)
                p = jnp.exp(s - _lane_pick(lse_ref[rows, :], h))
                do_b = dob_ref[rows, hs]
                dp = lax.dot_general(do_b, v, _DIMS["nt"], preferred_element_type=F32)
                ds = p * (dp - _lane_pick(delta_ref[rows, :], h))
                return (dk_acc + lax.dot_general(ds.astype(BF16), q, _DIMS["tn"], preferred_element_type=F32),
                        dv_acc + lax.dot_general(p.astype(BF16), do_b, _DIMS["tn"], preferred_element_type=F32),
                        dc_acc - jnp.sum(ds, axis=0, keepdims=True))

            init = (jnp.zeros((t, nk_p * LANES), F32), jnp.zeros((t, LANES), F32), jnp.zeros((1, t), F32))
            carry = block(ki, init, True)
            dk_acc, dv_acc, dc_acc = lax.fori_loop(ki + 1, nb, lambda qb, c: block(qb, c, False), carry)
            for n_p in range(nk_p):
                part = dk_acc[:, n_p * LANES:(n_p + 1) * LANES] * scale
                if k_sh[n_p]:
                    shared_acc[n_p] = shared_acc[n_p] + part
                else:
                    dk_refs[n_p][:, hs] = part.astype(grad_dtype)
            dv_ref[:, hs] = dv_acc.astype(grad_dtype)
            if bias is not None:
                refs[n_in + nk_p + 1][h:h + 1, :] = dc_acc
        for n_p in range(nk_p):
            if k_sh[n_p]:
                dk_refs[n_p][...] = shared_acc[n_p]

    tile_row = lambda b, i: b * nb + i
    seq_row = lambda b, i: b
    lane_seq = pl.BlockSpec((seq, LANES), lambda b, i: (b, 0))
    head_tile = pl.BlockSpec((t, HEADS * LANES), lambda b, i: (b * nb + i, 0))
    row_tile = pl.BlockSpec((None, None, HEADS, t), lambda b, i: (b, i, 0, 0))
    in_specs = _piece_specs(qp, seq, seq_row) + _piece_specs(kp, t, tile_row) + _piece_specs([vp + (False,)], t, tile_row)
    in_specs += [pl.BlockSpec((seq, HEADS * LANES), lambda b, i: (b, 0)), lane_seq, lane_seq]
    args = [p[0] for p in qp] + [p[0] for p in kp] + [vp[0], dob, lse, delta]
    if bias is not None:
        in_specs += [lane_seq, row_tile]
        args += list(bias)
    out_specs = [pl.BlockSpec((t, LANES if sh else HEADS * LANES), lambda b, i: (b * nb + i, 0)) for sh in k_sh] + [head_tile]
    out_shape = [jax.ShapeDtypeStruct((n_tok, LANES), F32) if sh else jax.ShapeDtypeStruct((n_tok, HEADS * LANES), grad_dtype)
                 for sh in k_sh]
    out_shape.append(jax.ShapeDtypeStruct((n_tok, HEADS * LANES), grad_dtype))
    if bias is not None:
        out_specs.append(row_tile)
        out_shape.append(jax.ShapeDtypeStruct((n_seq, nb, HEADS, t), F32))
    return _hosted_call(body, name, (n_seq, nb), in_specs, out_specs, out_shape, args, side)


HEAD_GROUPS = 2


def _attn_delta(name, o, do, n_tok):
    def fn(r, v):
        o_v, do_v = r
        tile = jnp.zeros((o_v.shape[0], LANES), F32)
        for h in range(HEADS):
            hs = slice(h * LANES, (h + 1) * LANES)
            tile = _lane_put(tile, h, jnp.sum(do_v[:, hs] * o_v[:, hs], axis=1, keepdims=True))
        return [tile, do_v], []

    return _rows(name, fn, [(o, HEADS * LANES, 0), (do, HEADS * LANES, 0)], [], [(LANES, F32), (HEADS * LANES, BF16)], [], n_tok)


def _attn_bwd(name, qp, kp, vp, dob, lse, delta, bias, unit, scale, n_seq, seq, t, grad_dtype, side=None):
    nb = seq // t
    n_tok = n_seq * seq
    ng = HEAD_GROUPS
    hg = HEADS // ng
    gw = hg * LANES
    nq, nk_p = len(qp), len(kp)
    q_sh, k_sh = [p[2] for p in qp], [p[2] for p in kp]
    assert not any(q_sh) and nq == nk_p
    nbias = 2 if bias is not None else 0
    n_in = nq + nk_p + 4 + nbias
    n_out = nq + nk_p + 1 + nbias

    def body(*refs):
        q_refs, k_refs = refs[:nq], refs[nq:nq + nk_p]
        v_ref, dob_ref, lse_ref, delta_ref = refs[nq + nk_p:nq + nk_p + 4]
        bias_refs = refs[nq + nk_p + 4:n_in]
        dq_refs, dk_refs = refs[n_in:n_in + nq], refs[n_in + nq:n_in + nq + nk_p]
        dv_ref = refs[n_in + nq + nk_p]
        dq_s, dcq_s = refs[n_in + n_out:]
        g, ki = pl.program_id(1), pl.program_id(2)

        @pl.when(ki == 0)
        def _():
            dq_s[...] = jnp.zeros_like(dq_s)
            dcq_s[...] = jnp.zeros_like(dcq_s)

        shared_acc = [jnp.zeros((t, LANES), F32) for _ in range(nk_p)]
        for hl in range(hg):
            h = g * hg + hl
            hs = slice(hl * LANES, (hl + 1) * LANES)
            k = _head_cat(k_refs, k_sh, slice(None), hl)
            v = v_ref[:, hs].astype(BF16)
            ck = bias_refs[1][pl.ds(h, 1), :] if bias is not None else None

            def block(qb, carry, diag, h=h, hl=hl, hs=hs, k=k, v=v, ck=ck):
                dk_acc, dv_acc, dc_acc = carry
                rows = _blk_rows(qb, t)
                q = _head_cat(q_refs, q_sh, rows, hl)
                s = lax.dot_general(q, k, _DIMS["nt"], preferred_element_type=F32) * scale
                if bias is not None:
                    s = s + _lane_pick(bias_refs[0][rows, :], ROPE + h) - ck
                if diag:
                    s = jnp.where(_diag_visible(t, unit), s, NEG_INF)
                p = jnp.exp(s - _lane_pick(lse_ref[rows, :], h))
                do_b = dob_ref[rows, hs]
                dp = lax.dot_general(do_b, v, _DIMS["nt"], preferred_element_type=F32)
                ds = p * (dp - _lane_pick(delta_ref[rows, :], h))
                ds_b = ds.astype(BF16)
                dq_blk = jnp.dot(ds_b, k, preferred_element_type=F32)
                for n_p in range(nq):
                    dq_s[rows, n_p * gw + hl * LANES:n_p * gw + (hl + 1) * LANES] += dq_blk[:, n_p * LANES:(n_p + 1) * LANES]
                if bias is not None:
                    lane = lax.broadcasted_iota(jnp.int32, (t, LANES), 1)
                    dcq_s[rows, :] += jnp.where(lane == ROPE + h, jnp.sum(ds, axis=1, keepdims=True), 0.0)
                return (dk_acc + lax.dot_general(ds_b, q, _DIMS["tn"], preferred_element_type=F32),
                        dv_acc + lax.dot_general(p.astype(BF16), do_b, _DIMS["tn"], preferred_element_type=F32),
                        dc_acc - jnp.sum(ds, axis=0, keepdims=True))

            init = (jnp.zeros((t, nk_p * LANES), F32), jnp.zeros((t, LANES), F32), jnp.zeros((1, t), F32))
            carry = block(ki, init, True)
            dk_acc, dv_acc, dc_acc = lax.fori_loop(ki + 1, nb, lambda qb, c: block(qb, c, False), carry)
            for n_p in range(nk_p):
                part = dk_acc[:, n_p * LANES:(n_p + 1) * LANES] * scale
                if k_sh[n_p]:
                    shared_acc[n_p] = shared_acc[n_p] + part
                else:
                    dk_refs[n_p][:, hs] = part.astype(grad_dtype)
            dv_ref[:, hs] = dv_acc.astype(grad_dtype)
            if bias is not None:
                refs[n_in + n_out - 1][hl:hl + 1, :] = dc_acc
        for n_p in range(nk_p):
            if k_sh[n_p]:
                dk_refs[n_p][...] = shared_acc[n_p]

        @pl.when(ki == nb - 1)
        def _():
            for n_p in range(nq):
                dq_refs[n_p][...] = (dq_s[:, n_p * gw:(n_p + 1) * gw] * scale).astype(grad_dtype)
            if bias is not None:
                refs[n_in + n_out - 2][...] = dcq_s[...]

    def spec(rows, row_idx, cb, shared):
        if shared:
            return pl.BlockSpec((rows, LANES), lambda b, g, i: (row_idx(b, i), cb))
        return pl.BlockSpec((rows, gw), lambda b, g, i: (row_idx(b, i), cb * ng + g))

    tile_row = lambda b, i: b * nb + i
    seq_row = lambda b, i: b
    lane_seq = pl.BlockSpec((seq, LANES), lambda b, g, i: (b, 0))
    in_specs = [spec(seq, seq_row, cb, sh) for _, cb, sh in qp] + [spec(t, tile_row, cb, sh) for _, cb, sh in kp]
    in_specs += [spec(t, tile_row, vp[1], False), spec(seq, seq_row, 0, False), lane_seq, lane_seq]
    args = [p[0] for p in qp] + [p[0] for p in kp] + [vp[0], dob, lse, delta]
    if bias is not None:
        in_specs += [lane_seq, pl.BlockSpec((None, None, HEADS, t), lambda b, g, i: (b, i, 0, 0))]
        args += list(bias)
    group_tile = pl.BlockSpec((None, t, LANES), lambda b, g, i: (g, b * nb + i, 0))
    out_specs = [spec(seq, seq_row, 0, False)] * nq
    out_specs += [group_tile if sh else spec(t, tile_row, 0, False) for sh in k_sh] + [spec(t, tile_row, 0, False)]
    head_shape = jax.ShapeDtypeStruct((n_tok, HEADS * LANES), grad_dtype)
    out_shape = [head_shape] * nq + [jax.ShapeDtypeStruct((ng, n_tok, LANES), F32) if sh else head_shape for sh in k_sh]
    out_shape.append(head_shape)
    if bias is not None:
        out_specs += [pl.BlockSpec((None, seq, LANES), lambda b, g, i: (g, b, 0)),
                      pl.BlockSpec((None, None, None, hg, t), lambda b, g, i: (b, g, i, 0, 0))]
        out_shape += [jax.ShapeDtypeStruct((ng, n_tok, LANES), F32), jax.ShapeDtypeStruct((n_seq, ng, nb, hg, t), F32)]
    return _hosted_call(body, name, (n_seq, ng, nb), in_specs, out_specs, out_shape, args, side,
                        semantics=("parallel", "arbitrary", "arbitrary"),
                        scratch=[pltpu.VMEM((seq, nq * gw), F32), pltpu.VMEM((seq, LANES), F32)])


def _old_attn_bwd_dq(name, qp, kp, vp, o, do, lse, bias, unit, scale, n_seq, seq, t):
    nb = seq // t
    n_tok = n_seq * seq
    nq, nk_p = len(qp), len(kp)
    nbias = 2 if bias is not None else 0
    n_in = nq + nk_p + 4 + nbias
    n_out = nq + (1 if bias is not None else 0)

    def body(*refs):
        q_refs, k_refs = refs[:nq], refs[nq:nq + nk_p]
        v_ref, o_ref, do_ref, lse_ref = refs[nq + nk_p:nq + nk_p + 4]
        bias_refs = refs[nq + nk_p + 4:n_in]
        outs = refs[n_in:n_in + n_out]
        dq_s, delta_s, dc_s = refs[n_in + n_out:]
        qi, ki = pl.program_id(2), pl.program_id(3)

        @pl.when(ki == 0)
        def _():
            dq_s[...] = jnp.zeros_like(dq_s)
            dc_s[...] = jnp.zeros_like(dc_s)
            delta_s[...] = jnp.sum(do_ref[...] * o_ref[...], axis=1, keepdims=True)

        @pl.when(ki <= qi)
        def _():
            s = _scores(q_refs, k_refs, bias_refs, qi, ki, t, unit, scale)
            p = jnp.exp(s - lse_ref[...])
            dp = lax.dot_general(do_ref[...].astype(BF16), v_ref[...].astype(BF16), _DIMS["nt"],
                                 preferred_element_type=F32)
            ds = p * (dp - delta_s[...])
            dq_s[...] += jnp.dot(ds.astype(BF16), _cat(k_refs), preferred_element_type=F32)
            dc_s[...] += jnp.sum(ds, axis=1, keepdims=True)

        @pl.when(ki == qi)
        def _():
            for n_p in range(nq):
                outs[n_p][...] = dq_s[:, n_p * LANES:(n_p + 1) * LANES] * scale
            if bias is not None:
                outs[nq][...] = dc_s[...]

    q_row = lambda b, i, j: b * nb + i
    k_row = lambda b, i, j: b * nb + jnp.minimum(j, i)
    head_q = pl.BlockSpec((t, LANES), lambda b, h, i, j: (b * nb + i, h))
    col_q = pl.BlockSpec((None, t, 1), lambda b, h, i, j: (h, b * nb + i, 0))
    in_specs = [_piece_spec(t, p, q_row) for p in qp] + [_piece_spec(t, p, k_row) for p in kp]
    in_specs += [_piece_spec(t, vp, k_row), head_q, head_q, col_q]
    args = [p[0] for p in qp] + [p[0] for p in kp] + [vp[0], o, do, lse]
    if bias is not None:
        in_specs += [col_q, pl.BlockSpec((None, 1, t), lambda b, h, i, j: (b * HEADS + h, 0, jnp.minimum(j, i)))]
        args += list(bias)
    out_specs = [head_q] * nq + ([col_q] if bias is not None else [])
    out_shape = [jax.ShapeDtypeStruct((n_tok, HEADS * LANES), F32)] * nq
    if bias is not None:
        out_shape.append(jax.ShapeDtypeStruct((HEADS, n_tok, 1), F32))
    return pl.pallas_call(
        body, name=name, grid=(n_seq, HEADS, nb, nb), in_specs=in_specs, out_specs=out_specs, out_shape=out_shape,
        scratch_shapes=[pltpu.VMEM((t, nq * LANES), F32), pltpu.VMEM((t, 1), F32), pltpu.VMEM((t, 1), F32)],
        compiler_params=_params(("parallel", "parallel", "arbitrary", "arbitrary")),
    )(*args)


def _old_attn_bwd_dkv(name, qp, kp, vp, o, do, lse, bias, unit, scale, n_seq, seq, t):
    nb = seq // t
    n_tok = n_seq * seq
    nq, nk_p = len(qp), len(kp)
    nbias = 2 if bias is not None else 0
    n_in = nq + nk_p + 4 + nbias
    n_out = nk_p + 1 + (1 if bias is not None else 0)

    def body(*refs):
        q_refs, k_refs = refs[:nq], refs[nq:nq + nk_p]
        v_ref, o_ref, do_ref, lse_ref = refs[nq + nk_p:nq + nk_p + 4]
        bias_refs = refs[nq + nk_p + 4:n_in]
        outs = refs[n_in:n_in + n_out]
        dk_s, dv_s, dc_s = refs[n_in + n_out:]
        ki, qi = pl.program_id(2), pl.program_id(3)

        @pl.when(qi == 0)
        def _():
            dk_s[...] = jnp.zeros_like(dk_s)
            dv_s[...] = jnp.zeros_like(dv_s)
            dc_s[...] = jnp.zeros_like(dc_s)

        @pl.when(qi >= ki)
        def _():
            s = _scores(q_refs, k_refs, bias_refs, qi, ki, t, unit, scale)
            p = jnp.exp(s - lse_ref[...])
            do_b = do_ref[...].astype(BF16)
            delta = jnp.sum(do_ref[...] * o_ref[...], axis=1, keepdims=True)
            dp = lax.dot_general(do_b, v_ref[...].astype(BF16), _DIMS["nt"], preferred_element_type=F32)
            ds = p * (dp - delta)
            dv_s[...] += lax.dot_general(p.astype(BF16), do_b, _DIMS["tn"], preferred_element_type=F32)
            dk_s[...] += lax.dot_general(ds.astype(BF16), _cat(q_refs), _DIMS["tn"], preferred_element_type=F32)
            dc_s[...] -= jnp.sum(ds, axis=0, keepdims=True)

        @pl.when(qi == nb - 1)
        def _():
            for n_p in range(nk_p):
                outs[n_p][...] = dk_s[:, n_p * LANES:(n_p + 1) * LANES] * scale
            outs[nk_p][...] = dv_s[...]
            if bias is not None:
                outs[nk_p + 1][...] = dc_s[...]

    q_row = lambda b, i, j: b * nb + jnp.maximum(j, i)
    k_row = lambda b, i, j: b * nb + i
    head_q = pl.BlockSpec((t, LANES), lambda b, h, i, j: (b * nb + jnp.maximum(j, i), h))
    col_q = pl.BlockSpec((None, t, 1), lambda b, h, i, j: (h, b * nb + jnp.maximum(j, i), 0))
    head_k = pl.BlockSpec((t, LANES), lambda b, h, i, j: (b * nb + i, h))
    row_k = pl.BlockSpec((None, 1, t), lambda b, h, i, j: (b * HEADS + h, 0, i))
    in_specs = [_piece_spec(t, p, q_row) for p in qp] + [_piece_spec(t, p, k_row) for p in kp]
    in_specs += [_piece_spec(t, vp, k_row), head_q, head_q, col_q]
    args = [p[0] for p in qp] + [p[0] for p in kp] + [vp[0], o, do, lse]
    if bias is not None:
        in_specs += [col_q, row_k]
        args += list(bias)
    out_specs = [head_k] * (nk_p + 1) + ([row_k] if bias is not None else [])
    out_shape = [jax.ShapeDtypeStruct((n_tok, HEADS * LANES), F32)] * (nk_p + 1)
    if bias is not None:
        out_shape.append(jax.ShapeDtypeStruct((n_seq * HEADS, 1, seq), F32))
    return pl.pallas_call(
        body, name=name, grid=(n_seq, HEADS, nb, nb), in_specs=in_specs, out_specs=out_specs, out_shape=out_shape,
        scratch_shapes=[pltpu.VMEM((t, nk_p * LANES), F32), pltpu.VMEM((t, LANES), F32), pltpu.VMEM((1, t), F32)],
        compiler_params=_params(("parallel", "parallel", "arbitrary", "arbitrary")),
    )(*args)


def _seq_cumsum(name, x, col_block, n_seq, seq, reverse, pre=None, vec=None):
    t = _tile(seq, 256, 128)
    nb = seq // t

    def body(*refs):
        x_ref = refs[0]
        vec_ref = refs[1] if vec is not None else None
        o_ref, carry = refs[-2], refs[-1]

        @pl.when(pl.program_id(1) == 0)
        def _():
            carry[...] = jnp.zeros_like(carry)

        v = x_ref[...]
        if pre is not None:
            v = pre(v, vec_ref[...])
        r = lax.broadcasted_iota(jnp.int32, (t, t), 0)
        c = lax.broadcasted_iota(jnp.int32, (t, t), 1)
        tri = jnp.where((c >= r) if reverse else (c <= r), 1.0, 0.0).astype(BF16)
        hi = v.astype(BF16)
        mid = (v - hi.astype(F32)).astype(BF16)
        lo = (v - hi.astype(F32) - mid.astype(F32)).astype(BF16)
        acc = jnp.dot(tri, hi, preferred_element_type=F32)
        acc += jnp.dot(tri, mid, preferred_element_type=F32)
        acc += jnp.dot(tri, lo, preferred_element_type=F32)
        o_ref[...] = acc + carry[...]
        carry[...] += _colsum(v)

    blk = (lambda b, i: (b * nb + nb - 1 - i)) if reverse else (lambda b, i: (b * nb + i))
    in_specs = [pl.BlockSpec((t, LANES), lambda b, i: (blk(b, i), col_block))]
    args = [x]
    if vec is not None:
        in_specs.append(pl.BlockSpec(vec.shape, lambda b, i: (0, 0)))
        args.append(vec)
    return pl.pallas_call(
        body, name=name, grid=(n_seq, nb), in_specs=in_specs,
        out_specs=pl.BlockSpec((t, LANES), lambda b, i: (blk(b, i), 0)),
        out_shape=jax.ShapeDtypeStruct((n_seq * seq, LANES), F32),
        scratch_shapes=[pltpu.VMEM((1, LANES), F32)],
        compiler_params=_params(("arbitrary", "arbitrary")),
    )(*args)


def _log_sigmoid(z):
    return -(jnp.maximum(-z, 0.0) + jnp.log(1.0 + jnp.exp(-jnp.abs(z))))


def _shift_down(u, prev_ref, n):
    out = pltpu.roll(u, n, 0)
    row = lax.broadcasted_iota(jnp.int32, u.shape, 0)
    for r in range(n):
        out = jnp.where(row == r, prev_ref[8 - n + r:8 - n + r + 1, :], out)
    return out


def _shift_up(u, next_ref, n):
    ts = u.shape[0]
    out = pltpu.roll(u, ts - n, 0)
    row = lax.broadcasted_iota(jnp.int32, u.shape, 0)
    for r in range(n):
        out = jnp.where(row == ts - n + r, next_ref[r:r + 1, :], out)
    return out


def _conv_taps(u, prev_ref, w_ref, b_ref):
    s1, s2 = _shift_down(u, prev_ref, 1), _shift_down(u, prev_ref, 2)
    return (w_ref[0:1, :] * s2 + w_ref[1:2, :] * s1 + w_ref[2:3, :] * u) + b_ref[...], s1, s2


def _conv_glu_fwd(u_il, cw_il, cb_il, n_seq, seq, wt):
    n_tok, two_f = u_il.shape
    nct = two_f // (2 * wt)
    ts = _tile(seq, 256, 8)
    ns = seq // ts

    def body(u_ref, w_ref, b_ref, a_ref, carry):
        @pl.when(pl.program_id(2) == 0)
        def _():
            carry[...] = jnp.zeros_like(carry)

        u = u_ref[...].astype(F32)
        uc, _, _ = _conv_taps(u, carry, w_ref, b_ref)
        gel, _ = _gelu_parts(uc[:, :wt])
        a_ref[...] = (gel * uc[:, wt:]).astype(a_ref.dtype)
        carry[...] = u[ts - 8:, :]

    return pl.pallas_call(
        body, name="conv_glu_fwd", grid=(nct, n_seq, ns),
        in_specs=[pl.BlockSpec((ts, 2 * wt), lambda j, b, s: (b * ns + s, j)),
                  pl.BlockSpec((3, 2 * wt), lambda j, b, s: (0, j)),
                  pl.BlockSpec((1, 2 * wt), lambda j, b, s: (0, j))],
        out_specs=pl.BlockSpec((ts, wt), lambda j, b, s: (b * ns + s, j)),
        out_shape=jax.ShapeDtypeStruct((n_tok, two_f // 2), BF16),
        scratch_shapes=[pltpu.VMEM((8, 2 * wt), F32)],
        compiler_params=_params(("parallel", "arbitrary", "arbitrary")),
    )(u_il, cw_il, cb_il)


def _conv_glu_bwd_pre(u_il, da, cw_il, cb_il, n_seq, seq, wt):
    n_tok, two_f = u_il.shape
    nct = two_f // (2 * wt)
    ts = _tile(seq, 256, 8)
    ns = seq // ts

    def body(u_ref, da_ref, w_ref, b_ref, d_ref, acc_ref, carry):
        first = jnp.logical_and(pl.program_id(1) == 0, pl.program_id(2) == 0)

        @pl.when(first)
        def _():
            acc_ref[...] = jnp.zeros_like(acc_ref)

        @pl.when(pl.program_id(2) == 0)
        def _():
            carry[...] = jnp.zeros_like(carry)

        u = u_ref[...].astype(F32)
        uc, s1, s2 = _conv_taps(u, carry, w_ref, b_ref)
        gel, dgel = _gelu_parts(uc[:, :wt])
        da_v = da_ref[...].astype(F32)
        d = jnp.concatenate([da_v * uc[:, wt:] * dgel, da_v * gel], axis=1)
        d_ref[...] = d.astype(d_ref.dtype)
        acc_ref[0:1, :] += _colsum(d * s2)
        acc_ref[1:2, :] += _colsum(d * s1)
        acc_ref[2:3, :] += _colsum(d * u)
        acc_ref[3:4, :] += _colsum(d)
        carry[...] = u[ts - 8:, :]

    return pl.pallas_call(
        body, name="conv_glu_bwd_pre", grid=(nct, n_seq, ns),
        in_specs=[pl.BlockSpec((ts, 2 * wt), lambda j, b, s: (b * ns + s, j)),
                  pl.BlockSpec((ts, wt), lambda j, b, s: (b * ns + s, j)),
                  pl.BlockSpec((3, 2 * wt), lambda j, b, s: (0, j)),
                  pl.BlockSpec((1, 2 * wt), lambda j, b, s: (0, j))],
        out_specs=[pl.BlockSpec((ts, 2 * wt), lambda j, b, s: (b * ns + s, j)),
                   pl.BlockSpec((8, 2 * wt), lambda j, b, s: (0, j))],
        out_shape=[jax.ShapeDtypeStruct((n_tok, two_f), BF16), jax.ShapeDtypeStruct((8, two_f), F32)],
        scratch_shapes=[pltpu.VMEM((8, 2 * wt), F32)],
        compiler_params=_params(("parallel", "arbitrary", "arbitrary")),
    )(u_il, da, cw_il, cb_il)


def _conv_bwd_input(d_il, cw_il, n_seq, seq, wt):
    n_tok, two_f = d_il.shape
    nct = two_f // (2 * wt)
    ts = _tile(seq, 256, 8)
    ns = seq // ts

    def body(d_ref, w_ref, o_ref, carry):
        @pl.when(pl.program_id(2) == 0)
        def _():
            carry[...] = jnp.zeros_like(carry)

        d = d_ref[...].astype(F32)
        o_ref[...] = (w_ref[2:3, :] * d + w_ref[1:2, :] * _shift_up(d, carry, 1)
                      + w_ref[0:1, :] * _shift_up(d, carry, 2)).astype(o_ref.dtype)
        carry[...] = d[:8, :]

    rev = lambda j, b, s: (b * ns + ns - 1 - s, j)
    return pl.pallas_call(
        body, name="conv_bwd_input", grid=(nct, n_seq, ns),
        in_specs=[pl.BlockSpec((ts, 2 * wt), rev), pl.BlockSpec((3, 2 * wt), lambda j, b, s: (0, j))],
        out_specs=pl.BlockSpec((ts, 2 * wt), rev),
        out_shape=jax.ShapeDtypeStruct((n_tok, two_f), BF16),
        scratch_shapes=[pltpu.VMEM((8, 2 * wt), F32)],
        compiler_params=_params(("parallel", "arbitrary", "arbitrary")),
    )(d_il, cw_il)


HBM = pl.BlockSpec(memory_space=pltpu.HBM)
_CHIP_FLIPS = ((1, 0), (0, 1), (1, 1))


def _place():
    x, y, c = lax.axis_index("x"), lax.axis_index("y"), lax.axis_index("c")
    return x, y, c, 2 * x + y


def _flip(v, f):
    return 1 - v if f else v


def _half_rows(c, half):
    return pl.ds(pl.multiple_of(c * half, 16), half)


def _remote(src, dst, ssem, rsem, dev):
    return pltpu.make_async_remote_copy(src_ref=src, dst_ref=dst, send_sem=ssem, recv_sem=rsem,
                                        device_id=dev, device_id_type=MESH)


def _comm_call(name, body, ins, out_shapes, n_sems):
    return pl.pallas_call(
        body, name=name, in_specs=[HBM] * len(ins), out_specs=[HBM] * len(out_shapes),
        out_shape=[pltpu.HBM(s.shape, s.dtype) for s in out_shapes],
        scratch_shapes=[pltpu.SemaphoreType.DMA((n_sems,)), pltpu.SemaphoreType.DMA((n_sems,))],
    )(*ins)


def _all_gather_weights(shards, smalls):
    side = _gather_side(shards, smalls)
    nt = len(shards) + len(smalls)

    def body(*refs):
        for part in (side.start, side.mid, side.end):
            part(refs[:nt], refs[nt:2 * nt], *refs[2 * nt:])

    res = _comm_call("all_gather_weights", body, side.ins, side.outs, side.n_sems)
    return res[:len(shards)], res[len(shards):]


def _pair_split(name, grads):
    n = len(grads)

    def body(*refs):
        src, got = refs[:n], refs[n:2 * n]
        ssem, rsem = refs[2 * n:]
        x, y, c, _ = _place()
        cps = []
        for w in range(n):
            half = grads[w].shape[1] // 2
            cp = _remote(src[w].at[:, _half_rows(1 - c, half)], got[w], ssem.at[w], rsem.at[w], (x, y, 1 - c))
            cp.start()
            cps.append(cp)
        for cp in cps:
            cp.wait()

    outs = [jax.ShapeDtypeStruct((g.shape[0], g.shape[1] // 2, g.shape[2]), g.dtype) for g in grads]
    return _comm_call(name, body, grads, outs, n)


def _chip_scatter(parts):
    side = _scatter_side(parts)

    def body(*refs):
        n = len(parts)
        side.start(refs[:n], refs[n:2 * n], *refs[2 * n:])
        side.end(refs[:n], refs[n:2 * n], *refs[2 * n:])

    return _comm_call("rs_chip_scatter", body, parts, side.outs, side.n_sems)


class _Side:
    def __init__(self, ins, outs, n_sems, start, mid, end, mid_step=None):
        self.ins, self.outs, self.n_sems = list(ins), list(outs), n_sems
        self.start, self.mid, self.end, self.mid_step = start, mid, end, mid_step
        self.aliases = {}


def _scatter_side(parts):
    n = len(parts)

    def copies(src, dst, ssem, rsem):
        x, y, c, _ = _place()
        out = []
        for w in range(n):
            for k, (fx, fy) in enumerate(_CHIP_FLIPS):
                px, py = _flip(x, fx), _flip(y, fy)
                out.append(_remote(src[w].at[2 * px + py], dst[w].at[k], ssem.at[w * 3 + k], rsem.at[w * 3 + k], (px, py, c)))
        return out

    def start(src, dst, ssem, rsem):
        for cp in copies(src, dst, ssem, rsem):
            cp.start()

    def end(src, dst, ssem, rsem):
        for cp in copies(src, dst, ssem, rsem):
            cp.wait()

    outs = [jax.ShapeDtypeStruct((3,) + p.shape[1:], p.dtype) for p in parts]
    return _Side(parts, outs, 3 * n, start, None, end)


def _gather_side(shards, smalls, mid_step=None, into=None):
    n, ns = len(shards), len(smalls)
    into = into or [(None, a.shape[0], 0) for a in shards]

    def dst_rows(w, c):
        half = shards[w].shape[0] // 2
        return pl.ds(pl.multiple_of(into[w][2] + c * half, 16), half)

    def ici(src, dst, ssem, rsem, w, k):
        x, y, c, me = _place()
        fx, fy = _CHIP_FLIPS[k]
        rows = _half_rows(c, shards[w].shape[0] // 2)
        return _remote(src[w].at[rows], dst[w].at[me, dst_rows(w, c)], ssem.at[w * 6 + k], rsem.at[w * 6 + k],
                       (_flip(x, fx), _flip(y, fy), c))

    def small(src, dst, ssem, rsem, s, k):
        x, y, c, me = _place()
        fx, fy = _CHIP_FLIPS[k]
        sem = 6 * n + 3 * s + k
        return _remote(src[n + s], dst[n + s].at[me], ssem.at[sem], rsem.at[sem], (_flip(x, fx), _flip(y, fy), c))

    def landed(dst, ssem, rsem, w, k, sender_c, sem_off):
        x, y, c, _ = _place()
        fx, fy = _CHIP_FLIPS[k]
        got = dst[w].at[2 * _flip(x, fx) + _flip(y, fy), dst_rows(w, sender_c)]
        return _remote(got, got, ssem.at[w * 6 + sem_off + k], rsem.at[w * 6 + sem_off + k], (x, y, 1 - c))

    def start(src, dst, ssem, rsem):
        for s in range(ns):
            for k in range(3):
                small(src, dst, ssem, rsem, s, k).start()
        for w in range(n):
            for k in range(3):
                ici(src, dst, ssem, rsem, w, k).start()

    def mid(src, dst, ssem, rsem):
        c = lax.axis_index("c")
        for w in range(n):
            for k in range(3):
                landed(dst, ssem, rsem, w, k, c, 0).wait_recv()
                landed(dst, ssem, rsem, w, k, c, 3).start()

    def end(src, dst, ssem, rsem):
        c = lax.axis_index("c")
        for w in range(n):
            for k in range(3):
                landed(dst, ssem, rsem, w, k, 1 - c, 3).wait_recv()
        for s in range(ns):
            for k in range(3):
                small(src, dst, ssem, rsem, s, k).wait()
        for w in range(n):
            for k in range(3):
                ici(src, dst, ssem, rsem, w, k).wait_send()
                landed(dst, ssem, rsem, w, k, c, 3).wait_send()

    outs = [jax.ShapeDtypeStruct((N_CHIPS, rows, a.shape[1]), a.dtype) for a, (_, rows, _) in zip(shards, into)]
    outs += [jax.ShapeDtypeStruct((N_CHIPS,) + a.shape, a.dtype) for a in smalls]
    filled = [(w, arr) for w, (arr, _, _) in enumerate(into) if arr is not None]
    side = _Side(list(shards) + list(smalls) + [arr for _, arr in filled], outs, 6 * n + 3 * ns, start, mid, end, mid_step)
    side.aliases = {n + ns + i: w for i, (w, _) in enumerate(filled)}
    return side


def _host(body, n_in, n_out, side, grid):
    if side is None:
        return body
    ns_in, ns_out = len(side.ins), len(side.outs)
    n_steps = math.prod(grid)
    mid_step = side.mid_step
    if side.mid is not None and not isinstance(mid_step, int):
        mid_step = min(n_steps - 1, int(mid_step * n_steps))

    def wrapped(*refs):
        ins, s_ins = refs[:n_in], refs[n_in:n_in + ns_in]
        outs = refs[n_in + ns_in:n_in + ns_in + n_out]
        s_outs = refs[n_in + ns_in + n_out:n_in + ns_in + n_out + ns_out]
        rest = refs[n_in + ns_in + n_out + ns_out:]
        sems = rest[-2:]
        step = 0
        for axis, extent in enumerate(grid):
            step = step * extent + pl.program_id(axis)

        @pl.when(step == 0)
        def _():
            side.start(s_ins, s_outs, *sems)

        if side.mid is not None:
            @pl.when(step == mid_step)
            def _():
                side.mid(s_ins, s_outs, *sems)

        body(*ins, *outs, *rest[:-2])

        @pl.when(step == n_steps - 1)
        def _():
            side.end(s_ins, s_outs, *sems)

    return wrapped


def _hosted_call(body, name, grid, in_specs, out_specs, out_shape, args, side, semantics=("parallel", "arbitrary"),
                 scratch=()):
    n_in, n_out = len(in_specs), len(out_specs)
    kern = _host(body, n_in, n_out, side, grid)
    if side is None:
        return pl.pallas_call(kern, name=name, grid=grid, in_specs=in_specs, out_specs=out_specs, out_shape=out_shape,
                              scratch_shapes=list(scratch), compiler_params=_params(semantics))(*args), []
    res = pl.pallas_call(
        kern, name=name, grid=grid, in_specs=in_specs + [HBM] * len(side.ins), out_specs=out_specs + [HBM] * len(side.outs),
        out_shape=list(out_shape) + [pltpu.HBM(s.shape, s.dtype) for s in side.outs],
        scratch_shapes=list(scratch) + [pltpu.SemaphoreType.DMA((side.n_sems,)), pltpu.SemaphoreType.DMA((side.n_sems,))],
        input_output_aliases={n_in + i: n_out + o for i, o in side.aliases.items()},
        compiler_params=_params(("arbitrary",) * len(grid)),
    )(*args, *side.ins)
    return res[:n_out], res[n_out:]


def _pair_swap(halves):
    n = len(halves)

    def body(*refs):
        src, dst = refs[:n], refs[n:2 * n]
        ssem, rsem = refs[2 * n:]
        x, y, c, _ = _place()
        cps = []
        for w in range(n):
            cp = _remote(src[w], dst[w], ssem.at[w], rsem.at[w], (x, y, 1 - c))
            cp.start()
            cps.append(cp)
        for cp in cps:
            cp.wait()

    outs = [jax.ShapeDtypeStruct(h.shape, h.dtype) for h in halves]
    return _comm_call("rs_pair_swap", body, halves, outs, n)


def _gather_small(vec):
    def body(src, dst, ssem, rsem):
        x, y, c, _ = _place()
        me = 4 * x + 2 * y + c
        cps = []
        for r in range(1, 8):
            dev = (_flip(x, r & 4), _flip(y, r & 2), _flip(c, r & 1))
            cp = _remote(src, dst.at[me], ssem.at[r - 1], rsem.at[r - 1], dev)
            cp.start()
            cps.append(cp)
        for cp in cps:
            cp.wait()

    out = jax.ShapeDtypeStruct((8,) + vec.shape, vec.dtype)
    return _comm_call("gather_small", body, [vec], [out], 7)[0]


def _pair_add(name, g, theirs, core):
    n, half, b = theirs.shape
    tr = _tile(half, 256, 16)
    nt = half // tr

    def body(c_ref, g_ref, t_ref, o_ref):
        o_ref[...] = (g_ref[...].astype(F32) + t_ref[...].astype(F32)).astype(o_ref.dtype)

    same = pl.BlockSpec((None, tr, b), lambda j, i, c: (j, i, 0))
    grid_spec = pltpu.PrefetchScalarGridSpec(
        num_scalar_prefetch=1, grid=(n, nt),
        in_specs=[pl.BlockSpec((None, tr, b), lambda j, i, c: (j, c[0] * nt + i, 0)), same], out_specs=same)
    return pl.pallas_call(body, name=name, grid_spec=grid_spec, out_shape=jax.ShapeDtypeStruct(theirs.shape, BF16),
                          compiler_params=_params(("parallel", "parallel")))(core, g, theirs)


def _sum_slots(name, stacked, first=None):
    n, r, c = stacked.shape
    tr = _tile(r, 256, 8)

    def body(*refs):
        s_ref, o_ref = refs[-2], refs[-1]
        acc = refs[0][...].astype(F32) if first is not None else s_ref[0].astype(F32)
        for s in range(0 if first is not None else 1, n):
            acc = acc + s_ref[s].astype(F32)
        o_ref[...] = acc

    row_spec = pl.BlockSpec((tr, c), lambda i: (i, 0))
    in_specs = ([row_spec] if first is not None else []) + [pl.BlockSpec((n, tr, c), lambda i: (0, i, 0))]
    args = ([first] if first is not None else []) + [stacked]
    return pl.pallas_call(
        body, name=name, grid=(r // tr,), in_specs=in_specs, out_specs=row_spec,
        out_shape=jax.ShapeDtypeStruct((r, c), F32), compiler_params=_params(("parallel",)),
    )(*args)


def _adam_math(w, g, m, v):
    bc1, bc2 = 1.0 - ADAM_B1 ** ADAM_STEP, 1.0 - ADAM_B2 ** ADAM_STEP
    nm = ADAM_B1 * m + (1.0 - ADAM_B1) * g
    nv = ADAM_B2 * v + (1.0 - ADAM_B2) * (g * g)
    return -ADAM_LR * ((nm / bc1) / (jnp.sqrt(nv / bc2) + ADAM_EPS) + ADAM_WD * w), nm, nv


def _adamw_halves(name, w, g_mine, g_theirs, m, v, core):
    r, c = w.shape
    h = r // 2
    tr = _tile(h, 128, 8)
    nth = h // tr

    def body(c_ref, w_ref, gm_ref, gt_ref, m_ref, v_ref, g_ref, d_ref, nm_ref, nv_ref):
        g = jnp.where(pl.program_id(0) // nth == c_ref[0], gm_ref[...], gt_ref[...])
        g_ref[...] = g
        d_ref[...], nm_ref[...], nv_ref[...] = _adam_math(w_ref[...], g, m_ref[...], v_ref[...])

    full = pl.BlockSpec((tr, c), lambda i, cr: (i, 0))
    half = pl.BlockSpec((tr, c), lambda i, cr: (i % nth, 0))
    grid_spec = pltpu.PrefetchScalarGridSpec(num_scalar_prefetch=1, grid=(r // tr,),
                                             in_specs=[full, half, half, full, full], out_specs=[full] * 4)
    return pl.pallas_call(body, name=name, grid_spec=grid_spec, out_shape=[jax.ShapeDtypeStruct((r, c), F32)] * 4,
                          compiler_params=_params(("parallel",)))(core, w, g_mine, g_theirs, m, v)


def _adamw(name, w, g, m, v):
    r, c = w.shape
    by_cols = r % 8 != 0 and c % LANES == 0
    tr, tc = (r, _tile(c, 256, LANES)) if by_cols else (_tile(r, 256, 8), c)

    def body(w_ref, g_ref, m_ref, v_ref, d_ref, nm_ref, nv_ref):
        d_ref[...], nm_ref[...], nv_ref[...] = _adam_math(w_ref[...], g_ref[...], m_ref[...], v_ref[...])

    spec = pl.BlockSpec((tr, tc), (lambda i: (0, i)) if by_cols else (lambda i: (i, 0)))
    return pl.pallas_call(
        body, name=name, grid=(c // tc if by_cols else r // tr,), in_specs=[spec] * 4, out_specs=[spec] * 3,
        out_shape=[jax.ShapeDtypeStruct((r, c), F32)] * 3, compiler_params=_params(("parallel",)),
    )(w, g, m, v)


def _pad_cols(a, cols):
    return jnp.pad(a, ((0, 0), (0, cols - a.shape[1])))


def _rot_cols(w):
    h = w.shape[-1] // 2
    return jnp.concatenate([-w[..., h:], w[..., :h]], axis=-1)


def _unrot_cols(d):
    h = d.shape[-1] // 2
    return jnp.concatenate([d[..., h:], -d[..., :h]], axis=-1)


def _logical(g):
    return jnp.transpose(g, (1, 0, 2)).reshape(g.shape[1], N_CHIPS * g.shape[2])


def _chunks(a, n):
    return jnp.transpose(a.reshape(a.shape[0], N_CHIPS, n), (1, 0, 2))


def kernel(x, positions, pre_mix_norm, w_in, q_a_norm, w_uq, kv_a_norm, w_ukv, b_forget, b_gate, w_branch_mla, w_branch_fox, w_out, post_mix_norm, pre_ffn_norm, w_up, conv_w, conv_b, w_down, post_ffn_norm, loss_target, m_pre_mix_norm, m_w_in, m_q_a_norm, m_w_uq, m_kv_a_norm, m_w_ukv, m_b_forget, m_b_gate, m_w_branch_mla, m_w_branch_fox, m_w_out, m_post_mix_norm, m_pre_ffn_norm, m_w_up, m_conv_w, m_conv_b, m_w_down, m_post_ffn_norm, v_pre_mix_norm, v_w_in, v_q_a_norm, v_w_uq, v_kv_a_norm, v_w_ukv, v_b_forget, v_b_gate, v_w_branch_mla, v_w_branch_fox, v_w_out, v_post_mix_norm, v_pre_ffn_norm, v_w_up, v_conv_w, v_conv_b, v_w_down, v_post_ffn_norm):
    n_seq, seq, d = x.shape
    n_tok = n_seq * seq
    d_in = N_CHIPS * w_in.shape[1]
    two_f = N_CHIPS * w_up.shape[1]
    ff_dim = two_f // 2
    assert d_in == QL + KVL + ROPE + 3 * HEADS * FDIM + HEADS + 2 * d
    n_in_shard = w_in.shape[1]
    in_pad = -(-n_in_shard // LANES) * LANES
    hd = HEADS * LANES
    xc, yc, cc = lax.axis_index("x"), lax.axis_index("y"), lax.axis_index("c")
    chip = 2 * xc + yc
    t_attn = _tile(seq, 512, 128)

    shards = [_pad_cols(w_in, in_pad).astype(BF16), w_uq.astype(BF16), w_ukv.astype(BF16), w_branch_mla.astype(BF16),
              w_branch_fox.astype(BF16), w_out.astype(BF16), w_up.astype(BF16), w_down.astype(BF16)]
    cw8 = jnp.pad(conv_w, ((0, 5), (0, 0)))
    put_own = lambda g, s: lax.dynamic_update_slice(g, s[None], (chip, 0, 0))
    gathered, (g_cw,) = _all_gather_weights(shards[:3], [cw8])
    g_in, g_uq, g_ukv = [put_own(g, s) for g, s in zip(gathered, shards[:3])]
    g_cw = put_own(g_cw, cw8)
    n_attn_steps = n_seq * (seq // t_attn)
    side_proj = _gather_side([shards[3], shards[4], shards[5]], [], mid_step=0.9)
    side_ffn = _gather_side([shards[7]], [], mid_step=0.7)
    up_rows = shards[6].shape[0]
    up_cuts = [0, up_rows // 8, up_rows // 2, 7 * up_rows // 8, up_rows]

    def side_up(piece, filled, mid_step):
        lo, hi = up_cuts[piece], up_cuts[piece + 1]
        return _gather_side([shards[6][lo:hi]], [], mid_step=mid_step, into=[(filled, up_rows, lo)])

    o_q, o_kv, o_kpe = 0, QL, QL + KVL
    o_f = o_kpe + ROPE
    o_fl = o_f + 3 * hd
    o_g = o_fl + HEADS

    def chip_cols(lo, hi):
        out = []
        while lo < hi:
            j = lo // n_in_shard
            end = min(hi, (j + 1) * n_in_shard)
            out.append((j, lo - j * n_in_shard, end - j * n_in_shard))
            lo = end
        return out

    take = lambda lo, hi: [g_in[j, :, a:b] for j, a, b in chip_cols(lo, hi)]
    w_kpe = jnp.concatenate(take(o_kpe, o_f), axis=1)
    zeros = lambda n: jnp.zeros((d, n), BF16)
    win_p = jnp.concatenate(
        take(o_g, d_in) + take(o_q, o_kpe) + [w_kpe, zeros(LANES - ROPE), _rot_cols(w_kpe)] + take(o_fl, o_g)
        + [zeros(LANES - ROPE - HEADS)] + take(o_f, o_fl), axis=1)
    n_p = win_p.shape[1]
    cb_gm, cb_gf = 0, 1
    c_lat = 2 * d
    c_kx, c_kr = c_lat + QL + KVL, c_lat + QL + KVL + LANES
    n_pa = c_kr + LANES
    assert n_p == n_pa + 3 * hd

    uq3 = _logical(g_uq).reshape(QL, HEADS, NOPE + ROPE)
    pe = uq3[:, :, NOPE:]
    pad_pe = lambda a: jnp.pad(a, ((0, 0), (0, 0), (0, LANES - ROPE))).reshape(QL, hd)
    wuq_p = jnp.concatenate([uq3[:, :, :NOPE].reshape(QL, hd), pad_pe(pe), pad_pe(_rot_cols(pe))], axis=1)
    ukv3 = _logical(g_ukv).reshape(KVL, HEADS, NOPE + VDIM)
    wukv_p = jnp.concatenate([ukv3[:, :, :NOPE].reshape(KVL, hd), ukv3[:, :, NOPE:].reshape(KVL, hd)], axis=1)

    n_bm, n_up = w_branch_mla.shape[1], w_up.shape[1]
    l_bm, l_up = _Chunked(n_bm), _Chunked(n_up)
    wt = n_up // 2
    n_ut = two_f // wt
    il = lambda cblk: jnp.where(cblk < n_ut // 2, 2 * cblk, 2 * (cblk - n_ut // 2) + 1)
    l_il = _Plain(il)
    to_il = lambda a: a.reshape(a.shape[0], 2, n_ut // 2, wt).transpose(0, 2, 1, 3).reshape(a.shape[0], two_f)
    from_il = lambda a: a.reshape(a.shape[0], n_ut // 2, 2, wt).transpose(0, 2, 1, 3).reshape(a.shape[0], two_f)

    inv_freq = 1.0 / (ROPE_THETA ** (jnp.arange(0, ROPE, 2, dtype=F32) / ROPE))
    ang = positions.astype(F32).reshape(n_tok, 1) * inv_freq
    cos, sin = jnp.cos(ang), jnp.sin(ang)
    cs = _pad_cols(jnp.concatenate([cos, cos], axis=1), LANES)
    sn = _pad_cols(jnp.concatenate([sin, sin], axis=1), LANES)

    row = lambda v: v.reshape(1, -1)
    x2 = x.reshape(n_tok, d)
    tgt = loss_target.reshape(n_tok, d)

    (h,) = _rows("rms_pre_mix", lambda r, v: ([r[0] * _rstd(r[0]) * v[0]], []),
                 [(x2, d, 0)], [row(pre_mix_norm)], [(d, BF16)], [], n_tok)
    proj, got = _mm("proj_in", "nn", h, win_p, n_tok, n_pa, d, side=side_proj)
    g_bm, g_bf, g_out = [put_own(g, s) for g, s in zip(got, side_proj.ins)]
    w_out_full = g_out.reshape(d, d)
    tn_f = _tile(3 * hd, 1024, 128)
    assert n_pa % tn_f == 0
    proj_f, (g_up,) = _mm("proj_in_fox", "nn", h, win_p, n_tok, 3 * hd, d, tn=tn_f, lb=_Plain(lambda cblk: cblk + n_pa // tn_f),
                          out_dtype=BF16, side=side_up(0, None, 0.8))

    bf_vec = jnp.pad(row(b_forget), ((0, 0), (ROPE, LANES - ROPE - HEADS)))

    def lat_fwd(r, v):
        ql, kvl = r[0], r[1]
        return [ql * _rstd(ql) * v[0], kvl * _rstd(kvl) * v[1], r[2] * r[4] + r[3] * r[5]], []

    qn, kvn, rk = _rows("latent_norms", lat_fwd,
                        [(proj, QL, c_lat // QL), (proj, KVL, (c_lat + QL) // KVL), (proj, LANES, c_kx // LANES),
                         (proj, LANES, c_kr // LANES), (cs, LANES, 0), (sn, LANES, 0)],
                        [row(q_a_norm), row(kv_a_norm)], [(QL, BF16), (KVL, BF16), (LANES, BF16)], [], n_tok)
    q_p = _mm("q_up", "nn", qn, wuq_p, n_tok, 3 * hd, QL)
    kv_p = _mm("kv_up", "nn", kvn, wukv_p, n_tok, 2 * hd, KVL, out_dtype=BF16)

    def rope_q(r, v):
        c8, s8 = jnp.tile(r[3], (1, HEADS)), jnp.tile(r[4], (1, HEADS))
        return [r[0], r[1] * c8 + r[2] * s8], []

    q_nope, rq = _rows("rope_q", rope_q, [(q_p, hd, 0), (q_p, hd, 1), (q_p, hd, 2), (cs, LANES, 0), (sn, LANES, 0)], [],
                       [(hd, BF16), (hd, BF16)], [], n_tok)

    mla_q = [(q_nope, 0, False), (rq, 0, False)]
    mla_k = [(kv_p, 0, False), (rk, 0, True)]
    mla_v = (kv_p, 1)
    mla_scale = (NOPE + ROPE) ** -0.5
    (o_mla, lse_mla), (g_up,) = _attn_fwd("mla_fwd", mla_q, mla_k, mla_v, None, CHUNK, mla_scale, n_seq, seq, t_attn,
                                          side=side_up(1, g_up, max(n_attn_steps - 2, 0)))

    c_run = _seq_cumsum("forget_cumsum", proj, c_kr // LANES, n_seq, seq, False,
                        pre=lambda z, b: _log_sigmoid(z + b), vec=bf_vec)
    nb_attn = seq // t_attn
    c_rowf = jnp.transpose(c_run[:, ROPE:ROPE + HEADS].reshape(n_seq, nb_attn, t_attn, HEADS), (0, 1, 3, 2))
    fox_q, fox_k, fox_v = [(proj_f, 0, False)], [(proj_f, 1, False)], (proj_f, 2)
    fox_scale = FDIM ** -0.5
    fox_bias = (c_run, c_rowf)
    (o_fox, lse_fox), (g_up,) = _attn_fwd("fox_fwd", fox_q, fox_k, fox_v, fox_bias, 1, fox_scale, n_seq, seq, t_attn,
                                          side=side_up(2, g_up, max(n_attn_steps - 2, 0)))

    pm, (g_up,) = _mm("branch_mla", "nn", o_mla, g_bm, n_tok, d, hd, lb=l_bm, tn=n_bm, out_dtype=BF16,
                      side=side_up(3, g_up, 0.7))
    g_up = put_own(g_up, shards[6])
    pf = _mm("branch_fox", "nn", o_fox, g_bf, n_tok, d, hd, lb=l_bm, tn=n_bm, out_dtype=BF16)
    bg = row(b_gate)

    def merge(r, v):
        return [_sigmoid(r[0] + v[0]) * r[2] + _sigmoid(r[1] + v[1]) * r[3]], []

    (merged,) = _rows("gate_merge", merge, [(proj, d, cb_gm), (proj, d, cb_gf), (pm, d, 0), (pf, d, 0)],
                      [bg[:, :d], bg[:, d:]], [(d, BF16)], [], n_tok)
    y1 = _mm("mix_out", "nn", merged, w_out_full, n_tok, d, d)

    def resid_norm(r, v):
        x1v = r[0] + r[1] * _rstd(r[1]) * v[0]
        return [x1v, x1v * _rstd(x1v) * v[1]], []

    x1, h2 = _rows("post_mix_pre_ffn", resid_norm, [(x2, d, 0), (y1, d, 0)], [row(post_mix_norm), row(pre_ffn_norm)],
                   [(d, F32), (d, BF16)], [], n_tok)

    u_il, got = _mm("ffn_up", "nn", h2, g_up, n_tok, two_f, d, lb=l_up, lo=l_il, tn=wt, out_dtype=BF16, side=side_ffn)
    w_down_full = put_own(got[0], side_ffn.ins[0]).reshape(ff_dim, d)
    cw_il = to_il(_logical(g_cw)[:3])
    cb_il = to_il(row(conv_b))
    act = _conv_glu_fwd(u_il, cw_il, cb_il, n_seq, seq, wt)
    ff = _mm("ffn_down", "nn", act, w_down_full, n_tok, d, ff_dim)

    def final(r, v):
        x1v, ffv, tg = r
        diff = x1v + ffv * _rstd(ffv) * v[0] - tg
        dx2v = diff / d
        dffv, dg4 = _rms_bwd(ffv, v[0], dx2v)
        sq = jnp.sum(jnp.sum(diff * diff, axis=1, keepdims=True), axis=0, keepdims=True)
        return [dx2v, dffv], [dg4, jnp.broadcast_to(sq, (1, LANES))]

    dx2, dff, dg_post_ffn, sq_sum = _rows("loss_post_ffn_bwd", final, [(x1, d, 0), (ff, d, 0), (tgt, d, 0)],
                                          [row(post_ffn_norm)], [(d, F32), (d, BF16)], [(1, d), (1, LANES)], n_tok)
    rs_parts, rs_landed = {}, {}
    core = jnp.reshape(cc, (1,)).astype(jnp.int32)

    def pair_reduce(tag, names, grads):
        theirs = _pair_split("rs_pair_split_" + tag, grads)
        for nm, g, b in zip(names, grads, theirs):
            rs_parts[nm] = _pair_add("rs_pair_add_" + nm, g, b, core)

    dact = _mm("ffn_down_dx", "nt", dff, w_down_full, n_tok, ff_dim, d, tn=wt, out_dtype=BF16)
    gw_down = _mm("ffn_down_dw", "tn", act, dff, ff_dim, d, n_tok, tm=wt, out_dtype=BF16)
    pair_reduce("down", ["w_down"], [gw_down.reshape(N_CHIPS, ff_dim // N_CHIPS, d)])
    d_il, conv_acc = _conv_glu_bwd_pre(u_il, dact, cw_il, cb_il, n_seq, seq, wt)
    du_il = _conv_bwd_input(d_il, cw_il, n_seq, seq, wt)
    gw_up, got = _mm("ffn_up_dw", "tn", h2, du_il, d, two_f, n_tok, lb=l_il, lo=l_up, tn=wt, out_dtype=BF16,
                     side=_scatter_side([rs_parts["w_down"]]))
    rs_landed["w_down"] = got[0]
    pair_reduce("up", ["w_up"], [gw_up])
    dh2, got = _mm("ffn_up_dx", "nt", du_il, g_up, n_tok, d, two_f, la=l_il, lb=l_up, tk=wt,
                   side=_scatter_side([rs_parts["w_up"]]))
    rs_landed["w_up"] = got[0]

    def mid_bwd(r, v):
        x1v, y1v, dx2v, dh2v = r
        d3, dg3 = _rms_bwd(x1v, v[1], dh2v)
        dx1v = dx2v + d3
        dy1v, dg2 = _rms_bwd(y1v, v[0], dx1v)
        return [dx1v, dy1v], [dg3, dg2]

    dx1, dy1, dg_pre_ffn, dg_post_mix = _rows(
        "pre_ffn_post_mix_bwd", mid_bwd, [(x1, d, 0), (y1, d, 0), (dx2, d, 0), (dh2, d, 0)],
        [row(post_mix_norm), row(pre_ffn_norm)], [(d, F32), (d, BF16)], [(1, d), (1, d)], n_tok)
    dmerged = _mm("mix_out_dx", "nt", dy1, w_out_full, n_tok, d, d, out_dtype=BF16)
    gw_out = _mm("mix_out_dw", "tn", merged, dy1, d, d, n_tok, out_dtype=BF16)

    def gate_bwd(r, v):
        zm, zf, pmv, pfv, dm = r
        gm, gf = _sigmoid(zm + v[0]), _sigmoid(zf + v[1])
        dzm, dzf = dm * pmv * gm * (1.0 - gm), dm * pfv * gf * (1.0 - gf)
        return [dm * gm, dm * gf, jnp.concatenate([dzm, dzf], axis=1)], [_colsum(dzm), _colsum(dzf)]

    dpm, dpf, dz, dbg_m, dbg_f = _rows(
        "gate_merge_bwd", gate_bwd, [(proj, d, cb_gm), (proj, d, cb_gf), (pm, d, 0), (pf, d, 0), (dmerged, d, 0)],
        [bg[:, :d], bg[:, d:]], [(d, BF16), (d, BF16), (2 * d, BF16)], [(1, d), (1, d)], n_tok)
    tk_b = min(n_bm, 512)
    do_mla = _mm("branch_mla_dx", "nt", dpm, g_bm, n_tok, hd, d, lb=l_bm, tk=tk_b)
    do_fox = _mm("branch_fox_dx", "nt", dpf, g_bf, n_tok, hd, d, lb=l_bm, tk=tk_b)
    gw_bm = _mm("branch_mla_dw", "tn", o_mla, dpm, hd, d, n_tok, lo=l_bm, tn=n_bm, out_dtype=BF16)
    gw_bf = _mm("branch_fox_dw", "tn", o_fox, dpf, hd, d, n_tok, lo=l_bm, tn=n_bm, out_dtype=BF16)

    pair_reduce("mix", ["w_out", "w_branch_mla", "w_branch_fox"], [gw_out.reshape(N_CHIPS, d // N_CHIPS, d), gw_bm, gw_bf])
    delta_mla, dob_mla = _attn_delta("mla_delta", o_mla, do_mla, n_tok)
    (dq_nope, drq, dk_nope, drk_g, dv_mla), got = _attn_bwd(
        "mla_bwd", mla_q, mla_k, mla_v, dob_mla, lse_mla, delta_mla, None, CHUNK, mla_scale, n_seq, seq, t_attn, BF16,
        side=_scatter_side([rs_parts[nm] for nm in ("w_out", "w_branch_mla", "w_branch_fox")]))
    rs_landed.update(zip(("w_out", "w_branch_mla", "w_branch_fox"), got))
    delta_fox, dob_fox = _attn_delta("fox_delta", o_fox, do_fox, n_tok)
    (dfq, dfk, dfv, dc_q, dc_k), _ = _attn_bwd("fox_bwd", fox_q, fox_k, fox_v, dob_fox, lse_fox, delta_fox, fox_bias, 1,
                                               fox_scale, n_seq, seq, t_attn, BF16)
    dc_k8 = jnp.transpose(dc_k, (0, 2, 4, 1, 3)).reshape(n_tok, HEADS)
    dc128 = dc_q[0] + dc_q[1] + jnp.pad(dc_k8, ((0, 0), (ROPE, LANES - ROPE - HEADS)))
    dlogf = _seq_cumsum("forget_cumsum_bwd", dc128, 0, n_seq, seq, True)

    def mla_pack(r, v):
        dqn_v, drq_v, dkn_v, dv_v, drk_a, drk_b, c1, s1 = r
        c8, s8 = jnp.tile(c1, (1, HEADS)), jnp.tile(s1, (1, HEADS))
        drk_v = drk_a + drk_b
        return [jnp.concatenate([dqn_v, drq_v * c8, drq_v * s8], axis=1), jnp.concatenate([dkn_v, dv_v], axis=1),
                drk_v * c1, drk_v * s1], []

    dq_p, dkv_p, dkx, dkr = _rows(
        "mla_rope_bwd", mla_pack,
        [(dq_nope, hd, 0), (drq, hd, 0), (dk_nope, hd, 0), (dv_mla, hd, 0), (drk_g[0], LANES, 0), (drk_g[1], LANES, 0),
         (cs, LANES, 0), (sn, LANES, 0)],
        [], [(3 * hd, BF16), (2 * hd, BF16), (LANES, F32), (LANES, F32)], [], n_tok)
    dqn = _mm("q_up_dx", "nt", dq_p, wuq_p, n_tok, QL, 3 * hd)
    gw_uq_p = _mm("q_up_dw", "tn", qn, dq_p, QL, 3 * hd, n_tok, out_dtype=BF16)
    dkvn = _mm("kv_up_dx", "nt", dkv_p, wukv_p, n_tok, KVL, 2 * hd)
    gw_ukv_p = _mm("kv_up_dw", "tn", kvn, dkv_p, KVL, 2 * hd, n_tok, out_dtype=BF16)

    def lat_bwd(r, v):
        ql, kvl, dqn_v, dkvn_v, dkx_v, dkr_v, zblk, dlf = r
        dql, dgq = _rms_bwd(ql, v[0], dqn_v)
        dkvl, dgkv = _rms_bwd(kvl, v[1], dkvn_v)
        dfl = dlf * _sigmoid(-(zblk + v[2]))
        return [jnp.concatenate([dql, dkvl, dkx_v, dkr_v + dfl], axis=1)], [dgq, dgkv, _colsum(dfl)]

    dlat, dg_q, dg_kv, dbf = _rows(
        "latent_bwd", lat_bwd,
        [(proj, QL, c_lat // QL), (proj, KVL, (c_lat + QL) // KVL), (dqn, QL, 0), (dkvn, KVL, 0), (dkx, LANES, 0),
         (dkr, LANES, 0), (proj, LANES, c_kr // LANES), (dlogf, LANES, 0)],
        [row(q_a_norm), row(kv_a_norm), bf_vec], [(QL + KVL + 2 * LANES, BF16)], [(1, QL), (1, KVL), (1, LANES)], n_tok)
    dproj = [dz, dlat, dfq, dfk, dfv]
    gw_in_p = _mm_parts("proj_in_dw", "tn", h, dproj, d, n_p, n_tok, tk=1024, out_dtype=BF16)

    f32 = lambda a: a.astype(F32)
    kr_blk = gw_in_p[:, c_kr:c_kr + LANES]
    d_kpe = (f32(gw_in_p[:, c_kx:c_kx + ROPE]) + _unrot_cols(f32(kr_blk[:, :ROPE]))).astype(BF16)
    in_pieces = [(o_q, gw_in_p, c_lat, QL + KVL), (o_kpe, d_kpe, 0, ROPE), (o_f, gw_in_p, n_pa, 3 * hd),
                 (o_fl, kr_blk, ROPE, HEADS), (o_g, gw_in_p, 0, 2 * d)]
    gc_in = []
    for j in range(N_CHIPS):
        lo, hi, cols = j * n_in_shard, (j + 1) * n_in_shard, []
        for first, arr, at, width in in_pieces:
            a, b = max(lo, first), min(hi, first + width)
            if a < b:
                cols.append(arr[:, at + a - first:at + b - first])
        cols.append(jnp.zeros((d, in_pad - n_in_shard), BF16))
        gc_in.append(jnp.concatenate(cols, axis=1))
    gc_in = jnp.stack(gc_in)
    uq_parts = [gw_uq_p[:, i * hd:(i + 1) * hd].reshape(QL, HEADS, LANES) for i in range(3)]
    d_pe = (f32(uq_parts[1][:, :, :ROPE]) + _unrot_cols(f32(uq_parts[2][:, :, :ROPE]))).astype(BF16)
    gc_uq = _chunks(jnp.concatenate([uq_parts[0], d_pe], axis=2).reshape(QL, HEADS * (NOPE + ROPE)), w_uq.shape[1])
    gc_ukv = _chunks(jnp.concatenate([gw_ukv_p[:, :hd].reshape(KVL, HEADS, NOPE), gw_ukv_p[:, hd:].reshape(KVL, HEADS, VDIM)],
                                     axis=2).reshape(KVL, HEADS * (NOPE + VDIM)), w_ukv.shape[1])
    grads = [gc_in, gc_uq, gc_ukv]

    late = ["w_in", "w_uq", "w_ukv"]
    pair_reduce("late", late, grads)
    dh, got = _mm_parts("proj_in_dx", "nt", dproj, win_p, n_tok, d, n_p, side=_scatter_side([rs_parts[nm] for nm in late]))
    rs_landed.update(zip(late, got))

    def first_bwd(r, v):
        dxa, dg1 = _rms_bwd(r[0], v[0], r[1])
        return [r[2] + dxa], [dg1]

    grad_x, dg_pre_mix = _rows("pre_mix_bwd", first_bwd, [(x2, d, 0), (dh, d, 0), (dx1, d, 0)], [row(pre_mix_norm)],
                               [(d, F32)], [(1, d)], n_tok)
    big = list(rs_parts)
    halves = [_sum_slots("rs_chip_sum_" + nm, rs_landed[nm],
                         first=lax.dynamic_index_in_dim(rs_parts[nm], chip, 0, keepdims=False)) for nm in big]
    other = _pair_swap(halves)
    g_halves = dict(zip(big, zip(halves, other)))

    conv_acc_l = from_il(conv_acc)
    pieces = [dg_pre_mix, dg_q, dg_kv, dbf, dbg_m, dbg_f, dg_post_mix, dg_pre_ffn, conv_acc_l[3:4], dg_post_ffn,
              conv_acc_l[0:1], conv_acc_l[1:2], conv_acc_l[2:3], sq_sum]
    sizes = [p.shape[1] for p in pieces]
    flat = jnp.concatenate(pieces, axis=1)
    n_rows = -(-flat.shape[1] // (8 * LANES)) * 8
    flat = _pad_cols(flat, n_rows * LANES).reshape(n_rows, LANES)
    slots = lax.dynamic_update_slice(_gather_small(flat), flat[None], (2 * chip + cc, 0, 0))
    total = _sum_slots("small_sum", slots).reshape(1, n_rows * LANES)
    offs = [sum(sizes[:i]) for i in range(len(sizes))]
    tot = [total[0, o:o + s] for o, s in zip(offs, sizes)]
    loss = 0.5 * tot[13][0] / d
    g_small = {"pre_mix_norm": tot[0], "q_a_norm": tot[1], "kv_a_norm": tot[2], "b_forget": tot[3][ROPE:ROPE + HEADS],
               "b_gate": jnp.concatenate([tot[4], tot[5]]), "post_mix_norm": tot[6], "pre_ffn_norm": tot[7],
               "conv_b": tot[8], "post_ffn_norm": tot[9]}
    gcw_full = jnp.stack([tot[10], tot[11], tot[12]])
    g_conv_w = lax.dynamic_slice(gcw_full, (0, chip * n_up), (3, n_up))

    given = dict(pre_mix_norm=(pre_mix_norm, m_pre_mix_norm, v_pre_mix_norm), w_in=(w_in, m_w_in, v_w_in),
                 q_a_norm=(q_a_norm, m_q_a_norm, v_q_a_norm), w_uq=(w_uq, m_w_uq, v_w_uq),
                 kv_a_norm=(kv_a_norm, m_kv_a_norm, v_kv_a_norm), w_ukv=(w_ukv, m_w_ukv, v_w_ukv),
                 b_forget=(b_forget, m_b_forget, v_b_forget), b_gate=(b_gate, m_b_gate, v_b_gate),
                 w_branch_mla=(w_branch_mla, m_w_branch_mla, v_w_branch_mla),
                 w_branch_fox=(w_branch_fox, m_w_branch_fox, v_w_branch_fox), w_out=(w_out, m_w_out, v_w_out),
                 post_mix_norm=(post_mix_norm, m_post_mix_norm, v_post_mix_norm),
                 pre_ffn_norm=(pre_ffn_norm, m_pre_ffn_norm, v_pre_ffn_norm), w_up=(w_up, m_w_up, v_w_up),
                 conv_w=(conv_w, m_conv_w, v_conv_w), conv_b=(conv_b, m_conv_b, v_conv_b),
                 w_down=(w_down, m_w_down, v_w_down), post_ffn_norm=(post_ffn_norm, m_post_ffn_norm, v_post_ffn_norm))
    order = list(given)
    grad, delta, new_m, new_v = {}, {}, {}, {}
    for nm in big:
        mine, theirs = g_halves[nm]
        if nm == "w_in":
            full = jnp.concatenate([jnp.where(cc == 0, mine, theirs), jnp.where(cc == 0, theirs, mine)], axis=0)
            grad[nm] = full[:, :n_in_shard]
            tr_out = _adamw("adamw_" + nm, *[jnp.transpose(a) for a in (given[nm][0], grad[nm], given[nm][1], given[nm][2])])
            delta[nm], new_m[nm], new_v[nm] = [jnp.transpose(a) for a in tr_out]
            continue
        grad[nm], delta[nm], new_m[nm], new_v[nm] = _adamw_halves("adamw_" + nm, given[nm][0], mine, theirs, given[nm][1],
                                                                  given[nm][2], core)
    grad["conv_w"] = g_conv_w
    delta["conv_w"], new_m["conv_w"], new_v["conv_w"] = _adamw("adamw_conv_w", conv_w, g_conv_w, m_conv_w, v_conv_w)
    small = list(g_small)
    padded = [-(-g_small[nm].shape[0] // LANES) * LANES for nm in small]
    s_rows = -(-sum(padded) // (8 * LANES)) * 8

    def pack(vals):
        cat = jnp.concatenate([jnp.pad(a, (0, p - a.shape[0])) for a, p in zip(vals, padded)])
        return jnp.pad(cat, (0, s_rows * LANES - cat.shape[0])).reshape(s_rows, LANES)

    packed = _adamw("adamw_small", pack([given[nm][0] for nm in small]), pack([g_small[nm] for nm in small]),
                    pack([given[nm][1] for nm in small]), pack([given[nm][2] for nm in small]))
    s_offs = [sum(padded[:i]) for i in range(len(small))]
    for nm, o in zip(small, s_offs):
        n_el = g_small[nm].shape[0]
        grad[nm] = g_small[nm]
        delta[nm], new_m[nm], new_v[nm] = [p.reshape(-1)[o:o + n_el] for p in packed]
    return (loss, grad_x.reshape(n_seq, seq, d), *[grad[nm] for nm in order], *[delta[nm] for nm in order],
            *[new_m[nm] for nm in order], *[new_v[nm] for nm in order])
```

```python
import functools
import math

import jax
import jax.numpy as jnp
from jax import lax
from jax.experimental import pallas as pl
from jax.experimental.pallas import tpu as pltpu

F32, BF16 = jnp.float32, jnp.bfloat16
MESH = pl.DeviceIdType.MESH

HEADS = 8
NOPE, ROPE, VDIM = 128, 64, 128
QL, KVL = 512, 256
FDIM = 128
CHUNK = 64
ROPE_THETA = 10000.0
EPS = 1e-6
NEG_INF = -1e30
ADAM_LR, ADAM_B1, ADAM_B2, ADAM_EPS, ADAM_WD, ADAM_STEP = 0.001, 0.9, 0.999, 1e-08, 0.01, 10

VMEM_LIMIT_BYTES = 52 * 1024 * 1024
LANES = 128
N_CHIPS = 4


def _params(sem):
    return pltpu.CompilerParams(dimension_semantics=sem, vmem_limit_bytes=VMEM_LIMIT_BYTES)


def _tile(n, target, mult):
    if n <= target:
        return n
    t = (target // mult) * mult
    while t >= mult:
        if n % t == 0:
            return t
        t -= mult
    raise ValueError(f"no tile for {n} (target {target}, multiple of {mult})")


class _Plain:
    def __init__(self, perm=None):
        self.perm = perm

    def spec(self, tr, tc, rc):
        perm = self.perm

        def imap(i, j, k):
            r, c = rc(i, j, k)
            return (r, perm(c) if perm is not None else c)

        return pl.BlockSpec((tr, tc), imap)

    def shape(self, rows, cols):
        return (rows, cols)


class _Chunked:
    def __init__(self, n):
        self.n = n

    def spec(self, tr, tc, rc):
        assert self.n % tc == 0, (self.n, tc)
        per = self.n // tc

        def imap(i, j, k):
            r, c = rc(i, j, k)
            return (c // per, r, c % per)

        return pl.BlockSpec((None, tr, tc), imap)

    def shape(self, rows, cols):
        assert cols == N_CHIPS * self.n
        return (N_CHIPS, rows, self.n)


_DIMS = {"nn": (((1,), (0,)), ((), ())), "nt": (((1,), (1,)), ((), ())), "tn": (((0,), (0,)), ((), ()))}


def _mm_single(name, mode, a, b, m, n, k, tm, tn, la, lb, lo, out_dtype, side):
    if mode == "nn":
        a_spec = la.spec(tm, k, lambda i, j, kk: (i, 0))
        b_spec = lb.spec(k, tn, lambda i, j, kk: (0, j))
    elif mode == "nt":
        a_spec = la.spec(tm, k, lambda i, j, kk: (i, 0))
        b_spec = lb.spec(tn, k, lambda i, j, kk: (j, 0))
    else:
        a_spec = la.spec(k, tm, lambda i, j, kk: (0, i))
        b_spec = lb.spec(k, tn, lambda i, j, kk: (0, j))
    o_spec = lo.spec(tm, tn, lambda i, j, kk: (i, j))
    dims = _DIMS[mode]

    def body(a_ref, b_ref, o_ref):
        o_ref[...] = lax.dot_general(a_ref[...].astype(BF16), b_ref[...].astype(BF16), dims,
                                     preferred_element_type=F32).astype(o_ref.dtype)

    (out,), got = _hosted_call(body, name, (m // tm, n // tn, 1), [a_spec, b_spec], [o_spec],
                               [jax.ShapeDtypeStruct(lo.shape(m, n), out_dtype)], (a, b), side,
                               semantics=("parallel", "parallel", "arbitrary"))
    return out if side is None else (out, got)


def _mm(name, mode, a, b, m, n, k, *, tm=1024, tn=1024, tk=2048, la=None, lb=None, lo=None, out_dtype=F32, side=None):
    la, lb, lo = la or _Plain(), lb or _Plain(), lo or _Plain()
    tm, tn, tk = _tile(m, tm, 128), _tile(n, tn, 128), _tile(k, tk, 128)
    nk = k // tk
    if nk == 1:
        return _mm_single(name, mode, a, b, m, n, k, tm, tn, la, lb, lo, out_dtype, side)
    if mode == "nn":
        a_spec = la.spec(tm, tk, lambda i, j, kk: (i, kk))
        b_spec = lb.spec(tk, tn, lambda i, j, kk: (kk, j))
    elif mode == "nt":
        a_spec = la.spec(tm, tk, lambda i, j, kk: (i, kk))
        b_spec = lb.spec(tn, tk, lambda i, j, kk: (j, kk))
    else:
        a_spec = la.spec(tk, tm, lambda i, j, kk: (kk, i))
        b_spec = lb.spec(tk, tn, lambda i, j, kk: (kk, j))
    o_spec = lo.spec(tm, tn, lambda i, j, kk: (i, j))
    dims = _DIMS[mode]

    def body(a_ref, b_ref, o_ref, acc_ref):
        kk = pl.program_id(2)

        @pl.when(kk == 0)
        def _():
            acc_ref[...] = jnp.zeros_like(acc_ref)

        acc_ref[...] += lax.dot_general(a_ref[...].astype(BF16), b_ref[...].astype(BF16), dims,
                                        preferred_element_type=F32)

        @pl.when(kk == nk - 1)
        def _():
            o_ref[...] = acc_ref[...].astype(o_ref.dtype)

    (out,), got = _hosted_call(body, name, (m // tm, n // tn, nk), [a_spec, b_spec], [o_spec],
                               [jax.ShapeDtypeStruct(lo.shape(m, n), out_dtype)], (a, b), side,
                               semantics=("parallel", "parallel", "arbitrary"), scratch=[pltpu.VMEM((tm, tn), F32)])
    return out if side is None else (out, got)


def _mm_parts(name, mode, a, b, m, n, k, *, part=1024, tm=1024, tn=1024, tk=2048, out_dtype=F32, side=None):
    parts = b if mode == "tn" else a
    widths = [p.shape[1] for p in parts]
    assert all(w % part == 0 for w in widths) and sum(widths) == (n if mode == "tn" else k)
    offs = [sum(widths[:i]) // part for i in range(len(widths))]
    nblk = [w // part for w in widths]
    if mode == "tn":
        tn, tk = part, _tile(k, tk, 128)
    else:
        tk, tn = part, _tile(n, tn, 128)
    tm = _tile(m, tm, 128)
    nk = k // tk
    grid = (m // tm, n // tn, nk)
    np_ = len(parts)

    def inside(idx, p):
        return jnp.logical_and(idx >= offs[p], idx < offs[p] + nblk[p])

    def part_spec(p):
        if mode == "tn":
            def imap(i, j, kk):
                on = inside(j, p)
                return (jnp.where(on, kk, 0), jnp.clip(j - offs[p], 0, nblk[p] - 1))
            return pl.BlockSpec((tk, tn), imap)

        def imap(i, j, kk):
            return (i, jnp.clip(kk - offs[p], 0, nblk[p] - 1))
        return pl.BlockSpec((tm, tk), imap)

    if mode == "tn":
        in_specs = [pl.BlockSpec((tk, tm), lambda i, j, kk: (kk, i))] + [part_spec(p) for p in range(np_)]
        args = [a] + list(parts)
    else:
        in_specs = [part_spec(p) for p in range(np_)] + [pl.BlockSpec((tn, tk), lambda i, j, kk: (j, kk))]
        args = list(parts) + [b]
    dims = _DIMS[mode]

    def body(*refs):
        o_ref, acc_ref = refs[-2], refs[-1]
        j, kk = pl.program_id(1), pl.program_id(2)

        @pl.when(kk == 0)
        def _():
            acc_ref[...] = jnp.zeros_like(acc_ref)

        for p in range(np_):
            @pl.when(inside(j if mode == "tn" else kk, p))
            def _(p=p):
                lhs, rhs = (refs[0], refs[1 + p]) if mode == "tn" else (refs[p], refs[np_])
                acc_ref[...] += lax.dot_general(lhs[...].astype(BF16), rhs[...].astype(BF16), dims, preferred_element_type=F32)

        @pl.when(kk == nk - 1)
        def _():
            o_ref[...] = acc_ref[...].astype(o_ref.dtype)

    (out,), got = _hosted_call(body, name, grid, in_specs, [pl.BlockSpec((tm, tn), lambda i, j, kk: (i, j))],
                               [jax.ShapeDtypeStruct((m, n), out_dtype)], args, side,
                               semantics=("parallel", "parallel", "arbitrary"), scratch=[pltpu.VMEM((tm, tn), F32)])
    return out if side is None else (out, got)


def _rows(name, fn, rows_in, vecs_in, rows_out, accs_out, n_rows, tr=256):
    tr = _tile(n_rows, tr, 16)
    nr, nv, no = len(rows_in), len(vecs_in), len(rows_out)

    def body(*refs):
        ins, vecs = refs[:nr], refs[nr:nr + nv]
        outs, accs = refs[nr + nv:nr + nv + no], refs[nr + nv + no:]
        ro, ac = fn([r[...] for r in ins], [v[...] for v in vecs])
        for o_ref, val in zip(outs, ro):
            o_ref[...] = val.astype(o_ref.dtype)
        if accs:
            @pl.when(pl.program_id(0) == 0)
            def _():
                for a_ref in accs:
                    a_ref[...] = jnp.zeros_like(a_ref)

            for a_ref, val in zip(accs, ac):
                a_ref[...] += val

    in_specs = [pl.BlockSpec((tr, cols), functools.partial(lambda i, cb: (i, cb), cb=cb)) for _, cols, cb in rows_in]
    in_specs += [pl.BlockSpec(v.shape, lambda i: (0, 0)) for v in vecs_in]
    out_specs = [pl.BlockSpec((tr, cols), lambda i: (i, 0)) for cols, _ in rows_out]
    out_specs += [pl.BlockSpec((r, cols), lambda i: (0, 0)) for r, cols in accs_out]
    out_shape = [jax.ShapeDtypeStruct((n_rows, cols), dt) for cols, dt in rows_out]
    out_shape += [jax.ShapeDtypeStruct((r, cols), F32) for r, cols in accs_out]
    res = pl.pallas_call(
        body, name=name, grid=(n_rows // tr,), in_specs=in_specs, out_specs=out_specs, out_shape=out_shape,
        compiler_params=_params(("arbitrary",)),
    )(*[a for a, _, _ in rows_in], *vecs_in)
    return res


def _colsum(v):
    return jnp.sum(v, axis=0, keepdims=True)


def _rstd(x):
    return lax.rsqrt(jnp.mean(x * x, axis=-1, keepdims=True) + EPS)


def _rms_bwd(x, g, dy):
    r = _rstd(x)
    xh = x * r
    dxh = dy * g
    dx = r * (dxh - xh * jnp.mean(dxh * xh, axis=-1, keepdims=True))
    return dx, _colsum(dy * xh)


def _sigmoid(z):
    return 1.0 / (1.0 + jnp.exp(-z))


_GELU_K = math.sqrt(2.0 / math.pi)


def _gelu_parts(g):
    t = jnp.tanh(_GELU_K * (g + 0.044715 * g * g * g))
    gel = 0.5 * g * (1.0 + t)
    dgel = 0.5 * (1.0 + t) + 0.5 * g * (1.0 - t * t) * (_GELU_K * (1.0 + 3.0 * 0.044715 * g * g))
    return gel, dgel


def _diag_visible(t, unit):
    rows = lax.broadcasted_iota(jnp.int32, (t, t), 0)
    cols = lax.broadcasted_iota(jnp.int32, (t, t), 1)
    if unit > 1:
        sh = int(math.log2(unit))
        assert 1 << sh == unit and t % unit == 0
        rows, cols = jnp.right_shift(rows, sh), jnp.right_shift(cols, sh)
    return cols <= rows


def _lane_pick(tile, lane):
    idx = lax.broadcasted_iota(jnp.int32, tile.shape, 1)
    return jnp.sum(jnp.where(idx == lane, tile, 0.0), axis=1, keepdims=True)


def _lane_put(tile, lane, col):
    idx = lax.broadcasted_iota(jnp.int32, tile.shape, 1)
    return jnp.where(idx == lane, col, tile)


def _head_cat(refs, shared, rows, h):
    hs = slice(h * LANES, (h + 1) * LANES)
    vals = [(r[rows, :] if sh else r[rows, hs]).astype(BF16) for r, sh in zip(refs, shared)]
    return vals[0] if len(vals) == 1 else jnp.concatenate(vals, axis=1)


def _blk_rows(i, t):
    return pl.ds(pl.multiple_of(i * t, t), t)


def _piece_specs(pieces, rows, row_idx):
    return [pl.BlockSpec((rows, LANES if sh else HEADS * LANES), functools.partial(lambda b, i, cb: (row_idx(b, i), cb), cb=cb))
            for _, cb, sh in pieces]


def _attn_fwd(name, qp, kp, vp, bias, unit, scale, n_seq, seq, t, side=None):
    nb = seq // t
    n_tok = n_seq * seq
    nq, nk_p = len(qp), len(kp)
    q_sh, k_sh = [p[2] for p in qp], [p[2] for p in kp]
    nbias = 2 if bias is not None else 0

    def body(*refs):
        q_refs, k_refs = refs[:nq], refs[nq:nq + nk_p]
        v_ref = refs[nq + nk_p]
        bias_refs = refs[nq + nk_p + 1:nq + nk_p + 1 + nbias]
        o_ref, lse_ref = refs[nq + nk_p + 1 + nbias:]
        qi = pl.program_id(1)
        lse_tile = jnp.zeros((t, LANES), F32)
        for h in range(HEADS):
            hs = slice(h * LANES, (h + 1) * LANES)
            q = _head_cat(q_refs, q_sh, slice(None), h)
            cq = _lane_pick(bias_refs[0][...], ROPE + h) if bias is not None else None

            def block(kb, carry, diag, h=h, hs=hs, q=q, cq=cq):
                m, l, acc = carry
                rows = _blk_rows(kb, t)
                s = lax.dot_general(q, _head_cat(k_refs, k_sh, rows, h), _DIMS["nt"], preferred_element_type=F32) * scale
                if bias is not None:
                    s = s + cq - bias_refs[1][kb, h:h + 1, :]
                if diag:
                    s = jnp.where(_diag_visible(t, unit), s, NEG_INF)
                m_new = jnp.maximum(m, jnp.max(s, axis=1, keepdims=True))
                alpha = jnp.exp(m - m_new)
                p = jnp.exp(s - m_new)
                l = alpha * l + jnp.sum(p, axis=1, keepdims=True)
                acc = alpha * acc + jnp.dot(p.astype(BF16), v_ref[rows, hs].astype(BF16), preferred_element_type=F32)
                return m_new, l, acc

            init = (jnp.full((t, 1), NEG_INF, F32), jnp.zeros((t, 1), F32), jnp.zeros((t, LANES), F32))
            carry = lax.fori_loop(0, qi, lambda kb, c: block(kb, c, False), init)
            m, l, acc = block(qi, carry, True)
            o_ref[:, hs] = acc / l
            lse_tile = _lane_put(lse_tile, h, m + jnp.log(l))
        lse_ref[...] = lse_tile

    tile_row = lambda b, i: b * nb + i
    seq_row = lambda b, i: b
    lane_tile = pl.BlockSpec((t, LANES), lambda b, i: (b * nb + i, 0))
    in_specs = _piece_specs(qp, t, tile_row) + _piece_specs(kp, seq, seq_row) + _piece_specs([vp + (False,)], seq, seq_row)
    args = [p[0] for p in qp] + [p[0] for p in kp] + [vp[0]]
    if bias is not None:
        in_specs += [lane_tile, pl.BlockSpec((None, nb, HEADS, t), lambda b, i: (b, 0, 0, 0))]
        args += list(bias)
    return _hosted_call(
        body, name, (n_seq, nb), in_specs,
        [pl.BlockSpec((t, HEADS * LANES), lambda b, i: (b * nb + i, 0)), lane_tile],
        [jax.ShapeDtypeStruct((n_tok, HEADS * LANES), F32), jax.ShapeDtypeStruct((n_tok, LANES), F32)], args, side)


def _attn_bwd_dq(name, qp, kp, vp, o, do, lse, bias, unit, scale, n_seq, seq, t, side=None, grad_dtype=F32):
    nb = seq // t
    n_tok = n_seq * seq
    nq, nk_p = len(qp), len(kp)
    q_sh, k_sh = [p[2] for p in qp], [p[2] for p in kp]
    nbias = 2 if bias is not None else 0
    n_in = nq + nk_p + 4 + nbias

    def body(*refs):
        q_refs, k_refs = refs[:nq], refs[nq:nq + nk_p]
        v_ref, o_ref, do_ref, lse_ref = refs[nq + nk_p:nq + nk_p + 4]
        bias_refs = refs[nq + nk_p + 4:n_in]
        dq_refs = refs[n_in:n_in + nq]
        delta_ref, dob_ref = refs[n_in + nq:n_in + nq + 2]
        qi = pl.program_id(1)
        delta_tile = jnp.zeros((t, LANES), F32)
        dc_tile = jnp.zeros((t, LANES), F32)
        lse_all = lse_ref[...]
        for h in range(HEADS):
            hs = slice(h * LANES, (h + 1) * LANES)
            q = _head_cat(q_refs, q_sh, slice(None), h)
            do_f = do_ref[:, hs]
            do_b = do_f.astype(BF16)
            dob_ref[:, hs] = do_b
            delta = jnp.sum(do_f * o_ref[:, hs], axis=1, keepdims=True)
            lse = _lane_pick(lse_all, h)
            cq = _lane_pick(bias_refs[0][...], ROPE + h) if bias is not None else None

            def block(kb, carry, diag, h=h, hs=hs, q=q, cq=cq, do_b=do_b, delta=delta, lse=lse):
                dq_acc, dc_acc = carry
                rows = _blk_rows(kb, t)
                k = _head_cat(k_refs, k_sh, rows, h)
                s = lax.dot_general(q, k, _DIMS["nt"], preferred_element_type=F32) * scale
                if bias is not None:
                    s = s + cq - bias_refs[1][kb, h:h + 1, :]
                if diag:
                    s = jnp.where(_diag_visible(t, unit), s, NEG_INF)
                p = jnp.exp(s - lse)
                dp = lax.dot_general(do_b, v_ref[rows, hs].astype(BF16), _DIMS["nt"], preferred_element_type=F32)
                ds = p * (dp - delta)
                return (dq_acc + jnp.dot(ds.astype(BF16), k, preferred_element_type=F32),
                        dc_acc + jnp.sum(ds, axis=1, keepdims=True))

            init = (jnp.zeros((t, nq * LANES), F32), jnp.zeros((t, 1), F32))
            carry = lax.fori_loop(0, qi, lambda kb, c: block(kb, c, False), init)
            dq_acc, dc_acc = block(qi, carry, True)
            for n_p in range(nq):
                dq_refs[n_p][:, hs] = (dq_acc[:, n_p * LANES:(n_p + 1) * LANES] * scale).astype(grad_dtype)
            delta_tile = _lane_put(delta_tile, h, delta)
            dc_tile = _lane_put(dc_tile, ROPE + h, dc_acc)
        delta_ref[...] = delta_tile
        if bias is not None:
            refs[n_in + nq + 2][...] = dc_tile

    tile_row = lambda b, i: b * nb + i
    seq_row = lambda b, i: b
    lane_tile = pl.BlockSpec((t, LANES), lambda b, i: (b * nb + i, 0))
    head_tile = pl.BlockSpec((t, HEADS * LANES), lambda b, i: (b * nb + i, 0))
    in_specs = _piece_specs(qp, t, tile_row) + _piece_specs(kp, seq, seq_row) + _piece_specs([vp + (False,)], seq, seq_row)
    in_specs += [head_tile, head_tile, lane_tile]
    args = [p[0] for p in qp] + [p[0] for p in kp] + [vp[0], o, do, lse]
    if bias is not None:
        in_specs += [lane_tile, pl.BlockSpec((None, nb, HEADS, t), lambda b, i: (b, 0, 0, 0))]
        args += list(bias)
    out_specs = [head_tile] * nq + [lane_tile, head_tile] + ([lane_tile] if bias is not None else [])
    out_shape = [jax.ShapeDtypeStruct((n_tok, HEADS * LANES), grad_dtype)] * nq
    out_shape += [jax.ShapeDtypeStruct((n_tok, LANES), F32), jax.ShapeDtypeStruct((n_tok, HEADS * LANES), BF16)]
    if bias is not None:
        out_shape.append(jax.ShapeDtypeStruct((n_tok, LANES), F32))
    return _hosted_call(body, name, (n_seq, nb), in_specs, out_specs, out_shape, args, side)


def _attn_bwd_dkv(name, qp, kp, vp, dob, lse, delta, bias, unit, scale, n_seq, seq, t, side=None, grad_dtype=F32):
    nb = seq // t
    n_tok = n_seq * seq
    nq, nk_p = len(qp), len(kp)
    q_sh, k_sh = [p[2] for p in qp], [p[2] for p in kp]
    nbias = 2 if bias is not None else 0
    n_in = nq + nk_p + 4 + nbias

    def body(*refs):
        q_refs, k_refs = refs[:nq], refs[nq:nq + nk_p]
        v_ref, dob_ref, lse_ref, delta_ref = refs[nq + nk_p:nq + nk_p + 4]
        bias_refs = refs[nq + nk_p + 4:n_in]
        dk_refs = refs[n_in:n_in + nk_p]
        dv_ref = refs[n_in + nk_p]
        ki = pl.program_id(1)
        shared_acc = [jnp.zeros((t, LANES), F32) for _ in range(nk_p)]
        for h in range(HEADS):
            hs = slice(h * LANES, (h + 1) * LANES)
            k = _head_cat(k_refs, k_sh, slice(None), h)
            v = v_ref[:, hs].astype(BF16)
            ck = bias_refs[1][h:h + 1, :] if bias is not None else None

            def block(qb, carry, diag, h=h, hs=hs, k=k, v=v, ck=ck):
                dk_acc, dv_acc, dc_acc = carry
                rows = _blk_rows(qb, t)
                q = _head_cat(q_refs, q_sh, rows, h)
                s = lax.dot_general(q, k, _DIMS["nt"], preferred_element_type=F32) * scale
                if bias is not None:
                    s = s + _lane_pick(bias_refs[0][rows, :], ROPE + h) - ck
                if diag:
                    s = jnp.where(_diag_visible(t, unit), s, NEG_INF)
                p = jnp.exp(s - _lane_pick(lse_ref[rows, :], h))
                do_b = dob_ref[rows, hs]
                dp = lax.dot_general(do_b, v, _DIMS["nt"], preferred_element_type=F32)
                ds = p * (dp - _lane_pick(delta_ref[rows, :], h))
                return (dk_acc + lax.dot_general(ds.astype(BF16), q, _DIMS["tn"], preferred_element_type=F32),
                        dv_acc + lax.dot_general(p.astype(BF16), do_b, _DIMS["tn"], preferred_element_type=F32),
                        dc_acc - jnp.sum(ds, axis=0, keepdims=True))

            init = (jnp.zeros((t, nk_p * LANES), F32), jnp.zeros((t, LANES), F32), jnp.zeros((1, t), F32))
            carry = block(ki, init, True)
            dk_acc, dv_acc, dc_acc = lax.fori_loop(ki + 1, nb, lambda qb, c: block(qb, c, False), carry)
            for n_p in range(nk_p):
                part = dk_acc[:, n_p * LANES:(n_p + 1) * LANES] * scale
                if k_sh[n_p]:
                    shared_acc[n_p] = shared_acc[n_p] + part
                else:
                    dk_refs[n_p][:, hs] = part.astype(grad_dtype)
            dv_ref[:, hs] = dv_acc.astype(grad_dtype)
            if bias is not None:
                refs[n_in + nk_p + 1][h:h + 1, :] = dc_acc
        for n_p in range(nk_p):
            if k_sh[n_p]:
                dk_refs[n_p][...] = shared_acc[n_p]

    tile_row = lambda b, i: b * nb + i
    seq_row = lambda b, i: b
    lane_seq = pl.BlockSpec((seq, LANES), lambda b, i: (b, 0))
    head_tile = pl.BlockSpec((t, HEADS * LANES), lambda b, i: (b * nb + i, 0))
    row_tile = pl.BlockSpec((None, None, HEADS, t), lambda b, i: (b, i, 0, 0))
    in_specs = _piece_specs(qp, seq, seq_row) + _piece_specs(kp, t, tile_row) + _piece_specs([vp + (False,)], t, tile_row)
    in_specs += [pl.BlockSpec((seq, HEADS * LANES), lambda b, i: (b, 0)), lane_seq, lane_seq]
    args = [p[0] for p in qp] + [p[0] for p in kp] + [vp[0], dob, lse, delta]
    if bias is not None:
        in_specs += [lane_seq, row_tile]
        args += list(bias)
    out_specs = [pl.BlockSpec((t, LANES if sh else HEADS * LANES), lambda b, i: (b * nb + i, 0)) for sh in k_sh] + [head_tile]
    out_shape = [jax.ShapeDtypeStruct((n_tok, LANES), F32) if sh else jax.ShapeDtypeStruct((n_tok, HEADS * LANES), grad_dtype)
                 for sh in k_sh]
    out_shape.append(jax.ShapeDtypeStruct((n_tok, HEADS * LANES), grad_dtype))
    if bias is not None:
        out_specs.append(row_tile)
        out_shape.append(jax.ShapeDtypeStruct((n_seq, nb, HEADS, t), F32))
    return _hosted_call(body, name, (n_seq, nb), in_specs, out_specs, out_shape, args, side)


HEAD_GROUPS = 2


def _attn_delta(name, o, do, n_tok):
    def fn(r, v):
        o_v, do_v = r
        tile = jnp.zeros((o_v.shape[0], LANES), F32)
        for h in range(HEADS):
            hs = slice(h * LANES, (h + 1) * LANES)
            tile = _lane_put(tile, h, jnp.sum(do_v[:, hs] * o_v[:, hs], axis=1, keepdims=True))
        return [tile, do_v], []

    return _rows(name, fn, [(o, HEADS * LANES, 0), (do, HEADS * LANES, 0)], [], [(LANES, F32), (HEADS * LANES, BF16)], [], n_tok)


def _attn_bwd(name, qp, kp, vp, dob, lse, delta, bias, unit, scale, n_seq, seq, t, grad_dtype, side=None):
    nb = seq // t
    n_tok = n_seq * seq
    ng = HEAD_GROUPS
    hg = HEADS // ng
    gw = hg * LANES
    nq, nk_p = len(qp), len(kp)
    q_sh, k_sh = [p[2] for p in qp], [p[2] for p in kp]
    assert not any(q_sh) and nq == nk_p
    nbias = 2 if bias is not None else 0
    n_in = nq + nk_p + 4 + nbias
    n_out = nq + nk_p + 1 + nbias

    def body(*refs):
        q_refs, k_refs = refs[:nq], refs[nq:nq + nk_p]
        v_ref, dob_ref, lse_ref, delta_ref = refs[nq + nk_p:nq + nk_p + 4]
        bias_refs = refs[nq + nk_p + 4:n_in]
        dq_refs, dk_refs = refs[n_in:n_in + nq], refs[n_in + nq:n_in + nq + nk_p]
        dv_ref = refs[n_in + nq + nk_p]
        dq_s, dcq_s = refs[n_in + n_out:]
        g, ki = pl.program_id(1), pl.program_id(2)

        @pl.when(ki == 0)
        def _():
            dq_s[...] = jnp.zeros_like(dq_s)
            dcq_s[...] = jnp.zeros_like(dcq_s)

        shared_acc = [jnp.zeros((t, LANES), F32) for _ in range(nk_p)]
        for hl in range(hg):
            h = g * hg + hl
            hs = slice(hl * LANES, (hl + 1) * LANES)
            k = _head_cat(k_refs, k_sh, slice(None), hl)
            v = v_ref[:, hs].astype(BF16)
            ck = bias_refs[1][pl.ds(h, 1), :] if bias is not None else None

            def block(qb, carry, diag, h=h, hl=hl, hs=hs, k=k, v=v, ck=ck):
                dk_acc, dv_acc, dc_acc = carry
                rows = _blk_rows(qb, t)
                q = _head_cat(q_refs, q_sh, rows, hl)
                s = lax.dot_general(q, k, _DIMS["nt"], preferred_element_type=F32) * scale
                if bias is not None:
                    s = s + _lane_pick(bias_refs[0][rows, :], ROPE + h) - ck
                if diag:
                    s = jnp.where(_diag_visible(t, unit), s, NEG_INF)
                p = jnp.exp(s - _lane_pick(lse_ref[rows, :], h))
                do_b = dob_ref[rows, hs]
                dp = lax.dot_general(do_b, v, _DIMS["nt"], preferred_element_type=F32)
                ds = p * (dp - _lane_pick(delta_ref[rows, :], h))
                ds_b = ds.astype(BF16)
                dq_blk = jnp.dot(ds_b, k, preferred_element_type=F32)
                for n_p in range(nq):
                    dq_s[rows, n_p * gw + hl * LANES:n_p * gw + (hl + 1) * LANES] += dq_blk[:, n_p * LANES:(n_p + 1) * LANES]
                if bias is not None:
                    lane = lax.broadcasted_iota(jnp.int32, (t, LANES), 1)
                    dcq_s[rows, :] += jnp.where(lane == ROPE + h, jnp.sum(ds, axis=1, keepdims=True), 0.0)
                return (dk_acc + lax.dot_general(ds_b, q, _DIMS["tn"], preferred_element_type=F32),
                        dv_acc + lax.dot_general(p.astype(BF16), do_b, _DIMS["tn"], preferred_element_type=F32),
                        dc_acc - jnp.sum(ds, axis=0, keepdims=True))

            init = (jnp.zeros((t, nk_p * LANES), F32), jnp.zeros((t, LANES), F32), jnp.zeros((1, t), F32))
            carry = block(ki, init, True)
            dk_acc, dv_acc, dc_acc = lax.fori_loop(ki + 1, nb, lambda qb, c: block(qb, c, False), carry)
            for n_p in range(nk_p):
                part = dk_acc[:, n_p * LANES:(n_p + 1) * LANES] * scale
                if k_sh[n_p]:
                    shared_acc[n_p] = shared_acc[n_p] + part
                else:
                    dk_refs[n_p][:, hs] = part.astype(grad_dtype)
            dv_ref[:, hs] = dv_acc.astype(grad_dtype)
            if bias is not None:
                refs[n_in + n_out - 1][hl:hl + 1, :] = dc_acc
        for n_p in range(nk_p):
            if k_sh[n_p]:
                dk_refs[n_p][...] = shared_acc[n_p]

        @pl.when(ki == nb - 1)
        def _():
            for n_p in range(nq):
                dq_refs[n_p][...] = (dq_s[:, n_p * gw:(n_p + 1) * gw] * scale).astype(grad_dtype)
            if bias is not None:
                refs[n_in + n_out - 2][...] = dcq_s[...]

    def spec(rows, row_idx, cb, shared):
        if shared:
            return pl.BlockSpec((rows, LANES), lambda b, g, i: (row_idx(b, i), cb))
        return pl.BlockSpec((rows, gw), lambda b, g, i: (row_idx(b, i), cb * ng + g))

    tile_row = lambda b, i: b * nb + i
    seq_row = lambda b, i: b
    lane_seq = pl.BlockSpec((seq, LANES), lambda b, g, i: (b, 0))
    in_specs = [spec(seq, seq_row, cb, sh) for _, cb, sh in qp] + [spec(t, tile_row, cb, sh) for _, cb, sh in kp]
    in_specs += [spec(t, tile_row, vp[1], False), spec(seq, seq_row, 0, False), lane_seq, lane_seq]
    args = [p[0] for p in qp] + [p[0] for p in kp] + [vp[0], dob, lse, delta]
    if bias is not None:
        in_specs += [lane_seq, pl.BlockSpec((None, None, HEADS, t), lambda b, g, i: (b, i, 0, 0))]
        args += list(bias)
    group_tile = pl.BlockSpec((None, t, LANES), lambda b, g, i: (g, b * nb + i, 0))
    out_specs = [spec(seq, seq_row, 0, False)] * nq
    out_specs += [group_tile if sh else spec(t, tile_row, 0, False) for sh in k_sh] + [spec(t, tile_row, 0, False)]
    head_shape = jax.ShapeDtypeStruct((n_tok, HEADS * LANES), grad_dtype)
    out_shape = [head_shape] * nq + [jax.ShapeDtypeStruct((ng, n_tok, LANES), F32) if sh else head_shape for sh in k_sh]
    out_shape.append(head_shape)
    if bias is not None:
        out_specs += [pl.BlockSpec((None, seq, LANES), lambda b, g, i: (g, b, 0)),
                      pl.BlockSpec((None, None, None, hg, t), lambda b, g, i: (b, g, i, 0, 0))]
        out_shape += [jax.ShapeDtypeStruct((ng, n_tok, LANES), F32), jax.ShapeDtypeStruct((n_seq, ng, nb, hg, t), F32)]
    return _hosted_call(body, name, (n_seq, ng, nb), in_specs, out_specs, out_shape, args, side,
                        semantics=("parallel", "arbitrary", "arbitrary"),
                        scratch=[pltpu.VMEM((seq, nq * gw), F32), pltpu.VMEM((seq, LANES), F32)])


def _old_attn_bwd_dq(name, qp, kp, vp, o, do, lse, bias, unit, scale, n_seq, seq, t):
    nb = seq // t
    n_tok = n_seq * seq
    nq, nk_p = len(qp), len(kp)
    nbias = 2 if bias is not None else 0
    n_in = nq + nk_p + 4 + nbias
    n_out = nq + (1 if bias is not None else 0)

    def body(*refs):
        q_refs, k_refs = refs[:nq], refs[nq:nq + nk_p]
        v_ref, o_ref, do_ref, lse_ref = refs[nq + nk_p:nq + nk_p + 4]
        bias_refs = refs[nq + nk_p + 4:n_in]
        outs = refs[n_in:n_in + n_out]
        dq_s, delta_s, dc_s = refs[n_in + n_out:]
        qi, ki = pl.program_id(2), pl.program_id(3)

        @pl.when(ki == 0)
        def _():
            dq_s[...] = jnp.zeros_like(dq_s)
            dc_s[...] = jnp.zeros_like(dc_s)
            delta_s[...] = jnp.sum(do_ref[...] * o_ref[...], axis=1, keepdims=True)

        @pl.when(ki <= qi)
        def _():
            s = _scores(q_refs, k_refs, bias_refs, qi, ki, t, unit, scale)
            p = jnp.exp(s - lse_ref[...])
            dp = lax.dot_general(do_ref[...].astype(BF16), v_ref[...].astype(BF16), _DIMS["nt"],
                                 preferred_element_type=F32)
            ds = p * (dp - delta_s[...])
            dq_s[...] += jnp.dot(ds.astype(BF16), _cat(k_refs), preferred_element_type=F32)
            dc_s[...] += jnp.sum(ds, axis=1, keepdims=True)

        @pl.when(ki == qi)
        def _():
            for n_p in range(nq):
                outs[n_p][...] = dq_s[:, n_p * LANES:(n_p + 1) * LANES] * scale
            if bias is not None:
                outs[nq][...] = dc_s[...]

    q_row = lambda b, i, j: b * nb + i
    k_row = lambda b, i, j: b * nb + jnp.minimum(j, i)
    head_q = pl.BlockSpec((t, LANES), lambda b, h, i, j: (b * nb + i, h))
    col_q = pl.BlockSpec((None, t, 1), lambda b, h, i, j: (h, b * nb + i, 0))
    in_specs = [_piece_spec(t, p, q_row) for p in qp] + [_piece_spec(t, p, k_row) for p in kp]
    in_specs += [_piece_spec(t, vp, k_row), head_q, head_q, col_q]
    args = [p[0] for p in qp] + [p[0] for p in kp] + [vp[0], o, do, lse]
    if bias is not None:
        in_specs += [col_q, pl.BlockSpec((None, 1, t), lambda b, h, i, j: (b * HEADS + h, 0, jnp.minimum(j, i)))]
        args += list(bias)
    out_specs = [head_q] * nq + ([col_q] if bias is not None else [])
    out_shape = [jax.ShapeDtypeStruct((n_tok, HEADS * LANES), F32)] * nq
    if bias is not None:
        out_shape.append(jax.ShapeDtypeStruct((HEADS, n_tok, 1), F32))
    return pl.pallas_call(
        body, name=name, grid=(n_seq, HEADS, nb, nb), in_specs=in_specs, out_specs=out_specs, out_shape=out_shape,
        scratch_shapes=[pltpu.VMEM((t, nq * LANES), F32), pltpu.VMEM((t, 1), F32), pltpu.VMEM((t, 1), F32)],
        compiler_params=_params(("parallel", "parallel", "arbitrary", "arbitrary")),
    )(*args)


def _old_attn_bwd_dkv(name, qp, kp, vp, o, do, lse, bias, unit, scale, n_seq, seq, t):
    nb = seq // t
    n_tok = n_seq * seq
    nq, nk_p = len(qp), len(kp)
    nbias = 2 if bias is not None else 0
    n_in = nq + nk_p + 4 + nbias
    n_out = nk_p + 1 + (1 if bias is not None else 0)

    def body(*refs):
        q_refs, k_refs = refs[:nq], refs[nq:nq + nk_p]
        v_ref, o_ref, do_ref, lse_ref = refs[nq + nk_p:nq + nk_p + 4]
        bias_refs = refs[nq + nk_p + 4:n_in]
        outs = refs[n_in:n_in + n_out]
        dk_s, dv_s, dc_s = refs[n_in + n_out:]
        ki, qi = pl.program_id(2), pl.program_id(3)

        @pl.when(qi == 0)
        def _():
            dk_s[...] = jnp.zeros_like(dk_s)
            dv_s[...] = jnp.zeros_like(dv_s)
            dc_s[...] = jnp.zeros_like(dc_s)

        @pl.when(qi >= ki)
        def _():
            s = _scores(q_refs, k_refs, bias_refs, qi, ki, t, unit, scale)
            p = jnp.exp(s - lse_ref[...])
            do_b = do_ref[...].astype(BF16)
            delta = jnp.sum(do_ref[...] * o_ref[...], axis=1, keepdims=True)
            dp = lax.dot_general(do_b, v_ref[...].astype(BF16), _DIMS["nt"], preferred_element_type=F32)
            ds = p * (dp - delta)
            dv_s[...] += lax.dot_general(p.astype(BF16), do_b, _DIMS["tn"], preferred_element_type=F32)
            dk_s[...] += lax.dot_general(ds.astype(BF16), _cat(q_refs), _DIMS["tn"], preferred_element_type=F32)
            dc_s[...] -= jnp.sum(ds, axis=0, keepdims=True)

        @pl.when(qi == nb - 1)
        def _():
            for n_p in range(nk_p):
                outs[n_p][...] = dk_s[:, n_p * LANES:(n_p + 1) * LANES] * scale
            outs[nk_p][...] = dv_s[...]
            if bias is not None:
                outs[nk_p + 1][...] = dc_s[...]

    q_row = lambda b, i, j: b * nb + jnp.maximum(j, i)
    k_row = lambda b, i, j: b * nb + i
    head_q = pl.BlockSpec((t, LANES), lambda b, h, i, j: (b * nb + jnp.maximum(j, i), h))
    col_q = pl.BlockSpec((None, t, 1), lambda b, h, i, j: (h, b * nb + jnp.maximum(j, i), 0))
    head_k = pl.BlockSpec((t, LANES), lambda b, h, i, j: (b * nb + i, h))
    row_k = pl.BlockSpec((None, 1, t), lambda b, h, i, j: (b * HEADS + h, 0, i))
    in_specs = [_piece_spec(t, p, q_row) for p in qp] + [_piece_spec(t, p, k_row) for p in kp]
    in_specs += [_piece_spec(t, vp, k_row), head_q, head_q, col_q]
    args = [p[0] for p in qp] + [p[0] for p in kp] + [vp[0], o, do, lse]
    if bias is not None:
        in_specs += [col_q, row_k]
        args += list(bias)
    out_specs = [head_k] * (nk_p + 1) + ([row_k] if bias is not None else [])
    out_shape = [jax.ShapeDtypeStruct((n_tok, HEADS * LANES), F32)] * (nk_p + 1)
    if bias is not None:
        out_shape.append(jax.ShapeDtypeStruct((n_seq * HEADS, 1, seq), F32))
    return pl.pallas_call(
        body, name=name, grid=(n_seq, HEADS, nb, nb), in_specs=in_specs, out_specs=out_specs, out_shape=out_shape,
        scratch_shapes=[pltpu.VMEM((t, nk_p * LANES), F32), pltpu.VMEM((t, LANES), F32), pltpu.VMEM((1, t), F32)],
        compiler_params=_params(("parallel", "parallel", "arbitrary", "arbitrary")),
    )(*args)


def _seq_cumsum(name, x, col_block, n_seq, seq, reverse, pre=None, vec=None):
    t = _tile(seq, 256, 128)
    nb = seq // t

    def body(*refs):
        x_ref = refs[0]
        vec_ref = refs[1] if vec is not None else None
        o_ref, carry = refs[-2], refs[-1]

        @pl.when(pl.program_id(1) == 0)
        def _():
            carry[...] = jnp.zeros_like(carry)

        v = x_ref[...]
        if pre is not None:
            v = pre(v, vec_ref[...])
        r = lax.broadcasted_iota(jnp.int32, (t, t), 0)
        c = lax.broadcasted_iota(jnp.int32, (t, t), 1)
        tri = jnp.where((c >= r) if reverse else (c <= r), 1.0, 0.0).astype(BF16)
        hi = v.astype(BF16)
        mid = (v - hi.astype(F32)).astype(BF16)
        lo = (v - hi.astype(F32) - mid.astype(F32)).astype(BF16)
        acc = jnp.dot(tri, hi, preferred_element_type=F32)
        acc += jnp.dot(tri, mid, preferred_element_type=F32)
        acc += jnp.dot(tri, lo, preferred_element_type=F32)
        o_ref[...] = acc + carry[...]
        carry[...] += _colsum(v)

    blk = (lambda b, i: (b * nb + nb - 1 - i)) if reverse else (lambda b, i: (b * nb + i))
    in_specs = [pl.BlockSpec((t, LANES), lambda b, i: (blk(b, i), col_block))]
    args = [x]
    if vec is not None:
        in_specs.append(pl.BlockSpec(vec.shape, lambda b, i: (0, 0)))
        args.append(vec)
    return pl.pallas_call(
        body, name=name, grid=(n_seq, nb), in_specs=in_specs,
        out_specs=pl.BlockSpec((t, LANES), lambda b, i: (blk(b, i), 0)),
        out_shape=jax.ShapeDtypeStruct((n_seq * seq, LANES), F32),
        scratch_shapes=[pltpu.VMEM((1, LANES), F32)],
        compiler_params=_params(("arbitrary", "arbitrary")),
    )(*args)


def _log_sigmoid(z):
    return -(jnp.maximum(-z, 0.0) + jnp.log(1.0 + jnp.exp(-jnp.abs(z))))


HALO = 16


def _shifted(ext, ts, first):
    r = lax.broadcasted_iota(jnp.int32, (ts, ts + HALO), 0)
    c = lax.broadcasted_iota(jnp.int32, (ts, ts + HALO), 1)
    return jnp.dot(jnp.where(c == r + first, 1.0, 0.0).astype(BF16), ext, preferred_element_type=F32)


def _conv_taps(u_b, prev_ref, w_ref, b_ref):
    ts = u_b.shape[0]
    ext = jnp.concatenate([prev_ref[...], u_b], axis=0)
    u, s1, s2 = u_b.astype(F32), _shifted(ext, ts, HALO - 1), _shifted(ext, ts, HALO - 2)
    return (w_ref[0:1, :] * s2 + w_ref[1:2, :] * s1 + w_ref[2:3, :] * u) + b_ref[...], u, s1, s2


def _conv_glu_fwd(u_il, cw_il, cb_il, n_seq, seq, wt):
    n_tok, two_f = u_il.shape
    nct = two_f // (2 * wt)
    ts = _tile(seq, 256, 8)
    ns = seq // ts

    def body(u_ref, w_ref, b_ref, a_ref, carry):
        @pl.when(pl.program_id(2) == 0)
        def _():
            carry[...] = jnp.zeros_like(carry)

        u_b = u_ref[...]
        uc, _, _, _ = _conv_taps(u_b, carry, w_ref, b_ref)
        gel, _ = _gelu_parts(uc[:, :wt])
        a_ref[...] = (gel * uc[:, wt:]).astype(a_ref.dtype)
        carry[...] = u_b[ts - HALO:, :]

    return pl.pallas_call(
        body, name="conv_glu_fwd", grid=(nct, n_seq, ns),
        in_specs=[pl.BlockSpec((ts, 2 * wt), lambda j, b, s: (b * ns + s, j)),
                  pl.BlockSpec((3, 2 * wt), lambda j, b, s: (0, j)),
                  pl.BlockSpec((1, 2 * wt), lambda j, b, s: (0, j))],
        out_specs=pl.BlockSpec((ts, wt), lambda j, b, s: (b * ns + s, j)),
        out_shape=jax.ShapeDtypeStruct((n_tok, two_f // 2), BF16),
        scratch_shapes=[pltpu.VMEM((HALO, 2 * wt), BF16)],
        compiler_params=_params(("parallel", "arbitrary", "arbitrary")),
    )(u_il, cw_il, cb_il)


def _conv_glu_bwd_pre(u_il, da, cw_il, cb_il, n_seq, seq, wt):
    n_tok, two_f = u_il.shape
    nct = two_f // (2 * wt)
    ts = _tile(seq, 256, 8)
    ns = seq // ts

    def body(u_ref, da_ref, w_ref, b_ref, d_ref, acc_ref, carry):
        first = jnp.logical_and(pl.program_id(1) == 0, pl.program_id(2) == 0)

        @pl.when(first)
        def _():
            acc_ref[...] = jnp.zeros_like(acc_ref)

        @pl.when(pl.program_id(2) == 0)
        def _():
            carry[...] = jnp.zeros_like(carry)

        u_b = u_ref[...]
        uc, u, s1, s2 = _conv_taps(u_b, carry, w_ref, b_ref)
        gel, dgel = _gelu_parts(uc[:, :wt])
        da_v = da_ref[...].astype(F32)
        d = jnp.concatenate([da_v * uc[:, wt:] * dgel, da_v * gel], axis=1)
        d_ref[...] = d.astype(d_ref.dtype)
        acc_ref[0:1, :] += _colsum(d * s2)
        acc_ref[1:2, :] += _colsum(d * s1)
        acc_ref[2:3, :] += _colsum(d * u)
        acc_ref[3:4, :] += _colsum(d)
        carry[...] = u_b[ts - HALO:, :]

    return pl.pallas_call(
        body, name="conv_glu_bwd_pre", grid=(nct, n_seq, ns),
        in_specs=[pl.BlockSpec((ts, 2 * wt), lambda j, b, s: (b * ns + s, j)),
                  pl.BlockSpec((ts, wt), lambda j, b, s: (b * ns + s, j)),
                  pl.BlockSpec((3, 2 * wt), lambda j, b, s: (0, j)),
                  pl.BlockSpec((1, 2 * wt), lambda j, b, s: (0, j))],
        out_specs=[pl.BlockSpec((ts, 2 * wt), lambda j, b, s: (b * ns + s, j)),
                   pl.BlockSpec((8, 2 * wt), lambda j, b, s: (0, j))],
        out_shape=[jax.ShapeDtypeStruct((n_tok, two_f), BF16), jax.ShapeDtypeStruct((8, two_f), F32)],
        scratch_shapes=[pltpu.VMEM((HALO, 2 * wt), BF16)],
        compiler_params=_params(("parallel", "arbitrary", "arbitrary")),
    )(u_il, da, cw_il, cb_il)


def _conv_bwd_input(d_il, cw_il, n_seq, seq, wt):
    n_tok, two_f = d_il.shape
    nct = two_f // (2 * wt)
    ts = _tile(seq, 256, 8)
    ns = seq // ts

    def body(d_ref, w_ref, o_ref, carry):
        @pl.when(pl.program_id(2) == 0)
        def _():
            carry[...] = jnp.zeros_like(carry)

        d_b = d_ref[...]
        ext = jnp.concatenate([d_b, carry[...]], axis=0)
        o_ref[...] = (w_ref[2:3, :] * d_b.astype(F32) + w_ref[1:2, :] * _shifted(ext, ts, 1)
                      + w_ref[0:1, :] * _shifted(ext, ts, 2)).astype(o_ref.dtype)
        carry[...] = d_b[:HALO, :]

    rev = lambda j, b, s: (b * ns + ns - 1 - s, j)
    return pl.pallas_call(
        body, name="conv_bwd_input", grid=(nct, n_seq, ns),
        in_specs=[pl.BlockSpec((ts, 2 * wt), rev), pl.BlockSpec((3, 2 * wt), lambda j, b, s: (0, j))],
        out_specs=pl.BlockSpec((ts, 2 * wt), rev),
        out_shape=jax.ShapeDtypeStruct((n_tok, two_f), BF16),
        scratch_shapes=[pltpu.VMEM((HALO, 2 * wt), BF16)],
        compiler_params=_params(("parallel", "arbitrary", "arbitrary")),
    )(d_il, cw_il)


HBM = pl.BlockSpec(memory_space=pltpu.HBM)
_CHIP_FLIPS = ((1, 0), (0, 1), (1, 1))


def _place():
    x, y, c = lax.axis_index("x"), lax.axis_index("y"), lax.axis_index("c")
    return x, y, c, 2 * x + y


def _flip(v, f):
    return 1 - v if f else v


def _half_rows(c, half):
    return pl.ds(pl.multiple_of(c * half, 16), half)


def _remote(src, dst, ssem, rsem, dev):
    return pltpu.make_async_remote_copy(src_ref=src, dst_ref=dst, send_sem=ssem, recv_sem=rsem,
                                        device_id=dev, device_id_type=MESH)


def _comm_call(name, body, ins, out_shapes, n_sems):
    return pl.pallas_call(
        body, name=name, in_specs=[HBM] * len(ins), out_specs=[HBM] * len(out_shapes),
        out_shape=[pltpu.HBM(s.shape, s.dtype) for s in out_shapes],
        scratch_shapes=[pltpu.SemaphoreType.DMA((n_sems,)), pltpu.SemaphoreType.DMA((n_sems,))],
    )(*ins)


def _all_gather_weights(shards, smalls):
    side = _gather_side(shards, smalls)
    nt = len(shards) + len(smalls)

    def body(*refs):
        for part in (side.start, side.mid, side.end):
            part(refs[:nt], refs[nt:2 * nt], *refs[2 * nt:])

    res = _comm_call("all_gather_weights", body, side.ins, side.outs, side.n_sems)
    return res[:len(shards)], res[len(shards):]


def _pair_split(name, grads):
    n = len(grads)

    def body(*refs):
        src, got = refs[:n], refs[n:2 * n]
        ssem, rsem = refs[2 * n:]
        x, y, c, _ = _place()
        cps = []
        for w in range(n):
            half = grads[w].shape[1] // 2
            cp = _remote(src[w].at[:, _half_rows(1 - c, half)], got[w], ssem.at[w], rsem.at[w], (x, y, 1 - c))
            cp.start()
            cps.append(cp)
        for cp in cps:
            cp.wait()

    outs = [jax.ShapeDtypeStruct((g.shape[0], g.shape[1] // 2, g.shape[2]), g.dtype) for g in grads]
    return _comm_call(name, body, grads, outs, n)


def _chip_scatter(parts):
    side = _scatter_side(parts)

    def body(*refs):
        n = len(parts)
        side.start(refs[:n], refs[n:2 * n], *refs[2 * n:])
        side.end(refs[:n], refs[n:2 * n], *refs[2 * n:])

    return _comm_call("rs_chip_scatter", body, parts, side.outs, side.n_sems)


class _Side:
    def __init__(self, ins, outs, n_sems, start, mid, end, mid_step=None):
        self.ins, self.outs, self.n_sems = list(ins), list(outs), n_sems
        self.start, self.mid, self.end, self.mid_step = start, mid, end, mid_step
        self.aliases = {}


def _scatter_side(parts):
    n = len(parts)

    def copies(src, dst, ssem, rsem):
        x, y, c, _ = _place()
        out = []
        for w in range(n):
            for k, (fx, fy) in enumerate(_CHIP_FLIPS):
                px, py = _flip(x, fx), _flip(y, fy)
                out.append(_remote(src[w].at[2 * px + py], dst[w].at[k], ssem.at[w * 3 + k], rsem.at[w * 3 + k], (px, py, c)))
        return out

    def start(src, dst, ssem, rsem):
        for cp in copies(src, dst, ssem, rsem):
            cp.start()

    def end(src, dst, ssem, rsem):
        for cp in copies(src, dst, ssem, rsem):
            cp.wait()

    outs = [jax.ShapeDtypeStruct((3,) + p.shape[1:], p.dtype) for p in parts]
    return _Side(parts, outs, 3 * n, start, None, end)


def _gather_side(shards, smalls, mid_step=None, into=None):
    n, ns = len(shards), len(smalls)
    into = into or [(None, a.shape[0], 0) for a in shards]

    def dst_rows(w, c):
        half = shards[w].shape[0] // 2
        return pl.ds(pl.multiple_of(into[w][2] + c * half, 16), half)

    def ici(src, dst, ssem, rsem, w, k):
        x, y, c, me = _place()
        fx, fy = _CHIP_FLIPS[k]
        rows = _half_rows(c, shards[w].shape[0] // 2)
        return _remote(src[w].at[rows], dst[w].at[me, dst_rows(w, c)], ssem.at[w * 6 + k], rsem.at[w * 6 + k],
                       (_flip(x, fx), _flip(y, fy), c))

    def small(src, dst, ssem, rsem, s, k):
        x, y, c, me = _place()
        fx, fy = _CHIP_FLIPS[k]
        sem = 6 * n + 3 * s + k
        return _remote(src[n + s], dst[n + s].at[me], ssem.at[sem], rsem.at[sem], (_flip(x, fx), _flip(y, fy), c))

    def landed(dst, ssem, rsem, w, k, sender_c, sem_off):
        x, y, c, _ = _place()
        fx, fy = _CHIP_FLIPS[k]
        got = dst[w].at[2 * _flip(x, fx) + _flip(y, fy), dst_rows(w, sender_c)]
        return _remote(got, got, ssem.at[w * 6 + sem_off + k], rsem.at[w * 6 + sem_off + k], (x, y, 1 - c))

    def start(src, dst, ssem, rsem):
        for s in range(ns):
            for k in range(3):
                small(src, dst, ssem, rsem, s, k).start()
        for w in range(n):
            for k in range(3):
                ici(src, dst, ssem, rsem, w, k).start()

    def mid(src, dst, ssem, rsem):
        c = lax.axis_index("c")
        for w in range(n):
            for k in range(3):
                landed(dst, ssem, rsem, w, k, c, 0).wait_recv()
                landed(dst, ssem, rsem, w, k, c, 3).start()

    def end(src, dst, ssem, rsem):
        c = lax.axis_index("c")
        for w in range(n):
            for k in range(3):
                landed(dst, ssem, rsem, w, k, 1 - c, 3).wait_recv()
        for s in range(ns):
            for k in range(3):
                small(src, dst, ssem, rsem, s, k).wait()
        for w in range(n):
            for k in range(3):
                ici(src, dst, ssem, rsem, w, k).wait_send()
                landed(dst, ssem, rsem, w, k, c, 3).wait_send()

    outs = [jax.ShapeDtypeStruct((N_CHIPS, rows, a.shape[1]), a.dtype) for a, (_, rows, _) in zip(shards, into)]
    outs += [jax.ShapeDtypeStruct((N_CHIPS,) + a.shape, a.dtype) for a in smalls]
    filled = [(w, arr) for w, (arr, _, _) in enumerate(into) if arr is not None]
    side = _Side(list(shards) + list(smalls) + [arr for _, arr in filled], outs, 6 * n + 3 * ns, start, mid, end, mid_step)
    side.aliases = {n + ns + i: w for i, (w, _) in enumerate(filled)}
    return side


def _host(body, n_in, n_out, side, grid):
    if side is None:
        return body
    ns_in, ns_out = len(side.ins), len(side.outs)
    n_steps = math.prod(grid)
    mid_step = side.mid_step
    if side.mid is not None and not isinstance(mid_step, int):
        mid_step = min(n_steps - 1, int(mid_step * n_steps))

    def wrapped(*refs):
        ins, s_ins = refs[:n_in], refs[n_in:n_in + ns_in]
        outs = refs[n_in + ns_in:n_in + ns_in + n_out]
        s_outs = refs[n_in + ns_in + n_out:n_in + ns_in + n_out + ns_out]
        rest = refs[n_in + ns_in + n_out + ns_out:]
        sems = rest[-2:]
        step = 0
        for axis, extent in enumerate(grid):
            step = step * extent + pl.program_id(axis)

        @pl.when(step == 0)
        def _():
            side.start(s_ins, s_outs, *sems)

        if side.mid is not None:
            @pl.when(step == mid_step)
            def _():
                side.mid(s_ins, s_outs, *sems)

        body(*ins, *outs, *rest[:-2])

        @pl.when(step == n_steps - 1)
        def _():
            side.end(s_ins, s_outs, *sems)

    return wrapped


def _hosted_call(body, name, grid, in_specs, out_specs, out_shape, args, side, semantics=("parallel", "arbitrary"),
                 scratch=()):
    n_in, n_out = len(in_specs), len(out_specs)
    kern = _host(body, n_in, n_out, side, grid)
    if side is None:
        return pl.pallas_call(kern, name=name, grid=grid, in_specs=in_specs, out_specs=out_specs, out_shape=out_shape,
                              scratch_shapes=list(scratch), compiler_params=_params(semantics))(*args), []
    res = pl.pallas_call(
        kern, name=name, grid=grid, in_specs=in_specs + [HBM] * len(side.ins), out_specs=out_specs + [HBM] * len(side.outs),
        out_shape=list(out_shape) + [pltpu.HBM(s.shape, s.dtype) for s in side.outs],
        scratch_shapes=list(scratch) + [pltpu.SemaphoreType.DMA((side.n_sems,)), pltpu.SemaphoreType.DMA((side.n_sems,))],
        input_output_aliases={n_in + i: n_out + o for i, o in side.aliases.items()},
        compiler_params=_params(("arbitrary",) * len(grid)),
    )(*args, *side.ins)
    return res[:n_out], res[n_out:]


def _pair_swap(halves):
    n = len(halves)

    def body(*refs):
        src, dst = refs[:n], refs[n:2 * n]
        ssem, rsem = refs[2 * n:]
        x, y, c, _ = _place()
        cps = []
        for w in range(n):
            cp = _remote(src[w], dst[w], ssem.at[w], rsem.at[w], (x, y, 1 - c))
            cp.start()
            cps.append(cp)
        for cp in cps:
            cp.wait()

    outs = [jax.ShapeDtypeStruct(h.shape, h.dtype) for h in halves]
    return _comm_call("rs_pair_swap", body, halves, outs, n)


def _gather_small(vec):
    def body(src, dst, ssem, rsem):
        x, y, c, _ = _place()
        me = 4 * x + 2 * y + c
        cps = []
        for r in range(1, 8):
            dev = (_flip(x, r & 4), _flip(y, r & 2), _flip(c, r & 1))
            cp = _remote(src, dst.at[me], ssem.at[r - 1], rsem.at[r - 1], dev)
            cp.start()
            cps.append(cp)
        for cp in cps:
            cp.wait()

    out = jax.ShapeDtypeStruct((8,) + vec.shape, vec.dtype)
    return _comm_call("gather_small", body, [vec], [out], 7)[0]


def _pair_add(name, g, theirs, core):
    n, half, b = theirs.shape
    tr = _tile(half, 256, 16)
    nt = half // tr

    def body(c_ref, g_ref, t_ref, o_ref):
        o_ref[...] = (g_ref[...].astype(F32) + t_ref[...].astype(F32)).astype(o_ref.dtype)

    same = pl.BlockSpec((None, tr, b), lambda j, i, c: (j, i, 0))
    grid_spec = pltpu.PrefetchScalarGridSpec(
        num_scalar_prefetch=1, grid=(n, nt),
        in_specs=[pl.BlockSpec((None, tr, b), lambda j, i, c: (j, c[0] * nt + i, 0)), same], out_specs=same)
    return pl.pallas_call(body, name=name, grid_spec=grid_spec, out_shape=jax.ShapeDtypeStruct(theirs.shape, BF16),
                          compiler_params=_params(("parallel", "parallel")))(core, g, theirs)


def _sum_slots(name, stacked, first=None):
    n, r, c = stacked.shape
    tr = _tile(r, 256, 8)

    def body(*refs):
        s_ref, o_ref = refs[-2], refs[-1]
        acc = refs[0][...].astype(F32) if first is not None else s_ref[0].astype(F32)
        for s in range(0 if first is not None else 1, n):
            acc = acc + s_ref[s].astype(F32)
        o_ref[...] = acc

    row_spec = pl.BlockSpec((tr, c), lambda i: (i, 0))
    in_specs = ([row_spec] if first is not None else []) + [pl.BlockSpec((n, tr, c), lambda i: (0, i, 0))]
    args = ([first] if first is not None else []) + [stacked]
    return pl.pallas_call(
        body, name=name, grid=(r // tr,), in_specs=in_specs, out_specs=row_spec,
        out_shape=jax.ShapeDtypeStruct((r, c), F32), compiler_params=_params(("parallel",)),
    )(*args)


def _adam_math(w, g, m, v):
    bc1, bc2 = 1.0 - ADAM_B1 ** ADAM_STEP, 1.0 - ADAM_B2 ** ADAM_STEP
    nm = ADAM_B1 * m + (1.0 - ADAM_B1) * g
    nv = ADAM_B2 * v + (1.0 - ADAM_B2) * (g * g)
    return -ADAM_LR * ((nm / bc1) / (jnp.sqrt(nv / bc2) + ADAM_EPS) + ADAM_WD * w), nm, nv


def _adamw_halves(name, w, g_mine, g_theirs, m, v, core):
    r, c = w.shape
    h = r // 2
    tr = _tile(h, 128, 8)
    nth = h // tr

    def body(c_ref, w_ref, gm_ref, gt_ref, m_ref, v_ref, g_ref, d_ref, nm_ref, nv_ref):
        g = jnp.where(pl.program_id(0) // nth == c_ref[0], gm_ref[...], gt_ref[...])
        g_ref[...] = g
        d_ref[...], nm_ref[...], nv_ref[...] = _adam_math(w_ref[...], g, m_ref[...], v_ref[...])

    full = pl.BlockSpec((tr, c), lambda i, cr: (i, 0))
    half = pl.BlockSpec((tr, c), lambda i, cr: (i % nth, 0))
    grid_spec = pltpu.PrefetchScalarGridSpec(num_scalar_prefetch=1, grid=(r // tr,),
                                             in_specs=[full, half, half, full, full], out_specs=[full] * 4)
    return pl.pallas_call(body, name=name, grid_spec=grid_spec, out_shape=[jax.ShapeDtypeStruct((r, c), F32)] * 4,
                          compiler_params=_params(("parallel",)))(core, w, g_mine, g_theirs, m, v)


def _adamw(name, w, g, m, v):
    r, c = w.shape
    by_cols = r % 8 != 0 and c % LANES == 0
    tr, tc = (r, _tile(c, 256, LANES)) if by_cols else (_tile(r, 256, 8), c)

    def body(w_ref, g_ref, m_ref, v_ref, d_ref, nm_ref, nv_ref):
        d_ref[...], nm_ref[...], nv_ref[...] = _adam_math(w_ref[...], g_ref[...], m_ref[...], v_ref[...])

    spec = pl.BlockSpec((tr, tc), (lambda i: (0, i)) if by_cols else (lambda i: (i, 0)))
    return pl.pallas_call(
        body, name=name, grid=(c // tc if by_cols else r // tr,), in_specs=[spec] * 4, out_specs=[spec] * 3,
        out_shape=[jax.ShapeDtypeStruct((r, c), F32)] * 3, compiler_params=_params(("parallel",)),
    )(w, g, m, v)


def _pad_cols(a, cols):
    return jnp.pad(a, ((0, 0), (0, cols - a.shape[1])))


def _rot_cols(w):
    h = w.shape[-1] // 2
    return jnp.concatenate([-w[..., h:], w[..., :h]], axis=-1)


def _unrot_cols(d):
    h = d.shape[-1] // 2
    return jnp.concatenate([d[..., h:], -d[..., :h]], axis=-1)


def _logical(g):
    return jnp.transpose(g, (1, 0, 2)).reshape(g.shape[1], N_CHIPS * g.shape[2])


def _chunks(a, n):
    return jnp.transpose(a.reshape(a.shape[0], N_CHIPS, n), (1, 0, 2))


def kernel(x, positions, pre_mix_norm, w_in, q_a_norm, w_uq, kv_a_norm, w_ukv, b_forget, b_gate, w_branch_mla, w_branch_fox, w_out, post_mix_norm, pre_ffn_norm, w_up, conv_w, conv_b, w_down, post_ffn_norm, loss_target, m_pre_mix_norm, m_w_in, m_q_a_norm, m_w_uq, m_kv_a_norm, m_w_ukv, m_b_forget, m_b_gate, m_w_branch_mla, m_w_branch_fox, m_w_out, m_post_mix_norm, m_pre_ffn_norm, m_w_up, m_conv_w, m_conv_b, m_w_down, m_post_ffn_norm, v_pre_mix_norm, v_w_in, v_q_a_norm, v_w_uq, v_kv_a_norm, v_w_ukv, v_b_forget, v_b_gate, v_w_branch_mla, v_w_branch_fox, v_w_out, v_post_mix_norm, v_pre_ffn_norm, v_w_up, v_conv_w, v_conv_b, v_w_down, v_post_ffn_norm):
    n_seq, seq, d = x.shape
    n_tok = n_seq * seq
    d_in = N_CHIPS * w_in.shape[1]
    two_f = N_CHIPS * w_up.shape[1]
    ff_dim = two_f // 2
    assert d_in == QL + KVL + ROPE + 3 * HEADS * FDIM + HEADS + 2 * d
    n_in_shard = w_in.shape[1]
    in_pad = -(-n_in_shard // LANES) * LANES
    hd = HEADS * LANES
    xc, yc, cc = lax.axis_index("x"), lax.axis_index("y"), lax.axis_index("c")
    chip = 2 * xc + yc
    t_attn = _tile(seq, 512, 128)

    shards = [_pad_cols(w_in, in_pad).astype(BF16), w_uq.astype(BF16), w_ukv.astype(BF16), w_branch_mla.astype(BF16),
              w_branch_fox.astype(BF16), w_out.astype(BF16), w_up.astype(BF16), w_down.astype(BF16)]
    cw8 = jnp.pad(conv_w, ((0, 5), (0, 0)))
    put_own = lambda g, s: lax.dynamic_update_slice(g, s[None], (chip, 0, 0))
    gathered, (g_cw,) = _all_gather_weights(shards[:3], [cw8])
    g_in, g_uq, g_ukv = [put_own(g, s) for g, s in zip(gathered, shards[:3])]
    g_cw = put_own(g_cw, cw8)
    n_attn_steps = n_seq * (seq // t_attn)
    side_proj = _gather_side([shards[3], shards[4], shards[5]], [], mid_step=0.9)
    side_ffn = _gather_side([shards[7]], [], mid_step=0.7)
    up_rows = shards[6].shape[0]
    up_cuts = [0, up_rows // 8, up_rows // 2, 7 * up_rows // 8, up_rows]

    def side_up(piece, filled, mid_step):
        lo, hi = up_cuts[piece], up_cuts[piece + 1]
        return _gather_side([shards[6][lo:hi]], [], mid_step=mid_step, into=[(filled, up_rows, lo)])

    o_q, o_kv, o_kpe = 0, QL, QL + KVL
    o_f = o_kpe + ROPE
    o_fl = o_f + 3 * hd
    o_g = o_fl + HEADS

    def chip_cols(lo, hi):
        out = []
        while lo < hi:
            j = lo // n_in_shard
            end = min(hi, (j + 1) * n_in_shard)
            out.append((j, lo - j * n_in_shard, end - j * n_in_shard))
            lo = end
        return out

    take = lambda lo, hi: [g_in[j, :, a:b] for j, a, b in chip_cols(lo, hi)]
    w_kpe = jnp.concatenate(take(o_kpe, o_f), axis=1)
    zeros = lambda n: jnp.zeros((d, n), BF16)
    win_p = jnp.concatenate(
        take(o_g, d_in) + take(o_q, o_kpe) + [w_kpe, zeros(LANES - ROPE), _rot_cols(w_kpe)] + take(o_fl, o_g)
        + [zeros(LANES - ROPE - HEADS)] + take(o_f, o_fl), axis=1)
    n_p = win_p.shape[1]
    cb_gm, cb_gf = 0, 1
    c_lat = 2 * d
    c_kx, c_kr = c_lat + QL + KVL, c_lat + QL + KVL + LANES
    n_pa = c_kr + LANES
    assert n_p == n_pa + 3 * hd

    uq3 = _logical(g_uq).reshape(QL, HEADS, NOPE + ROPE)
    pe = uq3[:, :, NOPE:]
    pad_pe = lambda a: jnp.pad(a, ((0, 0), (0, 0), (0, LANES - ROPE))).reshape(QL, hd)
    wuq_p = jnp.concatenate([uq3[:, :, :NOPE].reshape(QL, hd), pad_pe(pe), pad_pe(_rot_cols(pe))], axis=1)
    ukv3 = _logical(g_ukv).reshape(KVL, HEADS, NOPE + VDIM)
    wukv_p = jnp.concatenate([ukv3[:, :, :NOPE].reshape(KVL, hd), ukv3[:, :, NOPE:].reshape(KVL, hd)], axis=1)

    n_bm, n_up = w_branch_mla.shape[1], w_up.shape[1]
    l_bm, l_up = _Chunked(n_bm), _Chunked(n_up)
    wt = n_up // 2
    n_ut = two_f // wt
    il = lambda cblk: jnp.where(cblk < n_ut // 2, 2 * cblk, 2 * (cblk - n_ut // 2) + 1)
    l_il = _Plain(il)
    to_il = lambda a: a.reshape(a.shape[0], 2, n_ut // 2, wt).transpose(0, 2, 1, 3).reshape(a.shape[0], two_f)
    from_il = lambda a: a.reshape(a.shape[0], n_ut // 2, 2, wt).transpose(0, 2, 1, 3).reshape(a.shape[0], two_f)

    inv_freq = 1.0 / (ROPE_THETA ** (jnp.arange(0, ROPE, 2, dtype=F32) / ROPE))
    ang = positions.astype(F32).reshape(n_tok, 1) * inv_freq
    cos, sin = jnp.cos(ang), jnp.sin(ang)
    cs = _pad_cols(jnp.concatenate([cos, cos], axis=1), LANES)
    sn = _pad_cols(jnp.concatenate([sin, sin], axis=1), LANES)

    row = lambda v: v.reshape(1, -1)
    x2 = x.reshape(n_tok, d)
    tgt = loss_target.reshape(n_tok, d)

    (h,) = _rows("rms_pre_mix", lambda r, v: ([r[0] * _rstd(r[0]) * v[0]], []),
                 [(x2, d, 0)], [row(pre_mix_norm)], [(d, BF16)], [], n_tok)
    proj, got = _mm("proj_in", "nn", h, win_p, n_tok, n_pa, d, side=side_proj)
    g_bm, g_bf, g_out = [put_own(g, s) for g, s in zip(got, side_proj.ins)]
    w_out_full = g_out.reshape(d, d)
    tn_f = _tile(3 * hd, 1024, 128)
    assert n_pa % tn_f == 0
    proj_f, (g_up,) = _mm("proj_in_fox", "nn", h, win_p, n_tok, 3 * hd, d, tn=tn_f, lb=_Plain(lambda cblk: cblk + n_pa // tn_f),
                          out_dtype=BF16, side=side_up(0, None, 0.8))

    bf_vec = jnp.pad(row(b_forget), ((0, 0), (ROPE, LANES - ROPE - HEADS)))

    def lat_fwd(r, v):
        ql, kvl = r[0], r[1]
        return [ql * _rstd(ql) * v[0], kvl * _rstd(kvl) * v[1], r[2] * r[4] + r[3] * r[5]], []

    qn, kvn, rk = _rows("latent_norms", lat_fwd,
                        [(proj, QL, c_lat // QL), (proj, KVL, (c_lat + QL) // KVL), (proj, LANES, c_kx // LANES),
                         (proj, LANES, c_kr // LANES), (cs, LANES, 0), (sn, LANES, 0)],
                        [row(q_a_norm), row(kv_a_norm)], [(QL, BF16), (KVL, BF16), (LANES, BF16)], [], n_tok)
    q_p = _mm("q_up", "nn", qn, wuq_p, n_tok, 3 * hd, QL)
    kv_p = _mm("kv_up", "nn", kvn, wukv_p, n_tok, 2 * hd, KVL, out_dtype=BF16)

    def rope_q(r, v):
        c8, s8 = jnp.tile(r[3], (1, HEADS)), jnp.tile(r[4], (1, HEADS))
        return [r[0], r[1] * c8 + r[2] * s8], []

    q_nope, rq = _rows("rope_q", rope_q, [(q_p, hd, 0), (q_p, hd, 1), (q_p, hd, 2), (cs, LANES, 0), (sn, LANES, 0)], [],
                       [(hd, BF16), (hd, BF16)], [], n_tok)

    mla_q = [(q_nope, 0, False), (rq, 0, False)]
    mla_k = [(kv_p, 0, False), (rk, 0, True)]
    mla_v = (kv_p, 1)
    mla_scale = (NOPE + ROPE) ** -0.5
    (o_mla, lse_mla), (g_up,) = _attn_fwd("mla_fwd", mla_q, mla_k, mla_v, None, CHUNK, mla_scale, n_seq, seq, t_attn,
                                          side=side_up(1, g_up, max(n_attn_steps - 2, 0)))

    c_run = _seq_cumsum("forget_cumsum", proj, c_kr // LANES, n_seq, seq, False,
                        pre=lambda z, b: _log_sigmoid(z + b), vec=bf_vec)
    nb_attn = seq // t_attn
    c_rowf = jnp.transpose(c_run[:, ROPE:ROPE + HEADS].reshape(n_seq, nb_attn, t_attn, HEADS), (0, 1, 3, 2))
    fox_q, fox_k, fox_v = [(proj_f, 0, False)], [(proj_f, 1, False)], (proj_f, 2)
    fox_scale = FDIM ** -0.5
    fox_bias = (c_run, c_rowf)
    (o_fox, lse_fox), (g_up,) = _attn_fwd("fox_fwd", fox_q, fox_k, fox_v, fox_bias, 1, fox_scale, n_seq, seq, t_attn,
                                          side=side_up(2, g_up, max(n_attn_steps - 2, 0)))

    pm, (g_up,) = _mm("branch_mla", "nn", o_mla, g_bm, n_tok, d, hd, lb=l_bm, tn=n_bm, out_dtype=BF16,
                      side=side_up(3, g_up, 0.7))
    g_up = put_own(g_up, shards[6])
    pf = _mm("branch_fox", "nn", o_fox, g_bf, n_tok, d, hd, lb=l_bm, tn=n_bm, out_dtype=BF16)
    bg = row(b_gate)

    def merge(r, v):
        return [_sigmoid(r[0] + v[0]) * r[2] + _sigmoid(r[1] + v[1]) * r[3]], []

    (merged,) = _rows("gate_merge", merge, [(proj, d, cb_gm), (proj, d, cb_gf), (pm, d, 0), (pf, d, 0)],
                      [bg[:, :d], bg[:, d:]], [(d, BF16)], [], n_tok)
    y1 = _mm("mix_out", "nn", merged, w_out_full, n_tok, d, d)

    def resid_norm(r, v):
        x1v = r[0] + r[1] * _rstd(r[1]) * v[0]
        return [x1v, x1v * _rstd(x1v) * v[1]], []

    x1, h2 = _rows("post_mix_pre_ffn", resid_norm, [(x2, d, 0), (y1, d, 0)], [row(post_mix_norm), row(pre_ffn_norm)],
                   [(d, F32), (d, BF16)], [], n_tok)

    u_il, got = _mm("ffn_up", "nn", h2, g_up, n_tok, two_f, d, lb=l_up, lo=l_il, tn=wt, out_dtype=BF16, side=side_ffn)
    w_down_full = put_own(got[0], side_ffn.ins[0]).reshape(ff_dim, d)
    cw_il = to_il(_logical(g_cw)[:3])
    cb_il = to_il(row(conv_b))
    act = _conv_glu_fwd(u_il, cw_il, cb_il, n_seq, seq, wt)
    ff = _mm("ffn_down", "nn", act, w_down_full, n_tok, d, ff_dim)

    def final(r, v):
        x1v, ffv, tg = r
        diff = x1v + ffv * _rstd(ffv) * v[0] - tg
        dx2v = diff / d
        dffv, dg4 = _rms_bwd(ffv, v[0], dx2v)
        sq = jnp.sum(jnp.sum(diff * diff, axis=1, keepdims=True), axis=0, keepdims=True)
        return [dx2v, dffv], [dg4, jnp.broadcast_to(sq, (1, LANES))]

    dx2, dff, dg_post_ffn, sq_sum = _rows("loss_post_ffn_bwd", final, [(x1, d, 0), (ff, d, 0), (tgt, d, 0)],
                                          [row(post_ffn_norm)], [(d, F32), (d, BF16)], [(1, d), (1, LANES)], n_tok)
    rs_parts, rs_landed = {}, {}
    core = jnp.reshape(cc, (1,)).astype(jnp.int32)

    def pair_reduce(tag, names, grads):
        theirs = _pair_split("rs_pair_split_" + tag, grads)
        for nm, g, b in zip(names, grads, theirs):
            rs_parts[nm] = _pair_add("rs_pair_add_" + nm, g, b, core)

    dact = _mm("ffn_down_dx", "nt", dff, w_down_full, n_tok, ff_dim, d, tn=wt, out_dtype=BF16)
    gw_down = _mm("ffn_down_dw", "tn", act, dff, ff_dim, d, n_tok, tm=wt, out_dtype=BF16)
    pair_reduce("down", ["w_down"], [gw_down.reshape(N_CHIPS, ff_dim // N_CHIPS, d)])
    d_il, conv_acc = _conv_glu_bwd_pre(u_il, dact, cw_il, cb_il, n_seq, seq, wt)
    du_il = _conv_bwd_input(d_il, cw_il, n_seq, seq, wt)
    gw_up, got = _mm("ffn_up_dw", "tn", h2, du_il, d, two_f, n_tok, lb=l_il, lo=l_up, tn=wt, out_dtype=BF16,
                     side=_scatter_side([rs_parts["w_down"]]))
    rs_landed["w_down"] = got[0]
    pair_reduce("up", ["w_up"], [gw_up])
    dh2, got = _mm("ffn_up_dx", "nt", du_il, g_up, n_tok, d, two_f, la=l_il, lb=l_up, tk=wt,
                   side=_scatter_side([rs_parts["w_up"]]))
    rs_landed["w_up"] = got[0]

    def mid_bwd(r, v):
        x1v, y1v, dx2v, dh2v = r
        d3, dg3 = _rms_bwd(x1v, v[1], dh2v)
        dx1v = dx2v + d3
        dy1v, dg2 = _rms_bwd(y1v, v[0], dx1v)
        return [dx1v, dy1v], [dg3, dg2]

    dx1, dy1, dg_pre_ffn, dg_post_mix = _rows(
        "pre_ffn_post_mix_bwd", mid_bwd, [(x1, d, 0), (y1, d, 0), (dx2, d, 0), (dh2, d, 0)],
        [row(post_mix_norm), row(pre_ffn_norm)], [(d, F32), (d, BF16)], [(1, d), (1, d)], n_tok)
    dmerged = _mm("mix_out_dx", "nt", dy1, w_out_full, n_tok, d, d, out_dtype=BF16)
    gw_out = _mm("mix_out_dw", "tn", merged, dy1, d, d, n_tok, out_dtype=BF16)

    def gate_bwd(r, v):
        zm, zf, pmv, pfv, dm = r
        gm, gf = _sigmoid(zm + v[0]), _sigmoid(zf + v[1])
        dzm, dzf = dm * pmv * gm * (1.0 - gm), dm * pfv * gf * (1.0 - gf)
        return [dm * gm, dm * gf, jnp.concatenate([dzm, dzf], axis=1)], [_colsum(dzm), _colsum(dzf)]

    dpm, dpf, dz, dbg_m, dbg_f = _rows(
        "gate_merge_bwd", gate_bwd, [(proj, d, cb_gm), (proj, d, cb_gf), (pm, d, 0), (pf, d, 0), (dmerged, d, 0)],
        [bg[:, :d], bg[:, d:]], [(d, BF16), (d, BF16), (2 * d, BF16)], [(1, d), (1, d)], n_tok)
    tk_b = min(n_bm, 512)
    do_mla = _mm("branch_mla_dx", "nt", dpm, g_bm, n_tok, hd, d, lb=l_bm, tk=tk_b)
    do_fox = _mm("branch_fox_dx", "nt", dpf, g_bf, n_tok, hd, d, lb=l_bm, tk=tk_b)
    gw_bm = _mm("branch_mla_dw", "tn", o_mla, dpm, hd, d, n_tok, lo=l_bm, tn=n_bm, out_dtype=BF16)
    gw_bf = _mm("branch_fox_dw", "tn", o_fox, dpf, hd, d, n_tok, lo=l_bm, tn=n_bm, out_dtype=BF16)

    pair_reduce("mix", ["w_out", "w_branch_mla", "w_branch_fox"], [gw_out.reshape(N_CHIPS, d // N_CHIPS, d), gw_bm, gw_bf])
    delta_mla, dob_mla = _attn_delta("mla_delta", o_mla, do_mla, n_tok)
    (dq_nope, drq, dk_nope, drk_g, dv_mla), got = _attn_bwd(
        "mla_bwd", mla_q, mla_k, mla_v, dob_mla, lse_mla, delta_mla, None, CHUNK, mla_scale, n_seq, seq, t_attn, BF16,
        side=_scatter_side([rs_parts[nm] for nm in ("w_out", "w_branch_mla", "w_branch_fox")]))
    rs_landed.update(zip(("w_out", "w_branch_mla", "w_branch_fox"), got))
    delta_fox, dob_fox = _attn_delta("fox_delta", o_fox, do_fox, n_tok)
    (dfq, dfk, dfv, dc_q, dc_k), _ = _attn_bwd("fox_bwd", fox_q, fox_k, fox_v, dob_fox, lse_fox, delta_fox, fox_bias, 1,
                                               fox_scale, n_seq, seq, t_attn, BF16)
    dc_k8 = jnp.transpose(dc_k, (0, 2, 4, 1, 3)).reshape(n_tok, HEADS)
    dc128 = dc_q[0] + dc_q[1] + jnp.pad(dc_k8, ((0, 0), (ROPE, LANES - ROPE - HEADS)))
    dlogf = _seq_cumsum("forget_cumsum_bwd", dc128, 0, n_seq, seq, True)

    def mla_pack(r, v):
        dqn_v, drq_v, dkn_v, dv_v, drk_a, drk_b, c1, s1 = r
        c8, s8 = jnp.tile(c1, (1, HEADS)), jnp.tile(s1, (1, HEADS))
        drk_v = drk_a + drk_b
        return [jnp.concatenate([dqn_v, drq_v * c8, drq_v * s8], axis=1), jnp.concatenate([dkn_v, dv_v], axis=1),
                drk_v * c1, drk_v * s1], []

    dq_p, dkv_p, dkx, dkr = _rows(
        "mla_rope_bwd", mla_pack,
        [(dq_nope, hd, 0), (drq, hd, 0), (dk_nope, hd, 0), (dv_mla, hd, 0), (drk_g[0], LANES, 0), (drk_g[1], LANES, 0),
         (cs, LANES, 0), (sn, LANES, 0)],
        [], [(3 * hd, BF16), (2 * hd, BF16), (LANES, F32), (LANES, F32)], [], n_tok)
    dqn = _mm("q_up_dx", "nt", dq_p, wuq_p, n_tok, QL, 3 * hd)
    gw_uq_p = _mm("q_up_dw", "tn", qn, dq_p, QL, 3 * hd, n_tok, out_dtype=BF16)
    dkvn = _mm("kv_up_dx", "nt", dkv_p, wukv_p, n_tok, KVL, 2 * hd)
    gw_ukv_p = _mm("kv_up_dw", "tn", kvn, dkv_p, KVL, 2 * hd, n_tok, out_dtype=BF16)

    def lat_bwd(r, v):
        ql, kvl, dqn_v, dkvn_v, dkx_v, dkr_v, zblk, dlf = r
        dql, dgq = _rms_bwd(ql, v[0], dqn_v)
        dkvl, dgkv = _rms_bwd(kvl, v[1], dkvn_v)
        dfl = dlf * _sigmoid(-(zblk + v[2]))
        return [jnp.concatenate([dql, dkvl, dkx_v, dkr_v + dfl], axis=1)], [dgq, dgkv, _colsum(dfl)]

    dlat, dg_q, dg_kv, dbf = _rows(
        "latent_bwd", lat_bwd,
        [(proj, QL, c_lat // QL), (proj, KVL, (c_lat + QL) // KVL), (dqn, QL, 0), (dkvn, KVL, 0), (dkx, LANES, 0),
         (dkr, LANES, 0), (proj, LANES, c_kr // LANES), (dlogf, LANES, 0)],
        [row(q_a_norm), row(kv_a_norm), bf_vec], [(QL + KVL + 2 * LANES, BF16)], [(1, QL), (1, KVL), (1, LANES)], n_tok)
    dproj = [dz, dlat, dfq, dfk, dfv]
    gw_in_p = _mm_parts("proj_in_dw", "tn", h, dproj, d, n_p, n_tok, tk=1024, out_dtype=BF16)

    f32 = lambda a: a.astype(F32)
    kr_blk = gw_in_p[:, c_kr:c_kr + LANES]
    d_kpe = (f32(gw_in_p[:, c_kx:c_kx + ROPE]) + _unrot_cols(f32(kr_blk[:, :ROPE]))).astype(BF16)
    in_pieces = [(o_q, gw_in_p, c_lat, QL + KVL), (o_kpe, d_kpe, 0, ROPE), (o_f, gw_in_p, n_pa, 3 * hd),
                 (o_fl, kr_blk, ROPE, HEADS), (o_g, gw_in_p, 0, 2 * d)]
    gc_in = []
    for j in range(N_CHIPS):
        lo, hi, cols = j * n_in_shard, (j + 1) * n_in_shard, []
        for first, arr, at, width in in_pieces:
            a, b = max(lo, first), min(hi, first + width)
            if a < b:
                cols.append(arr[:, at + a - first:at + b - first])
        cols.append(jnp.zeros((d, in_pad - n_in_shard), BF16))
        gc_in.append(jnp.concatenate(cols, axis=1))
    gc_in = jnp.stack(gc_in)
    uq_parts = [gw_uq_p[:, i * hd:(i + 1) * hd].reshape(QL, HEADS, LANES) for i in range(3)]
    d_pe = (f32(uq_parts[1][:, :, :ROPE]) + _unrot_cols(f32(uq_parts[2][:, :, :ROPE]))).astype(BF16)
    gc_uq = _chunks(jnp.concatenate([uq_parts[0], d_pe], axis=2).reshape(QL, HEADS * (NOPE + ROPE)), w_uq.shape[1])
    gc_ukv = _chunks(jnp.concatenate([gw_ukv_p[:, :hd].reshape(KVL, HEADS, NOPE), gw_ukv_p[:, hd:].reshape(KVL, HEADS, VDIM)],
                                     axis=2).reshape(KVL, HEADS * (NOPE + VDIM)), w_ukv.shape[1])
    grads = [gc_in, gc_uq, gc_ukv]

    late = ["w_in", "w_uq", "w_ukv"]
    pair_reduce("late", late, grads)
    dh, got = _mm_parts("proj_in_dx", "nt", dproj, win_p, n_tok, d, n_p, side=_scatter_side([rs_parts[nm] for nm in late]))
    rs_landed.update(zip(late, got))

    def first_bwd(r, v):
        dxa, dg1 = _rms_bwd(r[0], v[0], r[1])
        return [r[2] + dxa], [dg1]

    grad_x, dg_pre_mix = _rows("pre_mix_bwd", first_bwd, [(x2, d, 0), (dh, d, 0), (dx1, d, 0)], [row(pre_mix_norm)],
                               [(d, F32)], [(1, d)], n_tok)
    big = list(rs_parts)
    halves = [_sum_slots("rs_chip_sum_" + nm, rs_landed[nm],
                         first=lax.dynamic_index_in_dim(rs_parts[nm], chip, 0, keepdims=False)) for nm in big]
    other = _pair_swap(halves)
    g_halves = dict(zip(big, zip(halves, other)))

    conv_acc_l = from_il(conv_acc)
    pieces = [dg_pre_mix, dg_q, dg_kv, dbf, dbg_m, dbg_f, dg_post_mix, dg_pre_ffn, conv_acc_l[3:4], dg_post_ffn,
              conv_acc_l[0:1], conv_acc_l[1:2], conv_acc_l[2:3], sq_sum]
    sizes = [p.shape[1] for p in pieces]
    flat = jnp.concatenate(pieces, axis=1)
    n_rows = -(-flat.shape[1] // (8 * LANES)) * 8
    flat = _pad_cols(flat, n_rows * LANES).reshape(n_rows, LANES)
    slots = lax.dynamic_update_slice(_gather_small(flat), flat[None], (2 * chip + cc, 0, 0))
    total = _sum_slots("small_sum", slots).reshape(1, n_rows * LANES)
    offs = [sum(sizes[:i]) for i in range(len(sizes))]
    tot = [total[0, o:o + s] for o, s in zip(offs, sizes)]
    loss = 0.5 * tot[13][0] / d
    g_small = {"pre_mix_norm": tot[0], "q_a_norm": tot[1], "kv_a_norm": tot[2], "b_forget": tot[3][ROPE:ROPE + HEADS],
               "b_gate": jnp.concatenate([tot[4], tot[5]]), "post_mix_norm": tot[6], "pre_ffn_norm": tot[7],
               "conv_b": tot[8], "post_ffn_norm": tot[9]}
    gcw_full = jnp.stack([tot[10], tot[11], tot[12]])
    g_conv_w = lax.dynamic_slice(gcw_full, (0, chip * n_up), (3, n_up))

    given = dict(pre_mix_norm=(pre_mix_norm, m_pre_mix_norm, v_pre_mix_norm), w_in=(w_in, m_w_in, v_w_in),
                 q_a_norm=(q_a_norm, m_q_a_norm, v_q_a_norm), w_uq=(w_uq, m_w_uq, v_w_uq),
                 kv_a_norm=(kv_a_norm, m_kv_a_norm, v_kv_a_norm), w_ukv=(w_ukv, m_w_ukv, v_w_ukv),
                 b_forget=(b_forget, m_b_forget, v_b_forget), b_gate=(b_gate, m_b_gate, v_b_gate),
                 w_branch_mla=(w_branch_mla, m_w_branch_mla, v_w_branch_mla),
                 w_branch_fox=(w_branch_fox, m_w_branch_fox, v_w_branch_fox), w_out=(w_out, m_w_out, v_w_out),
                 post_mix_norm=(post_mix_norm, m_post_mix_norm, v_post_mix_norm),
                 pre_ffn_norm=(pre_ffn_norm, m_pre_ffn_norm, v_pre_ffn_norm), w_up=(w_up, m_w_up, v_w_up),
                 conv_w=(conv_w, m_conv_w, v_conv_w), conv_b=(conv_b, m_conv_b, v_conv_b),
                 w_down=(w_down, m_w_down, v_w_down), post_ffn_norm=(post_ffn_norm, m_post_ffn_norm, v_post_ffn_norm))
    order = list(given)
    grad, delta, new_m, new_v = {}, {}, {}, {}
    for nm in big:
        mine, theirs = g_halves[nm]
        if nm == "w_in":
            full = jnp.concatenate([jnp.where(cc == 0, mine, theirs), jnp.where(cc == 0, theirs, mine)], axis=0)
            grad[nm] = full[:, :n_in_shard]
            tr_out = _adamw("adamw_" + nm, *[jnp.transpose(a) for a in (given[nm][0], grad[nm], given[nm][1], given[nm][2])])
            delta[nm], new_m[nm], new_v[nm] = [jnp.transpose(a) for a in tr_out]
            continue
        grad[nm], delta[nm], new_m[nm], new_v[nm] = _adamw_halves("adamw_" + nm, given[nm][0], mine, theirs, given[nm][1],
                                                                  given[nm][2], core)
    grad["conv_w"] = g_conv_w
    delta["conv_w"], new_m["conv_w"], new_v["conv_w"] = _adamw("adamw_conv_w", conv_w, g_conv_w, m_conv_w, v_conv_w)
    small = list(g_small)
    padded = [-(-g_small[nm].shape[0] // LANES) * LANES for nm in small]
    s_rows = -(-sum(padded) // (8 * LANES)) * 8

    def pack(vals):
        cat = jnp.concatenate([jnp.pad(a, (0, p - a.shape[0])) for a, p in zip(vals, padded)])
        return jnp.pad(cat, (0, s_rows * LANES - cat.shape[0])).reshape(s_rows, LANES)

    packed = _adamw("adamw_small", pack([given[nm][0] for nm in small]), pack([g_small[nm] for nm in small]),
                    pack([given[nm][1] for nm in small]), pack([given[nm][2] for nm in small]))
    s_offs = [sum(padded[:i]) for i in range(len(small))]
    for nm, o in zip(small, s_offs):
        n_el = g_small[nm].shape[0]
        grad[nm] = g_small[nm]
        delta[nm], new_m[nm], new_v[nm] = [p.reshape(-1)[o:o + n_el] for p in packed]
    return (loss, grad_x.reshape(n_seq, seq, d), *[grad[nm] for nm in order], *[delta[nm] for nm in order],
            *[new_m[nm] for nm in order], *[new_v[nm] for nm in order])
```

```python
import functools
import math

import jax
import jax.numpy as jnp
from jax import lax
from jax.experimental import pallas as pl
from jax.experimental.pallas import tpu as pltpu

F32, BF16 = jnp.float32, jnp.bfloat16
MESH = pl.DeviceIdType.MESH

HEADS = 8
NOPE, ROPE, VDIM = 128, 64, 128
QL, KVL = 512, 256
FDIM = 128
CHUNK = 64
ROPE_THETA = 10000.0
EPS = 1e-6
NEG_INF = -1e30
ADAM_LR, ADAM_B1, ADAM_B2, ADAM_EPS, ADAM_WD, ADAM_STEP = 0.001, 0.9, 0.999, 1e-08, 0.01, 10

VMEM_LIMIT_BYTES = 52 * 1024 * 1024
LANES = 128
N_CHIPS = 4


def _params(sem):
    return pltpu.CompilerParams(dimension_semantics=sem, vmem_limit_bytes=VMEM_LIMIT_BYTES)


def _tile(n, target, mult):
    if n <= target:
        return n
    t = (target // mult) * mult
    while t >= mult:
        if n % t == 0:
            return t
        t -= mult
    raise ValueError(f"no tile for {n} (target {target}, multiple of {mult})")


class _Plain:
    def __init__(self, perm=None):
        self.perm = perm

    def spec(self, tr, tc, rc):
        perm = self.perm

        def imap(i, j, k):
            r, c = rc(i, j, k)
            return (r, perm(c) if perm is not None else c)

        return pl.BlockSpec((tr, tc), imap)

    def shape(self, rows, cols):
        return (rows, cols)


class _Chunked:
    def __init__(self, n):
        self.n = n

    def spec(self, tr, tc, rc):
        assert self.n % tc == 0, (self.n, tc)
        per = self.n // tc

        def imap(i, j, k):
            r, c = rc(i, j, k)
            return (c // per, r, c % per)

        return pl.BlockSpec((None, tr, tc), imap)

    def shape(self, rows, cols):
        assert cols == N_CHIPS * self.n
        return (N_CHIPS, rows, self.n)


_DIMS = {"nn": (((1,), (0,)), ((), ())), "nt": (((1,), (1,)), ((), ())), "tn": (((0,), (0,)), ((), ()))}


def _mm_single(name, mode, a, b, m, n, k, tm, tn, la, lb, lo, out_dtype, side):
    if mode == "nn":
        a_spec = la.spec(tm, k, lambda i, j, kk: (i, 0))
        b_spec = lb.spec(k, tn, lambda i, j, kk: (0, j))
    elif mode == "nt":
        a_spec = la.spec(tm, k, lambda i, j, kk: (i, 0))
        b_spec = lb.spec(tn, k, lambda i, j, kk: (j, 0))
    else:
        a_spec = la.spec(k, tm, lambda i, j, kk: (0, i))
        b_spec = lb.spec(k, tn, lambda i, j, kk: (0, j))
    o_spec = lo.spec(tm, tn, lambda i, j, kk: (i, j))
    dims = _DIMS[mode]

    def body(a_ref, b_ref, o_ref):
        o_ref[...] = lax.dot_general(a_ref[...].astype(BF16), b_ref[...].astype(BF16), dims,
                                     preferred_element_type=F32).astype(o_ref.dtype)

    (out,), got = _hosted_call(body, name, (m // tm, n // tn, 1), [a_spec, b_spec], [o_spec],
                               [jax.ShapeDtypeStruct(lo.shape(m, n), out_dtype)], (a, b), side,
                               semantics=("parallel", "parallel", "arbitrary"))
    return out if side is None else (out, got)


def _mm(name, mode, a, b, m, n, k, *, tm=1024, tn=1024, tk=2048, la=None, lb=None, lo=None, out_dtype=F32, side=None):
    la, lb, lo = la or _Plain(), lb or _Plain(), lo or _Plain()
    tm, tn, tk = _tile(m, tm, 128), _tile(n, tn, 128), _tile(k, tk, 128)
    nk = k // tk
    if nk == 1:
        return _mm_single(name, mode, a, b, m, n, k, tm, tn, la, lb, lo, out_dtype, side)
    if mode == "nn":
        a_spec = la.spec(tm, tk, lambda i, j, kk: (i, kk))
        b_spec = lb.spec(tk, tn, lambda i, j, kk: (kk, j))
    elif mode == "nt":
        a_spec = la.spec(tm, tk, lambda i, j, kk: (i, kk))
        b_spec = lb.spec(tn, tk, lambda i, j, kk: (j, kk))
    else:
        a_spec = la.spec(tk, tm, lambda i, j, kk: (kk, i))
        b_spec = lb.spec(tk, tn, lambda i, j, kk: (kk, j))
    o_spec = lo.spec(tm, tn, lambda i, j, kk: (i, j))
    dims = _DIMS[mode]

    def body(a_ref, b_ref, o_ref, acc_ref):
        kk = pl.program_id(2)

        @pl.when(kk == 0)
        def _():
            acc_ref[...] = jnp.zeros_like(acc_ref)

        acc_ref[...] += lax.dot_general(a_ref[...].astype(BF16), b_ref[...].astype(BF16), dims,
                                        preferred_element_type=F32)

        @pl.when(kk == nk - 1)
        def _():
            o_ref[...] = acc_ref[...].astype(o_ref.dtype)

    (out,), got = _hosted_call(body, name, (m // tm, n // tn, nk), [a_spec, b_spec], [o_spec],
                               [jax.ShapeDtypeStruct(lo.shape(m, n), out_dtype)], (a, b), side,
                               semantics=("parallel", "parallel", "arbitrary"), scratch=[pltpu.VMEM((tm, tn), F32)])
    return out if side is None else (out, got)


def _mm_parts(name, mode, a, b, m, n, k, *, part=1024, tm=1024, tn=1024, tk=2048, out_dtype=F32, side=None):
    parts = b if mode == "tn" else a
    widths = [p.shape[1] for p in parts]
    assert all(w % part == 0 for w in widths) and sum(widths) == (n if mode == "tn" else k)
    offs = [sum(widths[:i]) // part for i in range(len(widths))]
    nblk = [w // part for w in widths]
    if mode == "tn":
        tn, tk = part, _tile(k, tk, 128)
    else:
        tk, tn = part, _tile(n, tn, 128)
    tm = _tile(m, tm, 128)
    nk = k // tk
    grid = (m // tm, n // tn, nk)
    np_ = len(parts)

    def inside(idx, p):
        return jnp.logical_and(idx >= offs[p], idx < offs[p] + nblk[p])

    def part_spec(p):
        if mode == "tn":
            def imap(i, j, kk):
                on = inside(j, p)
                return (jnp.where(on, kk, 0), jnp.clip(j - offs[p], 0, nblk[p] - 1))
            return pl.BlockSpec((tk, tn), imap)

        def imap(i, j, kk):
            return (i, jnp.clip(kk - offs[p], 0, nblk[p] - 1))
        return pl.BlockSpec((tm, tk), imap)

    if mode == "tn":
        in_specs = [pl.BlockSpec((tk, tm), lambda i, j, kk: (kk, i))] + [part_spec(p) for p in range(np_)]
        args = [a] + list(parts)
    else:
        in_specs = [part_spec(p) for p in range(np_)] + [pl.BlockSpec((tn, tk), lambda i, j, kk: (j, kk))]
        args = list(parts) + [b]
    dims = _DIMS[mode]

    def body(*refs):
        o_ref, acc_ref = refs[-2], refs[-1]
        j, kk = pl.program_id(1), pl.program_id(2)

        @pl.when(kk == 0)
        def _():
            acc_ref[...] = jnp.zeros_like(acc_ref)

        for p in range(np_):
            @pl.when(inside(j if mode == "tn" else kk, p))
            def _(p=p):
                lhs, rhs = (refs[0], refs[1 + p]) if mode == "tn" else (refs[p], refs[np_])
                acc_ref[...] += lax.dot_general(lhs[...].astype(BF16), rhs[...].astype(BF16), dims, preferred_element_type=F32)

        @pl.when(kk == nk - 1)
        def _():
            o_ref[...] = acc_ref[...].astype(o_ref.dtype)

    (out,), got = _hosted_call(body, name, grid, in_specs, [pl.BlockSpec((tm, tn), lambda i, j, kk: (i, j))],
                               [jax.ShapeDtypeStruct((m, n), out_dtype)], args, side,
                               semantics=("parallel", "parallel", "arbitrary"), scratch=[pltpu.VMEM((tm, tn), F32)])
    return out if side is None else (out, got)


def _rows(name, fn, rows_in, vecs_in, rows_out, accs_out, n_rows, tr=256):
    tr = _tile(n_rows, tr, 16)
    nr, nv, no = len(rows_in), len(vecs_in), len(rows_out)

    def body(*refs):
        ins, vecs = refs[:nr], refs[nr:nr + nv]
        outs, accs = refs[nr + nv:nr + nv + no], refs[nr + nv + no:]
        ro, ac = fn([r[...] for r in ins], [v[...] for v in vecs])
        for o_ref, val in zip(outs, ro):
            o_ref[...] = val.astype(o_ref.dtype)
        if accs:
            @pl.when(pl.program_id(0) == 0)
            def _():
                for a_ref in accs:
                    a_ref[...] = jnp.zeros_like(a_ref)

            for a_ref, val in zip(accs, ac):
                a_ref[...] += val

    in_specs = [pl.BlockSpec((tr, cols), functools.partial(lambda i, cb: (i, cb), cb=cb)) for _, cols, cb in rows_in]
    in_specs += [pl.BlockSpec(v.shape, lambda i: (0, 0)) for v in vecs_in]
    out_specs = [pl.BlockSpec((tr, cols), lambda i: (i, 0)) for cols, _ in rows_out]
    out_specs += [pl.BlockSpec((r, cols), lambda i: (0, 0)) for r, cols in accs_out]
    out_shape = [jax.ShapeDtypeStruct((n_rows, cols), dt) for cols, dt in rows_out]
    out_shape += [jax.ShapeDtypeStruct((r, cols), F32) for r, cols in accs_out]
    res = pl.pallas_call(
        body, name=name, grid=(n_rows // tr,), in_specs=in_specs, out_specs=out_specs, out_shape=out_shape,
        compiler_params=_params(("arbitrary",)),
    )(*[a for a, _, _ in rows_in], *vecs_in)
    return res


def _colsum(v):
    return jnp.sum(v, axis=0, keepdims=True)


def _rstd(x):
    return lax.rsqrt(jnp.mean(x * x, axis=-1, keepdims=True) + EPS)


def _rms_bwd(x, g, dy):
    r = _rstd(x)
    xh = x * r
    dxh = dy * g
    dx = r * (dxh - xh * jnp.mean(dxh * xh, axis=-1, keepdims=True))
    return dx, _colsum(dy * xh)


def _sigmoid(z):
    return 1.0 / (1.0 + jnp.exp(-z))


_GELU_K = math.sqrt(2.0 / math.pi)


def _gelu_parts(g):
    t = jnp.tanh(_GELU_K * (g + 0.044715 * g * g * g))
    gel = 0.5 * g * (1.0 + t)
    dgel = 0.5 * (1.0 + t) + 0.5 * g * (1.0 - t * t) * (_GELU_K * (1.0 + 3.0 * 0.044715 * g * g))
    return gel, dgel


def _diag_visible(t, unit):
    rows = lax.broadcasted_iota(jnp.int32, (t, t), 0)
    cols = lax.broadcasted_iota(jnp.int32, (t, t), 1)
    if unit > 1:
        sh = int(math.log2(unit))
        assert 1 << sh == unit and t % unit == 0
        rows, cols = jnp.right_shift(rows, sh), jnp.right_shift(cols, sh)
    return cols <= rows


def _lane_pick(tile, lane):
    idx = lax.broadcasted_iota(jnp.int32, tile.shape, 1)
    return jnp.sum(jnp.where(idx == lane, tile, 0.0), axis=1, keepdims=True)


def _lane_put(tile, lane, col):
    idx = lax.broadcasted_iota(jnp.int32, tile.shape, 1)
    return jnp.where(idx == lane, col, tile)


def _head_cat(refs, shared, rows, h):
    hs = slice(h * LANES, (h + 1) * LANES)
    vals = [(r[rows, :] if sh else r[rows, hs]).astype(BF16) for r, sh in zip(refs, shared)]
    return vals[0] if len(vals) == 1 else jnp.concatenate(vals, axis=1)


def _blk_rows(i, t):
    return pl.ds(pl.multiple_of(i * t, t), t)


def _piece_specs(pieces, rows, row_idx):
    return [pl.BlockSpec((rows, LANES if sh else HEADS * LANES), functools.partial(lambda b, i, cb: (row_idx(b, i), cb), cb=cb))
            for _, cb, sh in pieces]


def _attn_fwd(name, qp, kp, vp, bias, unit, scale, n_seq, seq, t, side=None):
    nb = seq // t
    n_tok = n_seq * seq
    nq, nk_p = len(qp), len(kp)
    q_sh, k_sh = [p[2] for p in qp], [p[2] for p in kp]
    nbias = 2 if bias is not None else 0

    def body(*refs):
        q_refs, k_refs = refs[:nq], refs[nq:nq + nk_p]
        v_ref = refs[nq + nk_p]
        bias_refs = refs[nq + nk_p + 1:nq + nk_p + 1 + nbias]
        o_ref, lse_ref = refs[nq + nk_p + 1 + nbias:]
        qi = pl.program_id(1)
        lse_tile = jnp.zeros((t, LANES), F32)
        for h in range(HEADS):
            hs = slice(h * LANES, (h + 1) * LANES)
            q = _head_cat(q_refs, q_sh, slice(None), h)
            cq = _lane_pick(bias_refs[0][...], ROPE + h) if bias is not None else None

            def block(kb, carry, diag, h=h, hs=hs, q=q, cq=cq):
                m, l, acc = carry
                rows = _blk_rows(kb, t)
                s = lax.dot_general(q, _head_cat(k_refs, k_sh, rows, h), _DIMS["nt"], preferred_element_type=F32) * scale
                if bias is not None:
                    s = s + cq - bias_refs[1][kb, h:h + 1, :]
                if diag:
                    s = jnp.where(_diag_visible(t, unit), s, NEG_INF)
                m_new = jnp.maximum(m, jnp.max(s, axis=1, keepdims=True))
                alpha = jnp.exp(m - m_new)
                p = jnp.exp(s - m_new)
                l = alpha * l + jnp.sum(p, axis=1, keepdims=True)
                acc = alpha * acc + jnp.dot(p.astype(BF16), v_ref[rows, hs].astype(BF16), preferred_element_type=F32)
                return m_new, l, acc

            init = (jnp.full((t, 1), NEG_INF, F32), jnp.zeros((t, 1), F32), jnp.zeros((t, LANES), F32))
            carry = lax.fori_loop(0, qi, lambda kb, c: block(kb, c, False), init)
            m, l, acc = block(qi, carry, True)
            o_ref[:, hs] = acc / l
            lse_tile = _lane_put(lse_tile, h, m + jnp.log(l))
        lse_ref[...] = lse_tile

    tile_row = lambda b, i: b * nb + i
    seq_row = lambda b, i: b
    lane_tile = pl.BlockSpec((t, LANES), lambda b, i: (b * nb + i, 0))
    in_specs = _piece_specs(qp, t, tile_row) + _piece_specs(kp, seq, seq_row) + _piece_specs([vp + (False,)], seq, seq_row)
    args = [p[0] for p in qp] + [p[0] for p in kp] + [vp[0]]
    if bias is not None:
        in_specs += [lane_tile, pl.BlockSpec((None, nb, HEADS, t), lambda b, i: (b, 0, 0, 0))]
        args += list(bias)
    return _hosted_call(
        body, name, (n_seq, nb), in_specs,
        [pl.BlockSpec((t, HEADS * LANES), lambda b, i: (b * nb + i, 0)), lane_tile],
        [jax.ShapeDtypeStruct((n_tok, HEADS * LANES), F32), jax.ShapeDtypeStruct((n_tok, LANES), F32)], args, side)


def _attn_bwd_dq(name, qp, kp, vp, o, do, lse, bias, unit, scale, n_seq, seq, t, side=None, grad_dtype=F32):
    nb = seq // t
    n_tok = n_seq * seq
    nq, nk_p = len(qp), len(kp)
    q_sh, k_sh = [p[2] for p in qp], [p[2] for p in kp]
    nbias = 2 if bias is not None else 0
    n_in = nq + nk_p + 4 + nbias

    def body(*refs):
        q_refs, k_refs = refs[:nq], refs[nq:nq + nk_p]
        v_ref, o_ref, do_ref, lse_ref = refs[nq + nk_p:nq + nk_p + 4]
        bias_refs = refs[nq + nk_p + 4:n_in]
        dq_refs = refs[n_in:n_in + nq]
        delta_ref, dob_ref = refs[n_in + nq:n_in + nq + 2]
        qi = pl.program_id(1)
        delta_tile = jnp.zeros((t, LANES), F32)
        dc_tile = jnp.zeros((t, LANES), F32)
        lse_all = lse_ref[...]
        for h in range(HEADS):
            hs = slice(h * LANES, (h + 1) * LANES)
            q = _head_cat(q_refs, q_sh, slice(None), h)
            do_f = do_ref[:, hs]
            do_b = do_f.astype(BF16)
            dob_ref[:, hs] = do_b
            delta = jnp.sum(do_f * o_ref[:, hs], axis=1, keepdims=True)
            lse = _lane_pick(lse_all, h)
            cq = _lane_pick(bias_refs[0][...], ROPE + h) if bias is not None else None

            def block(kb, carry, diag, h=h, hs=hs, q=q, cq=cq, do_b=do_b, delta=delta, lse=lse):
                dq_acc, dc_acc = carry
                rows = _blk_rows(kb, t)
                k = _head_cat(k_refs, k_sh, rows, h)
                s = lax.dot_general(q, k, _DIMS["nt"], preferred_element_type=F32) * scale
                if bias is not None:
                    s = s + cq - bias_refs[1][kb, h:h + 1, :]
                if diag:
                    s = jnp.where(_diag_visible(t, unit), s, NEG_INF)
                p = jnp.exp(s - lse)
                dp = lax.dot_general(do_b, v_ref[rows, hs].astype(BF16), _DIMS["nt"], preferred_element_type=F32)
                ds = p * (dp - delta)
                return (dq_acc + jnp.dot(ds.astype(BF16), k, preferred_element_type=F32),
                        dc_acc + jnp.sum(ds, axis=1, keepdims=True))

            init = (jnp.zeros((t, nq * LANES), F32), jnp.zeros((t, 1), F32))
            carry = lax.fori_loop(0, qi, lambda kb, c: block(kb, c, False), init)
            dq_acc, dc_acc = block(qi, carry, True)
            for n_p in range(nq):
                dq_refs[n_p][:, hs] = (dq_acc[:, n_p * LANES:(n_p + 1) * LANES] * scale).astype(grad_dtype)
            delta_tile = _lane_put(delta_tile, h, delta)
            dc_tile = _lane_put(dc_tile, ROPE + h, dc_acc)
        delta_ref[...] = delta_tile
        if bias is not None:
            refs[n_in + nq + 2][...] = dc_tile

    tile_row = lambda b, i: b * nb + i
    seq_row = lambda b, i: b
    lane_tile = pl.BlockSpec((t, LANES), lambda b, i: (b * nb + i, 0))
    head_tile = pl.BlockSpec((t, HEADS * LANES), lambda b, i: (b * nb + i, 0))
    in_specs = _piece_specs(qp, t, tile_row) + _piece_specs(kp, seq, seq_row) + _piece_specs([vp + (False,)], seq, seq_row)
    in_specs += [head_tile, head_tile, lane_tile]
    args = [p[0] for p in qp] + [p[0] for p in kp] + [vp[0], o, do, lse]
    if bias is not None:
        in_specs += [lane_tile, pl.BlockSpec((None, nb, HEADS, t), lambda b, i: (b, 0, 0, 0))]
        args += list(bias)
    out_specs = [head_tile] * nq + [lane_tile, head_tile] + ([lane_tile] if bias is not None else [])
    out_shape = [jax.ShapeDtypeStruct((n_tok, HEADS * LANES), grad_dtype)] * nq
    out_shape += [jax.ShapeDtypeStruct((n_tok, LANES), F32), jax.ShapeDtypeStruct((n_tok, HEADS * LANES), BF16)]
    if bias is not None:
        out_shape.append(jax.ShapeDtypeStruct((n_tok, LANES), F32))
    return _hosted_call(body, name, (n_seq, nb), in_specs, out_specs, out_shape, args, side)


def _attn_bwd_dkv(name, qp, kp, vp, dob, lse, delta, bias, unit, scale, n_seq, seq, t, side=None, grad_dtype=F32):
    nb = seq // t
    n_tok = n_seq * seq
    nq, nk_p = len(qp), len(kp)
    q_sh, k_sh = [p[2] for p in qp], [p[2] for p in kp]
    nbias = 2 if bias is not None else 0
    n_in = nq + nk_p + 4 + nbias

    def body(*refs):
        q_refs, k_refs = refs[:nq], refs[nq:nq + nk_p]
        v_ref, dob_ref, lse_ref, delta_ref = refs[nq + nk_p:nq + nk_p + 4]
        bias_refs = refs[nq + nk_p + 4:n_in]
        dk_refs = refs[n_in:n_in + nk_p]
        dv_ref = refs[n_in + nk_p]
        ki = pl.program_id(1)
        shared_acc = [jnp.zeros((t, LANES), F32) for _ in range(nk_p)]
        for h in range(HEADS):
            hs = slice(h * LANES, (h + 1) * LANES)
            k = _head_cat(k_refs, k_sh, slice(None), h)
            v = v_ref[:, hs].astype(BF16)
            ck = bias_refs[1][h:h + 1, :] if bias is not None else None

            def block(qb, carry, diag, h=h, hs=hs, k=k, v=v, ck=ck):
                dk_acc, dv_acc, dc_acc = carry
                rows = _blk_rows(qb, t)
                q = _head_cat(q_refs, q_sh, rows, h)
                s = lax.dot_general(q, k, _DIMS["nt"], preferred_element_type=F32) * scale
                if bias is not None:
                    s = s + _lane_pick(bias_refs[0][rows, :], ROPE + h) - ck
                if diag:
                    s = jnp.where(_diag_visible(t, unit), s, NEG_INF)
                p = jnp.exp(s - _lane_pick(lse_ref[rows, :], h))
                do_b = dob_ref[rows, hs]
                dp = lax.dot_general(do_b, v, _DIMS["nt"], preferred_element_type=F32)
                ds = p * (dp - _lane_pick(delta_ref[rows, :], h))
                return (dk_acc + lax.dot_general(ds.astype(BF16), q, _DIMS["tn"], preferred_element_type=F32),
                        dv_acc + lax.dot_general(p.astype(BF16), do_b, _DIMS["tn"], preferred_element_type=F32),
                        dc_acc - jnp.sum(ds, axis=0, keepdims=True))

            init = (jnp.zeros((t, nk_p * LANES), F32), jnp.zeros((t, LANES), F32), jnp.zeros((1, t), F32))
            carry = block(ki, init, True)
            dk_acc, dv_acc, dc_acc = lax.fori_loop(ki + 1, nb, lambda qb, c: block(qb, c, False), carry)
            for n_p in range(nk_p):
                part = dk_acc[:, n_p * LANES:(n_p + 1) * LANES] * scale
                if k_sh[n_p]:
                    shared_acc[n_p] = shared_acc[n_p] + part
                else:
                    dk_refs[n_p][:, hs] = part.astype(grad_dtype)
            dv_ref[:, hs] = dv_acc.astype(grad_dtype)
            if bias is not None:
                refs[n_in + nk_p + 1][h:h + 1, :] = dc_acc
        for n_p in range(nk_p):
            if k_sh[n_p]:
                dk_refs[n_p][...] = shared_acc[n_p]

    tile_row = lambda b, i: b * nb + i
    seq_row = lambda b, i: b
    lane_seq = pl.BlockSpec((seq, LANES), lambda b, i: (b, 0))
    head_tile = pl.BlockSpec((t, HEADS * LANES), lambda b, i: (b * nb + i, 0))
    row_tile = pl.BlockSpec((None, None, HEADS, t), lambda b, i: (b, i, 0, 0))
    in_specs = _piece_specs(qp, seq, seq_row) + _piece_specs(kp, t, tile_row) + _piece_specs([vp + (False,)], t, tile_row)
    in_specs += [pl.BlockSpec((seq, HEADS * LANES), lambda b, i: (b, 0)), lane_seq, lane_seq]
    args = [p[0] for p in qp] + [p[0] for p in kp] + [vp[0], dob, lse, delta]
    if bias is not None:
        in_specs += [lane_seq, row_tile]
        args += list(bias)
    out_specs = [pl.BlockSpec((t, LANES if sh else HEADS * LANES), lambda b, i: (b * nb + i, 0)) for sh in k_sh] + [head_tile]
    out_shape = [jax.ShapeDtypeStruct((n_tok, LANES), F32) if sh else jax.ShapeDtypeStruct((n_tok, HEADS * LANES), grad_dtype)
                 for sh in k_sh]
    out_shape.append(jax.ShapeDtypeStruct((n_tok, HEADS * LANES), grad_dtype))
    if bias is not None:
        out_specs.append(row_tile)
        out_shape.append(jax.ShapeDtypeStruct((n_seq, nb, HEADS, t), F32))
    return _hosted_call(body, name, (n_seq, nb), in_specs, out_specs, out_shape, args, side)


HEAD_GROUPS = 2


def _attn_delta(name, o, do, n_tok):
    def fn(r, v):
        o_v, do_v = r
        tile = jnp.zeros((o_v.shape[0], LANES), F32)
        for h in range(HEADS):
            hs = slice(h * LANES, (h + 1) * LANES)
            tile = _lane_put(tile, h, jnp.sum(do_v[:, hs] * o_v[:, hs], axis=1, keepdims=True))
        return [tile, do_v], []

    return _rows(name, fn, [(o, HEADS * LANES, 0), (do, HEADS * LANES, 0)], [], [(LANES, F32), (HEADS * LANES, BF16)], [], n_tok)


def _attn_bwd(name, qp, kp, vp, dob, lse, delta, bias, unit, scale, n_seq, seq, t, grad_dtype, side=None):
    nb = seq // t
    n_tok = n_seq * seq
    ng = HEAD_GROUPS
    hg = HEADS // ng
    gw = hg * LANES
    nq, nk_p = len(qp), len(kp)
    q_sh, k_sh = [p[2] for p in qp], [p[2] for p in kp]
    assert not any(q_sh) and nq == nk_p
    nbias = 2 if bias is not None else 0
    n_in = nq + nk_p + 4 + nbias
    n_out = nq + nk_p + 1 + nbias

    def body(*refs):
        q_refs, k_refs = refs[:nq], refs[nq:nq + nk_p]
        v_ref, dob_ref, lse_ref, delta_ref = refs[nq + nk_p:nq + nk_p + 4]
        bias_refs = refs[nq + nk_p + 4:n_in]
        dq_refs, dk_refs = refs[n_in:n_in + nq], refs[n_in + nq:n_in + nq + nk_p]
        dv_ref = refs[n_in + nq + nk_p]
        dq_s, dcq_s = refs[n_in + n_out:]
        g, ki = pl.program_id(1), pl.program_id(2)

        @pl.when(ki == 0)
        def _():
            dq_s[...] = jnp.zeros_like(dq_s)
            dcq_s[...] = jnp.zeros_like(dcq_s)

        shared_acc = [jnp.zeros((t, LANES), F32) for _ in range(nk_p)]
        for hl in range(hg):
            h = g * hg + hl
            hs = slice(hl * LANES, (hl + 1) * LANES)
            k = _head_cat(k_refs, k_sh, slice(None), hl)
            v = v_ref[:, hs].astype(BF16)
            ck = bias_refs[1][pl.ds(h, 1), :] if bias is not None else None

            def block(qb, carry, diag, h=h, hl=hl, hs=hs, k=k, v=v, ck=ck):
                dk_acc, dv_acc, dc_acc = carry
                rows = _blk_rows(qb, t)
                q = _head_cat(q_refs, q_sh, rows, hl)
                s = lax.dot_general(q, k, _DIMS["nt"], preferred_element_type=F32) * scale
                if bias is not None:
                    s = s + _lane_pick(bias_refs[0][rows, :], ROPE + h) - ck
                if diag:
                    s = jnp.where(_diag_visible(t, unit), s, NEG_INF)
                p = jnp.exp(s - _lane_pick(lse_ref[rows, :], h))
                do_b = dob_ref[rows, hs]
                dp = lax.dot_general(do_b, v, _DIMS["nt"], preferred_element_type=F32)
                ds = p * (dp - _lane_pick(delta_ref[rows, :], h))
                ds_b = ds.astype(BF16)
                dq_blk = jnp.dot(ds_b, k, preferred_element_type=F32)
                for n_p in range(nq):
                    dq_s[rows, n_p * gw + hl * LANES:n_p * gw + (hl + 1) * LANES] += dq_blk[:, n_p * LANES:(n_p + 1) * LANES]
                if bias is not None:
                    lane = lax.broadcasted_iota(jnp.int32, (t, LANES), 1)
                    dcq_s[rows, :] += jnp.where(lane == ROPE + h, jnp.sum(ds, axis=1, keepdims=True), 0.0)
                return (dk_acc + lax.dot_general(ds_b, q, _DIMS["tn"], preferred_element_type=F32),
                        dv_acc + lax.dot_general(p.astype(BF16), do_b, _DIMS["tn"], preferred_element_type=F32),
                        dc_acc - jnp.sum(ds, axis=0, keepdims=True))

            init = (jnp.zeros((t, nk_p * LANES), F32), jnp.zeros((t, LANES), F32), jnp.zeros((1, t), F32))
            carry = block(ki, init, True)
            dk_acc, dv_acc, dc_acc = lax.fori_loop(ki + 1, nb, lambda qb, c: block(qb, c, False), carry)
            for n_p in range(nk_p):
                part = dk_acc[:, n_p * LANES:(n_p + 1) * LANES] * scale
                if k_sh[n_p]:
                    shared_acc[n_p] = shared_acc[n_p] + part
                else:
                    dk_refs[n_p][:, hs] = part.astype(grad_dtype)
            dv_ref[:, hs] = dv_acc.astype(grad_dtype)
            if bias is not None:
                refs[n_in + n_out - 1][hl:hl + 1, :] = dc_acc
        for n_p in range(nk_p):
            if k_sh[n_p]:
                dk_refs[n_p][...] = shared_acc[n_p]

        @pl.when(ki == nb - 1)
        def _():
            for n_p in range(nq):
                dq_refs[n_p][...] = (dq_s[:, n_p * gw:(n_p + 1) * gw] * scale).astype(grad_dtype)
            if bias is not None:
                refs[n_in + n_out - 2][...] = dcq_s[...]

    def spec(rows, row_idx, cb, shared):
        if shared:
            return pl.BlockSpec((rows, LANES), lambda b, g, i: (row_idx(b, i), cb))
        return pl.BlockSpec((rows, gw), lambda b, g, i: (row_idx(b, i), cb * ng + g))

    tile_row = lambda b, i: b * nb + i
    seq_row = lambda b, i: b
    lane_seq = pl.BlockSpec((seq, LANES), lambda b, g, i: (b, 0))
    in_specs = [spec(seq, seq_row, cb, sh) for _, cb, sh in qp] + [spec(t, tile_row, cb, sh) for _, cb, sh in kp]
    in_specs += [spec(t, tile_row, vp[1], False), spec(seq, seq_row, 0, False), lane_seq, lane_seq]
    args = [p[0] for p in qp] + [p[0] for p in kp] + [vp[0], dob, lse, delta]
    if bias is not None:
        in_specs += [lane_seq, pl.BlockSpec((None, None, HEADS, t), lambda b, g, i: (b, i, 0, 0))]
        args += list(bias)
    group_tile = pl.BlockSpec((None, t, LANES), lambda b, g, i: (g, b * nb + i, 0))
    out_specs = [spec(seq, seq_row, 0, False)] * nq
    out_specs += [group_tile if sh else spec(t, tile_row, 0, False) for sh in k_sh] + [spec(t, tile_row, 0, False)]
    head_shape = jax.ShapeDtypeStruct((n_tok, HEADS * LANES), grad_dtype)
    out_shape = [head_shape] * nq + [jax.ShapeDtypeStruct((ng, n_tok, LANES), F32) if sh else head_shape for sh in k_sh]
    out_shape.append(head_shape)
    if bias is not None:
        out_specs += [pl.BlockSpec((None, seq, LANES), lambda b, g, i: (g, b, 0)),
                      pl.BlockSpec((None, None, None, hg, t), lambda b, g, i: (b, g, i, 0, 0))]
        out_shape += [jax.ShapeDtypeStruct((ng, n_tok, LANES), F32), jax.ShapeDtypeStruct((n_seq, ng, nb, hg, t), F32)]
    return _hosted_call(body, name, (n_seq, ng, nb), in_specs, out_specs, out_shape, args, side,
                        semantics=("parallel", "arbitrary", "arbitrary"),
                        scratch=[pltpu.VMEM((seq, nq * gw), F32), pltpu.VMEM((seq, LANES), F32)])


def _old_attn_bwd_dq(name, qp, kp, vp, o, do, lse, bias, unit, scale, n_seq, seq, t):
    nb = seq // t
    n_tok = n_seq * seq
    nq, nk_p = len(qp), len(kp)
    nbias = 2 if bias is not None else 0
    n_in = nq + nk_p + 4 + nbias
    n_out = nq + (1 if bias is not None else 0)

    def body(*refs):
        q_refs, k_refs = refs[:nq], refs[nq:nq + nk_p]
        v_ref, o_ref, do_ref, lse_ref = refs[nq + nk_p:nq + nk_p + 4]
        bias_refs = refs[nq + nk_p + 4:n_in]
        outs = refs[n_in:n_in + n_out]
        dq_s, delta_s, dc_s = refs[n_in + n_out:]
        qi, ki = pl.program_id(2), pl.program_id(3)

        @pl.when(ki == 0)
        def _():
            dq_s[...] = jnp.zeros_like(dq_s)
            dc_s[...] = jnp.zeros_like(dc_s)
            delta_s[...] = jnp.sum(do_ref[...] * o_ref[...], axis=1, keepdims=True)

        @pl.when(ki <= qi)
        def _():
            s = _scores(q_refs, k_refs, bias_refs, qi, ki, t, unit, scale)
            p = jnp.exp(s - lse_ref[...])
            dp = lax.dot_general(do_ref[...].astype(BF16), v_ref[...].astype(BF16), _DIMS["nt"],
                                 preferred_element_type=F32)
            ds = p * (dp - delta_s[...])
            dq_s[...] += jnp.dot(ds.astype(BF16), _cat(k_refs), preferred_element_type=F32)
            dc_s[...] += jnp.sum(ds, axis=1, keepdims=True)

        @pl.when(ki == qi)
        def _():
            for n_p in range(nq):
                outs[n_p][...] = dq_s[:, n_p * LANES:(n_p + 1) * LANES] * scale
            if bias is not None:
                outs[nq][...] = dc_s[...]

    q_row = lambda b, i, j: b * nb + i
    k_row = lambda b, i, j: b * nb + jnp.minimum(j, i)
    head_q = pl.BlockSpec((t, LANES), lambda b, h, i, j: (b * nb + i, h))
    col_q = pl.BlockSpec((None, t, 1), lambda b, h, i, j: (h, b * nb + i, 0))
    in_specs = [_piece_spec(t, p, q_row) for p in qp] + [_piece_spec(t, p, k_row) for p in kp]
    in_specs += [_piece_spec(t, vp, k_row), head_q, head_q, col_q]
    args = [p[0] for p in qp] + [p[0] for p in kp] + [vp[0], o, do, lse]
    if bias is not None:
        in_specs += [col_q, pl.BlockSpec((None, 1, t), lambda b, h, i, j: (b * HEADS + h, 0, jnp.minimum(j, i)))]
        args += list(bias)
    out_specs = [head_q] * nq + ([col_q] if bias is not None else [])
    out_shape = [jax.ShapeDtypeStruct((n_tok, HEADS * LANES), F32)] * nq
    if bias is not None:
        out_shape.append(jax.ShapeDtypeStruct((HEADS, n_tok, 1), F32))
    return pl.pallas_call(
        body, name=name, grid=(n_seq, HEADS, nb, nb), in_specs=in_specs, out_specs=out_specs, out_shape=out_shape,
        scratch_shapes=[pltpu.VMEM((t, nq * LANES), F32), pltpu.VMEM((t, 1), F32), pltpu.VMEM((t, 1), F32)],
        compiler_params=_params(("parallel", "parallel", "arbitrary", "arbitrary")),
    )(*args)


def _old_attn_bwd_dkv(name, qp, kp, vp, o, do, lse, bias, unit, scale, n_seq, seq, t):
    nb = seq // t
    n_tok = n_seq * seq
    nq, nk_p = len(qp), len(kp)
    nbias = 2 if bias is not None else 0
    n_in = nq + nk_p + 4 + nbias
    n_out = nk_p + 1 + (1 if bias is not None else 0)

    def body(*refs):
        q_refs, k_refs = refs[:nq], refs[nq:nq + nk_p]
        v_ref, o_ref, do_ref, lse_ref = refs[nq + nk_p:nq + nk_p + 4]
        bias_refs = refs[nq + nk_p + 4:n_in]
        outs = refs[n_in:n_in + n_out]
        dk_s, dv_s, dc_s = refs[n_in + n_out:]
        ki, qi = pl.program_id(2), pl.program_id(3)

        @pl.when(qi == 0)
        def _():
            dk_s[...] = jnp.zeros_like(dk_s)
            dv_s[...] = jnp.zeros_like(dv_s)
            dc_s[...] = jnp.zeros_like(dc_s)

        @pl.when(qi >= ki)
        def _():
            s = _scores(q_refs, k_refs, bias_refs, qi, ki, t, unit, scale)
            p = jnp.exp(s - lse_ref[...])
            do_b = do_ref[...].astype(BF16)
            delta = jnp.sum(do_ref[...] * o_ref[...], axis=1, keepdims=True)
            dp = lax.dot_general(do_b, v_ref[...].astype(BF16), _DIMS["nt"], preferred_element_type=F32)
            ds = p * (dp - delta)
            dv_s[...] += lax.dot_general(p.astype(BF16), do_b, _DIMS["tn"], preferred_element_type=F32)
            dk_s[...] += lax.dot_general(ds.astype(BF16), _cat(q_refs), _DIMS["tn"], preferred_element_type=F32)
            dc_s[...] -= jnp.sum(ds, axis=0, keepdims=True)

        @pl.when(qi == nb - 1)
        def _():
            for n_p in range(nk_p):
                outs[n_p][...] = dk_s[:, n_p * LANES:(n_p + 1) * LANES] * scale
            outs[nk_p][...] = dv_s[...]
            if bias is not None:
                outs[nk_p + 1][...] = dc_s[...]

    q_row = lambda b, i, j: b * nb + jnp.maximum(j, i)
    k_row = lambda b, i, j: b * nb + i
    head_q = pl.BlockSpec((t, LANES), lambda b, h, i, j: (b * nb + jnp.maximum(j, i), h))
    col_q = pl.BlockSpec((None, t, 1), lambda b, h, i, j: (h, b * nb + jnp.maximum(j, i), 0))
    head_k = pl.BlockSpec((t, LANES), lambda b, h, i, j: (b * nb + i, h))
    row_k = pl.BlockSpec((None, 1, t), lambda b, h, i, j: (b * HEADS + h, 0, i))
    in_specs = [_piece_spec(t, p, q_row) for p in qp] + [_piece_spec(t, p, k_row) for p in kp]
    in_specs += [_piece_spec(t, vp, k_row), head_q, head_q, col_q]
    args = [p[0] for p in qp] + [p[0] for p in kp] + [vp[0], o, do, lse]
    if bias is not None:
        in_specs += [col_q, row_k]
        args += list(bias)
    out_specs = [head_k] * (nk_p + 1) + ([row_k] if bias is not None else [])
    out_shape = [jax.ShapeDtypeStruct((n_tok, HEADS * LANES), F32)] * (nk_p + 1)
    if bias is not None:
        out_shape.append(jax.ShapeDtypeStruct((n_seq * HEADS, 1, seq), F32))
    return pl.pallas_call(
        body, name=name, grid=(n_seq, HEADS, nb, nb), in_specs=in_specs, out_specs=out_specs, out_shape=out_shape,
        scratch_shapes=[pltpu.VMEM((t, nk_p * LANES), F32), pltpu.VMEM((t, LANES), F32), pltpu.VMEM((1, t), F32)],
        compiler_params=_params(("parallel", "parallel", "arbitrary", "arbitrary")),
    )(*args)


def _seq_cumsum(name, x, col_block, n_seq, seq, reverse, pre=None, vec=None):
    t = _tile(seq, 256, 128)
    nb = seq // t

    def body(*refs):
        x_ref = refs[0]
        vec_ref = refs[1] if vec is not None else None
        o_ref, carry = refs[-2], refs[-1]

        @pl.when(pl.program_id(1) == 0)
        def _():
            carry[...] = jnp.zeros_like(carry)

        v = x_ref[...]
        if pre is not None:
            v = pre(v, vec_ref[...])
        r = lax.broadcasted_iota(jnp.int32, (t, t), 0)
        c = lax.broadcasted_iota(jnp.int32, (t, t), 1)
        tri = jnp.where((c >= r) if reverse else (c <= r), 1.0, 0.0).astype(BF16)
        hi = v.astype(BF16)
        mid = (v - hi.astype(F32)).astype(BF16)
        lo = (v - hi.astype(F32) - mid.astype(F32)).astype(BF16)
        acc = jnp.dot(tri, hi, preferred_element_type=F32)
        acc += jnp.dot(tri, mid, preferred_element_type=F32)
        acc += jnp.dot(tri, lo, preferred_element_type=F32)
        o_ref[...] = acc + carry[...]
        carry[...] += _colsum(v)

    blk = (lambda b, i: (b * nb + nb - 1 - i)) if reverse else (lambda b, i: (b * nb + i))
    in_specs = [pl.BlockSpec((t, LANES), lambda b, i: (blk(b, i), col_block))]
    args = [x]
    if vec is not None:
        in_specs.append(pl.BlockSpec(vec.shape, lambda b, i: (0, 0)))
        args.append(vec)
    return pl.pallas_call(
        body, name=name, grid=(n_seq, nb), in_specs=in_specs,
        out_specs=pl.BlockSpec((t, LANES), lambda b, i: (blk(b, i), 0)),
        out_shape=jax.ShapeDtypeStruct((n_seq * seq, LANES), F32),
        scratch_shapes=[pltpu.VMEM((1, LANES), F32)],
        compiler_params=_params(("arbitrary", "arbitrary")),
    )(*args)


def _log_sigmoid(z):
    return -(jnp.maximum(-z, 0.0) + jnp.log(1.0 + jnp.exp(-jnp.abs(z))))


def _shift_down(u, prev_ref, n):
    out = pltpu.roll(u, n, 0)
    row = lax.broadcasted_iota(jnp.int32, u.shape, 0)
    for r in range(n):
        out = jnp.where(row == r, prev_ref[8 - n + r:8 - n + r + 1, :], out)
    return out


def _shift_up(u, next_ref, n):
    ts = u.shape[0]
    out = pltpu.roll(u, ts - n, 0)
    row = lax.broadcasted_iota(jnp.int32, u.shape, 0)
    for r in range(n):
        out = jnp.where(row == ts - n + r, next_ref[r:r + 1, :], out)
    return out


def _conv_taps(u, prev_ref, w_ref, b_ref):
    s1, s2 = _shift_down(u, prev_ref, 1), _shift_down(u, prev_ref, 2)
    return (w_ref[0:1, :] * s2 + w_ref[1:2, :] * s1 + w_ref[2:3, :] * u) + b_ref[...], s1, s2


def _conv_glu_fwd(u_il, cw_il, cb_il, n_seq, seq, wt):
    n_tok, two_f = u_il.shape
    nct = two_f // (2 * wt)
    ts = _tile(seq, 256, 8)
    ns = seq // ts

    def body(u_ref, w_ref, b_ref, a_ref, carry):
        @pl.when(pl.program_id(2) == 0)
        def _():
            carry[...] = jnp.zeros_like(carry)

        u = u_ref[...].astype(F32)
        uc, _, _ = _conv_taps(u, carry, w_ref, b_ref)
        gel, _ = _gelu_parts(uc[:, :wt])
        a_ref[...] = (gel * uc[:, wt:]).astype(a_ref.dtype)
        carry[...] = u[ts - 8:, :]

    return pl.pallas_call(
        body, name="conv_glu_fwd", grid=(nct, n_seq, ns),
        in_specs=[pl.BlockSpec((ts, 2 * wt), lambda j, b, s: (b * ns + s, j)),
                  pl.BlockSpec((3, 2 * wt), lambda j, b, s: (0, j)),
                  pl.BlockSpec((1, 2 * wt), lambda j, b, s: (0, j))],
        out_specs=pl.BlockSpec((ts, wt), lambda j, b, s: (b * ns + s, j)),
        out_shape=jax.ShapeDtypeStruct((n_tok, two_f // 2), BF16),
        scratch_shapes=[pltpu.VMEM((8, 2 * wt), F32)],
        compiler_params=_params(("parallel", "arbitrary", "arbitrary")),
    )(u_il, cw_il, cb_il)


def _conv_glu_bwd_pre(u_il, da, cw_il, cb_il, n_seq, seq, wt):
    n_tok, two_f = u_il.shape
    nct = two_f // (2 * wt)
    ts = _tile(seq, 256, 8)
    ns = seq // ts

    def body(u_ref, da_ref, w_ref, b_ref, d_ref, acc_ref, carry):
        first = jnp.logical_and(pl.program_id(1) == 0, pl.program_id(2) == 0)

        @pl.when(first)
        def _():
            acc_ref[...] = jnp.zeros_like(acc_ref)

        @pl.when(pl.program_id(2) == 0)
        def _():
            carry[...] = jnp.zeros_like(carry)

        u = u_ref[...].astype(F32)
        uc, s1, s2 = _conv_taps(u, carry, w_ref, b_ref)
        gel, dgel = _gelu_parts(uc[:, :wt])
        da_v = da_ref[...].astype(F32)
        d = jnp.concatenate([da_v * uc[:, wt:] * dgel, da_v * gel], axis=1)
        d_ref[...] = d.astype(d_ref.dtype)
        acc_ref[0:1, :] += _colsum(d * s2)
        acc_ref[1:2, :] += _colsum(d * s1)
        acc_ref[2:3, :] += _colsum(d * u)
        acc_ref[3:4, :] += _colsum(d)
        carry[...] = u[ts - 8:, :]

    return pl.pallas_call(
        body, name="conv_glu_bwd_pre", grid=(nct, n_seq, ns),
        in_specs=[pl.BlockSpec((ts, 2 * wt), lambda j, b, s: (b * ns + s, j)),
                  pl.BlockSpec((ts, wt), lambda j, b, s: (b * ns + s, j)),
                  pl.BlockSpec((3, 2 * wt), lambda j, b, s: (0, j)),
                  pl.BlockSpec((1, 2 * wt), lambda j, b, s: (0, j))],
        out_specs=[pl.BlockSpec((ts, 2 * wt), lambda j, b, s: (b * ns + s, j)),
                   pl.BlockSpec((8, 2 * wt), lambda j, b, s: (0, j))],
        out_shape=[jax.ShapeDtypeStruct((n_tok, two_f), BF16), jax.ShapeDtypeStruct((8, two_f), F32)],
        scratch_shapes=[pltpu.VMEM((8, 2 * wt), F32)],
        compiler_params=_params(("parallel", "arbitrary", "arbitrary")),
    )(u_il, da, cw_il, cb_il)


def _conv_bwd_input(d_il, cw_il, n_seq, seq, wt):
    n_tok, two_f = d_il.shape
    nct = two_f // (2 * wt)
    ts = _tile(seq, 256, 8)
    ns = seq // ts

    def body(d_ref, w_ref, o_ref, carry):
        @pl.when(pl.program_id(2) == 0)
        def _():
            carry[...] = jnp.zeros_like(carry)

        d = d_ref[...].astype(F32)
        o_ref[...] = (w_ref[2:3, :] * d + w_ref[1:2, :] * _shift_up(d, carry, 1)
                      + w_ref[0:1, :] * _shift_up(d, carry, 2)).astype(o_ref.dtype)
        carry[...] = d[:8, :]

    rev = lambda j, b, s: (b * ns + ns - 1 - s, j)
    return pl.pallas_call(
        body, name="conv_bwd_input", grid=(nct, n_seq, ns),
        in_specs=[pl.BlockSpec((ts, 2 * wt), rev), pl.BlockSpec((3, 2 * wt), lambda j, b, s: (0, j))],
        out_specs=pl.BlockSpec((ts, 2 * wt), rev),
        out_shape=jax.ShapeDtypeStruct((n_tok, two_f), BF16),
        scratch_shapes=[pltpu.VMEM((8, 2 * wt), F32)],
        compiler_params=_params(("parallel", "arbitrary", "arbitrary")),
    )(d_il, cw_il)


HBM = pl.BlockSpec(memory_space=pltpu.HBM)
_CHIP_FLIPS = ((1, 0), (0, 1), (1, 1))


def _place():
    x, y, c = lax.axis_index("x"), lax.axis_index("y"), lax.axis_index("c")
    return x, y, c, 2 * x + y


def _flip(v, f):
    return 1 - v if f else v


def _half_rows(c, half):
    return pl.ds(pl.multiple_of(c * half, 16), half)


def _remote(src, dst, ssem, rsem, dev):
    return pltpu.make_async_remote_copy(src_ref=src, dst_ref=dst, send_sem=ssem, recv_sem=rsem,
                                        device_id=dev, device_id_type=MESH)


def _comm_call(name, body, ins, out_shapes, n_sems):
    return pl.pallas_call(
        body, name=name, in_specs=[HBM] * len(ins), out_specs=[HBM] * len(out_shapes),
        out_shape=[pltpu.HBM(s.shape, s.dtype) for s in out_shapes],
        scratch_shapes=[pltpu.SemaphoreType.DMA((n_sems,)), pltpu.SemaphoreType.DMA((n_sems,))],
    )(*ins)


def _all_gather_weights(shards, smalls):
    side = _gather_side(shards, smalls)
    nt = len(shards) + len(smalls)

    def body(*refs):
        for part in (side.start, side.mid, side.end):
            part(refs[:nt], refs[nt:2 * nt], *refs[2 * nt:])

    res = _comm_call("all_gather_weights", body, side.ins, side.outs, side.n_sems)
    return res[:len(shards)], res[len(shards):]


def _pair_split(name, grads):
    n = len(grads)

    def body(*refs):
        src, got = refs[:n], refs[n:2 * n]
        ssem, rsem = refs[2 * n:]
        x, y, c, _ = _place()
        cps = []
        for w in range(n):
            half = grads[w].shape[1] // 2
            cp = _remote(src[w].at[:, _half_rows(1 - c, half)], got[w], ssem.at[w], rsem.at[w], (x, y, 1 - c))
            cp.start()
            cps.append(cp)
        for cp in cps:
            cp.wait()

    outs = [jax.ShapeDtypeStruct((g.shape[0], g.shape[1] // 2, g.shape[2]), g.dtype) for g in grads]
    return _comm_call(name, body, grads, outs, n)


def _chip_scatter(parts):
    side = _scatter_side(parts)

    def body(*refs):
        n = len(parts)
        side.start(refs[:n], refs[n:2 * n], *refs[2 * n:])
        side.end(refs[:n], refs[n:2 * n], *refs[2 * n:])

    return _comm_call("rs_chip_scatter", body, parts, side.outs, side.n_sems)


class _Side:
    def __init__(self, ins, outs, n_sems, start, mid, end, mid_step=None):
        self.ins, self.outs, self.n_sems = list(ins), list(outs), n_sems
        self.start, self.mid, self.end, self.mid_step = start, mid, end, mid_step
        self.aliases = {}


def _scatter_side(parts):
    n = len(parts)

    def copies(src, dst, ssem, rsem):
        x, y, c, _ = _place()
        out = []
        for w in range(n):
            for k, (fx, fy) in enumerate(_CHIP_FLIPS):
                px, py = _flip(x, fx), _flip(y, fy)
                out.append(_remote(src[w].at[2 * px + py], dst[w].at[k], ssem.at[w * 3 + k], rsem.at[w * 3 + k], (px, py, c)))
        return out

    def start(src, dst, ssem, rsem):
        for cp in copies(src, dst, ssem, rsem):
            cp.start()

    def end(src, dst, ssem, rsem):
        for cp in copies(src, dst, ssem, rsem):
            cp.wait()

    outs = [jax.ShapeDtypeStruct((3,) + p.shape[1:], p.dtype) for p in parts]
    return _Side(parts, outs, 3 * n, start, None, end)


def _gather_side(shards, smalls, mid_step=None, into=None):
    n, ns = len(shards), len(smalls)
    into = into or [(None, a.shape[0], 0) for a in shards]

    def dst_rows(w, c):
        half = shards[w].shape[0] // 2
        return pl.ds(pl.multiple_of(into[w][2] + c * half, 16), half)

    def ici(src, dst, ssem, rsem, w, k):
        x, y, c, me = _place()
        fx, fy = _CHIP_FLIPS[k]
        rows = _half_rows(c, shards[w].shape[0] // 2)
        return _remote(src[w].at[rows], dst[w].at[me, dst_rows(w, c)], ssem.at[w * 6 + k], rsem.at[w * 6 + k],
                       (_flip(x, fx), _flip(y, fy), c))

    def small(src, dst, ssem, rsem, s, k):
        x, y, c, me = _place()
        fx, fy = _CHIP_FLIPS[k]
        sem = 6 * n + 3 * s + k
        return _remote(src[n + s], dst[n + s].at[me], ssem.at[sem], rsem.at[sem], (_flip(x, fx), _flip(y, fy), c))

    def landed(dst, ssem, rsem, w, k, sender_c, sem_off):
        x, y, c, _ = _place()
        fx, fy = _CHIP_FLIPS[k]
        got = dst[w].at[2 * _flip(x, fx) + _flip(y, fy), dst_rows(w, sender_c)]
        return _remote(got, got, ssem.at[w * 6 + sem_off + k], rsem.at[w * 6 + sem_off + k], (x, y, 1 - c))

    def start(src, dst, ssem, rsem):
        for s in range(ns):
            for k in range(3):
                small(src, dst, ssem, rsem, s, k).start()
        for w in range(n):
            for k in range(3):
                ici(src, dst, ssem, rsem, w, k).start()

    def mid(src, dst, ssem, rsem):
        c = lax.axis_index("c")
        for w in range(n):
            for k in range(3):
                landed(dst, ssem, rsem, w, k, c, 0).wait_recv()
                landed(dst, ssem, rsem, w, k, c, 3).start()

    def end(src, dst, ssem, rsem):
        c = lax.axis_index("c")
        for w in range(n):
            for k in range(3):
                landed(dst, ssem, rsem, w, k, 1 - c, 3).wait_recv()
        for s in range(ns):
            for k in range(3):
                small(src, dst, ssem, rsem, s, k).wait()
        for w in range(n):
            for k in range(3):
                ici(src, dst, ssem, rsem, w, k).wait_send()
                landed(dst, ssem, rsem, w, k, c, 3).wait_send()

    outs = [jax.ShapeDtypeStruct((N_CHIPS, rows, a.shape[1]), a.dtype) for a, (_, rows, _) in zip(shards, into)]
    outs += [jax.ShapeDtypeStruct((N_CHIPS,) + a.shape, a.dtype) for a in smalls]
    filled = [(w, arr) for w, (arr, _, _) in enumerate(into) if arr is not None]
    side = _Side(list(shards) + list(smalls) + [arr for _, arr in filled], outs, 6 * n + 3 * ns, start, mid, end, mid_step)
    side.aliases = {n + ns + i: w for i, (w, _) in enumerate(filled)}
    return side


def _host(body, n_in, n_out, side, grid):
    if side is None:
        return body
    ns_in, ns_out = len(side.ins), len(side.outs)
    n_steps = math.prod(grid)
    mid_step = side.mid_step
    if side.mid is not None and not isinstance(mid_step, int):
        mid_step = min(n_steps - 1, int(mid_step * n_steps))

    def wrapped(*refs):
        ins, s_ins = refs[:n_in], refs[n_in:n_in + ns_in]
        outs = refs[n_in + ns_in:n_in + ns_in + n_out]
        s_outs = refs[n_in + ns_in + n_out:n_in + ns_in + n_out + ns_out]
        rest = refs[n_in + ns_in + n_out + ns_out:]
        sems = rest[-2:]
        step = 0
        for axis, extent in enumerate(grid):
            step = step * extent + pl.program_id(axis)

        @pl.when(step == 0)
        def _():
            side.start(s_ins, s_outs, *sems)

        if side.mid is not None:
            @pl.when(step == mid_step)
            def _():
                side.mid(s_ins, s_outs, *sems)

        body(*ins, *outs, *rest[:-2])

        @pl.when(step == n_steps - 1)
        def _():
            side.end(s_ins, s_outs, *sems)

    return wrapped


def _hosted_call(body, name, grid, in_specs, out_specs, out_shape, args, side, semantics=("parallel", "arbitrary"),
                 scratch=()):
    n_in, n_out = len(in_specs), len(out_specs)
    kern = _host(body, n_in, n_out, side, grid)
    if side is None:
        return pl.pallas_call(kern, name=name, grid=grid, in_specs=in_specs, out_specs=out_specs, out_shape=out_shape,
                              scratch_shapes=list(scratch), compiler_params=_params(semantics))(*args), []
    res = pl.pallas_call(
        kern, name=name, grid=grid, in_specs=in_specs + [HBM] * len(side.ins), out_specs=out_specs + [HBM] * len(side.outs),
        out_shape=list(out_shape) + [pltpu.HBM(s.shape, s.dtype) for s in side.outs],
        scratch_shapes=list(scratch) + [pltpu.SemaphoreType.DMA((side.n_sems,)), pltpu.SemaphoreType.DMA((side.n_sems,))],
        input_output_aliases={n_in + i: n_out + o for i, o in side.aliases.items()},
        compiler_params=_params(("arbitrary",) * len(grid)),
    )(*args, *side.ins)
    return res[:n_out], res[n_out:]


def _pair_swap(halves):
    n = len(halves)

    def body(*refs):
        src, dst = refs[:n], refs[n:2 * n]
        ssem, rsem = refs[2 * n:]
        x, y, c, _ = _place()
        cps = []
        for w in range(n):
            cp = _remote(src[w], dst[w], ssem.at[w], rsem.at[w], (x, y, 1 - c))
            cp.start()
            cps.append(cp)
        for cp in cps:
            cp.wait()

    outs = [jax.ShapeDtypeStruct(h.shape, h.dtype) for h in halves]
    return _comm_call("rs_pair_swap", body, halves, outs, n)


def _gather_small(vec):
    def body(src, dst, ssem, rsem):
        x, y, c, _ = _place()
        me = 4 * x + 2 * y + c
        cps = []
        for r in range(1, 8):
            dev = (_flip(x, r & 4), _flip(y, r & 2), _flip(c, r & 1))
            cp = _remote(src, dst.at[me], ssem.at[r - 1], rsem.at[r - 1], dev)
            cp.start()
            cps.append(cp)
        for cp in cps:
            cp.wait()

    out = jax.ShapeDtypeStruct((8,) + vec.shape, vec.dtype)
    return _comm_call("gather_small", body, [vec], [out], 7)[0]


def _pair_add(name, g, theirs, core):
    n, half, b = theirs.shape
    tr = _tile(half, 256, 16)
    nt = half // tr

    def body(c_ref, g_ref, t_ref, o_ref):
        o_ref[...] = (g_ref[...].astype(F32) + t_ref[...].astype(F32)).astype(o_ref.dtype)

    same = pl.BlockSpec((None, tr, b), lambda j, i, c: (j, i, 0))
    grid_spec = pltpu.PrefetchScalarGridSpec(
        num_scalar_prefetch=1, grid=(n, nt),
        in_specs=[pl.BlockSpec((None, tr, b), lambda j, i, c: (j, c[0] * nt + i, 0)), same], out_specs=same)
    return pl.pallas_call(body, name=name, grid_spec=grid_spec, out_shape=jax.ShapeDtypeStruct(theirs.shape, BF16),
                          compiler_params=_params(("parallel", "parallel")))(core, g, theirs)


def _sum_slots(name, stacked, first=None):
    n, r, c = stacked.shape
    tr = _tile(r, 256, 8)

    def body(*refs):
        s_ref, o_ref = refs[-2], refs[-1]
        acc = refs[0][...].astype(F32) if first is not None else s_ref[0].astype(F32)
        for s in range(0 if first is not None else 1, n):
            acc = acc + s_ref[s].astype(F32)
        o_ref[...] = acc

    row_spec = pl.BlockSpec((tr, c), lambda i: (i, 0))
    in_specs = ([row_spec] if first is not None else []) + [pl.BlockSpec((n, tr, c), lambda i: (0, i, 0))]
    args = ([first] if first is not None else []) + [stacked]
    return pl.pallas_call(
        body, name=name, grid=(r // tr,), in_specs=in_specs, out_specs=row_spec,
        out_shape=jax.ShapeDtypeStruct((r, c), F32), compiler_params=_params(("parallel",)),
    )(*args)


def _adam_math(w, g, m, v):
    bc1, bc2 = 1.0 - ADAM_B1 ** ADAM_STEP, 1.0 - ADAM_B2 ** ADAM_STEP
    nm = ADAM_B1 * m + (1.0 - ADAM_B1) * g
    nv = ADAM_B2 * v + (1.0 - ADAM_B2) * (g * g)
    return -ADAM_LR * ((nm / bc1) / (jnp.sqrt(nv / bc2) + ADAM_EPS) + ADAM_WD * w), nm, nv


def _adamw_halves(name, w, g_mine, g_theirs, m, v, core):
    r, c = w.shape
    h = r // 2
    tr = _tile(h, 128, 8)
    nth = h // tr

    def body(c_ref, w_ref, gm_ref, gt_ref, m_ref, v_ref, g_ref, d_ref, nm_ref, nv_ref):
        g = jnp.where(pl.program_id(0) // nth == c_ref[0], gm_ref[...], gt_ref[...])
        g_ref[...] = g
        d_ref[...], nm_ref[...], nv_ref[...] = _adam_math(w_ref[...], g, m_ref[...], v_ref[...])

    full = pl.BlockSpec((tr, c), lambda i, cr: (i, 0))
    half = pl.BlockSpec((tr, c), lambda i, cr: (i % nth, 0))
    grid_spec = pltpu.PrefetchScalarGridSpec(num_scalar_prefetch=1, grid=(r // tr,),
                                             in_specs=[full, half, half, full, full], out_specs=[full] * 4)
    return pl.pallas_call(body, name=name, grid_spec=grid_spec, out_shape=[jax.ShapeDtypeStruct((r, c), F32)] * 4,
                          compiler_params=_params(("parallel",)))(core, w, g_mine, g_theirs, m, v)


def _adamw(name, w, g, m, v):
    r, c = w.shape
    by_cols = r % 8 != 0 and c % LANES == 0
    tr, tc = (r, _tile(c, 256, LANES)) if by_cols else (_tile(r, 256, 8), c)

    def body(w_ref, g_ref, m_ref, v_ref, d_ref, nm_ref, nv_ref):
        d_ref[...], nm_ref[...], nv_ref[...] = _adam_math(w_ref[...], g_ref[...], m_ref[...], v_ref[...])

    spec = pl.BlockSpec((tr, tc), (lambda i: (0, i)) if by_cols else (lambda i: (i, 0)))
    return pl.pallas_call(
        body, name=name, grid=(c // tc if by_cols else r // tr,), in_specs=[spec] * 4, out_specs=[spec] * 3,
        out_shape=[jax.ShapeDtypeStruct((r, c), F32)] * 3, compiler_params=_params(("parallel",)),
    )(w, g, m, v)


def _pad_cols(a, cols):
    return jnp.pad(a, ((0, 0), (0, cols - a.shape[1])))


def _rot_cols(w):
    h = w.shape[-1] // 2
    return jnp.concatenate([-w[..., h:], w[..., :h]], axis=-1)


def _unrot_cols(d):
    h = d.shape[-1] // 2
    return jnp.concatenate([d[..., h:], -d[..., :h]], axis=-1)


def _logical(g):
    return jnp.transpose(g, (1, 0, 2)).reshape(g.shape[1], N_CHIPS * g.shape[2])


def _chunks(a, n):
    return jnp.transpose(a.reshape(a.shape[0], N_CHIPS, n), (1, 0, 2))


def kernel(x, positions, pre_mix_norm, w_in, q_a_norm, w_uq, kv_a_norm, w_ukv, b_forget, b_gate, w_branch_mla, w_branch_fox, w_out, post_mix_norm, pre_ffn_norm, w_up, conv_w, conv_b, w_down, post_ffn_norm, loss_target, m_pre_mix_norm, m_w_in, m_q_a_norm, m_w_uq, m_kv_a_norm, m_w_ukv, m_b_forget, m_b_gate, m_w_branch_mla, m_w_branch_fox, m_w_out, m_post_mix_norm, m_pre_ffn_norm, m_w_up, m_conv_w, m_conv_b, m_w_down, m_post_ffn_norm, v_pre_mix_norm, v_w_in, v_q_a_norm, v_w_uq, v_kv_a_norm, v_w_ukv, v_b_forget, v_b_gate, v_w_branch_mla, v_w_branch_fox, v_w_out, v_post_mix_norm, v_pre_ffn_norm, v_w_up, v_conv_w, v_conv_b, v_w_down, v_post_ffn_norm):
    n_seq, seq, d = x.shape
    n_tok = n_seq * seq
    d_in = N_CHIPS * w_in.shape[1]
    two_f = N_CHIPS * w_up.shape[1]
    ff_dim = two_f // 2
    assert d_in == QL + KVL + ROPE + 3 * HEADS * FDIM + HEADS + 2 * d
    n_in_shard = w_in.shape[1]
    in_pad = -(-n_in_shard // LANES) * LANES
    hd = HEADS * LANES
    xc, yc, cc = lax.axis_index("x"), lax.axis_index("y"), lax.axis_index("c")
    chip = 2 * xc + yc
    t_attn = _tile(seq, 512, 128)

    shards = [_pad_cols(w_in, in_pad).astype(BF16), w_uq.astype(BF16), w_ukv.astype(BF16), w_branch_mla.astype(BF16),
              w_branch_fox.astype(BF16), w_out.astype(BF16), w_up.astype(BF16), w_down.astype(BF16)]
    cw8 = jnp.pad(conv_w, ((0, 5), (0, 0)))
    put_own = lambda g, s: lax.dynamic_update_slice(g, s[None], (chip, 0, 0))
    gathered, (g_cw,) = _all_gather_weights(shards[:3], [cw8])
    g_in, g_uq, g_ukv = [put_own(g, s) for g, s in zip(gathered, shards[:3])]
    g_cw = put_own(g_cw, cw8)
    n_attn_steps = n_seq * (seq // t_attn)
    side_proj = _gather_side([shards[3], shards[4], shards[5]], [], mid_step=0.9)
    side_ffn = _gather_side([shards[7]], [], mid_step=0.7)
    up_rows = shards[6].shape[0]
    up_cuts = [i * up_rows // 8 for i in (0, 1, 3, 5, 6, 7, 8)]

    def side_up(piece, filled, mid_step):
        lo, hi = up_cuts[piece], up_cuts[piece + 1]
        return _gather_side([shards[6][lo:hi]], [], mid_step=mid_step, into=[(filled, up_rows, lo)])

    o_q, o_kv, o_kpe = 0, QL, QL + KVL
    o_f = o_kpe + ROPE
    o_fl = o_f + 3 * hd
    o_g = o_fl + HEADS

    def chip_cols(lo, hi):
        out = []
        while lo < hi:
            j = lo // n_in_shard
            end = min(hi, (j + 1) * n_in_shard)
            out.append((j, lo - j * n_in_shard, end - j * n_in_shard))
            lo = end
        return out

    take = lambda lo, hi: [g_in[j, :, a:b] for j, a, b in chip_cols(lo, hi)]
    w_kpe = jnp.concatenate(take(o_kpe, o_f), axis=1)
    zeros = lambda n: jnp.zeros((d, n), BF16)
    win_p = jnp.concatenate(
        take(o_g, d_in) + take(o_q, o_kpe) + [w_kpe, zeros(LANES - ROPE), _rot_cols(w_kpe)] + take(o_fl, o_g)
        + [zeros(LANES - ROPE - HEADS)] + take(o_f, o_fl), axis=1)
    n_p = win_p.shape[1]
    cb_gm, cb_gf = 0, 1
    c_lat = 2 * d
    c_kx, c_kr = c_lat + QL + KVL, c_lat + QL + KVL + LANES
    n_pa = c_kr + LANES
    assert n_p == n_pa + 3 * hd

    uq3 = _logical(g_uq).reshape(QL, HEADS, NOPE + ROPE)
    pe = uq3[:, :, NOPE:]
    pad_pe = lambda a: jnp.pad(a, ((0, 0), (0, 0), (0, LANES - ROPE))).reshape(QL, hd)
    wuq_p = jnp.concatenate([uq3[:, :, :NOPE].reshape(QL, hd), pad_pe(pe), pad_pe(_rot_cols(pe))], axis=1)
    ukv3 = _logical(g_ukv).reshape(KVL, HEADS, NOPE + VDIM)
    wukv_p = jnp.concatenate([ukv3[:, :, :NOPE].reshape(KVL, hd), ukv3[:, :, NOPE:].reshape(KVL, hd)], axis=1)

    n_bm, n_up = w_branch_mla.shape[1], w_up.shape[1]
    l_bm, l_up = _Chunked(n_bm), _Chunked(n_up)
    wt = n_up // 2
    n_ut = two_f // wt
    il = lambda cblk: jnp.where(cblk < n_ut // 2, 2 * cblk, 2 * (cblk - n_ut // 2) + 1)
    l_il = _Plain(il)
    to_il = lambda a: a.reshape(a.shape[0], 2, n_ut // 2, wt).transpose(0, 2, 1, 3).reshape(a.shape[0], two_f)
    from_il = lambda a: a.reshape(a.shape[0], n_ut // 2, 2, wt).transpose(0, 2, 1, 3).reshape(a.shape[0], two_f)

    inv_freq = 1.0 / (ROPE_THETA ** (jnp.arange(0, ROPE, 2, dtype=F32) / ROPE))
    ang = positions.astype(F32).reshape(n_tok, 1) * inv_freq
    cos, sin = jnp.cos(ang), jnp.sin(ang)
    cs = _pad_cols(jnp.concatenate([cos, cos], axis=1), LANES)
    sn = _pad_cols(jnp.concatenate([sin, sin], axis=1), LANES)

    row = lambda v: v.reshape(1, -1)
    x2 = x.reshape(n_tok, d)
    tgt = loss_target.reshape(n_tok, d)

    (h,) = _rows("rms_pre_mix", lambda r, v: ([r[0] * _rstd(r[0]) * v[0]], []),
                 [(x2, d, 0)], [row(pre_mix_norm)], [(d, BF16)], [], n_tok)
    proj, got = _mm("proj_in", "nn", h, win_p, n_tok, n_pa, d, side=side_proj)
    g_bm, g_bf, g_out = [put_own(g, s) for g, s in zip(got, side_proj.ins)]
    w_out_full = g_out.reshape(d, d)
    tn_f = _tile(3 * hd, 1024, 128)
    assert n_pa % tn_f == 0
    proj_f, (g_up,) = _mm("proj_in_fox", "nn", h, win_p, n_tok, 3 * hd, d, tn=tn_f, lb=_Plain(lambda cblk: cblk + n_pa // tn_f),
                          out_dtype=BF16, side=side_up(0, None, 0.8))

    bf_vec = jnp.pad(row(b_forget), ((0, 0), (ROPE, LANES - ROPE - HEADS)))

    def lat_fwd(r, v):
        ql, kvl = r[0], r[1]
        return [ql * _rstd(ql) * v[0], kvl * _rstd(kvl) * v[1], r[2] * r[4] + r[3] * r[5]], []

    qn, kvn, rk = _rows("latent_norms", lat_fwd,
                        [(proj, QL, c_lat // QL), (proj, KVL, (c_lat + QL) // KVL), (proj, LANES, c_kx // LANES),
                         (proj, LANES, c_kr // LANES), (cs, LANES, 0), (sn, LANES, 0)],
                        [row(q_a_norm), row(kv_a_norm)], [(QL, BF16), (KVL, BF16), (LANES, BF16)], [], n_tok)
    q_p = _mm("q_up", "nn", qn, wuq_p, n_tok, 3 * hd, QL)
    kv_p = _mm("kv_up", "nn", kvn, wukv_p, n_tok, 2 * hd, KVL, out_dtype=BF16)

    def rope_q(r, v):
        c8, s8 = jnp.tile(r[3], (1, HEADS)), jnp.tile(r[4], (1, HEADS))
        return [r[0], r[1] * c8 + r[2] * s8], []

    q_nope, rq = _rows("rope_q", rope_q, [(q_p, hd, 0), (q_p, hd, 1), (q_p, hd, 2), (cs, LANES, 0), (sn, LANES, 0)], [],
                       [(hd, BF16), (hd, BF16)], [], n_tok)

    mla_q = [(q_nope, 0, False), (rq, 0, False)]
    mla_k = [(kv_p, 0, False), (rk, 0, True)]
    mla_v = (kv_p, 1)
    mla_scale = (NOPE + ROPE) ** -0.5
    (o_mla, lse_mla), (g_up,) = _attn_fwd("mla_fwd", mla_q, mla_k, mla_v, None, CHUNK, mla_scale, n_seq, seq, t_attn,
                                          side=side_up(1, g_up, max(n_attn_steps - 2, 0)))

    c_run = _seq_cumsum("forget_cumsum", proj, c_kr // LANES, n_seq, seq, False,
                        pre=lambda z, b: _log_sigmoid(z + b), vec=bf_vec)
    nb_attn = seq // t_attn
    c_rowf = jnp.transpose(c_run[:, ROPE:ROPE + HEADS].reshape(n_seq, nb_attn, t_attn, HEADS), (0, 1, 3, 2))
    fox_q, fox_k, fox_v = [(proj_f, 0, False)], [(proj_f, 1, False)], (proj_f, 2)
    fox_scale = FDIM ** -0.5
    fox_bias = (c_run, c_rowf)
    (o_fox, lse_fox), (g_up,) = _attn_fwd("fox_fwd", fox_q, fox_k, fox_v, fox_bias, 1, fox_scale, n_seq, seq, t_attn,
                                          side=side_up(2, g_up, max(n_attn_steps - 2, 0)))

    pm, (g_up,) = _mm("branch_mla", "nn", o_mla, g_bm, n_tok, d, hd, lb=l_bm, tn=n_bm, out_dtype=BF16,
                      side=side_up(3, g_up, 0.7))
    pf, (g_up,) = _mm("branch_fox", "nn", o_fox, g_bf, n_tok, d, hd, lb=l_bm, tn=n_bm, out_dtype=BF16,
                      side=side_up(4, g_up, 0.7))
    bg = row(b_gate)

    def merge(r, v):
        return [_sigmoid(r[0] + v[0]) * r[2] + _sigmoid(r[1] + v[1]) * r[3]], []

    (merged,) = _rows("gate_merge", merge, [(proj, d, cb_gm), (proj, d, cb_gf), (pm, d, 0), (pf, d, 0)],
                      [bg[:, :d], bg[:, d:]], [(d, BF16)], [], n_tok)
    y1, (g_up,) = _mm("mix_out", "nn", merged, w_out_full, n_tok, d, d, side=side_up(5, g_up, 0.7))
    g_up = put_own(g_up, shards[6])

    def resid_norm(r, v):
        x1v = r[0] + r[1] * _rstd(r[1]) * v[0]
        return [x1v, x1v * _rstd(x1v) * v[1]], []

    x1, h2 = _rows("post_mix_pre_ffn", resid_norm, [(x2, d, 0), (y1, d, 0)], [row(post_mix_norm), row(pre_ffn_norm)],
                   [(d, F32), (d, BF16)], [], n_tok)

    u_il, got = _mm("ffn_up", "nn", h2, g_up, n_tok, two_f, d, lb=l_up, lo=l_il, tn=wt, out_dtype=BF16, side=side_ffn)
    w_down_full = put_own(got[0], side_ffn.ins[0]).reshape(ff_dim, d)
    cw_il = to_il(_logical(g_cw)[:3])
    cb_il = to_il(row(conv_b))
    act = _conv_glu_fwd(u_il, cw_il, cb_il, n_seq, seq, wt)
    ff = _mm("ffn_down", "nn", act, w_down_full, n_tok, d, ff_dim)

    def final(r, v):
        x1v, ffv, tg = r
        diff = x1v + ffv * _rstd(ffv) * v[0] - tg
        dx2v = diff / d
        dffv, dg4 = _rms_bwd(ffv, v[0], dx2v)
        sq = jnp.sum(jnp.sum(diff * diff, axis=1, keepdims=True), axis=0, keepdims=True)
        return [dx2v, dffv], [dg4, jnp.broadcast_to(sq, (1, LANES))]

    dx2, dff, dg_post_ffn, sq_sum = _rows("loss_post_ffn_bwd", final, [(x1, d, 0), (ff, d, 0), (tgt, d, 0)],
                                          [row(post_ffn_norm)], [(d, F32), (d, BF16)], [(1, d), (1, LANES)], n_tok)
    rs_parts, rs_landed = {}, {}
    core = jnp.reshape(cc, (1,)).astype(jnp.int32)

    def pair_reduce(tag, names, grads):
        theirs = _pair_split("rs_pair_split_" + tag, grads)
        for nm, g, b in zip(names, grads, theirs):
            rs_parts[nm] = _pair_add("rs_pair_add_" + nm, g, b, core)

    dact = _mm("ffn_down_dx", "nt", dff, w_down_full, n_tok, ff_dim, d, tn=wt, out_dtype=BF16)
    gw_down = _mm("ffn_down_dw", "tn", act, dff, ff_dim, d, n_tok, tm=wt, out_dtype=BF16)
    pair_reduce("down", ["w_down"], [gw_down.reshape(N_CHIPS, ff_dim // N_CHIPS, d)])
    d_il, conv_acc = _conv_glu_bwd_pre(u_il, dact, cw_il, cb_il, n_seq, seq, wt)
    du_il = _conv_bwd_input(d_il, cw_il, n_seq, seq, wt)
    gw_up, got = _mm("ffn_up_dw", "tn", h2, du_il, d, two_f, n_tok, lb=l_il, lo=l_up, tn=wt, out_dtype=BF16,
                     side=_scatter_side([rs_parts["w_down"]]))
    rs_landed["w_down"] = got[0]
    pair_reduce("up", ["w_up"], [gw_up])
    dh2, got = _mm("ffn_up_dx", "nt", du_il, g_up, n_tok, d, two_f, la=l_il, lb=l_up, tk=wt,
                   side=_scatter_side([rs_parts["w_up"]]))
    rs_landed["w_up"] = got[0]

    def mid_bwd(r, v):
        x1v, y1v, dx2v, dh2v = r
        d3, dg3 = _rms_bwd(x1v, v[1], dh2v)
        dx1v = dx2v + d3
        dy1v, dg2 = _rms_bwd(y1v, v[0], dx1v)
        return [dx1v, dy1v], [dg3, dg2]

    dx1, dy1, dg_pre_ffn, dg_post_mix = _rows(
        "pre_ffn_post_mix_bwd", mid_bwd, [(x1, d, 0), (y1, d, 0), (dx2, d, 0), (dh2, d, 0)],
        [row(post_mix_norm), row(pre_ffn_norm)], [(d, F32), (d, BF16)], [(1, d), (1, d)], n_tok)
    dmerged = _mm("mix_out_dx", "nt", dy1, w_out_full, n_tok, d, d, out_dtype=BF16)
    gw_out = _mm("mix_out_dw", "tn", merged, dy1, d, d, n_tok, out_dtype=BF16)

    def gate_bwd(r, v):
        zm, zf, pmv, pfv, dm = r
        gm, gf = _sigmoid(zm + v[0]), _sigmoid(zf + v[1])
        dzm, dzf = dm * pmv * gm * (1.0 - gm), dm * pfv * gf * (1.0 - gf)
        return [dm * gm, dm * gf, jnp.concatenate([dzm, dzf], axis=1)], [_colsum(dzm), _colsum(dzf)]

    dpm, dpf, dz, dbg_m, dbg_f = _rows(
        "gate_merge_bwd", gate_bwd, [(proj, d, cb_gm), (proj, d, cb_gf), (pm, d, 0), (pf, d, 0), (dmerged, d, 0)],
        [bg[:, :d], bg[:, d:]], [(d, BF16), (d, BF16), (2 * d, BF16)], [(1, d), (1, d)], n_tok)
    tk_b = min(n_bm, 512)
    do_mla = _mm("branch_mla_dx", "nt", dpm, g_bm, n_tok, hd, d, lb=l_bm, tk=tk_b)
    do_fox = _mm("branch_fox_dx", "nt", dpf, g_bf, n_tok, hd, d, lb=l_bm, tk=tk_b)
    gw_bm = _mm("branch_mla_dw", "tn", o_mla, dpm, hd, d, n_tok, lo=l_bm, tn=n_bm, out_dtype=BF16)
    gw_bf = _mm("branch_fox_dw", "tn", o_fox, dpf, hd, d, n_tok, lo=l_bm, tn=n_bm, out_dtype=BF16)

    pair_reduce("mix", ["w_out", "w_branch_mla", "w_branch_fox"], [gw_out.reshape(N_CHIPS, d // N_CHIPS, d), gw_bm, gw_bf])
    delta_mla, dob_mla = _attn_delta("mla_delta", o_mla, do_mla, n_tok)
    (dq_nope, drq, dk_nope, drk_g, dv_mla), got = _attn_bwd(
        "mla_bwd", mla_q, mla_k, mla_v, dob_mla, lse_mla, delta_mla, None, CHUNK, mla_scale, n_seq, seq, t_attn, BF16,
        side=_scatter_side([rs_parts[nm] for nm in ("w_out", "w_branch_mla", "w_branch_fox")]))
    rs_landed.update(zip(("w_out", "w_branch_mla", "w_branch_fox"), got))
    delta_fox, dob_fox = _attn_delta("fox_delta", o_fox, do_fox, n_tok)
    (dfq, dfk, dfv, dc_q, dc_k), _ = _attn_bwd("fox_bwd", fox_q, fox_k, fox_v, dob_fox, lse_fox, delta_fox, fox_bias, 1,
                                               fox_scale, n_seq, seq, t_attn, BF16)
    dc_k8 = jnp.transpose(dc_k, (0, 2, 4, 1, 3)).reshape(n_tok, HEADS)
    dc128 = dc_q[0] + dc_q[1] + jnp.pad(dc_k8, ((0, 0), (ROPE, LANES - ROPE - HEADS)))
    dlogf = _seq_cumsum("forget_cumsum_bwd", dc128, 0, n_seq, seq, True)

    def mla_pack(r, v):
        dqn_v, drq_v, dkn_v, dv_v, drk_a, drk_b, c1, s1 = r
        c8, s8 = jnp.tile(c1, (1, HEADS)), jnp.tile(s1, (1, HEADS))
        drk_v = drk_a + drk_b
        return [jnp.concatenate([dqn_v, drq_v * c8, drq_v * s8], axis=1), jnp.concatenate([dkn_v, dv_v], axis=1),
                drk_v * c1, drk_v * s1], []

    dq_p, dkv_p, dkx, dkr = _rows(
        "mla_rope_bwd", mla_pack,
        [(dq_nope, hd, 0), (drq, hd, 0), (dk_nope, hd, 0), (dv_mla, hd, 0), (drk_g[0], LANES, 0), (drk_g[1], LANES, 0),
         (cs, LANES, 0), (sn, LANES, 0)],
        [], [(3 * hd, BF16), (2 * hd, BF16), (LANES, F32), (LANES, F32)], [], n_tok)
    dqn = _mm("q_up_dx", "nt", dq_p, wuq_p, n_tok, QL, 3 * hd)
    gw_uq_p = _mm("q_up_dw", "tn", qn, dq_p, QL, 3 * hd, n_tok, out_dtype=BF16)
    dkvn = _mm("kv_up_dx", "nt", dkv_p, wukv_p, n_tok, KVL, 2 * hd)
    gw_ukv_p = _mm("kv_up_dw", "tn", kvn, dkv_p, KVL, 2 * hd, n_tok, out_dtype=BF16)

    def lat_bwd(r, v):
        ql, kvl, dqn_v, dkvn_v, dkx_v, dkr_v, zblk, dlf = r
        dql, dgq = _rms_bwd(ql, v[0], dqn_v)
        dkvl, dgkv = _rms_bwd(kvl, v[1], dkvn_v)
        dfl = dlf * _sigmoid(-(zblk + v[2]))
        return [jnp.concatenate([dql, dkvl, dkx_v, dkr_v + dfl], axis=1)], [dgq, dgkv, _colsum(dfl)]

    dlat, dg_q, dg_kv, dbf = _rows(
        "latent_bwd", lat_bwd,
        [(proj, QL, c_lat // QL), (proj, KVL, (c_lat + QL) // KVL), (dqn, QL, 0), (dkvn, KVL, 0), (dkx, LANES, 0),
         (dkr, LANES, 0), (proj, LANES, c_kr // LANES), (dlogf, LANES, 0)],
        [row(q_a_norm), row(kv_a_norm), bf_vec], [(QL + KVL + 2 * LANES, BF16)], [(1, QL), (1, KVL), (1, LANES)], n_tok)
    dproj = [dz, dlat, dfq, dfk, dfv]
    gw_in_p = _mm_parts("proj_in_dw", "tn", h, dproj, d, n_p, n_tok, part=512, out_dtype=BF16)

    f32 = lambda a: a.astype(F32)
    kr_blk = gw_in_p[:, c_kr:c_kr + LANES]
    d_kpe = (f32(gw_in_p[:, c_kx:c_kx + ROPE]) + _unrot_cols(f32(kr_blk[:, :ROPE]))).astype(BF16)
    in_pieces = [(o_q, gw_in_p, c_lat, QL + KVL), (o_kpe, d_kpe, 0, ROPE), (o_f, gw_in_p, n_pa, 3 * hd),
                 (o_fl, kr_blk, ROPE, HEADS), (o_g, gw_in_p, 0, 2 * d)]
    gc_in = []
    for j in range(N_CHIPS):
        lo, hi, cols = j * n_in_shard, (j + 1) * n_in_shard, []
        for first, arr, at, width in in_pieces:
            a, b = max(lo, first), min(hi, first + width)
            if a < b:
                cols.append(arr[:, at + a - first:at + b - first])
        cols.append(jnp.zeros((d, in_pad - n_in_shard), BF16))
        gc_in.append(jnp.concatenate(cols, axis=1))
    gc_in = jnp.stack(gc_in)
    uq_parts = [gw_uq_p[:, i * hd:(i + 1) * hd].reshape(QL, HEADS, LANES) for i in range(3)]
    d_pe = (f32(uq_parts[1][:, :, :ROPE]) + _unrot_cols(f32(uq_parts[2][:, :, :ROPE]))).astype(BF16)
    gc_uq = _chunks(jnp.concatenate([uq_parts[0], d_pe], axis=2).reshape(QL, HEADS * (NOPE + ROPE)), w_uq.shape[1])
    gc_ukv = _chunks(jnp.concatenate([gw_ukv_p[:, :hd].reshape(KVL, HEADS, NOPE), gw_ukv_p[:, hd:].reshape(KVL, HEADS, VDIM)],
                                     axis=2).reshape(KVL, HEADS * (NOPE + VDIM)), w_ukv.shape[1])
    grads = [gc_in, gc_uq, gc_ukv]

    late = ["w_in", "w_uq", "w_ukv"]
    pair_reduce("late", late, grads)
    dh, got = _mm_parts("proj_in_dx", "nt", dproj, win_p, n_tok, d, n_p, side=_scatter_side([rs_parts[nm] for nm in late]))
    rs_landed.update(zip(late, got))

    def first_bwd(r, v):
        dxa, dg1 = _rms_bwd(r[0], v[0], r[1])
        return [r[2] + dxa], [dg1]

    grad_x, dg_pre_mix = _rows("pre_mix_bwd", first_bwd, [(x2, d, 0), (dh, d, 0), (dx1, d, 0)], [row(pre_mix_norm)],
                               [(d, F32)], [(1, d)], n_tok)
    big = list(rs_parts)
    halves = [_sum_slots("rs_chip_sum_" + nm, rs_landed[nm],
                         first=lax.dynamic_index_in_dim(rs_parts[nm], chip, 0, keepdims=False)) for nm in big]
    other = _pair_swap(halves)
    g_halves = dict(zip(big, zip(halves, other)))

    conv_acc_l = from_il(conv_acc)
    pieces = [dg_pre_mix, dg_q, dg_kv, dbf, dbg_m, dbg_f, dg_post_mix, dg_pre_ffn, conv_acc_l[3:4], dg_post_ffn,
              conv_acc_l[0:1], conv_acc_l[1:2], conv_acc_l[2:3], sq_sum]
    sizes = [p.shape[1] for p in pieces]
    flat = jnp.concatenate(pieces, axis=1)
    n_rows = -(-flat.shape[1] // (8 * LANES)) * 8
    flat = _pad_cols(flat, n_rows * LANES).reshape(n_rows, LANES)
    slots = lax.dynamic_update_slice(_gather_small(flat), flat[None], (2 * chip + cc, 0, 0))
    total = _sum_slots("small_sum", slots).reshape(1, n_rows * LANES)
    offs = [sum(sizes[:i]) for i in range(len(sizes))]
    tot = [total[0, o:o + s] for o, s in zip(offs, sizes)]
    loss = 0.5 * tot[13][0] / d
    g_small = {"pre_mix_norm": tot[0], "q_a_norm": tot[1], "kv_a_norm": tot[2], "b_forget": tot[3][ROPE:ROPE + HEADS],
               "b_gate": jnp.concatenate([tot[4], tot[5]]), "post_mix_norm": tot[6], "pre_ffn_norm": tot[7],
               "conv_b": tot[8], "post_ffn_norm": tot[9]}
    gcw_full = jnp.stack([tot[10], tot[11], tot[12]])
    g_conv_w = lax.dynamic_slice(gcw_full, (0, chip * n_up), (3, n_up))

    given = dict(pre_mix_norm=(pre_mix_norm, m_pre_mix_norm, v_pre_mix_norm), w_in=(w_in, m_w_in, v_w_in),
                 q_a_norm=(q_a_norm, m_q_a_norm, v_q_a_norm), w_uq=(w_uq, m_w_uq, v_w_uq),
                 kv_a_norm=(kv_a_norm, m_kv_a_norm, v_kv_a_norm), w_ukv=(w_ukv, m_w_ukv, v_w_ukv),
                 b_forget=(b_forget, m_b_forget, v_b_forget), b_gate=(b_gate, m_b_gate, v_b_gate),
                 w_branch_mla=(w_branch_mla, m_w_branch_mla, v_w_branch_mla),
                 w_branch_fox=(w_branch_fox, m_w_branch_fox, v_w_branch_fox), w_out=(w_out, m_w_out, v_w_out),
                 post_mix_norm=(post_mix_norm, m_post_mix_norm, v_post_mix_norm),
                 pre_ffn_norm=(pre_ffn_norm, m_pre_ffn_norm, v_pre_ffn_norm), w_up=(w_up, m_w_up, v_w_up),
                 conv_w=(conv_w, m_conv_w, v_conv_w), conv_b=(conv_b, m_conv_b, v_conv_b),
                 w_down=(w_down, m_w_down, v_w_down), post_ffn_norm=(post_ffn_norm, m_post_ffn_norm, v_post_ffn_norm))
    order = list(given)
    grad, delta, new_m, new_v = {}, {}, {}, {}
    for nm in big:
        mine, theirs = g_halves[nm]
        if nm == "w_in":
            full = jnp.concatenate([jnp.where(cc == 0, mine, theirs), jnp.where(cc == 0, theirs, mine)], axis=0)
            grad[nm] = full[:, :n_in_shard]
            tr_out = _adamw("adamw_" + nm, *[jnp.transpose(a) for a in (given[nm][0], grad[nm], given[nm][1], given[nm][2])])
            delta[nm], new_m[nm], new_v[nm] = [jnp.transpose(a) for a in tr_out]
            continue
        grad[nm], delta[nm], new_m[nm], new_v[nm] = _adamw_halves("adamw_" + nm, given[nm][0], mine, theirs, given[nm][1],
                                                                  given[nm][2], core)
    grad["conv_w"] = g_conv_w
    delta["conv_w"], new_m["conv_w"], new_v["conv_w"] = _adamw("adamw_conv_w", conv_w, g_conv_w, m_conv_w, v_conv_w)
    small = list(g_small)
    padded = [-(-g_small[nm].shape[0] // LANES) * LANES for nm in small]
    s_rows = -(-sum(padded) // (8 * LANES)) * 8

    def pack(vals):
        cat = jnp.concatenate([jnp.pad(a, (0, p - a.shape[0])) for a, p in zip(vals, padded)])
        return jnp.pad(cat, (0, s_rows * LANES - cat.shape[0])).reshape(s_rows, LANES)

    packed = _adamw("adamw_small", pack([given[nm][0] for nm in small]), pack([g_small[nm] for nm in small]),
                    pack([given[nm][1] for nm in small]), pack([given[nm][2] for nm in small]))
    s_offs = [sum(padded[:i]) for i in range(len(small))]
    for nm, o in zip(small, s_offs):
        n_el = g_small[nm].shape[0]
        grad[nm] = g_small[nm]
        delta[nm], new_m[nm], new_v[nm] = [p.reshape(-1)[o:o + n_el] for p in packed]
    return (loss, grad_x.reshape(n_seq, seq, d), *[grad[nm] for nm in order], *[delta[nm] for nm in order],
            *[new_m[nm] for nm in order], *[new_v[nm] for nm in order])
```

```python
import functools
import math

import jax
import jax.numpy as jnp
from jax import lax
from jax.experimental import pallas as pl
from jax.experimental.pallas import tpu as pltpu

F32, BF16 = jnp.float32, jnp.bfloat16
MESH = pl.DeviceIdType.MESH

HEADS = 8
NOPE, ROPE, VDIM = 128, 64, 128
QL, KVL = 512, 256
FDIM = 128
CHUNK = 64
ROPE_THETA = 10000.0
EPS = 1e-6
NEG_INF = -1e30
ADAM_LR, ADAM_B1, ADAM_B2, ADAM_EPS, ADAM_WD, ADAM_STEP = 0.001, 0.9, 0.999, 1e-08, 0.01, 10

VMEM_LIMIT_BYTES = 52 * 1024 * 1024
LANES = 128
N_CHIPS = 4


def _params(sem):
    return pltpu.CompilerParams(dimension_semantics=sem, vmem_limit_bytes=VMEM_LIMIT_BYTES)


def _tile(n, target, mult):
    if n <= target:
        return n
    t = (target // mult) * mult
    while t >= mult:
        if n % t == 0:
            return t
        t -= mult
    raise ValueError(f"no tile for {n} (target {target}, multiple of {mult})")


class _Plain:
    def __init__(self, perm=None):
        self.perm = perm

    def spec(self, tr, tc, rc):
        perm = self.perm

        def imap(i, j, k):
            r, c = rc(i, j, k)
            return (r, perm(c) if perm is not None else c)

        return pl.BlockSpec((tr, tc), imap)

    def shape(self, rows, cols):
        return (rows, cols)


class _Chunked:
    def __init__(self, n):
        self.n = n

    def spec(self, tr, tc, rc):
        assert self.n % tc == 0, (self.n, tc)
        per = self.n // tc

        def imap(i, j, k):
            r, c = rc(i, j, k)
            return (c // per, r, c % per)

        return pl.BlockSpec((None, tr, tc), imap)

    def shape(self, rows, cols):
        assert cols == N_CHIPS * self.n
        return (N_CHIPS, rows, self.n)


_DIMS = {"nn": (((1,), (0,)), ((), ())), "nt": (((1,), (1,)), ((), ())), "tn": (((0,), (0,)), ((), ()))}


def _mm_single(name, mode, a, b, m, n, k, tm, tn, la, lb, lo, out_dtype, side):
    if mode == "nn":
        a_spec = la.spec(tm, k, lambda i, j, kk: (i, 0))
        b_spec = lb.spec(k, tn, lambda i, j, kk: (0, j))
    elif mode == "nt":
        a_spec = la.spec(tm, k, lambda i, j, kk: (i, 0))
        b_spec = lb.spec(tn, k, lambda i, j, kk: (j, 0))
    else:
        a_spec = la.spec(k, tm, lambda i, j, kk: (0, i))
        b_spec = lb.spec(k, tn, lambda i, j, kk: (0, j))
    o_spec = lo.spec(tm, tn, lambda i, j, kk: (i, j))
    dims = _DIMS[mode]

    def body(a_ref, b_ref, o_ref):
        o_ref[...] = lax.dot_general(a_ref[...].astype(BF16), b_ref[...].astype(BF16), dims,
                                     preferred_element_type=F32).astype(o_ref.dtype)

    (out,), got = _hosted_call(body, name, (m // tm, n // tn, 1), [a_spec, b_spec], [o_spec],
                               [jax.ShapeDtypeStruct(lo.shape(m, n), out_dtype)], (a, b), side,
                               semantics=("parallel", "parallel", "arbitrary"))
    return out if side is None else (out, got)


def _mm(name, mode, a, b, m, n, k, *, tm=1024, tn=1024, tk=2048, la=None, lb=None, lo=None, out_dtype=F32, side=None):
    la, lb, lo = la or _Plain(), lb or _Plain(), lo or _Plain()
    tm, tn, tk = _tile(m, tm, 128), _tile(n, tn, 128), _tile(k, tk, 128)
    nk = k // tk
    if nk == 1:
        return _mm_single(name, mode, a, b, m, n, k, tm, tn, la, lb, lo, out_dtype, side)
    if mode == "nn":
        a_spec = la.spec(tm, tk, lambda i, j, kk: (i, kk))
        b_spec = lb.spec(tk, tn, lambda i, j, kk: (kk, j))
    elif mode == "nt":
        a_spec = la.spec(tm, tk, lambda i, j, kk: (i, kk))
        b_spec = lb.spec(tn, tk, lambda i, j, kk: (j, kk))
    else:
        a_spec = la.spec(tk, tm, lambda i, j, kk: (kk, i))
        b_spec = lb.spec(tk, tn, lambda i, j, kk: (kk, j))
    o_spec = lo.spec(tm, tn, lambda i, j, kk: (i, j))
    dims = _DIMS[mode]

    def body(a_ref, b_ref, o_ref, acc_ref):
        kk = pl.program_id(2)

        @pl.when(kk == 0)
        def _():
            acc_ref[...] = jnp.zeros_like(acc_ref)

        acc_ref[...] += lax.dot_general(a_ref[...].astype(BF16), b_ref[...].astype(BF16), dims,
                                        preferred_element_type=F32)

        @pl.when(kk == nk - 1)
        def _():
            o_ref[...] = acc_ref[...].astype(o_ref.dtype)

    (out,), got = _hosted_call(body, name, (m // tm, n // tn, nk), [a_spec, b_spec], [o_spec],
                               [jax.ShapeDtypeStruct(lo.shape(m, n), out_dtype)], (a, b), side,
                               semantics=("parallel", "parallel", "arbitrary"), scratch=[pltpu.VMEM((tm, tn), F32)])
    return out if side is None else (out, got)


def _mm_parts(name, mode, a, b, m, n, k, *, part=1024, tm=1024, tn=1024, tk=2048, out_dtype=F32, side=None):
    parts = b if mode == "tn" else a
    widths = [p.shape[1] for p in parts]
    assert all(w % part == 0 for w in widths) and sum(widths) == (n if mode == "tn" else k)
    offs = [sum(widths[:i]) // part for i in range(len(widths))]
    nblk = [w // part for w in widths]
    if mode == "tn":
        tn, tk = part, _tile(k, tk, 128)
    else:
        tk, tn = part, _tile(n, tn, 128)
    tm = _tile(m, tm, 128)
    nk = k // tk
    grid = (m // tm, n // tn, nk)
    np_ = len(parts)

    def inside(idx, p):
        return jnp.logical_and(idx >= offs[p], idx < offs[p] + nblk[p])

    def part_spec(p):
        if mode == "tn":
            def imap(i, j, kk):
                on = inside(j, p)
                return (jnp.where(on, kk, 0), jnp.clip(j - offs[p], 0, nblk[p] - 1))
            return pl.BlockSpec((tk, tn), imap)

        def imap(i, j, kk):
            return (i, jnp.clip(kk - offs[p], 0, nblk[p] - 1))
        return pl.BlockSpec((tm, tk), imap)

    if mode == "tn":
        in_specs = [pl.BlockSpec((tk, tm), lambda i, j, kk: (kk, i))] + [part_spec(p) for p in range(np_)]
        args = [a] + list(parts)
    else:
        in_specs = [part_spec(p) for p in range(np_)] + [pl.BlockSpec((tn, tk), lambda i, j, kk: (j, kk))]
        args = list(parts) + [b]
    dims = _DIMS[mode]

    def body(*refs):
        o_ref, acc_ref = refs[-2], refs[-1]
        j, kk = pl.program_id(1), pl.program_id(2)

        @pl.when(kk == 0)
        def _():
            acc_ref[...] = jnp.zeros_like(acc_ref)

        for p in range(np_):
            @pl.when(inside(j if mode == "tn" else kk, p))
            def _(p=p):
                lhs, rhs = (refs[0], refs[1 + p]) if mode == "tn" else (refs[p], refs[np_])
                acc_ref[...] += lax.dot_general(lhs[...].astype(BF16), rhs[...].astype(BF16), dims, preferred_element_type=F32)

        @pl.when(kk == nk - 1)
        def _():
            o_ref[...] = acc_ref[...].astype(o_ref.dtype)

    (out,), got = _hosted_call(body, name, grid, in_specs, [pl.BlockSpec((tm, tn), lambda i, j, kk: (i, j))],
                               [jax.ShapeDtypeStruct((m, n), out_dtype)], args, side,
                               semantics=("parallel", "parallel", "arbitrary"), scratch=[pltpu.VMEM((tm, tn), F32)])
    return out if side is None else (out, got)


def _rows(name, fn, rows_in, vecs_in, rows_out, accs_out, n_rows, tr=256):
    tr = _tile(n_rows, tr, 16)
    nr, nv, no = len(rows_in), len(vecs_in), len(rows_out)

    def body(*refs):
        ins, vecs = refs[:nr], refs[nr:nr + nv]
        outs, accs = refs[nr + nv:nr + nv + no], refs[nr + nv + no:]
        ro, ac = fn([r[...] for r in ins], [v[...] for v in vecs])
        for o_ref, val in zip(outs, ro):
            o_ref[...] = val.astype(o_ref.dtype)
        if accs:
            @pl.when(pl.program_id(0) == 0)
            def _():
                for a_ref in accs:
                    a_ref[...] = jnp.zeros_like(a_ref)

            for a_ref, val in zip(accs, ac):
                a_ref[...] += val

    in_specs = [pl.BlockSpec((tr, cols), functools.partial(lambda i, cb: (i, cb), cb=cb)) for _, cols, cb in rows_in]
    in_specs += [pl.BlockSpec(v.shape, lambda i: (0, 0)) for v in vecs_in]
    out_specs = [pl.BlockSpec((tr, cols), lambda i: (i, 0)) for cols, _ in rows_out]
    out_specs += [pl.BlockSpec((r, cols), lambda i: (0, 0)) for r, cols in accs_out]
    out_shape = [jax.ShapeDtypeStruct((n_rows, cols), dt) for cols, dt in rows_out]
    out_shape += [jax.ShapeDtypeStruct((r, cols), F32) for r, cols in accs_out]
    res = pl.pallas_call(
        body, name=name, grid=(n_rows // tr,), in_specs=in_specs, out_specs=out_specs, out_shape=out_shape,
        compiler_params=_params(("arbitrary",)),
    )(*[a for a, _, _ in rows_in], *vecs_in)
    return res


def _colsum(v):
    return jnp.sum(v, axis=0, keepdims=True)


def _rstd(x):
    return lax.rsqrt(jnp.mean(x * x, axis=-1, keepdims=True) + EPS)


def _rms_bwd(x, g, dy):
    r = _rstd(x)
    xh = x * r
    dxh = dy * g
    dx = r * (dxh - xh * jnp.mean(dxh * xh, axis=-1, keepdims=True))
    return dx, _colsum(dy * xh)


def _sigmoid(z):
    return 1.0 / (1.0 + jnp.exp(-z))


_GELU_K = math.sqrt(2.0 / math.pi)


def _gelu_parts(g):
    t = jnp.tanh(_GELU_K * (g + 0.044715 * g * g * g))
    gel = 0.5 * g * (1.0 + t)
    dgel = 0.5 * (1.0 + t) + 0.5 * g * (1.0 - t * t) * (_GELU_K * (1.0 + 3.0 * 0.044715 * g * g))
    return gel, dgel


def _diag_visible(t, unit):
    rows = lax.broadcasted_iota(jnp.int32, (t, t), 0)
    cols = lax.broadcasted_iota(jnp.int32, (t, t), 1)
    if unit > 1:
        sh = int(math.log2(unit))
        assert 1 << sh == unit and t % unit == 0
        rows, cols = jnp.right_shift(rows, sh), jnp.right_shift(cols, sh)
    return cols <= rows


def _lane_pick(tile, lane):
    idx = lax.broadcasted_iota(jnp.int32, tile.shape, 1)
    return jnp.sum(jnp.where(idx == lane, tile, 0.0), axis=1, keepdims=True)


def _lane_put(tile, lane, col):
    idx = lax.broadcasted_iota(jnp.int32, tile.shape, 1)
    return jnp.where(idx == lane, col, tile)


def _head_cat(refs, shared, rows, h):
    hs = slice(h * LANES, (h + 1) * LANES)
    vals = [(r[rows, :] if sh else r[rows, hs]).astype(BF16) for r, sh in zip(refs, shared)]
    return vals[0] if len(vals) == 1 else jnp.concatenate(vals, axis=1)


def _blk_rows(i, t):
    return pl.ds(pl.multiple_of(i * t, t), t)


def _piece_specs(pieces, rows, row_idx):
    return [pl.BlockSpec((rows, LANES if sh else HEADS * LANES), functools.partial(lambda b, i, cb: (row_idx(b, i), cb), cb=cb))
            for _, cb, sh in pieces]


def _attn_fwd(name, qp, kp, vp, bias, unit, scale, n_seq, seq, t, side=None):
    nb = seq // t
    n_tok = n_seq * seq
    nq, nk_p = len(qp), len(kp)
    q_sh, k_sh = [p[2] for p in qp], [p[2] for p in kp]
    nbias = 2 if bias is not None else 0

    def body(*refs):
        q_refs, k_refs = refs[:nq], refs[nq:nq + nk_p]
        v_ref = refs[nq + nk_p]
        bias_refs = refs[nq + nk_p + 1:nq + nk_p + 1 + nbias]
        o_ref, lse_ref = refs[nq + nk_p + 1 + nbias:]
        qi = pl.program_id(1)
        lse_tile = jnp.zeros((t, LANES), F32)
        for h in range(HEADS):
            hs = slice(h * LANES, (h + 1) * LANES)
            q = _head_cat(q_refs, q_sh, slice(None), h)
            cq = _lane_pick(bias_refs[0][...], ROPE + h) if bias is not None else None

            def block(kb, carry, diag, h=h, hs=hs, q=q, cq=cq):
                m, l, acc = carry
                rows = _blk_rows(kb, t)
                s = lax.dot_general(q, _head_cat(k_refs, k_sh, rows, h), _DIMS["nt"], preferred_element_type=F32) * scale
                if bias is not None:
                    s = s + cq - bias_refs[1][kb, h:h + 1, :]
                if diag:
                    s = jnp.where(_diag_visible(t, unit), s, NEG_INF)
                m_new = jnp.maximum(m, jnp.max(s, axis=1, keepdims=True))
                alpha = jnp.exp(m - m_new)
                p = jnp.exp(s - m_new)
                l = alpha * l + jnp.sum(p, axis=1, keepdims=True)
                acc = alpha * acc + jnp.dot(p.astype(BF16), v_ref[rows, hs].astype(BF16), preferred_element_type=F32)
                return m_new, l, acc

            init = (jnp.full((t, 1), NEG_INF, F32), jnp.zeros((t, 1), F32), jnp.zeros((t, LANES), F32))
            carry = lax.fori_loop(0, qi, lambda kb, c: block(kb, c, False), init)
            m, l, acc = block(qi, carry, True)
            o_ref[:, hs] = acc / l
            lse_tile = _lane_put(lse_tile, h, m + jnp.log(l))
        lse_ref[...] = lse_tile

    tile_row = lambda b, i: b * nb + i
    seq_row = lambda b, i: b
    lane_tile = pl.BlockSpec((t, LANES), lambda b, i: (b * nb + i, 0))
    in_specs = _piece_specs(qp, t, tile_row) + _piece_specs(kp, seq, seq_row) + _piece_specs([vp + (False,)], seq, seq_row)
    args = [p[0] for p in qp] + [p[0] for p in kp] + [vp[0]]
    if bias is not None:
        in_specs += [lane_tile, pl.BlockSpec((None, nb, HEADS, t), lambda b, i: (b, 0, 0, 0))]
        args += list(bias)
    return _hosted_call(
        body, name, (n_seq, nb), in_specs,
        [pl.BlockSpec((t, HEADS * LANES), lambda b, i: (b * nb + i, 0)), lane_tile],
        [jax.ShapeDtypeStruct((n_tok, HEADS * LANES), F32), jax.ShapeDtypeStruct((n_tok, LANES), F32)], args, side)


def _attn_bwd_dq(name, qp, kp, vp, o, do, lse, bias, unit, scale, n_seq, seq, t, side=None, grad_dtype=F32):
    nb = seq // t
    n_tok = n_seq * seq
    nq, nk_p = len(qp), len(kp)
    q_sh, k_sh = [p[2] for p in qp], [p[2] for p in kp]
    nbias = 2 if bias is not None else 0
    n_in = nq + nk_p + 4 + nbias

    def body(*refs):
        q_refs, k_refs = refs[:nq], refs[nq:nq + nk_p]
        v_ref, o_ref, do_ref, lse_ref = refs[nq + nk_p:nq + nk_p + 4]
        bias_refs = refs[nq + nk_p + 4:n_in]
        dq_refs = refs[n_in:n_in + nq]
        delta_ref, dob_ref = refs[n_in + nq:n_in + nq + 2]
        qi = pl.program_id(1)
        delta_tile = jnp.zeros((t, LANES), F32)
        dc_tile = jnp.zeros((t, LANES), F32)
        lse_all = lse_ref[...]
        for h in range(HEADS):
            hs = slice(h * LANES, (h + 1) * LANES)
            q = _head_cat(q_refs, q_sh, slice(None), h)
            do_f = do_ref[:, hs]
            do_b = do_f.astype(BF16)
            dob_ref[:, hs] = do_b
            delta = jnp.sum(do_f * o_ref[:, hs], axis=1, keepdims=True)
            lse = _lane_pick(lse_all, h)
            cq = _lane_pick(bias_refs[0][...], ROPE + h) if bias is not None else None

            def block(kb, carry, diag, h=h, hs=hs, q=q, cq=cq, do_b=do_b, delta=delta, lse=lse):
                dq_acc, dc_acc = carry
                rows = _blk_rows(kb, t)
                k = _head_cat(k_refs, k_sh, rows, h)
                s = lax.dot_general(q, k, _DIMS["nt"], preferred_element_type=F32) * scale
                if bias is not None:
                    s = s + cq - bias_refs[1][kb, h:h + 1, :]
                if diag:
                    s = jnp.where(_diag_visible(t, unit), s, NEG_INF)
                p = jnp.exp(s - lse)
                dp = lax.dot_general(do_b, v_ref[rows, hs].astype(BF16), _DIMS["nt"], preferred_element_type=F32)
                ds = p * (dp - delta)
                return (dq_acc + jnp.dot(ds.astype(BF16), k, preferred_element_type=F32),
                        dc_acc + jnp.sum(ds, axis=1, keepdims=True))

            init = (jnp.zeros((t, nq * LANES), F32), jnp.zeros((t, 1), F32))
            carry = lax.fori_loop(0, qi, lambda kb, c: block(kb, c, False), init)
            dq_acc, dc_acc = block(qi, carry, True)
            for n_p in range(nq):
                dq_refs[n_p][:, hs] = (dq_acc[:, n_p * LANES:(n_p + 1) * LANES] * scale).astype(grad_dtype)
            delta_tile = _lane_put(delta_tile, h, delta)
            dc_tile = _lane_put(dc_tile, ROPE + h, dc_acc)
        delta_ref[...] = delta_tile
        if bias is not None:
            refs[n_in + nq + 2][...] = dc_tile

    tile_row = lambda b, i: b * nb + i
    seq_row = lambda b, i: b
    lane_tile = pl.BlockSpec((t, LANES), lambda b, i: (b * nb + i, 0))
    head_tile = pl.BlockSpec((t, HEADS * LANES), lambda b, i: (b * nb + i, 0))
    in_specs = _piece_specs(qp, t, tile_row) + _piece_specs(kp, seq, seq_row) + _piece_specs([vp + (False,)], seq, seq_row)
    in_specs += [head_tile, head_tile, lane_tile]
    args = [p[0] for p in qp] + [p[0] for p in kp] + [vp[0], o, do, lse]
    if bias is not None:
        in_specs += [lane_tile, pl.BlockSpec((None, nb, HEADS, t), lambda b, i: (b, 0, 0, 0))]
        args += list(bias)
    out_specs = [head_tile] * nq + [lane_tile, head_tile] + ([lane_tile] if bias is not None else [])
    out_shape = [jax.ShapeDtypeStruct((n_tok, HEADS * LANES), grad_dtype)] * nq
    out_shape += [jax.ShapeDtypeStruct((n_tok, LANES), F32), jax.ShapeDtypeStruct((n_tok, HEADS * LANES), BF16)]
    if bias is not None:
        out_shape.append(jax.ShapeDtypeStruct((n_tok, LANES), F32))
    return _hosted_call(body, name, (n_seq, nb), in_specs, out_specs, out_shape, args, side)


def _attn_bwd_dkv(name, qp, kp, vp, dob, lse, delta, bias, unit, scale, n_seq, seq, t, side=None, grad_dtype=F32):
    nb = seq // t
    n_tok = n_seq * seq
    nq, nk_p = len(qp), len(kp)
    q_sh, k_sh = [p[2] for p in qp], [p[2] for p in kp]
    nbias = 2 if bias is not None else 0
    n_in = nq + nk_p + 4 + nbias

    def body(*refs):
        q_refs, k_refs = refs[:nq], refs[nq:nq + nk_p]
        v_ref, dob_ref, lse_ref, delta_ref = refs[nq + nk_p:nq + nk_p + 4]
        bias_refs = refs[nq + nk_p + 4:n_in]
        dk_refs = refs[n_in:n_in + nk_p]
        dv_ref = refs[n_in + nk_p]
        ki = pl.program_id(1)
        shared_acc = [jnp.zeros((t, LANES), F32) for _ in range(nk_p)]
        for h in range(HEADS):
            hs = slice(h * LANES, (h + 1) * LANES)
            k = _head_cat(k_refs, k_sh, slice(None), h)
            v = v_ref[:, hs].astype(BF16)
            ck = bias_refs[1][h:h + 1, :] if bias is not None else None

            def block(qb, carry, diag, h=h, hs=hs, k=k, v=v, ck=ck):
                dk_acc, dv_acc, dc_acc = carry
                rows = _blk_rows(qb, t)
                q = _head_cat(q_refs, q_sh, rows, h)
                s = lax.dot_general(q, k, _DIMS["nt"], preferred_element_type=F32) * scale
                if bias is not None:
                    s = s + _lane_pick(bias_refs[0][rows, :], ROPE + h) - ck
                if diag:
                    s = jnp.where(_diag_visible(t, unit), s, NEG_INF)
                p = jnp.exp(s - _lane_pick(lse_ref[rows, :], h))
                do_b = dob_ref[rows, hs]
                dp = lax.dot_general(do_b, v, _DIMS["nt"], preferred_element_type=F32)
                ds = p * (dp - _lane_pick(delta_ref[rows, :], h))
                return (dk_acc + lax.dot_general(ds.astype(BF16), q, _DIMS["tn"], preferred_element_type=F32),
                        dv_acc + lax.dot_general(p.astype(BF16), do_b, _DIMS["tn"], preferred_element_type=F32),
                        dc_acc - jnp.sum(ds, axis=0, keepdims=True))

            init = (jnp.zeros((t, nk_p * LANES), F32), jnp.zeros((t, LANES), F32), jnp.zeros((1, t), F32))
            carry = block(ki, init, True)
            dk_acc, dv_acc, dc_acc = lax.fori_loop(ki + 1, nb, lambda qb, c: block(qb, c, False), carry)
            for n_p in range(nk_p):
                part = dk_acc[:, n_p * LANES:(n_p + 1) * LANES] * scale
                if k_sh[n_p]:
                    shared_acc[n_p] = shared_acc[n_p] + part
                else:
                    dk_refs[n_p][:, hs] = part.astype(grad_dtype)
            dv_ref[:, hs] = dv_acc.astype(grad_dtype)
            if bias is not None:
                refs[n_in + nk_p + 1][h:h + 1, :] = dc_acc
        for n_p in range(nk_p):
            if k_sh[n_p]:
                dk_refs[n_p][...] = shared_acc[n_p]

    tile_row = lambda b, i: b * nb + i
    seq_row = lambda b, i: b
    lane_seq = pl.BlockSpec((seq, LANES), lambda b, i: (b, 0))
    head_tile = pl.BlockSpec((t, HEADS * LANES), lambda b, i: (b * nb + i, 0))
    row_tile = pl.BlockSpec((None, None, HEADS, t), lambda b, i: (b, i, 0, 0))
    in_specs = _piece_specs(qp, seq, seq_row) + _piece_specs(kp, t, tile_row) + _piece_specs([vp + (False,)], t, tile_row)
    in_specs += [pl.BlockSpec((seq, HEADS * LANES), lambda b, i: (b, 0)), lane_seq, lane_seq]
    args = [p[0] for p in qp] + [p[0] for p in kp] + [vp[0], dob, lse, delta]
    if bias is not None:
        in_specs += [lane_seq, row_tile]
        args += list(bias)
    out_specs = [pl.BlockSpec((t, LANES if sh else HEADS * LANES), lambda b, i: (b * nb + i, 0)) for sh in k_sh] + [head_tile]
    out_shape = [jax.ShapeDtypeStruct((n_tok, LANES), F32) if sh else jax.ShapeDtypeStruct((n_tok, HEADS * LANES), grad_dtype)
                 for sh in k_sh]
    out_shape.append(jax.ShapeDtypeStruct((n_tok, HEADS * LANES), grad_dtype))
    if bias is not None:
        out_specs.append(row_tile)
        out_shape.append(jax.ShapeDtypeStruct((n_seq, nb, HEADS, t), F32))
    return _hosted_call(body, name, (n_seq, nb), in_specs, out_specs, out_shape, args, side)


HEAD_GROUPS = 2


def _attn_delta(name, o, do, n_tok):
    def fn(r, v):
        o_v, do_v = r
        tile = jnp.zeros((o_v.shape[0], LANES), F32)
        for h in range(HEADS):
            hs = slice(h * LANES, (h + 1) * LANES)
            tile = _lane_put(tile, h, jnp.sum(do_v[:, hs] * o_v[:, hs], axis=1, keepdims=True))
        return [tile, do_v], []

    return _rows(name, fn, [(o, HEADS * LANES, 0), (do, HEADS * LANES, 0)], [], [(LANES, F32), (HEADS * LANES, BF16)], [], n_tok)


def _attn_bwd(name, qp, kp, vp, dob, lse, delta, bias, unit, scale, n_seq, seq, t, grad_dtype, side=None):
    nb = seq // t
    n_tok = n_seq * seq
    ng = HEAD_GROUPS
    hg = HEADS // ng
    gw = hg * LANES
    nq, nk_p = len(qp), len(kp)
    q_sh, k_sh = [p[2] for p in qp], [p[2] for p in kp]
    assert not any(q_sh) and nq == nk_p
    nbias = 2 if bias is not None else 0
    n_in = nq + nk_p + 4 + nbias
    n_out = nq + nk_p + 1 + nbias

    def body(*refs):
        q_refs, k_refs = refs[:nq], refs[nq:nq + nk_p]
        v_ref, dob_ref, lse_ref, delta_ref = refs[nq + nk_p:nq + nk_p + 4]
        bias_refs = refs[nq + nk_p + 4:n_in]
        dq_refs, dk_refs = refs[n_in:n_in + nq], refs[n_in + nq:n_in + nq + nk_p]
        dv_ref = refs[n_in + nq + nk_p]
        dq_s, dcq_s = refs[n_in + n_out:]
        g, ki = pl.program_id(1), pl.program_id(2)

        @pl.when(ki == 0)
        def _():
            dq_s[...] = jnp.zeros_like(dq_s)
            dcq_s[...] = jnp.zeros_like(dcq_s)

        shared_acc = [jnp.zeros((t, LANES), F32) for _ in range(nk_p)]
        for hl in range(hg):
            h = g * hg + hl
            hs = slice(hl * LANES, (hl + 1) * LANES)
            k = _head_cat(k_refs, k_sh, slice(None), hl)
            v = v_ref[:, hs].astype(BF16)
            ck = bias_refs[1][pl.ds(h, 1), :] if bias is not None else None

            def block(qb, carry, diag, h=h, hl=hl, hs=hs, k=k, v=v, ck=ck):
                dk_acc, dv_acc, dc_acc = carry
                rows = _blk_rows(qb, t)
                q = _head_cat(q_refs, q_sh, rows, hl)
                s = lax.dot_general(q, k, _DIMS["nt"], preferred_element_type=F32) * scale
                if bias is not None:
                    s = s + _lane_pick(bias_refs[0][rows, :], ROPE + h) - ck
                if diag:
                    s = jnp.where(_diag_visible(t, unit), s, NEG_INF)
                p = jnp.exp(s - _lane_pick(lse_ref[rows, :], h))
                do_b = dob_ref[rows, hs]
                dp = lax.dot_general(do_b, v, _DIMS["nt"], preferred_element_type=F32)
                ds = p * (dp - _lane_pick(delta_ref[rows, :], h))
                ds_b = ds.astype(BF16)
                dq_blk = jnp.dot(ds_b, k, preferred_element_type=F32)
                for n_p in range(nq):
                    dq_s[rows, n_p * gw + hl * LANES:n_p * gw + (hl + 1) * LANES] += dq_blk[:, n_p * LANES:(n_p + 1) * LANES]
                if bias is not None:
                    lane = lax.broadcasted_iota(jnp.int32, (t, LANES), 1)
                    dcq_s[rows, :] += jnp.where(lane == ROPE + h, jnp.sum(ds, axis=1, keepdims=True), 0.0)
                return (dk_acc + lax.dot_general(ds_b, q, _DIMS["tn"], preferred_element_type=F32),
                        dv_acc + lax.dot_general(p.astype(BF16), do_b, _DIMS["tn"], preferred_element_type=F32),
                        dc_acc - jnp.sum(ds, axis=0, keepdims=True))

            init = (jnp.zeros((t, nk_p * LANES), F32), jnp.zeros((t, LANES), F32), jnp.zeros((1, t), F32))
            carry = block(ki, init, True)
            dk_acc, dv_acc, dc_acc = lax.fori_loop(ki + 1, nb, lambda qb, c: block(qb, c, False), carry)
            for n_p in range(nk_p):
                part = dk_acc[:, n_p * LANES:(n_p + 1) * LANES] * scale
                if k_sh[n_p]:
                    shared_acc[n_p] = shared_acc[n_p] + part
                else:
                    dk_refs[n_p][:, hs] = part.astype(grad_dtype)
            dv_ref[:, hs] = dv_acc.astype(grad_dtype)
            if bias is not None:
                refs[n_in + n_out - 1][hl:hl + 1, :] = dc_acc
        for n_p in range(nk_p):
            if k_sh[n_p]:
                dk_refs[n_p][...] = shared_acc[n_p]

        @pl.when(ki == nb - 1)
        def _():
            for n_p in range(nq):
                dq_refs[n_p][...] = (dq_s[:, n_p * gw:(n_p + 1) * gw] * scale).astype(grad_dtype)
            if bias is not None:
                refs[n_in + n_out - 2][...] = dcq_s[...]

    def spec(rows, row_idx, cb, shared):
        if shared:
            return pl.BlockSpec((rows, LANES), lambda b, g, i: (row_idx(b, i), cb))
        return pl.BlockSpec((rows, gw), lambda b, g, i: (row_idx(b, i), cb * ng + g))

    tile_row = lambda b, i: b * nb + i
    seq_row = lambda b, i: b
    lane_seq = pl.BlockSpec((seq, LANES), lambda b, g, i: (b, 0))
    in_specs = [spec(seq, seq_row, cb, sh) for _, cb, sh in qp] + [spec(t, tile_row, cb, sh) for _, cb, sh in kp]
    in_specs += [spec(t, tile_row, vp[1], False), spec(seq, seq_row, 0, False), lane_seq, lane_seq]
    args = [p[0] for p in qp] + [p[0] for p in kp] + [vp[0], dob, lse, delta]
    if bias is not None:
        in_specs += [lane_seq, pl.BlockSpec((None, None, HEADS, t), lambda b, g, i: (b, i, 0, 0))]
        args += list(bias)
    group_tile = pl.BlockSpec((None, t, LANES), lambda b, g, i: (g, b * nb + i, 0))
    out_specs = [spec(seq, seq_row, 0, False)] * nq
    out_specs += [group_tile if sh else spec(t, tile_row, 0, False) for sh in k_sh] + [spec(t, tile_row, 0, False)]
    head_shape = jax.ShapeDtypeStruct((n_tok, HEADS * LANES), grad_dtype)
    out_shape = [head_shape] * nq + [jax.ShapeDtypeStruct((ng, n_tok, LANES), F32) if sh else head_shape for sh in k_sh]
    out_shape.append(head_shape)
    if bias is not None:
        out_specs += [pl.BlockSpec((None, seq, LANES), lambda b, g, i: (g, b, 0)),
                      pl.BlockSpec((None, None, None, hg, t), lambda b, g, i: (b, g, i, 0, 0))]
        out_shape += [jax.ShapeDtypeStruct((ng, n_tok, LANES), F32), jax.ShapeDtypeStruct((n_seq, ng, nb, hg, t), F32)]
    return _hosted_call(body, name, (n_seq, ng, nb), in_specs, out_specs, out_shape, args, side,
                        semantics=("parallel", "arbitrary", "arbitrary"),
                        scratch=[pltpu.VMEM((seq, nq * gw), F32), pltpu.VMEM((seq, LANES), F32)])


def _old_attn_bwd_dq(name, qp, kp, vp, o, do, lse, bias, unit, scale, n_seq, seq, t):
    nb = seq // t
    n_tok = n_seq * seq
    nq, nk_p = len(qp), len(kp)
    nbias = 2 if bias is not None else 0
    n_in = nq + nk_p + 4 + nbias
    n_out = nq + (1 if bias is not None else 0)

    def body(*refs):
        q_refs, k_refs = refs[:nq], refs[nq:nq + nk_p]
        v_ref, o_ref, do_ref, lse_ref = refs[nq + nk_p:nq + nk_p + 4]
        bias_refs = refs[nq + nk_p + 4:n_in]
        outs = refs[n_in:n_in + n_out]
        dq_s, delta_s, dc_s = refs[n_in + n_out:]
        qi, ki = pl.program_id(2), pl.program_id(3)

        @pl.when(ki == 0)
        def _():
            dq_s[...] = jnp.zeros_like(dq_s)
            dc_s[...] = jnp.zeros_like(dc_s)
            delta_s[...] = jnp.sum(do_ref[...] * o_ref[...], axis=1, keepdims=True)

        @pl.when(ki <= qi)
        def _():
            s = _scores(q_refs, k_refs, bias_refs, qi, ki, t, unit, scale)
            p = jnp.exp(s - lse_ref[...])
            dp = lax.dot_general(do_ref[...].astype(BF16), v_ref[...].astype(BF16), _DIMS["nt"],
                                 preferred_element_type=F32)
            ds = p * (dp - delta_s[...])
            dq_s[...] += jnp.dot(ds.astype(BF16), _cat(k_refs), preferred_element_type=F32)
            dc_s[...] += jnp.sum(ds, axis=1, keepdims=True)

        @pl.when(ki == qi)
        def _():
            for n_p in range(nq):
                outs[n_p][...] = dq_s[:, n_p * LANES:(n_p + 1) * LANES] * scale
            if bias is not None:
                outs[nq][...] = dc_s[...]

    q_row = lambda b, i, j: b * nb + i
    k_row = lambda b, i, j: b * nb + jnp.minimum(j, i)
    head_q = pl.BlockSpec((t, LANES), lambda b, h, i, j: (b * nb + i, h))
    col_q = pl.BlockSpec((None, t, 1), lambda b, h, i, j: (h, b * nb + i, 0))
    in_specs = [_piece_spec(t, p, q_row) for p in qp] + [_piece_spec(t, p, k_row) for p in kp]
    in_specs += [_piece_spec(t, vp, k_row), head_q, head_q, col_q]
    args = [p[0] for p in qp] + [p[0] for p in kp] + [vp[0], o, do, lse]
    if bias is not None:
        in_specs += [col_q, pl.BlockSpec((None, 1, t), lambda b, h, i, j: (b * HEADS + h, 0, jnp.minimum(j, i)))]
        args += list(bias)
    out_specs = [head_q] * nq + ([col_q] if bias is not None else [])
    out_shape = [jax.ShapeDtypeStruct((n_tok, HEADS * LANES), F32)] * nq
    if bias is not None:
        out_shape.append(jax.ShapeDtypeStruct((HEADS, n_tok, 1), F32))
    return pl.pallas_call(
        body, name=name, grid=(n_seq, HEADS, nb, nb), in_specs=in_specs, out_specs=out_specs, out_shape=out_shape,
        scratch_shapes=[pltpu.VMEM((t, nq * LANES), F32), pltpu.VMEM((t, 1), F32), pltpu.VMEM((t, 1), F32)],
        compiler_params=_params(("parallel", "parallel", "arbitrary", "arbitrary")),
    )(*args)


def _old_attn_bwd_dkv(name, qp, kp, vp, o, do, lse, bias, unit, scale, n_seq, seq, t):
    nb = seq // t
    n_tok = n_seq * seq
    nq, nk_p = len(qp), len(kp)
    nbias = 2 if bias is not None else 0
    n_in = nq + nk_p + 4 + nbias
    n_out = nk_p + 1 + (1 if bias is not None else 0)

    def body(*refs):
        q_refs, k_refs = refs[:nq], refs[nq:nq + nk_p]
        v_ref, o_ref, do_ref, lse_ref = refs[nq + nk_p:nq + nk_p + 4]
        bias_refs = refs[nq + nk_p + 4:n_in]
        outs = refs[n_in:n_in + n_out]
        dk_s, dv_s, dc_s = refs[n_in + n_out:]
        ki, qi = pl.program_id(2), pl.program_id(3)

        @pl.when(qi == 0)
        def _():
            dk_s[...] = jnp.zeros_like(dk_s)
            dv_s[...] = jnp.zeros_like(dv_s)
            dc_s[...] = jnp.zeros_like(dc_s)

        @pl.when(qi >= ki)
        def _():
            s = _scores(q_refs, k_refs, bias_refs, qi, ki, t, unit, scale)
            p = jnp.exp(s - lse_ref[...])
            do_b = do_ref[...].astype(BF16)
            delta = jnp.sum(do_ref[...] * o_ref[...], axis=1, keepdims=True)
            dp = lax.dot_general(do_b, v_ref[...].astype(BF16), _DIMS["nt"], preferred_element_type=F32)
            ds = p * (dp - delta)
            dv_s[...] += lax.dot_general(p.astype(BF16), do_b, _DIMS["tn"], preferred_element_type=F32)
            dk_s[...] += lax.dot_general(ds.astype(BF16), _cat(q_refs), _DIMS["tn"], preferred_element_type=F32)
            dc_s[...] -= jnp.sum(ds, axis=0, keepdims=True)

        @pl.when(qi == nb - 1)
        def _():
            for n_p in range(nk_p):
                outs[n_p][...] = dk_s[:, n_p * LANES:(n_p + 1) * LANES] * scale
            outs[nk_p][...] = dv_s[...]
            if bias is not None:
                outs[nk_p + 1][...] = dc_s[...]

    q_row = lambda b, i, j: b * nb + jnp.maximum(j, i)
    k_row = lambda b, i, j: b * nb + i
    head_q = pl.BlockSpec((t, LANES), lambda b, h, i, j: (b * nb + jnp.maximum(j, i), h))
    col_q = pl.BlockSpec((None, t, 1), lambda b, h, i, j: (h, b * nb + jnp.maximum(j, i), 0))
    head_k = pl.BlockSpec((t, LANES), lambda b, h, i, j: (b * nb + i, h))
    row_k = pl.BlockSpec((None, 1, t), lambda b, h, i, j: (b * HEADS + h, 0, i))
    in_specs = [_piece_spec(t, p, q_row) for p in qp] + [_piece_spec(t, p, k_row) for p in kp]
    in_specs += [_piece_spec(t, vp, k_row), head_q, head_q, col_q]
    args = [p[0] for p in qp] + [p[0] for p in kp] + [vp[0], o, do, lse]
    if bias is not None:
        in_specs += [col_q, row_k]
        args += list(bias)
    out_specs = [head_k] * (nk_p + 1) + ([row_k] if bias is not None else [])
    out_shape = [jax.ShapeDtypeStruct((n_tok, HEADS * LANES), F32)] * (nk_p + 1)
    if bias is not None:
        out_shape.append(jax.ShapeDtypeStruct((n_seq * HEADS, 1, seq), F32))
    return pl.pallas_call(
        body, name=name, grid=(n_seq, HEADS, nb, nb), in_specs=in_specs, out_specs=out_specs, out_shape=out_shape,
        scratch_shapes=[pltpu.VMEM((t, nk_p * LANES), F32), pltpu.VMEM((t, LANES), F32), pltpu.VMEM((1, t), F32)],
        compiler_params=_params(("parallel", "parallel", "arbitrary", "arbitrary")),
    )(*args)


def _seq_cumsum(name, x, col_block, n_seq, seq, reverse, pre=None, vec=None):
    t = _tile(seq, 256, 128)
    nb = seq // t

    def body(*refs):
        x_ref = refs[0]
        vec_ref = refs[1] if vec is not None else None
        o_ref, carry = refs[-2], refs[-1]

        @pl.when(pl.program_id(1) == 0)
        def _():
            carry[...] = jnp.zeros_like(carry)

        v = x_ref[...]
        if pre is not None:
            v = pre(v, vec_ref[...])
        r = lax.broadcasted_iota(jnp.int32, (t, t), 0)
        c = lax.broadcasted_iota(jnp.int32, (t, t), 1)
        tri = jnp.where((c >= r) if reverse else (c <= r), 1.0, 0.0).astype(BF16)
        hi = v.astype(BF16)
        mid = (v - hi.astype(F32)).astype(BF16)
        lo = (v - hi.astype(F32) - mid.astype(F32)).astype(BF16)
        acc = jnp.dot(tri, hi, preferred_element_type=F32)
        acc += jnp.dot(tri, mid, preferred_element_type=F32)
        acc += jnp.dot(tri, lo, preferred_element_type=F32)
        o_ref[...] = acc + carry[...]
        carry[...] += _colsum(v)

    blk = (lambda b, i: (b * nb + nb - 1 - i)) if reverse else (lambda b, i: (b * nb + i))
    in_specs = [pl.BlockSpec((t, LANES), lambda b, i: (blk(b, i), col_block))]
    args = [x]
    if vec is not None:
        in_specs.append(pl.BlockSpec(vec.shape, lambda b, i: (0, 0)))
        args.append(vec)
    return pl.pallas_call(
        body, name=name, grid=(n_seq, nb), in_specs=in_specs,
        out_specs=pl.BlockSpec((t, LANES), lambda b, i: (blk(b, i), 0)),
        out_shape=jax.ShapeDtypeStruct((n_seq * seq, LANES), F32),
        scratch_shapes=[pltpu.VMEM((1, LANES), F32)],
        compiler_params=_params(("arbitrary", "arbitrary")),
    )(*args)


def _log_sigmoid(z):
    return -(jnp.maximum(-z, 0.0) + jnp.log(1.0 + jnp.exp(-jnp.abs(z))))


def _shift_down(u, prev_ref, n):
    out = pltpu.roll(u, n, 0)
    row = lax.broadcasted_iota(jnp.int32, u.shape, 0)
    for r in range(n):
        out = jnp.where(row == r, prev_ref[8 - n + r:8 - n + r + 1, :], out)
    return out


def _shift_up(u, next_ref, n):
    ts = u.shape[0]
    out = pltpu.roll(u, ts - n, 0)
    row = lax.broadcasted_iota(jnp.int32, u.shape, 0)
    for r in range(n):
        out = jnp.where(row == ts - n + r, next_ref[r:r + 1, :], out)
    return out


def _conv_taps(u, prev_ref, w_ref, b_ref):
    s1, s2 = _shift_down(u, prev_ref, 1), _shift_down(u, prev_ref, 2)
    return (w_ref[0:1, :] * s2 + w_ref[1:2, :] * s1 + w_ref[2:3, :] * u) + b_ref[...], s1, s2


def _conv_glu_fwd(u_il, cw_il, cb_il, n_seq, seq, wt):
    n_tok, two_f = u_il.shape
    nct = two_f // (2 * wt)
    ts = _tile(seq, 256, 8)
    ns = seq // ts

    def body(u_ref, w_ref, b_ref, a_ref, carry):
        @pl.when(pl.program_id(2) == 0)
        def _():
            carry[...] = jnp.zeros_like(carry)

        u = u_ref[...].astype(F32)
        uc, _, _ = _conv_taps(u, carry, w_ref, b_ref)
        gel, _ = _gelu_parts(uc[:, :wt])
        a_ref[...] = (gel * uc[:, wt:]).astype(a_ref.dtype)
        carry[...] = u[ts - 8:, :]

    return pl.pallas_call(
        body, name="conv_glu_fwd", grid=(nct, n_seq, ns),
        in_specs=[pl.BlockSpec((ts, 2 * wt), lambda j, b, s: (b * ns + s, j)),
                  pl.BlockSpec((3, 2 * wt), lambda j, b, s: (0, j)),
                  pl.BlockSpec((1, 2 * wt), lambda j, b, s: (0, j))],
        out_specs=pl.BlockSpec((ts, wt), lambda j, b, s: (b * ns + s, j)),
        out_shape=jax.ShapeDtypeStruct((n_tok, two_f // 2), BF16),
        scratch_shapes=[pltpu.VMEM((8, 2 * wt), F32)],
        compiler_params=_params(("parallel", "arbitrary", "arbitrary")),
    )(u_il, cw_il, cb_il)


def _conv_glu_bwd_pre(u_il, da, cw_il, cb_il, n_seq, seq, wt):
    n_tok, two_f = u_il.shape
    nct = two_f // (2 * wt)
    ts = _tile(seq, 256, 8)
    ns = seq // ts

    def body(u_ref, da_ref, w_ref, b_ref, d_ref, acc_ref, carry):
        first = jnp.logical_and(pl.program_id(1) == 0, pl.program_id(2) == 0)

        @pl.when(first)
        def _():
            acc_ref[...] = jnp.zeros_like(acc_ref)

        @pl.when(pl.program_id(2) == 0)
        def _():
            carry[...] = jnp.zeros_like(carry)

        u = u_ref[...].astype(F32)
        uc, s1, s2 = _conv_taps(u, carry, w_ref, b_ref)
        gel, dgel = _gelu_parts(uc[:, :wt])
        da_v = da_ref[...].astype(F32)
        d = jnp.concatenate([da_v * uc[:, wt:] * dgel, da_v * gel], axis=1)
        d_ref[...] = d.astype(d_ref.dtype)
        acc_ref[0:1, :] += _colsum(d * s2)
        acc_ref[1:2, :] += _colsum(d * s1)
        acc_ref[2:3, :] += _colsum(d * u)
        acc_ref[3:4, :] += _colsum(d)
        carry[...] = u[ts - 8:, :]

    return pl.pallas_call(
        body, name="conv_glu_bwd_pre", grid=(nct, n_seq, ns),
        in_specs=[pl.BlockSpec((ts, 2 * wt), lambda j, b, s: (b * ns + s, j)),
                  pl.BlockSpec((ts, wt), lambda j, b, s: (b * ns + s, j)),
                  pl.BlockSpec((3, 2 * wt), lambda j, b, s: (0, j)),
                  pl.BlockSpec((1, 2 * wt), lambda j, b, s: (0, j))],
        out_specs=[pl.BlockSpec((ts, 2 * wt), lambda j, b, s: (b * ns + s, j)),
                   pl.BlockSpec((8, 2 * wt), lambda j, b, s: (0, j))],
        out_shape=[jax.ShapeDtypeStruct((n_tok, two_f), BF16), jax.ShapeDtypeStruct((8, two_f), F32)],
        scratch_shapes=[pltpu.VMEM((8, 2 * wt), F32)],
        compiler_params=_params(("parallel", "arbitrary", "arbitrary")),
    )(u_il, da, cw_il, cb_il)


def _conv_bwd_input(d_il, cw_il, n_seq, seq, wt):
    n_tok, two_f = d_il.shape
    nct = two_f // (2 * wt)
    ts = _tile(seq, 256, 8)
    ns = seq // ts

    def body(d_ref, w_ref, o_ref, carry):
        @pl.when(pl.program_id(2) == 0)
        def _():
            carry[...] = jnp.zeros_like(carry)

        d = d_ref[...].astype(F32)
        o_ref[...] = (w_ref[2:3, :] * d + w_ref[1:2, :] * _shift_up(d, carry, 1)
                      + w_ref[0:1, :] * _shift_up(d, carry, 2)).astype(o_ref.dtype)
        carry[...] = d[:8, :]

    rev = lambda j, b, s: (b * ns + ns - 1 - s, j)
    return pl.pallas_call(
        body, name="conv_bwd_input", grid=(nct, n_seq, ns),
        in_specs=[pl.BlockSpec((ts, 2 * wt), rev), pl.BlockSpec((3, 2 * wt), lambda j, b, s: (0, j))],
        out_specs=pl.BlockSpec((ts, 2 * wt), rev),
        out_shape=jax.ShapeDtypeStruct((n_tok, two_f), BF16),
        scratch_shapes=[pltpu.VMEM((8, 2 * wt), F32)],
        compiler_params=_params(("parallel", "arbitrary", "arbitrary")),
    )(d_il, cw_il)


HBM = pl.BlockSpec(memory_space=pltpu.HBM)
_CHIP_FLIPS = ((1, 0), (0, 1), (1, 1))


def _place():
    x, y, c = lax.axis_index("x"), lax.axis_index("y"), lax.axis_index("c")
    return x, y, c, 2 * x + y


def _flip(v, f):
    return 1 - v if f else v


def _half_rows(c, half):
    return pl.ds(pl.multiple_of(c * half, 16), half)


def _remote(src, dst, ssem, rsem, dev):
    return pltpu.make_async_remote_copy(src_ref=src, dst_ref=dst, send_sem=ssem, recv_sem=rsem,
                                        device_id=dev, device_id_type=MESH)


def _comm_call(name, body, ins, out_shapes, n_sems):
    return pl.pallas_call(
        body, name=name, in_specs=[HBM] * len(ins), out_specs=[HBM] * len(out_shapes),
        out_shape=[pltpu.HBM(s.shape, s.dtype) for s in out_shapes],
        scratch_shapes=[pltpu.SemaphoreType.DMA((n_sems,)), pltpu.SemaphoreType.DMA((n_sems,))],
    )(*ins)


def _all_gather_weights(shards, smalls):
    side = _gather_side(shards, smalls)
    nt = len(shards) + len(smalls)

    def body(*refs):
        for part in (side.start, side.mid, side.end):
            part(refs[:nt], refs[nt:2 * nt], *refs[2 * nt:])

    res = _comm_call("all_gather_weights", body, side.ins, side.outs, side.n_sems)
    return res[:len(shards)], res[len(shards):]


def _pair_split(name, grads):
    n = len(grads)

    def body(*refs):
        src, got = refs[:n], refs[n:2 * n]
        ssem, rsem = refs[2 * n:]
        x, y, c, _ = _place()
        cps = []
        for w in range(n):
            half = grads[w].shape[1] // 2
            cp = _remote(src[w].at[:, _half_rows(1 - c, half)], got[w], ssem.at[w], rsem.at[w], (x, y, 1 - c))
            cp.start()
            cps.append(cp)
        for cp in cps:
            cp.wait()

    outs = [jax.ShapeDtypeStruct((g.shape[0], g.shape[1] // 2, g.shape[2]), g.dtype) for g in grads]
    return _comm_call(name, body, grads, outs, n)


def _chip_scatter(parts):
    side = _scatter_side(parts)

    def body(*refs):
        n = len(parts)
        side.start(refs[:n], refs[n:2 * n], *refs[2 * n:])
        side.end(refs[:n], refs[n:2 * n], *refs[2 * n:])

    return _comm_call("rs_chip_scatter", body, parts, side.outs, side.n_sems)


class _Side:
    def __init__(self, ins, outs, n_sems, start, mid, end, mid_step=None):
        self.ins, self.outs, self.n_sems = list(ins), list(outs), n_sems
        self.start, self.mid, self.end, self.mid_step = start, mid, end, mid_step
        self.aliases = {}


def _scatter_side(parts):
    n = len(parts)

    def copies(src, dst, ssem, rsem):
        x, y, c, _ = _place()
        out = []
        for w in range(n):
            for k, (fx, fy) in enumerate(_CHIP_FLIPS):
                px, py = _flip(x, fx), _flip(y, fy)
                out.append(_remote(src[w].at[2 * px + py], dst[w].at[k], ssem.at[w * 3 + k], rsem.at[w * 3 + k], (px, py, c)))
        return out

    def start(src, dst, ssem, rsem):
        for cp in copies(src, dst, ssem, rsem):
            cp.start()

    def end(src, dst, ssem, rsem):
        for cp in copies(src, dst, ssem, rsem):
            cp.wait()

    outs = [jax.ShapeDtypeStruct((3,) + p.shape[1:], p.dtype) for p in parts]
    return _Side(parts, outs, 3 * n, start, None, end)


def _gather_side(shards, smalls, mid_step=None, into=None):
    n, ns = len(shards), len(smalls)
    into = into or [(None, a.shape[0], 0) for a in shards]

    def dst_rows(w, c):
        half = shards[w].shape[0] // 2
        return pl.ds(pl.multiple_of(into[w][2] + c * half, 16), half)

    def ici(src, dst, ssem, rsem, w, k):
        x, y, c, me = _place()
        fx, fy = _CHIP_FLIPS[k]
        rows = _half_rows(c, shards[w].shape[0] // 2)
        return _remote(src[w].at[rows], dst[w].at[me, dst_rows(w, c)], ssem.at[w * 6 + k], rsem.at[w * 6 + k],
                       (_flip(x, fx), _flip(y, fy), c))

    def small(src, dst, ssem, rsem, s, k):
        x, y, c, me = _place()
        fx, fy = _CHIP_FLIPS[k]
        sem = 6 * n + 3 * s + k
        return _remote(src[n + s], dst[n + s].at[me], ssem.at[sem], rsem.at[sem], (_flip(x, fx), _flip(y, fy), c))

    def own(src, dst, ssem, rsem, i):
        x, y, c, me = _place()
        sem = 6 * n + 3 * ns + i
        if i < n:
            to = dst[i].at[me, pl.ds(into[i][2], shards[i].shape[0])]
        else:
            to = dst[i].at[me]
        return _remote(src[i], to, ssem.at[sem], rsem.at[sem], (x, y, 1 - c))

    def landed(dst, ssem, rsem, w, k, sender_c, sem_off):
        x, y, c, _ = _place()
        fx, fy = _CHIP_FLIPS[k]
        got = dst[w].at[2 * _flip(x, fx) + _flip(y, fy), dst_rows(w, sender_c)]
        return _remote(got, got, ssem.at[w * 6 + sem_off + k], rsem.at[w * 6 + sem_off + k], (x, y, 1 - c))

    def start(src, dst, ssem, rsem):
        for i in range(n + ns):
            own(src, dst, ssem, rsem, i).start()
        for s in range(ns):
            for k in range(3):
                small(src, dst, ssem, rsem, s, k).start()
        for w in range(n):
            for k in range(3):
                ici(src, dst, ssem, rsem, w, k).start()

    def mid(src, dst, ssem, rsem):
        c = lax.axis_index("c")
        for w in range(n):
            for k in range(3):
                landed(dst, ssem, rsem, w, k, c, 0).wait_recv()
                landed(dst, ssem, rsem, w, k, c, 3).start()

    def end(src, dst, ssem, rsem):
        c = lax.axis_index("c")
        for w in range(n):
            for k in range(3):
                landed(dst, ssem, rsem, w, k, 1 - c, 3).wait_recv()
        for i in range(n + ns):
            own(src, dst, ssem, rsem, i).wait()
        for s in range(ns):
            for k in range(3):
                small(src, dst, ssem, rsem, s, k).wait()
        for w in range(n):
            for k in range(3):
                ici(src, dst, ssem, rsem, w, k).wait_send()
                landed(dst, ssem, rsem, w, k, c, 3).wait_send()

    outs = [jax.ShapeDtypeStruct((N_CHIPS, rows, a.shape[1]), a.dtype) for a, (_, rows, _) in zip(shards, into)]
    outs += [jax.ShapeDtypeStruct((N_CHIPS,) + a.shape, a.dtype) for a in smalls]
    filled = [(w, arr) for w, (arr, _, _) in enumerate(into) if arr is not None]
    side = _Side(list(shards) + list(smalls) + [arr for _, arr in filled], outs, 7 * n + 4 * ns, start, mid, end, mid_step)
    side.aliases = {n + ns + i: w for i, (w, _) in enumerate(filled)}
    return side


def _host(body, n_in, n_out, side, grid):
    if side is None:
        return body
    ns_in, ns_out = len(side.ins), len(side.outs)
    n_steps = math.prod(grid)
    mid_step = side.mid_step
    if side.mid is not None and not isinstance(mid_step, int):
        mid_step = min(n_steps - 1, int(mid_step * n_steps))

    def wrapped(*refs):
        ins, s_ins = refs[:n_in], refs[n_in:n_in + ns_in]
        outs = refs[n_in + ns_in:n_in + ns_in + n_out]
        s_outs = refs[n_in + ns_in + n_out:n_in + ns_in + n_out + ns_out]
        rest = refs[n_in + ns_in + n_out + ns_out:]
        sems = rest[-2:]
        step = 0
        for axis, extent in enumerate(grid):
            step = step * extent + pl.program_id(axis)

        @pl.when(step == 0)
        def _():
            side.start(s_ins, s_outs, *sems)

        if side.mid is not None:
            @pl.when(step == mid_step)
            def _():
                side.mid(s_ins, s_outs, *sems)

        body(*ins, *outs, *rest[:-2])

        @pl.when(step == n_steps - 1)
        def _():
            side.end(s_ins, s_outs, *sems)

    return wrapped


def _hosted_call(body, name, grid, in_specs, out_specs, out_shape, args, side, semantics=("parallel", "arbitrary"),
                 scratch=()):
    n_in, n_out = len(in_specs), len(out_specs)
    kern = _host(body, n_in, n_out, side, grid)
    if side is None:
        return pl.pallas_call(kern, name=name, grid=grid, in_specs=in_specs, out_specs=out_specs, out_shape=out_shape,
                              scratch_shapes=list(scratch), compiler_params=_params(semantics))(*args), []
    res = pl.pallas_call(
        kern, name=name, grid=grid, in_specs=in_specs + [HBM] * len(side.ins), out_specs=out_specs + [HBM] * len(side.outs),
        out_shape=list(out_shape) + [pltpu.HBM(s.shape, s.dtype) for s in side.outs],
        scratch_shapes=list(scratch) + [pltpu.SemaphoreType.DMA((side.n_sems,)), pltpu.SemaphoreType.DMA((side.n_sems,))],
        input_output_aliases={n_in + i: n_out + o for i, o in side.aliases.items()},
        compiler_params=_params(("arbitrary",) * len(grid)),
    )(*args, *side.ins)
    return res[:n_out], res[n_out:]


def _pair_swap(halves):
    n = len(halves)

    def body(*refs):
        src, dst = refs[:n], refs[n:2 * n]
        ssem, rsem = refs[2 * n:]
        x, y, c, _ = _place()
        cps = []
        for w in range(n):
            cp = _remote(src[w], dst[w], ssem.at[w], rsem.at[w], (x, y, 1 - c))
            cp.start()
            cps.append(cp)
        for cp in cps:
            cp.wait()

    outs = [jax.ShapeDtypeStruct(h.shape, h.dtype) for h in halves]
    return _comm_call("rs_pair_swap", body, halves, outs, n)


def _gather_small(vec):
    def body(src, dst, ssem, rsem):
        x, y, c, _ = _place()
        me = 4 * x + 2 * y + c
        cps = []
        for r in range(1, 8):
            dev = (_flip(x, r & 4), _flip(y, r & 2), _flip(c, r & 1))
            cp = _remote(src, dst.at[me], ssem.at[r - 1], rsem.at[r - 1], dev)
            cp.start()
            cps.append(cp)
        for cp in cps:
            cp.wait()

    out = jax.ShapeDtypeStruct((8,) + vec.shape, vec.dtype)
    return _comm_call("gather_small", body, [vec], [out], 7)[0]


def _pair_add(name, g, theirs, core):
    n, half, b = theirs.shape
    tr = _tile(half, 256, 16)
    nt = half // tr

    def body(c_ref, g_ref, t_ref, o_ref):
        o_ref[...] = (g_ref[...].astype(F32) + t_ref[...].astype(F32)).astype(o_ref.dtype)

    same = pl.BlockSpec((None, tr, b), lambda j, i, c: (j, i, 0))
    grid_spec = pltpu.PrefetchScalarGridSpec(
        num_scalar_prefetch=1, grid=(n, nt),
        in_specs=[pl.BlockSpec((None, tr, b), lambda j, i, c: (j, c[0] * nt + i, 0)), same], out_specs=same)
    return pl.pallas_call(body, name=name, grid_spec=grid_spec, out_shape=jax.ShapeDtypeStruct(theirs.shape, BF16),
                          compiler_params=_params(("parallel", "parallel")))(core, g, theirs)


def _chip_sum(name, parts, landed, chip1):
    n, r, c = landed.shape
    tr = _tile(r, 256, 16)

    def body(ix_ref, p_ref, s_ref, o_ref):
        acc = p_ref[...].astype(F32)
        for s in range(n):
            acc = acc + s_ref[s].astype(F32)
        o_ref[...] = acc

    grid_spec = pltpu.PrefetchScalarGridSpec(
        num_scalar_prefetch=1, grid=(r // tr,),
        in_specs=[pl.BlockSpec((None, tr, c), lambda i, ix: (ix[0], i, 0)), pl.BlockSpec((n, tr, c), lambda i, ix: (0, i, 0))],
        out_specs=pl.BlockSpec((tr, c), lambda i, ix: (i, 0)))
    return pl.pallas_call(body, name=name, grid_spec=grid_spec, out_shape=jax.ShapeDtypeStruct((r, c), F32),
                          compiler_params=_params(("parallel",)))(chip1, parts, landed)


def _sum_slots(name, stacked, first=None):
    n, r, c = stacked.shape
    tr = _tile(r, 256, 8)

    def body(*refs):
        s_ref, o_ref = refs[-2], refs[-1]
        acc = refs[0][...].astype(F32) if first is not None else s_ref[0].astype(F32)
        for s in range(0 if first is not None else 1, n):
            acc = acc + s_ref[s].astype(F32)
        o_ref[...] = acc

    row_spec = pl.BlockSpec((tr, c), lambda i: (i, 0))
    in_specs = ([row_spec] if first is not None else []) + [pl.BlockSpec((n, tr, c), lambda i: (0, i, 0))]
    args = ([first] if first is not None else []) + [stacked]
    return pl.pallas_call(
        body, name=name, grid=(r // tr,), in_specs=in_specs, out_specs=row_spec,
        out_shape=jax.ShapeDtypeStruct((r, c), F32), compiler_params=_params(("parallel",)),
    )(*args)


def _adam_math(w, g, m, v):
    bc1, bc2 = 1.0 - ADAM_B1 ** ADAM_STEP, 1.0 - ADAM_B2 ** ADAM_STEP
    nm = ADAM_B1 * m + (1.0 - ADAM_B1) * g
    nv = ADAM_B2 * v + (1.0 - ADAM_B2) * (g * g)
    return -ADAM_LR * ((nm / bc1) / (jnp.sqrt(nv / bc2) + ADAM_EPS) + ADAM_WD * w), nm, nv


def _adamw_halves(name, w, g_mine, g_theirs, m, v, core):
    r, c = w.shape
    h = r // 2
    tr = _tile(h, 128, 8)
    nth = h // tr

    def body(c_ref, w_ref, gm_ref, gt_ref, m_ref, v_ref, g_ref, d_ref, nm_ref, nv_ref):
        g = jnp.where(pl.program_id(0) // nth == c_ref[0], gm_ref[...], gt_ref[...])
        g_ref[...] = g
        d_ref[...], nm_ref[...], nv_ref[...] = _adam_math(w_ref[...], g, m_ref[...], v_ref[...])

    full = pl.BlockSpec((tr, c), lambda i, cr: (i, 0))
    half = pl.BlockSpec((tr, c), lambda i, cr: (i % nth, 0))
    grid_spec = pltpu.PrefetchScalarGridSpec(num_scalar_prefetch=1, grid=(r // tr,),
                                             in_specs=[full, half, half, full, full], out_specs=[full] * 4)
    return pl.pallas_call(body, name=name, grid_spec=grid_spec, out_shape=[jax.ShapeDtypeStruct((r, c), F32)] * 4,
                          compiler_params=_params(("parallel",)))(core, w, g_mine, g_theirs, m, v)


def _adamw(name, w, g, m, v):
    r, c = w.shape
    by_cols = r % 8 != 0 and c % LANES == 0
    tr, tc = (r, _tile(c, 256, LANES)) if by_cols else (_tile(r, 256, 8), c)

    def body(w_ref, g_ref, m_ref, v_ref, d_ref, nm_ref, nv_ref):
        d_ref[...], nm_ref[...], nv_ref[...] = _adam_math(w_ref[...], g_ref[...], m_ref[...], v_ref[...])

    spec = pl.BlockSpec((tr, tc), (lambda i: (0, i)) if by_cols else (lambda i: (i, 0)))
    return pl.pallas_call(
        body, name=name, grid=(c // tc if by_cols else r // tr,), in_specs=[spec] * 4, out_specs=[spec] * 3,
        out_shape=[jax.ShapeDtypeStruct((r, c), F32)] * 3, compiler_params=_params(("parallel",)),
    )(w, g, m, v)


def _pad_cols(a, cols):
    return jnp.pad(a, ((0, 0), (0, cols - a.shape[1])))


def _rot_cols(w):
    h = w.shape[-1] // 2
    return jnp.concatenate([-w[..., h:], w[..., :h]], axis=-1)


def _unrot_cols(d):
    h = d.shape[-1] // 2
    return jnp.concatenate([d[..., h:], -d[..., :h]], axis=-1)


def _logical(g):
    return jnp.transpose(g, (1, 0, 2)).reshape(g.shape[1], N_CHIPS * g.shape[2])


def _chunks(a, n):
    return jnp.transpose(a.reshape(a.shape[0], N_CHIPS, n), (1, 0, 2))


def kernel(x, positions, pre_mix_norm, w_in, q_a_norm, w_uq, kv_a_norm, w_ukv, b_forget, b_gate, w_branch_mla, w_branch_fox, w_out, post_mix_norm, pre_ffn_norm, w_up, conv_w, conv_b, w_down, post_ffn_norm, loss_target, m_pre_mix_norm, m_w_in, m_q_a_norm, m_w_uq, m_kv_a_norm, m_w_ukv, m_b_forget, m_b_gate, m_w_branch_mla, m_w_branch_fox, m_w_out, m_post_mix_norm, m_pre_ffn_norm, m_w_up, m_conv_w, m_conv_b, m_w_down, m_post_ffn_norm, v_pre_mix_norm, v_w_in, v_q_a_norm, v_w_uq, v_kv_a_norm, v_w_ukv, v_b_forget, v_b_gate, v_w_branch_mla, v_w_branch_fox, v_w_out, v_post_mix_norm, v_pre_ffn_norm, v_w_up, v_conv_w, v_conv_b, v_w_down, v_post_ffn_norm):
    n_seq, seq, d = x.shape
    n_tok = n_seq * seq
    d_in = N_CHIPS * w_in.shape[1]
    two_f = N_CHIPS * w_up.shape[1]
    ff_dim = two_f // 2
    assert d_in == QL + KVL + ROPE + 3 * HEADS * FDIM + HEADS + 2 * d
    n_in_shard = w_in.shape[1]
    in_pad = -(-n_in_shard // LANES) * LANES
    hd = HEADS * LANES
    xc, yc, cc = lax.axis_index("x"), lax.axis_index("y"), lax.axis_index("c")
    chip = 2 * xc + yc
    t_attn = _tile(seq, 512, 128)

    shards = [_pad_cols(w_in, in_pad).astype(BF16), w_uq.astype(BF16), w_ukv.astype(BF16), w_branch_mla.astype(BF16),
              w_branch_fox.astype(BF16), w_out.astype(BF16), w_up.astype(BF16), w_down.astype(BF16)]
    cw8 = jnp.pad(conv_w, ((0, 5), (0, 0)))
    (g_in, g_uq, g_ukv), (g_cw,) = _all_gather_weights(shards[:3], [cw8])
    n_attn_steps = n_seq * (seq // t_attn)
    side_proj = _gather_side([shards[3], shards[4], shards[5]], [], mid_step=0.9)
    side_ffn = _gather_side([shards[7]], [], mid_step=0.7)
    up_rows = shards[6].shape[0]
    up_cuts = [0, up_rows // 8, up_rows // 2, 7 * up_rows // 8, up_rows]

    def side_up(piece, filled, mid_step):
        lo, hi = up_cuts[piece], up_cuts[piece + 1]
        return _gather_side([shards[6][lo:hi]], [], mid_step=mid_step, into=[(filled, up_rows, lo)])

    o_q, o_kv, o_kpe = 0, QL, QL + KVL
    o_f = o_kpe + ROPE
    o_fl = o_f + 3 * hd
    o_g = o_fl + HEADS

    def chip_cols(lo, hi):
        out = []
        while lo < hi:
            j = lo // n_in_shard
            end = min(hi, (j + 1) * n_in_shard)
            out.append((j, lo - j * n_in_shard, end - j * n_in_shard))
            lo = end
        return out

    take = lambda lo, hi: [g_in[j, :, a:b] for j, a, b in chip_cols(lo, hi)]
    w_kpe = jnp.concatenate(take(o_kpe, o_f), axis=1)
    zeros = lambda n: jnp.zeros((d, n), BF16)
    win_p = jnp.concatenate(
        take(o_g, d_in) + take(o_q, o_kpe) + [w_kpe, zeros(LANES - ROPE), _rot_cols(w_kpe)] + take(o_fl, o_g)
        + [zeros(LANES - ROPE - HEADS)] + take(o_f, o_fl), axis=1)
    n_p = win_p.shape[1]
    cb_gm, cb_gf = 0, 1
    c_lat = 2 * d
    c_kx, c_kr = c_lat + QL + KVL, c_lat + QL + KVL + LANES
    n_pa = c_kr + LANES
    assert n_p == n_pa + 3 * hd

    uq3 = _logical(g_uq).reshape(QL, HEADS, NOPE + ROPE)
    pe = uq3[:, :, NOPE:]
    pad_pe = lambda a: jnp.pad(a, ((0, 0), (0, 0), (0, LANES - ROPE))).reshape(QL, hd)
    wuq_p = jnp.concatenate([uq3[:, :, :NOPE].reshape(QL, hd), pad_pe(pe), pad_pe(_rot_cols(pe))], axis=1)
    ukv3 = _logical(g_ukv).reshape(KVL, HEADS, NOPE + VDIM)
    wukv_p = jnp.concatenate([ukv3[:, :, :NOPE].reshape(KVL, hd), ukv3[:, :, NOPE:].reshape(KVL, hd)], axis=1)

    n_bm, n_up = w_branch_mla.shape[1], w_up.shape[1]
    l_bm, l_up = _Chunked(n_bm), _Chunked(n_up)
    wt = n_up // 2
    n_ut = two_f // wt
    il = lambda cblk: jnp.where(cblk < n_ut // 2, 2 * cblk, 2 * (cblk - n_ut // 2) + 1)
    l_il = _Plain(il)
    to_il = lambda a: a.reshape(a.shape[0], 2, n_ut // 2, wt).transpose(0, 2, 1, 3).reshape(a.shape[0], two_f)
    from_il = lambda a: a.reshape(a.shape[0], n_ut // 2, 2, wt).transpose(0, 2, 1, 3).reshape(a.shape[0], two_f)

    inv_freq = 1.0 / (ROPE_THETA ** (jnp.arange(0, ROPE, 2, dtype=F32) / ROPE))
    ang = positions.astype(F32).reshape(n_tok, 1) * inv_freq
    cos, sin = jnp.cos(ang), jnp.sin(ang)
    cs = _pad_cols(jnp.concatenate([cos, cos], axis=1), LANES)
    sn = _pad_cols(jnp.concatenate([sin, sin], axis=1), LANES)

    row = lambda v: v.reshape(1, -1)
    x2 = x.reshape(n_tok, d)
    tgt = loss_target.reshape(n_tok, d)

    (h,) = _rows("rms_pre_mix", lambda r, v: ([r[0] * _rstd(r[0]) * v[0]], []),
                 [(x2, d, 0)], [row(pre_mix_norm)], [(d, BF16)], [], n_tok)
    proj, (g_bm, g_bf, g_out) = _mm("proj_in", "nn", h, win_p, n_tok, n_pa, d, side=side_proj)
    w_out_full = g_out.reshape(d, d)
    tn_f = _tile(3 * hd, 1024, 128)
    assert n_pa % tn_f == 0
    proj_f, (g_up,) = _mm("proj_in_fox", "nn", h, win_p, n_tok, 3 * hd, d, tn=tn_f, lb=_Plain(lambda cblk: cblk + n_pa // tn_f),
                          out_dtype=BF16, side=side_up(0, None, 0.8))

    bf_vec = jnp.pad(row(b_forget), ((0, 0), (ROPE, LANES - ROPE - HEADS)))

    def lat_fwd(r, v):
        ql, kvl = r[0], r[1]
        return [ql * _rstd(ql) * v[0], kvl * _rstd(kvl) * v[1], r[2] * r[4] + r[3] * r[5]], []

    qn, kvn, rk = _rows("latent_norms", lat_fwd,
                        [(proj, QL, c_lat // QL), (proj, KVL, (c_lat + QL) // KVL), (proj, LANES, c_kx // LANES),
                         (proj, LANES, c_kr // LANES), (cs, LANES, 0), (sn, LANES, 0)],
                        [row(q_a_norm), row(kv_a_norm)], [(QL, BF16), (KVL, BF16), (LANES, BF16)], [], n_tok)
    q_p = _mm("q_up", "nn", qn, wuq_p, n_tok, 3 * hd, QL)
    kv_p = _mm("kv_up", "nn", kvn, wukv_p, n_tok, 2 * hd, KVL, out_dtype=BF16)

    def rope_q(r, v):
        c8, s8 = jnp.tile(r[3], (1, HEADS)), jnp.tile(r[4], (1, HEADS))
        return [r[0], r[1] * c8 + r[2] * s8], []

    q_nope, rq = _rows("rope_q", rope_q, [(q_p, hd, 0), (q_p, hd, 1), (q_p, hd, 2), (cs, LANES, 0), (sn, LANES, 0)], [],
                       [(hd, BF16), (hd, BF16)], [], n_tok)

    mla_q = [(q_nope, 0, False), (rq, 0, False)]
    mla_k = [(kv_p, 0, False), (rk, 0, True)]
    mla_v = (kv_p, 1)
    mla_scale = (NOPE + ROPE) ** -0.5
    (o_mla, lse_mla), (g_up,) = _attn_fwd("mla_fwd", mla_q, mla_k, mla_v, None, CHUNK, mla_scale, n_seq, seq, t_attn,
                                          side=side_up(1, g_up, max(n_attn_steps - 2, 0)))

    c_run = _seq_cumsum("forget_cumsum", proj, c_kr // LANES, n_seq, seq, False,
                        pre=lambda z, b: _log_sigmoid(z + b), vec=bf_vec)
    nb_attn = seq // t_attn
    c_rowf = jnp.transpose(c_run[:, ROPE:ROPE + HEADS].reshape(n_seq, nb_attn, t_attn, HEADS), (0, 1, 3, 2))
    fox_q, fox_k, fox_v = [(proj_f, 0, False)], [(proj_f, 1, False)], (proj_f, 2)
    fox_scale = FDIM ** -0.5
    fox_bias = (c_run, c_rowf)
    (o_fox, lse_fox), (g_up,) = _attn_fwd("fox_fwd", fox_q, fox_k, fox_v, fox_bias, 1, fox_scale, n_seq, seq, t_attn,
                                          side=side_up(2, g_up, max(n_attn_steps - 2, 0)))

    pm, (g_up,) = _mm("branch_mla", "nn", o_mla, g_bm, n_tok, d, hd, lb=l_bm, tn=n_bm, out_dtype=BF16,
                      side=side_up(3, g_up, 0.7))
    pf = _mm("branch_fox", "nn", o_fox, g_bf, n_tok, d, hd, lb=l_bm, tn=n_bm, out_dtype=BF16)
    bg = row(b_gate)

    def merge(r, v):
        return [_sigmoid(r[0] + v[0]) * r[2] + _sigmoid(r[1] + v[1]) * r[3]], []

    (merged,) = _rows("gate_merge", merge, [(proj, d, cb_gm), (proj, d, cb_gf), (pm, d, 0), (pf, d, 0)],
                      [bg[:, :d], bg[:, d:]], [(d, BF16)], [], n_tok)
    y1 = _mm("mix_out", "nn", merged, w_out_full, n_tok, d, d)

    def resid_norm(r, v):
        x1v = r[0] + r[1] * _rstd(r[1]) * v[0]
        return [x1v, x1v * _rstd(x1v) * v[1]], []

    x1, h2 = _rows("post_mix_pre_ffn", resid_norm, [(x2, d, 0), (y1, d, 0)], [row(post_mix_norm), row(pre_ffn_norm)],
                   [(d, F32), (d, BF16)], [], n_tok)

    u_il, got = _mm("ffn_up", "nn", h2, g_up, n_tok, two_f, d, lb=l_up, lo=l_il, tn=wt, out_dtype=BF16, side=side_ffn)
    w_down_full = got[0].reshape(ff_dim, d)
    cw_il = to_il(_logical(g_cw)[:3])
    cb_il = to_il(row(conv_b))
    act = _conv_glu_fwd(u_il, cw_il, cb_il, n_seq, seq, wt)
    ff = _mm("ffn_down", "nn", act, w_down_full, n_tok, d, ff_dim)

    def final(r, v):
        x1v, ffv, tg = r
        diff = x1v + ffv * _rstd(ffv) * v[0] - tg
        dx2v = diff / d
        dffv, dg4 = _rms_bwd(ffv, v[0], dx2v)
        sq = jnp.sum(jnp.sum(diff * diff, axis=1, keepdims=True), axis=0, keepdims=True)
        return [dx2v, dffv], [dg4, jnp.broadcast_to(sq, (1, LANES))]

    dx2, dff, dg_post_ffn, sq_sum = _rows("loss_post_ffn_bwd", final, [(x1, d, 0), (ff, d, 0), (tgt, d, 0)],
                                          [row(post_ffn_norm)], [(d, F32), (d, BF16)], [(1, d), (1, LANES)], n_tok)
    rs_parts, rs_landed = {}, {}
    core = jnp.reshape(cc, (1,)).astype(jnp.int32)

    def pair_reduce(tag, names, grads):
        theirs = _pair_split("rs_pair_split_" + tag, grads)
        for nm, g, b in zip(names, grads, theirs):
            rs_parts[nm] = _pair_add("rs_pair_add_" + nm, g, b, core)

    dact = _mm("ffn_down_dx", "nt", dff, w_down_full, n_tok, ff_dim, d, tn=wt, out_dtype=BF16)
    gw_down = _mm("ffn_down_dw", "tn", act, dff, ff_dim, d, n_tok, tm=wt, out_dtype=BF16)
    pair_reduce("down", ["w_down"], [gw_down.reshape(N_CHIPS, ff_dim // N_CHIPS, d)])
    d_il, conv_acc = _conv_glu_bwd_pre(u_il, dact, cw_il, cb_il, n_seq, seq, wt)
    du_il = _conv_bwd_input(d_il, cw_il, n_seq, seq, wt)
    gw_up, got = _mm("ffn_up_dw", "tn", h2, du_il, d, two_f, n_tok, lb=l_il, lo=l_up, tn=wt, out_dtype=BF16,
                     side=_scatter_side([rs_parts["w_down"]]))
    rs_landed["w_down"] = got[0]
    pair_reduce("up", ["w_up"], [gw_up])
    dh2, got = _mm("ffn_up_dx", "nt", du_il, g_up, n_tok, d, two_f, la=l_il, lb=l_up, tk=wt,
                   side=_scatter_side([rs_parts["w_up"]]))
    rs_landed["w_up"] = got[0]

    def mid_bwd(r, v):
        x1v, y1v, dx2v, dh2v = r
        d3, dg3 = _rms_bwd(x1v, v[1], dh2v)
        dx1v = dx2v + d3
        dy1v, dg2 = _rms_bwd(y1v, v[0], dx1v)
        return [dx1v, dy1v], [dg3, dg2]

    dx1, dy1, dg_pre_ffn, dg_post_mix = _rows(
        "pre_ffn_post_mix_bwd", mid_bwd, [(x1, d, 0), (y1, d, 0), (dx2, d, 0), (dh2, d, 0)],
        [row(post_mix_norm), row(pre_ffn_norm)], [(d, F32), (d, BF16)], [(1, d), (1, d)], n_tok)
    dmerged = _mm("mix_out_dx", "nt", dy1, w_out_full, n_tok, d, d, out_dtype=BF16)
    gw_out = _mm("mix_out_dw", "tn", merged, dy1, d, d, n_tok, out_dtype=BF16)

    def gate_bwd(r, v):
        zm, zf, pmv, pfv, dm = r
        gm, gf = _sigmoid(zm + v[0]), _sigmoid(zf + v[1])
        dzm, dzf = dm * pmv * gm * (1.0 - gm), dm * pfv * gf * (1.0 - gf)
        return [dm * gm, dm * gf, jnp.concatenate([dzm, dzf], axis=1)], [_colsum(dzm), _colsum(dzf)]

    dpm, dpf, dz, dbg_m, dbg_f = _rows(
        "gate_merge_bwd", gate_bwd, [(proj, d, cb_gm), (proj, d, cb_gf), (pm, d, 0), (pf, d, 0), (dmerged, d, 0)],
        [bg[:, :d], bg[:, d:]], [(d, BF16), (d, BF16), (2 * d, BF16)], [(1, d), (1, d)], n_tok)
    tk_b = min(n_bm, 512)
    do_mla = _mm("branch_mla_dx", "nt", dpm, g_bm, n_tok, hd, d, lb=l_bm, tk=tk_b)
    do_fox = _mm("branch_fox_dx", "nt", dpf, g_bf, n_tok, hd, d, lb=l_bm, tk=tk_b)
    gw_bm = _mm("branch_mla_dw", "tn", o_mla, dpm, hd, d, n_tok, lo=l_bm, tn=n_bm, out_dtype=BF16)
    gw_bf = _mm("branch_fox_dw", "tn", o_fox, dpf, hd, d, n_tok, lo=l_bm, tn=n_bm, out_dtype=BF16)

    pair_reduce("mix", ["w_out", "w_branch_mla", "w_branch_fox"], [gw_out.reshape(N_CHIPS, d // N_CHIPS, d), gw_bm, gw_bf])
    delta_mla, dob_mla = _attn_delta("mla_delta", o_mla, do_mla, n_tok)
    (dq_nope, drq, dk_nope, drk_g, dv_mla), got = _attn_bwd(
        "mla_bwd", mla_q, mla_k, mla_v, dob_mla, lse_mla, delta_mla, None, CHUNK, mla_scale, n_seq, seq, t_attn, BF16,
        side=_scatter_side([rs_parts[nm] for nm in ("w_out", "w_branch_mla", "w_branch_fox")]))
    rs_landed.update(zip(("w_out", "w_branch_mla", "w_branch_fox"), got))
    delta_fox, dob_fox = _attn_delta("fox_delta", o_fox, do_fox, n_tok)
    (dfq, dfk, dfv, dc_q, dc_k), _ = _attn_bwd("fox_bwd", fox_q, fox_k, fox_v, dob_fox, lse_fox, delta_fox, fox_bias, 1,
                                               fox_scale, n_seq, seq, t_attn, BF16)
    dc_k8 = jnp.transpose(dc_k, (0, 2, 4, 1, 3)).reshape(n_tok, HEADS)
    dc128 = dc_q[0] + dc_q[1] + jnp.pad(dc_k8, ((0, 0), (ROPE, LANES - ROPE - HEADS)))
    dlogf = _seq_cumsum("forget_cumsum_bwd", dc128, 0, n_seq, seq, True)

    def mla_pack(r, v):
        dqn_v, drq_v, dkn_v, dv_v, drk_a, drk_b, c1, s1 = r
        c8, s8 = jnp.tile(c1, (1, HEADS)), jnp.tile(s1, (1, HEADS))
        drk_v = drk_a + drk_b
        return [jnp.concatenate([dqn_v, drq_v * c8, drq_v * s8], axis=1), jnp.concatenate([dkn_v, dv_v], axis=1),
                drk_v * c1, drk_v * s1], []

    dq_p, dkv_p, dkx, dkr = _rows(
        "mla_rope_bwd", mla_pack,
        [(dq_nope, hd, 0), (drq, hd, 0), (dk_nope, hd, 0), (dv_mla, hd, 0), (drk_g[0], LANES, 0), (drk_g[1], LANES, 0),
         (cs, LANES, 0), (sn, LANES, 0)],
        [], [(3 * hd, BF16), (2 * hd, BF16), (LANES, F32), (LANES, F32)], [], n_tok)
    dqn = _mm("q_up_dx", "nt", dq_p, wuq_p, n_tok, QL, 3 * hd)
    gw_uq_p = _mm("q_up_dw", "tn", qn, dq_p, QL, 3 * hd, n_tok, out_dtype=BF16)
    dkvn = _mm("kv_up_dx", "nt", dkv_p, wukv_p, n_tok, KVL, 2 * hd)
    gw_ukv_p = _mm("kv_up_dw", "tn", kvn, dkv_p, KVL, 2 * hd, n_tok, out_dtype=BF16)

    def lat_bwd(r, v):
        ql, kvl, dqn_v, dkvn_v, dkx_v, dkr_v, zblk, dlf = r
        dql, dgq = _rms_bwd(ql, v[0], dqn_v)
        dkvl, dgkv = _rms_bwd(kvl, v[1], dkvn_v)
        dfl = dlf * _sigmoid(-(zblk + v[2]))
        return [jnp.concatenate([dql, dkvl, dkx_v, dkr_v + dfl], axis=1)], [dgq, dgkv, _colsum(dfl)]

    dlat, dg_q, dg_kv, dbf = _rows(
        "latent_bwd", lat_bwd,
        [(proj, QL, c_lat // QL), (proj, KVL, (c_lat + QL) // KVL), (dqn, QL, 0), (dkvn, KVL, 0), (dkx, LANES, 0),
         (dkr, LANES, 0), (proj, LANES, c_kr // LANES), (dlogf, LANES, 0)],
        [row(q_a_norm), row(kv_a_norm), bf_vec], [(QL + KVL + 2 * LANES, BF16)], [(1, QL), (1, KVL), (1, LANES)], n_tok)
    dproj = [dz, dlat, dfq, dfk, dfv]
    gw_in_p = _mm_parts("proj_in_dw", "tn", h, dproj, d, n_p, n_tok, tk=1024, out_dtype=BF16)

    f32 = lambda a: a.astype(F32)
    kr_blk = gw_in_p[:, c_kr:c_kr + LANES]
    d_kpe = (f32(gw_in_p[:, c_kx:c_kx + ROPE]) + _unrot_cols(f32(kr_blk[:, :ROPE]))).astype(BF16)
    in_pieces = [(o_q, gw_in_p, c_lat, QL + KVL), (o_kpe, d_kpe, 0, ROPE), (o_f, gw_in_p, n_pa, 3 * hd),
                 (o_fl, kr_blk, ROPE, HEADS), (o_g, gw_in_p, 0, 2 * d)]
    gc_in = []
    for j in range(N_CHIPS):
        lo, hi, cols = j * n_in_shard, (j + 1) * n_in_shard, []
        for first, arr, at, width in in_pieces:
            a, b = max(lo, first), min(hi, first + width)
            if a < b:
                cols.append(arr[:, at + a - first:at + b - first])
        cols.append(jnp.zeros((d, in_pad - n_in_shard), BF16))
        gc_in.append(jnp.concatenate(cols, axis=1))
    gc_in = jnp.stack(gc_in)
    uq_parts = [gw_uq_p[:, i * hd:(i + 1) * hd].reshape(QL, HEADS, LANES) for i in range(3)]
    d_pe = (f32(uq_parts[1][:, :, :ROPE]) + _unrot_cols(f32(uq_parts[2][:, :, :ROPE]))).astype(BF16)
    gc_uq = _chunks(jnp.concatenate([uq_parts[0], d_pe], axis=2).reshape(QL, HEADS * (NOPE + ROPE)), w_uq.shape[1])
    gc_ukv = _chunks(jnp.concatenate([gw_ukv_p[:, :hd].reshape(KVL, HEADS, NOPE), gw_ukv_p[:, hd:].reshape(KVL, HEADS, VDIM)],
                                     axis=2).reshape(KVL, HEADS * (NOPE + VDIM)), w_ukv.shape[1])
    grads = [gc_in, gc_uq, gc_ukv]

    late = ["w_in", "w_uq", "w_ukv"]
    pair_reduce("late", late, grads)
    dh, got = _mm_parts("proj_in_dx", "nt", dproj, win_p, n_tok, d, n_p, side=_scatter_side([rs_parts[nm] for nm in late]))
    rs_landed.update(zip(late, got))

    def first_bwd(r, v):
        dxa, dg1 = _rms_bwd(r[0], v[0], r[1])
        return [r[2] + dxa], [dg1]

    grad_x, dg_pre_mix = _rows("pre_mix_bwd", first_bwd, [(x2, d, 0), (dh, d, 0), (dx1, d, 0)], [row(pre_mix_norm)],
                               [(d, F32)], [(1, d)], n_tok)
    big = list(rs_parts)
    chip1 = jnp.reshape(chip, (1,)).astype(jnp.int32)
    halves = [_chip_sum("rs_chip_sum_" + nm, rs_parts[nm], rs_landed[nm], chip1) for nm in big]
    other = _pair_swap(halves)
    g_halves = dict(zip(big, zip(halves, other)))

    conv_acc_l = from_il(conv_acc)
    pieces = [dg_pre_mix, dg_q, dg_kv, dbf, dbg_m, dbg_f, dg_post_mix, dg_pre_ffn, conv_acc_l[3:4], dg_post_ffn,
              conv_acc_l[0:1], conv_acc_l[1:2], conv_acc_l[2:3], sq_sum]
    sizes = [p.shape[1] for p in pieces]
    flat = jnp.concatenate(pieces, axis=1)
    n_rows = -(-flat.shape[1] // (8 * LANES)) * 8
    flat = _pad_cols(flat, n_rows * LANES).reshape(n_rows, LANES)
    slots = lax.dynamic_update_slice(_gather_small(flat), flat[None], (2 * chip + cc, 0, 0))
    total = _sum_slots("small_sum", slots).reshape(1, n_rows * LANES)
    offs = [sum(sizes[:i]) for i in range(len(sizes))]
    tot = [total[0, o:o + s] for o, s in zip(offs, sizes)]
    loss = 0.5 * tot[13][0] / d
    g_small = {"pre_mix_norm": tot[0], "q_a_norm": tot[1], "kv_a_norm": tot[2], "b_forget": tot[3][ROPE:ROPE + HEADS],
               "b_gate": jnp.concatenate([tot[4], tot[5]]), "post_mix_norm": tot[6], "pre_ffn_norm": tot[7],
               "conv_b": tot[8], "post_ffn_norm": tot[9]}
    gcw_full = jnp.stack([tot[10], tot[11], tot[12]])
    g_conv_w = lax.dynamic_slice(gcw_full, (0, chip * n_up), (3, n_up))

    given = dict(pre_mix_norm=(pre_mix_norm, m_pre_mix_norm, v_pre_mix_norm), w_in=(w_in, m_w_in, v_w_in),
                 q_a_norm=(q_a_norm, m_q_a_norm, v_q_a_norm), w_uq=(w_uq, m_w_uq, v_w_uq),
                 kv_a_norm=(kv_a_norm, m_kv_a_norm, v_kv_a_norm), w_ukv=(w_ukv, m_w_ukv, v_w_ukv),
                 b_forget=(b_forget, m_b_forget, v_b_forget), b_gate=(b_gate, m_b_gate, v_b_gate),
                 w_branch_mla=(w_branch_mla, m_w_branch_mla, v_w_branch_mla),
                 w_branch_fox=(w_branch_fox, m_w_branch_fox, v_w_branch_fox), w_out=(w_out, m_w_out, v_w_out),
                 post_mix_norm=(post_mix_norm, m_post_mix_norm, v_post_mix_norm),
                 pre_ffn_norm=(pre_ffn_norm, m_pre_ffn_norm, v_pre_ffn_norm), w_up=(w_up, m_w_up, v_w_up),
                 conv_w=(conv_w, m_conv_w, v_conv_w), conv_b=(conv_b, m_conv_b, v_conv_b),
                 w_down=(w_down, m_w_down, v_w_down), post_ffn_norm=(post_ffn_norm, m_post_ffn_norm, v_post_ffn_norm))
    order = list(given)
    grad, delta, new_m, new_v = {}, {}, {}, {}
    for nm in big:
        mine, theirs = g_halves[nm]
        if nm == "w_in":
            full = jnp.concatenate([jnp.where(cc == 0, mine, theirs), jnp.where(cc == 0, theirs, mine)], axis=0)
            grad[nm] = full[:, :n_in_shard]
            tr_out = _adamw("adamw_" + nm, *[jnp.transpose(a) for a in (given[nm][0], grad[nm], given[nm][1], given[nm][2])])
            delta[nm], new_m[nm], new_v[nm] = [jnp.transpose(a) for a in tr_out]
            continue
        grad[nm], delta[nm], new_m[nm], new_v[nm] = _adamw_halves("adamw_" + nm, given[nm][0], mine, theirs, given[nm][1],
                                                                  given[nm][2], core)
    grad["conv_w"] = g_conv_w
    delta["conv_w"], new_m["conv_w"], new_v["conv_w"] = _adamw("adamw_conv_w", conv_w, g_conv_w, m_conv_w, v_conv_w)
    small = list(g_small)
    padded = [-(-g_small[nm].shape[0] // LANES) * LANES for nm in small]
    s_rows = -(-sum(padded) // (8 * LANES)) * 8

    def pack(vals):
        cat = jnp.concatenate([jnp.pad(a, (0, p - a.shape[0])) for a, p in zip(vals, padded)])
        return jnp.pad(cat, (0, s_rows * LANES - cat.shape[0])).reshape(s_rows, LANES)

    packed = _adamw("adamw_small", pack([given[nm][0] for nm in small]), pack([g_small[nm] for nm in small]),
                    pack([given[nm][1] for nm in small]), pack([given[nm][2] for nm in small]))
    s_offs = [sum(padded[:i]) for i in range(len(small))]
    for nm, o in zip(small, s_offs):
        n_el = g_small[nm].shape[0]
        grad[nm] = g_small[nm]
        delta[nm], new_m[nm], new_v[nm] = [p.reshape(-1)[o:o + n_el] for p in packed]
    return (loss, grad_x.reshape(n_seq, seq, d), *[grad[nm] for nm in order], *[delta[nm] for nm in order],
            *[new_m[nm] for nm in order], *[new_v[nm] for nm in order])
```

```python
import functools
import math

import jax
import jax.numpy as jnp
from jax import lax
from jax.experimental import pallas as pl
from jax.experimental.pallas import tpu as pltpu

F32, BF16 = jnp.float32, jnp.bfloat16
MESH = pl.DeviceIdType.MESH

HEADS = 8
NOPE, ROPE, VDIM = 128, 64, 128
QL, KVL = 512, 256
FDIM = 128
CHUNK = 64
ROPE_THETA = 10000.0
EPS = 1e-6
NEG_INF = -1e30
ADAM_LR, ADAM_B1, ADAM_B2, ADAM_EPS, ADAM_WD, ADAM_STEP = 0.001, 0.9, 0.999, 1e-08, 0.01, 10

VMEM_LIMIT_BYTES = 52 * 1024 * 1024
LANES = 128
N_CHIPS = 4


def _params(sem):
    return pltpu.CompilerParams(dimension_semantics=sem, vmem_limit_bytes=VMEM_LIMIT_BYTES)


def _tile(n, target, mult):
    if n <= target:
        return n
    t = (target // mult) * mult
    while t >= mult:
        if n % t == 0:
            return t
        t -= mult
    raise ValueError(f"no tile for {n} (target {target}, multiple of {mult})")


class _Plain:
    def __init__(self, perm=None):
        self.perm = perm

    def spec(self, tr, tc, rc):
        perm = self.perm

        def imap(i, j, k):
            r, c = rc(i, j, k)
            return (r, perm(c) if perm is not None else c)

        return pl.BlockSpec((tr, tc), imap)

    def shape(self, rows, cols):
        return (rows, cols)


class _Chunked:
    def __init__(self, n):
        self.n = n

    def spec(self, tr, tc, rc):
        assert self.n % tc == 0, (self.n, tc)
        per = self.n // tc

        def imap(i, j, k):
            r, c = rc(i, j, k)
            return (c // per, r, c % per)

        return pl.BlockSpec((None, tr, tc), imap)

    def shape(self, rows, cols):
        assert cols == N_CHIPS * self.n
        return (N_CHIPS, rows, self.n)


_DIMS = {"nn": (((1,), (0,)), ((), ())), "nt": (((1,), (1,)), ((), ())), "tn": (((0,), (0,)), ((), ()))}


def _mm_single(name, mode, a, b, m, n, k, tm, tn, la, lb, lo, out_dtype, side):
    if mode == "nn":
        a_spec = la.spec(tm, k, lambda i, j, kk: (i, 0))
        b_spec = lb.spec(k, tn, lambda i, j, kk: (0, j))
    elif mode == "nt":
        a_spec = la.spec(tm, k, lambda i, j, kk: (i, 0))
        b_spec = lb.spec(tn, k, lambda i, j, kk: (j, 0))
    else:
        a_spec = la.spec(k, tm, lambda i, j, kk: (0, i))
        b_spec = lb.spec(k, tn, lambda i, j, kk: (0, j))
    o_spec = lo.spec(tm, tn, lambda i, j, kk: (i, j))
    dims = _DIMS[mode]

    def body(a_ref, b_ref, o_ref):
        o_ref[...] = lax.dot_general(a_ref[...].astype(BF16), b_ref[...].astype(BF16), dims,
                                     preferred_element_type=F32).astype(o_ref.dtype)

    (out,), got = _hosted_call(body, name, (m // tm, n // tn, 1), [a_spec, b_spec], [o_spec],
                               [jax.ShapeDtypeStruct(lo.shape(m, n), out_dtype)], (a, b), side,
                               semantics=("parallel", "parallel", "arbitrary"))
    return out if side is None else (out, got)


def _mm(name, mode, a, b, m, n, k, *, tm=1024, tn=1024, tk=2048, la=None, lb=None, lo=None, out_dtype=F32, side=None):
    la, lb, lo = la or _Plain(), lb or _Plain(), lo or _Plain()
    tm, tn, tk = _tile(m, tm, 128), _tile(n, tn, 128), _tile(k, tk, 128)
    nk = k // tk
    if nk == 1:
        return _mm_single(name, mode, a, b, m, n, k, tm, tn, la, lb, lo, out_dtype, side)
    if mode == "nn":
        a_spec = la.spec(tm, tk, lambda i, j, kk: (i, kk))
        b_spec = lb.spec(tk, tn, lambda i, j, kk: (kk, j))
    elif mode == "nt":
        a_spec = la.spec(tm, tk, lambda i, j, kk: (i, kk))
        b_spec = lb.spec(tn, tk, lambda i, j, kk: (j, kk))
    else:
        a_spec = la.spec(tk, tm, lambda i, j, kk: (kk, i))
        b_spec = lb.spec(tk, tn, lambda i, j, kk: (kk, j))
    o_spec = lo.spec(tm, tn, lambda i, j, kk: (i, j))
    dims = _DIMS[mode]

    def body(a_ref, b_ref, o_ref, acc_ref):
        kk = pl.program_id(2)

        @pl.when(kk == 0)
        def _():
            acc_ref[...] = jnp.zeros_like(acc_ref)

        acc_ref[...] += lax.dot_general(a_ref[...].astype(BF16), b_ref[...].astype(BF16), dims,
                                        preferred_element_type=F32)

        @pl.when(kk == nk - 1)
        def _():
            o_ref[...] = acc_ref[...].astype(o_ref.dtype)

    (out,), got = _hosted_call(body, name, (m // tm, n // tn, nk), [a_spec, b_spec], [o_spec],
                               [jax.ShapeDtypeStruct(lo.shape(m, n), out_dtype)], (a, b), side,
                               semantics=("parallel", "parallel", "arbitrary"), scratch=[pltpu.VMEM((tm, tn), F32)])
    return out if side is None else (out, got)


def _mm_parts(name, mode, a, b, m, n, k, *, part=1024, tm=1024, tn=1024, tk=2048, out_dtype=F32, side=None):
    parts = b if mode == "tn" else a
    widths = [p.shape[1] for p in parts]
    assert all(w % part == 0 for w in widths) and sum(widths) == (n if mode == "tn" else k)
    offs = [sum(widths[:i]) // part for i in range(len(widths))]
    nblk = [w // part for w in widths]
    if mode == "tn":
        tn, tk = part, _tile(k, tk, 128)
    else:
        tk, tn = part, _tile(n, tn, 128)
    tm = _tile(m, tm, 128)
    nk = k // tk
    grid = (m // tm, n // tn, nk)
    np_ = len(parts)

    def inside(idx, p):
        return jnp.logical_and(idx >= offs[p], idx < offs[p] + nblk[p])

    def part_spec(p):
        if mode == "tn":
            def imap(i, j, kk):
                on = inside(j, p)
                return (jnp.where(on, kk, 0), jnp.clip(j - offs[p], 0, nblk[p] - 1))
            return pl.BlockSpec((tk, tn), imap)

        def imap(i, j, kk):
            return (i, jnp.clip(kk - offs[p], 0, nblk[p] - 1))
        return pl.BlockSpec((tm, tk), imap)

    if mode == "tn":
        in_specs = [pl.BlockSpec((tk, tm), lambda i, j, kk: (kk, i))] + [part_spec(p) for p in range(np_)]
        args = [a] + list(parts)
    else:
        in_specs = [part_spec(p) for p in range(np_)] + [pl.BlockSpec((tn, tk), lambda i, j, kk: (j, kk))]
        args = list(parts) + [b]
    dims = _DIMS[mode]

    def body(*refs):
        o_ref, acc_ref = refs[-2], refs[-1]
        j, kk = pl.program_id(1), pl.program_id(2)

        @pl.when(kk == 0)
        def _():
            acc_ref[...] = jnp.zeros_like(acc_ref)

        for p in range(np_):
            @pl.when(inside(j if mode == "tn" else kk, p))
            def _(p=p):
                lhs, rhs = (refs[0], refs[1 + p]) if mode == "tn" else (refs[p], refs[np_])
                acc_ref[...] += lax.dot_general(lhs[...].astype(BF16), rhs[...].astype(BF16), dims, preferred_element_type=F32)

        @pl.when(kk == nk - 1)
        def _():
            o_ref[...] = acc_ref[...].astype(o_ref.dtype)

    (out,), got = _hosted_call(body, name, grid, in_specs, [pl.BlockSpec((tm, tn), lambda i, j, kk: (i, j))],
                               [jax.ShapeDtypeStruct((m, n), out_dtype)], args, side,
                               semantics=("parallel", "parallel", "arbitrary"), scratch=[pltpu.VMEM((tm, tn), F32)])
    return out if side is None else (out, got)


def _rows(name, fn, rows_in, vecs_in, rows_out, accs_out, n_rows, tr=256):
    tr = _tile(n_rows, tr, 16)
    nr, nv, no = len(rows_in), len(vecs_in), len(rows_out)

    def body(*refs):
        ins, vecs = refs[:nr], refs[nr:nr + nv]
        outs, accs = refs[nr + nv:nr + nv + no], refs[nr + nv + no:]
        ro, ac = fn([r[...] for r in ins], [v[...] for v in vecs])
        for o_ref, val in zip(outs, ro):
            o_ref[...] = val.astype(o_ref.dtype)
        if accs:
            @pl.when(pl.program_id(0) == 0)
            def _():
                for a_ref in accs:
                    a_ref[...] = jnp.zeros_like(a_ref)

            for a_ref, val in zip(accs, ac):
                a_ref[...] += val

    in_specs = [pl.BlockSpec((tr, cols), functools.partial(lambda i, cb: (i, cb), cb=cb)) for _, cols, cb in rows_in]
    in_specs += [pl.BlockSpec(v.shape, lambda i: (0, 0)) for v in vecs_in]
    out_specs = [pl.BlockSpec((tr, cols), lambda i: (i, 0)) for cols, _ in rows_out]
    out_specs += [pl.BlockSpec((r, cols), lambda i: (0, 0)) for r, cols in accs_out]
    out_shape = [jax.ShapeDtypeStruct((n_rows, cols), dt) for cols, dt in rows_out]
    out_shape += [jax.ShapeDtypeStruct((r, cols), F32) for r, cols in accs_out]
    res = pl.pallas_call(
        body, name=name, grid=(n_rows // tr,), in_specs=in_specs, out_specs=out_specs, out_shape=out_shape,
        compiler_params=_params(("arbitrary",)),
    )(*[a for a, _, _ in rows_in], *vecs_in)
    return res


def _colsum(v):
    return jnp.sum(v, axis=0, keepdims=True)


def _rstd(x):
    return lax.rsqrt(jnp.mean(x * x, axis=-1, keepdims=True) + EPS)


def _rms_bwd(x, g, dy):
    r = _rstd(x)
    xh = x * r
    dxh = dy * g
    dx = r * (dxh - xh * jnp.mean(dxh * xh, axis=-1, keepdims=True))
    return dx, _colsum(dy * xh)


def _sigmoid(z):
    return 1.0 / (1.0 + jnp.exp(-z))


_GELU_K = math.sqrt(2.0 / math.pi)


def _gelu_parts(g):
    t = jnp.tanh(_GELU_K * (g + 0.044715 * g * g * g))
    gel = 0.5 * g * (1.0 + t)
    dgel = 0.5 * (1.0 + t) + 0.5 * g * (1.0 - t * t) * (_GELU_K * (1.0 + 3.0 * 0.044715 * g * g))
    return gel, dgel


def _diag_visible(t, unit):
    rows = lax.broadcasted_iota(jnp.int32, (t, t), 0)
    cols = lax.broadcasted_iota(jnp.int32, (t, t), 1)
    if unit > 1:
        sh = int(math.log2(unit))
        assert 1 << sh == unit and t % unit == 0
        rows, cols = jnp.right_shift(rows, sh), jnp.right_shift(cols, sh)
    return cols <= rows


def _lane_pick(tile, lane):
    idx = lax.broadcasted_iota(jnp.int32, tile.shape, 1)
    return jnp.sum(jnp.where(idx == lane, tile, 0.0), axis=1, keepdims=True)


def _lane_put(tile, lane, col):
    idx = lax.broadcasted_iota(jnp.int32, tile.shape, 1)
    return jnp.where(idx == lane, col, tile)


def _head_cat(refs, shared, rows, h):
    hs = slice(h * LANES, (h + 1) * LANES)
    vals = [(r[rows, :] if sh else r[rows, hs]).astype(BF16) for r, sh in zip(refs, shared)]
    return vals[0] if len(vals) == 1 else jnp.concatenate(vals, axis=1)


def _blk_rows(i, t):
    return pl.ds(pl.multiple_of(i * t, t), t)


def _piece_specs(pieces, rows, row_idx):
    return [pl.BlockSpec((rows, LANES if sh else HEADS * LANES), functools.partial(lambda b, i, cb: (row_idx(b, i), cb), cb=cb))
            for _, cb, sh in pieces]


def _attn_fwd(name, qp, kp, vp, bias, unit, scale, n_seq, seq, t, side=None):
    nb = seq // t
    n_tok = n_seq * seq
    nq, nk_p = len(qp), len(kp)
    q_sh, k_sh = [p[2] for p in qp], [p[2] for p in kp]
    nbias = 2 if bias is not None else 0

    def body(*refs):
        q_refs, k_refs = refs[:nq], refs[nq:nq + nk_p]
        v_ref = refs[nq + nk_p]
        bias_refs = refs[nq + nk_p + 1:nq + nk_p + 1 + nbias]
        o_ref, lse_ref = refs[nq + nk_p + 1 + nbias:]
        qi = pl.program_id(1)
        lse_tile = jnp.zeros((t, LANES), F32)
        for h in range(HEADS):
            hs = slice(h * LANES, (h + 1) * LANES)
            q = _head_cat(q_refs, q_sh, slice(None), h)
            cq = _lane_pick(bias_refs[0][...], ROPE + h) if bias is not None else None

            def block(kb, carry, diag, h=h, hs=hs, q=q, cq=cq):
                m, l, acc = carry
                rows = _blk_rows(kb, t)
                s = lax.dot_general(q, _head_cat(k_refs, k_sh, rows, h), _DIMS["nt"], preferred_element_type=F32) * scale
                if bias is not None:
                    s = s + cq - bias_refs[1][kb, h:h + 1, :]
                if diag:
                    s = jnp.where(_diag_visible(t, unit), s, NEG_INF)
                m_new = jnp.maximum(m, jnp.max(s, axis=1, keepdims=True))
                alpha = jnp.exp(m - m_new)
                p = jnp.exp(s - m_new)
                l = alpha * l + jnp.sum(p, axis=1, keepdims=True)
                acc = alpha * acc + jnp.dot(p.astype(BF16), v_ref[rows, hs].astype(BF16), preferred_element_type=F32)
                return m_new, l, acc

            init = (jnp.full((t, 1), NEG_INF, F32), jnp.zeros((t, 1), F32), jnp.zeros((t, LANES), F32))
            carry = lax.fori_loop(0, qi, lambda kb, c: block(kb, c, False), init)
            m, l, acc = block(qi, carry, True)
            o_ref[:, hs] = acc / l
            lse_tile = _lane_put(lse_tile, h, m + jnp.log(l))
        lse_ref[...] = lse_tile

    tile_row = lambda b, i: b * nb + i
    seq_row = lambda b, i: b
    lane_tile = pl.BlockSpec((t, LANES), lambda b, i: (b * nb + i, 0))
    in_specs = _piece_specs(qp, t, tile_row) + _piece_specs(kp, seq, seq_row) + _piece_specs([vp + (False,)], seq, seq_row)
    args = [p[0] for p in qp] + [p[0] for p in kp] + [vp[0]]
    if bias is not None:
        in_specs += [lane_tile, pl.BlockSpec((None, nb, HEADS, t), lambda b, i: (b, 0, 0, 0))]
        args += list(bias)
    return _hosted_call(
        body, name, (n_seq, nb), in_specs,
        [pl.BlockSpec((t, HEADS * LANES), lambda b, i: (b * nb + i, 0)), lane_tile],
        [jax.ShapeDtypeStruct((n_tok, HEADS * LANES), F32), jax.ShapeDtypeStruct((n_tok, LANES), F32)], args, side)


HEAD_GROUPS = 2


def _attn_delta(name, o, do, n_tok):
    def fn(r, v):
        o_v, do_v = r
        tile = jnp.zeros((o_v.shape[0], LANES), F32)
        for h in range(HEADS):
            hs = slice(h * LANES, (h + 1) * LANES)
            tile = _lane_put(tile, h, jnp.sum(do_v[:, hs] * o_v[:, hs], axis=1, keepdims=True))
        return [tile, do_v], []

    return _rows(name, fn, [(o, HEADS * LANES, 0), (do, HEADS * LANES, 0)], [], [(LANES, F32), (HEADS * LANES, BF16)], [], n_tok)


def _attn_bwd(name, qp, kp, vp, dob, lse, delta, bias, unit, scale, n_seq, seq, t, grad_dtype, side=None):
    nb = seq // t
    n_tok = n_seq * seq
    ng = HEAD_GROUPS
    hg = HEADS // ng
    gw = hg * LANES
    nq, nk_p = len(qp), len(kp)
    q_sh, k_sh = [p[2] for p in qp], [p[2] for p in kp]
    assert not any(q_sh) and nq == nk_p
    nbias = 2 if bias is not None else 0
    n_in = nq + nk_p + 4 + nbias
    n_out = nq + nk_p + 1 + nbias

    def body(*refs):
        q_refs, k_refs = refs[:nq], refs[nq:nq + nk_p]
        v_ref, dob_ref, lse_ref, delta_ref = refs[nq + nk_p:nq + nk_p + 4]
        bias_refs = refs[nq + nk_p + 4:n_in]
        dq_refs, dk_refs = refs[n_in:n_in + nq], refs[n_in + nq:n_in + nq + nk_p]
        dv_ref = refs[n_in + nq + nk_p]
        dq_s, dcq_s = refs[n_in + n_out:]
        g, ki = pl.program_id(1), pl.program_id(2)

        @pl.when(ki == 0)
        def _():
            dq_s[...] = jnp.zeros_like(dq_s)
            dcq_s[...] = jnp.zeros_like(dcq_s)

        shared_acc = [jnp.zeros((t, LANES), F32) for _ in range(nk_p)]
        for hl in range(hg):
            h = g * hg + hl
            hs = slice(hl * LANES, (hl + 1) * LANES)
            k = _head_cat(k_refs, k_sh, slice(None), hl)
            v = v_ref[:, hs].astype(BF16)
            ck = bias_refs[1][pl.ds(h, 1), :] if bias is not None else None

            def block(qb, carry, diag, h=h, hl=hl, hs=hs, k=k, v=v, ck=ck):
                dk_acc, dv_acc, dc_acc = carry
                rows = _blk_rows(qb, t)
                q = _head_cat(q_refs, q_sh, rows, hl)
                s = lax.dot_general(q, k, _DIMS["nt"], preferred_element_type=F32) * scale
                if bias is not None:
                    s = s + _lane_pick(bias_refs[0][rows, :], ROPE + h) - ck
                if diag:
                    s = jnp.where(_diag_visible(t, unit), s, NEG_INF)
                p = jnp.exp(s - _lane_pick(lse_ref[rows, :], h))
                do_b = dob_ref[rows, hs]
                dp = lax.dot_general(do_b, v, _DIMS["nt"], preferred_element_type=F32)
                ds = p * (dp - _lane_pick(delta_ref[rows, :], h))
                ds_b = ds.astype(BF16)
                dq_blk = jnp.dot(ds_b, k, preferred_element_type=F32)
                for n_p in range(nq):
                    dq_s[rows, n_p * gw + hl * LANES:n_p * gw + (hl + 1) * LANES] += dq_blk[:, n_p * LANES:(n_p + 1) * LANES]
                if bias is not None:
                    lane = lax.broadcasted_iota(jnp.int32, (t, LANES), 1)
                    dcq_s[rows, :] += jnp.where(lane == ROPE + h, jnp.sum(ds, axis=1, keepdims=True), 0.0)
                return (dk_acc + lax.dot_general(ds_b, q, _DIMS["tn"], preferred_element_type=F32),
                        dv_acc + lax.dot_general(p.astype(BF16), do_b, _DIMS["tn"], preferred_element_type=F32),
                        dc_acc - jnp.sum(ds, axis=0, keepdims=True))

            init = (jnp.zeros((t, nk_p * LANES), F32), jnp.zeros((t, LANES), F32), jnp.zeros((1, t), F32))
            carry = block(ki, init, True)
            dk_acc, dv_acc, dc_acc = lax.fori_loop(ki + 1, nb, lambda qb, c: block(qb, c, False), carry)
            for n_p in range(nk_p):
                part = dk_acc[:, n_p * LANES:(n_p + 1) * LANES] * scale
                if k_sh[n_p]:
                    shared_acc[n_p] = shared_acc[n_p] + part
                else:
                    dk_refs[n_p][:, hs] = part.astype(grad_dtype)
            dv_ref[:, hs] = dv_acc.astype(grad_dtype)
            if bias is not None:
                refs[n_in + n_out - 1][hl:hl + 1, :] = dc_acc
        for n_p in range(nk_p):
            if k_sh[n_p]:
                dk_refs[n_p][...] = shared_acc[n_p]

        @pl.when(ki == nb - 1)
        def _():
            for n_p in range(nq):
                dq_refs[n_p][...] = (dq_s[:, n_p * gw:(n_p + 1) * gw] * scale).astype(grad_dtype)
            if bias is not None:
                refs[n_in + n_out - 2][...] = dcq_s[...]

    def spec(rows, row_idx, cb, shared):
        if shared:
            return pl.BlockSpec((rows, LANES), lambda b, g, i: (row_idx(b, i), cb))
        return pl.BlockSpec((rows, gw), lambda b, g, i: (row_idx(b, i), cb * ng + g))

    tile_row = lambda b, i: b * nb + i
    seq_row = lambda b, i: b
    lane_seq = pl.BlockSpec((seq, LANES), lambda b, g, i: (b, 0))
    in_specs = [spec(seq, seq_row, cb, sh) for _, cb, sh in qp] + [spec(t, tile_row, cb, sh) for _, cb, sh in kp]
    in_specs += [spec(t, tile_row, vp[1], False), spec(seq, seq_row, 0, False), lane_seq, lane_seq]
    args = [p[0] for p in qp] + [p[0] for p in kp] + [vp[0], dob, lse, delta]
    if bias is not None:
        in_specs += [lane_seq, pl.BlockSpec((None, None, HEADS, t), lambda b, g, i: (b, i, 0, 0))]
        args += list(bias)
    group_tile = pl.BlockSpec((None, t, LANES), lambda b, g, i: (g, b * nb + i, 0))
    out_specs = [spec(seq, seq_row, 0, False)] * nq
    out_specs += [group_tile if sh else spec(t, tile_row, 0, False) for sh in k_sh] + [spec(t, tile_row, 0, False)]
    head_shape = jax.ShapeDtypeStruct((n_tok, HEADS * LANES), grad_dtype)
    out_shape = [head_shape] * nq + [jax.ShapeDtypeStruct((ng, n_tok, LANES), F32) if sh else head_shape for sh in k_sh]
    out_shape.append(head_shape)
    if bias is not None:
        out_specs += [pl.BlockSpec((None, seq, LANES), lambda b, g, i: (g, b, 0)),
                      pl.BlockSpec((None, None, None, hg, t), lambda b, g, i: (b, g, i, 0, 0))]
        out_shape += [jax.ShapeDtypeStruct((ng, n_tok, LANES), F32), jax.ShapeDtypeStruct((n_seq, ng, nb, hg, t), F32)]
    return _hosted_call(body, name, (n_seq, ng, nb), in_specs, out_specs, out_shape, args, side,
                        semantics=("parallel", "arbitrary", "arbitrary"),
                        scratch=[pltpu.VMEM((seq, nq * gw), F32), pltpu.VMEM((seq, LANES), F32)])


def _seq_cumsum(name, x, col_block, n_seq, seq, reverse, pre=None, vec=None):
    t = _tile(seq, 256, 128)
    nb = seq // t

    def body(*refs):
        x_ref = refs[0]
        vec_ref = refs[1] if vec is not None else None
        o_ref, carry = refs[-2], refs[-1]

        @pl.when(pl.program_id(1) == 0)
        def _():
            carry[...] = jnp.zeros_like(carry)

        v = x_ref[...]
        if pre is not None:
            v = pre(v, vec_ref[...])
        r = lax.broadcasted_iota(jnp.int32, (t, t), 0)
        c = lax.broadcasted_iota(jnp.int32, (t, t), 1)
        tri = jnp.where((c >= r) if reverse else (c <= r), 1.0, 0.0).astype(BF16)
        hi = v.astype(BF16)
        mid = (v - hi.astype(F32)).astype(BF16)
        lo = (v - hi.astype(F32) - mid.astype(F32)).astype(BF16)
        acc = jnp.dot(tri, hi, preferred_element_type=F32)
        acc += jnp.dot(tri, mid, preferred_element_type=F32)
        acc += jnp.dot(tri, lo, preferred_element_type=F32)
        o_ref[...] = acc + carry[...]
        carry[...] += _colsum(v)

    blk = (lambda b, i: (b * nb + nb - 1 - i)) if reverse else (lambda b, i: (b * nb + i))
    in_specs = [pl.BlockSpec((t, LANES), lambda b, i: (blk(b, i), col_block))]
    args = [x]
    if vec is not None:
        in_specs.append(pl.BlockSpec(vec.shape, lambda b, i: (0, 0)))
        args.append(vec)
    return pl.pallas_call(
        body, name=name, grid=(n_seq, nb), in_specs=in_specs,
        out_specs=pl.BlockSpec((t, LANES), lambda b, i: (blk(b, i), 0)),
        out_shape=jax.ShapeDtypeStruct((n_seq * seq, LANES), F32),
        scratch_shapes=[pltpu.VMEM((1, LANES), F32)],
        compiler_params=_params(("arbitrary", "arbitrary")),
    )(*args)


def _log_sigmoid(z):
    return -(jnp.maximum(-z, 0.0) + jnp.log(1.0 + jnp.exp(-jnp.abs(z))))


def _shift_down(u, prev_ref, n):
    out = pltpu.roll(u, n, 0)
    row = lax.broadcasted_iota(jnp.int32, u.shape, 0)
    for r in range(n):
        out = jnp.where(row == r, prev_ref[8 - n + r:8 - n + r + 1, :], out)
    return out


def _shift_up(u, next_ref, n):
    ts = u.shape[0]
    out = pltpu.roll(u, ts - n, 0)
    row = lax.broadcasted_iota(jnp.int32, u.shape, 0)
    for r in range(n):
        out = jnp.where(row == ts - n + r, next_ref[r:r + 1, :], out)
    return out


def _conv_taps(u, prev_ref, w_ref, b_ref):
    s1, s2 = _shift_down(u, prev_ref, 1), _shift_down(u, prev_ref, 2)
    return (w_ref[0:1, :] * s2 + w_ref[1:2, :] * s1 + w_ref[2:3, :] * u) + b_ref[...], s1, s2


def _conv_glu_fwd(u_il, cw_il, cb_il, n_seq, seq, wt):
    n_tok, two_f = u_il.shape
    nct = two_f // (2 * wt)
    ts = _tile(seq, 256, 8)
    ns = seq // ts

    def body(u_ref, w_ref, b_ref, a_ref, carry):
        @pl.when(pl.program_id(2) == 0)
        def _():
            carry[...] = jnp.zeros_like(carry)

        u = u_ref[...].astype(F32)
        uc, _, _ = _conv_taps(u, carry, w_ref, b_ref)
        gel, _ = _gelu_parts(uc[:, :wt])
        a_ref[...] = (gel * uc[:, wt:]).astype(a_ref.dtype)
        carry[...] = u[ts - 8:, :]

    return pl.pallas_call(
        body, name="conv_glu_fwd", grid=(nct, n_seq, ns),
        in_specs=[pl.BlockSpec((ts, 2 * wt), lambda j, b, s: (b * ns + s, j)),
                  pl.BlockSpec((3, 2 * wt), lambda j, b, s: (0, j)),
                  pl.BlockSpec((1, 2 * wt), lambda j, b, s: (0, j))],
        out_specs=pl.BlockSpec((ts, wt), lambda j, b, s: (b * ns + s, j)),
        out_shape=jax.ShapeDtypeStruct((n_tok, two_f // 2), BF16),
        scratch_shapes=[pltpu.VMEM((8, 2 * wt), F32)],
        compiler_params=_params(("parallel", "arbitrary", "arbitrary")),
    )(u_il, cw_il, cb_il)


def _conv_glu_bwd_pre(u_il, da, cw_il, cb_il, n_seq, seq, wt):
    n_tok, two_f = u_il.shape
    nct = two_f // (2 * wt)
    ts = _tile(seq, 256, 8)
    ns = seq // ts

    def body(u_ref, da_ref, w_ref, b_ref, d_ref, acc_ref, carry):
        first = jnp.logical_and(pl.program_id(1) == 0, pl.program_id(2) == 0)

        @pl.when(first)
        def _():
            acc_ref[...] = jnp.zeros_like(acc_ref)

        @pl.when(pl.program_id(2) == 0)
        def _():
            carry[...] = jnp.zeros_like(carry)

        u = u_ref[...].astype(F32)
        uc, s1, s2 = _conv_taps(u, carry, w_ref, b_ref)
        gel, dgel = _gelu_parts(uc[:, :wt])
        da_v = da_ref[...].astype(F32)
        d = jnp.concatenate([da_v * uc[:, wt:] * dgel, da_v * gel], axis=1)
        d_ref[...] = d.astype(d_ref.dtype)
        acc_ref[0:1, :] += _colsum(d * s2)
        acc_ref[1:2, :] += _colsum(d * s1)
        acc_ref[2:3, :] += _colsum(d * u)
        acc_ref[3:4, :] += _colsum(d)
        carry[...] = u[ts - 8:, :]

    return pl.pallas_call(
        body, name="conv_glu_bwd_pre", grid=(nct, n_seq, ns),
        in_specs=[pl.BlockSpec((ts, 2 * wt), lambda j, b, s: (b * ns + s, j)),
                  pl.BlockSpec((ts, wt), lambda j, b, s: (b * ns + s, j)),
                  pl.BlockSpec((3, 2 * wt), lambda j, b, s: (0, j)),
                  pl.BlockSpec((1, 2 * wt), lambda j, b, s: (0, j))],
        out_specs=[pl.BlockSpec((ts, 2 * wt), lambda j, b, s: (b * ns + s, j)),
                   pl.BlockSpec((8, 2 * wt), lambda j, b, s: (0, j))],
        out_shape=[jax.ShapeDtypeStruct((n_tok, two_f), BF16), jax.ShapeDtypeStruct((8, two_f), F32)],
        scratch_shapes=[pltpu.VMEM((8, 2 * wt), F32)],
        compiler_params=_params(("parallel", "arbitrary", "arbitrary")),
    )(u_il, da, cw_il, cb_il)


def _conv_bwd_input(d_il, cw_il, n_seq, seq, wt):
    n_tok, two_f = d_il.shape
    nct = two_f // (2 * wt)
    ts = _tile(seq, 256, 8)
    ns = seq // ts

    def body(d_ref, w_ref, o_ref, carry):
        @pl.when(pl.program_id(2) == 0)
        def _():
            carry[...] = jnp.zeros_like(carry)

        d = d_ref[...].astype(F32)
        o_ref[...] = (w_ref[2:3, :] * d + w_ref[1:2, :] * _shift_up(d, carry, 1)
                      + w_ref[0:1, :] * _shift_up(d, carry, 2)).astype(o_ref.dtype)
        carry[...] = d[:8, :]

    rev = lambda j, b, s: (b * ns + ns - 1 - s, j)
    return pl.pallas_call(
        body, name="conv_bwd_input", grid=(nct, n_seq, ns),
        in_specs=[pl.BlockSpec((ts, 2 * wt), rev), pl.BlockSpec((3, 2 * wt), lambda j, b, s: (0, j))],
        out_specs=pl.BlockSpec((ts, 2 * wt), rev),
        out_shape=jax.ShapeDtypeStruct((n_tok, two_f), BF16),
        scratch_shapes=[pltpu.VMEM((8, 2 * wt), F32)],
        compiler_params=_params(("parallel", "arbitrary", "arbitrary")),
    )(d_il, cw_il)


HBM = pl.BlockSpec(memory_space=pltpu.HBM)
_CHIP_FLIPS = ((1, 0), (0, 1), (1, 1))


def _place():
    x, y, c = lax.axis_index("x"), lax.axis_index("y"), lax.axis_index("c")
    return x, y, c, 2 * x + y


def _flip(v, f):
    return 1 - v if f else v


def _half_rows(c, half):
    return pl.ds(pl.multiple_of(c * half, 16), half)


def _remote(src, dst, ssem, rsem, dev):
    return pltpu.make_async_remote_copy(src_ref=src, dst_ref=dst, send_sem=ssem, recv_sem=rsem,
                                        device_id=dev, device_id_type=MESH)


def _comm_call(name, body, ins, out_shapes, n_sems):
    return pl.pallas_call(
        body, name=name, in_specs=[HBM] * len(ins), out_specs=[HBM] * len(out_shapes),
        out_shape=[pltpu.HBM(s.shape, s.dtype) for s in out_shapes],
        scratch_shapes=[pltpu.SemaphoreType.DMA((n_sems,)), pltpu.SemaphoreType.DMA((n_sems,))],
    )(*ins)


def _all_gather_weights(shards, smalls):
    side = _gather_side(shards, smalls)
    nt = len(shards) + len(smalls)

    def body(*refs):
        for part in (side.start, side.mid, side.end):
            part(refs[:nt], refs[nt:2 * nt], *refs[2 * nt:])

    res = _comm_call("all_gather_weights", body, side.ins, side.outs, side.n_sems)
    return res[:len(shards)], res[len(shards):]


def _pair_split(name, grads):
    n = len(grads)

    def body(*refs):
        src, got = refs[:n], refs[n:2 * n]
        ssem, rsem = refs[2 * n:]
        x, y, c, _ = _place()
        cps = []
        for w in range(n):
            half = grads[w].shape[1] // 2
            cp = _remote(src[w].at[:, _half_rows(1 - c, half)], got[w], ssem.at[w], rsem.at[w], (x, y, 1 - c))
            cp.start()
            cps.append(cp)
        for cp in cps:
            cp.wait()

    outs = [jax.ShapeDtypeStruct((g.shape[0], g.shape[1] // 2, g.shape[2]), g.dtype) for g in grads]
    return _comm_call(name, body, grads, outs, n)


class _Side:
    def __init__(self, ins, outs, n_sems, start, mid, end, mid_step=None):
        self.ins, self.outs, self.n_sems = list(ins), list(outs), n_sems
        self.start, self.mid, self.end, self.mid_step = start, mid, end, mid_step
        self.aliases = {}


def _scatter_side(parts):
    n = len(parts)

    def copies(src, dst, ssem, rsem):
        x, y, c, _ = _place()
        out = []
        for w in range(n):
            for k, (fx, fy) in enumerate(_CHIP_FLIPS):
                px, py = _flip(x, fx), _flip(y, fy)
                out.append(_remote(src[w].at[2 * px + py], dst[w].at[k], ssem.at[w * 3 + k], rsem.at[w * 3 + k], (px, py, c)))
        return out

    def start(src, dst, ssem, rsem):
        for cp in copies(src, dst, ssem, rsem):
            cp.start()

    def end(src, dst, ssem, rsem):
        for cp in copies(src, dst, ssem, rsem):
            cp.wait()

    outs = [jax.ShapeDtypeStruct((3,) + p.shape[1:], p.dtype) for p in parts]
    return _Side(parts, outs, 3 * n, start, None, end)


def _gather_side(shards, smalls, mid_step=None, into=None):
    n, ns = len(shards), len(smalls)
    into = into or [(None, a.shape[0], 0) for a in shards]

    def dst_rows(w, c):
        half = shards[w].shape[0] // 2
        return pl.ds(pl.multiple_of(into[w][2] + c * half, 16), half)

    def ici(src, dst, ssem, rsem, w, k):
        x, y, c, me = _place()
        fx, fy = _CHIP_FLIPS[k]
        rows = _half_rows(c, shards[w].shape[0] // 2)
        return _remote(src[w].at[rows], dst[w].at[me, dst_rows(w, c)], ssem.at[w * 6 + k], rsem.at[w * 6 + k],
                       (_flip(x, fx), _flip(y, fy), c))

    def small(src, dst, ssem, rsem, s, k):
        x, y, c, me = _place()
        fx, fy = _CHIP_FLIPS[k]
        sem = 6 * n + 3 * s + k
        return _remote(src[n + s], dst[n + s].at[me], ssem.at[sem], rsem.at[sem], (_flip(x, fx), _flip(y, fy), c))

    def own(src, dst, ssem, rsem, i):
        x, y, c, me = _place()
        sem = 6 * n + 3 * ns + i
        if i < n:
            to = dst[i].at[me, pl.ds(into[i][2], shards[i].shape[0])]
        else:
            to = dst[i].at[me]
        return _remote(src[i], to, ssem.at[sem], rsem.at[sem], (x, y, 1 - c))

    def landed(dst, ssem, rsem, w, k, sender_c, sem_off):
        x, y, c, _ = _place()
        fx, fy = _CHIP_FLIPS[k]
        got = dst[w].at[2 * _flip(x, fx) + _flip(y, fy), dst_rows(w, sender_c)]
        return _remote(got, got, ssem.at[w * 6 + sem_off + k], rsem.at[w * 6 + sem_off + k], (x, y, 1 - c))

    def start(src, dst, ssem, rsem):
        for i in range(n + ns):
            own(src, dst, ssem, rsem, i).start()
        for s in range(ns):
            for k in range(3):
                small(src, dst, ssem, rsem, s, k).start()
        for w in range(n):
            for k in range(3):
                ici(src, dst, ssem, rsem, w, k).start()

    def mid(src, dst, ssem, rsem):
        c = lax.axis_index("c")
        for w in range(n):
            for k in range(3):
                landed(dst, ssem, rsem, w, k, c, 0).wait_recv()
                landed(dst, ssem, rsem, w, k, c, 3).start()

    def end(src, dst, ssem, rsem):
        c = lax.axis_index("c")
        for w in range(n):
            for k in range(3):
                landed(dst, ssem, rsem, w, k, 1 - c, 3).wait_recv()
        for i in range(n + ns):
            own(src, dst, ssem, rsem, i).wait()
        for s in range(ns):
            for k in range(3):
                small(src, dst, ssem, rsem, s, k).wait()
        for w in range(n):
            for k in range(3):
                ici(src, dst, ssem, rsem, w, k).wait_send()
                landed(dst, ssem, rsem, w, k, c, 3).wait_send()

    outs = [jax.ShapeDtypeStruct((N_CHIPS, rows, a.shape[1]), a.dtype) for a, (_, rows, _) in zip(shards, into)]
    outs += [jax.ShapeDtypeStruct((N_CHIPS,) + a.shape, a.dtype) for a in smalls]
    filled = [(w, arr) for w, (arr, _, _) in enumerate(into) if arr is not None]
    side = _Side(list(shards) + list(smalls) + [arr for _, arr in filled], outs, 7 * n + 4 * ns, start, mid, end, mid_step)
    side.aliases = {n + ns + i: w for i, (w, _) in enumerate(filled)}
    return side


def _host(body, n_in, n_out, side, grid):
    if side is None:
        return body
    ns_in, ns_out = len(side.ins), len(side.outs)
    n_steps = math.prod(grid)
    mid_step = side.mid_step
    if side.mid is not None and not isinstance(mid_step, int):
        mid_step = min(n_steps - 1, int(mid_step * n_steps))

    def wrapped(*refs):
        ins, s_ins = refs[:n_in], refs[n_in:n_in + ns_in]
        outs = refs[n_in + ns_in:n_in + ns_in + n_out]
        s_outs = refs[n_in + ns_in + n_out:n_in + ns_in + n_out + ns_out]
        rest = refs[n_in + ns_in + n_out + ns_out:]
        sems = rest[-2:]
        step = 0
        for axis, extent in enumerate(grid):
            step = step * extent + pl.program_id(axis)

        @pl.when(step == 0)
        def _():
            side.start(s_ins, s_outs, *sems)

        if side.mid is not None:
            @pl.when(step == mid_step)
            def _():
                side.mid(s_ins, s_outs, *sems)

        body(*ins, *outs, *rest[:-2])

        @pl.when(step == n_steps - 1)
        def _():
            side.end(s_ins, s_outs, *sems)

    return wrapped


def _hosted_call(body, name, grid, in_specs, out_specs, out_shape, args, side, semantics=("parallel", "arbitrary"),
                 scratch=()):
    n_in, n_out = len(in_specs), len(out_specs)
    kern = _host(body, n_in, n_out, side, grid)
    if side is None:
        return pl.pallas_call(kern, name=name, grid=grid, in_specs=in_specs, out_specs=out_specs, out_shape=out_shape,
                              scratch_shapes=list(scratch), compiler_params=_params(semantics))(*args), []
    res = pl.pallas_call(
        kern, name=name, grid=grid, in_specs=in_specs + [HBM] * len(side.ins), out_specs=out_specs + [HBM] * len(side.outs),
        out_shape=list(out_shape) + [pltpu.HBM(s.shape, s.dtype) for s in side.outs],
        scratch_shapes=list(scratch) + [pltpu.SemaphoreType.DMA((side.n_sems,)), pltpu.SemaphoreType.DMA((side.n_sems,))],
        input_output_aliases={n_in + i: n_out + o for i, o in side.aliases.items()},
        compiler_params=_params(("arbitrary",) * len(grid)),
    )(*args, *side.ins)
    return res[:n_out], res[n_out:]


def _pair_swap(halves):
    n = len(halves)

    def body(*refs):
        src, dst = refs[:n], refs[n:2 * n]
        ssem, rsem = refs[2 * n:]
        x, y, c, _ = _place()
        cps = []
        for w in range(n):
            cp = _remote(src[w], dst[w], ssem.at[w], rsem.at[w], (x, y, 1 - c))
            cp.start()
            cps.append(cp)
        for cp in cps:
            cp.wait()

    outs = [jax.ShapeDtypeStruct(h.shape, h.dtype) for h in halves]
    return _comm_call("rs_pair_swap", body, halves, outs, n)


def _gather_small(vec):
    def body(src, dst, ssem, rsem):
        x, y, c, _ = _place()
        me = 4 * x + 2 * y + c
        cps = []
        for r in range(1, 8):
            dev = (_flip(x, r & 4), _flip(y, r & 2), _flip(c, r & 1))
            cp = _remote(src, dst.at[me], ssem.at[r - 1], rsem.at[r - 1], dev)
            cp.start()
            cps.append(cp)
        for cp in cps:
            cp.wait()

    out = jax.ShapeDtypeStruct((8,) + vec.shape, vec.dtype)
    return _comm_call("gather_small", body, [vec], [out], 7)[0]


def _pair_add(name, g, theirs, core):
    n, half, b = theirs.shape
    tr = _tile(half, 256, 16)
    nt = half // tr

    def body(c_ref, g_ref, t_ref, o_ref):
        o_ref[...] = (g_ref[...].astype(F32) + t_ref[...].astype(F32)).astype(o_ref.dtype)

    same = pl.BlockSpec((None, tr, b), lambda j, i, c: (j, i, 0))
    grid_spec = pltpu.PrefetchScalarGridSpec(
        num_scalar_prefetch=1, grid=(n, nt),
        in_specs=[pl.BlockSpec((None, tr, b), lambda j, i, c: (j, c[0] * nt + i, 0)), same], out_specs=same)
    return pl.pallas_call(body, name=name, grid_spec=grid_spec, out_shape=jax.ShapeDtypeStruct(theirs.shape, BF16),
                          compiler_params=_params(("parallel", "parallel")))(core, g, theirs)


def _chip_sum(name, parts, landed, chip1):
    n, r, c = landed.shape
    tr = _tile(r, 256, 16)

    def body(ix_ref, p_ref, s_ref, o_ref):
        acc = p_ref[...].astype(F32)
        for s in range(n):
            acc = acc + s_ref[s].astype(F32)
        o_ref[...] = acc

    grid_spec = pltpu.PrefetchScalarGridSpec(
        num_scalar_prefetch=1, grid=(r // tr,),
        in_specs=[pl.BlockSpec((None, tr, c), lambda i, ix: (ix[0], i, 0)), pl.BlockSpec((n, tr, c), lambda i, ix: (0, i, 0))],
        out_specs=pl.BlockSpec((tr, c), lambda i, ix: (i, 0)))
    return pl.pallas_call(body, name=name, grid_spec=grid_spec, out_shape=jax.ShapeDtypeStruct((r, c), F32),
                          compiler_params=_params(("parallel",)))(chip1, parts, landed)


def _sum_slots(name, stacked):
    n, r, c = stacked.shape
    tr = _tile(r, 256, 8)

    def body(s_ref, o_ref):
        acc = s_ref[0]
        for s in range(1, n):
            acc = acc + s_ref[s]
        o_ref[...] = acc

    return pl.pallas_call(
        body, name=name, grid=(r // tr,), in_specs=[pl.BlockSpec((n, tr, c), lambda i: (0, i, 0))],
        out_specs=pl.BlockSpec((tr, c), lambda i: (i, 0)),
        out_shape=jax.ShapeDtypeStruct((r, c), F32), compiler_params=_params(("parallel",)),
    )(stacked)


def _adam_math(w, g, m, v):
    bc1, bc2 = 1.0 - ADAM_B1 ** ADAM_STEP, 1.0 - ADAM_B2 ** ADAM_STEP
    nm = ADAM_B1 * m + (1.0 - ADAM_B1) * g
    nv = ADAM_B2 * v + (1.0 - ADAM_B2) * (g * g)
    return -ADAM_LR * ((nm / bc1) / (jnp.sqrt(nv / bc2) + ADAM_EPS) + ADAM_WD * w), nm, nv


def _adamw_halves(name, w, g_mine, g_theirs, m, v, core):
    r, c = w.shape
    h = r // 2
    tr = _tile(h, 128, 8)
    nth = h // tr

    def body(c_ref, w_ref, gm_ref, gt_ref, m_ref, v_ref, g_ref, d_ref, nm_ref, nv_ref):
        g = jnp.where(pl.program_id(0) // nth == c_ref[0], gm_ref[...], gt_ref[...])
        g_ref[...] = g
        d_ref[...], nm_ref[...], nv_ref[...] = _adam_math(w_ref[...], g, m_ref[...], v_ref[...])

    full = pl.BlockSpec((tr, c), lambda i, cr: (i, 0))
    half = pl.BlockSpec((tr, c), lambda i, cr: (i % nth, 0))
    grid_spec = pltpu.PrefetchScalarGridSpec(num_scalar_prefetch=1, grid=(r // tr,),
                                             in_specs=[full, half, half, full, full], out_specs=[full] * 4)
    return pl.pallas_call(body, name=name, grid_spec=grid_spec, out_shape=[jax.ShapeDtypeStruct((r, c), F32)] * 4,
                          compiler_params=_params(("parallel",)))(core, w, g_mine, g_theirs, m, v)


def _adamw(name, w, g, m, v):
    r, c = w.shape
    by_cols = r % 8 != 0 and c % LANES == 0
    tr, tc = (r, _tile(c, 256, LANES)) if by_cols else (_tile(r, 256, 8), c)

    def body(w_ref, g_ref, m_ref, v_ref, d_ref, nm_ref, nv_ref):
        d_ref[...], nm_ref[...], nv_ref[...] = _adam_math(w_ref[...], g_ref[...], m_ref[...], v_ref[...])

    spec = pl.BlockSpec((tr, tc), (lambda i: (0, i)) if by_cols else (lambda i: (i, 0)))
    return pl.pallas_call(
        body, name=name, grid=(c // tc if by_cols else r // tr,), in_specs=[spec] * 4, out_specs=[spec] * 3,
        out_shape=[jax.ShapeDtypeStruct((r, c), F32)] * 3, compiler_params=_params(("parallel",)),
    )(w, g, m, v)


def _pad_cols(a, cols):
    return jnp.pad(a, ((0, 0), (0, cols - a.shape[1])))


def _rot_cols(w):
    h = w.shape[-1] // 2
    return jnp.concatenate([-w[..., h:], w[..., :h]], axis=-1)


def _unrot_cols(d):
    h = d.shape[-1] // 2
    return jnp.concatenate([d[..., h:], -d[..., :h]], axis=-1)


def _logical(g):
    return jnp.transpose(g, (1, 0, 2)).reshape(g.shape[1], N_CHIPS * g.shape[2])


def _chunks(a, n):
    return jnp.transpose(a.reshape(a.shape[0], N_CHIPS, n), (1, 0, 2))


def kernel(x, positions, pre_mix_norm, w_in, q_a_norm, w_uq, kv_a_norm, w_ukv, b_forget, b_gate, w_branch_mla, w_branch_fox, w_out, post_mix_norm, pre_ffn_norm, w_up, conv_w, conv_b, w_down, post_ffn_norm, loss_target, m_pre_mix_norm, m_w_in, m_q_a_norm, m_w_uq, m_kv_a_norm, m_w_ukv, m_b_forget, m_b_gate, m_w_branch_mla, m_w_branch_fox, m_w_out, m_post_mix_norm, m_pre_ffn_norm, m_w_up, m_conv_w, m_conv_b, m_w_down, m_post_ffn_norm, v_pre_mix_norm, v_w_in, v_q_a_norm, v_w_uq, v_kv_a_norm, v_w_ukv, v_b_forget, v_b_gate, v_w_branch_mla, v_w_branch_fox, v_w_out, v_post_mix_norm, v_pre_ffn_norm, v_w_up, v_conv_w, v_conv_b, v_w_down, v_post_ffn_norm):
    n_seq, seq, d = x.shape
    n_tok = n_seq * seq
    d_in = N_CHIPS * w_in.shape[1]
    two_f = N_CHIPS * w_up.shape[1]
    ff_dim = two_f // 2
    assert d_in == QL + KVL + ROPE + 3 * HEADS * FDIM + HEADS + 2 * d
    n_in_shard = w_in.shape[1]
    in_pad = -(-n_in_shard // LANES) * LANES
    hd = HEADS * LANES
    xc, yc, cc = lax.axis_index("x"), lax.axis_index("y"), lax.axis_index("c")
    chip = 2 * xc + yc
    t_attn = _tile(seq, 512, 128)

    shards = [_pad_cols(w_in, in_pad).astype(BF16), w_uq.astype(BF16), w_ukv.astype(BF16), w_branch_mla.astype(BF16),
              w_branch_fox.astype(BF16), w_out.astype(BF16), w_up.astype(BF16), w_down.astype(BF16)]
    cw8 = jnp.pad(conv_w, ((0, 5), (0, 0)))
    (g_in, g_uq, g_ukv), (g_cw,) = _all_gather_weights(shards[:3], [cw8])
    n_attn_steps = n_seq * (seq // t_attn)
    side_proj = _gather_side([shards[3], shards[4], shards[5]], [], mid_step=0.9)
    side_ffn = _gather_side([shards[7]], [], mid_step=0.7)
    up_rows = shards[6].shape[0]
    up_cuts = [0, up_rows // 8, up_rows // 2, 7 * up_rows // 8, up_rows]

    def side_up(piece, filled, mid_step):
        lo, hi = up_cuts[piece], up_cuts[piece + 1]
        return _gather_side([shards[6][lo:hi]], [], mid_step=mid_step, into=[(filled, up_rows, lo)])

    o_q, o_kv, o_kpe = 0, QL, QL + KVL
    o_f = o_kpe + ROPE
    o_fl = o_f + 3 * hd
    o_g = o_fl + HEADS

    def chip_cols(lo, hi):
        out = []
        while lo < hi:
            j = lo // n_in_shard
            end = min(hi, (j + 1) * n_in_shard)
            out.append((j, lo - j * n_in_shard, end - j * n_in_shard))
            lo = end
        return out

    take = lambda lo, hi: [g_in[j, :, a:b] for j, a, b in chip_cols(lo, hi)]
    w_kpe = jnp.concatenate(take(o_kpe, o_f), axis=1)
    zeros = lambda n: jnp.zeros((d, n), BF16)
    win_p = jnp.concatenate(
        take(o_g, d_in) + take(o_q, o_kpe) + [w_kpe, zeros(LANES - ROPE), _rot_cols(w_kpe)] + take(o_fl, o_g)
        + [zeros(LANES - ROPE - HEADS)] + take(o_f, o_fl), axis=1)
    n_p = win_p.shape[1]
    cb_gm, cb_gf = 0, 1
    c_lat = 2 * d
    c_kx, c_kr = c_lat + QL + KVL, c_lat + QL + KVL + LANES
    n_pa = c_kr + LANES
    assert n_p == n_pa + 3 * hd

    uq3 = _logical(g_uq).reshape(QL, HEADS, NOPE + ROPE)
    pe = uq3[:, :, NOPE:]
    pad_pe = lambda a: jnp.pad(a, ((0, 0), (0, 0), (0, LANES - ROPE))).reshape(QL, hd)
    wuq_p = jnp.concatenate([uq3[:, :, :NOPE].reshape(QL, hd), pad_pe(pe), pad_pe(_rot_cols(pe))], axis=1)
    ukv3 = _logical(g_ukv).reshape(KVL, HEADS, NOPE + VDIM)
    wukv_p = jnp.concatenate([ukv3[:, :, :NOPE].reshape(KVL, hd), ukv3[:, :, NOPE:].reshape(KVL, hd)], axis=1)

    n_bm, n_up = w_branch_mla.shape[1], w_up.shape[1]
    l_bm, l_up = _Chunked(n_bm), _Chunked(n_up)
    wt = n_up // 2
    n_ut = two_f // wt
    il = lambda cblk: jnp.where(cblk < n_ut // 2, 2 * cblk, 2 * (cblk - n_ut // 2) + 1)
    l_il = _Plain(il)
    to_il = lambda a: a.reshape(a.shape[0], 2, n_ut // 2, wt).transpose(0, 2, 1, 3).reshape(a.shape[0], two_f)
    from_il = lambda a: a.reshape(a.shape[0], n_ut // 2, 2, wt).transpose(0, 2, 1, 3).reshape(a.shape[0], two_f)

    inv_freq = 1.0 / (ROPE_THETA ** (jnp.arange(0, ROPE, 2, dtype=F32) / ROPE))
    ang = positions.astype(F32).reshape(n_tok, 1) * inv_freq
    cos, sin = jnp.cos(ang), jnp.sin(ang)
    cs = _pad_cols(jnp.concatenate([cos, cos], axis=1), LANES)
    sn = _pad_cols(jnp.concatenate([sin, sin], axis=1), LANES)

    row = lambda v: v.reshape(1, -1)
    x2 = x.reshape(n_tok, d)
    tgt = loss_target.reshape(n_tok, d)

    (h,) = _rows("rms_pre_mix", lambda r, v: ([r[0] * _rstd(r[0]) * v[0]], []),
                 [(x2, d, 0)], [row(pre_mix_norm)], [(d, BF16)], [], n_tok)
    proj, (g_bm, g_bf, g_out) = _mm("proj_in", "nn", h, win_p, n_tok, n_pa, d, tm=2048, side=side_proj)
    w_out_full = g_out.reshape(d, d)
    tn_f = _tile(3 * hd, 1024, 128)
    assert n_pa % tn_f == 0
    proj_f, (g_up,) = _mm("proj_in_fox", "nn", h, win_p, n_tok, 3 * hd, d, tn=tn_f, lb=_Plain(lambda cblk: cblk + n_pa // tn_f),
                          out_dtype=BF16, side=side_up(0, None, 0.8))

    bf_vec = jnp.pad(row(b_forget), ((0, 0), (ROPE, LANES - ROPE - HEADS)))

    def lat_fwd(r, v):
        ql, kvl = r[0], r[1]
        return [ql * _rstd(ql) * v[0], kvl * _rstd(kvl) * v[1], r[2] * r[4] + r[3] * r[5]], []

    qn, kvn, rk = _rows("latent_norms", lat_fwd,
                        [(proj, QL, c_lat // QL), (proj, KVL, (c_lat + QL) // KVL), (proj, LANES, c_kx // LANES),
                         (proj, LANES, c_kr // LANES), (cs, LANES, 0), (sn, LANES, 0)],
                        [row(q_a_norm), row(kv_a_norm)], [(QL, BF16), (KVL, BF16), (LANES, BF16)], [], n_tok)
    q_p = _mm("q_up", "nn", qn, wuq_p, n_tok, 3 * hd, QL)
    kv_p = _mm("kv_up", "nn", kvn, wukv_p, n_tok, 2 * hd, KVL, out_dtype=BF16)

    def rope_q(r, v):
        c8, s8 = jnp.tile(r[3], (1, HEADS)), jnp.tile(r[4], (1, HEADS))
        return [r[0], r[1] * c8 + r[2] * s8], []

    q_nope, rq = _rows("rope_q", rope_q, [(q_p, hd, 0), (q_p, hd, 1), (q_p, hd, 2), (cs, LANES, 0), (sn, LANES, 0)], [],
                       [(hd, BF16), (hd, BF16)], [], n_tok)

    mla_q = [(q_nope, 0, False), (rq, 0, False)]
    mla_k = [(kv_p, 0, False), (rk, 0, True)]
    mla_v = (kv_p, 1)
    mla_scale = (NOPE + ROPE) ** -0.5
    (o_mla, lse_mla), (g_up,) = _attn_fwd("mla_fwd", mla_q, mla_k, mla_v, None, CHUNK, mla_scale, n_seq, seq, t_attn,
                                          side=side_up(1, g_up, max(n_attn_steps - 2, 0)))

    c_run = _seq_cumsum("forget_cumsum", proj, c_kr // LANES, n_seq, seq, False,
                        pre=lambda z, b: _log_sigmoid(z + b), vec=bf_vec)
    nb_attn = seq // t_attn
    c_rowf = jnp.transpose(c_run[:, ROPE:ROPE + HEADS].reshape(n_seq, nb_attn, t_attn, HEADS), (0, 1, 3, 2))
    fox_q, fox_k, fox_v = [(proj_f, 0, False)], [(proj_f, 1, False)], (proj_f, 2)
    fox_scale = FDIM ** -0.5
    fox_bias = (c_run, c_rowf)
    (o_fox, lse_fox), (g_up,) = _attn_fwd("fox_fwd", fox_q, fox_k, fox_v, fox_bias, 1, fox_scale, n_seq, seq, t_attn,
                                          side=side_up(2, g_up, max(n_attn_steps - 2, 0)))

    pm, (g_up,) = _mm("branch_mla", "nn", o_mla, g_bm, n_tok, d, hd, lb=l_bm, tn=n_bm, out_dtype=BF16,
                      side=side_up(3, g_up, 0.7))
    pf = _mm("branch_fox", "nn", o_fox, g_bf, n_tok, d, hd, lb=l_bm, tn=n_bm, out_dtype=BF16)
    bg = row(b_gate)

    def merge(r, v):
        return [_sigmoid(r[0] + v[0]) * r[2] + _sigmoid(r[1] + v[1]) * r[3]], []

    (merged,) = _rows("gate_merge", merge, [(proj, d, cb_gm), (proj, d, cb_gf), (pm, d, 0), (pf, d, 0)],
                      [bg[:, :d], bg[:, d:]], [(d, BF16)], [], n_tok)
    y1 = _mm("mix_out", "nn", merged, w_out_full, n_tok, d, d)

    def resid_norm(r, v):
        x1v = r[0] + r[1] * _rstd(r[1]) * v[0]
        return [x1v, x1v * _rstd(x1v) * v[1]], []

    x1, h2 = _rows("post_mix_pre_ffn", resid_norm, [(x2, d, 0), (y1, d, 0)], [row(post_mix_norm), row(pre_ffn_norm)],
                   [(d, F32), (d, BF16)], [], n_tok)

    u_il, got = _mm("ffn_up", "nn", h2, g_up, n_tok, two_f, d, lb=l_up, lo=l_il, tm=2048, tn=wt, out_dtype=BF16,
                    side=side_ffn)
    w_down_full = got[0].reshape(ff_dim, d)
    cw_il = to_il(_logical(g_cw)[:3])
    cb_il = to_il(row(conv_b))
    act = _conv_glu_fwd(u_il, cw_il, cb_il, n_seq, seq, wt)
    ff = _mm("ffn_down", "nn", act, w_down_full, n_tok, d, ff_dim)

    def final(r, v):
        x1v, ffv, tg = r
        diff = x1v + ffv * _rstd(ffv) * v[0] - tg
        dx2v = diff / d
        dffv, dg4 = _rms_bwd(ffv, v[0], dx2v)
        sq = jnp.sum(jnp.sum(diff * diff, axis=1, keepdims=True), axis=0, keepdims=True)
        return [dx2v, dffv], [dg4, jnp.broadcast_to(sq, (1, LANES))]

    dx2, dff, dg_post_ffn, sq_sum = _rows("loss_post_ffn_bwd", final, [(x1, d, 0), (ff, d, 0), (tgt, d, 0)],
                                          [row(post_ffn_norm)], [(d, F32), (d, BF16)], [(1, d), (1, LANES)], n_tok)
    rs_parts, rs_landed = {}, {}
    core = jnp.reshape(cc, (1,)).astype(jnp.int32)

    def pair_reduce(tag, names, grads):
        theirs = _pair_split("rs_pair_split_" + tag, grads)
        for nm, g, b in zip(names, grads, theirs):
            rs_parts[nm] = _pair_add("rs_pair_add_" + nm, g, b, core)

    dact = _mm("ffn_down_dx", "nt", dff, w_down_full, n_tok, ff_dim, d, tm=2048, tn=wt, out_dtype=BF16)
    gw_down = _mm("ffn_down_dw", "tn", act, dff, ff_dim, d, n_tok, tm=wt, out_dtype=BF16)
    pair_reduce("down", ["w_down"], [gw_down.reshape(N_CHIPS, ff_dim // N_CHIPS, d)])
    d_il, conv_acc = _conv_glu_bwd_pre(u_il, dact, cw_il, cb_il, n_seq, seq, wt)
    du_il = _conv_bwd_input(d_il, cw_il, n_seq, seq, wt)
    gw_up, got = _mm("ffn_up_dw", "tn", h2, du_il, d, two_f, n_tok, lb=l_il, lo=l_up, tn=wt, out_dtype=BF16,
                     side=_scatter_side([rs_parts["w_down"]]))
    rs_landed["w_down"] = got[0]
    pair_reduce("up", ["w_up"], [gw_up])
    dh2, got = _mm("ffn_up_dx", "nt", du_il, g_up, n_tok, d, two_f, la=l_il, lb=l_up, tk=wt,
                   side=_scatter_side([rs_parts["w_up"]]))
    rs_landed["w_up"] = got[0]

    def mid_bwd(r, v):
        x1v, y1v, dx2v, dh2v = r
        d3, dg3 = _rms_bwd(x1v, v[1], dh2v)
        dx1v = dx2v + d3
        dy1v, dg2 = _rms_bwd(y1v, v[0], dx1v)
        return [dx1v, dy1v], [dg3, dg2]

    dx1, dy1, dg_pre_ffn, dg_post_mix = _rows(
        "pre_ffn_post_mix_bwd", mid_bwd, [(x1, d, 0), (y1, d, 0), (dx2, d, 0), (dh2, d, 0)],
        [row(post_mix_norm), row(pre_ffn_norm)], [(d, F32), (d, BF16)], [(1, d), (1, d)], n_tok)
    dmerged = _mm("mix_out_dx", "nt", dy1, w_out_full, n_tok, d, d, out_dtype=BF16)
    gw_out = _mm("mix_out_dw", "tn", merged, dy1, d, d, n_tok, out_dtype=BF16)

    def gate_bwd(r, v):
        zm, zf, pmv, pfv, dm = r
        gm, gf = _sigmoid(zm + v[0]), _sigmoid(zf + v[1])
        dzm, dzf = dm * pmv * gm * (1.0 - gm), dm * pfv * gf * (1.0 - gf)
        return [dm * gm, dm * gf, jnp.concatenate([dzm, dzf], axis=1)], [_colsum(dzm), _colsum(dzf)]

    dpm, dpf, dz, dbg_m, dbg_f = _rows(
        "gate_merge_bwd", gate_bwd, [(proj, d, cb_gm), (proj, d, cb_gf), (pm, d, 0), (pf, d, 0), (dmerged, d, 0)],
        [bg[:, :d], bg[:, d:]], [(d, BF16), (d, BF16), (2 * d, BF16)], [(1, d), (1, d)], n_tok)
    tk_b = min(n_bm, 512)
    do_mla = _mm("branch_mla_dx", "nt", dpm, g_bm, n_tok, hd, d, lb=l_bm, tk=tk_b)
    do_fox = _mm("branch_fox_dx", "nt", dpf, g_bf, n_tok, hd, d, lb=l_bm, tk=tk_b)
    gw_bm = _mm("branch_mla_dw", "tn", o_mla, dpm, hd, d, n_tok, lo=l_bm, tn=n_bm, out_dtype=BF16)
    gw_bf = _mm("branch_fox_dw", "tn", o_fox, dpf, hd, d, n_tok, lo=l_bm, tn=n_bm, out_dtype=BF16)

    pair_reduce("mix", ["w_out", "w_branch_mla", "w_branch_fox"], [gw_out.reshape(N_CHIPS, d // N_CHIPS, d), gw_bm, gw_bf])
    delta_mla, dob_mla = _attn_delta("mla_delta", o_mla, do_mla, n_tok)
    (dq_nope, drq, dk_nope, drk_g, dv_mla), got = _attn_bwd(
        "mla_bwd", mla_q, mla_k, mla_v, dob_mla, lse_mla, delta_mla, None, CHUNK, mla_scale, n_seq, seq, t_attn, BF16,
        side=_scatter_side([rs_parts[nm] for nm in ("w_out", "w_branch_mla", "w_branch_fox")]))
    rs_landed.update(zip(("w_out", "w_branch_mla", "w_branch_fox"), got))
    delta_fox, dob_fox = _attn_delta("fox_delta", o_fox, do_fox, n_tok)
    (dfq, dfk, dfv, dc_q, dc_k), _ = _attn_bwd("fox_bwd", fox_q, fox_k, fox_v, dob_fox, lse_fox, delta_fox, fox_bias, 1,
                                               fox_scale, n_seq, seq, t_attn, BF16)
    dc_k8 = jnp.transpose(dc_k, (0, 2, 4, 1, 3)).reshape(n_tok, HEADS)
    dc128 = dc_q[0] + dc_q[1] + jnp.pad(dc_k8, ((0, 0), (ROPE, LANES - ROPE - HEADS)))
    dlogf = _seq_cumsum("forget_cumsum_bwd", dc128, 0, n_seq, seq, True)

    def mla_pack(r, v):
        dqn_v, drq_v, dkn_v, dv_v, drk_a, drk_b, c1, s1 = r
        c8, s8 = jnp.tile(c1, (1, HEADS)), jnp.tile(s1, (1, HEADS))
        drk_v = drk_a + drk_b
        return [jnp.concatenate([dqn_v, drq_v * c8, drq_v * s8], axis=1), jnp.concatenate([dkn_v, dv_v], axis=1),
                drk_v * c1, drk_v * s1], []

    dq_p, dkv_p, dkx, dkr = _rows(
        "mla_rope_bwd", mla_pack,
        [(dq_nope, hd, 0), (drq, hd, 0), (dk_nope, hd, 0), (dv_mla, hd, 0), (drk_g[0], LANES, 0), (drk_g[1], LANES, 0),
         (cs, LANES, 0), (sn, LANES, 0)],
        [], [(3 * hd, BF16), (2 * hd, BF16), (LANES, F32), (LANES, F32)], [], n_tok)
    dqn = _mm("q_up_dx", "nt", dq_p, wuq_p, n_tok, QL, 3 * hd)
    gw_uq_p = _mm("q_up_dw", "tn", qn, dq_p, QL, 3 * hd, n_tok, out_dtype=BF16)
    dkvn = _mm("kv_up_dx", "nt", dkv_p, wukv_p, n_tok, KVL, 2 * hd)
    gw_ukv_p = _mm("kv_up_dw", "tn", kvn, dkv_p, KVL, 2 * hd, n_tok, out_dtype=BF16)

    def lat_bwd(r, v):
        ql, kvl, dqn_v, dkvn_v, dkx_v, dkr_v, zblk, dlf = r
        dql, dgq = _rms_bwd(ql, v[0], dqn_v)
        dkvl, dgkv = _rms_bwd(kvl, v[1], dkvn_v)
        dfl = dlf * _sigmoid(-(zblk + v[2]))
        return [jnp.concatenate([dql, dkvl, dkx_v, dkr_v + dfl], axis=1)], [dgq, dgkv, _colsum(dfl)]

    dlat, dg_q, dg_kv, dbf = _rows(
        "latent_bwd", lat_bwd,
        [(proj, QL, c_lat // QL), (proj, KVL, (c_lat + QL) // KVL), (dqn, QL, 0), (dkvn, KVL, 0), (dkx, LANES, 0),
         (dkr, LANES, 0), (proj, LANES, c_kr // LANES), (dlogf, LANES, 0)],
        [row(q_a_norm), row(kv_a_norm), bf_vec], [(QL + KVL + 2 * LANES, BF16)], [(1, QL), (1, KVL), (1, LANES)], n_tok)
    dproj = [dz, dlat, dfq, dfk, dfv]
    gw_in_p = _mm_parts("proj_in_dw", "tn", h, dproj, d, n_p, n_tok, tk=1024, out_dtype=BF16)

    f32 = lambda a: a.astype(F32)
    kr_blk = gw_in_p[:, c_kr:c_kr + LANES]
    d_kpe = (f32(gw_in_p[:, c_kx:c_kx + ROPE]) + _unrot_cols(f32(kr_blk[:, :ROPE]))).astype(BF16)
    in_pieces = [(o_q, gw_in_p, c_lat, QL + KVL), (o_kpe, d_kpe, 0, ROPE), (o_f, gw_in_p, n_pa, 3 * hd),
                 (o_fl, kr_blk, ROPE, HEADS), (o_g, gw_in_p, 0, 2 * d)]
    gc_in = []
    for j in range(N_CHIPS):
        lo, hi, cols = j * n_in_shard, (j + 1) * n_in_shard, []
        for first, arr, at, width in in_pieces:
            a, b = max(lo, first), min(hi, first + width)
            if a < b:
                cols.append(arr[:, at + a - first:at + b - first])
        cols.append(jnp.zeros((d, in_pad - n_in_shard), BF16))
        gc_in.append(jnp.concatenate(cols, axis=1))
    gc_in = jnp.stack(gc_in)
    uq_parts = [gw_uq_p[:, i * hd:(i + 1) * hd].reshape(QL, HEADS, LANES) for i in range(3)]
    d_pe = (f32(uq_parts[1][:, :, :ROPE]) + _unrot_cols(f32(uq_parts[2][:, :, :ROPE]))).astype(BF16)
    gc_uq = _chunks(jnp.concatenate([uq_parts[0], d_pe], axis=2).reshape(QL, HEADS * (NOPE + ROPE)), w_uq.shape[1])
    gc_ukv = _chunks(jnp.concatenate([gw_ukv_p[:, :hd].reshape(KVL, HEADS, NOPE), gw_ukv_p[:, hd:].reshape(KVL, HEADS, VDIM)],
                                     axis=2).reshape(KVL, HEADS * (NOPE + VDIM)), w_ukv.shape[1])
    grads = [gc_in, gc_uq, gc_ukv]

    late = ["w_in", "w_uq", "w_ukv"]
    pair_reduce("late", late, grads)
    dh, got = _mm_parts("proj_in_dx", "nt", dproj, win_p, n_tok, d, n_p, side=_scatter_side([rs_parts[nm] for nm in late]))
    rs_landed.update(zip(late, got))

    def first_bwd(r, v):
        dxa, dg1 = _rms_bwd(r[0], v[0], r[1])
        return [r[2] + dxa], [dg1]

    grad_x, dg_pre_mix = _rows("pre_mix_bwd", first_bwd, [(x2, d, 0), (dh, d, 0), (dx1, d, 0)], [row(pre_mix_norm)],
                               [(d, F32)], [(1, d)], n_tok)
    big = list(rs_parts)
    chip1 = jnp.reshape(chip, (1,)).astype(jnp.int32)
    halves = [_chip_sum("rs_chip_sum_" + nm, rs_parts[nm], rs_landed[nm], chip1) for nm in big]
    other = _pair_swap(halves)
    g_halves = dict(zip(big, zip(halves, other)))

    conv_acc_l = from_il(conv_acc)
    pieces = [dg_pre_mix, dg_q, dg_kv, dbf, dbg_m, dbg_f, dg_post_mix, dg_pre_ffn, conv_acc_l[3:4], dg_post_ffn,
              conv_acc_l[0:1], conv_acc_l[1:2], conv_acc_l[2:3], sq_sum]
    sizes = [p.shape[1] for p in pieces]
    flat = jnp.concatenate(pieces, axis=1)
    n_rows = -(-flat.shape[1] // (8 * LANES)) * 8
    flat = _pad_cols(flat, n_rows * LANES).reshape(n_rows, LANES)
    slots = lax.dynamic_update_slice(_gather_small(flat), flat[None], (2 * chip + cc, 0, 0))
    total = _sum_slots("small_sum", slots).reshape(1, n_rows * LANES)
    offs = [sum(sizes[:i]) for i in range(len(sizes))]
    tot = [total[0, o:o + s] for o, s in zip(offs, sizes)]
    loss = 0.5 * tot[13][0] / d
    g_small = {"pre_mix_norm": tot[0], "q_a_norm": tot[1], "kv_a_norm": tot[2], "b_forget": tot[3][ROPE:ROPE + HEADS],
               "b_gate": jnp.concatenate([tot[4], tot[5]]), "post_mix_norm": tot[6], "pre_ffn_norm": tot[7],
               "conv_b": tot[8], "post_ffn_norm": tot[9]}
    gcw_full = jnp.stack([tot[10], tot[11], tot[12]])
    g_conv_w = lax.dynamic_slice(gcw_full, (0, chip * n_up), (3, n_up))

    given = dict(pre_mix_norm=(pre_mix_norm, m_pre_mix_norm, v_pre_mix_norm), w_in=(w_in, m_w_in, v_w_in),
                 q_a_norm=(q_a_norm, m_q_a_norm, v_q_a_norm), w_uq=(w_uq, m_w_uq, v_w_uq),
                 kv_a_norm=(kv_a_norm, m_kv_a_norm, v_kv_a_norm), w_ukv=(w_ukv, m_w_ukv, v_w_ukv),
                 b_forget=(b_forget, m_b_forget, v_b_forget), b_gate=(b_gate, m_b_gate, v_b_gate),
                 w_branch_mla=(w_branch_mla, m_w_branch_mla, v_w_branch_mla),
                 w_branch_fox=(w_branch_fox, m_w_branch_fox, v_w_branch_fox), w_out=(w_out, m_w_out, v_w_out),
                 post_mix_norm=(post_mix_norm, m_post_mix_norm, v_post_mix_norm),
                 pre_ffn_norm=(pre_ffn_norm, m_pre_ffn_norm, v_pre_ffn_norm), w_up=(w_up, m_w_up, v_w_up),
                 conv_w=(conv_w, m_conv_w, v_conv_w), conv_b=(conv_b, m_conv_b, v_conv_b),
                 w_down=(w_down, m_w_down, v_w_down), post_ffn_norm=(post_ffn_norm, m_post_ffn_norm, v_post_ffn_norm))
    order = list(given)
    grad, delta, new_m, new_v = {}, {}, {}, {}
    for nm in big:
        mine, theirs = g_halves[nm]
        if nm == "w_in":
            full = jnp.concatenate([jnp.where(cc == 0, mine, theirs), jnp.where(cc == 0, theirs, mine)], axis=0)
            grad[nm] = full[:, :n_in_shard]
            tr_out = _adamw("adamw_" + nm, *[jnp.transpose(a) for a in (given[nm][0], grad[nm], given[nm][1], given[nm][2])])
            delta[nm], new_m[nm], new_v[nm] = [jnp.transpose(a) for a in tr_out]
            continue
        grad[nm], delta[nm], new_m[nm], new_v[nm] = _adamw_halves("adamw_" + nm, given[nm][0], mine, theirs, given[nm][1],
                                                                  given[nm][2], core)
    grad["conv_w"] = g_conv_w
    delta["conv_w"], new_m["conv_w"], new_v["conv_w"] = _adamw("adamw_conv_w", conv_w, g_conv_w, m_conv_w, v_conv_w)
    small = list(g_small)
    padded = [-(-g_small[nm].shape[0] // LANES) * LANES for nm in small]
    s_rows = -(-sum(padded) // (8 * LANES)) * 8

    def pack(vals):
        cat = jnp.concatenate([jnp.pad(a, (0, p - a.shape[0])) for a, p in zip(vals, padded)])
        return jnp.pad(cat, (0, s_rows * LANES - cat.shape[0])).reshape(s_rows, LANES)

    packed = _adamw("adamw_small", pack([given[nm][0] for nm in small]), pack([g_small[nm] for nm in small]),
                    pack([given[nm][1] for nm in small]), pack([given[nm][2] for nm in small]))
    s_offs = [sum(padded[:i]) for i in range(len(small))]
    for nm, o in zip(small, s_offs):
        n_el = g_small[nm].shape[0]
        grad[nm] = g_small[nm]
        delta[nm], new_m[nm], new_v[nm] = [p.reshape(-1)[o:o + n_el] for p in packed]
    return (loss, grad_x.reshape(n_seq, seq, d), *[grad[nm] for nm in order], *[delta[nm] for nm in order],
            *[new_m[nm] for nm in order], *[new_v[nm] for nm in order])
```

```python
import functools
import math

import jax
import jax.numpy as jnp
from jax import lax
from jax.experimental import pallas as pl
from jax.experimental.pallas import tpu as pltpu

F32, BF16 = jnp.float32, jnp.bfloat16
MESH = pl.DeviceIdType.MESH

HEADS = 8
NOPE, ROPE, VDIM = 128, 64, 128
QL, KVL = 512, 256
FDIM = 128
CHUNK = 64
ROPE_THETA = 10000.0
EPS = 1e-6
NEG_INF = -1e30
ADAM_LR, ADAM_B1, ADAM_B2, ADAM_EPS, ADAM_WD, ADAM_STEP = 0.001, 0.9, 0.999, 1e-08, 0.01, 10

VMEM_LIMIT_BYTES = 52 * 1024 * 1024
LANES = 128
N_CHIPS = 4


def _params(sem):
    return pltpu.CompilerParams(dimension_semantics=sem, vmem_limit_bytes=VMEM_LIMIT_BYTES)


def _tile(n, target, mult):
    if n <= target:
        return n
    t = (target // mult) * mult
    while t >= mult:
        if n % t == 0:
            return t
        t -= mult
    raise ValueError(f"no tile for {n} (target {target}, multiple of {mult})")


class _Plain:
    def __init__(self, perm=None):
        self.perm = perm

    def spec(self, tr, tc, rc):
        perm = self.perm

        def imap(i, j, k):
            r, c = rc(i, j, k)
            return (r, perm(c) if perm is not None else c)

        return pl.BlockSpec((tr, tc), imap)

    def shape(self, rows, cols):
        return (rows, cols)


class _Chunked:
    def __init__(self, n):
        self.n = n

    def spec(self, tr, tc, rc):
        assert self.n % tc == 0, (self.n, tc)
        per = self.n // tc

        def imap(i, j, k):
            r, c = rc(i, j, k)
            return (c // per, r, c % per)

        return pl.BlockSpec((None, tr, tc), imap)

    def shape(self, rows, cols):
        assert cols == N_CHIPS * self.n
        return (N_CHIPS, rows, self.n)


_DIMS = {"nn": (((1,), (0,)), ((), ())), "nt": (((1,), (1,)), ((), ())), "tn": (((0,), (0,)), ((), ()))}


def _mm_single(name, mode, a, b, m, n, k, tm, tn, la, lb, lo, out_dtype, side):
    if mode == "nn":
        a_spec = la.spec(tm, k, lambda i, j, kk: (i, 0))
        b_spec = lb.spec(k, tn, lambda i, j, kk: (0, j))
    elif mode == "nt":
        a_spec = la.spec(tm, k, lambda i, j, kk: (i, 0))
        b_spec = lb.spec(tn, k, lambda i, j, kk: (j, 0))
    else:
        a_spec = la.spec(k, tm, lambda i, j, kk: (0, i))
        b_spec = lb.spec(k, tn, lambda i, j, kk: (0, j))
    o_spec = lo.spec(tm, tn, lambda i, j, kk: (i, j))
    dims = _DIMS[mode]

    def body(a_ref, b_ref, o_ref):
        o_ref[...] = lax.dot_general(a_ref[...].astype(BF16), b_ref[...].astype(BF16), dims,
                                     preferred_element_type=F32).astype(o_ref.dtype)

    (out,), got = _hosted_call(body, name, (m // tm, n // tn, 1), [a_spec, b_spec], [o_spec],
                               [jax.ShapeDtypeStruct(lo.shape(m, n), out_dtype)], (a, b), side,
                               semantics=("parallel", "parallel", "arbitrary"))
    return out if side is None else (out, got)


def _mm(name, mode, a, b, m, n, k, *, tm=1024, tn=1024, tk=2048, la=None, lb=None, lo=None, out_dtype=F32, side=None):
    la, lb, lo = la or _Plain(), lb or _Plain(), lo or _Plain()
    tm, tn, tk = _tile(m, tm, 128), _tile(n, tn, 128), _tile(k, tk, 128)
    nk = k // tk
    if nk == 1:
        return _mm_single(name, mode, a, b, m, n, k, tm, tn, la, lb, lo, out_dtype, side)
    if mode == "nn":
        a_spec = la.spec(tm, tk, lambda i, j, kk: (i, kk))
        b_spec = lb.spec(tk, tn, lambda i, j, kk: (kk, j))
    elif mode == "nt":
        a_spec = la.spec(tm, tk, lambda i, j, kk: (i, kk))
        b_spec = lb.spec(tn, tk, lambda i, j, kk: (j, kk))
    else:
        a_spec = la.spec(tk, tm, lambda i, j, kk: (kk, i))
        b_spec = lb.spec(tk, tn, lambda i, j, kk: (kk, j))
    o_spec = lo.spec(tm, tn, lambda i, j, kk: (i, j))
    dims = _DIMS[mode]

    def body(a_ref, b_ref, o_ref, acc_ref):
        kk = pl.program_id(2)

        @pl.when(kk == 0)
        def _():
            acc_ref[...] = jnp.zeros_like(acc_ref)

        acc_ref[...] += lax.dot_general(a_ref[...].astype(BF16), b_ref[...].astype(BF16), dims,
                                        preferred_element_type=F32)

        @pl.when(kk == nk - 1)
        def _():
            o_ref[...] = acc_ref[...].astype(o_ref.dtype)

    (out,), got = _hosted_call(body, name, (m // tm, n // tn, nk), [a_spec, b_spec], [o_spec],
                               [jax.ShapeDtypeStruct(lo.shape(m, n), out_dtype)], (a, b), side,
                               semantics=("parallel", "parallel", "arbitrary"), scratch=[pltpu.VMEM((tm, tn), F32)])
    return out if side is None else (out, got)


def _mm_parts(name, mode, a, b, m, n, k, *, part=1024, tm=1024, tn=1024, tk=2048, out_dtype=F32, side=None):
    parts = b if mode == "tn" else a
    widths = [p.shape[1] for p in parts]
    assert all(w % part == 0 for w in widths) and sum(widths) == (n if mode == "tn" else k)
    offs = [sum(widths[:i]) // part for i in range(len(widths))]
    nblk = [w // part for w in widths]
    if mode == "tn":
        tn, tk = part, _tile(k, tk, 128)
    else:
        tk, tn = part, _tile(n, tn, 128)
    tm = _tile(m, tm, 128)
    nk = k // tk
    grid = (m // tm, n // tn, nk)
    np_ = len(parts)

    def inside(idx, p):
        return jnp.logical_and(idx >= offs[p], idx < offs[p] + nblk[p])

    def part_spec(p):
        if mode == "tn":
            def imap(i, j, kk):
                on = inside(j, p)
                return (jnp.where(on, kk, 0), jnp.clip(j - offs[p], 0, nblk[p] - 1))
            return pl.BlockSpec((tk, tn), imap)

        def imap(i, j, kk):
            return (i, jnp.clip(kk - offs[p], 0, nblk[p] - 1))
        return pl.BlockSpec((tm, tk), imap)

    if mode == "tn":
        in_specs = [pl.BlockSpec((tk, tm), lambda i, j, kk: (kk, i))] + [part_spec(p) for p in range(np_)]
        args = [a] + list(parts)
    else:
        in_specs = [part_spec(p) for p in range(np_)] + [pl.BlockSpec((tn, tk), lambda i, j, kk: (j, kk))]
        args = list(parts) + [b]
    dims = _DIMS[mode]

    def body(*refs):
        o_ref, acc_ref = refs[-2], refs[-1]
        j, kk = pl.program_id(1), pl.program_id(2)

        @pl.when(kk == 0)
        def _():
            acc_ref[...] = jnp.zeros_like(acc_ref)

        for p in range(np_):
            @pl.when(inside(j if mode == "tn" else kk, p))
            def _(p=p):
                lhs, rhs = (refs[0], refs[1 + p]) if mode == "tn" else (refs[p], refs[np_])
                acc_ref[...] += lax.dot_general(lhs[...].astype(BF16), rhs[...].astype(BF16), dims, preferred_element_type=F32)

        @pl.when(kk == nk - 1)
        def _():
            o_ref[...] = acc_ref[...].astype(o_ref.dtype)

    (out,), got = _hosted_call(body, name, grid, in_specs, [pl.BlockSpec((tm, tn), lambda i, j, kk: (i, j))],
                               [jax.ShapeDtypeStruct((m, n), out_dtype)], args, side,
                               semantics=("parallel", "parallel", "arbitrary"), scratch=[pltpu.VMEM((tm, tn), F32)])
    return out if side is None else (out, got)


def _rows(name, fn, rows_in, vecs_in, rows_out, accs_out, n_rows, tr=256):
    tr = _tile(n_rows, tr, 16)
    nr, nv, no = len(rows_in), len(vecs_in), len(rows_out)

    def body(*refs):
        ins, vecs = refs[:nr], refs[nr:nr + nv]
        outs, accs = refs[nr + nv:nr + nv + no], refs[nr + nv + no:]
        ro, ac = fn([r[...] for r in ins], [v[...] for v in vecs])
        for o_ref, val in zip(outs, ro):
            o_ref[...] = val.astype(o_ref.dtype)
        if accs:
            @pl.when(pl.program_id(0) == 0)
            def _():
                for a_ref in accs:
                    a_ref[...] = jnp.zeros_like(a_ref)

            for a_ref, val in zip(accs, ac):
                a_ref[...] += val

    in_specs = [pl.BlockSpec((tr, cols), functools.partial(lambda i, cb: (i, cb), cb=cb)) for _, cols, cb in rows_in]
    in_specs += [pl.BlockSpec(v.shape, lambda i: (0, 0)) for v in vecs_in]
    out_specs = [pl.BlockSpec((tr, cols), lambda i: (i, 0)) for cols, _ in rows_out]
    out_specs += [pl.BlockSpec((r, cols), lambda i: (0, 0)) for r, cols in accs_out]
    out_shape = [jax.ShapeDtypeStruct((n_rows, cols), dt) for cols, dt in rows_out]
    out_shape += [jax.ShapeDtypeStruct((r, cols), F32) for r, cols in accs_out]
    res = pl.pallas_call(
        body, name=name, grid=(n_rows // tr,), in_specs=in_specs, out_specs=out_specs, out_shape=out_shape,
        compiler_params=_params(("arbitrary",)),
    )(*[a for a, _, _ in rows_in], *vecs_in)
    return res


def _colsum(v):
    return jnp.sum(v, axis=0, keepdims=True)


def _rstd(x):
    return lax.rsqrt(jnp.mean(x * x, axis=-1, keepdims=True) + EPS)


def _rms_bwd(x, g, dy):
    r = _rstd(x)
    xh = x * r
    dxh = dy * g
    dx = r * (dxh - xh * jnp.mean(dxh * xh, axis=-1, keepdims=True))
    return dx, _colsum(dy * xh)


def _sigmoid(z):
    return 1.0 / (1.0 + jnp.exp(-z))


_GELU_K = math.sqrt(2.0 / math.pi)


def _gelu_parts(g):
    t = jnp.tanh(_GELU_K * (g + 0.044715 * g * g * g))
    gel = 0.5 * g * (1.0 + t)
    dgel = 0.5 * (1.0 + t) + 0.5 * g * (1.0 - t * t) * (_GELU_K * (1.0 + 3.0 * 0.044715 * g * g))
    return gel, dgel


def _diag_visible(t, unit):
    rows = lax.broadcasted_iota(jnp.int32, (t, t), 0)
    cols = lax.broadcasted_iota(jnp.int32, (t, t), 1)
    if unit > 1:
        sh = int(math.log2(unit))
        assert 1 << sh == unit and t % unit == 0
        rows, cols = jnp.right_shift(rows, sh), jnp.right_shift(cols, sh)
    return cols <= rows


def _lane_pick(tile, lane):
    idx = lax.broadcasted_iota(jnp.int32, tile.shape, 1)
    return jnp.sum(jnp.where(idx == lane, tile, 0.0), axis=1, keepdims=True)


def _lane_put(tile, lane, col):
    idx = lax.broadcasted_iota(jnp.int32, tile.shape, 1)
    return jnp.where(idx == lane, col, tile)


def _head_cat(refs, shared, rows, h):
    hs = slice(h * LANES, (h + 1) * LANES)
    vals = [(r[rows, :] if sh else r[rows, hs]).astype(BF16) for r, sh in zip(refs, shared)]
    return vals[0] if len(vals) == 1 else jnp.concatenate(vals, axis=1)


def _blk_rows(i, t):
    return pl.ds(pl.multiple_of(i * t, t), t)


def _piece_specs(pieces, rows, row_idx):
    return [pl.BlockSpec((rows, LANES if sh else HEADS * LANES), functools.partial(lambda b, i, cb: (row_idx(b, i), cb), cb=cb))
            for _, cb, sh in pieces]


def _attn_fwd(name, qp, kp, vp, bias, unit, scale, n_seq, seq, t, side=None):
    nb = seq // t
    n_tok = n_seq * seq
    nq, nk_p = len(qp), len(kp)
    q_sh, k_sh = [p[2] for p in qp], [p[2] for p in kp]
    nbias = 2 if bias is not None else 0

    def body(*refs):
        q_refs, k_refs = refs[:nq], refs[nq:nq + nk_p]
        v_ref = refs[nq + nk_p]
        bias_refs = refs[nq + nk_p + 1:nq + nk_p + 1 + nbias]
        o_ref, lse_ref = refs[nq + nk_p + 1 + nbias:]
        qi = pl.program_id(1)

        def tile(kq):
            keys = slice(0, (kq + 1) * t)
            lse_tile = jnp.zeros((t, LANES), F32)
            for h in range(HEADS):
                hs = slice(h * LANES, (h + 1) * LANES)
                s = lax.dot_general(_head_cat(q_refs, q_sh, slice(None), h), _head_cat(k_refs, k_sh, keys, h), _DIMS["nt"],
                                    preferred_element_type=F32) * scale
                if bias is not None:
                    ck = [bias_refs[1][kb, h:h + 1, :] for kb in range(kq + 1)]
                    s = s + _lane_pick(bias_refs[0][...], ROPE + h) - (ck[0] if kq == 0 else jnp.concatenate(ck, axis=1))
                last = jnp.where(_diag_visible(t, unit), s[:, kq * t:], NEG_INF)
                s = last if kq == 0 else jnp.concatenate([s[:, :kq * t], last], axis=1)
                m = jnp.max(s, axis=1, keepdims=True)
                p = jnp.exp(s - m)
                l = jnp.sum(p, axis=1, keepdims=True)
                o_ref[:, hs] = jnp.dot(p.astype(BF16), v_ref[keys, hs].astype(BF16), preferred_element_type=F32) / l
                lse_tile = _lane_put(lse_tile, h, m + jnp.log(l))
            lse_ref[...] = lse_tile

        for kq in range(nb):
            pl.when(qi == kq)(functools.partial(tile, kq))

    tile_row = lambda b, i: b * nb + i
    seq_row = lambda b, i: b
    lane_tile = pl.BlockSpec((t, LANES), lambda b, i: (b * nb + i, 0))
    in_specs = _piece_specs(qp, t, tile_row) + _piece_specs(kp, seq, seq_row) + _piece_specs([vp + (False,)], seq, seq_row)
    args = [p[0] for p in qp] + [p[0] for p in kp] + [vp[0]]
    if bias is not None:
        in_specs += [lane_tile, pl.BlockSpec((None, nb, HEADS, t), lambda b, i: (b, 0, 0, 0))]
        args += list(bias)
    return _hosted_call(
        body, name, (n_seq, nb), in_specs,
        [pl.BlockSpec((t, HEADS * LANES), lambda b, i: (b * nb + i, 0)), lane_tile],
        [jax.ShapeDtypeStruct((n_tok, HEADS * LANES), F32), jax.ShapeDtypeStruct((n_tok, LANES), F32)], args, side)


HEAD_GROUPS = 2


def _attn_delta(name, o, do, n_tok):
    def fn(r, v):
        o_v, do_v = r
        tile = jnp.zeros((o_v.shape[0], LANES), F32)
        for h in range(HEADS):
            hs = slice(h * LANES, (h + 1) * LANES)
            tile = _lane_put(tile, h, jnp.sum(do_v[:, hs] * o_v[:, hs], axis=1, keepdims=True))
        return [tile, do_v], []

    return _rows(name, fn, [(o, HEADS * LANES, 0), (do, HEADS * LANES, 0)], [], [(LANES, F32), (HEADS * LANES, BF16)], [], n_tok)


def _attn_bwd(name, qp, kp, vp, dob, lse, delta, bias, unit, scale, n_seq, seq, t, grad_dtype, side=None):
    nb = seq // t
    n_tok = n_seq * seq
    ng = HEAD_GROUPS
    hg = HEADS // ng
    gw = hg * LANES
    nq, nk_p = len(qp), len(kp)
    q_sh, k_sh = [p[2] for p in qp], [p[2] for p in kp]
    assert not any(q_sh) and nq == nk_p
    nbias = 2 if bias is not None else 0
    n_in = nq + nk_p + 4 + nbias
    n_out = nq + nk_p + 1 + nbias

    def body(*refs):
        q_refs, k_refs = refs[:nq], refs[nq:nq + nk_p]
        v_ref, dob_ref, lse_ref, delta_ref = refs[nq + nk_p:nq + nk_p + 4]
        bias_refs = refs[nq + nk_p + 4:n_in]
        dq_refs, dk_refs = refs[n_in:n_in + nq], refs[n_in + nq:n_in + nq + nk_p]
        dv_ref = refs[n_in + nq + nk_p]
        dq_s, dcq_s = refs[n_in + n_out:]
        g, ki = pl.program_id(1), pl.program_id(2)

        @pl.when(ki == 0)
        def _():
            dq_s[...] = jnp.zeros_like(dq_s)
            dcq_s[...] = jnp.zeros_like(dcq_s)

        shared_acc = [jnp.zeros((t, LANES), F32) for _ in range(nk_p)]
        for hl in range(hg):
            h = g * hg + hl
            hs = slice(hl * LANES, (hl + 1) * LANES)
            k = _head_cat(k_refs, k_sh, slice(None), hl)
            v = v_ref[:, hs].astype(BF16)
            ck = bias_refs[1][pl.ds(h, 1), :] if bias is not None else None

            def block(qb, carry, diag, h=h, hl=hl, hs=hs, k=k, v=v, ck=ck):
                dk_acc, dv_acc, dc_acc = carry
                rows = _blk_rows(qb, t)
                q = _head_cat(q_refs, q_sh, rows, hl)
                s = lax.dot_general(q, k, _DIMS["nt"], preferred_element_type=F32) * scale
                if bias is not None:
                    s = s + _lane_pick(bias_refs[0][rows, :], ROPE + h) - ck
                if diag:
                    s = jnp.where(_diag_visible(t, unit), s, NEG_INF)
                p = jnp.exp(s - _lane_pick(lse_ref[rows, :], h))
                do_b = dob_ref[rows, hs]
                dp = lax.dot_general(do_b, v, _DIMS["nt"], preferred_element_type=F32)
                ds = p * (dp - _lane_pick(delta_ref[rows, :], h))
                ds_b = ds.astype(BF16)
                dq_blk = jnp.dot(ds_b, k, preferred_element_type=F32)
                for n_p in range(nq):
                    dq_s[rows, n_p * gw + hl * LANES:n_p * gw + (hl + 1) * LANES] += dq_blk[:, n_p * LANES:(n_p + 1) * LANES]
                if bias is not None:
                    lane = lax.broadcasted_iota(jnp.int32, (t, LANES), 1)
                    dcq_s[rows, :] += jnp.where(lane == ROPE + h, jnp.sum(ds, axis=1, keepdims=True), 0.0)
                return (dk_acc + lax.dot_general(ds_b, q, _DIMS["tn"], preferred_element_type=F32),
                        dv_acc + lax.dot_general(p.astype(BF16), do_b, _DIMS["tn"], preferred_element_type=F32),
                        dc_acc - jnp.sum(ds, axis=0, keepdims=True))

            init = (jnp.zeros((t, nk_p * LANES), F32), jnp.zeros((t, LANES), F32), jnp.zeros((1, t), F32))
            carry = block(ki, init, True)
            dk_acc, dv_acc, dc_acc = lax.fori_loop(ki + 1, nb, lambda qb, c: block(qb, c, False), carry)
            for n_p in range(nk_p):
                part = dk_acc[:, n_p * LANES:(n_p + 1) * LANES] * scale
                if k_sh[n_p]:
                    shared_acc[n_p] = shared_acc[n_p] + part
                else:
                    dk_refs[n_p][:, hs] = part.astype(grad_dtype)
            dv_ref[:, hs] = dv_acc.astype(grad_dtype)
            if bias is not None:
                refs[n_in + n_out - 1][hl:hl + 1, :] = dc_acc
        for n_p in range(nk_p):
            if k_sh[n_p]:
                dk_refs[n_p][...] = shared_acc[n_p]

        @pl.when(ki == nb - 1)
        def _():
            for n_p in range(nq):
                dq_refs[n_p][...] = (dq_s[:, n_p * gw:(n_p + 1) * gw] * scale).astype(grad_dtype)
            if bias is not None:
                refs[n_in + n_out - 2][...] = dcq_s[...]

    def spec(rows, row_idx, cb, shared):
        if shared:
            return pl.BlockSpec((rows, LANES), lambda b, g, i: (row_idx(b, i), cb))
        return pl.BlockSpec((rows, gw), lambda b, g, i: (row_idx(b, i), cb * ng + g))

    tile_row = lambda b, i: b * nb + i
    seq_row = lambda b, i: b
    lane_seq = pl.BlockSpec((seq, LANES), lambda b, g, i: (b, 0))
    in_specs = [spec(seq, seq_row, cb, sh) for _, cb, sh in qp] + [spec(t, tile_row, cb, sh) for _, cb, sh in kp]
    in_specs += [spec(t, tile_row, vp[1], False), spec(seq, seq_row, 0, False), lane_seq, lane_seq]
    args = [p[0] for p in qp] + [p[0] for p in kp] + [vp[0], dob, lse, delta]
    if bias is not None:
        in_specs += [lane_seq, pl.BlockSpec((None, None, HEADS, t), lambda b, g, i: (b, i, 0, 0))]
        args += list(bias)
    group_tile = pl.BlockSpec((None, t, LANES), lambda b, g, i: (g, b * nb + i, 0))
    out_specs = [spec(seq, seq_row, 0, False)] * nq
    out_specs += [group_tile if sh else spec(t, tile_row, 0, False) for sh in k_sh] + [spec(t, tile_row, 0, False)]
    head_shape = jax.ShapeDtypeStruct((n_tok, HEADS * LANES), grad_dtype)
    out_shape = [head_shape] * nq + [jax.ShapeDtypeStruct((ng, n_tok, LANES), F32) if sh else head_shape for sh in k_sh]
    out_shape.append(head_shape)
    if bias is not None:
        out_specs += [pl.BlockSpec((None, seq, LANES), lambda b, g, i: (g, b, 0)),
                      pl.BlockSpec((None, None, None, hg, t), lambda b, g, i: (b, g, i, 0, 0))]
        out_shape += [jax.ShapeDtypeStruct((ng, n_tok, LANES), F32), jax.ShapeDtypeStruct((n_seq, ng, nb, hg, t), F32)]
    return _hosted_call(body, name, (n_seq, ng, nb), in_specs, out_specs, out_shape, args, side,
                        semantics=("parallel", "arbitrary", "arbitrary"),
                        scratch=[pltpu.VMEM((seq, nq * gw), F32), pltpu.VMEM((seq, LANES), F32)])


def _seq_cumsum(name, x, col_block, n_seq, seq, reverse, pre=None, vec=None):
    t = _tile(seq, 256, 128)
    nb = seq // t

    def body(*refs):
        x_ref = refs[0]
        vec_ref = refs[1] if vec is not None else None
        o_ref, carry = refs[-2], refs[-1]

        @pl.when(pl.program_id(1) == 0)
        def _():
            carry[...] = jnp.zeros_like(carry)

        v = x_ref[...]
        if pre is not None:
            v = pre(v, vec_ref[...])
        r = lax.broadcasted_iota(jnp.int32, (t, t), 0)
        c = lax.broadcasted_iota(jnp.int32, (t, t), 1)
        tri = jnp.where((c >= r) if reverse else (c <= r), 1.0, 0.0).astype(BF16)
        hi = v.astype(BF16)
        mid = (v - hi.astype(F32)).astype(BF16)
        lo = (v - hi.astype(F32) - mid.astype(F32)).astype(BF16)
        acc = jnp.dot(tri, hi, preferred_element_type=F32)
        acc += jnp.dot(tri, mid, preferred_element_type=F32)
        acc += jnp.dot(tri, lo, preferred_element_type=F32)
        o_ref[...] = acc + carry[...]
        carry[...] += _colsum(v)

    blk = (lambda b, i: (b * nb + nb - 1 - i)) if reverse else (lambda b, i: (b * nb + i))
    in_specs = [pl.BlockSpec((t, LANES), lambda b, i: (blk(b, i), col_block))]
    args = [x]
    if vec is not None:
        in_specs.append(pl.BlockSpec(vec.shape, lambda b, i: (0, 0)))
        args.append(vec)
    return pl.pallas_call(
        body, name=name, grid=(n_seq, nb), in_specs=in_specs,
        out_specs=pl.BlockSpec((t, LANES), lambda b, i: (blk(b, i), 0)),
        out_shape=jax.ShapeDtypeStruct((n_seq * seq, LANES), F32),
        scratch_shapes=[pltpu.VMEM((1, LANES), F32)],
        compiler_params=_params(("arbitrary", "arbitrary")),
    )(*args)


def _log_sigmoid(z):
    return -(jnp.maximum(-z, 0.0) + jnp.log(1.0 + jnp.exp(-jnp.abs(z))))


def _shift_down(u, prev_ref, n):
    out = pltpu.roll(u, n, 0)
    row = lax.broadcasted_iota(jnp.int32, u.shape, 0)
    for r in range(n):
        out = jnp.where(row == r, prev_ref[8 - n + r:8 - n + r + 1, :], out)
    return out


def _shift_up(u, next_ref, n):
    ts = u.shape[0]
    out = pltpu.roll(u, ts - n, 0)
    row = lax.broadcasted_iota(jnp.int32, u.shape, 0)
    for r in range(n):
        out = jnp.where(row == ts - n + r, next_ref[r:r + 1, :], out)
    return out


def _conv_taps(u, prev_ref, w_ref, b_ref):
    s1, s2 = _shift_down(u, prev_ref, 1), _shift_down(u, prev_ref, 2)
    return (w_ref[0:1, :] * s2 + w_ref[1:2, :] * s1 + w_ref[2:3, :] * u) + b_ref[...], s1, s2


def _conv_glu_fwd(u_il, cw_il, cb_il, n_seq, seq, wt):
    n_tok, two_f = u_il.shape
    nct = two_f // (2 * wt)
    ts = _tile(seq, 256, 8)
    ns = seq // ts

    def body(u_ref, w_ref, b_ref, a_ref, carry):
        @pl.when(pl.program_id(2) == 0)
        def _():
            carry[...] = jnp.zeros_like(carry)

        u = u_ref[...].astype(F32)
        uc, _, _ = _conv_taps(u, carry, w_ref, b_ref)
        gel, _ = _gelu_parts(uc[:, :wt])
        a_ref[...] = (gel * uc[:, wt:]).astype(a_ref.dtype)
        carry[...] = u[ts - 8:, :]

    return pl.pallas_call(
        body, name="conv_glu_fwd", grid=(nct, n_seq, ns),
        in_specs=[pl.BlockSpec((ts, 2 * wt), lambda j, b, s: (b * ns + s, j)),
                  pl.BlockSpec((3, 2 * wt), lambda j, b, s: (0, j)),
                  pl.BlockSpec((1, 2 * wt), lambda j, b, s: (0, j))],
        out_specs=pl.BlockSpec((ts, wt), lambda j, b, s: (b * ns + s, j)),
        out_shape=jax.ShapeDtypeStruct((n_tok, two_f // 2), BF16),
        scratch_shapes=[pltpu.VMEM((8, 2 * wt), F32)],
        compiler_params=_params(("parallel", "arbitrary", "arbitrary")),
    )(u_il, cw_il, cb_il)


def _conv_glu_bwd_pre(u_il, da, cw_il, cb_il, n_seq, seq, wt):
    n_tok, two_f = u_il.shape
    nct = two_f // (2 * wt)
    ts = _tile(seq, 256, 8)
    ns = seq // ts

    def body(u_ref, da_ref, w_ref, b_ref, d_ref, acc_ref, carry):
        first = jnp.logical_and(pl.program_id(1) == 0, pl.program_id(2) == 0)

        @pl.when(first)
        def _():
            acc_ref[...] = jnp.zeros_like(acc_ref)

        @pl.when(pl.program_id(2) == 0)
        def _():
            carry[...] = jnp.zeros_like(carry)

        u = u_ref[...].astype(F32)
        uc, s1, s2 = _conv_taps(u, carry, w_ref, b_ref)
        gel, dgel = _gelu_parts(uc[:, :wt])
        da_v = da_ref[...].astype(F32)
        d = jnp.concatenate([da_v * uc[:, wt:] * dgel, da_v * gel], axis=1)
        d_ref[...] = d.astype(d_ref.dtype)
        acc_ref[0:1, :] += _colsum(d * s2)
        acc_ref[1:2, :] += _colsum(d * s1)
        acc_ref[2:3, :] += _colsum(d * u)
        acc_ref[3:4, :] += _colsum(d)
        carry[...] = u[ts - 8:, :]

    return pl.pallas_call(
        body, name="conv_glu_bwd_pre", grid=(nct, n_seq, ns),
        in_specs=[pl.BlockSpec((ts, 2 * wt), lambda j, b, s: (b * ns + s, j)),
                  pl.BlockSpec((ts, wt), lambda j, b, s: (b * ns + s, j)),
                  pl.BlockSpec((3, 2 * wt), lambda j, b, s: (0, j)),
                  pl.BlockSpec((1, 2 * wt), lambda j, b, s: (0, j))],
        out_specs=[pl.BlockSpec((ts, 2 * wt), lambda j, b, s: (b * ns + s, j)),
                   pl.BlockSpec((8, 2 * wt), lambda j, b, s: (0, j))],
        out_shape=[jax.ShapeDtypeStruct((n_tok, two_f), BF16), jax.ShapeDtypeStruct((8, two_f), F32)],
        scratch_shapes=[pltpu.VMEM((8, 2 * wt), F32)],
        compiler_params=_params(("parallel", "arbitrary", "arbitrary")),
    )(u_il, da, cw_il, cb_il)


def _conv_bwd_input(d_il, cw_il, n_seq, seq, wt):
    n_tok, two_f = d_il.shape
    nct = two_f // (2 * wt)
    ts = _tile(seq, 256, 8)
    ns = seq // ts

    def body(d_ref, w_ref, o_ref, carry):
        @pl.when(pl.program_id(2) == 0)
        def _():
            carry[...] = jnp.zeros_like(carry)

        d = d_ref[...].astype(F32)
        o_ref[...] = (w_ref[2:3, :] * d + w_ref[1:2, :] * _shift_up(d, carry, 1)
                      + w_ref[0:1, :] * _shift_up(d, carry, 2)).astype(o_ref.dtype)
        carry[...] = d[:8, :]

    rev = lambda j, b, s: (b * ns + ns - 1 - s, j)
    return pl.pallas_call(
        body, name="conv_bwd_input", grid=(nct, n_seq, ns),
        in_specs=[pl.BlockSpec((ts, 2 * wt), rev), pl.BlockSpec((3, 2 * wt), lambda j, b, s: (0, j))],
        out_specs=pl.BlockSpec((ts, 2 * wt), rev),
        out_shape=jax.ShapeDtypeStruct((n_tok, two_f), BF16),
        scratch_shapes=[pltpu.VMEM((8, 2 * wt), F32)],
        compiler_params=_params(("parallel", "arbitrary", "arbitrary")),
    )(d_il, cw_il)


HBM = pl.BlockSpec(memory_space=pltpu.HBM)
_CHIP_FLIPS = ((1, 0), (0, 1), (1, 1))


def _place():
    x, y, c = lax.axis_index("x"), lax.axis_index("y"), lax.axis_index("c")
    return x, y, c, 2 * x + y


def _flip(v, f):
    return 1 - v if f else v


def _half_rows(c, half):
    return pl.ds(pl.multiple_of(c * half, 16), half)


def _remote(src, dst, ssem, rsem, dev):
    return pltpu.make_async_remote_copy(src_ref=src, dst_ref=dst, send_sem=ssem, recv_sem=rsem,
                                        device_id=dev, device_id_type=MESH)


def _comm_call(name, body, ins, out_shapes, n_sems):
    return pl.pallas_call(
        body, name=name, in_specs=[HBM] * len(ins), out_specs=[HBM] * len(out_shapes),
        out_shape=[pltpu.HBM(s.shape, s.dtype) for s in out_shapes],
        scratch_shapes=[pltpu.SemaphoreType.DMA((n_sems,)), pltpu.SemaphoreType.DMA((n_sems,))],
    )(*ins)


def _all_gather_weights(shards, smalls):
    side = _gather_side(shards, smalls)
    nt = len(shards) + len(smalls)

    def body(*refs):
        for part in (side.start, side.mid, side.end):
            part(refs[:nt], refs[nt:2 * nt], *refs[2 * nt:])

    res = _comm_call("all_gather_weights", body, side.ins, side.outs, side.n_sems)
    return res[:len(shards)], res[len(shards):]


def _pair_split(name, grads):
    n = len(grads)

    def body(*refs):
        src, got = refs[:n], refs[n:2 * n]
        ssem, rsem = refs[2 * n:]
        x, y, c, _ = _place()
        cps = []
        for w in range(n):
            half = grads[w].shape[1] // 2
            cp = _remote(src[w].at[:, _half_rows(1 - c, half)], got[w], ssem.at[w], rsem.at[w], (x, y, 1 - c))
            cp.start()
            cps.append(cp)
        for cp in cps:
            cp.wait()

    outs = [jax.ShapeDtypeStruct((g.shape[0], g.shape[1] // 2, g.shape[2]), g.dtype) for g in grads]
    return _comm_call(name, body, grads, outs, n)


class _Side:
    def __init__(self, ins, outs, n_sems, start, mid, end, mid_step=None):
        self.ins, self.outs, self.n_sems = list(ins), list(outs), n_sems
        self.start, self.mid, self.end, self.mid_step = start, mid, end, mid_step
        self.aliases = {}


def _scatter_side(parts):
    n = len(parts)

    def copies(src, dst, ssem, rsem):
        x, y, c, _ = _place()
        out = []
        for w in range(n):
            for k, (fx, fy) in enumerate(_CHIP_FLIPS):
                px, py = _flip(x, fx), _flip(y, fy)
                out.append(_remote(src[w].at[2 * px + py], dst[w].at[k], ssem.at[w * 3 + k], rsem.at[w * 3 + k], (px, py, c)))
        return out

    def start(src, dst, ssem, rsem):
        for cp in copies(src, dst, ssem, rsem):
            cp.start()

    def end(src, dst, ssem, rsem):
        for cp in copies(src, dst, ssem, rsem):
            cp.wait()

    outs = [jax.ShapeDtypeStruct((3,) + p.shape[1:], p.dtype) for p in parts]
    return _Side(parts, outs, 3 * n, start, None, end)


def _gather_side(shards, smalls, mid_step=None, into=None):
    n, ns = len(shards), len(smalls)
    into = into or [(None, a.shape[0], 0) for a in shards]

    def dst_rows(w, c):
        half = shards[w].shape[0] // 2
        return pl.ds(pl.multiple_of(into[w][2] + c * half, 16), half)

    def ici(src, dst, ssem, rsem, w, k):
        x, y, c, me = _place()
        fx, fy = _CHIP_FLIPS[k]
        rows = _half_rows(c, shards[w].shape[0] // 2)
        return _remote(src[w].at[rows], dst[w].at[me, dst_rows(w, c)], ssem.at[w * 6 + k], rsem.at[w * 6 + k],
                       (_flip(x, fx), _flip(y, fy), c))

    def small(src, dst, ssem, rsem, s, k):
        x, y, c, me = _place()
        fx, fy = _CHIP_FLIPS[k]
        sem = 6 * n + 3 * s + k
        return _remote(src[n + s], dst[n + s].at[me], ssem.at[sem], rsem.at[sem], (_flip(x, fx), _flip(y, fy), c))

    def own(src, dst, ssem, rsem, i):
        x, y, c, me = _place()
        sem = 6 * n + 3 * ns + i
        if i < n:
            to = dst[i].at[me, pl.ds(into[i][2], shards[i].shape[0])]
        else:
            to = dst[i].at[me]
        return _remote(src[i], to, ssem.at[sem], rsem.at[sem], (x, y, 1 - c))

    def landed(dst, ssem, rsem, w, k, sender_c, sem_off):
        x, y, c, _ = _place()
        fx, fy = _CHIP_FLIPS[k]
        got = dst[w].at[2 * _flip(x, fx) + _flip(y, fy), dst_rows(w, sender_c)]
        return _remote(got, got, ssem.at[w * 6 + sem_off + k], rsem.at[w * 6 + sem_off + k], (x, y, 1 - c))

    def start(src, dst, ssem, rsem):
        for i in range(n + ns):
            own(src, dst, ssem, rsem, i).start()
        for s in range(ns):
            for k in range(3):
                small(src, dst, ssem, rsem, s, k).start()
        for w in range(n):
            for k in range(3):
                ici(src, dst, ssem, rsem, w, k).start()

    def mid(src, dst, ssem, rsem):
        c = lax.axis_index("c")
        for w in range(n):
            for k in range(3):
                landed(dst, ssem, rsem, w, k, c, 0).wait_recv()
                landed(dst, ssem, rsem, w, k, c, 3).start()

    def end(src, dst, ssem, rsem):
        c = lax.axis_index("c")
        for w in range(n):
            for k in range(3):
                landed(dst, ssem, rsem, w, k, 1 - c, 3).wait_recv()
        for i in range(n + ns):
            own(src, dst, ssem, rsem, i).wait()
        for s in range(ns):
            for k in range(3):
                small(src, dst, ssem, rsem, s, k).wait()
        for w in range(n):
            for k in range(3):
                ici(src, dst, ssem, rsem, w, k).wait_send()
                landed(dst, ssem, rsem, w, k, c, 3).wait_send()

    outs = [jax.ShapeDtypeStruct((N_CHIPS, rows, a.shape[1]), a.dtype) for a, (_, rows, _) in zip(shards, into)]
    outs += [jax.ShapeDtypeStruct((N_CHIPS,) + a.shape, a.dtype) for a in smalls]
    filled = [(w, arr) for w, (arr, _, _) in enumerate(into) if arr is not None]
    side = _Side(list(shards) + list(smalls) + [arr for _, arr in filled], outs, 7 * n + 4 * ns, start, mid, end, mid_step)
    side.aliases = {n + ns + i: w for i, (w, _) in enumerate(filled)}
    return side


def _host(body, n_in, n_out, side, grid):
    if side is None:
        return body
    ns_in, ns_out = len(side.ins), len(side.outs)
    n_steps = math.prod(grid)
    mid_step = side.mid_step
    if side.mid is not None and not isinstance(mid_step, int):
        mid_step = min(n_steps - 1, int(mid_step * n_steps))

    def wrapped(*refs):
        ins, s_ins = refs[:n_in], refs[n_in:n_in + ns_in]
        outs = refs[n_in + ns_in:n_in + ns_in + n_out]
        s_outs = refs[n_in + ns_in + n_out:n_in + ns_in + n_out + ns_out]
        rest = refs[n_in + ns_in + n_out + ns_out:]
        sems = rest[-2:]
        step = 0
        for axis, extent in enumerate(grid):
            step = step * extent + pl.program_id(axis)

        @pl.when(step == 0)
        def _():
            side.start(s_ins, s_outs, *sems)

        if side.mid is not None:
            @pl.when(step == mid_step)
            def _():
                side.mid(s_ins, s_outs, *sems)

        body(*ins, *outs, *rest[:-2])

        @pl.when(step == n_steps - 1)
        def _():
            side.end(s_ins, s_outs, *sems)

    return wrapped


def _hosted_call(body, name, grid, in_specs, out_specs, out_shape, args, side, semantics=("parallel", "arbitrary"),
                 scratch=()):
    n_in, n_out = len(in_specs), len(out_specs)
    kern = _host(body, n_in, n_out, side, grid)
    if side is None:
        return pl.pallas_call(kern, name=name, grid=grid, in_specs=in_specs, out_specs=out_specs, out_shape=out_shape,
                              scratch_shapes=list(scratch), compiler_params=_params(semantics))(*args), []
    res = pl.pallas_call(
        kern, name=name, grid=grid, in_specs=in_specs + [HBM] * len(side.ins), out_specs=out_specs + [HBM] * len(side.outs),
        out_shape=list(out_shape) + [pltpu.HBM(s.shape, s.dtype) for s in side.outs],
        scratch_shapes=list(scratch) + [pltpu.SemaphoreType.DMA((side.n_sems,)), pltpu.SemaphoreType.DMA((side.n_sems,))],
        input_output_aliases={n_in + i: n_out + o for i, o in side.aliases.items()},
        compiler_params=_params(("arbitrary",) * len(grid)),
    )(*args, *side.ins)
    return res[:n_out], res[n_out:]


def _pair_swap(halves):
    n = len(halves)

    def body(*refs):
        src, dst = refs[:n], refs[n:2 * n]
        ssem, rsem = refs[2 * n:]
        x, y, c, _ = _place()
        cps = []
        for w in range(n):
            cp = _remote(src[w], dst[w], ssem.at[w], rsem.at[w], (x, y, 1 - c))
            cp.start()
            cps.append(cp)
        for cp in cps:
            cp.wait()

    outs = [jax.ShapeDtypeStruct(h.shape, h.dtype) for h in halves]
    return _comm_call("rs_pair_swap", body, halves, outs, n)


def _gather_small(vec):
    def body(src, dst, ssem, rsem):
        x, y, c, _ = _place()
        me = 4 * x + 2 * y + c
        cps = []
        for r in range(1, 8):
            dev = (_flip(x, r & 4), _flip(y, r & 2), _flip(c, r & 1))
            cp = _remote(src, dst.at[me], ssem.at[r - 1], rsem.at[r - 1], dev)
            cp.start()
            cps.append(cp)
        for cp in cps:
            cp.wait()

    out = jax.ShapeDtypeStruct((8,) + vec.shape, vec.dtype)
    return _comm_call("gather_small", body, [vec], [out], 7)[0]


def _pair_add(name, g, theirs, core):
    n, half, b = theirs.shape
    tr = _tile(half, 256, 16)
    nt = half // tr

    def body(c_ref, g_ref, t_ref, o_ref):
        o_ref[...] = (g_ref[...].astype(F32) + t_ref[...].astype(F32)).astype(o_ref.dtype)

    same = pl.BlockSpec((None, tr, b), lambda j, i, c: (j, i, 0))
    grid_spec = pltpu.PrefetchScalarGridSpec(
        num_scalar_prefetch=1, grid=(n, nt),
        in_specs=[pl.BlockSpec((None, tr, b), lambda j, i, c: (j, c[0] * nt + i, 0)), same], out_specs=same)
    return pl.pallas_call(body, name=name, grid_spec=grid_spec, out_shape=jax.ShapeDtypeStruct(theirs.shape, BF16),
                          compiler_params=_params(("parallel", "parallel")))(core, g, theirs)


def _chip_sum(name, parts, landed, chip1):
    n, r, c = landed.shape
    tr = _tile(r, 256, 16)

    def body(ix_ref, p_ref, s_ref, o_ref):
        acc = p_ref[...].astype(F32)
        for s in range(n):
            acc = acc + s_ref[s].astype(F32)
        o_ref[...] = acc

    grid_spec = pltpu.PrefetchScalarGridSpec(
        num_scalar_prefetch=1, grid=(r // tr,),
        in_specs=[pl.BlockSpec((None, tr, c), lambda i, ix: (ix[0], i, 0)), pl.BlockSpec((n, tr, c), lambda i, ix: (0, i, 0))],
        out_specs=pl.BlockSpec((tr, c), lambda i, ix: (i, 0)))
    return pl.pallas_call(body, name=name, grid_spec=grid_spec, out_shape=jax.ShapeDtypeStruct((r, c), F32),
                          compiler_params=_params(("parallel",)))(chip1, parts, landed)


def _sum_slots(name, stacked):
    n, r, c = stacked.shape
    tr = _tile(r, 256, 8)

    def body(s_ref, o_ref):
        acc = s_ref[0]
        for s in range(1, n):
            acc = acc + s_ref[s]
        o_ref[...] = acc

    return pl.pallas_call(
        body, name=name, grid=(r // tr,), in_specs=[pl.BlockSpec((n, tr, c), lambda i: (0, i, 0))],
        out_specs=pl.BlockSpec((tr, c), lambda i: (i, 0)),
        out_shape=jax.ShapeDtypeStruct((r, c), F32), compiler_params=_params(("parallel",)),
    )(stacked)


def _adam_math(w, g, m, v):
    bc1, bc2 = 1.0 - ADAM_B1 ** ADAM_STEP, 1.0 - ADAM_B2 ** ADAM_STEP
    nm = ADAM_B1 * m + (1.0 - ADAM_B1) * g
    nv = ADAM_B2 * v + (1.0 - ADAM_B2) * (g * g)
    return -ADAM_LR * ((nm / bc1) / (jnp.sqrt(nv / bc2) + ADAM_EPS) + ADAM_WD * w), nm, nv


def _adamw_halves(name, w, g_mine, g_theirs, m, v, core):
    r, c = w.shape
    h = r // 2
    tr = _tile(h, 128, 8)
    nth = h // tr

    def body(c_ref, w_ref, gm_ref, gt_ref, m_ref, v_ref, g_ref, d_ref, nm_ref, nv_ref):
        g = jnp.where(pl.program_id(0) // nth == c_ref[0], gm_ref[...], gt_ref[...])
        g_ref[...] = g
        d_ref[...], nm_ref[...], nv_ref[...] = _adam_math(w_ref[...], g, m_ref[...], v_ref[...])

    full = pl.BlockSpec((tr, c), lambda i, cr: (i, 0))
    half = pl.BlockSpec((tr, c), lambda i, cr: (i % nth, 0))
    grid_spec = pltpu.PrefetchScalarGridSpec(num_scalar_prefetch=1, grid=(r // tr,),
                                             in_specs=[full, half, half, full, full], out_specs=[full] * 4)
    return pl.pallas_call(body, name=name, grid_spec=grid_spec, out_shape=[jax.ShapeDtypeStruct((r, c), F32)] * 4,
                          compiler_params=_params(("parallel",)))(core, w, g_mine, g_theirs, m, v)


def _adamw(name, w, g, m, v):
    r, c = w.shape
    by_cols = r % 8 != 0 and c % LANES == 0
    tr, tc = (r, _tile(c, 256, LANES)) if by_cols else (_tile(r, 256, 8), c)

    def body(w_ref, g_ref, m_ref, v_ref, d_ref, nm_ref, nv_ref):
        d_ref[...], nm_ref[...], nv_ref[...] = _adam_math(w_ref[...], g_ref[...], m_ref[...], v_ref[...])

    spec = pl.BlockSpec((tr, tc), (lambda i: (0, i)) if by_cols else (lambda i: (i, 0)))
    return pl.pallas_call(
        body, name=name, grid=(c // tc if by_cols else r // tr,), in_specs=[spec] * 4, out_specs=[spec] * 3,
        out_shape=[jax.ShapeDtypeStruct((r, c), F32)] * 3, compiler_params=_params(("parallel",)),
    )(w, g, m, v)


def _pad_cols(a, cols):
    return jnp.pad(a, ((0, 0), (0, cols - a.shape[1])))


def _rot_cols(w):
    h = w.shape[-1] // 2
    return jnp.concatenate([-w[..., h:], w[..., :h]], axis=-1)


def _unrot_cols(d):
    h = d.shape[-1] // 2
    return jnp.concatenate([d[..., h:], -d[..., :h]], axis=-1)


def _logical(g):
    return jnp.transpose(g, (1, 0, 2)).reshape(g.shape[1], N_CHIPS * g.shape[2])


def _chunks(a, n):
    return jnp.transpose(a.reshape(a.shape[0], N_CHIPS, n), (1, 0, 2))


def kernel(x, positions, pre_mix_norm, w_in, q_a_norm, w_uq, kv_a_norm, w_ukv, b_forget, b_gate, w_branch_mla, w_branch_fox, w_out, post_mix_norm, pre_ffn_norm, w_up, conv_w, conv_b, w_down, post_ffn_norm, loss_target, m_pre_mix_norm, m_w_in, m_q_a_norm, m_w_uq, m_kv_a_norm, m_w_ukv, m_b_forget, m_b_gate, m_w_branch_mla, m_w_branch_fox, m_w_out, m_post_mix_norm, m_pre_ffn_norm, m_w_up, m_conv_w, m_conv_b, m_w_down, m_post_ffn_norm, v_pre_mix_norm, v_w_in, v_q_a_norm, v_w_uq, v_kv_a_norm, v_w_ukv, v_b_forget, v_b_gate, v_w_branch_mla, v_w_branch_fox, v_w_out, v_post_mix_norm, v_pre_ffn_norm, v_w_up, v_conv_w, v_conv_b, v_w_down, v_post_ffn_norm):
    n_seq, seq, d = x.shape
    n_tok = n_seq * seq
    d_in = N_CHIPS * w_in.shape[1]
    two_f = N_CHIPS * w_up.shape[1]
    ff_dim = two_f // 2
    assert d_in == QL + KVL + ROPE + 3 * HEADS * FDIM + HEADS + 2 * d
    n_in_shard = w_in.shape[1]
    in_pad = -(-n_in_shard // LANES) * LANES
    hd = HEADS * LANES
    xc, yc, cc = lax.axis_index("x"), lax.axis_index("y"), lax.axis_index("c")
    chip = 2 * xc + yc
    t_attn = _tile(seq, 512, 128)

    shards = [_pad_cols(w_in, in_pad).astype(BF16), w_uq.astype(BF16), w_ukv.astype(BF16), w_branch_mla.astype(BF16),
              w_branch_fox.astype(BF16), w_out.astype(BF16), w_up.astype(BF16), w_down.astype(BF16)]
    cw8 = jnp.pad(conv_w, ((0, 5), (0, 0)))
    (g_in, g_uq, g_ukv), (g_cw,) = _all_gather_weights(shards[:3], [cw8])
    n_attn_steps = n_seq * (seq // t_attn)
    side_proj = _gather_side([shards[3], shards[4], shards[5]], [], mid_step=0.9)
    side_ffn = _gather_side([shards[7]], [], mid_step=0.7)
    up_rows = shards[6].shape[0]
    up_cuts = [0, up_rows // 8, up_rows // 2, 7 * up_rows // 8, up_rows]

    def side_up(piece, filled, mid_step):
        lo, hi = up_cuts[piece], up_cuts[piece + 1]
        return _gather_side([shards[6][lo:hi]], [], mid_step=mid_step, into=[(filled, up_rows, lo)])

    o_q, o_kv, o_kpe = 0, QL, QL + KVL
    o_f = o_kpe + ROPE
    o_fl = o_f + 3 * hd
    o_g = o_fl + HEADS

    def chip_cols(lo, hi):
        out = []
        while lo < hi:
            j = lo // n_in_shard
            end = min(hi, (j + 1) * n_in_shard)
            out.append((j, lo - j * n_in_shard, end - j * n_in_shard))
            lo = end
        return out

    take = lambda lo, hi: [g_in[j, :, a:b] for j, a, b in chip_cols(lo, hi)]
    w_kpe = jnp.concatenate(take(o_kpe, o_f), axis=1)
    zeros = lambda n: jnp.zeros((d, n), BF16)
    win_p = jnp.concatenate(
        take(o_g, d_in) + take(o_q, o_kpe) + [w_kpe, zeros(LANES - ROPE), _rot_cols(w_kpe)] + take(o_fl, o_g)
        + [zeros(LANES - ROPE - HEADS)] + take(o_f, o_fl), axis=1)
    n_p = win_p.shape[1]
    cb_gm, cb_gf = 0, 1
    c_lat = 2 * d
    c_kx, c_kr = c_lat + QL + KVL, c_lat + QL + KVL + LANES
    n_pa = c_kr + LANES
    assert n_p == n_pa + 3 * hd

    uq3 = _logical(g_uq).reshape(QL, HEADS, NOPE + ROPE)
    pe = uq3[:, :, NOPE:]
    pad_pe = lambda a: jnp.pad(a, ((0, 0), (0, 0), (0, LANES - ROPE))).reshape(QL, hd)
    wuq_p = jnp.concatenate([uq3[:, :, :NOPE].reshape(QL, hd), pad_pe(pe), pad_pe(_rot_cols(pe))], axis=1)
    ukv3 = _logical(g_ukv).reshape(KVL, HEADS, NOPE + VDIM)
    wukv_p = jnp.concatenate([ukv3[:, :, :NOPE].reshape(KVL, hd), ukv3[:, :, NOPE:].reshape(KVL, hd)], axis=1)

    n_bm, n_up = w_branch_mla.shape[1], w_up.shape[1]
    l_bm, l_up = _Chunked(n_bm), _Chunked(n_up)
    wt = n_up // 2
    n_ut = two_f // wt
    il = lambda cblk: jnp.where(cblk < n_ut // 2, 2 * cblk, 2 * (cblk - n_ut // 2) + 1)
    l_il = _Plain(il)
    to_il = lambda a: a.reshape(a.shape[0], 2, n_ut // 2, wt).transpose(0, 2, 1, 3).reshape(a.shape[0], two_f)
    from_il = lambda a: a.reshape(a.shape[0], n_ut // 2, 2, wt).transpose(0, 2, 1, 3).reshape(a.shape[0], two_f)

    inv_freq = 1.0 / (ROPE_THETA ** (jnp.arange(0, ROPE, 2, dtype=F32) / ROPE))
    ang = positions.astype(F32).reshape(n_tok, 1) * inv_freq
    cos, sin = jnp.cos(ang), jnp.sin(ang)
    cs = _pad_cols(jnp.concatenate([cos, cos], axis=1), LANES)
    sn = _pad_cols(jnp.concatenate([sin, sin], axis=1), LANES)

    row = lambda v: v.reshape(1, -1)
    x2 = x.reshape(n_tok, d)
    tgt = loss_target.reshape(n_tok, d)

    (h,) = _rows("rms_pre_mix", lambda r, v: ([r[0] * _rstd(r[0]) * v[0]], []),
                 [(x2, d, 0)], [row(pre_mix_norm)], [(d, BF16)], [], n_tok)
    proj, (g_bm, g_bf, g_out) = _mm("proj_in", "nn", h, win_p, n_tok, n_pa, d, tm=2048, side=side_proj)
    w_out_full = g_out.reshape(d, d)
    tn_f = _tile(3 * hd, 1024, 128)
    assert n_pa % tn_f == 0
    proj_f, (g_up,) = _mm("proj_in_fox", "nn", h, win_p, n_tok, 3 * hd, d, tn=tn_f, lb=_Plain(lambda cblk: cblk + n_pa // tn_f),
                          out_dtype=BF16, side=side_up(0, None, 0.8))

    bf_vec = jnp.pad(row(b_forget), ((0, 0), (ROPE, LANES - ROPE - HEADS)))

    def lat_fwd(r, v):
        ql, kvl = r[0], r[1]
        return [ql * _rstd(ql) * v[0], kvl * _rstd(kvl) * v[1], r[2] * r[4] + r[3] * r[5]], []

    qn, kvn, rk = _rows("latent_norms", lat_fwd,
                        [(proj, QL, c_lat // QL), (proj, KVL, (c_lat + QL) // KVL), (proj, LANES, c_kx // LANES),
                         (proj, LANES, c_kr // LANES), (cs, LANES, 0), (sn, LANES, 0)],
                        [row(q_a_norm), row(kv_a_norm)], [(QL, BF16), (KVL, BF16), (LANES, BF16)], [], n_tok)
    q_p = _mm("q_up", "nn", qn, wuq_p, n_tok, 3 * hd, QL)
    kv_p = _mm("kv_up", "nn", kvn, wukv_p, n_tok, 2 * hd, KVL, out_dtype=BF16)

    def rope_q(r, v):
        c8, s8 = jnp.tile(r[3], (1, HEADS)), jnp.tile(r[4], (1, HEADS))
        return [r[0], r[1] * c8 + r[2] * s8], []

    q_nope, rq = _rows("rope_q", rope_q, [(q_p, hd, 0), (q_p, hd, 1), (q_p, hd, 2), (cs, LANES, 0), (sn, LANES, 0)], [],
                       [(hd, BF16), (hd, BF16)], [], n_tok)

    mla_q = [(q_nope, 0, False), (rq, 0, False)]
    mla_k = [(kv_p, 0, False), (rk, 0, True)]
    mla_v = (kv_p, 1)
    mla_scale = (NOPE + ROPE) ** -0.5
    (o_mla, lse_mla), (g_up,) = _attn_fwd("mla_fwd", mla_q, mla_k, mla_v, None, CHUNK, mla_scale, n_seq, seq, t_attn,
                                          side=side_up(1, g_up, max(n_attn_steps - 2, 0)))

    c_run = _seq_cumsum("forget_cumsum", proj, c_kr // LANES, n_seq, seq, False,
                        pre=lambda z, b: _log_sigmoid(z + b), vec=bf_vec)
    nb_attn = seq // t_attn
    c_rowf = jnp.transpose(c_run[:, ROPE:ROPE + HEADS].reshape(n_seq, nb_attn, t_attn, HEADS), (0, 1, 3, 2))
    fox_q, fox_k, fox_v = [(proj_f, 0, False)], [(proj_f, 1, False)], (proj_f, 2)
    fox_scale = FDIM ** -0.5
    fox_bias = (c_run, c_rowf)
    (o_fox, lse_fox), (g_up,) = _attn_fwd("fox_fwd", fox_q, fox_k, fox_v, fox_bias, 1, fox_scale, n_seq, seq, t_attn,
                                          side=side_up(2, g_up, max(n_attn_steps - 2, 0)))

    pm, (g_up,) = _mm("branch_mla", "nn", o_mla, g_bm, n_tok, d, hd, lb=l_bm, tn=n_bm, out_dtype=BF16,
                      side=side_up(3, g_up, 0.7))
    pf = _mm("branch_fox", "nn", o_fox, g_bf, n_tok, d, hd, lb=l_bm, tn=n_bm, out_dtype=BF16)
    bg = row(b_gate)

    def merge(r, v):
        return [_sigmoid(r[0] + v[0]) * r[2] + _sigmoid(r[1] + v[1]) * r[3]], []

    (merged,) = _rows("gate_merge", merge, [(proj, d, cb_gm), (proj, d, cb_gf), (pm, d, 0), (pf, d, 0)],
                      [bg[:, :d], bg[:, d:]], [(d, BF16)], [], n_tok)
    y1 = _mm("mix_out", "nn", merged, w_out_full, n_tok, d, d)

    def resid_norm(r, v):
        x1v = r[0] + r[1] * _rstd(r[1]) * v[0]
        return [x1v, x1v * _rstd(x1v) * v[1]], []

    x1, h2 = _rows("post_mix_pre_ffn", resid_norm, [(x2, d, 0), (y1, d, 0)], [row(post_mix_norm), row(pre_ffn_norm)],
                   [(d, F32), (d, BF16)], [], n_tok)

    u_il, got = _mm("ffn_up", "nn", h2, g_up, n_tok, two_f, d, lb=l_up, lo=l_il, tm=2048, tn=wt, out_dtype=BF16,
                    side=side_ffn)
    w_down_full = got[0].reshape(ff_dim, d)
    cw_il = to_il(_logical(g_cw)[:3])
    cb_il = to_il(row(conv_b))
    act = _conv_glu_fwd(u_il, cw_il, cb_il, n_seq, seq, wt)
    ff = _mm("ffn_down", "nn", act, w_down_full, n_tok, d, ff_dim)

    def final(r, v):
        x1v, ffv, tg = r
        diff = x1v + ffv * _rstd(ffv) * v[0] - tg
        dx2v = diff / d
        dffv, dg4 = _rms_bwd(ffv, v[0], dx2v)
        sq = jnp.sum(jnp.sum(diff * diff, axis=1, keepdims=True), axis=0, keepdims=True)
        return [dx2v, dffv], [dg4, jnp.broadcast_to(sq, (1, LANES))]

    dx2, dff, dg_post_ffn, sq_sum = _rows("loss_post_ffn_bwd", final, [(x1, d, 0), (ff, d, 0), (tgt, d, 0)],
                                          [row(post_ffn_norm)], [(d, F32), (d, BF16)], [(1, d), (1, LANES)], n_tok)
    rs_parts, rs_landed = {}, {}
    core = jnp.reshape(cc, (1,)).astype(jnp.int32)

    def pair_reduce(tag, names, grads):
        theirs = _pair_split("rs_pair_split_" + tag, grads)
        for nm, g, b in zip(names, grads, theirs):
            rs_parts[nm] = _pair_add("rs_pair_add_" + nm, g, b, core)

    dact = _mm("ffn_down_dx", "nt", dff, w_down_full, n_tok, ff_dim, d, tm=2048, tn=wt, out_dtype=BF16)
    gw_down = _mm("ffn_down_dw", "tn", act, dff, ff_dim, d, n_tok, tm=wt, out_dtype=BF16)
    pair_reduce("down", ["w_down"], [gw_down.reshape(N_CHIPS, ff_dim // N_CHIPS, d)])
    d_il, conv_acc = _conv_glu_bwd_pre(u_il, dact, cw_il, cb_il, n_seq, seq, wt)
    du_il = _conv_bwd_input(d_il, cw_il, n_seq, seq, wt)
    gw_up, got = _mm("ffn_up_dw", "tn", h2, du_il, d, two_f, n_tok, lb=l_il, lo=l_up, tn=wt, out_dtype=BF16,
                     side=_scatter_side([rs_parts["w_down"]]))
    rs_landed["w_down"] = got[0]
    pair_reduce("up", ["w_up"], [gw_up])
    dh2, got = _mm("ffn_up_dx", "nt", du_il, g_up, n_tok, d, two_f, la=l_il, lb=l_up, tk=wt,
                   side=_scatter_side([rs_parts["w_up"]]))
    rs_landed["w_up"] = got[0]

    def mid_bwd(r, v):
        x1v, y1v, dx2v, dh2v = r
        d3, dg3 = _rms_bwd(x1v, v[1], dh2v)
        dx1v = dx2v + d3
        dy1v, dg2 = _rms_bwd(y1v, v[0], dx1v)
        return [dx1v, dy1v], [dg3, dg2]

    dx1, dy1, dg_pre_ffn, dg_post_mix = _rows(
        "pre_ffn_post_mix_bwd", mid_bwd, [(x1, d, 0), (y1, d, 0), (dx2, d, 0), (dh2, d, 0)],
        [row(post_mix_norm), row(pre_ffn_norm)], [(d, F32), (d, BF16)], [(1, d), (1, d)], n_tok)
    dmerged = _mm("mix_out_dx", "nt", dy1, w_out_full, n_tok, d, d, out_dtype=BF16)
    gw_out = _mm("mix_out_dw", "tn", merged, dy1, d, d, n_tok, out_dtype=BF16)

    def gate_bwd(r, v):
        zm, zf, pmv, pfv, dm = r
        gm, gf = _sigmoid(zm + v[0]), _sigmoid(zf + v[1])
        dzm, dzf = dm * pmv * gm * (1.0 - gm), dm * pfv * gf * (1.0 - gf)
        return [dm * gm, dm * gf, jnp.concatenate([dzm, dzf], axis=1)], [_colsum(dzm), _colsum(dzf)]

    dpm, dpf, dz, dbg_m, dbg_f = _rows(
        "gate_merge_bwd", gate_bwd, [(proj, d, cb_gm), (proj, d, cb_gf), (pm, d, 0), (pf, d, 0), (dmerged, d, 0)],
        [bg[:, :d], bg[:, d:]], [(d, BF16), (d, BF16), (2 * d, BF16)], [(1, d), (1, d)], n_tok)
    tk_b = min(n_bm, 512)
    do_mla = _mm("branch_mla_dx", "nt", dpm, g_bm, n_tok, hd, d, lb=l_bm, tk=tk_b)
    do_fox = _mm("branch_fox_dx", "nt", dpf, g_bf, n_tok, hd, d, lb=l_bm, tk=tk_b)
    gw_bm = _mm("branch_mla_dw", "tn", o_mla, dpm, hd, d, n_tok, lo=l_bm, tn=n_bm, out_dtype=BF16)
    gw_bf = _mm("branch_fox_dw", "tn", o_fox, dpf, hd, d, n_tok, lo=l_bm, tn=n_bm, out_dtype=BF16)

    pair_reduce("mix", ["w_out", "w_branch_mla", "w_branch_fox"], [gw_out.reshape(N_CHIPS, d // N_CHIPS, d), gw_bm, gw_bf])
    delta_mla, dob_mla = _attn_delta("mla_delta", o_mla, do_mla, n_tok)
    (dq_nope, drq, dk_nope, drk_g, dv_mla), got = _attn_bwd(
        "mla_bwd", mla_q, mla_k, mla_v, dob_mla, lse_mla, delta_mla, None, CHUNK, mla_scale, n_seq, seq, t_attn, BF16,
        side=_scatter_side([rs_parts[nm] for nm in ("w_out", "w_branch_mla", "w_branch_fox")]))
    rs_landed.update(zip(("w_out", "w_branch_mla", "w_branch_fox"), got))
    delta_fox, dob_fox = _attn_delta("fox_delta", o_fox, do_fox, n_tok)
    (dfq, dfk, dfv, dc_q, dc_k), _ = _attn_bwd("fox_bwd", fox_q, fox_k, fox_v, dob_fox, lse_fox, delta_fox, fox_bias, 1,
                                               fox_scale, n_seq, seq, t_attn, BF16)
    dc_k8 = jnp.transpose(dc_k, (0, 2, 4, 1, 3)).reshape(n_tok, HEADS)
    dc128 = dc_q[0] + dc_q[1] + jnp.pad(dc_k8, ((0, 0), (ROPE, LANES - ROPE - HEADS)))
    dlogf = _seq_cumsum("forget_cumsum_bwd", dc128, 0, n_seq, seq, True)

    def mla_pack(r, v):
        dqn_v, drq_v, dkn_v, dv_v, drk_a, drk_b, c1, s1 = r
        c8, s8 = jnp.tile(c1, (1, HEADS)), jnp.tile(s1, (1, HEADS))
        drk_v = drk_a + drk_b
        return [jnp.concatenate([dqn_v, drq_v * c8, drq_v * s8], axis=1), jnp.concatenate([dkn_v, dv_v], axis=1),
                drk_v * c1, drk_v * s1], []

    dq_p, dkv_p, dkx, dkr = _rows(
        "mla_rope_bwd", mla_pack,
        [(dq_nope, hd, 0), (drq, hd, 0), (dk_nope, hd, 0), (dv_mla, hd, 0), (drk_g[0], LANES, 0), (drk_g[1], LANES, 0),
         (cs, LANES, 0), (sn, LANES, 0)],
        [], [(3 * hd, BF16), (2 * hd, BF16), (LANES, F32), (LANES, F32)], [], n_tok)
    dqn = _mm("q_up_dx", "nt", dq_p, wuq_p, n_tok, QL, 3 * hd)
    gw_uq_p = _mm("q_up_dw", "tn", qn, dq_p, QL, 3 * hd, n_tok, out_dtype=BF16)
    dkvn = _mm("kv_up_dx", "nt", dkv_p, wukv_p, n_tok, KVL, 2 * hd)
    gw_ukv_p = _mm("kv_up_dw", "tn", kvn, dkv_p, KVL, 2 * hd, n_tok, out_dtype=BF16)

    def lat_bwd(r, v):
        ql, kvl, dqn_v, dkvn_v, dkx_v, dkr_v, zblk, dlf = r
        dql, dgq = _rms_bwd(ql, v[0], dqn_v)
        dkvl, dgkv = _rms_bwd(kvl, v[1], dkvn_v)
        dfl = dlf * _sigmoid(-(zblk + v[2]))
        return [jnp.concatenate([dql, dkvl, dkx_v, dkr_v + dfl], axis=1)], [dgq, dgkv, _colsum(dfl)]

    dlat, dg_q, dg_kv, dbf = _rows(
        "latent_bwd", lat_bwd,
        [(proj, QL, c_lat // QL), (proj, KVL, (c_lat + QL) // KVL), (dqn, QL, 0), (dkvn, KVL, 0), (dkx, LANES, 0),
         (dkr, LANES, 0), (proj, LANES, c_kr // LANES), (dlogf, LANES, 0)],
        [row(q_a_norm), row(kv_a_norm), bf_vec], [(QL + KVL + 2 * LANES, BF16)], [(1, QL), (1, KVL), (1, LANES)], n_tok)
    dproj = [dz, dlat, dfq, dfk, dfv]
    gw_in_p = _mm_parts("proj_in_dw", "tn", h, dproj, d, n_p, n_tok, tk=1024, out_dtype=BF16)

    f32 = lambda a: a.astype(F32)
    kr_blk = gw_in_p[:, c_kr:c_kr + LANES]
    d_kpe = (f32(gw_in_p[:, c_kx:c_kx + ROPE]) + _unrot_cols(f32(kr_blk[:, :ROPE]))).astype(BF16)
    in_pieces = [(o_q, gw_in_p, c_lat, QL + KVL), (o_kpe, d_kpe, 0, ROPE), (o_f, gw_in_p, n_pa, 3 * hd),
                 (o_fl, kr_blk, ROPE, HEADS), (o_g, gw_in_p, 0, 2 * d)]
    gc_in = []
    for j in range(N_CHIPS):
        lo, hi, cols = j * n_in_shard, (j + 1) * n_in_shard, []
        for first, arr, at, width in in_pieces:
            a, b = max(lo, first), min(hi, first + width)
            if a < b:
                cols.append(arr[:, at + a - first:at + b - first])
        cols.append(jnp.zeros((d, in_pad - n_in_shard), BF16))
        gc_in.append(jnp.concatenate(cols, axis=1))
    gc_in = jnp.stack(gc_in)
    uq_parts = [gw_uq_p[:, i * hd:(i + 1) * hd].reshape(QL, HEADS, LANES) for i in range(3)]
    d_pe = (f32(uq_parts[1][:, :, :ROPE]) + _unrot_cols(f32(uq_parts[2][:, :, :ROPE]))).astype(BF16)
    gc_uq = _chunks(jnp.concatenate([uq_parts[0], d_pe], axis=2).reshape(QL, HEADS * (NOPE + ROPE)), w_uq.shape[1])
    gc_ukv = _chunks(jnp.concatenate([gw_ukv_p[:, :hd].reshape(KVL, HEADS, NOPE), gw_ukv_p[:, hd:].reshape(KVL, HEADS, VDIM)],
                                     axis=2).reshape(KVL, HEADS * (NOPE + VDIM)), w_ukv.shape[1])
    grads = [gc_in, gc_uq, gc_ukv]

    late = ["w_in", "w_uq", "w_ukv"]
    pair_reduce("late", late, grads)
    dh, got = _mm_parts("proj_in_dx", "nt", dproj, win_p, n_tok, d, n_p, side=_scatter_side([rs_parts[nm] for nm in late]))
    rs_landed.update(zip(late, got))

    def first_bwd(r, v):
        dxa, dg1 = _rms_bwd(r[0], v[0], r[1])
        return [r[2] + dxa], [dg1]

    grad_x, dg_pre_mix = _rows("pre_mix_bwd", first_bwd, [(x2, d, 0), (dh, d, 0), (dx1, d, 0)], [row(pre_mix_norm)],
                               [(d, F32)], [(1, d)], n_tok)
    big = list(rs_parts)
    chip1 = jnp.reshape(chip, (1,)).astype(jnp.int32)
    halves = [_chip_sum("rs_chip_sum_" + nm, rs_parts[nm], rs_landed[nm], chip1) for nm in big]
    other = _pair_swap(halves)
    g_halves = dict(zip(big, zip(halves, other)))

    conv_acc_l = from_il(conv_acc)
    pieces = [dg_pre_mix, dg_q, dg_kv, dbf, dbg_m, dbg_f, dg_post_mix, dg_pre_ffn, conv_acc_l[3:4], dg_post_ffn,
              conv_acc_l[0:1], conv_acc_l[1:2], conv_acc_l[2:3], sq_sum]
    sizes = [p.shape[1] for p in pieces]
    flat = jnp.concatenate(pieces, axis=1)
    n_rows = -(-flat.shape[1] // (8 * LANES)) * 8
    flat = _pad_cols(flat, n_rows * LANES).reshape(n_rows, LANES)
    slots = lax.dynamic_update_slice(_gather_small(flat), flat[None], (2 * chip + cc, 0, 0))
    total = _sum_slots("small_sum", slots).reshape(1, n_rows * LANES)
    offs = [sum(sizes[:i]) for i in range(len(sizes))]
    tot = [total[0, o:o + s] for o, s in zip(offs, sizes)]
    loss = 0.5 * tot[13][0] / d
    g_small = {"pre_mix_norm": tot[0], "q_a_norm": tot[1], "kv_a_norm": tot[2], "b_forget": tot[3][ROPE:ROPE + HEADS],
               "b_gate": jnp.concatenate([tot[4], tot[5]]), "post_mix_norm": tot[6], "pre_ffn_norm": tot[7],
               "conv_b": tot[8], "post_ffn_norm": tot[9]}
    gcw_full = jnp.stack([tot[10], tot[11], tot[12]])
    g_conv_w = lax.dynamic_slice(gcw_full, (0, chip * n_up), (3, n_up))

    given = dict(pre_mix_norm=(pre_mix_norm, m_pre_mix_norm, v_pre_mix_norm), w_in=(w_in, m_w_in, v_w_in),
                 q_a_norm=(q_a_norm, m_q_a_norm, v_q_a_norm), w_uq=(w_uq, m_w_uq, v_w_uq),
                 kv_a_norm=(kv_a_norm, m_kv_a_norm, v_kv_a_norm), w_ukv=(w_ukv, m_w_ukv, v_w_ukv),
                 b_forget=(b_forget, m_b_forget, v_b_forget), b_gate=(b_gate, m_b_gate, v_b_gate),
                 w_branch_mla=(w_branch_mla, m_w_branch_mla, v_w_branch_mla),
                 w_branch_fox=(w_branch_fox, m_w_branch_fox, v_w_branch_fox), w_out=(w_out, m_w_out, v_w_out),
                 post_mix_norm=(post_mix_norm, m_post_mix_norm, v_post_mix_norm),
                 pre_ffn_norm=(pre_ffn_norm, m_pre_ffn_norm, v_pre_ffn_norm), w_up=(w_up, m_w_up, v_w_up),
                 conv_w=(conv_w, m_conv_w, v_conv_w), conv_b=(conv_b, m_conv_b, v_conv_b),
                 w_down=(w_down, m_w_down, v_w_down), post_ffn_norm=(post_ffn_norm, m_post_ffn_norm, v_post_ffn_norm))
    order = list(given)
    grad, delta, new_m, new_v = {}, {}, {}, {}
    for nm in big:
        mine, theirs = g_halves[nm]
        if nm == "w_in":
            full = jnp.concatenate([jnp.where(cc == 0, mine, theirs), jnp.where(cc == 0, theirs, mine)], axis=0)
            grad[nm] = full[:, :n_in_shard]
            tr_out = _adamw("adamw_" + nm, *[jnp.transpose(a) for a in (given[nm][0], grad[nm], given[nm][1], given[nm][2])])
            delta[nm], new_m[nm], new_v[nm] = [jnp.transpose(a) for a in tr_out]
            continue
        grad[nm], delta[nm], new_m[nm], new_v[nm] = _adamw_halves("adamw_" + nm, given[nm][0], mine, theirs, given[nm][1],
                                                                  given[nm][2], core)
    grad["conv_w"] = g_conv_w
    delta["conv_w"], new_m["conv_w"], new_v["conv_w"] = _adamw("adamw_conv_w", conv_w, g_conv_w, m_conv_w, v_conv_w)
    small = list(g_small)
    padded = [-(-g_small[nm].shape[0] // LANES) * LANES for nm in small]
    s_rows = -(-sum(padded) // (8 * LANES)) * 8

    def pack(vals):
        cat = jnp.concatenate([jnp.pad(a, (0, p - a.shape[0])) for a, p in zip(vals, padded)])
        return jnp.pad(cat, (0, s_rows * LANES - cat.shape[0])).reshape(s_rows, LANES)

    packed = _adamw("adamw_small", pack([given[nm][0] for nm in small]), pack([g_small[nm] for nm in small]),
                    pack([given[nm][1] for nm in small]), pack([given[nm][2] for nm in small]))
    s_offs = [sum(padded[:i]) for i in range(len(small))]
    for nm, o in zip(small, s_offs):
        n_el = g_small[nm].shape[0]
        grad[nm] = g_small[nm]
        delta[nm], new_m[nm], new_v[nm] = [p.reshape(-1)[o:o + n_el] for p in packed]
    return (loss, grad_x.reshape(n_seq, seq, d), *[grad[nm] for nm in order], *[delta[nm] for nm in order],
            *[new_m[nm] for nm in order], *[new_v[nm] for nm in order])
```

```python
import functools
import math

import jax
import jax.numpy as jnp
from jax import lax
from jax.experimental import pallas as pl
from jax.experimental.pallas import tpu as pltpu

F32, BF16 = jnp.float32, jnp.bfloat16
MESH = pl.DeviceIdType.MESH

HEADS = 8
NOPE, ROPE, VDIM = 128, 64, 128
QL, KVL = 512, 256
FDIM = 128
CHUNK = 64
ROPE_THETA = 10000.0
EPS = 1e-6
NEG_INF = -1e30
ADAM_LR, ADAM_B1, ADAM_B2, ADAM_EPS, ADAM_WD, ADAM_STEP = 0.001, 0.9, 0.999, 1e-08, 0.01, 10

VMEM_LIMIT_BYTES = 52 * 1024 * 1024
LANES = 128
N_CHIPS = 4


def _params(sem):
    return pltpu.CompilerParams(dimension_semantics=sem, vmem_limit_bytes=VMEM_LIMIT_BYTES)


def _tile(n, target, mult):
    if n <= target:
        return n
    t = (target // mult) * mult
    while t >= mult:
        if n % t == 0:
            return t
        t -= mult
    raise ValueError(f"no tile for {n} (target {target}, multiple of {mult})")


class _Plain:
    def __init__(self, perm=None):
        self.perm = perm

    def spec(self, tr, tc, rc):
        perm = self.perm

        def imap(i, j, k):
            r, c = rc(i, j, k)
            return (r, perm(c) if perm is not None else c)

        return pl.BlockSpec((tr, tc), imap)

    def shape(self, rows, cols):
        return (rows, cols)


class _Chunked:
    def __init__(self, n):
        self.n = n

    def spec(self, tr, tc, rc):
        assert self.n % tc == 0, (self.n, tc)
        per = self.n // tc

        def imap(i, j, k):
            r, c = rc(i, j, k)
            return (c // per, r, c % per)

        return pl.BlockSpec((None, tr, tc), imap)

    def shape(self, rows, cols):
        assert cols == N_CHIPS * self.n
        return (N_CHIPS, rows, self.n)


_DIMS = {"nn": (((1,), (0,)), ((), ())), "nt": (((1,), (1,)), ((), ())), "tn": (((0,), (0,)), ((), ()))}


def _mm_single(name, mode, a, b, m, n, k, tm, tn, la, lb, lo, out_dtype, side):
    if mode == "nn":
        a_spec = la.spec(tm, k, lambda i, j, kk: (i, 0))
        b_spec = lb.spec(k, tn, lambda i, j, kk: (0, j))
    elif mode == "nt":
        a_spec = la.spec(tm, k, lambda i, j, kk: (i, 0))
        b_spec = lb.spec(tn, k, lambda i, j, kk: (j, 0))
    else:
        a_spec = la.spec(k, tm, lambda i, j, kk: (0, i))
        b_spec = lb.spec(k, tn, lambda i, j, kk: (0, j))
    o_spec = lo.spec(tm, tn, lambda i, j, kk: (i, j))
    dims = _DIMS[mode]

    def body(a_ref, b_ref, o_ref):
        o_ref[...] = lax.dot_general(a_ref[...].astype(BF16), b_ref[...].astype(BF16), dims,
                                     preferred_element_type=F32).astype(o_ref.dtype)

    (out,), got = _hosted_call(body, name, (m // tm, n // tn, 1), [a_spec, b_spec], [o_spec],
                               [jax.ShapeDtypeStruct(lo.shape(m, n), out_dtype)], (a, b), side,
                               semantics=("parallel", "parallel", "arbitrary"))
    return out if side is None else (out, got)


def _mm(name, mode, a, b, m, n, k, *, tm=1024, tn=1024, tk=2048, la=None, lb=None, lo=None, out_dtype=F32, side=None):
    la, lb, lo = la or _Plain(), lb or _Plain(), lo or _Plain()
    tm, tn, tk = _tile(m, tm, 128), _tile(n, tn, 128), _tile(k, tk, 128)
    nk = k // tk
    if nk == 1:
        return _mm_single(name, mode, a, b, m, n, k, tm, tn, la, lb, lo, out_dtype, side)
    if mode == "nn":
        a_spec = la.spec(tm, tk, lambda i, j, kk: (i, kk))
        b_spec = lb.spec(tk, tn, lambda i, j, kk: (kk, j))
    elif mode == "nt":
        a_spec = la.spec(tm, tk, lambda i, j, kk: (i, kk))
        b_spec = lb.spec(tn, tk, lambda i, j, kk: (j, kk))
    else:
        a_spec = la.spec(tk, tm, lambda i, j, kk: (kk, i))
        b_spec = lb.spec(tk, tn, lambda i, j, kk: (kk, j))
    o_spec = lo.spec(tm, tn, lambda i, j, kk: (i, j))
    dims = _DIMS[mode]

    def body(a_ref, b_ref, o_ref, acc_ref):
        kk = pl.program_id(2)

        @pl.when(kk == 0)
        def _():
            acc_ref[...] = jnp.zeros_like(acc_ref)

        acc_ref[...] += lax.dot_general(a_ref[...].astype(BF16), b_ref[...].astype(BF16), dims,
                                        preferred_element_type=F32)

        @pl.when(kk == nk - 1)
        def _():
            o_ref[...] = acc_ref[...].astype(o_ref.dtype)

    (out,), got = _hosted_call(body, name, (m // tm, n // tn, nk), [a_spec, b_spec], [o_spec],
                               [jax.ShapeDtypeStruct(lo.shape(m, n), out_dtype)], (a, b), side,
                               semantics=("parallel", "parallel", "arbitrary"), scratch=[pltpu.VMEM((tm, tn), F32)])
    return out if side is None else (out, got)


def _mm_parts(name, mode, a, b, m, n, k, *, part=1024, tm=1024, tn=1024, tk=2048, out_dtype=F32, side=None):
    parts = b if mode == "tn" else a
    widths = [p.shape[1] for p in parts]
    assert all(w % part == 0 for w in widths) and sum(widths) == (n if mode == "tn" else k)
    offs = [sum(widths[:i]) // part for i in range(len(widths))]
    nblk = [w // part for w in widths]
    if mode == "tn":
        tn, tk = part, _tile(k, tk, 128)
    else:
        tk, tn = part, _tile(n, tn, 128)
    tm = _tile(m, tm, 128)
    nk = k // tk
    grid = (m // tm, n // tn, nk)
    np_ = len(parts)

    def inside(idx, p):
        return jnp.logical_and(idx >= offs[p], idx < offs[p] + nblk[p])

    def part_spec(p):
        if mode == "tn":
            def imap(i, j, kk):
                on = inside(j, p)
                return (jnp.where(on, kk, 0), jnp.clip(j - offs[p], 0, nblk[p] - 1))
            return pl.BlockSpec((tk, tn), imap)

        def imap(i, j, kk):
            return (i, jnp.clip(kk - offs[p], 0, nblk[p] - 1))
        return pl.BlockSpec((tm, tk), imap)

    if mode == "tn":
        in_specs = [pl.BlockSpec((tk, tm), lambda i, j, kk: (kk, i))] + [part_spec(p) for p in range(np_)]
        args = [a] + list(parts)
    else:
        in_specs = [part_spec(p) for p in range(np_)] + [pl.BlockSpec((tn, tk), lambda i, j, kk: (j, kk))]
        args = list(parts) + [b]
    dims = _DIMS[mode]

    def body(*refs):
        o_ref, acc_ref = refs[-2], refs[-1]
        j, kk = pl.program_id(1), pl.program_id(2)

        @pl.when(kk == 0)
        def _():
            acc_ref[...] = jnp.zeros_like(acc_ref)

        for p in range(np_):
            @pl.when(inside(j if mode == "tn" else kk, p))
            def _(p=p):
                lhs, rhs = (refs[0], refs[1 + p]) if mode == "tn" else (refs[p], refs[np_])
                acc_ref[...] += lax.dot_general(lhs[...].astype(BF16), rhs[...].astype(BF16), dims, preferred_element_type=F32)

        @pl.when(kk == nk - 1)
        def _():
            o_ref[...] = acc_ref[...].astype(o_ref.dtype)

    (out,), got = _hosted_call(body, name, grid, in_specs, [pl.BlockSpec((tm, tn), lambda i, j, kk: (i, j))],
                               [jax.ShapeDtypeStruct((m, n), out_dtype)], args, side,
                               semantics=("parallel", "parallel", "arbitrary"), scratch=[pltpu.VMEM((tm, tn), F32)])
    return out if side is None else (out, got)


def _rows(name, fn, rows_in, vecs_in, rows_out, accs_out, n_rows, tr=256):
    tr = _tile(n_rows, tr, 16)
    nr, nv, no = len(rows_in), len(vecs_in), len(rows_out)

    def body(*refs):
        ins, vecs = refs[:nr], refs[nr:nr + nv]
        outs, accs = refs[nr + nv:nr + nv + no], refs[nr + nv + no:]
        ro, ac = fn([r[...] for r in ins], [v[...] for v in vecs])
        for o_ref, val in zip(outs, ro):
            o_ref[...] = val.astype(o_ref.dtype)
        if accs:
            @pl.when(pl.program_id(0) == 0)
            def _():
                for a_ref in accs:
                    a_ref[...] = jnp.zeros_like(a_ref)

            for a_ref, val in zip(accs, ac):
                a_ref[...] += val

    in_specs = [pl.BlockSpec((tr, cols), functools.partial(lambda i, cb: (i, cb), cb=cb)) for _, cols, cb in rows_in]
    in_specs += [pl.BlockSpec(v.shape, lambda i: (0, 0)) for v in vecs_in]
    out_specs = [pl.BlockSpec((tr, cols), lambda i: (i, 0)) for cols, _ in rows_out]
    out_specs += [pl.BlockSpec((r, cols), lambda i: (0, 0)) for r, cols in accs_out]
    out_shape = [jax.ShapeDtypeStruct((n_rows, cols), dt) for cols, dt in rows_out]
    out_shape += [jax.ShapeDtypeStruct((r, cols), F32) for r, cols in accs_out]
    res = pl.pallas_call(
        body, name=name, grid=(n_rows // tr,), in_specs=in_specs, out_specs=out_specs, out_shape=out_shape,
        compiler_params=_params(("arbitrary",)),
    )(*[a for a, _, _ in rows_in], *vecs_in)
    return res


def _colsum(v):
    return jnp.sum(v, axis=0, keepdims=True)


def _rstd(x):
    return lax.rsqrt(jnp.mean(x * x, axis=-1, keepdims=True) + EPS)


def _rms_bwd(x, g, dy):
    r = _rstd(x)
    xh = x * r
    dxh = dy * g
    dx = r * (dxh - xh * jnp.mean(dxh * xh, axis=-1, keepdims=True))
    return dx, _colsum(dy * xh)


def _sigmoid(z):
    return 1.0 / (1.0 + jnp.exp(-z))


_GELU_K = math.sqrt(2.0 / math.pi)


def _gelu_parts(g):
    t = jnp.tanh(_GELU_K * (g + 0.044715 * g * g * g))
    gel = 0.5 * g * (1.0 + t)
    dgel = 0.5 * (1.0 + t) + 0.5 * g * (1.0 - t * t) * (_GELU_K * (1.0 + 3.0 * 0.044715 * g * g))
    return gel, dgel


def _diag_visible(t, unit):
    rows = lax.broadcasted_iota(jnp.int32, (t, t), 0)
    cols = lax.broadcasted_iota(jnp.int32, (t, t), 1)
    if unit > 1:
        sh = int(math.log2(unit))
        assert 1 << sh == unit and t % unit == 0
        rows, cols = jnp.right_shift(rows, sh), jnp.right_shift(cols, sh)
    return cols <= rows


def _lane_pick(tile, lane):
    idx = lax.broadcasted_iota(jnp.int32, tile.shape, 1)
    return jnp.sum(jnp.where(idx == lane, tile, 0.0), axis=1, keepdims=True)


def _lane_put(tile, lane, col):
    idx = lax.broadcasted_iota(jnp.int32, tile.shape, 1)
    return jnp.where(idx == lane, col, tile)


def _head_cat(refs, shared, rows, h):
    hs = slice(h * LANES, (h + 1) * LANES)
    vals = [(r[rows, :] if sh else r[rows, hs]).astype(BF16) for r, sh in zip(refs, shared)]
    return vals[0] if len(vals) == 1 else jnp.concatenate(vals, axis=1)


def _blk_rows(i, t):
    return pl.ds(pl.multiple_of(i * t, t), t)


def _piece_specs(pieces, rows, row_idx):
    return [pl.BlockSpec((rows, LANES if sh else HEADS * LANES), functools.partial(lambda b, i, cb: (row_idx(b, i), cb), cb=cb))
            for _, cb, sh in pieces]


def _attn_fwd(name, qp, kp, vp, bias, unit, scale, n_seq, seq, t, side=None):
    nb = seq // t
    n_tok = n_seq * seq
    nq, nk_p = len(qp), len(kp)
    q_sh, k_sh = [p[2] for p in qp], [p[2] for p in kp]
    nbias = 2 if bias is not None else 0

    def body(*refs):
        q_refs, k_refs = refs[:nq], refs[nq:nq + nk_p]
        v_ref = refs[nq + nk_p]
        bias_refs = refs[nq + nk_p + 1:nq + nk_p + 1 + nbias]
        o_ref, lse_ref = refs[nq + nk_p + 1 + nbias:]
        qi = pl.program_id(1)
        lse_tile = jnp.zeros((t, LANES), F32)
        for h in range(HEADS):
            hs = slice(h * LANES, (h + 1) * LANES)
            q = _head_cat(q_refs, q_sh, slice(None), h)
            cq = _lane_pick(bias_refs[0][...], ROPE + h) if bias is not None else None

            def block(kb, carry, diag, h=h, hs=hs, q=q, cq=cq):
                m, l, acc = carry
                rows = _blk_rows(kb, t)
                s = lax.dot_general(q, _head_cat(k_refs, k_sh, rows, h), _DIMS["nt"], preferred_element_type=F32) * scale
                if bias is not None:
                    s = s + cq - bias_refs[1][kb, h:h + 1, :]
                if diag:
                    s = jnp.where(_diag_visible(t, unit), s, NEG_INF)
                m_new = jnp.maximum(m, jnp.max(s, axis=1, keepdims=True))
                alpha = jnp.exp(m - m_new)
                p = jnp.exp(s - m_new)
                l = alpha * l + jnp.sum(p, axis=1, keepdims=True)
                acc = alpha * acc + jnp.dot(p.astype(BF16), v_ref[rows, hs].astype(BF16), preferred_element_type=F32)
                return m_new, l, acc

            init = (jnp.full((t, 1), NEG_INF, F32), jnp.zeros((t, 1), F32), jnp.zeros((t, LANES), F32))
            carry = lax.fori_loop(0, qi, lambda kb, c: block(kb, c, False), init)
            m, l, acc = block(qi, carry, True)
            o_ref[:, hs] = acc / l
            lse_tile = _lane_put(lse_tile, h, m + jnp.log(l))
        lse_ref[...] = lse_tile

    tile_row = lambda b, i: b * nb + i
    seq_row = lambda b, i: b
    lane_tile = pl.BlockSpec((t, LANES), lambda b, i: (b * nb + i, 0))
    in_specs = _piece_specs(qp, t, tile_row) + _piece_specs(kp, seq, seq_row) + _piece_specs([vp + (False,)], seq, seq_row)
    args = [p[0] for p in qp] + [p[0] for p in kp] + [vp[0]]
    if bias is not None:
        in_specs += [lane_tile, pl.BlockSpec((None, nb, HEADS, t), lambda b, i: (b, 0, 0, 0))]
        args += list(bias)
    return _hosted_call(
        body, name, (n_seq, nb), in_specs,
        [pl.BlockSpec((t, HEADS * LANES), lambda b, i: (b * nb + i, 0)), lane_tile],
        [jax.ShapeDtypeStruct((n_tok, HEADS * LANES), F32), jax.ShapeDtypeStruct((n_tok, LANES), F32)], args, side)


HEAD_GROUPS = 2


def _attn_delta(name, o, do, n_tok):
    def fn(r, v):
        o_v, do_v = r
        tile = jnp.zeros((o_v.shape[0], LANES), F32)
        for h in range(HEADS):
            hs = slice(h * LANES, (h + 1) * LANES)
            tile = _lane_put(tile, h, jnp.sum(do_v[:, hs] * o_v[:, hs], axis=1, keepdims=True))
        return [tile, do_v], []

    return _rows(name, fn, [(o, HEADS * LANES, 0), (do, HEADS * LANES, 0)], [], [(LANES, F32), (HEADS * LANES, BF16)], [], n_tok)


def _attn_bwd(name, qp, kp, vp, dob, lse, delta, bias, unit, scale, n_seq, seq, t, grad_dtype, side=None):
    nb = seq // t
    n_tok = n_seq * seq
    ng = HEAD_GROUPS
    hg = HEADS // ng
    gw = hg * LANES
    nq, nk_p = len(qp), len(kp)
    q_sh, k_sh = [p[2] for p in qp], [p[2] for p in kp]
    assert not any(q_sh) and nq == nk_p
    nbias = 2 if bias is not None else 0
    n_in = nq + nk_p + 4 + nbias
    n_out = nq + nk_p + 1 + nbias

    def body(*refs):
        q_refs, k_refs = refs[:nq], refs[nq:nq + nk_p]
        v_ref, dob_ref, lse_ref, delta_ref = refs[nq + nk_p:nq + nk_p + 4]
        bias_refs = refs[nq + nk_p + 4:n_in]
        dq_refs, dk_refs = refs[n_in:n_in + nq], refs[n_in + nq:n_in + nq + nk_p]
        dv_ref = refs[n_in + nq + nk_p]
        dq_s, dcq_s = refs[n_in + n_out:]
        g, ki = pl.program_id(1), pl.program_id(2)

        @pl.when(ki == 0)
        def _():
            dq_s[...] = jnp.zeros_like(dq_s)
            dcq_s[...] = jnp.zeros_like(dcq_s)

        shared_acc = [jnp.zeros((t, LANES), F32) for _ in range(nk_p)]
        for hl in range(hg):
            h = g * hg + hl
            hs = slice(hl * LANES, (hl + 1) * LANES)
            k = _head_cat(k_refs, k_sh, slice(None), hl)
            v = v_ref[:, hs].astype(BF16)
            ck = bias_refs[1][pl.ds(h, 1), :] if bias is not None else None

            def block(qb, carry, diag, h=h, hl=hl, hs=hs, k=k, v=v, ck=ck):
                dk_acc, dv_acc, dc_acc = carry
                rows = _blk_rows(qb, t)
                q = _head_cat(q_refs, q_sh, rows, hl)
                s = lax.dot_general(q, k, _DIMS["nt"], preferred_element_type=F32) * scale
                if bias is not None:
                    s = s + _lane_pick(bias_refs[0][rows, :], ROPE + h) - ck
                if diag:
                    s = jnp.where(_diag_visible(t, unit), s, NEG_INF)
                p = jnp.exp(s - _lane_pick(lse_ref[rows, :], h))
                do_b = dob_ref[rows, hs]
                dp = lax.dot_general(do_b, v, _DIMS["nt"], preferred_element_type=F32)
                ds = p * (dp - _lane_pick(delta_ref[rows, :], h))
                ds_b = ds.astype(BF16)
                dq_blk = jnp.dot(ds_b, k, preferred_element_type=F32)
                for n_p in range(nq):
                    dq_s[rows, n_p * gw + hl * LANES:n_p * gw + (hl + 1) * LANES] += dq_blk[:, n_p * LANES:(n_p + 1) * LANES]
                if bias is not None:
                    lane = lax.broadcasted_iota(jnp.int32, (t, LANES), 1)
                    dcq_s[rows, :] += jnp.where(lane == ROPE + h, jnp.sum(ds, axis=1, keepdims=True), 0.0)
                return (dk_acc + lax.dot_general(ds_b, q, _DIMS["tn"], preferred_element_type=F32),
                        dv_acc + lax.dot_general(p.astype(BF16), do_b, _DIMS["tn"], preferred_element_type=F32),
                        dc_acc - jnp.sum(ds, axis=0, keepdims=True))

            init = (jnp.zeros((t, nk_p * LANES), F32), jnp.zeros((t, LANES), F32), jnp.zeros((1, t), F32))
            carry = block(ki, init, True)
            dk_acc, dv_acc, dc_acc = lax.fori_loop(ki + 1, nb, lambda qb, c: block(qb, c, False), carry)
            for n_p in range(nk_p):
                part = dk_acc[:, n_p * LANES:(n_p + 1) * LANES] * scale
                if k_sh[n_p]:
                    shared_acc[n_p] = shared_acc[n_p] + part
                else:
                    dk_refs[n_p][:, hs] = part.astype(grad_dtype)
            dv_ref[:, hs] = dv_acc.astype(grad_dtype)
            if bias is not None:
                refs[n_in + n_out - 1][hl:hl + 1, :] = dc_acc
        for n_p in range(nk_p):
            if k_sh[n_p]:
                dk_refs[n_p][...] = shared_acc[n_p]

        @pl.when(ki == nb - 1)
        def _():
            for n_p in range(nq):
                dq_refs[n_p][...] = (dq_s[:, n_p * gw:(n_p + 1) * gw] * scale).astype(grad_dtype)
            if bias is not None:
                refs[n_in + n_out - 2][...] = dcq_s[...]

    def spec(rows, row_idx, cb, shared):
        if shared:
            return pl.BlockSpec((rows, LANES), lambda b, g, i: (row_idx(b, i), cb))
        return pl.BlockSpec((rows, gw), lambda b, g, i: (row_idx(b, i), cb * ng + g))

    tile_row = lambda b, i: b * nb + i
    seq_row = lambda b, i: b
    lane_seq = pl.BlockSpec((seq, LANES), lambda b, g, i: (b, 0))
    in_specs = [spec(seq, seq_row, cb, sh) for _, cb, sh in qp] + [spec(t, tile_row, cb, sh) for _, cb, sh in kp]
    in_specs += [spec(t, tile_row, vp[1], False), spec(seq, seq_row, 0, False), lane_seq, lane_seq]
    args = [p[0] for p in qp] + [p[0] for p in kp] + [vp[0], dob, lse, delta]
    if bias is not None:
        in_specs += [lane_seq, pl.BlockSpec((None, None, HEADS, t), lambda b, g, i: (b, i, 0, 0))]
        args += list(bias)
    group_tile = pl.BlockSpec((None, t, LANES), lambda b, g, i: (g, b * nb + i, 0))
    out_specs = [spec(seq, seq_row, 0, False)] * nq
    out_specs += [group_tile if sh else spec(t, tile_row, 0, False) for sh in k_sh] + [spec(t, tile_row, 0, False)]
    head_shape = jax.ShapeDtypeStruct((n_tok, HEADS * LANES), grad_dtype)
    out_shape = [head_shape] * nq + [jax.ShapeDtypeStruct((ng, n_tok, LANES), F32) if sh else head_shape for sh in k_sh]
    out_shape.append(head_shape)
    if bias is not None:
        out_specs += [pl.BlockSpec((None, seq, LANES), lambda b, g, i: (g, b, 0)),
                      pl.BlockSpec((None, None, None, hg, t), lambda b, g, i: (b, g, i, 0, 0))]
        out_shape += [jax.ShapeDtypeStruct((ng, n_tok, LANES), F32), jax.ShapeDtypeStruct((n_seq, ng, nb, hg, t), F32)]
    return _hosted_call(body, name, (n_seq, ng, nb), in_specs, out_specs, out_shape, args, side,
                        semantics=("parallel", "arbitrary", "arbitrary"),
                        scratch=[pltpu.VMEM((seq, nq * gw), F32), pltpu.VMEM((seq, LANES), F32)])


def _seq_cumsum(name, x, col_block, n_seq, seq, reverse, pre=None, vec=None):
    t = _tile(seq, 256, 128)
    nb = seq // t

    def body(*refs):
        x_ref = refs[0]
        vec_ref = refs[1] if vec is not None else None
        o_ref, carry = refs[-2], refs[-1]

        @pl.when(pl.program_id(1) == 0)
        def _():
            carry[...] = jnp.zeros_like(carry)

        v = x_ref[...]
        if pre is not None:
            v = pre(v, vec_ref[...])
        r = lax.broadcasted_iota(jnp.int32, (t, t), 0)
        c = lax.broadcasted_iota(jnp.int32, (t, t), 1)
        tri = jnp.where((c >= r) if reverse else (c <= r), 1.0, 0.0).astype(BF16)
        hi = v.astype(BF16)
        mid = (v - hi.astype(F32)).astype(BF16)
        lo = (v - hi.astype(F32) - mid.astype(F32)).astype(BF16)
        acc = jnp.dot(tri, hi, preferred_element_type=F32)
        acc += jnp.dot(tri, mid, preferred_element_type=F32)
        acc += jnp.dot(tri, lo, preferred_element_type=F32)
        o_ref[...] = acc + carry[...]
        carry[...] += _colsum(v)

    blk = (lambda b, i: (b * nb + nb - 1 - i)) if reverse else (lambda b, i: (b * nb + i))
    in_specs = [pl.BlockSpec((t, LANES), lambda b, i: (blk(b, i), col_block))]
    args = [x]
    if vec is not None:
        in_specs.append(pl.BlockSpec(vec.shape, lambda b, i: (0, 0)))
        args.append(vec)
    return pl.pallas_call(
        body, name=name, grid=(n_seq, nb), in_specs=in_specs,
        out_specs=pl.BlockSpec((t, LANES), lambda b, i: (blk(b, i), 0)),
        out_shape=jax.ShapeDtypeStruct((n_seq * seq, LANES), F32),
        scratch_shapes=[pltpu.VMEM((1, LANES), F32)],
        compiler_params=_params(("arbitrary", "arbitrary")),
    )(*args)


def _log_sigmoid(z):
    return -(jnp.maximum(-z, 0.0) + jnp.log(1.0 + jnp.exp(-jnp.abs(z))))


def _shift_down(u, prev_ref, n):
    out = pltpu.roll(u, n, 0)
    row = lax.broadcasted_iota(jnp.int32, u.shape, 0)
    for r in range(n):
        out = jnp.where(row == r, prev_ref[8 - n + r:8 - n + r + 1, :], out)
    return out


def _shift_up(u, next_ref, n):
    ts = u.shape[0]
    out = pltpu.roll(u, ts - n, 0)
    row = lax.broadcasted_iota(jnp.int32, u.shape, 0)
    for r in range(n):
        out = jnp.where(row == ts - n + r, next_ref[r:r + 1, :], out)
    return out


def _conv_taps(u, prev_ref, w_ref, b_ref):
    s1, s2 = _shift_down(u, prev_ref, 1), _shift_down(u, prev_ref, 2)
    return (w_ref[0:1, :] * s2 + w_ref[1:2, :] * s1 + w_ref[2:3, :] * u) + b_ref[...], s1, s2


def _conv_glu_fwd(u_il, cw_il, cb_il, n_seq, seq, wt):
    n_tok, two_f = u_il.shape
    nct = two_f // (2 * wt)
    ts = _tile(seq, 256, 8)
    ns = seq // ts

    def body(u_ref, w_ref, b_ref, a_ref, carry):
        @pl.when(pl.program_id(2) == 0)
        def _():
            carry[...] = jnp.zeros_like(carry)

        u = u_ref[...].astype(F32)
        uc, _, _ = _conv_taps(u, carry, w_ref, b_ref)
        gel, _ = _gelu_parts(uc[:, :wt])
        a_ref[...] = (gel * uc[:, wt:]).astype(a_ref.dtype)
        carry[...] = u[ts - 8:, :]

    return pl.pallas_call(
        body, name="conv_glu_fwd", grid=(nct, n_seq, ns),
        in_specs=[pl.BlockSpec((ts, 2 * wt), lambda j, b, s: (b * ns + s, j)),
                  pl.BlockSpec((3, 2 * wt), lambda j, b, s: (0, j)),
                  pl.BlockSpec((1, 2 * wt), lambda j, b, s: (0, j))],
        out_specs=pl.BlockSpec((ts, wt), lambda j, b, s: (b * ns + s, j)),
        out_shape=jax.ShapeDtypeStruct((n_tok, two_f // 2), BF16),
        scratch_shapes=[pltpu.VMEM((8, 2 * wt), F32)],
        compiler_params=_params(("parallel", "arbitrary", "arbitrary")),
    )(u_il, cw_il, cb_il)


def _conv_glu_bwd_pre(u_il, da, cw_il, cb_il, n_seq, seq, wt):
    n_tok, two_f = u_il.shape
    nct = two_f // (2 * wt)
    ts = _tile(seq, 256, 8)
    ns = seq // ts

    def body(u_ref, da_ref, w_ref, b_ref, d_ref, acc_ref, carry):
        first = jnp.logical_and(pl.program_id(1) == 0, pl.program_id(2) == 0)

        @pl.when(first)
        def _():
            acc_ref[...] = jnp.zeros_like(acc_ref)

        @pl.when(pl.program_id(2) == 0)
        def _():
            carry[...] = jnp.zeros_like(carry)

        u = u_ref[...].astype(F32)
        uc, s1, s2 = _conv_taps(u, carry, w_ref, b_ref)
        gel, dgel = _gelu_parts(uc[:, :wt])
        da_v = da_ref[...].astype(F32)
        d = jnp.concatenate([da_v * uc[:, wt:] * dgel, da_v * gel], axis=1)
        d_ref[...] = d.astype(d_ref.dtype)
        acc_ref[0:1, :] += _colsum(d * s2)
        acc_ref[1:2, :] += _colsum(d * s1)
        acc_ref[2:3, :] += _colsum(d * u)
        acc_ref[3:4, :] += _colsum(d)
        carry[...] = u[ts - 8:, :]

    return pl.pallas_call(
        body, name="conv_glu_bwd_pre", grid=(nct, n_seq, ns),
        in_specs=[pl.BlockSpec((ts, 2 * wt), lambda j, b, s: (b * ns + s, j)),
                  pl.BlockSpec((ts, wt), lambda j, b, s: (b * ns + s, j)),
                  pl.BlockSpec((3, 2 * wt), lambda j, b, s: (0, j)),
                  pl.BlockSpec((1, 2 * wt), lambda j, b, s: (0, j))],
        out_specs=[pl.BlockSpec((ts, 2 * wt), lambda j, b, s: (b * ns + s, j)),
                   pl.BlockSpec((8, 2 * wt), lambda j, b, s: (0, j))],
        out_shape=[jax.ShapeDtypeStruct((n_tok, two_f), BF16), jax.ShapeDtypeStruct((8, two_f), F32)],
        scratch_shapes=[pltpu.VMEM((8, 2 * wt), F32)],
        compiler_params=_params(("parallel", "arbitrary", "arbitrary")),
    )(u_il, da, cw_il, cb_il)


def _conv_bwd_input(d_il, cw_il, n_seq, seq, wt):
    n_tok, two_f = d_il.shape
    nct = two_f // (2 * wt)
    ts = _tile(seq, 256, 8)
    ns = seq // ts

    def body(d_ref, w_ref, o_ref, carry):
        @pl.when(pl.program_id(2) == 0)
        def _():
            carry[...] = jnp.zeros_like(carry)

        d = d_ref[...].astype(F32)
        o_ref[...] = (w_ref[2:3, :] * d + w_ref[1:2, :] * _shift_up(d, carry, 1)
                      + w_ref[0:1, :] * _shift_up(d, carry, 2)).astype(o_ref.dtype)
        carry[...] = d[:8, :]

    rev = lambda j, b, s: (b * ns + ns - 1 - s, j)
    return pl.pallas_call(
        body, name="conv_bwd_input", grid=(nct, n_seq, ns),
        in_specs=[pl.BlockSpec((ts, 2 * wt), rev), pl.BlockSpec((3, 2 * wt), lambda j, b, s: (0, j))],
        out_specs=pl.BlockSpec((ts, 2 * wt), rev),
        out_shape=jax.ShapeDtypeStruct((n_tok, two_f), BF16),
        scratch_shapes=[pltpu.VMEM((8, 2 * wt), F32)],
        compiler_params=_params(("parallel", "arbitrary", "arbitrary")),
    )(d_il, cw_il)


HBM = pl.BlockSpec(memory_space=pltpu.HBM)
_CHIP_FLIPS = ((1, 0), (0, 1), (1, 1))


def _place():
    x, y, c = lax.axis_index("x"), lax.axis_index("y"), lax.axis_index("c")
    return x, y, c, 2 * x + y


def _flip(v, f):
    return 1 - v if f else v


def _half_rows(c, half):
    return pl.ds(pl.multiple_of(c * half, 16), half)


def _remote(src, dst, ssem, rsem, dev):
    return pltpu.make_async_remote_copy(src_ref=src, dst_ref=dst, send_sem=ssem, recv_sem=rsem,
                                        device_id=dev, device_id_type=MESH)


def _comm_call(name, body, ins, out_shapes, n_sems):
    return pl.pallas_call(
        body, name=name, in_specs=[HBM] * len(ins), out_specs=[HBM] * len(out_shapes),
        out_shape=[pltpu.HBM(s.shape, s.dtype) for s in out_shapes],
        scratch_shapes=[pltpu.SemaphoreType.DMA((n_sems,)), pltpu.SemaphoreType.DMA((n_sems,))],
    )(*ins)


def _all_gather_weights(shards, smalls):
    side = _gather_side(shards, smalls)
    nt = len(shards) + len(smalls)

    def body(*refs):
        for part in (side.start, side.mid, side.end):
            part(refs[:nt], refs[nt:2 * nt], *refs[2 * nt:])

    res = _comm_call("all_gather_weights", body, side.ins, side.outs, side.n_sems)
    return res[:len(shards)], res[len(shards):]


def _pair_split(name, grads):
    n = len(grads)

    def body(*refs):
        src, got = refs[:n], refs[n:2 * n]
        ssem, rsem = refs[2 * n:]
        x, y, c, _ = _place()
        cps = []
        for w in range(n):
            half = grads[w].shape[1] // 2
            cp = _remote(src[w].at[:, _half_rows(1 - c, half)], got[w], ssem.at[w], rsem.at[w], (x, y, 1 - c))
            cp.start()
            cps.append(cp)
        for cp in cps:
            cp.wait()

    outs = [jax.ShapeDtypeStruct((g.shape[0], g.shape[1] // 2, g.shape[2]), g.dtype) for g in grads]
    return _comm_call(name, body, grads, outs, n)


class _Side:
    def __init__(self, ins, outs, n_sems, start, mid, end, mid_step=None):
        self.ins, self.outs, self.n_sems = list(ins), list(outs), n_sems
        self.start, self.mid, self.end, self.mid_step = start, mid, end, mid_step
        self.aliases = {}


def _scatter_side(parts):
    n = len(parts)

    def copies(src, dst, ssem, rsem):
        x, y, c, _ = _place()
        out = []
        for w in range(n):
            for k, (fx, fy) in enumerate(_CHIP_FLIPS):
                px, py = _flip(x, fx), _flip(y, fy)
                out.append(_remote(src[w].at[2 * px + py], dst[w].at[k], ssem.at[w * 3 + k], rsem.at[w * 3 + k], (px, py, c)))
        return out

    def start(src, dst, ssem, rsem):
        for cp in copies(src, dst, ssem, rsem):
            cp.start()

    def end(src, dst, ssem, rsem):
        for cp in copies(src, dst, ssem, rsem):
            cp.wait()

    outs = [jax.ShapeDtypeStruct((3,) + p.shape[1:], p.dtype) for p in parts]
    return _Side(parts, outs, 3 * n, start, None, end)


def _gather_side(shards, smalls, mid_step=None, into=None):
    n, ns = len(shards), len(smalls)
    into = into or [(None, a.shape[0], 0) for a in shards]

    def dst_rows(w, c):
        half = shards[w].shape[0] // 2
        return pl.ds(pl.multiple_of(into[w][2] + c * half, 16), half)

    def ici(src, dst, ssem, rsem, w, k):
        x, y, c, me = _place()
        fx, fy = _CHIP_FLIPS[k]
        rows = _half_rows(c, shards[w].shape[0] // 2)
        return _remote(src[w].at[rows], dst[w].at[me, dst_rows(w, c)], ssem.at[w * 6 + k], rsem.at[w * 6 + k],
                       (_flip(x, fx), _flip(y, fy), c))

    def small(src, dst, ssem, rsem, s, k):
        x, y, c, me = _place()
        fx, fy = _CHIP_FLIPS[k]
        sem = 6 * n + 3 * s + k
        return _remote(src[n + s], dst[n + s].at[me], ssem.at[sem], rsem.at[sem], (_flip(x, fx), _flip(y, fy), c))

    def own(src, dst, ssem, rsem, i):
        x, y, c, me = _place()
        sem = 6 * n + 3 * ns + i
        if i < n:
            to = dst[i].at[me, pl.ds(into[i][2], shards[i].shape[0])]
        else:
            to = dst[i].at[me]
        return _remote(src[i], to, ssem.at[sem], rsem.at[sem], (x, y, 1 - c))

    def landed(dst, ssem, rsem, w, k, sender_c, sem_off):
        x, y, c, _ = _place()
        fx, fy = _CHIP_FLIPS[k]
        got = dst[w].at[2 * _flip(x, fx) + _flip(y, fy), dst_rows(w, sender_c)]
        return _remote(got, got, ssem.at[w * 6 + sem_off + k], rsem.at[w * 6 + sem_off + k], (x, y, 1 - c))

    def start(src, dst, ssem, rsem):
        for i in range(n + ns):
            own(src, dst, ssem, rsem, i).start()
        for s in range(ns):
            for k in range(3):
                small(src, dst, ssem, rsem, s, k).start()
        for w in range(n):
            for k in range(3):
                ici(src, dst, ssem, rsem, w, k).start()

    def mid(src, dst, ssem, rsem):
        c = lax.axis_index("c")
        for w in range(n):
            for k in range(3):
                landed(dst, ssem, rsem, w, k, c, 0).wait_recv()
                landed(dst, ssem, rsem, w, k, c, 3).start()

    def end(src, dst, ssem, rsem):
        c = lax.axis_index("c")
        for w in range(n):
            for k in range(3):
                landed(dst, ssem, rsem, w, k, 1 - c, 3).wait_recv()
        for i in range(n + ns):
            own(src, dst, ssem, rsem, i).wait()
        for s in range(ns):
            for k in range(3):
                small(src, dst, ssem, rsem, s, k).wait()
        for w in range(n):
            for k in range(3):
                ici(src, dst, ssem, rsem, w, k).wait_send()
                landed(dst, ssem, rsem, w, k, c, 3).wait_send()

    outs = [jax.ShapeDtypeStruct((N_CHIPS, rows, a.shape[1]), a.dtype) for a, (_, rows, _) in zip(shards, into)]
    outs += [jax.ShapeDtypeStruct((N_CHIPS,) + a.shape, a.dtype) for a in smalls]
    filled = [(w, arr) for w, (arr, _, _) in enumerate(into) if arr is not None]
    side = _Side(list(shards) + list(smalls) + [arr for _, arr in filled], outs, 7 * n + 4 * ns, start, mid, end, mid_step)
    side.aliases = {n + ns + i: w for i, (w, _) in enumerate(filled)}
    return side


def _host(body, n_in, n_out, side, grid):
    if side is None:
        return body
    ns_in, ns_out = len(side.ins), len(side.outs)
    n_steps = math.prod(grid)
    mid_step = side.mid_step
    if side.mid is not None and not isinstance(mid_step, int):
        mid_step = min(n_steps - 1, int(mid_step * n_steps))

    def wrapped(*refs):
        ins, s_ins = refs[:n_in], refs[n_in:n_in + ns_in]
        outs = refs[n_in + ns_in:n_in + ns_in + n_out]
        s_outs = refs[n_in + ns_in + n_out:n_in + ns_in + n_out + ns_out]
        rest = refs[n_in + ns_in + n_out + ns_out:]
        sems = rest[-2:]
        step = 0
        for axis, extent in enumerate(grid):
            step = step * extent + pl.program_id(axis)

        @pl.when(step == 0)
        def _():
            side.start(s_ins, s_outs, *sems)

        if side.mid is not None:
            @pl.when(step == mid_step)
            def _():
                side.mid(s_ins, s_outs, *sems)

        body(*ins, *outs, *rest[:-2])

        @pl.when(step == n_steps - 1)
        def _():
            side.end(s_ins, s_outs, *sems)

    return wrapped


def _hosted_call(body, name, grid, in_specs, out_specs, out_shape, args, side, semantics=("parallel", "arbitrary"),
                 scratch=()):
    n_in, n_out = len(in_specs), len(out_specs)
    kern = _host(body, n_in, n_out, side, grid)
    if side is None:
        return pl.pallas_call(kern, name=name, grid=grid, in_specs=in_specs, out_specs=out_specs, out_shape=out_shape,
                              scratch_shapes=list(scratch), compiler_params=_params(semantics))(*args), []
    res = pl.pallas_call(
        kern, name=name, grid=grid, in_specs=in_specs + [HBM] * len(side.ins), out_specs=out_specs + [HBM] * len(side.outs),
        out_shape=list(out_shape) + [pltpu.HBM(s.shape, s.dtype) for s in side.outs],
        scratch_shapes=list(scratch) + [pltpu.SemaphoreType.DMA((side.n_sems,)), pltpu.SemaphoreType.DMA((side.n_sems,))],
        input_output_aliases={n_in + i: n_out + o for i, o in side.aliases.items()},
        compiler_params=_params(("arbitrary",) * len(grid)),
    )(*args, *side.ins)
    return res[:n_out], res[n_out:]


def _pair_swap(halves):
    n = len(halves)

    def body(*refs):
        src, dst = refs[:n], refs[n:2 * n]
        ssem, rsem = refs[2 * n:]
        x, y, c, _ = _place()
        cps = []
        for w in range(n):
            cp = _remote(src[w], dst[w], ssem.at[w], rsem.at[w], (x, y, 1 - c))
            cp.start()
            cps.append(cp)
        for cp in cps:
            cp.wait()

    outs = [jax.ShapeDtypeStruct(h.shape, h.dtype) for h in halves]
    return _comm_call("rs_pair_swap", body, halves, outs, n)


def _gather_small(vec):
    def body(src, dst, ssem, rsem):
        x, y, c, _ = _place()
        me = 4 * x + 2 * y + c
        cps = []
        for r in range(1, 8):
            dev = (_flip(x, r & 4), _flip(y, r & 2), _flip(c, r & 1))
            cp = _remote(src, dst.at[me], ssem.at[r - 1], rsem.at[r - 1], dev)
            cp.start()
            cps.append(cp)
        for cp in cps:
            cp.wait()

    out = jax.ShapeDtypeStruct((8,) + vec.shape, vec.dtype)
    return _comm_call("gather_small", body, [vec], [out], 7)[0]


def _pair_add(name, g, theirs, core):
    n, half, b = theirs.shape
    tr = _tile(half, 256, 16)
    nt = half // tr

    def body(c_ref, g_ref, t_ref, o_ref):
        o_ref[...] = (g_ref[...].astype(F32) + t_ref[...].astype(F32)).astype(o_ref.dtype)

    same = pl.BlockSpec((None, tr, b), lambda j, i, c: (j, i, 0))
    grid_spec = pltpu.PrefetchScalarGridSpec(
        num_scalar_prefetch=1, grid=(n, nt),
        in_specs=[pl.BlockSpec((None, tr, b), lambda j, i, c: (j, c[0] * nt + i, 0)), same], out_specs=same)
    return pl.pallas_call(body, name=name, grid_spec=grid_spec, out_shape=jax.ShapeDtypeStruct(theirs.shape, BF16),
                          compiler_params=_params(("parallel", "parallel")))(core, g, theirs)


def _chip_sum(name, parts, landed, chip1):
    n, r, c = landed.shape
    tr = _tile(r, 256, 16)

    def body(ix_ref, p_ref, s_ref, o_ref):
        acc = p_ref[...].astype(F32)
        for s in range(n):
            acc = acc + s_ref[s].astype(F32)
        o_ref[...] = acc

    grid_spec = pltpu.PrefetchScalarGridSpec(
        num_scalar_prefetch=1, grid=(r // tr,),
        in_specs=[pl.BlockSpec((None, tr, c), lambda i, ix: (ix[0], i, 0)), pl.BlockSpec((n, tr, c), lambda i, ix: (0, i, 0))],
        out_specs=pl.BlockSpec((tr, c), lambda i, ix: (i, 0)))
    return pl.pallas_call(body, name=name, grid_spec=grid_spec, out_shape=jax.ShapeDtypeStruct((r, c), F32),
                          compiler_params=_params(("parallel",)))(chip1, parts, landed)


def _sum_slots(name, stacked):
    n, r, c = stacked.shape
    tr = _tile(r, 256, 8)

    def body(s_ref, o_ref):
        acc = s_ref[0]
        for s in range(1, n):
            acc = acc + s_ref[s]
        o_ref[...] = acc

    return pl.pallas_call(
        body, name=name, grid=(r // tr,), in_specs=[pl.BlockSpec((n, tr, c), lambda i: (0, i, 0))],
        out_specs=pl.BlockSpec((tr, c), lambda i: (i, 0)),
        out_shape=jax.ShapeDtypeStruct((r, c), F32), compiler_params=_params(("parallel",)),
    )(stacked)


def _adam_math(w, g, m, v):
    bc1, bc2 = 1.0 - ADAM_B1 ** ADAM_STEP, 1.0 - ADAM_B2 ** ADAM_STEP
    nm = ADAM_B1 * m + (1.0 - ADAM_B1) * g
    nv = ADAM_B2 * v + (1.0 - ADAM_B2) * (g * g)
    return -ADAM_LR * ((nm / bc1) / (jnp.sqrt(nv / bc2) + ADAM_EPS) + ADAM_WD * w), nm, nv


def _adamw_halves(name, w, g_mine, g_theirs, m, v, core):
    r, c = w.shape
    h = r // 2
    tr = _tile(h, 128, 8)
    nth = h // tr

    def body(c_ref, w_ref, gm_ref, gt_ref, m_ref, v_ref, g_ref, d_ref, nm_ref, nv_ref):
        g = jnp.where(pl.program_id(0) // nth == c_ref[0], gm_ref[...], gt_ref[...])
        g_ref[...] = g
        d_ref[...], nm_ref[...], nv_ref[...] = _adam_math(w_ref[...], g, m_ref[...], v_ref[...])

    full = pl.BlockSpec((tr, c), lambda i, cr: (i, 0))
    half = pl.BlockSpec((tr, c), lambda i, cr: (i % nth, 0))
    grid_spec = pltpu.PrefetchScalarGridSpec(num_scalar_prefetch=1, grid=(r // tr,),
                                             in_specs=[full, half, half, full, full], out_specs=[full] * 4)
    return pl.pallas_call(body, name=name, grid_spec=grid_spec, out_shape=[jax.ShapeDtypeStruct((r, c), F32)] * 4,
                          compiler_params=_params(("parallel",)))(core, w, g_mine, g_theirs, m, v)


def _adamw(name, w, g, m, v):
    r, c = w.shape
    by_cols = r % 8 != 0 and c % LANES == 0
    tr, tc = (r, _tile(c, 256, LANES)) if by_cols else (_tile(r, 256, 8), c)

    def body(w_ref, g_ref, m_ref, v_ref, d_ref, nm_ref, nv_ref):
        d_ref[...], nm_ref[...], nv_ref[...] = _adam_math(w_ref[...], g_ref[...], m_ref[...], v_ref[...])

    spec = pl.BlockSpec((tr, tc), (lambda i: (0, i)) if by_cols else (lambda i: (i, 0)))
    return pl.pallas_call(
        body, name=name, grid=(c // tc if by_cols else r // tr,), in_specs=[spec] * 4, out_specs=[spec] * 3,
        out_shape=[jax.ShapeDtypeStruct((r, c), F32)] * 3, compiler_params=_params(("parallel",)),
    )(w, g, m, v)


def _pad_cols(a, cols):
    return jnp.pad(a, ((0, 0), (0, cols - a.shape[1])))


def _rot_cols(w):
    h = w.shape[-1] // 2
    return jnp.concatenate([-w[..., h:], w[..., :h]], axis=-1)


def _unrot_cols(d):
    h = d.shape[-1] // 2
    return jnp.concatenate([d[..., h:], -d[..., :h]], axis=-1)


def _logical(g):
    return jnp.transpose(g, (1, 0, 2)).reshape(g.shape[1], N_CHIPS * g.shape[2])


def _chunks(a, n):
    return jnp.transpose(a.reshape(a.shape[0], N_CHIPS, n), (1, 0, 2))


def kernel(x, positions, pre_mix_norm, w_in, q_a_norm, w_uq, kv_a_norm, w_ukv, b_forget, b_gate, w_branch_mla, w_branch_fox, w_out, post_mix_norm, pre_ffn_norm, w_up, conv_w, conv_b, w_down, post_ffn_norm, loss_target, m_pre_mix_norm, m_w_in, m_q_a_norm, m_w_uq, m_kv_a_norm, m_w_ukv, m_b_forget, m_b_gate, m_w_branch_mla, m_w_branch_fox, m_w_out, m_post_mix_norm, m_pre_ffn_norm, m_w_up, m_conv_w, m_conv_b, m_w_down, m_post_ffn_norm, v_pre_mix_norm, v_w_in, v_q_a_norm, v_w_uq, v_kv_a_norm, v_w_ukv, v_b_forget, v_b_gate, v_w_branch_mla, v_w_branch_fox, v_w_out, v_post_mix_norm, v_pre_ffn_norm, v_w_up, v_conv_w, v_conv_b, v_w_down, v_post_ffn_norm):
    n_seq, seq, d = x.shape
    n_tok = n_seq * seq
    d_in = N_CHIPS * w_in.shape[1]
    two_f = N_CHIPS * w_up.shape[1]
    ff_dim = two_f // 2
    assert d_in == QL + KVL + ROPE + 3 * HEADS * FDIM + HEADS + 2 * d
    n_in_shard = w_in.shape[1]
    in_pad = -(-n_in_shard // LANES) * LANES
    hd = HEADS * LANES
    xc, yc, cc = lax.axis_index("x"), lax.axis_index("y"), lax.axis_index("c")
    chip = 2 * xc + yc
    t_attn = _tile(seq, 512, 128)

    shards = [_pad_cols(w_in, in_pad).astype(BF16), w_uq.astype(BF16), w_ukv.astype(BF16), w_branch_mla.astype(BF16),
              w_branch_fox.astype(BF16), w_out.astype(BF16), w_up.astype(BF16), w_down.astype(BF16)]
    cw8 = jnp.pad(conv_w, ((0, 5), (0, 0)))
    (g_in,), _ = _all_gather_weights(shards[:1], [])
    n_attn_steps = n_seq * (seq // t_attn)
    side_proj = _gather_side([shards[3], shards[4], shards[5], shards[1], shards[2]], [cw8],
                             mid_step=0.9)
    side_ffn = _gather_side([shards[7]], [], mid_step=0.7)
    up_rows = shards[6].shape[0]
    up_cuts = [0, up_rows // 8, up_rows // 2, 7 * up_rows // 8, up_rows]

    def side_up(piece, filled, mid_step):
        lo, hi = up_cuts[piece], up_cuts[piece + 1]
        return _gather_side([shards[6][lo:hi]], [], mid_step=mid_step, into=[(filled, up_rows, lo)])

    o_q, o_kv, o_kpe = 0, QL, QL + KVL
    o_f = o_kpe + ROPE
    o_fl = o_f + 3 * hd
    o_g = o_fl + HEADS

    def chip_cols(lo, hi):
        out = []
        while lo < hi:
            j = lo // n_in_shard
            end = min(hi, (j + 1) * n_in_shard)
            out.append((j, lo - j * n_in_shard, end - j * n_in_shard))
            lo = end
        return out

    take = lambda lo, hi: [g_in[j, :, a:b] for j, a, b in chip_cols(lo, hi)]
    w_kpe = jnp.concatenate(take(o_kpe, o_f), axis=1)
    zeros = lambda n: jnp.zeros((d, n), BF16)
    win_p = jnp.concatenate(
        take(o_g, d_in) + take(o_q, o_kpe) + [w_kpe, zeros(LANES - ROPE), _rot_cols(w_kpe)] + take(o_fl, o_g)
        + [zeros(LANES - ROPE - HEADS)] + take(o_f, o_fl), axis=1)
    n_p = win_p.shape[1]
    cb_gm, cb_gf = 0, 1
    c_lat = 2 * d
    c_kx, c_kr = c_lat + QL + KVL, c_lat + QL + KVL + LANES
    n_pa = c_kr + LANES
    assert n_p == n_pa + 3 * hd

    n_bm, n_up = w_branch_mla.shape[1], w_up.shape[1]
    l_bm, l_up = _Chunked(n_bm), _Chunked(n_up)
    wt = n_up // 2
    n_ut = two_f // wt
    il = lambda cblk: jnp.where(cblk < n_ut // 2, 2 * cblk, 2 * (cblk - n_ut // 2) + 1)
    l_il = _Plain(il)
    to_il = lambda a: a.reshape(a.shape[0], 2, n_ut // 2, wt).transpose(0, 2, 1, 3).reshape(a.shape[0], two_f)
    from_il = lambda a: a.reshape(a.shape[0], n_ut // 2, 2, wt).transpose(0, 2, 1, 3).reshape(a.shape[0], two_f)

    inv_freq = 1.0 / (ROPE_THETA ** (jnp.arange(0, ROPE, 2, dtype=F32) / ROPE))
    ang = positions.astype(F32).reshape(n_tok, 1) * inv_freq
    cos, sin = jnp.cos(ang), jnp.sin(ang)
    cs = _pad_cols(jnp.concatenate([cos, cos], axis=1), LANES)
    sn = _pad_cols(jnp.concatenate([sin, sin], axis=1), LANES)

    row = lambda v: v.reshape(1, -1)
    x2 = x.reshape(n_tok, d)
    tgt = loss_target.reshape(n_tok, d)

    (h,) = _rows("rms_pre_mix", lambda r, v: ([r[0] * _rstd(r[0]) * v[0]], []),
                 [(x2, d, 0)], [row(pre_mix_norm)], [(d, BF16)], [], n_tok)
    proj, (g_bm, g_bf, g_out, g_uq, g_ukv, g_cw) = _mm("proj_in", "nn", h, win_p, n_tok, n_pa, d, tm=2048, side=side_proj)
    w_out_full = g_out.reshape(d, d)
    uq3 = _logical(g_uq).reshape(QL, HEADS, NOPE + ROPE)
    pe = uq3[:, :, NOPE:]
    pad_pe = lambda a: jnp.pad(a, ((0, 0), (0, 0), (0, LANES - ROPE))).reshape(QL, hd)
    wuq_p = jnp.concatenate([uq3[:, :, :NOPE].reshape(QL, hd), pad_pe(pe), pad_pe(_rot_cols(pe))], axis=1)
    ukv3 = _logical(g_ukv).reshape(KVL, HEADS, NOPE + VDIM)
    wukv_p = jnp.concatenate([ukv3[:, :, :NOPE].reshape(KVL, hd), ukv3[:, :, NOPE:].reshape(KVL, hd)], axis=1)
    tn_f = _tile(3 * hd, 1024, 128)
    assert n_pa % tn_f == 0
    proj_f, (g_up,) = _mm("proj_in_fox", "nn", h, win_p, n_tok, 3 * hd, d, tn=tn_f, lb=_Plain(lambda cblk: cblk + n_pa // tn_f),
                          out_dtype=BF16, side=side_up(0, None, 0.8))

    bf_vec = jnp.pad(row(b_forget), ((0, 0), (ROPE, LANES - ROPE - HEADS)))

    def lat_fwd(r, v):
        ql, kvl = r[0], r[1]
        return [ql * _rstd(ql) * v[0], kvl * _rstd(kvl) * v[1], r[2] * r[4] + r[3] * r[5]], []

    qn, kvn, rk = _rows("latent_norms", lat_fwd,
                        [(proj, QL, c_lat // QL), (proj, KVL, (c_lat + QL) // KVL), (proj, LANES, c_kx // LANES),
                         (proj, LANES, c_kr // LANES), (cs, LANES, 0), (sn, LANES, 0)],
                        [row(q_a_norm), row(kv_a_norm)], [(QL, BF16), (KVL, BF16), (LANES, BF16)], [], n_tok)
    q_p = _mm("q_up", "nn", qn, wuq_p, n_tok, 3 * hd, QL)
    kv_p = _mm("kv_up", "nn", kvn, wukv_p, n_tok, 2 * hd, KVL, out_dtype=BF16)

    def rope_q(r, v):
        c8, s8 = jnp.tile(r[3], (1, HEADS)), jnp.tile(r[4], (1, HEADS))
        return [r[0], r[1] * c8 + r[2] * s8], []

    q_nope, rq = _rows("rope_q", rope_q, [(q_p, hd, 0), (q_p, hd, 1), (q_p, hd, 2), (cs, LANES, 0), (sn, LANES, 0)], [],
                       [(hd, BF16), (hd, BF16)], [], n_tok)

    mla_q = [(q_nope, 0, False), (rq, 0, False)]
    mla_k = [(kv_p, 0, False), (rk, 0, True)]
    mla_v = (kv_p, 1)
    mla_scale = (NOPE + ROPE) ** -0.5
    (o_mla, lse_mla), (g_up,) = _attn_fwd("mla_fwd", mla_q, mla_k, mla_v, None, CHUNK, mla_scale, n_seq, seq, t_attn,
                                          side=side_up(1, g_up, max(n_attn_steps - 2, 0)))

    c_run = _seq_cumsum("forget_cumsum", proj, c_kr // LANES, n_seq, seq, False,
                        pre=lambda z, b: _log_sigmoid(z + b), vec=bf_vec)
    nb_attn = seq // t_attn
    c_rowf = jnp.transpose(c_run[:, ROPE:ROPE + HEADS].reshape(n_seq, nb_attn, t_attn, HEADS), (0, 1, 3, 2))
    fox_q, fox_k, fox_v = [(proj_f, 0, False)], [(proj_f, 1, False)], (proj_f, 2)
    fox_scale = FDIM ** -0.5
    fox_bias = (c_run, c_rowf)
    (o_fox, lse_fox), (g_up,) = _attn_fwd("fox_fwd", fox_q, fox_k, fox_v, fox_bias, 1, fox_scale, n_seq, seq, t_attn,
                                          side=side_up(2, g_up, max(n_attn_steps - 2, 0)))

    pm, (g_up,) = _mm("branch_mla", "nn", o_mla, g_bm, n_tok, d, hd, lb=l_bm, tn=n_bm, out_dtype=BF16,
                      side=side_up(3, g_up, 0.7))
    pf = _mm("branch_fox", "nn", o_fox, g_bf, n_tok, d, hd, lb=l_bm, tn=n_bm, out_dtype=BF16)
    bg = row(b_gate)

    def merge(r, v):
        return [_sigmoid(r[0] + v[0]) * r[2] + _sigmoid(r[1] + v[1]) * r[3]], []

    (merged,) = _rows("gate_merge", merge, [(proj, d, cb_gm), (proj, d, cb_gf), (pm, d, 0), (pf, d, 0)],
                      [bg[:, :d], bg[:, d:]], [(d, BF16)], [], n_tok)
    y1 = _mm("mix_out", "nn", merged, w_out_full, n_tok, d, d)

    def resid_norm(r, v):
        x1v = r[0] + r[1] * _rstd(r[1]) * v[0]
        return [x1v, x1v * _rstd(x1v) * v[1]], []

    x1, h2 = _rows("post_mix_pre_ffn", resid_norm, [(x2, d, 0), (y1, d, 0)], [row(post_mix_norm), row(pre_ffn_norm)],
                   [(d, F32), (d, BF16)], [], n_tok)

    u_il, got = _mm("ffn_up", "nn", h2, g_up, n_tok, two_f, d, lb=l_up, lo=l_il, tm=2048, tn=wt, out_dtype=BF16,
                    side=side_ffn)
    w_down_full = got[0].reshape(ff_dim, d)
    cw_il = to_il(_logical(g_cw)[:3])
    cb_il = to_il(row(conv_b))
    act = _conv_glu_fwd(u_il, cw_il, cb_il, n_seq, seq, wt)
    ff = _mm("ffn_down", "nn", act, w_down_full, n_tok, d, ff_dim)

    def final(r, v):
        x1v, ffv, tg = r
        diff = x1v + ffv * _rstd(ffv) * v[0] - tg
        dx2v = diff / d
        dffv, dg4 = _rms_bwd(ffv, v[0], dx2v)
        sq = jnp.sum(jnp.sum(diff * diff, axis=1, keepdims=True), axis=0, keepdims=True)
        return [dx2v, dffv], [dg4, jnp.broadcast_to(sq, (1, LANES))]

    dx2, dff, dg_post_ffn, sq_sum = _rows("loss_post_ffn_bwd", final, [(x1, d, 0), (ff, d, 0), (tgt, d, 0)],
                                          [row(post_ffn_norm)], [(d, F32), (d, BF16)], [(1, d), (1, LANES)], n_tok)
    rs_parts, rs_landed = {}, {}
    core = jnp.reshape(cc, (1,)).astype(jnp.int32)

    def pair_reduce(tag, names, grads):
        theirs = _pair_split("rs_pair_split_" + tag, grads)
        for nm, g, b in zip(names, grads, theirs):
            rs_parts[nm] = _pair_add("rs_pair_add_" + nm, g, b, core)

    dact = _mm("ffn_down_dx", "nt", dff, w_down_full, n_tok, ff_dim, d, tm=2048, tn=wt, out_dtype=BF16)
    gw_down = _mm("ffn_down_dw", "tn", act, dff, ff_dim, d, n_tok, tm=wt, out_dtype=BF16)
    pair_reduce("down", ["w_down"], [gw_down.reshape(N_CHIPS, ff_dim // N_CHIPS, d)])
    d_il, conv_acc = _conv_glu_bwd_pre(u_il, dact, cw_il, cb_il, n_seq, seq, wt)
    du_il = _conv_bwd_input(d_il, cw_il, n_seq, seq, wt)
    gw_up, got = _mm("ffn_up_dw", "tn", h2, du_il, d, two_f, n_tok, lb=l_il, lo=l_up, tn=wt, out_dtype=BF16,
                     side=_scatter_side([rs_parts["w_down"]]))
    rs_landed["w_down"] = got[0]
    pair_reduce("up", ["w_up"], [gw_up])
    dh2, got = _mm("ffn_up_dx", "nt", du_il, g_up, n_tok, d, two_f, la=l_il, lb=l_up, tk=wt,
                   side=_scatter_side([rs_parts["w_up"]]))
    rs_landed["w_up"] = got[0]

    def mid_bwd(r, v):
        x1v, y1v, dx2v, dh2v = r
        d3, dg3 = _rms_bwd(x1v, v[1], dh2v)
        dx1v = dx2v + d3
        dy1v, dg2 = _rms_bwd(y1v, v[0], dx1v)
        return [dx1v, dy1v], [dg3, dg2]

    dx1, dy1, dg_pre_ffn, dg_post_mix = _rows(
        "pre_ffn_post_mix_bwd", mid_bwd, [(x1, d, 0), (y1, d, 0), (dx2, d, 0), (dh2, d, 0)],
        [row(post_mix_norm), row(pre_ffn_norm)], [(d, F32), (d, BF16)], [(1, d), (1, d)], n_tok)
    dmerged = _mm("mix_out_dx", "nt", dy1, w_out_full, n_tok, d, d, out_dtype=BF16)
    gw_out = _mm("mix_out_dw", "tn", merged, dy1, d, d, n_tok, out_dtype=BF16)

    def gate_bwd(r, v):
        zm, zf, pmv, pfv, dm = r
        gm, gf = _sigmoid(zm + v[0]), _sigmoid(zf + v[1])
        dzm, dzf = dm * pmv * gm * (1.0 - gm), dm * pfv * gf * (1.0 - gf)
        return [dm * gm, dm * gf, jnp.concatenate([dzm, dzf], axis=1)], [_colsum(dzm), _colsum(dzf)]

    dpm, dpf, dz, dbg_m, dbg_f = _rows(
        "gate_merge_bwd", gate_bwd, [(proj, d, cb_gm), (proj, d, cb_gf), (pm, d, 0), (pf, d, 0), (dmerged, d, 0)],
        [bg[:, :d], bg[:, d:]], [(d, BF16), (d, BF16), (2 * d, BF16)], [(1, d), (1, d)], n_tok)
    tk_b = min(n_bm, 512)
    do_mla = _mm("branch_mla_dx", "nt", dpm, g_bm, n_tok, hd, d, lb=l_bm, tk=tk_b)
    do_fox = _mm("branch_fox_dx", "nt", dpf, g_bf, n_tok, hd, d, lb=l_bm, tk=tk_b)
    gw_bm = _mm("branch_mla_dw", "tn", o_mla, dpm, hd, d, n_tok, lo=l_bm, tn=n_bm, out_dtype=BF16)
    gw_bf = _mm("branch_fox_dw", "tn", o_fox, dpf, hd, d, n_tok, lo=l_bm, tn=n_bm, out_dtype=BF16)

    pair_reduce("mix", ["w_out", "w_branch_mla", "w_branch_fox"], [gw_out.reshape(N_CHIPS, d // N_CHIPS, d), gw_bm, gw_bf])
    delta_mla, dob_mla = _attn_delta("mla_delta", o_mla, do_mla, n_tok)
    (dq_nope, drq, dk_nope, drk_g, dv_mla), got = _attn_bwd(
        "mla_bwd", mla_q, mla_k, mla_v, dob_mla, lse_mla, delta_mla, None, CHUNK, mla_scale, n_seq, seq, t_attn, BF16,
        side=_scatter_side([rs_parts[nm] for nm in ("w_out", "w_branch_mla", "w_branch_fox")]))
    rs_landed.update(zip(("w_out", "w_branch_mla", "w_branch_fox"), got))
    delta_fox, dob_fox = _attn_delta("fox_delta", o_fox, do_fox, n_tok)
    (dfq, dfk, dfv, dc_q, dc_k), _ = _attn_bwd("fox_bwd", fox_q, fox_k, fox_v, dob_fox, lse_fox, delta_fox, fox_bias, 1,
                                               fox_scale, n_seq, seq, t_attn, BF16)
    dc_k8 = jnp.transpose(dc_k, (0, 2, 4, 1, 3)).reshape(n_tok, HEADS)
    dc128 = dc_q[0] + dc_q[1] + jnp.pad(dc_k8, ((0, 0), (ROPE, LANES - ROPE - HEADS)))
    dlogf = _seq_cumsum("forget_cumsum_bwd", dc128, 0, n_seq, seq, True)

    def mla_pack(r, v):
        dqn_v, drq_v, dkn_v, dv_v, drk_a, drk_b, c1, s1 = r
        c8, s8 = jnp.tile(c1, (1, HEADS)), jnp.tile(s1, (1, HEADS))
        drk_v = drk_a + drk_b
        return [jnp.concatenate([dqn_v, drq_v * c8, drq_v * s8], axis=1), jnp.concatenate([dkn_v, dv_v], axis=1),
                drk_v * c1, drk_v * s1], []

    dq_p, dkv_p, dkx, dkr = _rows(
        "mla_rope_bwd", mla_pack,
        [(dq_nope, hd, 0), (drq, hd, 0), (dk_nope, hd, 0), (dv_mla, hd, 0), (drk_g[0], LANES, 0), (drk_g[1], LANES, 0),
         (cs, LANES, 0), (sn, LANES, 0)],
        [], [(3 * hd, BF16), (2 * hd, BF16), (LANES, F32), (LANES, F32)], [], n_tok)
    dqn = _mm("q_up_dx", "nt", dq_p, wuq_p, n_tok, QL, 3 * hd)
    gw_uq_p = _mm("q_up_dw", "tn", qn, dq_p, QL, 3 * hd, n_tok, out_dtype=BF16)
    dkvn = _mm("kv_up_dx", "nt", dkv_p, wukv_p, n_tok, KVL, 2 * hd)
    gw_ukv_p = _mm("kv_up_dw", "tn", kvn, dkv_p, KVL, 2 * hd, n_tok, out_dtype=BF16)

    def lat_bwd(r, v):
        ql, kvl, dqn_v, dkvn_v, dkx_v, dkr_v, zblk, dlf = r
        dql, dgq = _rms_bwd(ql, v[0], dqn_v)
        dkvl, dgkv = _rms_bwd(kvl, v[1], dkvn_v)
        dfl = dlf * _sigmoid(-(zblk + v[2]))
        return [jnp.concatenate([dql, dkvl, dkx_v, dkr_v + dfl], axis=1)], [dgq, dgkv, _colsum(dfl)]

    dlat, dg_q, dg_kv, dbf = _rows(
        "latent_bwd", lat_bwd,
        [(proj, QL, c_lat // QL), (proj, KVL, (c_lat + QL) // KVL), (dqn, QL, 0), (dkvn, KVL, 0), (dkx, LANES, 0),
         (dkr, LANES, 0), (proj, LANES, c_kr // LANES), (dlogf, LANES, 0)],
        [row(q_a_norm), row(kv_a_norm), bf_vec], [(QL + KVL + 2 * LANES, BF16)], [(1, QL), (1, KVL), (1, LANES)], n_tok)
    dproj = [dz, dlat, dfq, dfk, dfv]
    gw_in_p = _mm_parts("proj_in_dw", "tn", h, dproj, d, n_p, n_tok, tk=1024, out_dtype=BF16)

    f32 = lambda a: a.astype(F32)
    kr_blk = gw_in_p[:, c_kr:c_kr + LANES]
    d_kpe = (f32(gw_in_p[:, c_kx:c_kx + ROPE]) + _unrot_cols(f32(kr_blk[:, :ROPE]))).astype(BF16)
    in_pieces = [(o_q, gw_in_p, c_lat, QL + KVL), (o_kpe, d_kpe, 0, ROPE), (o_f, gw_in_p, n_pa, 3 * hd),
                 (o_fl, kr_blk, ROPE, HEADS), (o_g, gw_in_p, 0, 2 * d)]
    gc_in = []
    for j in range(N_CHIPS):
        lo, hi, cols = j * n_in_shard, (j + 1) * n_in_shard, []
        for first, arr, at, width in in_pieces:
            a, b = max(lo, first), min(hi, first + width)
            if a < b:
                cols.append(arr[:, at + a - first:at + b - first])
        cols.append(jnp.zeros((d, in_pad - n_in_shard), BF16))
        gc_in.append(jnp.concatenate(cols, axis=1))
    gc_in = jnp.stack(gc_in)
    uq_parts = [gw_uq_p[:, i * hd:(i + 1) * hd].reshape(QL, HEADS, LANES) for i in range(3)]
    d_pe = (f32(uq_parts[1][:, :, :ROPE]) + _unrot_cols(f32(uq_parts[2][:, :, :ROPE]))).astype(BF16)
    gc_uq = _chunks(jnp.concatenate([uq_parts[0], d_pe], axis=2).reshape(QL, HEADS * (NOPE + ROPE)), w_uq.shape[1])
    gc_ukv = _chunks(jnp.concatenate([gw_ukv_p[:, :hd].reshape(KVL, HEADS, NOPE), gw_ukv_p[:, hd:].reshape(KVL, HEADS, VDIM)],
                                     axis=2).reshape(KVL, HEADS * (NOPE + VDIM)), w_ukv.shape[1])
    grads = [gc_in, gc_uq, gc_ukv]

    late = ["w_in", "w_uq", "w_ukv"]
    pair_reduce("late", late, grads)
    dh, got = _mm_parts("proj_in_dx", "nt", dproj, win_p, n_tok, d, n_p, side=_scatter_side([rs_parts[nm] for nm in late]))
    rs_landed.update(zip(late, got))

    def first_bwd(r, v):
        dxa, dg1 = _rms_bwd(r[0], v[0], r[1])
        return [r[2] + dxa], [dg1]

    grad_x, dg_pre_mix = _rows("pre_mix_bwd", first_bwd, [(x2, d, 0), (dh, d, 0), (dx1, d, 0)], [row(pre_mix_norm)],
                               [(d, F32)], [(1, d)], n_tok)
    big = list(rs_parts)
    chip1 = jnp.reshape(chip, (1,)).astype(jnp.int32)
    halves = [_chip_sum("rs_chip_sum_" + nm, rs_parts[nm], rs_landed[nm], chip1) for nm in big]
    other = _pair_swap(halves)
    g_halves = dict(zip(big, zip(halves, other)))

    conv_acc_l = from_il(conv_acc)
    pieces = [dg_pre_mix, dg_q, dg_kv, dbf, dbg_m, dbg_f, dg_post_mix, dg_pre_ffn, conv_acc_l[3:4], dg_post_ffn,
              conv_acc_l[0:1], conv_acc_l[1:2], conv_acc_l[2:3], sq_sum]
    sizes = [p.shape[1] for p in pieces]
    flat = jnp.concatenate(pieces, axis=1)
    n_rows = -(-flat.shape[1] // (8 * LANES)) * 8
    flat = _pad_cols(flat, n_rows * LANES).reshape(n_rows, LANES)
    slots = lax.dynamic_update_slice(_gather_small(flat), flat[None], (2 * chip + cc, 0, 0))
    total = _sum_slots("small_sum", slots).reshape(1, n_rows * LANES)
    offs = [sum(sizes[:i]) for i in range(len(sizes))]
    tot = [total[0, o:o + s] for o, s in zip(offs, sizes)]
    loss = 0.5 * tot[13][0] / d
    g_small = {"pre_mix_norm": tot[0], "q_a_norm": tot[1], "kv_a_norm": tot[2], "b_forget": tot[3][ROPE:ROPE + HEADS],
               "b_gate": jnp.concatenate([tot[4], tot[5]]), "post_mix_norm": tot[6], "pre_ffn_norm": tot[7],
               "conv_b": tot[8], "post_ffn_norm": tot[9]}
    gcw_full = jnp.stack([tot[10], tot[11], tot[12]])
    g_conv_w = lax.dynamic_slice(gcw_full, (0, chip * n_up), (3, n_up))

    given = dict(pre_mix_norm=(pre_mix_norm, m_pre_mix_norm, v_pre_mix_norm), w_in=(w_in, m_w_in, v_w_in),
                 q_a_norm=(q_a_norm, m_q_a_norm, v_q_a_norm), w_uq=(w_uq, m_w_uq, v_w_uq),
                 kv_a_norm=(kv_a_norm, m_kv_a_norm, v_kv_a_norm), w_ukv=(w_ukv, m_w_ukv, v_w_ukv),
                 b_forget=(b_forget, m_b_forget, v_b_forget), b_gate=(b_gate, m_b_gate, v_b_gate),
                 w_branch_mla=(w_branch_mla, m_w_branch_mla, v_w_branch_mla),
                 w_branch_fox=(w_branch_fox, m_w_branch_fox, v_w_branch_fox), w_out=(w_out, m_w_out, v_w_out),
                 post_mix_norm=(post_mix_norm, m_post_mix_norm, v_post_mix_norm),
                 pre_ffn_norm=(pre_ffn_norm, m_pre_ffn_norm, v_pre_ffn_norm), w_up=(w_up, m_w_up, v_w_up),
                 conv_w=(conv_w, m_conv_w, v_conv_w), conv_b=(conv_b, m_conv_b, v_conv_b),
                 w_down=(w_down, m_w_down, v_w_down), post_ffn_norm=(post_ffn_norm, m_post_ffn_norm, v_post_ffn_norm))
    order = list(given)
    grad, delta, new_m, new_v = {}, {}, {}, {}
    for nm in big:
        mine, theirs = g_halves[nm]
        if nm == "w_in":
            full = jnp.concatenate([jnp.where(cc == 0, mine, theirs), jnp.where(cc == 0, theirs, mine)], axis=0)
            grad[nm] = full[:, :n_in_shard]
            tr_out = _adamw("adamw_" + nm, *[jnp.transpose(a) for a in (given[nm][0], grad[nm], given[nm][1], given[nm][2])])
            delta[nm], new_m[nm], new_v[nm] = [jnp.transpose(a) for a in tr_out]
            continue
        grad[nm], delta[nm], new_m[nm], new_v[nm] = _adamw_halves("adamw_" + nm, given[nm][0], mine, theirs, given[nm][1],
                                                                  given[nm][2], core)
    grad["conv_w"] = g_conv_w
    delta["conv_w"], new_m["conv_w"], new_v["conv_w"] = _adamw("adamw_conv_w", conv_w, g_conv_w, m_conv_w, v_conv_w)
    small = list(g_small)
    padded = [-(-g_small[nm].shape[0] // LANES) * LANES for nm in small]
    s_rows = -(-sum(padded) // (8 * LANES)) * 8

    def pack(vals):
        cat = jnp.concatenate([jnp.pad(a, (0, p - a.shape[0])) for a, p in zip(vals, padded)])
        return jnp.pad(cat, (0, s_rows * LANES - cat.shape[0])).reshape(s_rows, LANES)

    packed = _adamw("adamw_small", pack([given[nm][0] for nm in small]), pack([g_small[nm] for nm in small]),
                    pack([given[nm][1] for nm in small]), pack([given[nm][2] for nm in small]))
    s_offs = [sum(padded[:i]) for i in range(len(small))]
    for nm, o in zip(small, s_offs):
        n_el = g_small[nm].shape[0]
        grad[nm] = g_small[nm]
        delta[nm], new_m[nm], new_v[nm] = [p.reshape(-1)[o:o + n_el] for p in packed]
    return (loss, grad_x.reshape(n_seq, seq, d), *[grad[nm] for nm in order], *[delta[nm] for nm in order],
            *[new_m[nm] for nm in order], *[new_v[nm] for nm in order])
```

```python
import functools
import math

import jax
import jax.numpy as jnp
from jax import lax
from jax.experimental import pallas as pl
from jax.experimental.pallas import tpu as pltpu

F32, BF16 = jnp.float32, jnp.bfloat16
MESH = pl.DeviceIdType.MESH

HEADS = 8
NOPE, ROPE, VDIM = 128, 64, 128
QL, KVL = 512, 256
FDIM = 128
CHUNK = 64
ROPE_THETA = 10000.0
EPS = 1e-6
NEG_INF = -1e30
ADAM_LR, ADAM_B1, ADAM_B2, ADAM_EPS, ADAM_WD, ADAM_STEP = 0.001, 0.9, 0.999, 1e-08, 0.01, 10

VMEM_LIMIT_BYTES = 52 * 1024 * 1024
LANES = 128
N_CHIPS = 4


def _params(sem):
    return pltpu.CompilerParams(dimension_semantics=sem, vmem_limit_bytes=VMEM_LIMIT_BYTES)


def _tile(n, target, mult):
    if n <= target:
        return n
    t = (target // mult) * mult
    while t >= mult:
        if n % t == 0:
            return t
        t -= mult
    raise ValueError(f"no tile for {n} (target {target}, multiple of {mult})")


class _Plain:
    def __init__(self, perm=None):
        self.perm = perm

    def spec(self, tr, tc, rc):
        perm = self.perm

        def imap(i, j, k):
            r, c = rc(i, j, k)
            return (r, perm(c) if perm is not None else c)

        return pl.BlockSpec((tr, tc), imap)

    def shape(self, rows, cols):
        return (rows, cols)


class _Chunked:
    def __init__(self, n):
        self.n = n

    def spec(self, tr, tc, rc):
        assert self.n % tc == 0, (self.n, tc)
        per = self.n // tc

        def imap(i, j, k):
            r, c = rc(i, j, k)
            return (c // per, r, c % per)

        return pl.BlockSpec((None, tr, tc), imap)

    def shape(self, rows, cols):
        assert cols == N_CHIPS * self.n
        return (N_CHIPS, rows, self.n)


_DIMS = {"nn": (((1,), (0,)), ((), ())), "nt": (((1,), (1,)), ((), ())), "tn": (((0,), (0,)), ((), ()))}


def _mm_single(name, mode, a, b, m, n, k, tm, tn, la, lb, lo, out_dtype, side):
    if mode == "nn":
        a_spec = la.spec(tm, k, lambda i, j, kk: (i, 0))
        b_spec = lb.spec(k, tn, lambda i, j, kk: (0, j))
    elif mode == "nt":
        a_spec = la.spec(tm, k, lambda i, j, kk: (i, 0))
        b_spec = lb.spec(tn, k, lambda i, j, kk: (j, 0))
    else:
        a_spec = la.spec(k, tm, lambda i, j, kk: (0, i))
        b_spec = lb.spec(k, tn, lambda i, j, kk: (0, j))
    o_spec = lo.spec(tm, tn, lambda i, j, kk: (i, j))
    dims = _DIMS[mode]

    def body(a_ref, b_ref, o_ref):
        o_ref[...] = lax.dot_general(a_ref[...].astype(BF16), b_ref[...].astype(BF16), dims,
                                     preferred_element_type=F32).astype(o_ref.dtype)

    (out,), got = _hosted_call(body, name, (m // tm, n // tn, 1), [a_spec, b_spec], [o_spec],
                               [jax.ShapeDtypeStruct(lo.shape(m, n), out_dtype)], (a, b), side,
                               semantics=("parallel", "parallel", "arbitrary"))
    return out if side is None else (out, got)


def _mm(name, mode, a, b, m, n, k, *, tm=1024, tn=1024, tk=2048, la=None, lb=None, lo=None, out_dtype=F32, side=None):
    la, lb, lo = la or _Plain(), lb or _Plain(), lo or _Plain()
    tm, tn, tk = _tile(m, tm, 128), _tile(n, tn, 128), _tile(k, tk, 128)
    nk = k // tk
    if nk == 1:
        return _mm_single(name, mode, a, b, m, n, k, tm, tn, la, lb, lo, out_dtype, side)
    if mode == "nn":
        a_spec = la.spec(tm, tk, lambda i, j, kk: (i, kk))
        b_spec = lb.spec(tk, tn, lambda i, j, kk: (kk, j))
    elif mode == "nt":
        a_spec = la.spec(tm, tk, lambda i, j, kk: (i, kk))
        b_spec = lb.spec(tn, tk, lambda i, j, kk: (j, kk))
    else:
        a_spec = la.spec(tk, tm, lambda i, j, kk: (kk, i))
        b_spec = lb.spec(tk, tn, lambda i, j, kk: (kk, j))
    o_spec = lo.spec(tm, tn, lambda i, j, kk: (i, j))
    dims = _DIMS[mode]

    def body(a_ref, b_ref, o_ref, acc_ref):
        kk = pl.program_id(2)

        @pl.when(kk == 0)
        def _():
            acc_ref[...] = jnp.zeros_like(acc_ref)

        acc_ref[...] += lax.dot_general(a_ref[...].astype(BF16), b_ref[...].astype(BF16), dims,
                                        preferred_element_type=F32)

        @pl.when(kk == nk - 1)
        def _():
            o_ref[...] = acc_ref[...].astype(o_ref.dtype)

    (out,), got = _hosted_call(body, name, (m // tm, n // tn, nk), [a_spec, b_spec], [o_spec],
                               [jax.ShapeDtypeStruct(lo.shape(m, n), out_dtype)], (a, b), side,
                               semantics=("parallel", "parallel", "arbitrary"), scratch=[pltpu.VMEM((tm, tn), F32)])
    return out if side is None else (out, got)


def _mm_parts(name, mode, a, b, m, n, k, *, part=1024, tm=1024, tn=1024, tk=2048, out_dtype=F32, side=None):
    parts = b if mode == "tn" else a
    widths = [p.shape[1] for p in parts]
    assert all(w % part == 0 for w in widths) and sum(widths) == (n if mode == "tn" else k)
    offs = [sum(widths[:i]) // part for i in range(len(widths))]
    nblk = [w // part for w in widths]
    if mode == "tn":
        tn, tk = part, _tile(k, tk, 128)
    else:
        tk, tn = part, _tile(n, tn, 128)
    tm = _tile(m, tm, 128)
    nk = k // tk
    grid = (m // tm, n // tn, nk)
    np_ = len(parts)

    def inside(idx, p):
        return jnp.logical_and(idx >= offs[p], idx < offs[p] + nblk[p])

    def part_spec(p):
        if mode == "tn":
            def imap(i, j, kk):
                on = inside(j, p)
                return (jnp.where(on, kk, 0), jnp.clip(j - offs[p], 0, nblk[p] - 1))
            return pl.BlockSpec((tk, tn), imap)

        def imap(i, j, kk):
            return (i, jnp.clip(kk - offs[p], 0, nblk[p] - 1))
        return pl.BlockSpec((tm, tk), imap)

    if mode == "tn":
        in_specs = [pl.BlockSpec((tk, tm), lambda i, j, kk: (kk, i))] + [part_spec(p) for p in range(np_)]
        args = [a] + list(parts)
    else:
        in_specs = [part_spec(p) for p in range(np_)] + [pl.BlockSpec((tn, tk), lambda i, j, kk: (j, kk))]
        args = list(parts) + [b]
    dims = _DIMS[mode]

    def body(*refs):
        o_ref, acc_ref = refs[-2], refs[-1]
        j, kk = pl.program_id(1), pl.program_id(2)

        @pl.when(kk == 0)
        def _():
            acc_ref[...] = jnp.zeros_like(acc_ref)

        for p in range(np_):
            @pl.when(inside(j if mode == "tn" else kk, p))
            def _(p=p):
                lhs, rhs = (refs[0], refs[1 + p]) if mode == "tn" else (refs[p], refs[np_])
                acc_ref[...] += lax.dot_general(lhs[...].astype(BF16), rhs[...].astype(BF16), dims, preferred_element_type=F32)

        @pl.when(kk == nk - 1)
        def _():
            o_ref[...] = acc_ref[...].astype(o_ref.dtype)

    (out,), got = _hosted_call(body, name, grid, in_specs, [pl.BlockSpec((tm, tn), lambda i, j, kk: (i, j))],
                               [jax.ShapeDtypeStruct((m, n), out_dtype)], args, side,
                               semantics=("parallel", "parallel", "arbitrary"), scratch=[pltpu.VMEM((tm, tn), F32)])
    return out if side is None else (out, got)


def _rows(name, fn, rows_in, vecs_in, rows_out, accs_out, n_rows, tr=256):
    tr = _tile(n_rows, tr, 16)
    nr, nv, no = len(rows_in), len(vecs_in), len(rows_out)

    def body(*refs):
        ins, vecs = refs[:nr], refs[nr:nr + nv]
        outs, accs = refs[nr + nv:nr + nv + no], refs[nr + nv + no:]
        ro, ac = fn([r[...] for r in ins], [v[...] for v in vecs])
        for o_ref, val in zip(outs, ro):
            o_ref[...] = val.astype(o_ref.dtype)
        if accs:
            @pl.when(pl.program_id(0) == 0)
            def _():
                for a_ref in accs:
                    a_ref[...] = jnp.zeros_like(a_ref)

            for a_ref, val in zip(accs, ac):
                a_ref[...] += val

    in_specs = [pl.BlockSpec((tr, cols), functools.partial(lambda i, cb: (i, cb), cb=cb)) for _, cols, cb in rows_in]
    in_specs += [pl.BlockSpec(v.shape, lambda i: (0, 0)) for v in vecs_in]
    out_specs = [pl.BlockSpec((tr, cols), lambda i: (i, 0)) for cols, _ in rows_out]
    out_specs += [pl.BlockSpec((r, cols), lambda i: (0, 0)) for r, cols in accs_out]
    out_shape = [jax.ShapeDtypeStruct((n_rows, cols), dt) for cols, dt in rows_out]
    out_shape += [jax.ShapeDtypeStruct((r, cols), F32) for r, cols in accs_out]
    res = pl.pallas_call(
        body, name=name, grid=(n_rows // tr,), in_specs=in_specs, out_specs=out_specs, out_shape=out_shape,
        compiler_params=_params(("arbitrary",)),
    )(*[a for a, _, _ in rows_in], *vecs_in)
    return res


def _colsum(v):
    return jnp.sum(v, axis=0, keepdims=True)


def _rstd(x):
    return lax.rsqrt(jnp.mean(x * x, axis=-1, keepdims=True) + EPS)


def _rms_bwd(x, g, dy):
    r = _rstd(x)
    xh = x * r
    dxh = dy * g
    dx = r * (dxh - xh * jnp.mean(dxh * xh, axis=-1, keepdims=True))
    return dx, _colsum(dy * xh)


def _sigmoid(z):
    return 1.0 / (1.0 + jnp.exp(-z))


_GELU_K = math.sqrt(2.0 / math.pi)


def _gelu_parts(g):
    t = jnp.tanh(_GELU_K * (g + 0.044715 * g * g * g))
    gel = 0.5 * g * (1.0 + t)
    dgel = 0.5 * (1.0 + t) + 0.5 * g * (1.0 - t * t) * (_GELU_K * (1.0 + 3.0 * 0.044715 * g * g))
    return gel, dgel


def _diag_visible(t, unit):
    rows = lax.broadcasted_iota(jnp.int32, (t, t), 0)
    cols = lax.broadcasted_iota(jnp.int32, (t, t), 1)
    if unit > 1:
        sh = int(math.log2(unit))
        assert 1 << sh == unit and t % unit == 0
        rows, cols = jnp.right_shift(rows, sh), jnp.right_shift(cols, sh)
    return cols <= rows


def _lane_pick(tile, lane):
    idx = lax.broadcasted_iota(jnp.int32, tile.shape, 1)
    return jnp.sum(jnp.where(idx == lane, tile, 0.0), axis=1, keepdims=True)


def _lane_put(tile, lane, col):
    idx = lax.broadcasted_iota(jnp.int32, tile.shape, 1)
    return jnp.where(idx == lane, col, tile)


def _head_cat(refs, shared, rows, h):
    hs = slice(h * LANES, (h + 1) * LANES)
    vals = [(r[rows, :] if sh else r[rows, hs]).astype(BF16) for r, sh in zip(refs, shared)]
    return vals[0] if len(vals) == 1 else jnp.concatenate(vals, axis=1)


def _blk_rows(i, t):
    return pl.ds(pl.multiple_of(i * t, t), t)


def _piece_specs(pieces, rows, row_idx):
    return [pl.BlockSpec((rows, LANES if sh else HEADS * LANES), functools.partial(lambda b, i, cb: (row_idx(b, i), cb), cb=cb))
            for _, cb, sh in pieces]


def _attn_fwd(name, qp, kp, vp, bias, unit, scale, n_seq, seq, t, side=None):
    nb = seq // t
    n_tok = n_seq * seq
    nq, nk_p = len(qp), len(kp)
    q_sh, k_sh = [p[2] for p in qp], [p[2] for p in kp]
    nbias = 2 if bias is not None else 0

    def body(*refs):
        q_refs, k_refs = refs[:nq], refs[nq:nq + nk_p]
        v_ref = refs[nq + nk_p]
        bias_refs = refs[nq + nk_p + 1:nq + nk_p + 1 + nbias]
        o_ref, lse_ref = refs[nq + nk_p + 1 + nbias:]
        qi = pl.program_id(1)
        lse_tile = jnp.zeros((t, LANES), F32)
        for h in range(HEADS):
            hs = slice(h * LANES, (h + 1) * LANES)
            q = _head_cat(q_refs, q_sh, slice(None), h)
            cq = _lane_pick(bias_refs[0][...], ROPE + h) if bias is not None else None

            def block(kb, carry, diag, h=h, hs=hs, q=q, cq=cq):
                m, l, acc = carry
                rows = _blk_rows(kb, t)
                s = lax.dot_general(q, _head_cat(k_refs, k_sh, rows, h), _DIMS["nt"], preferred_element_type=F32) * scale
                if bias is not None:
                    s = s + cq - bias_refs[1][kb, h:h + 1, :]
                if diag:
                    s = jnp.where(_diag_visible(t, unit), s, NEG_INF)
                m_new = jnp.maximum(m, jnp.max(s, axis=1, keepdims=True))
                alpha = jnp.exp(m - m_new)
                p = jnp.exp(s - m_new)
                l = alpha * l + jnp.sum(p, axis=1, keepdims=True)
                acc = alpha * acc + jnp.dot(p.astype(BF16), v_ref[rows, hs].astype(BF16), preferred_element_type=F32)
                return m_new, l, acc

            init = (jnp.full((t, 1), NEG_INF, F32), jnp.zeros((t, 1), F32), jnp.zeros((t, LANES), F32))
            carry = lax.fori_loop(0, qi, lambda kb, c: block(kb, c, False), init)
            m, l, acc = block(qi, carry, True)
            o_ref[:, hs] = acc / l
            lse_tile = _lane_put(lse_tile, h, m + jnp.log(l))
        lse_ref[...] = lse_tile

    tile_row = lambda b, i: b * nb + i
    seq_row = lambda b, i: b
    lane_tile = pl.BlockSpec((t, LANES), lambda b, i: (b * nb + i, 0))
    in_specs = _piece_specs(qp, t, tile_row) + _piece_specs(kp, seq, seq_row) + _piece_specs([vp + (False,)], seq, seq_row)
    args = [p[0] for p in qp] + [p[0] for p in kp] + [vp[0]]
    if bias is not None:
        in_specs += [lane_tile, pl.BlockSpec((None, nb, HEADS, t), lambda b, i: (b, 0, 0, 0))]
        args += list(bias)
    return _hosted_call(
        body, name, (n_seq, nb), in_specs,
        [pl.BlockSpec((t, HEADS * LANES), lambda b, i: (b * nb + i, 0)), lane_tile],
        [jax.ShapeDtypeStruct((n_tok, HEADS * LANES), F32), jax.ShapeDtypeStruct((n_tok, LANES), F32)], args, side)


HEAD_GROUPS = 2


def _attn_delta(name, o, do, n_tok):
    def fn(r, v):
        o_v, do_v = r
        tile = jnp.zeros((o_v.shape[0], LANES), F32)
        for h in range(HEADS):
            hs = slice(h * LANES, (h + 1) * LANES)
            tile = _lane_put(tile, h, jnp.sum(do_v[:, hs] * o_v[:, hs], axis=1, keepdims=True))
        return [tile, do_v], []

    return _rows(name, fn, [(o, HEADS * LANES, 0), (do, HEADS * LANES, 0)], [], [(LANES, F32), (HEADS * LANES, BF16)], [], n_tok)


def _attn_bwd(name, qp, kp, vp, dob, lse, delta, bias, unit, scale, n_seq, seq, t, grad_dtype, side=None):
    nb = seq // t
    n_tok = n_seq * seq
    ng = HEAD_GROUPS
    hg = HEADS // ng
    gw = hg * LANES
    nq, nk_p = len(qp), len(kp)
    q_sh, k_sh = [p[2] for p in qp], [p[2] for p in kp]
    assert not any(q_sh) and nq == nk_p
    nbias = 2 if bias is not None else 0
    n_in = nq + nk_p + 4 + nbias
    n_out = nq + nk_p + 1 + nbias

    def body(*refs):
        q_refs, k_refs = refs[:nq], refs[nq:nq + nk_p]
        v_ref, dob_ref, lse_ref, delta_ref = refs[nq + nk_p:nq + nk_p + 4]
        bias_refs = refs[nq + nk_p + 4:n_in]
        dq_refs, dk_refs = refs[n_in:n_in + nq], refs[n_in + nq:n_in + nq + nk_p]
        dv_ref = refs[n_in + nq + nk_p]
        dq_s, dcq_s = refs[n_in + n_out:]
        g, ki = pl.program_id(1), pl.program_id(2)

        @pl.when(ki == 0)
        def _():
            dq_s[...] = jnp.zeros_like(dq_s)
            dcq_s[...] = jnp.zeros_like(dcq_s)

        shared_acc = [jnp.zeros((t, LANES), F32) for _ in range(nk_p)]
        for hl in range(hg):
            h = g * hg + hl
            hs = slice(hl * LANES, (hl + 1) * LANES)
            k = _head_cat(k_refs, k_sh, slice(None), hl)
            v = v_ref[:, hs].astype(BF16)
            ck = bias_refs[1][pl.ds(h, 1), :] if bias is not None else None

            def block(qb, carry, diag, h=h, hl=hl, hs=hs, k=k, v=v, ck=ck):
                dk_acc, dv_acc, dc_acc = carry
                rows = _blk_rows(qb, t)
                q = _head_cat(q_refs, q_sh, rows, hl)
                s = lax.dot_general(q, k, _DIMS["nt"], preferred_element_type=F32) * scale
                if bias is not None:
                    s = s + _lane_pick(bias_refs[0][rows, :], ROPE + h) - ck
                if diag:
                    s = jnp.where(_diag_visible(t, unit), s, NEG_INF)
                p = jnp.exp(s - _lane_pick(lse_ref[rows, :], h))
                do_b = dob_ref[rows, hs]
                dp = lax.dot_general(do_b, v, _DIMS["nt"], preferred_element_type=F32)
                ds = p * (dp - _lane_pick(delta_ref[rows, :], h))
                ds_b = ds.astype(BF16)
                dq_blk = jnp.dot(ds_b, k, preferred_element_type=F32)
                for n_p in range(nq):
                    dq_s[rows, n_p * gw + hl * LANES:n_p * gw + (hl + 1) * LANES] += dq_blk[:, n_p * LANES:(n_p + 1) * LANES]
                if bias is not None:
                    lane = lax.broadcasted_iota(jnp.int32, (t, LANES), 1)
                    dcq_s[rows, :] += jnp.where(lane == ROPE + h, jnp.sum(ds, axis=1, keepdims=True), 0.0)
                return (dk_acc + lax.dot_general(ds_b, q, _DIMS["tn"], preferred_element_type=F32),
                        dv_acc + lax.dot_general(p.astype(BF16), do_b, _DIMS["tn"], preferred_element_type=F32),
                        dc_acc - jnp.sum(ds, axis=0, keepdims=True))

            init = (jnp.zeros((t, nk_p * LANES), F32), jnp.zeros((t, LANES), F32), jnp.zeros((1, t), F32))
            carry = block(ki, init, True)
            dk_acc, dv_acc, dc_acc = lax.fori_loop(ki + 1, nb, lambda qb, c: block(qb, c, False), carry)
            for n_p in range(nk_p):
                part = dk_acc[:, n_p * LANES:(n_p + 1) * LANES] * scale
                if k_sh[n_p]:
                    shared_acc[n_p] = shared_acc[n_p] + part
                else:
                    dk_refs[n_p][:, hs] = part.astype(grad_dtype)
            dv_ref[:, hs] = dv_acc.astype(grad_dtype)
            if bias is not None:
                refs[n_in + n_out - 1][hl:hl + 1, :] = dc_acc
        for n_p in range(nk_p):
            if k_sh[n_p]:
                dk_refs[n_p][...] = shared_acc[n_p]

        @pl.when(ki == nb - 1)
        def _():
            for n_p in range(nq):
                dq_refs[n_p][...] = (dq_s[:, n_p * gw:(n_p + 1) * gw] * scale).astype(grad_dtype)
            if bias is not None:
                refs[n_in + n_out - 2][...] = dcq_s[...]

    def spec(rows, row_idx, cb, shared):
        if shared:
            return pl.BlockSpec((rows, LANES), lambda b, g, i: (row_idx(b, i), cb))
        return pl.BlockSpec((rows, gw), lambda b, g, i: (row_idx(b, i), cb * ng + g))

    tile_row = lambda b, i: b * nb + i
    seq_row = lambda b, i: b
    lane_seq = pl.BlockSpec((seq, LANES), lambda b, g, i: (b, 0))
    in_specs = [spec(seq, seq_row, cb, sh) for _, cb, sh in qp] + [spec(t, tile_row, cb, sh) for _, cb, sh in kp]
    in_specs += [spec(t, tile_row, vp[1], False), spec(seq, seq_row, 0, False), lane_seq, lane_seq]
    args = [p[0] for p in qp] + [p[0] for p in kp] + [vp[0], dob, lse, delta]
    if bias is not None:
        in_specs += [lane_seq, pl.BlockSpec((None, None, HEADS, t), lambda b, g, i: (b, i, 0, 0))]
        args += list(bias)
    group_tile = pl.BlockSpec((None, t, LANES), lambda b, g, i: (g, b * nb + i, 0))
    out_specs = [spec(seq, seq_row, 0, False)] * nq
    out_specs += [group_tile if sh else spec(t, tile_row, 0, False) for sh in k_sh] + [spec(t, tile_row, 0, False)]
    head_shape = jax.ShapeDtypeStruct((n_tok, HEADS * LANES), grad_dtype)
    out_shape = [head_shape] * nq + [jax.ShapeDtypeStruct((ng, n_tok, LANES), F32) if sh else head_shape for sh in k_sh]
    out_shape.append(head_shape)
    if bias is not None:
        out_specs += [pl.BlockSpec((None, seq, LANES), lambda b, g, i: (g, b, 0)),
                      pl.BlockSpec((None, None, None, hg, t), lambda b, g, i: (b, g, i, 0, 0))]
        out_shape += [jax.ShapeDtypeStruct((ng, n_tok, LANES), F32), jax.ShapeDtypeStruct((n_seq, ng, nb, hg, t), F32)]
    return _hosted_call(body, name, (n_seq, ng, nb), in_specs, out_specs, out_shape, args, side,
                        semantics=("parallel", "arbitrary", "arbitrary"),
                        scratch=[pltpu.VMEM((seq, nq * gw), F32), pltpu.VMEM((seq, LANES), F32)])


def _seq_cumsum(name, x, col_block, n_seq, seq, reverse, pre=None, vec=None):
    t = _tile(seq, 256, 128)
    nb = seq // t

    def body(*refs):
        x_ref = refs[0]
        vec_ref = refs[1] if vec is not None else None
        o_ref, carry = refs[-2], refs[-1]

        @pl.when(pl.program_id(1) == 0)
        def _():
            carry[...] = jnp.zeros_like(carry)

        v = x_ref[...]
        if pre is not None:
            v = pre(v, vec_ref[...])
        r = lax.broadcasted_iota(jnp.int32, (t, t), 0)
        c = lax.broadcasted_iota(jnp.int32, (t, t), 1)
        tri = jnp.where((c >= r) if reverse else (c <= r), 1.0, 0.0).astype(BF16)
        hi = v.astype(BF16)
        mid = (v - hi.astype(F32)).astype(BF16)
        lo = (v - hi.astype(F32) - mid.astype(F32)).astype(BF16)
        acc = jnp.dot(tri, hi, preferred_element_type=F32)
        acc += jnp.dot(tri, mid, preferred_element_type=F32)
        acc += jnp.dot(tri, lo, preferred_element_type=F32)
        o_ref[...] = acc + carry[...]
        carry[...] += _colsum(v)

    blk = (lambda b, i: (b * nb + nb - 1 - i)) if reverse else (lambda b, i: (b * nb + i))
    in_specs = [pl.BlockSpec((t, LANES), lambda b, i: (blk(b, i), col_block))]
    args = [x]
    if vec is not None:
        in_specs.append(pl.BlockSpec(vec.shape, lambda b, i: (0, 0)))
        args.append(vec)
    return pl.pallas_call(
        body, name=name, grid=(n_seq, nb), in_specs=in_specs,
        out_specs=pl.BlockSpec((t, LANES), lambda b, i: (blk(b, i), 0)),
        out_shape=jax.ShapeDtypeStruct((n_seq * seq, LANES), F32),
        scratch_shapes=[pltpu.VMEM((1, LANES), F32)],
        compiler_params=_params(("arbitrary", "arbitrary")),
    )(*args)


def _log_sigmoid(z):
    return -(jnp.maximum(-z, 0.0) + jnp.log(1.0 + jnp.exp(-jnp.abs(z))))


def _shift_down(u, prev_ref, n):
    out = pltpu.roll(u, n, 0)
    row = lax.broadcasted_iota(jnp.int32, u.shape, 0)
    for r in range(n):
        out = jnp.where(row == r, prev_ref[8 - n + r:8 - n + r + 1, :], out)
    return out


def _shift_up(u, next_ref, n):
    ts = u.shape[0]
    out = pltpu.roll(u, ts - n, 0)
    row = lax.broadcasted_iota(jnp.int32, u.shape, 0)
    for r in range(n):
        out = jnp.where(row == ts - n + r, next_ref[r:r + 1, :], out)
    return out


def _conv_taps(u, prev_ref, w_ref, b_ref):
    s1, s2 = _shift_down(u, prev_ref, 1), _shift_down(u, prev_ref, 2)
    return (w_ref[0:1, :] * s2 + w_ref[1:2, :] * s1 + w_ref[2:3, :] * u) + b_ref[...], s1, s2


def _conv_glu_fwd(u_il, cw_il, cb_il, n_seq, seq, wt):
    n_tok, two_f = u_il.shape
    nct = two_f // (2 * wt)
    ts = _tile(seq, 256, 8)
    ns = seq // ts

    def body(u_ref, w_ref, b_ref, a_ref, carry):
        @pl.when(pl.program_id(2) == 0)
        def _():
            carry[...] = jnp.zeros_like(carry)

        u = u_ref[...].astype(F32)
        uc, _, _ = _conv_taps(u, carry, w_ref, b_ref)
        gel, _ = _gelu_parts(uc[:, :wt])
        a_ref[...] = (gel * uc[:, wt:]).astype(a_ref.dtype)
        carry[...] = u[ts - 8:, :]

    return pl.pallas_call(
        body, name="conv_glu_fwd", grid=(nct, n_seq, ns),
        in_specs=[pl.BlockSpec((ts, 2 * wt), lambda j, b, s: (b * ns + s, j)),
                  pl.BlockSpec((3, 2 * wt), lambda j, b, s: (0, j)),
                  pl.BlockSpec((1, 2 * wt), lambda j, b, s: (0, j))],
        out_specs=pl.BlockSpec((ts, wt), lambda j, b, s: (b * ns + s, j)),
        out_shape=jax.ShapeDtypeStruct((n_tok, two_f // 2), BF16),
        scratch_shapes=[pltpu.VMEM((8, 2 * wt), F32)],
        compiler_params=_params(("parallel", "arbitrary", "arbitrary")),
    )(u_il, cw_il, cb_il)


def _conv_glu_bwd_pre(u_il, da, cw_il, cb_il, n_seq, seq, wt):
    n_tok, two_f = u_il.shape
    nct = two_f // (2 * wt)
    ts = _tile(seq, 256, 8)
    ns = seq // ts

    def body(u_ref, da_ref, w_ref, b_ref, d_ref, acc_ref, carry):
        first = jnp.logical_and(pl.program_id(1) == 0, pl.program_id(2) == 0)

        @pl.when(first)
        def _():
            acc_ref[...] = jnp.zeros_like(acc_ref)

        @pl.when(pl.program_id(2) == 0)
        def _():
            carry[...] = jnp.zeros_like(carry)

        u = u_ref[...].astype(F32)
        uc, s1, s2 = _conv_taps(u, carry, w_ref, b_ref)
        gel, dgel = _gelu_parts(uc[:, :wt])
        da_v = da_ref[...].astype(F32)
        d = jnp.concatenate([da_v * uc[:, wt:] * dgel, da_v * gel], axis=1)
        d_ref[...] = d.astype(d_ref.dtype)
        acc_ref[0:1, :] += _colsum(d * s2)
        acc_ref[1:2, :] += _colsum(d * s1)
        acc_ref[2:3, :] += _colsum(d * u)
        acc_ref[3:4, :] += _colsum(d)
        carry[...] = u[ts - 8:, :]

    return pl.pallas_call(
        body, name="conv_glu_bwd_pre", grid=(nct, n_seq, ns),
        in_specs=[pl.BlockSpec((ts, 2 * wt), lambda j, b, s: (b * ns + s, j)),
                  pl.BlockSpec((ts, wt), lambda j, b, s: (b * ns + s, j)),
                  pl.BlockSpec((3, 2 * wt), lambda j, b, s: (0, j)),
                  pl.BlockSpec((1, 2 * wt), lambda j, b, s: (0, j))],
        out_specs=[pl.BlockSpec((ts, 2 * wt), lambda j, b, s: (b * ns + s, j)),
                   pl.BlockSpec((8, 2 * wt), lambda j, b, s: (0, j))],
        out_shape=[jax.ShapeDtypeStruct((n_tok, two_f), BF16), jax.ShapeDtypeStruct((8, two_f), F32)],
        scratch_shapes=[pltpu.VMEM((8, 2 * wt), F32)],
        compiler_params=_params(("parallel", "arbitrary", "arbitrary")),
    )(u_il, da, cw_il, cb_il)


def _conv_bwd_input(d_il, cw_il, n_seq, seq, wt):
    n_tok, two_f = d_il.shape
    nct = two_f // (2 * wt)
    ts = _tile(seq, 256, 8)
    ns = seq // ts

    def body(d_ref, w_ref, o_ref, carry):
        @pl.when(pl.program_id(2) == 0)
        def _():
            carry[...] = jnp.zeros_like(carry)

        d = d_ref[...].astype(F32)
        o_ref[...] = (w_ref[2:3, :] * d + w_ref[1:2, :] * _shift_up(d, carry, 1)
                      + w_ref[0:1, :] * _shift_up(d, carry, 2)).astype(o_ref.dtype)
        carry[...] = d[:8, :]

    rev = lambda j, b, s: (b * ns + ns - 1 - s, j)
    return pl.pallas_call(
        body, name="conv_bwd_input", grid=(nct, n_seq, ns),
        in_specs=[pl.BlockSpec((ts, 2 * wt), rev), pl.BlockSpec((3, 2 * wt), lambda j, b, s: (0, j))],
        out_specs=pl.BlockSpec((ts, 2 * wt), rev),
        out_shape=jax.ShapeDtypeStruct((n_tok, two_f), BF16),
        scratch_shapes=[pltpu.VMEM((8, 2 * wt), F32)],
        compiler_params=_params(("parallel", "arbitrary", "arbitrary")),
    )(d_il, cw_il)


HBM = pl.BlockSpec(memory_space=pltpu.HBM)
_CHIP_FLIPS = ((1, 0), (0, 1), (1, 1))


def _place():
    x, y, c = lax.axis_index("x"), lax.axis_index("y"), lax.axis_index("c")
    return x, y, c, 2 * x + y


def _flip(v, f):
    return 1 - v if f else v


def _half_rows(c, half):
    return pl.ds(pl.multiple_of(c * half, 16), half)


def _remote(src, dst, ssem, rsem, dev):
    return pltpu.make_async_remote_copy(src_ref=src, dst_ref=dst, send_sem=ssem, recv_sem=rsem,
                                        device_id=dev, device_id_type=MESH)


def _comm_call(name, body, ins, out_shapes, n_sems):
    return pl.pallas_call(
        body, name=name, in_specs=[HBM] * len(ins), out_specs=[HBM] * len(out_shapes),
        out_shape=[pltpu.HBM(s.shape, s.dtype) for s in out_shapes],
        scratch_shapes=[pltpu.SemaphoreType.DMA((n_sems,)), pltpu.SemaphoreType.DMA((n_sems,))],
    )(*ins)


def _all_gather_weights(shards, smalls):
    side = _gather_side(shards, smalls)
    nt = len(shards) + len(smalls)

    def body(*refs):
        for part in (side.start, side.mid, side.end):
            part(refs[:nt], refs[nt:2 * nt], *refs[2 * nt:])

    res = _comm_call("all_gather_weights", body, side.ins, side.outs, side.n_sems)
    return res[:len(shards)], res[len(shards):]


def _pair_split(name, grads):
    n = len(grads)

    def body(*refs):
        src, got = refs[:n], refs[n:2 * n]
        ssem, rsem = refs[2 * n:]
        x, y, c, _ = _place()
        cps = []
        for w in range(n):
            half = grads[w].shape[1] // 2
            cp = _remote(src[w].at[:, _half_rows(1 - c, half)], got[w], ssem.at[w], rsem.at[w], (x, y, 1 - c))
            cp.start()
            cps.append(cp)
        for cp in cps:
            cp.wait()

    outs = [jax.ShapeDtypeStruct((g.shape[0], g.shape[1] // 2, g.shape[2]), g.dtype) for g in grads]
    return _comm_call(name, body, grads, outs, n)


class _Side:
    def __init__(self, ins, outs, n_sems, start, mid, end, mid_step=None):
        self.ins, self.outs, self.n_sems = list(ins), list(outs), n_sems
        self.start, self.mid, self.end, self.mid_step = start, mid, end, mid_step
        self.aliases = {}


def _scatter_side(parts):
    n = len(parts)

    def copies(src, dst, ssem, rsem):
        x, y, c, _ = _place()
        out = []
        for w in range(n):
            for k, (fx, fy) in enumerate(_CHIP_FLIPS):
                px, py = _flip(x, fx), _flip(y, fy)
                out.append(_remote(src[w].at[2 * px + py], dst[w].at[k], ssem.at[w * 3 + k], rsem.at[w * 3 + k], (px, py, c)))
        return out

    def start(src, dst, ssem, rsem):
        for cp in copies(src, dst, ssem, rsem):
            cp.start()

    def end(src, dst, ssem, rsem):
        for cp in copies(src, dst, ssem, rsem):
            cp.wait()

    outs = [jax.ShapeDtypeStruct((3,) + p.shape[1:], p.dtype) for p in parts]
    return _Side(parts, outs, 3 * n, start, None, end)


def _gather_side(shards, smalls, mid_step=None, into=None):
    n, ns = len(shards), len(smalls)
    into = into or [(None, a.shape[0], 0) for a in shards]

    def dst_rows(w, c):
        half = shards[w].shape[0] // 2
        return pl.ds(pl.multiple_of(into[w][2] + c * half, 16), half)

    def ici(src, dst, ssem, rsem, w, k):
        x, y, c, me = _place()
        fx, fy = _CHIP_FLIPS[k]
        rows = _half_rows(c, shards[w].shape[0] // 2)
        return _remote(src[w].at[rows], dst[w].at[me, dst_rows(w, c)], ssem.at[w * 6 + k], rsem.at[w * 6 + k],
                       (_flip(x, fx), _flip(y, fy), c))

    def small(src, dst, ssem, rsem, s, k):
        x, y, c, me = _place()
        fx, fy = _CHIP_FLIPS[k]
        sem = 6 * n + 3 * s + k
        return _remote(src[n + s], dst[n + s].at[me], ssem.at[sem], rsem.at[sem], (_flip(x, fx), _flip(y, fy), c))

    def own(src, dst, ssem, rsem, i):
        x, y, c, me = _place()
        sem = 6 * n + 3 * ns + i
        if i < n:
            to = dst[i].at[me, pl.ds(into[i][2], shards[i].shape[0])]
        else:
            to = dst[i].at[me]
        return _remote(src[i], to, ssem.at[sem], rsem.at[sem], (x, y, 1 - c))

    def landed(dst, ssem, rsem, w, k, sender_c, sem_off):
        x, y, c, _ = _place()
        fx, fy = _CHIP_FLIPS[k]
        got = dst[w].at[2 * _flip(x, fx) + _flip(y, fy), dst_rows(w, sender_c)]
        return _remote(got, got, ssem.at[w * 6 + sem_off + k], rsem.at[w * 6 + sem_off + k], (x, y, 1 - c))

    def start(src, dst, ssem, rsem):
        for i in range(n + ns):
            own(src, dst, ssem, rsem, i).start()
        for s in range(ns):
            for k in range(3):
                small(src, dst, ssem, rsem, s, k).start()
        for w in range(n):
            for k in range(3):
                ici(src, dst, ssem, rsem, w, k).start()

    def mid(src, dst, ssem, rsem):
        c = lax.axis_index("c")
        for w in range(n):
            for k in range(3):
                landed(dst, ssem, rsem, w, k, c, 0).wait_recv()
                landed(dst, ssem, rsem, w, k, c, 3).start()

    def end(src, dst, ssem, rsem):
        c = lax.axis_index("c")
        for w in range(n):
            for k in range(3):
                landed(dst, ssem, rsem, w, k, 1 - c, 3).wait_recv()
        for i in range(n + ns):
            own(src, dst, ssem, rsem, i).wait()
        for s in range(ns):
            for k in range(3):
                small(src, dst, ssem, rsem, s, k).wait()
        for w in range(n):
            for k in range(3):
                ici(src, dst, ssem, rsem, w, k).wait_send()
                landed(dst, ssem, rsem, w, k, c, 3).wait_send()

    outs = [jax.ShapeDtypeStruct((N_CHIPS, rows, a.shape[1]), a.dtype) for a, (_, rows, _) in zip(shards, into)]
    outs += [jax.ShapeDtypeStruct((N_CHIPS,) + a.shape, a.dtype) for a in smalls]
    filled = [(w, arr) for w, (arr, _, _) in enumerate(into) if arr is not None]
    side = _Side(list(shards) + list(smalls) + [arr for _, arr in filled], outs, 7 * n + 4 * ns, start, mid, end, mid_step)
    side.aliases = {n + ns + i: w for i, (w, _) in enumerate(filled)}
    return side


def _host(body, n_in, n_out, side, grid):
    if side is None:
        return body
    ns_in, ns_out = len(side.ins), len(side.outs)
    n_steps = math.prod(grid)
    mid_step = side.mid_step
    if side.mid is not None and not isinstance(mid_step, int):
        mid_step = min(n_steps - 1, int(mid_step * n_steps))

    def wrapped(*refs):
        ins, s_ins = refs[:n_in], refs[n_in:n_in + ns_in]
        outs = refs[n_in + ns_in:n_in + ns_in + n_out]
        s_outs = refs[n_in + ns_in + n_out:n_in + ns_in + n_out + ns_out]
        rest = refs[n_in + ns_in + n_out + ns_out:]
        sems = rest[-2:]
        step = 0
        for axis, extent in enumerate(grid):
            step = step * extent + pl.program_id(axis)

        @pl.when(step == 0)
        def _():
            side.start(s_ins, s_outs, *sems)

        if side.mid is not None:
            @pl.when(step == mid_step)
            def _():
                side.mid(s_ins, s_outs, *sems)

        body(*ins, *outs, *rest[:-2])

        @pl.when(step == n_steps - 1)
        def _():
            side.end(s_ins, s_outs, *sems)

    return wrapped


def _hosted_call(body, name, grid, in_specs, out_specs, out_shape, args, side, semantics=("parallel", "arbitrary"),
                 scratch=()):
    n_in, n_out = len(in_specs), len(out_specs)
    kern = _host(body, n_in, n_out, side, grid)
    if side is None:
        return pl.pallas_call(kern, name=name, grid=grid, in_specs=in_specs, out_specs=out_specs, out_shape=out_shape,
                              scratch_shapes=list(scratch), compiler_params=_params(semantics))(*args), []
    res = pl.pallas_call(
        kern, name=name, grid=grid, in_specs=in_specs + [HBM] * len(side.ins), out_specs=out_specs + [HBM] * len(side.outs),
        out_shape=list(out_shape) + [pltpu.HBM(s.shape, s.dtype) for s in side.outs],
        scratch_shapes=list(scratch) + [pltpu.SemaphoreType.DMA((side.n_sems,)), pltpu.SemaphoreType.DMA((side.n_sems,))],
        input_output_aliases={n_in + i: n_out + o for i, o in side.aliases.items()},
        compiler_params=_params(("arbitrary",) * len(grid)),
    )(*args, *side.ins)
    return res[:n_out], res[n_out:]


def _pair_swap(halves):
    n = len(halves)

    def body(*refs):
        src, dst = refs[:n], refs[n:2 * n]
        ssem, rsem = refs[2 * n:]
        x, y, c, _ = _place()
        cps = []
        for w in range(n):
            cp = _remote(src[w], dst[w], ssem.at[w], rsem.at[w], (x, y, 1 - c))
            cp.start()
            cps.append(cp)
        for cp in cps:
            cp.wait()

    outs = [jax.ShapeDtypeStruct(h.shape, h.dtype) for h in halves]
    return _comm_call("rs_pair_swap", body, halves, outs, n)


def _gather_small(vec):
    def body(src, dst, ssem, rsem):
        x, y, c, _ = _place()
        me = 4 * x + 2 * y + c
        cps = []
        for r in range(1, 8):
            dev = (_flip(x, r & 4), _flip(y, r & 2), _flip(c, r & 1))
            cp = _remote(src, dst.at[me], ssem.at[r - 1], rsem.at[r - 1], dev)
            cp.start()
            cps.append(cp)
        for cp in cps:
            cp.wait()

    out = jax.ShapeDtypeStruct((8,) + vec.shape, vec.dtype)
    return _comm_call("gather_small", body, [vec], [out], 7)[0]


def _pair_add(name, g, theirs, core):
    n, half, b = theirs.shape
    tr = _tile(half, 256, 16)
    nt = half // tr

    def body(c_ref, g_ref, t_ref, o_ref):
        o_ref[...] = (g_ref[...].astype(F32) + t_ref[...].astype(F32)).astype(o_ref.dtype)

    same = pl.BlockSpec((None, tr, b), lambda j, i, c: (j, i, 0))
    grid_spec = pltpu.PrefetchScalarGridSpec(
        num_scalar_prefetch=1, grid=(n, nt),
        in_specs=[pl.BlockSpec((None, tr, b), lambda j, i, c: (j, c[0] * nt + i, 0)), same], out_specs=same)
    return pl.pallas_call(body, name=name, grid_spec=grid_spec, out_shape=jax.ShapeDtypeStruct(theirs.shape, BF16),
                          compiler_params=_params(("parallel", "parallel")))(core, g, theirs)


def _chip_sum(name, parts, landed, chip1):
    n, r, c = landed.shape
    tr = _tile(r, 256, 16)

    def body(ix_ref, p_ref, s_ref, o_ref):
        acc = p_ref[...].astype(F32)
        for s in range(n):
            acc = acc + s_ref[s].astype(F32)
        o_ref[...] = acc

    grid_spec = pltpu.PrefetchScalarGridSpec(
        num_scalar_prefetch=1, grid=(r // tr,),
        in_specs=[pl.BlockSpec((None, tr, c), lambda i, ix: (ix[0], i, 0)), pl.BlockSpec((n, tr, c), lambda i, ix: (0, i, 0))],
        out_specs=pl.BlockSpec((tr, c), lambda i, ix: (i, 0)))
    return pl.pallas_call(body, name=name, grid_spec=grid_spec, out_shape=jax.ShapeDtypeStruct((r, c), F32),
                          compiler_params=_params(("parallel",)))(chip1, parts, landed)


def _sum_slots(name, stacked):
    n, r, c = stacked.shape
    tr = _tile(r, 256, 8)

    def body(s_ref, o_ref):
        acc = s_ref[0]
        for s in range(1, n):
            acc = acc + s_ref[s]
        o_ref[...] = acc

    return pl.pallas_call(
        body, name=name, grid=(r // tr,), in_specs=[pl.BlockSpec((n, tr, c), lambda i: (0, i, 0))],
        out_specs=pl.BlockSpec((tr, c), lambda i: (i, 0)),
        out_shape=jax.ShapeDtypeStruct((r, c), F32), compiler_params=_params(("parallel",)),
    )(stacked)


def _adam_math(w, g, m, v):
    bc1, bc2 = 1.0 - ADAM_B1 ** ADAM_STEP, 1.0 - ADAM_B2 ** ADAM_STEP
    nm = ADAM_B1 * m + (1.0 - ADAM_B1) * g
    nv = ADAM_B2 * v + (1.0 - ADAM_B2) * (g * g)
    return -ADAM_LR * ((nm / bc1) / (jnp.sqrt(nv / bc2) + ADAM_EPS) + ADAM_WD * w), nm, nv


def _adamw_halves(name, w, g_mine, g_theirs, m, v, core):
    r, c = w.shape
    h = r // 2
    tr = _tile(h, 128, 8)
    nth = h // tr

    def body(c_ref, w_ref, gm_ref, gt_ref, m_ref, v_ref, g_ref, d_ref, nm_ref, nv_ref):
        g = jnp.where(pl.program_id(0) // nth == c_ref[0], gm_ref[...], gt_ref[...])
        g_ref[...] = g
        d_ref[...], nm_ref[...], nv_ref[...] = _adam_math(w_ref[...], g, m_ref[...], v_ref[...])

    full = pl.BlockSpec((tr, c), lambda i, cr: (i, 0))
    half = pl.BlockSpec((tr, c), lambda i, cr: (i % nth, 0))
    grid_spec = pltpu.PrefetchScalarGridSpec(num_scalar_prefetch=1, grid=(r // tr,),
                                             in_specs=[full, half, half, full, full], out_specs=[full] * 4)
    return pl.pallas_call(body, name=name, grid_spec=grid_spec, out_shape=[jax.ShapeDtypeStruct((r, c), F32)] * 4,
                          compiler_params=_params(("parallel",)))(core, w, g_mine, g_theirs, m, v)


def _adamw(name, w, g, m, v):
    r, c = w.shape
    by_cols = r % 8 != 0 and c % LANES == 0
    tr, tc = (r, _tile(c, 256, LANES)) if by_cols else (_tile(r, 256, 8), c)

    def body(w_ref, g_ref, m_ref, v_ref, d_ref, nm_ref, nv_ref):
        d_ref[...], nm_ref[...], nv_ref[...] = _adam_math(w_ref[...], g_ref[...], m_ref[...], v_ref[...])

    spec = pl.BlockSpec((tr, tc), (lambda i: (0, i)) if by_cols else (lambda i: (i, 0)))
    return pl.pallas_call(
        body, name=name, grid=(c // tc if by_cols else r // tr,), in_specs=[spec] * 4, out_specs=[spec] * 3,
        out_shape=[jax.ShapeDtypeStruct((r, c), F32)] * 3, compiler_params=_params(("parallel",)),
    )(w, g, m, v)


def _pad_cols(a, cols):
    return jnp.pad(a, ((0, 0), (0, cols - a.shape[1])))


def _rot_cols(w):
    h = w.shape[-1] // 2
    return jnp.concatenate([-w[..., h:], w[..., :h]], axis=-1)


def _unrot_cols(d):
    h = d.shape[-1] // 2
    return jnp.concatenate([d[..., h:], -d[..., :h]], axis=-1)


def _logical(g):
    return jnp.transpose(g, (1, 0, 2)).reshape(g.shape[1], N_CHIPS * g.shape[2])


def _chunks(a, n):
    return jnp.transpose(a.reshape(a.shape[0], N_CHIPS, n), (1, 0, 2))


def kernel(x, positions, pre_mix_norm, w_in, q_a_norm, w_uq, kv_a_norm, w_ukv, b_forget, b_gate, w_branch_mla, w_branch_fox, w_out, post_mix_norm, pre_ffn_norm, w_up, conv_w, conv_b, w_down, post_ffn_norm, loss_target, m_pre_mix_norm, m_w_in, m_q_a_norm, m_w_uq, m_kv_a_norm, m_w_ukv, m_b_forget, m_b_gate, m_w_branch_mla, m_w_branch_fox, m_w_out, m_post_mix_norm, m_pre_ffn_norm, m_w_up, m_conv_w, m_conv_b, m_w_down, m_post_ffn_norm, v_pre_mix_norm, v_w_in, v_q_a_norm, v_w_uq, v_kv_a_norm, v_w_ukv, v_b_forget, v_b_gate, v_w_branch_mla, v_w_branch_fox, v_w_out, v_post_mix_norm, v_pre_ffn_norm, v_w_up, v_conv_w, v_conv_b, v_w_down, v_post_ffn_norm):
    n_seq, seq, d = x.shape
    n_tok = n_seq * seq
    d_in = N_CHIPS * w_in.shape[1]
    two_f = N_CHIPS * w_up.shape[1]
    ff_dim = two_f // 2
    assert d_in == QL + KVL + ROPE + 3 * HEADS * FDIM + HEADS + 2 * d
    n_in_shard = w_in.shape[1]
    in_pad = -(-n_in_shard // LANES) * LANES
    hd = HEADS * LANES
    xc, yc, cc = lax.axis_index("x"), lax.axis_index("y"), lax.axis_index("c")
    chip = 2 * xc + yc
    t_attn = _tile(seq, 512, 128)

    shards = [_pad_cols(w_in, in_pad).astype(BF16), w_uq.astype(BF16), w_ukv.astype(BF16), w_branch_mla.astype(BF16),
              w_branch_fox.astype(BF16), w_out.astype(BF16), w_up.astype(BF16), w_down.astype(BF16)]
    cw8 = jnp.pad(conv_w, ((0, 5), (0, 0)))
    (g_in,), _ = _all_gather_weights(shards[:1], [])
    side_proj = _gather_side([shards[3], shards[4], shards[5], shards[1], shards[2]], [cw8],
                             mid_step=0.9)
    side_ffn = _gather_side([shards[7]], [], mid_step=0.7)
    up_rows = shards[6].shape[0]
    up_cuts = [0, up_rows // 8, up_rows // 2, 7 * up_rows // 8, up_rows]

    def side_up(piece, filled, mid_step):
        lo, hi = up_cuts[piece], up_cuts[piece + 1]
        return _gather_side([shards[6][lo:hi]], [], mid_step=mid_step, into=[(filled, up_rows, lo)])

    o_q, o_kv, o_kpe = 0, QL, QL + KVL
    o_f = o_kpe + ROPE
    o_fl = o_f + 3 * hd
    o_g = o_fl + HEADS

    def chip_cols(lo, hi):
        out = []
        while lo < hi:
            j = lo // n_in_shard
            end = min(hi, (j + 1) * n_in_shard)
            out.append((j, lo - j * n_in_shard, end - j * n_in_shard))
            lo = end
        return out

    take = lambda lo, hi: [g_in[j, :, a:b] for j, a, b in chip_cols(lo, hi)]
    w_kpe = jnp.concatenate(take(o_kpe, o_f), axis=1)
    zeros = lambda n: jnp.zeros((d, n), BF16)
    win_p = jnp.concatenate(
        take(o_g, d_in) + take(o_q, o_kpe) + [w_kpe, zeros(LANES - ROPE), _rot_cols(w_kpe)] + take(o_fl, o_g)
        + [zeros(LANES - ROPE - HEADS)] + take(o_f, o_fl), axis=1)
    n_p = win_p.shape[1]
    cb_gm, cb_gf = 0, 1
    c_lat = 2 * d
    c_kx, c_kr = c_lat + QL + KVL, c_lat + QL + KVL + LANES
    n_pa = c_kr + LANES
    assert n_p == n_pa + 3 * hd

    n_bm, n_up = w_branch_mla.shape[1], w_up.shape[1]
    l_bm, l_up = _Chunked(n_bm), _Chunked(n_up)
    wt = n_up // 2
    n_ut = two_f // wt
    il = lambda cblk: jnp.where(cblk < n_ut // 2, 2 * cblk, 2 * (cblk - n_ut // 2) + 1)
    l_il = _Plain(il)
    to_il = lambda a: a.reshape(a.shape[0], 2, n_ut // 2, wt).transpose(0, 2, 1, 3).reshape(a.shape[0], two_f)
    from_il = lambda a: a.reshape(a.shape[0], n_ut // 2, 2, wt).transpose(0, 2, 1, 3).reshape(a.shape[0], two_f)

    inv_freq = 1.0 / (ROPE_THETA ** (jnp.arange(0, ROPE, 2, dtype=F32) / ROPE))
    ang = positions.astype(F32).reshape(n_tok, 1) * inv_freq
    cos, sin = jnp.cos(ang), jnp.sin(ang)
    cs = _pad_cols(jnp.concatenate([cos, cos], axis=1), LANES)
    sn = _pad_cols(jnp.concatenate([sin, sin], axis=1), LANES)

    row = lambda v: v.reshape(1, -1)
    x2 = x.reshape(n_tok, d)
    tgt = loss_target.reshape(n_tok, d)

    (h,) = _rows("rms_pre_mix", lambda r, v: ([r[0] * _rstd(r[0]) * v[0]], []),
                 [(x2, d, 0)], [row(pre_mix_norm)], [(d, BF16)], [], n_tok)
    proj, (g_bm, g_bf, g_out, g_uq, g_ukv, g_cw) = _mm("proj_in", "nn", h, win_p, n_tok, n_pa, d, tm=2048, side=side_proj)
    w_out_full = g_out.reshape(d, d)
    uq3 = _logical(g_uq).reshape(QL, HEADS, NOPE + ROPE)
    pe = uq3[:, :, NOPE:]
    pad_pe = lambda a: jnp.pad(a, ((0, 0), (0, 0), (0, LANES - ROPE))).reshape(QL, hd)
    wuq_p = jnp.concatenate([uq3[:, :, :NOPE].reshape(QL, hd), pad_pe(pe), pad_pe(_rot_cols(pe))], axis=1)
    ukv3 = _logical(g_ukv).reshape(KVL, HEADS, NOPE + VDIM)
    wukv_p = jnp.concatenate([ukv3[:, :, :NOPE].reshape(KVL, hd), ukv3[:, :, NOPE:].reshape(KVL, hd)], axis=1)
    tn_f = _tile(3 * hd, 1024, 128)
    assert n_pa % tn_f == 0
    proj_f, (g_up,) = _mm("proj_in_fox", "nn", h, win_p, n_tok, 3 * hd, d, tn=tn_f, lb=_Plain(lambda cblk: cblk + n_pa // tn_f),
                          out_dtype=BF16, side=side_up(0, None, 0.8))

    bf_vec = jnp.pad(row(b_forget), ((0, 0), (ROPE, LANES - ROPE - HEADS)))

    def lat_fwd(r, v):
        ql, kvl = r[0], r[1]
        return [ql * _rstd(ql) * v[0], kvl * _rstd(kvl) * v[1], r[2] * r[4] + r[3] * r[5]], []

    qn, kvn, rk = _rows("latent_norms", lat_fwd,
                        [(proj, QL, c_lat // QL), (proj, KVL, (c_lat + QL) // KVL), (proj, LANES, c_kx // LANES),
                         (proj, LANES, c_kr // LANES), (cs, LANES, 0), (sn, LANES, 0)],
                        [row(q_a_norm), row(kv_a_norm)], [(QL, BF16), (KVL, BF16), (LANES, BF16)], [], n_tok)
    q_p = _mm("q_up", "nn", qn, wuq_p, n_tok, 3 * hd, QL)
    kv_p = _mm("kv_up", "nn", kvn, wukv_p, n_tok, 2 * hd, KVL, out_dtype=BF16)

    def rope_q(r, v):
        c8, s8 = jnp.tile(r[3], (1, HEADS)), jnp.tile(r[4], (1, HEADS))
        return [r[0], r[1] * c8 + r[2] * s8], []

    q_nope, rq = _rows("rope_q", rope_q, [(q_p, hd, 0), (q_p, hd, 1), (q_p, hd, 2), (cs, LANES, 0), (sn, LANES, 0)], [],
                       [(hd, BF16), (hd, BF16)], [], n_tok)

    mla_q = [(q_nope, 0, False), (rq, 0, False)]
    mla_k = [(kv_p, 0, False), (rk, 0, True)]
    mla_v = (kv_p, 1)
    mla_scale = (NOPE + ROPE) ** -0.5
    t_fwd = _tile(seq, 1024, 128)
    n_fwd_steps = n_seq * (seq // t_fwd)
    (o_mla, lse_mla), (g_up,) = _attn_fwd("mla_fwd", mla_q, mla_k, mla_v, None, CHUNK, mla_scale, n_seq, seq, t_fwd,
                                          side=side_up(1, g_up, n_fwd_steps - 1))

    c_run = _seq_cumsum("forget_cumsum", proj, c_kr // LANES, n_seq, seq, False,
                        pre=lambda z, b: _log_sigmoid(z + b), vec=bf_vec)
    nb_attn = seq // t_attn
    c_rowf = jnp.transpose(c_run[:, ROPE:ROPE + HEADS].reshape(n_seq, nb_attn, t_attn, HEADS), (0, 1, 3, 2))
    fox_q, fox_k, fox_v = [(proj_f, 0, False)], [(proj_f, 1, False)], (proj_f, 2)
    fox_scale = FDIM ** -0.5
    fox_bias = (c_run, c_rowf)
    c_rowf_fwd = jnp.transpose(c_run[:, ROPE:ROPE + HEADS].reshape(n_seq, seq // t_fwd, t_fwd, HEADS), (0, 1, 3, 2))
    (o_fox, lse_fox), (g_up,) = _attn_fwd("fox_fwd", fox_q, fox_k, fox_v, (c_run, c_rowf_fwd), 1, fox_scale, n_seq, seq, t_fwd,
                                          side=side_up(2, g_up, n_fwd_steps - 1))

    pm, (g_up,) = _mm("branch_mla", "nn", o_mla, g_bm, n_tok, d, hd, lb=l_bm, tn=n_bm, out_dtype=BF16,
                      side=side_up(3, g_up, 0.7))
    pf = _mm("branch_fox", "nn", o_fox, g_bf, n_tok, d, hd, lb=l_bm, tn=n_bm, out_dtype=BF16)
    bg = row(b_gate)

    def merge(r, v):
        return [_sigmoid(r[0] + v[0]) * r[2] + _sigmoid(r[1] + v[1]) * r[3]], []

    (merged,) = _rows("gate_merge", merge, [(proj, d, cb_gm), (proj, d, cb_gf), (pm, d, 0), (pf, d, 0)],
                      [bg[:, :d], bg[:, d:]], [(d, BF16)], [], n_tok)
    y1 = _mm("mix_out", "nn", merged, w_out_full, n_tok, d, d)

    def resid_norm(r, v):
        x1v = r[0] + r[1] * _rstd(r[1]) * v[0]
        return [x1v, x1v * _rstd(x1v) * v[1]], []

    x1, h2 = _rows("post_mix_pre_ffn", resid_norm, [(x2, d, 0), (y1, d, 0)], [row(post_mix_norm), row(pre_ffn_norm)],
                   [(d, F32), (d, BF16)], [], n_tok)

    u_il, got = _mm("ffn_up", "nn", h2, g_up, n_tok, two_f, d, lb=l_up, lo=l_il, tm=2048, tn=wt, out_dtype=BF16,
                    side=side_ffn)
    w_down_full = got[0].reshape(ff_dim, d)
    cw_il = to_il(_logical(g_cw)[:3])
    cb_il = to_il(row(conv_b))
    act = _conv_glu_fwd(u_il, cw_il, cb_il, n_seq, seq, wt)
    ff = _mm("ffn_down", "nn", act, w_down_full, n_tok, d, ff_dim)

    def final(r, v):
        x1v, ffv, tg = r
        diff = x1v + ffv * _rstd(ffv) * v[0] - tg
        dx2v = diff / d
        dffv, dg4 = _rms_bwd(ffv, v[0], dx2v)
        sq = jnp.sum(jnp.sum(diff * diff, axis=1, keepdims=True), axis=0, keepdims=True)
        return [dx2v, dffv], [dg4, jnp.broadcast_to(sq, (1, LANES))]

    dx2, dff, dg_post_ffn, sq_sum = _rows("loss_post_ffn_bwd", final, [(x1, d, 0), (ff, d, 0), (tgt, d, 0)],
                                          [row(post_ffn_norm)], [(d, F32), (d, BF16)], [(1, d), (1, LANES)], n_tok)
    rs_parts, rs_landed = {}, {}
    core = jnp.reshape(cc, (1,)).astype(jnp.int32)

    def pair_reduce(tag, names, grads):
        theirs = _pair_split("rs_pair_split_" + tag, grads)
        for nm, g, b in zip(names, grads, theirs):
            rs_parts[nm] = _pair_add("rs_pair_add_" + nm, g, b, core)

    dact = _mm("ffn_down_dx", "nt", dff, w_down_full, n_tok, ff_dim, d, tm=2048, tn=wt, out_dtype=BF16)
    gw_down = _mm("ffn_down_dw", "tn", act, dff, ff_dim, d, n_tok, tm=wt, out_dtype=BF16)
    pair_reduce("down", ["w_down"], [gw_down.reshape(N_CHIPS, ff_dim // N_CHIPS, d)])
    d_il, conv_acc = _conv_glu_bwd_pre(u_il, dact, cw_il, cb_il, n_seq, seq, wt)
    du_il = _conv_bwd_input(d_il, cw_il, n_seq, seq, wt)
    gw_up, got = _mm("ffn_up_dw", "tn", h2, du_il, d, two_f, n_tok, lb=l_il, lo=l_up, tn=wt, out_dtype=BF16,
                     side=_scatter_side([rs_parts["w_down"]]))
    rs_landed["w_down"] = got[0]
    pair_reduce("up", ["w_up"], [gw_up])
    dh2, got = _mm("ffn_up_dx", "nt", du_il, g_up, n_tok, d, two_f, la=l_il, lb=l_up, tk=wt,
                   side=_scatter_side([rs_parts["w_up"]]))
    rs_landed["w_up"] = got[0]

    def mid_bwd(r, v):
        x1v, y1v, dx2v, dh2v = r
        d3, dg3 = _rms_bwd(x1v, v[1], dh2v)
        dx1v = dx2v + d3
        dy1v, dg2 = _rms_bwd(y1v, v[0], dx1v)
        return [dx1v, dy1v], [dg3, dg2]

    dx1, dy1, dg_pre_ffn, dg_post_mix = _rows(
        "pre_ffn_post_mix_bwd", mid_bwd, [(x1, d, 0), (y1, d, 0), (dx2, d, 0), (dh2, d, 0)],
        [row(post_mix_norm), row(pre_ffn_norm)], [(d, F32), (d, BF16)], [(1, d), (1, d)], n_tok)
    dmerged = _mm("mix_out_dx", "nt", dy1, w_out_full, n_tok, d, d, out_dtype=BF16)
    gw_out = _mm("mix_out_dw", "tn", merged, dy1, d, d, n_tok, out_dtype=BF16)

    def gate_bwd(r, v):
        zm, zf, pmv, pfv, dm = r
        gm, gf = _sigmoid(zm + v[0]), _sigmoid(zf + v[1])
        dzm, dzf = dm * pmv * gm * (1.0 - gm), dm * pfv * gf * (1.0 - gf)
        return [dm * gm, dm * gf, jnp.concatenate([dzm, dzf], axis=1)], [_colsum(dzm), _colsum(dzf)]

    dpm, dpf, dz, dbg_m, dbg_f = _rows(
        "gate_merge_bwd", gate_bwd, [(proj, d, cb_gm), (proj, d, cb_gf), (pm, d, 0), (pf, d, 0), (dmerged, d, 0)],
        [bg[:, :d], bg[:, d:]], [(d, BF16), (d, BF16), (2 * d, BF16)], [(1, d), (1, d)], n_tok)
    tk_b = min(n_bm, 512)
    do_mla = _mm("branch_mla_dx", "nt", dpm, g_bm, n_tok, hd, d, lb=l_bm, tk=tk_b)
    do_fox = _mm("branch_fox_dx", "nt", dpf, g_bf, n_tok, hd, d, lb=l_bm, tk=tk_b)
    gw_bm = _mm("branch_mla_dw", "tn", o_mla, dpm, hd, d, n_tok, lo=l_bm, tn=n_bm, out_dtype=BF16)
    gw_bf = _mm("branch_fox_dw", "tn", o_fox, dpf, hd, d, n_tok, lo=l_bm, tn=n_bm, out_dtype=BF16)

    pair_reduce("mix", ["w_out", "w_branch_mla", "w_branch_fox"], [gw_out.reshape(N_CHIPS, d // N_CHIPS, d), gw_bm, gw_bf])
    delta_mla, dob_mla = _attn_delta("mla_delta", o_mla, do_mla, n_tok)
    (dq_nope, drq, dk_nope, drk_g, dv_mla), got = _attn_bwd(
        "mla_bwd", mla_q, mla_k, mla_v, dob_mla, lse_mla, delta_mla, None, CHUNK, mla_scale, n_seq, seq, t_attn, BF16,
        side=_scatter_side([rs_parts[nm] for nm in ("w_out", "w_branch_mla", "w_branch_fox")]))
    rs_landed.update(zip(("w_out", "w_branch_mla", "w_branch_fox"), got))
    delta_fox, dob_fox = _attn_delta("fox_delta", o_fox, do_fox, n_tok)
    (dfq, dfk, dfv, dc_q, dc_k), _ = _attn_bwd("fox_bwd", fox_q, fox_k, fox_v, dob_fox, lse_fox, delta_fox, fox_bias, 1,
                                               fox_scale, n_seq, seq, t_attn, BF16)
    dc_k8 = jnp.transpose(dc_k, (0, 2, 4, 1, 3)).reshape(n_tok, HEADS)
    dc128 = dc_q[0] + dc_q[1] + jnp.pad(dc_k8, ((0, 0), (ROPE, LANES - ROPE - HEADS)))
    dlogf = _seq_cumsum("forget_cumsum_bwd", dc128, 0, n_seq, seq, True)

    def mla_pack(r, v):
        dqn_v, drq_v, dkn_v, dv_v, drk_a, drk_b, c1, s1 = r
        c8, s8 = jnp.tile(c1, (1, HEADS)), jnp.tile(s1, (1, HEADS))
        drk_v = drk_a + drk_b
        return [jnp.concatenate([dqn_v, drq_v * c8, drq_v * s8], axis=1), jnp.concatenate([dkn_v, dv_v], axis=1),
                drk_v * c1, drk_v * s1], []

    dq_p, dkv_p, dkx, dkr = _rows(
        "mla_rope_bwd", mla_pack,
        [(dq_nope, hd, 0), (drq, hd, 0), (dk_nope, hd, 0), (dv_mla, hd, 0), (drk_g[0], LANES, 0), (drk_g[1], LANES, 0),
         (cs, LANES, 0), (sn, LANES, 0)],
        [], [(3 * hd, BF16), (2 * hd, BF16), (LANES, F32), (LANES, F32)], [], n_tok)
    dqn = _mm("q_up_dx", "nt", dq_p, wuq_p, n_tok, QL, 3 * hd)
    gw_uq_p = _mm("q_up_dw", "tn", qn, dq_p, QL, 3 * hd, n_tok, out_dtype=BF16)
    dkvn = _mm("kv_up_dx", "nt", dkv_p, wukv_p, n_tok, KVL, 2 * hd)
    gw_ukv_p = _mm("kv_up_dw", "tn", kvn, dkv_p, KVL, 2 * hd, n_tok, out_dtype=BF16)

    def lat_bwd(r, v):
        ql, kvl, dqn_v, dkvn_v, dkx_v, dkr_v, zblk, dlf = r
        dql, dgq = _rms_bwd(ql, v[0], dqn_v)
        dkvl, dgkv = _rms_bwd(kvl, v[1], dkvn_v)
        dfl = dlf * _sigmoid(-(zblk + v[2]))
        return [jnp.concatenate([dql, dkvl, dkx_v, dkr_v + dfl], axis=1)], [dgq, dgkv, _colsum(dfl)]

    dlat, dg_q, dg_kv, dbf = _rows(
        "latent_bwd", lat_bwd,
        [(proj, QL, c_lat // QL), (proj, KVL, (c_lat + QL) // KVL), (dqn, QL, 0), (dkvn, KVL, 0), (dkx, LANES, 0),
         (dkr, LANES, 0), (proj, LANES, c_kr // LANES), (dlogf, LANES, 0)],
        [row(q_a_norm), row(kv_a_norm), bf_vec], [(QL + KVL + 2 * LANES, BF16)], [(1, QL), (1, KVL), (1, LANES)], n_tok)
    dproj = [dz, dlat, dfq, dfk, dfv]
    gw_in_p = _mm_parts("proj_in_dw", "tn", h, dproj, d, n_p, n_tok, tk=1024, out_dtype=BF16)

    f32 = lambda a: a.astype(F32)
    kr_blk = gw_in_p[:, c_kr:c_kr + LANES]
    d_kpe = (f32(gw_in_p[:, c_kx:c_kx + ROPE]) + _unrot_cols(f32(kr_blk[:, :ROPE]))).astype(BF16)
    in_pieces = [(o_q, gw_in_p, c_lat, QL + KVL), (o_kpe, d_kpe, 0, ROPE), (o_f, gw_in_p, n_pa, 3 * hd),
                 (o_fl, kr_blk, ROPE, HEADS), (o_g, gw_in_p, 0, 2 * d)]
    gc_in = []
    for j in range(N_CHIPS):
        lo, hi, cols = j * n_in_shard, (j + 1) * n_in_shard, []
        for first, arr, at, width in in_pieces:
            a, b = max(lo, first), min(hi, first + width)
            if a < b:
                cols.append(arr[:, at + a - first:at + b - first])
        cols.append(jnp.zeros((d, in_pad - n_in_shard), BF16))
        gc_in.append(jnp.concatenate(cols, axis=1))
    gc_in = jnp.stack(gc_in)
    uq_parts = [gw_uq_p[:, i * hd:(i + 1) * hd].reshape(QL, HEADS, LANES) for i in range(3)]
    d_pe = (f32(uq_parts[1][:, :, :ROPE]) + _unrot_cols(f32(uq_parts[2][:, :, :ROPE]))).astype(BF16)
    gc_uq = _chunks(jnp.concatenate([uq_parts[0], d_pe], axis=2).reshape(QL, HEADS * (NOPE + ROPE)), w_uq.shape[1])
    gc_ukv = _chunks(jnp.concatenate([gw_ukv_p[:, :hd].reshape(KVL, HEADS, NOPE), gw_ukv_p[:, hd:].reshape(KVL, HEADS, VDIM)],
                                     axis=2).reshape(KVL, HEADS * (NOPE + VDIM)), w_ukv.shape[1])
    grads = [gc_in, gc_uq, gc_ukv]

    late = ["w_in", "w_uq", "w_ukv"]
    pair_reduce("late", late, grads)
    dh, got = _mm_parts("proj_in_dx", "nt", dproj, win_p, n_tok, d, n_p, side=_scatter_side([rs_parts[nm] for nm in late]))
    rs_landed.update(zip(late, got))

    def first_bwd(r, v):
        dxa, dg1 = _rms_bwd(r[0], v[0], r[1])
        return [r[2] + dxa], [dg1]

    grad_x, dg_pre_mix = _rows("pre_mix_bwd", first_bwd, [(x2, d, 0), (dh, d, 0), (dx1, d, 0)], [row(pre_mix_norm)],
                               [(d, F32)], [(1, d)], n_tok)
    big = list(rs_parts)
    chip1 = jnp.reshape(chip, (1,)).astype(jnp.int32)
    halves = [_chip_sum("rs_chip_sum_" + nm, rs_parts[nm], rs_landed[nm], chip1) for nm in big]
    other = _pair_swap(halves)
    g_halves = dict(zip(big, zip(halves, other)))

    conv_acc_l = from_il(conv_acc)
    pieces = [dg_pre_mix, dg_q, dg_kv, dbf, dbg_m, dbg_f, dg_post_mix, dg_pre_ffn, conv_acc_l[3:4], dg_post_ffn,
              conv_acc_l[0:1], conv_acc_l[1:2], conv_acc_l[2:3], sq_sum]
    sizes = [p.shape[1] for p in pieces]
    flat = jnp.concatenate(pieces, axis=1)
    n_rows = -(-flat.shape[1] // (8 * LANES)) * 8
    flat = _pad_cols(flat, n_rows * LANES).reshape(n_rows, LANES)
    slots = lax.dynamic_update_slice(_gather_small(flat), flat[None], (2 * chip + cc, 0, 0))
    total = _sum_slots("small_sum", slots).reshape(1, n_rows * LANES)
    offs = [sum(sizes[:i]) for i in range(len(sizes))]
    tot = [total[0, o:o + s] for o, s in zip(offs, sizes)]
    loss = 0.5 * tot[13][0] / d
    g_small = {"pre_mix_norm": tot[0], "q_a_norm": tot[1], "kv_a_norm": tot[2], "b_forget": tot[3][ROPE:ROPE + HEADS],
               "b_gate": jnp.concatenate([tot[4], tot[5]]), "post_mix_norm": tot[6], "pre_ffn_norm": tot[7],
               "conv_b": tot[8], "post_ffn_norm": tot[9]}
    gcw_full = jnp.stack([tot[10], tot[11], tot[12]])
    g_conv_w = lax.dynamic_slice(gcw_full, (0, chip * n_up), (3, n_up))

    given = dict(pre_mix_norm=(pre_mix_norm, m_pre_mix_norm, v_pre_mix_norm), w_in=(w_in, m_w_in, v_w_in),
                 q_a_norm=(q_a_norm, m_q_a_norm, v_q_a_norm), w_uq=(w_uq, m_w_uq, v_w_uq),
                 kv_a_norm=(kv_a_norm, m_kv_a_norm, v_kv_a_norm), w_ukv=(w_ukv, m_w_ukv, v_w_ukv),
                 b_forget=(b_forget, m_b_forget, v_b_forget), b_gate=(b_gate, m_b_gate, v_b_gate),
                 w_branch_mla=(w_branch_mla, m_w_branch_mla, v_w_branch_mla),
                 w_branch_fox=(w_branch_fox, m_w_branch_fox, v_w_branch_fox), w_out=(w_out, m_w_out, v_w_out),
                 post_mix_norm=(post_mix_norm, m_post_mix_norm, v_post_mix_norm),
                 pre_ffn_norm=(pre_ffn_norm, m_pre_ffn_norm, v_pre_ffn_norm), w_up=(w_up, m_w_up, v_w_up),
                 conv_w=(conv_w, m_conv_w, v_conv_w), conv_b=(conv_b, m_conv_b, v_conv_b),
                 w_down=(w_down, m_w_down, v_w_down), post_ffn_norm=(post_ffn_norm, m_post_ffn_norm, v_post_ffn_norm))
    order = list(given)
    grad, delta, new_m, new_v = {}, {}, {}, {}
    for nm in big:
        mine, theirs = g_halves[nm]
        if nm == "w_in":
            full = jnp.concatenate([jnp.where(cc == 0, mine, theirs), jnp.where(cc == 0, theirs, mine)], axis=0)
            grad[nm] = full[:, :n_in_shard]
            tr_out = _adamw("adamw_" + nm, *[jnp.transpose(a) for a in (given[nm][0], grad[nm], given[nm][1], given[nm][2])])
            delta[nm], new_m[nm], new_v[nm] = [jnp.transpose(a) for a in tr_out]
            continue
        grad[nm], delta[nm], new_m[nm], new_v[nm] = _adamw_halves("adamw_" + nm, given[nm][0], mine, theirs, given[nm][1],
                                                                  given[nm][2], core)
    grad["conv_w"] = g_conv_w
    delta["conv_w"], new_m["conv_w"], new_v["conv_w"] = _adamw("adamw_conv_w", conv_w, g_conv_w, m_conv_w, v_conv_w)
    small = list(g_small)
    padded = [-(-g_small[nm].shape[0] // LANES) * LANES for nm in small]
    s_rows = -(-sum(padded) // (8 * LANES)) * 8

    def pack(vals):
        cat = jnp.concatenate([jnp.pad(a, (0, p - a.shape[0])) for a, p in zip(vals, padded)])
        return jnp.pad(cat, (0, s_rows * LANES - cat.shape[0])).reshape(s_rows, LANES)

    packed = _adamw("adamw_small", pack([given[nm][0] for nm in small]), pack([g_small[nm] for nm in small]),
                    pack([given[nm][1] for nm in small]), pack([given[nm][2] for nm in small]))
    s_offs = [sum(padded[:i]) for i in range(len(small))]
    for nm, o in zip(small, s_offs):
        n_el = g_small[nm].shape[0]
        grad[nm] = g_small[nm]
        delta[nm], new_m[nm], new_v[nm] = [p.reshape(-1)[o:o + n_el] for p in packed]
    return (loss, grad_x.reshape(n_seq, seq, d), *[grad[nm] for nm in order], *[delta[nm] for nm in order],
            *[new_m[nm] for nm in order], *[new_v[nm] for nm in order])
```

```python
import functools
import math

import jax
import jax.numpy as jnp
from jax import lax
from jax.experimental import pallas as pl
from jax.experimental.pallas import tpu as pltpu

F32, BF16 = jnp.float32, jnp.bfloat16
MESH = pl.DeviceIdType.MESH

HEADS = 8
NOPE, ROPE, VDIM = 128, 64, 128
QL, KVL = 512, 256
FDIM = 128
CHUNK = 64
ROPE_THETA = 10000.0
EPS = 1e-6
NEG_INF = -1e30
ADAM_LR, ADAM_B1, ADAM_B2, ADAM_EPS, ADAM_WD, ADAM_STEP = 0.001, 0.9, 0.999, 1e-08, 0.01, 10

VMEM_LIMIT_BYTES = 52 * 1024 * 1024
LANES = 128
N_CHIPS = 4


def _params(sem):
    return pltpu.CompilerParams(dimension_semantics=sem, vmem_limit_bytes=VMEM_LIMIT_BYTES)


def _tile(n, target, mult):
    if n <= target:
        return n
    t = (target // mult) * mult
    while t >= mult:
        if n % t == 0:
            return t
        t -= mult
    raise ValueError(f"no tile for {n} (target {target}, multiple of {mult})")


class _Plain:
    def __init__(self, perm=None):
        self.perm = perm

    def spec(self, tr, tc, rc):
        perm = self.perm

        def imap(i, j, k):
            r, c = rc(i, j, k)
            return (r, perm(c) if perm is not None else c)

        return pl.BlockSpec((tr, tc), imap)

    def shape(self, rows, cols):
        return (rows, cols)


class _Chunked:
    def __init__(self, n):
        self.n = n

    def spec(self, tr, tc, rc):
        assert self.n % tc == 0, (self.n, tc)
        per = self.n // tc

        def imap(i, j, k):
            r, c = rc(i, j, k)
            return (c // per, r, c % per)

        return pl.BlockSpec((None, tr, tc), imap)

    def shape(self, rows, cols):
        assert cols == N_CHIPS * self.n
        return (N_CHIPS, rows, self.n)


_DIMS = {"nn": (((1,), (0,)), ((), ())), "nt": (((1,), (1,)), ((), ())), "tn": (((0,), (0,)), ((), ()))}


def _mm_single(name, mode, a, b, m, n, k, tm, tn, la, lb, lo, out_dtype, side):
    if mode == "nn":
        a_spec = la.spec(tm, k, lambda i, j, kk: (i, 0))
        b_spec = lb.spec(k, tn, lambda i, j, kk: (0, j))
    elif mode == "nt":
        a_spec = la.spec(tm, k, lambda i, j, kk: (i, 0))
        b_spec = lb.spec(tn, k, lambda i, j, kk: (j, 0))
    else:
        a_spec = la.spec(k, tm, lambda i, j, kk: (0, i))
        b_spec = lb.spec(k, tn, lambda i, j, kk: (0, j))
    o_spec = lo.spec(tm, tn, lambda i, j, kk: (i, j))
    dims = _DIMS[mode]

    def body(a_ref, b_ref, o_ref):
        o_ref[...] = lax.dot_general(a_ref[...].astype(BF16), b_ref[...].astype(BF16), dims,
                                     preferred_element_type=F32).astype(o_ref.dtype)

    (out,), got = _hosted_call(body, name, (m // tm, n // tn, 1), [a_spec, b_spec], [o_spec],
                               [jax.ShapeDtypeStruct(lo.shape(m, n), out_dtype)], (a, b), side,
                               semantics=("parallel", "parallel", "arbitrary"))
    return out if side is None else (out, got)


def _mm(name, mode, a, b, m, n, k, *, tm=1024, tn=1024, tk=2048, la=None, lb=None, lo=None, out_dtype=F32, side=None):
    la, lb, lo = la or _Plain(), lb or _Plain(), lo or _Plain()
    tm, tn, tk = _tile(m, tm, 128), _tile(n, tn, 128), _tile(k, tk, 128)
    nk = k // tk
    if nk == 1:
        return _mm_single(name, mode, a, b, m, n, k, tm, tn, la, lb, lo, out_dtype, side)
    if mode == "nn":
        a_spec = la.spec(tm, tk, lambda i, j, kk: (i, kk))
        b_spec = lb.spec(tk, tn, lambda i, j, kk: (kk, j))
    elif mode == "nt":
        a_spec = la.spec(tm, tk, lambda i, j, kk: (i, kk))
        b_spec = lb.spec(tn, tk, lambda i, j, kk: (j, kk))
    else:
        a_spec = la.spec(tk, tm, lambda i, j, kk: (kk, i))
        b_spec = lb.spec(tk, tn, lambda i, j, kk: (kk, j))
    o_spec = lo.spec(tm, tn, lambda i, j, kk: (i, j))
    dims = _DIMS[mode]

    def body(a_ref, b_ref, o_ref, acc_ref):
        kk = pl.program_id(2)

        @pl.when(kk == 0)
        def _():
            acc_ref[...] = jnp.zeros_like(acc_ref)

        acc_ref[...] += lax.dot_general(a_ref[...].astype(BF16), b_ref[...].astype(BF16), dims,
                                        preferred_element_type=F32)

        @pl.when(kk == nk - 1)
        def _():
            o_ref[...] = acc_ref[...].astype(o_ref.dtype)

    (out,), got = _hosted_call(body, name, (m // tm, n // tn, nk), [a_spec, b_spec], [o_spec],
                               [jax.ShapeDtypeStruct(lo.shape(m, n), out_dtype)], (a, b), side,
                               semantics=("parallel", "parallel", "arbitrary"), scratch=[pltpu.VMEM((tm, tn), F32)])
    return out if side is None else (out, got)


def _mm_parts(name, mode, a, b, m, n, k, *, part=1024, tm=1024, tn=1024, tk=2048, out_dtype=F32, side=None):
    parts = b if mode == "tn" else a
    widths = [p.shape[1] for p in parts]
    assert all(w % part == 0 for w in widths) and sum(widths) == (n if mode == "tn" else k)
    offs = [sum(widths[:i]) // part for i in range(len(widths))]
    nblk = [w // part for w in widths]
    if mode == "tn":
        tn, tk = part, _tile(k, tk, 128)
    else:
        tk, tn = part, _tile(n, tn, 128)
    tm = _tile(m, tm, 128)
    nk = k // tk
    grid = (m // tm, n // tn, nk)
    np_ = len(parts)

    def inside(idx, p):
        return jnp.logical_and(idx >= offs[p], idx < offs[p] + nblk[p])

    def part_spec(p):
        if mode == "tn":
            def imap(i, j, kk):
                on = inside(j, p)
                return (jnp.where(on, kk, 0), jnp.clip(j - offs[p], 0, nblk[p] - 1))
            return pl.BlockSpec((tk, tn), imap)

        def imap(i, j, kk):
            return (i, jnp.clip(kk - offs[p], 0, nblk[p] - 1))
        return pl.BlockSpec((tm, tk), imap)

    if mode == "tn":
        in_specs = [pl.BlockSpec((tk, tm), lambda i, j, kk: (kk, i))] + [part_spec(p) for p in range(np_)]
        args = [a] + list(parts)
    else:
        in_specs = [part_spec(p) for p in range(np_)] + [pl.BlockSpec((tn, tk), lambda i, j, kk: (j, kk))]
        args = list(parts) + [b]
    dims = _DIMS[mode]

    def body(*refs):
        o_ref, acc_ref = refs[-2], refs[-1]
        j, kk = pl.program_id(1), pl.program_id(2)

        @pl.when(kk == 0)
        def _():
            acc_ref[...] = jnp.zeros_like(acc_ref)

        for p in range(np_):
            @pl.when(inside(j if mode == "tn" else kk, p))
            def _(p=p):
                lhs, rhs = (refs[0], refs[1 + p]) if mode == "tn" else (refs[p], refs[np_])
                acc_ref[...] += lax.dot_general(lhs[...].astype(BF16), rhs[...].astype(BF16), dims, preferred_element_type=F32)

        @pl.when(kk == nk - 1)
        def _():
            o_ref[...] = acc_ref[...].astype(o_ref.dtype)

    (out,), got = _hosted_call(body, name, grid, in_specs, [pl.BlockSpec((tm, tn), lambda i, j, kk: (i, j))],
                               [jax.ShapeDtypeStruct((m, n), out_dtype)], args, side,
                               semantics=("parallel", "parallel", "arbitrary"), scratch=[pltpu.VMEM((tm, tn), F32)])
    return out if side is None else (out, got)


def _rows(name, fn, rows_in, vecs_in, rows_out, accs_out, n_rows, tr=256):
    tr = _tile(n_rows, tr, 16)
    nr, nv, no = len(rows_in), len(vecs_in), len(rows_out)

    def body(*refs):
        ins, vecs = refs[:nr], refs[nr:nr + nv]
        outs, accs = refs[nr + nv:nr + nv + no], refs[nr + nv + no:]
        ro, ac = fn([r[...] for r in ins], [v[...] for v in vecs])
        for o_ref, val in zip(outs, ro):
            o_ref[...] = val.astype(o_ref.dtype)
        if accs:
            @pl.when(pl.program_id(0) == 0)
            def _():
                for a_ref in accs:
                    a_ref[...] = jnp.zeros_like(a_ref)

            for a_ref, val in zip(accs, ac):
                a_ref[...] += val

    in_specs = [pl.BlockSpec((tr, cols), functools.partial(lambda i, cb: (i, cb), cb=cb)) for _, cols, cb in rows_in]
    in_specs += [pl.BlockSpec(v.shape, lambda i: (0, 0)) for v in vecs_in]
    out_specs = [pl.BlockSpec((tr, cols), lambda i: (i, 0)) for cols, _ in rows_out]
    out_specs += [pl.BlockSpec((r, cols), lambda i: (0, 0)) for r, cols in accs_out]
    out_shape = [jax.ShapeDtypeStruct((n_rows, cols), dt) for cols, dt in rows_out]
    out_shape += [jax.ShapeDtypeStruct((r, cols), F32) for r, cols in accs_out]
    res = pl.pallas_call(
        body, name=name, grid=(n_rows // tr,), in_specs=in_specs, out_specs=out_specs, out_shape=out_shape,
        compiler_params=_params(("arbitrary",)),
    )(*[a for a, _, _ in rows_in], *vecs_in)
    return res


def _colsum(v):
    return jnp.sum(v, axis=0, keepdims=True)


def _rstd(x):
    return lax.rsqrt(jnp.mean(x * x, axis=-1, keepdims=True) + EPS)


def _rms_bwd(x, g, dy):
    r = _rstd(x)
    xh = x * r
    dxh = dy * g
    dx = r * (dxh - xh * jnp.mean(dxh * xh, axis=-1, keepdims=True))
    return dx, _colsum(dy * xh)


def _sigmoid(z):
    return 1.0 / (1.0 + jnp.exp(-z))


_GELU_K = math.sqrt(2.0 / math.pi)


def _gelu_parts(g):
    t = jnp.tanh(_GELU_K * (g + 0.044715 * g * g * g))
    gel = 0.5 * g * (1.0 + t)
    dgel = 0.5 * (1.0 + t) + 0.5 * g * (1.0 - t * t) * (_GELU_K * (1.0 + 3.0 * 0.044715 * g * g))
    return gel, dgel


def _diag_visible(t, unit):
    rows = lax.broadcasted_iota(jnp.int32, (t, t), 0)
    cols = lax.broadcasted_iota(jnp.int32, (t, t), 1)
    if unit > 1:
        sh = int(math.log2(unit))
        assert 1 << sh == unit and t % unit == 0
        rows, cols = jnp.right_shift(rows, sh), jnp.right_shift(cols, sh)
    return cols <= rows


def _lane_pick(tile, lane):
    idx = lax.broadcasted_iota(jnp.int32, tile.shape, 1)
    return jnp.sum(jnp.where(idx == lane, tile, 0.0), axis=1, keepdims=True)


def _lane_put(tile, lane, col):
    idx = lax.broadcasted_iota(jnp.int32, tile.shape, 1)
    return jnp.where(idx == lane, col, tile)


def _head_cat(refs, shared, rows, h):
    hs = slice(h * LANES, (h + 1) * LANES)
    vals = [(r[rows, :] if sh else r[rows, hs]).astype(BF16) for r, sh in zip(refs, shared)]
    return vals[0] if len(vals) == 1 else jnp.concatenate(vals, axis=1)


def _blk_rows(i, t):
    return pl.ds(pl.multiple_of(i * t, t), t)


def _piece_specs(pieces, rows, row_idx):
    return [pl.BlockSpec((rows, LANES if sh else HEADS * LANES), functools.partial(lambda b, i, cb: (row_idx(b, i), cb), cb=cb))
            for _, cb, sh in pieces]


def _attn_fwd(name, qp, kp, vp, bias, unit, scale, n_seq, seq, t, side=None):
    nb = seq // t
    n_tok = n_seq * seq
    nq, nk_p = len(qp), len(kp)
    q_sh, k_sh = [p[2] for p in qp], [p[2] for p in kp]
    nbias = 2 if bias is not None else 0

    def body(*refs):
        q_refs, k_refs = refs[:nq], refs[nq:nq + nk_p]
        v_ref = refs[nq + nk_p]
        bias_refs = refs[nq + nk_p + 1:nq + nk_p + 1 + nbias]
        o_ref, lse_ref = refs[nq + nk_p + 1 + nbias:]
        qi = pl.program_id(1)
        lse_tile = jnp.zeros((t, LANES), F32)
        for h in range(HEADS):
            hs = slice(h * LANES, (h + 1) * LANES)
            q = _head_cat(q_refs, q_sh, slice(None), h)
            cq = _lane_pick(bias_refs[0][...], ROPE + h) if bias is not None else None

            def block(kb, carry, diag, h=h, hs=hs, q=q, cq=cq):
                m, l, acc = carry
                rows = _blk_rows(kb, t)
                s = lax.dot_general(q, _head_cat(k_refs, k_sh, rows, h), _DIMS["nt"], preferred_element_type=F32) * scale
                if bias is not None:
                    s = s + cq - bias_refs[1][kb, h:h + 1, :]
                if diag:
                    s = jnp.where(_diag_visible(t, unit), s, NEG_INF)
                m_new = jnp.maximum(m, jnp.max(s, axis=1, keepdims=True))
                alpha = jnp.exp(m - m_new)
                p = jnp.exp(s - m_new)
                l = alpha * l + jnp.sum(p, axis=1, keepdims=True)
                acc = alpha * acc + jnp.dot(p.astype(BF16), v_ref[rows, hs].astype(BF16), preferred_element_type=F32)
                return m_new, l, acc

            init = (jnp.full((t, 1), NEG_INF, F32), jnp.zeros((t, 1), F32), jnp.zeros((t, LANES), F32))
            carry = lax.fori_loop(0, qi, lambda kb, c: block(kb, c, False), init)
            m, l, acc = block(qi, carry, True)
            o_ref[:, hs] = acc / l
            lse_tile = _lane_put(lse_tile, h, m + jnp.log(l))
        lse_ref[...] = lse_tile

    tile_row = lambda b, i: b * nb + i
    seq_row = lambda b, i: b
    lane_tile = pl.BlockSpec((t, LANES), lambda b, i: (b * nb + i, 0))
    in_specs = _piece_specs(qp, t, tile_row) + _piece_specs(kp, seq, seq_row) + _piece_specs([vp + (False,)], seq, seq_row)
    args = [p[0] for p in qp] + [p[0] for p in kp] + [vp[0]]
    if bias is not None:
        in_specs += [lane_tile, pl.BlockSpec((None, nb, HEADS, t), lambda b, i: (b, 0, 0, 0))]
        args += list(bias)
    return _hosted_call(
        body, name, (n_seq, nb), in_specs,
        [pl.BlockSpec((t, HEADS * LANES), lambda b, i: (b * nb + i, 0)), lane_tile],
        [jax.ShapeDtypeStruct((n_tok, HEADS * LANES), F32), jax.ShapeDtypeStruct((n_tok, LANES), F32)], args, side)


HEAD_GROUPS = 2


def _attn_delta(name, o, do, n_tok):
    def fn(r, v):
        o_v, do_v = r
        tile = jnp.zeros((o_v.shape[0], LANES), F32)
        for h in range(HEADS):
            hs = slice(h * LANES, (h + 1) * LANES)
            tile = _lane_put(tile, h, jnp.sum(do_v[:, hs] * o_v[:, hs], axis=1, keepdims=True))
        return [tile, do_v], []

    return _rows(name, fn, [(o, HEADS * LANES, 0), (do, HEADS * LANES, 0)], [], [(LANES, F32), (HEADS * LANES, BF16)], [], n_tok)


def _attn_bwd(name, qp, kp, vp, dob, lse, delta, bias, unit, scale, n_seq, seq, t, grad_dtype, side=None):
    nb = seq // t
    n_tok = n_seq * seq
    ng = HEAD_GROUPS
    hg = HEADS // ng
    gw = hg * LANES
    nq, nk_p = len(qp), len(kp)
    q_sh, k_sh = [p[2] for p in qp], [p[2] for p in kp]
    assert not any(q_sh) and nq == nk_p
    nbias = 2 if bias is not None else 0
    n_in = nq + nk_p + 4 + nbias
    n_out = nq + nk_p + 1 + nbias

    def body(*refs):
        q_refs, k_refs = refs[:nq], refs[nq:nq + nk_p]
        v_ref, dob_ref, lse_ref, delta_ref = refs[nq + nk_p:nq + nk_p + 4]
        bias_refs = refs[nq + nk_p + 4:n_in]
        dq_refs, dk_refs = refs[n_in:n_in + nq], refs[n_in + nq:n_in + nq + nk_p]
        dv_ref = refs[n_in + nq + nk_p]
        dq_s, dcq_s = refs[n_in + n_out:]
        g, ki = pl.program_id(1), pl.program_id(2)

        @pl.when(ki == 0)
        def _():
            dq_s[...] = jnp.zeros_like(dq_s)
            dcq_s[...] = jnp.zeros_like(dcq_s)

        shared_acc = [jnp.zeros((t, LANES), F32) for _ in range(nk_p)]
        for hl in range(hg):
            h = g * hg + hl
            hs = slice(hl * LANES, (hl + 1) * LANES)
            k = _head_cat(k_refs, k_sh, slice(None), hl)
            v = v_ref[:, hs].astype(BF16)
            ck = bias_refs[1][pl.ds(h, 1), :] if bias is not None else None

            def block(qb, carry, diag, h=h, hl=hl, hs=hs, k=k, v=v, ck=ck):
                dk_acc, dv_acc, dc_acc = carry
                rows = _blk_rows(qb, t)
                q = _head_cat(q_refs, q_sh, rows, hl)
                s = lax.dot_general(q, k, _DIMS["nt"], preferred_element_type=F32) * scale
                if bias is not None:
                    s = s + _lane_pick(bias_refs[0][rows, :], ROPE + h) - ck
                if diag:
                    s = jnp.where(_diag_visible(t, unit), s, NEG_INF)
                p = jnp.exp(s - _lane_pick(lse_ref[rows, :], h))
                do_b = dob_ref[rows, hs]
                dp = lax.dot_general(do_b, v, _DIMS["nt"], preferred_element_type=F32)
                ds = p * (dp - _lane_pick(delta_ref[rows, :], h))
                ds_b = ds.astype(BF16)
                dq_blk = jnp.dot(ds_b, k, preferred_element_type=F32)
                for n_p in range(nq):
                    dq_s[rows, n_p * gw + hl * LANES:n_p * gw + (hl + 1) * LANES] += dq_blk[:, n_p * LANES:(n_p + 1) * LANES]
                if bias is not None:
                    lane = lax.broadcasted_iota(jnp.int32, (t, LANES), 1)
                    dcq_s[rows, :] += jnp.where(lane == ROPE + h, jnp.sum(ds, axis=1, keepdims=True), 0.0)
                return (dk_acc + lax.dot_general(ds_b, q, _DIMS["tn"], preferred_element_type=F32),
                        dv_acc + lax.dot_general(p.astype(BF16), do_b, _DIMS["tn"], preferred_element_type=F32),
                        dc_acc - jnp.sum(ds, axis=0, keepdims=True))

            init = (jnp.zeros((t, nk_p * LANES), F32), jnp.zeros((t, LANES), F32), jnp.zeros((1, t), F32))
            carry = block(ki, init, True)
            dk_acc, dv_acc, dc_acc = lax.fori_loop(ki + 1, nb, lambda qb, c: block(qb, c, False), carry)
            for n_p in range(nk_p):
                part = dk_acc[:, n_p * LANES:(n_p + 1) * LANES] * scale
                if k_sh[n_p]:
                    shared_acc[n_p] = shared_acc[n_p] + part
                else:
                    dk_refs[n_p][:, hs] = part.astype(grad_dtype)
            dv_ref[:, hs] = dv_acc.astype(grad_dtype)
            if bias is not None:
                refs[n_in + n_out - 1][hl:hl + 1, :] = dc_acc
        for n_p in range(nk_p):
            if k_sh[n_p]:
                dk_refs[n_p][...] = shared_acc[n_p]

        @pl.when(ki == nb - 1)
        def _():
            for n_p in range(nq):
                dq_refs[n_p][...] = (dq_s[:, n_p * gw:(n_p + 1) * gw] * scale).astype(grad_dtype)
            if bias is not None:
                refs[n_in + n_out - 2][...] = dcq_s[...]

    def spec(rows, row_idx, cb, shared):
        if shared:
            return pl.BlockSpec((rows, LANES), lambda b, g, i: (row_idx(b, i), cb))
        return pl.BlockSpec((rows, gw), lambda b, g, i: (row_idx(b, i), cb * ng + g))

    tile_row = lambda b, i: b * nb + i
    seq_row = lambda b, i: b
    lane_seq = pl.BlockSpec((seq, LANES), lambda b, g, i: (b, 0))
    in_specs = [spec(seq, seq_row, cb, sh) for _, cb, sh in qp] + [spec(t, tile_row, cb, sh) for _, cb, sh in kp]
    in_specs += [spec(t, tile_row, vp[1], False), spec(seq, seq_row, 0, False), lane_seq, lane_seq]
    args = [p[0] for p in qp] + [p[0] for p in kp] + [vp[0], dob, lse, delta]
    if bias is not None:
        in_specs += [lane_seq, pl.BlockSpec((None, None, HEADS, t), lambda b, g, i: (b, i, 0, 0))]
        args += list(bias)
    group_tile = pl.BlockSpec((None, t, LANES), lambda b, g, i: (g, b * nb + i, 0))
    out_specs = [spec(seq, seq_row, 0, False)] * nq
    out_specs += [group_tile if sh else spec(t, tile_row, 0, False) for sh in k_sh] + [spec(t, tile_row, 0, False)]
    head_shape = jax.ShapeDtypeStruct((n_tok, HEADS * LANES), grad_dtype)
    out_shape = [head_shape] * nq + [jax.ShapeDtypeStruct((ng, n_tok, LANES), F32) if sh else head_shape for sh in k_sh]
    out_shape.append(head_shape)
    if bias is not None:
        out_specs += [pl.BlockSpec((None, seq, LANES), lambda b, g, i: (g, b, 0)),
                      pl.BlockSpec((None, None, None, hg, t), lambda b, g, i: (b, g, i, 0, 0))]
        out_shape += [jax.ShapeDtypeStruct((ng, n_tok, LANES), F32), jax.ShapeDtypeStruct((n_seq, ng, nb, hg, t), F32)]
    return _hosted_call(body, name, (n_seq, ng, nb), in_specs, out_specs, out_shape, args, side,
                        semantics=("parallel", "arbitrary", "arbitrary"),
                        scratch=[pltpu.VMEM((seq, nq * gw), F32), pltpu.VMEM((seq, LANES), F32)])


def _seq_cumsum(name, x, col_block, n_seq, seq, reverse, pre=None, vec=None):
    t = _tile(seq, 256, 128)
    nb = seq // t

    def body(*refs):
        x_ref = refs[0]
        vec_ref = refs[1] if vec is not None else None
        o_ref, carry = refs[-2], refs[-1]

        @pl.when(pl.program_id(1) == 0)
        def _():
            carry[...] = jnp.zeros_like(carry)

        v = x_ref[...]
        if pre is not None:
            v = pre(v, vec_ref[...])
        r = lax.broadcasted_iota(jnp.int32, (t, t), 0)
        c = lax.broadcasted_iota(jnp.int32, (t, t), 1)
        tri = jnp.where((c >= r) if reverse else (c <= r), 1.0, 0.0).astype(BF16)
        hi = v.astype(BF16)
        mid = (v - hi.astype(F32)).astype(BF16)
        lo = (v - hi.astype(F32) - mid.astype(F32)).astype(BF16)
        acc = jnp.dot(tri, hi, preferred_element_type=F32)
        acc += jnp.dot(tri, mid, preferred_element_type=F32)
        acc += jnp.dot(tri, lo, preferred_element_type=F32)
        o_ref[...] = acc + carry[...]
        carry[...] += _colsum(v)

    blk = (lambda b, i: (b * nb + nb - 1 - i)) if reverse else (lambda b, i: (b * nb + i))
    in_specs = [pl.BlockSpec((t, LANES), lambda b, i: (blk(b, i), col_block))]
    args = [x]
    if vec is not None:
        in_specs.append(pl.BlockSpec(vec.shape, lambda b, i: (0, 0)))
        args.append(vec)
    return pl.pallas_call(
        body, name=name, grid=(n_seq, nb), in_specs=in_specs,
        out_specs=pl.BlockSpec((t, LANES), lambda b, i: (blk(b, i), 0)),
        out_shape=jax.ShapeDtypeStruct((n_seq * seq, LANES), F32),
        scratch_shapes=[pltpu.VMEM((1, LANES), F32)],
        compiler_params=_params(("arbitrary", "arbitrary")),
    )(*args)


def _log_sigmoid(z):
    return -(jnp.maximum(-z, 0.0) + jnp.log(1.0 + jnp.exp(-jnp.abs(z))))


def _shift_down(u, prev_ref, n):
    out = pltpu.roll(u, n, 0)
    row = lax.broadcasted_iota(jnp.int32, u.shape, 0)
    for r in range(n):
        out = jnp.where(row == r, prev_ref[8 - n + r:8 - n + r + 1, :], out)
    return out


def _shift_up(u, next_ref, n):
    ts = u.shape[0]
    out = pltpu.roll(u, ts - n, 0)
    row = lax.broadcasted_iota(jnp.int32, u.shape, 0)
    for r in range(n):
        out = jnp.where(row == ts - n + r, next_ref[r:r + 1, :], out)
    return out


def _conv_taps(u, prev_ref, w_ref, b_ref):
    s1, s2 = _shift_down(u, prev_ref, 1), _shift_down(u, prev_ref, 2)
    return (w_ref[0:1, :] * s2 + w_ref[1:2, :] * s1 + w_ref[2:3, :] * u) + b_ref[...], s1, s2


def _conv_glu_fwd(u_il, cw_il, cb_il, n_seq, seq, wt):
    n_tok, two_f = u_il.shape
    nct = two_f // (2 * wt)
    ts = _tile(seq, 256, 8)
    ns = seq // ts

    def body(u_ref, w_ref, b_ref, a_ref, carry):
        @pl.when(pl.program_id(2) == 0)
        def _():
            carry[...] = jnp.zeros_like(carry)

        u = u_ref[...].astype(F32)
        uc, _, _ = _conv_taps(u, carry, w_ref, b_ref)
        gel, _ = _gelu_parts(uc[:, :wt])
        a_ref[...] = (gel * uc[:, wt:]).astype(a_ref.dtype)
        carry[...] = u[ts - 8:, :]

    return pl.pallas_call(
        body, name="conv_glu_fwd", grid=(nct, n_seq, ns),
        in_specs=[pl.BlockSpec((ts, 2 * wt), lambda j, b, s: (b * ns + s, j)),
                  pl.BlockSpec((3, 2 * wt), lambda j, b, s: (0, j)),
                  pl.BlockSpec((1, 2 * wt), lambda j, b, s: (0, j))],
        out_specs=pl.BlockSpec((ts, wt), lambda j, b, s: (b * ns + s, j)),
        out_shape=jax.ShapeDtypeStruct((n_tok, two_f // 2), BF16),
        scratch_shapes=[pltpu.VMEM((8, 2 * wt), F32)],
        compiler_params=_params(("parallel", "arbitrary", "arbitrary")),
    )(u_il, cw_il, cb_il)


def _conv_glu_bwd_pre(u_il, da, cw_il, cb_il, n_seq, seq, wt):
    n_tok, two_f = u_il.shape
    nct = two_f // (2 * wt)
    ts = _tile(seq, 256, 8)
    ns = seq // ts

    def body(u_ref, da_ref, w_ref, b_ref, d_ref, acc_ref, carry):
        first = jnp.logical_and(pl.program_id(1) == 0, pl.program_id(2) == 0)

        @pl.when(first)
        def _():
            acc_ref[...] = jnp.zeros_like(acc_ref)

        @pl.when(pl.program_id(2) == 0)
        def _():
            carry[...] = jnp.zeros_like(carry)

        u = u_ref[...].astype(F32)
        uc, s1, s2 = _conv_taps(u, carry, w_ref, b_ref)
        gel, dgel = _gelu_parts(uc[:, :wt])
        da_v = da_ref[...].astype(F32)
        d = jnp.concatenate([da_v * uc[:, wt:] * dgel, da_v * gel], axis=1)
        d_ref[...] = d.astype(d_ref.dtype)
        acc_ref[0:1, :] += _colsum(d * s2)
        acc_ref[1:2, :] += _colsum(d * s1)
        acc_ref[2:3, :] += _colsum(d * u)
        acc_ref[3:4, :] += _colsum(d)
        carry[...] = u[ts - 8:, :]

    return pl.pallas_call(
        body, name="conv_glu_bwd_pre", grid=(nct, n_seq, ns),
        in_specs=[pl.BlockSpec((ts, 2 * wt), lambda j, b, s: (b * ns + s, j)),
                  pl.BlockSpec((ts, wt), lambda j, b, s: (b * ns + s, j)),
                  pl.BlockSpec((3, 2 * wt), lambda j, b, s: (0, j)),
                  pl.BlockSpec((1, 2 * wt), lambda j, b, s: (0, j))],
        out_specs=[pl.BlockSpec((ts, 2 * wt), lambda j, b, s: (b * ns + s, j)),
                   pl.BlockSpec((8, 2 * wt), lambda j, b, s: (0, j))],
        out_shape=[jax.ShapeDtypeStruct((n_tok, two_f), BF16), jax.ShapeDtypeStruct((8, two_f), F32)],
        scratch_shapes=[pltpu.VMEM((8, 2 * wt), F32)],
        compiler_params=_params(("parallel", "arbitrary", "arbitrary")),
    )(u_il, da, cw_il, cb_il)


def _conv_bwd_input(d_il, cw_il, n_seq, seq, wt):
    n_tok, two_f = d_il.shape
    nct = two_f // (2 * wt)
    ts = _tile(seq, 256, 8)
    ns = seq // ts

    def body(d_ref, w_ref, o_ref, carry):
        @pl.when(pl.program_id(2) == 0)
        def _():
            carry[...] = jnp.zeros_like(carry)

        d = d_ref[...].astype(F32)
        o_ref[...] = (w_ref[2:3, :] * d + w_ref[1:2, :] * _shift_up(d, carry, 1)
                      + w_ref[0:1, :] * _shift_up(d, carry, 2)).astype(o_ref.dtype)
        carry[...] = d[:8, :]

    rev = lambda j, b, s: (b * ns + ns - 1 - s, j)
    return pl.pallas_call(
        body, name="conv_bwd_input", grid=(nct, n_seq, ns),
        in_specs=[pl.BlockSpec((ts, 2 * wt), rev), pl.BlockSpec((3, 2 * wt), lambda j, b, s: (0, j))],
        out_specs=pl.BlockSpec((ts, 2 * wt), rev),
        out_shape=jax.ShapeDtypeStruct((n_tok, two_f), BF16),
        scratch_shapes=[pltpu.VMEM((8, 2 * wt), F32)],
        compiler_params=_params(("parallel", "arbitrary", "arbitrary")),
    )(d_il, cw_il)


HBM = pl.BlockSpec(memory_space=pltpu.HBM)
_CHIP_FLIPS = ((1, 0), (0, 1), (1, 1))


def _place():
    x, y, c = lax.axis_index("x"), lax.axis_index("y"), lax.axis_index("c")
    return x, y, c, 2 * x + y


def _flip(v, f):
    return 1 - v if f else v


def _half_rows(c, half):
    return pl.ds(pl.multiple_of(c * half, 16), half)


def _remote(src, dst, ssem, rsem, dev):
    return pltpu.make_async_remote_copy(src_ref=src, dst_ref=dst, send_sem=ssem, recv_sem=rsem,
                                        device_id=dev, device_id_type=MESH)


def _comm_call(name, body, ins, out_shapes, n_sems):
    return pl.pallas_call(
        body, name=name, in_specs=[HBM] * len(ins), out_specs=[HBM] * len(out_shapes),
        out_shape=[pltpu.HBM(s.shape, s.dtype) for s in out_shapes],
        scratch_shapes=[pltpu.SemaphoreType.DMA((n_sems,)), pltpu.SemaphoreType.DMA((n_sems,))],
    )(*ins)


def _all_gather_weights(shards, smalls):
    side = _gather_side(shards, smalls)
    nt = len(shards) + len(smalls)

    def body(*refs):
        for part in (side.start, side.mid, side.end):
            part(refs[:nt], refs[nt:2 * nt], *refs[2 * nt:])

    res = _comm_call("all_gather_weights", body, side.ins, side.outs, side.n_sems)
    return res[:len(shards)], res[len(shards):]


def _pair_split(name, grads):
    n = len(grads)

    def body(*refs):
        src, got = refs[:n], refs[n:2 * n]
        ssem, rsem = refs[2 * n:]
        x, y, c, _ = _place()
        cps = []
        for w in range(n):
            half = grads[w].shape[1] // 2
            cp = _remote(src[w].at[:, _half_rows(1 - c, half)], got[w], ssem.at[w], rsem.at[w], (x, y, 1 - c))
            cp.start()
            cps.append(cp)
        for cp in cps:
            cp.wait()

    outs = [jax.ShapeDtypeStruct((g.shape[0], g.shape[1] // 2, g.shape[2]), g.dtype) for g in grads]
    return _comm_call(name, body, grads, outs, n)


class _Side:
    def __init__(self, ins, outs, n_sems, start, mid, end, mid_step=None):
        self.ins, self.outs, self.n_sems = list(ins), list(outs), n_sems
        self.start, self.mid, self.end, self.mid_step = start, mid, end, mid_step
        self.aliases = {}


def _scatter_side(parts):
    n = len(parts)

    def copies(src, dst, ssem, rsem):
        x, y, c, _ = _place()
        out = []
        for w in range(n):
            for k, (fx, fy) in enumerate(_CHIP_FLIPS):
                px, py = _flip(x, fx), _flip(y, fy)
                out.append(_remote(src[w].at[2 * px + py], dst[w].at[k], ssem.at[w * 3 + k], rsem.at[w * 3 + k], (px, py, c)))
        return out

    def start(src, dst, ssem, rsem):
        for cp in copies(src, dst, ssem, rsem):
            cp.start()

    def end(src, dst, ssem, rsem):
        for cp in copies(src, dst, ssem, rsem):
            cp.wait()

    outs = [jax.ShapeDtypeStruct((3,) + p.shape[1:], p.dtype) for p in parts]
    return _Side(parts, outs, 3 * n, start, None, end)


def _gather_side(shards, smalls, mid_step=None, into=None):
    n, ns = len(shards), len(smalls)
    into = into or [(None, a.shape[0], 0) for a in shards]

    def dst_rows(w, c):
        half = shards[w].shape[0] // 2
        return pl.ds(pl.multiple_of(into[w][2] + c * half, 16), half)

    def ici(src, dst, ssem, rsem, w, k):
        x, y, c, me = _place()
        fx, fy = _CHIP_FLIPS[k]
        rows = _half_rows(c, shards[w].shape[0] // 2)
        return _remote(src[w].at[rows], dst[w].at[me, dst_rows(w, c)], ssem.at[w * 6 + k], rsem.at[w * 6 + k],
                       (_flip(x, fx), _flip(y, fy), c))

    def small(src, dst, ssem, rsem, s, k):
        x, y, c, me = _place()
        fx, fy = _CHIP_FLIPS[k]
        sem = 6 * n + 3 * s + k
        return _remote(src[n + s], dst[n + s].at[me], ssem.at[sem], rsem.at[sem], (_flip(x, fx), _flip(y, fy), c))

    def own(src, dst, ssem, rsem, i):
        x, y, c, me = _place()
        sem = 6 * n + 3 * ns + i
        if i < n:
            to = dst[i].at[me, pl.ds(into[i][2], shards[i].shape[0])]
        else:
            to = dst[i].at[me]
        return _remote(src[i], to, ssem.at[sem], rsem.at[sem], (x, y, 1 - c))

    def landed(dst, ssem, rsem, w, k, sender_c, sem_off):
        x, y, c, _ = _place()
        fx, fy = _CHIP_FLIPS[k]
        got = dst[w].at[2 * _flip(x, fx) + _flip(y, fy), dst_rows(w, sender_c)]
        return _remote(got, got, ssem.at[w * 6 + sem_off + k], rsem.at[w * 6 + sem_off + k], (x, y, 1 - c))

    def start(src, dst, ssem, rsem):
        for i in range(n + ns):
            own(src, dst, ssem, rsem, i).start()
        for s in range(ns):
            for k in range(3):
                small(src, dst, ssem, rsem, s, k).start()
        for w in range(n):
            for k in range(3):
                ici(src, dst, ssem, rsem, w, k).start()

    def mid(src, dst, ssem, rsem):
        c = lax.axis_index("c")
        for w in range(n):
            for k in range(3):
                landed(dst, ssem, rsem, w, k, c, 0).wait_recv()
                landed(dst, ssem, rsem, w, k, c, 3).start()

    def end(src, dst, ssem, rsem):
        c = lax.axis_index("c")
        for w in range(n):
            for k in range(3):
                landed(dst, ssem, rsem, w, k, 1 - c, 3).wait_recv()
        for i in range(n + ns):
            own(src, dst, ssem, rsem, i).wait()
        for s in range(ns):
            for k in range(3):
                small(src, dst, ssem, rsem, s, k).wait()
        for w in range(n):
            for k in range(3):
                ici(src, dst, ssem, rsem, w, k).wait_send()
                landed(dst, ssem, rsem, w, k, c, 3).wait_send()

    outs = [jax.ShapeDtypeStruct((N_CHIPS, rows, a.shape[1]), a.dtype) for a, (_, rows, _) in zip(shards, into)]
    outs += [jax.ShapeDtypeStruct((N_CHIPS,) + a.shape, a.dtype) for a in smalls]
    filled = [(w, arr) for w, (arr, _, _) in enumerate(into) if arr is not None]
    side = _Side(list(shards) + list(smalls) + [arr for _, arr in filled], outs, 7 * n + 4 * ns, start, mid, end, mid_step)
    side.aliases = {n + ns + i: w for i, (w, _) in enumerate(filled)}
    return side


def _host(body, n_in, n_out, side, grid):
    if side is None:
        return body
    ns_in, ns_out = len(side.ins), len(side.outs)
    n_steps = math.prod(grid)
    mid_step = side.mid_step
    if side.mid is not None and not isinstance(mid_step, int):
        mid_step = min(n_steps - 1, int(mid_step * n_steps))

    def wrapped(*refs):
        ins, s_ins = refs[:n_in], refs[n_in:n_in + ns_in]
        outs = refs[n_in + ns_in:n_in + ns_in + n_out]
        s_outs = refs[n_in + ns_in + n_out:n_in + ns_in + n_out + ns_out]
        rest = refs[n_in + ns_in + n_out + ns_out:]
        sems = rest[-2:]
        step = 0
        for axis, extent in enumerate(grid):
            step = step * extent + pl.program_id(axis)

        @pl.when(step == 0)
        def _():
            side.start(s_ins, s_outs, *sems)

        if side.mid is not None:
            @pl.when(step == mid_step)
            def _():
                side.mid(s_ins, s_outs, *sems)

        body(*ins, *outs, *rest[:-2])

        @pl.when(step == n_steps - 1)
        def _():
            side.end(s_ins, s_outs, *sems)

    return wrapped


def _hosted_call(body, name, grid, in_specs, out_specs, out_shape, args, side, semantics=("parallel", "arbitrary"),
                 scratch=()):
    n_in, n_out = len(in_specs), len(out_specs)
    kern = _host(body, n_in, n_out, side, grid)
    if side is None:
        return pl.pallas_call(kern, name=name, grid=grid, in_specs=in_specs, out_specs=out_specs, out_shape=out_shape,
                              scratch_shapes=list(scratch), compiler_params=_params(semantics))(*args), []
    res = pl.pallas_call(
        kern, name=name, grid=grid, in_specs=in_specs + [HBM] * len(side.ins), out_specs=out_specs + [HBM] * len(side.outs),
        out_shape=list(out_shape) + [pltpu.HBM(s.shape, s.dtype) for s in side.outs],
        scratch_shapes=list(scratch) + [pltpu.SemaphoreType.DMA((side.n_sems,)), pltpu.SemaphoreType.DMA((side.n_sems,))],
        input_output_aliases={n_in + i: n_out + o for i, o in side.aliases.items()},
        compiler_params=_params(("arbitrary",) * len(grid)),
    )(*args, *side.ins)
    return res[:n_out], res[n_out:]


def _swap_side(halves):
    n = len(halves)

    def copies(src, dst, ssem, rsem):
        x, y, c, _ = _place()
        return [_remote(src[w], dst[w], ssem.at[w], rsem.at[w], (x, y, 1 - c)) for w in range(n)]

    def start(src, dst, ssem, rsem):
        for cp in copies(src, dst, ssem, rsem):
            cp.start()

    def end(src, dst, ssem, rsem):
        for cp in copies(src, dst, ssem, rsem):
            cp.wait()

    return _Side(halves, [jax.ShapeDtypeStruct(h.shape, h.dtype) for h in halves], n, start, None, end)


def _pair_swap(halves):
    n = len(halves)

    def body(*refs):
        src, dst = refs[:n], refs[n:2 * n]
        ssem, rsem = refs[2 * n:]
        x, y, c, _ = _place()
        cps = []
        for w in range(n):
            cp = _remote(src[w], dst[w], ssem.at[w], rsem.at[w], (x, y, 1 - c))
            cp.start()
            cps.append(cp)
        for cp in cps:
            cp.wait()

    outs = [jax.ShapeDtypeStruct(h.shape, h.dtype) for h in halves]
    return _comm_call("rs_pair_swap", body, halves, outs, n)


def _gather_small(vec):
    def body(src, dst, ssem, rsem):
        x, y, c, _ = _place()
        me = 4 * x + 2 * y + c
        cps = []
        for r in range(1, 8):
            dev = (_flip(x, r & 4), _flip(y, r & 2), _flip(c, r & 1))
            cp = _remote(src, dst.at[me], ssem.at[r - 1], rsem.at[r - 1], dev)
            cp.start()
            cps.append(cp)
        for cp in cps:
            cp.wait()

    out = jax.ShapeDtypeStruct((8,) + vec.shape, vec.dtype)
    return _comm_call("gather_small", body, [vec], [out], 7)[0]


def _pair_add(name, g, theirs, core):
    n, half, b = theirs.shape
    tr = _tile(half, 256, 16)
    nt = half // tr

    def body(c_ref, g_ref, t_ref, o_ref):
        o_ref[...] = (g_ref[...].astype(F32) + t_ref[...].astype(F32)).astype(o_ref.dtype)

    same = pl.BlockSpec((None, tr, b), lambda j, i, c: (j, i, 0))
    grid_spec = pltpu.PrefetchScalarGridSpec(
        num_scalar_prefetch=1, grid=(n, nt),
        in_specs=[pl.BlockSpec((None, tr, b), lambda j, i, c: (j, c[0] * nt + i, 0)), same], out_specs=same)
    return pl.pallas_call(body, name=name, grid_spec=grid_spec, out_shape=jax.ShapeDtypeStruct(theirs.shape, BF16),
                          compiler_params=_params(("parallel", "parallel")))(core, g, theirs)


def _chip_sum(name, parts, landed, chip1):
    n, r, c = landed.shape
    tr = _tile(r, 256, 16)

    def body(ix_ref, p_ref, s_ref, o_ref):
        acc = p_ref[...].astype(F32)
        for s in range(n):
            acc = acc + s_ref[s].astype(F32)
        o_ref[...] = acc

    grid_spec = pltpu.PrefetchScalarGridSpec(
        num_scalar_prefetch=1, grid=(r // tr,),
        in_specs=[pl.BlockSpec((None, tr, c), lambda i, ix: (ix[0], i, 0)), pl.BlockSpec((n, tr, c), lambda i, ix: (0, i, 0))],
        out_specs=pl.BlockSpec((tr, c), lambda i, ix: (i, 0)))
    return pl.pallas_call(body, name=name, grid_spec=grid_spec, out_shape=jax.ShapeDtypeStruct((r, c), F32),
                          compiler_params=_params(("parallel",)))(chip1, parts, landed)


def _sum_slots(name, stacked):
    n, r, c = stacked.shape
    tr = _tile(r, 256, 8)

    def body(s_ref, o_ref):
        acc = s_ref[0]
        for s in range(1, n):
            acc = acc + s_ref[s]
        o_ref[...] = acc

    return pl.pallas_call(
        body, name=name, grid=(r // tr,), in_specs=[pl.BlockSpec((n, tr, c), lambda i: (0, i, 0))],
        out_specs=pl.BlockSpec((tr, c), lambda i: (i, 0)),
        out_shape=jax.ShapeDtypeStruct((r, c), F32), compiler_params=_params(("parallel",)),
    )(stacked)


def _adam_math(w, g, m, v):
    bc1, bc2 = 1.0 - ADAM_B1 ** ADAM_STEP, 1.0 - ADAM_B2 ** ADAM_STEP
    nm = ADAM_B1 * m + (1.0 - ADAM_B1) * g
    nv = ADAM_B2 * v + (1.0 - ADAM_B2) * (g * g)
    return -ADAM_LR * ((nm / bc1) / (jnp.sqrt(nv / bc2) + ADAM_EPS) + ADAM_WD * w), nm, nv


def _adamw_halves(name, w, g_mine, g_theirs, m, v, core):
    r, c = w.shape
    h = r // 2
    tr = _tile(h, 128, 8)
    nth = h // tr

    def body(c_ref, w_ref, gm_ref, gt_ref, m_ref, v_ref, g_ref, d_ref, nm_ref, nv_ref):
        g = jnp.where(pl.program_id(0) // nth == c_ref[0], gm_ref[...], gt_ref[...])
        g_ref[...] = g
        d_ref[...], nm_ref[...], nv_ref[...] = _adam_math(w_ref[...], g, m_ref[...], v_ref[...])

    full = pl.BlockSpec((tr, c), lambda i, cr: (i, 0))
    half = pl.BlockSpec((tr, c), lambda i, cr: (i % nth, 0))
    grid_spec = pltpu.PrefetchScalarGridSpec(num_scalar_prefetch=1, grid=(r // tr,),
                                             in_specs=[full, half, half, full, full], out_specs=[full] * 4)
    return pl.pallas_call(body, name=name, grid_spec=grid_spec, out_shape=[jax.ShapeDtypeStruct((r, c), F32)] * 4,
                          compiler_params=_params(("parallel",)))(core, w, g_mine, g_theirs, m, v)


def _adamw(name, w, g, m, v):
    r, c = w.shape
    by_cols = r % 8 != 0 and c % LANES == 0
    tr, tc = (r, _tile(c, 256, LANES)) if by_cols else (_tile(r, 256, 8), c)

    def body(w_ref, g_ref, m_ref, v_ref, d_ref, nm_ref, nv_ref):
        d_ref[...], nm_ref[...], nv_ref[...] = _adam_math(w_ref[...], g_ref[...], m_ref[...], v_ref[...])

    spec = pl.BlockSpec((tr, tc), (lambda i: (0, i)) if by_cols else (lambda i: (i, 0)))
    return pl.pallas_call(
        body, name=name, grid=(c // tc if by_cols else r // tr,), in_specs=[spec] * 4, out_specs=[spec] * 3,
        out_shape=[jax.ShapeDtypeStruct((r, c), F32)] * 3, compiler_params=_params(("parallel",)),
    )(w, g, m, v)


def _pad_cols(a, cols):
    return jnp.pad(a, ((0, 0), (0, cols - a.shape[1])))


def _rot_cols(w):
    h = w.shape[-1] // 2
    return jnp.concatenate([-w[..., h:], w[..., :h]], axis=-1)


def _unrot_cols(d):
    h = d.shape[-1] // 2
    return jnp.concatenate([d[..., h:], -d[..., :h]], axis=-1)


def _logical(g):
    return jnp.transpose(g, (1, 0, 2)).reshape(g.shape[1], N_CHIPS * g.shape[2])


def _chunks(a, n):
    return jnp.transpose(a.reshape(a.shape[0], N_CHIPS, n), (1, 0, 2))


def kernel(x, positions, pre_mix_norm, w_in, q_a_norm, w_uq, kv_a_norm, w_ukv, b_forget, b_gate, w_branch_mla, w_branch_fox, w_out, post_mix_norm, pre_ffn_norm, w_up, conv_w, conv_b, w_down, post_ffn_norm, loss_target, m_pre_mix_norm, m_w_in, m_q_a_norm, m_w_uq, m_kv_a_norm, m_w_ukv, m_b_forget, m_b_gate, m_w_branch_mla, m_w_branch_fox, m_w_out, m_post_mix_norm, m_pre_ffn_norm, m_w_up, m_conv_w, m_conv_b, m_w_down, m_post_ffn_norm, v_pre_mix_norm, v_w_in, v_q_a_norm, v_w_uq, v_kv_a_norm, v_w_ukv, v_b_forget, v_b_gate, v_w_branch_mla, v_w_branch_fox, v_w_out, v_post_mix_norm, v_pre_ffn_norm, v_w_up, v_conv_w, v_conv_b, v_w_down, v_post_ffn_norm):
    n_seq, seq, d = x.shape
    n_tok = n_seq * seq
    d_in = N_CHIPS * w_in.shape[1]
    two_f = N_CHIPS * w_up.shape[1]
    ff_dim = two_f // 2
    assert d_in == QL + KVL + ROPE + 3 * HEADS * FDIM + HEADS + 2 * d
    n_in_shard = w_in.shape[1]
    in_pad = -(-n_in_shard // LANES) * LANES
    hd = HEADS * LANES
    xc, yc, cc = lax.axis_index("x"), lax.axis_index("y"), lax.axis_index("c")
    chip = 2 * xc + yc
    t_attn = _tile(seq, 512, 128)

    shards = [_pad_cols(w_in, in_pad).astype(BF16), w_uq.astype(BF16), w_ukv.astype(BF16), w_branch_mla.astype(BF16),
              w_branch_fox.astype(BF16), w_out.astype(BF16), w_up.astype(BF16), w_down.astype(BF16)]
    cw8 = jnp.pad(conv_w, ((0, 5), (0, 0)))
    (g_in,), _ = _all_gather_weights(shards[:1], [])
    n_attn_steps = n_seq * (seq // t_attn)
    side_proj = _gather_side([shards[3], shards[4], shards[5], shards[1], shards[2]], [cw8],
                             mid_step=0.9)
    side_ffn = _gather_side([shards[7]], [], mid_step=0.7)
    up_rows = shards[6].shape[0]
    up_cuts = [0, up_rows // 8, up_rows // 2, 7 * up_rows // 8, up_rows]

    def side_up(piece, filled, mid_step):
        lo, hi = up_cuts[piece], up_cuts[piece + 1]
        return _gather_side([shards[6][lo:hi]], [], mid_step=mid_step, into=[(filled, up_rows, lo)])

    o_q, o_kv, o_kpe = 0, QL, QL + KVL
    o_f = o_kpe + ROPE
    o_fl = o_f + 3 * hd
    o_g = o_fl + HEADS

    def chip_cols(lo, hi):
        out = []
        while lo < hi:
            j = lo // n_in_shard
            end = min(hi, (j + 1) * n_in_shard)
            out.append((j, lo - j * n_in_shard, end - j * n_in_shard))
            lo = end
        return out

    take = lambda lo, hi: [g_in[j, :, a:b] for j, a, b in chip_cols(lo, hi)]
    w_kpe = jnp.concatenate(take(o_kpe, o_f), axis=1)
    zeros = lambda n: jnp.zeros((d, n), BF16)
    win_p = jnp.concatenate(
        take(o_g, d_in) + take(o_q, o_kpe) + [w_kpe, zeros(LANES - ROPE), _rot_cols(w_kpe)] + take(o_fl, o_g)
        + [zeros(LANES - ROPE - HEADS)] + take(o_f, o_fl), axis=1)
    n_p = win_p.shape[1]
    cb_gm, cb_gf = 0, 1
    c_lat = 2 * d
    c_kx, c_kr = c_lat + QL + KVL, c_lat + QL + KVL + LANES
    n_pa = c_kr + LANES
    assert n_p == n_pa + 3 * hd

    n_bm, n_up = w_branch_mla.shape[1], w_up.shape[1]
    l_bm, l_up = _Chunked(n_bm), _Chunked(n_up)
    wt = n_up // 2
    n_ut = two_f // wt
    il = lambda cblk: jnp.where(cblk < n_ut // 2, 2 * cblk, 2 * (cblk - n_ut // 2) + 1)
    l_il = _Plain(il)
    to_il = lambda a: a.reshape(a.shape[0], 2, n_ut // 2, wt).transpose(0, 2, 1, 3).reshape(a.shape[0], two_f)
    from_il = lambda a: a.reshape(a.shape[0], n_ut // 2, 2, wt).transpose(0, 2, 1, 3).reshape(a.shape[0], two_f)

    inv_freq = 1.0 / (ROPE_THETA ** (jnp.arange(0, ROPE, 2, dtype=F32) / ROPE))
    ang = positions.astype(F32).reshape(n_tok, 1) * inv_freq
    cos, sin = jnp.cos(ang), jnp.sin(ang)
    cs = _pad_cols(jnp.concatenate([cos, cos], axis=1), LANES)
    sn = _pad_cols(jnp.concatenate([sin, sin], axis=1), LANES)

    row = lambda v: v.reshape(1, -1)
    x2 = x.reshape(n_tok, d)
    tgt = loss_target.reshape(n_tok, d)

    (h,) = _rows("rms_pre_mix", lambda r, v: ([r[0] * _rstd(r[0]) * v[0]], []),
                 [(x2, d, 0)], [row(pre_mix_norm)], [(d, BF16)], [], n_tok)
    proj, (g_bm, g_bf, g_out, g_uq, g_ukv, g_cw) = _mm("proj_in", "nn", h, win_p, n_tok, n_pa, d, tm=2048, side=side_proj)
    w_out_full = g_out.reshape(d, d)
    uq3 = _logical(g_uq).reshape(QL, HEADS, NOPE + ROPE)
    pe = uq3[:, :, NOPE:]
    pad_pe = lambda a: jnp.pad(a, ((0, 0), (0, 0), (0, LANES - ROPE))).reshape(QL, hd)
    wuq_p = jnp.concatenate([uq3[:, :, :NOPE].reshape(QL, hd), pad_pe(pe), pad_pe(_rot_cols(pe))], axis=1)
    ukv3 = _logical(g_ukv).reshape(KVL, HEADS, NOPE + VDIM)
    wukv_p = jnp.concatenate([ukv3[:, :, :NOPE].reshape(KVL, hd), ukv3[:, :, NOPE:].reshape(KVL, hd)], axis=1)
    tn_f = _tile(3 * hd, 1024, 128)
    assert n_pa % tn_f == 0
    proj_f, (g_up,) = _mm("proj_in_fox", "nn", h, win_p, n_tok, 3 * hd, d, tn=tn_f, lb=_Plain(lambda cblk: cblk + n_pa // tn_f),
                          out_dtype=BF16, side=side_up(0, None, 0.8))

    bf_vec = jnp.pad(row(b_forget), ((0, 0), (ROPE, LANES - ROPE - HEADS)))

    def lat_fwd(r, v):
        ql, kvl = r[0], r[1]
        return [ql * _rstd(ql) * v[0], kvl * _rstd(kvl) * v[1], r[2] * r[4] + r[3] * r[5]], []

    qn, kvn, rk = _rows("latent_norms", lat_fwd,
                        [(proj, QL, c_lat // QL), (proj, KVL, (c_lat + QL) // KVL), (proj, LANES, c_kx // LANES),
                         (proj, LANES, c_kr // LANES), (cs, LANES, 0), (sn, LANES, 0)],
                        [row(q_a_norm), row(kv_a_norm)], [(QL, BF16), (KVL, BF16), (LANES, BF16)], [], n_tok)
    q_p = _mm("q_up", "nn", qn, wuq_p, n_tok, 3 * hd, QL)
    kv_p = _mm("kv_up", "nn", kvn, wukv_p, n_tok, 2 * hd, KVL, out_dtype=BF16)

    def rope_q(r, v):
        c8, s8 = jnp.tile(r[3], (1, HEADS)), jnp.tile(r[4], (1, HEADS))
        return [r[0], r[1] * c8 + r[2] * s8], []

    q_nope, rq = _rows("rope_q", rope_q, [(q_p, hd, 0), (q_p, hd, 1), (q_p, hd, 2), (cs, LANES, 0), (sn, LANES, 0)], [],
                       [(hd, BF16), (hd, BF16)], [], n_tok)

    mla_q = [(q_nope, 0, False), (rq, 0, False)]
    mla_k = [(kv_p, 0, False), (rk, 0, True)]
    mla_v = (kv_p, 1)
    mla_scale = (NOPE + ROPE) ** -0.5
    (o_mla, lse_mla), (g_up,) = _attn_fwd("mla_fwd", mla_q, mla_k, mla_v, None, CHUNK, mla_scale, n_seq, seq, t_attn,
                                          side=side_up(1, g_up, max(n_attn_steps - 2, 0)))

    c_run = _seq_cumsum("forget_cumsum", proj, c_kr // LANES, n_seq, seq, False,
                        pre=lambda z, b: _log_sigmoid(z + b), vec=bf_vec)
    nb_attn = seq // t_attn
    c_rowf = jnp.transpose(c_run[:, ROPE:ROPE + HEADS].reshape(n_seq, nb_attn, t_attn, HEADS), (0, 1, 3, 2))
    fox_q, fox_k, fox_v = [(proj_f, 0, False)], [(proj_f, 1, False)], (proj_f, 2)
    fox_scale = FDIM ** -0.5
    fox_bias = (c_run, c_rowf)
    (o_fox, lse_fox), (g_up,) = _attn_fwd("fox_fwd", fox_q, fox_k, fox_v, fox_bias, 1, fox_scale, n_seq, seq, t_attn,
                                          side=side_up(2, g_up, max(n_attn_steps - 2, 0)))

    pm, (g_up,) = _mm("branch_mla", "nn", o_mla, g_bm, n_tok, d, hd, lb=l_bm, tn=n_bm, out_dtype=BF16,
                      side=side_up(3, g_up, 0.7))
    pf = _mm("branch_fox", "nn", o_fox, g_bf, n_tok, d, hd, lb=l_bm, tn=n_bm, out_dtype=BF16)
    bg = row(b_gate)

    def merge(r, v):
        return [_sigmoid(r[0] + v[0]) * r[2] + _sigmoid(r[1] + v[1]) * r[3]], []

    (merged,) = _rows("gate_merge", merge, [(proj, d, cb_gm), (proj, d, cb_gf), (pm, d, 0), (pf, d, 0)],
                      [bg[:, :d], bg[:, d:]], [(d, BF16)], [], n_tok)
    y1 = _mm("mix_out", "nn", merged, w_out_full, n_tok, d, d)

    def resid_norm(r, v):
        x1v = r[0] + r[1] * _rstd(r[1]) * v[0]
        return [x1v, x1v * _rstd(x1v) * v[1]], []

    x1, h2 = _rows("post_mix_pre_ffn", resid_norm, [(x2, d, 0), (y1, d, 0)], [row(post_mix_norm), row(pre_ffn_norm)],
                   [(d, F32), (d, BF16)], [], n_tok)

    u_il, got = _mm("ffn_up", "nn", h2, g_up, n_tok, two_f, d, lb=l_up, lo=l_il, tm=2048, tn=wt, out_dtype=BF16,
                    side=side_ffn)
    w_down_full = got[0].reshape(ff_dim, d)
    cw_il = to_il(_logical(g_cw)[:3])
    cb_il = to_il(row(conv_b))
    act = _conv_glu_fwd(u_il, cw_il, cb_il, n_seq, seq, wt)
    ff = _mm("ffn_down", "nn", act, w_down_full, n_tok, d, ff_dim)

    def final(r, v):
        x1v, ffv, tg = r
        diff = x1v + ffv * _rstd(ffv) * v[0] - tg
        dx2v = diff / d
        dffv, dg4 = _rms_bwd(ffv, v[0], dx2v)
        sq = jnp.sum(jnp.sum(diff * diff, axis=1, keepdims=True), axis=0, keepdims=True)
        return [dx2v, dffv], [dg4, jnp.broadcast_to(sq, (1, LANES))]

    dx2, dff, dg_post_ffn, sq_sum = _rows("loss_post_ffn_bwd", final, [(x1, d, 0), (ff, d, 0), (tgt, d, 0)],
                                          [row(post_ffn_norm)], [(d, F32), (d, BF16)], [(1, d), (1, LANES)], n_tok)
    rs_parts, rs_landed = {}, {}
    core = jnp.reshape(cc, (1,)).astype(jnp.int32)

    def pair_reduce(tag, names, grads):
        theirs = _pair_split("rs_pair_split_" + tag, grads)
        for nm, g, b in zip(names, grads, theirs):
            rs_parts[nm] = _pair_add("rs_pair_add_" + nm, g, b, core)

    dact = _mm("ffn_down_dx", "nt", dff, w_down_full, n_tok, ff_dim, d, tm=2048, tn=wt, out_dtype=BF16)
    gw_down = _mm("ffn_down_dw", "tn", act, dff, ff_dim, d, n_tok, tm=wt, out_dtype=BF16)
    pair_reduce("down", ["w_down"], [gw_down.reshape(N_CHIPS, ff_dim // N_CHIPS, d)])
    d_il, conv_acc = _conv_glu_bwd_pre(u_il, dact, cw_il, cb_il, n_seq, seq, wt)
    du_il = _conv_bwd_input(d_il, cw_il, n_seq, seq, wt)
    gw_up, got = _mm("ffn_up_dw", "tn", h2, du_il, d, two_f, n_tok, lb=l_il, lo=l_up, tn=wt, out_dtype=BF16,
                     side=_scatter_side([rs_parts["w_down"]]))
    rs_landed["w_down"] = got[0]
    pair_reduce("up", ["w_up"], [gw_up])
    dh2, got = _mm("ffn_up_dx", "nt", du_il, g_up, n_tok, d, two_f, la=l_il, lb=l_up, tk=wt,
                   side=_scatter_side([rs_parts["w_up"]]))
    rs_landed["w_up"] = got[0]

    def mid_bwd(r, v):
        x1v, y1v, dx2v, dh2v = r
        d3, dg3 = _rms_bwd(x1v, v[1], dh2v)
        dx1v = dx2v + d3
        dy1v, dg2 = _rms_bwd(y1v, v[0], dx1v)
        return [dx1v, dy1v], [dg3, dg2]

    dx1, dy1, dg_pre_ffn, dg_post_mix = _rows(
        "pre_ffn_post_mix_bwd", mid_bwd, [(x1, d, 0), (y1, d, 0), (dx2, d, 0), (dh2, d, 0)],
        [row(post_mix_norm), row(pre_ffn_norm)], [(d, F32), (d, BF16)], [(1, d), (1, d)], n_tok)
    dmerged = _mm("mix_out_dx", "nt", dy1, w_out_full, n_tok, d, d, out_dtype=BF16)
    gw_out = _mm("mix_out_dw", "tn", merged, dy1, d, d, n_tok, out_dtype=BF16)

    def gate_bwd(r, v):
        zm, zf, pmv, pfv, dm = r
        gm, gf = _sigmoid(zm + v[0]), _sigmoid(zf + v[1])
        dzm, dzf = dm * pmv * gm * (1.0 - gm), dm * pfv * gf * (1.0 - gf)
        return [dm * gm, dm * gf, jnp.concatenate([dzm, dzf], axis=1)], [_colsum(dzm), _colsum(dzf)]

    dpm, dpf, dz, dbg_m, dbg_f = _rows(
        "gate_merge_bwd", gate_bwd, [(proj, d, cb_gm), (proj, d, cb_gf), (pm, d, 0), (pf, d, 0), (dmerged, d, 0)],
        [bg[:, :d], bg[:, d:]], [(d, BF16), (d, BF16), (2 * d, BF16)], [(1, d), (1, d)], n_tok)
    tk_b = min(n_bm, 512)
    do_mla = _mm("branch_mla_dx", "nt", dpm, g_bm, n_tok, hd, d, lb=l_bm, tk=tk_b)
    do_fox = _mm("branch_fox_dx", "nt", dpf, g_bf, n_tok, hd, d, lb=l_bm, tk=tk_b)
    gw_bm = _mm("branch_mla_dw", "tn", o_mla, dpm, hd, d, n_tok, lo=l_bm, tn=n_bm, out_dtype=BF16)
    gw_bf = _mm("branch_fox_dw", "tn", o_fox, dpf, hd, d, n_tok, lo=l_bm, tn=n_bm, out_dtype=BF16)

    pair_reduce("mix", ["w_out", "w_branch_mla", "w_branch_fox"], [gw_out.reshape(N_CHIPS, d // N_CHIPS, d), gw_bm, gw_bf])
    delta_mla, dob_mla = _attn_delta("mla_delta", o_mla, do_mla, n_tok)
    (dq_nope, drq, dk_nope, drk_g, dv_mla), got = _attn_bwd(
        "mla_bwd", mla_q, mla_k, mla_v, dob_mla, lse_mla, delta_mla, None, CHUNK, mla_scale, n_seq, seq, t_attn, BF16,
        side=_scatter_side([rs_parts[nm] for nm in ("w_out", "w_branch_mla", "w_branch_fox")]))
    rs_landed.update(zip(("w_out", "w_branch_mla", "w_branch_fox"), got))
    delta_fox, dob_fox = _attn_delta("fox_delta", o_fox, do_fox, n_tok)
    chip1 = jnp.reshape(chip, (1,)).astype(jnp.int32)
    early = ("w_down", "w_up")
    early_halves = [_chip_sum("rs_chip_sum_" + nm, rs_parts[nm], rs_landed[nm], chip1) for nm in early]
    (dfq, dfk, dfv, dc_q, dc_k), early_other = _attn_bwd("fox_bwd", fox_q, fox_k, fox_v, dob_fox, lse_fox, delta_fox, fox_bias, 1,
                                                         fox_scale, n_seq, seq, t_attn, BF16, side=_swap_side(early_halves))
    dc_k8 = jnp.transpose(dc_k, (0, 2, 4, 1, 3)).reshape(n_tok, HEADS)
    dc128 = dc_q[0] + dc_q[1] + jnp.pad(dc_k8, ((0, 0), (ROPE, LANES - ROPE - HEADS)))
    dlogf = _seq_cumsum("forget_cumsum_bwd", dc128, 0, n_seq, seq, True)

    def mla_pack(r, v):
        dqn_v, drq_v, dkn_v, dv_v, drk_a, drk_b, c1, s1 = r
        c8, s8 = jnp.tile(c1, (1, HEADS)), jnp.tile(s1, (1, HEADS))
        drk_v = drk_a + drk_b
        return [jnp.concatenate([dqn_v, drq_v * c8, drq_v * s8], axis=1), jnp.concatenate([dkn_v, dv_v], axis=1),
                drk_v * c1, drk_v * s1], []

    dq_p, dkv_p, dkx, dkr = _rows(
        "mla_rope_bwd", mla_pack,
        [(dq_nope, hd, 0), (drq, hd, 0), (dk_nope, hd, 0), (dv_mla, hd, 0), (drk_g[0], LANES, 0), (drk_g[1], LANES, 0),
         (cs, LANES, 0), (sn, LANES, 0)],
        [], [(3 * hd, BF16), (2 * hd, BF16), (LANES, F32), (LANES, F32)], [], n_tok)
    dqn = _mm("q_up_dx", "nt", dq_p, wuq_p, n_tok, QL, 3 * hd)
    gw_uq_p = _mm("q_up_dw", "tn", qn, dq_p, QL, 3 * hd, n_tok, out_dtype=BF16)
    dkvn = _mm("kv_up_dx", "nt", dkv_p, wukv_p, n_tok, KVL, 2 * hd)
    gw_ukv_p = _mm("kv_up_dw", "tn", kvn, dkv_p, KVL, 2 * hd, n_tok, out_dtype=BF16)

    def lat_bwd(r, v):
        ql, kvl, dqn_v, dkvn_v, dkx_v, dkr_v, zblk, dlf = r
        dql, dgq = _rms_bwd(ql, v[0], dqn_v)
        dkvl, dgkv = _rms_bwd(kvl, v[1], dkvn_v)
        dfl = dlf * _sigmoid(-(zblk + v[2]))
        return [jnp.concatenate([dql, dkvl, dkx_v, dkr_v + dfl], axis=1)], [dgq, dgkv, _colsum(dfl)]

    dlat, dg_q, dg_kv, dbf = _rows(
        "latent_bwd", lat_bwd,
        [(proj, QL, c_lat // QL), (proj, KVL, (c_lat + QL) // KVL), (dqn, QL, 0), (dkvn, KVL, 0), (dkx, LANES, 0),
         (dkr, LANES, 0), (proj, LANES, c_kr // LANES), (dlogf, LANES, 0)],
        [row(q_a_norm), row(kv_a_norm), bf_vec], [(QL + KVL + 2 * LANES, BF16)], [(1, QL), (1, KVL), (1, LANES)], n_tok)
    dproj = [dz, dlat, dfq, dfk, dfv]
    gw_in_p = _mm_parts("proj_in_dw", "tn", h, dproj, d, n_p, n_tok, tk=1024, out_dtype=BF16)

    f32 = lambda a: a.astype(F32)
    kr_blk = gw_in_p[:, c_kr:c_kr + LANES]
    d_kpe = (f32(gw_in_p[:, c_kx:c_kx + ROPE]) + _unrot_cols(f32(kr_blk[:, :ROPE]))).astype(BF16)
    in_pieces = [(o_q, gw_in_p, c_lat, QL + KVL), (o_kpe, d_kpe, 0, ROPE), (o_f, gw_in_p, n_pa, 3 * hd),
                 (o_fl, kr_blk, ROPE, HEADS), (o_g, gw_in_p, 0, 2 * d)]
    gc_in = []
    for j in range(N_CHIPS):
        lo, hi, cols = j * n_in_shard, (j + 1) * n_in_shard, []
        for first, arr, at, width in in_pieces:
            a, b = max(lo, first), min(hi, first + width)
            if a < b:
                cols.append(arr[:, at + a - first:at + b - first])
        cols.append(jnp.zeros((d, in_pad - n_in_shard), BF16))
        gc_in.append(jnp.concatenate(cols, axis=1))
    gc_in = jnp.stack(gc_in)
    uq_parts = [gw_uq_p[:, i * hd:(i + 1) * hd].reshape(QL, HEADS, LANES) for i in range(3)]
    d_pe = (f32(uq_parts[1][:, :, :ROPE]) + _unrot_cols(f32(uq_parts[2][:, :, :ROPE]))).astype(BF16)
    gc_uq = _chunks(jnp.concatenate([uq_parts[0], d_pe], axis=2).reshape(QL, HEADS * (NOPE + ROPE)), w_uq.shape[1])
    gc_ukv = _chunks(jnp.concatenate([gw_ukv_p[:, :hd].reshape(KVL, HEADS, NOPE), gw_ukv_p[:, hd:].reshape(KVL, HEADS, VDIM)],
                                     axis=2).reshape(KVL, HEADS * (NOPE + VDIM)), w_ukv.shape[1])
    grads = [gc_in, gc_uq, gc_ukv]

    late = ["w_in", "w_uq", "w_ukv"]
    pair_reduce("late", late, grads)
    dh, got = _mm_parts("proj_in_dx", "nt", dproj, win_p, n_tok, d, n_p, side=_scatter_side([rs_parts[nm] for nm in late]))
    rs_landed.update(zip(late, got))

    def first_bwd(r, v):
        dxa, dg1 = _rms_bwd(r[0], v[0], r[1])
        return [r[2] + dxa], [dg1]

    grad_x, dg_pre_mix = _rows("pre_mix_bwd", first_bwd, [(x2, d, 0), (dh, d, 0), (dx1, d, 0)], [row(pre_mix_norm)],
                               [(d, F32)], [(1, d)], n_tok)
    big = list(rs_parts)
    rest = [nm for nm in big if nm not in early]
    halves = [_chip_sum("rs_chip_sum_" + nm, rs_parts[nm], rs_landed[nm], chip1) for nm in rest]
    g_halves = dict(zip(rest, zip(halves, _pair_swap(halves))))
    g_halves.update(zip(early, zip(early_halves, early_other)))

    conv_acc_l = from_il(conv_acc)
    pieces = [dg_pre_mix, dg_q, dg_kv, dbf, dbg_m, dbg_f, dg_post_mix, dg_pre_ffn, conv_acc_l[3:4], dg_post_ffn,
              conv_acc_l[0:1], conv_acc_l[1:2], conv_acc_l[2:3], sq_sum]
    sizes = [p.shape[1] for p in pieces]
    flat = jnp.concatenate(pieces, axis=1)
    n_rows = -(-flat.shape[1] // (8 * LANES)) * 8
    flat = _pad_cols(flat, n_rows * LANES).reshape(n_rows, LANES)
    slots = lax.dynamic_update_slice(_gather_small(flat), flat[None], (2 * chip + cc, 0, 0))
    total = _sum_slots("small_sum", slots).reshape(1, n_rows * LANES)
    offs = [sum(sizes[:i]) for i in range(len(sizes))]
    tot = [total[0, o:o + s] for o, s in zip(offs, sizes)]
    loss = 0.5 * tot[13][0] / d
    g_small = {"pre_mix_norm": tot[0], "q_a_norm": tot[1], "kv_a_norm": tot[2], "b_forget": tot[3][ROPE:ROPE + HEADS],
               "b_gate": jnp.concatenate([tot[4], tot[5]]), "post_mix_norm": tot[6], "pre_ffn_norm": tot[7],
               "conv_b": tot[8], "post_ffn_norm": tot[9]}
    gcw_full = jnp.stack([tot[10], tot[11], tot[12]])
    g_conv_w = lax.dynamic_slice(gcw_full, (0, chip * n_up), (3, n_up))

    given = dict(pre_mix_norm=(pre_mix_norm, m_pre_mix_norm, v_pre_mix_norm), w_in=(w_in, m_w_in, v_w_in),
                 q_a_norm=(q_a_norm, m_q_a_norm, v_q_a_norm), w_uq=(w_uq, m_w_uq, v_w_uq),
                 kv_a_norm=(kv_a_norm, m_kv_a_norm, v_kv_a_norm), w_ukv=(w_ukv, m_w_ukv, v_w_ukv),
                 b_forget=(b_forget, m_b_forget, v_b_forget), b_gate=(b_gate, m_b_gate, v_b_gate),
                 w_branch_mla=(w_branch_mla, m_w_branch_mla, v_w_branch_mla),
                 w_branch_fox=(w_branch_fox, m_w_branch_fox, v_w_branch_fox), w_out=(w_out, m_w_out, v_w_out),
                 post_mix_norm=(post_mix_norm, m_post_mix_norm, v_post_mix_norm),
                 pre_ffn_norm=(pre_ffn_norm, m_pre_ffn_norm, v_pre_ffn_norm), w_up=(w_up, m_w_up, v_w_up),
                 conv_w=(conv_w, m_conv_w, v_conv_w), conv_b=(conv_b, m_conv_b, v_conv_b),
                 w_down=(w_down, m_w_down, v_w_down), post_ffn_norm=(post_ffn_norm, m_post_ffn_norm, v_post_ffn_norm))
    order = list(given)
    grad, delta, new_m, new_v = {}, {}, {}, {}
    for nm in big:
        mine, theirs = g_halves[nm]
        if nm == "w_in":
            full = jnp.concatenate([jnp.where(cc == 0, mine, theirs), jnp.where(cc == 0, theirs, mine)], axis=0)
            grad[nm] = full[:, :n_in_shard]
            tr_out = _adamw("adamw_" + nm, *[jnp.transpose(a) for a in (given[nm][0], grad[nm], given[nm][1], given[nm][2])])
            delta[nm], new_m[nm], new_v[nm] = [jnp.transpose(a) for a in tr_out]
            continue
        grad[nm], delta[nm], new_m[nm], new_v[nm] = _adamw_halves("adamw_" + nm, given[nm][0], mine, theirs, given[nm][1],
                                                                  given[nm][2], core)
    grad["conv_w"] = g_conv_w
    delta["conv_w"], new_m["conv_w"], new_v["conv_w"] = _adamw("adamw_conv_w", conv_w, g_conv_w, m_conv_w, v_conv_w)
    small = list(g_small)
    padded = [-(-g_small[nm].shape[0] // LANES) * LANES for nm in small]
    s_rows = -(-sum(padded) // (8 * LANES)) * 8

    def pack(vals):
        cat = jnp.concatenate([jnp.pad(a, (0, p - a.shape[0])) for a, p in zip(vals, padded)])
        return jnp.pad(cat, (0, s_rows * LANES - cat.shape[0])).reshape(s_rows, LANES)

    packed = _adamw("adamw_small", pack([given[nm][0] for nm in small]), pack([g_small[nm] for nm in small]),
                    pack([given[nm][1] for nm in small]), pack([given[nm][2] for nm in small]))
    s_offs = [sum(padded[:i]) for i in range(len(small))]
    for nm, o in zip(small, s_offs):
        n_el = g_small[nm].shape[0]
        grad[nm] = g_small[nm]
        delta[nm], new_m[nm], new_v[nm] = [p.reshape(-1)[o:o + n_el] for p in packed]
    return (loss, grad_x.reshape(n_seq, seq, d), *[grad[nm] for nm in order], *[delta[nm] for nm in order],
            *[new_m[nm] for nm in order], *[new_v[nm] for nm in order])
```

```python
import functools
import math

import jax
import jax.numpy as jnp
from jax import lax
from jax.experimental import pallas as pl
from jax.experimental.pallas import tpu as pltpu

F32, BF16 = jnp.float32, jnp.bfloat16
MESH = pl.DeviceIdType.MESH

HEADS = 8
NOPE, ROPE, VDIM = 128, 64, 128
QL, KVL = 512, 256
FDIM = 128
CHUNK = 64
ROPE_THETA = 10000.0
EPS = 1e-6
NEG_INF = -1e30
ADAM_LR, ADAM_B1, ADAM_B2, ADAM_EPS, ADAM_WD, ADAM_STEP = 0.001, 0.9, 0.999, 1e-08, 0.01, 10

VMEM_LIMIT_BYTES = 52 * 1024 * 1024
LANES = 128
N_CHIPS = 4


def _params(sem):
    return pltpu.CompilerParams(dimension_semantics=sem, vmem_limit_bytes=VMEM_LIMIT_BYTES)


def _tile(n, target, mult):
    if n <= target:
        return n
    t = (target // mult) * mult
    while t >= mult:
        if n % t == 0:
            return t
        t -= mult
    raise ValueError(f"no tile for {n} (target {target}, multiple of {mult})")


class _Plain:
    def __init__(self, perm=None):
        self.perm = perm

    def spec(self, tr, tc, rc):
        perm = self.perm

        def imap(i, j, k):
            r, c = rc(i, j, k)
            return (r, perm(c) if perm is not None else c)

        return pl.BlockSpec((tr, tc), imap)

    def shape(self, rows, cols):
        return (rows, cols)


class _Chunked:
    def __init__(self, n):
        self.n = n

    def spec(self, tr, tc, rc):
        assert self.n % tc == 0, (self.n, tc)
        per = self.n // tc

        def imap(i, j, k):
            r, c = rc(i, j, k)
            return (c // per, r, c % per)

        return pl.BlockSpec((None, tr, tc), imap)

    def shape(self, rows, cols):
        assert cols == N_CHIPS * self.n
        return (N_CHIPS, rows, self.n)


_DIMS = {"nn": (((1,), (0,)), ((), ())), "nt": (((1,), (1,)), ((), ())), "tn": (((0,), (0,)), ((), ()))}


def _mm_single(name, mode, a, b, m, n, k, tm, tn, la, lb, lo, out_dtype, side):
    if mode == "nn":
        a_spec = la.spec(tm, k, lambda i, j, kk: (i, 0))
        b_spec = lb.spec(k, tn, lambda i, j, kk: (0, j))
    elif mode == "nt":
        a_spec = la.spec(tm, k, lambda i, j, kk: (i, 0))
        b_spec = lb.spec(tn, k, lambda i, j, kk: (j, 0))
    else:
        a_spec = la.spec(k, tm, lambda i, j, kk: (0, i))
        b_spec = lb.spec(k, tn, lambda i, j, kk: (0, j))
    o_spec = lo.spec(tm, tn, lambda i, j, kk: (i, j))
    dims = _DIMS[mode]

    def body(a_ref, b_ref, o_ref):
        o_ref[...] = lax.dot_general(a_ref[...].astype(BF16), b_ref[...].astype(BF16), dims,
                                     preferred_element_type=F32).astype(o_ref.dtype)

    (out,), got = _hosted_call(body, name, (m // tm, n // tn, 1), [a_spec, b_spec], [o_spec],
                               [jax.ShapeDtypeStruct(lo.shape(m, n), out_dtype)], (a, b), side,
                               semantics=("parallel", "parallel", "arbitrary"))
    return out if side is None else (out, got)


def _mm(name, mode, a, b, m, n, k, *, tm=1024, tn=1024, tk=2048, la=None, lb=None, lo=None, out_dtype=F32, side=None):
    la, lb, lo = la or _Plain(), lb or _Plain(), lo or _Plain()
    tm, tn, tk = _tile(m, tm, 128), _tile(n, tn, 128), _tile(k, tk, 128)
    nk = k // tk
    if nk == 1:
        return _mm_single(name, mode, a, b, m, n, k, tm, tn, la, lb, lo, out_dtype, side)
    if mode == "nn":
        a_spec = la.spec(tm, tk, lambda i, j, kk: (i, kk))
        b_spec = lb.spec(tk, tn, lambda i, j, kk: (kk, j))
    elif mode == "nt":
        a_spec = la.spec(tm, tk, lambda i, j, kk: (i, kk))
        b_spec = lb.spec(tn, tk, lambda i, j, kk: (j, kk))
    else:
        a_spec = la.spec(tk, tm, lambda i, j, kk: (kk, i))
        b_spec = lb.spec(tk, tn, lambda i, j, kk: (kk, j))
    o_spec = lo.spec(tm, tn, lambda i, j, kk: (i, j))
    dims = _DIMS[mode]

    def body(a_ref, b_ref, o_ref, acc_ref):
        kk = pl.program_id(2)

        @pl.when(kk == 0)
        def _():
            acc_ref[...] = jnp.zeros_like(acc_ref)

        acc_ref[...] += lax.dot_general(a_ref[...].astype(BF16), b_ref[...].astype(BF16), dims,
                                        preferred_element_type=F32)

        @pl.when(kk == nk - 1)
        def _():
            o_ref[...] = acc_ref[...].astype(o_ref.dtype)

    (out,), got = _hosted_call(body, name, (m // tm, n // tn, nk), [a_spec, b_spec], [o_spec],
                               [jax.ShapeDtypeStruct(lo.shape(m, n), out_dtype)], (a, b), side,
                               semantics=("parallel", "parallel", "arbitrary"), scratch=[pltpu.VMEM((tm, tn), F32)])
    return out if side is None else (out, got)


def _mm_parts(name, mode, a, b, m, n, k, *, part=1024, tm=1024, tn=1024, tk=2048, out_dtype=F32, side=None):
    parts = b if mode == "tn" else a
    widths = [p.shape[1] for p in parts]
    assert all(w % part == 0 for w in widths) and sum(widths) == (n if mode == "tn" else k)
    offs = [sum(widths[:i]) // part for i in range(len(widths))]
    nblk = [w // part for w in widths]
    if mode == "tn":
        tn, tk = part, _tile(k, tk, 128)
    else:
        tk, tn = part, _tile(n, tn, 128)
    tm = _tile(m, tm, 128)
    nk = k // tk
    grid = (m // tm, n // tn, nk)
    np_ = len(parts)

    def inside(idx, p):
        return jnp.logical_and(idx >= offs[p], idx < offs[p] + nblk[p])

    def part_spec(p):
        if mode == "tn":
            def imap(i, j, kk):
                on = inside(j, p)
                return (jnp.where(on, kk, 0), jnp.clip(j - offs[p], 0, nblk[p] - 1))
            return pl.BlockSpec((tk, tn), imap)

        def imap(i, j, kk):
            return (i, jnp.clip(kk - offs[p], 0, nblk[p] - 1))
        return pl.BlockSpec((tm, tk), imap)

    if mode == "tn":
        in_specs = [pl.BlockSpec((tk, tm), lambda i, j, kk: (kk, i))] + [part_spec(p) for p in range(np_)]
        args = [a] + list(parts)
    else:
        in_specs = [part_spec(p) for p in range(np_)] + [pl.BlockSpec((tn, tk), lambda i, j, kk: (j, kk))]
        args = list(parts) + [b]
    dims = _DIMS[mode]

    def body(*refs):
        o_ref, acc_ref = refs[-2], refs[-1]
        j, kk = pl.program_id(1), pl.program_id(2)

        @pl.when(kk == 0)
        def _():
            acc_ref[...] = jnp.zeros_like(acc_ref)

        for p in range(np_):
            @pl.when(inside(j if mode == "tn" else kk, p))
            def _(p=p):
                lhs, rhs = (refs[0], refs[1 + p]) if mode == "tn" else (refs[p], refs[np_])
                acc_ref[...] += lax.dot_general(lhs[...].astype(BF16), rhs[...].astype(BF16), dims, preferred_element_type=F32)

        @pl.when(kk == nk - 1)
        def _():
            o_ref[...] = acc_ref[...].astype(o_ref.dtype)

    (out,), got = _hosted_call(body, name, grid, in_specs, [pl.BlockSpec((tm, tn), lambda i, j, kk: (i, j))],
                               [jax.ShapeDtypeStruct((m, n), out_dtype)], args, side,
                               semantics=("parallel", "parallel", "arbitrary"), scratch=[pltpu.VMEM((tm, tn), F32)])
    return out if side is None else (out, got)


def _rows(name, fn, rows_in, vecs_in, rows_out, accs_out, n_rows, tr=256):
    tr = _tile(n_rows, tr, 16)
    nr, nv, no = len(rows_in), len(vecs_in), len(rows_out)

    def body(*refs):
        ins, vecs = refs[:nr], refs[nr:nr + nv]
        outs, accs = refs[nr + nv:nr + nv + no], refs[nr + nv + no:]
        ro, ac = fn([r[...] for r in ins], [v[...] for v in vecs])
        for o_ref, val in zip(outs, ro):
            o_ref[...] = val.astype(o_ref.dtype)
        if accs:
            @pl.when(pl.program_id(0) == 0)
            def _():
                for a_ref in accs:
                    a_ref[...] = jnp.zeros_like(a_ref)

            for a_ref, val in zip(accs, ac):
                a_ref[...] += val

    in_specs = [pl.BlockSpec((tr, cols), functools.partial(lambda i, cb: (i, cb), cb=cb)) for _, cols, cb in rows_in]
    in_specs += [pl.BlockSpec(v.shape, lambda i: (0, 0)) for v in vecs_in]
    out_specs = [pl.BlockSpec((tr, cols), lambda i: (i, 0)) for cols, _ in rows_out]
    out_specs += [pl.BlockSpec((r, cols), lambda i: (0, 0)) for r, cols in accs_out]
    out_shape = [jax.ShapeDtypeStruct((n_rows, cols), dt) for cols, dt in rows_out]
    out_shape += [jax.ShapeDtypeStruct((r, cols), F32) for r, cols in accs_out]
    res = pl.pallas_call(
        body, name=name, grid=(n_rows // tr,), in_specs=in_specs, out_specs=out_specs, out_shape=out_shape,
        compiler_params=_params(("arbitrary",)),
    )(*[a for a, _, _ in rows_in], *vecs_in)
    return res


def _colsum(v):
    return jnp.sum(v, axis=0, keepdims=True)


def _rstd(x):
    return lax.rsqrt(jnp.mean(x * x, axis=-1, keepdims=True) + EPS)


def _rms_bwd(x, g, dy):
    r = _rstd(x)
    xh = x * r
    dxh = dy * g
    dx = r * (dxh - xh * jnp.mean(dxh * xh, axis=-1, keepdims=True))
    return dx, _colsum(dy * xh)


def _sigmoid(z):
    return 1.0 / (1.0 + jnp.exp(-z))


_GELU_K = math.sqrt(2.0 / math.pi)


def _gelu_parts(g):
    t = jnp.tanh(_GELU_K * (g + 0.044715 * g * g * g))
    gel = 0.5 * g * (1.0 + t)
    dgel = 0.5 * (1.0 + t) + 0.5 * g * (1.0 - t * t) * (_GELU_K * (1.0 + 3.0 * 0.044715 * g * g))
    return gel, dgel


def _diag_visible(t, unit):
    rows = lax.broadcasted_iota(jnp.int32, (t, t), 0)
    cols = lax.broadcasted_iota(jnp.int32, (t, t), 1)
    if unit > 1:
        sh = int(math.log2(unit))
        assert 1 << sh == unit and t % unit == 0
        rows, cols = jnp.right_shift(rows, sh), jnp.right_shift(cols, sh)
    return cols <= rows


def _lane_pick(tile, lane):
    idx = lax.broadcasted_iota(jnp.int32, tile.shape, 1)
    return jnp.sum(jnp.where(idx == lane, tile, 0.0), axis=1, keepdims=True)


def _lane_put(tile, lane, col):
    idx = lax.broadcasted_iota(jnp.int32, tile.shape, 1)
    return jnp.where(idx == lane, col, tile)


def _head_cat(refs, shared, rows, h):
    hs = slice(h * LANES, (h + 1) * LANES)
    vals = [(r[rows, :] if sh else r[rows, hs]).astype(BF16) for r, sh in zip(refs, shared)]
    return vals[0] if len(vals) == 1 else jnp.concatenate(vals, axis=1)


def _blk_rows(i, t):
    return pl.ds(pl.multiple_of(i * t, t), t)


def _piece_specs(pieces, rows, row_idx):
    return [pl.BlockSpec((rows, LANES if sh else HEADS * LANES), functools.partial(lambda b, i, cb: (row_idx(b, i), cb), cb=cb))
            for _, cb, sh in pieces]


def _attn_fwd(name, qp, kp, vp, bias, unit, scale, n_seq, seq, t, side=None):
    nb = seq // t
    n_tok = n_seq * seq
    nq, nk_p = len(qp), len(kp)
    q_sh, k_sh = [p[2] for p in qp], [p[2] for p in kp]
    nbias = 2 if bias is not None else 0

    def body(*refs):
        q_refs, k_refs = refs[:nq], refs[nq:nq + nk_p]
        v_ref = refs[nq + nk_p]
        bias_refs = refs[nq + nk_p + 1:nq + nk_p + 1 + nbias]
        o_ref, lse_ref = refs[nq + nk_p + 1 + nbias:]
        qi = pl.program_id(1)
        lse_tile = jnp.zeros((t, LANES), F32)
        for h in range(HEADS):
            hs = slice(h * LANES, (h + 1) * LANES)
            q = _head_cat(q_refs, q_sh, slice(None), h)
            cq = _lane_pick(bias_refs[0][...], ROPE + h) if bias is not None else None

            def block(kb, carry, diag, h=h, hs=hs, q=q, cq=cq):
                m, l, acc = carry
                rows = _blk_rows(kb, t)
                s = lax.dot_general(q, _head_cat(k_refs, k_sh, rows, h), _DIMS["nt"], preferred_element_type=F32) * scale
                if bias is not None:
                    s = s + cq - bias_refs[1][kb, h:h + 1, :]
                if diag:
                    s = jnp.where(_diag_visible(t, unit), s, NEG_INF)
                m_new = jnp.maximum(m, jnp.max(s, axis=1, keepdims=True))
                alpha = jnp.exp(m - m_new)
                p = jnp.exp(s - m_new)
                l = alpha * l + jnp.sum(p, axis=1, keepdims=True)
                acc = alpha * acc + jnp.dot(p.astype(BF16), v_ref[rows, hs].astype(BF16), preferred_element_type=F32)
                return m_new, l, acc

            init = (jnp.full((t, 1), NEG_INF, F32), jnp.zeros((t, 1), F32), jnp.zeros((t, LANES), F32))
            carry = lax.fori_loop(0, qi, lambda kb, c: block(kb, c, False), init)
            m, l, acc = block(qi, carry, True)
            o_ref[:, hs] = acc / l
            lse_tile = _lane_put(lse_tile, h, m + jnp.log(l))
        lse_ref[...] = lse_tile

    tile_row = lambda b, i: b * nb + i
    seq_row = lambda b, i: b
    lane_tile = pl.BlockSpec((t, LANES), lambda b, i: (b * nb + i, 0))
    in_specs = _piece_specs(qp, t, tile_row) + _piece_specs(kp, seq, seq_row) + _piece_specs([vp + (False,)], seq, seq_row)
    args = [p[0] for p in qp] + [p[0] for p in kp] + [vp[0]]
    if bias is not None:
        in_specs += [lane_tile, pl.BlockSpec((None, nb, HEADS, t), lambda b, i: (b, 0, 0, 0))]
        args += list(bias)
    return _hosted_call(
        body, name, (n_seq, nb), in_specs,
        [pl.BlockSpec((t, HEADS * LANES), lambda b, i: (b * nb + i, 0)), lane_tile],
        [jax.ShapeDtypeStruct((n_tok, HEADS * LANES), F32), jax.ShapeDtypeStruct((n_tok, LANES), F32)], args, side)


HEAD_GROUPS = 2


def _attn_delta(name, o, do, n_tok):
    def fn(r, v):
        o_v, do_v = r
        tile = jnp.zeros((o_v.shape[0], LANES), F32)
        for h in range(HEADS):
            hs = slice(h * LANES, (h + 1) * LANES)
            tile = _lane_put(tile, h, jnp.sum(do_v[:, hs] * o_v[:, hs], axis=1, keepdims=True))
        return [tile, do_v], []

    return _rows(name, fn, [(o, HEADS * LANES, 0), (do, HEADS * LANES, 0)], [], [(LANES, F32), (HEADS * LANES, BF16)], [], n_tok)


def _attn_bwd(name, qp, kp, vp, dob, lse, delta, bias, unit, scale, n_seq, seq, t, grad_dtype, side=None):
    nb = seq // t
    n_tok = n_seq * seq
    ng = HEAD_GROUPS
    hg = HEADS // ng
    gw = hg * LANES
    nq, nk_p = len(qp), len(kp)
    q_sh, k_sh = [p[2] for p in qp], [p[2] for p in kp]
    assert not any(q_sh) and nq == nk_p
    nbias = 2 if bias is not None else 0
    n_in = nq + nk_p + 4 + nbias
    n_out = nq + nk_p + 1 + nbias

    def body(*refs):
        q_refs, k_refs = refs[:nq], refs[nq:nq + nk_p]
        v_ref, dob_ref, lse_ref, delta_ref = refs[nq + nk_p:nq + nk_p + 4]
        bias_refs = refs[nq + nk_p + 4:n_in]
        dq_refs, dk_refs = refs[n_in:n_in + nq], refs[n_in + nq:n_in + nq + nk_p]
        dv_ref = refs[n_in + nq + nk_p]
        dq_s, dcq_s = refs[n_in + n_out:]
        g, ki = pl.program_id(1), pl.program_id(2)

        @pl.when(ki == 0)
        def _():
            dq_s[...] = jnp.zeros_like(dq_s)
            dcq_s[...] = jnp.zeros_like(dcq_s)

        shared_acc = [jnp.zeros((t, LANES), F32) for _ in range(nk_p)]
        for hl in range(hg):
            h = g * hg + hl
            hs = slice(hl * LANES, (hl + 1) * LANES)
            k = _head_cat(k_refs, k_sh, slice(None), hl)
            v = v_ref[:, hs].astype(BF16)
            ck = bias_refs[1][pl.ds(h, 1), :] if bias is not None else None

            def block(qb, carry, diag, h=h, hl=hl, hs=hs, k=k, v=v, ck=ck):
                dk_acc, dv_acc, dc_acc = carry
                rows = _blk_rows(qb, t)
                q = _head_cat(q_refs, q_sh, rows, hl)
                s = lax.dot_general(q, k, _DIMS["nt"], preferred_element_type=F32) * scale
                if bias is not None:
                    s = s + _lane_pick(bias_refs[0][rows, :], ROPE + h) - ck
                if diag:
                    s = jnp.where(_diag_visible(t, unit), s, NEG_INF)
                p = jnp.exp(s - _lane_pick(lse_ref[rows, :], h))
                do_b = dob_ref[rows, hs]
                dp = lax.dot_general(do_b, v, _DIMS["nt"], preferred_element_type=F32)
                ds = p * (dp - _lane_pick(delta_ref[rows, :], h))
                ds_b = ds.astype(BF16)
                dq_blk = jnp.dot(ds_b, k, preferred_element_type=F32)
                for n_p in range(nq):
                    dq_s[rows, n_p * gw + hl * LANES:n_p * gw + (hl + 1) * LANES] += dq_blk[:, n_p * LANES:(n_p + 1) * LANES]
                if bias is not None:
                    lane = lax.broadcasted_iota(jnp.int32, (t, LANES), 1)
                    dcq_s[rows, :] += jnp.where(lane == ROPE + h, jnp.sum(ds, axis=1, keepdims=True), 0.0)
                return (dk_acc + lax.dot_general(ds_b, q, _DIMS["tn"], preferred_element_type=F32),
                        dv_acc + lax.dot_general(p.astype(BF16), do_b, _DIMS["tn"], preferred_element_type=F32),
                        dc_acc - jnp.sum(ds, axis=0, keepdims=True))

            init = (jnp.zeros((t, nk_p * LANES), F32), jnp.zeros((t, LANES), F32), jnp.zeros((1, t), F32))
            carry = block(ki, init, True)
            dk_acc, dv_acc, dc_acc = lax.fori_loop(ki + 1, nb, lambda qb, c: block(qb, c, False), carry)
            for n_p in range(nk_p):
                part = dk_acc[:, n_p * LANES:(n_p + 1) * LANES] * scale
                if k_sh[n_p]:
                    shared_acc[n_p] = shared_acc[n_p] + part
                else:
                    dk_refs[n_p][:, hs] = part.astype(grad_dtype)
            dv_ref[:, hs] = dv_acc.astype(grad_dtype)
            if bias is not None:
                refs[n_in + n_out - 1][hl:hl + 1, :] = dc_acc
        for n_p in range(nk_p):
            if k_sh[n_p]:
                dk_refs[n_p][...] = shared_acc[n_p]

        @pl.when(ki == nb - 1)
        def _():
            for n_p in range(nq):
                dq_refs[n_p][...] = (dq_s[:, n_p * gw:(n_p + 1) * gw] * scale).astype(grad_dtype)
            if bias is not None:
                refs[n_in + n_out - 2][...] = dcq_s[...]

    def spec(rows, row_idx, cb, shared):
        if shared:
            return pl.BlockSpec((rows, LANES), lambda b, g, i: (row_idx(b, i), cb))
        return pl.BlockSpec((rows, gw), lambda b, g, i: (row_idx(b, i), cb * ng + g))

    tile_row = lambda b, i: b * nb + i
    seq_row = lambda b, i: b
    lane_seq = pl.BlockSpec((seq, LANES), lambda b, g, i: (b, 0))
    in_specs = [spec(seq, seq_row, cb, sh) for _, cb, sh in qp] + [spec(t, tile_row, cb, sh) for _, cb, sh in kp]
    in_specs += [spec(t, tile_row, vp[1], False), spec(seq, seq_row, 0, False), lane_seq, lane_seq]
    args = [p[0] for p in qp] + [p[0] for p in kp] + [vp[0], dob, lse, delta]
    if bias is not None:
        in_specs += [lane_seq, pl.BlockSpec((None, None, HEADS, t), lambda b, g, i: (b, i, 0, 0))]
        args += list(bias)
    group_tile = pl.BlockSpec((None, t, LANES), lambda b, g, i: (g, b * nb + i, 0))
    out_specs = [spec(seq, seq_row, 0, False)] * nq
    out_specs += [group_tile if sh else spec(t, tile_row, 0, False) for sh in k_sh] + [spec(t, tile_row, 0, False)]
    head_shape = jax.ShapeDtypeStruct((n_tok, HEADS * LANES), grad_dtype)
    out_shape = [head_shape] * nq + [jax.ShapeDtypeStruct((ng, n_tok, LANES), F32) if sh else head_shape for sh in k_sh]
    out_shape.append(head_shape)
    if bias is not None:
        out_specs += [pl.BlockSpec((None, seq, LANES), lambda b, g, i: (g, b, 0)),
                      pl.BlockSpec((None, None, None, hg, t), lambda b, g, i: (b, g, i, 0, 0))]
        out_shape += [jax.ShapeDtypeStruct((ng, n_tok, LANES), F32), jax.ShapeDtypeStruct((n_seq, ng, nb, hg, t), F32)]
    return _hosted_call(body, name, (n_seq, ng, nb), in_specs, out_specs, out_shape, args, side,
                        semantics=("parallel", "arbitrary", "arbitrary"),
                        scratch=[pltpu.VMEM((seq, nq * gw), F32), pltpu.VMEM((seq, LANES), F32)])


def _seq_cumsum(name, x, col_block, n_seq, seq, reverse, pre=None, vec=None):
    t = _tile(seq, 256, 128)
    nb = seq // t

    def body(*refs):
        x_ref = refs[0]
        vec_ref = refs[1] if vec is not None else None
        o_ref, carry = refs[-2], refs[-1]

        @pl.when(pl.program_id(1) == 0)
        def _():
            carry[...] = jnp.zeros_like(carry)

        v = x_ref[...]
        if pre is not None:
            v = pre(v, vec_ref[...])
        r = lax.broadcasted_iota(jnp.int32, (t, t), 0)
        c = lax.broadcasted_iota(jnp.int32, (t, t), 1)
        tri = jnp.where((c >= r) if reverse else (c <= r), 1.0, 0.0).astype(BF16)
        hi = v.astype(BF16)
        mid = (v - hi.astype(F32)).astype(BF16)
        lo = (v - hi.astype(F32) - mid.astype(F32)).astype(BF16)
        acc = jnp.dot(tri, hi, preferred_element_type=F32)
        acc += jnp.dot(tri, mid, preferred_element_type=F32)
        acc += jnp.dot(tri, lo, preferred_element_type=F32)
        o_ref[...] = acc + carry[...]
        carry[...] += _colsum(v)

    blk = (lambda b, i: (b * nb + nb - 1 - i)) if reverse else (lambda b, i: (b * nb + i))
    in_specs = [pl.BlockSpec((t, LANES), lambda b, i: (blk(b, i), col_block))]
    args = [x]
    if vec is not None:
        in_specs.append(pl.BlockSpec(vec.shape, lambda b, i: (0, 0)))
        args.append(vec)
    return pl.pallas_call(
        body, name=name, grid=(n_seq, nb), in_specs=in_specs,
        out_specs=pl.BlockSpec((t, LANES), lambda b, i: (blk(b, i), 0)),
        out_shape=jax.ShapeDtypeStruct((n_seq * seq, LANES), F32),
        scratch_shapes=[pltpu.VMEM((1, LANES), F32)],
        compiler_params=_params(("arbitrary", "arbitrary")),
    )(*args)


def _log_sigmoid(z):
    return -(jnp.maximum(-z, 0.0) + jnp.log(1.0 + jnp.exp(-jnp.abs(z))))


def _shift_down(u, prev_ref, n):
    out = pltpu.roll(u, n, 0)
    row = lax.broadcasted_iota(jnp.int32, u.shape, 0)
    for r in range(n):
        out = jnp.where(row == r, prev_ref[8 - n + r:8 - n + r + 1, :], out)
    return out


def _shift_up(u, next_ref, n):
    ts = u.shape[0]
    out = pltpu.roll(u, ts - n, 0)
    row = lax.broadcasted_iota(jnp.int32, u.shape, 0)
    for r in range(n):
        out = jnp.where(row == ts - n + r, next_ref[r:r + 1, :], out)
    return out


def _conv_taps(u, prev_ref, w_ref, b_ref):
    s1, s2 = _shift_down(u, prev_ref, 1), _shift_down(u, prev_ref, 2)
    return (w_ref[0:1, :] * s2 + w_ref[1:2, :] * s1 + w_ref[2:3, :] * u) + b_ref[...], s1, s2


def _conv_glu_fwd(u_il, cw_il, cb_il, n_seq, seq, wt):
    n_tok, two_f = u_il.shape
    nct = two_f // (2 * wt)
    ts = _tile(seq, 256, 8)
    ns = seq // ts

    def body(u_ref, w_ref, b_ref, a_ref, carry):
        @pl.when(pl.program_id(2) == 0)
        def _():
            carry[...] = jnp.zeros_like(carry)

        u = u_ref[...].astype(F32)
        uc, _, _ = _conv_taps(u, carry, w_ref, b_ref)
        gel, _ = _gelu_parts(uc[:, :wt])
        a_ref[...] = (gel * uc[:, wt:]).astype(a_ref.dtype)
        carry[...] = u[ts - 8:, :]

    return pl.pallas_call(
        body, name="conv_glu_fwd", grid=(nct, n_seq, ns),
        in_specs=[pl.BlockSpec((ts, 2 * wt), lambda j, b, s: (b * ns + s, j)),
                  pl.BlockSpec((3, 2 * wt), lambda j, b, s: (0, j)),
                  pl.BlockSpec((1, 2 * wt), lambda j, b, s: (0, j))],
        out_specs=pl.BlockSpec((ts, wt), lambda j, b, s: (b * ns + s, j)),
        out_shape=jax.ShapeDtypeStruct((n_tok, two_f // 2), BF16),
        scratch_shapes=[pltpu.VMEM((8, 2 * wt), F32)],
        compiler_params=_params(("parallel", "arbitrary", "arbitrary")),
    )(u_il, cw_il, cb_il)


def _conv_glu_bwd_pre(u_il, da, cw_il, cb_il, n_seq, seq, wt):
    n_tok, two_f = u_il.shape
    nct = two_f // (2 * wt)
    ts = _tile(seq, 256, 8)
    ns = seq // ts

    def body(u_ref, da_ref, w_ref, b_ref, d_ref, acc_ref, carry):
        first = jnp.logical_and(pl.program_id(1) == 0, pl.program_id(2) == 0)

        @pl.when(first)
        def _():
            acc_ref[...] = jnp.zeros_like(acc_ref)

        @pl.when(pl.program_id(2) == 0)
        def _():
            carry[...] = jnp.zeros_like(carry)

        u = u_ref[...].astype(F32)
        uc, s1, s2 = _conv_taps(u, carry, w_ref, b_ref)
        gel, dgel = _gelu_parts(uc[:, :wt])
        da_v = da_ref[...].astype(F32)
        d = jnp.concatenate([da_v * uc[:, wt:] * dgel, da_v * gel], axis=1)
        d_ref[...] = d.astype(d_ref.dtype)
        acc_ref[0:1, :] += _colsum(d * s2)
        acc_ref[1:2, :] += _colsum(d * s1)
        acc_ref[2:3, :] += _colsum(d * u)
        acc_ref[3:4, :] += _colsum(d)
        carry[...] = u[ts - 8:, :]

    return pl.pallas_call(
        body, name="conv_glu_bwd_pre", grid=(nct, n_seq, ns),
        in_specs=[pl.BlockSpec((ts, 2 * wt), lambda j, b, s: (b * ns + s, j)),
                  pl.BlockSpec((ts, wt), lambda j, b, s: (b * ns + s, j)),
                  pl.BlockSpec((3, 2 * wt), lambda j, b, s: (0, j)),
                  pl.BlockSpec((1, 2 * wt), lambda j, b, s: (0, j))],
        out_specs=[pl.BlockSpec((ts, 2 * wt), lambda j, b, s: (b * ns + s, j)),
                   pl.BlockSpec((8, 2 * wt), lambda j, b, s: (0, j))],
        out_shape=[jax.ShapeDtypeStruct((n_tok, two_f), BF16), jax.ShapeDtypeStruct((8, two_f), F32)],
        scratch_shapes=[pltpu.VMEM((8, 2 * wt), F32)],
        compiler_params=_params(("parallel", "arbitrary", "arbitrary")),
    )(u_il, da, cw_il, cb_il)


def _conv_bwd_input(d_il, cw_il, n_seq, seq, wt):
    n_tok, two_f = d_il.shape
    nct = two_f // (2 * wt)
    ts = _tile(seq, 256, 8)
    ns = seq // ts

    def body(d_ref, w_ref, o_ref, carry):
        @pl.when(pl.program_id(2) == 0)
        def _():
            carry[...] = jnp.zeros_like(carry)

        d = d_ref[...].astype(F32)
        o_ref[...] = (w_ref[2:3, :] * d + w_ref[1:2, :] * _shift_up(d, carry, 1)
                      + w_ref[0:1, :] * _shift_up(d, carry, 2)).astype(o_ref.dtype)
        carry[...] = d[:8, :]

    rev = lambda j, b, s: (b * ns + ns - 1 - s, j)
    return pl.pallas_call(
        body, name="conv_bwd_input", grid=(nct, n_seq, ns),
        in_specs=[pl.BlockSpec((ts, 2 * wt), rev), pl.BlockSpec((3, 2 * wt), lambda j, b, s: (0, j))],
        out_specs=pl.BlockSpec((ts, 2 * wt), rev),
        out_shape=jax.ShapeDtypeStruct((n_tok, two_f), BF16),
        scratch_shapes=[pltpu.VMEM((8, 2 * wt), F32)],
        compiler_params=_params(("parallel", "arbitrary", "arbitrary")),
    )(d_il, cw_il)


HBM = pl.BlockSpec(memory_space=pltpu.HBM)
_CHIP_FLIPS = ((1, 0), (0, 1), (1, 1))


def _place():
    x, y, c = lax.axis_index("x"), lax.axis_index("y"), lax.axis_index("c")
    return x, y, c, 2 * x + y


def _flip(v, f):
    return 1 - v if f else v


def _half_rows(c, half):
    return pl.ds(pl.multiple_of(c * half, 16), half)


def _remote(src, dst, ssem, rsem, dev):
    return pltpu.make_async_remote_copy(src_ref=src, dst_ref=dst, send_sem=ssem, recv_sem=rsem,
                                        device_id=dev, device_id_type=MESH)


def _comm_call(name, body, ins, out_shapes, n_sems):
    return pl.pallas_call(
        body, name=name, in_specs=[HBM] * len(ins), out_specs=[HBM] * len(out_shapes),
        out_shape=[pltpu.HBM(s.shape, s.dtype) for s in out_shapes],
        scratch_shapes=[pltpu.SemaphoreType.DMA((n_sems,)), pltpu.SemaphoreType.DMA((n_sems,))],
    )(*ins)


def _all_gather_weights(shards, smalls):
    side = _gather_side(shards, smalls)
    nt = len(shards) + len(smalls)

    def body(*refs):
        for part in (side.start, side.mid, side.end):
            part(refs[:nt], refs[nt:2 * nt], *refs[2 * nt:])

    res = _comm_call("all_gather_weights", body, side.ins, side.outs, side.n_sems)
    return res[:len(shards)], res[len(shards):]


def _pair_split(name, grads):
    n = len(grads)

    def body(*refs):
        src, got = refs[:n], refs[n:2 * n]
        ssem, rsem = refs[2 * n:]
        x, y, c, _ = _place()
        cps = []
        for w in range(n):
            half = grads[w].shape[1] // 2
            cp = _remote(src[w].at[:, _half_rows(1 - c, half)], got[w], ssem.at[w], rsem.at[w], (x, y, 1 - c))
            cp.start()
            cps.append(cp)
        for cp in cps:
            cp.wait()

    outs = [jax.ShapeDtypeStruct((g.shape[0], g.shape[1] // 2, g.shape[2]), g.dtype) for g in grads]
    return _comm_call(name, body, grads, outs, n)


class _Side:
    def __init__(self, ins, outs, n_sems, start, mid, end, mid_step=None):
        self.ins, self.outs, self.n_sems = list(ins), list(outs), n_sems
        self.start, self.mid, self.end, self.mid_step = start, mid, end, mid_step
        self.aliases = {}


def _scatter_side(parts):
    n = len(parts)

    def copies(src, dst, ssem, rsem):
        x, y, c, _ = _place()
        out = []
        for w in range(n):
            for k, (fx, fy) in enumerate(_CHIP_FLIPS):
                px, py = _flip(x, fx), _flip(y, fy)
                out.append(_remote(src[w].at[2 * px + py], dst[w].at[k], ssem.at[w * 3 + k], rsem.at[w * 3 + k], (px, py, c)))
        return out

    def start(src, dst, ssem, rsem):
        for cp in copies(src, dst, ssem, rsem):
            cp.start()

    def end(src, dst, ssem, rsem):
        for cp in copies(src, dst, ssem, rsem):
            cp.wait()

    outs = [jax.ShapeDtypeStruct((3,) + p.shape[1:], p.dtype) for p in parts]
    return _Side(parts, outs, 3 * n, start, None, end)


def _gather_side(shards, smalls, mid_step=None, into=None):
    n, ns = len(shards), len(smalls)
    into = into or [(None, a.shape[0], 0) for a in shards]

    def dst_rows(w, c):
        half = shards[w].shape[0] // 2
        return pl.ds(pl.multiple_of(into[w][2] + c * half, 16), half)

    def ici(src, dst, ssem, rsem, w, k):
        x, y, c, me = _place()
        fx, fy = _CHIP_FLIPS[k]
        rows = _half_rows(c, shards[w].shape[0] // 2)
        return _remote(src[w].at[rows], dst[w].at[me, dst_rows(w, c)], ssem.at[w * 6 + k], rsem.at[w * 6 + k],
                       (_flip(x, fx), _flip(y, fy), c))

    def small(src, dst, ssem, rsem, s, k):
        x, y, c, me = _place()
        fx, fy = _CHIP_FLIPS[k]
        sem = 6 * n + 3 * s + k
        return _remote(src[n + s], dst[n + s].at[me], ssem.at[sem], rsem.at[sem], (_flip(x, fx), _flip(y, fy), c))

    def own(src, dst, ssem, rsem, i):
        x, y, c, me = _place()
        sem = 6 * n + 3 * ns + i
        if i < n:
            to = dst[i].at[me, pl.ds(into[i][2], shards[i].shape[0])]
        else:
            to = dst[i].at[me]
        return _remote(src[i], to, ssem.at[sem], rsem.at[sem], (x, y, 1 - c))

    def landed(dst, ssem, rsem, w, k, sender_c, sem_off):
        x, y, c, _ = _place()
        fx, fy = _CHIP_FLIPS[k]
        got = dst[w].at[2 * _flip(x, fx) + _flip(y, fy), dst_rows(w, sender_c)]
        return _remote(got, got, ssem.at[w * 6 + sem_off + k], rsem.at[w * 6 + sem_off + k], (x, y, 1 - c))

    def start(src, dst, ssem, rsem):
        for i in range(n + ns):
            own(src, dst, ssem, rsem, i).start()
        for s in range(ns):
            for k in range(3):
                small(src, dst, ssem, rsem, s, k).start()
        for w in range(n):
            for k in range(3):
                ici(src, dst, ssem, rsem, w, k).start()

    def mid(src, dst, ssem, rsem):
        c = lax.axis_index("c")
        for w in range(n):
            for k in range(3):
                landed(dst, ssem, rsem, w, k, c, 0).wait_recv()
                landed(dst, ssem, rsem, w, k, c, 3).start()

    def end(src, dst, ssem, rsem):
        c = lax.axis_index("c")
        for w in range(n):
            for k in range(3):
                landed(dst, ssem, rsem, w, k, 1 - c, 3).wait_recv()
        for i in range(n + ns):
            own(src, dst, ssem, rsem, i).wait()
        for s in range(ns):
            for k in range(3):
                small(src, dst, ssem, rsem, s, k).wait()
        for w in range(n):
            for k in range(3):
                ici(src, dst, ssem, rsem, w, k).wait_send()
                landed(dst, ssem, rsem, w, k, c, 3).wait_send()

    outs = [jax.ShapeDtypeStruct((N_CHIPS, rows, a.shape[1]), a.dtype) for a, (_, rows, _) in zip(shards, into)]
    outs += [jax.ShapeDtypeStruct((N_CHIPS,) + a.shape, a.dtype) for a in smalls]
    filled = [(w, arr) for w, (arr, _, _) in enumerate(into) if arr is not None]
    side = _Side(list(shards) + list(smalls) + [arr for _, arr in filled], outs, 7 * n + 4 * ns, start, mid, end, mid_step)
    side.aliases = {n + ns + i: w for i, (w, _) in enumerate(filled)}
    return side


def _host(body, n_in, n_out, side, grid):
    if side is None:
        return body
    ns_in, ns_out = len(side.ins), len(side.outs)
    n_steps = math.prod(grid)
    mid_step = side.mid_step
    if side.mid is not None and not isinstance(mid_step, int):
        mid_step = min(n_steps - 1, int(mid_step * n_steps))

    def wrapped(*refs):
        ins, s_ins = refs[:n_in], refs[n_in:n_in + ns_in]
        outs = refs[n_in + ns_in:n_in + ns_in + n_out]
        s_outs = refs[n_in + ns_in + n_out:n_in + ns_in + n_out + ns_out]
        rest = refs[n_in + ns_in + n_out + ns_out:]
        sems = rest[-2:]
        step = 0
        for axis, extent in enumerate(grid):
            step = step * extent + pl.program_id(axis)

        @pl.when(step == 0)
        def _():
            side.start(s_ins, s_outs, *sems)

        if side.mid is not None:
            @pl.when(step == mid_step)
            def _():
                side.mid(s_ins, s_outs, *sems)

        body(*ins, *outs, *rest[:-2])

        @pl.when(step == n_steps - 1)
        def _():
            side.end(s_ins, s_outs, *sems)

    return wrapped


def _hosted_call(body, name, grid, in_specs, out_specs, out_shape, args, side, semantics=("parallel", "arbitrary"),
                 scratch=()):
    n_in, n_out = len(in_specs), len(out_specs)
    kern = _host(body, n_in, n_out, side, grid)
    if side is None:
        return pl.pallas_call(kern, name=name, grid=grid, in_specs=in_specs, out_specs=out_specs, out_shape=out_shape,
                              scratch_shapes=list(scratch), compiler_params=_params(semantics))(*args), []
    res = pl.pallas_call(
        kern, name=name, grid=grid, in_specs=in_specs + [HBM] * len(side.ins), out_specs=out_specs + [HBM] * len(side.outs),
        out_shape=list(out_shape) + [pltpu.HBM(s.shape, s.dtype) for s in side.outs],
        scratch_shapes=list(scratch) + [pltpu.SemaphoreType.DMA((side.n_sems,)), pltpu.SemaphoreType.DMA((side.n_sems,))],
        input_output_aliases={n_in + i: n_out + o for i, o in side.aliases.items()},
        compiler_params=_params(("arbitrary",) * len(grid)),
    )(*args, *side.ins)
    return res[:n_out], res[n_out:]


def _swap_side(halves):
    n = len(halves)

    def copies(src, dst, ssem, rsem):
        x, y, c, _ = _place()
        return [_remote(src[w], dst[w], ssem.at[w], rsem.at[w], (x, y, 1 - c)) for w in range(n)]

    def start(src, dst, ssem, rsem):
        for cp in copies(src, dst, ssem, rsem):
            cp.start()

    def end(src, dst, ssem, rsem):
        for cp in copies(src, dst, ssem, rsem):
            cp.wait()

    return _Side(halves, [jax.ShapeDtypeStruct(h.shape, h.dtype) for h in halves], n, start, None, end)


def _pair_swap(halves):
    n = len(halves)

    def body(*refs):
        src, dst = refs[:n], refs[n:2 * n]
        ssem, rsem = refs[2 * n:]
        x, y, c, _ = _place()
        cps = []
        for w in range(n):
            cp = _remote(src[w], dst[w], ssem.at[w], rsem.at[w], (x, y, 1 - c))
            cp.start()
            cps.append(cp)
        for cp in cps:
            cp.wait()

    outs = [jax.ShapeDtypeStruct(h.shape, h.dtype) for h in halves]
    return _comm_call("rs_pair_swap", body, halves, outs, n)


def _gather_small(vec):
    def body(src, dst, ssem, rsem):
        x, y, c, _ = _place()
        me = 4 * x + 2 * y + c
        cps = []
        for r in range(1, 8):
            dev = (_flip(x, r & 4), _flip(y, r & 2), _flip(c, r & 1))
            cp = _remote(src, dst.at[me], ssem.at[r - 1], rsem.at[r - 1], dev)
            cp.start()
            cps.append(cp)
        for cp in cps:
            cp.wait()

    out = jax.ShapeDtypeStruct((8,) + vec.shape, vec.dtype)
    return _comm_call("gather_small", body, [vec], [out], 7)[0]


def _pair_add(name, g, theirs, core):
    n, half, b = theirs.shape
    tr = _tile(half, 256, 16)
    nt = half // tr

    def body(c_ref, g_ref, t_ref, o_ref):
        o_ref[...] = (g_ref[...].astype(F32) + t_ref[...].astype(F32)).astype(o_ref.dtype)

    same = pl.BlockSpec((None, tr, b), lambda j, i, c: (j, i, 0))
    grid_spec = pltpu.PrefetchScalarGridSpec(
        num_scalar_prefetch=1, grid=(n, nt),
        in_specs=[pl.BlockSpec((None, tr, b), lambda j, i, c: (j, c[0] * nt + i, 0)), same], out_specs=same)
    return pl.pallas_call(body, name=name, grid_spec=grid_spec, out_shape=jax.ShapeDtypeStruct(theirs.shape, BF16),
                          compiler_params=_params(("parallel", "parallel")))(core, g, theirs)


def _chip_sum(name, parts, landed, chip1):
    n, r, c = landed.shape
    tr = _tile(r, 256, 16)

    def body(ix_ref, p_ref, s_ref, o_ref):
        acc = p_ref[...].astype(F32)
        for s in range(n):
            acc = acc + s_ref[s].astype(F32)
        o_ref[...] = acc

    grid_spec = pltpu.PrefetchScalarGridSpec(
        num_scalar_prefetch=1, grid=(r // tr,),
        in_specs=[pl.BlockSpec((None, tr, c), lambda i, ix: (ix[0], i, 0)), pl.BlockSpec((n, tr, c), lambda i, ix: (0, i, 0))],
        out_specs=pl.BlockSpec((tr, c), lambda i, ix: (i, 0)))
    return pl.pallas_call(body, name=name, grid_spec=grid_spec, out_shape=jax.ShapeDtypeStruct((r, c), F32),
                          compiler_params=_params(("parallel",)))(chip1, parts, landed)


def _sum_slots(name, stacked):
    n, r, c = stacked.shape
    tr = _tile(r, 256, 8)

    def body(s_ref, o_ref):
        acc = s_ref[0]
        for s in range(1, n):
            acc = acc + s_ref[s]
        o_ref[...] = acc

    return pl.pallas_call(
        body, name=name, grid=(r // tr,), in_specs=[pl.BlockSpec((n, tr, c), lambda i: (0, i, 0))],
        out_specs=pl.BlockSpec((tr, c), lambda i: (i, 0)),
        out_shape=jax.ShapeDtypeStruct((r, c), F32), compiler_params=_params(("parallel",)),
    )(stacked)


def _adam_math(w, g, m, v):
    bc1, bc2 = 1.0 - ADAM_B1 ** ADAM_STEP, 1.0 - ADAM_B2 ** ADAM_STEP
    nm = ADAM_B1 * m + (1.0 - ADAM_B1) * g
    nv = ADAM_B2 * v + (1.0 - ADAM_B2) * (g * g)
    return -ADAM_LR * ((nm / bc1) / (jnp.sqrt(nv / bc2) + ADAM_EPS) + ADAM_WD * w), nm, nv


def _adamw_halves(name, w, g_mine, g_theirs, m, v, core):
    r, c = w.shape
    h = r // 2
    tr = _tile(h, 128, 8)
    nth = h // tr

    def body(c_ref, w_ref, gm_ref, gt_ref, m_ref, v_ref, g_ref, d_ref, nm_ref, nv_ref):
        g = jnp.where(pl.program_id(0) // nth == c_ref[0], gm_ref[...], gt_ref[...])
        g_ref[...] = g
        d_ref[...], nm_ref[...], nv_ref[...] = _adam_math(w_ref[...], g, m_ref[...], v_ref[...])

    full = pl.BlockSpec((tr, c), lambda i, cr: (i, 0))
    half = pl.BlockSpec((tr, c), lambda i, cr: (i % nth, 0))
    grid_spec = pltpu.PrefetchScalarGridSpec(num_scalar_prefetch=1, grid=(r // tr,),
                                             in_specs=[full, half, half, full, full], out_specs=[full] * 4)
    return pl.pallas_call(body, name=name, grid_spec=grid_spec, out_shape=[jax.ShapeDtypeStruct((r, c), F32)] * 4,
                          compiler_params=_params(("parallel",)))(core, w, g_mine, g_theirs, m, v)


def _adamw(name, w, g, m, v):
    r, c = w.shape
    by_cols = r % 8 != 0 and c % LANES == 0
    tr, tc = (r, _tile(c, 256, LANES)) if by_cols else (_tile(r, 256, 8), c)

    def body(w_ref, g_ref, m_ref, v_ref, d_ref, nm_ref, nv_ref):
        d_ref[...], nm_ref[...], nv_ref[...] = _adam_math(w_ref[...], g_ref[...], m_ref[...], v_ref[...])

    spec = pl.BlockSpec((tr, tc), (lambda i: (0, i)) if by_cols else (lambda i: (i, 0)))
    return pl.pallas_call(
        body, name=name, grid=(c // tc if by_cols else r // tr,), in_specs=[spec] * 4, out_specs=[spec] * 3,
        out_shape=[jax.ShapeDtypeStruct((r, c), F32)] * 3, compiler_params=_params(("parallel",)),
    )(w, g, m, v)


def _pad_cols(a, cols):
    return jnp.pad(a, ((0, 0), (0, cols - a.shape[1])))


def _rot_cols(w):
    h = w.shape[-1] // 2
    return jnp.concatenate([-w[..., h:], w[..., :h]], axis=-1)


def _unrot_cols(d):
    h = d.shape[-1] // 2
    return jnp.concatenate([d[..., h:], -d[..., :h]], axis=-1)


def _logical(g):
    return jnp.transpose(g, (1, 0, 2)).reshape(g.shape[1], N_CHIPS * g.shape[2])


def _chunks(a, n):
    return jnp.transpose(a.reshape(a.shape[0], N_CHIPS, n), (1, 0, 2))


def kernel(x, positions, pre_mix_norm, w_in, q_a_norm, w_uq, kv_a_norm, w_ukv, b_forget, b_gate, w_branch_mla, w_branch_fox, w_out, post_mix_norm, pre_ffn_norm, w_up, conv_w, conv_b, w_down, post_ffn_norm, loss_target, m_pre_mix_norm, m_w_in, m_q_a_norm, m_w_uq, m_kv_a_norm, m_w_ukv, m_b_forget, m_b_gate, m_w_branch_mla, m_w_branch_fox, m_w_out, m_post_mix_norm, m_pre_ffn_norm, m_w_up, m_conv_w, m_conv_b, m_w_down, m_post_ffn_norm, v_pre_mix_norm, v_w_in, v_q_a_norm, v_w_uq, v_kv_a_norm, v_w_ukv, v_b_forget, v_b_gate, v_w_branch_mla, v_w_branch_fox, v_w_out, v_post_mix_norm, v_pre_ffn_norm, v_w_up, v_conv_w, v_conv_b, v_w_down, v_post_ffn_norm):
    n_seq, seq, d = x.shape
    n_tok = n_seq * seq
    d_in = N_CHIPS * w_in.shape[1]
    two_f = N_CHIPS * w_up.shape[1]
    ff_dim = two_f // 2
    assert d_in == QL + KVL + ROPE + 3 * HEADS * FDIM + HEADS + 2 * d
    n_in_shard = w_in.shape[1]
    in_pad = -(-n_in_shard // LANES) * LANES
    hd = HEADS * LANES
    xc, yc, cc = lax.axis_index("x"), lax.axis_index("y"), lax.axis_index("c")
    chip = 2 * xc + yc
    t_attn = _tile(seq, 512, 128)

    shards = [_pad_cols(w_in, in_pad).astype(BF16), w_uq.astype(BF16), w_ukv.astype(BF16), w_branch_mla.astype(BF16),
              w_branch_fox.astype(BF16), w_out.astype(BF16), w_up.astype(BF16), w_down.astype(BF16)]
    cw8 = jnp.pad(conv_w, ((0, 5), (0, 0)))
    (g_in,), _ = _all_gather_weights(shards[:1], [])
    n_attn_steps = n_seq * (seq // t_attn)
    side_proj = _gather_side([shards[3], shards[4], shards[5], shards[1], shards[2]], [cw8],
                             mid_step=0.9)
    side_ffn = _gather_side([shards[7]], [], mid_step=0.7)
    up_rows = shards[6].shape[0]
    up_cuts = [0, up_rows // 8, up_rows // 2, 7 * up_rows // 8, up_rows]

    def side_up(piece, filled, mid_step):
        lo, hi = up_cuts[piece], up_cuts[piece + 1]
        return _gather_side([shards[6][lo:hi]], [], mid_step=mid_step, into=[(filled, up_rows, lo)])

    o_q, o_kv, o_kpe = 0, QL, QL + KVL
    o_f = o_kpe + ROPE
    o_fl = o_f + 3 * hd
    o_g = o_fl + HEADS

    def chip_cols(lo, hi):
        out = []
        while lo < hi:
            j = lo // n_in_shard
            end = min(hi, (j + 1) * n_in_shard)
            out.append((j, lo - j * n_in_shard, end - j * n_in_shard))
            lo = end
        return out

    take = lambda lo, hi: [g_in[j, :, a:b] for j, a, b in chip_cols(lo, hi)]
    w_kpe = jnp.concatenate(take(o_kpe, o_f), axis=1)
    zeros = lambda n: jnp.zeros((d, n), BF16)
    win_p = jnp.concatenate(
        take(o_g, d_in) + take(o_q, o_kpe) + [w_kpe, zeros(LANES - ROPE), _rot_cols(w_kpe)] + take(o_fl, o_g)
        + [zeros(LANES - ROPE - HEADS)] + take(o_f, o_fl), axis=1)
    n_p = win_p.shape[1]
    cb_gm, cb_gf = 0, 1
    c_lat = 2 * d
    c_kx, c_kr = c_lat + QL + KVL, c_lat + QL + KVL + LANES
    n_pa = c_kr + LANES
    assert n_p == n_pa + 3 * hd

    n_bm, n_up = w_branch_mla.shape[1], w_up.shape[1]
    l_bm, l_up = _Chunked(n_bm), _Chunked(n_up)
    wt = n_up // 2
    n_ut = two_f // wt
    il = lambda cblk: jnp.where(cblk < n_ut // 2, 2 * cblk, 2 * (cblk - n_ut // 2) + 1)
    l_il = _Plain(il)
    to_il = lambda a: a.reshape(a.shape[0], 2, n_ut // 2, wt).transpose(0, 2, 1, 3).reshape(a.shape[0], two_f)
    from_il = lambda a: a.reshape(a.shape[0], n_ut // 2, 2, wt).transpose(0, 2, 1, 3).reshape(a.shape[0], two_f)

    inv_freq = 1.0 / (ROPE_THETA ** (jnp.arange(0, ROPE, 2, dtype=F32) / ROPE))
    ang = positions.astype(F32).reshape(n_tok, 1) * inv_freq
    cos, sin = jnp.cos(ang), jnp.sin(ang)
    cs = _pad_cols(jnp.concatenate([cos, cos], axis=1), LANES)
    sn = _pad_cols(jnp.concatenate([sin, sin], axis=1), LANES)

    row = lambda v: v.reshape(1, -1)
    x2 = x.reshape(n_tok, d)
    tgt = loss_target.reshape(n_tok, d)

    (h,) = _rows("rms_pre_mix", lambda r, v: ([r[0] * _rstd(r[0]) * v[0]], []),
                 [(x2, d, 0)], [row(pre_mix_norm)], [(d, BF16)], [], n_tok)
    proj, (g_bm, g_bf, g_out, g_uq, g_ukv, g_cw) = _mm("proj_in", "nn", h, win_p, n_tok, n_pa, d, tm=2048, side=side_proj)
    w_out_full = g_out.reshape(d, d)
    uq3 = _logical(g_uq).reshape(QL, HEADS, NOPE + ROPE)
    pe = uq3[:, :, NOPE:]
    pad_pe = lambda a: jnp.pad(a, ((0, 0), (0, 0), (0, LANES - ROPE))).reshape(QL, hd)
    wuq_p = jnp.concatenate([uq3[:, :, :NOPE].reshape(QL, hd), pad_pe(pe), pad_pe(_rot_cols(pe))], axis=1)
    ukv3 = _logical(g_ukv).reshape(KVL, HEADS, NOPE + VDIM)
    wukv_p = jnp.concatenate([ukv3[:, :, :NOPE].reshape(KVL, hd), ukv3[:, :, NOPE:].reshape(KVL, hd)], axis=1)
    tn_f = _tile(3 * hd, 1024, 128)
    assert n_pa % tn_f == 0
    proj_f, (g_up,) = _mm("proj_in_fox", "nn", h, win_p, n_tok, 3 * hd, d, tn=tn_f, lb=_Plain(lambda cblk: cblk + n_pa // tn_f),
                          out_dtype=BF16, side=side_up(0, None, 0.8))

    bf_vec = jnp.pad(row(b_forget), ((0, 0), (ROPE, LANES - ROPE - HEADS)))

    def lat_fwd(r, v):
        ql, kvl = r[0], r[1]
        return [ql * _rstd(ql) * v[0], kvl * _rstd(kvl) * v[1], r[2] * r[4] + r[3] * r[5]], []

    qn, kvn, rk = _rows("latent_norms", lat_fwd,
                        [(proj, QL, c_lat // QL), (proj, KVL, (c_lat + QL) // KVL), (proj, LANES, c_kx // LANES),
                         (proj, LANES, c_kr // LANES), (cs, LANES, 0), (sn, LANES, 0)],
                        [row(q_a_norm), row(kv_a_norm)], [(QL, BF16), (KVL, BF16), (LANES, BF16)], [], n_tok)
    q_p = _mm("q_up", "nn", qn, wuq_p, n_tok, 3 * hd, QL)
    kv_p = _mm("kv_up", "nn", kvn, wukv_p, n_tok, 2 * hd, KVL, out_dtype=BF16)

    def rope_q(r, v):
        c8, s8 = jnp.tile(r[3], (1, HEADS)), jnp.tile(r[4], (1, HEADS))
        return [r[0], r[1] * c8 + r[2] * s8], []

    q_nope, rq = _rows("rope_q", rope_q, [(q_p, hd, 0), (q_p, hd, 1), (q_p, hd, 2), (cs, LANES, 0), (sn, LANES, 0)], [],
                       [(hd, BF16), (hd, BF16)], [], n_tok)

    mla_q = [(q_nope, 0, False), (rq, 0, False)]
    mla_k = [(kv_p, 0, False), (rk, 0, True)]
    mla_v = (kv_p, 1)
    mla_scale = (NOPE + ROPE) ** -0.5
    (o_mla, lse_mla), (g_up,) = _attn_fwd("mla_fwd", mla_q, mla_k, mla_v, None, CHUNK, mla_scale, n_seq, seq, t_attn,
                                          side=side_up(1, g_up, max(n_attn_steps - 2, 0)))

    c_run = _seq_cumsum("forget_cumsum", proj, c_kr // LANES, n_seq, seq, False,
                        pre=lambda z, b: _log_sigmoid(z + b), vec=bf_vec)
    nb_attn = seq // t_attn
    c_rowf = jnp.transpose(c_run[:, ROPE:ROPE + HEADS].reshape(n_seq, nb_attn, t_attn, HEADS), (0, 1, 3, 2))
    fox_q, fox_k, fox_v = [(proj_f, 0, False)], [(proj_f, 1, False)], (proj_f, 2)
    fox_scale = FDIM ** -0.5
    fox_bias = (c_run, c_rowf)
    (o_fox, lse_fox), (g_up,) = _attn_fwd("fox_fwd", fox_q, fox_k, fox_v, fox_bias, 1, fox_scale, n_seq, seq, t_attn,
                                          side=side_up(2, g_up, max(n_attn_steps - 2, 0)))

    pm, (g_up,) = _mm("branch_mla", "nn", o_mla, g_bm, n_tok, d, hd, lb=l_bm, tn=n_bm, out_dtype=BF16,
                      side=side_up(3, g_up, 0.7))
    pf = _mm("branch_fox", "nn", o_fox, g_bf, n_tok, d, hd, lb=l_bm, tn=n_bm, out_dtype=BF16)
    bg = row(b_gate)

    def merge(r, v):
        return [_sigmoid(r[0] + v[0]) * r[2] + _sigmoid(r[1] + v[1]) * r[3]], []

    (merged,) = _rows("gate_merge", merge, [(proj, d, cb_gm), (proj, d, cb_gf), (pm, d, 0), (pf, d, 0)],
                      [bg[:, :d], bg[:, d:]], [(d, BF16)], [], n_tok)
    y1 = _mm("mix_out", "nn", merged, w_out_full, n_tok, d, d)

    def resid_norm(r, v):
        x1v = r[0] + r[1] * _rstd(r[1]) * v[0]
        return [x1v, x1v * _rstd(x1v) * v[1]], []

    x1, h2 = _rows("post_mix_pre_ffn", resid_norm, [(x2, d, 0), (y1, d, 0)], [row(post_mix_norm), row(pre_ffn_norm)],
                   [(d, F32), (d, BF16)], [], n_tok)

    u_il, got = _mm("ffn_up", "nn", h2, g_up, n_tok, two_f, d, lb=l_up, lo=l_il, tm=2048, tn=wt, out_dtype=BF16,
                    side=side_ffn)
    w_down_full = got[0].reshape(ff_dim, d)
    cw_il = to_il(_logical(g_cw)[:3])
    cb_il = to_il(row(conv_b))
    act = _conv_glu_fwd(u_il, cw_il, cb_il, n_seq, seq, wt)
    ff = _mm("ffn_down", "nn", act, w_down_full, n_tok, d, ff_dim)

    def final(r, v):
        x1v, ffv, tg = r
        diff = x1v + ffv * _rstd(ffv) * v[0] - tg
        dx2v = diff / d
        dffv, dg4 = _rms_bwd(ffv, v[0], dx2v)
        sq = jnp.sum(jnp.sum(diff * diff, axis=1, keepdims=True), axis=0, keepdims=True)
        return [dx2v, dffv], [dg4, jnp.broadcast_to(sq, (1, LANES))]

    dx2, dff, dg_post_ffn, sq_sum = _rows("loss_post_ffn_bwd", final, [(x1, d, 0), (ff, d, 0), (tgt, d, 0)],
                                          [row(post_ffn_norm)], [(d, F32), (d, BF16)], [(1, d), (1, LANES)], n_tok)
    rs_parts, rs_landed = {}, {}
    core = jnp.reshape(cc, (1,)).astype(jnp.int32)

    def pair_reduce(tag, names, grads):
        theirs = _pair_split("rs_pair_split_" + tag, grads)
        for nm, g, b in zip(names, grads, theirs):
            rs_parts[nm] = _pair_add("rs_pair_add_" + nm, g, b, core)

    dact = _mm("ffn_down_dx", "nt", dff, w_down_full, n_tok, ff_dim, d, tm=2048, tn=wt, out_dtype=BF16)
    gw_down = _mm("ffn_down_dw", "tn", act, dff, ff_dim, d, n_tok, tm=wt, out_dtype=BF16)
    pair_reduce("down", ["w_down"], [gw_down.reshape(N_CHIPS, ff_dim // N_CHIPS, d)])
    d_il, conv_acc = _conv_glu_bwd_pre(u_il, dact, cw_il, cb_il, n_seq, seq, wt)
    du_il = _conv_bwd_input(d_il, cw_il, n_seq, seq, wt)
    gw_up, got = _mm("ffn_up_dw", "tn", h2, du_il, d, two_f, n_tok, lb=l_il, lo=l_up, tn=wt, out_dtype=BF16,
                     side=_scatter_side([rs_parts["w_down"]]))
    rs_landed["w_down"] = got[0]
    pair_reduce("up", ["w_up"], [gw_up])
    dh2, got = _mm("ffn_up_dx", "nt", du_il, g_up, n_tok, d, two_f, la=l_il, lb=l_up, tk=wt,
                   side=_scatter_side([rs_parts["w_up"]]))
    rs_landed["w_up"] = got[0]

    def mid_bwd(r, v):
        x1v, y1v, dx2v, dh2v = r
        d3, dg3 = _rms_bwd(x1v, v[1], dh2v)
        dx1v = dx2v + d3
        dy1v, dg2 = _rms_bwd(y1v, v[0], dx1v)
        return [dx1v, dy1v], [dg3, dg2]

    dx1, dy1, dg_pre_ffn, dg_post_mix = _rows(
        "pre_ffn_post_mix_bwd", mid_bwd, [(x1, d, 0), (y1, d, 0), (dx2, d, 0), (dh2, d, 0)],
        [row(post_mix_norm), row(pre_ffn_norm)], [(d, F32), (d, BF16)], [(1, d), (1, d)], n_tok)
    dmerged = _mm("mix_out_dx", "nt", dy1, w_out_full, n_tok, d, d, out_dtype=BF16)
    gw_out = _mm("mix_out_dw", "tn", merged, dy1, d, d, n_tok, out_dtype=BF16)

    def gate_bwd(r, v):
        zm, zf, pmv, pfv, dm = r
        gm, gf = _sigmoid(zm + v[0]), _sigmoid(zf + v[1])
        dzm, dzf = dm * pmv * gm * (1.0 - gm), dm * pfv * gf * (1.0 - gf)
        return [dm * gm, dm * gf, jnp.concatenate([dzm, dzf], axis=1)], [_colsum(dzm), _colsum(dzf)]

    dpm, dpf, dz, dbg_m, dbg_f = _rows(
        "gate_merge_bwd", gate_bwd, [(proj, d, cb_gm), (proj, d, cb_gf), (pm, d, 0), (pf, d, 0), (dmerged, d, 0)],
        [bg[:, :d], bg[:, d:]], [(d, BF16), (d, BF16), (2 * d, BF16)], [(1, d), (1, d)], n_tok)
    tk_b = min(n_bm, 512)
    do_mla = _mm("branch_mla_dx", "nt", dpm, g_bm, n_tok, hd, d, lb=l_bm, tk=tk_b)
    do_fox = _mm("branch_fox_dx", "nt", dpf, g_bf, n_tok, hd, d, lb=l_bm, tk=tk_b)
    gw_bm = _mm("branch_mla_dw", "tn", o_mla, dpm, hd, d, n_tok, lo=l_bm, tn=n_bm, out_dtype=BF16)
    gw_bf = _mm("branch_fox_dw", "tn", o_fox, dpf, hd, d, n_tok, lo=l_bm, tn=n_bm, out_dtype=BF16)

    pair_reduce("mix", ["w_out", "w_branch_mla", "w_branch_fox"], [gw_out.reshape(N_CHIPS, d // N_CHIPS, d), gw_bm, gw_bf])
    delta_mla, dob_mla = _attn_delta("mla_delta", o_mla, do_mla, n_tok)
    (dq_nope, drq, dk_nope, drk_g, dv_mla), got = _attn_bwd(
        "mla_bwd", mla_q, mla_k, mla_v, dob_mla, lse_mla, delta_mla, None, CHUNK, mla_scale, n_seq, seq, t_attn, BF16,
        side=_scatter_side([rs_parts[nm] for nm in ("w_out", "w_branch_mla", "w_branch_fox")]))
    rs_landed.update(zip(("w_out", "w_branch_mla", "w_branch_fox"), got))
    delta_fox, dob_fox = _attn_delta("fox_delta", o_fox, do_fox, n_tok)
    chip1 = jnp.reshape(chip, (1,)).astype(jnp.int32)
    early = ("w_down", "w_up", "w_out", "w_branch_mla", "w_branch_fox")
    early_halves = [_chip_sum("rs_chip_sum_" + nm, rs_parts[nm], rs_landed[nm], chip1) for nm in early]
    (dfq, dfk, dfv, dc_q, dc_k), early_other = _attn_bwd("fox_bwd", fox_q, fox_k, fox_v, dob_fox, lse_fox, delta_fox, fox_bias, 1,
                                                         fox_scale, n_seq, seq, t_attn, BF16, side=_swap_side(early_halves))
    dc_k8 = jnp.transpose(dc_k, (0, 2, 4, 1, 3)).reshape(n_tok, HEADS)
    dc128 = dc_q[0] + dc_q[1] + jnp.pad(dc_k8, ((0, 0), (ROPE, LANES - ROPE - HEADS)))
    dlogf = _seq_cumsum("forget_cumsum_bwd", dc128, 0, n_seq, seq, True)

    def mla_pack(r, v):
        dqn_v, drq_v, dkn_v, dv_v, drk_a, drk_b, c1, s1 = r
        c8, s8 = jnp.tile(c1, (1, HEADS)), jnp.tile(s1, (1, HEADS))
        drk_v = drk_a + drk_b
        return [jnp.concatenate([dqn_v, drq_v * c8, drq_v * s8], axis=1), jnp.concatenate([dkn_v, dv_v], axis=1),
                drk_v * c1, drk_v * s1], []

    dq_p, dkv_p, dkx, dkr = _rows(
        "mla_rope_bwd", mla_pack,
        [(dq_nope, hd, 0), (drq, hd, 0), (dk_nope, hd, 0), (dv_mla, hd, 0), (drk_g[0], LANES, 0), (drk_g[1], LANES, 0),
         (cs, LANES, 0), (sn, LANES, 0)],
        [], [(3 * hd, BF16), (2 * hd, BF16), (LANES, F32), (LANES, F32)], [], n_tok)
    dqn = _mm("q_up_dx", "nt", dq_p, wuq_p, n_tok, QL, 3 * hd)
    gw_uq_p = _mm("q_up_dw", "tn", qn, dq_p, QL, 3 * hd, n_tok, out_dtype=BF16)
    dkvn = _mm("kv_up_dx", "nt", dkv_p, wukv_p, n_tok, KVL, 2 * hd)
    gw_ukv_p = _mm("kv_up_dw", "tn", kvn, dkv_p, KVL, 2 * hd, n_tok, out_dtype=BF16)

    def lat_bwd(r, v):
        ql, kvl, dqn_v, dkvn_v, dkx_v, dkr_v, zblk, dlf = r
        dql, dgq = _rms_bwd(ql, v[0], dqn_v)
        dkvl, dgkv = _rms_bwd(kvl, v[1], dkvn_v)
        dfl = dlf * _sigmoid(-(zblk + v[2]))
        return [jnp.concatenate([dql, dkvl, dkx_v, dkr_v + dfl], axis=1)], [dgq, dgkv, _colsum(dfl)]

    dlat, dg_q, dg_kv, dbf = _rows(
        "latent_bwd", lat_bwd,
        [(proj, QL, c_lat // QL), (proj, KVL, (c_lat + QL) // KVL), (dqn, QL, 0), (dkvn, KVL, 0), (dkx, LANES, 0),
         (dkr, LANES, 0), (proj, LANES, c_kr // LANES), (dlogf, LANES, 0)],
        [row(q_a_norm), row(kv_a_norm), bf_vec], [(QL + KVL + 2 * LANES, BF16)], [(1, QL), (1, KVL), (1, LANES)], n_tok)
    dproj = [dz, dlat, dfq, dfk, dfv]
    gw_in_p = _mm_parts("proj_in_dw", "tn", h, dproj, d, n_p, n_tok, tk=1024, out_dtype=BF16)

    f32 = lambda a: a.astype(F32)
    kr_blk = gw_in_p[:, c_kr:c_kr + LANES]
    d_kpe = (f32(gw_in_p[:, c_kx:c_kx + ROPE]) + _unrot_cols(f32(kr_blk[:, :ROPE]))).astype(BF16)
    in_pieces = [(o_q, gw_in_p, c_lat, QL + KVL), (o_kpe, d_kpe, 0, ROPE), (o_f, gw_in_p, n_pa, 3 * hd),
                 (o_fl, kr_blk, ROPE, HEADS), (o_g, gw_in_p, 0, 2 * d)]
    gc_in = []
    for j in range(N_CHIPS):
        lo, hi, cols = j * n_in_shard, (j + 1) * n_in_shard, []
        for first, arr, at, width in in_pieces:
            a, b = max(lo, first), min(hi, first + width)
            if a < b:
                cols.append(arr[:, at + a - first:at + b - first])
        cols.append(jnp.zeros((d, in_pad - n_in_shard), BF16))
        gc_in.append(jnp.concatenate(cols, axis=1))
    gc_in = jnp.stack(gc_in)
    uq_parts = [gw_uq_p[:, i * hd:(i + 1) * hd].reshape(QL, HEADS, LANES) for i in range(3)]
    d_pe = (f32(uq_parts[1][:, :, :ROPE]) + _unrot_cols(f32(uq_parts[2][:, :, :ROPE]))).astype(BF16)
    gc_uq = _chunks(jnp.concatenate([uq_parts[0], d_pe], axis=2).reshape(QL, HEADS * (NOPE + ROPE)), w_uq.shape[1])
    gc_ukv = _chunks(jnp.concatenate([gw_ukv_p[:, :hd].reshape(KVL, HEADS, NOPE), gw_ukv_p[:, hd:].reshape(KVL, HEADS, VDIM)],
                                     axis=2).reshape(KVL, HEADS * (NOPE + VDIM)), w_ukv.shape[1])
    grads = [gc_in, gc_uq, gc_ukv]

    late = ["w_in", "w_uq", "w_ukv"]
    pair_reduce("late", late, grads)
    dh, got = _mm_parts("proj_in_dx", "nt", dproj, win_p, n_tok, d, n_p, side=_scatter_side([rs_parts[nm] for nm in late]))
    rs_landed.update(zip(late, got))

    def first_bwd(r, v):
        dxa, dg1 = _rms_bwd(r[0], v[0], r[1])
        return [r[2] + dxa], [dg1]

    grad_x, dg_pre_mix = _rows("pre_mix_bwd", first_bwd, [(x2, d, 0), (dh, d, 0), (dx1, d, 0)], [row(pre_mix_norm)],
                               [(d, F32)], [(1, d)], n_tok)
    big = list(rs_parts)
    rest = [nm for nm in big if nm not in early]
    halves = [_chip_sum("rs_chip_sum_" + nm, rs_parts[nm], rs_landed[nm], chip1) for nm in rest]
    g_halves = dict(zip(rest, zip(halves, _pair_swap(halves))))
    g_halves.update(zip(early, zip(early_halves, early_other)))

    conv_acc_l = from_il(conv_acc)
    pieces = [dg_pre_mix, dg_q, dg_kv, dbf, dbg_m, dbg_f, dg_post_mix, dg_pre_ffn, conv_acc_l[3:4], dg_post_ffn,
              conv_acc_l[0:1], conv_acc_l[1:2], conv_acc_l[2:3], sq_sum]
    sizes = [p.shape[1] for p in pieces]
    flat = jnp.concatenate(pieces, axis=1)
    n_rows = -(-flat.shape[1] // (8 * LANES)) * 8
    flat = _pad_cols(flat, n_rows * LANES).reshape(n_rows, LANES)
    slots = lax.dynamic_update_slice(_gather_small(flat), flat[None], (2 * chip + cc, 0, 0))
    total = _sum_slots("small_sum", slots).reshape(1, n_rows * LANES)
    offs = [sum(sizes[:i]) for i in range(len(sizes))]
    tot = [total[0, o:o + s] for o, s in zip(offs, sizes)]
    loss = 0.5 * tot[13][0] / d
    g_small = {"pre_mix_norm": tot[0], "q_a_norm": tot[1], "kv_a_norm": tot[2], "b_forget": tot[3][ROPE:ROPE + HEADS],
               "b_gate": jnp.concatenate([tot[4], tot[5]]), "post_mix_norm": tot[6], "pre_ffn_norm": tot[7],
               "conv_b": tot[8], "post_ffn_norm": tot[9]}
    gcw_full = jnp.stack([tot[10], tot[11], tot[12]])
    g_conv_w = lax.dynamic_slice(gcw_full, (0, chip * n_up), (3, n_up))

    given = dict(pre_mix_norm=(pre_mix_norm, m_pre_mix_norm, v_pre_mix_norm), w_in=(w_in, m_w_in, v_w_in),
                 q_a_norm=(q_a_norm, m_q_a_norm, v_q_a_norm), w_uq=(w_uq, m_w_uq, v_w_uq),
                 kv_a_norm=(kv_a_norm, m_kv_a_norm, v_kv_a_norm), w_ukv=(w_ukv, m_w_ukv, v_w_ukv),
                 b_forget=(b_forget, m_b_forget, v_b_forget), b_gate=(b_gate, m_b_gate, v_b_gate),
                 w_branch_mla=(w_branch_mla, m_w_branch_mla, v_w_branch_mla),
                 w_branch_fox=(w_branch_fox, m_w_branch_fox, v_w_branch_fox), w_out=(w_out, m_w_out, v_w_out),
                 post_mix_norm=(post_mix_norm, m_post_mix_norm, v_post_mix_norm),
                 pre_ffn_norm=(pre_ffn_norm, m_pre_ffn_norm, v_pre_ffn_norm), w_up=(w_up, m_w_up, v_w_up),
                 conv_w=(conv_w, m_conv_w, v_conv_w), conv_b=(conv_b, m_conv_b, v_conv_b),
                 w_down=(w_down, m_w_down, v_w_down), post_ffn_norm=(post_ffn_norm, m_post_ffn_norm, v_post_ffn_norm))
    order = list(given)
    grad, delta, new_m, new_v = {}, {}, {}, {}
    for nm in big:
        mine, theirs = g_halves[nm]
        if nm == "w_in":
            full = jnp.concatenate([jnp.where(cc == 0, mine, theirs), jnp.where(cc == 0, theirs, mine)], axis=0)
            grad[nm] = full[:, :n_in_shard]
            tr_out = _adamw("adamw_" + nm, *[jnp.transpose(a) for a in (given[nm][0], grad[nm], given[nm][1], given[nm][2])])
            delta[nm], new_m[nm], new_v[nm] = [jnp.transpose(a) for a in tr_out]
            continue
        grad[nm], delta[nm], new_m[nm], new_v[nm] = _adamw_halves("adamw_" + nm, given[nm][0], mine, theirs, given[nm][1],
                                                                  given[nm][2], core)
    grad["conv_w"] = g_conv_w
    delta["conv_w"], new_m["conv_w"], new_v["conv_w"] = _adamw("adamw_conv_w", conv_w, g_conv_w, m_conv_w, v_conv_w)
    small = list(g_small)
    padded = [-(-g_small[nm].shape[0] // LANES) * LANES for nm in small]
    s_rows = -(-sum(padded) // (8 * LANES)) * 8

    def pack(vals):
        cat = jnp.concatenate([jnp.pad(a, (0, p - a.shape[0])) for a, p in zip(vals, padded)])
        return jnp.pad(cat, (0, s_rows * LANES - cat.shape[0])).reshape(s_rows, LANES)

    packed = _adamw("adamw_small", pack([given[nm][0] for nm in small]), pack([g_small[nm] for nm in small]),
                    pack([given[nm][1] for nm in small]), pack([given[nm][2] for nm in small]))
    s_offs = [sum(padded[:i]) for i in range(len(small))]
    for nm, o in zip(small, s_offs):
        n_el = g_small[nm].shape[0]
        grad[nm] = g_small[nm]
        delta[nm], new_m[nm], new_v[nm] = [p.reshape(-1)[o:o + n_el] for p in packed]
    return (loss, grad_x.reshape(n_seq, seq, d), *[grad[nm] for nm in order], *[delta[nm] for nm in order],
            *[new_m[nm] for nm in order], *[new_v[nm] for nm in order])
```
